```python
import jax, jax.numpy as jnp
from jax import lax
import numpy as np

D_MODEL = 1024
BATCH = 16
SEQ = 2048
DEPTH = 1

EPS = 1e-6
POOL_WINDOWS = (2, 4, 8, 16)
N_POOL_GROUPS = len(POOL_WINDOWS)
POOL_WIDTH = D_MODEL // 2
POOL_GC = POOL_WIDTH // N_POOL_GROUPS
HEAD_DIM = 64
N_Q_HEADS = (D_MODEL // 2) // HEAD_DIM
N_KV_HEADS = 2
GROUP = N_Q_HEADS // N_KV_HEADS
ATTN_WIDTH = N_Q_HEADS * HEAD_DIM
KV_WIDTH = N_KV_HEADS * HEAD_DIM
WINDOW = 128
BLOCK = 128
NEG_INF = -1e30
ROPE_THETA = 500000.0
ROT_DIM = HEAD_DIM // 4
N_BRANCHES = 2
GATE_WIDTH = N_BRANCHES * D_MODEL
IN_SPLITS = (POOL_WIDTH, POOL_WIDTH + ATTN_WIDTH, POOL_WIDTH + ATTN_WIDTH + KV_WIDTH,
             POOL_WIDTH + ATTN_WIDTH + 2 * KV_WIDTH)
IN_WIDTH = POOL_WIDTH + ATTN_WIDTH + 2 * KV_WIDTH + GATE_WIDTH
D_FF = 4 * D_MODEL

kernel_name = "hybrid_pool_swa_sink_gated_block"


def _rmsnorm(x, g):
    xf = x.astype(jnp.float32)
    r = lax.rsqrt(jnp.mean(xf * xf, axis=-1, keepdims=True) + EPS)
    return (xf * r * g.astype(jnp.float32)).astype(x.dtype)


def _partial_rotary(t, cos, sin):
    half = ROT_DIM // 2
    t1 = t[..., :half]
    t2 = t[..., half:ROT_DIM]
    c = cos[None, :, None, :].astype(t.dtype)
    s = sin[None, :, None, :].astype(t.dtype)
    return jnp.concatenate([t1 * c - t2 * s, t2 * c + t1 * s, t[..., ROT_DIM:]], axis=-1)


def _multiscale_pool(u, w_pool, pool_scale):
    B, S, _ = u.shape
    uf = u.astype(jnp.float32)
    cs = jnp.pad(jnp.cumsum(uf, axis=1), ((0, 0), (1, 0), (0, 0)))
    t = jnp.arange(S)
    pooled = []
    for gi, w in enumerate(POOL_WINDOWS):
        c = cs[..., gi * POOL_GC:(gi + 1) * POOL_GC]
        upper = c[:, 1:]
        lower = jnp.pad(c[:, :S + 1 - w], ((0, 0), (w - 1, 0), (0, 0)))
        count = jnp.minimum(t + 1, w).astype(jnp.float32)[None, :, None]
        pooled.append((upper - lower) / count)
    pooled = jnp.stack(pooled, axis=2)
    diff = (pooled - uf.reshape(B, S, N_POOL_GROUPS, POOL_GC)).astype(u.dtype)
    mixed = jnp.einsum('bsgc,gcd->bsgd', diff, w_pool)
    return mixed.reshape(B, S, POOL_WIDTH) * pool_scale


def _sliding_window_sink_attention(q, k, v, sinks):
    B, S = q.shape[0], q.shape[1]
    nb = S // BLOCK
    qb = q.reshape(B, nb, BLOCK, N_KV_HEADS, GROUP, HEAD_DIM)

    def with_prev(t):
        tb = t.reshape(B, nb, BLOCK, N_KV_HEADS, HEAD_DIM)
        prev = jnp.pad(tb[:, :-1], ((0, 0), (1, 0), (0, 0), (0, 0), (0, 0)))
        return jnp.concatenate([prev, tb], axis=2)

    kk = with_prev(k)
    vv = with_prev(v)
    scale = HEAD_DIM ** -0.5
    s = jnp.einsum('bnqhgd,bnkhd->bnhgqk', qb, kk).astype(jnp.float32) * scale
    qi = jnp.arange(BLOCK)[:, None]
    kj = jnp.arange(2 * BLOCK)[None, :]
    rel = qi + BLOCK - kj
    band = (rel >= 0) & (rel < WINDOW)
    has_prev = (jnp.arange(nb) > 0)[:, None, None] | (kj >= BLOCK)[None]
    valid = band[None] & has_prev
    s = jnp.where(valid[None, :, None, None], s, NEG_INF)
    sink = jnp.broadcast_to(sinks.astype(jnp.float32).reshape(1, 1, N_KV_HEADS, GROUP, 1, 1),
                            s.shape[:-1] + (1,))
    p = jax.nn.softmax(jnp.concatenate([s, sink], axis=-1), axis=-1)[..., :-1]
    o = jnp.einsum('bnhgqk,bnkhd->bnqhgd', p.astype(v.dtype), vv)
    return o.reshape(B, S, ATTN_WIDTH)


def _fwd_setup_inputs(seed: int = 0) -> dict:
    key = jax.random.key(seed)
    ks = jax.random.split(key, 16)
    nrm = jax.random.normal
    f32 = jnp.float32

    def gain(k):
        return 1.0 + 0.1 * nrm(k, (DEPTH, D_MODEL), f32)

    return {
        "x": nrm(ks[0], (BATCH, SEQ, D_MODEL), f32),
        "g_mix_pre": gain(ks[1]),
        "w_in": nrm(ks[2], (DEPTH, D_MODEL, IN_WIDTH), f32) * D_MODEL ** -0.5,
        "b_in": 0.02 * nrm(ks[3], (DEPTH, IN_WIDTH), f32),
        "w_pool": nrm(ks[4], (DEPTH, N_POOL_GROUPS, POOL_GC, POOL_GC), f32) * POOL_GC ** -0.5,
        "pool_scale": 1.0 + 0.1 * nrm(ks[5], (DEPTH, POOL_WIDTH), f32),
        "attn_sinks": 0.5 * nrm(ks[6], (DEPTH, N_Q_HEADS), f32),
        "w_branch_pool": nrm(ks[7], (DEPTH, POOL_WIDTH, D_MODEL), f32) * POOL_WIDTH ** -0.5,
        "w_branch_attn": nrm(ks[8], (DEPTH, ATTN_WIDTH, D_MODEL), f32) * ATTN_WIDTH ** -0.5,
        "w_out": nrm(ks[9], (DEPTH, D_MODEL, D_MODEL), f32) * D_MODEL ** -0.5,
        "g_mix_post": gain(ks[10]),
        "g_mlp_pre": gain(ks[11]),
        "w_up": nrm(ks[12], (DEPTH, D_MODEL, D_FF), f32) * D_MODEL ** -0.5,
        "w_down": nrm(ks[13], (DEPTH, D_FF, D_MODEL), f32) * D_FF ** -0.5,
        "g_mlp_post": gain(ks[14]),
    }


def _fwd_reference(x, g_mix_pre, w_in, b_in, w_pool, pool_scale, attn_sinks, w_branch_pool,
              w_branch_attn, w_out, g_mix_post, g_mlp_pre, w_up, w_down, g_mlp_post):
    B, S, _ = x.shape
    pos = jnp.arange(S, dtype=jnp.float32)
    inv_freq = ROPE_THETA ** (-jnp.arange(0, ROT_DIM, 2, dtype=jnp.float32) / ROT_DIM)
    ang = pos[:, None] * inv_freq[None, :]
    cos, sin = jnp.cos(ang), jnp.sin(ang)

    for l in range(DEPTH):
        h = _rmsnorm(x, g_mix_pre[l])
        proj = jnp.einsum('bsd,de->bse', h, w_in[l]) + b_in[l]
        u_pool, q, k, v, gates = jnp.split(proj, IN_SPLITS, axis=-1)

        y_pool = _multiscale_pool(u_pool, w_pool[l], pool_scale[l])

        q = _partial_rotary(q.reshape(B, S, N_Q_HEADS, HEAD_DIM), cos, sin)
        k = _partial_rotary(k.reshape(B, S, N_KV_HEADS, HEAD_DIM), cos, sin)
        v = v.reshape(B, S, N_KV_HEADS, HEAD_DIM)
        y_attn = _sliding_window_sink_attention(
            q.reshape(B, S, N_KV_HEADS, GROUP, HEAD_DIM), k, v, attn_sinks[l])

        g = jax.nn.sigmoid(gates.astype(jnp.float32)).astype(x.dtype)
        g_pool, g_attn = g[..., :D_MODEL], g[..., D_MODEL:]
        merged = (g_pool * jnp.einsum('bsc,cd->bsd', y_pool, w_branch_pool[l])
                  + g_attn * jnp.einsum('bsc,cd->bsd', y_attn, w_branch_attn[l]))
        mix = jnp.einsum('bsd,de->bse', merged, w_out[l])
        x = x + _rmsnorm(mix, g_mix_post[l])

        h2 = _rmsnorm(x, g_mlp_pre[l])
        ff = jnp.einsum('bsf,fd->bsd',
                        jnp.square(jax.nn.relu(jnp.einsum('bsd,df->bsf', h2, w_up[l]))), w_down[l])
        x = x + _rmsnorm(ff, g_mlp_post[l])
    return x


import jax as _jax
import jax.numpy as _jnp

TWIN_FORMAT = 'train_step'
FWD_PARAMS = ['x', 'g_mix_pre', 'w_in', 'b_in', 'w_pool', 'pool_scale', 'attn_sinks', 'w_branch_pool', 'w_branch_attn', 'w_out', 'g_mix_post', 'g_mlp_pre', 'w_up', 'w_down', 'g_mlp_post']
TWIN_WEIGHTS = ['g_mix_pre', 'w_in', 'b_in', 'w_pool', 'pool_scale', 'attn_sinks', 'w_branch_pool', 'w_branch_attn', 'w_out', 'g_mix_post', 'g_mlp_pre', 'w_up', 'w_down', 'g_mlp_post']
TWIN_DIFF_INPUT = 'x'
TWIN_INPUTS = ['x', 'g_mix_pre', 'w_in', 'b_in', 'w_pool', 'pool_scale', 'attn_sinks', 'w_branch_pool', 'w_branch_attn', 'w_out', 'g_mix_post', 'g_mlp_pre', 'w_up', 'w_down', 'g_mlp_post', 'loss_target', 'm_g_mix_pre', 'm_w_in', 'm_b_in', 'm_w_pool', 'm_pool_scale', 'm_attn_sinks', 'm_w_branch_pool', 'm_w_branch_attn', 'm_w_out', 'm_g_mix_post', 'm_g_mlp_pre', 'm_w_up', 'm_w_down', 'm_g_mlp_post', 'v_g_mix_pre', 'v_w_in', 'v_b_in', 'v_w_pool', 'v_pool_scale', 'v_attn_sinks', 'v_w_branch_pool', 'v_w_branch_attn', 'v_w_out', 'v_g_mix_post', 'v_g_mlp_pre', 'v_w_up', 'v_w_down', 'v_g_mlp_post']
TWIN_OUTPUTS = ['loss', 'grad_x', 'grad_g_mix_pre', 'grad_w_in', 'grad_b_in', 'grad_w_pool', 'grad_pool_scale', 'grad_attn_sinks', 'grad_w_branch_pool', 'grad_w_branch_attn', 'grad_w_out', 'grad_g_mix_post', 'grad_g_mlp_pre', 'grad_w_up', 'grad_w_down', 'grad_g_mlp_post', 'delta_g_mix_pre', 'delta_w_in', 'delta_b_in', 'delta_w_pool', 'delta_pool_scale', 'delta_attn_sinks', 'delta_w_branch_pool', 'delta_w_branch_attn', 'delta_w_out', 'delta_g_mix_post', 'delta_g_mlp_pre', 'delta_w_up', 'delta_w_down', 'delta_g_mlp_post', 'new_m_g_mix_pre', 'new_m_w_in', 'new_m_b_in', 'new_m_w_pool', 'new_m_pool_scale', 'new_m_attn_sinks', 'new_m_w_branch_pool', 'new_m_w_branch_attn', 'new_m_w_out', 'new_m_g_mix_post', 'new_m_g_mlp_pre', 'new_m_w_up', 'new_m_w_down', 'new_m_g_mlp_post', 'new_v_g_mix_pre', 'new_v_w_in', 'new_v_b_in', 'new_v_w_pool', 'new_v_pool_scale', 'new_v_attn_sinks', 'new_v_w_branch_pool', 'new_v_w_branch_attn', 'new_v_w_out', 'new_v_g_mix_post', 'new_v_g_mlp_pre', 'new_v_w_up', 'new_v_w_down', 'new_v_g_mlp_post']
TWIN_LEAF_KINDS = {'loss': 'loss', 'grad_x': 'grad_x', 'grad_g_mix_pre': 'grad_w', 'grad_w_in': 'grad_w', 'grad_b_in': 'grad_w', 'grad_w_pool': 'grad_w', 'grad_pool_scale': 'grad_w', 'grad_attn_sinks': 'grad_w', 'grad_w_branch_pool': 'grad_w', 'grad_w_branch_attn': 'grad_w', 'grad_w_out': 'grad_w', 'grad_g_mix_post': 'grad_w', 'grad_g_mlp_pre': 'grad_w', 'grad_w_up': 'grad_w', 'grad_w_down': 'grad_w', 'grad_g_mlp_post': 'grad_w', 'delta_g_mix_pre': 'delta_w', 'delta_w_in': 'delta_w', 'delta_b_in': 'delta_w', 'delta_w_pool': 'delta_w', 'delta_pool_scale': 'delta_w', 'delta_attn_sinks': 'delta_w', 'delta_w_branch_pool': 'delta_w', 'delta_w_branch_attn': 'delta_w', 'delta_w_out': 'delta_w', 'delta_g_mix_post': 'delta_w', 'delta_g_mlp_pre': 'delta_w', 'delta_w_up': 'delta_w', 'delta_w_down': 'delta_w', 'delta_g_mlp_post': 'delta_w', 'new_m_g_mix_pre': 'new_m', 'new_m_w_in': 'new_m', 'new_m_b_in': 'new_m', 'new_m_w_pool': 'new_m', 'new_m_pool_scale': 'new_m', 'new_m_attn_sinks': 'new_m', 'new_m_w_branch_pool': 'new_m', 'new_m_w_branch_attn': 'new_m', 'new_m_w_out': 'new_m', 'new_m_g_mix_post': 'new_m', 'new_m_g_mlp_pre': 'new_m', 'new_m_w_up': 'new_m', 'new_m_w_down': 'new_m', 'new_m_g_mlp_post': 'new_m', 'new_v_g_mix_pre': 'new_v', 'new_v_w_in': 'new_v', 'new_v_b_in': 'new_v', 'new_v_w_pool': 'new_v', 'new_v_pool_scale': 'new_v', 'new_v_attn_sinks': 'new_v', 'new_v_w_branch_pool': 'new_v', 'new_v_w_branch_attn': 'new_v', 'new_v_w_out': 'new_v', 'new_v_g_mix_post': 'new_v', 'new_v_g_mlp_pre': 'new_v', 'new_v_w_up': 'new_v', 'new_v_w_down': 'new_v', 'new_v_g_mlp_post': 'new_v'}


def _forward(args):
    return _fwd_reference(*[args[k] for k in FWD_PARAMS])


def _output_shape():
    out = _jax.eval_shape(lambda: _forward(_fwd_setup_inputs(0)))
    return out.shape, out.dtype

N_MICROBATCH = 1
ADAM_LR = 0.001
ADAM_B1 = 0.9
ADAM_B2 = 0.999
ADAM_EPS = 1e-08
ADAM_WD = 0.01
ADAM_STEP = 10
PER_EXAMPLE_BATCH_AXIS = {'x': 0, 'loss_target': 0}
SHARED_INPUTS = []
_WEIGHT_DTYPES = {'g_mix_pre': _jnp.float32, 'w_in': _jnp.float32, 'b_in': _jnp.float32, 'w_pool': _jnp.float32, 'pool_scale': _jnp.float32, 'attn_sinks': _jnp.float32, 'w_branch_pool': _jnp.float32, 'w_branch_attn': _jnp.float32, 'w_out': _jnp.float32, 'g_mix_post': _jnp.float32, 'g_mlp_pre': _jnp.float32, 'w_up': _jnp.float32, 'w_down': _jnp.float32, 'g_mlp_post': _jnp.float32}
MOMENT_SCALE = {'g_mix_pre': 8.692892e-01, 'w_in': 4.435002e-01, 'b_in': 7.036433e+00, 'w_pool': 1.429632e+00, 'pool_scale': 1.914983e+00, 'attn_sinks': 1.045618e-01, 'w_branch_pool': 1.222384e+00, 'w_branch_attn': 1.396273e-01, 'w_out': 1.331106e+00, 'g_mix_post': 3.237432e+01, 'g_mlp_pre': 1.149702e+00, 'w_up': 5.487525e-01, 'w_down': 1.421781e+00, 'g_mlp_post': 3.299142e+01}


def _to_microbatches(a, axis):
    t = _jnp.moveaxis(a, axis, 0)
    t = t.reshape((N_MICROBATCH, t.shape[0] // N_MICROBATCH) + t.shape[1:])
    return _jnp.moveaxis(t, 1, axis + 1)


def setup_inputs(seed: int = 0) -> dict:
    inp = _fwd_setup_inputs(seed)
    key = _jax.random.fold_in(_jax.random.key(seed), 7919)
    shape, _ = _output_shape()
    out = dict(inp)
    out["loss_target"] = _jax.random.normal(_jax.random.fold_in(key, 0), shape, _jnp.float32)
    for i, name in enumerate(TWIN_WEIGHTS):
        w = inp[name].astype(_jnp.float32)
        if MOMENT_SCALE is None:
            s = _jnp.sqrt(_jnp.mean(_jnp.square(w)) + 1e-30)
        else:
            s = MOMENT_SCALE[name]
        km, kv = _jax.random.split(_jax.random.fold_in(key, i + 1))
        out[name] = w
        out["m_" + name] = s * _jax.random.normal(km, w.shape, _jnp.float32)
        out["v_" + name] = (s * s) * _jax.random.uniform(kv, w.shape, _jnp.float32, 0.5, 1.5)
    if N_MICROBATCH > 1:
        for name, axis in PER_EXAMPLE_BATCH_AXIS.items():
            out[name] = _to_microbatches(out[name], axis)
    return {'x': out['x'], 'g_mix_pre': out['g_mix_pre'], 'w_in': out['w_in'], 'b_in': out['b_in'], 'w_pool': out['w_pool'], 'pool_scale': out['pool_scale'], 'attn_sinks': out['attn_sinks'], 'w_branch_pool': out['w_branch_pool'], 'w_branch_attn': out['w_branch_attn'], 'w_out': out['w_out'], 'g_mix_post': out['g_mix_post'], 'g_mlp_pre': out['g_mlp_pre'], 'w_up': out['w_up'], 'w_down': out['w_down'], 'g_mlp_post': out['g_mlp_post'], 'loss_target': out['loss_target'], 'm_g_mix_pre': out['m_g_mix_pre'], 'm_w_in': out['m_w_in'], 'm_b_in': out['m_b_in'], 'm_w_pool': out['m_w_pool'], 'm_pool_scale': out['m_pool_scale'], 'm_attn_sinks': out['m_attn_sinks'], 'm_w_branch_pool': out['m_w_branch_pool'], 'm_w_branch_attn': out['m_w_branch_attn'], 'm_w_out': out['m_w_out'], 'm_g_mix_post': out['m_g_mix_post'], 'm_g_mlp_pre': out['m_g_mlp_pre'], 'm_w_up': out['m_w_up'], 'm_w_down': out['m_w_down'], 'm_g_mlp_post': out['m_g_mlp_post'], 'v_g_mix_pre': out['v_g_mix_pre'], 'v_w_in': out['v_w_in'], 'v_b_in': out['v_b_in'], 'v_w_pool': out['v_w_pool'], 'v_pool_scale': out['v_pool_scale'], 'v_attn_sinks': out['v_attn_sinks'], 'v_w_branch_pool': out['v_w_branch_pool'], 'v_w_branch_attn': out['v_w_branch_attn'], 'v_w_out': out['v_w_out'], 'v_g_mix_post': out['v_g_mix_post'], 'v_g_mlp_pre': out['v_g_mlp_pre'], 'v_w_up': out['v_w_up'], 'v_w_down': out['v_w_down'], 'v_g_mlp_post': out['v_g_mlp_post']}


def _loss(weights, diff, rest, loss_target):
    with _jax.named_scope("forward"):
        args = {**rest, TWIN_DIFF_INPUT: diff, **{k: w.astype(_WEIGHT_DTYPES[k]) for k, w in weights.items()}}
        y = _forward(args)
    with _jax.named_scope("loss_head"):
        err = _jnp.square(y.astype(_jnp.float32) - loss_target)
        return 0.5 * _jnp.sum(_jnp.mean(err, axis=-1)) if err.ndim else 0.5 * err


def _adamw(w, g, m, v):
    m = ADAM_B1 * m + (1.0 - ADAM_B1) * g
    v = ADAM_B2 * v + (1.0 - ADAM_B2) * _jnp.square(g)
    m_hat = m / (1.0 - ADAM_B1 ** ADAM_STEP)
    v_hat = v / (1.0 - ADAM_B2 ** ADAM_STEP)
    delta = -ADAM_LR * (m_hat / (_jnp.sqrt(v_hat) + ADAM_EPS) + ADAM_WD * w)
    return delta, m, v


def reference(x, g_mix_pre, w_in, b_in, w_pool, pool_scale, attn_sinks, w_branch_pool, w_branch_attn, w_out, g_mix_post, g_mlp_pre, w_up, w_down, g_mlp_post, loss_target, m_g_mix_pre, m_w_in, m_b_in, m_w_pool, m_pool_scale, m_attn_sinks, m_w_branch_pool, m_w_branch_attn, m_w_out, m_g_mix_post, m_g_mlp_pre, m_w_up, m_w_down, m_g_mlp_post, v_g_mix_pre, v_w_in, v_b_in, v_w_pool, v_pool_scale, v_attn_sinks, v_w_branch_pool, v_w_branch_attn, v_w_out, v_g_mix_post, v_g_mlp_pre, v_w_up, v_w_down, v_g_mlp_post):
    given = dict(x=x, g_mix_pre=g_mix_pre, w_in=w_in, b_in=b_in, w_pool=w_pool, pool_scale=pool_scale, attn_sinks=attn_sinks, w_branch_pool=w_branch_pool, w_branch_attn=w_branch_attn, w_out=w_out, g_mix_post=g_mix_post, g_mlp_pre=g_mlp_pre, w_up=w_up, w_down=w_down, g_mlp_post=g_mlp_post, loss_target=loss_target, m_g_mix_pre=m_g_mix_pre, m_w_in=m_w_in, m_b_in=m_b_in, m_w_pool=m_w_pool, m_pool_scale=m_pool_scale, m_attn_sinks=m_attn_sinks, m_w_branch_pool=m_w_branch_pool, m_w_branch_attn=m_w_branch_attn, m_w_out=m_w_out, m_g_mix_post=m_g_mix_post, m_g_mlp_pre=m_g_mlp_pre, m_w_up=m_w_up, m_w_down=m_w_down, m_g_mlp_post=m_g_mlp_post, v_g_mix_pre=v_g_mix_pre, v_w_in=v_w_in, v_b_in=v_b_in, v_w_pool=v_w_pool, v_pool_scale=v_pool_scale, v_attn_sinks=v_attn_sinks, v_w_branch_pool=v_w_branch_pool, v_w_branch_attn=v_w_branch_attn, v_w_out=v_w_out, v_g_mix_post=v_g_mix_post, v_g_mlp_pre=v_g_mlp_pre, v_w_up=v_w_up, v_w_down=v_w_down, v_g_mlp_post=v_g_mlp_post)
    weights = {n: given[n] for n in TWIN_WEIGHTS}
    shared = {n: given[n] for n in SHARED_INPUTS}
    per_example = {n: given[n] for n in ['x']}
    grad_fn = _jax.value_and_grad(_loss, argnums=(0, 1))

    def one_microbatch(ex, loss_target):
        ex = dict(ex)
        diff = ex.pop(TWIN_DIFF_INPUT)
        return grad_fn(weights, diff, {**shared, **ex}, loss_target)

    if N_MICROBATCH == 1:
        loss, (grad_w, grad_x) = one_microbatch(per_example, given["loss_target"])
    else:
        def body(carry, xs):
            loss_sum, grad_sum = carry
            l_k, (gw_k, gx_k) = one_microbatch(xs[0], xs[1])
            with _jax.named_scope("update"):
                return (loss_sum + l_k, _jax.tree.map(_jnp.add, grad_sum, gw_k)), gx_k

        init = (_jnp.zeros((), _jnp.float32), _jax.tree.map(_jnp.zeros_like, weights))
        (loss, grad_w), grad_x = _jax.lax.scan(body, init, (per_example, given["loss_target"]))
    with _jax.named_scope("update"):
        delta_w, new_m, new_v = {}, {}, {}
        for n in TWIN_WEIGHTS:
            delta_w[n], new_m[n], new_v[n] = _adamw(weights[n], grad_w[n], given["m_" + n], given["v_" + n])
    return (loss, grad_x, *[grad_w[n] for n in TWIN_WEIGHTS], *[delta_w[n] for n in TWIN_WEIGHTS],
            *[new_m[n] for n in TWIN_WEIGHTS], *[new_v[n] for n in TWIN_WEIGHTS])
```

```python
import functools

import jax
import jax.numpy as jnp
from jax import lax
from jax.experimental import pallas as pl
from jax.experimental.pallas import tpu as pltpu

F32 = jnp.float32
MXU_DTYPE = jnp.bfloat16
MESH = pl.DeviceIdType.MESH

D_MODEL = 1024
POOL_WINDOWS = (2, 4, 8, 16)
POOL_WIDTH = 512
POOL_GC = 128
HEAD_DIM = 64
N_Q_HEADS = 8
N_KV_HEADS = 2
GROUP = 4
ATTN_WIDTH = 512
KV_WIDTH = 128
BLOCK = 128
GATE_WIDTH = 2048
IN_WIDTH = 3328
D_FF = 4096
EPS = 1e-6
NEG_INF = -1e30
ROPE_THETA = 500000.0
ROT_DIM = 16
SCALE = HEAD_DIM ** -0.5
C_Q, C_K, C_V, C_G = 512, 1024, 1152, 1280

ADAM_LR = 0.001
ADAM_B1 = 0.9
ADAM_B2 = 0.999
ADAM_EPS = 1e-08
ADAM_WD = 0.01
ADAM_STEP = 10

N_DEV = 8
LANES = 128
VMEM_LIMIT = 56 * 1024 * 1024

NN = (((1,), (0,)), ((), ()))
NT = (((1,), (1,)), ((), ()))
TN = (((0,), (0,)), ((), ()))


def _dot(a, b, dims):
    return lax.dot_general(a, b, dims, preferred_element_type=F32)


def _params(sem=None):
    return pltpu.CompilerParams(dimension_semantics=sem, vmem_limit_bytes=VMEM_LIMIT)


def _tile(n, pref):
    t = min(n, pref)
    assert n % t == 0, (n, t)
    return t


def _rms_r(x):
    return lax.rsqrt(jnp.mean(x * x, axis=-1, keepdims=True) + EPS)


def _rms_bwd(dn, x, r, g):
    xh = x * r
    dxh = dn * g
    dx = r * (dxh - xh * jnp.mean(dxh * xh, axis=-1, keepdims=True))
    return dx, dn * xh


def _rot(t, c, sa, sb):
    outs = []
    for j in range(t.shape[1] // LANES):
        tj = t[:, LANES * j:LANES * (j + 1)]
        outs.append(tj * c + pltpu.roll(tj, LANES - 8, 1) * sa + pltpu.roll(tj, 8, 1) * sb)
    return outs[0] if len(outs) == 1 else jnp.concatenate(outs, axis=1)


def _rot_tables(S):
    pos = jnp.arange(S, dtype=F32)
    inv_freq = ROPE_THETA ** (-jnp.arange(0, ROT_DIM, 2, dtype=F32) / ROT_DIM)
    ang = pos[:, None] * inv_freq[None, :]
    cos, sin = jnp.cos(ang), jnp.sin(ang)
    one = jnp.ones((S, HEAD_DIM - ROT_DIM), F32)
    zero = jnp.zeros((S, HEAD_DIM - ROT_DIM), F32)
    z8 = jnp.zeros((S, 8), F32)
    c = jnp.concatenate([cos, cos, one], axis=1)
    sa = jnp.concatenate([-sin, z8, zero], axis=1)
    sb = jnp.concatenate([z8, sin, zero], axis=1)
    rep = LANES // HEAD_DIM
    return jnp.tile(c, (1, rep)), jnp.tile(sa, (1, rep)), jnp.tile(sb, (1, rep))


def _lane_tile4(k):
    lane = lax.broadcasted_iota(jnp.int32, k.shape, 1)
    rk = pltpu.roll(k, HEAD_DIM, 1)
    x0 = jnp.where(lane < HEAD_DIM, k, rk)
    x1 = jnp.where(lane < HEAD_DIM, rk, k)
    return jnp.concatenate([x0, x0, x1, x1], axis=1)


def _fold_heads(acc):
    zs = []
    for hk in range(N_KV_HEADS):
        a = acc[:, 256 * hk:256 * hk + LANES] + acc[:, 256 * hk + LANES:256 * (hk + 1)]
        zs.append(a + pltpu.roll(a, HEAD_DIM, 1))
    lane = lax.broadcasted_iota(jnp.int32, zs[0].shape, 1)
    return jnp.where(lane < HEAD_DIM, zs[0], zs[1])


def _inproj_call(x, g1, win_t, b_in, rc, rsa, rsb, S):
    T = x.shape[0]
    tm = _tile(S, 512)
    nst = S // tm

    def body(x_ref, g1_ref, w_ref, b_ref, c_ref, sa_ref, sb_ref,
             h_ref, u_ref, q_ref, k4_ref, v4_ref, g_ref):
        xv = x_ref[...]
        hb = ((xv * _rms_r(xv)) * g1_ref[...]).astype(MXU_DTYPE)
        h_ref[...] = hb

        def proj(lo, hi):
            return _dot(hb, w_ref[lo:hi, :], NT) + b_ref[:, lo:hi]

        c, sa, sb = c_ref[...], sa_ref[...], sb_ref[...]
        u_ref[...] = proj(0, C_Q)
        q_ref[...] = _rot(proj(C_Q, C_K), c, sa, sb).astype(MXU_DTYPE)
        kv = proj(C_K, C_G)
        k4_ref[...] = _lane_tile4(_rot(kv[:, :KV_WIDTH], c, sa, sb)).astype(MXU_DTYPE)
        v4_ref[...] = _lane_tile4(kv[:, KV_WIDTH:]).astype(MXU_DTYPE)
        g_ref[...] = jax.nn.sigmoid(proj(C_G, IN_WIDTH))

    tok = lambda w: pl.BlockSpec((tm, w), lambda i: (i, 0))
    full = lambda a: pl.BlockSpec(a.shape, lambda i: (0,) * a.ndim)
    tab = pl.BlockSpec((tm, LANES), lambda i: (i % nst, 0))
    return pl.pallas_call(
        body, name="inproj_fwd", grid=(T // tm,),
        in_specs=[tok(D_MODEL), full(g1), full(win_t), full(b_in), tab, tab, tab],
        out_specs=[tok(D_MODEL), tok(POOL_WIDTH), tok(ATTN_WIDTH), tok(512), tok(512), tok(GATE_WIDTH)],
        out_shape=[jax.ShapeDtypeStruct((T, D_MODEL), MXU_DTYPE), jax.ShapeDtypeStruct((T, POOL_WIDTH), F32),
                   jax.ShapeDtypeStruct((T, ATTN_WIDTH), MXU_DTYPE), jax.ShapeDtypeStruct((T, 512), MXU_DTYPE),
                   jax.ShapeDtypeStruct((T, 512), MXU_DTYPE), jax.ShapeDtypeStruct((T, GATE_WIDTH), F32)],
        compiler_params=_params(("arbitrary",)),
    )(x, g1, win_t, b_in, rc, rsa, rsb)


def _shift_rows(a, k, rows):
    n = a.shape[0]
    if k > 0:
        return jnp.where(rows >= k, pltpu.roll(a, k, 0), 0.0)
    return jnp.where(rows < n + k, pltpu.roll(a, n + k, 0), 0.0)


def _win_sum(a, w, rows, sign):
    s, k = a, 1
    while k < w:
        s = s + _shift_rows(s, sign * k, rows)
        k *= 2
    return s


def _pool_diff(ug, w, rows):
    inv = 1.0 / jnp.minimum(rows + 1, w).astype(F32)
    return _win_sum(ug, w, rows, 1) * inv - ug, inv


def _pool_call(u, w_pool, pool_scale, S):
    T = u.shape[0]

    def body(u_ref, w_ref, ps_ref, y_ref):
        rows = lax.broadcasted_iota(jnp.int32, (S, POOL_GC), 0)
        for gi, w in enumerate(POOL_WINDOWS):
            sl = slice(POOL_GC * gi, POOL_GC * (gi + 1))
            diff, _ = _pool_diff(u_ref[:, sl], w, rows)
            mixed = _dot(diff.astype(MXU_DTYPE), w_ref[gi], NN)
            y_ref[:, sl] = (mixed * ps_ref[:, sl]).astype(MXU_DTYPE)

    seq = pl.BlockSpec((S, POOL_WIDTH), lambda b: (b, 0))
    return pl.pallas_call(
        body, name="pool_fwd", grid=(T // S,),
        in_specs=[seq, pl.BlockSpec(w_pool.shape, lambda b: (0, 0, 0)), pl.BlockSpec(pool_scale.shape, lambda b: (0, 0))],
        out_specs=seq, out_shape=jax.ShapeDtypeStruct((T, POOL_WIDTH), MXU_DTYPE),
        compiler_params=_params(("arbitrary",)),
    )(u, w_pool, pool_scale)


def _pool_bwd_call(u, dyp, w_pool, pool_scale, S):
    T = u.shape[0]

    def body(u_ref, dy_ref, w_ref, ps_ref, du_ref, dw_ref, dps_ref):
        @pl.when(pl.program_id(0) == 0)
        def _():
            dw_ref[...] = jnp.zeros_like(dw_ref)
            dps_ref[...] = jnp.zeros_like(dps_ref)

        rows = lax.broadcasted_iota(jnp.int32, (S, POOL_GC), 0)
        for gi, w in enumerate(POOL_WINDOWS):
            sl = slice(POOL_GC * gi, POOL_GC * (gi + 1))
            diff, inv = _pool_diff(u_ref[:, sl], w, rows)
            diffb = diff.astype(MXU_DTYPE)
            wg = w_ref[gi]
            mixed = _dot(diffb, wg, NN)
            dy = dy_ref[:, sl]
            dps_ref[:, sl] += jnp.sum(dy * mixed, axis=0, keepdims=True)
            dmb = (dy * ps_ref[:, sl]).astype(MXU_DTYPE)
            dw_ref[gi] += _dot(diffb, dmb, TN)
            ddiff = _dot(dmb, wg, NT)
            du_ref[:, sl] = (_win_sum(ddiff * inv, w, rows, -1) - ddiff).astype(MXU_DTYPE)

    seq = pl.BlockSpec((S, POOL_WIDTH), lambda b: (b, 0))
    return pl.pallas_call(
        body, name="pool_bwd", grid=(T // S,),
        in_specs=[seq, seq, pl.BlockSpec(w_pool.shape, lambda b: (0, 0, 0)), pl.BlockSpec(pool_scale.shape, lambda b: (0, 0))],
        out_specs=[seq, pl.BlockSpec(w_pool.shape, lambda b: (0, 0, 0)), pl.BlockSpec(pool_scale.shape, lambda b: (0, 0))],
        out_shape=[jax.ShapeDtypeStruct((T, POOL_WIDTH), MXU_DTYPE), jax.ShapeDtypeStruct(w_pool.shape, F32),
                   jax.ShapeDtypeStruct(pool_scale.shape, F32)],
        compiler_params=_params(("arbitrary",)),
    )(u, dyp, w_pool, pool_scale)


def _attn_consts():
    lane_g = lax.broadcasted_iota(jnp.int32, (BLOCK, 256), 1) >> 6
    row = lax.broadcasted_iota(jnp.int32, (GROUP * BLOCK, 256), 0)
    col = lax.broadcasted_iota(jnp.int32, (GROUP * BLOCK, 256), 1)
    rel0 = (row & (BLOCK - 1)) - col
    rgrp = lax.broadcasted_iota(jnp.int32, (GROUP * BLOCK, 1), 0) >> 7
    return lane_g, rel0, rgrp


def _sink_rows(sink_ref, hk, rgrp):
    sv = jnp.zeros(rgrp.shape, F32)
    for g in range(GROUP):
        sv = jnp.where(rgrp == g, sink_ref[0, GROUP * hk + g], sv)
    return sv


def _stack_heads(xb, lane_g):
    return jnp.concatenate([jnp.where(lane_g == g, xb, jnp.zeros_like(xb)) for g in range(GROUP)], axis=0)


def _unstack_heads(xs, lane_g):
    out = jnp.where(lane_g == 0, xs[0:BLOCK], 0.0)
    for g in range(1, GROUP):
        out = out + jnp.where(lane_g == g, xs[BLOCK * g:BLOCK * (g + 1)], 0.0)
    return out


def _attn_probs(qs, kb, off, rel0, sv):
    s = _dot(qs, kb, NT) * SCALE
    rel = rel0 + off
    s = jnp.where((rel >= 0) & (rel < BLOCK), s, NEG_INF)
    m = jnp.maximum(jnp.max(s, axis=1, keepdims=True), sv)
    e = jnp.exp(s - m)
    es = jnp.exp(sv - m)
    inv_l = 1.0 / (jnp.sum(e, axis=1, keepdims=True) + es)
    return e * inv_l, es * inv_l


def _attn_call(sinks, q, k4, v4, S):
    T = q.shape[0]
    nb = S // BLOCK

    def body(sink_ref, q_ref, k_ref, v_ref, o_ref):
        lane_g, rel0, rgrp = _attn_consts()
        for hk in range(N_KV_HEADS):
            cs = slice(256 * hk, 256 * (hk + 1))
            sv = _sink_rows(sink_ref, hk, rgrp)

            def blk(n, carry):
                q0 = pl.multiple_of(n * BLOCK, BLOCK)
                k0 = pl.multiple_of(jnp.maximum(n - 1, 0) * BLOCK, BLOCK)
                qs = _stack_heads(q_ref[pl.ds(q0, BLOCK), cs], lane_g)
                p, _ = _attn_probs(qs, k_ref[pl.ds(k0, 2 * BLOCK), cs], q0 - k0, rel0, sv)
                o = _dot(p.astype(MXU_DTYPE), v_ref[pl.ds(k0, 2 * BLOCK), cs], NN)
                o_ref[pl.ds(q0, BLOCK), cs] = _unstack_heads(o, lane_g).astype(MXU_DTYPE)
                return carry

            lax.fori_loop(0, nb, blk, 0)

    seq = pl.BlockSpec((S, ATTN_WIDTH), lambda b: (b, 0))
    return pl.pallas_call(
        body, name="attn_fwd", grid=(T // S,),
        in_specs=[pl.BlockSpec(memory_space=pltpu.SMEM), seq, seq, seq],
        out_specs=seq, out_shape=jax.ShapeDtypeStruct((T, ATTN_WIDTH), MXU_DTYPE),
        compiler_params=_params(("arbitrary",)),
    )(sinks, q, k4, v4)


def _attn_bwd_call(sinks, q, k4, v4, do, rc, rsa, rsb, S):
    T = q.shape[0]
    nb = S // BLOCK

    def body(sink_ref, q_ref, k_ref, v_ref, do_ref, c_ref, sa_ref, sb_ref,
             dq_ref, dk_ref, dv_ref, ds_ref, dk_acc, dv_acc):
        lane_g, rel0, rgrp = _attn_consts()
        lane1 = lax.broadcasted_iota(jnp.int32, (1, LANES), 1)
        dk_acc[...] = jnp.zeros_like(dk_acc)
        dv_acc[...] = jnp.zeros_like(dv_acc)
        dsink = jnp.zeros((1, LANES), F32)
        for hk in range(N_KV_HEADS):
            cs = slice(256 * hk, 256 * (hk + 1))
            sv = _sink_rows(sink_ref, hk, rgrp)

            def blk(n, dsink):
                q0 = pl.multiple_of(n * BLOCK, BLOCK)
                k0 = pl.multiple_of(jnp.maximum(n - 1, 0) * BLOCK, BLOCK)
                qs = _stack_heads(q_ref[pl.ds(q0, BLOCK), cs], lane_g)
                dos = _stack_heads(do_ref[pl.ds(q0, BLOCK), cs], lane_g)
                kb = k_ref[pl.ds(k0, 2 * BLOCK), cs]
                vb = v_ref[pl.ds(k0, 2 * BLOCK), cs]
                p, ps = _attn_probs(qs, kb, q0 - k0, rel0, sv)
                dp = _dot(dos, vb, NT)
                delta = jnp.sum(p * dp, axis=1, keepdims=True)
                dsb = (p * (dp - delta) * SCALE).astype(MXU_DTYPE)
                dqb = _unstack_heads(_dot(dsb, kb, NN), lane_g)
                rows = pl.ds(q0, BLOCK)
                dq_ref[rows, cs] = _rot(dqb, c_ref[rows, :], -sa_ref[rows, :], -sb_ref[rows, :]).astype(MXU_DTYPE)
                dk_acc[pl.ds(k0, 2 * BLOCK), cs] += _dot(dsb, qs, TN)
                dv_acc[pl.ds(k0, 2 * BLOCK), cs] += _dot(p.astype(MXU_DTYPE), dos, TN)
                psd = ps * delta
                for g in range(GROUP):
                    val = -jnp.sum(psd[BLOCK * g:BLOCK * (g + 1)], axis=0, keepdims=True)
                    dsink = dsink + jnp.where(lane1 == GROUP * hk + g, val, 0.0)
                return dsink

            dsink = lax.fori_loop(0, nb, blk, dsink)
        dk_ref[...] = _rot(_fold_heads(dk_acc[...]), c_ref[...], -sa_ref[...], -sb_ref[...]).astype(MXU_DTYPE)
        dv_ref[...] = _fold_heads(dv_acc[...]).astype(MXU_DTYPE)
        ds_ref[...] = jnp.broadcast_to(dsink, ds_ref.shape)

    seq = pl.BlockSpec((S, ATTN_WIDTH), lambda b: (b, 0))
    kvs = pl.BlockSpec((S, KV_WIDTH), lambda b: (b, 0))
    tab = pl.BlockSpec((S, LANES), lambda b: (0, 0))
    nseq = T // S
    return pl.pallas_call(
        body, name="attn_bwd", grid=(nseq,),
        in_specs=[pl.BlockSpec(memory_space=pltpu.SMEM), seq, seq, seq, seq, tab, tab, tab],
        out_specs=[seq, kvs, kvs, pl.BlockSpec((8, LANES), lambda b: (b, 0))],
        out_shape=[jax.ShapeDtypeStruct((T, ATTN_WIDTH), MXU_DTYPE), jax.ShapeDtypeStruct((T, KV_WIDTH), MXU_DTYPE),
                   jax.ShapeDtypeStruct((T, KV_WIDTH), MXU_DTYPE), jax.ShapeDtypeStruct((8 * nseq, LANES), F32)],
        scratch_shapes=[pltpu.VMEM((S, 512), F32), pltpu.VMEM((S, 512), F32)],
        compiler_params=_params(("arbitrary",)),
    )(sinks, q, k4, v4, do, rc, rsa, rsb)


def _branch_weights(wbp_ref, wba_ref, wbp_s, wba_s):
    @pl.when(pl.program_id(0) == 0)
    def _():
        for j in range(N_DEV):
            wbp_s[:, LANES * j:LANES * (j + 1)] = wbp_ref[j]
            wba_s[:, LANES * j:LANES * (j + 1)] = wba_ref[j]


def _mix_fwd_call(yp, ya, g, x, wbp, wba, wout, g2, g3):
    T = x.shape[0]
    tm = _tile(T, 256)

    def body(yp_ref, ya_ref, g_ref, x_ref, wbp_ref, wba_ref, wout_ref, g2_ref, g3_ref,
             mix_ref, x1_ref, h2_ref, wbp_s, wba_s):
        _branch_weights(wbp_ref, wba_ref, wbp_s, wba_s)
        bp = _dot(yp_ref[...], wbp_s[...], NN)
        ba = _dot(ya_ref[...], wba_s[...], NN)
        merged = g_ref[:, :D_MODEL] * bp + g_ref[:, D_MODEL:] * ba
        mix = _dot(merged.astype(MXU_DTYPE), wout_ref[...], NN)
        mix_ref[...] = mix
        x1 = x_ref[...] + (mix * _rms_r(mix)) * g2_ref[...]
        x1_ref[...] = x1
        h2_ref[...] = ((x1 * _rms_r(x1)) * g3_ref[...]).astype(MXU_DTYPE)

    tok = lambda w: pl.BlockSpec((tm, w), lambda i: (i, 0))
    full = lambda a: pl.BlockSpec(a.shape, lambda i: (0,) * a.ndim)
    return pl.pallas_call(
        body, name="mix_fwd", grid=(T // tm,),
        in_specs=[tok(POOL_WIDTH), tok(ATTN_WIDTH), tok(GATE_WIDTH), tok(D_MODEL), full(wbp), full(wba), full(wout),
                  full(g2), full(g3)],
        out_specs=[tok(D_MODEL), tok(D_MODEL), tok(D_MODEL)],
        out_shape=[jax.ShapeDtypeStruct((T, D_MODEL), F32), jax.ShapeDtypeStruct((T, D_MODEL), F32),
                   jax.ShapeDtypeStruct((T, D_MODEL), MXU_DTYPE)],
        scratch_shapes=[pltpu.VMEM((POOL_WIDTH, D_MODEL), MXU_DTYPE), pltpu.VMEM((ATTN_WIDTH, D_MODEL), MXU_DTYPE)],
        compiler_params=_params(("arbitrary",)),
    )(yp, ya, g, x, wbp, wba, wout, g2, g3)


def _mix_bwd_call(dx1, mix, yp, ya, g, wbp, wba, wout, g2):
    T = dx1.shape[0]
    tm = _tile(T, 256)

    def body(dx1_ref, mix_ref, yp_ref, ya_ref, g_ref, wbp_ref, wba_ref, wout_ref, g2_ref,
             dmix_ref, merged_ref, dbp_ref, dba_ref, dyp_ref, do_ref, dgates_ref, dg2_ref, dbg_ref, wbp_s, wba_s):
        _branch_weights(wbp_ref, wba_ref, wbp_s, wba_s)

        @pl.when(pl.program_id(0) == 0)
        def _():
            dg2_ref[...] = jnp.zeros_like(dg2_ref)
            dbg_ref[...] = jnp.zeros_like(dbg_ref)

        mix = mix_ref[...]
        dmix, dg2 = _rms_bwd(dx1_ref[...], mix, _rms_r(mix), g2_ref[...])
        dg2_ref[...] += jnp.sum(dg2, axis=0, keepdims=True)
        dmixb = dmix.astype(MXU_DTYPE)
        dmix_ref[...] = dmixb
        dmerged = _dot(dmixb, wout_ref[...], NT)
        bp = _dot(yp_ref[...], wbp_s[...], NN)
        ba = _dot(ya_ref[...], wba_s[...], NN)
        gp, ga = g_ref[:, :D_MODEL], g_ref[:, D_MODEL:]
        merged_ref[...] = (gp * bp + ga * ba).astype(MXU_DTYPE)
        dgp = dmerged * bp * (gp * (1.0 - gp))
        dga = dmerged * ba * (ga * (1.0 - ga))
        dbg_ref[:, :D_MODEL] += jnp.sum(dgp, axis=0, keepdims=True)
        dbg_ref[:, D_MODEL:] += jnp.sum(dga, axis=0, keepdims=True)
        dgates_ref[:, :D_MODEL] = dgp.astype(MXU_DTYPE)
        dgates_ref[:, D_MODEL:] = dga.astype(MXU_DTYPE)
        dbp = (dmerged * gp).astype(MXU_DTYPE)
        dba = (dmerged * ga).astype(MXU_DTYPE)
        dbp_ref[...] = dbp
        dba_ref[...] = dba
        dyp_ref[...] = _dot(dbp, wbp_s[...], NT)
        do_ref[...] = _dot(dba, wba_s[...], NT).astype(MXU_DTYPE)

    tok = lambda w: pl.BlockSpec((tm, w), lambda i: (i, 0))
    full = lambda a: pl.BlockSpec(a.shape, lambda i: (0,) * a.ndim)
    acc = lambda w: pl.BlockSpec((1, w), lambda i: (0, 0))
    sd = jax.ShapeDtypeStruct
    return pl.pallas_call(
        body, name="mix_bwd", grid=(T // tm,),
        in_specs=[tok(D_MODEL), tok(D_MODEL), tok(POOL_WIDTH), tok(ATTN_WIDTH), tok(GATE_WIDTH), full(wbp), full(wba),
                  full(wout), full(g2)],
        out_specs=[tok(D_MODEL), tok(D_MODEL), tok(D_MODEL), tok(D_MODEL), tok(POOL_WIDTH), tok(ATTN_WIDTH),
                   tok(GATE_WIDTH), acc(D_MODEL), acc(GATE_WIDTH)],
        out_shape=[sd((T, D_MODEL), MXU_DTYPE), sd((T, D_MODEL), MXU_DTYPE), sd((T, D_MODEL), MXU_DTYPE),
                   sd((T, D_MODEL), MXU_DTYPE), sd((T, POOL_WIDTH), F32), sd((T, ATTN_WIDTH), MXU_DTYPE),
                   sd((T, GATE_WIDTH), MXU_DTYPE), sd((1, D_MODEL), F32), sd((1, GATE_WIDTH), F32)],
        scratch_shapes=[pltpu.VMEM((POOL_WIDTH, D_MODEL), MXU_DTYPE), pltpu.VMEM((ATTN_WIDTH, D_MODEL), MXU_DTYPE)],
        compiler_params=_params(("arbitrary",)),
    )(dx1, mix, yp, ya, g, wbp, wba, wout, g2)


def _mlp_call(x1, h2, target, wup, wdown, g3, g4):
    T = x1.shape[0]
    tm = _tile(T, 256)
    fc = D_FF // N_DEV

    def body(x1_ref, h2_ref, t_ref, wup_ref, wdown_ref, g3_ref, g4_ref,
             act_ref, da_ref, dff_ref, dx1_ref, dg3_ref, dg4_ref, loss_ref, rl_s):
        @pl.when(pl.program_id(0) == 0)
        def _():
            dg3_ref[...] = jnp.zeros_like(dg3_ref)
            dg4_ref[...] = jnp.zeros_like(dg4_ref)
            loss_ref[...] = jnp.zeros_like(loss_ref)

        h2 = h2_ref[...]
        ff = jnp.zeros((tm, D_MODEL), F32)
        for j in range(N_DEV):
            sl = slice(fc * j, fc * (j + 1))
            rl = jnp.maximum(_dot(h2, wup_ref[j], NN), 0.0)
            rl_s[:, sl] = rl
            actb = (rl * rl).astype(MXU_DTYPE)
            act_ref[:, sl] = actb
            ff = ff + _dot(actb, wdown_ref[j], NN)
        x1 = x1_ref[...]
        r4 = _rms_r(ff)
        err = x1 + (ff * r4) * g4_ref[...] - t_ref[...]
        loss_ref[...] += jnp.sum(err * err, axis=0, keepdims=True)
        dy = err * (1.0 / D_MODEL)
        dff, dg4 = _rms_bwd(dy, ff, r4, g4_ref[...])
        dg4_ref[...] += jnp.sum(dg4, axis=0, keepdims=True)
        dffb = dff.astype(MXU_DTYPE)
        dff_ref[...] = dffb
        dh2 = jnp.zeros((tm, D_MODEL), F32)
        for j in range(N_DEV):
            sl = slice(fc * j, fc * (j + 1))
            dab = (_dot(dffb, wdown_ref[j], NT) * (2.0 * rl_s[:, sl])).astype(MXU_DTYPE)
            da_ref[:, sl] = dab
            dh2 = dh2 + _dot(dab, wup_ref[j], NT)
        dx1, dg3 = _rms_bwd(dh2, x1, _rms_r(x1), g3_ref[...])
        dg3_ref[...] += jnp.sum(dg3, axis=0, keepdims=True)
        dx1_ref[...] = dy + dx1

    tok = lambda w: pl.BlockSpec((tm, w), lambda i: (i, 0))
    full = lambda a: pl.BlockSpec(a.shape, lambda i: (0,) * a.ndim, pipeline_mode=pl.Buffered(1))
    vec = pl.BlockSpec((1, D_MODEL), lambda i: (0, 0))
    sd = jax.ShapeDtypeStruct
    return pl.pallas_call(
        body, name="mlp_fwd_bwd", grid=(T // tm,),
        in_specs=[tok(D_MODEL), tok(D_MODEL), tok(D_MODEL), full(wup), full(wdown), vec, vec],
        out_specs=[tok(D_FF), tok(D_FF), tok(D_MODEL), tok(D_MODEL), vec, vec, vec],
        out_shape=[sd((T, D_FF), MXU_DTYPE), sd((T, D_FF), MXU_DTYPE), sd((T, D_MODEL), MXU_DTYPE),
                   sd((T, D_MODEL), F32), sd((1, D_MODEL), F32), sd((1, D_MODEL), F32), sd((1, D_MODEL), F32)],
        scratch_shapes=[pltpu.VMEM((tm, D_FF), F32)],
        compiler_params=_params(("arbitrary",)),
    )(x1, h2, target, wup, wdown, g3, g4)


def _inproj_bwd_call(du, dq, dk, dv, dgates, dx1, x, win_t, g1):
    T = x.shape[0]
    tm = _tile(T, 256)

    def body(du_ref, dq_ref, dk_ref, dv_ref, dgt_ref, dx1_ref, x_ref, w_ref, g1_ref, gx_ref, dg1_ref, db_ref):
        @pl.when(pl.program_id(0) == 0)
        def _():
            dg1_ref[...] = jnp.zeros_like(dg1_ref)
            db_ref[...] = jnp.zeros_like(db_ref)

        dh = jnp.zeros((tm, D_MODEL), F32)
        for ref, lo, hi in ((du_ref, 0, C_Q), (dq_ref, C_Q, C_K), (dk_ref, C_K, C_V), (dv_ref, C_V, C_G),
                            (dgt_ref, C_G, IN_WIDTH)):
            piece = ref[...]
            dh = dh + _dot(piece, w_ref[lo:hi, :], NN)
            if hi <= C_G:
                db_ref[:, lo:hi] += jnp.sum(piece.astype(F32), axis=0, keepdims=True)
        xv = x_ref[...]
        dx, dg1 = _rms_bwd(dh, xv, _rms_r(xv), g1_ref[...])
        dg1_ref[...] += jnp.sum(dg1, axis=0, keepdims=True)
        gx_ref[...] = dx1_ref[...] + dx

    tok = lambda w: pl.BlockSpec((tm, w), lambda i: (i, 0))
    full = lambda a: pl.BlockSpec(a.shape, lambda i: (0,) * a.ndim)
    sd = jax.ShapeDtypeStruct
    return pl.pallas_call(
        body, name="inproj_bwd", grid=(T // tm,),
        in_specs=[tok(POOL_WIDTH), tok(ATTN_WIDTH), tok(KV_WIDTH), tok(KV_WIDTH), tok(GATE_WIDTH), tok(D_MODEL),
                  tok(D_MODEL), full(win_t), full(g1)],
        out_specs=[tok(D_MODEL), pl.BlockSpec((1, D_MODEL), lambda i: (0, 0)), pl.BlockSpec((1, C_G), lambda i: (0, 0))],
        out_shape=[sd((T, D_MODEL), F32), sd((1, D_MODEL), F32), sd((1, C_G), F32)],
        compiler_params=_params(("arbitrary",)),
    )(du, dq, dk, dv, dgates, dx1, x, win_t, g1)


def _wgrad_rows_call(a, b, name):
    T, K = a.shape
    N = b.shape[1]
    tm = _tile(T, 512)
    kb = min(K, 1024)
    per = kb // (K // N_DEV)

    def body(a_ref, b_ref, o_ref):
        @pl.when(pl.program_id(1) == 0)
        def _():
            o_ref[...] = jnp.zeros_like(o_ref)

        d = _dot(a_ref[...], b_ref[...], TN)
        rs = kb // per
        for j in range(per):
            o_ref[j] += d[rs * j:rs * (j + 1)]

    return pl.pallas_call(
        body, name=name, grid=(K // kb, T // tm),
        in_specs=[pl.BlockSpec((tm, kb), lambda i, t: (t, i)), pl.BlockSpec((tm, N), lambda i, t: (t, 0))],
        out_specs=pl.BlockSpec((per, K // N_DEV, N), lambda i, t: (i, 0, 0)),
        out_shape=jax.ShapeDtypeStruct((N_DEV, K // N_DEV, N), F32),
        compiler_params=_params(("arbitrary", "arbitrary")),
    )(a, b)


def _wgrad_cols_call(a, b, name):
    T, K = a.shape
    N = b.shape[1]
    tm = _tile(T, 512)
    nb = min(N, 1024)
    per = nb // (N // N_DEV)

    def body(a_ref, b_ref, o_ref):
        @pl.when(pl.program_id(1) == 0)
        def _():
            o_ref[...] = jnp.zeros_like(o_ref)

        d = _dot(a_ref[...], b_ref[...], TN)
        cs = nb // per
        for j in range(per):
            o_ref[j] += d[:, cs * j:cs * (j + 1)]

    return pl.pallas_call(
        body, name=name, grid=(N // nb, T // tm),
        in_specs=[pl.BlockSpec((tm, K), lambda i, t: (t, 0)), pl.BlockSpec((tm, nb), lambda i, t: (t, i))],
        out_specs=pl.BlockSpec((per, K, N // N_DEV), lambda i, t: (i, 0, 0)),
        out_shape=jax.ShapeDtypeStruct((N_DEV, K, N // N_DEV), F32),
        compiler_params=_params(("arbitrary", "arbitrary")),
    )(a, b)


def _wgrad_in_call(du, dq, dk, dv, dgates, h):
    T = h.shape[0]
    tm = _tile(T, 512)
    rows = IN_WIDTH // N_DEV

    def body(du_ref, dq_ref, dk_ref, dv_ref, dgt_ref, h_ref, o_ref, acc, sem):
        t = pl.program_id(0)

        @pl.when(t == 0)
        def _():
            acc[...] = jnp.zeros_like(acc)

        hv = h_ref[...]
        for ref, lo, hi in ((du_ref, 0, C_Q), (dq_ref, C_Q, C_K), (dk_ref, C_K, C_V), (dv_ref, C_V, C_G),
                            (dgt_ref, C_G, IN_WIDTH)):
            acc[lo:hi, :] += _dot(ref[...], hv, TN)

        @pl.when(t == pl.num_programs(0) - 1)
        def _():
            copies = [pltpu.make_async_copy(acc.at[pl.ds(rows * j, rows), :], o_ref.at[j], sem.at[j])
                      for j in range(N_DEV)]
            for cp in copies:
                cp.start()
            for cp in copies:
                cp.wait()

    tok = lambda w: pl.BlockSpec((tm, w), lambda t: (t, 0))
    return pl.pallas_call(
        body, name="wgrad_in", grid=(T // tm,),
        in_specs=[tok(POOL_WIDTH), tok(ATTN_WIDTH), tok(KV_WIDTH), tok(KV_WIDTH), tok(GATE_WIDTH), tok(D_MODEL)],
        out_specs=pl.BlockSpec(memory_space=pl.ANY),
        out_shape=jax.ShapeDtypeStruct((N_DEV, rows, D_MODEL), F32),
        scratch_shapes=[pltpu.VMEM((IN_WIDTH, D_MODEL), F32), pltpu.SemaphoreType.DMA((N_DEV,))],
        compiler_params=_params(("arbitrary",)),
    )(du, dq, dk, dv, dgates, h)


def _coords():
    return lax.axis_index("x"), lax.axis_index("y"), lax.axis_index("c")


def _allgather_call(shards):
    n = len(shards)

    def body(*refs):
        ins, outs = refs[:n], refs[n:2 * n]
        send_sems, recv_sems, local_sems = refs[2 * n:]
        x, y, c = _coords()
        me, sibling = (x, y, c), (x, y, 1 - c)
        chips = [(1 - x, y), (x, 1 - y), (1 - x, 1 - y)]

        def slot(p):
            return 4 * p[0] + 2 * p[1] + p[2]

        def copy(t, k, block, to, src=None):
            dst = outs[t].at[slot(block)]
            return pltpu.make_async_remote_copy(
                src_ref=dst if src is None else src, dst_ref=dst, send_sem=send_sems.at[t, k],
                recv_sem=recv_sems.at[t, k], device_id=to, device_id_type=MESH)

        mine = [pltpu.make_async_copy(ins[t], outs[t].at[slot(me)], local_sems.at[t]) for t in range(n)]
        for cp in mine:
            cp.start()
        first = []
        for t in range(n):
            first.append(copy(t, 0, me, sibling, src=ins[t]))
            first += [copy(t, 1 + j, me, (*chip, c), src=ins[t]) for j, chip in enumerate(chips)]
        for cp in first:
            cp.start()
        passed = []
        for t in range(n):
            for j, chip in enumerate(chips):
                copy(t, 1 + j, (*chip, c), me).wait_recv()
                fwd = copy(t, 4 + j, (*chip, c), sibling)
                fwd.start()
                passed.append(fwd)
        for t in range(n):
            copy(t, 0, sibling, me).wait_recv()
            for j, chip in enumerate(chips):
                copy(t, 4 + j, (*chip, 1 - c), me).wait_recv()
        for cp in first + passed:
            cp.wait_send()
        for cp in mine:
            cp.wait()

    hbm = pl.BlockSpec(memory_space=pl.ANY)
    return pl.pallas_call(
        body, name="allgather_weights",
        in_specs=[hbm] * n, out_specs=[hbm] * n,
        out_shape=[jax.ShapeDtypeStruct((N_DEV,) + s.shape, s.dtype) for s in shards],
        scratch_shapes=[pltpu.SemaphoreType.DMA((n, 7)), pltpu.SemaphoreType.DMA((n, 7)), pltpu.SemaphoreType.DMA((n,))],
    )(*shards)


def _rs_sibling_call(grads):
    n = len(grads)

    def body(*refs):
        ins, outs = refs[:n], refs[n:2 * n]
        send_sems, recv_sems = refs[2 * n:]
        x, y, c = _coords()
        copies = []
        for t in range(n):
            for q in range(4):
                copies.append(pltpu.make_async_remote_copy(
                    src_ref=ins[t].at[q, 1 - c], dst_ref=outs[t].at[q], send_sem=send_sems.at[t, q],
                    recv_sem=recv_sems.at[t, q], device_id=(x, y, 1 - c), device_id_type=MESH))
        for cp in copies:
            cp.start()
        for cp in copies:
            cp.wait()

    hbm = pl.BlockSpec(memory_space=pl.ANY)
    return pl.pallas_call(
        body, name="rs_sibling",
        in_specs=[hbm] * n, out_specs=[hbm] * n,
        out_shape=[jax.ShapeDtypeStruct((4,) + g.shape[2:], g.dtype) for g in grads],
        scratch_shapes=[pltpu.SemaphoreType.DMA((n, 4)), pltpu.SemaphoreType.DMA((n, 4))],
    )(*grads)


def _chip_sum_call(cidx, grads, recvd, out_dtypes):
    n = len(grads)

    def body(c_ref, *refs):
        for t in range(n):
            refs[2 * n + t][0] = (refs[t][0, 0] + refs[n + t][0]).astype(out_dtypes[t])

    in_specs = [pl.BlockSpec((1, 1) + g.shape[2:], lambda q, c_ref: (q, c_ref[0], 0, 0)) for g in grads]
    in_specs += [pl.BlockSpec((1,) + r.shape[1:], lambda q, c_ref: (q, 0, 0)) for r in recvd]
    return pl.pallas_call(
        body, name="rs_chip_sum",
        grid_spec=pltpu.PrefetchScalarGridSpec(
            num_scalar_prefetch=1, grid=(4,), in_specs=in_specs,
            out_specs=[pl.BlockSpec((1,) + r.shape[1:], lambda q, c_ref: (q, 0, 0)) for r in recvd]),
        out_shape=[jax.ShapeDtypeStruct(r.shape, dt) for r, dt in zip(recvd, out_dtypes)],
        compiler_params=_params(("arbitrary",)),
    )(cidx, *grads, *recvd)


def _rs_chips_call(sums):
    n = len(sums)

    def body(*refs):
        ins, outs = refs[:n], refs[n:2 * n]
        send_sems, recv_sems = refs[2 * n:]
        x, y, c = _coords()
        chips = [(1 - x, y), (x, 1 - y), (1 - x, 1 - y)]
        copies = []
        for t in range(n):
            for r, (px, py) in enumerate(chips):
                copies.append(pltpu.make_async_remote_copy(
                    src_ref=ins[t].at[2 * px + py], dst_ref=outs[t].at[r], send_sem=send_sems.at[t, r],
                    recv_sem=recv_sems.at[t, r], device_id=(px, py, c), device_id_type=MESH))
        for cp in copies:
            cp.start()
        for cp in copies:
            cp.wait()

    hbm = pl.BlockSpec(memory_space=pl.ANY)
    return pl.pallas_call(
        body, name="rs_chips",
        in_specs=[hbm] * n, out_specs=[hbm] * n,
        out_shape=[jax.ShapeDtypeStruct((3,) + s.shape[1:], s.dtype) for s in sums],
        scratch_shapes=[pltpu.SemaphoreType.DMA((n, 3)), pltpu.SemaphoreType.DMA((n, 3))],
    )(*sums)


def _final_sum_call(idx, grads, recvd1, recvd2):
    n = len(grads)
    nsteps = 2

    def body(i_ref, *refs):
        for t in range(n):
            g, r1, r2, o = refs[t], refs[n + t], refs[2 * n + t], refs[3 * n + t]
            s = g[0, 0] + r1[0]
            for r in range(3):
                s = s + r2[r].astype(F32)
            o[...] = s

    def rows(a):
        r = a.shape[-2]
        return r // nsteps if (r // nsteps) % 16 == 0 else r

    def step(a):
        return (lambda i: i) if rows(a) != a.shape[-2] else (lambda i: 0)

    in_specs = [pl.BlockSpec((1, 1, rows(g), g.shape[3]), lambda i, s, st=step(g): (s[0], s[1], st(i), 0)) for g in grads]
    in_specs += [pl.BlockSpec((1, rows(r), r.shape[2]), lambda i, s, st=step(r): (s[0], st(i), 0)) for r in recvd1]
    in_specs += [pl.BlockSpec((3, rows(r), r.shape[2]), lambda i, s, st=step(r): (0, st(i), 0)) for r in recvd2]
    return pl.pallas_call(
        body, name="rs_final_sum",
        grid_spec=pltpu.PrefetchScalarGridSpec(
            num_scalar_prefetch=1, grid=(nsteps,), in_specs=in_specs,
            out_specs=[pl.BlockSpec((rows(r), r.shape[2]), lambda i, s, st=step(r): (st(i), 0)) for r in recvd2]),
        out_shape=[jax.ShapeDtypeStruct(r.shape[1:], F32) for r in recvd2],
        compiler_params=_params(("arbitrary",)),
    )(idx, *grads, *recvd1, *recvd2)


def _allgather_small_call(shard):
    def body(in_ref, out_ref, send_sems, recv_sems, local_sem):
        x, y, c = _coords()
        me = 4 * x + 2 * y + c
        mine = pltpu.make_async_copy(in_ref, out_ref.at[me], local_sem)
        mine.start()
        copies = []
        for k in range(1, N_DEV):
            fx, fy, fc = (k >> 2) & 1, (k >> 1) & 1, k & 1
            peer = (x ^ fx, y ^ fy, c ^ fc)
            copies.append(pltpu.make_async_remote_copy(
                src_ref=in_ref, dst_ref=out_ref.at[me], send_sem=send_sems.at[k - 1],
                recv_sem=recv_sems.at[k - 1], device_id=peer, device_id_type=MESH))
        for cp in copies:
            cp.start()
        for cp in copies:
            cp.wait()
        mine.wait()

    hbm = pl.BlockSpec(memory_space=pl.ANY)
    return pl.pallas_call(
        body, name="allgather_small", in_specs=[hbm], out_specs=hbm,
        out_shape=jax.ShapeDtypeStruct((N_DEV,) + shard.shape, shard.dtype),
        scratch_shapes=[pltpu.SemaphoreType.DMA((7,)), pltpu.SemaphoreType.DMA((7,)), pltpu.SemaphoreType.DMA],
    )(shard)


def _adamw(w, g, m, v):
    m = ADAM_B1 * m + (1.0 - ADAM_B1) * g
    v = ADAM_B2 * v + (1.0 - ADAM_B2) * (g * g)
    m_hat = m / (1.0 - ADAM_B1 ** ADAM_STEP)
    v_hat = v / (1.0 - ADAM_B2 ** ADAM_STEP)
    delta = -ADAM_LR * (m_hat / (jnp.sqrt(v_hat) + ADAM_EPS) + ADAM_WD * w)
    return delta, m, v


def _adamw_call(ws, gs, ms, vs, nsteps, name):
    n = len(ws)

    def body(*refs):
        for t in range(n):
            w, g, m, v = (refs[k * n + t][...] for k in range(4))
            d, m2, v2 = _adamw(w, g, m, v)
            refs[4 * n + t][...] = d
            refs[5 * n + t][...] = m2
            refs[6 * n + t][...] = v2

    def spec(a):
        assert a.shape[0] % nsteps == 0 and (nsteps == 1 or (a.shape[0] // nsteps) % 8 == 0), a.shape
        return pl.BlockSpec((a.shape[0] // nsteps, a.shape[1]), lambda i: (i, 0))

    specs = [spec(a) for a in ws]
    outs = pl.pallas_call(
        body, name=name, grid=(nsteps,),
        in_specs=specs * 4, out_specs=specs * 3,
        out_shape=[jax.ShapeDtypeStruct(a.shape, F32) for a in ws] * 3,
        compiler_params=_params(("arbitrary",)),
    )(*ws, *gs, *ms, *vs)
    return outs[:n], outs[n:2 * n], outs[2 * n:]


def _rows128(a, pad_rows):
    flat = a.reshape(-1).astype(F32)
    flat = jnp.pad(flat, (0, pad_rows * LANES - flat.shape[0]))
    return flat.reshape(pad_rows, LANES)


_SMALL = (("g_mix_pre", 8), ("b_in", 32), ("w_pool", 512), ("pool_scale", 8), ("attn_sinks", 8),
          ("g_mix_post", 8), ("g_mlp_pre", 8), ("g_mlp_post", 8), ("loss", 8))
_SMALL_ROWS = 640


def kernel(x, g_mix_pre, w_in, b_in, w_pool, pool_scale, attn_sinks, w_branch_pool, w_branch_attn, w_out, g_mix_post, g_mlp_pre, w_up, w_down, g_mlp_post, loss_target, m_g_mix_pre, m_w_in, m_b_in, m_w_pool, m_pool_scale, m_attn_sinks, m_w_branch_pool, m_w_branch_attn, m_w_out, m_g_mix_post, m_g_mlp_pre, m_w_up, m_w_down, m_g_mlp_post, v_g_mix_pre, v_w_in, v_b_in, v_w_pool, v_pool_scale, v_attn_sinks, v_w_branch_pool, v_w_branch_attn, v_w_out, v_g_mix_post, v_g_mlp_pre, v_w_up, v_w_down, v_g_mlp_post):
    B, S, _ = x.shape
    T = B * S
    xt = x.reshape(T, D_MODEL)
    tgt = loss_target.reshape(T, D_MODEL)
    cx, cy, cc = _coords()

    shards = [w_in[0].T.astype(MXU_DTYPE), w_branch_pool[0].astype(MXU_DTYPE), w_branch_attn[0].astype(MXU_DTYPE),
              w_out[0].astype(MXU_DTYPE), w_up[0].astype(MXU_DTYPE), w_down[0].astype(MXU_DTYPE)]
    win_s, wbp_s, wba_s, wout_s, wup_s, wdown_s = _allgather_call(shards)
    win_t = win_s.reshape(IN_WIDTH, D_MODEL)
    wout_f = wout_s.reshape(D_MODEL, D_MODEL)
    wpool_b = w_pool[0].astype(MXU_DTYPE)
    rc, rsa, rsb = _rot_tables(S)

    h, u, q, k4, v4, g = _inproj_call(xt, g_mix_pre, win_t, b_in, rc, rsa, rsb, S)
    yp = _pool_call(u, wpool_b, pool_scale, S)
    ya = _attn_call(attn_sinks, q, k4, v4, S)
    mix, x1, h2 = _mix_fwd_call(yp, ya, g, xt, wbp_s, wba_s, wout_f, g_mix_post, g_mlp_pre)

    act, da, dff, dx1, dg3, dg4, lossvec = _mlp_call(x1, h2, tgt, wup_s, wdown_s, g_mlp_pre, g_mlp_post)
    gw_down = _wgrad_rows_call(act, dff, "wgrad_down")
    gw_up = _wgrad_cols_call(h2, da, "wgrad_up")
    dmix, merged, dbp, dba, dyp, do, dgates, dg2, dbg = _mix_bwd_call(dx1, mix, yp, ya, g, wbp_s, wba_s, wout_f, g_mix_post)
    gw_out = _wgrad_rows_call(merged, dmix, "wgrad_out")
    gw_bp = _wgrad_cols_call(yp, dbp, "wgrad_bp")
    gw_ba = _wgrad_cols_call(ya, dba, "wgrad_ba")
    dq, dk, dv, dsink = _attn_bwd_call(attn_sinks, q, k4, v4, do, rc, rsa, rsb, S)
    du, dwp, dps = _pool_bwd_call(u, dyp, wpool_b, pool_scale, S)
    gx, dg1, dba_in = _inproj_bwd_call(du, dq, dk, dv, dgates, dx1, xt, win_t, g_mix_pre)
    gw_in = _wgrad_in_call(du, dq, dk, dv, dgates, h)

    small = {"g_mix_pre": dg1, "b_in": jnp.concatenate([dba_in, dbg], axis=1), "w_pool": dwp, "pool_scale": dps,
             "attn_sinks": jnp.sum(dsink.reshape(B, 8, LANES)[:, 0, :N_Q_HEADS], axis=0), "g_mix_post": dg2,
             "g_mlp_pre": dg3, "g_mlp_post": dg4, "loss": lossvec}
    packed = jnp.concatenate([_rows128(small[k], r) for k, r in _SMALL]
                             + [jnp.zeros((_SMALL_ROWS - sum(r for _, r in _SMALL), LANES), F32)], axis=0)
    gw_small = packed.reshape(N_DEV, _SMALL_ROWS // N_DEV, LANES)

    grads = [gw_in, gw_down, gw_out, gw_up, gw_bp, gw_ba, gw_small]
    grads4 = [gr.reshape((4, 2) + gr.shape[1:]) for gr in grads]
    recvd1 = _rs_sibling_call(grads4)
    sums = _chip_sum_call(jnp.reshape(cc, (1,)).astype(jnp.int32), grads4, recvd1, [MXU_DTYPE] * 6 + [F32])
    recvd2 = _rs_chips_call(sums)
    idx = jnp.stack([2 * cx + cy, cc]).astype(jnp.int32)
    g_in_t, g_down, g_out, g_up, g_bp, g_ba, g_small = _final_sum_call(idx, grads4, recvd1, recvd2)
    small_all = _allgather_small_call(g_small).reshape(_SMALL_ROWS, LANES)

    big_w = [w_in[0], w_branch_pool[0], w_branch_attn[0], w_out[0], w_up[0], w_down[0]]
    big_g = [g_in_t.T, g_bp, g_ba, g_out, g_up, g_down]
    big_m = [m_w_in[0], m_w_branch_pool[0], m_w_branch_attn[0], m_w_out[0], m_w_up[0], m_w_down[0]]
    big_v = [v_w_in[0], v_w_branch_pool[0], v_w_branch_attn[0], v_w_out[0], v_w_up[0], v_w_down[0]]
    big_d, big_m2, big_v2 = _adamw_call(big_w, big_g, big_m, big_v, N_DEV, "adamw_shards")

    names = [k for k, _ in _SMALL[:-1]]
    sm_w = dict(g_mix_pre=g_mix_pre, b_in=b_in, w_pool=w_pool, pool_scale=pool_scale, attn_sinks=attn_sinks,
                g_mix_post=g_mix_post, g_mlp_pre=g_mlp_pre, g_mlp_post=g_mlp_post)
    sm_m = dict(g_mix_pre=m_g_mix_pre, b_in=m_b_in, w_pool=m_w_pool, pool_scale=m_pool_scale, attn_sinks=m_attn_sinks,
                g_mix_post=m_g_mix_post, g_mlp_pre=m_g_mlp_pre, g_mlp_post=m_g_mlp_post)
    sm_v = dict(g_mix_pre=v_g_mix_pre, b_in=v_b_in, w_pool=v_w_pool, pool_scale=v_pool_scale, attn_sinks=v_attn_sinks,
                g_mix_post=v_g_mix_post, g_mlp_pre=v_g_mlp_pre, g_mlp_post=v_g_mlp_post)
    sm_g, off = {}, 0
    for k, r in _SMALL:
        size = sm_w[k].size if k != "loss" else D_MODEL
        sm_g[k] = small_all[off:off + r].reshape(-1)[:size]
        off += r
    loss = (0.5 / D_MODEL) * jnp.sum(sm_g["loss"])
    two_d = lambda a: a.reshape(-1, a.shape[-1])
    sd_, sm2_, sv2_ = _adamw_call([two_d(sm_w[k]) for k in names], [two_d(sm_g[k].reshape(sm_w[k].shape)) for k in names],
                                  [two_d(sm_m[k]) for k in names], [two_d(sm_v[k]) for k in names], 1, "adamw_small")
    like = lambda vals: {k: a.reshape(sm_w[k].shape) for k, a in zip(names, vals)}
    sm_d, sm_m2, sm_v2 = like(sd_), like(sm2_), like(sv2_)
    sm_gr = {k: sm_g[k].reshape(sm_w[k].shape) for k in names}

    order = ["g_mix_pre", "w_in", "b_in", "w_pool", "pool_scale", "attn_sinks", "w_branch_pool", "w_branch_attn",
             "w_out", "g_mix_post", "g_mlp_pre", "w_up", "w_down", "g_mlp_post"]
    big_names = ["w_in", "w_branch_pool", "w_branch_attn", "w_out", "w_up", "w_down"]
    lead = lambda a: a[None]
    tables = []
    for small_t, big_t in ((sm_gr, big_g), (sm_d, big_d), (sm_m2, big_m2), (sm_v2, big_v2)):
        bt = dict(zip(big_names, big_t))
        tables.append([lead(bt[k]) if k in bt else small_t[k] for k in order])
    return (loss, gx.reshape(B, S, D_MODEL), *tables[0], *tables[1], *tables[2], *tables[3])
```

```python
import functools

import jax
import jax.numpy as jnp
from jax import lax
from jax.experimental import pallas as pl
from jax.experimental.pallas import tpu as pltpu

F32 = jnp.float32
MXU_DTYPE = jnp.bfloat16
MESH = pl.DeviceIdType.MESH

D_MODEL = 1024
POOL_WINDOWS = (2, 4, 8, 16)
POOL_WIDTH = 512
POOL_GC = 128
HEAD_DIM = 64
N_Q_HEADS = 8
N_KV_HEADS = 2
GROUP = 4
ATTN_WIDTH = 512
KV_WIDTH = 128
BLOCK = 128
GATE_WIDTH = 2048
IN_WIDTH = 3328
D_FF = 4096
EPS = 1e-6
NEG_INF = -1e30
ROPE_THETA = 500000.0
ROT_DIM = 16
SCALE = HEAD_DIM ** -0.5
C_Q, C_K, C_V, C_G = 512, 1024, 1152, 1280

ADAM_LR = 0.001
ADAM_B1 = 0.9
ADAM_B2 = 0.999
ADAM_EPS = 1e-08
ADAM_WD = 0.01
ADAM_STEP = 10

N_DEV = 8
LANES = 128
VMEM_LIMIT = 56 * 1024 * 1024

NN = (((1,), (0,)), ((), ()))
NT = (((1,), (1,)), ((), ()))
TN = (((0,), (0,)), ((), ()))


def _dot(a, b, dims):
    return lax.dot_general(a, b, dims, preferred_element_type=F32)


def _params(sem=None):
    return pltpu.CompilerParams(dimension_semantics=sem, vmem_limit_bytes=VMEM_LIMIT)


def _tile(n, pref):
    t = min(n, pref)
    assert n % t == 0, (n, t)
    return t


class _Rider:
    def __init__(self, ins, out_shape, n_remote, n_local, plan, aliases=None):
        self.ins, self.out_shape, self.n_remote, self.n_local = list(ins), list(out_shape), n_remote, n_local
        self.plan, self.aliases = plan, dict(aliases or {})


def _merge_riders(a, b):
    na_in, na_out = len(a.ins), len(a.out_shape)

    def plan(ins, outs, send, recv, loc, r0, l0):
        ra, la = a.plan(ins[:na_in], outs[:na_out], send, recv, loc, r0, l0)
        rb, lb = b.plan(ins[na_in:], outs[na_out:], send, recv, loc, r0 + a.n_remote, l0 + a.n_local)
        return ra + rb, la + lb

    aliases = dict(a.aliases)
    aliases.update({na_in + i: na_out + o for i, o in b.aliases.items()})
    return _Rider(a.ins + b.ins, a.out_shape + b.out_shape, a.n_remote + b.n_remote, a.n_local + b.n_local, plan, aliases)


def _launch(body, args, *, name, grid, in_specs, out_specs, out_shape, scratch_shapes=(), sem=None, rider=None):
    if rider is None:
        return pl.pallas_call(body, name=name, grid=grid, in_specs=in_specs, out_specs=out_specs, out_shape=out_shape,
                              scratch_shapes=list(scratch_shapes), compiler_params=_params(sem))(*args)
    n_in, n_out, n_scr = len(args), len(out_shape), len(scratch_shapes)
    r_in, r_out = len(rider.ins), len(rider.out_shape)

    def wrapped(*refs):
        ins, rins = refs[:n_in], refs[n_in:n_in + r_in]
        o0 = n_in + r_in
        outs, routs = refs[o0:o0 + n_out], refs[o0 + n_out:o0 + n_out + r_out]
        s0 = o0 + n_out + r_out
        scr = refs[s0:s0 + n_scr]
        send, recv, loc = refs[s0 + n_scr:]
        first, last = None, None
        for d in range(len(grid)):
            f, l = pl.program_id(d) == 0, pl.program_id(d) == pl.num_programs(d) - 1
            first = f if first is None else first & f
            last = l if last is None else last & l

        def start():
            remote, local = rider.plan(rins, routs, send, recv, loc, 0, 0)
            for cp in local + remote:
                cp.start()

        def finish():
            remote, local = rider.plan(rins, routs, send, recv, loc, 0, 0)
            for cp in remote + local:
                cp.wait()

        if first is None:
            start()
            body(*ins, *outs, *scr)
            finish()
        else:
            pl.when(first)(start)
            body(*ins, *outs, *scr)
            pl.when(last)(finish)

    hbm = pl.BlockSpec(memory_space=pl.ANY)
    dma = pltpu.SemaphoreType.DMA
    res = pl.pallas_call(
        wrapped, name=name, grid=grid, in_specs=list(in_specs) + [hbm] * r_in,
        out_specs=list(out_specs) + [hbm] * r_out, out_shape=list(out_shape) + rider.out_shape,
        scratch_shapes=list(scratch_shapes) + [dma((rider.n_remote,)), dma((rider.n_remote,)), dma((max(rider.n_local, 1),))],
        input_output_aliases={n_in + i: n_out + o for i, o in rider.aliases.items()},
        compiler_params=_params(sem),
    )(*args, *rider.ins)
    return list(res[:n_out]), list(res[n_out:])


def _comm_call(rider, name):
    return _launch(lambda: None, [], name=name, grid=(), in_specs=[], out_specs=[], out_shape=[], rider=rider)[1]


def _rms_r(x):
    return lax.rsqrt(jnp.mean(x * x, axis=-1, keepdims=True) + EPS)


def _rms_bwd(dn, x, r, g):
    xh = x * r
    dxh = dn * g
    dx = r * (dxh - xh * jnp.mean(dxh * xh, axis=-1, keepdims=True))
    return dx, dn * xh


def _rot(t, c, sa, sb):
    outs = []
    for j in range(t.shape[1] // LANES):
        tj = t[:, LANES * j:LANES * (j + 1)]
        outs.append(tj * c + pltpu.roll(tj, LANES - 8, 1) * sa + pltpu.roll(tj, 8, 1) * sb)
    return outs[0] if len(outs) == 1 else jnp.concatenate(outs, axis=1)


def _rot_tables(S):
    pos = jnp.arange(S, dtype=F32)
    inv_freq = ROPE_THETA ** (-jnp.arange(0, ROT_DIM, 2, dtype=F32) / ROT_DIM)
    ang = pos[:, None] * inv_freq[None, :]
    cos, sin = jnp.cos(ang), jnp.sin(ang)
    one = jnp.ones((S, HEAD_DIM - ROT_DIM), F32)
    zero = jnp.zeros((S, HEAD_DIM - ROT_DIM), F32)
    z8 = jnp.zeros((S, 8), F32)
    c = jnp.concatenate([cos, cos, one], axis=1)
    sa = jnp.concatenate([-sin, z8, zero], axis=1)
    sb = jnp.concatenate([z8, sin, zero], axis=1)
    rep = LANES // HEAD_DIM
    return jnp.tile(c, (1, rep)), jnp.tile(sa, (1, rep)), jnp.tile(sb, (1, rep))


def _lane_tile4(k):
    lane = lax.broadcasted_iota(jnp.int32, k.shape, 1)
    rk = pltpu.roll(k, HEAD_DIM, 1)
    x0 = jnp.where(lane < HEAD_DIM, k, rk)
    x1 = jnp.where(lane < HEAD_DIM, rk, k)
    return jnp.concatenate([x0, x0, x1, x1], axis=1)


def _fold_heads(acc):
    zs = []
    for hk in range(N_KV_HEADS):
        a = acc[:, 256 * hk:256 * hk + LANES] + acc[:, 256 * hk + LANES:256 * (hk + 1)]
        zs.append(a + pltpu.roll(a, HEAD_DIM, 1))
    lane = lax.broadcasted_iota(jnp.int32, zs[0].shape, 1)
    return jnp.where(lane < HEAD_DIM, zs[0], zs[1])


def _inproj_call(x, g1, win_t, b_in, rc, rsa, rsb, S, rider=None):
    T = x.shape[0]
    tm = _tile(S, 512)
    nst = S // tm

    def body(x_ref, g1_ref, w_ref, b_ref, c_ref, sa_ref, sb_ref,
             h_ref, u_ref, q_ref, k4_ref, v4_ref, g_ref):
        xv = x_ref[...]
        hb = ((xv * _rms_r(xv)) * g1_ref[...]).astype(MXU_DTYPE)
        h_ref[...] = hb

        def proj(lo, hi):
            return _dot(hb, w_ref[lo:hi, :], NT) + b_ref[:, lo:hi]

        c, sa, sb = c_ref[...], sa_ref[...], sb_ref[...]
        u_ref[...] = proj(0, C_Q)
        q_ref[...] = _rot(proj(C_Q, C_K), c, sa, sb).astype(MXU_DTYPE)
        kv = proj(C_K, C_G)
        k4_ref[...] = _lane_tile4(_rot(kv[:, :KV_WIDTH], c, sa, sb)).astype(MXU_DTYPE)
        v4_ref[...] = _lane_tile4(kv[:, KV_WIDTH:]).astype(MXU_DTYPE)
        g_ref[...] = jax.nn.sigmoid(proj(C_G, IN_WIDTH))

    tok = lambda w: pl.BlockSpec((tm, w), lambda i: (i, 0))
    full = lambda a: pl.BlockSpec(a.shape, lambda i: (0,) * a.ndim)
    tab = pl.BlockSpec((tm, LANES), lambda i: (i % nst, 0))
    return _launch(
        body, [x, g1, win_t, b_in, rc, rsa, rsb], name="inproj_fwd", grid=(T // tm,),
        in_specs=[tok(D_MODEL), full(g1), full(win_t), full(b_in), tab, tab, tab],
        out_specs=[tok(D_MODEL), tok(POOL_WIDTH), tok(ATTN_WIDTH), tok(512), tok(512), tok(GATE_WIDTH)],
        out_shape=[jax.ShapeDtypeStruct((T, D_MODEL), MXU_DTYPE), jax.ShapeDtypeStruct((T, POOL_WIDTH), F32),
                   jax.ShapeDtypeStruct((T, ATTN_WIDTH), MXU_DTYPE), jax.ShapeDtypeStruct((T, 512), MXU_DTYPE),
                   jax.ShapeDtypeStruct((T, 512), MXU_DTYPE), jax.ShapeDtypeStruct((T, GATE_WIDTH), F32)],
        sem=("arbitrary",), rider=rider)


def _shift_rows(a, k, rows):
    n = a.shape[0]
    if k > 0:
        return jnp.where(rows >= k, pltpu.roll(a, k, 0), 0.0)
    return jnp.where(rows < n + k, pltpu.roll(a, n + k, 0), 0.0)


def _win_sum(a, w, rows, sign):
    s, k = a, 1
    while k < w:
        s = s + _shift_rows(s, sign * k, rows)
        k *= 2
    return s


def _pool_diff(ug, w, rows):
    inv = 1.0 / jnp.minimum(rows + 1, w).astype(F32)
    return _win_sum(ug, w, rows, 1) * inv - ug, inv


def _pool_call(u, w_pool, pool_scale, S):
    T = u.shape[0]

    def body(u_ref, w_ref, ps_ref, y_ref):
        rows = lax.broadcasted_iota(jnp.int32, (S, POOL_GC), 0)
        for gi, w in enumerate(POOL_WINDOWS):
            sl = slice(POOL_GC * gi, POOL_GC * (gi + 1))
            diff, _ = _pool_diff(u_ref[:, sl], w, rows)
            mixed = _dot(diff.astype(MXU_DTYPE), w_ref[gi], NN)
            y_ref[:, sl] = (mixed * ps_ref[:, sl]).astype(MXU_DTYPE)

    seq = pl.BlockSpec((S, POOL_WIDTH), lambda b: (b, 0))
    return pl.pallas_call(
        body, name="pool_fwd", grid=(T // S,),
        in_specs=[seq, pl.BlockSpec(w_pool.shape, lambda b: (0, 0, 0)), pl.BlockSpec(pool_scale.shape, lambda b: (0, 0))],
        out_specs=seq, out_shape=jax.ShapeDtypeStruct((T, POOL_WIDTH), MXU_DTYPE),
        compiler_params=_params(("arbitrary",)),
    )(u, w_pool, pool_scale)


def _pool_bwd_call(u, dyp, w_pool, pool_scale, S):
    T = u.shape[0]

    def body(u_ref, dy_ref, w_ref, ps_ref, du_ref, dw_ref, dps_ref):
        @pl.when(pl.program_id(0) == 0)
        def _():
            dw_ref[...] = jnp.zeros_like(dw_ref)
            dps_ref[...] = jnp.zeros_like(dps_ref)

        rows = lax.broadcasted_iota(jnp.int32, (S, POOL_GC), 0)
        for gi, w in enumerate(POOL_WINDOWS):
            sl = slice(POOL_GC * gi, POOL_GC * (gi + 1))
            diff, inv = _pool_diff(u_ref[:, sl], w, rows)
            diffb = diff.astype(MXU_DTYPE)
            wg = w_ref[gi]
            mixed = _dot(diffb, wg, NN)
            dy = dy_ref[:, sl]
            dps_ref[:, sl] += jnp.sum(dy * mixed, axis=0, keepdims=True)
            dmb = (dy * ps_ref[:, sl]).astype(MXU_DTYPE)
            dw_ref[gi] += _dot(diffb, dmb, TN)
            ddiff = _dot(dmb, wg, NT)
            du_ref[:, sl] = (_win_sum(ddiff * inv, w, rows, -1) - ddiff).astype(MXU_DTYPE)

    seq = pl.BlockSpec((S, POOL_WIDTH), lambda b: (b, 0))
    return pl.pallas_call(
        body, name="pool_bwd", grid=(T // S,),
        in_specs=[seq, seq, pl.BlockSpec(w_pool.shape, lambda b: (0, 0, 0)), pl.BlockSpec(pool_scale.shape, lambda b: (0, 0))],
        out_specs=[seq, pl.BlockSpec(w_pool.shape, lambda b: (0, 0, 0)), pl.BlockSpec(pool_scale.shape, lambda b: (0, 0))],
        out_shape=[jax.ShapeDtypeStruct((T, POOL_WIDTH), MXU_DTYPE), jax.ShapeDtypeStruct(w_pool.shape, F32),
                   jax.ShapeDtypeStruct(pool_scale.shape, F32)],
        compiler_params=_params(("arbitrary",)),
    )(u, dyp, w_pool, pool_scale)


def _attn_consts():
    lane_g = lax.broadcasted_iota(jnp.int32, (BLOCK, 256), 1) >> 6
    row = lax.broadcasted_iota(jnp.int32, (GROUP * BLOCK, 256), 0)
    col = lax.broadcasted_iota(jnp.int32, (GROUP * BLOCK, 256), 1)
    rel0 = (row & (BLOCK - 1)) - col
    rgrp = lax.broadcasted_iota(jnp.int32, (GROUP * BLOCK, 1), 0) >> 7
    return lane_g, rel0, rgrp


def _sink_rows(sink_ref, hk, rgrp):
    sv = jnp.zeros(rgrp.shape, F32)
    for g in range(GROUP):
        sv = jnp.where(rgrp == g, sink_ref[0, GROUP * hk + g], sv)
    return sv


def _stack_heads(xb, lane_g):
    return jnp.concatenate([jnp.where(lane_g == g, xb, jnp.zeros_like(xb)) for g in range(GROUP)], axis=0)


def _unstack_heads(xs, lane_g):
    out = jnp.where(lane_g == 0, xs[0:BLOCK], 0.0)
    for g in range(1, GROUP):
        out = out + jnp.where(lane_g == g, xs[BLOCK * g:BLOCK * (g + 1)], 0.0)
    return out


def _attn_probs(qs, kb, off, rel0, sv):
    s = _dot(qs, kb, NT) * SCALE
    rel = rel0 + off
    s = jnp.where((rel >= 0) & (rel < BLOCK), s, NEG_INF)
    m = jnp.maximum(jnp.max(s, axis=1, keepdims=True), sv)
    e = jnp.exp(s - m)
    es = jnp.exp(sv - m)
    inv_l = 1.0 / (jnp.sum(e, axis=1, keepdims=True) + es)
    return e * inv_l, es * inv_l


def _attn_call(sinks, q, k4, v4, S, rider=None):
    T = q.shape[0]
    nb = S // BLOCK

    def body(sink_ref, q_ref, k_ref, v_ref, o_ref):
        lane_g, rel0, rgrp = _attn_consts()
        for hk in range(N_KV_HEADS):
            cs = slice(256 * hk, 256 * (hk + 1))
            sv = _sink_rows(sink_ref, hk, rgrp)

            def blk(n, carry):
                q0 = pl.multiple_of(n * BLOCK, BLOCK)
                k0 = pl.multiple_of(jnp.maximum(n - 1, 0) * BLOCK, BLOCK)
                qs = _stack_heads(q_ref[pl.ds(q0, BLOCK), cs], lane_g)
                p, _ = _attn_probs(qs, k_ref[pl.ds(k0, 2 * BLOCK), cs], q0 - k0, rel0, sv)
                o = _dot(p.astype(MXU_DTYPE), v_ref[pl.ds(k0, 2 * BLOCK), cs], NN)
                o_ref[pl.ds(q0, BLOCK), cs] = _unstack_heads(o, lane_g).astype(MXU_DTYPE)
                return carry

            lax.fori_loop(0, nb, blk, 0)

    seq = pl.BlockSpec((S, ATTN_WIDTH), lambda b: (b, 0))
    return _launch(
        body, [sinks, q, k4, v4], name="attn_fwd", grid=(T // S,),
        in_specs=[pl.BlockSpec(memory_space=pltpu.SMEM), seq, seq, seq],
        out_specs=[seq], out_shape=[jax.ShapeDtypeStruct((T, ATTN_WIDTH), MXU_DTYPE)],
        sem=("arbitrary",), rider=rider)


def _attn_bwd_call(sinks, q, k4, v4, do, rc, rsa, rsb, S, rider=None):
    T = q.shape[0]
    nb = S // BLOCK

    def body(sink_ref, q_ref, k_ref, v_ref, do_ref, c_ref, sa_ref, sb_ref,
             dq_ref, dk_ref, dv_ref, ds_ref, dk_acc, dv_acc):
        lane_g, rel0, rgrp = _attn_consts()
        lane1 = lax.broadcasted_iota(jnp.int32, (1, LANES), 1)
        dk_acc[...] = jnp.zeros_like(dk_acc)
        dv_acc[...] = jnp.zeros_like(dv_acc)
        dsink = jnp.zeros((1, LANES), F32)
        for hk in range(N_KV_HEADS):
            cs = slice(256 * hk, 256 * (hk + 1))
            sv = _sink_rows(sink_ref, hk, rgrp)

            def blk(n, dsink):
                q0 = pl.multiple_of(n * BLOCK, BLOCK)
                k0 = pl.multiple_of(jnp.maximum(n - 1, 0) * BLOCK, BLOCK)
                qs = _stack_heads(q_ref[pl.ds(q0, BLOCK), cs], lane_g)
                dos = _stack_heads(do_ref[pl.ds(q0, BLOCK), cs], lane_g)
                kb = k_ref[pl.ds(k0, 2 * BLOCK), cs]
                vb = v_ref[pl.ds(k0, 2 * BLOCK), cs]
                p, ps = _attn_probs(qs, kb, q0 - k0, rel0, sv)
                dp = _dot(dos, vb, NT)
                delta = jnp.sum(p * dp, axis=1, keepdims=True)
                dsb = (p * (dp - delta) * SCALE).astype(MXU_DTYPE)
                dqb = _unstack_heads(_dot(dsb, kb, NN), lane_g)
                rows = pl.ds(q0, BLOCK)
                dq_ref[rows, cs] = _rot(dqb, c_ref[rows, :], -sa_ref[rows, :], -sb_ref[rows, :]).astype(MXU_DTYPE)
                dk_acc[pl.ds(k0, 2 * BLOCK), cs] += _dot(dsb, qs, TN)
                dv_acc[pl.ds(k0, 2 * BLOCK), cs] += _dot(p.astype(MXU_DTYPE), dos, TN)
                psd = ps * delta
                for g in range(GROUP):
                    val = -jnp.sum(psd[BLOCK * g:BLOCK * (g + 1)], axis=0, keepdims=True)
                    dsink = dsink + jnp.where(lane1 == GROUP * hk + g, val, 0.0)
                return dsink

            dsink = lax.fori_loop(0, nb, blk, dsink)
        dk_ref[...] = _rot(_fold_heads(dk_acc[...]), c_ref[...], -sa_ref[...], -sb_ref[...]).astype(MXU_DTYPE)
        dv_ref[...] = _fold_heads(dv_acc[...]).astype(MXU_DTYPE)
        ds_ref[...] = jnp.broadcast_to(dsink, ds_ref.shape)

    seq = pl.BlockSpec((S, ATTN_WIDTH), lambda b: (b, 0))
    kvs = pl.BlockSpec((S, KV_WIDTH), lambda b: (b, 0))
    tab = pl.BlockSpec((S, LANES), lambda b: (0, 0))
    nseq = T // S
    return _launch(
        body, [sinks, q, k4, v4, do, rc, rsa, rsb], name="attn_bwd", grid=(nseq,),
        in_specs=[pl.BlockSpec(memory_space=pltpu.SMEM), seq, seq, seq, seq, tab, tab, tab],
        out_specs=[seq, kvs, kvs, pl.BlockSpec((8, LANES), lambda b: (b, 0))],
        out_shape=[jax.ShapeDtypeStruct((T, ATTN_WIDTH), MXU_DTYPE), jax.ShapeDtypeStruct((T, KV_WIDTH), MXU_DTYPE),
                   jax.ShapeDtypeStruct((T, KV_WIDTH), MXU_DTYPE), jax.ShapeDtypeStruct((8 * nseq, LANES), F32)],
        scratch_shapes=[pltpu.VMEM((S, 512), F32), pltpu.VMEM((S, 512), F32)],
        sem=("arbitrary",), rider=rider)


def _branch_weights(wbp_ref, wba_ref, wbp_s, wba_s):
    @pl.when(pl.program_id(0) == 0)
    def _():
        for j in range(N_DEV):
            wbp_s[:, LANES * j:LANES * (j + 1)] = wbp_ref[j]
            wba_s[:, LANES * j:LANES * (j + 1)] = wba_ref[j]


def _mix_fwd_call(yp, ya, g, x, wbp, wba, wout, g2, g3, rider=None):
    T = x.shape[0]
    tm = _tile(T, 256)

    def body(yp_ref, ya_ref, g_ref, x_ref, wbp_ref, wba_ref, wout_ref, g2_ref, g3_ref,
             mix_ref, x1_ref, h2_ref, wbp_s, wba_s):
        _branch_weights(wbp_ref, wba_ref, wbp_s, wba_s)
        bp = _dot(yp_ref[...], wbp_s[...], NN)
        ba = _dot(ya_ref[...], wba_s[...], NN)
        merged = g_ref[:, :D_MODEL] * bp + g_ref[:, D_MODEL:] * ba
        mix = _dot(merged.astype(MXU_DTYPE), wout_ref[...], NN)
        mix_ref[...] = mix
        x1 = x_ref[...] + (mix * _rms_r(mix)) * g2_ref[...]
        x1_ref[...] = x1
        h2_ref[...] = ((x1 * _rms_r(x1)) * g3_ref[...]).astype(MXU_DTYPE)

    tok = lambda w: pl.BlockSpec((tm, w), lambda i: (i, 0))
    full = lambda a: pl.BlockSpec(a.shape, lambda i: (0,) * a.ndim)
    return _launch(
        body, [yp, ya, g, x, wbp, wba, wout, g2, g3], name="mix_fwd", grid=(T // tm,),
        in_specs=[tok(POOL_WIDTH), tok(ATTN_WIDTH), tok(GATE_WIDTH), tok(D_MODEL), full(wbp), full(wba), full(wout),
                  full(g2), full(g3)],
        out_specs=[tok(D_MODEL), tok(D_MODEL), tok(D_MODEL)],
        out_shape=[jax.ShapeDtypeStruct((T, D_MODEL), F32), jax.ShapeDtypeStruct((T, D_MODEL), F32),
                   jax.ShapeDtypeStruct((T, D_MODEL), MXU_DTYPE)],
        scratch_shapes=[pltpu.VMEM((POOL_WIDTH, D_MODEL), MXU_DTYPE), pltpu.VMEM((ATTN_WIDTH, D_MODEL), MXU_DTYPE)],
        sem=("arbitrary",), rider=rider)


def _mix_bwd_call(dx1, mix, yp, ya, g, wbp, wba, wout, g2, rider=None):
    T = dx1.shape[0]
    tm = _tile(T, 256)

    def body(dx1_ref, mix_ref, yp_ref, ya_ref, g_ref, wbp_ref, wba_ref, wout_ref, g2_ref,
             dmix_ref, merged_ref, dbp_ref, dba_ref, dyp_ref, do_ref, dgates_ref, dg2_ref, dbg_ref, wbp_s, wba_s):
        _branch_weights(wbp_ref, wba_ref, wbp_s, wba_s)

        @pl.when(pl.program_id(0) == 0)
        def _():
            dg2_ref[...] = jnp.zeros_like(dg2_ref)
            dbg_ref[...] = jnp.zeros_like(dbg_ref)

        mix = mix_ref[...]
        dmix, dg2 = _rms_bwd(dx1_ref[...], mix, _rms_r(mix), g2_ref[...])
        dg2_ref[...] += jnp.sum(dg2, axis=0, keepdims=True)
        dmixb = dmix.astype(MXU_DTYPE)
        dmix_ref[...] = dmixb
        dmerged = _dot(dmixb, wout_ref[...], NT)
        bp = _dot(yp_ref[...], wbp_s[...], NN)
        ba = _dot(ya_ref[...], wba_s[...], NN)
        gp, ga = g_ref[:, :D_MODEL], g_ref[:, D_MODEL:]
        merged_ref[...] = (gp * bp + ga * ba).astype(MXU_DTYPE)
        dgp = dmerged * bp * (gp * (1.0 - gp))
        dga = dmerged * ba * (ga * (1.0 - ga))
        dbg_ref[:, :D_MODEL] += jnp.sum(dgp, axis=0, keepdims=True)
        dbg_ref[:, D_MODEL:] += jnp.sum(dga, axis=0, keepdims=True)
        dgates_ref[:, :D_MODEL] = dgp.astype(MXU_DTYPE)
        dgates_ref[:, D_MODEL:] = dga.astype(MXU_DTYPE)
        dbp = (dmerged * gp).astype(MXU_DTYPE)
        dba = (dmerged * ga).astype(MXU_DTYPE)
        dbp_ref[...] = dbp
        dba_ref[...] = dba
        dyp_ref[...] = _dot(dbp, wbp_s[...], NT)
        do_ref[...] = _dot(dba, wba_s[...], NT).astype(MXU_DTYPE)

    tok = lambda w: pl.BlockSpec((tm, w), lambda i: (i, 0))
    full = lambda a: pl.BlockSpec(a.shape, lambda i: (0,) * a.ndim)
    acc = lambda w: pl.BlockSpec((1, w), lambda i: (0, 0))
    sd = jax.ShapeDtypeStruct
    return _launch(
        body, [dx1, mix, yp, ya, g, wbp, wba, wout, g2], name="mix_bwd", grid=(T // tm,),
        in_specs=[tok(D_MODEL), tok(D_MODEL), tok(POOL_WIDTH), tok(ATTN_WIDTH), tok(GATE_WIDTH), full(wbp), full(wba),
                  full(wout), full(g2)],
        out_specs=[tok(D_MODEL), tok(D_MODEL), tok(D_MODEL), tok(D_MODEL), tok(POOL_WIDTH), tok(ATTN_WIDTH),
                   tok(GATE_WIDTH), acc(D_MODEL), acc(GATE_WIDTH)],
        out_shape=[sd((T, D_MODEL), MXU_DTYPE), sd((T, D_MODEL), MXU_DTYPE), sd((T, D_MODEL), MXU_DTYPE),
                   sd((T, D_MODEL), MXU_DTYPE), sd((T, POOL_WIDTH), F32), sd((T, ATTN_WIDTH), MXU_DTYPE),
                   sd((T, GATE_WIDTH), MXU_DTYPE), sd((1, D_MODEL), F32), sd((1, GATE_WIDTH), F32)],
        scratch_shapes=[pltpu.VMEM((POOL_WIDTH, D_MODEL), MXU_DTYPE), pltpu.VMEM((ATTN_WIDTH, D_MODEL), MXU_DTYPE)],
        sem=("arbitrary",), rider=rider)


def _mlp_call(x1, h2, target, wup, wdown, g3, g4):
    T = x1.shape[0]
    tm = _tile(T, 256)
    fc = D_FF // N_DEV

    def body(x1_ref, h2_ref, t_ref, wup_ref, wdown_ref, g3_ref, g4_ref,
             act_ref, da_ref, dff_ref, dx1_ref, dg3_ref, dg4_ref, loss_ref, rl_s):
        @pl.when(pl.program_id(0) == 0)
        def _():
            dg3_ref[...] = jnp.zeros_like(dg3_ref)
            dg4_ref[...] = jnp.zeros_like(dg4_ref)
            loss_ref[...] = jnp.zeros_like(loss_ref)

        h2 = h2_ref[...]
        ff = jnp.zeros((tm, D_MODEL), F32)
        for j in range(N_DEV):
            sl = slice(fc * j, fc * (j + 1))
            rl = jnp.maximum(_dot(h2, wup_ref[j], NN), 0.0)
            rl_s[:, sl] = rl
            actb = (rl * rl).astype(MXU_DTYPE)
            act_ref[:, sl] = actb
            ff = ff + _dot(actb, wdown_ref[j], NN)
        x1 = x1_ref[...]
        r4 = _rms_r(ff)
        err = x1 + (ff * r4) * g4_ref[...] - t_ref[...]
        loss_ref[...] += jnp.sum(err * err, axis=0, keepdims=True)
        dy = err * (1.0 / D_MODEL)
        dff, dg4 = _rms_bwd(dy, ff, r4, g4_ref[...])
        dg4_ref[...] += jnp.sum(dg4, axis=0, keepdims=True)
        dffb = dff.astype(MXU_DTYPE)
        dff_ref[...] = dffb
        dh2 = jnp.zeros((tm, D_MODEL), F32)
        for j in range(N_DEV):
            sl = slice(fc * j, fc * (j + 1))
            dab = (_dot(dffb, wdown_ref[j], NT) * (2.0 * rl_s[:, sl])).astype(MXU_DTYPE)
            da_ref[:, sl] = dab
            dh2 = dh2 + _dot(dab, wup_ref[j], NT)
        dx1, dg3 = _rms_bwd(dh2, x1, _rms_r(x1), g3_ref[...])
        dg3_ref[...] += jnp.sum(dg3, axis=0, keepdims=True)
        dx1_ref[...] = dy + dx1

    tok = lambda w: pl.BlockSpec((tm, w), lambda i: (i, 0))
    full = lambda a: pl.BlockSpec(a.shape, lambda i: (0,) * a.ndim, pipeline_mode=pl.Buffered(1))
    vec = pl.BlockSpec((1, D_MODEL), lambda i: (0, 0))
    sd = jax.ShapeDtypeStruct
    return pl.pallas_call(
        body, name="mlp_fwd_bwd", grid=(T // tm,),
        in_specs=[tok(D_MODEL), tok(D_MODEL), tok(D_MODEL), full(wup), full(wdown), vec, vec],
        out_specs=[tok(D_FF), tok(D_FF), tok(D_MODEL), tok(D_MODEL), vec, vec, vec],
        out_shape=[sd((T, D_FF), MXU_DTYPE), sd((T, D_FF), MXU_DTYPE), sd((T, D_MODEL), MXU_DTYPE),
                   sd((T, D_MODEL), F32), sd((1, D_MODEL), F32), sd((1, D_MODEL), F32), sd((1, D_MODEL), F32)],
        scratch_shapes=[pltpu.VMEM((tm, D_FF), F32)],
        compiler_params=_params(("arbitrary",)),
    )(x1, h2, target, wup, wdown, g3, g4)


def _inproj_bwd_call(du, dq, dk, dv, dgates, dx1, x, win_t, g1, rider=None):
    T = x.shape[0]
    tm = _tile(T, 256)

    def body(du_ref, dq_ref, dk_ref, dv_ref, dgt_ref, dx1_ref, x_ref, w_ref, g1_ref, gx_ref, dg1_ref, db_ref):
        @pl.when(pl.program_id(0) == 0)
        def _():
            dg1_ref[...] = jnp.zeros_like(dg1_ref)
            db_ref[...] = jnp.zeros_like(db_ref)

        dh = jnp.zeros((tm, D_MODEL), F32)
        for ref, lo, hi in ((du_ref, 0, C_Q), (dq_ref, C_Q, C_K), (dk_ref, C_K, C_V), (dv_ref, C_V, C_G),
                            (dgt_ref, C_G, IN_WIDTH)):
            piece = ref[...]
            dh = dh + _dot(piece, w_ref[lo:hi, :], NN)
            if hi <= C_G:
                db_ref[:, lo:hi] += jnp.sum(piece.astype(F32), axis=0, keepdims=True)
        xv = x_ref[...]
        dx, dg1 = _rms_bwd(dh, xv, _rms_r(xv), g1_ref[...])
        dg1_ref[...] += jnp.sum(dg1, axis=0, keepdims=True)
        gx_ref[...] = dx1_ref[...] + dx

    tok = lambda w: pl.BlockSpec((tm, w), lambda i: (i, 0))
    full = lambda a: pl.BlockSpec(a.shape, lambda i: (0,) * a.ndim)
    sd = jax.ShapeDtypeStruct
    return _launch(
        body, [du, dq, dk, dv, dgates, dx1, x, win_t, g1], name="inproj_bwd", grid=(T // tm,),
        in_specs=[tok(POOL_WIDTH), tok(ATTN_WIDTH), tok(KV_WIDTH), tok(KV_WIDTH), tok(GATE_WIDTH), tok(D_MODEL),
                  tok(D_MODEL), full(win_t), full(g1)],
        out_specs=[tok(D_MODEL), pl.BlockSpec((1, D_MODEL), lambda i: (0, 0)), pl.BlockSpec((1, C_G), lambda i: (0, 0))],
        out_shape=[sd((T, D_MODEL), F32), sd((1, D_MODEL), F32), sd((1, C_G), F32)],
        sem=("arbitrary",), rider=rider)


def _wgrad_rows_call(a, b, name):
    T, K = a.shape
    N = b.shape[1]
    tm = _tile(T, 512)
    kb = min(K, 1024)
    per = kb // (K // N_DEV)

    def body(a_ref, b_ref, o_ref):
        @pl.when(pl.program_id(1) == 0)
        def _():
            o_ref[...] = jnp.zeros_like(o_ref)

        d = _dot(a_ref[...], b_ref[...], TN)
        rs = kb // per
        for j in range(per):
            o_ref[j] += d[rs * j:rs * (j + 1)]

    return pl.pallas_call(
        body, name=name, grid=(K // kb, T // tm),
        in_specs=[pl.BlockSpec((tm, kb), lambda i, t: (t, i)), pl.BlockSpec((tm, N), lambda i, t: (t, 0))],
        out_specs=pl.BlockSpec((per, K // N_DEV, N), lambda i, t: (i, 0, 0)),
        out_shape=jax.ShapeDtypeStruct((N_DEV, K // N_DEV, N), F32),
        compiler_params=_params(("arbitrary", "arbitrary")),
    )(a, b)


def _wgrad_cols_call(a, b, name, rider=None):
    T, K = a.shape
    N = b.shape[1]
    tm = _tile(T, 512)
    nb = min(N, 1024)
    per = nb // (N // N_DEV)

    def body(a_ref, b_ref, o_ref):
        @pl.when(pl.program_id(1) == 0)
        def _():
            o_ref[...] = jnp.zeros_like(o_ref)

        d = _dot(a_ref[...], b_ref[...], TN)
        cs = nb // per
        for j in range(per):
            o_ref[j] += d[:, cs * j:cs * (j + 1)]

    return _launch(
        body, [a, b], name=name, grid=(N // nb, T // tm),
        in_specs=[pl.BlockSpec((tm, K), lambda i, t: (t, 0)), pl.BlockSpec((tm, nb), lambda i, t: (t, i))],
        out_specs=[pl.BlockSpec((per, K, N // N_DEV), lambda i, t: (i, 0, 0))],
        out_shape=[jax.ShapeDtypeStruct((N_DEV, K, N // N_DEV), F32)],
        sem=("arbitrary", "arbitrary"), rider=rider)


def _wgrad_in_call(du, dq, dk, dv, dgates, h, rider=None):
    T = h.shape[0]
    tm = _tile(T, 512)
    rows = IN_WIDTH // N_DEV

    def body(du_ref, dq_ref, dk_ref, dv_ref, dgt_ref, h_ref, o_ref, acc, sem):
        t = pl.program_id(0)

        @pl.when(t == 0)
        def _():
            acc[...] = jnp.zeros_like(acc)

        hv = h_ref[...]
        for ref, lo, hi in ((du_ref, 0, C_Q), (dq_ref, C_Q, C_K), (dk_ref, C_K, C_V), (dv_ref, C_V, C_G),
                            (dgt_ref, C_G, IN_WIDTH)):
            acc[lo:hi, :] += _dot(ref[...], hv, TN)

        @pl.when(t == pl.num_programs(0) - 1)
        def _():
            copies = [pltpu.make_async_copy(acc.at[pl.ds(rows * j, rows), :], o_ref.at[j], sem.at[j])
                      for j in range(N_DEV)]
            for cp in copies:
                cp.start()
            for cp in copies:
                cp.wait()

    tok = lambda w: pl.BlockSpec((tm, w), lambda t: (t, 0))
    return _launch(
        body, [du, dq, dk, dv, dgates, h], name="wgrad_in", grid=(T // tm,),
        in_specs=[tok(POOL_WIDTH), tok(ATTN_WIDTH), tok(KV_WIDTH), tok(KV_WIDTH), tok(GATE_WIDTH), tok(D_MODEL)],
        out_specs=[pl.BlockSpec(memory_space=pl.ANY)],
        out_shape=[jax.ShapeDtypeStruct((N_DEV, rows, D_MODEL), F32)],
        scratch_shapes=[pltpu.VMEM((IN_WIDTH, D_MODEL), F32), pltpu.SemaphoreType.DMA((N_DEV,))],
        sem=("arbitrary",), rider=rider)


def _coords():
    return lax.axis_index("x"), lax.axis_index("y"), lax.axis_index("c")


def _allgather_call(shards):
    n = len(shards)

    def body(*refs):
        ins, outs = refs[:n], refs[n:2 * n]
        send_sems, recv_sems, local_sems = refs[2 * n:]
        x, y, c = _coords()
        me, sibling = (x, y, c), (x, y, 1 - c)
        chips = [(1 - x, y), (x, 1 - y), (1 - x, 1 - y)]

        def slot(p):
            return 4 * p[0] + 2 * p[1] + p[2]

        def copy(t, k, block, to, src=None):
            dst = outs[t].at[slot(block)]
            return pltpu.make_async_remote_copy(
                src_ref=dst if src is None else src, dst_ref=dst, send_sem=send_sems.at[t, k],
                recv_sem=recv_sems.at[t, k], device_id=to, device_id_type=MESH)

        mine = [pltpu.make_async_copy(ins[t], outs[t].at[slot(me)], local_sems.at[t]) for t in range(n)]
        for cp in mine:
            cp.start()
        first = []
        for t in range(n):
            first.append(copy(t, 0, me, sibling, src=ins[t]))
            first += [copy(t, 1 + j, me, (*chip, c), src=ins[t]) for j, chip in enumerate(chips)]
        for cp in first:
            cp.start()
        passed = []
        for t in range(n):
            for j, chip in enumerate(chips):
                copy(t, 1 + j, (*chip, c), me).wait_recv()
                fwd = copy(t, 4 + j, (*chip, c), sibling)
                fwd.start()
                passed.append(fwd)
        for t in range(n):
            copy(t, 0, sibling, me).wait_recv()
            for j, chip in enumerate(chips):
                copy(t, 4 + j, (*chip, 1 - c), me).wait_recv()
        for cp in first + passed:
            cp.wait_send()
        for cp in mine:
            cp.wait()

    hbm = pl.BlockSpec(memory_space=pl.ANY)
    return pl.pallas_call(
        body, name="allgather_weights",
        in_specs=[hbm] * n, out_specs=[hbm] * n,
        out_shape=[jax.ShapeDtypeStruct((N_DEV,) + s.shape, s.dtype) for s in shards],
        scratch_shapes=[pltpu.SemaphoreType.DMA((n, 7)), pltpu.SemaphoreType.DMA((n, 7)), pltpu.SemaphoreType.DMA((n,))],
    )(*shards)


def _slot(p):
    return 4 * p[0] + 2 * p[1] + p[2]


def _rider_ag_stage1(shards):
    n = len(shards)

    def plan(ins, outs, send, recv, loc, r0, l0):
        x, y, c = _coords()
        peers = [(x, y, 1 - c), (1 - x, y, c), (x, 1 - y, c), (1 - x, 1 - y, c)]
        remote, local = [], []
        for t in range(n):
            dst = outs[t].at[_slot((x, y, c))]
            local.append(pltpu.make_async_copy(ins[t], dst, loc.at[l0 + t]))
            for k, peer in enumerate(peers):
                remote.append(pltpu.make_async_remote_copy(
                    src_ref=ins[t], dst_ref=dst, send_sem=send.at[r0 + 4 * t + k], recv_sem=recv.at[r0 + 4 * t + k],
                    device_id=peer, device_id_type=MESH))
        return remote, local

    return _Rider(shards, [jax.ShapeDtypeStruct((N_DEV,) + s.shape, s.dtype) for s in shards], 4 * n, n, plan)


def _rider_ag_forward(bufs):
    n = len(bufs)

    def plan(ins, outs, send, recv, loc, r0, l0):
        x, y, c = _coords()
        remote = []
        for t in range(n):
            for j, chip in enumerate([(1 - x, y), (x, 1 - y), (1 - x, 1 - y)]):
                s = _slot((*chip, c))
                remote.append(pltpu.make_async_remote_copy(
                    src_ref=ins[t].at[s], dst_ref=outs[t].at[s], send_sem=send.at[r0 + 3 * t + j],
                    recv_sem=recv.at[r0 + 3 * t + j], device_id=(x, y, 1 - c), device_id_type=MESH))
        return remote, []

    return _Rider(bufs, [jax.ShapeDtypeStruct(b.shape, b.dtype) for b in bufs], 3 * n, 0, plan,
                  aliases={t: t for t in range(n)})


def _rider_rs_sibling(grads):
    n = len(grads)

    def plan(ins, outs, send, recv, loc, r0, l0):
        x, y, c = _coords()
        remote = []
        for t in range(n):
            for q in range(4):
                remote.append(pltpu.make_async_remote_copy(
                    src_ref=ins[t].at[q, 1 - c], dst_ref=outs[t].at[q], send_sem=send.at[r0 + 4 * t + q],
                    recv_sem=recv.at[r0 + 4 * t + q], device_id=(x, y, 1 - c), device_id_type=MESH))
        return remote, []

    return _Rider(grads, [jax.ShapeDtypeStruct((4,) + g.shape[2:], g.dtype) for g in grads], 4 * n, 0, plan)


def _rider_rs_chips(sums):
    n = len(sums)

    def plan(ins, outs, send, recv, loc, r0, l0):
        x, y, c = _coords()
        remote = []
        for t in range(n):
            for r, (px, py) in enumerate([(1 - x, y), (x, 1 - y), (1 - x, 1 - y)]):
                remote.append(pltpu.make_async_remote_copy(
                    src_ref=ins[t].at[2 * px + py], dst_ref=outs[t].at[r], send_sem=send.at[r0 + 3 * t + r],
                    recv_sem=recv.at[r0 + 3 * t + r], device_id=(px, py, c), device_id_type=MESH))
        return remote, []

    return _Rider(sums, [jax.ShapeDtypeStruct((3,) + s.shape[1:], s.dtype) for s in sums], 3 * n, 0, plan)


def _rider_gather_direct(parts):
    n = len(parts)

    def plan(ins, outs, send, recv, loc, r0, l0):
        x, y, c = _coords()
        me = _slot((x, y, c))
        remote, local = [], []
        for t in range(n):
            local.append(pltpu.make_async_copy(ins[t], outs[t].at[me], loc.at[l0 + t]))
            for k in range(1, N_DEV):
                peer = (x ^ ((k >> 2) & 1), y ^ ((k >> 1) & 1), c ^ (k & 1))
                remote.append(pltpu.make_async_remote_copy(
                    src_ref=ins[t], dst_ref=outs[t].at[me], send_sem=send.at[r0 + 7 * t + k - 1],
                    recv_sem=recv.at[r0 + 7 * t + k - 1], device_id=peer, device_id_type=MESH))
        return remote, local

    return _Rider(parts, [jax.ShapeDtypeStruct((N_DEV,) + p.shape, p.dtype) for p in parts], 7 * n, n, plan)


def _chip_sum_call(cidx, grads, recvd, out_dtypes, name):
    n = len(grads)

    def body(c_ref, *refs):
        for t in range(n):
            refs[2 * n + t][0] = (refs[t][0, 0] + refs[n + t][0]).astype(out_dtypes[t])

    in_specs = [pl.BlockSpec((1, 1) + g.shape[2:], lambda q, c_ref: (q, c_ref[0], 0, 0)) for g in grads]
    in_specs += [pl.BlockSpec((1,) + r.shape[1:], lambda q, c_ref: (q, 0, 0)) for r in recvd]
    return pl.pallas_call(
        body, name=name,
        grid_spec=pltpu.PrefetchScalarGridSpec(
            num_scalar_prefetch=1, grid=(4,), in_specs=in_specs,
            out_specs=[pl.BlockSpec((1,) + r.shape[1:], lambda q, c_ref: (q, 0, 0)) for r in recvd]),
        out_shape=[jax.ShapeDtypeStruct(r.shape, dt) for r, dt in zip(recvd, out_dtypes)],
        compiler_params=_params(("arbitrary",)),
    )(cidx, *grads, *recvd)


def _final_sum_call(idx, grads, recvd1, recvd2):
    n = len(grads)
    nsteps = 2

    def body(i_ref, *refs):
        for t in range(n):
            g, r1, r2, o = refs[t], refs[n + t], refs[2 * n + t], refs[3 * n + t]
            s = g[0, 0] + r1[0]
            for r in range(3):
                s = s + r2[r].astype(F32)
            o[...] = s

    def rows(a):
        r = a.shape[-2]
        return r // nsteps if (r // nsteps) % 16 == 0 else r

    def step(a):
        return (lambda i: i) if rows(a) != a.shape[-2] else (lambda i: 0)

    in_specs = [pl.BlockSpec((1, 1, rows(g), g.shape[3]), lambda i, s, st=step(g): (s[0], s[1], st(i), 0)) for g in grads]
    in_specs += [pl.BlockSpec((1, rows(r), r.shape[2]), lambda i, s, st=step(r): (s[0], st(i), 0)) for r in recvd1]
    in_specs += [pl.BlockSpec((3, rows(r), r.shape[2]), lambda i, s, st=step(r): (0, st(i), 0)) for r in recvd2]
    return pl.pallas_call(
        body, name="rs_final_sum",
        grid_spec=pltpu.PrefetchScalarGridSpec(
            num_scalar_prefetch=1, grid=(nsteps,), in_specs=in_specs,
            out_specs=[pl.BlockSpec((rows(r), r.shape[2]), lambda i, s, st=step(r): (st(i), 0)) for r in recvd2]),
        out_shape=[jax.ShapeDtypeStruct(r.shape[1:], F32) for r in recvd2],
        compiler_params=_params(("arbitrary",)),
    )(idx, *grads, *recvd1, *recvd2)


def _sum8_call(parts):
    def body(p_ref, o_ref):
        s = p_ref[0]
        for j in range(1, N_DEV):
            s = s + p_ref[j]
        o_ref[...] = s

    return pl.pallas_call(body, name="sum_small_partials",
                          out_shape=jax.ShapeDtypeStruct(parts.shape[1:], parts.dtype))(parts)


def _adamw(w, g, m, v):
    m = ADAM_B1 * m + (1.0 - ADAM_B1) * g
    v = ADAM_B2 * v + (1.0 - ADAM_B2) * (g * g)
    m_hat = m / (1.0 - ADAM_B1 ** ADAM_STEP)
    v_hat = v / (1.0 - ADAM_B2 ** ADAM_STEP)
    delta = -ADAM_LR * (m_hat / (jnp.sqrt(v_hat) + ADAM_EPS) + ADAM_WD * w)
    return delta, m, v


def _adamw_call(ws, gs, ms, vs, nsteps, name):
    n = len(ws)

    def body(*refs):
        for t in range(n):
            w, g, m, v = (refs[k * n + t][...] for k in range(4))
            d, m2, v2 = _adamw(w, g, m, v)
            refs[4 * n + t][...] = d
            refs[5 * n + t][...] = m2
            refs[6 * n + t][...] = v2

    def spec(a):
        assert a.shape[0] % nsteps == 0 and (nsteps == 1 or (a.shape[0] // nsteps) % 8 == 0), a.shape
        return pl.BlockSpec((a.shape[0] // nsteps, a.shape[1]), lambda i: (i, 0))

    specs = [spec(a) for a in ws]
    outs = pl.pallas_call(
        body, name=name, grid=(nsteps,),
        in_specs=specs * 4, out_specs=specs * 3,
        out_shape=[jax.ShapeDtypeStruct(a.shape, F32) for a in ws] * 3,
        compiler_params=_params(("arbitrary",)),
    )(*ws, *gs, *ms, *vs)
    return outs[:n], outs[n:2 * n], outs[2 * n:]


def _rows128(a, pad_rows):
    flat = a.reshape(-1).astype(F32)
    flat = jnp.pad(flat, (0, pad_rows * LANES - flat.shape[0]))
    return flat.reshape(pad_rows, LANES)


_SMALL_A = (("w_pool", 512), ("pool_scale", 8), ("attn_sinks", 8), ("g_mix_post", 8), ("g_mlp_pre", 8),
            ("g_mlp_post", 8), ("loss", 8), ("b_in_gates", 16))
_SMALL_A_ROWS = 640
_SMALL_B = (("g_mix_pre", 8), ("b_in_head", 16))


def _pack(parts, layout, total_rows):
    rows = [_rows128(parts[k], r) for k, r in layout]
    pad = total_rows - sum(r for _, r in layout)
    if pad:
        rows.append(jnp.zeros((pad, LANES), F32))
    return jnp.concatenate(rows, axis=0)


def _unpack(buf, layout, sizes):
    out, off = {}, 0
    for k, r in layout:
        out[k] = buf[off:off + r].reshape(-1)[:sizes[k]]
        off += r
    return out


def kernel(x, g_mix_pre, w_in, b_in, w_pool, pool_scale, attn_sinks, w_branch_pool, w_branch_attn, w_out, g_mix_post, g_mlp_pre, w_up, w_down, g_mlp_post, loss_target, m_g_mix_pre, m_w_in, m_b_in, m_w_pool, m_pool_scale, m_attn_sinks, m_w_branch_pool, m_w_branch_attn, m_w_out, m_g_mix_post, m_g_mlp_pre, m_w_up, m_w_down, m_g_mlp_post, v_g_mix_pre, v_w_in, v_b_in, v_w_pool, v_pool_scale, v_attn_sinks, v_w_branch_pool, v_w_branch_attn, v_w_out, v_g_mix_post, v_g_mlp_pre, v_w_up, v_w_down, v_g_mlp_post):
    B, S, _ = x.shape
    T = B * S
    xt = x.reshape(T, D_MODEL)
    tgt = loss_target.reshape(T, D_MODEL)
    cx, cy, cc = _coords()

    cidx = jnp.reshape(cc, (1,)).astype(jnp.int32)
    by_chip = lambda gr: gr.reshape((4, 2) + gr.shape[1:])
    bf = lambda w: w[0].astype(MXU_DTYPE)

    (win_s,) = _allgather_call([w_in[0].T.astype(MXU_DTYPE)])
    win_t = win_s.reshape(IN_WIDTH, D_MODEL)
    wpool_b = bf(w_pool)
    rc, rsa, rsb = _rot_tables(S)

    (h, u, q, k4, v4, g), ag1 = _inproj_call(
        xt, g_mix_pre, win_t, b_in, rc, rsa, rsb, S,
        rider=_rider_ag_stage1([bf(w_branch_pool), bf(w_branch_attn), bf(w_out)]))
    yp = _pool_call(u, wpool_b, pool_scale, S)
    (ya,), (wbp_s, wba_s, wout_s, wup_1) = _attn_call(
        attn_sinks, q, k4, v4, S, rider=_merge_riders(_rider_ag_forward(ag1), _rider_ag_stage1([bf(w_up)])))
    wout_f = wout_s.reshape(D_MODEL, D_MODEL)
    (mix, x1, h2), (wup_s, wdown_1) = _mix_fwd_call(
        yp, ya, g, xt, wbp_s, wba_s, wout_f, g_mix_post, g_mlp_pre,
        rider=_merge_riders(_rider_ag_forward([wup_1]), _rider_ag_stage1([bf(w_down)])))
    (wdown_s,) = _comm_call(_rider_ag_forward([wdown_1]), "allgather_finish")

    act, da, dff, dx1, dg3, dg4, lossvec = _mlp_call(x1, h2, tgt, wup_s, wdown_s, g_mlp_pre, g_mlp_post)
    gw_down = by_chip(_wgrad_rows_call(act, dff, "wgrad_down"))
    (gw_up,), (r1_down,) = _wgrad_cols_call(h2, da, "wgrad_up", rider=_rider_rs_sibling([gw_down]))
    gw_up = by_chip(gw_up)
    (s_down,) = _chip_sum_call(cidx, [gw_down], [r1_down], [MXU_DTYPE], "rs_chip_sum_down")
    (dmix, merged, dbp, dba, dyp, do, dgates, dg2, dbg), (r2_down, r1_up) = _mix_bwd_call(
        dx1, mix, yp, ya, g, wbp_s, wba_s, wout_f, g_mix_post,
        rider=_merge_riders(_rider_rs_chips([s_down]), _rider_rs_sibling([gw_up])))
    (s_up,) = _chip_sum_call(cidx, [gw_up], [r1_up], [MXU_DTYPE], "rs_chip_sum_up")
    gw_out = by_chip(_wgrad_rows_call(merged, dmix, "wgrad_out"))
    gw_bp = by_chip(_wgrad_cols_call(yp, dbp, "wgrad_bp")[0])
    gw_ba = by_chip(_wgrad_cols_call(ya, dba, "wgrad_ba")[0])
    (dq, dk, dv, dsink), (r2_up, r1_out, r1_bp, r1_ba) = _attn_bwd_call(
        attn_sinks, q, k4, v4, do, rc, rsa, rsb, S,
        rider=_merge_riders(_rider_rs_chips([s_up]), _rider_rs_sibling([gw_out, gw_bp, gw_ba])))
    s_obb = _chip_sum_call(cidx, [gw_out, gw_bp, gw_ba], [r1_out, r1_bp, r1_ba], [MXU_DTYPE] * 3, "rs_chip_sum_branch")
    du, dwp, dps = _pool_bwd_call(u, dyp, wpool_b, pool_scale, S)
    (gw_in,), (r2_out, r2_bp, r2_ba) = _wgrad_in_call(du, dq, dk, dv, dgates, h, rider=_rider_rs_chips(s_obb))
    gw_in = by_chip(gw_in)

    small_a = {"w_pool": dwp, "pool_scale": dps,
               "attn_sinks": jnp.sum(dsink.reshape(B, 8, LANES)[:, 0, :N_Q_HEADS], axis=0), "g_mix_post": dg2,
               "g_mlp_pre": dg3, "g_mlp_post": dg4, "loss": lossvec, "b_in_gates": dbg}
    gw_sa = by_chip(_pack(small_a, _SMALL_A, _SMALL_A_ROWS).reshape(N_DEV, _SMALL_A_ROWS // N_DEV, LANES))
    r1_in, r1_sa = _comm_call(_rider_rs_sibling([gw_in, gw_sa]), "rs_sibling_in")
    s_in, s_sa = _chip_sum_call(cidx, [gw_in, gw_sa], [r1_in, r1_sa], [MXU_DTYPE, F32], "rs_chip_sum_in")
    (gx, dg1, dba_in), (r2_in, r2_sa) = _inproj_bwd_call(
        du, dq, dk, dv, dgates, dx1, xt, win_t, g_mix_pre, rider=_rider_rs_chips([s_in, s_sa]))

    idx = jnp.stack([2 * cx + cy, cc]).astype(jnp.int32)
    g_in_t, g_down, g_out, g_up, g_bp, g_ba, g_sa = _final_sum_call(
        idx, [gw_in, gw_down, gw_out, gw_up, gw_bp, gw_ba, gw_sa], [r1_in, r1_down, r1_out, r1_up, r1_bp, r1_ba, r1_sa],
        [r2_in, r2_down, r2_out, r2_up, r2_bp, r2_ba, r2_sa])
    part_b = _pack({"g_mix_pre": dg1, "b_in_head": dba_in}, _SMALL_B, sum(r for _, r in _SMALL_B))
    sa_all, sb_all = _comm_call(_rider_gather_direct([g_sa, part_b]), "allgather_small")
    sb_sum = _sum8_call(sb_all)

    big_w = [w_in[0], w_branch_pool[0], w_branch_attn[0], w_out[0], w_up[0], w_down[0]]
    big_g = [g_in_t.T, g_bp, g_ba, g_out, g_up, g_down]
    big_m = [m_w_in[0], m_w_branch_pool[0], m_w_branch_attn[0], m_w_out[0], m_w_up[0], m_w_down[0]]
    big_v = [v_w_in[0], v_w_branch_pool[0], v_w_branch_attn[0], v_w_out[0], v_w_up[0], v_w_down[0]]
    big_d, big_m2, big_v2 = _adamw_call(big_w, big_g, big_m, big_v, N_DEV, "adamw_shards")

    names = ["g_mix_pre", "b_in", "w_pool", "pool_scale", "attn_sinks", "g_mix_post", "g_mlp_pre", "g_mlp_post"]
    sm_w = dict(g_mix_pre=g_mix_pre, b_in=b_in, w_pool=w_pool, pool_scale=pool_scale, attn_sinks=attn_sinks,
                g_mix_post=g_mix_post, g_mlp_pre=g_mlp_pre, g_mlp_post=g_mlp_post)
    sm_m = dict(g_mix_pre=m_g_mix_pre, b_in=m_b_in, w_pool=m_w_pool, pool_scale=m_pool_scale, attn_sinks=m_attn_sinks,
                g_mix_post=m_g_mix_post, g_mlp_pre=m_g_mlp_pre, g_mlp_post=m_g_mlp_post)
    sm_v = dict(g_mix_pre=v_g_mix_pre, b_in=v_b_in, w_pool=v_w_pool, pool_scale=v_pool_scale, attn_sinks=v_attn_sinks,
                g_mix_post=v_g_mix_post, g_mlp_pre=v_g_mlp_pre, g_mlp_post=v_g_mlp_post)
    sizes = {k: sm_w[k].size for k in names}
    sizes.update(loss=D_MODEL, b_in_gates=GATE_WIDTH, b_in_head=C_G)
    sm_g = _unpack(sa_all.reshape(_SMALL_A_ROWS, LANES), _SMALL_A, sizes)
    sm_g.update(_unpack(sb_sum, _SMALL_B, sizes))
    sm_g["b_in"] = jnp.concatenate([sm_g["b_in_head"], sm_g["b_in_gates"]])
    loss = (0.5 / D_MODEL) * jnp.sum(sm_g["loss"])
    two_d = lambda a: a.reshape(-1, a.shape[-1])
    sd_, sm2_, sv2_ = _adamw_call([two_d(sm_w[k]) for k in names], [two_d(sm_g[k].reshape(sm_w[k].shape)) for k in names],
                                  [two_d(sm_m[k]) for k in names], [two_d(sm_v[k]) for k in names], 1, "adamw_small")
    like = lambda vals: {k: a.reshape(sm_w[k].shape) for k, a in zip(names, vals)}
    sm_d, sm_m2, sm_v2 = like(sd_), like(sm2_), like(sv2_)
    sm_gr = {k: sm_g[k].reshape(sm_w[k].shape) for k in names}

    order = ["g_mix_pre", "w_in", "b_in", "w_pool", "pool_scale", "attn_sinks", "w_branch_pool", "w_branch_attn",
             "w_out", "g_mix_post", "g_mlp_pre", "w_up", "w_down", "g_mlp_post"]
    big_names = ["w_in", "w_branch_pool", "w_branch_attn", "w_out", "w_up", "w_down"]
    lead = lambda a: a[None]
    tables = []
    for small_t, big_t in ((sm_gr, big_g), (sm_d, big_d), (sm_m2, big_m2), (sm_v2, big_v2)):
        bt = dict(zip(big_names, big_t))
        tables.append([lead(bt[k]) if k in bt else small_t[k] for k in order])
    return (loss, gx.reshape(B, S, D_MODEL), *tables[0], *tables[1], *tables[2], *tables[3])
```

```python
import functools

import jax
import jax.numpy as jnp
from jax import lax
from jax.experimental import pallas as pl
from jax.experimental.pallas import tpu as pltpu

F32 = jnp.float32
MXU_DTYPE = jnp.bfloat16
MESH = pl.DeviceIdType.MESH

D_MODEL = 1024
POOL_WINDOWS = (2, 4, 8, 16)
POOL_WIDTH = 512
POOL_GC = 128
HEAD_DIM = 64
N_Q_HEADS = 8
N_KV_HEADS = 2
GROUP = 4
ATTN_WIDTH = 512
KV_WIDTH = 128
BLOCK = 128
GATE_WIDTH = 2048
IN_WIDTH = 3328
D_FF = 4096
EPS = 1e-6
NEG_INF = -1e30
ROPE_THETA = 500000.0
ROT_DIM = 16
SCALE = HEAD_DIM ** -0.5
C_Q, C_K, C_V, C_G = 512, 1024, 1152, 1280

ADAM_LR = 0.001
ADAM_B1 = 0.9
ADAM_B2 = 0.999
ADAM_EPS = 1e-08
ADAM_WD = 0.01
ADAM_STEP = 10

N_DEV = 8
LANES = 128
VMEM_LIMIT = 56 * 1024 * 1024

NN = (((1,), (0,)), ((), ()))
NT = (((1,), (1,)), ((), ()))
TN = (((0,), (0,)), ((), ()))


def _dot(a, b, dims):
    return lax.dot_general(a, b, dims, preferred_element_type=F32)


def _params(sem=None):
    return pltpu.CompilerParams(dimension_semantics=sem, vmem_limit_bytes=VMEM_LIMIT)


def _tile(n, pref):
    t = min(n, pref)
    assert n % t == 0, (n, t)
    return t


class _Rider:
    def __init__(self, ins, out_shape, n_remote, n_local, plan, aliases=None):
        self.ins, self.out_shape, self.n_remote, self.n_local = list(ins), list(out_shape), n_remote, n_local
        self.plan, self.aliases = plan, dict(aliases or {})


def _merge_riders(a, b):
    na_in, na_out = len(a.ins), len(a.out_shape)

    def plan(ins, outs, send, recv, loc, r0, l0):
        ra, la = a.plan(ins[:na_in], outs[:na_out], send, recv, loc, r0, l0)
        rb, lb = b.plan(ins[na_in:], outs[na_out:], send, recv, loc, r0 + a.n_remote, l0 + a.n_local)
        return ra + rb, la + lb

    aliases = dict(a.aliases)
    aliases.update({na_in + i: na_out + o for i, o in b.aliases.items()})
    return _Rider(a.ins + b.ins, a.out_shape + b.out_shape, a.n_remote + b.n_remote, a.n_local + b.n_local, plan, aliases)


def _launch(body, args, *, name, grid, in_specs, out_specs, out_shape, scratch_shapes=(), sem=None, rider=None):
    if rider is None:
        return pl.pallas_call(body, name=name, grid=grid, in_specs=in_specs, out_specs=out_specs, out_shape=out_shape,
                              scratch_shapes=list(scratch_shapes), compiler_params=_params(sem))(*args)
    n_in, n_out, n_scr = len(args), len(out_shape), len(scratch_shapes)
    r_in, r_out = len(rider.ins), len(rider.out_shape)

    def wrapped(*refs):
        ins, rins = refs[:n_in], refs[n_in:n_in + r_in]
        o0 = n_in + r_in
        outs, routs = refs[o0:o0 + n_out], refs[o0 + n_out:o0 + n_out + r_out]
        s0 = o0 + n_out + r_out
        scr = refs[s0:s0 + n_scr]
        send, recv, loc = refs[s0 + n_scr:]
        first, last = None, None
        for d in range(len(grid)):
            f, l = pl.program_id(d) == 0, pl.program_id(d) == pl.num_programs(d) - 1
            first = f if first is None else first & f
            last = l if last is None else last & l

        def start():
            remote, local = rider.plan(rins, routs, send, recv, loc, 0, 0)
            for cp in local + remote:
                cp.start()

        def finish():
            remote, local = rider.plan(rins, routs, send, recv, loc, 0, 0)
            for cp in remote + local:
                cp.wait()

        if first is None:
            start()
            body(*ins, *outs, *scr)
            finish()
        else:
            pl.when(first)(start)
            body(*ins, *outs, *scr)
            pl.when(last)(finish)

    hbm = pl.BlockSpec(memory_space=pl.ANY)
    dma = pltpu.SemaphoreType.DMA
    res = pl.pallas_call(
        wrapped, name=name, grid=grid, in_specs=list(in_specs) + [hbm] * r_in,
        out_specs=list(out_specs) + [hbm] * r_out, out_shape=list(out_shape) + rider.out_shape,
        scratch_shapes=list(scratch_shapes) + [dma((rider.n_remote,)), dma((rider.n_remote,)), dma((max(rider.n_local, 1),))],
        input_output_aliases={n_in + i: n_out + o for i, o in rider.aliases.items()},
        compiler_params=_params(sem),
    )(*args, *rider.ins)
    return list(res[:n_out]), list(res[n_out:])


def _comm_call(rider, name):
    return _launch(lambda: None, [], name=name, grid=(), in_specs=[], out_specs=[], out_shape=[], rider=rider)[1]


def _rms_r(x):
    return lax.rsqrt(jnp.mean(x * x, axis=-1, keepdims=True) + EPS)


def _rms_bwd(dn, x, r, g):
    xh = x * r
    dxh = dn * g
    dx = r * (dxh - xh * jnp.mean(dxh * xh, axis=-1, keepdims=True))
    return dx, dn * xh


def _rot(t, c, sa, sb):
    outs = []
    for j in range(t.shape[1] // LANES):
        tj = t[:, LANES * j:LANES * (j + 1)]
        outs.append(tj * c + pltpu.roll(tj, LANES - 8, 1) * sa + pltpu.roll(tj, 8, 1) * sb)
    return outs[0] if len(outs) == 1 else jnp.concatenate(outs, axis=1)


def _rot_tables(S):
    pos = jnp.arange(S, dtype=F32)
    inv_freq = ROPE_THETA ** (-jnp.arange(0, ROT_DIM, 2, dtype=F32) / ROT_DIM)
    ang = pos[:, None] * inv_freq[None, :]
    cos, sin = jnp.cos(ang), jnp.sin(ang)
    one = jnp.ones((S, HEAD_DIM - ROT_DIM), F32)
    zero = jnp.zeros((S, HEAD_DIM - ROT_DIM), F32)
    z8 = jnp.zeros((S, 8), F32)
    c = jnp.concatenate([cos, cos, one], axis=1)
    sa = jnp.concatenate([-sin, z8, zero], axis=1)
    sb = jnp.concatenate([z8, sin, zero], axis=1)
    rep = LANES // HEAD_DIM
    return jnp.tile(c, (1, rep)), jnp.tile(sa, (1, rep)), jnp.tile(sb, (1, rep))


def _lane_tile4(k):
    lane = lax.broadcasted_iota(jnp.int32, k.shape, 1)
    rk = pltpu.roll(k, HEAD_DIM, 1)
    x0 = jnp.where(lane < HEAD_DIM, k, rk)
    x1 = jnp.where(lane < HEAD_DIM, rk, k)
    return jnp.concatenate([x0, x0, x1, x1], axis=1)


def _fold_heads(acc):
    zs = []
    for hk in range(N_KV_HEADS):
        a = acc[:, 256 * hk:256 * hk + LANES] + acc[:, 256 * hk + LANES:256 * (hk + 1)]
        zs.append(a + pltpu.roll(a, HEAD_DIM, 1))
    lane = lax.broadcasted_iota(jnp.int32, zs[0].shape, 1)
    return jnp.where(lane < HEAD_DIM, zs[0], zs[1])


def _inproj_call(x, g1, win_t, b_in, rc, rsa, rsb, S, rider=None):
    T = x.shape[0]
    tm = _tile(S, 512)
    nst = S // tm

    def body(x_ref, g1_ref, w_ref, b_ref, c_ref, sa_ref, sb_ref,
             h_ref, u_ref, q_ref, k4_ref, v4_ref, g_ref):
        xv = x_ref[...]
        hb = ((xv * _rms_r(xv)) * g1_ref[...]).astype(MXU_DTYPE)
        h_ref[...] = hb

        def proj(lo, hi):
            return _dot(hb, w_ref[lo:hi, :], NT) + b_ref[:, lo:hi]

        c, sa, sb = c_ref[...], sa_ref[...], sb_ref[...]
        u_ref[...] = proj(0, C_Q)
        q_ref[...] = (_rot(proj(C_Q, C_K), c, sa, sb) * SCALE).astype(MXU_DTYPE)
        kv = proj(C_K, C_G)
        k4_ref[...] = _lane_tile4(_rot(kv[:, :KV_WIDTH], c, sa, sb)).astype(MXU_DTYPE)
        v4_ref[...] = _lane_tile4(kv[:, KV_WIDTH:]).astype(MXU_DTYPE)
        g_ref[...] = jax.nn.sigmoid(proj(C_G, IN_WIDTH)).astype(MXU_DTYPE)

    tok = lambda w: pl.BlockSpec((tm, w), lambda i: (i, 0))
    full = lambda a: pl.BlockSpec(a.shape, lambda i: (0,) * a.ndim)
    tab = pl.BlockSpec((tm, LANES), lambda i: (i % nst, 0))
    return _launch(
        body, [x, g1, win_t, b_in, rc, rsa, rsb], name="inproj_fwd", grid=(T // tm,),
        in_specs=[tok(D_MODEL), full(g1), full(win_t), full(b_in), tab, tab, tab],
        out_specs=[tok(D_MODEL), tok(POOL_WIDTH), tok(ATTN_WIDTH), tok(512), tok(512), tok(GATE_WIDTH)],
        out_shape=[jax.ShapeDtypeStruct((T, D_MODEL), MXU_DTYPE), jax.ShapeDtypeStruct((T, POOL_WIDTH), F32),
                   jax.ShapeDtypeStruct((T, ATTN_WIDTH), MXU_DTYPE), jax.ShapeDtypeStruct((T, 512), MXU_DTYPE),
                   jax.ShapeDtypeStruct((T, 512), MXU_DTYPE), jax.ShapeDtypeStruct((T, GATE_WIDTH), MXU_DTYPE)],
        sem=("arbitrary",), rider=rider)


def _shift_rows(a, k, rows):
    n = a.shape[0]
    if k > 0:
        return jnp.where(rows >= k, pltpu.roll(a, k, 0), 0.0)
    return jnp.where(rows < n + k, pltpu.roll(a, n + k, 0), 0.0)


def _win_sum(a, w, rows, sign):
    s, k = a, 1
    while k < w:
        s = s + _shift_rows(s, sign * k, rows)
        k *= 2
    return s


def _pool_diff(ug, w, rows):
    inv = 1.0 / jnp.minimum(rows + 1, w).astype(F32)
    return _win_sum(ug, w, rows, 1) * inv - ug, inv


def _pool_call(u, w_pool, pool_scale, S):
    T = u.shape[0]

    def body(u_ref, w_ref, ps_ref, y_ref):
        rows = lax.broadcasted_iota(jnp.int32, (S, POOL_GC), 0)
        for gi, w in enumerate(POOL_WINDOWS):
            sl = slice(POOL_GC * gi, POOL_GC * (gi + 1))
            diff, _ = _pool_diff(u_ref[:, sl], w, rows)
            mixed = _dot(diff.astype(MXU_DTYPE), w_ref[gi], NN)
            y_ref[:, sl] = (mixed * ps_ref[:, sl]).astype(MXU_DTYPE)

    seq = pl.BlockSpec((S, POOL_WIDTH), lambda b: (b, 0))
    return pl.pallas_call(
        body, name="pool_fwd", grid=(T // S,),
        in_specs=[seq, pl.BlockSpec(w_pool.shape, lambda b: (0, 0, 0)), pl.BlockSpec(pool_scale.shape, lambda b: (0, 0))],
        out_specs=seq, out_shape=jax.ShapeDtypeStruct((T, POOL_WIDTH), MXU_DTYPE),
        compiler_params=_params(("arbitrary",)),
    )(u, w_pool, pool_scale)


def _pool_bwd_call(u, dyp, w_pool, pool_scale, S):
    T = u.shape[0]

    def body(u_ref, dy_ref, w_ref, ps_ref, du_ref, dw_ref, dps_ref):
        @pl.when(pl.program_id(0) == 0)
        def _():
            dw_ref[...] = jnp.zeros_like(dw_ref)
            dps_ref[...] = jnp.zeros_like(dps_ref)

        rows = lax.broadcasted_iota(jnp.int32, (S, POOL_GC), 0)
        for gi, w in enumerate(POOL_WINDOWS):
            sl = slice(POOL_GC * gi, POOL_GC * (gi + 1))
            diff, inv = _pool_diff(u_ref[:, sl], w, rows)
            diffb = diff.astype(MXU_DTYPE)
            wg = w_ref[gi]
            mixed = _dot(diffb, wg, NN)
            dy = dy_ref[:, sl]
            dps_ref[:, sl] += jnp.sum(dy * mixed, axis=0, keepdims=True)
            dmb = (dy * ps_ref[:, sl]).astype(MXU_DTYPE)
            dw_ref[gi] += _dot(diffb, dmb, TN)
            ddiff = _dot(dmb, wg, NT)
            du_ref[:, sl] = (_win_sum(ddiff * inv, w, rows, -1) - ddiff).astype(MXU_DTYPE)

    seq = pl.BlockSpec((S, POOL_WIDTH), lambda b: (b, 0))
    return pl.pallas_call(
        body, name="pool_bwd", grid=(T // S,),
        in_specs=[seq, seq, pl.BlockSpec(w_pool.shape, lambda b: (0, 0, 0)), pl.BlockSpec(pool_scale.shape, lambda b: (0, 0))],
        out_specs=[seq, pl.BlockSpec(w_pool.shape, lambda b: (0, 0, 0)), pl.BlockSpec(pool_scale.shape, lambda b: (0, 0))],
        out_shape=[jax.ShapeDtypeStruct((T, POOL_WIDTH), MXU_DTYPE), jax.ShapeDtypeStruct(w_pool.shape, F32),
                   jax.ShapeDtypeStruct(pool_scale.shape, F32)],
        compiler_params=_params(("arbitrary",)),
    )(u, dyp, w_pool, pool_scale)


def _attn_consts():
    lane_g = lax.broadcasted_iota(jnp.int32, (BLOCK, 256), 1) >> 6
    rgrp = lax.broadcasted_iota(jnp.int32, (GROUP * BLOCK, 1), 0) >> 7
    rel = lax.broadcasted_iota(jnp.int32, (BLOCK, 256), 0) - lax.broadcasted_iota(jnp.int32, (BLOCK, 256), 1)

    def bias(off):
        ok = (rel + off >= 0) & (rel + off < BLOCK)
        return jnp.concatenate([jnp.where(ok, 0.0, NEG_INF)] * GROUP, axis=0)

    return lane_g, rgrp, bias(0), bias(BLOCK)


def _sink_rows(sink_ref, hk, rgrp):
    sv = jnp.zeros(rgrp.shape, F32)
    for g in range(GROUP):
        sv = jnp.where(rgrp == g, sink_ref[0, GROUP * hk + g], sv)
    return sv


def _stack_heads(xb, lane_g):
    return jnp.concatenate([jnp.where(lane_g == g, xb, jnp.zeros_like(xb)) for g in range(GROUP)], axis=0)


def _unstack_heads(xs, lane_g):
    out = jnp.where(lane_g == 0, xs[0:BLOCK], 0.0)
    for g in range(1, GROUP):
        out = out + jnp.where(lane_g == g, xs[BLOCK * g:BLOCK * (g + 1)], 0.0)
    return out


def _attn_probs(qs, kb, bias, sv):
    s = _dot(qs, kb, NT) + bias
    m = jnp.maximum(jnp.max(s, axis=1, keepdims=True), sv)
    e = jnp.exp(s - m)
    es = jnp.exp(sv - m)
    inv_l = 1.0 / (jnp.sum(e, axis=1, keepdims=True) + es)
    return e * inv_l, es * inv_l


def _attn_blocks(nb, blk, carry):
    carry = blk(0, 0, True, carry)
    return lax.fori_loop(1, nb, lambda n, c: blk(pl.multiple_of(n * BLOCK, BLOCK),
                                                 pl.multiple_of((n - 1) * BLOCK, BLOCK), False, c), carry)


def _attn_call(sinks, q, k4, v4, S, rider=None):
    T = q.shape[0]
    nb = S // BLOCK

    def body(sink_ref, q_ref, k_ref, v_ref, o_ref):
        lane_g, rgrp, bias_first, bias_later = _attn_consts()
        svs = [_sink_rows(sink_ref, hk, rgrp) for hk in range(N_KV_HEADS)]

        def blk(q0, k0, first, carry):
            for hk in range(N_KV_HEADS):
                cs = slice(256 * hk, 256 * (hk + 1))
                qs = _stack_heads(q_ref[pl.ds(q0, BLOCK), cs], lane_g)
                p, _ = _attn_probs(qs, k_ref[pl.ds(k0, 2 * BLOCK), cs], bias_first if first else bias_later, svs[hk])
                o = _dot(p.astype(MXU_DTYPE), v_ref[pl.ds(k0, 2 * BLOCK), cs], NN)
                o_ref[pl.ds(q0, BLOCK), cs] = _unstack_heads(o, lane_g).astype(MXU_DTYPE)
            return carry

        _attn_blocks(nb, blk, 0)

    seq = pl.BlockSpec((S, ATTN_WIDTH), lambda b: (b, 0))
    return _launch(
        body, [sinks, q, k4, v4], name="attn_fwd", grid=(T // S,),
        in_specs=[pl.BlockSpec(memory_space=pltpu.SMEM), seq, seq, seq],
        out_specs=[seq], out_shape=[jax.ShapeDtypeStruct((T, ATTN_WIDTH), MXU_DTYPE)],
        sem=("arbitrary",), rider=rider)


def _attn_bwd_call(sinks, q, k4, v4, do, rc, rsa, rsb, S, rider=None):
    T = q.shape[0]
    nb = S // BLOCK

    def body(sink_ref, q_ref, k_ref, v_ref, do_ref, c_ref, sa_ref, sb_ref,
             dq_ref, dk_ref, dv_ref, ds_ref, dk_acc, dv_acc):
        lane_g, rgrp, bias_first, bias_later = _attn_consts()
        svs = [_sink_rows(sink_ref, hk, rgrp) for hk in range(N_KV_HEADS)]
        lane1 = lax.broadcasted_iota(jnp.int32, (1, LANES), 1)
        dk_acc[...] = jnp.zeros_like(dk_acc)
        dv_acc[...] = jnp.zeros_like(dv_acc)

        def blk(q0, k0, first, dsink):
            rows = pl.ds(q0, BLOCK)
            c, sa, sb = c_ref[rows, :], sa_ref[rows, :], sb_ref[rows, :]
            for hk in range(N_KV_HEADS):
                cs = slice(256 * hk, 256 * (hk + 1))
                qs = _stack_heads(q_ref[rows, cs], lane_g)
                dos = _stack_heads(do_ref[rows, cs], lane_g)
                kb = k_ref[pl.ds(k0, 2 * BLOCK), cs]
                vb = v_ref[pl.ds(k0, 2 * BLOCK), cs]
                p, ps = _attn_probs(qs, kb, bias_first if first else bias_later, svs[hk])
                dp = _dot(dos, vb, NT)
                delta = jnp.sum(p * dp, axis=1, keepdims=True)
                dsb = (p * (dp - delta)).astype(MXU_DTYPE)
                dqb = _unstack_heads(_dot(dsb, kb, NN), lane_g) * SCALE
                dq_ref[rows, cs] = _rot(dqb, c, -sa, -sb).astype(MXU_DTYPE)
                dk_acc[pl.ds(k0, 2 * BLOCK), cs] += _dot(dsb, qs, TN)
                dv_acc[pl.ds(k0, 2 * BLOCK), cs] += _dot(p.astype(MXU_DTYPE), dos, TN)
                psd = ps * delta
                for g in range(GROUP):
                    val = -jnp.sum(psd[BLOCK * g:BLOCK * (g + 1)], axis=0, keepdims=True)
                    dsink = dsink + jnp.where(lane1 == GROUP * hk + g, val, 0.0)
            return dsink

        dsink = _attn_blocks(nb, blk, jnp.zeros((1, LANES), F32))
        dk_ref[...] = _rot(_fold_heads(dk_acc[...]), c_ref[...], -sa_ref[...], -sb_ref[...]).astype(MXU_DTYPE)
        dv_ref[...] = _fold_heads(dv_acc[...]).astype(MXU_DTYPE)
        ds_ref[...] = jnp.broadcast_to(dsink, ds_ref.shape)

    seq = pl.BlockSpec((S, ATTN_WIDTH), lambda b: (b, 0))
    kvs = pl.BlockSpec((S, KV_WIDTH), lambda b: (b, 0))
    tab = pl.BlockSpec((S, LANES), lambda b: (0, 0))
    nseq = T // S
    return _launch(
        body, [sinks, q, k4, v4, do, rc, rsa, rsb], name="attn_bwd", grid=(nseq,),
        in_specs=[pl.BlockSpec(memory_space=pltpu.SMEM), seq, seq, seq, seq, tab, tab, tab],
        out_specs=[seq, kvs, kvs, pl.BlockSpec((8, LANES), lambda b: (b, 0))],
        out_shape=[jax.ShapeDtypeStruct((T, ATTN_WIDTH), MXU_DTYPE), jax.ShapeDtypeStruct((T, KV_WIDTH), MXU_DTYPE),
                   jax.ShapeDtypeStruct((T, KV_WIDTH), MXU_DTYPE), jax.ShapeDtypeStruct((8 * nseq, LANES), F32)],
        scratch_shapes=[pltpu.VMEM((S, 512), F32), pltpu.VMEM((S, 512), F32)],
        sem=("arbitrary",), rider=rider)


def _branch_weights(wbp_ref, wba_ref, wbp_s, wba_s):
    @pl.when(pl.program_id(0) == 0)
    def _():
        for j in range(N_DEV):
            wbp_s[:, LANES * j:LANES * (j + 1)] = wbp_ref[j]
            wba_s[:, LANES * j:LANES * (j + 1)] = wba_ref[j]


def _mix_fwd_call(yp, ya, g, x, wbp, wba, wout, g2, g3, rider=None):
    T = x.shape[0]
    tm = _tile(T, 512)

    def body(yp_ref, ya_ref, g_ref, x_ref, wbp_ref, wba_ref, wout_ref, g2_ref, g3_ref,
             mix_ref, x1_ref, h2_ref, wbp_s, wba_s):
        _branch_weights(wbp_ref, wba_ref, wbp_s, wba_s)
        bp = _dot(yp_ref[...], wbp_s[...], NN)
        ba = _dot(ya_ref[...], wba_s[...], NN)
        merged = g_ref[:, :D_MODEL].astype(F32) * bp + g_ref[:, D_MODEL:].astype(F32) * ba
        mix = _dot(merged.astype(MXU_DTYPE), wout_ref[...], NN)
        mix_ref[...] = mix
        x1 = x_ref[...] + (mix * _rms_r(mix)) * g2_ref[...]
        x1_ref[...] = x1
        h2_ref[...] = ((x1 * _rms_r(x1)) * g3_ref[...]).astype(MXU_DTYPE)

    tok = lambda w: pl.BlockSpec((tm, w), lambda i: (i, 0))
    full = lambda a: pl.BlockSpec(a.shape, lambda i: (0,) * a.ndim)
    return _launch(
        body, [yp, ya, g, x, wbp, wba, wout, g2, g3], name="mix_fwd", grid=(T // tm,),
        in_specs=[tok(POOL_WIDTH), tok(ATTN_WIDTH), tok(GATE_WIDTH), tok(D_MODEL), full(wbp), full(wba), full(wout),
                  full(g2), full(g3)],
        out_specs=[tok(D_MODEL), tok(D_MODEL), tok(D_MODEL)],
        out_shape=[jax.ShapeDtypeStruct((T, D_MODEL), F32), jax.ShapeDtypeStruct((T, D_MODEL), F32),
                   jax.ShapeDtypeStruct((T, D_MODEL), MXU_DTYPE)],
        scratch_shapes=[pltpu.VMEM((POOL_WIDTH, D_MODEL), MXU_DTYPE), pltpu.VMEM((ATTN_WIDTH, D_MODEL), MXU_DTYPE)],
        sem=("arbitrary",), rider=rider)


def _mix_bwd_call(dx1, mix, yp, ya, g, wbp, wba, wout, g2, rider=None):
    T = dx1.shape[0]
    tm = _tile(T, 512)

    def body(dx1_ref, mix_ref, yp_ref, ya_ref, g_ref, wbp_ref, wba_ref, wout_ref, g2_ref,
             dmix_ref, merged_ref, dbp_ref, dba_ref, dyp_ref, do_ref, dgates_ref, dg2_ref, dbg_ref, wbp_s, wba_s):
        _branch_weights(wbp_ref, wba_ref, wbp_s, wba_s)

        @pl.when(pl.program_id(0) == 0)
        def _():
            dg2_ref[...] = jnp.zeros_like(dg2_ref)
            dbg_ref[...] = jnp.zeros_like(dbg_ref)

        mix = mix_ref[...]
        dmix, dg2 = _rms_bwd(dx1_ref[...], mix, _rms_r(mix), g2_ref[...])
        dg2_ref[...] += jnp.sum(dg2, axis=0, keepdims=True)
        dmixb = dmix.astype(MXU_DTYPE)
        dmix_ref[...] = dmixb
        dmerged = _dot(dmixb, wout_ref[...], NT)
        bp = _dot(yp_ref[...], wbp_s[...], NN)
        ba = _dot(ya_ref[...], wba_s[...], NN)
        gp, ga = g_ref[:, :D_MODEL].astype(F32), g_ref[:, D_MODEL:].astype(F32)
        merged_ref[...] = (gp * bp + ga * ba).astype(MXU_DTYPE)
        dgp = dmerged * bp * (gp * (1.0 - gp))
        dga = dmerged * ba * (ga * (1.0 - ga))
        dbg_ref[:, :D_MODEL] += jnp.sum(dgp, axis=0, keepdims=True)
        dbg_ref[:, D_MODEL:] += jnp.sum(dga, axis=0, keepdims=True)
        dgates_ref[:, :D_MODEL] = dgp.astype(MXU_DTYPE)
        dgates_ref[:, D_MODEL:] = dga.astype(MXU_DTYPE)
        dbp = (dmerged * gp).astype(MXU_DTYPE)
        dba = (dmerged * ga).astype(MXU_DTYPE)
        dbp_ref[...] = dbp
        dba_ref[...] = dba
        dyp_ref[...] = _dot(dbp, wbp_s[...], NT)
        do_ref[...] = _dot(dba, wba_s[...], NT).astype(MXU_DTYPE)

    tok = lambda w: pl.BlockSpec((tm, w), lambda i: (i, 0))
    full = lambda a: pl.BlockSpec(a.shape, lambda i: (0,) * a.ndim)
    acc = lambda w: pl.BlockSpec((1, w), lambda i: (0, 0))
    sd = jax.ShapeDtypeStruct
    return _launch(
        body, [dx1, mix, yp, ya, g, wbp, wba, wout, g2], name="mix_bwd", grid=(T // tm,),
        in_specs=[tok(D_MODEL), tok(D_MODEL), tok(POOL_WIDTH), tok(ATTN_WIDTH), tok(GATE_WIDTH), full(wbp), full(wba),
                  full(wout), full(g2)],
        out_specs=[tok(D_MODEL), tok(D_MODEL), tok(D_MODEL), tok(D_MODEL), tok(POOL_WIDTH), tok(ATTN_WIDTH),
                   tok(GATE_WIDTH), acc(D_MODEL), acc(GATE_WIDTH)],
        out_shape=[sd((T, D_MODEL), MXU_DTYPE), sd((T, D_MODEL), MXU_DTYPE), sd((T, D_MODEL), MXU_DTYPE),
                   sd((T, D_MODEL), MXU_DTYPE), sd((T, POOL_WIDTH), F32), sd((T, ATTN_WIDTH), MXU_DTYPE),
                   sd((T, GATE_WIDTH), MXU_DTYPE), sd((1, D_MODEL), F32), sd((1, GATE_WIDTH), F32)],
        scratch_shapes=[pltpu.VMEM((POOL_WIDTH, D_MODEL), MXU_DTYPE), pltpu.VMEM((ATTN_WIDTH, D_MODEL), MXU_DTYPE)],
        sem=("arbitrary",), rider=rider)


def _mlp_call(x1, h2, target, wup, wdown, g3, g4):
    T = x1.shape[0]
    tm = _tile(T, 256)
    fc = D_FF // N_DEV

    def body(x1_ref, h2_ref, t_ref, wup_ref, wdown_ref, g3_ref, g4_ref,
             act_ref, da_ref, dff_ref, dx1_ref, dg3_ref, dg4_ref, loss_ref, rl_s):
        @pl.when(pl.program_id(0) == 0)
        def _():
            dg3_ref[...] = jnp.zeros_like(dg3_ref)
            dg4_ref[...] = jnp.zeros_like(dg4_ref)
            loss_ref[...] = jnp.zeros_like(loss_ref)

        h2 = h2_ref[...]
        ff = jnp.zeros((tm, D_MODEL), F32)
        for j in range(N_DEV):
            sl = slice(fc * j, fc * (j + 1))
            rl = jnp.maximum(_dot(h2, wup_ref[j], NN), 0.0)
            rl_s[:, sl] = rl
            actb = (rl * rl).astype(MXU_DTYPE)
            act_ref[:, sl] = actb
            ff = ff + _dot(actb, wdown_ref[j], NN)
        x1 = x1_ref[...]
        r4 = _rms_r(ff)
        err = x1 + (ff * r4) * g4_ref[...] - t_ref[...]
        loss_ref[...] += jnp.sum(err * err, axis=0, keepdims=True)
        dy = err * (1.0 / D_MODEL)
        dff, dg4 = _rms_bwd(dy, ff, r4, g4_ref[...])
        dg4_ref[...] += jnp.sum(dg4, axis=0, keepdims=True)
        dffb = dff.astype(MXU_DTYPE)
        dff_ref[...] = dffb
        dh2 = jnp.zeros((tm, D_MODEL), F32)
        for j in range(N_DEV):
            sl = slice(fc * j, fc * (j + 1))
            dab = (_dot(dffb, wdown_ref[j], NT) * (2.0 * rl_s[:, sl])).astype(MXU_DTYPE)
            da_ref[:, sl] = dab
            dh2 = dh2 + _dot(dab, wup_ref[j], NT)
        dx1, dg3 = _rms_bwd(dh2, x1, _rms_r(x1), g3_ref[...])
        dg3_ref[...] += jnp.sum(dg3, axis=0, keepdims=True)
        dx1_ref[...] = dy + dx1

    tok = lambda w: pl.BlockSpec((tm, w), lambda i: (i, 0))
    full = lambda a: pl.BlockSpec(a.shape, lambda i: (0,) * a.ndim, pipeline_mode=pl.Buffered(1))
    vec = pl.BlockSpec((1, D_MODEL), lambda i: (0, 0))
    sd = jax.ShapeDtypeStruct
    return pl.pallas_call(
        body, name="mlp_fwd_bwd", grid=(T // tm,),
        in_specs=[tok(D_MODEL), tok(D_MODEL), tok(D_MODEL), full(wup), full(wdown), vec, vec],
        out_specs=[tok(D_FF), tok(D_FF), tok(D_MODEL), tok(D_MODEL), vec, vec, vec],
        out_shape=[sd((T, D_FF), MXU_DTYPE), sd((T, D_FF), MXU_DTYPE), sd((T, D_MODEL), MXU_DTYPE),
                   sd((T, D_MODEL), F32), sd((1, D_MODEL), F32), sd((1, D_MODEL), F32), sd((1, D_MODEL), F32)],
        scratch_shapes=[pltpu.VMEM((tm, D_FF), F32)],
        compiler_params=_params(("arbitrary",)),
    )(x1, h2, target, wup, wdown, g3, g4)


def _inproj_bwd_call(du, dq, dk, dv, dgates, dx1, x, win_t, g1, rider=None):
    T = x.shape[0]
    tm = _tile(T, 512)

    def body(du_ref, dq_ref, dk_ref, dv_ref, dgt_ref, dx1_ref, x_ref, w_ref, g1_ref, gx_ref, dg1_ref, db_ref):
        @pl.when(pl.program_id(0) == 0)
        def _():
            dg1_ref[...] = jnp.zeros_like(dg1_ref)
            db_ref[...] = jnp.zeros_like(db_ref)

        dh = jnp.zeros((tm, D_MODEL), F32)
        for ref, lo, hi in ((du_ref, 0, C_Q), (dq_ref, C_Q, C_K), (dk_ref, C_K, C_V), (dv_ref, C_V, C_G),
                            (dgt_ref, C_G, IN_WIDTH)):
            piece = ref[...]
            dh = dh + _dot(piece, w_ref[lo:hi, :], NN)
            if hi <= C_G:
                db_ref[:, lo:hi] += jnp.sum(piece.astype(F32), axis=0, keepdims=True)
        xv = x_ref[...]
        dx, dg1 = _rms_bwd(dh, xv, _rms_r(xv), g1_ref[...])
        dg1_ref[...] += jnp.sum(dg1, axis=0, keepdims=True)
        gx_ref[...] = dx1_ref[...] + dx

    tok = lambda w: pl.BlockSpec((tm, w), lambda i: (i, 0))
    full = lambda a: pl.BlockSpec(a.shape, lambda i: (0,) * a.ndim)
    sd = jax.ShapeDtypeStruct
    return _launch(
        body, [du, dq, dk, dv, dgates, dx1, x, win_t, g1], name="inproj_bwd", grid=(T // tm,),
        in_specs=[tok(POOL_WIDTH), tok(ATTN_WIDTH), tok(KV_WIDTH), tok(KV_WIDTH), tok(GATE_WIDTH), tok(D_MODEL),
                  tok(D_MODEL), full(win_t), full(g1)],
        out_specs=[tok(D_MODEL), pl.BlockSpec((1, D_MODEL), lambda i: (0, 0)), pl.BlockSpec((1, C_G), lambda i: (0, 0))],
        out_shape=[sd((T, D_MODEL), F32), sd((1, D_MODEL), F32), sd((1, C_G), F32)],
        sem=("arbitrary",), rider=rider)


def _wgrad_rows_call(a, b, name):
    T, K = a.shape
    N = b.shape[1]
    tm = _tile(T, 512)
    kb = min(K, 1024)
    per = kb // (K // N_DEV)

    def body(a_ref, b_ref, o_ref):
        @pl.when(pl.program_id(1) == 0)
        def _():
            o_ref[...] = jnp.zeros_like(o_ref)

        d = _dot(a_ref[...], b_ref[...], TN)
        rs = kb // per
        for j in range(per):
            o_ref[j] += d[rs * j:rs * (j + 1)]

    return pl.pallas_call(
        body, name=name, grid=(K // kb, T // tm),
        in_specs=[pl.BlockSpec((tm, kb), lambda i, t: (t, i)), pl.BlockSpec((tm, N), lambda i, t: (t, 0))],
        out_specs=pl.BlockSpec((per, K // N_DEV, N), lambda i, t: (i, 0, 0)),
        out_shape=jax.ShapeDtypeStruct((N_DEV, K // N_DEV, N), F32),
        compiler_params=_params(("arbitrary", "arbitrary")),
    )(a, b)


def _wgrad_cols_call(a, b, name, rider=None):
    T, K = a.shape
    N = b.shape[1]
    tm = _tile(T, 512)
    nb = min(N, 1024)
    per = nb // (N // N_DEV)

    def body(a_ref, b_ref, o_ref):
        @pl.when(pl.program_id(1) == 0)
        def _():
            o_ref[...] = jnp.zeros_like(o_ref)

        d = _dot(a_ref[...], b_ref[...], TN)
        cs = nb // per
        for j in range(per):
            o_ref[j] += d[:, cs * j:cs * (j + 1)]

    return _launch(
        body, [a, b], name=name, grid=(N // nb, T // tm),
        in_specs=[pl.BlockSpec((tm, K), lambda i, t: (t, 0)), pl.BlockSpec((tm, nb), lambda i, t: (t, i))],
        out_specs=[pl.BlockSpec((per, K, N // N_DEV), lambda i, t: (i, 0, 0))],
        out_shape=[jax.ShapeDtypeStruct((N_DEV, K, N // N_DEV), F32)],
        sem=("arbitrary", "arbitrary"), rider=rider)


def _wgrad_in_call(du, dq, dk, dv, dgates, h, rider=None):
    T = h.shape[0]
    tm = _tile(T, 512)
    rows = IN_WIDTH // N_DEV

    def body(du_ref, dq_ref, dk_ref, dv_ref, dgt_ref, h_ref, o_ref, acc, sem):
        t = pl.program_id(0)

        @pl.when(t == 0)
        def _():
            acc[...] = jnp.zeros_like(acc)

        hv = h_ref[...]
        for ref, lo, hi in ((du_ref, 0, C_Q), (dq_ref, C_Q, C_K), (dk_ref, C_K, C_V), (dv_ref, C_V, C_G),
                            (dgt_ref, C_G, IN_WIDTH)):
            acc[lo:hi, :] += _dot(ref[...], hv, TN)

        @pl.when(t == pl.num_programs(0) - 1)
        def _():
            copies = [pltpu.make_async_copy(acc.at[pl.ds(rows * j, rows), :], o_ref.at[j], sem.at[j])
                      for j in range(N_DEV)]
            for cp in copies:
                cp.start()
            for cp in copies:
                cp.wait()

    tok = lambda w: pl.BlockSpec((tm, w), lambda t: (t, 0))
    return _launch(
        body, [du, dq, dk, dv, dgates, h], name="wgrad_in", grid=(T // tm,),
        in_specs=[tok(POOL_WIDTH), tok(ATTN_WIDTH), tok(KV_WIDTH), tok(KV_WIDTH), tok(GATE_WIDTH), tok(D_MODEL)],
        out_specs=[pl.BlockSpec(memory_space=pl.ANY)],
        out_shape=[jax.ShapeDtypeStruct((N_DEV, rows, D_MODEL), F32)],
        scratch_shapes=[pltpu.VMEM((IN_WIDTH, D_MODEL), F32), pltpu.SemaphoreType.DMA((N_DEV,))],
        sem=("arbitrary",), rider=rider)


def _coords():
    return lax.axis_index("x"), lax.axis_index("y"), lax.axis_index("c")


def _allgather_call(shards):
    n = len(shards)

    def body(*refs):
        ins, outs = refs[:n], refs[n:2 * n]
        send_sems, recv_sems, local_sems = refs[2 * n:]
        x, y, c = _coords()
        me, sibling = (x, y, c), (x, y, 1 - c)
        chips = [(1 - x, y), (x, 1 - y), (1 - x, 1 - y)]

        def slot(p):
            return 4 * p[0] + 2 * p[1] + p[2]

        def copy(t, k, block, to, src=None):
            dst = outs[t].at[slot(block)]
            return pltpu.make_async_remote_copy(
                src_ref=dst if src is None else src, dst_ref=dst, send_sem=send_sems.at[t, k],
                recv_sem=recv_sems.at[t, k], device_id=to, device_id_type=MESH)

        mine = [pltpu.make_async_copy(ins[t], outs[t].at[slot(me)], local_sems.at[t]) for t in range(n)]
        for cp in mine:
            cp.start()
        first = []
        for t in range(n):
            first.append(copy(t, 0, me, sibling, src=ins[t]))
            first += [copy(t, 1 + j, me, (*chip, c), src=ins[t]) for j, chip in enumerate(chips)]
        for cp in first:
            cp.start()
        passed = []
        for t in range(n):
            for j, chip in enumerate(chips):
                copy(t, 1 + j, (*chip, c), me).wait_recv()
                fwd = copy(t, 4 + j, (*chip, c), sibling)
                fwd.start()
                passed.append(fwd)
        for t in range(n):
            copy(t, 0, sibling, me).wait_recv()
            for j, chip in enumerate(chips):
                copy(t, 4 + j, (*chip, 1 - c), me).wait_recv()
        for cp in first + passed:
            cp.wait_send()
        for cp in mine:
            cp.wait()

    hbm = pl.BlockSpec(memory_space=pl.ANY)
    return pl.pallas_call(
        body, name="allgather_weights",
        in_specs=[hbm] * n, out_specs=[hbm] * n,
        out_shape=[jax.ShapeDtypeStruct((N_DEV,) + s.shape, s.dtype) for s in shards],
        scratch_shapes=[pltpu.SemaphoreType.DMA((n, 7)), pltpu.SemaphoreType.DMA((n, 7)), pltpu.SemaphoreType.DMA((n,))],
    )(*shards)


def _slot(p):
    return 4 * p[0] + 2 * p[1] + p[2]


def _rider_ag_stage1(shards):
    n = len(shards)

    def plan(ins, outs, send, recv, loc, r0, l0):
        x, y, c = _coords()
        peers = [(x, y, 1 - c), (1 - x, y, c), (x, 1 - y, c), (1 - x, 1 - y, c)]
        remote, local = [], []
        for t in range(n):
            dst = outs[t].at[_slot((x, y, c))]
            local.append(pltpu.make_async_copy(ins[t], dst, loc.at[l0 + t]))
            for k, peer in enumerate(peers):
                remote.append(pltpu.make_async_remote_copy(
                    src_ref=ins[t], dst_ref=dst, send_sem=send.at[r0 + 4 * t + k], recv_sem=recv.at[r0 + 4 * t + k],
                    device_id=peer, device_id_type=MESH))
        return remote, local

    return _Rider(shards, [jax.ShapeDtypeStruct((N_DEV,) + s.shape, s.dtype) for s in shards], 4 * n, n, plan)


def _rider_ag_forward(bufs):
    n = len(bufs)

    def plan(ins, outs, send, recv, loc, r0, l0):
        x, y, c = _coords()
        remote = []
        for t in range(n):
            for j, chip in enumerate([(1 - x, y), (x, 1 - y), (1 - x, 1 - y)]):
                s = _slot((*chip, c))
                remote.append(pltpu.make_async_remote_copy(
                    src_ref=ins[t].at[s], dst_ref=outs[t].at[s], send_sem=send.at[r0 + 3 * t + j],
                    recv_sem=recv.at[r0 + 3 * t + j], device_id=(x, y, 1 - c), device_id_type=MESH))
        return remote, []

    return _Rider(bufs, [jax.ShapeDtypeStruct(b.shape, b.dtype) for b in bufs], 3 * n, 0, plan,
                  aliases={t: t for t in range(n)})


def _rider_rs_sibling(grads):
    n = len(grads)

    def plan(ins, outs, send, recv, loc, r0, l0):
        x, y, c = _coords()
        remote = []
        for t in range(n):
            for q in range(4):
                remote.append(pltpu.make_async_remote_copy(
                    src_ref=ins[t].at[q, 1 - c], dst_ref=outs[t].at[q], send_sem=send.at[r0 + 4 * t + q],
                    recv_sem=recv.at[r0 + 4 * t + q], device_id=(x, y, 1 - c), device_id_type=MESH))
        return remote, []

    return _Rider(grads, [jax.ShapeDtypeStruct((4,) + g.shape[2:], g.dtype) for g in grads], 4 * n, 0, plan)


def _rider_rs_chips(sums):
    n = len(sums)

    def plan(ins, outs, send, recv, loc, r0, l0):
        x, y, c = _coords()
        remote = []
        for t in range(n):
            for r, (px, py) in enumerate([(1 - x, y), (x, 1 - y), (1 - x, 1 - y)]):
                remote.append(pltpu.make_async_remote_copy(
                    src_ref=ins[t].at[2 * px + py], dst_ref=outs[t].at[r], send_sem=send.at[r0 + 3 * t + r],
                    recv_sem=recv.at[r0 + 3 * t + r], device_id=(px, py, c), device_id_type=MESH))
        return remote, []

    return _Rider(sums, [jax.ShapeDtypeStruct((3,) + s.shape[1:], s.dtype) for s in sums], 3 * n, 0, plan)


def _rider_gather_direct(parts):
    n = len(parts)

    def plan(ins, outs, send, recv, loc, r0, l0):
        x, y, c = _coords()
        me = _slot((x, y, c))
        remote, local = [], []
        for t in range(n):
            local.append(pltpu.make_async_copy(ins[t], outs[t].at[me], loc.at[l0 + t]))
            for k in range(1, N_DEV):
                peer = (x ^ ((k >> 2) & 1), y ^ ((k >> 1) & 1), c ^ (k & 1))
                remote.append(pltpu.make_async_remote_copy(
                    src_ref=ins[t], dst_ref=outs[t].at[me], send_sem=send.at[r0 + 7 * t + k - 1],
                    recv_sem=recv.at[r0 + 7 * t + k - 1], device_id=peer, device_id_type=MESH))
        return remote, local

    return _Rider(parts, [jax.ShapeDtypeStruct((N_DEV,) + p.shape, p.dtype) for p in parts], 7 * n, n, plan)


def _chip_sum_call(cidx, grads, recvd, out_dtypes, name):
    n = len(grads)

    def body(c_ref, *refs):
        for t in range(n):
            refs[2 * n + t][0] = (refs[t][0, 0] + refs[n + t][0]).astype(out_dtypes[t])

    in_specs = [pl.BlockSpec((1, 1) + g.shape[2:], lambda q, c_ref: (q, c_ref[0], 0, 0)) for g in grads]
    in_specs += [pl.BlockSpec((1,) + r.shape[1:], lambda q, c_ref: (q, 0, 0)) for r in recvd]
    return pl.pallas_call(
        body, name=name,
        grid_spec=pltpu.PrefetchScalarGridSpec(
            num_scalar_prefetch=1, grid=(4,), in_specs=in_specs,
            out_specs=[pl.BlockSpec((1,) + r.shape[1:], lambda q, c_ref: (q, 0, 0)) for r in recvd]),
        out_shape=[jax.ShapeDtypeStruct(r.shape, dt) for r, dt in zip(recvd, out_dtypes)],
        compiler_params=_params(("arbitrary",)),
    )(cidx, *grads, *recvd)


def _final_sum_call(idx, grads, recvd1, recvd2):
    n = len(grads)
    nsteps = 2

    def body(i_ref, *refs):
        for t in range(n):
            g, r1, r2, o = refs[t], refs[n + t], refs[2 * n + t], refs[3 * n + t]
            s = g[0, 0] + r1[0]
            for r in range(3):
                s = s + r2[r].astype(F32)
            o[...] = s

    def rows(a):
        r = a.shape[-2]
        return r // nsteps if (r // nsteps) % 16 == 0 else r

    def step(a):
        return (lambda i: i) if rows(a) != a.shape[-2] else (lambda i: 0)

    in_specs = [pl.BlockSpec((1, 1, rows(g), g.shape[3]), lambda i, s, st=step(g): (s[0], s[1], st(i), 0)) for g in grads]
    in_specs += [pl.BlockSpec((1, rows(r), r.shape[2]), lambda i, s, st=step(r): (s[0], st(i), 0)) for r in recvd1]
    in_specs += [pl.BlockSpec((3, rows(r), r.shape[2]), lambda i, s, st=step(r): (0, st(i), 0)) for r in recvd2]
    return pl.pallas_call(
        body, name="rs_final_sum",
        grid_spec=pltpu.PrefetchScalarGridSpec(
            num_scalar_prefetch=1, grid=(nsteps,), in_specs=in_specs,
            out_specs=[pl.BlockSpec((rows(r), r.shape[2]), lambda i, s, st=step(r): (st(i), 0)) for r in recvd2]),
        out_shape=[jax.ShapeDtypeStruct(r.shape[1:], F32) for r in recvd2],
        compiler_params=_params(("arbitrary",)),
    )(idx, *grads, *recvd1, *recvd2)


def _sum8_call(parts):
    def body(p_ref, o_ref):
        s = p_ref[0]
        for j in range(1, N_DEV):
            s = s + p_ref[j]
        o_ref[...] = s

    return pl.pallas_call(body, name="sum_small_partials",
                          out_shape=jax.ShapeDtypeStruct(parts.shape[1:], parts.dtype))(parts)


def _adamw(w, g, m, v):
    m = ADAM_B1 * m + (1.0 - ADAM_B1) * g
    v = ADAM_B2 * v + (1.0 - ADAM_B2) * (g * g)
    m_hat = m / (1.0 - ADAM_B1 ** ADAM_STEP)
    v_hat = v / (1.0 - ADAM_B2 ** ADAM_STEP)
    delta = -ADAM_LR * (m_hat / (jnp.sqrt(v_hat) + ADAM_EPS) + ADAM_WD * w)
    return delta, m, v


def _adamw_call(ws, gs, ms, vs, nsteps, name):
    n = len(ws)

    def body(*refs):
        for t in range(n):
            w, g, m, v = (refs[k * n + t][...] for k in range(4))
            d, m2, v2 = _adamw(w, g, m, v)
            refs[4 * n + t][...] = d
            refs[5 * n + t][...] = m2
            refs[6 * n + t][...] = v2

    def spec(a):
        assert a.shape[0] % nsteps == 0 and (nsteps == 1 or (a.shape[0] // nsteps) % 8 == 0), a.shape
        return pl.BlockSpec((a.shape[0] // nsteps, a.shape[1]), lambda i: (i, 0))

    specs = [spec(a) for a in ws]
    outs = pl.pallas_call(
        body, name=name, grid=(nsteps,),
        in_specs=specs * 4, out_specs=specs * 3,
        out_shape=[jax.ShapeDtypeStruct(a.shape, F32) for a in ws] * 3,
        compiler_params=_params(("arbitrary",)),
    )(*ws, *gs, *ms, *vs)
    return outs[:n], outs[n:2 * n], outs[2 * n:]


def _adamw_rs_call(idx, gws, r1s, r2s, ws, ms, vs, nsteps, name):
    n = len(ws)

    def body(i_ref, *refs):
        for t in range(n):
            gw, r1, r2, w, m, v = (refs[k * n + t] for k in range(6))
            g = gw[0, 0] + r1[0]
            for r in range(3):
                g = g + r2[r].astype(F32)
            d, m2, v2 = _adamw(w[...], g, m[...], v[...])
            refs[6 * n + t][...] = g
            refs[7 * n + t][...] = d
            refs[8 * n + t][...] = m2
            refs[9 * n + t][...] = v2

    def rb(a):
        r = a.shape[0] // nsteps
        assert a.shape[0] % nsteps == 0 and r % 16 == 0, a.shape
        return r

    in_specs = [pl.BlockSpec((1, 1, rb(w), w.shape[1]), lambda i, s: (s[0], s[1], i, 0)) for w in ws]
    in_specs += [pl.BlockSpec((1, rb(w), w.shape[1]), lambda i, s: (s[0], i, 0)) for w in ws]
    in_specs += [pl.BlockSpec((3, rb(w), w.shape[1]), lambda i, s: (0, i, 0)) for w in ws]
    plain = [pl.BlockSpec((rb(w), w.shape[1]), lambda i, s: (i, 0)) for w in ws]
    outs = pl.pallas_call(
        body, name=name,
        grid_spec=pltpu.PrefetchScalarGridSpec(num_scalar_prefetch=1, grid=(nsteps,), in_specs=in_specs + plain * 3,
                                               out_specs=plain * 4),
        out_shape=[jax.ShapeDtypeStruct(w.shape, F32) for w in ws] * 4,
        compiler_params=_params(("arbitrary",)),
    )(idx, *gws, *r1s, *r2s, *ws, *ms, *vs)
    return outs[:n], outs[n:2 * n], outs[2 * n:3 * n], outs[3 * n:]


def _rows128(a, pad_rows):
    flat = a.reshape(-1).astype(F32)
    flat = jnp.pad(flat, (0, pad_rows * LANES - flat.shape[0]))
    return flat.reshape(pad_rows, LANES)


_SMALL_A = (("w_pool", 512), ("pool_scale", 8), ("attn_sinks", 8), ("g_mix_post", 8), ("g_mlp_pre", 8),
            ("g_mlp_post", 8), ("loss", 8), ("b_in_gates", 16))
_SMALL_A_ROWS = 640
_SMALL_B = (("g_mix_pre", 8), ("b_in_head", 16))


def _pack(parts, layout, total_rows):
    rows = [_rows128(parts[k], r) for k, r in layout]
    pad = total_rows - sum(r for _, r in layout)
    if pad:
        rows.append(jnp.zeros((pad, LANES), F32))
    return jnp.concatenate(rows, axis=0)


def _unpack(buf, layout, sizes):
    out, off = {}, 0
    for k, r in layout:
        out[k] = buf[off:off + r].reshape(-1)[:sizes[k]]
        off += r
    return out


def kernel(x, g_mix_pre, w_in, b_in, w_pool, pool_scale, attn_sinks, w_branch_pool, w_branch_attn, w_out, g_mix_post, g_mlp_pre, w_up, w_down, g_mlp_post, loss_target, m_g_mix_pre, m_w_in, m_b_in, m_w_pool, m_pool_scale, m_attn_sinks, m_w_branch_pool, m_w_branch_attn, m_w_out, m_g_mix_post, m_g_mlp_pre, m_w_up, m_w_down, m_g_mlp_post, v_g_mix_pre, v_w_in, v_b_in, v_w_pool, v_pool_scale, v_attn_sinks, v_w_branch_pool, v_w_branch_attn, v_w_out, v_g_mix_post, v_g_mlp_pre, v_w_up, v_w_down, v_g_mlp_post):
    B, S, _ = x.shape
    T = B * S
    xt = x.reshape(T, D_MODEL)
    tgt = loss_target.reshape(T, D_MODEL)
    cx, cy, cc = _coords()

    cidx = jnp.reshape(cc, (1,)).astype(jnp.int32)
    by_chip = lambda gr: gr.reshape((4, 2) + gr.shape[1:])
    bf = lambda w: w[0].astype(MXU_DTYPE)

    (win_s,) = _allgather_call([w_in[0].T.astype(MXU_DTYPE)])
    win_t = win_s.reshape(IN_WIDTH, D_MODEL)
    wpool_b = bf(w_pool)
    rc, rsa, rsb = _rot_tables(S)

    (h, u, q, k4, v4, g), ag1 = _inproj_call(
        xt, g_mix_pre, win_t, b_in, rc, rsa, rsb, S,
        rider=_rider_ag_stage1([bf(w_branch_pool), bf(w_branch_attn), bf(w_out)]))
    yp = _pool_call(u, wpool_b, pool_scale, S)
    (ya,), (wbp_s, wba_s, wout_s, wup_1) = _attn_call(
        attn_sinks, q, k4, v4, S, rider=_merge_riders(_rider_ag_forward(ag1), _rider_ag_stage1([bf(w_up)])))
    wout_f = wout_s.reshape(D_MODEL, D_MODEL)
    (mix, x1, h2), (wup_s, wdown_1) = _mix_fwd_call(
        yp, ya, g, xt, wbp_s, wba_s, wout_f, g_mix_post, g_mlp_pre,
        rider=_merge_riders(_rider_ag_forward([wup_1]), _rider_ag_stage1([bf(w_down)])))
    (wdown_s,) = _comm_call(_rider_ag_forward([wdown_1]), "allgather_finish")

    act, da, dff, dx1, dg3, dg4, lossvec = _mlp_call(x1, h2, tgt, wup_s, wdown_s, g_mlp_pre, g_mlp_post)
    gw_down = by_chip(_wgrad_rows_call(act, dff, "wgrad_down"))
    (gw_up,), (r1_down,) = _wgrad_cols_call(h2, da, "wgrad_up", rider=_rider_rs_sibling([gw_down]))
    gw_up = by_chip(gw_up)
    (s_down,) = _chip_sum_call(cidx, [gw_down], [r1_down], [MXU_DTYPE], "rs_chip_sum_down")
    (dmix, merged, dbp, dba, dyp, do, dgates, dg2, dbg), (r2_down, r1_up) = _mix_bwd_call(
        dx1, mix, yp, ya, g, wbp_s, wba_s, wout_f, g_mix_post,
        rider=_merge_riders(_rider_rs_chips([s_down]), _rider_rs_sibling([gw_up])))
    (s_up,) = _chip_sum_call(cidx, [gw_up], [r1_up], [MXU_DTYPE], "rs_chip_sum_up")
    gw_out = by_chip(_wgrad_rows_call(merged, dmix, "wgrad_out"))
    gw_bp = by_chip(_wgrad_cols_call(yp, dbp, "wgrad_bp")[0])
    gw_ba = by_chip(_wgrad_cols_call(ya, dba, "wgrad_ba")[0])
    (dq, dk, dv, dsink), (r2_up, r1_out, r1_bp, r1_ba) = _attn_bwd_call(
        attn_sinks, q, k4, v4, do, rc, rsa, rsb, S,
        rider=_merge_riders(_rider_rs_chips([s_up]), _rider_rs_sibling([gw_out, gw_bp, gw_ba])))
    s_obb = _chip_sum_call(cidx, [gw_out, gw_bp, gw_ba], [r1_out, r1_bp, r1_ba], [MXU_DTYPE] * 3, "rs_chip_sum_branch")
    du, dwp, dps = _pool_bwd_call(u, dyp, wpool_b, pool_scale, S)
    (gw_in,), (r2_out, r2_bp, r2_ba) = _wgrad_in_call(du, dq, dk, dv, dgates, h, rider=_rider_rs_chips(s_obb))
    gw_in = by_chip(gw_in)

    small_a = {"w_pool": dwp, "pool_scale": dps,
               "attn_sinks": jnp.sum(dsink.reshape(B, 8, LANES)[:, 0, :N_Q_HEADS], axis=0), "g_mix_post": dg2,
               "g_mlp_pre": dg3, "g_mlp_post": dg4, "loss": lossvec, "b_in_gates": dbg}
    gw_sa = by_chip(_pack(small_a, _SMALL_A, _SMALL_A_ROWS).reshape(N_DEV, _SMALL_A_ROWS // N_DEV, LANES))
    r1_in, r1_sa = _comm_call(_rider_rs_sibling([gw_in, gw_sa]), "rs_sibling_in")
    s_in, s_sa = _chip_sum_call(cidx, [gw_in, gw_sa], [r1_in, r1_sa], [MXU_DTYPE, F32], "rs_chip_sum_in")
    (gx, dg1, dba_in), (r2_in, r2_sa) = _inproj_bwd_call(
        du, dq, dk, dv, dgates, dx1, xt, win_t, g_mix_pre, rider=_rider_rs_chips([s_in, s_sa]))

    idx = jnp.stack([2 * cx + cy, cc]).astype(jnp.int32)
    (g_sa,) = _final_sum_call(idx, [gw_sa], [r1_sa], [r2_sa])
    part_b = _pack({"g_mix_pre": dg1, "b_in_head": dba_in}, _SMALL_B, sum(r for _, r in _SMALL_B))
    sa_all, sb_all = _comm_call(_rider_gather_direct([g_sa, part_b]), "allgather_small")
    sb_sum = _sum8_call(sb_all)

    in_t = _adamw_rs_call(idx, [gw_in], [r1_in], [r2_in], [w_in[0].T], [m_w_in[0].T], [v_w_in[0].T], 2, "adamw_w_in")
    rest = _adamw_rs_call(
        idx, [gw_bp, gw_ba, gw_out, gw_up, gw_down], [r1_bp, r1_ba, r1_out, r1_up, r1_down],
        [r2_bp, r2_ba, r2_out, r2_up, r2_down], [w_branch_pool[0], w_branch_attn[0], w_out[0], w_up[0], w_down[0]],
        [m_w_branch_pool[0], m_w_branch_attn[0], m_w_out[0], m_w_up[0], m_w_down[0]],
        [v_w_branch_pool[0], v_w_branch_attn[0], v_w_out[0], v_w_up[0], v_w_down[0]], N_DEV, "adamw_shards")
    big_g, big_d, big_m2, big_v2 = ([a[0].T] + list(b) for a, b in zip(in_t, rest))

    names = ["g_mix_pre", "b_in", "w_pool", "pool_scale", "attn_sinks", "g_mix_post", "g_mlp_pre", "g_mlp_post"]
    sm_w = dict(g_mix_pre=g_mix_pre, b_in=b_in, w_pool=w_pool, pool_scale=pool_scale, attn_sinks=attn_sinks,
                g_mix_post=g_mix_post, g_mlp_pre=g_mlp_pre, g_mlp_post=g_mlp_post)
    sm_m = dict(g_mix_pre=m_g_mix_pre, b_in=m_b_in, w_pool=m_w_pool, pool_scale=m_pool_scale, attn_sinks=m_attn_sinks,
                g_mix_post=m_g_mix_post, g_mlp_pre=m_g_mlp_pre, g_mlp_post=m_g_mlp_post)
    sm_v = dict(g_mix_pre=v_g_mix_pre, b_in=v_b_in, w_pool=v_w_pool, pool_scale=v_pool_scale, attn_sinks=v_attn_sinks,
                g_mix_post=v_g_mix_post, g_mlp_pre=v_g_mlp_pre, g_mlp_post=v_g_mlp_post)
    sizes = {k: sm_w[k].size for k in names}
    sizes.update(loss=D_MODEL, b_in_gates=GATE_WIDTH, b_in_head=C_G)
    sm_g = _unpack(sa_all.reshape(_SMALL_A_ROWS, LANES), _SMALL_A, sizes)
    sm_g.update(_unpack(sb_sum, _SMALL_B, sizes))
    sm_g["b_in"] = jnp.concatenate([sm_g["b_in_head"], sm_g["b_in_gates"]])
    loss = (0.5 / D_MODEL) * jnp.sum(sm_g["loss"])
    two_d = lambda a: a.reshape(-1, a.shape[-1])
    sd_, sm2_, sv2_ = _adamw_call([two_d(sm_w[k]) for k in names], [two_d(sm_g[k].reshape(sm_w[k].shape)) for k in names],
                                  [two_d(sm_m[k]) for k in names], [two_d(sm_v[k]) for k in names], 1, "adamw_small")
    like = lambda vals: {k: a.reshape(sm_w[k].shape) for k, a in zip(names, vals)}
    sm_d, sm_m2, sm_v2 = like(sd_), like(sm2_), like(sv2_)
    sm_gr = {k: sm_g[k].reshape(sm_w[k].shape) for k in names}

    order = ["g_mix_pre", "w_in", "b_in", "w_pool", "pool_scale", "attn_sinks", "w_branch_pool", "w_branch_attn",
             "w_out", "g_mix_post", "g_mlp_pre", "w_up", "w_down", "g_mlp_post"]
    big_names = ["w_in", "w_branch_pool", "w_branch_attn", "w_out", "w_up", "w_down"]
    lead = lambda a: a[None]
    tables = []
    for small_t, big_t in ((sm_gr, big_g), (sm_d, big_d), (sm_m2, big_m2), (sm_v2, big_v2)):
        bt = dict(zip(big_names, big_t))
        tables.append([lead(bt[k]) if k in bt else small_t[k] for k in order])
    return (loss, gx.reshape(B, S, D_MODEL), *tables[0], *tables[1], *tables[2], *tables[3])
```

```python
import functools

import jax
import jax.numpy as jnp
from jax import lax
from jax.experimental import pallas as pl
from jax.experimental.pallas import tpu as pltpu

F32 = jnp.float32
MXU_DTYPE = jnp.bfloat16
MESH = pl.DeviceIdType.MESH

D_MODEL = 1024
POOL_WINDOWS = (2, 4, 8, 16)
POOL_WIDTH = 512
POOL_GC = 128
HEAD_DIM = 64
N_Q_HEADS = 8
N_KV_HEADS = 2
GROUP = 4
ATTN_WIDTH = 512
KV_WIDTH = 128
BLOCK = 128
GATE_WIDTH = 2048
IN_WIDTH = 3328
D_FF = 4096
EPS = 1e-6
NEG_INF = -1e30
ROPE_THETA = 500000.0
ROT_DIM = 16
SCALE = HEAD_DIM ** -0.5
C_Q, C_K, C_V, C_G = 512, 1024, 1152, 1280

ADAM_LR = 0.001
ADAM_B1 = 0.9
ADAM_B2 = 0.999
ADAM_EPS = 1e-08
ADAM_WD = 0.01
ADAM_STEP = 10

N_DEV = 8
LANES = 128
VMEM_LIMIT = 56 * 1024 * 1024

NN = (((1,), (0,)), ((), ()))
NT = (((1,), (1,)), ((), ()))
TN = (((0,), (0,)), ((), ()))


def _dot(a, b, dims):
    return lax.dot_general(a, b, dims, preferred_element_type=F32)


def _params(sem=None):
    return pltpu.CompilerParams(dimension_semantics=sem, vmem_limit_bytes=VMEM_LIMIT)


def _tile(n, pref):
    t = min(n, pref)
    assert n % t == 0, (n, t)
    return t


class _Rider:
    def __init__(self, ins, out_shape, n_remote, n_local, plan, aliases=None):
        self.ins, self.out_shape, self.n_remote, self.n_local = list(ins), list(out_shape), n_remote, n_local
        self.plan, self.aliases = plan, dict(aliases or {})


def _merge_riders(a, b):
    na_in, na_out = len(a.ins), len(a.out_shape)

    def plan(ins, outs, send, recv, loc, r0, l0):
        ra, la = a.plan(ins[:na_in], outs[:na_out], send, recv, loc, r0, l0)
        rb, lb = b.plan(ins[na_in:], outs[na_out:], send, recv, loc, r0 + a.n_remote, l0 + a.n_local)
        return ra + rb, la + lb

    aliases = dict(a.aliases)
    aliases.update({na_in + i: na_out + o for i, o in b.aliases.items()})
    return _Rider(a.ins + b.ins, a.out_shape + b.out_shape, a.n_remote + b.n_remote, a.n_local + b.n_local, plan, aliases)


def _launch(body, args, *, name, grid, in_specs, out_specs, out_shape, scratch_shapes=(), sem=None, rider=None):
    if rider is None:
        return pl.pallas_call(body, name=name, grid=grid, in_specs=in_specs, out_specs=out_specs, out_shape=out_shape,
                              scratch_shapes=list(scratch_shapes), compiler_params=_params(sem))(*args)
    n_in, n_out, n_scr = len(args), len(out_shape), len(scratch_shapes)
    r_in, r_out = len(rider.ins), len(rider.out_shape)

    def wrapped(*refs):
        ins, rins = refs[:n_in], refs[n_in:n_in + r_in]
        o0 = n_in + r_in
        outs, routs = refs[o0:o0 + n_out], refs[o0 + n_out:o0 + n_out + r_out]
        s0 = o0 + n_out + r_out
        scr = refs[s0:s0 + n_scr]
        send, recv, loc = refs[s0 + n_scr:]
        first, last = None, None
        for d in range(len(grid)):
            f, l = pl.program_id(d) == 0, pl.program_id(d) == pl.num_programs(d) - 1
            first = f if first is None else first & f
            last = l if last is None else last & l

        def start():
            remote, local = rider.plan(rins, routs, send, recv, loc, 0, 0)
            for cp in local + remote:
                cp.start()

        def finish():
            remote, local = rider.plan(rins, routs, send, recv, loc, 0, 0)
            for cp in remote + local:
                cp.wait()

        if first is None:
            start()
            body(*ins, *outs, *scr)
            finish()
        else:
            pl.when(first)(start)
            body(*ins, *outs, *scr)
            pl.when(last)(finish)

    hbm = pl.BlockSpec(memory_space=pl.ANY)
    dma = pltpu.SemaphoreType.DMA
    res = pl.pallas_call(
        wrapped, name=name, grid=grid, in_specs=list(in_specs) + [hbm] * r_in,
        out_specs=list(out_specs) + [hbm] * r_out, out_shape=list(out_shape) + rider.out_shape,
        scratch_shapes=list(scratch_shapes) + [dma((rider.n_remote,)), dma((rider.n_remote,)), dma((max(rider.n_local, 1),))],
        input_output_aliases={n_in + i: n_out + o for i, o in rider.aliases.items()},
        compiler_params=_params(sem),
    )(*args, *rider.ins)
    return list(res[:n_out]), list(res[n_out:])


def _comm_call(rider, name):
    return _launch(lambda: None, [], name=name, grid=(), in_specs=[], out_specs=[], out_shape=[], rider=rider)[1]


def _rms_r(x):
    return lax.rsqrt(jnp.mean(x * x, axis=-1, keepdims=True) + EPS)


def _rms_bwd(dn, x, r, g):
    xh = x * r
    dxh = dn * g
    dx = r * (dxh - xh * jnp.mean(dxh * xh, axis=-1, keepdims=True))
    return dx, dn * xh


def _rot(t, c, sa, sb):
    outs = []
    for j in range(t.shape[1] // LANES):
        tj = t[:, LANES * j:LANES * (j + 1)]
        outs.append(tj * c + pltpu.roll(tj, LANES - 8, 1) * sa + pltpu.roll(tj, 8, 1) * sb)
    return outs[0] if len(outs) == 1 else jnp.concatenate(outs, axis=1)


def _rot_tables(S):
    pos = jnp.arange(S, dtype=F32)
    inv_freq = ROPE_THETA ** (-jnp.arange(0, ROT_DIM, 2, dtype=F32) / ROT_DIM)
    ang = pos[:, None] * inv_freq[None, :]
    cos, sin = jnp.cos(ang), jnp.sin(ang)
    one = jnp.ones((S, HEAD_DIM - ROT_DIM), F32)
    zero = jnp.zeros((S, HEAD_DIM - ROT_DIM), F32)
    z8 = jnp.zeros((S, 8), F32)
    c = jnp.concatenate([cos, cos, one], axis=1)
    sa = jnp.concatenate([-sin, z8, zero], axis=1)
    sb = jnp.concatenate([z8, sin, zero], axis=1)
    rep = LANES // HEAD_DIM
    return jnp.tile(c, (1, rep)), jnp.tile(sa, (1, rep)), jnp.tile(sb, (1, rep))


def _lane_tile4(k):
    lane = lax.broadcasted_iota(jnp.int32, k.shape, 1)
    rk = pltpu.roll(k, HEAD_DIM, 1)
    x0 = jnp.where(lane < HEAD_DIM, k, rk)
    x1 = jnp.where(lane < HEAD_DIM, rk, k)
    return jnp.concatenate([x0, x0, x1, x1], axis=1)


def _fold_heads(acc):
    zs = []
    for hk in range(N_KV_HEADS):
        a = acc[:, 256 * hk:256 * hk + LANES] + acc[:, 256 * hk + LANES:256 * (hk + 1)]
        zs.append(a + pltpu.roll(a, HEAD_DIM, 1))
    lane = lax.broadcasted_iota(jnp.int32, zs[0].shape, 1)
    return jnp.where(lane < HEAD_DIM, zs[0], zs[1])


def _inproj_call(x, g1, win_t, b_in, rc, rsa, rsb, S, rider=None):
    T = x.shape[0]
    tm = _tile(S, 512)
    nst = S // tm

    def body(x_ref, g1_ref, w_ref, b_ref, c_ref, sa_ref, sb_ref,
             h_ref, u_ref, q_ref, k4_ref, v4_ref, g_ref):
        xv = x_ref[...]
        hb = ((xv * _rms_r(xv)) * g1_ref[...]).astype(MXU_DTYPE)
        h_ref[...] = hb

        def proj(lo, hi):
            return _dot(hb, w_ref[lo:hi, :], NT) + b_ref[:, lo:hi]

        c, sa, sb = c_ref[...], sa_ref[...], sb_ref[...]
        u_ref[...] = proj(0, C_Q)
        q_ref[...] = (_rot(proj(C_Q, C_K), c, sa, sb) * SCALE).astype(MXU_DTYPE)
        kv = proj(C_K, C_G)
        k4_ref[...] = _lane_tile4(_rot(kv[:, :KV_WIDTH], c, sa, sb)).astype(MXU_DTYPE)
        v4_ref[...] = _lane_tile4(kv[:, KV_WIDTH:]).astype(MXU_DTYPE)
        g_ref[...] = jax.nn.sigmoid(proj(C_G, IN_WIDTH)).astype(MXU_DTYPE)

    tok = lambda w: pl.BlockSpec((tm, w), lambda i: (i, 0))
    full = lambda a: pl.BlockSpec(a.shape, lambda i: (0,) * a.ndim)
    tab = pl.BlockSpec((tm, LANES), lambda i: (i % nst, 0))
    return _launch(
        body, [x, g1, win_t, b_in, rc, rsa, rsb], name="inproj_fwd", grid=(T // tm,),
        in_specs=[tok(D_MODEL), full(g1), full(win_t), full(b_in), tab, tab, tab],
        out_specs=[tok(D_MODEL), tok(POOL_WIDTH), tok(ATTN_WIDTH), tok(512), tok(512), tok(GATE_WIDTH)],
        out_shape=[jax.ShapeDtypeStruct((T, D_MODEL), MXU_DTYPE), jax.ShapeDtypeStruct((T, POOL_WIDTH), F32),
                   jax.ShapeDtypeStruct((T, ATTN_WIDTH), MXU_DTYPE), jax.ShapeDtypeStruct((T, 512), MXU_DTYPE),
                   jax.ShapeDtypeStruct((T, 512), MXU_DTYPE), jax.ShapeDtypeStruct((T, GATE_WIDTH), MXU_DTYPE)],
        sem=("arbitrary",), rider=rider)


def _shift_rows(a, k, rows):
    n = a.shape[0]
    if k > 0:
        return jnp.where(rows >= k, pltpu.roll(a, k, 0), 0.0)
    return jnp.where(rows < n + k, pltpu.roll(a, n + k, 0), 0.0)


def _win_sum(a, w, rows, sign):
    s, k = a, 1
    while k < w:
        s = s + _shift_rows(s, sign * k, rows)
        k *= 2
    return s


def _pool_diff(ug, w, rows):
    inv = 1.0 / jnp.minimum(rows + 1, w).astype(F32)
    return _win_sum(ug, w, rows, 1) * inv - ug, inv


def _pool_call(u, w_pool, pool_scale, S):
    T = u.shape[0]

    def body(u_ref, w_ref, ps_ref, y_ref):
        rows = lax.broadcasted_iota(jnp.int32, (S, POOL_GC), 0)
        for gi, w in enumerate(POOL_WINDOWS):
            sl = slice(POOL_GC * gi, POOL_GC * (gi + 1))
            diff, _ = _pool_diff(u_ref[:, sl], w, rows)
            mixed = _dot(diff.astype(MXU_DTYPE), w_ref[gi], NN)
            y_ref[:, sl] = (mixed * ps_ref[:, sl]).astype(MXU_DTYPE)

    seq = pl.BlockSpec((S, POOL_WIDTH), lambda b: (b, 0))
    return pl.pallas_call(
        body, name="pool_fwd", grid=(T // S,),
        in_specs=[seq, pl.BlockSpec(w_pool.shape, lambda b: (0, 0, 0)), pl.BlockSpec(pool_scale.shape, lambda b: (0, 0))],
        out_specs=seq, out_shape=jax.ShapeDtypeStruct((T, POOL_WIDTH), MXU_DTYPE),
        compiler_params=_params(("arbitrary",)),
    )(u, w_pool, pool_scale)


def _pool_bwd_call(u, dyp, w_pool, pool_scale, S, rider=None):
    T = u.shape[0]

    def body(u_ref, dy_ref, w_ref, ps_ref, du_ref, dw_ref, dps_ref):
        @pl.when(pl.program_id(0) == 0)
        def _():
            dw_ref[...] = jnp.zeros_like(dw_ref)
            dps_ref[...] = jnp.zeros_like(dps_ref)

        rows = lax.broadcasted_iota(jnp.int32, (S, POOL_GC), 0)
        for gi, w in enumerate(POOL_WINDOWS):
            sl = slice(POOL_GC * gi, POOL_GC * (gi + 1))
            diff, inv = _pool_diff(u_ref[:, sl], w, rows)
            diffb = diff.astype(MXU_DTYPE)
            wg = w_ref[gi]
            mixed = _dot(diffb, wg, NN)
            dy = dy_ref[:, sl]
            dps_ref[:, sl] += jnp.sum(dy * mixed, axis=0, keepdims=True)
            dmb = (dy * ps_ref[:, sl]).astype(MXU_DTYPE)
            dw_ref[gi] += _dot(diffb, dmb, TN)
            ddiff = _dot(dmb, wg, NT)
            du_ref[:, sl] = (_win_sum(ddiff * inv, w, rows, -1) - ddiff).astype(MXU_DTYPE)

    seq = pl.BlockSpec((S, POOL_WIDTH), lambda b: (b, 0))
    return _launch(
        body, [u, dyp, w_pool, pool_scale], name="pool_bwd", grid=(T // S,),
        in_specs=[seq, seq, pl.BlockSpec(w_pool.shape, lambda b: (0, 0, 0)), pl.BlockSpec(pool_scale.shape, lambda b: (0, 0))],
        out_specs=[seq, pl.BlockSpec(w_pool.shape, lambda b: (0, 0, 0)), pl.BlockSpec(pool_scale.shape, lambda b: (0, 0))],
        out_shape=[jax.ShapeDtypeStruct((T, POOL_WIDTH), MXU_DTYPE), jax.ShapeDtypeStruct(w_pool.shape, F32),
                   jax.ShapeDtypeStruct(pool_scale.shape, F32)],
        sem=("arbitrary",), rider=rider)


def _attn_consts():
    lane_g = lax.broadcasted_iota(jnp.int32, (BLOCK, 256), 1) >> 6
    rgrp = lax.broadcasted_iota(jnp.int32, (GROUP * BLOCK, 1), 0) >> 7
    rel = lax.broadcasted_iota(jnp.int32, (BLOCK, 256), 0) - lax.broadcasted_iota(jnp.int32, (BLOCK, 256), 1)

    def bias(off):
        ok = (rel + off >= 0) & (rel + off < BLOCK)
        return jnp.concatenate([jnp.where(ok, 0.0, NEG_INF)] * GROUP, axis=0)

    return lane_g, rgrp, bias(0), bias(BLOCK)


def _sink_rows(sink_ref, hk, rgrp):
    sv = jnp.zeros(rgrp.shape, F32)
    for g in range(GROUP):
        sv = jnp.where(rgrp == g, sink_ref[0, GROUP * hk + g], sv)
    return sv


def _stack_heads(xb, lane_g):
    return jnp.concatenate([jnp.where(lane_g == g, xb, jnp.zeros_like(xb)) for g in range(GROUP)], axis=0)


def _unstack_heads(xs, lane_g):
    out = jnp.where(lane_g == 0, xs[0:BLOCK], 0.0)
    for g in range(1, GROUP):
        out = out + jnp.where(lane_g == g, xs[BLOCK * g:BLOCK * (g + 1)], 0.0)
    return out


def _attn_probs(qs, kb, bias, sv):
    s = _dot(qs, kb, NT) + bias
    m = jnp.maximum(jnp.max(s, axis=1, keepdims=True), sv)
    e = jnp.exp(s - m)
    es = jnp.exp(sv - m)
    inv_l = 1.0 / (jnp.sum(e, axis=1, keepdims=True) + es)
    return e * inv_l, es * inv_l


def _attn_blocks(nb, blk, carry):
    carry = blk(0, 0, True, carry)
    return lax.fori_loop(1, nb, lambda n, c: blk(pl.multiple_of(n * BLOCK, BLOCK),
                                                 pl.multiple_of((n - 1) * BLOCK, BLOCK), False, c), carry)


def _attn_call(sinks, q, k4, v4, S, rider=None):
    T = q.shape[0]
    nb = S // BLOCK

    def body(sink_ref, q_ref, k_ref, v_ref, o_ref):
        lane_g, rgrp, bias_first, bias_later = _attn_consts()
        svs = [_sink_rows(sink_ref, hk, rgrp) for hk in range(N_KV_HEADS)]

        def blk(q0, k0, first, carry):
            for hk in range(N_KV_HEADS):
                cs = slice(256 * hk, 256 * (hk + 1))
                qs = _stack_heads(q_ref[pl.ds(q0, BLOCK), cs], lane_g)
                p, _ = _attn_probs(qs, k_ref[pl.ds(k0, 2 * BLOCK), cs], bias_first if first else bias_later, svs[hk])
                o = _dot(p.astype(MXU_DTYPE), v_ref[pl.ds(k0, 2 * BLOCK), cs], NN)
                o_ref[pl.ds(q0, BLOCK), cs] = _unstack_heads(o, lane_g).astype(MXU_DTYPE)
            return carry

        _attn_blocks(nb, blk, 0)

    seq = pl.BlockSpec((S, ATTN_WIDTH), lambda b: (b, 0))
    return _launch(
        body, [sinks, q, k4, v4], name="attn_fwd", grid=(T // S,),
        in_specs=[pl.BlockSpec(memory_space=pltpu.SMEM), seq, seq, seq],
        out_specs=[seq], out_shape=[jax.ShapeDtypeStruct((T, ATTN_WIDTH), MXU_DTYPE)],
        sem=("arbitrary",), rider=rider)


def _attn_bwd_call(sinks, q, k4, v4, do, rc, rsa, rsb, S, rider=None):
    T = q.shape[0]
    nb = S // BLOCK

    def body(sink_ref, q_ref, k_ref, v_ref, do_ref, c_ref, sa_ref, sb_ref,
             dq_ref, dk_ref, dv_ref, ds_ref, dk_acc, dv_acc):
        lane_g, rgrp, bias_first, bias_later = _attn_consts()
        svs = [_sink_rows(sink_ref, hk, rgrp) for hk in range(N_KV_HEADS)]
        lane1 = lax.broadcasted_iota(jnp.int32, (1, LANES), 1)
        dk_acc[...] = jnp.zeros_like(dk_acc)
        dv_acc[...] = jnp.zeros_like(dv_acc)

        def blk(q0, k0, first, dsink):
            rows = pl.ds(q0, BLOCK)
            c, sa, sb = c_ref[rows, :], sa_ref[rows, :], sb_ref[rows, :]
            for hk in range(N_KV_HEADS):
                cs = slice(256 * hk, 256 * (hk + 1))
                qs = _stack_heads(q_ref[rows, cs], lane_g)
                dos = _stack_heads(do_ref[rows, cs], lane_g)
                kb = k_ref[pl.ds(k0, 2 * BLOCK), cs]
                vb = v_ref[pl.ds(k0, 2 * BLOCK), cs]
                p, ps = _attn_probs(qs, kb, bias_first if first else bias_later, svs[hk])
                dp = _dot(dos, vb, NT)
                delta = jnp.sum(p * dp, axis=1, keepdims=True)
                dsb = (p * (dp - delta)).astype(MXU_DTYPE)
                dqb = _unstack_heads(_dot(dsb, kb, NN), lane_g) * SCALE
                dq_ref[rows, cs] = _rot(dqb, c, -sa, -sb).astype(MXU_DTYPE)
                dk_acc[pl.ds(k0, 2 * BLOCK), cs] += _dot(dsb, qs, TN)
                dv_acc[pl.ds(k0, 2 * BLOCK), cs] += _dot(p.astype(MXU_DTYPE), dos, TN)
                psd = ps * delta
                for g in range(GROUP):
                    val = -jnp.sum(psd[BLOCK * g:BLOCK * (g + 1)], axis=0, keepdims=True)
                    dsink = dsink + jnp.where(lane1 == GROUP * hk + g, val, 0.0)
            return dsink

        dsink = _attn_blocks(nb, blk, jnp.zeros((1, LANES), F32))
        dk_ref[...] = _rot(_fold_heads(dk_acc[...]), c_ref[...], -sa_ref[...], -sb_ref[...]).astype(MXU_DTYPE)
        dv_ref[...] = _fold_heads(dv_acc[...]).astype(MXU_DTYPE)
        ds_ref[...] = jnp.broadcast_to(dsink, ds_ref.shape)

    seq = pl.BlockSpec((S, ATTN_WIDTH), lambda b: (b, 0))
    kvs = pl.BlockSpec((S, KV_WIDTH), lambda b: (b, 0))
    tab = pl.BlockSpec((S, LANES), lambda b: (0, 0))
    nseq = T // S
    return _launch(
        body, [sinks, q, k4, v4, do, rc, rsa, rsb], name="attn_bwd", grid=(nseq,),
        in_specs=[pl.BlockSpec(memory_space=pltpu.SMEM), seq, seq, seq, seq, tab, tab, tab],
        out_specs=[seq, kvs, kvs, pl.BlockSpec((8, LANES), lambda b: (b, 0))],
        out_shape=[jax.ShapeDtypeStruct((T, ATTN_WIDTH), MXU_DTYPE), jax.ShapeDtypeStruct((T, KV_WIDTH), MXU_DTYPE),
                   jax.ShapeDtypeStruct((T, KV_WIDTH), MXU_DTYPE), jax.ShapeDtypeStruct((8 * nseq, LANES), F32)],
        scratch_shapes=[pltpu.VMEM((S, 512), F32), pltpu.VMEM((S, 512), F32)],
        sem=("arbitrary",), rider=rider)


def _branch_weights(wbp_ref, wba_ref, wbp_s, wba_s):
    @pl.when(pl.program_id(0) == 0)
    def _():
        for j in range(N_DEV):
            wbp_s[:, LANES * j:LANES * (j + 1)] = wbp_ref[j]
            wba_s[:, LANES * j:LANES * (j + 1)] = wba_ref[j]


def _mix_fwd_call(yp, ya, g, x, wbp, wba, wout, g2, g3, rider=None):
    T = x.shape[0]
    tm = _tile(T, 512)

    def body(yp_ref, ya_ref, g_ref, x_ref, wbp_ref, wba_ref, wout_ref, g2_ref, g3_ref,
             mix_ref, x1_ref, h2_ref, wbp_s, wba_s):
        _branch_weights(wbp_ref, wba_ref, wbp_s, wba_s)
        bp = _dot(yp_ref[...], wbp_s[...], NN)
        ba = _dot(ya_ref[...], wba_s[...], NN)
        merged = g_ref[:, :D_MODEL].astype(F32) * bp + g_ref[:, D_MODEL:].astype(F32) * ba
        mix = _dot(merged.astype(MXU_DTYPE), wout_ref[...], NN)
        mix_ref[...] = mix
        x1 = x_ref[...] + (mix * _rms_r(mix)) * g2_ref[...]
        x1_ref[...] = x1
        h2_ref[...] = ((x1 * _rms_r(x1)) * g3_ref[...]).astype(MXU_DTYPE)

    tok = lambda w: pl.BlockSpec((tm, w), lambda i: (i, 0))
    full = lambda a: pl.BlockSpec(a.shape, lambda i: (0,) * a.ndim)
    return _launch(
        body, [yp, ya, g, x, wbp, wba, wout, g2, g3], name="mix_fwd", grid=(T // tm,),
        in_specs=[tok(POOL_WIDTH), tok(ATTN_WIDTH), tok(GATE_WIDTH), tok(D_MODEL), full(wbp), full(wba), full(wout),
                  full(g2), full(g3)],
        out_specs=[tok(D_MODEL), tok(D_MODEL), tok(D_MODEL)],
        out_shape=[jax.ShapeDtypeStruct((T, D_MODEL), F32), jax.ShapeDtypeStruct((T, D_MODEL), F32),
                   jax.ShapeDtypeStruct((T, D_MODEL), MXU_DTYPE)],
        scratch_shapes=[pltpu.VMEM((POOL_WIDTH, D_MODEL), MXU_DTYPE), pltpu.VMEM((ATTN_WIDTH, D_MODEL), MXU_DTYPE)],
        sem=("arbitrary",), rider=rider)


def _mix_bwd_call(dx1, mix, yp, ya, g, wbp, wba, wout, g2, rider=None):
    T = dx1.shape[0]
    tm = _tile(T, 512)

    def body(dx1_ref, mix_ref, yp_ref, ya_ref, g_ref, wbp_ref, wba_ref, wout_ref, g2_ref,
             dmix_ref, merged_ref, dbp_ref, dba_ref, dyp_ref, do_ref, dgates_ref, dg2_ref, dbg_ref, wbp_s, wba_s):
        _branch_weights(wbp_ref, wba_ref, wbp_s, wba_s)

        @pl.when(pl.program_id(0) == 0)
        def _():
            dg2_ref[...] = jnp.zeros_like(dg2_ref)
            dbg_ref[...] = jnp.zeros_like(dbg_ref)

        mix = mix_ref[...]
        dmix, dg2 = _rms_bwd(dx1_ref[...], mix, _rms_r(mix), g2_ref[...])
        dg2_ref[...] += jnp.sum(dg2, axis=0, keepdims=True)
        dmixb = dmix.astype(MXU_DTYPE)
        dmix_ref[...] = dmixb
        dmerged = _dot(dmixb, wout_ref[...], NT)
        bp = _dot(yp_ref[...], wbp_s[...], NN)
        ba = _dot(ya_ref[...], wba_s[...], NN)
        gp, ga = g_ref[:, :D_MODEL].astype(F32), g_ref[:, D_MODEL:].astype(F32)
        merged_ref[...] = (gp * bp + ga * ba).astype(MXU_DTYPE)
        dgp = dmerged * bp * (gp * (1.0 - gp))
        dga = dmerged * ba * (ga * (1.0 - ga))
        dbg_ref[:, :D_MODEL] += jnp.sum(dgp, axis=0, keepdims=True)
        dbg_ref[:, D_MODEL:] += jnp.sum(dga, axis=0, keepdims=True)
        dgates_ref[:, :D_MODEL] = dgp.astype(MXU_DTYPE)
        dgates_ref[:, D_MODEL:] = dga.astype(MXU_DTYPE)
        dbp = (dmerged * gp).astype(MXU_DTYPE)
        dba = (dmerged * ga).astype(MXU_DTYPE)
        dbp_ref[...] = dbp
        dba_ref[...] = dba
        dyp_ref[...] = _dot(dbp, wbp_s[...], NT)
        do_ref[...] = _dot(dba, wba_s[...], NT).astype(MXU_DTYPE)

    tok = lambda w: pl.BlockSpec((tm, w), lambda i: (i, 0))
    full = lambda a: pl.BlockSpec(a.shape, lambda i: (0,) * a.ndim)
    acc = lambda w: pl.BlockSpec((1, w), lambda i: (0, 0))
    sd = jax.ShapeDtypeStruct
    return _launch(
        body, [dx1, mix, yp, ya, g, wbp, wba, wout, g2], name="mix_bwd", grid=(T // tm,),
        in_specs=[tok(D_MODEL), tok(D_MODEL), tok(POOL_WIDTH), tok(ATTN_WIDTH), tok(GATE_WIDTH), full(wbp), full(wba),
                  full(wout), full(g2)],
        out_specs=[tok(D_MODEL), tok(D_MODEL), tok(D_MODEL), tok(D_MODEL), tok(POOL_WIDTH), tok(ATTN_WIDTH),
                   tok(GATE_WIDTH), acc(D_MODEL), acc(GATE_WIDTH)],
        out_shape=[sd((T, D_MODEL), MXU_DTYPE), sd((T, D_MODEL), MXU_DTYPE), sd((T, D_MODEL), MXU_DTYPE),
                   sd((T, D_MODEL), MXU_DTYPE), sd((T, POOL_WIDTH), F32), sd((T, ATTN_WIDTH), MXU_DTYPE),
                   sd((T, GATE_WIDTH), MXU_DTYPE), sd((1, D_MODEL), F32), sd((1, GATE_WIDTH), F32)],
        scratch_shapes=[pltpu.VMEM((POOL_WIDTH, D_MODEL), MXU_DTYPE), pltpu.VMEM((ATTN_WIDTH, D_MODEL), MXU_DTYPE)],
        sem=("arbitrary",), rider=rider)


def _mlp_call(x1, h2, target, wup, wdown, g3, g4):
    T = x1.shape[0]
    tm = _tile(T, 256)
    fc = D_FF // N_DEV

    def body(x1_ref, h2_ref, t_ref, wup_ref, wdown_ref, g3_ref, g4_ref,
             act_ref, da_ref, dff_ref, dx1_ref, dg3_ref, dg4_ref, loss_ref, rl_s):
        @pl.when(pl.program_id(0) == 0)
        def _():
            dg3_ref[...] = jnp.zeros_like(dg3_ref)
            dg4_ref[...] = jnp.zeros_like(dg4_ref)
            loss_ref[...] = jnp.zeros_like(loss_ref)

        h2 = h2_ref[...]
        ff = jnp.zeros((tm, D_MODEL), F32)
        for j in range(N_DEV):
            sl = slice(fc * j, fc * (j + 1))
            rl = jnp.maximum(_dot(h2, wup_ref[j], NN), 0.0)
            rl_s[:, sl] = rl
            actb = (rl * rl).astype(MXU_DTYPE)
            act_ref[:, sl] = actb
            ff = ff + _dot(actb, wdown_ref[j], NN)
        x1 = x1_ref[...]
        r4 = _rms_r(ff)
        err = x1 + (ff * r4) * g4_ref[...] - t_ref[...]
        loss_ref[...] += jnp.sum(err * err, axis=0, keepdims=True)
        dy = err * (1.0 / D_MODEL)
        dff, dg4 = _rms_bwd(dy, ff, r4, g4_ref[...])
        dg4_ref[...] += jnp.sum(dg4, axis=0, keepdims=True)
        dffb = dff.astype(MXU_DTYPE)
        dff_ref[...] = dffb
        dh2 = jnp.zeros((tm, D_MODEL), F32)
        for j in range(N_DEV):
            sl = slice(fc * j, fc * (j + 1))
            dab = (_dot(dffb, wdown_ref[j], NT) * (2.0 * rl_s[:, sl])).astype(MXU_DTYPE)
            da_ref[:, sl] = dab
            dh2 = dh2 + _dot(dab, wup_ref[j], NT)
        dx1, dg3 = _rms_bwd(dh2, x1, _rms_r(x1), g3_ref[...])
        dg3_ref[...] += jnp.sum(dg3, axis=0, keepdims=True)
        dx1_ref[...] = dy + dx1

    tok = lambda w: pl.BlockSpec((tm, w), lambda i: (i, 0))
    full = lambda a: pl.BlockSpec(a.shape, lambda i: (0,) * a.ndim, pipeline_mode=pl.Buffered(1))
    vec = pl.BlockSpec((1, D_MODEL), lambda i: (0, 0))
    sd = jax.ShapeDtypeStruct
    return pl.pallas_call(
        body, name="mlp_fwd_bwd", grid=(T // tm,),
        in_specs=[tok(D_MODEL), tok(D_MODEL), tok(D_MODEL), full(wup), full(wdown), vec, vec],
        out_specs=[tok(D_FF), tok(D_FF), tok(D_MODEL), tok(D_MODEL), vec, vec, vec],
        out_shape=[sd((T, D_FF), MXU_DTYPE), sd((T, D_FF), MXU_DTYPE), sd((T, D_MODEL), MXU_DTYPE),
                   sd((T, D_MODEL), F32), sd((1, D_MODEL), F32), sd((1, D_MODEL), F32), sd((1, D_MODEL), F32)],
        scratch_shapes=[pltpu.VMEM((tm, D_FF), F32)],
        compiler_params=_params(("arbitrary",)),
    )(x1, h2, target, wup, wdown, g3, g4)


def _inproj_bwd_call(du, dq, dk, dv, dgates, dx1, x, win_t, g1, rider=None):
    T = x.shape[0]
    tm = _tile(T, 512)

    def body(du_ref, dq_ref, dk_ref, dv_ref, dgt_ref, dx1_ref, x_ref, w_ref, g1_ref, gx_ref, dg1_ref, db_ref):
        @pl.when(pl.program_id(0) == 0)
        def _():
            dg1_ref[...] = jnp.zeros_like(dg1_ref)
            db_ref[...] = jnp.zeros_like(db_ref)

        dh = jnp.zeros((tm, D_MODEL), F32)
        for ref, lo, hi in ((du_ref, 0, C_Q), (dq_ref, C_Q, C_K), (dk_ref, C_K, C_V), (dv_ref, C_V, C_G),
                            (dgt_ref, C_G, IN_WIDTH)):
            piece = ref[...]
            dh = dh + _dot(piece, w_ref[lo:hi, :], NN)
            if hi <= C_G:
                db_ref[:, lo:hi] += jnp.sum(piece.astype(F32), axis=0, keepdims=True)
        xv = x_ref[...]
        dx, dg1 = _rms_bwd(dh, xv, _rms_r(xv), g1_ref[...])
        dg1_ref[...] += jnp.sum(dg1, axis=0, keepdims=True)
        gx_ref[...] = dx1_ref[...] + dx

    tok = lambda w: pl.BlockSpec((tm, w), lambda i: (i, 0))
    full = lambda a: pl.BlockSpec(a.shape, lambda i: (0,) * a.ndim)
    sd = jax.ShapeDtypeStruct
    return _launch(
        body, [du, dq, dk, dv, dgates, dx1, x, win_t, g1], name="inproj_bwd", grid=(T // tm,),
        in_specs=[tok(POOL_WIDTH), tok(ATTN_WIDTH), tok(KV_WIDTH), tok(KV_WIDTH), tok(GATE_WIDTH), tok(D_MODEL),
                  tok(D_MODEL), full(win_t), full(g1)],
        out_specs=[tok(D_MODEL), pl.BlockSpec((1, D_MODEL), lambda i: (0, 0)), pl.BlockSpec((1, C_G), lambda i: (0, 0))],
        out_shape=[sd((T, D_MODEL), F32), sd((1, D_MODEL), F32), sd((1, C_G), F32)],
        sem=("arbitrary",), rider=rider)


WGRAD_TOKENS = 1024


def _wgrad_rows_call(a, b, name, rider=None):
    T, K = a.shape
    N = b.shape[1]
    tm = _tile(T, WGRAD_TOKENS)
    kb = min(K, 1024)
    per = kb // (K // N_DEV)

    def body(a_ref, b_ref, o_ref):
        @pl.when(pl.program_id(1) == 0)
        def _():
            o_ref[...] = jnp.zeros_like(o_ref)

        d = _dot(a_ref[...], b_ref[...], TN)
        rs = kb // per
        for j in range(per):
            o_ref[j] += d[rs * j:rs * (j + 1)]

    return _launch(
        body, [a, b], name=name, grid=(K // kb, T // tm),
        in_specs=[pl.BlockSpec((tm, kb), lambda i, t: (t, i)), pl.BlockSpec((tm, N), lambda i, t: (t, 0))],
        out_specs=[pl.BlockSpec((per, K // N_DEV, N), lambda i, t: (i, 0, 0))],
        out_shape=[jax.ShapeDtypeStruct((N_DEV, K // N_DEV, N), F32)],
        sem=("arbitrary", "arbitrary"), rider=rider)


def _wgrad_cols_call(pairs, name, rider=None):
    T, K = pairs[0][0].shape
    N = pairs[0][1].shape[1]
    assert all(a.shape == (T, K) and b.shape == (T, N) for a, b in pairs)
    n = len(pairs)
    tm = _tile(T, WGRAD_TOKENS)
    nb = min(N, 1024)
    per = nb // (N // N_DEV)

    def body(*refs):
        for t in range(n):
            a_ref, b_ref, o_ref = refs[2 * t], refs[2 * t + 1], refs[2 * n + t]

            @pl.when(pl.program_id(1) == 0)
            def _():
                o_ref[...] = jnp.zeros_like(o_ref)

            d = _dot(a_ref[...], b_ref[...], TN)
            cs = nb // per
            for j in range(per):
                o_ref[j] += d[:, cs * j:cs * (j + 1)]

    return _launch(
        body, [m for ab in pairs for m in ab], name=name, grid=(N // nb, T // tm),
        in_specs=[pl.BlockSpec((tm, K), lambda i, t: (t, 0)), pl.BlockSpec((tm, nb), lambda i, t: (t, i))] * n,
        out_specs=[pl.BlockSpec((per, K, N // N_DEV), lambda i, t: (i, 0, 0))] * n,
        out_shape=[jax.ShapeDtypeStruct((N_DEV, K, N // N_DEV), F32)] * n,
        sem=("arbitrary", "arbitrary"), rider=rider)


def _wgrad_in_call(du, dq, dk, dv, dgates, h, rider=None):
    T = h.shape[0]
    tm = _tile(T, WGRAD_TOKENS)
    rows = IN_WIDTH // N_DEV

    def body(du_ref, dq_ref, dk_ref, dv_ref, dgt_ref, h_ref, o_ref, acc, sem):
        t = pl.program_id(0)

        @pl.when(t == 0)
        def _():
            acc[...] = jnp.zeros_like(acc)

        hv = h_ref[...]
        for ref, lo, hi in ((du_ref, 0, C_Q), (dq_ref, C_Q, C_K), (dk_ref, C_K, C_V), (dv_ref, C_V, C_G),
                            (dgt_ref, C_G, IN_WIDTH)):
            acc[lo:hi, :] += _dot(ref[...], hv, TN)

        @pl.when(t == pl.num_programs(0) - 1)
        def _():
            copies = [pltpu.make_async_copy(acc.at[pl.ds(rows * j, rows), :], o_ref.at[j], sem.at[j])
                      for j in range(N_DEV)]
            for cp in copies:
                cp.start()
            for cp in copies:
                cp.wait()

    tok = lambda w: pl.BlockSpec((tm, w), lambda t: (t, 0))
    return _launch(
        body, [du, dq, dk, dv, dgates, h], name="wgrad_in", grid=(T // tm,),
        in_specs=[tok(POOL_WIDTH), tok(ATTN_WIDTH), tok(KV_WIDTH), tok(KV_WIDTH), tok(GATE_WIDTH), tok(D_MODEL)],
        out_specs=[pl.BlockSpec(memory_space=pl.ANY)],
        out_shape=[jax.ShapeDtypeStruct((N_DEV, rows, D_MODEL), F32)],
        scratch_shapes=[pltpu.VMEM((IN_WIDTH, D_MODEL), F32), pltpu.SemaphoreType.DMA((N_DEV,))],
        sem=("arbitrary",), rider=rider)


def _coords():
    return lax.axis_index("x"), lax.axis_index("y"), lax.axis_index("c")


def _allgather_call(shards):
    n = len(shards)

    def body(*refs):
        ins, outs = refs[:n], refs[n:2 * n]
        send_sems, recv_sems, local_sems = refs[2 * n:]
        x, y, c = _coords()
        me, sibling = (x, y, c), (x, y, 1 - c)
        chips = [(1 - x, y), (x, 1 - y), (1 - x, 1 - y)]

        def slot(p):
            return 4 * p[0] + 2 * p[1] + p[2]

        def copy(t, k, block, to, src=None):
            dst = outs[t].at[slot(block)]
            return pltpu.make_async_remote_copy(
                src_ref=dst if src is None else src, dst_ref=dst, send_sem=send_sems.at[t, k],
                recv_sem=recv_sems.at[t, k], device_id=to, device_id_type=MESH)

        mine = [pltpu.make_async_copy(ins[t], outs[t].at[slot(me)], local_sems.at[t]) for t in range(n)]
        for cp in mine:
            cp.start()
        first = []
        for t in range(n):
            first.append(copy(t, 0, me, sibling, src=ins[t]))
            first += [copy(t, 1 + j, me, (*chip, c), src=ins[t]) for j, chip in enumerate(chips)]
        for cp in first:
            cp.start()
        passed = []
        for t in range(n):
            for j, chip in enumerate(chips):
                copy(t, 1 + j, (*chip, c), me).wait_recv()
                fwd = copy(t, 4 + j, (*chip, c), sibling)
                fwd.start()
                passed.append(fwd)
        for t in range(n):
            copy(t, 0, sibling, me).wait_recv()
            for j, chip in enumerate(chips):
                copy(t, 4 + j, (*chip, 1 - c), me).wait_recv()
        for cp in first + passed:
            cp.wait_send()
        for cp in mine:
            cp.wait()

    hbm = pl.BlockSpec(memory_space=pl.ANY)
    return pl.pallas_call(
        body, name="allgather_weights",
        in_specs=[hbm] * n, out_specs=[hbm] * n,
        out_shape=[jax.ShapeDtypeStruct((N_DEV,) + s.shape, s.dtype) for s in shards],
        scratch_shapes=[pltpu.SemaphoreType.DMA((n, 7)), pltpu.SemaphoreType.DMA((n, 7)), pltpu.SemaphoreType.DMA((n,))],
    )(*shards)


def _slot(p):
    return 4 * p[0] + 2 * p[1] + p[2]


def _rows(ref, span):
    return ref if span is None else ref.at[pl.ds(span[0], span[1])]


ALL = "all"


def _rows(ref, span):
    return ref if span == ALL else ref.at[pl.ds(span[0], span[1])]


def _rider_ag(items):
    ins, out_shape, aliases, where = [], [], {}, []
    n_remote = n_local = 0
    for t, (shard, buf, snd, fwd) in enumerate(items):
        i_shard = i_buf = None
        if snd is not None:
            i_shard = len(ins)
            ins.append(shard)
        if buf is not None:
            i_buf = len(ins)
            ins.append(buf)
            aliases[i_buf] = t
            out_shape.append(jax.ShapeDtypeStruct(buf.shape, buf.dtype))
        else:
            assert fwd is None and snd is not None
            out_shape.append(jax.ShapeDtypeStruct((N_DEV,) + shard.shape, shard.dtype))
        where.append((i_shard, i_buf, n_remote, n_local))
        n_remote += (4 if snd is not None else 0) + (3 if fwd is not None else 0)
        n_local += 1 if snd is not None else 0

    def plan(rins, routs, send, recv, loc, r0, l0):
        x, y, c = _coords()
        peers = [(x, y, 1 - c), (1 - x, y, c), (x, 1 - y, c), (1 - x, 1 - y, c)]
        remote, local = [], []
        for t, (shard, buf, snd, fwd) in enumerate(items):
            i_shard, i_buf, k, l = where[t]
            k, l = r0 + k, l0 + l
            if snd is not None:
                src, dst = _rows(rins[i_shard], snd), _rows(routs[t].at[_slot((x, y, c))], snd)
                local.append(pltpu.make_async_copy(src, dst, loc.at[l]))
                for peer in peers:
                    remote.append(pltpu.make_async_remote_copy(
                        src_ref=src, dst_ref=dst, send_sem=send.at[k], recv_sem=recv.at[k],
                        device_id=peer, device_id_type=MESH))
                    k += 1
            if fwd is not None:
                for px, py, pc in peers[1:]:
                    s = _slot((px, py, pc))
                    remote.append(pltpu.make_async_remote_copy(
                        src_ref=_rows(rins[i_buf].at[s], fwd), dst_ref=_rows(routs[t].at[s], fwd),
                        send_sem=send.at[k], recv_sem=recv.at[k], device_id=peers[0], device_id_type=MESH))
                    k += 1
        return remote, local

    return _Rider(ins, out_shape, n_remote, n_local, plan, aliases)


def _rider_rs_sibling(grads):
    n = len(grads)

    def plan(ins, outs, send, recv, loc, r0, l0):
        x, y, c = _coords()
        remote = []
        for t in range(n):
            for q in range(4):
                remote.append(pltpu.make_async_remote_copy(
                    src_ref=ins[t].at[q, 1 - c], dst_ref=outs[t].at[q], send_sem=send.at[r0 + 4 * t + q],
                    recv_sem=recv.at[r0 + 4 * t + q], device_id=(x, y, 1 - c), device_id_type=MESH))
        return remote, []

    return _Rider(grads, [jax.ShapeDtypeStruct((4,) + g.shape[2:], g.dtype) for g in grads], 4 * n, 0, plan)


def _rider_rs_chips(sums, rows=None, into=None):
    n = len(sums)
    rows = rows or [ALL] * n

    def plan(ins, outs, send, recv, loc, r0, l0):
        x, y, c = _coords()
        remote = []
        for t in range(n):
            for r, (px, py) in enumerate([(1 - x, y), (x, 1 - y), (1 - x, 1 - y)]):
                remote.append(pltpu.make_async_remote_copy(
                    src_ref=_rows(ins[t].at[2 * px + py], rows[t]), dst_ref=_rows(outs[t].at[r], rows[t]),
                    send_sem=send.at[r0 + 3 * t + r], recv_sem=recv.at[r0 + 3 * t + r],
                    device_id=(px, py, c), device_id_type=MESH))
        return remote, []

    out_shape = [jax.ShapeDtypeStruct((3,) + s.shape[1:], s.dtype) for s in sums]
    if into is None:
        return _Rider(sums, out_shape, 3 * n, 0, plan)
    return _Rider(list(sums) + list(into), out_shape, 3 * n, 0, plan, aliases={n + t: t for t in range(n)})


def _rider_gather_direct(parts):
    n = len(parts)

    def plan(ins, outs, send, recv, loc, r0, l0):
        x, y, c = _coords()
        me = _slot((x, y, c))
        remote, local = [], []
        for t in range(n):
            local.append(pltpu.make_async_copy(ins[t], outs[t].at[me], loc.at[l0 + t]))
            for k in range(1, N_DEV):
                peer = (x ^ ((k >> 2) & 1), y ^ ((k >> 1) & 1), c ^ (k & 1))
                remote.append(pltpu.make_async_remote_copy(
                    src_ref=ins[t], dst_ref=outs[t].at[me], send_sem=send.at[r0 + 7 * t + k - 1],
                    recv_sem=recv.at[r0 + 7 * t + k - 1], device_id=peer, device_id_type=MESH))
        return remote, local

    return _Rider(parts, [jax.ShapeDtypeStruct((N_DEV,) + p.shape, p.dtype) for p in parts], 7 * n, n, plan)


def _chip_sum_call(cidx, grads, recvd, out_dtypes, name):
    n = len(grads)

    def body(c_ref, *refs):
        for t in range(n):
            refs[2 * n + t][0] = (refs[t][0, 0] + refs[n + t][0]).astype(out_dtypes[t])

    in_specs = [pl.BlockSpec((1, 1) + g.shape[2:], lambda q, c_ref: (q, c_ref[0], 0, 0)) for g in grads]
    in_specs += [pl.BlockSpec((1,) + r.shape[1:], lambda q, c_ref: (q, 0, 0)) for r in recvd]
    return pl.pallas_call(
        body, name=name,
        grid_spec=pltpu.PrefetchScalarGridSpec(
            num_scalar_prefetch=1, grid=(4,), in_specs=in_specs,
            out_specs=[pl.BlockSpec((1,) + r.shape[1:], lambda q, c_ref: (q, 0, 0)) for r in recvd]),
        out_shape=[jax.ShapeDtypeStruct(r.shape, dt) for r, dt in zip(recvd, out_dtypes)],
        compiler_params=_params(("arbitrary",)),
    )(cidx, *grads, *recvd)


def _final_sum_call(idx, grads, recvd1, recvd2):
    n = len(grads)
    nsteps = 2

    def body(i_ref, *refs):
        for t in range(n):
            g, r1, r2, o = refs[t], refs[n + t], refs[2 * n + t], refs[3 * n + t]
            s = g[0, 0] + r1[0]
            for r in range(3):
                s = s + r2[r].astype(F32)
            o[...] = s

    def rows(a):
        r = a.shape[-2]
        return r // nsteps if (r // nsteps) % 16 == 0 else r

    def step(a):
        return (lambda i: i) if rows(a) != a.shape[-2] else (lambda i: 0)

    in_specs = [pl.BlockSpec((1, 1, rows(g), g.shape[3]), lambda i, s, st=step(g): (s[0], s[1], st(i), 0)) for g in grads]
    in_specs += [pl.BlockSpec((1, rows(r), r.shape[2]), lambda i, s, st=step(r): (s[0], st(i), 0)) for r in recvd1]
    in_specs += [pl.BlockSpec((3, rows(r), r.shape[2]), lambda i, s, st=step(r): (0, st(i), 0)) for r in recvd2]
    return pl.pallas_call(
        body, name="rs_final_sum",
        grid_spec=pltpu.PrefetchScalarGridSpec(
            num_scalar_prefetch=1, grid=(nsteps,), in_specs=in_specs,
            out_specs=[pl.BlockSpec((rows(r), r.shape[2]), lambda i, s, st=step(r): (st(i), 0)) for r in recvd2]),
        out_shape=[jax.ShapeDtypeStruct(r.shape[1:], F32) for r in recvd2],
        compiler_params=_params(("arbitrary",)),
    )(idx, *grads, *recvd1, *recvd2)


def _sum8_call(parts):
    def body(p_ref, o_ref):
        s = p_ref[0]
        for j in range(1, N_DEV):
            s = s + p_ref[j]
        o_ref[...] = s

    return pl.pallas_call(body, name="sum_small_partials",
                          out_shape=jax.ShapeDtypeStruct(parts.shape[1:], parts.dtype))(parts)


def _adamw(w, g, m, v):
    m = ADAM_B1 * m + (1.0 - ADAM_B1) * g
    v = ADAM_B2 * v + (1.0 - ADAM_B2) * (g * g)
    m_hat = m / (1.0 - ADAM_B1 ** ADAM_STEP)
    v_hat = v / (1.0 - ADAM_B2 ** ADAM_STEP)
    delta = -ADAM_LR * (m_hat / (jnp.sqrt(v_hat) + ADAM_EPS) + ADAM_WD * w)
    return delta, m, v


def _adamw_call(ws, gs, ms, vs, nsteps, name):
    n = len(ws)

    def body(*refs):
        for t in range(n):
            w, g, m, v = (refs[k * n + t][...] for k in range(4))
            d, m2, v2 = _adamw(w, g, m, v)
            refs[4 * n + t][...] = d
            refs[5 * n + t][...] = m2
            refs[6 * n + t][...] = v2

    def spec(a):
        assert a.shape[0] % nsteps == 0 and (nsteps == 1 or (a.shape[0] // nsteps) % 8 == 0), a.shape
        return pl.BlockSpec((a.shape[0] // nsteps, a.shape[1]), lambda i: (i, 0))

    specs = [spec(a) for a in ws]
    outs = pl.pallas_call(
        body, name=name, grid=(nsteps,),
        in_specs=specs * 4, out_specs=specs * 3,
        out_shape=[jax.ShapeDtypeStruct(a.shape, F32) for a in ws] * 3,
        compiler_params=_params(("arbitrary",)),
    )(*ws, *gs, *ms, *vs)
    return outs[:n], outs[n:2 * n], outs[2 * n:]


def _adamw_rs_call(idx, gws, r1s, r2s, ws, ms, vs, nsteps, name):
    n = len(ws)

    def body(i_ref, *refs):
        for t in range(n):
            gw, r1, r2, w, m, v = (refs[k * n + t] for k in range(6))
            g = gw[0, 0] + r1[0]
            for r in range(3):
                g = g + r2[r].astype(F32)
            d, m2, v2 = _adamw(w[...], g, m[...], v[...])
            refs[6 * n + t][...] = g
            refs[7 * n + t][...] = d
            refs[8 * n + t][...] = m2
            refs[9 * n + t][...] = v2

    def rb(a):
        r = a.shape[0] // nsteps
        assert a.shape[0] % nsteps == 0 and r % 16 == 0, a.shape
        return r

    in_specs = [pl.BlockSpec((1, 1, rb(w), w.shape[1]), lambda i, s: (s[0], s[1], i, 0)) for w in ws]
    in_specs += [pl.BlockSpec((1, rb(w), w.shape[1]), lambda i, s: (s[0], i, 0)) for w in ws]
    in_specs += [pl.BlockSpec((3, rb(w), w.shape[1]), lambda i, s: (0, i, 0)) for w in ws]
    plain = [pl.BlockSpec((rb(w), w.shape[1]), lambda i, s: (i, 0)) for w in ws]
    outs = pl.pallas_call(
        body, name=name,
        grid_spec=pltpu.PrefetchScalarGridSpec(num_scalar_prefetch=1, grid=(nsteps,), in_specs=in_specs + plain * 3,
                                               out_specs=plain * 4),
        out_shape=[jax.ShapeDtypeStruct(w.shape, F32) for w in ws] * 4,
        compiler_params=_params(("arbitrary",)),
    )(idx, *gws, *r1s, *r2s, *ws, *ms, *vs)
    return outs[:n], outs[n:2 * n], outs[2 * n:3 * n], outs[3 * n:]


def _rows128(a, pad_rows):
    flat = a.reshape(-1).astype(F32)
    flat = jnp.pad(flat, (0, pad_rows * LANES - flat.shape[0]))
    return flat.reshape(pad_rows, LANES)


_SMALL_A = (("w_pool", 512), ("pool_scale", 8), ("attn_sinks", 8), ("g_mix_post", 8), ("g_mlp_pre", 8),
            ("g_mlp_post", 8), ("loss", 8), ("b_in_gates", 16))
_SMALL_A_ROWS = 640
_SMALL_B = (("g_mix_pre", 8), ("b_in_head", 16))


def _pack(parts, layout, total_rows):
    rows = [_rows128(parts[k], r) for k, r in layout]
    pad = total_rows - sum(r for _, r in layout)
    if pad:
        rows.append(jnp.zeros((pad, LANES), F32))
    return jnp.concatenate(rows, axis=0)


def _unpack(buf, layout, sizes):
    out, off = {}, 0
    for k, r in layout:
        out[k] = buf[off:off + r].reshape(-1)[:sizes[k]]
        off += r
    return out


def kernel(x, g_mix_pre, w_in, b_in, w_pool, pool_scale, attn_sinks, w_branch_pool, w_branch_attn, w_out, g_mix_post, g_mlp_pre, w_up, w_down, g_mlp_post, loss_target, m_g_mix_pre, m_w_in, m_b_in, m_w_pool, m_pool_scale, m_attn_sinks, m_w_branch_pool, m_w_branch_attn, m_w_out, m_g_mix_post, m_g_mlp_pre, m_w_up, m_w_down, m_g_mlp_post, v_g_mix_pre, v_w_in, v_b_in, v_w_pool, v_pool_scale, v_attn_sinks, v_w_branch_pool, v_w_branch_attn, v_w_out, v_g_mix_post, v_g_mlp_pre, v_w_up, v_w_down, v_g_mlp_post):
    B, S, _ = x.shape
    T = B * S
    xt = x.reshape(T, D_MODEL)
    tgt = loss_target.reshape(T, D_MODEL)
    cx, cy, cc = _coords()

    cidx = jnp.reshape(cc, (1,)).astype(jnp.int32)
    by_chip = lambda gr: gr.reshape((4, 2) + gr.shape[1:])
    bf = lambda w: w[0].astype(MXU_DTYPE)

    (win_s,) = _allgather_call([w_in[0].T.astype(MXU_DTYPE)])
    win_t = win_s.reshape(IN_WIDTH, D_MODEL)
    wpool_b = bf(w_pool)
    rc, rsa, rsb = _rot_tables(S)

    up_a, up_b = (0, D_MODEL // 2), (D_MODEL // 2, D_MODEL // 2)
    dn_a, dn_b = (0, D_FF // 16), (D_FF // 16, D_FF // 16)
    wup_l, wdown_l = bf(w_up), bf(w_down)
    (h, u, q, k4, v4, g), (wbp_1, wba_1, wout_1, wup_1) = _inproj_call(
        xt, g_mix_pre, win_t, b_in, rc, rsa, rsb, S,
        rider=_rider_ag([(bf(w_branch_pool), None, ALL, None), (bf(w_branch_attn), None, ALL, None),
                         (bf(w_out), None, ALL, None), (wup_l, None, up_a, None)]))
    yp = _pool_call(u, wpool_b, pool_scale, S)
    (ya,), (wbp_s, wba_s, wout_s, wup_2, wdown_1) = _attn_call(
        attn_sinks, q, k4, v4, S,
        rider=_rider_ag([(None, wbp_1, None, ALL), (None, wba_1, None, ALL), (None, wout_1, None, ALL),
                         (wup_l, wup_1, up_b, up_a), (wdown_l, None, dn_a, None)]))
    wout_f = wout_s.reshape(D_MODEL, D_MODEL)
    (mix, x1, h2), (wup_s, wdown_2) = _mix_fwd_call(
        yp, ya, g, xt, wbp_s, wba_s, wout_f, g_mix_post, g_mlp_pre,
        rider=_rider_ag([(None, wup_2, None, up_b), (wdown_l, wdown_1, dn_b, dn_a)]))
    (wdown_s,) = _comm_call(_rider_ag([(None, wdown_2, None, dn_b)]), "allgather_finish")

    act, da, dff, dx1, dg3, dg4, lossvec = _mlp_call(x1, h2, tgt, wup_s, wdown_s, g_mlp_pre, g_mlp_post)
    gw_down = by_chip(_wgrad_rows_call(act, dff, "wgrad_down")[0])
    (gw_up,), (r1_down,) = _wgrad_cols_call([(h2, da)], "wgrad_up", rider=_rider_rs_sibling([gw_down]))
    gw_up = by_chip(gw_up)
    (s_down,) = _chip_sum_call(cidx, [gw_down], [r1_down], [MXU_DTYPE], "rs_chip_sum_down")
    (dmix, merged, dbp, dba, dyp, do, dgates, dg2, dbg), (r2_down_a, r1_up) = _mix_bwd_call(
        dx1, mix, yp, ya, g, wbp_s, wba_s, wout_f, g_mix_post,
        rider=_merge_riders(_rider_rs_chips([s_down], rows=[dn_a]), _rider_rs_sibling([gw_up])))
    (s_up,) = _chip_sum_call(cidx, [gw_up], [r1_up], [MXU_DTYPE], "rs_chip_sum_up")
    (dq, dk, dv, dsink), (r2_down, r2_up_a) = _attn_bwd_call(
        attn_sinks, q, k4, v4, do, rc, rsa, rsb, S,
        rider=_merge_riders(_rider_rs_chips([s_down], rows=[dn_b], into=[r2_down_a]),
                            _rider_rs_chips([s_up], rows=[up_a])))
    gw_out = by_chip(_wgrad_rows_call(merged, dmix, "wgrad_out")[0])
    (gw_bp, gw_ba), (r2_up,) = _wgrad_cols_call(
        [(yp, dbp), (ya, dba)], "wgrad_branch", rider=_rider_rs_chips([s_up], rows=[up_b], into=[r2_up_a]))
    gw_bp, gw_ba = by_chip(gw_bp), by_chip(gw_ba)
    (du, dwp, dps), (r1_out, r1_bp, r1_ba) = _pool_bwd_call(
        u, dyp, wpool_b, pool_scale, S, rider=_rider_rs_sibling([gw_out, gw_bp, gw_ba]))
    s_obb = _chip_sum_call(cidx, [gw_out, gw_bp, gw_ba], [r1_out, r1_bp, r1_ba], [MXU_DTYPE] * 3, "rs_chip_sum_branch")
    (gw_in,), (r2_out, r2_bp, r2_ba) = _wgrad_in_call(du, dq, dk, dv, dgates, h, rider=_rider_rs_chips(s_obb))
    gw_in = by_chip(gw_in)

    small_a = {"w_pool": dwp, "pool_scale": dps,
               "attn_sinks": jnp.sum(dsink.reshape(B, 8, LANES)[:, 0, :N_Q_HEADS], axis=0), "g_mix_post": dg2,
               "g_mlp_pre": dg3, "g_mlp_post": dg4, "loss": lossvec, "b_in_gates": dbg}
    gw_sa = by_chip(_pack(small_a, _SMALL_A, _SMALL_A_ROWS).reshape(N_DEV, _SMALL_A_ROWS // N_DEV, LANES))
    r1_in, r1_sa = _comm_call(_rider_rs_sibling([gw_in, gw_sa]), "rs_sibling_in")
    s_in, s_sa = _chip_sum_call(cidx, [gw_in, gw_sa], [r1_in, r1_sa], [MXU_DTYPE, F32], "rs_chip_sum_in")
    (gx, dg1, dba_in), (r2_in, r2_sa) = _inproj_bwd_call(
        du, dq, dk, dv, dgates, dx1, xt, win_t, g_mix_pre, rider=_rider_rs_chips([s_in, s_sa]))

    idx = jnp.stack([2 * cx + cy, cc]).astype(jnp.int32)
    (g_sa,) = _final_sum_call(idx, [gw_sa], [r1_sa], [r2_sa])
    part_b = _pack({"g_mix_pre": dg1, "b_in_head": dba_in}, _SMALL_B, sum(r for _, r in _SMALL_B))
    sa_all, sb_all = _comm_call(_rider_gather_direct([g_sa, part_b]), "allgather_small")
    sb_sum = _sum8_call(sb_all)

    in_t = _adamw_rs_call(idx, [gw_in], [r1_in], [r2_in], [w_in[0].T], [m_w_in[0].T], [v_w_in[0].T], 2, "adamw_w_in")
    rest = _adamw_rs_call(
        idx, [gw_bp, gw_ba, gw_out, gw_up, gw_down], [r1_bp, r1_ba, r1_out, r1_up, r1_down],
        [r2_bp, r2_ba, r2_out, r2_up, r2_down], [w_branch_pool[0], w_branch_attn[0], w_out[0], w_up[0], w_down[0]],
        [m_w_branch_pool[0], m_w_branch_attn[0], m_w_out[0], m_w_up[0], m_w_down[0]],
        [v_w_branch_pool[0], v_w_branch_attn[0], v_w_out[0], v_w_up[0], v_w_down[0]], N_DEV, "adamw_shards")
    big_g, big_d, big_m2, big_v2 = ([a[0].T] + list(b) for a, b in zip(in_t, rest))

    names = ["g_mix_pre", "b_in", "w_pool", "pool_scale", "attn_sinks", "g_mix_post", "g_mlp_pre", "g_mlp_post"]
    sm_w = dict(g_mix_pre=g_mix_pre, b_in=b_in, w_pool=w_pool, pool_scale=pool_scale, attn_sinks=attn_sinks,
                g_mix_post=g_mix_post, g_mlp_pre=g_mlp_pre, g_mlp_post=g_mlp_post)
    sm_m = dict(g_mix_pre=m_g_mix_pre, b_in=m_b_in, w_pool=m_w_pool, pool_scale=m_pool_scale, attn_sinks=m_attn_sinks,
                g_mix_post=m_g_mix_post, g_mlp_pre=m_g_mlp_pre, g_mlp_post=m_g_mlp_post)
    sm_v = dict(g_mix_pre=v_g_mix_pre, b_in=v_b_in, w_pool=v_w_pool, pool_scale=v_pool_scale, attn_sinks=v_attn_sinks,
                g_mix_post=v_g_mix_post, g_mlp_pre=v_g_mlp_pre, g_mlp_post=v_g_mlp_post)
    sizes = {k: sm_w[k].size for k in names}
    sizes.update(loss=D_MODEL, b_in_gates=GATE_WIDTH, b_in_head=C_G)
    sm_g = _unpack(sa_all.reshape(_SMALL_A_ROWS, LANES), _SMALL_A, sizes)
    sm_g.update(_unpack(sb_sum, _SMALL_B, sizes))
    sm_g["b_in"] = jnp.concatenate([sm_g["b_in_head"], sm_g["b_in_gates"]])
    loss = (0.5 / D_MODEL) * jnp.sum(sm_g["loss"])
    two_d = lambda a: a.reshape(-1, a.shape[-1])
    sd_, sm2_, sv2_ = _adamw_call([two_d(sm_w[k]) for k in names], [two_d(sm_g[k].reshape(sm_w[k].shape)) for k in names],
                                  [two_d(sm_m[k]) for k in names], [two_d(sm_v[k]) for k in names], 1, "adamw_small")
    like = lambda vals: {k: a.reshape(sm_w[k].shape) for k, a in zip(names, vals)}
    sm_d, sm_m2, sm_v2 = like(sd_), like(sm2_), like(sv2_)
    sm_gr = {k: sm_g[k].reshape(sm_w[k].shape) for k in names}

    order = ["g_mix_pre", "w_in", "b_in", "w_pool", "pool_scale", "attn_sinks", "w_branch_pool", "w_branch_attn",
             "w_out", "g_mix_post", "g_mlp_pre", "w_up", "w_down", "g_mlp_post"]
    big_names = ["w_in", "w_branch_pool", "w_branch_attn", "w_out", "w_up", "w_down"]
    lead = lambda a: a[None]
    tables = []
    for small_t, big_t in ((sm_gr, big_g), (sm_d, big_d), (sm_m2, big_m2), (sm_v2, big_v2)):
        bt = dict(zip(big_names, big_t))
        tables.append([lead(bt[k]) if k in bt else small_t[k] for k in order])
    return (loss, gx.reshape(B, S, D_MODEL), *tables[0], *tables[1], *tables[2], *tables[3])
```

```python
import functools

import jax
import jax.numpy as jnp
from jax import lax
from jax.experimental import pallas as pl
from jax.experimental.pallas import tpu as pltpu

F32 = jnp.float32
MXU_DTYPE = jnp.bfloat16
MESH = pl.DeviceIdType.MESH

D_MODEL = 1024
POOL_WINDOWS = (2, 4, 8, 16)
POOL_WIDTH = 512
POOL_GC = 128
HEAD_DIM = 64
N_Q_HEADS = 8
N_KV_HEADS = 2
GROUP = 4
ATTN_WIDTH = 512
KV_WIDTH = 128
BLOCK = 128
GATE_WIDTH = 2048
IN_WIDTH = 3328
D_FF = 4096
EPS = 1e-6
NEG_INF = -1e30
ROPE_THETA = 500000.0
ROT_DIM = 16
SCALE = HEAD_DIM ** -0.5
C_Q, C_K, C_V, C_G = 512, 1024, 1152, 1280

ADAM_LR = 0.001
ADAM_B1 = 0.9
ADAM_B2 = 0.999
ADAM_EPS = 1e-08
ADAM_WD = 0.01
ADAM_STEP = 10

N_DEV = 8
LANES = 128
VMEM_LIMIT = 56 * 1024 * 1024

NN = (((1,), (0,)), ((), ()))
NT = (((1,), (1,)), ((), ()))
TN = (((0,), (0,)), ((), ()))


def _dot(a, b, dims):
    return lax.dot_general(a, b, dims, preferred_element_type=F32)


def _params(sem=None):
    return pltpu.CompilerParams(dimension_semantics=sem, vmem_limit_bytes=VMEM_LIMIT)


def _tile(n, pref):
    t = min(n, pref)
    assert n % t == 0, (n, t)
    return t


class _Rider:
    def __init__(self, ins, out_shape, n_remote, n_local, plan, aliases=None):
        self.ins, self.out_shape, self.n_remote, self.n_local = list(ins), list(out_shape), n_remote, n_local
        self.plan, self.aliases = plan, dict(aliases or {})


def _after(token, rider=None):
    r = rider or _Rider([], [], 0, 0, lambda ins, outs, send, recv, loc, r0, l0: ([], []))
    return _Rider(r.ins + [token], r.out_shape, r.n_remote, r.n_local, r.plan, r.aliases)


def _merge_riders(a, b):
    na_in, na_out = len(a.ins), len(a.out_shape)

    def plan(ins, outs, send, recv, loc, r0, l0):
        ra, la = a.plan(ins[:na_in], outs[:na_out], send, recv, loc, r0, l0)
        rb, lb = b.plan(ins[na_in:], outs[na_out:], send, recv, loc, r0 + a.n_remote, l0 + a.n_local)
        return ra + rb, la + lb

    aliases = dict(a.aliases)
    aliases.update({na_in + i: na_out + o for i, o in b.aliases.items()})
    return _Rider(a.ins + b.ins, a.out_shape + b.out_shape, a.n_remote + b.n_remote, a.n_local + b.n_local, plan, aliases)


def _launch(body, args, *, name, grid, in_specs, out_specs, out_shape, scratch_shapes=(), sem=None, rider=None):
    if rider is None:
        return pl.pallas_call(body, name=name, grid=grid, in_specs=in_specs, out_specs=out_specs, out_shape=out_shape,
                              scratch_shapes=list(scratch_shapes), compiler_params=_params(sem))(*args)
    n_in, n_out, n_scr = len(args), len(out_shape), len(scratch_shapes)
    r_in, r_out = len(rider.ins), len(rider.out_shape)
    copies = rider.n_remote + rider.n_local > 0

    def wrapped(*refs):
        ins, rins = refs[:n_in], refs[n_in:n_in + r_in]
        o0 = n_in + r_in
        outs, routs = refs[o0:o0 + n_out], refs[o0 + n_out:o0 + n_out + r_out]
        s0 = o0 + n_out + r_out
        scr = refs[s0:s0 + n_scr]
        if not copies:
            return body(*ins, *outs, *scr)
        send, recv, loc = refs[s0 + n_scr:]
        first, last = None, None
        for d in range(len(grid)):
            f, l = pl.program_id(d) == 0, pl.program_id(d) == pl.num_programs(d) - 1
            first = f if first is None else first & f
            last = l if last is None else last & l

        def start():
            remote, local = rider.plan(rins, routs, send, recv, loc, 0, 0)
            for cp in local + remote:
                cp.start()

        def finish():
            remote, local = rider.plan(rins, routs, send, recv, loc, 0, 0)
            for cp in remote + local:
                cp.wait()

        if first is None:
            start()
            body(*ins, *outs, *scr)
            finish()
        else:
            pl.when(first)(start)
            body(*ins, *outs, *scr)
            pl.when(last)(finish)

    hbm = pl.BlockSpec(memory_space=pl.ANY)
    dma = pltpu.SemaphoreType.DMA
    res = pl.pallas_call(
        wrapped, name=name, grid=grid, in_specs=list(in_specs) + [hbm] * r_in,
        out_specs=list(out_specs) + [hbm] * r_out, out_shape=list(out_shape) + rider.out_shape,
        scratch_shapes=list(scratch_shapes) + (
            [dma((max(rider.n_remote, 1),)), dma((max(rider.n_remote, 1),)), dma((max(rider.n_local, 1),))] if copies else []),
        input_output_aliases={n_in + i: n_out + o for i, o in rider.aliases.items()},
        compiler_params=_params(sem),
    )(*args, *rider.ins)
    return list(res[:n_out]), list(res[n_out:])


def _comm_call(rider, name):
    return _launch(lambda: None, [], name=name, grid=(), in_specs=[], out_specs=[], out_shape=[], rider=rider)[1]


_HBM = pl.BlockSpec(memory_space=pltpu.HBM)
_SEM = pl.BlockSpec(memory_space=pltpu.SEMAPHORE)
_EFFECT = pltpu.SideEffectType.DATAFLOW_SIDE_EFFECTING


def _copies_start(riders, name):
    assert all(r.n_local == 0 and not r.aliases for r in riders)
    sizes = [(len(r.ins), len(r.out_shape)) for r in riders]
    bufs = []
    for r in riders:
        bufs += [pltpu.with_memory_space_constraint(a, pltpu.HBM) for a in r.ins]
        bufs += [pltpu.with_memory_space_constraint(lax.empty(s.shape, s.dtype), pltpu.HBM) for s in r.out_shape]
    nb, ng = len(bufs), len(riders)

    def body(*refs):
        sems, token, at = refs[2 * nb:2 * nb + 2 * ng], refs[-1], 0
        for g, (r, (ni, no)) in enumerate(zip(riders, sizes)):
            remote, _ = r.plan(refs[at:at + ni], refs[at + ni:at + ni + no], sems[2 * g], sems[2 * g + 1], None, 0, 0)
            for cp in remote:
                cp.start()
            at += ni + no
        token[...] = jnp.zeros_like(token)

    res = pl.pallas_call(
        body, name=name, in_specs=[_HBM] * nb,
        out_specs=[_HBM] * nb + [_SEM] * (2 * ng) + [pl.BlockSpec(memory_space=pltpu.VMEM)],
        out_shape=[pltpu.HBM(a.shape, a.dtype) for a in bufs]
        + [pltpu.SemaphoreType.DMA((r.n_remote,)) for r in riders for _ in range(2)]
        + [jax.ShapeDtypeStruct((8, LANES), F32)],
        input_output_aliases={i: i for i in range(nb)},
        compiler_params=pltpu.CompilerParams(has_side_effects=_EFFECT),
    )(*bufs)
    handles, at = [], 0
    for g, (r, (ni, no)) in enumerate(zip(riders, sizes)):
        handles.append((r, list(res[at:at + ni + no]), res[nb + 2 * g], res[nb + 2 * g + 1]))
        at += ni + no
    return handles, res[-1]


def _copies_wait(handles, after, name):
    bufs = [b for _, bs, _, _ in handles for b in bs]
    sems = [s for _, _, send, recv in handles for s in (send, recv)]
    nb, ng = len(bufs), len(handles)

    def body(*refs):
        at = 0
        for g, (rider, bs, _, _) in enumerate(handles):
            ni = len(rider.ins)
            remote, _ = rider.plan(refs[at:at + ni], refs[at + ni:at + len(bs)], refs[nb + 2 * g], refs[nb + 2 * g + 1],
                                   None, 0, 0)
            for cp in remote:
                cp.wait_send()
                cp.wait_recv()
            at += len(bs)

    res = pl.pallas_call(
        body, name=name, in_specs=[_HBM] * nb + [_SEM] * (2 * ng) + [pl.BlockSpec(memory_space=pl.ANY)],
        out_specs=[_HBM] * nb, out_shape=[pltpu.HBM(a.shape, a.dtype) for a in bufs],
        input_output_aliases={i: i for i in range(nb)},
        compiler_params=pltpu.CompilerParams(has_side_effects=_EFFECT),
    )(*bufs, *sems, after)
    lands, at = [], 0
    for rider, bs, _, _ in handles:
        lands += list(res[at + len(rider.ins):at + len(bs)])
        at += len(bs)
    return lands


def _rms_r(x):
    return lax.rsqrt(jnp.mean(x * x, axis=-1, keepdims=True) + EPS)


def _rms_bwd(dn, x, r, g):
    xh = x * r
    dxh = dn * g
    dx = r * (dxh - xh * jnp.mean(dxh * xh, axis=-1, keepdims=True))
    return dx, dn * xh


def _rot(t, c, sa, sb):
    outs = []
    for j in range(t.shape[1] // LANES):
        tj = t[:, LANES * j:LANES * (j + 1)]
        outs.append(tj * c + pltpu.roll(tj, LANES - 8, 1) * sa + pltpu.roll(tj, 8, 1) * sb)
    return outs[0] if len(outs) == 1 else jnp.concatenate(outs, axis=1)


def _rot_tables(S):
    pos = jnp.arange(S, dtype=F32)
    inv_freq = ROPE_THETA ** (-jnp.arange(0, ROT_DIM, 2, dtype=F32) / ROT_DIM)
    ang = pos[:, None] * inv_freq[None, :]
    cos, sin = jnp.cos(ang), jnp.sin(ang)
    one = jnp.ones((S, HEAD_DIM - ROT_DIM), F32)
    zero = jnp.zeros((S, HEAD_DIM - ROT_DIM), F32)
    z8 = jnp.zeros((S, 8), F32)
    c = jnp.concatenate([cos, cos, one], axis=1)
    sa = jnp.concatenate([-sin, z8, zero], axis=1)
    sb = jnp.concatenate([z8, sin, zero], axis=1)
    rep = LANES // HEAD_DIM
    return jnp.tile(c, (1, rep)), jnp.tile(sa, (1, rep)), jnp.tile(sb, (1, rep))


def _lane_tile4(k):
    lane = lax.broadcasted_iota(jnp.int32, k.shape, 1)
    rk = pltpu.roll(k, HEAD_DIM, 1)
    x0 = jnp.where(lane < HEAD_DIM, k, rk)
    x1 = jnp.where(lane < HEAD_DIM, rk, k)
    return jnp.concatenate([x0, x0, x1, x1], axis=1)


def _fold_heads(acc):
    zs = []
    for hk in range(N_KV_HEADS):
        a = acc[:, 256 * hk:256 * hk + LANES] + acc[:, 256 * hk + LANES:256 * (hk + 1)]
        zs.append(a + pltpu.roll(a, HEAD_DIM, 1))
    lane = lax.broadcasted_iota(jnp.int32, zs[0].shape, 1)
    return jnp.where(lane < HEAD_DIM, zs[0], zs[1])


def _inproj_call(x, g1, win_t, b_in, rc, rsa, rsb, S, rider=None):
    T = x.shape[0]
    tm = _tile(S, 512)
    nst = S // tm

    def body(x_ref, g1_ref, w_ref, b_ref, c_ref, sa_ref, sb_ref,
             h_ref, u_ref, q_ref, k4_ref, v4_ref, g_ref):
        xv = x_ref[...]
        hb = ((xv * _rms_r(xv)) * g1_ref[...]).astype(MXU_DTYPE)
        h_ref[...] = hb

        def proj(lo, hi):
            return _dot(hb, w_ref[lo:hi, :], NT) + b_ref[:, lo:hi]

        c, sa, sb = c_ref[...], sa_ref[...], sb_ref[...]
        u_ref[...] = proj(0, C_Q)
        q_ref[...] = (_rot(proj(C_Q, C_K), c, sa, sb) * SCALE).astype(MXU_DTYPE)
        kv = proj(C_K, C_G)
        k4_ref[...] = _lane_tile4(_rot(kv[:, :KV_WIDTH], c, sa, sb)).astype(MXU_DTYPE)
        v4_ref[...] = _lane_tile4(kv[:, KV_WIDTH:]).astype(MXU_DTYPE)
        g_ref[...] = jax.nn.sigmoid(proj(C_G, IN_WIDTH)).astype(MXU_DTYPE)

    tok = lambda w: pl.BlockSpec((tm, w), lambda i: (i, 0))
    full = lambda a: pl.BlockSpec(a.shape, lambda i: (0,) * a.ndim)
    tab = pl.BlockSpec((tm, LANES), lambda i: (i % nst, 0))
    return _launch(
        body, [x, g1, win_t, b_in, rc, rsa, rsb], name="inproj_fwd", grid=(T // tm,),
        in_specs=[tok(D_MODEL), full(g1), full(win_t), full(b_in), tab, tab, tab],
        out_specs=[tok(D_MODEL), tok(POOL_WIDTH), tok(ATTN_WIDTH), tok(512), tok(512), tok(GATE_WIDTH)],
        out_shape=[jax.ShapeDtypeStruct((T, D_MODEL), MXU_DTYPE), jax.ShapeDtypeStruct((T, POOL_WIDTH), F32),
                   jax.ShapeDtypeStruct((T, ATTN_WIDTH), MXU_DTYPE), jax.ShapeDtypeStruct((T, 512), MXU_DTYPE),
                   jax.ShapeDtypeStruct((T, 512), MXU_DTYPE), jax.ShapeDtypeStruct((T, GATE_WIDTH), MXU_DTYPE)],
        sem=("arbitrary",), rider=rider)


def _shift_rows(a, k, rows):
    n = a.shape[0]
    if k > 0:
        return jnp.where(rows >= k, pltpu.roll(a, k, 0), 0.0)
    return jnp.where(rows < n + k, pltpu.roll(a, n + k, 0), 0.0)


def _win_sum(a, w, rows, sign):
    s, k = a, 1
    while k < w:
        s = s + _shift_rows(s, sign * k, rows)
        k *= 2
    return s


def _pool_diff(ug, w, rows):
    inv = 1.0 / jnp.minimum(rows + 1, w).astype(F32)
    return _win_sum(ug, w, rows, 1) * inv - ug, inv


def _pool_call(u, w_pool, pool_scale, S):
    T = u.shape[0]

    def body(u_ref, w_ref, ps_ref, y_ref):
        rows = lax.broadcasted_iota(jnp.int32, (S, POOL_GC), 0)
        for gi, w in enumerate(POOL_WINDOWS):
            sl = slice(POOL_GC * gi, POOL_GC * (gi + 1))
            diff, _ = _pool_diff(u_ref[:, sl], w, rows)
            mixed = _dot(diff.astype(MXU_DTYPE), w_ref[gi], NN)
            y_ref[:, sl] = (mixed * ps_ref[:, sl]).astype(MXU_DTYPE)

    seq = pl.BlockSpec((S, POOL_WIDTH), lambda b: (b, 0))
    return pl.pallas_call(
        body, name="pool_fwd", grid=(T // S,),
        in_specs=[seq, pl.BlockSpec(w_pool.shape, lambda b: (0, 0, 0)), pl.BlockSpec(pool_scale.shape, lambda b: (0, 0))],
        out_specs=seq, out_shape=jax.ShapeDtypeStruct((T, POOL_WIDTH), MXU_DTYPE),
        compiler_params=_params(("arbitrary",)),
    )(u, w_pool, pool_scale)


def _pool_bwd_call(u, dyp, w_pool, pool_scale, S, rider=None):
    T = u.shape[0]

    def body(u_ref, dy_ref, w_ref, ps_ref, du_ref, dw_ref, dps_ref):
        @pl.when(pl.program_id(0) == 0)
        def _():
            dw_ref[...] = jnp.zeros_like(dw_ref)
            dps_ref[...] = jnp.zeros_like(dps_ref)

        rows = lax.broadcasted_iota(jnp.int32, (S, POOL_GC), 0)
        for gi, w in enumerate(POOL_WINDOWS):
            sl = slice(POOL_GC * gi, POOL_GC * (gi + 1))
            diff, inv = _pool_diff(u_ref[:, sl], w, rows)
            diffb = diff.astype(MXU_DTYPE)
            wg = w_ref[gi]
            mixed = _dot(diffb, wg, NN)
            dy = dy_ref[:, sl]
            dps_ref[:, sl] += jnp.sum(dy * mixed, axis=0, keepdims=True)
            dmb = (dy * ps_ref[:, sl]).astype(MXU_DTYPE)
            dw_ref[gi] += _dot(diffb, dmb, TN)
            ddiff = _dot(dmb, wg, NT)
            du_ref[:, sl] = (_win_sum(ddiff * inv, w, rows, -1) - ddiff).astype(MXU_DTYPE)

    seq = pl.BlockSpec((S, POOL_WIDTH), lambda b: (b, 0))
    return _launch(
        body, [u, dyp, w_pool, pool_scale], name="pool_bwd", grid=(T // S,),
        in_specs=[seq, seq, pl.BlockSpec(w_pool.shape, lambda b: (0, 0, 0)), pl.BlockSpec(pool_scale.shape, lambda b: (0, 0))],
        out_specs=[seq, pl.BlockSpec(w_pool.shape, lambda b: (0, 0, 0)), pl.BlockSpec(pool_scale.shape, lambda b: (0, 0))],
        out_shape=[jax.ShapeDtypeStruct((T, POOL_WIDTH), MXU_DTYPE), jax.ShapeDtypeStruct(w_pool.shape, F32),
                   jax.ShapeDtypeStruct(pool_scale.shape, F32)],
        sem=("arbitrary",), rider=rider)


def _attn_consts():
    lane_g = lax.broadcasted_iota(jnp.int32, (BLOCK, 256), 1) >> 6
    rgrp = lax.broadcasted_iota(jnp.int32, (GROUP * BLOCK, 1), 0) >> 7
    rel = lax.broadcasted_iota(jnp.int32, (BLOCK, 256), 0) - lax.broadcasted_iota(jnp.int32, (BLOCK, 256), 1)

    def bias(off):
        ok = (rel + off >= 0) & (rel + off < BLOCK)
        return jnp.concatenate([jnp.where(ok, 0.0, NEG_INF)] * GROUP, axis=0)

    return lane_g, rgrp, bias(0), bias(BLOCK)


def _sink_rows(sink_ref, hk, rgrp):
    sv = jnp.zeros(rgrp.shape, F32)
    for g in range(GROUP):
        sv = jnp.where(rgrp == g, sink_ref[0, GROUP * hk + g], sv)
    return sv


def _stack_heads(xb, lane_g):
    return jnp.concatenate([jnp.where(lane_g == g, xb, jnp.zeros_like(xb)) for g in range(GROUP)], axis=0)


def _unstack_heads(xs, lane_g):
    out = jnp.where(lane_g == 0, xs[0:BLOCK], 0.0)
    for g in range(1, GROUP):
        out = out + jnp.where(lane_g == g, xs[BLOCK * g:BLOCK * (g + 1)], 0.0)
    return out


def _attn_probs(qs, kb, bias, sv):
    s = _dot(qs, kb, NT) + bias
    m = jnp.maximum(jnp.max(s, axis=1, keepdims=True), sv)
    e = jnp.exp(s - m)
    es = jnp.exp(sv - m)
    inv_l = 1.0 / (jnp.sum(e, axis=1, keepdims=True) + es)
    return e * inv_l, es * inv_l


def _attn_blocks(nb, blk, carry):
    carry = blk(0, 0, True, carry)
    return lax.fori_loop(1, nb, lambda n, c: blk(pl.multiple_of(n * BLOCK, BLOCK),
                                                 pl.multiple_of((n - 1) * BLOCK, BLOCK), False, c), carry)


def _attn_call(sinks, q, k4, v4, S, rider=None):
    T = q.shape[0]
    nb = S // BLOCK

    def body(sink_ref, q_ref, k_ref, v_ref, o_ref):
        lane_g, rgrp, bias_first, bias_later = _attn_consts()
        svs = [_sink_rows(sink_ref, hk, rgrp) for hk in range(N_KV_HEADS)]

        def blk(q0, k0, first, carry):
            for hk in range(N_KV_HEADS):
                cs = slice(256 * hk, 256 * (hk + 1))
                qs = _stack_heads(q_ref[pl.ds(q0, BLOCK), cs], lane_g)
                p, _ = _attn_probs(qs, k_ref[pl.ds(k0, 2 * BLOCK), cs], bias_first if first else bias_later, svs[hk])
                o = _dot(p.astype(MXU_DTYPE), v_ref[pl.ds(k0, 2 * BLOCK), cs], NN)
                o_ref[pl.ds(q0, BLOCK), cs] = _unstack_heads(o, lane_g).astype(MXU_DTYPE)
            return carry

        _attn_blocks(nb, blk, 0)

    seq = pl.BlockSpec((S, ATTN_WIDTH), lambda b: (b, 0))
    return _launch(
        body, [sinks, q, k4, v4], name="attn_fwd", grid=(T // S,),
        in_specs=[pl.BlockSpec(memory_space=pltpu.SMEM), seq, seq, seq],
        out_specs=[seq], out_shape=[jax.ShapeDtypeStruct((T, ATTN_WIDTH), MXU_DTYPE)],
        sem=("arbitrary",), rider=rider)


def _attn_bwd_call(sinks, q, k4, v4, do, rc, rsa, rsb, S, rider=None):
    T = q.shape[0]
    nb = S // BLOCK

    def body(sink_ref, q_ref, k_ref, v_ref, do_ref, c_ref, sa_ref, sb_ref,
             dq_ref, dk_ref, dv_ref, ds_ref, dk_acc, dv_acc):
        lane_g, rgrp, bias_first, bias_later = _attn_consts()
        svs = [_sink_rows(sink_ref, hk, rgrp) for hk in range(N_KV_HEADS)]
        lane1 = lax.broadcasted_iota(jnp.int32, (1, LANES), 1)
        dk_acc[...] = jnp.zeros_like(dk_acc)
        dv_acc[...] = jnp.zeros_like(dv_acc)

        def blk(q0, k0, first, dsink):
            rows = pl.ds(q0, BLOCK)
            c, sa, sb = c_ref[rows, :], sa_ref[rows, :], sb_ref[rows, :]
            for hk in range(N_KV_HEADS):
                cs = slice(256 * hk, 256 * (hk + 1))
                qs = _stack_heads(q_ref[rows, cs], lane_g)
                dos = _stack_heads(do_ref[rows, cs], lane_g)
                kb = k_ref[pl.ds(k0, 2 * BLOCK), cs]
                vb = v_ref[pl.ds(k0, 2 * BLOCK), cs]
                p, ps = _attn_probs(qs, kb, bias_first if first else bias_later, svs[hk])
                dp = _dot(dos, vb, NT)
                delta = jnp.sum(p * dp, axis=1, keepdims=True)
                dsb = (p * (dp - delta)).astype(MXU_DTYPE)
                dqb = _unstack_heads(_dot(dsb, kb, NN), lane_g) * SCALE
                dq_ref[rows, cs] = _rot(dqb, c, -sa, -sb).astype(MXU_DTYPE)
                dk_acc[pl.ds(k0, 2 * BLOCK), cs] += _dot(dsb, qs, TN)
                dv_acc[pl.ds(k0, 2 * BLOCK), cs] += _dot(p.astype(MXU_DTYPE), dos, TN)
                psd = ps * delta
                for g in range(GROUP):
                    val = -jnp.sum(psd[BLOCK * g:BLOCK * (g + 1)], axis=0, keepdims=True)
                    dsink = dsink + jnp.where(lane1 == GROUP * hk + g, val, 0.0)
            return dsink

        dsink = _attn_blocks(nb, blk, jnp.zeros((1, LANES), F32))
        dk_ref[...] = _rot(_fold_heads(dk_acc[...]), c_ref[...], -sa_ref[...], -sb_ref[...]).astype(MXU_DTYPE)
        dv_ref[...] = _fold_heads(dv_acc[...]).astype(MXU_DTYPE)
        ds_ref[...] = jnp.broadcast_to(dsink, ds_ref.shape)

    seq = pl.BlockSpec((S, ATTN_WIDTH), lambda b: (b, 0))
    kvs = pl.BlockSpec((S, KV_WIDTH), lambda b: (b, 0))
    tab = pl.BlockSpec((S, LANES), lambda b: (0, 0))
    nseq = T // S
    return _launch(
        body, [sinks, q, k4, v4, do, rc, rsa, rsb], name="attn_bwd", grid=(nseq,),
        in_specs=[pl.BlockSpec(memory_space=pltpu.SMEM), seq, seq, seq, seq, tab, tab, tab],
        out_specs=[seq, kvs, kvs, pl.BlockSpec((8, LANES), lambda b: (b, 0))],
        out_shape=[jax.ShapeDtypeStruct((T, ATTN_WIDTH), MXU_DTYPE), jax.ShapeDtypeStruct((T, KV_WIDTH), MXU_DTYPE),
                   jax.ShapeDtypeStruct((T, KV_WIDTH), MXU_DTYPE), jax.ShapeDtypeStruct((8 * nseq, LANES), F32)],
        scratch_shapes=[pltpu.VMEM((S, 512), F32), pltpu.VMEM((S, 512), F32)],
        sem=("arbitrary",), rider=rider)


def _branch_weights(wbp_ref, wba_ref, wbp_s, wba_s):
    @pl.when(pl.program_id(0) == 0)
    def _():
        for j in range(N_DEV):
            wbp_s[:, LANES * j:LANES * (j + 1)] = wbp_ref[j]
            wba_s[:, LANES * j:LANES * (j + 1)] = wba_ref[j]


def _mix_fwd_call(yp, ya, g, x, wbp, wba, wout, g2, g3, rider=None):
    T = x.shape[0]
    tm = _tile(T, 512)

    def body(yp_ref, ya_ref, g_ref, x_ref, wbp_ref, wba_ref, wout_ref, g2_ref, g3_ref,
             mix_ref, x1_ref, h2_ref, wbp_s, wba_s):
        _branch_weights(wbp_ref, wba_ref, wbp_s, wba_s)
        bp = _dot(yp_ref[...], wbp_s[...], NN)
        ba = _dot(ya_ref[...], wba_s[...], NN)
        merged = g_ref[:, :D_MODEL].astype(F32) * bp + g_ref[:, D_MODEL:].astype(F32) * ba
        mix = _dot(merged.astype(MXU_DTYPE), wout_ref[...], NN)
        mix_ref[...] = mix
        x1 = x_ref[...] + (mix * _rms_r(mix)) * g2_ref[...]
        x1_ref[...] = x1
        h2_ref[...] = ((x1 * _rms_r(x1)) * g3_ref[...]).astype(MXU_DTYPE)

    tok = lambda w: pl.BlockSpec((tm, w), lambda i: (i, 0))
    full = lambda a: pl.BlockSpec(a.shape, lambda i: (0,) * a.ndim)
    return _launch(
        body, [yp, ya, g, x, wbp, wba, wout, g2, g3], name="mix_fwd", grid=(T // tm,),
        in_specs=[tok(POOL_WIDTH), tok(ATTN_WIDTH), tok(GATE_WIDTH), tok(D_MODEL), full(wbp), full(wba), full(wout),
                  full(g2), full(g3)],
        out_specs=[tok(D_MODEL), tok(D_MODEL), tok(D_MODEL)],
        out_shape=[jax.ShapeDtypeStruct((T, D_MODEL), F32), jax.ShapeDtypeStruct((T, D_MODEL), F32),
                   jax.ShapeDtypeStruct((T, D_MODEL), MXU_DTYPE)],
        scratch_shapes=[pltpu.VMEM((POOL_WIDTH, D_MODEL), MXU_DTYPE), pltpu.VMEM((ATTN_WIDTH, D_MODEL), MXU_DTYPE)],
        sem=("arbitrary",), rider=rider)


def _mix_bwd_call(dx1, mix, yp, ya, g, wbp, wba, wout, g2, rider=None):
    T = dx1.shape[0]
    tm = _tile(T, 512)

    def body(dx1_ref, mix_ref, yp_ref, ya_ref, g_ref, wbp_ref, wba_ref, wout_ref, g2_ref,
             dmix_ref, merged_ref, dbp_ref, dba_ref, dyp_ref, do_ref, dgates_ref, dg2_ref, dbg_ref, wbp_s, wba_s):
        _branch_weights(wbp_ref, wba_ref, wbp_s, wba_s)

        @pl.when(pl.program_id(0) == 0)
        def _():
            dg2_ref[...] = jnp.zeros_like(dg2_ref)
            dbg_ref[...] = jnp.zeros_like(dbg_ref)

        mix = mix_ref[...]
        dmix, dg2 = _rms_bwd(dx1_ref[...], mix, _rms_r(mix), g2_ref[...])
        dg2_ref[...] += jnp.sum(dg2, axis=0, keepdims=True)
        dmixb = dmix.astype(MXU_DTYPE)
        dmix_ref[...] = dmixb
        dmerged = _dot(dmixb, wout_ref[...], NT)
        bp = _dot(yp_ref[...], wbp_s[...], NN)
        ba = _dot(ya_ref[...], wba_s[...], NN)
        gp, ga = g_ref[:, :D_MODEL].astype(F32), g_ref[:, D_MODEL:].astype(F32)
        merged_ref[...] = (gp * bp + ga * ba).astype(MXU_DTYPE)
        dgp = dmerged * bp * (gp * (1.0 - gp))
        dga = dmerged * ba * (ga * (1.0 - ga))
        dbg_ref[:, :D_MODEL] += jnp.sum(dgp, axis=0, keepdims=True)
        dbg_ref[:, D_MODEL:] += jnp.sum(dga, axis=0, keepdims=True)
        dgates_ref[:, :D_MODEL] = dgp.astype(MXU_DTYPE)
        dgates_ref[:, D_MODEL:] = dga.astype(MXU_DTYPE)
        dbp = (dmerged * gp).astype(MXU_DTYPE)
        dba = (dmerged * ga).astype(MXU_DTYPE)
        dbp_ref[...] = dbp
        dba_ref[...] = dba
        dyp_ref[...] = _dot(dbp, wbp_s[...], NT)
        do_ref[...] = _dot(dba, wba_s[...], NT).astype(MXU_DTYPE)

    tok = lambda w: pl.BlockSpec((tm, w), lambda i: (i, 0))
    full = lambda a: pl.BlockSpec(a.shape, lambda i: (0,) * a.ndim)
    acc = lambda w: pl.BlockSpec((1, w), lambda i: (0, 0))
    sd = jax.ShapeDtypeStruct
    return _launch(
        body, [dx1, mix, yp, ya, g, wbp, wba, wout, g2], name="mix_bwd", grid=(T // tm,),
        in_specs=[tok(D_MODEL), tok(D_MODEL), tok(POOL_WIDTH), tok(ATTN_WIDTH), tok(GATE_WIDTH), full(wbp), full(wba),
                  full(wout), full(g2)],
        out_specs=[tok(D_MODEL), tok(D_MODEL), tok(D_MODEL), tok(D_MODEL), tok(POOL_WIDTH), tok(ATTN_WIDTH),
                   tok(GATE_WIDTH), acc(D_MODEL), acc(GATE_WIDTH)],
        out_shape=[sd((T, D_MODEL), MXU_DTYPE), sd((T, D_MODEL), MXU_DTYPE), sd((T, D_MODEL), MXU_DTYPE),
                   sd((T, D_MODEL), MXU_DTYPE), sd((T, POOL_WIDTH), F32), sd((T, ATTN_WIDTH), MXU_DTYPE),
                   sd((T, GATE_WIDTH), MXU_DTYPE), sd((1, D_MODEL), F32), sd((1, GATE_WIDTH), F32)],
        scratch_shapes=[pltpu.VMEM((POOL_WIDTH, D_MODEL), MXU_DTYPE), pltpu.VMEM((ATTN_WIDTH, D_MODEL), MXU_DTYPE)],
        sem=("arbitrary",), rider=rider)


def _mlp_call(x1, h2, target, wup, wdown, g3, g4):
    T = x1.shape[0]
    tm = _tile(T, 256)
    fc = D_FF // N_DEV

    def body(x1_ref, h2_ref, t_ref, wup_ref, wdown_ref, g3_ref, g4_ref,
             act_ref, da_ref, dff_ref, dx1_ref, dg3_ref, dg4_ref, loss_ref, rl_s):
        @pl.when(pl.program_id(0) == 0)
        def _():
            dg3_ref[...] = jnp.zeros_like(dg3_ref)
            dg4_ref[...] = jnp.zeros_like(dg4_ref)
            loss_ref[...] = jnp.zeros_like(loss_ref)

        h2 = h2_ref[...]
        ff = jnp.zeros((tm, D_MODEL), F32)
        for j in range(N_DEV):
            sl = slice(fc * j, fc * (j + 1))
            rl = jnp.maximum(_dot(h2, wup_ref[j], NN), 0.0)
            rl_s[:, sl] = rl
            actb = (rl * rl).astype(MXU_DTYPE)
            act_ref[:, sl] = actb
            ff = ff + _dot(actb, wdown_ref[j], NN)
        x1 = x1_ref[...]
        r4 = _rms_r(ff)
        err = x1 + (ff * r4) * g4_ref[...] - t_ref[...]
        loss_ref[...] += jnp.sum(err * err, axis=0, keepdims=True)
        dy = err * (1.0 / D_MODEL)
        dff, dg4 = _rms_bwd(dy, ff, r4, g4_ref[...])
        dg4_ref[...] += jnp.sum(dg4, axis=0, keepdims=True)
        dffb = dff.astype(MXU_DTYPE)
        dff_ref[...] = dffb
        dh2 = jnp.zeros((tm, D_MODEL), F32)
        for j in range(N_DEV):
            sl = slice(fc * j, fc * (j + 1))
            dab = (_dot(dffb, wdown_ref[j], NT) * (2.0 * rl_s[:, sl])).astype(MXU_DTYPE)
            da_ref[:, sl] = dab
            dh2 = dh2 + _dot(dab, wup_ref[j], NT)
        dx1, dg3 = _rms_bwd(dh2, x1, _rms_r(x1), g3_ref[...])
        dg3_ref[...] += jnp.sum(dg3, axis=0, keepdims=True)
        dx1_ref[...] = dy + dx1

    tok = lambda w: pl.BlockSpec((tm, w), lambda i: (i, 0))
    full = lambda a: pl.BlockSpec(a.shape, lambda i: (0,) * a.ndim, pipeline_mode=pl.Buffered(1))
    vec = pl.BlockSpec((1, D_MODEL), lambda i: (0, 0))
    sd = jax.ShapeDtypeStruct
    return pl.pallas_call(
        body, name="mlp_fwd_bwd", grid=(T // tm,),
        in_specs=[tok(D_MODEL), tok(D_MODEL), tok(D_MODEL), full(wup), full(wdown), vec, vec],
        out_specs=[tok(D_FF), tok(D_FF), tok(D_MODEL), tok(D_MODEL), vec, vec, vec],
        out_shape=[sd((T, D_FF), MXU_DTYPE), sd((T, D_FF), MXU_DTYPE), sd((T, D_MODEL), MXU_DTYPE),
                   sd((T, D_MODEL), F32), sd((1, D_MODEL), F32), sd((1, D_MODEL), F32), sd((1, D_MODEL), F32)],
        scratch_shapes=[pltpu.VMEM((tm, D_FF), F32)],
        compiler_params=_params(("arbitrary",)),
    )(x1, h2, target, wup, wdown, g3, g4)


def _inproj_bwd_call(du, dq, dk, dv, dgates, dx1, x, win_t, g1, rider=None):
    T = x.shape[0]
    tm = _tile(T, 512)

    def body(du_ref, dq_ref, dk_ref, dv_ref, dgt_ref, dx1_ref, x_ref, w_ref, g1_ref, gx_ref, dg1_ref, db_ref):
        @pl.when(pl.program_id(0) == 0)
        def _():
            dg1_ref[...] = jnp.zeros_like(dg1_ref)
            db_ref[...] = jnp.zeros_like(db_ref)

        dh = jnp.zeros((tm, D_MODEL), F32)
        for ref, lo, hi in ((du_ref, 0, C_Q), (dq_ref, C_Q, C_K), (dk_ref, C_K, C_V), (dv_ref, C_V, C_G),
                            (dgt_ref, C_G, IN_WIDTH)):
            piece = ref[...]
            dh = dh + _dot(piece, w_ref[lo:hi, :], NN)
            if hi <= C_G:
                db_ref[:, lo:hi] += jnp.sum(piece.astype(F32), axis=0, keepdims=True)
        xv = x_ref[...]
        dx, dg1 = _rms_bwd(dh, xv, _rms_r(xv), g1_ref[...])
        dg1_ref[...] += jnp.sum(dg1, axis=0, keepdims=True)
        gx_ref[...] = dx1_ref[...] + dx

    tok = lambda w: pl.BlockSpec((tm, w), lambda i: (i, 0))
    full = lambda a: pl.BlockSpec(a.shape, lambda i: (0,) * a.ndim)
    sd = jax.ShapeDtypeStruct
    return _launch(
        body, [du, dq, dk, dv, dgates, dx1, x, win_t, g1], name="inproj_bwd", grid=(T // tm,),
        in_specs=[tok(POOL_WIDTH), tok(ATTN_WIDTH), tok(KV_WIDTH), tok(KV_WIDTH), tok(GATE_WIDTH), tok(D_MODEL),
                  tok(D_MODEL), full(win_t), full(g1)],
        out_specs=[tok(D_MODEL), pl.BlockSpec((1, D_MODEL), lambda i: (0, 0)), pl.BlockSpec((1, C_G), lambda i: (0, 0))],
        out_shape=[sd((T, D_MODEL), F32), sd((1, D_MODEL), F32), sd((1, C_G), F32)],
        sem=("arbitrary",), rider=rider)


WGRAD_TOKENS = 1024


def _wgrad_rows_call(a, b, name, rider=None):
    T, K = a.shape
    N = b.shape[1]
    tm = _tile(T, WGRAD_TOKENS)
    kb = min(K, 1024)
    per = kb // (K // N_DEV)

    def body(a_ref, b_ref, o_ref):
        @pl.when(pl.program_id(1) == 0)
        def _():
            o_ref[...] = jnp.zeros_like(o_ref)

        d = _dot(a_ref[...], b_ref[...], TN)
        rs = kb // per
        for j in range(per):
            o_ref[j] += d[rs * j:rs * (j + 1)]

    return _launch(
        body, [a, b], name=name, grid=(K // kb, T // tm),
        in_specs=[pl.BlockSpec((tm, kb), lambda i, t: (t, i)), pl.BlockSpec((tm, N), lambda i, t: (t, 0))],
        out_specs=[pl.BlockSpec((per, K // N_DEV, N), lambda i, t: (i, 0, 0))],
        out_shape=[jax.ShapeDtypeStruct((N_DEV, K // N_DEV, N), F32)],
        sem=("arbitrary", "arbitrary"), rider=rider)


def _wgrad_cols_call(pairs, name, rider=None):
    T, K = pairs[0][0].shape
    N = pairs[0][1].shape[1]
    assert all(a.shape == (T, K) and b.shape == (T, N) for a, b in pairs)
    n = len(pairs)
    tm = _tile(T, WGRAD_TOKENS)
    nb = min(N, 1024)
    per = nb // (N // N_DEV)

    def body(*refs):
        for t in range(n):
            a_ref, b_ref, o_ref = refs[2 * t], refs[2 * t + 1], refs[2 * n + t]

            @pl.when(pl.program_id(1) == 0)
            def _():
                o_ref[...] = jnp.zeros_like(o_ref)

            d = _dot(a_ref[...], b_ref[...], TN)
            cs = nb // per
            for j in range(per):
                o_ref[j] += d[:, cs * j:cs * (j + 1)]

    return _launch(
        body, [m for ab in pairs for m in ab], name=name, grid=(N // nb, T // tm),
        in_specs=[pl.BlockSpec((tm, K), lambda i, t: (t, 0)), pl.BlockSpec((tm, nb), lambda i, t: (t, i))] * n,
        out_specs=[pl.BlockSpec((per, K, N // N_DEV), lambda i, t: (i, 0, 0))] * n,
        out_shape=[jax.ShapeDtypeStruct((N_DEV, K, N // N_DEV), F32)] * n,
        sem=("arbitrary", "arbitrary"), rider=rider)


def _wgrad_in_call(du, dq, dk, dv, dgates, h, rider=None):
    T = h.shape[0]
    tm = _tile(T, WGRAD_TOKENS)
    rows = IN_WIDTH // N_DEV

    def body(du_ref, dq_ref, dk_ref, dv_ref, dgt_ref, h_ref, o_ref, acc, sem):
        t = pl.program_id(0)

        @pl.when(t == 0)
        def _():
            acc[...] = jnp.zeros_like(acc)

        hv = h_ref[...]
        for ref, lo, hi in ((du_ref, 0, C_Q), (dq_ref, C_Q, C_K), (dk_ref, C_K, C_V), (dv_ref, C_V, C_G),
                            (dgt_ref, C_G, IN_WIDTH)):
            acc[lo:hi, :] += _dot(ref[...], hv, TN)

        @pl.when(t == pl.num_programs(0) - 1)
        def _():
            copies = [pltpu.make_async_copy(acc.at[pl.ds(rows * j, rows), :], o_ref.at[j], sem.at[j])
                      for j in range(N_DEV)]
            for cp in copies:
                cp.start()
            for cp in copies:
                cp.wait()

    tok = lambda w: pl.BlockSpec((tm, w), lambda t: (t, 0))
    return _launch(
        body, [du, dq, dk, dv, dgates, h], name="wgrad_in", grid=(T // tm,),
        in_specs=[tok(POOL_WIDTH), tok(ATTN_WIDTH), tok(KV_WIDTH), tok(KV_WIDTH), tok(GATE_WIDTH), tok(D_MODEL)],
        out_specs=[pl.BlockSpec(memory_space=pl.ANY)],
        out_shape=[jax.ShapeDtypeStruct((N_DEV, rows, D_MODEL), F32)],
        scratch_shapes=[pltpu.VMEM((IN_WIDTH, D_MODEL), F32), pltpu.SemaphoreType.DMA((N_DEV,))],
        sem=("arbitrary",), rider=rider)


def _coords():
    return lax.axis_index("x"), lax.axis_index("y"), lax.axis_index("c")


def _allgather_call(shards):
    n = len(shards)

    def body(*refs):
        ins, outs = refs[:n], refs[n:2 * n]
        send_sems, recv_sems, local_sems = refs[2 * n:]
        x, y, c = _coords()
        me, sibling = (x, y, c), (x, y, 1 - c)
        chips = [(1 - x, y), (x, 1 - y), (1 - x, 1 - y)]

        def slot(p):
            return 4 * p[0] + 2 * p[1] + p[2]

        def copy(t, k, block, to, src=None):
            dst = outs[t].at[slot(block)]
            return pltpu.make_async_remote_copy(
                src_ref=dst if src is None else src, dst_ref=dst, send_sem=send_sems.at[t, k],
                recv_sem=recv_sems.at[t, k], device_id=to, device_id_type=MESH)

        mine = [pltpu.make_async_copy(ins[t], outs[t].at[slot(me)], local_sems.at[t]) for t in range(n)]
        for cp in mine:
            cp.start()
        first = []
        for t in range(n):
            first.append(copy(t, 0, me, sibling, src=ins[t]))
            first += [copy(t, 1 + j, me, (*chip, c), src=ins[t]) for j, chip in enumerate(chips)]
        for cp in first:
            cp.start()
        passed = []
        for t in range(n):
            for j, chip in enumerate(chips):
                copy(t, 1 + j, (*chip, c), me).wait_recv()
                fwd = copy(t, 4 + j, (*chip, c), sibling)
                fwd.start()
                passed.append(fwd)
        for t in range(n):
            copy(t, 0, sibling, me).wait_recv()
            for j, chip in enumerate(chips):
                copy(t, 4 + j, (*chip, 1 - c), me).wait_recv()
        for cp in first + passed:
            cp.wait_send()
        for cp in mine:
            cp.wait()

    hbm = pl.BlockSpec(memory_space=pl.ANY)
    return pl.pallas_call(
        body, name="allgather_weights",
        in_specs=[hbm] * n, out_specs=[hbm] * n,
        out_shape=[jax.ShapeDtypeStruct((N_DEV,) + s.shape, s.dtype) for s in shards],
        scratch_shapes=[pltpu.SemaphoreType.DMA((n, 7)), pltpu.SemaphoreType.DMA((n, 7)), pltpu.SemaphoreType.DMA((n,))],
    )(*shards)


def _slot(p):
    return 4 * p[0] + 2 * p[1] + p[2]


def _rows(ref, span):
    return ref if span is None else ref.at[pl.ds(span[0], span[1])]


ALL = "all"
LOCAL = "local"


def _rows(ref, span):
    return ref if span == ALL else ref.at[pl.ds(span[0], span[1])]


def _rider_ag(items):
    ins, out_shape, aliases, where = [], [], {}, []
    n_remote = n_local = 0
    for t, (shard, buf, snd, fwd) in enumerate(items):
        i_shard = i_buf = None
        if snd is not None:
            i_shard = len(ins)
            ins.append(shard)
        if buf is not None:
            i_buf = len(ins)
            ins.append(buf)
            aliases[i_buf] = t
            out_shape.append(jax.ShapeDtypeStruct(buf.shape, buf.dtype))
        else:
            assert fwd is None and snd is not None
            out_shape.append(jax.ShapeDtypeStruct((N_DEV,) + shard.shape, shard.dtype))
        where.append((i_shard, i_buf, n_remote, n_local))
        n_remote += (4 if snd not in (None, LOCAL) else 0) + (3 if fwd is not None else 0)
        n_local += 1 if snd is not None else 0

    def plan(rins, routs, send, recv, loc, r0, l0):
        x, y, c = _coords()
        peers = [(x, y, 1 - c), (1 - x, y, c), (x, 1 - y, c), (1 - x, 1 - y, c)]
        remote, local = [], []
        for t, (shard, buf, snd, fwd) in enumerate(items):
            i_shard, i_buf, k, l = where[t]
            k, l = r0 + k, l0 + l
            if snd is not None:
                span = ALL if snd == LOCAL else snd
                src, dst = _rows(rins[i_shard], span), _rows(routs[t].at[_slot((x, y, c))], span)
                local.append(pltpu.make_async_copy(src, dst, loc.at[l]))
                for peer in (peers if snd != LOCAL else []):
                    remote.append(pltpu.make_async_remote_copy(
                        src_ref=src, dst_ref=dst, send_sem=send.at[k], recv_sem=recv.at[k],
                        device_id=peer, device_id_type=MESH))
                    k += 1
            if fwd is not None:
                for px, py, pc in peers[1:]:
                    s = _slot((px, py, pc))
                    remote.append(pltpu.make_async_remote_copy(
                        src_ref=_rows(rins[i_buf].at[s], fwd), dst_ref=_rows(routs[t].at[s], fwd),
                        send_sem=send.at[k], recv_sem=recv.at[k], device_id=peers[0], device_id_type=MESH))
                    k += 1
        return remote, local

    return _Rider(ins, out_shape, n_remote, n_local, plan, aliases)


def _rider_ag_remote(shards):
    n = len(shards)

    def plan(ins, outs, send, recv, loc, r0, l0):
        x, y, c = _coords()
        remote = []
        for t in range(n):
            dst = outs[t].at[_slot((x, y, c))]
            for k, peer in enumerate([(x, y, 1 - c), (1 - x, y, c), (x, 1 - y, c), (1 - x, 1 - y, c)]):
                remote.append(pltpu.make_async_remote_copy(
                    src_ref=ins[t], dst_ref=dst, send_sem=send.at[r0 + 4 * t + k], recv_sem=recv.at[r0 + 4 * t + k],
                    device_id=peer, device_id_type=MESH))
        return remote, []

    return _Rider(shards, [jax.ShapeDtypeStruct((N_DEV,) + s.shape, s.dtype) for s in shards], 4 * n, 0, plan)


def _rider_rs_sibling(grads):
    n = len(grads)

    def plan(ins, outs, send, recv, loc, r0, l0):
        x, y, c = _coords()
        remote = []
        for t in range(n):
            for q in range(4):
                remote.append(pltpu.make_async_remote_copy(
                    src_ref=ins[t].at[q, 1 - c], dst_ref=outs[t].at[q], send_sem=send.at[r0 + 4 * t + q],
                    recv_sem=recv.at[r0 + 4 * t + q], device_id=(x, y, 1 - c), device_id_type=MESH))
        return remote, []

    return _Rider(grads, [jax.ShapeDtypeStruct((4,) + g.shape[2:], g.dtype) for g in grads], 4 * n, 0, plan)


def _rider_rs_chips(sums, rows=None, into=None):
    n = len(sums)
    rows = rows or [ALL] * n

    def plan(ins, outs, send, recv, loc, r0, l0):
        x, y, c = _coords()
        remote = []
        for t in range(n):
            for r, (px, py) in enumerate([(1 - x, y), (x, 1 - y), (1 - x, 1 - y)]):
                remote.append(pltpu.make_async_remote_copy(
                    src_ref=_rows(ins[t].at[2 * px + py], rows[t]), dst_ref=_rows(outs[t].at[r], rows[t]),
                    send_sem=send.at[r0 + 3 * t + r], recv_sem=recv.at[r0 + 3 * t + r],
                    device_id=(px, py, c), device_id_type=MESH))
        return remote, []

    out_shape = [jax.ShapeDtypeStruct((3,) + s.shape[1:], s.dtype) for s in sums]
    if into is None:
        return _Rider(sums, out_shape, 3 * n, 0, plan)
    return _Rider(list(sums) + list(into), out_shape, 3 * n, 0, plan, aliases={n + t: t for t in range(n)})


def _rider_gather_direct(parts):
    n = len(parts)

    def plan(ins, outs, send, recv, loc, r0, l0):
        x, y, c = _coords()
        me = _slot((x, y, c))
        remote, local = [], []
        for t in range(n):
            local.append(pltpu.make_async_copy(ins[t], outs[t].at[me], loc.at[l0 + t]))
            for k in range(1, N_DEV):
                peer = (x ^ ((k >> 2) & 1), y ^ ((k >> 1) & 1), c ^ (k & 1))
                remote.append(pltpu.make_async_remote_copy(
                    src_ref=ins[t], dst_ref=outs[t].at[me], send_sem=send.at[r0 + 7 * t + k - 1],
                    recv_sem=recv.at[r0 + 7 * t + k - 1], device_id=peer, device_id_type=MESH))
        return remote, local

    return _Rider(parts, [jax.ShapeDtypeStruct((N_DEV,) + p.shape, p.dtype) for p in parts], 7 * n, n, plan)


def _chip_sum_call(cidx, grads, recvd, out_dtypes, name):
    n = len(grads)

    def body(c_ref, *refs):
        for t in range(n):
            refs[2 * n + t][0] = (refs[t][0, 0] + refs[n + t][0]).astype(out_dtypes[t])

    in_specs = [pl.BlockSpec((1, 1) + g.shape[2:], lambda q, c_ref: (q, c_ref[0], 0, 0)) for g in grads]
    in_specs += [pl.BlockSpec((1,) + r.shape[1:], lambda q, c_ref: (q, 0, 0)) for r in recvd]
    return pl.pallas_call(
        body, name=name,
        grid_spec=pltpu.PrefetchScalarGridSpec(
            num_scalar_prefetch=1, grid=(4,), in_specs=in_specs,
            out_specs=[pl.BlockSpec((1,) + r.shape[1:], lambda q, c_ref: (q, 0, 0)) for r in recvd]),
        out_shape=[jax.ShapeDtypeStruct(r.shape, dt) for r, dt in zip(recvd, out_dtypes)],
        compiler_params=_params(("arbitrary",)),
    )(cidx, *grads, *recvd)


def _final_sum_call(idx, grads, recvd1, recvd2):
    n = len(grads)
    nsteps = 2

    def body(i_ref, *refs):
        for t in range(n):
            g, r1, r2, o = refs[t], refs[n + t], refs[2 * n + t], refs[3 * n + t]
            s = g[0, 0] + r1[0]
            for r in range(3):
                s = s + r2[r].astype(F32)
            o[...] = s

    def rows(a):
        r = a.shape[-2]
        return r // nsteps if (r // nsteps) % 16 == 0 else r

    def step(a):
        return (lambda i: i) if rows(a) != a.shape[-2] else (lambda i: 0)

    in_specs = [pl.BlockSpec((1, 1, rows(g), g.shape[3]), lambda i, s, st=step(g): (s[0], s[1], st(i), 0)) for g in grads]
    in_specs += [pl.BlockSpec((1, rows(r), r.shape[2]), lambda i, s, st=step(r): (s[0], st(i), 0)) for r in recvd1]
    in_specs += [pl.BlockSpec((3, rows(r), r.shape[2]), lambda i, s, st=step(r): (0, st(i), 0)) for r in recvd2]
    return pl.pallas_call(
        body, name="rs_final_sum",
        grid_spec=pltpu.PrefetchScalarGridSpec(
            num_scalar_prefetch=1, grid=(nsteps,), in_specs=in_specs,
            out_specs=[pl.BlockSpec((rows(r), r.shape[2]), lambda i, s, st=step(r): (st(i), 0)) for r in recvd2]),
        out_shape=[jax.ShapeDtypeStruct(r.shape[1:], F32) for r in recvd2],
        compiler_params=_params(("arbitrary",)),
    )(idx, *grads, *recvd1, *recvd2)


def _sum8_call(parts):
    def body(p_ref, o_ref):
        s = p_ref[0]
        for j in range(1, N_DEV):
            s = s + p_ref[j]
        o_ref[...] = s

    return pl.pallas_call(body, name="sum_small_partials",
                          out_shape=jax.ShapeDtypeStruct(parts.shape[1:], parts.dtype))(parts)


def _adamw(w, g, m, v):
    m = ADAM_B1 * m + (1.0 - ADAM_B1) * g
    v = ADAM_B2 * v + (1.0 - ADAM_B2) * (g * g)
    m_hat = m / (1.0 - ADAM_B1 ** ADAM_STEP)
    v_hat = v / (1.0 - ADAM_B2 ** ADAM_STEP)
    delta = -ADAM_LR * (m_hat / (jnp.sqrt(v_hat) + ADAM_EPS) + ADAM_WD * w)
    return delta, m, v


def _adamw_call(ws, gs, ms, vs, nsteps, name):
    n = len(ws)

    def body(*refs):
        for t in range(n):
            w, g, m, v = (refs[k * n + t][...] for k in range(4))
            d, m2, v2 = _adamw(w, g, m, v)
            refs[4 * n + t][...] = d
            refs[5 * n + t][...] = m2
            refs[6 * n + t][...] = v2

    def spec(a):
        assert a.shape[0] % nsteps == 0 and (nsteps == 1 or (a.shape[0] // nsteps) % 8 == 0), a.shape
        return pl.BlockSpec((a.shape[0] // nsteps, a.shape[1]), lambda i: (i, 0))

    specs = [spec(a) for a in ws]
    outs = pl.pallas_call(
        body, name=name, grid=(nsteps,),
        in_specs=specs * 4, out_specs=specs * 3,
        out_shape=[jax.ShapeDtypeStruct(a.shape, F32) for a in ws] * 3,
        compiler_params=_params(("arbitrary",)),
    )(*ws, *gs, *ms, *vs)
    return outs[:n], outs[n:2 * n], outs[2 * n:]


def _adamw_rs_call(idx, gws, r1s, r2s, ws, ms, vs, nsteps, name):
    n = len(ws)

    def body(i_ref, *refs):
        for t in range(n):
            gw, r1, r2, w, m, v = (refs[k * n + t] for k in range(6))
            g = gw[0, 0] + r1[0]
            for r in range(3):
                g = g + r2[r].astype(F32)
            d, m2, v2 = _adamw(w[...], g, m[...], v[...])
            refs[6 * n + t][...] = g
            refs[7 * n + t][...] = d
            refs[8 * n + t][...] = m2
            refs[9 * n + t][...] = v2

    def rb(a):
        r = a.shape[0] // nsteps
        assert a.shape[0] % nsteps == 0 and r % 16 == 0, a.shape
        return r

    in_specs = [pl.BlockSpec((1, 1, rb(w), w.shape[1]), lambda i, s: (s[0], s[1], i, 0)) for w in ws]
    in_specs += [pl.BlockSpec((1, rb(w), w.shape[1]), lambda i, s: (s[0], i, 0)) for w in ws]
    in_specs += [pl.BlockSpec((3, rb(w), w.shape[1]), lambda i, s: (0, i, 0)) for w in ws]
    plain = [pl.BlockSpec((rb(w), w.shape[1]), lambda i, s: (i, 0)) for w in ws]
    outs = pl.pallas_call(
        body, name=name,
        grid_spec=pltpu.PrefetchScalarGridSpec(num_scalar_prefetch=1, grid=(nsteps,), in_specs=in_specs + plain * 3,
                                               out_specs=plain * 4),
        out_shape=[jax.ShapeDtypeStruct(w.shape, F32) for w in ws] * 4,
        compiler_params=_params(("arbitrary",)),
    )(idx, *gws, *r1s, *r2s, *ws, *ms, *vs)
    return outs[:n], outs[n:2 * n], outs[2 * n:3 * n], outs[3 * n:]


def _rows128(a, pad_rows):
    flat = a.reshape(-1).astype(F32)
    flat = jnp.pad(flat, (0, pad_rows * LANES - flat.shape[0]))
    return flat.reshape(pad_rows, LANES)


_SMALL_A = (("w_pool", 512), ("pool_scale", 8), ("attn_sinks", 8), ("g_mix_post", 8), ("g_mlp_pre", 8),
            ("g_mlp_post", 8), ("loss", 8), ("b_in_gates", 16))
_SMALL_A_ROWS = 640
_SMALL_B = (("g_mix_pre", 8), ("b_in_head", 16))


def _pack(parts, layout, total_rows):
    rows = [_rows128(parts[k], r) for k, r in layout]
    pad = total_rows - sum(r for _, r in layout)
    if pad:
        rows.append(jnp.zeros((pad, LANES), F32))
    return jnp.concatenate(rows, axis=0)


def _unpack(buf, layout, sizes):
    out, off = {}, 0
    for k, r in layout:
        out[k] = buf[off:off + r].reshape(-1)[:sizes[k]]
        off += r
    return out


def kernel(x, g_mix_pre, w_in, b_in, w_pool, pool_scale, attn_sinks, w_branch_pool, w_branch_attn, w_out, g_mix_post, g_mlp_pre, w_up, w_down, g_mlp_post, loss_target, m_g_mix_pre, m_w_in, m_b_in, m_w_pool, m_pool_scale, m_attn_sinks, m_w_branch_pool, m_w_branch_attn, m_w_out, m_g_mix_post, m_g_mlp_pre, m_w_up, m_w_down, m_g_mlp_post, v_g_mix_pre, v_w_in, v_b_in, v_w_pool, v_pool_scale, v_attn_sinks, v_w_branch_pool, v_w_branch_attn, v_w_out, v_g_mix_post, v_g_mlp_pre, v_w_up, v_w_down, v_g_mlp_post):
    B, S, _ = x.shape
    T = B * S
    xt = x.reshape(T, D_MODEL)
    tgt = loss_target.reshape(T, D_MODEL)
    cx, cy, cc = _coords()

    cidx = jnp.reshape(cc, (1,)).astype(jnp.int32)
    by_chip = lambda gr: gr.reshape((4, 2) + gr.shape[1:])
    bf = lambda w: w[0].astype(MXU_DTYPE)

    (win_s,) = _allgather_call([w_in[0].T.astype(MXU_DTYPE)])
    win_t = win_s.reshape(IN_WIDTH, D_MODEL)
    wpool_b = bf(w_pool)
    rc, rsa, rsb = _rot_tables(S)

    up_a, up_b = (0, D_MODEL // 2), (D_MODEL // 2, D_MODEL // 2)
    dn_a, dn_b = (0, D_FF // 16), (D_FF // 16, D_FF // 16)
    wup_l, wdown_l = bf(w_up), bf(w_down)
    (h, u, q, k4, v4, g), (wbp_1, wba_1, wout_1, wup_1) = _inproj_call(
        xt, g_mix_pre, win_t, b_in, rc, rsa, rsb, S,
        rider=_rider_ag([(bf(w_branch_pool), None, ALL, None), (bf(w_branch_attn), None, ALL, None),
                         (bf(w_out), None, ALL, None), (wup_l, None, up_a, None)]))
    yp = _pool_call(u, wpool_b, pool_scale, S)
    (ya,), (wbp_s, wba_s, wout_s, wup_2, wdown_1) = _attn_call(
        attn_sinks, q, k4, v4, S,
        rider=_rider_ag([(None, wbp_1, None, ALL), (None, wba_1, None, ALL), (None, wout_1, None, ALL),
                         (wup_l, wup_1, up_b, up_a), (wdown_l, None, dn_a, None)]))
    wout_f = wout_s.reshape(D_MODEL, D_MODEL)
    (mix, x1, h2), (wup_s, wdown_2) = _mix_fwd_call(
        yp, ya, g, xt, wbp_s, wba_s, wout_f, g_mix_post, g_mlp_pre,
        rider=_rider_ag([(None, wup_2, None, up_b), (wdown_l, wdown_1, dn_b, dn_a)]))
    (wdown_s,) = _comm_call(_rider_ag([(None, wdown_2, None, dn_b)]), "allgather_finish")

    act, da, dff, dx1, dg3, dg4, lossvec = _mlp_call(x1, h2, tgt, wup_s, wdown_s, g_mlp_pre, g_mlp_post)
    gw_down = by_chip(_wgrad_rows_call(act, dff, "wgrad_down")[0])
    (gw_up,), (r1_down,) = _wgrad_cols_call([(h2, da)], "wgrad_up", rider=_rider_rs_sibling([gw_down]))
    gw_up = by_chip(gw_up)
    (s_down,) = _chip_sum_call(cidx, [gw_down], [r1_down], [MXU_DTYPE], "rs_chip_sum_down")
    (c_down,), tok = _copies_start([_rider_rs_chips([s_down])], "rs_chips_start_down")
    (dmix, merged, dbp, dba, dyp, do, dgates, dg2, dbg), (r1_up,) = _mix_bwd_call(
        dx1, mix, yp, ya, g, wbp_s, wba_s, wout_f, g_mix_post, rider=_after(tok, _rider_rs_sibling([gw_up])))
    (s_up,) = _chip_sum_call(cidx, [gw_up], [r1_up], [MXU_DTYPE], "rs_chip_sum_up")
    (c_up,), tok = _copies_start([_rider_rs_chips([s_up])], "rs_chips_start_up")
    (dq, dk, dv, dsink), _ = _attn_bwd_call(attn_sinks, q, k4, v4, do, rc, rsa, rsb, S, rider=_after(tok))
    gw_out = by_chip(_wgrad_rows_call(merged, dmix, "wgrad_out")[0])
    gw_bp, gw_ba = _wgrad_cols_call([(yp, dbp), (ya, dba)], "wgrad_branch")
    gw_bp, gw_ba = by_chip(gw_bp), by_chip(gw_ba)
    (du, dwp, dps), (r1_out, r1_bp, r1_ba) = _pool_bwd_call(
        u, dyp, wpool_b, pool_scale, S, rider=_rider_rs_sibling([gw_out, gw_bp, gw_ba]))
    s_obb = _chip_sum_call(cidx, [gw_out, gw_bp, gw_ba], [r1_out, r1_bp, r1_ba], [MXU_DTYPE] * 3, "rs_chip_sum_branch")
    (c_obb,), tok = _copies_start([_rider_rs_chips(s_obb)], "rs_chips_start_branch")
    (gw_in,), _ = _wgrad_in_call(du, dq, dk, dv, dgates, h, rider=_after(tok))
    gw_in = by_chip(gw_in)

    small_a = {"w_pool": dwp, "pool_scale": dps,
               "attn_sinks": jnp.sum(dsink.reshape(B, 8, LANES)[:, 0, :N_Q_HEADS], axis=0), "g_mix_post": dg2,
               "g_mlp_pre": dg3, "g_mlp_post": dg4, "loss": lossvec, "b_in_gates": dbg}
    gw_sa = by_chip(_pack(small_a, _SMALL_A, _SMALL_A_ROWS).reshape(N_DEV, _SMALL_A_ROWS // N_DEV, LANES))
    r1_in, r1_sa = _comm_call(_rider_rs_sibling([gw_in, gw_sa]), "rs_sibling_in")
    s_in, s_sa = _chip_sum_call(cidx, [gw_in, gw_sa], [r1_in, r1_sa], [MXU_DTYPE, F32], "rs_chip_sum_in")
    (c_in,), tok = _copies_start([_rider_rs_chips([s_in, s_sa])], "rs_chips_start_in")
    (gx, dg1, dba_in), _ = _inproj_bwd_call(du, dq, dk, dv, dgates, dx1, xt, win_t, g_mix_pre, rider=_after(tok))
    r2_down, r2_up, r2_out, r2_bp, r2_ba, r2_in, r2_sa = _copies_wait([c_down, c_up, c_obb, c_in], dg1, "rs_chips_wait")

    idx = jnp.stack([2 * cx + cy, cc]).astype(jnp.int32)
    (g_sa,) = _final_sum_call(idx, [gw_sa], [r1_sa], [r2_sa])
    part_b = _pack({"g_mix_pre": dg1, "b_in_head": dba_in}, _SMALL_B, sum(r for _, r in _SMALL_B))
    sa_all, sb_all = _comm_call(_rider_gather_direct([g_sa, part_b]), "allgather_small")
    sb_sum = _sum8_call(sb_all)

    in_t = _adamw_rs_call(idx, [gw_in], [r1_in], [r2_in], [w_in[0].T], [m_w_in[0].T], [v_w_in[0].T], 2, "adamw_w_in")
    rest = _adamw_rs_call(
        idx, [gw_bp, gw_ba, gw_out, gw_up, gw_down], [r1_bp, r1_ba, r1_out, r1_up, r1_down],
        [r2_bp, r2_ba, r2_out, r2_up, r2_down], [w_branch_pool[0], w_branch_attn[0], w_out[0], w_up[0], w_down[0]],
        [m_w_branch_pool[0], m_w_branch_attn[0], m_w_out[0], m_w_up[0], m_w_down[0]],
        [v_w_branch_pool[0], v_w_branch_attn[0], v_w_out[0], v_w_up[0], v_w_down[0]], N_DEV, "adamw_shards")
    big_g, big_d, big_m2, big_v2 = ([a[0].T] + list(b) for a, b in zip(in_t, rest))

    names = ["g_mix_pre", "b_in", "w_pool", "pool_scale", "attn_sinks", "g_mix_post", "g_mlp_pre", "g_mlp_post"]
    sm_w = dict(g_mix_pre=g_mix_pre, b_in=b_in, w_pool=w_pool, pool_scale=pool_scale, attn_sinks=attn_sinks,
                g_mix_post=g_mix_post, g_mlp_pre=g_mlp_pre, g_mlp_post=g_mlp_post)
    sm_m = dict(g_mix_pre=m_g_mix_pre, b_in=m_b_in, w_pool=m_w_pool, pool_scale=m_pool_scale, attn_sinks=m_attn_sinks,
                g_mix_post=m_g_mix_post, g_mlp_pre=m_g_mlp_pre, g_mlp_post=m_g_mlp_post)
    sm_v = dict(g_mix_pre=v_g_mix_pre, b_in=v_b_in, w_pool=v_w_pool, pool_scale=v_pool_scale, attn_sinks=v_attn_sinks,
                g_mix_post=v_g_mix_post, g_mlp_pre=v_g_mlp_pre, g_mlp_post=v_g_mlp_post)
    sizes = {k: sm_w[k].size for k in names}
    sizes.update(loss=D_MODEL, b_in_gates=GATE_WIDTH, b_in_head=C_G)
    sm_g = _unpack(sa_all.reshape(_SMALL_A_ROWS, LANES), _SMALL_A, sizes)
    sm_g.update(_unpack(sb_sum, _SMALL_B, sizes))
    sm_g["b_in"] = jnp.concatenate([sm_g["b_in_head"], sm_g["b_in_gates"]])
    loss = (0.5 / D_MODEL) * jnp.sum(sm_g["loss"])
    two_d = lambda a: a.reshape(-1, a.shape[-1])
    sd_, sm2_, sv2_ = _adamw_call([two_d(sm_w[k]) for k in names], [two_d(sm_g[k].reshape(sm_w[k].shape)) for k in names],
                                  [two_d(sm_m[k]) for k in names], [two_d(sm_v[k]) for k in names], 1, "adamw_small")
    like = lambda vals: {k: a.reshape(sm_w[k].shape) for k, a in zip(names, vals)}
    sm_d, sm_m2, sm_v2 = like(sd_), like(sm2_), like(sv2_)
    sm_gr = {k: sm_g[k].reshape(sm_w[k].shape) for k in names}

    order = ["g_mix_pre", "w_in", "b_in", "w_pool", "pool_scale", "attn_sinks", "w_branch_pool", "w_branch_attn",
             "w_out", "g_mix_post", "g_mlp_pre", "w_up", "w_down", "g_mlp_post"]
    big_names = ["w_in", "w_branch_pool", "w_branch_attn", "w_out", "w_up", "w_down"]
    lead = lambda a: a[None]
    tables = []
    for small_t, big_t in ((sm_gr, big_g), (sm_d, big_d), (sm_m2, big_m2), (sm_v2, big_v2)):
        bt = dict(zip(big_names, big_t))
        tables.append([lead(bt[k]) if k in bt else small_t[k] for k in order])
    return (loss, gx.reshape(B, S, D_MODEL), *tables[0], *tables[1], *tables[2], *tables[3])
```

```python
import functools

import jax
import jax.numpy as jnp
from jax import lax
from jax.experimental import pallas as pl
from jax.experimental.pallas import tpu as pltpu

F32 = jnp.float32
MXU_DTYPE = jnp.bfloat16
MESH = pl.DeviceIdType.MESH

D_MODEL = 1024
POOL_WINDOWS = (2, 4, 8, 16)
POOL_WIDTH = 512
POOL_GC = 128
HEAD_DIM = 64
N_Q_HEADS = 8
N_KV_HEADS = 2
GROUP = 4
ATTN_WIDTH = 512
KV_WIDTH = 128
BLOCK = 128
GATE_WIDTH = 2048
IN_WIDTH = 3328
D_FF = 4096
EPS = 1e-6
NEG_INF = -1e30
ROPE_THETA = 500000.0
ROT_DIM = 16
SCALE = HEAD_DIM ** -0.5
C_Q, C_K, C_V, C_G = 512, 1024, 1152, 1280

ADAM_LR = 0.001
ADAM_B1 = 0.9
ADAM_B2 = 0.999
ADAM_EPS = 1e-08
ADAM_WD = 0.01
ADAM_STEP = 10

N_DEV = 8
LANES = 128
VMEM_LIMIT = 56 * 1024 * 1024

NN = (((1,), (0,)), ((), ()))
NT = (((1,), (1,)), ((), ()))
TN = (((0,), (0,)), ((), ()))


def _dot(a, b, dims):
    return lax.dot_general(a, b, dims, preferred_element_type=F32)


def _params(sem=None):
    return pltpu.CompilerParams(dimension_semantics=sem, vmem_limit_bytes=VMEM_LIMIT)


def _tile(n, pref):
    t = min(n, pref)
    assert n % t == 0, (n, t)
    return t


class _Rider:
    def __init__(self, ins, out_shape, n_remote, n_local, plan, aliases=None):
        self.ins, self.out_shape, self.n_remote, self.n_local = list(ins), list(out_shape), n_remote, n_local
        self.plan, self.aliases = plan, dict(aliases or {})


def _after(token, rider=None):
    r = rider or _Rider([], [], 0, 0, lambda ins, outs, send, recv, loc, r0, l0: ([], []))
    return _Rider(r.ins + [token], r.out_shape, r.n_remote, r.n_local, r.plan, r.aliases)


def _merge_riders(a, b):
    na_in, na_out = len(a.ins), len(a.out_shape)

    def plan(ins, outs, send, recv, loc, r0, l0):
        ra, la = a.plan(ins[:na_in], outs[:na_out], send, recv, loc, r0, l0)
        rb, lb = b.plan(ins[na_in:], outs[na_out:], send, recv, loc, r0 + a.n_remote, l0 + a.n_local)
        return ra + rb, la + lb

    aliases = dict(a.aliases)
    aliases.update({na_in + i: na_out + o for i, o in b.aliases.items()})
    return _Rider(a.ins + b.ins, a.out_shape + b.out_shape, a.n_remote + b.n_remote, a.n_local + b.n_local, plan, aliases)


def _launch(body, args, *, name, grid, in_specs, out_specs, out_shape, scratch_shapes=(), sem=None, rider=None):
    if rider is None:
        return pl.pallas_call(body, name=name, grid=grid, in_specs=in_specs, out_specs=out_specs, out_shape=out_shape,
                              scratch_shapes=list(scratch_shapes), compiler_params=_params(sem))(*args)
    n_in, n_out, n_scr = len(args), len(out_shape), len(scratch_shapes)
    r_in, r_out = len(rider.ins), len(rider.out_shape)
    copies = rider.n_remote + rider.n_local > 0

    def wrapped(*refs):
        ins, rins = refs[:n_in], refs[n_in:n_in + r_in]
        o0 = n_in + r_in
        outs, routs = refs[o0:o0 + n_out], refs[o0 + n_out:o0 + n_out + r_out]
        s0 = o0 + n_out + r_out
        scr = refs[s0:s0 + n_scr]
        if not copies:
            return body(*ins, *outs, *scr)
        send, recv, loc = refs[s0 + n_scr:]
        first, last = None, None
        for d in range(len(grid)):
            f, l = pl.program_id(d) == 0, pl.program_id(d) == pl.num_programs(d) - 1
            first = f if first is None else first & f
            last = l if last is None else last & l

        def start():
            remote, local = rider.plan(rins, routs, send, recv, loc, 0, 0)
            for cp in local + remote:
                cp.start()

        def finish():
            remote, local = rider.plan(rins, routs, send, recv, loc, 0, 0)
            for cp in remote + local:
                cp.wait()

        if first is None:
            start()
            body(*ins, *outs, *scr)
            finish()
        else:
            pl.when(first)(start)
            body(*ins, *outs, *scr)
            pl.when(last)(finish)

    hbm = pl.BlockSpec(memory_space=pl.ANY)
    dma = pltpu.SemaphoreType.DMA
    res = pl.pallas_call(
        wrapped, name=name, grid=grid, in_specs=list(in_specs) + [hbm] * r_in,
        out_specs=list(out_specs) + [hbm] * r_out, out_shape=list(out_shape) + rider.out_shape,
        scratch_shapes=list(scratch_shapes) + (
            [dma((max(rider.n_remote, 1),)), dma((max(rider.n_remote, 1),)), dma((max(rider.n_local, 1),))] if copies else []),
        input_output_aliases={n_in + i: n_out + o for i, o in rider.aliases.items()},
        compiler_params=_params(sem),
    )(*args, *rider.ins)
    return list(res[:n_out]), list(res[n_out:])


def _comm_call(rider, name):
    return _launch(lambda: None, [], name=name, grid=(), in_specs=[], out_specs=[], out_shape=[], rider=rider)[1]


_HBM = pl.BlockSpec(memory_space=pltpu.HBM)
_SEM = pl.BlockSpec(memory_space=pltpu.SEMAPHORE)
_EFFECT = pltpu.SideEffectType.DATAFLOW_SIDE_EFFECTING


def _copies_start(riders, name):
    assert all(r.n_local == 0 and not r.aliases for r in riders)
    sizes = [(len(r.ins), len(r.out_shape)) for r in riders]
    bufs = []
    for r in riders:
        bufs += [pltpu.with_memory_space_constraint(a, pltpu.HBM) for a in r.ins]
        bufs += [pltpu.with_memory_space_constraint(lax.empty(s.shape, s.dtype), pltpu.HBM) for s in r.out_shape]
    nb, ng = len(bufs), len(riders)

    def body(*refs):
        sems, token, at = refs[2 * nb:2 * nb + 2 * ng], refs[-1], 0
        for g, (r, (ni, no)) in enumerate(zip(riders, sizes)):
            remote, _ = r.plan(refs[at:at + ni], refs[at + ni:at + ni + no], sems[2 * g], sems[2 * g + 1], None, 0, 0)
            for cp in remote:
                cp.start()
            at += ni + no
        token[...] = jnp.zeros_like(token)

    res = pl.pallas_call(
        body, name=name, in_specs=[_HBM] * nb,
        out_specs=[_HBM] * nb + [_SEM] * (2 * ng) + [pl.BlockSpec(memory_space=pltpu.VMEM)],
        out_shape=[pltpu.HBM(a.shape, a.dtype) for a in bufs]
        + [pltpu.SemaphoreType.DMA((r.n_remote,)) for r in riders for _ in range(2)]
        + [jax.ShapeDtypeStruct((8, LANES), F32)],
        input_output_aliases={i: i for i in range(nb)},
        compiler_params=pltpu.CompilerParams(has_side_effects=_EFFECT),
    )(*bufs)
    handles, at = [], 0
    for g, (r, (ni, no)) in enumerate(zip(riders, sizes)):
        handles.append((r, list(res[at:at + ni + no]), res[nb + 2 * g], res[nb + 2 * g + 1]))
        at += ni + no
    return handles, res[-1]


def _copies_wait(handles, after, name):
    bufs = [b for _, bs, _, _ in handles for b in bs]
    sems = [s for _, _, send, recv in handles for s in (send, recv)]
    nb, ng = len(bufs), len(handles)

    def body(*refs):
        at = 0
        for g, (rider, bs, _, _) in enumerate(handles):
            ni = len(rider.ins)
            remote, _ = rider.plan(refs[at:at + ni], refs[at + ni:at + len(bs)], refs[nb + 2 * g], refs[nb + 2 * g + 1],
                                   None, 0, 0)
            for cp in remote:
                cp.wait_send()
                cp.wait_recv()
            at += len(bs)

    res = pl.pallas_call(
        body, name=name, in_specs=[_HBM] * nb + [_SEM] * (2 * ng) + [pl.BlockSpec(memory_space=pl.ANY)],
        out_specs=[_HBM] * nb, out_shape=[pltpu.HBM(a.shape, a.dtype) for a in bufs],
        input_output_aliases={i: i for i in range(nb)},
        compiler_params=pltpu.CompilerParams(has_side_effects=_EFFECT),
    )(*bufs, *sems, after)
    lands, at = [], 0
    for rider, bs, _, _ in handles:
        lands += list(res[at + len(rider.ins):at + len(bs)])
        at += len(bs)
    return lands


def _rms_r(x):
    return lax.rsqrt(jnp.mean(x * x, axis=-1, keepdims=True) + EPS)


def _rms_bwd(dn, x, r, g):
    xh = x * r
    dxh = dn * g
    dx = r * (dxh - xh * jnp.mean(dxh * xh, axis=-1, keepdims=True))
    return dx, dn * xh


def _rot(t, c, sa, sb):
    outs = []
    for j in range(t.shape[1] // LANES):
        tj = t[:, LANES * j:LANES * (j + 1)]
        outs.append(tj * c + pltpu.roll(tj, LANES - 8, 1) * sa + pltpu.roll(tj, 8, 1) * sb)
    return outs[0] if len(outs) == 1 else jnp.concatenate(outs, axis=1)


def _rot_tables(S):
    pos = jnp.arange(S, dtype=F32)
    inv_freq = ROPE_THETA ** (-jnp.arange(0, ROT_DIM, 2, dtype=F32) / ROT_DIM)
    ang = pos[:, None] * inv_freq[None, :]
    cos, sin = jnp.cos(ang), jnp.sin(ang)
    one = jnp.ones((S, HEAD_DIM - ROT_DIM), F32)
    zero = jnp.zeros((S, HEAD_DIM - ROT_DIM), F32)
    z8 = jnp.zeros((S, 8), F32)
    c = jnp.concatenate([cos, cos, one], axis=1)
    sa = jnp.concatenate([-sin, z8, zero], axis=1)
    sb = jnp.concatenate([z8, sin, zero], axis=1)
    rep = LANES // HEAD_DIM
    return jnp.tile(c, (1, rep)), jnp.tile(sa, (1, rep)), jnp.tile(sb, (1, rep))


def _lane_tile4(k):
    lane = lax.broadcasted_iota(jnp.int32, k.shape, 1)
    rk = pltpu.roll(k, HEAD_DIM, 1)
    x0 = jnp.where(lane < HEAD_DIM, k, rk)
    x1 = jnp.where(lane < HEAD_DIM, rk, k)
    return jnp.concatenate([x0, x0, x1, x1], axis=1)


def _fold_heads(acc):
    zs = []
    for hk in range(N_KV_HEADS):
        a = acc[:, 256 * hk:256 * hk + LANES] + acc[:, 256 * hk + LANES:256 * (hk + 1)]
        zs.append(a + pltpu.roll(a, HEAD_DIM, 1))
    lane = lax.broadcasted_iota(jnp.int32, zs[0].shape, 1)
    return jnp.where(lane < HEAD_DIM, zs[0], zs[1])


def _inproj_call(x, g1, win_t, b_in, rc, rsa, rsb, S, rider=None):
    T = x.shape[0]
    tm = _tile(S, 512)
    nst = S // tm

    def body(x_ref, g1_ref, w_ref, b_ref, c_ref, sa_ref, sb_ref,
             h_ref, u_ref, q_ref, k4_ref, v4_ref, g_ref):
        xv = x_ref[...]
        hb = ((xv * _rms_r(xv)) * g1_ref[...]).astype(MXU_DTYPE)
        h_ref[...] = hb

        def proj(lo, hi):
            return _dot(hb, w_ref[lo:hi, :], NT) + b_ref[:, lo:hi]

        c, sa, sb = c_ref[...], sa_ref[...], sb_ref[...]
        u_ref[...] = proj(0, C_Q)
        q_ref[...] = (_rot(proj(C_Q, C_K), c, sa, sb) * SCALE).astype(MXU_DTYPE)
        kv = proj(C_K, C_G)
        k4_ref[...] = _lane_tile4(_rot(kv[:, :KV_WIDTH], c, sa, sb)).astype(MXU_DTYPE)
        v4_ref[...] = _lane_tile4(kv[:, KV_WIDTH:]).astype(MXU_DTYPE)
        g_ref[...] = jax.nn.sigmoid(proj(C_G, IN_WIDTH)).astype(MXU_DTYPE)

    tok = lambda w: pl.BlockSpec((tm, w), lambda i: (i, 0))
    full = lambda a: pl.BlockSpec(a.shape, lambda i: (0,) * a.ndim)
    tab = pl.BlockSpec((tm, LANES), lambda i: (i % nst, 0))
    return _launch(
        body, [x, g1, win_t, b_in, rc, rsa, rsb], name="inproj_fwd", grid=(T // tm,),
        in_specs=[tok(D_MODEL), full(g1), full(win_t), full(b_in), tab, tab, tab],
        out_specs=[tok(D_MODEL), tok(POOL_WIDTH), tok(ATTN_WIDTH), tok(512), tok(512), tok(GATE_WIDTH)],
        out_shape=[jax.ShapeDtypeStruct((T, D_MODEL), MXU_DTYPE), jax.ShapeDtypeStruct((T, POOL_WIDTH), F32),
                   jax.ShapeDtypeStruct((T, ATTN_WIDTH), MXU_DTYPE), jax.ShapeDtypeStruct((T, 512), MXU_DTYPE),
                   jax.ShapeDtypeStruct((T, 512), MXU_DTYPE), jax.ShapeDtypeStruct((T, GATE_WIDTH), MXU_DTYPE)],
        sem=("arbitrary",), rider=rider)


def _shift_rows(a, k, rows):
    n = a.shape[0]
    if k > 0:
        return jnp.where(rows >= k, pltpu.roll(a, k, 0), 0.0)
    return jnp.where(rows < n + k, pltpu.roll(a, n + k, 0), 0.0)


def _win_sum(a, w, rows, sign):
    s, k = a, 1
    while k < w:
        s = s + _shift_rows(s, sign * k, rows)
        k *= 2
    return s


def _pool_diff(ug, w, rows):
    inv = 1.0 / jnp.minimum(rows + 1, w).astype(F32)
    return _win_sum(ug, w, rows, 1) * inv - ug, inv


def _pool_call(u, w_pool, pool_scale, S):
    T = u.shape[0]

    def body(u_ref, w_ref, ps_ref, y_ref):
        rows = lax.broadcasted_iota(jnp.int32, (S, POOL_GC), 0)
        for gi, w in enumerate(POOL_WINDOWS):
            sl = slice(POOL_GC * gi, POOL_GC * (gi + 1))
            diff, _ = _pool_diff(u_ref[:, sl], w, rows)
            mixed = _dot(diff.astype(MXU_DTYPE), w_ref[gi], NN)
            y_ref[:, sl] = (mixed * ps_ref[:, sl]).astype(MXU_DTYPE)

    seq = pl.BlockSpec((S, POOL_WIDTH), lambda b: (b, 0))
    return pl.pallas_call(
        body, name="pool_fwd", grid=(T // S,),
        in_specs=[seq, pl.BlockSpec(w_pool.shape, lambda b: (0, 0, 0)), pl.BlockSpec(pool_scale.shape, lambda b: (0, 0))],
        out_specs=seq, out_shape=jax.ShapeDtypeStruct((T, POOL_WIDTH), MXU_DTYPE),
        compiler_params=_params(("arbitrary",)),
    )(u, w_pool, pool_scale)


def _pool_bwd_call(u, dyp, w_pool, pool_scale, S, rider=None):
    T = u.shape[0]

    def body(u_ref, dy_ref, w_ref, ps_ref, du_ref, dw_ref, dps_ref):
        @pl.when(pl.program_id(0) == 0)
        def _():
            dw_ref[...] = jnp.zeros_like(dw_ref)
            dps_ref[...] = jnp.zeros_like(dps_ref)

        rows = lax.broadcasted_iota(jnp.int32, (S, POOL_GC), 0)
        for gi, w in enumerate(POOL_WINDOWS):
            sl = slice(POOL_GC * gi, POOL_GC * (gi + 1))
            diff, inv = _pool_diff(u_ref[:, sl], w, rows)
            diffb = diff.astype(MXU_DTYPE)
            wg = w_ref[gi]
            mixed = _dot(diffb, wg, NN)
            dy = dy_ref[:, sl]
            dps_ref[:, sl] += jnp.sum(dy * mixed, axis=0, keepdims=True)
            dmb = (dy * ps_ref[:, sl]).astype(MXU_DTYPE)
            dw_ref[gi] += _dot(diffb, dmb, TN)
            ddiff = _dot(dmb, wg, NT)
            du_ref[:, sl] = (_win_sum(ddiff * inv, w, rows, -1) - ddiff).astype(MXU_DTYPE)

    seq = pl.BlockSpec((S, POOL_WIDTH), lambda b: (b, 0))
    return _launch(
        body, [u, dyp, w_pool, pool_scale], name="pool_bwd", grid=(T // S,),
        in_specs=[seq, seq, pl.BlockSpec(w_pool.shape, lambda b: (0, 0, 0)), pl.BlockSpec(pool_scale.shape, lambda b: (0, 0))],
        out_specs=[seq, pl.BlockSpec(w_pool.shape, lambda b: (0, 0, 0)), pl.BlockSpec(pool_scale.shape, lambda b: (0, 0))],
        out_shape=[jax.ShapeDtypeStruct((T, POOL_WIDTH), MXU_DTYPE), jax.ShapeDtypeStruct(w_pool.shape, F32),
                   jax.ShapeDtypeStruct(pool_scale.shape, F32)],
        sem=("arbitrary",), rider=rider)


def _attn_consts():
    lane_g = lax.broadcasted_iota(jnp.int32, (BLOCK, 256), 1) >> 6
    rgrp = lax.broadcasted_iota(jnp.int32, (GROUP * BLOCK, 1), 0) >> 7
    rel = lax.broadcasted_iota(jnp.int32, (BLOCK, 256), 0) - lax.broadcasted_iota(jnp.int32, (BLOCK, 256), 1)

    def bias(off):
        ok = (rel + off >= 0) & (rel + off < BLOCK)
        return jnp.concatenate([jnp.where(ok, 0.0, NEG_INF)] * GROUP, axis=0)

    return lane_g, rgrp, bias(0), bias(BLOCK)


def _sink_rows(sink_ref, hk, rgrp):
    sv = jnp.zeros(rgrp.shape, F32)
    for g in range(GROUP):
        sv = jnp.where(rgrp == g, sink_ref[0, GROUP * hk + g], sv)
    return sv


def _stack_heads(xb, lane_g):
    return jnp.concatenate([jnp.where(lane_g == g, xb, jnp.zeros_like(xb)) for g in range(GROUP)], axis=0)


def _unstack_heads(xs, lane_g):
    out = jnp.where(lane_g == 0, xs[0:BLOCK], 0.0)
    for g in range(1, GROUP):
        out = out + jnp.where(lane_g == g, xs[BLOCK * g:BLOCK * (g + 1)], 0.0)
    return out


def _attn_probs(qs, kb, bias, sv):
    s = _dot(qs, kb, NT) + bias
    m = jnp.maximum(jnp.max(s, axis=1, keepdims=True), sv)
    e = jnp.exp(s - m)
    es = jnp.exp(sv - m)
    inv_l = 1.0 / (jnp.sum(e, axis=1, keepdims=True) + es)
    return e * inv_l, es * inv_l


def _attn_blocks(nb, blk, carry):
    carry = blk(0, 0, True, carry)
    return lax.fori_loop(1, nb, lambda n, c: blk(pl.multiple_of(n * BLOCK, BLOCK),
                                                 pl.multiple_of((n - 1) * BLOCK, BLOCK), False, c), carry)


def _attn_call(sinks, q, k4, v4, S, rider=None):
    T = q.shape[0]
    nb = S // BLOCK

    def body(sink_ref, q_ref, k_ref, v_ref, o_ref):
        lane_g, rgrp, bias_first, bias_later = _attn_consts()
        svs = [_sink_rows(sink_ref, hk, rgrp) for hk in range(N_KV_HEADS)]

        def blk(q0, k0, first, carry):
            for hk in range(N_KV_HEADS):
                cs = slice(256 * hk, 256 * (hk + 1))
                qs = _stack_heads(q_ref[pl.ds(q0, BLOCK), cs], lane_g)
                p, _ = _attn_probs(qs, k_ref[pl.ds(k0, 2 * BLOCK), cs], bias_first if first else bias_later, svs[hk])
                o = _dot(p.astype(MXU_DTYPE), v_ref[pl.ds(k0, 2 * BLOCK), cs], NN)
                o_ref[pl.ds(q0, BLOCK), cs] = _unstack_heads(o, lane_g).astype(MXU_DTYPE)
            return carry

        _attn_blocks(nb, blk, 0)

    seq = pl.BlockSpec((S, ATTN_WIDTH), lambda b: (b, 0))
    return _launch(
        body, [sinks, q, k4, v4], name="attn_fwd", grid=(T // S,),
        in_specs=[pl.BlockSpec(memory_space=pltpu.SMEM), seq, seq, seq],
        out_specs=[seq], out_shape=[jax.ShapeDtypeStruct((T, ATTN_WIDTH), MXU_DTYPE)],
        sem=("arbitrary",), rider=rider)


def _attn_bwd_call(sinks, q, k4, v4, do, rc, rsa, rsb, S, rider=None):
    T = q.shape[0]
    nb = S // BLOCK

    def body(sink_ref, q_ref, k_ref, v_ref, do_ref, c_ref, sa_ref, sb_ref,
             dq_ref, dk_ref, dv_ref, ds_ref, dk_acc, dv_acc):
        lane_g, rgrp, bias_first, bias_later = _attn_consts()
        svs = [_sink_rows(sink_ref, hk, rgrp) for hk in range(N_KV_HEADS)]
        lane1 = lax.broadcasted_iota(jnp.int32, (1, LANES), 1)
        dk_acc[...] = jnp.zeros_like(dk_acc)
        dv_acc[...] = jnp.zeros_like(dv_acc)

        def blk(q0, k0, first, dsink):
            rows = pl.ds(q0, BLOCK)
            c, sa, sb = c_ref[rows, :], sa_ref[rows, :], sb_ref[rows, :]
            for hk in range(N_KV_HEADS):
                cs = slice(256 * hk, 256 * (hk + 1))
                qs = _stack_heads(q_ref[rows, cs], lane_g)
                dos = _stack_heads(do_ref[rows, cs], lane_g)
                kb = k_ref[pl.ds(k0, 2 * BLOCK), cs]
                vb = v_ref[pl.ds(k0, 2 * BLOCK), cs]
                p, ps = _attn_probs(qs, kb, bias_first if first else bias_later, svs[hk])
                dp = _dot(dos, vb, NT)
                delta = jnp.sum(p * dp, axis=1, keepdims=True)
                dsb = (p * (dp - delta)).astype(MXU_DTYPE)
                dqb = _unstack_heads(_dot(dsb, kb, NN), lane_g) * SCALE
                dq_ref[rows, cs] = _rot(dqb, c, -sa, -sb).astype(MXU_DTYPE)
                dk_acc[pl.ds(k0, 2 * BLOCK), cs] += _dot(dsb, qs, TN)
                dv_acc[pl.ds(k0, 2 * BLOCK), cs] += _dot(p.astype(MXU_DTYPE), dos, TN)
                psd = ps * delta
                for g in range(GROUP):
                    val = -jnp.sum(psd[BLOCK * g:BLOCK * (g + 1)], axis=0, keepdims=True)
                    dsink = dsink + jnp.where(lane1 == GROUP * hk + g, val, 0.0)
            return dsink

        dsink = _attn_blocks(nb, blk, jnp.zeros((1, LANES), F32))
        dk_ref[...] = _rot(_fold_heads(dk_acc[...]), c_ref[...], -sa_ref[...], -sb_ref[...]).astype(MXU_DTYPE)
        dv_ref[...] = _fold_heads(dv_acc[...]).astype(MXU_DTYPE)
        ds_ref[...] = jnp.broadcast_to(dsink, ds_ref.shape)

    seq = pl.BlockSpec((S, ATTN_WIDTH), lambda b: (b, 0))
    kvs = pl.BlockSpec((S, KV_WIDTH), lambda b: (b, 0))
    tab = pl.BlockSpec((S, LANES), lambda b: (0, 0))
    nseq = T // S
    return _launch(
        body, [sinks, q, k4, v4, do, rc, rsa, rsb], name="attn_bwd", grid=(nseq,),
        in_specs=[pl.BlockSpec(memory_space=pltpu.SMEM), seq, seq, seq, seq, tab, tab, tab],
        out_specs=[seq, kvs, kvs, pl.BlockSpec((8, LANES), lambda b: (b, 0))],
        out_shape=[jax.ShapeDtypeStruct((T, ATTN_WIDTH), MXU_DTYPE), jax.ShapeDtypeStruct((T, KV_WIDTH), MXU_DTYPE),
                   jax.ShapeDtypeStruct((T, KV_WIDTH), MXU_DTYPE), jax.ShapeDtypeStruct((8 * nseq, LANES), F32)],
        scratch_shapes=[pltpu.VMEM((S, 512), F32), pltpu.VMEM((S, 512), F32)],
        sem=("arbitrary",), rider=rider)


def _branch_weights(wbp_ref, wba_ref, wbp_s, wba_s):
    @pl.when(pl.program_id(0) == 0)
    def _():
        for j in range(N_DEV):
            wbp_s[:, LANES * j:LANES * (j + 1)] = wbp_ref[j]
            wba_s[:, LANES * j:LANES * (j + 1)] = wba_ref[j]


def _mix_fwd_call(yp, ya, g, x, wbp, wba, wout, g2, g3, rider=None):
    T = x.shape[0]
    tm = _tile(T, 512)

    def body(yp_ref, ya_ref, g_ref, x_ref, wbp_ref, wba_ref, wout_ref, g2_ref, g3_ref,
             mix_ref, x1_ref, h2_ref, wbp_s, wba_s):
        _branch_weights(wbp_ref, wba_ref, wbp_s, wba_s)
        bp = _dot(yp_ref[...], wbp_s[...], NN)
        ba = _dot(ya_ref[...], wba_s[...], NN)
        merged = g_ref[:, :D_MODEL].astype(F32) * bp + g_ref[:, D_MODEL:].astype(F32) * ba
        mix = _dot(merged.astype(MXU_DTYPE), wout_ref[...], NN)
        mix_ref[...] = mix
        x1 = x_ref[...] + (mix * _rms_r(mix)) * g2_ref[...]
        x1_ref[...] = x1
        h2_ref[...] = ((x1 * _rms_r(x1)) * g3_ref[...]).astype(MXU_DTYPE)

    tok = lambda w: pl.BlockSpec((tm, w), lambda i: (i, 0))
    full = lambda a: pl.BlockSpec(a.shape, lambda i: (0,) * a.ndim)
    return _launch(
        body, [yp, ya, g, x, wbp, wba, wout, g2, g3], name="mix_fwd", grid=(T // tm,),
        in_specs=[tok(POOL_WIDTH), tok(ATTN_WIDTH), tok(GATE_WIDTH), tok(D_MODEL), full(wbp), full(wba), full(wout),
                  full(g2), full(g3)],
        out_specs=[tok(D_MODEL), tok(D_MODEL), tok(D_MODEL)],
        out_shape=[jax.ShapeDtypeStruct((T, D_MODEL), F32), jax.ShapeDtypeStruct((T, D_MODEL), F32),
                   jax.ShapeDtypeStruct((T, D_MODEL), MXU_DTYPE)],
        scratch_shapes=[pltpu.VMEM((POOL_WIDTH, D_MODEL), MXU_DTYPE), pltpu.VMEM((ATTN_WIDTH, D_MODEL), MXU_DTYPE)],
        sem=("arbitrary",), rider=rider)


def _mix_bwd_call(dx1, mix, yp, ya, g, wbp, wba, wout, g2, rider=None):
    T = dx1.shape[0]
    tm = _tile(T, 512)

    def body(dx1_ref, mix_ref, yp_ref, ya_ref, g_ref, wbp_ref, wba_ref, wout_ref, g2_ref,
             dmix_ref, merged_ref, dbp_ref, dba_ref, dyp_ref, do_ref, dgates_ref, dg2_ref, dbg_ref, wbp_s, wba_s):
        _branch_weights(wbp_ref, wba_ref, wbp_s, wba_s)

        @pl.when(pl.program_id(0) == 0)
        def _():
            dg2_ref[...] = jnp.zeros_like(dg2_ref)
            dbg_ref[...] = jnp.zeros_like(dbg_ref)

        mix = mix_ref[...]
        dmix, dg2 = _rms_bwd(dx1_ref[...], mix, _rms_r(mix), g2_ref[...])
        dg2_ref[...] += jnp.sum(dg2, axis=0, keepdims=True)
        dmixb = dmix.astype(MXU_DTYPE)
        dmix_ref[...] = dmixb
        dmerged = _dot(dmixb, wout_ref[...], NT)
        bp = _dot(yp_ref[...], wbp_s[...], NN)
        ba = _dot(ya_ref[...], wba_s[...], NN)
        gp, ga = g_ref[:, :D_MODEL].astype(F32), g_ref[:, D_MODEL:].astype(F32)
        merged_ref[...] = (gp * bp + ga * ba).astype(MXU_DTYPE)
        dgp = dmerged * bp * (gp * (1.0 - gp))
        dga = dmerged * ba * (ga * (1.0 - ga))
        dbg_ref[:, :D_MODEL] += jnp.sum(dgp, axis=0, keepdims=True)
        dbg_ref[:, D_MODEL:] += jnp.sum(dga, axis=0, keepdims=True)
        dgates_ref[:, :D_MODEL] = dgp.astype(MXU_DTYPE)
        dgates_ref[:, D_MODEL:] = dga.astype(MXU_DTYPE)
        dbp = (dmerged * gp).astype(MXU_DTYPE)
        dba = (dmerged * ga).astype(MXU_DTYPE)
        dbp_ref[...] = dbp
        dba_ref[...] = dba
        dyp_ref[...] = _dot(dbp, wbp_s[...], NT)
        do_ref[...] = _dot(dba, wba_s[...], NT).astype(MXU_DTYPE)

    tok = lambda w: pl.BlockSpec((tm, w), lambda i: (i, 0))
    full = lambda a: pl.BlockSpec(a.shape, lambda i: (0,) * a.ndim)
    acc = lambda w: pl.BlockSpec((1, w), lambda i: (0, 0))
    sd = jax.ShapeDtypeStruct
    return _launch(
        body, [dx1, mix, yp, ya, g, wbp, wba, wout, g2], name="mix_bwd", grid=(T // tm,),
        in_specs=[tok(D_MODEL), tok(D_MODEL), tok(POOL_WIDTH), tok(ATTN_WIDTH), tok(GATE_WIDTH), full(wbp), full(wba),
                  full(wout), full(g2)],
        out_specs=[tok(D_MODEL), tok(D_MODEL), tok(D_MODEL), tok(D_MODEL), tok(POOL_WIDTH), tok(ATTN_WIDTH),
                   tok(GATE_WIDTH), acc(D_MODEL), acc(GATE_WIDTH)],
        out_shape=[sd((T, D_MODEL), MXU_DTYPE), sd((T, D_MODEL), MXU_DTYPE), sd((T, D_MODEL), MXU_DTYPE),
                   sd((T, D_MODEL), MXU_DTYPE), sd((T, POOL_WIDTH), F32), sd((T, ATTN_WIDTH), MXU_DTYPE),
                   sd((T, GATE_WIDTH), MXU_DTYPE), sd((1, D_MODEL), F32), sd((1, GATE_WIDTH), F32)],
        scratch_shapes=[pltpu.VMEM((POOL_WIDTH, D_MODEL), MXU_DTYPE), pltpu.VMEM((ATTN_WIDTH, D_MODEL), MXU_DTYPE)],
        sem=("arbitrary",), rider=rider)


def _mlp_call(x1, h2, target, wup, wdown, g3, g4):
    T = x1.shape[0]
    tm = _tile(T, 256)
    fc = D_FF // N_DEV

    def body(x1_ref, h2_ref, t_ref, wup_ref, wdown_ref, g3_ref, g4_ref,
             act_ref, da_ref, dff_ref, dx1_ref, dg3_ref, dg4_ref, loss_ref, rl_s):
        @pl.when(pl.program_id(0) == 0)
        def _():
            dg3_ref[...] = jnp.zeros_like(dg3_ref)
            dg4_ref[...] = jnp.zeros_like(dg4_ref)
            loss_ref[...] = jnp.zeros_like(loss_ref)

        h2 = h2_ref[...]
        ff = jnp.zeros((tm, D_MODEL), F32)
        for j in range(N_DEV):
            sl = slice(fc * j, fc * (j + 1))
            rl = jnp.maximum(_dot(h2, wup_ref[j], NN), 0.0)
            rl_s[:, sl] = rl
            actb = (rl * rl).astype(MXU_DTYPE)
            act_ref[:, sl] = actb
            ff = ff + _dot(actb, wdown_ref[j], NN)
        x1 = x1_ref[...]
        r4 = _rms_r(ff)
        err = x1 + (ff * r4) * g4_ref[...] - t_ref[...]
        loss_ref[...] += jnp.sum(err * err, axis=0, keepdims=True)
        dy = err * (1.0 / D_MODEL)
        dff, dg4 = _rms_bwd(dy, ff, r4, g4_ref[...])
        dg4_ref[...] += jnp.sum(dg4, axis=0, keepdims=True)
        dffb = dff.astype(MXU_DTYPE)
        dff_ref[...] = dffb
        dh2 = jnp.zeros((tm, D_MODEL), F32)
        for j in range(N_DEV):
            sl = slice(fc * j, fc * (j + 1))
            dab = (_dot(dffb, wdown_ref[j], NT) * (2.0 * rl_s[:, sl])).astype(MXU_DTYPE)
            da_ref[:, sl] = dab
            dh2 = dh2 + _dot(dab, wup_ref[j], NT)
        dx1, dg3 = _rms_bwd(dh2, x1, _rms_r(x1), g3_ref[...])
        dg3_ref[...] += jnp.sum(dg3, axis=0, keepdims=True)
        dx1_ref[...] = dy + dx1

    tok = lambda w: pl.BlockSpec((tm, w), lambda i: (i, 0))
    full = lambda a: pl.BlockSpec(a.shape, lambda i: (0,) * a.ndim, pipeline_mode=pl.Buffered(1))
    vec = pl.BlockSpec((1, D_MODEL), lambda i: (0, 0))
    sd = jax.ShapeDtypeStruct
    return pl.pallas_call(
        body, name="mlp_fwd_bwd", grid=(T // tm,),
        in_specs=[tok(D_MODEL), tok(D_MODEL), tok(D_MODEL), full(wup), full(wdown), vec, vec],
        out_specs=[tok(D_FF), tok(D_FF), tok(D_MODEL), tok(D_MODEL), vec, vec, vec],
        out_shape=[sd((T, D_FF), MXU_DTYPE), sd((T, D_FF), MXU_DTYPE), sd((T, D_MODEL), MXU_DTYPE),
                   sd((T, D_MODEL), F32), sd((1, D_MODEL), F32), sd((1, D_MODEL), F32), sd((1, D_MODEL), F32)],
        scratch_shapes=[pltpu.VMEM((tm, D_FF), F32)],
        compiler_params=_params(("arbitrary",)),
    )(x1, h2, target, wup, wdown, g3, g4)


def _inproj_bwd_call(du, dq, dk, dv, dgates, dx1, x, win_t, g1, rider=None):
    T = x.shape[0]
    tm = _tile(T, 512)

    def body(du_ref, dq_ref, dk_ref, dv_ref, dgt_ref, dx1_ref, x_ref, w_ref, g1_ref, gx_ref, dg1_ref, db_ref):
        @pl.when(pl.program_id(0) == 0)
        def _():
            dg1_ref[...] = jnp.zeros_like(dg1_ref)
            db_ref[...] = jnp.zeros_like(db_ref)

        dh = jnp.zeros((tm, D_MODEL), F32)
        for ref, lo, hi in ((du_ref, 0, C_Q), (dq_ref, C_Q, C_K), (dk_ref, C_K, C_V), (dv_ref, C_V, C_G),
                            (dgt_ref, C_G, IN_WIDTH)):
            piece = ref[...]
            dh = dh + _dot(piece, w_ref[lo:hi, :], NN)
            if hi <= C_G:
                db_ref[:, lo:hi] += jnp.sum(piece.astype(F32), axis=0, keepdims=True)
        xv = x_ref[...]
        dx, dg1 = _rms_bwd(dh, xv, _rms_r(xv), g1_ref[...])
        dg1_ref[...] += jnp.sum(dg1, axis=0, keepdims=True)
        gx_ref[...] = dx1_ref[...] + dx

    tok = lambda w: pl.BlockSpec((tm, w), lambda i: (i, 0))
    full = lambda a: pl.BlockSpec(a.shape, lambda i: (0,) * a.ndim)
    sd = jax.ShapeDtypeStruct
    return _launch(
        body, [du, dq, dk, dv, dgates, dx1, x, win_t, g1], name="inproj_bwd", grid=(T // tm,),
        in_specs=[tok(POOL_WIDTH), tok(ATTN_WIDTH), tok(KV_WIDTH), tok(KV_WIDTH), tok(GATE_WIDTH), tok(D_MODEL),
                  tok(D_MODEL), full(win_t), full(g1)],
        out_specs=[tok(D_MODEL), pl.BlockSpec((1, D_MODEL), lambda i: (0, 0)), pl.BlockSpec((1, C_G), lambda i: (0, 0))],
        out_shape=[sd((T, D_MODEL), F32), sd((1, D_MODEL), F32), sd((1, C_G), F32)],
        sem=("arbitrary",), rider=rider)


WGRAD_TOKENS = 1024


def _wgrad_rows_call(a, b, name, rider=None):
    T, K = a.shape
    N = b.shape[1]
    tm = _tile(T, WGRAD_TOKENS)
    kb = min(K, 1024)
    per = kb // (K // N_DEV)

    def body(a_ref, b_ref, o_ref):
        @pl.when(pl.program_id(1) == 0)
        def _():
            o_ref[...] = jnp.zeros_like(o_ref)

        d = _dot(a_ref[...], b_ref[...], TN)
        rs = kb // per
        for j in range(per):
            o_ref[j] += d[rs * j:rs * (j + 1)]

    return _launch(
        body, [a, b], name=name, grid=(K // kb, T // tm),
        in_specs=[pl.BlockSpec((tm, kb), lambda i, t: (t, i)), pl.BlockSpec((tm, N), lambda i, t: (t, 0))],
        out_specs=[pl.BlockSpec((per, K // N_DEV, N), lambda i, t: (i, 0, 0))],
        out_shape=[jax.ShapeDtypeStruct((N_DEV, K // N_DEV, N), F32)],
        sem=("arbitrary", "arbitrary"), rider=rider)


def _wgrad_cols_call(pairs, name, rider=None):
    T, K = pairs[0][0].shape
    N = pairs[0][1].shape[1]
    assert all(a.shape == (T, K) and b.shape == (T, N) for a, b in pairs)
    n = len(pairs)
    tm = _tile(T, WGRAD_TOKENS)
    nb = min(N, 1024)
    per = nb // (N // N_DEV)

    def body(*refs):
        for t in range(n):
            a_ref, b_ref, o_ref = refs[2 * t], refs[2 * t + 1], refs[2 * n + t]

            @pl.when(pl.program_id(1) == 0)
            def _():
                o_ref[...] = jnp.zeros_like(o_ref)

            d = _dot(a_ref[...], b_ref[...], TN)
            cs = nb // per
            for j in range(per):
                o_ref[j] += d[:, cs * j:cs * (j + 1)]

    return _launch(
        body, [m for ab in pairs for m in ab], name=name, grid=(N // nb, T // tm),
        in_specs=[pl.BlockSpec((tm, K), lambda i, t: (t, 0)), pl.BlockSpec((tm, nb), lambda i, t: (t, i))] * n,
        out_specs=[pl.BlockSpec((per, K, N // N_DEV), lambda i, t: (i, 0, 0))] * n,
        out_shape=[jax.ShapeDtypeStruct((N_DEV, K, N // N_DEV), F32)] * n,
        sem=("arbitrary", "arbitrary"), rider=rider)


def _wgrad_in_call(du, dq, dk, dv, dgates, h, rider=None):
    T = h.shape[0]
    tm = _tile(T, WGRAD_TOKENS)
    rows = IN_WIDTH // N_DEV

    def body(du_ref, dq_ref, dk_ref, dv_ref, dgt_ref, h_ref, o_ref, acc, sem):
        t = pl.program_id(0)

        @pl.when(t == 0)
        def _():
            acc[...] = jnp.zeros_like(acc)

        hv = h_ref[...]
        for ref, lo, hi in ((du_ref, 0, C_Q), (dq_ref, C_Q, C_K), (dk_ref, C_K, C_V), (dv_ref, C_V, C_G),
                            (dgt_ref, C_G, IN_WIDTH)):
            acc[lo:hi, :] += _dot(ref[...], hv, TN)

        @pl.when(t == pl.num_programs(0) - 1)
        def _():
            copies = [pltpu.make_async_copy(acc.at[pl.ds(rows * j, rows), :], o_ref.at[j], sem.at[j])
                      for j in range(N_DEV)]
            for cp in copies:
                cp.start()
            for cp in copies:
                cp.wait()

    tok = lambda w: pl.BlockSpec((tm, w), lambda t: (t, 0))
    return _launch(
        body, [du, dq, dk, dv, dgates, h], name="wgrad_in", grid=(T // tm,),
        in_specs=[tok(POOL_WIDTH), tok(ATTN_WIDTH), tok(KV_WIDTH), tok(KV_WIDTH), tok(GATE_WIDTH), tok(D_MODEL)],
        out_specs=[pl.BlockSpec(memory_space=pl.ANY)],
        out_shape=[jax.ShapeDtypeStruct((N_DEV, rows, D_MODEL), F32)],
        scratch_shapes=[pltpu.VMEM((IN_WIDTH, D_MODEL), F32), pltpu.SemaphoreType.DMA((N_DEV,))],
        sem=("arbitrary",), rider=rider)


def _coords():
    return lax.axis_index("x"), lax.axis_index("y"), lax.axis_index("c")


def _allgather_call(shards):
    n = len(shards)

    def body(*refs):
        ins, outs = refs[:n], refs[n:2 * n]
        send_sems, recv_sems, local_sems = refs[2 * n:]
        x, y, c = _coords()
        me, sibling = (x, y, c), (x, y, 1 - c)
        chips = [(1 - x, y), (x, 1 - y), (1 - x, 1 - y)]

        def slot(p):
            return 4 * p[0] + 2 * p[1] + p[2]

        def copy(t, k, block, to, src=None):
            dst = outs[t].at[slot(block)]
            return pltpu.make_async_remote_copy(
                src_ref=dst if src is None else src, dst_ref=dst, send_sem=send_sems.at[t, k],
                recv_sem=recv_sems.at[t, k], device_id=to, device_id_type=MESH)

        mine = [pltpu.make_async_copy(ins[t], outs[t].at[slot(me)], local_sems.at[t]) for t in range(n)]
        for cp in mine:
            cp.start()
        first = []
        for t in range(n):
            first.append(copy(t, 0, me, sibling, src=ins[t]))
            first += [copy(t, 1 + j, me, (*chip, c), src=ins[t]) for j, chip in enumerate(chips)]
        for cp in first:
            cp.start()
        passed = []
        for t in range(n):
            for j, chip in enumerate(chips):
                copy(t, 1 + j, (*chip, c), me).wait_recv()
                fwd = copy(t, 4 + j, (*chip, c), sibling)
                fwd.start()
                passed.append(fwd)
        for t in range(n):
            copy(t, 0, sibling, me).wait_recv()
            for j, chip in enumerate(chips):
                copy(t, 4 + j, (*chip, 1 - c), me).wait_recv()
        for cp in first + passed:
            cp.wait_send()
        for cp in mine:
            cp.wait()

    hbm = pl.BlockSpec(memory_space=pl.ANY)
    return pl.pallas_call(
        body, name="allgather_weights",
        in_specs=[hbm] * n, out_specs=[hbm] * n,
        out_shape=[jax.ShapeDtypeStruct((N_DEV,) + s.shape, s.dtype) for s in shards],
        scratch_shapes=[pltpu.SemaphoreType.DMA((n, 7)), pltpu.SemaphoreType.DMA((n, 7)), pltpu.SemaphoreType.DMA((n,))],
    )(*shards)


def _slot(p):
    return 4 * p[0] + 2 * p[1] + p[2]


def _rows(ref, span):
    return ref if span is None else ref.at[pl.ds(span[0], span[1])]


ALL = "all"
LOCAL = "local"


def _rows(ref, span):
    return ref if span == ALL else ref.at[pl.ds(span[0], span[1])]


def _rider_ag(items):
    ins, out_shape, aliases, where = [], [], {}, []
    n_remote = n_local = 0
    for t, (shard, buf, snd, fwd) in enumerate(items):
        i_shard = i_buf = None
        if snd is not None:
            i_shard = len(ins)
            ins.append(shard)
        if buf is not None:
            i_buf = len(ins)
            ins.append(buf)
            aliases[i_buf] = t
            out_shape.append(jax.ShapeDtypeStruct(buf.shape, buf.dtype))
        else:
            assert fwd is None and snd is not None
            out_shape.append(jax.ShapeDtypeStruct((N_DEV,) + shard.shape, shard.dtype))
        where.append((i_shard, i_buf, n_remote, n_local))
        n_remote += (4 if snd not in (None, LOCAL) else 0) + (3 if fwd is not None else 0)
        n_local += 1 if snd is not None else 0

    def plan(rins, routs, send, recv, loc, r0, l0):
        x, y, c = _coords()
        peers = [(x, y, 1 - c), (1 - x, y, c), (x, 1 - y, c), (1 - x, 1 - y, c)]
        remote, local = [], []
        for t, (shard, buf, snd, fwd) in enumerate(items):
            i_shard, i_buf, k, l = where[t]
            k, l = r0 + k, l0 + l
            if snd is not None:
                span = ALL if snd == LOCAL else snd
                src, dst = _rows(rins[i_shard], span), _rows(routs[t].at[_slot((x, y, c))], span)
                local.append(pltpu.make_async_copy(src, dst, loc.at[l]))
                for peer in (peers if snd != LOCAL else []):
                    remote.append(pltpu.make_async_remote_copy(
                        src_ref=src, dst_ref=dst, send_sem=send.at[k], recv_sem=recv.at[k],
                        device_id=peer, device_id_type=MESH))
                    k += 1
            if fwd is not None:
                for px, py, pc in peers[1:]:
                    s = _slot((px, py, pc))
                    remote.append(pltpu.make_async_remote_copy(
                        src_ref=_rows(rins[i_buf].at[s], fwd), dst_ref=_rows(routs[t].at[s], fwd),
                        send_sem=send.at[k], recv_sem=recv.at[k], device_id=peers[0], device_id_type=MESH))
                    k += 1
        return remote, local

    return _Rider(ins, out_shape, n_remote, n_local, plan, aliases)


def _rider_ag_remote(shards):
    n = len(shards)

    def plan(ins, outs, send, recv, loc, r0, l0):
        x, y, c = _coords()
        remote = []
        for t in range(n):
            dst = outs[t].at[_slot((x, y, c))]
            for k, peer in enumerate([(x, y, 1 - c), (1 - x, y, c), (x, 1 - y, c), (1 - x, 1 - y, c)]):
                remote.append(pltpu.make_async_remote_copy(
                    src_ref=ins[t], dst_ref=dst, send_sem=send.at[r0 + 4 * t + k], recv_sem=recv.at[r0 + 4 * t + k],
                    device_id=peer, device_id_type=MESH))
        return remote, []

    return _Rider(shards, [jax.ShapeDtypeStruct((N_DEV,) + s.shape, s.dtype) for s in shards], 4 * n, 0, plan)


def _rider_rs_sibling(grads):
    n = len(grads)

    def plan(ins, outs, send, recv, loc, r0, l0):
        x, y, c = _coords()
        remote = []
        for t in range(n):
            for q in range(4):
                remote.append(pltpu.make_async_remote_copy(
                    src_ref=ins[t].at[q, 1 - c], dst_ref=outs[t].at[q], send_sem=send.at[r0 + 4 * t + q],
                    recv_sem=recv.at[r0 + 4 * t + q], device_id=(x, y, 1 - c), device_id_type=MESH))
        return remote, []

    return _Rider(grads, [jax.ShapeDtypeStruct((4,) + g.shape[2:], g.dtype) for g in grads], 4 * n, 0, plan)


def _rider_rs_chips(sums, rows=None, into=None):
    n = len(sums)
    rows = rows or [ALL] * n

    def plan(ins, outs, send, recv, loc, r0, l0):
        x, y, c = _coords()
        remote = []
        for t in range(n):
            for r, (px, py) in enumerate([(1 - x, y), (x, 1 - y), (1 - x, 1 - y)]):
                remote.append(pltpu.make_async_remote_copy(
                    src_ref=_rows(ins[t].at[2 * px + py], rows[t]), dst_ref=_rows(outs[t].at[r], rows[t]),
                    send_sem=send.at[r0 + 3 * t + r], recv_sem=recv.at[r0 + 3 * t + r],
                    device_id=(px, py, c), device_id_type=MESH))
        return remote, []

    out_shape = [jax.ShapeDtypeStruct((3,) + s.shape[1:], s.dtype) for s in sums]
    if into is None:
        return _Rider(sums, out_shape, 3 * n, 0, plan)
    return _Rider(list(sums) + list(into), out_shape, 3 * n, 0, plan, aliases={n + t: t for t in range(n)})


def _rider_gather_direct(parts):
    n = len(parts)

    def plan(ins, outs, send, recv, loc, r0, l0):
        x, y, c = _coords()
        me = _slot((x, y, c))
        remote, local = [], []
        for t in range(n):
            local.append(pltpu.make_async_copy(ins[t], outs[t].at[me], loc.at[l0 + t]))
            for k in range(1, N_DEV):
                peer = (x ^ ((k >> 2) & 1), y ^ ((k >> 1) & 1), c ^ (k & 1))
                remote.append(pltpu.make_async_remote_copy(
                    src_ref=ins[t], dst_ref=outs[t].at[me], send_sem=send.at[r0 + 7 * t + k - 1],
                    recv_sem=recv.at[r0 + 7 * t + k - 1], device_id=peer, device_id_type=MESH))
        return remote, local

    return _Rider(parts, [jax.ShapeDtypeStruct((N_DEV,) + p.shape, p.dtype) for p in parts], 7 * n, n, plan)


def _chip_sum_call(cidx, grads, recvd, out_dtypes, name):
    n = len(grads)

    def body(c_ref, *refs):
        for t in range(n):
            refs[2 * n + t][0] = (refs[t][0, 0] + refs[n + t][0]).astype(out_dtypes[t])

    in_specs = [pl.BlockSpec((1, 1) + g.shape[2:], lambda q, c_ref: (q, c_ref[0], 0, 0)) for g in grads]
    in_specs += [pl.BlockSpec((1,) + r.shape[1:], lambda q, c_ref: (q, 0, 0)) for r in recvd]
    return pl.pallas_call(
        body, name=name,
        grid_spec=pltpu.PrefetchScalarGridSpec(
            num_scalar_prefetch=1, grid=(4,), in_specs=in_specs,
            out_specs=[pl.BlockSpec((1,) + r.shape[1:], lambda q, c_ref: (q, 0, 0)) for r in recvd]),
        out_shape=[jax.ShapeDtypeStruct(r.shape, dt) for r, dt in zip(recvd, out_dtypes)],
        compiler_params=_params(("arbitrary",)),
    )(cidx, *grads, *recvd)


def _final_sum_call(idx, grads, recvd1, recvd2):
    n = len(grads)
    nsteps = 2

    def body(i_ref, *refs):
        for t in range(n):
            g, r1, r2, o = refs[t], refs[n + t], refs[2 * n + t], refs[3 * n + t]
            s = g[0, 0] + r1[0]
            for r in range(3):
                s = s + r2[r].astype(F32)
            o[...] = s

    def rows(a):
        r = a.shape[-2]
        return r // nsteps if (r // nsteps) % 16 == 0 else r

    def step(a):
        return (lambda i: i) if rows(a) != a.shape[-2] else (lambda i: 0)

    in_specs = [pl.BlockSpec((1, 1, rows(g), g.shape[3]), lambda i, s, st=step(g): (s[0], s[1], st(i), 0)) for g in grads]
    in_specs += [pl.BlockSpec((1, rows(r), r.shape[2]), lambda i, s, st=step(r): (s[0], st(i), 0)) for r in recvd1]
    in_specs += [pl.BlockSpec((3, rows(r), r.shape[2]), lambda i, s, st=step(r): (0, st(i), 0)) for r in recvd2]
    return pl.pallas_call(
        body, name="rs_final_sum",
        grid_spec=pltpu.PrefetchScalarGridSpec(
            num_scalar_prefetch=1, grid=(nsteps,), in_specs=in_specs,
            out_specs=[pl.BlockSpec((rows(r), r.shape[2]), lambda i, s, st=step(r): (st(i), 0)) for r in recvd2]),
        out_shape=[jax.ShapeDtypeStruct(r.shape[1:], F32) for r in recvd2],
        compiler_params=_params(("arbitrary",)),
    )(idx, *grads, *recvd1, *recvd2)


def _sum8_call(parts):
    def body(p_ref, o_ref):
        s = p_ref[0]
        for j in range(1, N_DEV):
            s = s + p_ref[j]
        o_ref[...] = s

    return pl.pallas_call(body, name="sum_small_partials",
                          out_shape=jax.ShapeDtypeStruct(parts.shape[1:], parts.dtype))(parts)


def _adamw(w, g, m, v):
    m = ADAM_B1 * m + (1.0 - ADAM_B1) * g
    v = ADAM_B2 * v + (1.0 - ADAM_B2) * (g * g)
    m_hat = m / (1.0 - ADAM_B1 ** ADAM_STEP)
    v_hat = v / (1.0 - ADAM_B2 ** ADAM_STEP)
    delta = -ADAM_LR * (m_hat / (jnp.sqrt(v_hat) + ADAM_EPS) + ADAM_WD * w)
    return delta, m, v


def _adamw_call(ws, gs, ms, vs, nsteps, name):
    n = len(ws)

    def body(*refs):
        for t in range(n):
            w, g, m, v = (refs[k * n + t][...] for k in range(4))
            d, m2, v2 = _adamw(w, g, m, v)
            refs[4 * n + t][...] = d
            refs[5 * n + t][...] = m2
            refs[6 * n + t][...] = v2

    def spec(a):
        assert a.shape[0] % nsteps == 0 and (nsteps == 1 or (a.shape[0] // nsteps) % 8 == 0), a.shape
        return pl.BlockSpec((a.shape[0] // nsteps, a.shape[1]), lambda i: (i, 0))

    specs = [spec(a) for a in ws]
    outs = pl.pallas_call(
        body, name=name, grid=(nsteps,),
        in_specs=specs * 4, out_specs=specs * 3,
        out_shape=[jax.ShapeDtypeStruct(a.shape, F32) for a in ws] * 3,
        compiler_params=_params(("arbitrary",)),
    )(*ws, *gs, *ms, *vs)
    return outs[:n], outs[n:2 * n], outs[2 * n:]


def _adamw_rs_call(idx, gws, r1s, r2s, ws, ms, vs, nsteps, name):
    n = len(ws)

    def body(i_ref, *refs):
        for t in range(n):
            gw, r1, r2, w, m, v = (refs[k * n + t] for k in range(6))
            g = gw[0, 0] + r1[0]
            for r in range(3):
                g = g + r2[r].astype(F32)
            d, m2, v2 = _adamw(w[...], g, m[...], v[...])
            refs[6 * n + t][...] = g
            refs[7 * n + t][...] = d
            refs[8 * n + t][...] = m2
            refs[9 * n + t][...] = v2

    def rb(a):
        r = a.shape[0] // nsteps
        assert a.shape[0] % nsteps == 0 and r % 16 == 0, a.shape
        return r

    in_specs = [pl.BlockSpec((1, 1, rb(w), w.shape[1]), lambda i, s: (s[0], s[1], i, 0)) for w in ws]
    in_specs += [pl.BlockSpec((1, rb(w), w.shape[1]), lambda i, s: (s[0], i, 0)) for w in ws]
    in_specs += [pl.BlockSpec((3, rb(w), w.shape[1]), lambda i, s: (0, i, 0)) for w in ws]
    plain = [pl.BlockSpec((rb(w), w.shape[1]), lambda i, s: (i, 0)) for w in ws]
    outs = pl.pallas_call(
        body, name=name,
        grid_spec=pltpu.PrefetchScalarGridSpec(num_scalar_prefetch=1, grid=(nsteps,), in_specs=in_specs + plain * 3,
                                               out_specs=plain * 4),
        out_shape=[jax.ShapeDtypeStruct(w.shape, F32) for w in ws] * 4,
        compiler_params=_params(("arbitrary",)),
    )(idx, *gws, *r1s, *r2s, *ws, *ms, *vs)
    return outs[:n], outs[n:2 * n], outs[2 * n:3 * n], outs[3 * n:]


def _rows128(a, pad_rows):
    flat = a.reshape(-1).astype(F32)
    flat = jnp.pad(flat, (0, pad_rows * LANES - flat.shape[0]))
    return flat.reshape(pad_rows, LANES)


_SMALL_A = (("w_pool", 512), ("pool_scale", 8), ("attn_sinks", 8), ("g_mix_post", 8), ("g_mlp_pre", 8),
            ("g_mlp_post", 8), ("loss", 8), ("b_in_gates", 16))
_SMALL_A_ROWS = 640
_SMALL_B = (("g_mix_pre", 8), ("b_in_head", 16))


def _pack(parts, layout, total_rows):
    rows = [_rows128(parts[k], r) for k, r in layout]
    pad = total_rows - sum(r for _, r in layout)
    if pad:
        rows.append(jnp.zeros((pad, LANES), F32))
    return jnp.concatenate(rows, axis=0)


def _unpack(buf, layout, sizes):
    out, off = {}, 0
    for k, r in layout:
        out[k] = buf[off:off + r].reshape(-1)[:sizes[k]]
        off += r
    return out


def kernel(x, g_mix_pre, w_in, b_in, w_pool, pool_scale, attn_sinks, w_branch_pool, w_branch_attn, w_out, g_mix_post, g_mlp_pre, w_up, w_down, g_mlp_post, loss_target, m_g_mix_pre, m_w_in, m_b_in, m_w_pool, m_pool_scale, m_attn_sinks, m_w_branch_pool, m_w_branch_attn, m_w_out, m_g_mix_post, m_g_mlp_pre, m_w_up, m_w_down, m_g_mlp_post, v_g_mix_pre, v_w_in, v_b_in, v_w_pool, v_pool_scale, v_attn_sinks, v_w_branch_pool, v_w_branch_attn, v_w_out, v_g_mix_post, v_g_mlp_pre, v_w_up, v_w_down, v_g_mlp_post):
    B, S, _ = x.shape
    T = B * S
    xt = x.reshape(T, D_MODEL)
    tgt = loss_target.reshape(T, D_MODEL)
    cx, cy, cc = _coords()

    cidx = jnp.reshape(cc, (1,)).astype(jnp.int32)
    by_chip = lambda gr: gr.reshape((4, 2) + gr.shape[1:])
    bf = lambda w: w[0].astype(MXU_DTYPE)

    win_l, wbp_l, wba_l, wout_l = w_in[0].T.astype(MXU_DTYPE), bf(w_branch_pool), bf(w_branch_attn), bf(w_out)
    wup_l, wdown_l = bf(w_up), bf(w_down)
    (c_in, c_br, c_up, c_dn), tok = _copies_start(
        [_rider_ag_remote([win_l]), _rider_ag_remote([wbp_l, wba_l, wout_l]), _rider_ag_remote([wup_l]),
         _rider_ag_remote([wdown_l])], "allgather_start")
    (win_1,) = _copies_wait([c_in], tok, "allgather_wait_in")
    (win_s,) = _comm_call(_rider_ag([(win_l, win_1, LOCAL, ALL)]), "allgather_pass_in")
    win_t = win_s.reshape(IN_WIDTH, D_MODEL)
    wpool_b = bf(w_pool)
    rc, rsa, rsb = _rot_tables(S)

    h, u, q, k4, v4, g = _inproj_call(xt, g_mix_pre, win_t, b_in, rc, rsa, rsb, S)
    yp = _pool_call(u, wpool_b, pool_scale, S)
    wbp_1, wba_1, wout_1 = _copies_wait([c_br], h, "allgather_wait_branch")
    (ya,), (wbp_s, wba_s, wout_s) = _attn_call(
        attn_sinks, q, k4, v4, S,
        rider=_rider_ag([(wbp_l, wbp_1, LOCAL, ALL), (wba_l, wba_1, LOCAL, ALL), (wout_l, wout_1, LOCAL, ALL)]))
    wout_f = wout_s.reshape(D_MODEL, D_MODEL)
    (wup_1,) = _copies_wait([c_up], ya, "allgather_wait_up")
    (mix, x1, h2), (wup_s,) = _mix_fwd_call(
        yp, ya, g, xt, wbp_s, wba_s, wout_f, g_mix_post, g_mlp_pre, rider=_rider_ag([(wup_l, wup_1, LOCAL, ALL)]))
    (wdown_1,) = _copies_wait([c_dn], h2, "allgather_wait_down")
    (wdown_s,) = _comm_call(_rider_ag([(wdown_l, wdown_1, LOCAL, ALL)]), "allgather_pass_down")

    act, da, dff, dx1, dg3, dg4, lossvec = _mlp_call(x1, h2, tgt, wup_s, wdown_s, g_mlp_pre, g_mlp_post)
    gw_down = by_chip(_wgrad_rows_call(act, dff, "wgrad_down")[0])
    (gw_up,), (r1_down,) = _wgrad_cols_call([(h2, da)], "wgrad_up", rider=_rider_rs_sibling([gw_down]))
    gw_up = by_chip(gw_up)
    (s_down,) = _chip_sum_call(cidx, [gw_down], [r1_down], [MXU_DTYPE], "rs_chip_sum_down")
    (c_down,), tok = _copies_start([_rider_rs_chips([s_down])], "rs_chips_start_down")
    (dmix, merged, dbp, dba, dyp, do, dgates, dg2, dbg), (r1_up,) = _mix_bwd_call(
        dx1, mix, yp, ya, g, wbp_s, wba_s, wout_f, g_mix_post, rider=_after(tok, _rider_rs_sibling([gw_up])))
    (s_up,) = _chip_sum_call(cidx, [gw_up], [r1_up], [MXU_DTYPE], "rs_chip_sum_up")
    (c_up,), tok = _copies_start([_rider_rs_chips([s_up])], "rs_chips_start_up")
    (dq, dk, dv, dsink), _ = _attn_bwd_call(attn_sinks, q, k4, v4, do, rc, rsa, rsb, S, rider=_after(tok))
    gw_out = by_chip(_wgrad_rows_call(merged, dmix, "wgrad_out")[0])
    gw_bp, gw_ba = _wgrad_cols_call([(yp, dbp), (ya, dba)], "wgrad_branch")
    gw_bp, gw_ba = by_chip(gw_bp), by_chip(gw_ba)
    (du, dwp, dps), (r1_out, r1_bp, r1_ba) = _pool_bwd_call(
        u, dyp, wpool_b, pool_scale, S, rider=_rider_rs_sibling([gw_out, gw_bp, gw_ba]))
    s_obb = _chip_sum_call(cidx, [gw_out, gw_bp, gw_ba], [r1_out, r1_bp, r1_ba], [MXU_DTYPE] * 3, "rs_chip_sum_branch")
    (c_obb,), tok = _copies_start([_rider_rs_chips(s_obb)], "rs_chips_start_branch")
    (gw_in,), _ = _wgrad_in_call(du, dq, dk, dv, dgates, h, rider=_after(tok))
    gw_in = by_chip(gw_in)

    small_a = {"w_pool": dwp, "pool_scale": dps,
               "attn_sinks": jnp.sum(dsink.reshape(B, 8, LANES)[:, 0, :N_Q_HEADS], axis=0), "g_mix_post": dg2,
               "g_mlp_pre": dg3, "g_mlp_post": dg4, "loss": lossvec, "b_in_gates": dbg}
    gw_sa = by_chip(_pack(small_a, _SMALL_A, _SMALL_A_ROWS).reshape(N_DEV, _SMALL_A_ROWS // N_DEV, LANES))
    r1_in, r1_sa = _comm_call(_rider_rs_sibling([gw_in, gw_sa]), "rs_sibling_in")
    s_in, s_sa = _chip_sum_call(cidx, [gw_in, gw_sa], [r1_in, r1_sa], [MXU_DTYPE, F32], "rs_chip_sum_in")
    (c_in,), tok = _copies_start([_rider_rs_chips([s_in, s_sa])], "rs_chips_start_in")
    (gx, dg1, dba_in), _ = _inproj_bwd_call(du, dq, dk, dv, dgates, dx1, xt, win_t, g_mix_pre, rider=_after(tok))
    r2_down, r2_up, r2_out, r2_bp, r2_ba, r2_in, r2_sa = _copies_wait([c_down, c_up, c_obb, c_in], dg1, "rs_chips_wait")

    idx = jnp.stack([2 * cx + cy, cc]).astype(jnp.int32)
    (g_sa,) = _final_sum_call(idx, [gw_sa], [r1_sa], [r2_sa])
    part_b = _pack({"g_mix_pre": dg1, "b_in_head": dba_in}, _SMALL_B, sum(r for _, r in _SMALL_B))
    sa_all, sb_all = _comm_call(_rider_gather_direct([g_sa, part_b]), "allgather_small")
    sb_sum = _sum8_call(sb_all)

    in_t = _adamw_rs_call(idx, [gw_in], [r1_in], [r2_in], [w_in[0].T], [m_w_in[0].T], [v_w_in[0].T], 2, "adamw_w_in")
    rest = _adamw_rs_call(
        idx, [gw_bp, gw_ba, gw_out, gw_up, gw_down], [r1_bp, r1_ba, r1_out, r1_up, r1_down],
        [r2_bp, r2_ba, r2_out, r2_up, r2_down], [w_branch_pool[0], w_branch_attn[0], w_out[0], w_up[0], w_down[0]],
        [m_w_branch_pool[0], m_w_branch_attn[0], m_w_out[0], m_w_up[0], m_w_down[0]],
        [v_w_branch_pool[0], v_w_branch_attn[0], v_w_out[0], v_w_up[0], v_w_down[0]], N_DEV, "adamw_shards")
    big_g, big_d, big_m2, big_v2 = ([a[0].T] + list(b) for a, b in zip(in_t, rest))

    names = ["g_mix_pre", "b_in", "w_pool", "pool_scale", "attn_sinks", "g_mix_post", "g_mlp_pre", "g_mlp_post"]
    sm_w = dict(g_mix_pre=g_mix_pre, b_in=b_in, w_pool=w_pool, pool_scale=pool_scale, attn_sinks=attn_sinks,
                g_mix_post=g_mix_post, g_mlp_pre=g_mlp_pre, g_mlp_post=g_mlp_post)
    sm_m = dict(g_mix_pre=m_g_mix_pre, b_in=m_b_in, w_pool=m_w_pool, pool_scale=m_pool_scale, attn_sinks=m_attn_sinks,
                g_mix_post=m_g_mix_post, g_mlp_pre=m_g_mlp_pre, g_mlp_post=m_g_mlp_post)
    sm_v = dict(g_mix_pre=v_g_mix_pre, b_in=v_b_in, w_pool=v_w_pool, pool_scale=v_pool_scale, attn_sinks=v_attn_sinks,
                g_mix_post=v_g_mix_post, g_mlp_pre=v_g_mlp_pre, g_mlp_post=v_g_mlp_post)
    sizes = {k: sm_w[k].size for k in names}
    sizes.update(loss=D_MODEL, b_in_gates=GATE_WIDTH, b_in_head=C_G)
    sm_g = _unpack(sa_all.reshape(_SMALL_A_ROWS, LANES), _SMALL_A, sizes)
    sm_g.update(_unpack(sb_sum, _SMALL_B, sizes))
    sm_g["b_in"] = jnp.concatenate([sm_g["b_in_head"], sm_g["b_in_gates"]])
    loss = (0.5 / D_MODEL) * jnp.sum(sm_g["loss"])
    two_d = lambda a: a.reshape(-1, a.shape[-1])
    sd_, sm2_, sv2_ = _adamw_call([two_d(sm_w[k]) for k in names], [two_d(sm_g[k].reshape(sm_w[k].shape)) for k in names],
                                  [two_d(sm_m[k]) for k in names], [two_d(sm_v[k]) for k in names], 1, "adamw_small")
    like = lambda vals: {k: a.reshape(sm_w[k].shape) for k, a in zip(names, vals)}
    sm_d, sm_m2, sm_v2 = like(sd_), like(sm2_), like(sv2_)
    sm_gr = {k: sm_g[k].reshape(sm_w[k].shape) for k in names}

    order = ["g_mix_pre", "w_in", "b_in", "w_pool", "pool_scale", "attn_sinks", "w_branch_pool", "w_branch_attn",
             "w_out", "g_mix_post", "g_mlp_pre", "w_up", "w_down", "g_mlp_post"]
    big_names = ["w_in", "w_branch_pool", "w_branch_attn", "w_out", "w_up", "w_down"]
    lead = lambda a: a[None]
    tables = []
    for small_t, big_t in ((sm_gr, big_g), (sm_d, big_d), (sm_m2, big_m2), (sm_v2, big_v2)):
        bt = dict(zip(big_names, big_t))
        tables.append([lead(bt[k]) if k in bt else small_t[k] for k in order])
    return (loss, gx.reshape(B, S, D_MODEL), *tables[0], *tables[1], *tables[2], *tables[3])
```

```python
import functools

import jax
import jax.numpy as jnp
from jax import lax
from jax.experimental import pallas as pl
from jax.experimental.pallas import tpu as pltpu

F32 = jnp.float32
MXU_DTYPE = jnp.bfloat16
MESH = pl.DeviceIdType.MESH

D_MODEL = 1024
POOL_WINDOWS = (2, 4, 8, 16)
POOL_WIDTH = 512
POOL_GC = 128
HEAD_DIM = 64
N_Q_HEADS = 8
N_KV_HEADS = 2
GROUP = 4
ATTN_WIDTH = 512
KV_WIDTH = 128
BLOCK = 128
GATE_WIDTH = 2048
IN_WIDTH = 3328
D_FF = 4096
EPS = 1e-6
NEG_INF = -1e30
ROPE_THETA = 500000.0
ROT_DIM = 16
SCALE = HEAD_DIM ** -0.5
C_Q, C_K, C_V, C_G = 512, 1024, 1152, 1280

ADAM_LR = 0.001
ADAM_B1 = 0.9
ADAM_B2 = 0.999
ADAM_EPS = 1e-08
ADAM_WD = 0.01
ADAM_STEP = 10

N_DEV = 8
LANES = 128
VMEM_LIMIT = 56 * 1024 * 1024

NN = (((1,), (0,)), ((), ()))
NT = (((1,), (1,)), ((), ()))
TN = (((0,), (0,)), ((), ()))


def _dot(a, b, dims):
    return lax.dot_general(a, b, dims, preferred_element_type=F32)


def _params(sem=None):
    return pltpu.CompilerParams(dimension_semantics=sem, vmem_limit_bytes=VMEM_LIMIT)


def _tile(n, pref):
    t = min(n, pref)
    assert n % t == 0, (n, t)
    return t


class _Rider:
    def __init__(self, ins, out_shape, n_remote, n_local, plan, aliases=None):
        self.ins, self.out_shape, self.n_remote, self.n_local = list(ins), list(out_shape), n_remote, n_local
        self.plan, self.aliases = plan, dict(aliases or {})


def _after(token, rider=None):
    r = rider or _Rider([], [], 0, 0, lambda ins, outs, send, recv, loc, r0, l0: ([], []))
    return _Rider(r.ins + [token], r.out_shape, r.n_remote, r.n_local, r.plan, r.aliases)


def _merge_riders(a, b):
    na_in, na_out = len(a.ins), len(a.out_shape)

    def plan(ins, outs, send, recv, loc, r0, l0):
        ra, la = a.plan(ins[:na_in], outs[:na_out], send, recv, loc, r0, l0)
        rb, lb = b.plan(ins[na_in:], outs[na_out:], send, recv, loc, r0 + a.n_remote, l0 + a.n_local)
        return ra + rb, la + lb

    aliases = dict(a.aliases)
    aliases.update({na_in + i: na_out + o for i, o in b.aliases.items()})
    return _Rider(a.ins + b.ins, a.out_shape + b.out_shape, a.n_remote + b.n_remote, a.n_local + b.n_local, plan, aliases)


def _launch(body, args, *, name, grid, in_specs, out_specs, out_shape, scratch_shapes=(), sem=None, rider=None):
    if rider is None:
        return pl.pallas_call(body, name=name, grid=grid, in_specs=in_specs, out_specs=out_specs, out_shape=out_shape,
                              scratch_shapes=list(scratch_shapes), compiler_params=_params(sem))(*args)
    n_in, n_out, n_scr = len(args), len(out_shape), len(scratch_shapes)
    r_in, r_out = len(rider.ins), len(rider.out_shape)
    copies = rider.n_remote + rider.n_local > 0

    def wrapped(*refs):
        ins, rins = refs[:n_in], refs[n_in:n_in + r_in]
        o0 = n_in + r_in
        outs, routs = refs[o0:o0 + n_out], refs[o0 + n_out:o0 + n_out + r_out]
        s0 = o0 + n_out + r_out
        scr = refs[s0:s0 + n_scr]
        if not copies:
            return body(*ins, *outs, *scr)
        send, recv, loc = refs[s0 + n_scr:]
        first, last = None, None
        for d in range(len(grid)):
            f, l = pl.program_id(d) == 0, pl.program_id(d) == pl.num_programs(d) - 1
            first = f if first is None else first & f
            last = l if last is None else last & l

        def start():
            remote, local = rider.plan(rins, routs, send, recv, loc, 0, 0)
            for cp in local + remote:
                cp.start()

        def finish():
            remote, local = rider.plan(rins, routs, send, recv, loc, 0, 0)
            for cp in remote + local:
                cp.wait()

        if first is None:
            start()
            body(*ins, *outs, *scr)
            finish()
        else:
            pl.when(first)(start)
            body(*ins, *outs, *scr)
            pl.when(last)(finish)

    hbm = pl.BlockSpec(memory_space=pl.ANY)
    dma = pltpu.SemaphoreType.DMA
    res = pl.pallas_call(
        wrapped, name=name, grid=grid, in_specs=list(in_specs) + [hbm] * r_in,
        out_specs=list(out_specs) + [hbm] * r_out, out_shape=list(out_shape) + rider.out_shape,
        scratch_shapes=list(scratch_shapes) + (
            [dma((max(rider.n_remote, 1),)), dma((max(rider.n_remote, 1),)), dma((max(rider.n_local, 1),))] if copies else []),
        input_output_aliases={n_in + i: n_out + o for i, o in rider.aliases.items()},
        compiler_params=_params(sem),
    )(*args, *rider.ins)
    return list(res[:n_out]), list(res[n_out:])


def _comm_call(rider, name):
    return _launch(lambda: None, [], name=name, grid=(), in_specs=[], out_specs=[], out_shape=[], rider=rider)[1]


_HBM = pl.BlockSpec(memory_space=pltpu.HBM)
_SEM = pl.BlockSpec(memory_space=pltpu.SEMAPHORE)
_EFFECT = pltpu.SideEffectType.DATAFLOW_SIDE_EFFECTING


def _copies_start(riders, name):
    assert all(r.n_local == 0 and not r.aliases for r in riders)
    sizes = [(len(r.ins), len(r.out_shape)) for r in riders]
    bufs = []
    for r in riders:
        bufs += [pltpu.with_memory_space_constraint(a, pltpu.HBM) for a in r.ins]
        bufs += [pltpu.with_memory_space_constraint(lax.empty(s.shape, s.dtype), pltpu.HBM) for s in r.out_shape]
    nb, ng = len(bufs), len(riders)

    def body(*refs):
        sems, token, at = refs[2 * nb:2 * nb + 2 * ng], refs[-1], 0
        for g, (r, (ni, no)) in enumerate(zip(riders, sizes)):
            remote, _ = r.plan(refs[at:at + ni], refs[at + ni:at + ni + no], sems[2 * g], sems[2 * g + 1], None, 0, 0)
            for cp in remote:
                cp.start()
            at += ni + no
        token[...] = jnp.zeros_like(token)

    res = pl.pallas_call(
        body, name=name, in_specs=[_HBM] * nb,
        out_specs=[_HBM] * nb + [_SEM] * (2 * ng) + [pl.BlockSpec(memory_space=pltpu.VMEM)],
        out_shape=[pltpu.HBM(a.shape, a.dtype) for a in bufs]
        + [pltpu.SemaphoreType.DMA((r.n_remote,)) for r in riders for _ in range(2)]
        + [jax.ShapeDtypeStruct((8, LANES), F32)],
        input_output_aliases={i: i for i in range(nb)},
        compiler_params=pltpu.CompilerParams(has_side_effects=_EFFECT),
    )(*bufs)
    handles, at = [], 0
    for g, (r, (ni, no)) in enumerate(zip(riders, sizes)):
        handles.append((r, list(res[at:at + ni + no]), res[nb + 2 * g], res[nb + 2 * g + 1]))
        at += ni + no
    return handles, res[-1]


def _copies_wait(handles, after, name):
    bufs = [b for _, bs, _, _ in handles for b in bs]
    sems = [s for _, _, send, recv in handles for s in (send, recv)]
    nb, ng = len(bufs), len(handles)

    def body(*refs):
        at = 0
        for g, (rider, bs, _, _) in enumerate(handles):
            ni = len(rider.ins)
            remote, _ = rider.plan(refs[at:at + ni], refs[at + ni:at + len(bs)], refs[nb + 2 * g], refs[nb + 2 * g + 1],
                                   None, 0, 0)
            for cp in remote:
                cp.wait_send()
                cp.wait_recv()
            at += len(bs)

    res = pl.pallas_call(
        body, name=name, in_specs=[_HBM] * nb + [_SEM] * (2 * ng) + [pl.BlockSpec(memory_space=pl.ANY)],
        out_specs=[_HBM] * nb, out_shape=[pltpu.HBM(a.shape, a.dtype) for a in bufs],
        input_output_aliases={i: i for i in range(nb)},
        compiler_params=pltpu.CompilerParams(has_side_effects=_EFFECT),
    )(*bufs, *sems, after)
    lands, at = [], 0
    for rider, bs, _, _ in handles:
        lands += list(res[at + len(rider.ins):at + len(bs)])
        at += len(bs)
    return lands


def _rms_r(x):
    return lax.rsqrt(jnp.mean(x * x, axis=-1, keepdims=True) + EPS)


def _rms_bwd(dn, x, r, g):
    xh = x * r
    dxh = dn * g
    dx = r * (dxh - xh * jnp.mean(dxh * xh, axis=-1, keepdims=True))
    return dx, dn * xh


def _rot(t, c, sa, sb):
    outs = []
    for j in range(t.shape[1] // LANES):
        tj = t[:, LANES * j:LANES * (j + 1)]
        outs.append(tj * c + pltpu.roll(tj, LANES - 8, 1) * sa + pltpu.roll(tj, 8, 1) * sb)
    return outs[0] if len(outs) == 1 else jnp.concatenate(outs, axis=1)


def _rot_tables(S):
    pos = jnp.arange(S, dtype=F32)
    inv_freq = ROPE_THETA ** (-jnp.arange(0, ROT_DIM, 2, dtype=F32) / ROT_DIM)
    ang = pos[:, None] * inv_freq[None, :]
    cos, sin = jnp.cos(ang), jnp.sin(ang)
    one = jnp.ones((S, HEAD_DIM - ROT_DIM), F32)
    zero = jnp.zeros((S, HEAD_DIM - ROT_DIM), F32)
    z8 = jnp.zeros((S, 8), F32)
    c = jnp.concatenate([cos, cos, one], axis=1)
    sa = jnp.concatenate([-sin, z8, zero], axis=1)
    sb = jnp.concatenate([z8, sin, zero], axis=1)
    rep = LANES // HEAD_DIM
    return jnp.tile(c, (1, rep)), jnp.tile(sa, (1, rep)), jnp.tile(sb, (1, rep))


def _lane_tile4(k):
    lane = lax.broadcasted_iota(jnp.int32, k.shape, 1)
    rk = pltpu.roll(k, HEAD_DIM, 1)
    x0 = jnp.where(lane < HEAD_DIM, k, rk)
    x1 = jnp.where(lane < HEAD_DIM, rk, k)
    return jnp.concatenate([x0, x0, x1, x1], axis=1)


def _fold_heads(acc):
    zs = []
    for hk in range(N_KV_HEADS):
        a = acc[:, 256 * hk:256 * hk + LANES] + acc[:, 256 * hk + LANES:256 * (hk + 1)]
        zs.append(a + pltpu.roll(a, HEAD_DIM, 1))
    lane = lax.broadcasted_iota(jnp.int32, zs[0].shape, 1)
    return jnp.where(lane < HEAD_DIM, zs[0], zs[1])


def _inproj_call(x, g1, win_t, b_in, rc, rsa, rsb, S, rider=None):
    T = x.shape[0]
    tm = _tile(S, 512)
    nst = S // tm

    def body(x_ref, g1_ref, w_ref, b_ref, c_ref, sa_ref, sb_ref,
             h_ref, u_ref, q_ref, k4_ref, v4_ref, g_ref):
        xv = x_ref[...]
        hb = ((xv * _rms_r(xv)) * g1_ref[...]).astype(MXU_DTYPE)
        h_ref[...] = hb

        def proj(lo, hi):
            return _dot(hb, w_ref[lo:hi, :], NT) + b_ref[:, lo:hi]

        c, sa, sb = c_ref[...], sa_ref[...], sb_ref[...]
        u_ref[...] = proj(0, C_Q)
        q_ref[...] = (_rot(proj(C_Q, C_K), c, sa, sb) * SCALE).astype(MXU_DTYPE)
        kv = proj(C_K, C_G)
        k4_ref[...] = _lane_tile4(_rot(kv[:, :KV_WIDTH], c, sa, sb)).astype(MXU_DTYPE)
        v4_ref[...] = _lane_tile4(kv[:, KV_WIDTH:]).astype(MXU_DTYPE)
        g_ref[...] = jax.nn.sigmoid(proj(C_G, IN_WIDTH)).astype(MXU_DTYPE)

    tok = lambda w: pl.BlockSpec((tm, w), lambda i: (i, 0))
    full = lambda a: pl.BlockSpec(a.shape, lambda i: (0,) * a.ndim)
    tab = pl.BlockSpec((tm, LANES), lambda i: (i % nst, 0))
    return _launch(
        body, [x, g1, win_t, b_in, rc, rsa, rsb], name="inproj_fwd", grid=(T // tm,),
        in_specs=[tok(D_MODEL), full(g1), full(win_t), full(b_in), tab, tab, tab],
        out_specs=[tok(D_MODEL), tok(POOL_WIDTH), tok(ATTN_WIDTH), tok(512), tok(512), tok(GATE_WIDTH)],
        out_shape=[jax.ShapeDtypeStruct((T, D_MODEL), MXU_DTYPE), jax.ShapeDtypeStruct((T, POOL_WIDTH), F32),
                   jax.ShapeDtypeStruct((T, ATTN_WIDTH), MXU_DTYPE), jax.ShapeDtypeStruct((T, 512), MXU_DTYPE),
                   jax.ShapeDtypeStruct((T, 512), MXU_DTYPE), jax.ShapeDtypeStruct((T, GATE_WIDTH), MXU_DTYPE)],
        sem=("arbitrary",), rider=rider)


def _shift_rows(a, k, rows):
    n = a.shape[0]
    if k > 0:
        return jnp.where(rows >= k, pltpu.roll(a, k, 0), 0.0)
    return jnp.where(rows < n + k, pltpu.roll(a, n + k, 0), 0.0)


def _win_sum(a, w, rows, sign):
    s, k = a, 1
    while k < w:
        s = s + _shift_rows(s, sign * k, rows)
        k *= 2
    return s


def _pool_diff(ug, w, rows):
    inv = 1.0 / jnp.minimum(rows + 1, w).astype(F32)
    return _win_sum(ug, w, rows, 1) * inv - ug, inv


def _pool_call(u, w_pool, pool_scale, S):
    T = u.shape[0]

    def body(u_ref, w_ref, ps_ref, y_ref):
        rows = lax.broadcasted_iota(jnp.int32, (S, POOL_GC), 0)
        for gi, w in enumerate(POOL_WINDOWS):
            sl = slice(POOL_GC * gi, POOL_GC * (gi + 1))
            diff, _ = _pool_diff(u_ref[:, sl], w, rows)
            mixed = _dot(diff.astype(MXU_DTYPE), w_ref[gi], NN)
            y_ref[:, sl] = (mixed * ps_ref[:, sl]).astype(MXU_DTYPE)

    seq = pl.BlockSpec((S, POOL_WIDTH), lambda b: (b, 0))
    return pl.pallas_call(
        body, name="pool_fwd", grid=(T // S,),
        in_specs=[seq, pl.BlockSpec(w_pool.shape, lambda b: (0, 0, 0)), pl.BlockSpec(pool_scale.shape, lambda b: (0, 0))],
        out_specs=seq, out_shape=jax.ShapeDtypeStruct((T, POOL_WIDTH), MXU_DTYPE),
        compiler_params=_params(("arbitrary",)),
    )(u, w_pool, pool_scale)


def _pool_bwd_call(u, dyp, w_pool, pool_scale, S, rider=None):
    T = u.shape[0]

    def body(u_ref, dy_ref, w_ref, ps_ref, du_ref, dw_ref, dps_ref):
        @pl.when(pl.program_id(0) == 0)
        def _():
            dw_ref[...] = jnp.zeros_like(dw_ref)
            dps_ref[...] = jnp.zeros_like(dps_ref)

        rows = lax.broadcasted_iota(jnp.int32, (S, POOL_GC), 0)
        for gi, w in enumerate(POOL_WINDOWS):
            sl = slice(POOL_GC * gi, POOL_GC * (gi + 1))
            diff, inv = _pool_diff(u_ref[:, sl], w, rows)
            diffb = diff.astype(MXU_DTYPE)
            wg = w_ref[gi]
            mixed = _dot(diffb, wg, NN)
            dy = dy_ref[:, sl]
            dps_ref[:, sl] += jnp.sum(dy * mixed, axis=0, keepdims=True)
            dmb = (dy * ps_ref[:, sl]).astype(MXU_DTYPE)
            dw_ref[gi] += _dot(diffb, dmb, TN)
            ddiff = _dot(dmb, wg, NT)
            du_ref[:, sl] = (_win_sum(ddiff * inv, w, rows, -1) - ddiff).astype(MXU_DTYPE)

    seq = pl.BlockSpec((S, POOL_WIDTH), lambda b: (b, 0))
    return _launch(
        body, [u, dyp, w_pool, pool_scale], name="pool_bwd", grid=(T // S,),
        in_specs=[seq, seq, pl.BlockSpec(w_pool.shape, lambda b: (0, 0, 0)), pl.BlockSpec(pool_scale.shape, lambda b: (0, 0))],
        out_specs=[seq, pl.BlockSpec(w_pool.shape, lambda b: (0, 0, 0)), pl.BlockSpec(pool_scale.shape, lambda b: (0, 0))],
        out_shape=[jax.ShapeDtypeStruct((T, POOL_WIDTH), MXU_DTYPE), jax.ShapeDtypeStruct(w_pool.shape, F32),
                   jax.ShapeDtypeStruct(pool_scale.shape, F32)],
        sem=("arbitrary",), rider=rider)


def _attn_consts():
    lane_g = lax.broadcasted_iota(jnp.int32, (BLOCK, 256), 1) >> 6
    rgrp = lax.broadcasted_iota(jnp.int32, (GROUP * BLOCK, 1), 0) >> 7
    rel = lax.broadcasted_iota(jnp.int32, (BLOCK, 256), 0) - lax.broadcasted_iota(jnp.int32, (BLOCK, 256), 1)

    def bias(off):
        ok = (rel + off >= 0) & (rel + off < BLOCK)
        return jnp.concatenate([jnp.where(ok, 0.0, NEG_INF)] * GROUP, axis=0)

    return lane_g, rgrp, bias(0), bias(BLOCK)


def _sink_rows(sink_ref, hk, rgrp):
    sv = jnp.zeros(rgrp.shape, F32)
    for g in range(GROUP):
        sv = jnp.where(rgrp == g, sink_ref[0, GROUP * hk + g], sv)
    return sv


def _stack_heads(xb, lane_g):
    return jnp.concatenate([jnp.where(lane_g == g, xb, jnp.zeros_like(xb)) for g in range(GROUP)], axis=0)


def _unstack_heads(xs, lane_g):
    out = jnp.where(lane_g == 0, xs[0:BLOCK], 0.0)
    for g in range(1, GROUP):
        out = out + jnp.where(lane_g == g, xs[BLOCK * g:BLOCK * (g + 1)], 0.0)
    return out


def _attn_probs(qs, kb, bias, sv):
    s = _dot(qs, kb, NT) + bias
    m = jnp.maximum(jnp.max(s, axis=1, keepdims=True), sv)
    e = jnp.exp(s - m)
    es = jnp.exp(sv - m)
    inv_l = 1.0 / (jnp.sum(e, axis=1, keepdims=True) + es)
    return e * inv_l, es * inv_l


def _attn_blocks(nb, blk, carry):
    carry = blk(0, 0, True, carry)
    return lax.fori_loop(1, nb, lambda n, c: blk(pl.multiple_of(n * BLOCK, BLOCK),
                                                 pl.multiple_of((n - 1) * BLOCK, BLOCK), False, c), carry)


def _attn_call(sinks, q, k4, v4, S, rider=None):
    T = q.shape[0]
    nb = S // BLOCK

    def body(sink_ref, q_ref, k_ref, v_ref, o_ref):
        lane_g, rgrp, bias_first, bias_later = _attn_consts()
        svs = [_sink_rows(sink_ref, hk, rgrp) for hk in range(N_KV_HEADS)]

        def blk(q0, k0, first, carry):
            for hk in range(N_KV_HEADS):
                cs = slice(256 * hk, 256 * (hk + 1))
                qs = _stack_heads(q_ref[pl.ds(q0, BLOCK), cs], lane_g)
                p, _ = _attn_probs(qs, k_ref[pl.ds(k0, 2 * BLOCK), cs], bias_first if first else bias_later, svs[hk])
                o = _dot(p.astype(MXU_DTYPE), v_ref[pl.ds(k0, 2 * BLOCK), cs], NN)
                o_ref[pl.ds(q0, BLOCK), cs] = _unstack_heads(o, lane_g).astype(MXU_DTYPE)
            return carry

        _attn_blocks(nb, blk, 0)

    seq = pl.BlockSpec((S, ATTN_WIDTH), lambda b: (b, 0))
    return _launch(
        body, [sinks, q, k4, v4], name="attn_fwd", grid=(T // S,),
        in_specs=[pl.BlockSpec(memory_space=pltpu.SMEM), seq, seq, seq],
        out_specs=[seq], out_shape=[jax.ShapeDtypeStruct((T, ATTN_WIDTH), MXU_DTYPE)],
        sem=("arbitrary",), rider=rider)


def _attn_bwd_call(sinks, q, k4, v4, do, rc, rsa, rsb, S, rider=None):
    T = q.shape[0]
    nb = S // BLOCK

    def body(sink_ref, q_ref, k_ref, v_ref, do_ref, c_ref, sa_ref, sb_ref,
             dq_ref, dk_ref, dv_ref, ds_ref, dk_acc, dv_acc):
        lane_g, rgrp, bias_first, bias_later = _attn_consts()
        svs = [_sink_rows(sink_ref, hk, rgrp) for hk in range(N_KV_HEADS)]
        lane1 = lax.broadcasted_iota(jnp.int32, (1, LANES), 1)
        dk_acc[...] = jnp.zeros_like(dk_acc)
        dv_acc[...] = jnp.zeros_like(dv_acc)

        def blk(q0, k0, first, dsink):
            rows = pl.ds(q0, BLOCK)
            c, sa, sb = c_ref[rows, :], sa_ref[rows, :], sb_ref[rows, :]
            for hk in range(N_KV_HEADS):
                cs = slice(256 * hk, 256 * (hk + 1))
                qs = _stack_heads(q_ref[rows, cs], lane_g)
                dos = _stack_heads(do_ref[rows, cs], lane_g)
                kb = k_ref[pl.ds(k0, 2 * BLOCK), cs]
                vb = v_ref[pl.ds(k0, 2 * BLOCK), cs]
                p, ps = _attn_probs(qs, kb, bias_first if first else bias_later, svs[hk])
                dp = _dot(dos, vb, NT)
                delta = jnp.sum(p * dp, axis=1, keepdims=True)
                dsb = (p * (dp - delta)).astype(MXU_DTYPE)
                dqb = _unstack_heads(_dot(dsb, kb, NN), lane_g) * SCALE
                dq_ref[rows, cs] = _rot(dqb, c, -sa, -sb).astype(MXU_DTYPE)
                dk_acc[pl.ds(k0, 2 * BLOCK), cs] += _dot(dsb, qs, TN)
                dv_acc[pl.ds(k0, 2 * BLOCK), cs] += _dot(p.astype(MXU_DTYPE), dos, TN)
                psd = ps * delta
                for g in range(GROUP):
                    val = -jnp.sum(psd[BLOCK * g:BLOCK * (g + 1)], axis=0, keepdims=True)
                    dsink = dsink + jnp.where(lane1 == GROUP * hk + g, val, 0.0)
            return dsink

        dsink = _attn_blocks(nb, blk, jnp.zeros((1, LANES), F32))
        dk_ref[...] = _rot(_fold_heads(dk_acc[...]), c_ref[...], -sa_ref[...], -sb_ref[...]).astype(MXU_DTYPE)
        dv_ref[...] = _fold_heads(dv_acc[...]).astype(MXU_DTYPE)
        ds_ref[...] = jnp.broadcast_to(dsink, ds_ref.shape)

    seq = pl.BlockSpec((S, ATTN_WIDTH), lambda b: (b, 0))
    kvs = pl.BlockSpec((S, KV_WIDTH), lambda b: (b, 0))
    tab = pl.BlockSpec((S, LANES), lambda b: (0, 0))
    nseq = T // S
    return _launch(
        body, [sinks, q, k4, v4, do, rc, rsa, rsb], name="attn_bwd", grid=(nseq,),
        in_specs=[pl.BlockSpec(memory_space=pltpu.SMEM), seq, seq, seq, seq, tab, tab, tab],
        out_specs=[seq, kvs, kvs, pl.BlockSpec((8, LANES), lambda b: (b, 0))],
        out_shape=[jax.ShapeDtypeStruct((T, ATTN_WIDTH), MXU_DTYPE), jax.ShapeDtypeStruct((T, KV_WIDTH), MXU_DTYPE),
                   jax.ShapeDtypeStruct((T, KV_WIDTH), MXU_DTYPE), jax.ShapeDtypeStruct((8 * nseq, LANES), F32)],
        scratch_shapes=[pltpu.VMEM((S, 512), F32), pltpu.VMEM((S, 512), F32)],
        sem=("arbitrary",), rider=rider)


def _branch_weights(wbp_ref, wba_ref, wbp_s, wba_s):
    @pl.when(pl.program_id(0) == 0)
    def _():
        for j in range(N_DEV):
            wbp_s[:, LANES * j:LANES * (j + 1)] = wbp_ref[j]
            wba_s[:, LANES * j:LANES * (j + 1)] = wba_ref[j]


def _mix_fwd_call(yp, ya, g, x, wbp, wba, wout, g2, g3, rider=None):
    T = x.shape[0]
    tm = _tile(T, 512)

    def body(yp_ref, ya_ref, g_ref, x_ref, wbp_ref, wba_ref, wout_ref, g2_ref, g3_ref,
             mix_ref, x1_ref, h2_ref, wbp_s, wba_s):
        _branch_weights(wbp_ref, wba_ref, wbp_s, wba_s)
        bp = _dot(yp_ref[...], wbp_s[...], NN)
        ba = _dot(ya_ref[...], wba_s[...], NN)
        merged = g_ref[:, :D_MODEL].astype(F32) * bp + g_ref[:, D_MODEL:].astype(F32) * ba
        mix = _dot(merged.astype(MXU_DTYPE), wout_ref[...], NN)
        mix_ref[...] = mix
        x1 = x_ref[...] + (mix * _rms_r(mix)) * g2_ref[...]
        x1_ref[...] = x1
        h2_ref[...] = ((x1 * _rms_r(x1)) * g3_ref[...]).astype(MXU_DTYPE)

    tok = lambda w: pl.BlockSpec((tm, w), lambda i: (i, 0))
    full = lambda a: pl.BlockSpec(a.shape, lambda i: (0,) * a.ndim)
    return _launch(
        body, [yp, ya, g, x, wbp, wba, wout, g2, g3], name="mix_fwd", grid=(T // tm,),
        in_specs=[tok(POOL_WIDTH), tok(ATTN_WIDTH), tok(GATE_WIDTH), tok(D_MODEL), full(wbp), full(wba), full(wout),
                  full(g2), full(g3)],
        out_specs=[tok(D_MODEL), tok(D_MODEL), tok(D_MODEL)],
        out_shape=[jax.ShapeDtypeStruct((T, D_MODEL), F32), jax.ShapeDtypeStruct((T, D_MODEL), F32),
                   jax.ShapeDtypeStruct((T, D_MODEL), MXU_DTYPE)],
        scratch_shapes=[pltpu.VMEM((POOL_WIDTH, D_MODEL), MXU_DTYPE), pltpu.VMEM((ATTN_WIDTH, D_MODEL), MXU_DTYPE)],
        sem=("arbitrary",), rider=rider)


def _mix_bwd_call(dx1, mix, yp, ya, g, wbp, wba, wout, g2, rider=None):
    T = dx1.shape[0]
    tm = _tile(T, 256)

    def body(dx1_ref, mix_ref, yp_ref, ya_ref, g_ref, wbp_ref, wba_ref, wout_ref, g2_ref,
             dyp_ref, do_ref, dgates_ref, dg2_ref, dbg_ref, gout_ref, gbp_ref, gba_ref,
             wbp_s, wba_s, acc_out, acc_bp, acc_ba, sem):
        _branch_weights(wbp_ref, wba_ref, wbp_s, wba_s)
        step = pl.program_id(0)

        @pl.when(step == 0)
        def _():
            dg2_ref[...] = jnp.zeros_like(dg2_ref)
            dbg_ref[...] = jnp.zeros_like(dbg_ref)
            acc_out[...] = jnp.zeros_like(acc_out)
            acc_bp[...] = jnp.zeros_like(acc_bp)
            acc_ba[...] = jnp.zeros_like(acc_ba)

        mix = mix_ref[...]
        dmix, dg2 = _rms_bwd(dx1_ref[...], mix, _rms_r(mix), g2_ref[...])
        dg2_ref[...] += jnp.sum(dg2, axis=0, keepdims=True)
        dmixb = dmix.astype(MXU_DTYPE)
        dmerged = _dot(dmixb, wout_ref[...], NT)
        yp, ya = yp_ref[...], ya_ref[...]
        bp = _dot(yp, wbp_s[...], NN)
        ba = _dot(ya, wba_s[...], NN)
        gp, ga = g_ref[:, :D_MODEL].astype(F32), g_ref[:, D_MODEL:].astype(F32)
        acc_out[...] += _dot((gp * bp + ga * ba).astype(MXU_DTYPE), dmixb, TN)
        dgp = dmerged * bp * (gp * (1.0 - gp))
        dga = dmerged * ba * (ga * (1.0 - ga))
        dbg_ref[:, :D_MODEL] += jnp.sum(dgp, axis=0, keepdims=True)
        dbg_ref[:, D_MODEL:] += jnp.sum(dga, axis=0, keepdims=True)
        dgates_ref[:, :D_MODEL] = dgp.astype(MXU_DTYPE)
        dgates_ref[:, D_MODEL:] = dga.astype(MXU_DTYPE)
        dbp = (dmerged * gp).astype(MXU_DTYPE)
        dba = (dmerged * ga).astype(MXU_DTYPE)
        acc_bp[...] += _dot(yp, dbp, TN)
        acc_ba[...] += _dot(ya, dba, TN)
        dyp_ref[...] = _dot(dbp, wbp_s[...], NT)
        do_ref[...] = _dot(dba, wba_s[...], NT).astype(MXU_DTYPE)

        @pl.when(step == pl.num_programs(0) - 1)
        def _():
            copies = [pltpu.make_async_copy(acc_out, gout_ref, sem.at[0])]
            for j in range(N_DEV):
                cols = slice(LANES * j, LANES * (j + 1))
                copies.append(pltpu.make_async_copy(acc_bp.at[:, cols], gbp_ref.at[j], sem.at[1 + j]))
                copies.append(pltpu.make_async_copy(acc_ba.at[:, cols], gba_ref.at[j], sem.at[1 + N_DEV + j]))
            for cp in copies:
                cp.start()
            for cp in copies:
                cp.wait()

    tok = lambda w: pl.BlockSpec((tm, w), lambda i: (i, 0))
    full = lambda a: pl.BlockSpec(a.shape, lambda i: (0,) * a.ndim)
    acc = lambda w: pl.BlockSpec((1, w), lambda i: (0, 0))
    hbm = pl.BlockSpec(memory_space=pl.ANY)
    sd = jax.ShapeDtypeStruct
    return _launch(
        body, [dx1, mix, yp, ya, g, wbp, wba, wout, g2], name="mix_bwd", grid=(T // tm,),
        in_specs=[tok(D_MODEL), tok(D_MODEL), tok(POOL_WIDTH), tok(ATTN_WIDTH), tok(GATE_WIDTH), full(wbp), full(wba),
                  full(wout), full(g2)],
        out_specs=[tok(POOL_WIDTH), tok(ATTN_WIDTH), tok(GATE_WIDTH), acc(D_MODEL), acc(GATE_WIDTH), hbm, hbm, hbm],
        out_shape=[sd((T, POOL_WIDTH), F32), sd((T, ATTN_WIDTH), MXU_DTYPE), sd((T, GATE_WIDTH), MXU_DTYPE),
                   sd((1, D_MODEL), F32), sd((1, GATE_WIDTH), F32), sd((D_MODEL, D_MODEL), F32),
                   sd((N_DEV, POOL_WIDTH, LANES), F32), sd((N_DEV, ATTN_WIDTH, LANES), F32)],
        scratch_shapes=[pltpu.VMEM((POOL_WIDTH, D_MODEL), MXU_DTYPE), pltpu.VMEM((ATTN_WIDTH, D_MODEL), MXU_DTYPE),
                        pltpu.VMEM((D_MODEL, D_MODEL), F32), pltpu.VMEM((POOL_WIDTH, D_MODEL), F32),
                        pltpu.VMEM((ATTN_WIDTH, D_MODEL), F32), pltpu.SemaphoreType.DMA((1 + 2 * N_DEV,))],
        sem=("arbitrary",), rider=rider)


def _mlp_call(x1, h2, target, wup, wdown, g3, g4):
    T = x1.shape[0]
    tm = _tile(T, 256)
    fc = D_FF // N_DEV

    def body(x1_ref, h2_ref, t_ref, wup_ref, wdown_ref, g3_ref, g4_ref,
             act_ref, da_ref, dff_ref, dx1_ref, dg3_ref, dg4_ref, loss_ref, rl_s):
        @pl.when(pl.program_id(0) == 0)
        def _():
            dg3_ref[...] = jnp.zeros_like(dg3_ref)
            dg4_ref[...] = jnp.zeros_like(dg4_ref)
            loss_ref[...] = jnp.zeros_like(loss_ref)

        h2 = h2_ref[...]
        ff = jnp.zeros((tm, D_MODEL), F32)
        for j in range(N_DEV):
            sl = slice(fc * j, fc * (j + 1))
            rl = jnp.maximum(_dot(h2, wup_ref[j], NN), 0.0)
            rl_s[:, sl] = rl
            actb = (rl * rl).astype(MXU_DTYPE)
            act_ref[:, sl] = actb
            ff = ff + _dot(actb, wdown_ref[j], NN)
        x1 = x1_ref[...]
        r4 = _rms_r(ff)
        err = x1 + (ff * r4) * g4_ref[...] - t_ref[...]
        loss_ref[...] += jnp.sum(err * err, axis=0, keepdims=True)
        dy = err * (1.0 / D_MODEL)
        dff, dg4 = _rms_bwd(dy, ff, r4, g4_ref[...])
        dg4_ref[...] += jnp.sum(dg4, axis=0, keepdims=True)
        dffb = dff.astype(MXU_DTYPE)
        dff_ref[...] = dffb
        dh2 = jnp.zeros((tm, D_MODEL), F32)
        for j in range(N_DEV):
            sl = slice(fc * j, fc * (j + 1))
            dab = (_dot(dffb, wdown_ref[j], NT) * (2.0 * rl_s[:, sl])).astype(MXU_DTYPE)
            da_ref[:, sl] = dab
            dh2 = dh2 + _dot(dab, wup_ref[j], NT)
        dx1, dg3 = _rms_bwd(dh2, x1, _rms_r(x1), g3_ref[...])
        dg3_ref[...] += jnp.sum(dg3, axis=0, keepdims=True)
        dx1_ref[...] = dy + dx1

    tok = lambda w: pl.BlockSpec((tm, w), lambda i: (i, 0))
    full = lambda a: pl.BlockSpec(a.shape, lambda i: (0,) * a.ndim, pipeline_mode=pl.Buffered(1))
    vec = pl.BlockSpec((1, D_MODEL), lambda i: (0, 0))
    sd = jax.ShapeDtypeStruct
    return pl.pallas_call(
        body, name="mlp_fwd_bwd", grid=(T // tm,),
        in_specs=[tok(D_MODEL), tok(D_MODEL), tok(D_MODEL), full(wup), full(wdown), vec, vec],
        out_specs=[tok(D_FF), tok(D_FF), tok(D_MODEL), tok(D_MODEL), vec, vec, vec],
        out_shape=[sd((T, D_FF), MXU_DTYPE), sd((T, D_FF), MXU_DTYPE), sd((T, D_MODEL), MXU_DTYPE),
                   sd((T, D_MODEL), F32), sd((1, D_MODEL), F32), sd((1, D_MODEL), F32), sd((1, D_MODEL), F32)],
        scratch_shapes=[pltpu.VMEM((tm, D_FF), F32)],
        compiler_params=_params(("arbitrary",)),
    )(x1, h2, target, wup, wdown, g3, g4)


def _inproj_bwd_call(du, dq, dk, dv, dgates, dx1, x, win_t, g1, rider=None):
    T = x.shape[0]
    tm = _tile(T, 512)

    def body(du_ref, dq_ref, dk_ref, dv_ref, dgt_ref, dx1_ref, x_ref, w_ref, g1_ref, gx_ref, dg1_ref, db_ref):
        @pl.when(pl.program_id(0) == 0)
        def _():
            dg1_ref[...] = jnp.zeros_like(dg1_ref)
            db_ref[...] = jnp.zeros_like(db_ref)

        dh = jnp.zeros((tm, D_MODEL), F32)
        for ref, lo, hi in ((du_ref, 0, C_Q), (dq_ref, C_Q, C_K), (dk_ref, C_K, C_V), (dv_ref, C_V, C_G),
                            (dgt_ref, C_G, IN_WIDTH)):
            piece = ref[...]
            dh = dh + _dot(piece, w_ref[lo:hi, :], NN)
            if hi <= C_G:
                db_ref[:, lo:hi] += jnp.sum(piece.astype(F32), axis=0, keepdims=True)
        xv = x_ref[...]
        dx, dg1 = _rms_bwd(dh, xv, _rms_r(xv), g1_ref[...])
        dg1_ref[...] += jnp.sum(dg1, axis=0, keepdims=True)
        gx_ref[...] = dx1_ref[...] + dx

    tok = lambda w: pl.BlockSpec((tm, w), lambda i: (i, 0))
    full = lambda a: pl.BlockSpec(a.shape, lambda i: (0,) * a.ndim)
    sd = jax.ShapeDtypeStruct
    return _launch(
        body, [du, dq, dk, dv, dgates, dx1, x, win_t, g1], name="inproj_bwd", grid=(T // tm,),
        in_specs=[tok(POOL_WIDTH), tok(ATTN_WIDTH), tok(KV_WIDTH), tok(KV_WIDTH), tok(GATE_WIDTH), tok(D_MODEL),
                  tok(D_MODEL), full(win_t), full(g1)],
        out_specs=[tok(D_MODEL), pl.BlockSpec((1, D_MODEL), lambda i: (0, 0)), pl.BlockSpec((1, C_G), lambda i: (0, 0))],
        out_shape=[sd((T, D_MODEL), F32), sd((1, D_MODEL), F32), sd((1, C_G), F32)],
        sem=("arbitrary",), rider=rider)


WGRAD_TOKENS = 1024


def _wgrad_rows_call(a, b, name, rider=None):
    T, K = a.shape
    N = b.shape[1]
    tm = _tile(T, WGRAD_TOKENS)
    kb = min(K, 1024)
    per = kb // (K // N_DEV)

    def body(a_ref, b_ref, o_ref):
        @pl.when(pl.program_id(1) == 0)
        def _():
            o_ref[...] = jnp.zeros_like(o_ref)

        d = _dot(a_ref[...], b_ref[...], TN)
        rs = kb // per
        for j in range(per):
            o_ref[j] += d[rs * j:rs * (j + 1)]

    return _launch(
        body, [a, b], name=name, grid=(K // kb, T // tm),
        in_specs=[pl.BlockSpec((tm, kb), lambda i, t: (t, i)), pl.BlockSpec((tm, N), lambda i, t: (t, 0))],
        out_specs=[pl.BlockSpec((per, K // N_DEV, N), lambda i, t: (i, 0, 0))],
        out_shape=[jax.ShapeDtypeStruct((N_DEV, K // N_DEV, N), F32)],
        sem=("arbitrary", "arbitrary"), rider=rider)


def _wgrad_cols_call(pairs, name, rider=None):
    T, K = pairs[0][0].shape
    N = pairs[0][1].shape[1]
    assert all(a.shape == (T, K) and b.shape == (T, N) for a, b in pairs)
    n = len(pairs)
    tm = _tile(T, WGRAD_TOKENS)
    nb = min(N, 1024)
    per = nb // (N // N_DEV)

    def body(*refs):
        for t in range(n):
            a_ref, b_ref, o_ref = refs[2 * t], refs[2 * t + 1], refs[2 * n + t]

            @pl.when(pl.program_id(1) == 0)
            def _():
                o_ref[...] = jnp.zeros_like(o_ref)

            d = _dot(a_ref[...], b_ref[...], TN)
            cs = nb // per
            for j in range(per):
                o_ref[j] += d[:, cs * j:cs * (j + 1)]

    return _launch(
        body, [m for ab in pairs for m in ab], name=name, grid=(N // nb, T // tm),
        in_specs=[pl.BlockSpec((tm, K), lambda i, t: (t, 0)), pl.BlockSpec((tm, nb), lambda i, t: (t, i))] * n,
        out_specs=[pl.BlockSpec((per, K, N // N_DEV), lambda i, t: (i, 0, 0))] * n,
        out_shape=[jax.ShapeDtypeStruct((N_DEV, K, N // N_DEV), F32)] * n,
        sem=("arbitrary", "arbitrary"), rider=rider)


def _wgrad_in_call(du, dq, dk, dv, dgates, h, rider=None):
    T = h.shape[0]
    tm = _tile(T, WGRAD_TOKENS)
    rows = IN_WIDTH // N_DEV

    def body(du_ref, dq_ref, dk_ref, dv_ref, dgt_ref, h_ref, o_ref, acc, sem):
        t = pl.program_id(0)

        @pl.when(t == 0)
        def _():
            acc[...] = jnp.zeros_like(acc)

        hv = h_ref[...]
        for ref, lo, hi in ((du_ref, 0, C_Q), (dq_ref, C_Q, C_K), (dk_ref, C_K, C_V), (dv_ref, C_V, C_G),
                            (dgt_ref, C_G, IN_WIDTH)):
            acc[lo:hi, :] += _dot(ref[...], hv, TN)

        @pl.when(t == pl.num_programs(0) - 1)
        def _():
            copies = [pltpu.make_async_copy(acc.at[pl.ds(rows * j, rows), :], o_ref.at[j], sem.at[j])
                      for j in range(N_DEV)]
            for cp in copies:
                cp.start()
            for cp in copies:
                cp.wait()

    tok = lambda w: pl.BlockSpec((tm, w), lambda t: (t, 0))
    return _launch(
        body, [du, dq, dk, dv, dgates, h], name="wgrad_in", grid=(T // tm,),
        in_specs=[tok(POOL_WIDTH), tok(ATTN_WIDTH), tok(KV_WIDTH), tok(KV_WIDTH), tok(GATE_WIDTH), tok(D_MODEL)],
        out_specs=[pl.BlockSpec(memory_space=pl.ANY)],
        out_shape=[jax.ShapeDtypeStruct((N_DEV, rows, D_MODEL), F32)],
        scratch_shapes=[pltpu.VMEM((IN_WIDTH, D_MODEL), F32), pltpu.SemaphoreType.DMA((N_DEV,))],
        sem=("arbitrary",), rider=rider)


def _coords():
    return lax.axis_index("x"), lax.axis_index("y"), lax.axis_index("c")


def _allgather_call(shards):
    n = len(shards)

    def body(*refs):
        ins, outs = refs[:n], refs[n:2 * n]
        send_sems, recv_sems, local_sems = refs[2 * n:]
        x, y, c = _coords()
        me, sibling = (x, y, c), (x, y, 1 - c)
        chips = [(1 - x, y), (x, 1 - y), (1 - x, 1 - y)]

        def slot(p):
            return 4 * p[0] + 2 * p[1] + p[2]

        def copy(t, k, block, to, src=None):
            dst = outs[t].at[slot(block)]
            return pltpu.make_async_remote_copy(
                src_ref=dst if src is None else src, dst_ref=dst, send_sem=send_sems.at[t, k],
                recv_sem=recv_sems.at[t, k], device_id=to, device_id_type=MESH)

        mine = [pltpu.make_async_copy(ins[t], outs[t].at[slot(me)], local_sems.at[t]) for t in range(n)]
        for cp in mine:
            cp.start()
        first = []
        for t in range(n):
            first.append(copy(t, 0, me, sibling, src=ins[t]))
            first += [copy(t, 1 + j, me, (*chip, c), src=ins[t]) for j, chip in enumerate(chips)]
        for cp in first:
            cp.start()
        passed = []
        for t in range(n):
            for j, chip in enumerate(chips):
                copy(t, 1 + j, (*chip, c), me).wait_recv()
                fwd = copy(t, 4 + j, (*chip, c), sibling)
                fwd.start()
                passed.append(fwd)
        for t in range(n):
            copy(t, 0, sibling, me).wait_recv()
            for j, chip in enumerate(chips):
                copy(t, 4 + j, (*chip, 1 - c), me).wait_recv()
        for cp in first + passed:
            cp.wait_send()
        for cp in mine:
            cp.wait()

    hbm = pl.BlockSpec(memory_space=pl.ANY)
    return pl.pallas_call(
        body, name="allgather_weights",
        in_specs=[hbm] * n, out_specs=[hbm] * n,
        out_shape=[jax.ShapeDtypeStruct((N_DEV,) + s.shape, s.dtype) for s in shards],
        scratch_shapes=[pltpu.SemaphoreType.DMA((n, 7)), pltpu.SemaphoreType.DMA((n, 7)), pltpu.SemaphoreType.DMA((n,))],
    )(*shards)


def _slot(p):
    return 4 * p[0] + 2 * p[1] + p[2]


def _rows(ref, span):
    return ref if span is None else ref.at[pl.ds(span[0], span[1])]


ALL = "all"
LOCAL = "local"


def _rows(ref, span):
    return ref if span == ALL else ref.at[pl.ds(span[0], span[1])]


def _rider_ag(items):
    ins, out_shape, aliases, where = [], [], {}, []
    n_remote = n_local = 0
    for t, (shard, buf, snd, fwd) in enumerate(items):
        i_shard = i_buf = None
        if snd is not None:
            i_shard = len(ins)
            ins.append(shard)
        if buf is not None:
            i_buf = len(ins)
            ins.append(buf)
            aliases[i_buf] = t
            out_shape.append(jax.ShapeDtypeStruct(buf.shape, buf.dtype))
        else:
            assert fwd is None and snd is not None
            out_shape.append(jax.ShapeDtypeStruct((N_DEV,) + shard.shape, shard.dtype))
        where.append((i_shard, i_buf, n_remote, n_local))
        n_remote += (4 if snd not in (None, LOCAL) else 0) + (3 if fwd is not None else 0)
        n_local += 1 if snd is not None else 0

    def plan(rins, routs, send, recv, loc, r0, l0):
        x, y, c = _coords()
        peers = [(x, y, 1 - c), (1 - x, y, c), (x, 1 - y, c), (1 - x, 1 - y, c)]
        remote, local = [], []
        for t, (shard, buf, snd, fwd) in enumerate(items):
            i_shard, i_buf, k, l = where[t]
            k, l = r0 + k, l0 + l
            if snd is not None:
                span = ALL if snd == LOCAL else snd
                src, dst = _rows(rins[i_shard], span), _rows(routs[t].at[_slot((x, y, c))], span)
                local.append(pltpu.make_async_copy(src, dst, loc.at[l]))
                for peer in (peers if snd != LOCAL else []):
                    remote.append(pltpu.make_async_remote_copy(
                        src_ref=src, dst_ref=dst, send_sem=send.at[k], recv_sem=recv.at[k],
                        device_id=peer, device_id_type=MESH))
                    k += 1
            if fwd is not None:
                for px, py, pc in peers[1:]:
                    s = _slot((px, py, pc))
                    remote.append(pltpu.make_async_remote_copy(
                        src_ref=_rows(rins[i_buf].at[s], fwd), dst_ref=_rows(routs[t].at[s], fwd),
                        send_sem=send.at[k], recv_sem=recv.at[k], device_id=peers[0], device_id_type=MESH))
                    k += 1
        return remote, local

    return _Rider(ins, out_shape, n_remote, n_local, plan, aliases)


def _rider_ag_remote(shards):
    n = len(shards)

    def plan(ins, outs, send, recv, loc, r0, l0):
        x, y, c = _coords()
        remote = []
        for t in range(n):
            dst = outs[t].at[_slot((x, y, c))]
            for k, peer in enumerate([(x, y, 1 - c), (1 - x, y, c), (x, 1 - y, c), (1 - x, 1 - y, c)]):
                remote.append(pltpu.make_async_remote_copy(
                    src_ref=ins[t], dst_ref=dst, send_sem=send.at[r0 + 4 * t + k], recv_sem=recv.at[r0 + 4 * t + k],
                    device_id=peer, device_id_type=MESH))
        return remote, []

    return _Rider(shards, [jax.ShapeDtypeStruct((N_DEV,) + s.shape, s.dtype) for s in shards], 4 * n, 0, plan)


def _rider_rs_sibling(grads):
    n = len(grads)

    def plan(ins, outs, send, recv, loc, r0, l0):
        x, y, c = _coords()
        remote = []
        for t in range(n):
            for q in range(4):
                remote.append(pltpu.make_async_remote_copy(
                    src_ref=ins[t].at[q, 1 - c], dst_ref=outs[t].at[q], send_sem=send.at[r0 + 4 * t + q],
                    recv_sem=recv.at[r0 + 4 * t + q], device_id=(x, y, 1 - c), device_id_type=MESH))
        return remote, []

    return _Rider(grads, [jax.ShapeDtypeStruct((4,) + g.shape[2:], g.dtype) for g in grads], 4 * n, 0, plan)


def _rider_rs_chips(sums, rows=None, into=None):
    n = len(sums)
    rows = rows or [ALL] * n

    def plan(ins, outs, send, recv, loc, r0, l0):
        x, y, c = _coords()
        remote = []
        for t in range(n):
            for r, (px, py) in enumerate([(1 - x, y), (x, 1 - y), (1 - x, 1 - y)]):
                remote.append(pltpu.make_async_remote_copy(
                    src_ref=_rows(ins[t].at[2 * px + py], rows[t]), dst_ref=_rows(outs[t].at[r], rows[t]),
                    send_sem=send.at[r0 + 3 * t + r], recv_sem=recv.at[r0 + 3 * t + r],
                    device_id=(px, py, c), device_id_type=MESH))
        return remote, []

    out_shape = [jax.ShapeDtypeStruct((3,) + s.shape[1:], s.dtype) for s in sums]
    if into is None:
        return _Rider(sums, out_shape, 3 * n, 0, plan)
    return _Rider(list(sums) + list(into), out_shape, 3 * n, 0, plan, aliases={n + t: t for t in range(n)})


def _rider_gather_direct(parts):
    n = len(parts)

    def plan(ins, outs, send, recv, loc, r0, l0):
        x, y, c = _coords()
        me = _slot((x, y, c))
        remote, local = [], []
        for t in range(n):
            local.append(pltpu.make_async_copy(ins[t], outs[t].at[me], loc.at[l0 + t]))
            for k in range(1, N_DEV):
                peer = (x ^ ((k >> 2) & 1), y ^ ((k >> 1) & 1), c ^ (k & 1))
                remote.append(pltpu.make_async_remote_copy(
                    src_ref=ins[t], dst_ref=outs[t].at[me], send_sem=send.at[r0 + 7 * t + k - 1],
                    recv_sem=recv.at[r0 + 7 * t + k - 1], device_id=peer, device_id_type=MESH))
        return remote, local

    return _Rider(parts, [jax.ShapeDtypeStruct((N_DEV,) + p.shape, p.dtype) for p in parts], 7 * n, n, plan)


def _chip_sum_call(cidx, grads, recvd, out_dtypes, name):
    n = len(grads)

    def body(c_ref, *refs):
        for t in range(n):
            refs[2 * n + t][0] = (refs[t][0, 0] + refs[n + t][0]).astype(out_dtypes[t])

    in_specs = [pl.BlockSpec((1, 1) + g.shape[2:], lambda q, c_ref: (q, c_ref[0], 0, 0)) for g in grads]
    in_specs += [pl.BlockSpec((1,) + r.shape[1:], lambda q, c_ref: (q, 0, 0)) for r in recvd]
    return pl.pallas_call(
        body, name=name,
        grid_spec=pltpu.PrefetchScalarGridSpec(
            num_scalar_prefetch=1, grid=(4,), in_specs=in_specs,
            out_specs=[pl.BlockSpec((1,) + r.shape[1:], lambda q, c_ref: (q, 0, 0)) for r in recvd]),
        out_shape=[jax.ShapeDtypeStruct(r.shape, dt) for r, dt in zip(recvd, out_dtypes)],
        compiler_params=_params(("arbitrary",)),
    )(cidx, *grads, *recvd)


def _final_sum_call(idx, grads, recvd1, recvd2):
    n = len(grads)
    nsteps = 2

    def body(i_ref, *refs):
        for t in range(n):
            g, r1, r2, o = refs[t], refs[n + t], refs[2 * n + t], refs[3 * n + t]
            s = g[0, 0] + r1[0]
            for r in range(3):
                s = s + r2[r].astype(F32)
            o[...] = s

    def rows(a):
        r = a.shape[-2]
        return r // nsteps if (r // nsteps) % 16 == 0 else r

    def step(a):
        return (lambda i: i) if rows(a) != a.shape[-2] else (lambda i: 0)

    in_specs = [pl.BlockSpec((1, 1, rows(g), g.shape[3]), lambda i, s, st=step(g): (s[0], s[1], st(i), 0)) for g in grads]
    in_specs += [pl.BlockSpec((1, rows(r), r.shape[2]), lambda i, s, st=step(r): (s[0], st(i), 0)) for r in recvd1]
    in_specs += [pl.BlockSpec((3, rows(r), r.shape[2]), lambda i, s, st=step(r): (0, st(i), 0)) for r in recvd2]
    return pl.pallas_call(
        body, name="rs_final_sum",
        grid_spec=pltpu.PrefetchScalarGridSpec(
            num_scalar_prefetch=1, grid=(nsteps,), in_specs=in_specs,
            out_specs=[pl.BlockSpec((rows(r), r.shape[2]), lambda i, s, st=step(r): (st(i), 0)) for r in recvd2]),
        out_shape=[jax.ShapeDtypeStruct(r.shape[1:], F32) for r in recvd2],
        compiler_params=_params(("arbitrary",)),
    )(idx, *grads, *recvd1, *recvd2)


def _sum8_call(parts):
    def body(p_ref, o_ref):
        s = p_ref[0]
        for j in range(1, N_DEV):
            s = s + p_ref[j]
        o_ref[...] = s

    return pl.pallas_call(body, name="sum_small_partials",
                          out_shape=jax.ShapeDtypeStruct(parts.shape[1:], parts.dtype))(parts)


def _adamw(w, g, m, v):
    m = ADAM_B1 * m + (1.0 - ADAM_B1) * g
    v = ADAM_B2 * v + (1.0 - ADAM_B2) * (g * g)
    m_hat = m / (1.0 - ADAM_B1 ** ADAM_STEP)
    v_hat = v / (1.0 - ADAM_B2 ** ADAM_STEP)
    delta = -ADAM_LR * (m_hat / (jnp.sqrt(v_hat) + ADAM_EPS) + ADAM_WD * w)
    return delta, m, v


def _adamw_call(ws, gs, ms, vs, nsteps, name):
    n = len(ws)

    def body(*refs):
        for t in range(n):
            w, g, m, v = (refs[k * n + t][...] for k in range(4))
            d, m2, v2 = _adamw(w, g, m, v)
            refs[4 * n + t][...] = d
            refs[5 * n + t][...] = m2
            refs[6 * n + t][...] = v2

    def spec(a):
        assert a.shape[0] % nsteps == 0 and (nsteps == 1 or (a.shape[0] // nsteps) % 8 == 0), a.shape
        return pl.BlockSpec((a.shape[0] // nsteps, a.shape[1]), lambda i: (i, 0))

    specs = [spec(a) for a in ws]
    outs = pl.pallas_call(
        body, name=name, grid=(nsteps,),
        in_specs=specs * 4, out_specs=specs * 3,
        out_shape=[jax.ShapeDtypeStruct(a.shape, F32) for a in ws] * 3,
        compiler_params=_params(("arbitrary",)),
    )(*ws, *gs, *ms, *vs)
    return outs[:n], outs[n:2 * n], outs[2 * n:]


def _adamw_rs_call(idx, gws, r1s, r2s, ws, ms, vs, nsteps, name):
    n = len(ws)

    def body(i_ref, *refs):
        for t in range(n):
            gw, r1, r2, w, m, v = (refs[k * n + t] for k in range(6))
            g = gw[0, 0] + r1[0]
            for r in range(3):
                g = g + r2[r].astype(F32)
            d, m2, v2 = _adamw(w[...], g, m[...], v[...])
            refs[6 * n + t][...] = g
            refs[7 * n + t][...] = d
            refs[8 * n + t][...] = m2
            refs[9 * n + t][...] = v2

    def rb(a):
        r = a.shape[0] // nsteps
        assert a.shape[0] % nsteps == 0 and r % 16 == 0, a.shape
        return r

    in_specs = [pl.BlockSpec((1, 1, rb(w), w.shape[1]), lambda i, s: (s[0], s[1], i, 0)) for w in ws]
    in_specs += [pl.BlockSpec((1, rb(w), w.shape[1]), lambda i, s: (s[0], i, 0)) for w in ws]
    in_specs += [pl.BlockSpec((3, rb(w), w.shape[1]), lambda i, s: (0, i, 0)) for w in ws]
    plain = [pl.BlockSpec((rb(w), w.shape[1]), lambda i, s: (i, 0)) for w in ws]
    outs = pl.pallas_call(
        body, name=name,
        grid_spec=pltpu.PrefetchScalarGridSpec(num_scalar_prefetch=1, grid=(nsteps,), in_specs=in_specs + plain * 3,
                                               out_specs=plain * 4),
        out_shape=[jax.ShapeDtypeStruct(w.shape, F32) for w in ws] * 4,
        compiler_params=_params(("arbitrary",)),
    )(idx, *gws, *r1s, *r2s, *ws, *ms, *vs)
    return outs[:n], outs[n:2 * n], outs[2 * n:3 * n], outs[3 * n:]


def _rows128(a, pad_rows):
    flat = a.reshape(-1).astype(F32)
    flat = jnp.pad(flat, (0, pad_rows * LANES - flat.shape[0]))
    return flat.reshape(pad_rows, LANES)


_SMALL_A = (("w_pool", 512), ("pool_scale", 8), ("attn_sinks", 8), ("g_mix_post", 8), ("g_mlp_pre", 8),
            ("g_mlp_post", 8), ("loss", 8), ("b_in_gates", 16))
_SMALL_A_ROWS = 640
_SMALL_B = (("g_mix_pre", 8), ("b_in_head", 16))


def _pack(parts, layout, total_rows):
    rows = [_rows128(parts[k], r) for k, r in layout]
    pad = total_rows - sum(r for _, r in layout)
    if pad:
        rows.append(jnp.zeros((pad, LANES), F32))
    return jnp.concatenate(rows, axis=0)


def _unpack(buf, layout, sizes):
    out, off = {}, 0
    for k, r in layout:
        out[k] = buf[off:off + r].reshape(-1)[:sizes[k]]
        off += r
    return out


def kernel(x, g_mix_pre, w_in, b_in, w_pool, pool_scale, attn_sinks, w_branch_pool, w_branch_attn, w_out, g_mix_post, g_mlp_pre, w_up, w_down, g_mlp_post, loss_target, m_g_mix_pre, m_w_in, m_b_in, m_w_pool, m_pool_scale, m_attn_sinks, m_w_branch_pool, m_w_branch_attn, m_w_out, m_g_mix_post, m_g_mlp_pre, m_w_up, m_w_down, m_g_mlp_post, v_g_mix_pre, v_w_in, v_b_in, v_w_pool, v_pool_scale, v_attn_sinks, v_w_branch_pool, v_w_branch_attn, v_w_out, v_g_mix_post, v_g_mlp_pre, v_w_up, v_w_down, v_g_mlp_post):
    B, S, _ = x.shape
    T = B * S
    xt = x.reshape(T, D_MODEL)
    tgt = loss_target.reshape(T, D_MODEL)
    cx, cy, cc = _coords()

    cidx = jnp.reshape(cc, (1,)).astype(jnp.int32)
    by_chip = lambda gr: gr.reshape((4, 2) + gr.shape[1:])
    bf = lambda w: w[0].astype(MXU_DTYPE)

    (win_s,) = _allgather_call([w_in[0].T.astype(MXU_DTYPE)])
    win_t = win_s.reshape(IN_WIDTH, D_MODEL)
    wpool_b = bf(w_pool)
    rc, rsa, rsb = _rot_tables(S)

    up_a, up_b = (0, D_MODEL // 2), (D_MODEL // 2, D_MODEL // 2)
    dn_a, dn_b = (0, D_FF // 16), (D_FF // 16, D_FF // 16)
    wup_l, wdown_l = bf(w_up), bf(w_down)
    (h, u, q, k4, v4, g), (wbp_1, wba_1, wout_1, wup_1) = _inproj_call(
        xt, g_mix_pre, win_t, b_in, rc, rsa, rsb, S,
        rider=_rider_ag([(bf(w_branch_pool), None, ALL, None), (bf(w_branch_attn), None, ALL, None),
                         (bf(w_out), None, ALL, None), (wup_l, None, up_a, None)]))
    yp = _pool_call(u, wpool_b, pool_scale, S)
    (ya,), (wbp_s, wba_s, wout_s, wup_2, wdown_1) = _attn_call(
        attn_sinks, q, k4, v4, S,
        rider=_rider_ag([(None, wbp_1, None, ALL), (None, wba_1, None, ALL), (None, wout_1, None, ALL),
                         (wup_l, wup_1, up_b, up_a), (wdown_l, None, dn_a, None)]))
    wout_f = wout_s.reshape(D_MODEL, D_MODEL)
    (mix, x1, h2), (wup_s, wdown_2) = _mix_fwd_call(
        yp, ya, g, xt, wbp_s, wba_s, wout_f, g_mix_post, g_mlp_pre,
        rider=_rider_ag([(None, wup_2, None, up_b), (wdown_l, wdown_1, dn_b, dn_a)]))
    (wdown_s,) = _comm_call(_rider_ag([(None, wdown_2, None, dn_b)]), "allgather_finish")

    act, da, dff, dx1, dg3, dg4, lossvec = _mlp_call(x1, h2, tgt, wup_s, wdown_s, g_mlp_pre, g_mlp_post)
    gw_down = by_chip(_wgrad_rows_call(act, dff, "wgrad_down")[0])
    (gw_up,), (r1_down,) = _wgrad_cols_call([(h2, da)], "wgrad_up", rider=_rider_rs_sibling([gw_down]))
    gw_up = by_chip(gw_up)
    (s_down,) = _chip_sum_call(cidx, [gw_down], [r1_down], [MXU_DTYPE], "rs_chip_sum_down")
    (c_down,), tok = _copies_start([_rider_rs_chips([s_down])], "rs_chips_start_down")
    (dyp, do, dgates, dg2, dbg, gw_out, gw_bp, gw_ba), (r1_up,) = _mix_bwd_call(
        dx1, mix, yp, ya, g, wbp_s, wba_s, wout_f, g_mix_post, rider=_after(tok, _rider_rs_sibling([gw_up])))
    gw_out = by_chip(gw_out.reshape(N_DEV, D_MODEL // N_DEV, D_MODEL))
    gw_bp, gw_ba = by_chip(gw_bp), by_chip(gw_ba)
    (s_up,) = _chip_sum_call(cidx, [gw_up], [r1_up], [MXU_DTYPE], "rs_chip_sum_up")
    (c_up,), tok = _copies_start([_rider_rs_chips([s_up])], "rs_chips_start_up")
    (dq, dk, dv, dsink), (r1_out, r1_bp, r1_ba) = _attn_bwd_call(
        attn_sinks, q, k4, v4, do, rc, rsa, rsb, S, rider=_after(tok, _rider_rs_sibling([gw_out, gw_bp, gw_ba])))
    s_obb = _chip_sum_call(cidx, [gw_out, gw_bp, gw_ba], [r1_out, r1_bp, r1_ba], [MXU_DTYPE] * 3, "rs_chip_sum_branch")
    (c_obb,), tok = _copies_start([_rider_rs_chips(s_obb)], "rs_chips_start_branch")
    (du, dwp, dps), _ = _pool_bwd_call(u, dyp, wpool_b, pool_scale, S, rider=_after(tok))
    (gw_in,) = _wgrad_in_call(du, dq, dk, dv, dgates, h)
    gw_in = by_chip(gw_in)

    small_a = {"w_pool": dwp, "pool_scale": dps,
               "attn_sinks": jnp.sum(dsink.reshape(B, 8, LANES)[:, 0, :N_Q_HEADS], axis=0), "g_mix_post": dg2,
               "g_mlp_pre": dg3, "g_mlp_post": dg4, "loss": lossvec, "b_in_gates": dbg}
    gw_sa = by_chip(_pack(small_a, _SMALL_A, _SMALL_A_ROWS).reshape(N_DEV, _SMALL_A_ROWS // N_DEV, LANES))
    r1_in, r1_sa = _comm_call(_rider_rs_sibling([gw_in, gw_sa]), "rs_sibling_in")
    s_in, s_sa = _chip_sum_call(cidx, [gw_in, gw_sa], [r1_in, r1_sa], [MXU_DTYPE, F32], "rs_chip_sum_in")
    (c_in,), tok = _copies_start([_rider_rs_chips([s_in, s_sa])], "rs_chips_start_in")
    (gx, dg1, dba_in), _ = _inproj_bwd_call(du, dq, dk, dv, dgates, dx1, xt, win_t, g_mix_pre, rider=_after(tok))
    r2_down, r2_up, r2_out, r2_bp, r2_ba, r2_in, r2_sa = _copies_wait([c_down, c_up, c_obb, c_in], dg1, "rs_chips_wait")

    idx = jnp.stack([2 * cx + cy, cc]).astype(jnp.int32)
    (g_sa,) = _final_sum_call(idx, [gw_sa], [r1_sa], [r2_sa])
    part_b = _pack({"g_mix_pre": dg1, "b_in_head": dba_in}, _SMALL_B, sum(r for _, r in _SMALL_B))
    sa_all, sb_all = _comm_call(_rider_gather_direct([g_sa, part_b]), "allgather_small")
    sb_sum = _sum8_call(sb_all)

    in_t = _adamw_rs_call(idx, [gw_in], [r1_in], [r2_in], [w_in[0].T], [m_w_in[0].T], [v_w_in[0].T], 2, "adamw_w_in")
    rest = _adamw_rs_call(
        idx, [gw_bp, gw_ba, gw_out, gw_up, gw_down], [r1_bp, r1_ba, r1_out, r1_up, r1_down],
        [r2_bp, r2_ba, r2_out, r2_up, r2_down], [w_branch_pool[0], w_branch_attn[0], w_out[0], w_up[0], w_down[0]],
        [m_w_branch_pool[0], m_w_branch_attn[0], m_w_out[0], m_w_up[0], m_w_down[0]],
        [v_w_branch_pool[0], v_w_branch_attn[0], v_w_out[0], v_w_up[0], v_w_down[0]], N_DEV, "adamw_shards")
    big_g, big_d, big_m2, big_v2 = ([a[0].T] + list(b) for a, b in zip(in_t, rest))

    names = ["g_mix_pre", "b_in", "w_pool", "pool_scale", "attn_sinks", "g_mix_post", "g_mlp_pre", "g_mlp_post"]
    sm_w = dict(g_mix_pre=g_mix_pre, b_in=b_in, w_pool=w_pool, pool_scale=pool_scale, attn_sinks=attn_sinks,
                g_mix_post=g_mix_post, g_mlp_pre=g_mlp_pre, g_mlp_post=g_mlp_post)
    sm_m = dict(g_mix_pre=m_g_mix_pre, b_in=m_b_in, w_pool=m_w_pool, pool_scale=m_pool_scale, attn_sinks=m_attn_sinks,
                g_mix_post=m_g_mix_post, g_mlp_pre=m_g_mlp_pre, g_mlp_post=m_g_mlp_post)
    sm_v = dict(g_mix_pre=v_g_mix_pre, b_in=v_b_in, w_pool=v_w_pool, pool_scale=v_pool_scale, attn_sinks=v_attn_sinks,
                g_mix_post=v_g_mix_post, g_mlp_pre=v_g_mlp_pre, g_mlp_post=v_g_mlp_post)
    sizes = {k: sm_w[k].size for k in names}
    sizes.update(loss=D_MODEL, b_in_gates=GATE_WIDTH, b_in_head=C_G)
    sm_g = _unpack(sa_all.reshape(_SMALL_A_ROWS, LANES), _SMALL_A, sizes)
    sm_g.update(_unpack(sb_sum, _SMALL_B, sizes))
    sm_g["b_in"] = jnp.concatenate([sm_g["b_in_head"], sm_g["b_in_gates"]])
    loss = (0.5 / D_MODEL) * jnp.sum(sm_g["loss"])
    two_d = lambda a: a.reshape(-1, a.shape[-1])
    sd_, sm2_, sv2_ = _adamw_call([two_d(sm_w[k]) for k in names], [two_d(sm_g[k].reshape(sm_w[k].shape)) for k in names],
                                  [two_d(sm_m[k]) for k in names], [two_d(sm_v[k]) for k in names], 1, "adamw_small")
    like = lambda vals: {k: a.reshape(sm_w[k].shape) for k, a in zip(names, vals)}
    sm_d, sm_m2, sm_v2 = like(sd_), like(sm2_), like(sv2_)
    sm_gr = {k: sm_g[k].reshape(sm_w[k].shape) for k in names}

    order = ["g_mix_pre", "w_in", "b_in", "w_pool", "pool_scale", "attn_sinks", "w_branch_pool", "w_branch_attn",
             "w_out", "g_mix_post", "g_mlp_pre", "w_up", "w_down", "g_mlp_post"]
    big_names = ["w_in", "w_branch_pool", "w_branch_attn", "w_out", "w_up", "w_down"]
    lead = lambda a: a[None]
    tables = []
    for small_t, big_t in ((sm_gr, big_g), (sm_d, big_d), (sm_m2, big_m2), (sm_v2, big_v2)):
        bt = dict(zip(big_names, big_t))
        tables.append([lead(bt[k]) if k in bt else small_t[k] for k in order])
    return (loss, gx.reshape(B, S, D_MODEL), *tables[0], *tables[1], *tables[2], *tables[3])
```

```python
import functools

import jax
import jax.numpy as jnp
from jax import lax
from jax.experimental import pallas as pl
from jax.experimental.pallas import tpu as pltpu

F32 = jnp.float32
MXU_DTYPE = jnp.bfloat16
MESH = pl.DeviceIdType.MESH

D_MODEL = 1024
POOL_WINDOWS = (2, 4, 8, 16)
POOL_WIDTH = 512
POOL_GC = 128
HEAD_DIM = 64
N_Q_HEADS = 8
N_KV_HEADS = 2
GROUP = 4
ATTN_WIDTH = 512
KV_WIDTH = 128
BLOCK = 128
GATE_WIDTH = 2048
IN_WIDTH = 3328
D_FF = 4096
EPS = 1e-6
NEG_INF = -1e30
ROPE_THETA = 500000.0
ROT_DIM = 16
SCALE = HEAD_DIM ** -0.5
C_Q, C_K, C_V, C_G = 512, 1024, 1152, 1280

ADAM_LR = 0.001
ADAM_B1 = 0.9
ADAM_B2 = 0.999
ADAM_EPS = 1e-08
ADAM_WD = 0.01
ADAM_STEP = 10

N_DEV = 8
LANES = 128
VMEM_LIMIT = 56 * 1024 * 1024

NN = (((1,), (0,)), ((), ()))
NT = (((1,), (1,)), ((), ()))
TN = (((0,), (0,)), ((), ()))


def _dot(a, b, dims):
    return lax.dot_general(a, b, dims, preferred_element_type=F32)


def _params(sem=None):
    return pltpu.CompilerParams(dimension_semantics=sem, vmem_limit_bytes=VMEM_LIMIT)


def _tile(n, pref):
    t = min(n, pref)
    assert n % t == 0, (n, t)
    return t


class _Rider:
    def __init__(self, ins, out_shape, n_remote, n_local, plan, aliases=None):
        self.ins, self.out_shape, self.n_remote, self.n_local = list(ins), list(out_shape), n_remote, n_local
        self.plan, self.aliases = plan, dict(aliases or {})


def _after(token, rider=None):
    r = rider or _Rider([], [], 0, 0, lambda ins, outs, send, recv, loc, r0, l0: ([], []))
    return _Rider(r.ins + [token], r.out_shape, r.n_remote, r.n_local, r.plan, r.aliases)


def _launch(body, args, *, name, grid, in_specs, out_specs, out_shape, scratch_shapes=(), sem=None, rider=None):
    if rider is None:
        return pl.pallas_call(body, name=name, grid=grid, in_specs=in_specs, out_specs=out_specs, out_shape=out_shape,
                              scratch_shapes=list(scratch_shapes), compiler_params=_params(sem))(*args)
    n_in, n_out, n_scr = len(args), len(out_shape), len(scratch_shapes)
    r_in, r_out = len(rider.ins), len(rider.out_shape)
    copies = rider.n_remote + rider.n_local > 0

    def wrapped(*refs):
        ins, rins = refs[:n_in], refs[n_in:n_in + r_in]
        o0 = n_in + r_in
        outs, routs = refs[o0:o0 + n_out], refs[o0 + n_out:o0 + n_out + r_out]
        s0 = o0 + n_out + r_out
        scr = refs[s0:s0 + n_scr]
        if not copies:
            return body(*ins, *outs, *scr)
        send, recv, loc = refs[s0 + n_scr:]
        first, last = None, None
        for d in range(len(grid)):
            f, l = pl.program_id(d) == 0, pl.program_id(d) == pl.num_programs(d) - 1
            first = f if first is None else first & f
            last = l if last is None else last & l

        def start():
            remote, local = rider.plan(rins, routs, send, recv, loc, 0, 0)
            for cp in local + remote:
                cp.start()

        def finish():
            remote, local = rider.plan(rins, routs, send, recv, loc, 0, 0)
            for cp in remote + local:
                cp.wait()

        if first is None:
            start()
            body(*ins, *outs, *scr)
            finish()
        else:
            pl.when(first)(start)
            body(*ins, *outs, *scr)
            pl.when(last)(finish)

    hbm = pl.BlockSpec(memory_space=pl.ANY)
    dma = pltpu.SemaphoreType.DMA
    res = pl.pallas_call(
        wrapped, name=name, grid=grid, in_specs=list(in_specs) + [hbm] * r_in,
        out_specs=list(out_specs) + [hbm] * r_out, out_shape=list(out_shape) + rider.out_shape,
        scratch_shapes=list(scratch_shapes) + (
            [dma((max(rider.n_remote, 1),)), dma((max(rider.n_remote, 1),)), dma((max(rider.n_local, 1),))] if copies else []),
        input_output_aliases={n_in + i: n_out + o for i, o in rider.aliases.items()},
        compiler_params=_params(sem),
    )(*args, *rider.ins)
    return list(res[:n_out]), list(res[n_out:])


def _comm_call(rider, name):
    return _launch(lambda: None, [], name=name, grid=(), in_specs=[], out_specs=[], out_shape=[], rider=rider)[1]


_HBM = pl.BlockSpec(memory_space=pltpu.HBM)
_SEM = pl.BlockSpec(memory_space=pltpu.SEMAPHORE)
_EFFECT = pltpu.SideEffectType.DATAFLOW_SIDE_EFFECTING


def _copies_start(riders, name):
    assert all(r.n_local == 0 and not r.aliases for r in riders)
    sizes = [(len(r.ins), len(r.out_shape)) for r in riders]
    bufs = []
    for r in riders:
        bufs += [pltpu.with_memory_space_constraint(a, pltpu.HBM) for a in r.ins]
        bufs += [pltpu.with_memory_space_constraint(lax.empty(s.shape, s.dtype), pltpu.HBM) for s in r.out_shape]
    nb, ng = len(bufs), len(riders)

    def body(*refs):
        sems, token, at = refs[2 * nb:2 * nb + 2 * ng], refs[-1], 0
        for g, (r, (ni, no)) in enumerate(zip(riders, sizes)):
            remote, _ = r.plan(refs[at:at + ni], refs[at + ni:at + ni + no], sems[2 * g], sems[2 * g + 1], None, 0, 0)
            for cp in remote:
                cp.start()
            at += ni + no
        token[...] = jnp.zeros_like(token)

    res = pl.pallas_call(
        body, name=name, in_specs=[_HBM] * nb,
        out_specs=[_HBM] * nb + [_SEM] * (2 * ng) + [pl.BlockSpec(memory_space=pltpu.VMEM)],
        out_shape=[pltpu.HBM(a.shape, a.dtype) for a in bufs]
        + [pltpu.SemaphoreType.DMA((r.n_remote,)) for r in riders for _ in range(2)]
        + [jax.ShapeDtypeStruct((8, LANES), F32)],
        input_output_aliases={i: i for i in range(nb)},
        compiler_params=pltpu.CompilerParams(has_side_effects=_EFFECT),
    )(*bufs)
    handles, at = [], 0
    for g, (r, (ni, no)) in enumerate(zip(riders, sizes)):
        handles.append((r, list(res[at:at + ni + no]), res[nb + 2 * g], res[nb + 2 * g + 1]))
        at += ni + no
    return handles, res[-1]


def _copies_wait(handles, after, name):
    bufs = [b for _, bs, _, _ in handles for b in bs]
    sems = [s for _, _, send, recv in handles for s in (send, recv)]
    nb, ng = len(bufs), len(handles)

    def body(*refs):
        at = 0
        for g, (rider, bs, _, _) in enumerate(handles):
            ni = len(rider.ins)
            remote, _ = rider.plan(refs[at:at + ni], refs[at + ni:at + len(bs)], refs[nb + 2 * g], refs[nb + 2 * g + 1],
                                   None, 0, 0)
            for cp in remote:
                cp.wait_send()
                cp.wait_recv()
            at += len(bs)

    res = pl.pallas_call(
        body, name=name, in_specs=[_HBM] * nb + [_SEM] * (2 * ng) + [pl.BlockSpec(memory_space=pl.ANY)],
        out_specs=[_HBM] * nb, out_shape=[pltpu.HBM(a.shape, a.dtype) for a in bufs],
        input_output_aliases={i: i for i in range(nb)},
        compiler_params=pltpu.CompilerParams(has_side_effects=_EFFECT),
    )(*bufs, *sems, after)
    lands, at = [], 0
    for rider, bs, _, _ in handles:
        lands += list(res[at + len(rider.ins):at + len(bs)])
        at += len(bs)
    return lands


def _rms_r(x):
    return lax.rsqrt(jnp.mean(x * x, axis=-1, keepdims=True) + EPS)


def _rms_bwd(dn, x, r, g):
    xh = x * r
    dxh = dn * g
    dx = r * (dxh - xh * jnp.mean(dxh * xh, axis=-1, keepdims=True))
    return dx, dn * xh


def _rot(t, c, sa, sb):
    outs = []
    for j in range(t.shape[1] // LANES):
        tj = t[:, LANES * j:LANES * (j + 1)]
        outs.append(tj * c + pltpu.roll(tj, LANES - 8, 1) * sa + pltpu.roll(tj, 8, 1) * sb)
    return outs[0] if len(outs) == 1 else jnp.concatenate(outs, axis=1)


def _rot_tables(S):
    pos = jnp.arange(S, dtype=F32)
    inv_freq = ROPE_THETA ** (-jnp.arange(0, ROT_DIM, 2, dtype=F32) / ROT_DIM)
    ang = pos[:, None] * inv_freq[None, :]
    cos, sin = jnp.cos(ang), jnp.sin(ang)
    one = jnp.ones((S, HEAD_DIM - ROT_DIM), F32)
    zero = jnp.zeros((S, HEAD_DIM - ROT_DIM), F32)
    z8 = jnp.zeros((S, 8), F32)
    c = jnp.concatenate([cos, cos, one], axis=1)
    sa = jnp.concatenate([-sin, z8, zero], axis=1)
    sb = jnp.concatenate([z8, sin, zero], axis=1)
    rep = LANES // HEAD_DIM
    return jnp.tile(c, (1, rep)), jnp.tile(sa, (1, rep)), jnp.tile(sb, (1, rep))


def _lane_tile4(k):
    lane = lax.broadcasted_iota(jnp.int32, k.shape, 1)
    rk = pltpu.roll(k, HEAD_DIM, 1)
    x0 = jnp.where(lane < HEAD_DIM, k, rk)
    x1 = jnp.where(lane < HEAD_DIM, rk, k)
    return jnp.concatenate([x0, x0, x1, x1], axis=1)


def _fold_heads(acc):
    zs = []
    for hk in range(N_KV_HEADS):
        a = acc[:, 256 * hk:256 * hk + LANES] + acc[:, 256 * hk + LANES:256 * (hk + 1)]
        zs.append(a + pltpu.roll(a, HEAD_DIM, 1))
    lane = lax.broadcasted_iota(jnp.int32, zs[0].shape, 1)
    return jnp.where(lane < HEAD_DIM, zs[0], zs[1])


def _inproj_call(x, g1, win_t, b_in, rc, rsa, rsb, S, rider=None):
    T = x.shape[0]
    tm = _tile(S, 512)
    nst = S // tm

    def body(x_ref, g1_ref, w_ref, b_ref, c_ref, sa_ref, sb_ref,
             h_ref, u_ref, q_ref, k4_ref, v4_ref, g_ref):
        xv = x_ref[...]
        hb = ((xv * _rms_r(xv)) * g1_ref[...]).astype(MXU_DTYPE)
        h_ref[...] = hb

        def proj(lo, hi):
            return _dot(hb, w_ref[lo:hi, :], NT) + b_ref[:, lo:hi]

        c, sa, sb = c_ref[...], sa_ref[...], sb_ref[...]
        u_ref[...] = proj(0, C_Q)
        q_ref[...] = (_rot(proj(C_Q, C_K), c, sa, sb) * SCALE).astype(MXU_DTYPE)
        kv = proj(C_K, C_G)
        k4_ref[...] = _lane_tile4(_rot(kv[:, :KV_WIDTH], c, sa, sb)).astype(MXU_DTYPE)
        v4_ref[...] = _lane_tile4(kv[:, KV_WIDTH:]).astype(MXU_DTYPE)
        g_ref[...] = jax.nn.sigmoid(proj(C_G, IN_WIDTH)).astype(MXU_DTYPE)

    tok = lambda w: pl.BlockSpec((tm, w), lambda i: (i, 0))
    full = lambda a: pl.BlockSpec(a.shape, lambda i: (0,) * a.ndim)
    tab = pl.BlockSpec((tm, LANES), lambda i: (i % nst, 0))
    return _launch(
        body, [x, g1, win_t, b_in, rc, rsa, rsb], name="inproj_fwd", grid=(T // tm,),
        in_specs=[tok(D_MODEL), full(g1), full(win_t), full(b_in), tab, tab, tab],
        out_specs=[tok(D_MODEL), tok(POOL_WIDTH), tok(ATTN_WIDTH), tok(512), tok(512), tok(GATE_WIDTH)],
        out_shape=[jax.ShapeDtypeStruct((T, D_MODEL), MXU_DTYPE), jax.ShapeDtypeStruct((T, POOL_WIDTH), F32),
                   jax.ShapeDtypeStruct((T, ATTN_WIDTH), MXU_DTYPE), jax.ShapeDtypeStruct((T, 512), MXU_DTYPE),
                   jax.ShapeDtypeStruct((T, 512), MXU_DTYPE), jax.ShapeDtypeStruct((T, GATE_WIDTH), MXU_DTYPE)],
        sem=("arbitrary",), rider=rider)


def _shift_rows(a, k, rows):
    n = a.shape[0]
    if k > 0:
        return jnp.where(rows >= k, pltpu.roll(a, k, 0), 0.0)
    return jnp.where(rows < n + k, pltpu.roll(a, n + k, 0), 0.0)


def _win_sum(a, w, rows, sign):
    s, k = a, 1
    while k < w:
        s = s + _shift_rows(s, sign * k, rows)
        k *= 2
    return s


def _pool_diff(ug, w, rows):
    inv = 1.0 / jnp.minimum(rows + 1, w).astype(F32)
    return _win_sum(ug, w, rows, 1) * inv - ug, inv


def _pool_call(u, w_pool, pool_scale, S):
    T = u.shape[0]

    def body(u_ref, w_ref, ps_ref, y_ref):
        rows = lax.broadcasted_iota(jnp.int32, (S, POOL_GC), 0)
        for gi, w in enumerate(POOL_WINDOWS):
            sl = slice(POOL_GC * gi, POOL_GC * (gi + 1))
            diff, _ = _pool_diff(u_ref[:, sl], w, rows)
            mixed = _dot(diff.astype(MXU_DTYPE), w_ref[gi], NN)
            y_ref[:, sl] = (mixed * ps_ref[:, sl]).astype(MXU_DTYPE)

    seq = pl.BlockSpec((S, POOL_WIDTH), lambda b: (b, 0))
    return pl.pallas_call(
        body, name="pool_fwd", grid=(T // S,),
        in_specs=[seq, pl.BlockSpec(w_pool.shape, lambda b: (0, 0, 0)), pl.BlockSpec(pool_scale.shape, lambda b: (0, 0))],
        out_specs=seq, out_shape=jax.ShapeDtypeStruct((T, POOL_WIDTH), MXU_DTYPE),
        compiler_params=_params(("arbitrary",)),
    )(u, w_pool, pool_scale)


def _pool_bwd_call(u, dyp, w_pool, pool_scale, S, rider=None):
    T = u.shape[0]

    def body(u_ref, dy_ref, w_ref, ps_ref, du_ref, dw_ref, dps_ref):
        @pl.when(pl.program_id(0) == 0)
        def _():
            dw_ref[...] = jnp.zeros_like(dw_ref)
            dps_ref[...] = jnp.zeros_like(dps_ref)

        rows = lax.broadcasted_iota(jnp.int32, (S, POOL_GC), 0)
        for gi, w in enumerate(POOL_WINDOWS):
            sl = slice(POOL_GC * gi, POOL_GC * (gi + 1))
            diff, inv = _pool_diff(u_ref[:, sl], w, rows)
            diffb = diff.astype(MXU_DTYPE)
            wg = w_ref[gi]
            mixed = _dot(diffb, wg, NN)
            dy = dy_ref[:, sl]
            dps_ref[:, sl] += jnp.sum(dy * mixed, axis=0, keepdims=True)
            dmb = (dy * ps_ref[:, sl]).astype(MXU_DTYPE)
            dw_ref[gi] += _dot(diffb, dmb, TN)
            ddiff = _dot(dmb, wg, NT)
            du_ref[:, sl] = (_win_sum(ddiff * inv, w, rows, -1) - ddiff).astype(MXU_DTYPE)

    seq = pl.BlockSpec((S, POOL_WIDTH), lambda b: (b, 0))
    return _launch(
        body, [u, dyp, w_pool, pool_scale], name="pool_bwd", grid=(T // S,),
        in_specs=[seq, seq, pl.BlockSpec(w_pool.shape, lambda b: (0, 0, 0)), pl.BlockSpec(pool_scale.shape, lambda b: (0, 0))],
        out_specs=[seq, pl.BlockSpec(w_pool.shape, lambda b: (0, 0, 0)), pl.BlockSpec(pool_scale.shape, lambda b: (0, 0))],
        out_shape=[jax.ShapeDtypeStruct((T, POOL_WIDTH), MXU_DTYPE), jax.ShapeDtypeStruct(w_pool.shape, F32),
                   jax.ShapeDtypeStruct(pool_scale.shape, F32)],
        sem=("arbitrary",), rider=rider)


def _attn_consts():
    lane_g = lax.broadcasted_iota(jnp.int32, (BLOCK, 256), 1) >> 6
    rgrp = lax.broadcasted_iota(jnp.int32, (GROUP * BLOCK, 1), 0) >> 7
    rel = lax.broadcasted_iota(jnp.int32, (BLOCK, 256), 0) - lax.broadcasted_iota(jnp.int32, (BLOCK, 256), 1)

    def bias(off):
        ok = (rel + off >= 0) & (rel + off < BLOCK)
        return jnp.concatenate([jnp.where(ok, 0.0, NEG_INF)] * GROUP, axis=0)

    return lane_g, rgrp, bias(0), bias(BLOCK)


def _sink_rows(sink_ref, hk, rgrp):
    sv = jnp.zeros(rgrp.shape, F32)
    for g in range(GROUP):
        sv = jnp.where(rgrp == g, sink_ref[0, GROUP * hk + g], sv)
    return sv


def _stack_heads(xb, lane_g):
    return jnp.concatenate([jnp.where(lane_g == g, xb, jnp.zeros_like(xb)) for g in range(GROUP)], axis=0)


def _unstack_heads(xs, lane_g):
    out = jnp.where(lane_g == 0, xs[0:BLOCK], 0.0)
    for g in range(1, GROUP):
        out = out + jnp.where(lane_g == g, xs[BLOCK * g:BLOCK * (g + 1)], 0.0)
    return out


def _attn_probs(qs, kb, bias, sv):
    s = _dot(qs, kb, NT) + bias
    m = jnp.maximum(jnp.max(s, axis=1, keepdims=True), sv)
    e = jnp.exp(s - m)
    es = jnp.exp(sv - m)
    inv_l = 1.0 / (jnp.sum(e, axis=1, keepdims=True) + es)
    return e * inv_l, es * inv_l


def _attn_blocks(nb, blk, carry):
    carry = blk(0, 0, True, carry)
    return lax.fori_loop(1, nb, lambda n, c: blk(pl.multiple_of(n * BLOCK, BLOCK),
                                                 pl.multiple_of((n - 1) * BLOCK, BLOCK), False, c), carry)


def _attn_call(sinks, q, k4, v4, S, rider=None):
    T = q.shape[0]
    nb = S // BLOCK

    def body(sink_ref, q_ref, k_ref, v_ref, o_ref):
        lane_g, rgrp, bias_first, bias_later = _attn_consts()
        svs = [_sink_rows(sink_ref, hk, rgrp) for hk in range(N_KV_HEADS)]

        def blk(q0, k0, first, carry):
            for hk in range(N_KV_HEADS):
                cs = slice(256 * hk, 256 * (hk + 1))
                qs = _stack_heads(q_ref[pl.ds(q0, BLOCK), cs], lane_g)
                p, _ = _attn_probs(qs, k_ref[pl.ds(k0, 2 * BLOCK), cs], bias_first if first else bias_later, svs[hk])
                o = _dot(p.astype(MXU_DTYPE), v_ref[pl.ds(k0, 2 * BLOCK), cs], NN)
                o_ref[pl.ds(q0, BLOCK), cs] = _unstack_heads(o, lane_g).astype(MXU_DTYPE)
            return carry

        _attn_blocks(nb, blk, 0)

    seq = pl.BlockSpec((S, ATTN_WIDTH), lambda b: (b, 0))
    return _launch(
        body, [sinks, q, k4, v4], name="attn_fwd", grid=(T // S,),
        in_specs=[pl.BlockSpec(memory_space=pltpu.SMEM), seq, seq, seq],
        out_specs=[seq], out_shape=[jax.ShapeDtypeStruct((T, ATTN_WIDTH), MXU_DTYPE)],
        sem=("arbitrary",), rider=rider)


def _attn_bwd_call(sinks, q, k4, v4, do, rc, rsa, rsb, S, rider=None):
    T = q.shape[0]
    nb = S // BLOCK

    def body(sink_ref, q_ref, k_ref, v_ref, do_ref, c_ref, sa_ref, sb_ref,
             dq_ref, dk_ref, dv_ref, ds_ref, dk_acc, dv_acc):
        lane_g, rgrp, bias_first, bias_later = _attn_consts()
        svs = [_sink_rows(sink_ref, hk, rgrp) for hk in range(N_KV_HEADS)]
        lane1 = lax.broadcasted_iota(jnp.int32, (1, LANES), 1)
        dk_acc[...] = jnp.zeros_like(dk_acc)
        dv_acc[...] = jnp.zeros_like(dv_acc)

        def blk(q0, k0, first, dsink):
            rows = pl.ds(q0, BLOCK)
            c, sa, sb = c_ref[rows, :], sa_ref[rows, :], sb_ref[rows, :]
            for hk in range(N_KV_HEADS):
                cs = slice(256 * hk, 256 * (hk + 1))
                qs = _stack_heads(q_ref[rows, cs], lane_g)
                dos = _stack_heads(do_ref[rows, cs], lane_g)
                kb = k_ref[pl.ds(k0, 2 * BLOCK), cs]
                vb = v_ref[pl.ds(k0, 2 * BLOCK), cs]
                p, ps = _attn_probs(qs, kb, bias_first if first else bias_later, svs[hk])
                dp = _dot(dos, vb, NT)
                delta = jnp.sum(p * dp, axis=1, keepdims=True)
                dsb = (p * (dp - delta)).astype(MXU_DTYPE)
                dqb = _unstack_heads(_dot(dsb, kb, NN), lane_g) * SCALE
                dq_ref[rows, cs] = _rot(dqb, c, -sa, -sb).astype(MXU_DTYPE)
                dk_acc[pl.ds(k0, 2 * BLOCK), cs] += _dot(dsb, qs, TN)
                dv_acc[pl.ds(k0, 2 * BLOCK), cs] += _dot(p.astype(MXU_DTYPE), dos, TN)
                psd = ps * delta
                for g in range(GROUP):
                    val = -jnp.sum(psd[BLOCK * g:BLOCK * (g + 1)], axis=0, keepdims=True)
                    dsink = dsink + jnp.where(lane1 == GROUP * hk + g, val, 0.0)
            return dsink

        dsink = _attn_blocks(nb, blk, jnp.zeros((1, LANES), F32))
        dk_ref[...] = _rot(_fold_heads(dk_acc[...]), c_ref[...], -sa_ref[...], -sb_ref[...]).astype(MXU_DTYPE)
        dv_ref[...] = _fold_heads(dv_acc[...]).astype(MXU_DTYPE)
        ds_ref[...] = jnp.broadcast_to(dsink, ds_ref.shape)

    seq = pl.BlockSpec((S, ATTN_WIDTH), lambda b: (b, 0))
    kvs = pl.BlockSpec((S, KV_WIDTH), lambda b: (b, 0))
    tab = pl.BlockSpec((S, LANES), lambda b: (0, 0))
    nseq = T // S
    return _launch(
        body, [sinks, q, k4, v4, do, rc, rsa, rsb], name="attn_bwd", grid=(nseq,),
        in_specs=[pl.BlockSpec(memory_space=pltpu.SMEM), seq, seq, seq, seq, tab, tab, tab],
        out_specs=[seq, kvs, kvs, pl.BlockSpec((8, LANES), lambda b: (b, 0))],
        out_shape=[jax.ShapeDtypeStruct((T, ATTN_WIDTH), MXU_DTYPE), jax.ShapeDtypeStruct((T, KV_WIDTH), MXU_DTYPE),
                   jax.ShapeDtypeStruct((T, KV_WIDTH), MXU_DTYPE), jax.ShapeDtypeStruct((8 * nseq, LANES), F32)],
        scratch_shapes=[pltpu.VMEM((S, 512), F32), pltpu.VMEM((S, 512), F32)],
        sem=("arbitrary",), rider=rider)


def _branch_weights(wbp_ref, wba_ref, wbp_s, wba_s):
    @pl.when(pl.program_id(0) == 0)
    def _():
        for j in range(N_DEV):
            wbp_s[:, LANES * j:LANES * (j + 1)] = wbp_ref[j]
            wba_s[:, LANES * j:LANES * (j + 1)] = wba_ref[j]


def _mix_fwd_call(yp, ya, g, x, wbp, wba, wout, g2, g3, rider=None):
    T = x.shape[0]
    tm = _tile(T, 512)

    def body(yp_ref, ya_ref, g_ref, x_ref, wbp_ref, wba_ref, wout_ref, g2_ref, g3_ref,
             mix_ref, x1_ref, h2_ref, h2t_ref, wbp_s, wba_s):
        _branch_weights(wbp_ref, wba_ref, wbp_s, wba_s)
        bp = _dot(yp_ref[...], wbp_s[...], NN)
        ba = _dot(ya_ref[...], wba_s[...], NN)
        merged = g_ref[:, :D_MODEL].astype(F32) * bp + g_ref[:, D_MODEL:].astype(F32) * ba
        mix = _dot(merged.astype(MXU_DTYPE), wout_ref[...], NN)
        mix_ref[...] = mix
        x1 = x_ref[...] + (mix * _rms_r(mix)) * g2_ref[...]
        x1_ref[...] = x1
        h2 = (x1 * _rms_r(x1)) * g3_ref[...]
        h2_ref[...] = h2.astype(MXU_DTYPE)
        h2t_ref[...] = h2.T.astype(MXU_DTYPE)

    tok = lambda w: pl.BlockSpec((tm, w), lambda i: (i, 0))
    full = lambda a: pl.BlockSpec(a.shape, lambda i: (0,) * a.ndim)
    return _launch(
        body, [yp, ya, g, x, wbp, wba, wout, g2, g3], name="mix_fwd", grid=(T // tm,),
        in_specs=[tok(POOL_WIDTH), tok(ATTN_WIDTH), tok(GATE_WIDTH), tok(D_MODEL), full(wbp), full(wba), full(wout),
                  full(g2), full(g3)],
        out_specs=[tok(D_MODEL), tok(D_MODEL), tok(D_MODEL), pl.BlockSpec((D_MODEL, tm), lambda i: (0, i))],
        out_shape=[jax.ShapeDtypeStruct((T, D_MODEL), F32), jax.ShapeDtypeStruct((T, D_MODEL), F32),
                   jax.ShapeDtypeStruct((T, D_MODEL), MXU_DTYPE), jax.ShapeDtypeStruct((D_MODEL, T), MXU_DTYPE)],
        scratch_shapes=[pltpu.VMEM((POOL_WIDTH, D_MODEL), MXU_DTYPE), pltpu.VMEM((ATTN_WIDTH, D_MODEL), MXU_DTYPE)],
        sem=("arbitrary",), rider=rider)


def _mix_bwd_call(dx1, mix, yp, ya, g, wbp, wba, wout, g2, rider=None):
    T = dx1.shape[0]
    tm = _tile(T, 256)

    def body(dx1_ref, mix_ref, yp_ref, ya_ref, g_ref, wbp_ref, wba_ref, wout_ref, g2_ref,
             dyp_ref, do_ref, dgates_ref, dg2_ref, dbg_ref, gout_ref, gbp_ref, gba_ref,
             wbp_s, wba_s, acc_out, acc_bp, acc_ba, sem):
        _branch_weights(wbp_ref, wba_ref, wbp_s, wba_s)
        step = pl.program_id(0)

        @pl.when(step == 0)
        def _():
            dg2_ref[...] = jnp.zeros_like(dg2_ref)
            dbg_ref[...] = jnp.zeros_like(dbg_ref)
            acc_out[...] = jnp.zeros_like(acc_out)
            acc_bp[...] = jnp.zeros_like(acc_bp)
            acc_ba[...] = jnp.zeros_like(acc_ba)

        mix = mix_ref[...]
        dmix, dg2 = _rms_bwd(dx1_ref[...], mix, _rms_r(mix), g2_ref[...])
        dg2_ref[...] += jnp.sum(dg2, axis=0, keepdims=True)
        dmixb = dmix.astype(MXU_DTYPE)
        dmerged = _dot(dmixb, wout_ref[...], NT)
        yp, ya = yp_ref[...], ya_ref[...]
        bp = _dot(yp, wbp_s[...], NN)
        ba = _dot(ya, wba_s[...], NN)
        gp, ga = g_ref[:, :D_MODEL].astype(F32), g_ref[:, D_MODEL:].astype(F32)
        acc_out[...] += _dot((gp * bp + ga * ba).astype(MXU_DTYPE), dmixb, TN)
        dgp = dmerged * bp * (gp * (1.0 - gp))
        dga = dmerged * ba * (ga * (1.0 - ga))
        dbg_ref[:, :D_MODEL] += jnp.sum(dgp, axis=0, keepdims=True)
        dbg_ref[:, D_MODEL:] += jnp.sum(dga, axis=0, keepdims=True)
        dgates_ref[:, :D_MODEL] = dgp.astype(MXU_DTYPE)
        dgates_ref[:, D_MODEL:] = dga.astype(MXU_DTYPE)
        dbp = (dmerged * gp).astype(MXU_DTYPE)
        dba = (dmerged * ga).astype(MXU_DTYPE)
        acc_bp[...] += _dot(yp, dbp, TN)
        acc_ba[...] += _dot(ya, dba, TN)
        dyp_ref[...] = _dot(dbp, wbp_s[...], NT)
        do_ref[...] = _dot(dba, wba_s[...], NT).astype(MXU_DTYPE)

        @pl.when(step == pl.num_programs(0) - 1)
        def _():
            copies = [pltpu.make_async_copy(acc_out, gout_ref, sem.at[0])]
            for j in range(N_DEV):
                cols = slice(LANES * j, LANES * (j + 1))
                copies.append(pltpu.make_async_copy(acc_bp.at[:, cols], gbp_ref.at[j], sem.at[1 + j]))
                copies.append(pltpu.make_async_copy(acc_ba.at[:, cols], gba_ref.at[j], sem.at[1 + N_DEV + j]))
            for cp in copies:
                cp.start()
            for cp in copies:
                cp.wait()

    tok = lambda w: pl.BlockSpec((tm, w), lambda i: (i, 0))
    full = lambda a: pl.BlockSpec(a.shape, lambda i: (0,) * a.ndim)
    acc = lambda w: pl.BlockSpec((1, w), lambda i: (0, 0))
    hbm = pl.BlockSpec(memory_space=pl.ANY)
    sd = jax.ShapeDtypeStruct
    return _launch(
        body, [dx1, mix, yp, ya, g, wbp, wba, wout, g2], name="mix_bwd", grid=(T // tm,),
        in_specs=[tok(D_MODEL), tok(D_MODEL), tok(POOL_WIDTH), tok(ATTN_WIDTH), tok(GATE_WIDTH), full(wbp), full(wba),
                  full(wout), full(g2)],
        out_specs=[tok(POOL_WIDTH), tok(ATTN_WIDTH), tok(GATE_WIDTH), acc(D_MODEL), acc(GATE_WIDTH), hbm, hbm, hbm],
        out_shape=[sd((T, POOL_WIDTH), F32), sd((T, ATTN_WIDTH), MXU_DTYPE), sd((T, GATE_WIDTH), MXU_DTYPE),
                   sd((1, D_MODEL), F32), sd((1, GATE_WIDTH), F32), sd((D_MODEL, D_MODEL), F32),
                   sd((N_DEV, POOL_WIDTH, LANES), F32), sd((N_DEV, ATTN_WIDTH, LANES), F32)],
        scratch_shapes=[pltpu.VMEM((POOL_WIDTH, D_MODEL), MXU_DTYPE), pltpu.VMEM((ATTN_WIDTH, D_MODEL), MXU_DTYPE),
                        pltpu.VMEM((D_MODEL, D_MODEL), F32), pltpu.VMEM((POOL_WIDTH, D_MODEL), F32),
                        pltpu.VMEM((ATTN_WIDTH, D_MODEL), F32), pltpu.SemaphoreType.DMA((1 + 2 * N_DEV,))],
        sem=("arbitrary",), rider=rider)


def _mlp_call(x1, h2, target, wup, wdown, g3, g4):
    T = x1.shape[0]
    tm = _tile(T, 256)
    fc = D_FF // N_DEV

    def body(x1_ref, h2_ref, t_ref, wup_ref, wdown_ref, g3_ref, g4_ref,
             act_ref, da_ref, dff_ref, dx1_ref, dg3_ref, dg4_ref, loss_ref, rl_s):
        @pl.when(pl.program_id(0) == 0)
        def _():
            dg3_ref[...] = jnp.zeros_like(dg3_ref)
            dg4_ref[...] = jnp.zeros_like(dg4_ref)
            loss_ref[...] = jnp.zeros_like(loss_ref)

        h2 = h2_ref[...]
        ff = jnp.zeros((tm, D_MODEL), F32)
        for j in range(N_DEV):
            sl = slice(fc * j, fc * (j + 1))
            rl = jnp.maximum(_dot(h2, wup_ref[j], NN), 0.0)
            rl_s[:, sl] = rl
            act = rl * rl
            act_ref[sl, :] = act.T.astype(MXU_DTYPE)
            ff = ff + _dot(act.astype(MXU_DTYPE), wdown_ref[j], NN)
        x1 = x1_ref[...]
        r4 = _rms_r(ff)
        err = x1 + (ff * r4) * g4_ref[...] - t_ref[...]
        loss_ref[...] += jnp.sum(err * err, axis=0, keepdims=True)
        dy = err * (1.0 / D_MODEL)
        dff, dg4 = _rms_bwd(dy, ff, r4, g4_ref[...])
        dg4_ref[...] += jnp.sum(dg4, axis=0, keepdims=True)
        dffb = dff.astype(MXU_DTYPE)
        dff_ref[...] = dffb
        dh2 = jnp.zeros((tm, D_MODEL), F32)
        for j in range(N_DEV):
            sl = slice(fc * j, fc * (j + 1))
            dab = (_dot(dffb, wdown_ref[j], NT) * (2.0 * rl_s[:, sl])).astype(MXU_DTYPE)
            da_ref[:, sl] = dab
            dh2 = dh2 + _dot(dab, wup_ref[j], NT)
        dx1, dg3 = _rms_bwd(dh2, x1, _rms_r(x1), g3_ref[...])
        dg3_ref[...] += jnp.sum(dg3, axis=0, keepdims=True)
        dx1_ref[...] = dy + dx1

    tok = lambda w: pl.BlockSpec((tm, w), lambda i: (i, 0))
    full = lambda a: pl.BlockSpec(a.shape, lambda i: (0,) * a.ndim, pipeline_mode=pl.Buffered(1))
    vec = pl.BlockSpec((1, D_MODEL), lambda i: (0, 0))
    sd = jax.ShapeDtypeStruct
    return pl.pallas_call(
        body, name="mlp_fwd_bwd", grid=(T // tm,),
        in_specs=[tok(D_MODEL), tok(D_MODEL), tok(D_MODEL), full(wup), full(wdown), vec, vec],
        out_specs=[pl.BlockSpec((D_FF, tm), lambda i: (0, i)), tok(D_FF), tok(D_MODEL), tok(D_MODEL), vec, vec, vec],
        out_shape=[sd((D_FF, T), MXU_DTYPE), sd((T, D_FF), MXU_DTYPE), sd((T, D_MODEL), MXU_DTYPE),
                   sd((T, D_MODEL), F32), sd((1, D_MODEL), F32), sd((1, D_MODEL), F32), sd((1, D_MODEL), F32)],
        scratch_shapes=[pltpu.VMEM((tm, D_FF), F32)],
        compiler_params=_params(("arbitrary",)),
    )(x1, h2, target, wup, wdown, g3, g4)


def _inproj_bwd_call(du, dq, dk, dv, dgates, dx1, x, win_t, g1, rider=None):
    T = x.shape[0]
    tm = _tile(T, 512)

    def body(du_ref, dq_ref, dk_ref, dv_ref, dgt_ref, dx1_ref, x_ref, w_ref, g1_ref, gx_ref, dg1_ref, db_ref):
        @pl.when(pl.program_id(0) == 0)
        def _():
            dg1_ref[...] = jnp.zeros_like(dg1_ref)
            db_ref[...] = jnp.zeros_like(db_ref)

        dh = jnp.zeros((tm, D_MODEL), F32)
        for ref, lo, hi in ((du_ref, 0, C_Q), (dq_ref, C_Q, C_K), (dk_ref, C_K, C_V), (dv_ref, C_V, C_G),
                            (dgt_ref, C_G, IN_WIDTH)):
            piece = ref[...]
            dh = dh + _dot(piece, w_ref[lo:hi, :], NN)
            if hi <= C_G:
                db_ref[:, lo:hi] += jnp.sum(piece.astype(F32), axis=0, keepdims=True)
        xv = x_ref[...]
        dx, dg1 = _rms_bwd(dh, xv, _rms_r(xv), g1_ref[...])
        dg1_ref[...] += jnp.sum(dg1, axis=0, keepdims=True)
        gx_ref[...] = dx1_ref[...] + dx

    tok = lambda w: pl.BlockSpec((tm, w), lambda i: (i, 0))
    full = lambda a: pl.BlockSpec(a.shape, lambda i: (0,) * a.ndim)
    sd = jax.ShapeDtypeStruct
    return _launch(
        body, [du, dq, dk, dv, dgates, dx1, x, win_t, g1], name="inproj_bwd", grid=(T // tm,),
        in_specs=[tok(POOL_WIDTH), tok(ATTN_WIDTH), tok(KV_WIDTH), tok(KV_WIDTH), tok(GATE_WIDTH), tok(D_MODEL),
                  tok(D_MODEL), full(win_t), full(g1)],
        out_specs=[tok(D_MODEL), pl.BlockSpec((1, D_MODEL), lambda i: (0, 0)), pl.BlockSpec((1, C_G), lambda i: (0, 0))],
        out_shape=[sd((T, D_MODEL), F32), sd((1, D_MODEL), F32), sd((1, C_G), F32)],
        sem=("arbitrary",), rider=rider)


WGRAD_TOKENS = 1024


def _wgrad_rows_call(at, b, name, rider=None):
    K, T = at.shape
    N = b.shape[1]
    tm = _tile(T, WGRAD_TOKENS)
    kb = min(K, 1024)
    per = kb // (K // N_DEV)

    def body(a_ref, b_ref, o_ref):
        @pl.when(pl.program_id(1) == 0)
        def _():
            o_ref[...] = jnp.zeros_like(o_ref)

        d = _dot(a_ref[...], b_ref[...], NN)
        rs = kb // per
        for j in range(per):
            o_ref[j] += d[rs * j:rs * (j + 1)]

    return _launch(
        body, [at, b], name=name, grid=(K // kb, T // tm),
        in_specs=[pl.BlockSpec((kb, tm), lambda i, t: (i, t)), pl.BlockSpec((tm, N), lambda i, t: (t, 0))],
        out_specs=[pl.BlockSpec((per, K // N_DEV, N), lambda i, t: (i, 0, 0))],
        out_shape=[jax.ShapeDtypeStruct((N_DEV, K // N_DEV, N), F32)],
        sem=("arbitrary", "arbitrary"), rider=rider)


def _wgrad_cols_call(at, b, name, rider=None):
    K, T = at.shape
    N = b.shape[1]
    tm = _tile(T, WGRAD_TOKENS)
    nb = min(N, 1024)
    per = nb // (N // N_DEV)

    def body(a_ref, b_ref, o_ref):
        @pl.when(pl.program_id(1) == 0)
        def _():
            o_ref[...] = jnp.zeros_like(o_ref)

        d = _dot(a_ref[...], b_ref[...], NN)
        cs = nb // per
        for j in range(per):
            o_ref[j] += d[:, cs * j:cs * (j + 1)]

    return _launch(
        body, [at, b], name=name, grid=(N // nb, T // tm),
        in_specs=[pl.BlockSpec((K, tm), lambda i, t: (0, t)), pl.BlockSpec((tm, nb), lambda i, t: (t, i))],
        out_specs=[pl.BlockSpec((per, K, N // N_DEV), lambda i, t: (i, 0, 0))],
        out_shape=[jax.ShapeDtypeStruct((N_DEV, K, N // N_DEV), F32)],
        sem=("arbitrary", "arbitrary"), rider=rider)


def _wgrad_in_call(du, dq, dk, dv, dgates, h, rider=None):
    T = h.shape[0]
    tm = _tile(T, WGRAD_TOKENS)
    rows = IN_WIDTH // N_DEV

    def body(du_ref, dq_ref, dk_ref, dv_ref, dgt_ref, h_ref, o_ref, acc, sem):
        t = pl.program_id(0)

        @pl.when(t == 0)
        def _():
            acc[...] = jnp.zeros_like(acc)

        hv = h_ref[...]
        for ref, lo, hi in ((du_ref, 0, C_Q), (dq_ref, C_Q, C_K), (dk_ref, C_K, C_V), (dv_ref, C_V, C_G),
                            (dgt_ref, C_G, IN_WIDTH)):
            acc[lo:hi, :] += _dot(ref[...], hv, TN)

        @pl.when(t == pl.num_programs(0) - 1)
        def _():
            copies = [pltpu.make_async_copy(acc.at[pl.ds(rows * j, rows), :], o_ref.at[j], sem.at[j])
                      for j in range(N_DEV)]
            for cp in copies:
                cp.start()
            for cp in copies:
                cp.wait()

    tok = lambda w: pl.BlockSpec((tm, w), lambda t: (t, 0))
    return _launch(
        body, [du, dq, dk, dv, dgates, h], name="wgrad_in", grid=(T // tm,),
        in_specs=[tok(POOL_WIDTH), tok(ATTN_WIDTH), tok(KV_WIDTH), tok(KV_WIDTH), tok(GATE_WIDTH), tok(D_MODEL)],
        out_specs=[pl.BlockSpec(memory_space=pl.ANY)],
        out_shape=[jax.ShapeDtypeStruct((N_DEV, rows, D_MODEL), F32)],
        scratch_shapes=[pltpu.VMEM((IN_WIDTH, D_MODEL), F32), pltpu.SemaphoreType.DMA((N_DEV,))],
        sem=("arbitrary",), rider=rider)


def _coords():
    return lax.axis_index("x"), lax.axis_index("y"), lax.axis_index("c")


def _allgather_call(shards):
    n = len(shards)

    def body(*refs):
        ins, outs = refs[:n], refs[n:2 * n]
        send_sems, recv_sems, local_sems = refs[2 * n:]
        x, y, c = _coords()
        me, sibling = (x, y, c), (x, y, 1 - c)
        chips = [(1 - x, y), (x, 1 - y), (1 - x, 1 - y)]

        def slot(p):
            return 4 * p[0] + 2 * p[1] + p[2]

        def copy(t, k, block, to, src=None):
            dst = outs[t].at[slot(block)]
            return pltpu.make_async_remote_copy(
                src_ref=dst if src is None else src, dst_ref=dst, send_sem=send_sems.at[t, k],
                recv_sem=recv_sems.at[t, k], device_id=to, device_id_type=MESH)

        mine = [pltpu.make_async_copy(ins[t], outs[t].at[slot(me)], local_sems.at[t]) for t in range(n)]
        for cp in mine:
            cp.start()
        first = []
        for t in range(n):
            first.append(copy(t, 0, me, sibling, src=ins[t]))
            first += [copy(t, 1 + j, me, (*chip, c), src=ins[t]) for j, chip in enumerate(chips)]
        for cp in first:
            cp.start()
        passed = []
        for t in range(n):
            for j, chip in enumerate(chips):
                copy(t, 1 + j, (*chip, c), me).wait_recv()
                fwd = copy(t, 4 + j, (*chip, c), sibling)
                fwd.start()
                passed.append(fwd)
        for t in range(n):
            copy(t, 0, sibling, me).wait_recv()
            for j, chip in enumerate(chips):
                copy(t, 4 + j, (*chip, 1 - c), me).wait_recv()
        for cp in first + passed:
            cp.wait_send()
        for cp in mine:
            cp.wait()

    hbm = pl.BlockSpec(memory_space=pl.ANY)
    return pl.pallas_call(
        body, name="allgather_weights",
        in_specs=[hbm] * n, out_specs=[hbm] * n,
        out_shape=[jax.ShapeDtypeStruct((N_DEV,) + s.shape, s.dtype) for s in shards],
        scratch_shapes=[pltpu.SemaphoreType.DMA((n, 7)), pltpu.SemaphoreType.DMA((n, 7)), pltpu.SemaphoreType.DMA((n,))],
    )(*shards)


def _slot(p):
    return 4 * p[0] + 2 * p[1] + p[2]


def _rows(ref, span):
    return ref if span is None else ref.at[pl.ds(span[0], span[1])]


ALL = "all"
LOCAL = "local"


def _rows(ref, span):
    return ref if span == ALL else ref.at[pl.ds(span[0], span[1])]


def _rider_ag(items):
    ins, out_shape, aliases, where = [], [], {}, []
    n_remote = n_local = 0
    for t, (shard, buf, snd, fwd) in enumerate(items):
        i_shard = i_buf = None
        if snd is not None:
            i_shard = len(ins)
            ins.append(shard)
        if buf is not None:
            i_buf = len(ins)
            ins.append(buf)
            aliases[i_buf] = t
            out_shape.append(jax.ShapeDtypeStruct(buf.shape, buf.dtype))
        else:
            assert fwd is None and snd is not None
            out_shape.append(jax.ShapeDtypeStruct((N_DEV,) + shard.shape, shard.dtype))
        where.append((i_shard, i_buf, n_remote, n_local))
        n_remote += (4 if snd not in (None, LOCAL) else 0) + (3 if fwd is not None else 0)
        n_local += 1 if snd is not None else 0

    def plan(rins, routs, send, recv, loc, r0, l0):
        x, y, c = _coords()
        peers = [(x, y, 1 - c), (1 - x, y, c), (x, 1 - y, c), (1 - x, 1 - y, c)]
        remote, local = [], []
        for t, (shard, buf, snd, fwd) in enumerate(items):
            i_shard, i_buf, k, l = where[t]
            k, l = r0 + k, l0 + l
            if snd is not None:
                span = ALL if snd == LOCAL else snd
                src, dst = _rows(rins[i_shard], span), _rows(routs[t].at[_slot((x, y, c))], span)
                local.append(pltpu.make_async_copy(src, dst, loc.at[l]))
                for peer in (peers if snd != LOCAL else []):
                    remote.append(pltpu.make_async_remote_copy(
                        src_ref=src, dst_ref=dst, send_sem=send.at[k], recv_sem=recv.at[k],
                        device_id=peer, device_id_type=MESH))
                    k += 1
            if fwd is not None:
                for px, py, pc in peers[1:]:
                    s = _slot((px, py, pc))
                    remote.append(pltpu.make_async_remote_copy(
                        src_ref=_rows(rins[i_buf].at[s], fwd), dst_ref=_rows(routs[t].at[s], fwd),
                        send_sem=send.at[k], recv_sem=recv.at[k], device_id=peers[0], device_id_type=MESH))
                    k += 1
        return remote, local

    return _Rider(ins, out_shape, n_remote, n_local, plan, aliases)


def _rider_rs_sibling(grads):
    n = len(grads)

    def plan(ins, outs, send, recv, loc, r0, l0):
        x, y, c = _coords()
        remote = []
        for t in range(n):
            for q in range(4):
                remote.append(pltpu.make_async_remote_copy(
                    src_ref=ins[t].at[q, 1 - c], dst_ref=outs[t].at[q], send_sem=send.at[r0 + 4 * t + q],
                    recv_sem=recv.at[r0 + 4 * t + q], device_id=(x, y, 1 - c), device_id_type=MESH))
        return remote, []

    return _Rider(grads, [jax.ShapeDtypeStruct((4,) + g.shape[2:], g.dtype) for g in grads], 4 * n, 0, plan)


def _rider_rs_chips(sums, rows=None, into=None):
    n = len(sums)
    rows = rows or [ALL] * n

    def plan(ins, outs, send, recv, loc, r0, l0):
        x, y, c = _coords()
        remote = []
        for t in range(n):
            for r, (px, py) in enumerate([(1 - x, y), (x, 1 - y), (1 - x, 1 - y)]):
                remote.append(pltpu.make_async_remote_copy(
                    src_ref=_rows(ins[t].at[2 * px + py], rows[t]), dst_ref=_rows(outs[t].at[r], rows[t]),
                    send_sem=send.at[r0 + 3 * t + r], recv_sem=recv.at[r0 + 3 * t + r],
                    device_id=(px, py, c), device_id_type=MESH))
        return remote, []

    out_shape = [jax.ShapeDtypeStruct((3,) + s.shape[1:], s.dtype) for s in sums]
    if into is None:
        return _Rider(sums, out_shape, 3 * n, 0, plan)
    return _Rider(list(sums) + list(into), out_shape, 3 * n, 0, plan, aliases={n + t: t for t in range(n)})


def _rider_gather_remote(parts):
    n = len(parts)

    def plan(ins, outs, send, recv, loc, r0, l0):
        x, y, c = _coords()
        me = _slot((x, y, c))
        remote = []
        for t in range(n):
            for k in range(1, N_DEV):
                peer = (x ^ ((k >> 2) & 1), y ^ ((k >> 1) & 1), c ^ (k & 1))
                remote.append(pltpu.make_async_remote_copy(
                    src_ref=ins[t], dst_ref=outs[t].at[me], send_sem=send.at[r0 + 7 * t + k - 1],
                    recv_sem=recv.at[r0 + 7 * t + k - 1], device_id=peer, device_id_type=MESH))
        return remote, []

    return _Rider(parts, [jax.ShapeDtypeStruct((N_DEV,) + p.shape, p.dtype) for p in parts], 7 * n, 0, plan)


def _chip_sum_call(idx, grads, recvd, out_dtypes, name):
    n = len(grads)

    def body(i_ref, *refs):
        for t in range(n):
            refs[2 * n + t][0] = (refs[t][0, 0] + refs[n + t][0]).astype(out_dtypes[t])

    def chip(k, s):
        return jnp.where(k >= s[0], k + 1, k)

    in_specs = [pl.BlockSpec((1, 1) + g.shape[2:], lambda k, s: (chip(k, s), s[1], 0, 0)) for g in grads]
    in_specs += [pl.BlockSpec((1,) + r.shape[1:], lambda k, s: (chip(k, s), 0, 0)) for r in recvd]
    return pl.pallas_call(
        body, name=name,
        grid_spec=pltpu.PrefetchScalarGridSpec(
            num_scalar_prefetch=1, grid=(3,), in_specs=in_specs,
            out_specs=[pl.BlockSpec((1,) + r.shape[1:], lambda k, s: (chip(k, s), 0, 0)) for r in recvd]),
        out_shape=[jax.ShapeDtypeStruct(r.shape, dt) for r, dt in zip(recvd, out_dtypes)],
        compiler_params=_params(("arbitrary",)),
    )(idx, *grads, *recvd)


def _final_sum_call(idx, grads, recvd1, recvd2):
    n = len(grads)
    nsteps = 2

    def body(i_ref, *refs):
        for t in range(n):
            g, r1, r2, o = refs[t], refs[n + t], refs[2 * n + t], refs[3 * n + t]
            s = g[0, 0] + r1[0]
            for r in range(3):
                s = s + r2[r].astype(F32)
            o[...] = s

    def rows(a):
        r = a.shape[-2]
        return r // nsteps if (r // nsteps) % 16 == 0 else r

    def step(a):
        return (lambda i: i) if rows(a) != a.shape[-2] else (lambda i: 0)

    in_specs = [pl.BlockSpec((1, 1, rows(g), g.shape[3]), lambda i, s, st=step(g): (s[0], s[1], st(i), 0)) for g in grads]
    in_specs += [pl.BlockSpec((1, rows(r), r.shape[2]), lambda i, s, st=step(r): (s[0], st(i), 0)) for r in recvd1]
    in_specs += [pl.BlockSpec((3, rows(r), r.shape[2]), lambda i, s, st=step(r): (0, st(i), 0)) for r in recvd2]
    return pl.pallas_call(
        body, name="rs_final_sum",
        grid_spec=pltpu.PrefetchScalarGridSpec(
            num_scalar_prefetch=1, grid=(nsteps,), in_specs=in_specs,
            out_specs=[pl.BlockSpec((rows(r), r.shape[2]), lambda i, s, st=step(r): (st(i), 0)) for r in recvd2]),
        out_shape=[jax.ShapeDtypeStruct(r.shape[1:], F32) for r in recvd2],
        compiler_params=_params(("arbitrary",)),
    )(idx, *grads, *recvd1, *recvd2)


def _sum8_call(parts):
    def body(p_ref, o_ref):
        s = p_ref[0]
        for j in range(1, N_DEV):
            s = s + p_ref[j]
        o_ref[...] = s

    return pl.pallas_call(body, name="sum_small_partials",
                          out_shape=jax.ShapeDtypeStruct(parts.shape[1:], parts.dtype))(parts)


def _adamw(w, g, m, v):
    m = ADAM_B1 * m + (1.0 - ADAM_B1) * g
    v = ADAM_B2 * v + (1.0 - ADAM_B2) * (g * g)
    m_hat = m / (1.0 - ADAM_B1 ** ADAM_STEP)
    v_hat = v / (1.0 - ADAM_B2 ** ADAM_STEP)
    delta = -ADAM_LR * (m_hat / (jnp.sqrt(v_hat) + ADAM_EPS) + ADAM_WD * w)
    return delta, m, v


def _adamw_call(ws, gs, ms, vs, nsteps, name):
    n = len(ws)

    def body(*refs):
        for t in range(n):
            w, g, m, v = (refs[k * n + t][...] for k in range(4))
            d, m2, v2 = _adamw(w, g, m, v)
            refs[4 * n + t][...] = d
            refs[5 * n + t][...] = m2
            refs[6 * n + t][...] = v2

    def spec(a):
        assert a.shape[0] % nsteps == 0 and (nsteps == 1 or (a.shape[0] // nsteps) % 8 == 0), a.shape
        return pl.BlockSpec((a.shape[0] // nsteps, a.shape[1]), lambda i: (i, 0))

    specs = [spec(a) for a in ws]
    outs = pl.pallas_call(
        body, name=name, grid=(nsteps,),
        in_specs=specs * 4, out_specs=specs * 3,
        out_shape=[jax.ShapeDtypeStruct(a.shape, F32) for a in ws] * 3,
        compiler_params=_params(("arbitrary",)),
    )(*ws, *gs, *ms, *vs)
    return outs[:n], outs[n:2 * n], outs[2 * n:]


def _adamw_rs_call(idx, after, gws, r1s, r2s, ws, ms, vs, nsteps, name):
    n = len(ws)

    def body(i_ref, after_ref, *refs):
        for t in range(n):
            gw, r1, r2, w, m, v = (refs[k * n + t] for k in range(6))
            g = gw[0, 0] + r1[0]
            for r in range(3):
                g = g + r2[r].astype(F32)
            d, m2, v2 = _adamw(w[...], g, m[...], v[...])
            refs[6 * n + t][...] = g
            refs[7 * n + t][...] = d
            refs[8 * n + t][...] = m2
            refs[9 * n + t][...] = v2

    def rb(a):
        r = a.shape[0] // nsteps
        assert a.shape[0] % nsteps == 0 and r % 16 == 0, a.shape
        return r

    in_specs = [pl.BlockSpec((1, 1, rb(w), w.shape[1]), lambda i, s: (s[0], s[1], i, 0)) for w in ws]
    in_specs += [pl.BlockSpec((1, rb(w), w.shape[1]), lambda i, s: (s[0], i, 0)) for w in ws]
    in_specs += [pl.BlockSpec((3, rb(w), w.shape[1]), lambda i, s: (0, i, 0)) for w in ws]
    plain = [pl.BlockSpec((rb(w), w.shape[1]), lambda i, s: (i, 0)) for w in ws]
    outs = pl.pallas_call(
        body, name=name,
        grid_spec=pltpu.PrefetchScalarGridSpec(
            num_scalar_prefetch=1, grid=(nsteps,),
            in_specs=[pl.BlockSpec(memory_space=pl.ANY)] + in_specs + plain * 3, out_specs=plain * 4),
        out_shape=[jax.ShapeDtypeStruct(w.shape, F32) for w in ws] * 4,
        compiler_params=_params(("arbitrary",)),
    )(idx, after, *gws, *r1s, *r2s, *ws, *ms, *vs)
    return outs[:n], outs[n:2 * n], outs[2 * n:3 * n], outs[3 * n:]


def _rows128(a, pad_rows):
    flat = a.reshape(-1).astype(F32)
    flat = jnp.pad(flat, (0, pad_rows * LANES - flat.shape[0]))
    return flat.reshape(pad_rows, LANES)


_SMALL_A = (("w_pool", 512), ("pool_scale", 8), ("attn_sinks", 8), ("g_mix_post", 8), ("g_mlp_pre", 8),
            ("g_mlp_post", 8), ("loss", 8), ("b_in_gates", 16))
_SMALL_A_ROWS = 640
_SMALL_B = (("g_mix_pre", 8), ("b_in_head", 16))


def _pack(parts, layout, total_rows):
    rows = [_rows128(parts[k], r) for k, r in layout]
    pad = total_rows - sum(r for _, r in layout)
    if pad:
        rows.append(jnp.zeros((pad, LANES), F32))
    return jnp.concatenate(rows, axis=0)


def _unpack(buf, layout, sizes):
    out, off = {}, 0
    for k, r in layout:
        out[k] = buf[off:off + r].reshape(-1)[:sizes[k]]
        off += r
    return out


def kernel(x, g_mix_pre, w_in, b_in, w_pool, pool_scale, attn_sinks, w_branch_pool, w_branch_attn, w_out, g_mix_post, g_mlp_pre, w_up, w_down, g_mlp_post, loss_target, m_g_mix_pre, m_w_in, m_b_in, m_w_pool, m_pool_scale, m_attn_sinks, m_w_branch_pool, m_w_branch_attn, m_w_out, m_g_mix_post, m_g_mlp_pre, m_w_up, m_w_down, m_g_mlp_post, v_g_mix_pre, v_w_in, v_b_in, v_w_pool, v_pool_scale, v_attn_sinks, v_w_branch_pool, v_w_branch_attn, v_w_out, v_g_mix_post, v_g_mlp_pre, v_w_up, v_w_down, v_g_mlp_post):
    B, S, _ = x.shape
    T = B * S
    xt = x.reshape(T, D_MODEL)
    tgt = loss_target.reshape(T, D_MODEL)
    cx, cy, cc = _coords()

    cidx = jnp.stack([2 * cx + cy, cc]).astype(jnp.int32)
    by_chip = lambda gr: gr.reshape((4, 2) + gr.shape[1:])
    bf = lambda w: w[0].astype(MXU_DTYPE)

    (win_s,) = _allgather_call([w_in[0].T.astype(MXU_DTYPE)])
    win_t = win_s.reshape(IN_WIDTH, D_MODEL)
    wpool_b = bf(w_pool)
    rc, rsa, rsb = _rot_tables(S)

    up_a, up_b = (0, D_MODEL // 2), (D_MODEL // 2, D_MODEL // 2)
    dn_a, dn_b = (0, D_FF // 16), (D_FF // 16, D_FF // 16)
    wup_l, wdown_l = bf(w_up), bf(w_down)
    (h, u, q, k4, v4, g), (wbp_1, wba_1, wout_1, wup_1) = _inproj_call(
        xt, g_mix_pre, win_t, b_in, rc, rsa, rsb, S,
        rider=_rider_ag([(bf(w_branch_pool), None, ALL, None), (bf(w_branch_attn), None, ALL, None),
                         (bf(w_out), None, ALL, None), (wup_l, None, up_a, None)]))
    yp = _pool_call(u, wpool_b, pool_scale, S)
    (ya,), (wbp_s, wba_s, wout_s, wup_2, wdown_1) = _attn_call(
        attn_sinks, q, k4, v4, S,
        rider=_rider_ag([(None, wbp_1, None, ALL), (None, wba_1, None, ALL), (None, wout_1, None, ALL),
                         (wup_l, wup_1, up_b, up_a), (wdown_l, None, dn_a, None)]))
    wout_f = wout_s.reshape(D_MODEL, D_MODEL)
    (mix, x1, h2, h2_t), (wup_s, wdown_2) = _mix_fwd_call(
        yp, ya, g, xt, wbp_s, wba_s, wout_f, g_mix_post, g_mlp_pre,
        rider=_rider_ag([(None, wup_2, None, up_b), (wdown_l, wdown_1, dn_b, dn_a)]))
    (wdown_s,) = _comm_call(_rider_ag([(None, wdown_2, None, dn_b)]), "allgather_finish")

    act_t, da, dff, dx1, dg3, dg4, lossvec = _mlp_call(x1, h2, tgt, wup_s, wdown_s, g_mlp_pre, g_mlp_post)
    gw_down = by_chip(_wgrad_rows_call(act_t, dff, "wgrad_down")[0])
    (gw_up,), (r1_down,) = _wgrad_cols_call(h2_t, da, "wgrad_up", rider=_rider_rs_sibling([gw_down]))
    gw_up = by_chip(gw_up)
    (s_down,) = _chip_sum_call(cidx, [gw_down], [r1_down], [MXU_DTYPE], "rs_chip_sum_down")
    (c_down,), tok = _copies_start([_rider_rs_chips([s_down])], "rs_chips_start_down")
    (dyp, do, dgates, dg2, dbg, gw_out, gw_bp, gw_ba), (r1_up,) = _mix_bwd_call(
        dx1, mix, yp, ya, g, wbp_s, wba_s, wout_f, g_mix_post, rider=_after(tok, _rider_rs_sibling([gw_up])))
    gw_out = by_chip(gw_out.reshape(N_DEV, D_MODEL // N_DEV, D_MODEL))
    gw_bp, gw_ba = by_chip(gw_bp), by_chip(gw_ba)
    (s_up,) = _chip_sum_call(cidx, [gw_up], [r1_up], [MXU_DTYPE], "rs_chip_sum_up")
    (c_up,), tok = _copies_start([_rider_rs_chips([s_up])], "rs_chips_start_up")
    (dq, dk, dv, dsink), (r1_out, r1_bp, r1_ba) = _attn_bwd_call(
        attn_sinks, q, k4, v4, do, rc, rsa, rsb, S, rider=_after(tok, _rider_rs_sibling([gw_out, gw_bp, gw_ba])))
    s_obb = _chip_sum_call(cidx, [gw_out, gw_bp, gw_ba], [r1_out, r1_bp, r1_ba], [MXU_DTYPE] * 3, "rs_chip_sum_branch")
    (c_obb,), tok = _copies_start([_rider_rs_chips(s_obb)], "rs_chips_start_branch")
    (du, dwp, dps), _ = _pool_bwd_call(u, dyp, wpool_b, pool_scale, S, rider=_after(tok))
    (gw_in,) = _wgrad_in_call(du, dq, dk, dv, dgates, h)
    gw_in = by_chip(gw_in)

    small_a = {"w_pool": dwp, "pool_scale": dps,
               "attn_sinks": jnp.sum(dsink.reshape(B, 8, LANES)[:, 0, :N_Q_HEADS], axis=0), "g_mix_post": dg2,
               "g_mlp_pre": dg3, "g_mlp_post": dg4, "loss": lossvec, "b_in_gates": dbg}
    gw_sa = by_chip(_pack(small_a, _SMALL_A, _SMALL_A_ROWS).reshape(N_DEV, _SMALL_A_ROWS // N_DEV, LANES))
    r1_in, r1_sa = _comm_call(_rider_rs_sibling([gw_in, gw_sa]), "rs_sibling_in")
    s_in, s_sa = _chip_sum_call(cidx, [gw_in, gw_sa], [r1_in, r1_sa], [MXU_DTYPE, F32], "rs_chip_sum_in")
    (c_in,), tok = _copies_start([_rider_rs_chips([s_in, s_sa])], "rs_chips_start_in")
    (gx, dg1, dba_in), _ = _inproj_bwd_call(du, dq, dk, dv, dgates, dx1, xt, win_t, g_mix_pre, rider=_after(tok))
    r2_down, r2_up, r2_out, r2_bp, r2_ba, r2_in, r2_sa = _copies_wait([c_down, c_up, c_obb, c_in], dg1, "rs_chips_wait")

    (g_sa,) = _final_sum_call(cidx, [gw_sa], [r1_sa], [r2_sa])
    part_b = _pack({"g_mix_pre": dg1, "b_in_head": dba_in}, _SMALL_B, sum(r for _, r in _SMALL_B))
    (c_small,), tok = _copies_start([_rider_gather_remote([g_sa, part_b])], "allgather_small_start")

    in_t = _adamw_rs_call(cidx, tok, [gw_in], [r1_in], [r2_in], [w_in[0].T], [m_w_in[0].T], [v_w_in[0].T], 2,
                          "adamw_w_in")
    rest = _adamw_rs_call(
        cidx, tok, [gw_bp, gw_ba, gw_out, gw_up, gw_down], [r1_bp, r1_ba, r1_out, r1_up, r1_down],
        [r2_bp, r2_ba, r2_out, r2_up, r2_down], [w_branch_pool[0], w_branch_attn[0], w_out[0], w_up[0], w_down[0]],
        [m_w_branch_pool[0], m_w_branch_attn[0], m_w_out[0], m_w_up[0], m_w_down[0]],
        [v_w_branch_pool[0], v_w_branch_attn[0], v_w_out[0], v_w_up[0], v_w_down[0]], N_DEV, "adamw_shards")
    big_g, big_d, big_m2, big_v2 = ([a[0].T] + list(b) for a, b in zip(in_t, rest))

    sa_all, sb_all = _copies_wait([c_small], rest[0][0], "allgather_small_wait")
    me = (_slot((cx, cy, cc)), 0, 0)
    sa_all = lax.dynamic_update_slice(sa_all, g_sa[None], me)
    sb_sum = _sum8_call(lax.dynamic_update_slice(sb_all, part_b[None], me))

    names = ["g_mix_pre", "b_in", "w_pool", "pool_scale", "attn_sinks", "g_mix_post", "g_mlp_pre", "g_mlp_post"]
    sm_w = dict(g_mix_pre=g_mix_pre, b_in=b_in, w_pool=w_pool, pool_scale=pool_scale, attn_sinks=attn_sinks,
                g_mix_post=g_mix_post, g_mlp_pre=g_mlp_pre, g_mlp_post=g_mlp_post)
    sm_m = dict(g_mix_pre=m_g_mix_pre, b_in=m_b_in, w_pool=m_w_pool, pool_scale=m_pool_scale, attn_sinks=m_attn_sinks,
                g_mix_post=m_g_mix_post, g_mlp_pre=m_g_mlp_pre, g_mlp_post=m_g_mlp_post)
    sm_v = dict(g_mix_pre=v_g_mix_pre, b_in=v_b_in, w_pool=v_w_pool, pool_scale=v_pool_scale, attn_sinks=v_attn_sinks,
                g_mix_post=v_g_mix_post, g_mlp_pre=v_g_mlp_pre, g_mlp_post=v_g_mlp_post)
    sizes = {k: sm_w[k].size for k in names}
    sizes.update(loss=D_MODEL, b_in_gates=GATE_WIDTH, b_in_head=C_G)
    sm_g = _unpack(sa_all.reshape(_SMALL_A_ROWS, LANES), _SMALL_A, sizes)
    sm_g.update(_unpack(sb_sum, _SMALL_B, sizes))
    sm_g["b_in"] = jnp.concatenate([sm_g["b_in_head"], sm_g["b_in_gates"]])
    loss = (0.5 / D_MODEL) * jnp.sum(sm_g["loss"])
    two_d = lambda a: a.reshape(-1, a.shape[-1])
    sd_, sm2_, sv2_ = _adamw_call([two_d(sm_w[k]) for k in names], [two_d(sm_g[k].reshape(sm_w[k].shape)) for k in names],
                                  [two_d(sm_m[k]) for k in names], [two_d(sm_v[k]) for k in names], 1, "adamw_small")
    like = lambda vals: {k: a.reshape(sm_w[k].shape) for k, a in zip(names, vals)}
    sm_d, sm_m2, sm_v2 = like(sd_), like(sm2_), like(sv2_)
    sm_gr = {k: sm_g[k].reshape(sm_w[k].shape) for k in names}

    order = ["g_mix_pre", "w_in", "b_in", "w_pool", "pool_scale", "attn_sinks", "w_branch_pool", "w_branch_attn",
             "w_out", "g_mix_post", "g_mlp_pre", "w_up", "w_down", "g_mlp_post"]
    big_names = ["w_in", "w_branch_pool", "w_branch_attn", "w_out", "w_up", "w_down"]
    lead = lambda a: a[None]
    tables = []
    for small_t, big_t in ((sm_gr, big_g), (sm_d, big_d), (sm_m2, big_m2), (sm_v2, big_v2)):
        bt = dict(zip(big_names, big_t))
        tables.append([lead(bt[k]) if k in bt else small_t[k] for k in order])
    return (loss, gx.reshape(B, S, D_MODEL), *tables[0], *tables[1], *tables[2], *tables[3])
```

```python
import functools

import jax
import jax.numpy as jnp
from jax import lax
from jax.experimental import pallas as pl
from jax.experimental.pallas import tpu as pltpu

F32 = jnp.float32
MXU_DTYPE = jnp.bfloat16
MESH = pl.DeviceIdType.MESH

D_MODEL = 1024
POOL_WINDOWS = (2, 4, 8, 16)
POOL_WIDTH = 512
POOL_GC = 128
HEAD_DIM = 64
N_Q_HEADS = 8
N_KV_HEADS = 2
GROUP = 4
ATTN_WIDTH = 512
KV_WIDTH = 128
BLOCK = 128
GATE_WIDTH = 2048
IN_WIDTH = 3328
D_FF = 4096
EPS = 1e-6
NEG_INF = -1e30
ROPE_THETA = 500000.0
ROT_DIM = 16
SCALE = HEAD_DIM ** -0.5
C_Q, C_K, C_V, C_G = 512, 1024, 1152, 1280

ADAM_LR = 0.001
ADAM_B1 = 0.9
ADAM_B2 = 0.999
ADAM_EPS = 1e-08
ADAM_WD = 0.01
ADAM_STEP = 10

N_DEV = 8
LANES = 128
VMEM_LIMIT = 56 * 1024 * 1024

NN = (((1,), (0,)), ((), ()))
NT = (((1,), (1,)), ((), ()))
TN = (((0,), (0,)), ((), ()))


def _dot(a, b, dims):
    return lax.dot_general(a, b, dims, preferred_element_type=F32)


def _params(sem=None):
    return pltpu.CompilerParams(dimension_semantics=sem, vmem_limit_bytes=VMEM_LIMIT)


def _tile(n, pref):
    t = min(n, pref)
    assert n % t == 0, (n, t)
    return t


class _Rider:
    def __init__(self, ins, out_shape, n_remote, n_local, plan, aliases=None):
        self.ins, self.out_shape, self.n_remote, self.n_local = list(ins), list(out_shape), n_remote, n_local
        self.plan, self.aliases = plan, dict(aliases or {})


def _after(token, rider=None):
    r = rider or _Rider([], [], 0, 0, lambda ins, outs, send, recv, loc, r0, l0: ([], []))
    return _Rider(r.ins + [token], r.out_shape, r.n_remote, r.n_local, r.plan, r.aliases)


def _launch(body, args, *, name, grid, in_specs, out_specs, out_shape, scratch_shapes=(), sem=None, rider=None):
    if rider is None:
        return pl.pallas_call(body, name=name, grid=grid, in_specs=in_specs, out_specs=out_specs, out_shape=out_shape,
                              scratch_shapes=list(scratch_shapes), compiler_params=_params(sem))(*args)
    n_in, n_out, n_scr = len(args), len(out_shape), len(scratch_shapes)
    r_in, r_out = len(rider.ins), len(rider.out_shape)
    copies = rider.n_remote + rider.n_local > 0

    def wrapped(*refs):
        ins, rins = refs[:n_in], refs[n_in:n_in + r_in]
        o0 = n_in + r_in
        outs, routs = refs[o0:o0 + n_out], refs[o0 + n_out:o0 + n_out + r_out]
        s0 = o0 + n_out + r_out
        scr = refs[s0:s0 + n_scr]
        if not copies:
            return body(*ins, *outs, *scr)
        send, recv, loc = refs[s0 + n_scr:]
        first, last = None, None
        for d in range(len(grid)):
            f, l = pl.program_id(d) == 0, pl.program_id(d) == pl.num_programs(d) - 1
            first = f if first is None else first & f
            last = l if last is None else last & l

        def start():
            remote, local = rider.plan(rins, routs, send, recv, loc, 0, 0)
            for cp in local + remote:
                cp.start()

        def finish():
            remote, local = rider.plan(rins, routs, send, recv, loc, 0, 0)
            for cp in remote + local:
                cp.wait()

        if first is None:
            start()
            body(*ins, *outs, *scr)
            finish()
        else:
            pl.when(first)(start)
            body(*ins, *outs, *scr)
            pl.when(last)(finish)

    hbm = pl.BlockSpec(memory_space=pl.ANY)
    dma = pltpu.SemaphoreType.DMA
    res = pl.pallas_call(
        wrapped, name=name, grid=grid, in_specs=list(in_specs) + [hbm] * r_in,
        out_specs=list(out_specs) + [hbm] * r_out, out_shape=list(out_shape) + rider.out_shape,
        scratch_shapes=list(scratch_shapes) + (
            [dma((max(rider.n_remote, 1),)), dma((max(rider.n_remote, 1),)), dma((max(rider.n_local, 1),))] if copies else []),
        input_output_aliases={n_in + i: n_out + o for i, o in rider.aliases.items()},
        compiler_params=_params(sem),
    )(*args, *rider.ins)
    return list(res[:n_out]), list(res[n_out:])


def _comm_call(rider, name):
    return _launch(lambda: None, [], name=name, grid=(), in_specs=[], out_specs=[], out_shape=[], rider=rider)[1]


_HBM = pl.BlockSpec(memory_space=pltpu.HBM)
_SEM = pl.BlockSpec(memory_space=pltpu.SEMAPHORE)
_EFFECT = pltpu.SideEffectType.DATAFLOW_SIDE_EFFECTING


def _copies_start(riders, name):
    assert all(r.n_local == 0 and not r.aliases for r in riders)
    sizes = [(len(r.ins), len(r.out_shape)) for r in riders]
    bufs = []
    for r in riders:
        bufs += [pltpu.with_memory_space_constraint(a, pltpu.HBM) for a in r.ins]
        bufs += [pltpu.with_memory_space_constraint(lax.empty(s.shape, s.dtype), pltpu.HBM) for s in r.out_shape]
    nb, ng = len(bufs), len(riders)

    def body(*refs):
        sems, token, at = refs[2 * nb:2 * nb + 2 * ng], refs[-1], 0
        for g, (r, (ni, no)) in enumerate(zip(riders, sizes)):
            remote, _ = r.plan(refs[at:at + ni], refs[at + ni:at + ni + no], sems[2 * g], sems[2 * g + 1], None, 0, 0)
            for cp in remote:
                cp.start()
            at += ni + no
        token[...] = jnp.zeros_like(token)

    res = pl.pallas_call(
        body, name=name, in_specs=[_HBM] * nb,
        out_specs=[_HBM] * nb + [_SEM] * (2 * ng) + [pl.BlockSpec(memory_space=pltpu.VMEM)],
        out_shape=[pltpu.HBM(a.shape, a.dtype) for a in bufs]
        + [pltpu.SemaphoreType.DMA((r.n_remote,)) for r in riders for _ in range(2)]
        + [jax.ShapeDtypeStruct((8, LANES), F32)],
        input_output_aliases={i: i for i in range(nb)},
        compiler_params=pltpu.CompilerParams(has_side_effects=_EFFECT),
    )(*bufs)
    handles, at = [], 0
    for g, (r, (ni, no)) in enumerate(zip(riders, sizes)):
        handles.append((r, list(res[at:at + ni + no]), res[nb + 2 * g], res[nb + 2 * g + 1]))
        at += ni + no
    return handles, res[-1]


def _copies_wait(handles, after, name):
    bufs = [b for _, bs, _, _ in handles for b in bs]
    sems = [s for _, _, send, recv in handles for s in (send, recv)]
    nb, ng = len(bufs), len(handles)

    def body(*refs):
        at = 0
        for g, (rider, bs, _, _) in enumerate(handles):
            ni = len(rider.ins)
            remote, _ = rider.plan(refs[at:at + ni], refs[at + ni:at + len(bs)], refs[nb + 2 * g], refs[nb + 2 * g + 1],
                                   None, 0, 0)
            for cp in remote:
                cp.wait_send()
                cp.wait_recv()
            at += len(bs)

    res = pl.pallas_call(
        body, name=name, in_specs=[_HBM] * nb + [_SEM] * (2 * ng) + [pl.BlockSpec(memory_space=pl.ANY)],
        out_specs=[_HBM] * nb, out_shape=[pltpu.HBM(a.shape, a.dtype) for a in bufs],
        input_output_aliases={i: i for i in range(nb)},
        compiler_params=pltpu.CompilerParams(has_side_effects=_EFFECT),
    )(*bufs, *sems, after)
    lands, at = [], 0
    for rider, bs, _, _ in handles:
        lands += list(res[at + len(rider.ins):at + len(bs)])
        at += len(bs)
    return lands


def _rms_r(x):
    return lax.rsqrt(jnp.mean(x * x, axis=-1, keepdims=True) + EPS)


def _rms_bwd(dn, x, r, g):
    xh = x * r
    dxh = dn * g
    dx = r * (dxh - xh * jnp.mean(dxh * xh, axis=-1, keepdims=True))
    return dx, dn * xh


def _rot(t, c, sa, sb):
    outs = []
    for j in range(t.shape[1] // LANES):
        tj = t[:, LANES * j:LANES * (j + 1)]
        outs.append(tj * c + pltpu.roll(tj, LANES - 8, 1) * sa + pltpu.roll(tj, 8, 1) * sb)
    return outs[0] if len(outs) == 1 else jnp.concatenate(outs, axis=1)


def _rot_tables(S):
    pos = jnp.arange(S, dtype=F32)
    inv_freq = ROPE_THETA ** (-jnp.arange(0, ROT_DIM, 2, dtype=F32) / ROT_DIM)
    ang = pos[:, None] * inv_freq[None, :]
    cos, sin = jnp.cos(ang), jnp.sin(ang)
    one = jnp.ones((S, HEAD_DIM - ROT_DIM), F32)
    zero = jnp.zeros((S, HEAD_DIM - ROT_DIM), F32)
    z8 = jnp.zeros((S, 8), F32)
    c = jnp.concatenate([cos, cos, one], axis=1)
    sa = jnp.concatenate([-sin, z8, zero], axis=1)
    sb = jnp.concatenate([z8, sin, zero], axis=1)
    rep = LANES // HEAD_DIM
    return jnp.tile(c, (1, rep)), jnp.tile(sa, (1, rep)), jnp.tile(sb, (1, rep))


def _lane_tile4(k):
    lane = lax.broadcasted_iota(jnp.int32, k.shape, 1)
    rk = pltpu.roll(k, HEAD_DIM, 1)
    x0 = jnp.where(lane < HEAD_DIM, k, rk)
    x1 = jnp.where(lane < HEAD_DIM, rk, k)
    return jnp.concatenate([x0, x0, x1, x1], axis=1)


def _fold_heads(acc):
    zs = []
    for hk in range(N_KV_HEADS):
        a = acc[:, 256 * hk:256 * hk + LANES] + acc[:, 256 * hk + LANES:256 * (hk + 1)]
        zs.append(a + pltpu.roll(a, HEAD_DIM, 1))
    lane = lax.broadcasted_iota(jnp.int32, zs[0].shape, 1)
    return jnp.where(lane < HEAD_DIM, zs[0], zs[1])


def _inproj_call(x, g1, win_t, b_in, rc, rsa, rsb, S, rider=None):
    T = x.shape[0]
    tm = _tile(S, 512)
    nst = S // tm

    def body(x_ref, g1_ref, w_ref, b_ref, c_ref, sa_ref, sb_ref,
             h_ref, u_ref, q_ref, k4_ref, v4_ref, g_ref):
        xv = x_ref[...]
        hb = ((xv * _rms_r(xv)) * g1_ref[...]).astype(MXU_DTYPE)
        h_ref[...] = hb

        def proj(lo, hi):
            return _dot(hb, w_ref[lo:hi, :], NT) + b_ref[:, lo:hi]

        c, sa, sb = c_ref[...], sa_ref[...], sb_ref[...]
        u_ref[...] = proj(0, C_Q)
        q_ref[...] = (_rot(proj(C_Q, C_K), c, sa, sb) * SCALE).astype(MXU_DTYPE)
        kv = proj(C_K, C_G)
        k4_ref[...] = _lane_tile4(_rot(kv[:, :KV_WIDTH], c, sa, sb)).astype(MXU_DTYPE)
        v4_ref[...] = _lane_tile4(kv[:, KV_WIDTH:]).astype(MXU_DTYPE)
        g_ref[...] = jax.nn.sigmoid(proj(C_G, IN_WIDTH)).astype(MXU_DTYPE)

    tok = lambda w: pl.BlockSpec((tm, w), lambda i: (i, 0))
    full = lambda a: pl.BlockSpec(a.shape, lambda i: (0,) * a.ndim)
    tab = pl.BlockSpec((tm, LANES), lambda i: (i % nst, 0))
    return _launch(
        body, [x, g1, win_t, b_in, rc, rsa, rsb], name="inproj_fwd", grid=(T // tm,),
        in_specs=[tok(D_MODEL), full(g1), full(win_t), full(b_in), tab, tab, tab],
        out_specs=[tok(D_MODEL), tok(POOL_WIDTH), tok(ATTN_WIDTH), tok(512), tok(512), tok(GATE_WIDTH)],
        out_shape=[jax.ShapeDtypeStruct((T, D_MODEL), MXU_DTYPE), jax.ShapeDtypeStruct((T, POOL_WIDTH), F32),
                   jax.ShapeDtypeStruct((T, ATTN_WIDTH), MXU_DTYPE), jax.ShapeDtypeStruct((T, 512), MXU_DTYPE),
                   jax.ShapeDtypeStruct((T, 512), MXU_DTYPE), jax.ShapeDtypeStruct((T, GATE_WIDTH), MXU_DTYPE)],
        sem=("arbitrary",), rider=rider)


def _shift_rows(a, k, rows):
    n = a.shape[0]
    if k > 0:
        return jnp.where(rows >= k, pltpu.roll(a, k, 0), 0.0)
    return jnp.where(rows < n + k, pltpu.roll(a, n + k, 0), 0.0)


def _win_sum(a, w, rows, sign):
    s, k = a, 1
    while k < w:
        s = s + _shift_rows(s, sign * k, rows)
        k *= 2
    return s


def _pool_diff(ug, w, rows):
    inv = 1.0 / jnp.minimum(rows + 1, w).astype(F32)
    return _win_sum(ug, w, rows, 1) * inv - ug, inv


def _pool_call(u, w_pool, pool_scale, S):
    T = u.shape[0]

    def body(u_ref, w_ref, ps_ref, y_ref):
        rows = lax.broadcasted_iota(jnp.int32, (S, POOL_GC), 0)
        for gi, w in enumerate(POOL_WINDOWS):
            sl = slice(POOL_GC * gi, POOL_GC * (gi + 1))
            diff, _ = _pool_diff(u_ref[:, sl], w, rows)
            mixed = _dot(diff.astype(MXU_DTYPE), w_ref[gi], NN)
            y_ref[:, sl] = (mixed * ps_ref[:, sl]).astype(MXU_DTYPE)

    seq = pl.BlockSpec((S, POOL_WIDTH), lambda b: (b, 0))
    return pl.pallas_call(
        body, name="pool_fwd", grid=(T // S,),
        in_specs=[seq, pl.BlockSpec(w_pool.shape, lambda b: (0, 0, 0)), pl.BlockSpec(pool_scale.shape, lambda b: (0, 0))],
        out_specs=seq, out_shape=jax.ShapeDtypeStruct((T, POOL_WIDTH), MXU_DTYPE),
        compiler_params=_params(("arbitrary",)),
    )(u, w_pool, pool_scale)


def _pool_bwd_call(u, dyp, w_pool, pool_scale, S, rider=None):
    T = u.shape[0]

    def body(u_ref, dy_ref, w_ref, ps_ref, du_ref, dw_ref, dps_ref):
        @pl.when(pl.program_id(0) == 0)
        def _():
            dw_ref[...] = jnp.zeros_like(dw_ref)
            dps_ref[...] = jnp.zeros_like(dps_ref)

        rows = lax.broadcasted_iota(jnp.int32, (S, POOL_GC), 0)
        for gi, w in enumerate(POOL_WINDOWS):
            sl = slice(POOL_GC * gi, POOL_GC * (gi + 1))
            diff, inv = _pool_diff(u_ref[:, sl], w, rows)
            diffb = diff.astype(MXU_DTYPE)
            wg = w_ref[gi]
            mixed = _dot(diffb, wg, NN)
            dy = dy_ref[:, sl]
            dps_ref[:, sl] += jnp.sum(dy * mixed, axis=0, keepdims=True)
            dmb = (dy * ps_ref[:, sl]).astype(MXU_DTYPE)
            dw_ref[gi] += _dot(diffb, dmb, TN)
            ddiff = _dot(dmb, wg, NT)
            du_ref[:, sl] = (_win_sum(ddiff * inv, w, rows, -1) - ddiff).astype(MXU_DTYPE)

    seq = pl.BlockSpec((S, POOL_WIDTH), lambda b: (b, 0))
    return _launch(
        body, [u, dyp, w_pool, pool_scale], name="pool_bwd", grid=(T // S,),
        in_specs=[seq, seq, pl.BlockSpec(w_pool.shape, lambda b: (0, 0, 0)), pl.BlockSpec(pool_scale.shape, lambda b: (0, 0))],
        out_specs=[seq, pl.BlockSpec(w_pool.shape, lambda b: (0, 0, 0)), pl.BlockSpec(pool_scale.shape, lambda b: (0, 0))],
        out_shape=[jax.ShapeDtypeStruct((T, POOL_WIDTH), MXU_DTYPE), jax.ShapeDtypeStruct(w_pool.shape, F32),
                   jax.ShapeDtypeStruct(pool_scale.shape, F32)],
        sem=("arbitrary",), rider=rider)


def _attn_consts():
    lane_g = lax.broadcasted_iota(jnp.int32, (BLOCK, 256), 1) >> 6
    rgrp = lax.broadcasted_iota(jnp.int32, (GROUP * BLOCK, 1), 0) >> 7
    rel = lax.broadcasted_iota(jnp.int32, (BLOCK, 256), 0) - lax.broadcasted_iota(jnp.int32, (BLOCK, 256), 1)

    def bias(off):
        ok = (rel + off >= 0) & (rel + off < BLOCK)
        return jnp.concatenate([jnp.where(ok, 0.0, NEG_INF)] * GROUP, axis=0)

    return lane_g, rgrp, bias(0), bias(BLOCK)


def _sink_rows(sink_ref, hk, rgrp):
    sv = jnp.zeros(rgrp.shape, F32)
    for g in range(GROUP):
        sv = jnp.where(rgrp == g, sink_ref[0, GROUP * hk + g], sv)
    return sv


def _stack_heads(xb, lane_g):
    return jnp.concatenate([jnp.where(lane_g == g, xb, jnp.zeros_like(xb)) for g in range(GROUP)], axis=0)


def _unstack_heads(xs, lane_g):
    out = jnp.where(lane_g == 0, xs[0:BLOCK], 0.0)
    for g in range(1, GROUP):
        out = out + jnp.where(lane_g == g, xs[BLOCK * g:BLOCK * (g + 1)], 0.0)
    return out


def _attn_probs(qs, kb, bias, sv):
    s = _dot(qs, kb, NT) + bias
    m = jnp.maximum(jnp.max(s, axis=1, keepdims=True), sv)
    e = jnp.exp(s - m)
    es = jnp.exp(sv - m)
    inv_l = 1.0 / (jnp.sum(e, axis=1, keepdims=True) + es)
    return e * inv_l, es * inv_l


def _attn_blocks(nb, blk, carry, per=1):
    carry = blk(0, 0, True, carry)
    per = per if (nb - 1) % per == 0 else 1

    def step(i, c):
        for k in range(per):
            n = 1 + per * i + k
            c = blk(pl.multiple_of(n * BLOCK, BLOCK), pl.multiple_of((n - 1) * BLOCK, BLOCK), False, c)
        return c

    return lax.fori_loop(0, (nb - 1) // per, step, carry)


def _attn_call(sinks, q, k4, v4, S, rider=None):
    T = q.shape[0]
    nb = S // BLOCK

    def body(sink_ref, q_ref, k_ref, v_ref, o_ref):
        lane_g, rgrp, bias_first, bias_later = _attn_consts()
        svs = [_sink_rows(sink_ref, hk, rgrp) for hk in range(N_KV_HEADS)]

        def blk(q0, k0, first, carry):
            for hk in range(N_KV_HEADS):
                cs = slice(256 * hk, 256 * (hk + 1))
                qs = _stack_heads(q_ref[pl.ds(q0, BLOCK), cs], lane_g)
                p, _ = _attn_probs(qs, k_ref[pl.ds(k0, 2 * BLOCK), cs], bias_first if first else bias_later, svs[hk])
                o = _dot(p.astype(MXU_DTYPE), v_ref[pl.ds(k0, 2 * BLOCK), cs], NN)
                o_ref[pl.ds(q0, BLOCK), cs] = _unstack_heads(o, lane_g).astype(MXU_DTYPE)
            return carry

        _attn_blocks(nb, blk, 0, per=3)

    seq = pl.BlockSpec((S, ATTN_WIDTH), lambda b: (b, 0))
    return _launch(
        body, [sinks, q, k4, v4], name="attn_fwd", grid=(T // S,),
        in_specs=[pl.BlockSpec(memory_space=pltpu.SMEM), seq, seq, seq],
        out_specs=[seq], out_shape=[jax.ShapeDtypeStruct((T, ATTN_WIDTH), MXU_DTYPE)],
        sem=("arbitrary",), rider=rider)


def _attn_bwd_call(sinks, q, k4, v4, do, rc, rsa, rsb, S, rider=None):
    T = q.shape[0]
    nb = S // BLOCK

    def body(sink_ref, q_ref, k_ref, v_ref, do_ref, c_ref, sa_ref, sb_ref,
             dq_ref, dk_ref, dv_ref, ds_ref, dk_acc, dv_acc):
        lane_g, rgrp, bias_first, bias_later = _attn_consts()
        svs = [_sink_rows(sink_ref, hk, rgrp) for hk in range(N_KV_HEADS)]
        lane1 = lax.broadcasted_iota(jnp.int32, (1, LANES), 1)
        dk_acc[...] = jnp.zeros_like(dk_acc)
        dv_acc[...] = jnp.zeros_like(dv_acc)

        def blk(q0, k0, first, dsink):
            rows = pl.ds(q0, BLOCK)
            c, sa, sb = c_ref[rows, :], sa_ref[rows, :], sb_ref[rows, :]
            for hk in range(N_KV_HEADS):
                cs = slice(256 * hk, 256 * (hk + 1))
                qs = _stack_heads(q_ref[rows, cs], lane_g)
                dos = _stack_heads(do_ref[rows, cs], lane_g)
                kb = k_ref[pl.ds(k0, 2 * BLOCK), cs]
                vb = v_ref[pl.ds(k0, 2 * BLOCK), cs]
                p, ps = _attn_probs(qs, kb, bias_first if first else bias_later, svs[hk])
                dp = _dot(dos, vb, NT)
                delta = jnp.sum(p * dp, axis=1, keepdims=True)
                dsb = (p * (dp - delta)).astype(MXU_DTYPE)
                dqb = _unstack_heads(_dot(dsb, kb, NN), lane_g) * SCALE
                dq_ref[rows, cs] = _rot(dqb, c, -sa, -sb).astype(MXU_DTYPE)
                dk_acc[pl.ds(k0, 2 * BLOCK), cs] += _dot(dsb, qs, TN)
                dv_acc[pl.ds(k0, 2 * BLOCK), cs] += _dot(p.astype(MXU_DTYPE), dos, TN)
                psd = ps * delta
                for g in range(GROUP):
                    val = -jnp.sum(psd[BLOCK * g:BLOCK * (g + 1)], axis=0, keepdims=True)
                    dsink = dsink + jnp.where(lane1 == GROUP * hk + g, val, 0.0)
            return dsink

        dsink = _attn_blocks(nb, blk, jnp.zeros((1, LANES), F32))
        dk_ref[...] = _rot(_fold_heads(dk_acc[...]), c_ref[...], -sa_ref[...], -sb_ref[...]).astype(MXU_DTYPE)
        dv_ref[...] = _fold_heads(dv_acc[...]).astype(MXU_DTYPE)
        ds_ref[...] = jnp.broadcast_to(dsink, ds_ref.shape)

    seq = pl.BlockSpec((S, ATTN_WIDTH), lambda b: (b, 0))
    kvs = pl.BlockSpec((S, KV_WIDTH), lambda b: (b, 0))
    tab = pl.BlockSpec((S, LANES), lambda b: (0, 0))
    nseq = T // S
    return _launch(
        body, [sinks, q, k4, v4, do, rc, rsa, rsb], name="attn_bwd", grid=(nseq,),
        in_specs=[pl.BlockSpec(memory_space=pltpu.SMEM), seq, seq, seq, seq, tab, tab, tab],
        out_specs=[seq, kvs, kvs, pl.BlockSpec((8, LANES), lambda b: (b, 0))],
        out_shape=[jax.ShapeDtypeStruct((T, ATTN_WIDTH), MXU_DTYPE), jax.ShapeDtypeStruct((T, KV_WIDTH), MXU_DTYPE),
                   jax.ShapeDtypeStruct((T, KV_WIDTH), MXU_DTYPE), jax.ShapeDtypeStruct((8 * nseq, LANES), F32)],
        scratch_shapes=[pltpu.VMEM((S, 512), F32), pltpu.VMEM((S, 512), F32)],
        sem=("arbitrary",), rider=rider)


def _branch_weights(wbp_ref, wba_ref, wbp_s, wba_s):
    @pl.when(pl.program_id(0) == 0)
    def _():
        for j in range(N_DEV):
            wbp_s[:, LANES * j:LANES * (j + 1)] = wbp_ref[j]
            wba_s[:, LANES * j:LANES * (j + 1)] = wba_ref[j]


def _mix_fwd_call(yp, ya, g, x, wbp, wba, wout, g2, g3, rider=None):
    T = x.shape[0]
    tm = _tile(T, 512)

    def body(yp_ref, ya_ref, g_ref, x_ref, wbp_ref, wba_ref, wout_ref, g2_ref, g3_ref,
             mix_ref, x1_ref, h2_ref, h2t_ref, wbp_s, wba_s):
        _branch_weights(wbp_ref, wba_ref, wbp_s, wba_s)
        bp = _dot(yp_ref[...], wbp_s[...], NN)
        ba = _dot(ya_ref[...], wba_s[...], NN)
        merged = g_ref[:, :D_MODEL].astype(F32) * bp + g_ref[:, D_MODEL:].astype(F32) * ba
        mix = _dot(merged.astype(MXU_DTYPE), wout_ref[...], NN)
        mix_ref[...] = mix
        x1 = x_ref[...] + (mix * _rms_r(mix)) * g2_ref[...]
        x1_ref[...] = x1
        h2 = (x1 * _rms_r(x1)) * g3_ref[...]
        h2_ref[...] = h2.astype(MXU_DTYPE)
        h2t_ref[...] = h2.T.astype(MXU_DTYPE)

    tok = lambda w: pl.BlockSpec((tm, w), lambda i: (i, 0))
    full = lambda a: pl.BlockSpec(a.shape, lambda i: (0,) * a.ndim)
    return _launch(
        body, [yp, ya, g, x, wbp, wba, wout, g2, g3], name="mix_fwd", grid=(T // tm,),
        in_specs=[tok(POOL_WIDTH), tok(ATTN_WIDTH), tok(GATE_WIDTH), tok(D_MODEL), full(wbp), full(wba), full(wout),
                  full(g2), full(g3)],
        out_specs=[tok(D_MODEL), tok(D_MODEL), tok(D_MODEL), pl.BlockSpec((D_MODEL, tm), lambda i: (0, i))],
        out_shape=[jax.ShapeDtypeStruct((T, D_MODEL), F32), jax.ShapeDtypeStruct((T, D_MODEL), F32),
                   jax.ShapeDtypeStruct((T, D_MODEL), MXU_DTYPE), jax.ShapeDtypeStruct((D_MODEL, T), MXU_DTYPE)],
        scratch_shapes=[pltpu.VMEM((POOL_WIDTH, D_MODEL), MXU_DTYPE), pltpu.VMEM((ATTN_WIDTH, D_MODEL), MXU_DTYPE)],
        sem=("arbitrary",), rider=rider)


def _mix_bwd_call(dx1, mix, yp, ya, g, wbp, wba, wout, g2, rider=None):
    T = dx1.shape[0]
    tm = _tile(T, 512)

    def body(dx1_ref, mix_ref, yp_ref, ya_ref, g_ref, wbp_ref, wba_ref, wout_ref, g2_ref,
             dyp_ref, do_ref, dgates_ref, dg2_ref, dbg_ref, gout_ref, gbp_ref, gba_ref,
             wbp_s, wba_s, acc_out, acc_bp, acc_ba, sem):
        _branch_weights(wbp_ref, wba_ref, wbp_s, wba_s)
        step = pl.program_id(0)

        @pl.when(step == 0)
        def _():
            dg2_ref[...] = jnp.zeros_like(dg2_ref)
            dbg_ref[...] = jnp.zeros_like(dbg_ref)
            acc_out[...] = jnp.zeros_like(acc_out)
            acc_bp[...] = jnp.zeros_like(acc_bp)
            acc_ba[...] = jnp.zeros_like(acc_ba)

        mix = mix_ref[...]
        dmix, dg2 = _rms_bwd(dx1_ref[...], mix, _rms_r(mix), g2_ref[...])
        dg2_ref[...] += jnp.sum(dg2, axis=0, keepdims=True)
        dmixb = dmix.astype(MXU_DTYPE)
        dmerged = _dot(dmixb, wout_ref[...], NT)
        yp, ya = yp_ref[...], ya_ref[...]
        bp = _dot(yp, wbp_s[...], NN)
        ba = _dot(ya, wba_s[...], NN)
        gp, ga = g_ref[:, :D_MODEL].astype(F32), g_ref[:, D_MODEL:].astype(F32)
        acc_out[...] += _dot((gp * bp + ga * ba).astype(MXU_DTYPE), dmixb, TN)
        dgp = dmerged * bp * (gp * (1.0 - gp))
        dga = dmerged * ba * (ga * (1.0 - ga))
        dbg_ref[:, :D_MODEL] += jnp.sum(dgp, axis=0, keepdims=True)
        dbg_ref[:, D_MODEL:] += jnp.sum(dga, axis=0, keepdims=True)
        dgates_ref[:, :D_MODEL] = dgp.astype(MXU_DTYPE)
        dgates_ref[:, D_MODEL:] = dga.astype(MXU_DTYPE)
        dbp = (dmerged * gp).astype(MXU_DTYPE)
        dba = (dmerged * ga).astype(MXU_DTYPE)
        acc_bp[...] += _dot(yp, dbp, TN)
        acc_ba[...] += _dot(ya, dba, TN)
        dyp_ref[...] = _dot(dbp, wbp_s[...], NT)
        do_ref[...] = _dot(dba, wba_s[...], NT).astype(MXU_DTYPE)

        @pl.when(step == pl.num_programs(0) - 1)
        def _():
            copies = [pltpu.make_async_copy(acc_out, gout_ref, sem.at[0])]
            for j in range(N_DEV):
                cols = slice(LANES * j, LANES * (j + 1))
                copies.append(pltpu.make_async_copy(acc_bp.at[:, cols], gbp_ref.at[j], sem.at[1 + j]))
                copies.append(pltpu.make_async_copy(acc_ba.at[:, cols], gba_ref.at[j], sem.at[1 + N_DEV + j]))
            for cp in copies:
                cp.start()
            for cp in copies:
                cp.wait()

    tok = lambda w: pl.BlockSpec((tm, w), lambda i: (i, 0))
    full = lambda a: pl.BlockSpec(a.shape, lambda i: (0,) * a.ndim)
    acc = lambda w: pl.BlockSpec((1, w), lambda i: (0, 0))
    hbm = pl.BlockSpec(memory_space=pl.ANY)
    sd = jax.ShapeDtypeStruct
    return _launch(
        body, [dx1, mix, yp, ya, g, wbp, wba, wout, g2], name="mix_bwd", grid=(T // tm,),
        in_specs=[tok(D_MODEL), tok(D_MODEL), tok(POOL_WIDTH), tok(ATTN_WIDTH), tok(GATE_WIDTH), full(wbp), full(wba),
                  full(wout), full(g2)],
        out_specs=[tok(POOL_WIDTH), tok(ATTN_WIDTH), tok(GATE_WIDTH), acc(D_MODEL), acc(GATE_WIDTH), hbm, hbm, hbm],
        out_shape=[sd((T, POOL_WIDTH), F32), sd((T, ATTN_WIDTH), MXU_DTYPE), sd((T, GATE_WIDTH), MXU_DTYPE),
                   sd((1, D_MODEL), F32), sd((1, GATE_WIDTH), F32), sd((D_MODEL, D_MODEL), F32),
                   sd((N_DEV, POOL_WIDTH, LANES), F32), sd((N_DEV, ATTN_WIDTH, LANES), F32)],
        scratch_shapes=[pltpu.VMEM((POOL_WIDTH, D_MODEL), MXU_DTYPE), pltpu.VMEM((ATTN_WIDTH, D_MODEL), MXU_DTYPE),
                        pltpu.VMEM((D_MODEL, D_MODEL), F32), pltpu.VMEM((POOL_WIDTH, D_MODEL), F32),
                        pltpu.VMEM((ATTN_WIDTH, D_MODEL), F32), pltpu.SemaphoreType.DMA((1 + 2 * N_DEV,))],
        sem=("arbitrary",), rider=rider)


def _mlp_call(x1, h2, target, wup, wdown, g3, g4):
    T = x1.shape[0]
    tm = _tile(T, 256)
    fc = D_FF // N_DEV

    def body(x1_ref, h2_ref, t_ref, wup_ref, wdown_ref, g3_ref, g4_ref,
             act_ref, da_ref, dff_ref, dx1_ref, dg3_ref, dg4_ref, loss_ref, rl_s):
        @pl.when(pl.program_id(0) == 0)
        def _():
            dg3_ref[...] = jnp.zeros_like(dg3_ref)
            dg4_ref[...] = jnp.zeros_like(dg4_ref)
            loss_ref[...] = jnp.zeros_like(loss_ref)

        h2 = h2_ref[...]
        ff = jnp.zeros((tm, D_MODEL), F32)
        for j in range(N_DEV):
            sl = slice(fc * j, fc * (j + 1))
            rl = jnp.maximum(_dot(h2, wup_ref[j], NN), 0.0)
            rl_s[:, sl] = rl
            act = rl * rl
            act_ref[sl, :] = act.T.astype(MXU_DTYPE)
            ff = ff + _dot(act.astype(MXU_DTYPE), wdown_ref[j], NN)
        x1 = x1_ref[...]
        r4 = _rms_r(ff)
        err = x1 + (ff * r4) * g4_ref[...] - t_ref[...]
        loss_ref[...] += jnp.sum(err * err, axis=0, keepdims=True)
        dy = err * (1.0 / D_MODEL)
        dff, dg4 = _rms_bwd(dy, ff, r4, g4_ref[...])
        dg4_ref[...] += jnp.sum(dg4, axis=0, keepdims=True)
        dffb = dff.astype(MXU_DTYPE)
        dff_ref[...] = dffb
        dh2 = jnp.zeros((tm, D_MODEL), F32)
        for j in range(N_DEV):
            sl = slice(fc * j, fc * (j + 1))
            dab = (_dot(dffb, wdown_ref[j], NT) * (2.0 * rl_s[:, sl])).astype(MXU_DTYPE)
            da_ref[:, sl] = dab
            dh2 = dh2 + _dot(dab, wup_ref[j], NT)
        dx1, dg3 = _rms_bwd(dh2, x1, _rms_r(x1), g3_ref[...])
        dg3_ref[...] += jnp.sum(dg3, axis=0, keepdims=True)
        dx1_ref[...] = dy + dx1

    tok = lambda w: pl.BlockSpec((tm, w), lambda i: (i, 0))
    full = lambda a: pl.BlockSpec(a.shape, lambda i: (0,) * a.ndim, pipeline_mode=pl.Buffered(1))
    vec = pl.BlockSpec((1, D_MODEL), lambda i: (0, 0))
    sd = jax.ShapeDtypeStruct
    return pl.pallas_call(
        body, name="mlp_fwd_bwd", grid=(T // tm,),
        in_specs=[tok(D_MODEL), tok(D_MODEL), tok(D_MODEL), full(wup), full(wdown), vec, vec],
        out_specs=[pl.BlockSpec((D_FF, tm), lambda i: (0, i)), tok(D_FF), tok(D_MODEL), tok(D_MODEL), vec, vec, vec],
        out_shape=[sd((D_FF, T), MXU_DTYPE), sd((T, D_FF), MXU_DTYPE), sd((T, D_MODEL), MXU_DTYPE),
                   sd((T, D_MODEL), F32), sd((1, D_MODEL), F32), sd((1, D_MODEL), F32), sd((1, D_MODEL), F32)],
        scratch_shapes=[pltpu.VMEM((tm, D_FF), F32)],
        compiler_params=_params(("arbitrary",)),
    )(x1, h2, target, wup, wdown, g3, g4)


def _inproj_bwd_call(du, dq, dk, dv, dgates, dx1, x, win_t, g1, rider=None):
    T = x.shape[0]
    tm = _tile(T, 512)

    def body(du_ref, dq_ref, dk_ref, dv_ref, dgt_ref, dx1_ref, x_ref, w_ref, g1_ref, gx_ref, dg1_ref, db_ref):
        @pl.when(pl.program_id(0) == 0)
        def _():
            dg1_ref[...] = jnp.zeros_like(dg1_ref)
            db_ref[...] = jnp.zeros_like(db_ref)

        dh = jnp.zeros((tm, D_MODEL), F32)
        for ref, lo, hi in ((du_ref, 0, C_Q), (dq_ref, C_Q, C_K), (dk_ref, C_K, C_V), (dv_ref, C_V, C_G),
                            (dgt_ref, C_G, IN_WIDTH)):
            piece = ref[...]
            dh = dh + _dot(piece, w_ref[lo:hi, :], NN)
            if hi <= C_G:
                db_ref[:, lo:hi] += jnp.sum(piece.astype(F32), axis=0, keepdims=True)
        xv = x_ref[...]
        dx, dg1 = _rms_bwd(dh, xv, _rms_r(xv), g1_ref[...])
        dg1_ref[...] += jnp.sum(dg1, axis=0, keepdims=True)
        gx_ref[...] = dx1_ref[...] + dx

    tok = lambda w: pl.BlockSpec((tm, w), lambda i: (i, 0))
    full = lambda a: pl.BlockSpec(a.shape, lambda i: (0,) * a.ndim)
    sd = jax.ShapeDtypeStruct
    return _launch(
        body, [du, dq, dk, dv, dgates, dx1, x, win_t, g1], name="inproj_bwd", grid=(T // tm,),
        in_specs=[tok(POOL_WIDTH), tok(ATTN_WIDTH), tok(KV_WIDTH), tok(KV_WIDTH), tok(GATE_WIDTH), tok(D_MODEL),
                  tok(D_MODEL), full(win_t), full(g1)],
        out_specs=[tok(D_MODEL), pl.BlockSpec((1, D_MODEL), lambda i: (0, 0)), pl.BlockSpec((1, C_G), lambda i: (0, 0))],
        out_shape=[sd((T, D_MODEL), F32), sd((1, D_MODEL), F32), sd((1, C_G), F32)],
        sem=("arbitrary",), rider=rider)


WGRAD_TOKENS = 1024


def _wgrad_rows_call(at, b, name, rider=None):
    K, T = at.shape
    N = b.shape[1]
    tm = _tile(T, WGRAD_TOKENS)
    kb = min(K, 1024)
    per = kb // (K // N_DEV)

    def body(a_ref, b_ref, o_ref):
        @pl.when(pl.program_id(1) == 0)
        def _():
            o_ref[...] = jnp.zeros_like(o_ref)

        d = _dot(a_ref[...], b_ref[...], NN)
        rs = kb // per
        for j in range(per):
            o_ref[j] += d[rs * j:rs * (j + 1)]

    return _launch(
        body, [at, b], name=name, grid=(K // kb, T // tm),
        in_specs=[pl.BlockSpec((kb, tm), lambda i, t: (i, t)), pl.BlockSpec((tm, N), lambda i, t: (t, 0))],
        out_specs=[pl.BlockSpec((per, K // N_DEV, N), lambda i, t: (i, 0, 0))],
        out_shape=[jax.ShapeDtypeStruct((N_DEV, K // N_DEV, N), F32)],
        sem=("arbitrary", "arbitrary"), rider=rider)


def _wgrad_cols_call(at, b, name, rider=None):
    K, T = at.shape
    N = b.shape[1]
    tm = _tile(T, WGRAD_TOKENS)
    nb = min(N, 1024)
    per = nb // (N // N_DEV)

    def body(a_ref, b_ref, o_ref):
        @pl.when(pl.program_id(1) == 0)
        def _():
            o_ref[...] = jnp.zeros_like(o_ref)

        d = _dot(a_ref[...], b_ref[...], NN)
        cs = nb // per
        for j in range(per):
            o_ref[j] += d[:, cs * j:cs * (j + 1)]

    return _launch(
        body, [at, b], name=name, grid=(N // nb, T // tm),
        in_specs=[pl.BlockSpec((K, tm), lambda i, t: (0, t)), pl.BlockSpec((tm, nb), lambda i, t: (t, i))],
        out_specs=[pl.BlockSpec((per, K, N // N_DEV), lambda i, t: (i, 0, 0))],
        out_shape=[jax.ShapeDtypeStruct((N_DEV, K, N // N_DEV), F32)],
        sem=("arbitrary", "arbitrary"), rider=rider)


def _wgrad_in_call(du, dq, dk, dv, dgates, h, rider=None):
    T = h.shape[0]
    tm = _tile(T, WGRAD_TOKENS)
    rows = IN_WIDTH // N_DEV

    def body(du_ref, dq_ref, dk_ref, dv_ref, dgt_ref, h_ref, o_ref, acc, sem):
        t = pl.program_id(0)

        @pl.when(t == 0)
        def _():
            acc[...] = jnp.zeros_like(acc)

        hv = h_ref[...]
        for ref, lo, hi in ((du_ref, 0, C_Q), (dq_ref, C_Q, C_K), (dk_ref, C_K, C_V), (dv_ref, C_V, C_G),
                            (dgt_ref, C_G, IN_WIDTH)):
            acc[lo:hi, :] += _dot(ref[...], hv, TN)

        @pl.when(t == pl.num_programs(0) - 1)
        def _():
            copies = [pltpu.make_async_copy(acc.at[pl.ds(rows * j, rows), :], o_ref.at[j], sem.at[j])
                      for j in range(N_DEV)]
            for cp in copies:
                cp.start()
            for cp in copies:
                cp.wait()

    tok = lambda w: pl.BlockSpec((tm, w), lambda t: (t, 0))
    return _launch(
        body, [du, dq, dk, dv, dgates, h], name="wgrad_in", grid=(T // tm,),
        in_specs=[tok(POOL_WIDTH), tok(ATTN_WIDTH), tok(KV_WIDTH), tok(KV_WIDTH), tok(GATE_WIDTH), tok(D_MODEL)],
        out_specs=[pl.BlockSpec(memory_space=pl.ANY)],
        out_shape=[jax.ShapeDtypeStruct((N_DEV, rows, D_MODEL), F32)],
        scratch_shapes=[pltpu.VMEM((IN_WIDTH, D_MODEL), F32), pltpu.SemaphoreType.DMA((N_DEV,))],
        sem=("arbitrary",), rider=rider)


def _coords():
    return lax.axis_index("x"), lax.axis_index("y"), lax.axis_index("c")


def _allgather_call(shards):
    n = len(shards)

    def body(*refs):
        ins, outs = refs[:n], refs[n:2 * n]
        send_sems, recv_sems, local_sems = refs[2 * n:]
        x, y, c = _coords()
        me, sibling = (x, y, c), (x, y, 1 - c)
        chips = [(1 - x, y), (x, 1 - y), (1 - x, 1 - y)]

        def slot(p):
            return 4 * p[0] + 2 * p[1] + p[2]

        def copy(t, k, block, to, src=None):
            dst = outs[t].at[slot(block)]
            return pltpu.make_async_remote_copy(
                src_ref=dst if src is None else src, dst_ref=dst, send_sem=send_sems.at[t, k],
                recv_sem=recv_sems.at[t, k], device_id=to, device_id_type=MESH)

        mine = [pltpu.make_async_copy(ins[t], outs[t].at[slot(me)], local_sems.at[t]) for t in range(n)]
        for cp in mine:
            cp.start()
        first = []
        for t in range(n):
            first.append(copy(t, 0, me, sibling, src=ins[t]))
            first += [copy(t, 1 + j, me, (*chip, c), src=ins[t]) for j, chip in enumerate(chips)]
        for cp in first:
            cp.start()
        passed = []
        for t in range(n):
            for j, chip in enumerate(chips):
                copy(t, 1 + j, (*chip, c), me).wait_recv()
                fwd = copy(t, 4 + j, (*chip, c), sibling)
                fwd.start()
                passed.append(fwd)
        for t in range(n):
            copy(t, 0, sibling, me).wait_recv()
            for j, chip in enumerate(chips):
                copy(t, 4 + j, (*chip, 1 - c), me).wait_recv()
        for cp in first + passed:
            cp.wait_send()
        for cp in mine:
            cp.wait()

    hbm = pl.BlockSpec(memory_space=pl.ANY)
    return pl.pallas_call(
        body, name="allgather_weights",
        in_specs=[hbm] * n, out_specs=[hbm] * n,
        out_shape=[jax.ShapeDtypeStruct((N_DEV,) + s.shape, s.dtype) for s in shards],
        scratch_shapes=[pltpu.SemaphoreType.DMA((n, 7)), pltpu.SemaphoreType.DMA((n, 7)), pltpu.SemaphoreType.DMA((n,))],
    )(*shards)


def _slot(p):
    return 4 * p[0] + 2 * p[1] + p[2]


def _rows(ref, span):
    return ref if span is None else ref.at[pl.ds(span[0], span[1])]


ALL = "all"
LOCAL = "local"


def _rows(ref, span):
    return ref if span == ALL else ref.at[pl.ds(span[0], span[1])]


def _rider_ag(items):
    ins, out_shape, aliases, where = [], [], {}, []
    n_remote = n_local = 0
    for t, (shard, buf, snd, fwd) in enumerate(items):
        i_shard = i_buf = None
        if snd is not None:
            i_shard = len(ins)
            ins.append(shard)
        if buf is not None:
            i_buf = len(ins)
            ins.append(buf)
            aliases[i_buf] = t
            out_shape.append(jax.ShapeDtypeStruct(buf.shape, buf.dtype))
        else:
            assert fwd is None and snd is not None
            out_shape.append(jax.ShapeDtypeStruct((N_DEV,) + shard.shape, shard.dtype))
        where.append((i_shard, i_buf, n_remote, n_local))
        n_remote += (4 if snd not in (None, LOCAL) else 0) + (3 if fwd is not None else 0)
        n_local += 1 if snd is not None else 0

    def plan(rins, routs, send, recv, loc, r0, l0):
        x, y, c = _coords()
        peers = [(x, y, 1 - c), (1 - x, y, c), (x, 1 - y, c), (1 - x, 1 - y, c)]
        remote, local = [], []
        for t, (shard, buf, snd, fwd) in enumerate(items):
            i_shard, i_buf, k, l = where[t]
            k, l = r0 + k, l0 + l
            if snd is not None:
                span = ALL if snd == LOCAL else snd
                src, dst = _rows(rins[i_shard], span), _rows(routs[t].at[_slot((x, y, c))], span)
                local.append(pltpu.make_async_copy(src, dst, loc.at[l]))
                for peer in (peers if snd != LOCAL else []):
                    remote.append(pltpu.make_async_remote_copy(
                        src_ref=src, dst_ref=dst, send_sem=send.at[k], recv_sem=recv.at[k],
                        device_id=peer, device_id_type=MESH))
                    k += 1
            if fwd is not None:
                for px, py, pc in peers[1:]:
                    s = _slot((px, py, pc))
                    remote.append(pltpu.make_async_remote_copy(
                        src_ref=_rows(rins[i_buf].at[s], fwd), dst_ref=_rows(routs[t].at[s], fwd),
                        send_sem=send.at[k], recv_sem=recv.at[k], device_id=peers[0], device_id_type=MESH))
                    k += 1
        return remote, local

    return _Rider(ins, out_shape, n_remote, n_local, plan, aliases)


def _rider_rs_sibling(grads):
    n = len(grads)

    def plan(ins, outs, send, recv, loc, r0, l0):
        x, y, c = _coords()
        remote = []
        for t in range(n):
            for q in range(4):
                remote.append(pltpu.make_async_remote_copy(
                    src_ref=ins[t].at[q, 1 - c], dst_ref=outs[t].at[q], send_sem=send.at[r0 + 4 * t + q],
                    recv_sem=recv.at[r0 + 4 * t + q], device_id=(x, y, 1 - c), device_id_type=MESH))
        return remote, []

    return _Rider(grads, [jax.ShapeDtypeStruct((4,) + g.shape[2:], g.dtype) for g in grads], 4 * n, 0, plan)


def _rider_rs_chips(sums, rows=None, into=None):
    n = len(sums)
    rows = rows or [ALL] * n

    def plan(ins, outs, send, recv, loc, r0, l0):
        x, y, c = _coords()
        remote = []
        for t in range(n):
            for r, (px, py) in enumerate([(1 - x, y), (x, 1 - y), (1 - x, 1 - y)]):
                remote.append(pltpu.make_async_remote_copy(
                    src_ref=_rows(ins[t].at[2 * px + py], rows[t]), dst_ref=_rows(outs[t].at[r], rows[t]),
                    send_sem=send.at[r0 + 3 * t + r], recv_sem=recv.at[r0 + 3 * t + r],
                    device_id=(px, py, c), device_id_type=MESH))
        return remote, []

    out_shape = [jax.ShapeDtypeStruct((3,) + s.shape[1:], s.dtype) for s in sums]
    if into is None:
        return _Rider(sums, out_shape, 3 * n, 0, plan)
    return _Rider(list(sums) + list(into), out_shape, 3 * n, 0, plan, aliases={n + t: t for t in range(n)})


def _rider_gather_remote(parts):
    n = len(parts)

    def plan(ins, outs, send, recv, loc, r0, l0):
        x, y, c = _coords()
        me = _slot((x, y, c))
        remote = []
        for t in range(n):
            for k in range(1, N_DEV):
                peer = (x ^ ((k >> 2) & 1), y ^ ((k >> 1) & 1), c ^ (k & 1))
                remote.append(pltpu.make_async_remote_copy(
                    src_ref=ins[t], dst_ref=outs[t].at[me], send_sem=send.at[r0 + 7 * t + k - 1],
                    recv_sem=recv.at[r0 + 7 * t + k - 1], device_id=peer, device_id_type=MESH))
        return remote, []

    return _Rider(parts, [jax.ShapeDtypeStruct((N_DEV,) + p.shape, p.dtype) for p in parts], 7 * n, 0, plan)


def _chip_sum_call(idx, grads, recvd, out_dtypes, name):
    n = len(grads)

    def body(i_ref, *refs):
        for t in range(n):
            refs[2 * n + t][0] = (refs[t][0, 0] + refs[n + t][0]).astype(out_dtypes[t])

    def chip(k, s):
        return jnp.where(k >= s[0], k + 1, k)

    in_specs = [pl.BlockSpec((1, 1) + g.shape[2:], lambda k, s: (chip(k, s), s[1], 0, 0)) for g in grads]
    in_specs += [pl.BlockSpec((1,) + r.shape[1:], lambda k, s: (chip(k, s), 0, 0)) for r in recvd]
    return pl.pallas_call(
        body, name=name,
        grid_spec=pltpu.PrefetchScalarGridSpec(
            num_scalar_prefetch=1, grid=(3,), in_specs=in_specs,
            out_specs=[pl.BlockSpec((1,) + r.shape[1:], lambda k, s: (chip(k, s), 0, 0)) for r in recvd]),
        out_shape=[jax.ShapeDtypeStruct(r.shape, dt) for r, dt in zip(recvd, out_dtypes)],
        compiler_params=_params(("arbitrary",)),
    )(idx, *grads, *recvd)


def _final_sum_call(idx, grads, recvd1, recvd2):
    n = len(grads)
    nsteps = 2

    def body(i_ref, *refs):
        for t in range(n):
            g, r1, r2, o = refs[t], refs[n + t], refs[2 * n + t], refs[3 * n + t]
            s = g[0, 0] + r1[0]
            for r in range(3):
                s = s + r2[r].astype(F32)
            o[...] = s

    def rows(a):
        r = a.shape[-2]
        return r // nsteps if (r // nsteps) % 16 == 0 else r

    def step(a):
        return (lambda i: i) if rows(a) != a.shape[-2] else (lambda i: 0)

    in_specs = [pl.BlockSpec((1, 1, rows(g), g.shape[3]), lambda i, s, st=step(g): (s[0], s[1], st(i), 0)) for g in grads]
    in_specs += [pl.BlockSpec((1, rows(r), r.shape[2]), lambda i, s, st=step(r): (s[0], st(i), 0)) for r in recvd1]
    in_specs += [pl.BlockSpec((3, rows(r), r.shape[2]), lambda i, s, st=step(r): (0, st(i), 0)) for r in recvd2]
    return pl.pallas_call(
        body, name="rs_final_sum",
        grid_spec=pltpu.PrefetchScalarGridSpec(
            num_scalar_prefetch=1, grid=(nsteps,), in_specs=in_specs,
            out_specs=[pl.BlockSpec((rows(r), r.shape[2]), lambda i, s, st=step(r): (st(i), 0)) for r in recvd2]),
        out_shape=[jax.ShapeDtypeStruct(r.shape[1:], F32) for r in recvd2],
        compiler_params=_params(("arbitrary",)),
    )(idx, *grads, *recvd1, *recvd2)


def _sum8_call(parts):
    def body(p_ref, o_ref):
        s = p_ref[0]
        for j in range(1, N_DEV):
            s = s + p_ref[j]
        o_ref[...] = s

    return pl.pallas_call(body, name="sum_small_partials",
                          out_shape=jax.ShapeDtypeStruct(parts.shape[1:], parts.dtype))(parts)


def _adamw(w, g, m, v):
    m = ADAM_B1 * m + (1.0 - ADAM_B1) * g
    v = ADAM_B2 * v + (1.0 - ADAM_B2) * (g * g)
    m_hat = m / (1.0 - ADAM_B1 ** ADAM_STEP)
    v_hat = v / (1.0 - ADAM_B2 ** ADAM_STEP)
    delta = -ADAM_LR * (m_hat / (jnp.sqrt(v_hat) + ADAM_EPS) + ADAM_WD * w)
    return delta, m, v


def _adamw_call(ws, gs, ms, vs, nsteps, name):
    n = len(ws)

    def body(*refs):
        for t in range(n):
            w, g, m, v = (refs[k * n + t][...] for k in range(4))
            d, m2, v2 = _adamw(w, g, m, v)
            refs[4 * n + t][...] = d
            refs[5 * n + t][...] = m2
            refs[6 * n + t][...] = v2

    def spec(a):
        assert a.shape[0] % nsteps == 0 and (nsteps == 1 or (a.shape[0] // nsteps) % 8 == 0), a.shape
        return pl.BlockSpec((a.shape[0] // nsteps, a.shape[1]), lambda i: (i, 0))

    specs = [spec(a) for a in ws]
    outs = pl.pallas_call(
        body, name=name, grid=(nsteps,),
        in_specs=specs * 4, out_specs=specs * 3,
        out_shape=[jax.ShapeDtypeStruct(a.shape, F32) for a in ws] * 3,
        compiler_params=_params(("arbitrary",)),
    )(*ws, *gs, *ms, *vs)
    return outs[:n], outs[n:2 * n], outs[2 * n:]


def _adamw_rs_call(idx, after, gws, r1s, r2s, ws, ms, vs, nsteps, name):
    n = len(ws)

    def body(i_ref, after_ref, *refs):
        for t in range(n):
            gw, r1, r2, w, m, v = (refs[k * n + t] for k in range(6))
            g = gw[0, 0] + r1[0]
            for r in range(3):
                g = g + r2[r].astype(F32)
            d, m2, v2 = _adamw(w[...], g, m[...], v[...])
            refs[6 * n + t][...] = g
            refs[7 * n + t][...] = d
            refs[8 * n + t][...] = m2
            refs[9 * n + t][...] = v2

    def rb(a):
        r = a.shape[0] // nsteps
        assert a.shape[0] % nsteps == 0 and r % 16 == 0, a.shape
        return r

    in_specs = [pl.BlockSpec((1, 1, rb(w), w.shape[1]), lambda i, s: (s[0], s[1], i, 0)) for w in ws]
    in_specs += [pl.BlockSpec((1, rb(w), w.shape[1]), lambda i, s: (s[0], i, 0)) for w in ws]
    in_specs += [pl.BlockSpec((3, rb(w), w.shape[1]), lambda i, s: (0, i, 0)) for w in ws]
    plain = [pl.BlockSpec((rb(w), w.shape[1]), lambda i, s: (i, 0)) for w in ws]
    outs = pl.pallas_call(
        body, name=name,
        grid_spec=pltpu.PrefetchScalarGridSpec(
            num_scalar_prefetch=1, grid=(nsteps,),
            in_specs=[pl.BlockSpec(memory_space=pl.ANY)] + in_specs + plain * 3, out_specs=plain * 4),
        out_shape=[jax.ShapeDtypeStruct(w.shape, F32) for w in ws] * 4,
        compiler_params=_params(("arbitrary",)),
    )(idx, after, *gws, *r1s, *r2s, *ws, *ms, *vs)
    return outs[:n], outs[n:2 * n], outs[2 * n:3 * n], outs[3 * n:]


def _rows128(a, pad_rows):
    flat = a.reshape(-1).astype(F32)
    flat = jnp.pad(flat, (0, pad_rows * LANES - flat.shape[0]))
    return flat.reshape(pad_rows, LANES)


_SMALL_A = (("w_pool", 512), ("pool_scale", 8), ("attn_sinks", 8), ("g_mix_post", 8), ("g_mlp_pre", 8),
            ("g_mlp_post", 8), ("loss", 8), ("b_in_gates", 16))
_SMALL_A_ROWS = 640
_SMALL_B = (("g_mix_pre", 8), ("b_in_head", 16))


def _pack(parts, layout, total_rows):
    rows = [_rows128(parts[k], r) for k, r in layout]
    pad = total_rows - sum(r for _, r in layout)
    if pad:
        rows.append(jnp.zeros((pad, LANES), F32))
    return jnp.concatenate(rows, axis=0)


def _unpack(buf, layout, sizes):
    out, off = {}, 0
    for k, r in layout:
        out[k] = buf[off:off + r].reshape(-1)[:sizes[k]]
        off += r
    return out


def kernel(x, g_mix_pre, w_in, b_in, w_pool, pool_scale, attn_sinks, w_branch_pool, w_branch_attn, w_out, g_mix_post, g_mlp_pre, w_up, w_down, g_mlp_post, loss_target, m_g_mix_pre, m_w_in, m_b_in, m_w_pool, m_pool_scale, m_attn_sinks, m_w_branch_pool, m_w_branch_attn, m_w_out, m_g_mix_post, m_g_mlp_pre, m_w_up, m_w_down, m_g_mlp_post, v_g_mix_pre, v_w_in, v_b_in, v_w_pool, v_pool_scale, v_attn_sinks, v_w_branch_pool, v_w_branch_attn, v_w_out, v_g_mix_post, v_g_mlp_pre, v_w_up, v_w_down, v_g_mlp_post):
    B, S, _ = x.shape
    T = B * S
    xt = x.reshape(T, D_MODEL)
    tgt = loss_target.reshape(T, D_MODEL)
    cx, cy, cc = _coords()

    cidx = jnp.stack([2 * cx + cy, cc]).astype(jnp.int32)
    by_chip = lambda gr: gr.reshape((4, 2) + gr.shape[1:])
    bf = lambda w: w[0].astype(MXU_DTYPE)

    (win_s,) = _allgather_call([w_in[0].T.astype(MXU_DTYPE)])
    win_t = win_s.reshape(IN_WIDTH, D_MODEL)
    wpool_b = bf(w_pool)
    rc, rsa, rsb = _rot_tables(S)

    up_a, up_b = (0, D_MODEL // 2), (D_MODEL // 2, D_MODEL // 2)
    dn_a, dn_b = (0, D_FF // 16), (D_FF // 16, D_FF // 16)
    wup_l, wdown_l = bf(w_up), bf(w_down)
    (h, u, q, k4, v4, g), (wbp_1, wba_1, wout_1, wup_1) = _inproj_call(
        xt, g_mix_pre, win_t, b_in, rc, rsa, rsb, S,
        rider=_rider_ag([(bf(w_branch_pool), None, ALL, None), (bf(w_branch_attn), None, ALL, None),
                         (bf(w_out), None, ALL, None), (wup_l, None, up_a, None)]))
    yp = _pool_call(u, wpool_b, pool_scale, S)
    (ya,), (wbp_s, wba_s, wout_s, wup_2, wdown_1) = _attn_call(
        attn_sinks, q, k4, v4, S,
        rider=_rider_ag([(None, wbp_1, None, ALL), (None, wba_1, None, ALL), (None, wout_1, None, ALL),
                         (wup_l, wup_1, up_b, up_a), (wdown_l, None, dn_a, None)]))
    wout_f = wout_s.reshape(D_MODEL, D_MODEL)
    (mix, x1, h2, h2_t), (wup_s, wdown_2) = _mix_fwd_call(
        yp, ya, g, xt, wbp_s, wba_s, wout_f, g_mix_post, g_mlp_pre,
        rider=_rider_ag([(None, wup_2, None, up_b), (wdown_l, wdown_1, dn_b, dn_a)]))
    (wdown_s,) = _comm_call(_rider_ag([(None, wdown_2, None, dn_b)]), "allgather_finish")

    act_t, da, dff, dx1, dg3, dg4, lossvec = _mlp_call(x1, h2, tgt, wup_s, wdown_s, g_mlp_pre, g_mlp_post)
    gw_down = by_chip(_wgrad_rows_call(act_t, dff, "wgrad_down")[0])
    (gw_up,), (r1_down,) = _wgrad_cols_call(h2_t, da, "wgrad_up", rider=_rider_rs_sibling([gw_down]))
    gw_up = by_chip(gw_up)
    (s_down,) = _chip_sum_call(cidx, [gw_down], [r1_down], [MXU_DTYPE], "rs_chip_sum_down")
    (c_down,), tok = _copies_start([_rider_rs_chips([s_down])], "rs_chips_start_down")
    (dyp, do, dgates, dg2, dbg, gw_out, gw_bp, gw_ba), (r1_up,) = _mix_bwd_call(
        dx1, mix, yp, ya, g, wbp_s, wba_s, wout_f, g_mix_post, rider=_after(tok, _rider_rs_sibling([gw_up])))
    gw_out = by_chip(gw_out.reshape(N_DEV, D_MODEL // N_DEV, D_MODEL))
    gw_bp, gw_ba = by_chip(gw_bp), by_chip(gw_ba)
    (s_up,) = _chip_sum_call(cidx, [gw_up], [r1_up], [MXU_DTYPE], "rs_chip_sum_up")
    (c_up,), tok = _copies_start([_rider_rs_chips([s_up])], "rs_chips_start_up")
    (dq, dk, dv, dsink), (r1_out, r1_bp, r1_ba) = _attn_bwd_call(
        attn_sinks, q, k4, v4, do, rc, rsa, rsb, S, rider=_after(tok, _rider_rs_sibling([gw_out, gw_bp, gw_ba])))
    s_obb = _chip_sum_call(cidx, [gw_out, gw_bp, gw_ba], [r1_out, r1_bp, r1_ba], [MXU_DTYPE] * 3, "rs_chip_sum_branch")
    (c_obb,), tok = _copies_start([_rider_rs_chips(s_obb)], "rs_chips_start_branch")
    (du, dwp, dps), _ = _pool_bwd_call(u, dyp, wpool_b, pool_scale, S, rider=_after(tok))
    (gw_in,) = _wgrad_in_call(du, dq, dk, dv, dgates, h)
    gw_in = by_chip(gw_in)

    small_a = {"w_pool": dwp, "pool_scale": dps,
               "attn_sinks": jnp.sum(dsink.reshape(B, 8, LANES)[:, 0, :N_Q_HEADS], axis=0), "g_mix_post": dg2,
               "g_mlp_pre": dg3, "g_mlp_post": dg4, "loss": lossvec, "b_in_gates": dbg}
    gw_sa = by_chip(_pack(small_a, _SMALL_A, _SMALL_A_ROWS).reshape(N_DEV, _SMALL_A_ROWS // N_DEV, LANES))
    r1_in, r1_sa = _comm_call(_rider_rs_sibling([gw_in, gw_sa]), "rs_sibling_in")
    s_in, s_sa = _chip_sum_call(cidx, [gw_in, gw_sa], [r1_in, r1_sa], [MXU_DTYPE, F32], "rs_chip_sum_in")
    (c_in,), tok = _copies_start([_rider_rs_chips([s_in, s_sa])], "rs_chips_start_in")
    (gx, dg1, dba_in), _ = _inproj_bwd_call(du, dq, dk, dv, dgates, dx1, xt, win_t, g_mix_pre, rider=_after(tok))
    r2_down, r2_up, r2_out, r2_bp, r2_ba, r2_in, r2_sa = _copies_wait([c_down, c_up, c_obb, c_in], dg1, "rs_chips_wait")

    (g_sa,) = _final_sum_call(cidx, [gw_sa], [r1_sa], [r2_sa])
    part_b = _pack({"g_mix_pre": dg1, "b_in_head": dba_in}, _SMALL_B, sum(r for _, r in _SMALL_B))
    (c_small,), tok = _copies_start([_rider_gather_remote([g_sa, part_b])], "allgather_small_start")

    in_t = _adamw_rs_call(cidx, tok, [gw_in], [r1_in], [r2_in], [w_in[0].T], [m_w_in[0].T], [v_w_in[0].T], 2,
                          "adamw_w_in")
    rest = _adamw_rs_call(
        cidx, tok, [gw_bp, gw_ba, gw_out, gw_up, gw_down], [r1_bp, r1_ba, r1_out, r1_up, r1_down],
        [r2_bp, r2_ba, r2_out, r2_up, r2_down], [w_branch_pool[0], w_branch_attn[0], w_out[0], w_up[0], w_down[0]],
        [m_w_branch_pool[0], m_w_branch_attn[0], m_w_out[0], m_w_up[0], m_w_down[0]],
        [v_w_branch_pool[0], v_w_branch_attn[0], v_w_out[0], v_w_up[0], v_w_down[0]], N_DEV, "adamw_shards")
    big_g, big_d, big_m2, big_v2 = ([a[0].T] + list(b) for a, b in zip(in_t, rest))

    sa_all, sb_all = _copies_wait([c_small], rest[0][0], "allgather_small_wait")
    me = (_slot((cx, cy, cc)), 0, 0)
    sa_all = lax.dynamic_update_slice(sa_all, g_sa[None], me)
    sb_sum = _sum8_call(lax.dynamic_update_slice(sb_all, part_b[None], me))

    names = ["g_mix_pre", "b_in", "w_pool", "pool_scale", "attn_sinks", "g_mix_post", "g_mlp_pre", "g_mlp_post"]
    sm_w = dict(g_mix_pre=g_mix_pre, b_in=b_in, w_pool=w_pool, pool_scale=pool_scale, attn_sinks=attn_sinks,
                g_mix_post=g_mix_post, g_mlp_pre=g_mlp_pre, g_mlp_post=g_mlp_post)
    sm_m = dict(g_mix_pre=m_g_mix_pre, b_in=m_b_in, w_pool=m_w_pool, pool_scale=m_pool_scale, attn_sinks=m_attn_sinks,
                g_mix_post=m_g_mix_post, g_mlp_pre=m_g_mlp_pre, g_mlp_post=m_g_mlp_post)
    sm_v = dict(g_mix_pre=v_g_mix_pre, b_in=v_b_in, w_pool=v_w_pool, pool_scale=v_pool_scale, attn_sinks=v_attn_sinks,
                g_mix_post=v_g_mix_post, g_mlp_pre=v_g_mlp_pre, g_mlp_post=v_g_mlp_post)
    sizes = {k: sm_w[k].size for k in names}
    sizes.update(loss=D_MODEL, b_in_gates=GATE_WIDTH, b_in_head=C_G)
    sm_g = _unpack(sa_all.reshape(_SMALL_A_ROWS, LANES), _SMALL_A, sizes)
    sm_g.update(_unpack(sb_sum, _SMALL_B, sizes))
    sm_g["b_in"] = jnp.concatenate([sm_g["b_in_head"], sm_g["b_in_gates"]])
    loss = (0.5 / D_MODEL) * jnp.sum(sm_g["loss"])
    two_d = lambda a: a.reshape(-1, a.shape[-1])
    sd_, sm2_, sv2_ = _adamw_call([two_d(sm_w[k]) for k in names], [two_d(sm_g[k].reshape(sm_w[k].shape)) for k in names],
                                  [two_d(sm_m[k]) for k in names], [two_d(sm_v[k]) for k in names], 1, "adamw_small")
    like = lambda vals: {k: a.reshape(sm_w[k].shape) for k, a in zip(names, vals)}
    sm_d, sm_m2, sm_v2 = like(sd_), like(sm2_), like(sv2_)
    sm_gr = {k: sm_g[k].reshape(sm_w[k].shape) for k in names}

    order = ["g_mix_pre", "w_in", "b_in", "w_pool", "pool_scale", "attn_sinks", "w_branch_pool", "w_branch_attn",
             "w_out", "g_mix_post", "g_mlp_pre", "w_up", "w_down", "g_mlp_post"]
    big_names = ["w_in", "w_branch_pool", "w_branch_attn", "w_out", "w_up", "w_down"]
    lead = lambda a: a[None]
    tables = []
    for small_t, big_t in ((sm_gr, big_g), (sm_d, big_d), (sm_m2, big_m2), (sm_v2, big_v2)):
        bt = dict(zip(big_names, big_t))
        tables.append([lead(bt[k]) if k in bt else small_t[k] for k in order])
    return (loss, gx.reshape(B, S, D_MODEL), *tables[0], *tables[1], *tables[2], *tables[3])
```

```python
import functools

import jax
import jax.numpy as jnp
from jax import lax
from jax.experimental import pallas as pl
from jax.experimental.pallas import tpu as pltpu

F32 = jnp.float32
MXU_DTYPE = jnp.bfloat16
MESH = pl.DeviceIdType.MESH

D_MODEL = 1024
POOL_WINDOWS = (2, 4, 8, 16)
POOL_WIDTH = 512
POOL_GC = 128
HEAD_DIM = 64
N_Q_HEADS = 8
N_KV_HEADS = 2
GROUP = 4
ATTN_WIDTH = 512
KV_WIDTH = 128
BLOCK = 128
GATE_WIDTH = 2048
IN_WIDTH = 3328
D_FF = 4096
EPS = 1e-6
NEG_INF = -1e30
ROPE_THETA = 500000.0
ROT_DIM = 16
SCALE = HEAD_DIM ** -0.5
C_Q, C_K, C_V, C_G = 512, 1024, 1152, 1280

ADAM_LR = 0.001
ADAM_B1 = 0.9
ADAM_B2 = 0.999
ADAM_EPS = 1e-08
ADAM_WD = 0.01
ADAM_STEP = 10

N_DEV = 8
LANES = 128
VMEM_LIMIT = 56 * 1024 * 1024

NN = (((1,), (0,)), ((), ()))
NT = (((1,), (1,)), ((), ()))
TN = (((0,), (0,)), ((), ()))


def _dot(a, b, dims):
    return lax.dot_general(a, b, dims, preferred_element_type=F32)


def _params(sem=None):
    return pltpu.CompilerParams(dimension_semantics=sem, vmem_limit_bytes=VMEM_LIMIT)


def _tile(n, pref):
    t = min(n, pref)
    assert n % t == 0, (n, t)
    return t


class _Rider:
    def __init__(self, ins, out_shape, n_remote, n_local, plan, aliases=None):
        self.ins, self.out_shape, self.n_remote, self.n_local = list(ins), list(out_shape), n_remote, n_local
        self.plan, self.aliases = plan, dict(aliases or {})


def _after(token, rider=None):
    r = rider or _Rider([], [], 0, 0, lambda ins, outs, send, recv, loc, r0, l0: ([], []))
    return _Rider(r.ins + [token], r.out_shape, r.n_remote, r.n_local, r.plan, r.aliases)


def _launch(body, args, *, name, grid, in_specs, out_specs, out_shape, scratch_shapes=(), sem=None, rider=None):
    if rider is None:
        return pl.pallas_call(body, name=name, grid=grid, in_specs=in_specs, out_specs=out_specs, out_shape=out_shape,
                              scratch_shapes=list(scratch_shapes), compiler_params=_params(sem))(*args)
    n_in, n_out, n_scr = len(args), len(out_shape), len(scratch_shapes)
    r_in, r_out = len(rider.ins), len(rider.out_shape)
    copies = rider.n_remote + rider.n_local > 0

    def wrapped(*refs):
        ins, rins = refs[:n_in], refs[n_in:n_in + r_in]
        o0 = n_in + r_in
        outs, routs = refs[o0:o0 + n_out], refs[o0 + n_out:o0 + n_out + r_out]
        s0 = o0 + n_out + r_out
        scr = refs[s0:s0 + n_scr]
        if not copies:
            return body(*ins, *outs, *scr)
        send, recv, loc = refs[s0 + n_scr:]
        first, last = None, None
        for d in range(len(grid)):
            f, l = pl.program_id(d) == 0, pl.program_id(d) == pl.num_programs(d) - 1
            first = f if first is None else first & f
            last = l if last is None else last & l

        def start():
            remote, local = rider.plan(rins, routs, send, recv, loc, 0, 0)
            for cp in local + remote:
                cp.start()

        def finish():
            remote, local = rider.plan(rins, routs, send, recv, loc, 0, 0)
            for cp in remote + local:
                cp.wait()

        if first is None:
            start()
            body(*ins, *outs, *scr)
            finish()
        else:
            pl.when(first)(start)
            body(*ins, *outs, *scr)
            pl.when(last)(finish)

    hbm = pl.BlockSpec(memory_space=pl.ANY)
    dma = pltpu.SemaphoreType.DMA
    res = pl.pallas_call(
        wrapped, name=name, grid=grid, in_specs=list(in_specs) + [hbm] * r_in,
        out_specs=list(out_specs) + [hbm] * r_out, out_shape=list(out_shape) + rider.out_shape,
        scratch_shapes=list(scratch_shapes) + (
            [dma((max(rider.n_remote, 1),)), dma((max(rider.n_remote, 1),)), dma((max(rider.n_local, 1),))] if copies else []),
        input_output_aliases={n_in + i: n_out + o for i, o in rider.aliases.items()},
        compiler_params=_params(sem),
    )(*args, *rider.ins)
    return list(res[:n_out]), list(res[n_out:])


def _comm_call(rider, name):
    return _launch(lambda: None, [], name=name, grid=(), in_specs=[], out_specs=[], out_shape=[], rider=rider)[1]


_HBM = pl.BlockSpec(memory_space=pltpu.HBM)
_SEM = pl.BlockSpec(memory_space=pltpu.SEMAPHORE)
_EFFECT = pltpu.SideEffectType.DATAFLOW_SIDE_EFFECTING


def _copies_start(riders, name):
    assert all(r.n_local == 0 and not r.aliases for r in riders)
    sizes = [(len(r.ins), len(r.out_shape)) for r in riders]
    bufs = []
    for r in riders:
        bufs += [pltpu.with_memory_space_constraint(a, pltpu.HBM) for a in r.ins]
        bufs += [pltpu.with_memory_space_constraint(lax.empty(s.shape, s.dtype), pltpu.HBM) for s in r.out_shape]
    nb, ng = len(bufs), len(riders)

    def body(*refs):
        sems, token, at = refs[2 * nb:2 * nb + 2 * ng], refs[-1], 0
        for g, (r, (ni, no)) in enumerate(zip(riders, sizes)):
            remote, _ = r.plan(refs[at:at + ni], refs[at + ni:at + ni + no], sems[2 * g], sems[2 * g + 1], None, 0, 0)
            for cp in remote:
                cp.start()
            at += ni + no
        token[...] = jnp.zeros_like(token)

    res = pl.pallas_call(
        body, name=name, in_specs=[_HBM] * nb,
        out_specs=[_HBM] * nb + [_SEM] * (2 * ng) + [pl.BlockSpec(memory_space=pltpu.VMEM)],
        out_shape=[pltpu.HBM(a.shape, a.dtype) for a in bufs]
        + [pltpu.SemaphoreType.DMA((r.n_remote,)) for r in riders for _ in range(2)]
        + [jax.ShapeDtypeStruct((8, LANES), F32)],
        input_output_aliases={i: i for i in range(nb)},
        compiler_params=pltpu.CompilerParams(has_side_effects=_EFFECT),
    )(*bufs)
    handles, at = [], 0
    for g, (r, (ni, no)) in enumerate(zip(riders, sizes)):
        handles.append((r, list(res[at:at + ni + no]), res[nb + 2 * g], res[nb + 2 * g + 1]))
        at += ni + no
    return handles, res[-1]


def _copies_wait(handles, after, name):
    bufs = [b for _, bs, _, _ in handles for b in bs]
    sems = [s for _, _, send, recv in handles for s in (send, recv)]
    nb, ng = len(bufs), len(handles)

    def body(*refs):
        at = 0
        for g, (rider, bs, _, _) in enumerate(handles):
            ni = len(rider.ins)
            remote, _ = rider.plan(refs[at:at + ni], refs[at + ni:at + len(bs)], refs[nb + 2 * g], refs[nb + 2 * g + 1],
                                   None, 0, 0)
            for cp in remote:
                cp.wait_send()
                cp.wait_recv()
            at += len(bs)

    res = pl.pallas_call(
        body, name=name, in_specs=[_HBM] * nb + [_SEM] * (2 * ng) + [pl.BlockSpec(memory_space=pl.ANY)],
        out_specs=[_HBM] * nb, out_shape=[pltpu.HBM(a.shape, a.dtype) for a in bufs],
        input_output_aliases={i: i for i in range(nb)},
        compiler_params=pltpu.CompilerParams(has_side_effects=_EFFECT),
    )(*bufs, *sems, after)
    lands, at = [], 0
    for rider, bs, _, _ in handles:
        lands += list(res[at + len(rider.ins):at + len(bs)])
        at += len(bs)
    return lands


def _rms_r(x):
    return lax.rsqrt(jnp.mean(x * x, axis=-1, keepdims=True) + EPS)


def _rms_bwd(dn, x, r, g):
    xh = x * r
    dxh = dn * g
    dx = r * (dxh - xh * jnp.mean(dxh * xh, axis=-1, keepdims=True))
    return dx, dn * xh


def _rot(t, c, sa, sb):
    outs = []
    for j in range(t.shape[1] // LANES):
        tj = t[:, LANES * j:LANES * (j + 1)]
        outs.append(tj * c + pltpu.roll(tj, LANES - 8, 1) * sa + pltpu.roll(tj, 8, 1) * sb)
    return outs[0] if len(outs) == 1 else jnp.concatenate(outs, axis=1)


def _rot_tables(S):
    pos = jnp.arange(S, dtype=F32)
    inv_freq = ROPE_THETA ** (-jnp.arange(0, ROT_DIM, 2, dtype=F32) / ROT_DIM)
    ang = pos[:, None] * inv_freq[None, :]
    cos, sin = jnp.cos(ang), jnp.sin(ang)
    one = jnp.ones((S, HEAD_DIM - ROT_DIM), F32)
    zero = jnp.zeros((S, HEAD_DIM - ROT_DIM), F32)
    z8 = jnp.zeros((S, 8), F32)
    c = jnp.concatenate([cos, cos, one], axis=1)
    sa = jnp.concatenate([-sin, z8, zero], axis=1)
    sb = jnp.concatenate([z8, sin, zero], axis=1)
    rep = LANES // HEAD_DIM
    return jnp.tile(c, (1, rep)), jnp.tile(sa, (1, rep)), jnp.tile(sb, (1, rep))


def _lane_tile4(k):
    lane = lax.broadcasted_iota(jnp.int32, k.shape, 1)
    rk = pltpu.roll(k, HEAD_DIM, 1)
    x0 = jnp.where(lane < HEAD_DIM, k, rk)
    x1 = jnp.where(lane < HEAD_DIM, rk, k)
    return jnp.concatenate([x0, x0, x1, x1], axis=1)


def _fold_heads(acc):
    zs = []
    for hk in range(N_KV_HEADS):
        a = acc[:, 256 * hk:256 * hk + LANES] + acc[:, 256 * hk + LANES:256 * (hk + 1)]
        zs.append(a + pltpu.roll(a, HEAD_DIM, 1))
    lane = lax.broadcasted_iota(jnp.int32, zs[0].shape, 1)
    return jnp.where(lane < HEAD_DIM, zs[0], zs[1])


def _inproj_call(x, g1, win_t, b_in, rc, rsa, rsb, S, rider=None):
    T = x.shape[0]
    tm = _tile(S, 512)
    nst = S // tm

    def body(x_ref, g1_ref, w_ref, b_ref, c_ref, sa_ref, sb_ref,
             h_ref, u_ref, q_ref, k4_ref, v4_ref, g_ref):
        xv = x_ref[...]
        hb = ((xv * _rms_r(xv)) * g1_ref[...]).astype(MXU_DTYPE)
        h_ref[...] = hb

        def proj(lo, hi):
            return _dot(hb, w_ref[lo:hi, :], NT) + b_ref[:, lo:hi]

        c, sa, sb = c_ref[...], sa_ref[...], sb_ref[...]
        u_ref[...] = proj(0, C_Q)
        q_ref[...] = (_rot(proj(C_Q, C_K), c, sa, sb) * SCALE).astype(MXU_DTYPE)
        kv = proj(C_K, C_G)
        k4_ref[...] = _lane_tile4(_rot(kv[:, :KV_WIDTH], c, sa, sb)).astype(MXU_DTYPE)
        v4_ref[...] = _lane_tile4(kv[:, KV_WIDTH:]).astype(MXU_DTYPE)
        g_ref[...] = jax.nn.sigmoid(proj(C_G, IN_WIDTH)).astype(MXU_DTYPE)

    tok = lambda w: pl.BlockSpec((tm, w), lambda i: (i, 0))
    full = lambda a: pl.BlockSpec(a.shape, lambda i: (0,) * a.ndim)
    tab = pl.BlockSpec((tm, LANES), lambda i: (i % nst, 0))
    return _launch(
        body, [x, g1, win_t, b_in, rc, rsa, rsb], name="inproj_fwd", grid=(T // tm,),
        in_specs=[tok(D_MODEL), full(g1), full(win_t), full(b_in), tab, tab, tab],
        out_specs=[tok(D_MODEL), tok(POOL_WIDTH), tok(ATTN_WIDTH), tok(512), tok(512), tok(GATE_WIDTH)],
        out_shape=[jax.ShapeDtypeStruct((T, D_MODEL), MXU_DTYPE), jax.ShapeDtypeStruct((T, POOL_WIDTH), F32),
                   jax.ShapeDtypeStruct((T, ATTN_WIDTH), MXU_DTYPE), jax.ShapeDtypeStruct((T, 512), MXU_DTYPE),
                   jax.ShapeDtypeStruct((T, 512), MXU_DTYPE), jax.ShapeDtypeStruct((T, GATE_WIDTH), MXU_DTYPE)],
        sem=("arbitrary",), rider=rider)


def _shift_rows(a, k, rows):
    n = a.shape[0]
    if k > 0:
        return jnp.where(rows >= k, pltpu.roll(a, k, 0), 0.0)
    return jnp.where(rows < n + k, pltpu.roll(a, n + k, 0), 0.0)


def _win_sum(a, w, rows, sign):
    s, k = a, 1
    while k < w:
        s = s + _shift_rows(s, sign * k, rows)
        k *= 2
    return s


def _pool_diff(ug, w, rows):
    inv = 1.0 / jnp.minimum(rows + 1, w).astype(F32)
    return _win_sum(ug, w, rows, 1) * inv - ug, inv


def _pool_call(u, w_pool, pool_scale, S):
    T = u.shape[0]

    def body(u_ref, w_ref, ps_ref, y_ref):
        rows = lax.broadcasted_iota(jnp.int32, (S, POOL_GC), 0)
        for gi, w in enumerate(POOL_WINDOWS):
            sl = slice(POOL_GC * gi, POOL_GC * (gi + 1))
            diff, _ = _pool_diff(u_ref[:, sl], w, rows)
            mixed = _dot(diff.astype(MXU_DTYPE), w_ref[gi], NN)
            y_ref[:, sl] = (mixed * ps_ref[:, sl]).astype(MXU_DTYPE)

    seq = pl.BlockSpec((S, POOL_WIDTH), lambda b: (b, 0))
    return pl.pallas_call(
        body, name="pool_fwd", grid=(T // S,),
        in_specs=[seq, pl.BlockSpec(w_pool.shape, lambda b: (0, 0, 0)), pl.BlockSpec(pool_scale.shape, lambda b: (0, 0))],
        out_specs=seq, out_shape=jax.ShapeDtypeStruct((T, POOL_WIDTH), MXU_DTYPE),
        compiler_params=_params(("arbitrary",)),
    )(u, w_pool, pool_scale)


def _pool_bwd_call(u, dyp, w_pool, pool_scale, S, rider=None):
    T = u.shape[0]

    def body(u_ref, dy_ref, w_ref, ps_ref, du_ref, dw_ref, dps_ref):
        @pl.when(pl.program_id(0) == 0)
        def _():
            dw_ref[...] = jnp.zeros_like(dw_ref)
            dps_ref[...] = jnp.zeros_like(dps_ref)

        rows = lax.broadcasted_iota(jnp.int32, (S, POOL_GC), 0)
        for gi, w in enumerate(POOL_WINDOWS):
            sl = slice(POOL_GC * gi, POOL_GC * (gi + 1))
            diff, inv = _pool_diff(u_ref[:, sl], w, rows)
            diffb = diff.astype(MXU_DTYPE)
            wg = w_ref[gi]
            mixed = _dot(diffb, wg, NN)
            dy = dy_ref[:, sl]
            dps_ref[:, sl] += jnp.sum(dy * mixed, axis=0, keepdims=True)
            dmb = (dy * ps_ref[:, sl]).astype(MXU_DTYPE)
            dw_ref[gi] += _dot(diffb, dmb, TN)
            ddiff = _dot(dmb, wg, NT)
            du_ref[:, sl] = (_win_sum(ddiff * inv, w, rows, -1) - ddiff).astype(MXU_DTYPE)

    seq = pl.BlockSpec((S, POOL_WIDTH), lambda b: (b, 0))
    return _launch(
        body, [u, dyp, w_pool, pool_scale], name="pool_bwd", grid=(T // S,),
        in_specs=[seq, seq, pl.BlockSpec(w_pool.shape, lambda b: (0, 0, 0)), pl.BlockSpec(pool_scale.shape, lambda b: (0, 0))],
        out_specs=[seq, pl.BlockSpec(w_pool.shape, lambda b: (0, 0, 0)), pl.BlockSpec(pool_scale.shape, lambda b: (0, 0))],
        out_shape=[jax.ShapeDtypeStruct((T, POOL_WIDTH), MXU_DTYPE), jax.ShapeDtypeStruct(w_pool.shape, F32),
                   jax.ShapeDtypeStruct(pool_scale.shape, F32)],
        sem=("arbitrary",), rider=rider)


def _attn_consts():
    lane_g = lax.broadcasted_iota(jnp.int32, (BLOCK, 256), 1) >> 6
    rgrp = lax.broadcasted_iota(jnp.int32, (GROUP * BLOCK, 1), 0) >> 7
    rel = lax.broadcasted_iota(jnp.int32, (BLOCK, 256), 0) - lax.broadcasted_iota(jnp.int32, (BLOCK, 256), 1)

    def bias(off):
        ok = (rel + off >= 0) & (rel + off < BLOCK)
        return jnp.concatenate([jnp.where(ok, 0.0, NEG_INF)] * GROUP, axis=0)

    return lane_g, rgrp, bias(0), bias(BLOCK)


def _sink_rows(sink_ref, hk, rgrp):
    sv = jnp.zeros(rgrp.shape, F32)
    for g in range(GROUP):
        sv = jnp.where(rgrp == g, sink_ref[0, GROUP * hk + g], sv)
    return sv


def _stack_heads(xb, lane_g):
    return jnp.concatenate([jnp.where(lane_g == g, xb, jnp.zeros_like(xb)) for g in range(GROUP)], axis=0)


def _unstack_heads(xs, lane_g):
    out = jnp.where(lane_g == 0, xs[0:BLOCK], 0.0)
    for g in range(1, GROUP):
        out = out + jnp.where(lane_g == g, xs[BLOCK * g:BLOCK * (g + 1)], 0.0)
    return out


def _attn_probs(qs, kb, bias, sv):
    s = _dot(qs, kb, NT) + bias
    m = jnp.maximum(jnp.max(s, axis=1, keepdims=True), sv)
    e = jnp.exp(s - m)
    es = jnp.exp(sv - m)
    inv_l = 1.0 / (jnp.sum(e, axis=1, keepdims=True) + es)
    return e * inv_l, es * inv_l


def _attn_blocks(nb, blk, carry, per=1):
    carry = blk(0, 0, True, carry)
    per = per if (nb - 1) % per == 0 else 1

    def step(i, c):
        for k in range(per):
            n = 1 + per * i + k
            c = blk(pl.multiple_of(n * BLOCK, BLOCK), pl.multiple_of((n - 1) * BLOCK, BLOCK), False, c)
        return c

    return lax.fori_loop(0, (nb - 1) // per, step, carry)


def _attn_call(sinks, q, k4, v4, S, rider=None):
    T = q.shape[0]
    nb = S // BLOCK

    def body(sink_ref, q_ref, k_ref, v_ref, o_ref):
        lane_g, rgrp, bias_first, bias_later = _attn_consts()
        svs = [_sink_rows(sink_ref, hk, rgrp) for hk in range(N_KV_HEADS)]

        def blk(q0, k0, first, carry):
            for hk in range(N_KV_HEADS):
                cs = slice(256 * hk, 256 * (hk + 1))
                qs = _stack_heads(q_ref[pl.ds(q0, BLOCK), cs], lane_g)
                p, _ = _attn_probs(qs, k_ref[pl.ds(k0, 2 * BLOCK), cs], bias_first if first else bias_later, svs[hk])
                o = _dot(p.astype(MXU_DTYPE), v_ref[pl.ds(k0, 2 * BLOCK), cs], NN)
                o_ref[pl.ds(q0, BLOCK), cs] = _unstack_heads(o, lane_g).astype(MXU_DTYPE)
            return carry

        _attn_blocks(nb, blk, 0, per=3)

    seq = pl.BlockSpec((S, ATTN_WIDTH), lambda b: (b, 0))
    return _launch(
        body, [sinks, q, k4, v4], name="attn_fwd", grid=(T // S,),
        in_specs=[pl.BlockSpec(memory_space=pltpu.SMEM), seq, seq, seq],
        out_specs=[seq], out_shape=[jax.ShapeDtypeStruct((T, ATTN_WIDTH), MXU_DTYPE)],
        sem=("arbitrary",), rider=rider)


def _attn_bwd_call(sinks, q, k4, v4, do, rc, rsa, rsb, S, rider=None):
    T = q.shape[0]
    nb = S // BLOCK

    def body(sink_ref, q_ref, k_ref, v_ref, do_ref, c_ref, sa_ref, sb_ref,
             dq_ref, dk_ref, dv_ref, ds_ref, dk_acc, dv_acc):
        lane_g, rgrp, bias_first, bias_later = _attn_consts()
        svs = [_sink_rows(sink_ref, hk, rgrp) for hk in range(N_KV_HEADS)]
        lane1 = lax.broadcasted_iota(jnp.int32, (1, LANES), 1)
        dk_acc[...] = jnp.zeros_like(dk_acc)
        dv_acc[...] = jnp.zeros_like(dv_acc)

        def blk(q0, k0, first, dsink):
            rows = pl.ds(q0, BLOCK)
            c, sa, sb = c_ref[rows, :], sa_ref[rows, :], sb_ref[rows, :]
            for hk in range(N_KV_HEADS):
                cs = slice(256 * hk, 256 * (hk + 1))
                qs = _stack_heads(q_ref[rows, cs], lane_g)
                dos = _stack_heads(do_ref[rows, cs], lane_g)
                kb = k_ref[pl.ds(k0, 2 * BLOCK), cs]
                vb = v_ref[pl.ds(k0, 2 * BLOCK), cs]
                p, ps = _attn_probs(qs, kb, bias_first if first else bias_later, svs[hk])
                dp = _dot(dos, vb, NT)
                delta = jnp.sum(p * dp, axis=1, keepdims=True)
                dsb = (p * (dp - delta)).astype(MXU_DTYPE)
                dqb = _unstack_heads(_dot(dsb, kb, NN), lane_g) * SCALE
                dq_ref[rows, cs] = _rot(dqb, c, -sa, -sb).astype(MXU_DTYPE)
                dk_acc[pl.ds(k0, 2 * BLOCK), cs] += _dot(dsb, qs, TN)
                dv_acc[pl.ds(k0, 2 * BLOCK), cs] += _dot(p.astype(MXU_DTYPE), dos, TN)
                psd = ps * delta
                for g in range(GROUP):
                    val = -jnp.sum(psd[BLOCK * g:BLOCK * (g + 1)], axis=0, keepdims=True)
                    dsink = dsink + jnp.where(lane1 == GROUP * hk + g, val, 0.0)
            return dsink

        dsink = _attn_blocks(nb, blk, jnp.zeros((1, LANES), F32))
        dk_ref[...] = _rot(_fold_heads(dk_acc[...]), c_ref[...], -sa_ref[...], -sb_ref[...]).astype(MXU_DTYPE)
        dv_ref[...] = _fold_heads(dv_acc[...]).astype(MXU_DTYPE)
        ds_ref[...] = jnp.broadcast_to(dsink, ds_ref.shape)

    seq = pl.BlockSpec((S, ATTN_WIDTH), lambda b: (b, 0))
    kvs = pl.BlockSpec((S, KV_WIDTH), lambda b: (b, 0))
    tab = pl.BlockSpec((S, LANES), lambda b: (0, 0))
    nseq = T // S
    return _launch(
        body, [sinks, q, k4, v4, do, rc, rsa, rsb], name="attn_bwd", grid=(nseq,),
        in_specs=[pl.BlockSpec(memory_space=pltpu.SMEM), seq, seq, seq, seq, tab, tab, tab],
        out_specs=[seq, kvs, kvs, pl.BlockSpec((8, LANES), lambda b: (b, 0))],
        out_shape=[jax.ShapeDtypeStruct((T, ATTN_WIDTH), MXU_DTYPE), jax.ShapeDtypeStruct((T, KV_WIDTH), MXU_DTYPE),
                   jax.ShapeDtypeStruct((T, KV_WIDTH), MXU_DTYPE), jax.ShapeDtypeStruct((8 * nseq, LANES), F32)],
        scratch_shapes=[pltpu.VMEM((S, 512), F32), pltpu.VMEM((S, 512), F32)],
        sem=("arbitrary",), rider=rider)


def _branch_weights(wbp_ref, wba_ref, wbp_s, wba_s):
    @pl.when(pl.program_id(0) == 0)
    def _():
        for j in range(N_DEV):
            wbp_s[:, LANES * j:LANES * (j + 1)] = wbp_ref[j]
            wba_s[:, LANES * j:LANES * (j + 1)] = wba_ref[j]


def _mix_fwd_call(yp, ya, g, x, wbp, wba, wout, g2, g3, rider=None):
    T = x.shape[0]
    tm = _tile(T, 512)

    def body(yp_ref, ya_ref, g_ref, x_ref, wbp_ref, wba_ref, wout_ref, g2_ref, g3_ref,
             mix_ref, x1_ref, h2_ref, h2t_ref, wbp_s, wba_s):
        _branch_weights(wbp_ref, wba_ref, wbp_s, wba_s)
        bp = _dot(yp_ref[...], wbp_s[...], NN)
        ba = _dot(ya_ref[...], wba_s[...], NN)
        merged = g_ref[:, :D_MODEL].astype(F32) * bp + g_ref[:, D_MODEL:].astype(F32) * ba
        mix = _dot(merged.astype(MXU_DTYPE), wout_ref[...], NN)
        mix_ref[...] = mix
        x1 = x_ref[...] + (mix * _rms_r(mix)) * g2_ref[...]
        x1_ref[...] = x1
        h2 = (x1 * _rms_r(x1)) * g3_ref[...]
        h2_ref[...] = h2.astype(MXU_DTYPE)
        h2t_ref[...] = h2.T.astype(MXU_DTYPE)

    tok = lambda w: pl.BlockSpec((tm, w), lambda i: (i, 0))
    full = lambda a: pl.BlockSpec(a.shape, lambda i: (0,) * a.ndim)
    return _launch(
        body, [yp, ya, g, x, wbp, wba, wout, g2, g3], name="mix_fwd", grid=(T // tm,),
        in_specs=[tok(POOL_WIDTH), tok(ATTN_WIDTH), tok(GATE_WIDTH), tok(D_MODEL), full(wbp), full(wba), full(wout),
                  full(g2), full(g3)],
        out_specs=[tok(D_MODEL), tok(D_MODEL), tok(D_MODEL), pl.BlockSpec((D_MODEL, tm), lambda i: (0, i))],
        out_shape=[jax.ShapeDtypeStruct((T, D_MODEL), F32), jax.ShapeDtypeStruct((T, D_MODEL), F32),
                   jax.ShapeDtypeStruct((T, D_MODEL), MXU_DTYPE), jax.ShapeDtypeStruct((D_MODEL, T), MXU_DTYPE)],
        scratch_shapes=[pltpu.VMEM((POOL_WIDTH, D_MODEL), MXU_DTYPE), pltpu.VMEM((ATTN_WIDTH, D_MODEL), MXU_DTYPE)],
        sem=("arbitrary",), rider=rider)


def _mix_bwd_call(dx1, mix, yp, ya, g, wbp, wba, wout, g2, rider=None):
    T = dx1.shape[0]
    tm = _tile(T, 512)

    def body(dx1_ref, mix_ref, yp_ref, ya_ref, g_ref, wbp_ref, wba_ref, wout_ref, g2_ref,
             dyp_ref, do_ref, dgates_ref, dg2_ref, dbg_ref, gout_ref, gbp_ref, gba_ref,
             wbp_s, wba_s, acc_out, acc_bp, acc_ba, sem):
        _branch_weights(wbp_ref, wba_ref, wbp_s, wba_s)
        step = pl.program_id(0)

        @pl.when(step == 0)
        def _():
            dg2_ref[...] = jnp.zeros_like(dg2_ref)
            dbg_ref[...] = jnp.zeros_like(dbg_ref)
            acc_out[...] = jnp.zeros_like(acc_out)
            acc_bp[...] = jnp.zeros_like(acc_bp)
            acc_ba[...] = jnp.zeros_like(acc_ba)

        mix = mix_ref[...]
        dmix, dg2 = _rms_bwd(dx1_ref[...], mix, _rms_r(mix), g2_ref[...])
        dg2_ref[...] += jnp.sum(dg2, axis=0, keepdims=True)
        dmixb = dmix.astype(MXU_DTYPE)
        dmerged = _dot(dmixb, wout_ref[...], NT)
        yp, ya = yp_ref[...], ya_ref[...]
        bp = _dot(yp, wbp_s[...], NN)
        ba = _dot(ya, wba_s[...], NN)
        gp, ga = g_ref[:, :D_MODEL].astype(F32), g_ref[:, D_MODEL:].astype(F32)
        acc_out[...] += _dot((gp * bp + ga * ba).astype(MXU_DTYPE), dmixb, TN)
        dgp = dmerged * bp * (gp * (1.0 - gp))
        dga = dmerged * ba * (ga * (1.0 - ga))
        dbg_ref[:, :D_MODEL] += jnp.sum(dgp, axis=0, keepdims=True)
        dbg_ref[:, D_MODEL:] += jnp.sum(dga, axis=0, keepdims=True)
        dgates_ref[:, :D_MODEL] = dgp.astype(MXU_DTYPE)
        dgates_ref[:, D_MODEL:] = dga.astype(MXU_DTYPE)
        dbp = (dmerged * gp).astype(MXU_DTYPE)
        dba = (dmerged * ga).astype(MXU_DTYPE)
        acc_bp[...] += _dot(yp, dbp, TN)
        acc_ba[...] += _dot(ya, dba, TN)
        dyp_ref[...] = _dot(dbp, wbp_s[...], NT)
        do_ref[...] = _dot(dba, wba_s[...], NT).astype(MXU_DTYPE)

        @pl.when(step == pl.num_programs(0) - 1)
        def _():
            copies = [pltpu.make_async_copy(acc_out, gout_ref, sem.at[0])]
            for j in range(N_DEV):
                cols = slice(LANES * j, LANES * (j + 1))
                copies.append(pltpu.make_async_copy(acc_bp.at[:, cols], gbp_ref.at[j], sem.at[1 + j]))
                copies.append(pltpu.make_async_copy(acc_ba.at[:, cols], gba_ref.at[j], sem.at[1 + N_DEV + j]))
            for cp in copies:
                cp.start()
            for cp in copies:
                cp.wait()

    tok = lambda w: pl.BlockSpec((tm, w), lambda i: (i, 0))
    full = lambda a: pl.BlockSpec(a.shape, lambda i: (0,) * a.ndim)
    acc = lambda w: pl.BlockSpec((1, w), lambda i: (0, 0))
    hbm = pl.BlockSpec(memory_space=pl.ANY)
    sd = jax.ShapeDtypeStruct
    return _launch(
        body, [dx1, mix, yp, ya, g, wbp, wba, wout, g2], name="mix_bwd", grid=(T // tm,),
        in_specs=[tok(D_MODEL), tok(D_MODEL), tok(POOL_WIDTH), tok(ATTN_WIDTH), tok(GATE_WIDTH), full(wbp), full(wba),
                  full(wout), full(g2)],
        out_specs=[tok(POOL_WIDTH), tok(ATTN_WIDTH), tok(GATE_WIDTH), acc(D_MODEL), acc(GATE_WIDTH), hbm, hbm, hbm],
        out_shape=[sd((T, POOL_WIDTH), F32), sd((T, ATTN_WIDTH), MXU_DTYPE), sd((T, GATE_WIDTH), MXU_DTYPE),
                   sd((1, D_MODEL), F32), sd((1, GATE_WIDTH), F32), sd((D_MODEL, D_MODEL), F32),
                   sd((N_DEV, POOL_WIDTH, LANES), F32), sd((N_DEV, ATTN_WIDTH, LANES), F32)],
        scratch_shapes=[pltpu.VMEM((POOL_WIDTH, D_MODEL), MXU_DTYPE), pltpu.VMEM((ATTN_WIDTH, D_MODEL), MXU_DTYPE),
                        pltpu.VMEM((D_MODEL, D_MODEL), F32), pltpu.VMEM((POOL_WIDTH, D_MODEL), F32),
                        pltpu.VMEM((ATTN_WIDTH, D_MODEL), F32), pltpu.SemaphoreType.DMA((1 + 2 * N_DEV,))],
        sem=("arbitrary",), rider=rider)


def _mlp_call(x1, h2, target, wup, wdown, g3, g4):
    T = x1.shape[0]
    tm = _tile(T, 256)
    fc = D_FF // N_DEV

    def body(x1_ref, h2_ref, t_ref, wup_ref, wdown_ref, g3_ref, g4_ref,
             act_ref, da_ref, dff_ref, dx1_ref, dg3_ref, dg4_ref, loss_ref, rl_s):
        @pl.when(pl.program_id(0) == 0)
        def _():
            dg3_ref[...] = jnp.zeros_like(dg3_ref)
            dg4_ref[...] = jnp.zeros_like(dg4_ref)
            loss_ref[...] = jnp.zeros_like(loss_ref)

        h2 = h2_ref[...]
        ff = jnp.zeros((tm, D_MODEL), F32)
        for j in range(N_DEV):
            sl = slice(fc * j, fc * (j + 1))
            rl = jnp.maximum(_dot(h2, wup_ref[j], NN), 0.0)
            rl_s[:, sl] = rl
            act = rl * rl
            act_ref[sl, :] = act.T.astype(MXU_DTYPE)
            ff = ff + _dot(act.astype(MXU_DTYPE), wdown_ref[j], NN)
        x1 = x1_ref[...]
        r4 = _rms_r(ff)
        err = x1 + (ff * r4) * g4_ref[...] - t_ref[...]
        loss_ref[...] += jnp.sum(err * err, axis=0, keepdims=True)
        dy = err * (1.0 / D_MODEL)
        dff, dg4 = _rms_bwd(dy, ff, r4, g4_ref[...])
        dg4_ref[...] += jnp.sum(dg4, axis=0, keepdims=True)
        dffb = dff.astype(MXU_DTYPE)
        dff_ref[...] = dffb
        dh2 = jnp.zeros((tm, D_MODEL), F32)
        for j in range(N_DEV):
            sl = slice(fc * j, fc * (j + 1))
            dab = (_dot(dffb, wdown_ref[j], NT) * (2.0 * rl_s[:, sl])).astype(MXU_DTYPE)
            da_ref[:, sl] = dab
            dh2 = dh2 + _dot(dab, wup_ref[j], NT)
        dx1, dg3 = _rms_bwd(dh2, x1, _rms_r(x1), g3_ref[...])
        dg3_ref[...] += jnp.sum(dg3, axis=0, keepdims=True)
        dx1_ref[...] = dy + dx1

    tok = lambda w: pl.BlockSpec((tm, w), lambda i: (i, 0))
    full = lambda a: pl.BlockSpec(a.shape, lambda i: (0,) * a.ndim, pipeline_mode=pl.Buffered(1))
    vec = pl.BlockSpec((1, D_MODEL), lambda i: (0, 0))
    sd = jax.ShapeDtypeStruct
    return pl.pallas_call(
        body, name="mlp_fwd_bwd", grid=(T // tm,),
        in_specs=[tok(D_MODEL), tok(D_MODEL), tok(D_MODEL), full(wup), full(wdown), vec, vec],
        out_specs=[pl.BlockSpec((D_FF, tm), lambda i: (0, i)), tok(D_FF), tok(D_MODEL), tok(D_MODEL), vec, vec, vec],
        out_shape=[sd((D_FF, T), MXU_DTYPE), sd((T, D_FF), MXU_DTYPE), sd((T, D_MODEL), MXU_DTYPE),
                   sd((T, D_MODEL), F32), sd((1, D_MODEL), F32), sd((1, D_MODEL), F32), sd((1, D_MODEL), F32)],
        scratch_shapes=[pltpu.VMEM((tm, D_FF), F32)],
        compiler_params=_params(("arbitrary",)),
    )(x1, h2, target, wup, wdown, g3, g4)


def _inproj_bwd_call(du, dq, dk, dv, dgates, dx1, x, win_t, g1, rider=None):
    T = x.shape[0]
    tm = _tile(T, 512)

    def body(du_ref, dq_ref, dk_ref, dv_ref, dgt_ref, dx1_ref, x_ref, w_ref, g1_ref, gx_ref, dg1_ref, db_ref):
        @pl.when(pl.program_id(0) == 0)
        def _():
            dg1_ref[...] = jnp.zeros_like(dg1_ref)
            db_ref[...] = jnp.zeros_like(db_ref)

        dh = jnp.zeros((tm, D_MODEL), F32)
        for ref, lo, hi in ((du_ref, 0, C_Q), (dq_ref, C_Q, C_K), (dk_ref, C_K, C_V), (dv_ref, C_V, C_G),
                            (dgt_ref, C_G, IN_WIDTH)):
            piece = ref[...]
            dh = dh + _dot(piece, w_ref[lo:hi, :], NN)
            if hi <= C_G:
                db_ref[:, lo:hi] += jnp.sum(piece.astype(F32), axis=0, keepdims=True)
        xv = x_ref[...]
        dx, dg1 = _rms_bwd(dh, xv, _rms_r(xv), g1_ref[...])
        dg1_ref[...] += jnp.sum(dg1, axis=0, keepdims=True)
        gx_ref[...] = dx1_ref[...] + dx

    tok = lambda w: pl.BlockSpec((tm, w), lambda i: (i, 0))
    full = lambda a: pl.BlockSpec(a.shape, lambda i: (0,) * a.ndim)
    sd = jax.ShapeDtypeStruct
    return _launch(
        body, [du, dq, dk, dv, dgates, dx1, x, win_t, g1], name="inproj_bwd", grid=(T // tm,),
        in_specs=[tok(POOL_WIDTH), tok(ATTN_WIDTH), tok(KV_WIDTH), tok(KV_WIDTH), tok(GATE_WIDTH), tok(D_MODEL),
                  tok(D_MODEL), full(win_t), full(g1)],
        out_specs=[tok(D_MODEL), pl.BlockSpec((1, D_MODEL), lambda i: (0, 0)), pl.BlockSpec((1, C_G), lambda i: (0, 0))],
        out_shape=[sd((T, D_MODEL), F32), sd((1, D_MODEL), F32), sd((1, C_G), F32)],
        sem=("arbitrary",), rider=rider)


WGRAD_TOKENS = 1024


def _wgrad_rows_call(at, b, name, rider=None):
    K, T = at.shape
    N = b.shape[1]
    tm = _tile(T, WGRAD_TOKENS)
    kb = min(K, 1024)
    per = kb // (K // N_DEV)

    def body(a_ref, b_ref, o_ref):
        @pl.when(pl.program_id(1) == 0)
        def _():
            o_ref[...] = jnp.zeros_like(o_ref)

        d = _dot(a_ref[...], b_ref[...], NN)
        rs = kb // per
        for j in range(per):
            o_ref[j] += d[rs * j:rs * (j + 1)]

    return _launch(
        body, [at, b], name=name, grid=(K // kb, T // tm),
        in_specs=[pl.BlockSpec((kb, tm), lambda i, t: (i, t)), pl.BlockSpec((tm, N), lambda i, t: (t, 0))],
        out_specs=[pl.BlockSpec((per, K // N_DEV, N), lambda i, t: (i, 0, 0))],
        out_shape=[jax.ShapeDtypeStruct((N_DEV, K // N_DEV, N), F32)],
        sem=("arbitrary", "arbitrary"), rider=rider)


def _wgrad_cols_call(at, b, name, rider=None):
    K, T = at.shape
    N = b.shape[1]
    tm = _tile(T, WGRAD_TOKENS)
    nb = min(N, 1024)
    per = nb // (N // N_DEV)

    def body(a_ref, b_ref, o_ref):
        @pl.when(pl.program_id(1) == 0)
        def _():
            o_ref[...] = jnp.zeros_like(o_ref)

        d = _dot(a_ref[...], b_ref[...], NN)
        cs = nb // per
        for j in range(per):
            o_ref[j] += d[:, cs * j:cs * (j + 1)]

    return _launch(
        body, [at, b], name=name, grid=(N // nb, T // tm),
        in_specs=[pl.BlockSpec((K, tm), lambda i, t: (0, t)), pl.BlockSpec((tm, nb), lambda i, t: (t, i))],
        out_specs=[pl.BlockSpec((per, K, N // N_DEV), lambda i, t: (i, 0, 0))],
        out_shape=[jax.ShapeDtypeStruct((N_DEV, K, N // N_DEV), F32)],
        sem=("arbitrary", "arbitrary"), rider=rider)


def _wgrad_in_call(du, dq, dk, dv, dgates, h, rider=None):
    T = h.shape[0]
    tm = _tile(T, WGRAD_TOKENS)
    rows = IN_WIDTH // N_DEV

    def body(du_ref, dq_ref, dk_ref, dv_ref, dgt_ref, h_ref, o_ref, acc, sem):
        t = pl.program_id(0)

        @pl.when(t == 0)
        def _():
            acc[...] = jnp.zeros_like(acc)

        hv = h_ref[...]
        for ref, lo, hi in ((du_ref, 0, C_Q), (dq_ref, C_Q, C_K), (dk_ref, C_K, C_V), (dv_ref, C_V, C_G),
                            (dgt_ref, C_G, IN_WIDTH)):
            acc[lo:hi, :] += _dot(ref[...], hv, TN)

        @pl.when(t == pl.num_programs(0) - 1)
        def _():
            copies = [pltpu.make_async_copy(acc.at[pl.ds(rows * j, rows), :], o_ref.at[j], sem.at[j])
                      for j in range(N_DEV)]
            for cp in copies:
                cp.start()
            for cp in copies:
                cp.wait()

    tok = lambda w: pl.BlockSpec((tm, w), lambda t: (t, 0))
    return _launch(
        body, [du, dq, dk, dv, dgates, h], name="wgrad_in", grid=(T // tm,),
        in_specs=[tok(POOL_WIDTH), tok(ATTN_WIDTH), tok(KV_WIDTH), tok(KV_WIDTH), tok(GATE_WIDTH), tok(D_MODEL)],
        out_specs=[pl.BlockSpec(memory_space=pl.ANY)],
        out_shape=[jax.ShapeDtypeStruct((N_DEV, rows, D_MODEL), F32)],
        scratch_shapes=[pltpu.VMEM((IN_WIDTH, D_MODEL), F32), pltpu.SemaphoreType.DMA((N_DEV,))],
        sem=("arbitrary",), rider=rider)


def _coords():
    return lax.axis_index("x"), lax.axis_index("y"), lax.axis_index("c")


def _allgather_call(shards):
    n = len(shards)

    def body(*refs):
        ins, outs = refs[:n], refs[n:2 * n]
        send_sems, recv_sems, local_sems = refs[2 * n:]
        x, y, c = _coords()
        me, sibling = (x, y, c), (x, y, 1 - c)
        chips = [(1 - x, y), (x, 1 - y), (1 - x, 1 - y)]

        def slot(p):
            return 4 * p[0] + 2 * p[1] + p[2]

        def copy(t, k, block, to, src=None):
            dst = outs[t].at[slot(block)]
            return pltpu.make_async_remote_copy(
                src_ref=dst if src is None else src, dst_ref=dst, send_sem=send_sems.at[t, k],
                recv_sem=recv_sems.at[t, k], device_id=to, device_id_type=MESH)

        mine = [pltpu.make_async_copy(ins[t], outs[t].at[slot(me)], local_sems.at[t]) for t in range(n)]
        for cp in mine:
            cp.start()
        first = []
        for t in range(n):
            first.append(copy(t, 0, me, sibling, src=ins[t]))
            first += [copy(t, 1 + j, me, (*chip, c), src=ins[t]) for j, chip in enumerate(chips)]
        for cp in first:
            cp.start()
        passed = []
        for t in range(n):
            for j, chip in enumerate(chips):
                copy(t, 1 + j, (*chip, c), me).wait_recv()
                fwd = copy(t, 4 + j, (*chip, c), sibling)
                fwd.start()
                passed.append(fwd)
        for t in range(n):
            copy(t, 0, sibling, me).wait_recv()
            for j, chip in enumerate(chips):
                copy(t, 4 + j, (*chip, 1 - c), me).wait_recv()
        for cp in first + passed:
            cp.wait_send()
        for cp in mine:
            cp.wait()

    hbm = pl.BlockSpec(memory_space=pl.ANY)
    return pl.pallas_call(
        body, name="allgather_weights",
        in_specs=[hbm] * n, out_specs=[hbm] * n,
        out_shape=[jax.ShapeDtypeStruct((N_DEV,) + s.shape, s.dtype) for s in shards],
        scratch_shapes=[pltpu.SemaphoreType.DMA((n, 7)), pltpu.SemaphoreType.DMA((n, 7)), pltpu.SemaphoreType.DMA((n,))],
    )(*shards)


def _slot(p):
    return 4 * p[0] + 2 * p[1] + p[2]


def _rows(ref, span):
    return ref if span is None else ref.at[pl.ds(span[0], span[1])]


ALL = "all"
LOCAL = "local"


def _rows(ref, span):
    return ref if span == ALL else ref.at[pl.ds(span[0], span[1])]


def _rider_ag(items):
    ins, out_shape, aliases, where = [], [], {}, []
    n_remote = n_local = 0
    for t, (shard, buf, snd, fwd) in enumerate(items):
        i_shard = i_buf = None
        if snd is not None:
            i_shard = len(ins)
            ins.append(shard)
        if buf is not None:
            i_buf = len(ins)
            ins.append(buf)
            aliases[i_buf] = t
            out_shape.append(jax.ShapeDtypeStruct(buf.shape, buf.dtype))
        else:
            assert fwd is None and snd is not None
            out_shape.append(jax.ShapeDtypeStruct((N_DEV,) + shard.shape, shard.dtype))
        where.append((i_shard, i_buf, n_remote, n_local))
        n_remote += (4 if snd not in (None, LOCAL) else 0) + (3 if fwd is not None else 0)
        n_local += 1 if snd is not None else 0

    def plan(rins, routs, send, recv, loc, r0, l0):
        x, y, c = _coords()
        peers = [(x, y, 1 - c), (1 - x, y, c), (x, 1 - y, c), (1 - x, 1 - y, c)]
        remote, local = [], []
        for t, (shard, buf, snd, fwd) in enumerate(items):
            i_shard, i_buf, k, l = where[t]
            k, l = r0 + k, l0 + l
            if snd is not None:
                span = ALL if snd == LOCAL else snd
                src, dst = _rows(rins[i_shard], span), _rows(routs[t].at[_slot((x, y, c))], span)
                local.append(pltpu.make_async_copy(src, dst, loc.at[l]))
                for peer in (peers if snd != LOCAL else []):
                    remote.append(pltpu.make_async_remote_copy(
                        src_ref=src, dst_ref=dst, send_sem=send.at[k], recv_sem=recv.at[k],
                        device_id=peer, device_id_type=MESH))
                    k += 1
            if fwd is not None:
                for px, py, pc in peers[1:]:
                    s = _slot((px, py, pc))
                    remote.append(pltpu.make_async_remote_copy(
                        src_ref=_rows(rins[i_buf].at[s], fwd), dst_ref=_rows(routs[t].at[s], fwd),
                        send_sem=send.at[k], recv_sem=recv.at[k], device_id=peers[0], device_id_type=MESH))
                    k += 1
        return remote, local

    return _Rider(ins, out_shape, n_remote, n_local, plan, aliases)


def _rider_ag_remote(shards):
    n = len(shards)

    def plan(ins, outs, send, recv, loc, r0, l0):
        x, y, c = _coords()
        remote = []
        for t in range(n):
            dst = outs[t].at[_slot((x, y, c))]
            for k, peer in enumerate([(x, y, 1 - c), (1 - x, y, c), (x, 1 - y, c), (1 - x, 1 - y, c)]):
                remote.append(pltpu.make_async_remote_copy(
                    src_ref=ins[t], dst_ref=dst, send_sem=send.at[r0 + 4 * t + k], recv_sem=recv.at[r0 + 4 * t + k],
                    device_id=peer, device_id_type=MESH))
        return remote, []

    return _Rider(shards, [jax.ShapeDtypeStruct((N_DEV,) + s.shape, s.dtype) for s in shards], 4 * n, 0, plan)


def _rider_rs_sibling(grads):
    n = len(grads)

    def plan(ins, outs, send, recv, loc, r0, l0):
        x, y, c = _coords()
        remote = []
        for t in range(n):
            for q in range(4):
                remote.append(pltpu.make_async_remote_copy(
                    src_ref=ins[t].at[q, 1 - c], dst_ref=outs[t].at[q], send_sem=send.at[r0 + 4 * t + q],
                    recv_sem=recv.at[r0 + 4 * t + q], device_id=(x, y, 1 - c), device_id_type=MESH))
        return remote, []

    return _Rider(grads, [jax.ShapeDtypeStruct((4,) + g.shape[2:], g.dtype) for g in grads], 4 * n, 0, plan)


def _rider_rs_chips(sums, rows=None, into=None):
    n = len(sums)
    rows = rows or [ALL] * n

    def plan(ins, outs, send, recv, loc, r0, l0):
        x, y, c = _coords()
        remote = []
        for t in range(n):
            for r, (px, py) in enumerate([(1 - x, y), (x, 1 - y), (1 - x, 1 - y)]):
                remote.append(pltpu.make_async_remote_copy(
                    src_ref=_rows(ins[t].at[2 * px + py], rows[t]), dst_ref=_rows(outs[t].at[r], rows[t]),
                    send_sem=send.at[r0 + 3 * t + r], recv_sem=recv.at[r0 + 3 * t + r],
                    device_id=(px, py, c), device_id_type=MESH))
        return remote, []

    out_shape = [jax.ShapeDtypeStruct((3,) + s.shape[1:], s.dtype) for s in sums]
    if into is None:
        return _Rider(sums, out_shape, 3 * n, 0, plan)
    return _Rider(list(sums) + list(into), out_shape, 3 * n, 0, plan, aliases={n + t: t for t in range(n)})


def _rider_gather_remote(parts):
    n = len(parts)

    def plan(ins, outs, send, recv, loc, r0, l0):
        x, y, c = _coords()
        me = _slot((x, y, c))
        remote = []
        for t in range(n):
            for k in range(1, N_DEV):
                peer = (x ^ ((k >> 2) & 1), y ^ ((k >> 1) & 1), c ^ (k & 1))
                remote.append(pltpu.make_async_remote_copy(
                    src_ref=ins[t], dst_ref=outs[t].at[me], send_sem=send.at[r0 + 7 * t + k - 1],
                    recv_sem=recv.at[r0 + 7 * t + k - 1], device_id=peer, device_id_type=MESH))
        return remote, []

    return _Rider(parts, [jax.ShapeDtypeStruct((N_DEV,) + p.shape, p.dtype) for p in parts], 7 * n, 0, plan)


def _chip_sum_call(idx, grads, recvd, out_dtypes, name):
    n = len(grads)

    def body(i_ref, *refs):
        for t in range(n):
            refs[2 * n + t][0] = (refs[t][0, 0] + refs[n + t][0]).astype(out_dtypes[t])

    def chip(k, s):
        return jnp.where(k >= s[0], k + 1, k)

    in_specs = [pl.BlockSpec((1, 1) + g.shape[2:], lambda k, s: (chip(k, s), s[1], 0, 0)) for g in grads]
    in_specs += [pl.BlockSpec((1,) + r.shape[1:], lambda k, s: (chip(k, s), 0, 0)) for r in recvd]
    return pl.pallas_call(
        body, name=name,
        grid_spec=pltpu.PrefetchScalarGridSpec(
            num_scalar_prefetch=1, grid=(3,), in_specs=in_specs,
            out_specs=[pl.BlockSpec((1,) + r.shape[1:], lambda k, s: (chip(k, s), 0, 0)) for r in recvd]),
        out_shape=[jax.ShapeDtypeStruct(r.shape, dt) for r, dt in zip(recvd, out_dtypes)],
        compiler_params=_params(("arbitrary",)),
    )(idx, *grads, *recvd)


def _final_sum_call(idx, grads, recvd1, recvd2):
    n = len(grads)
    nsteps = 2

    def body(i_ref, *refs):
        for t in range(n):
            g, r1, r2, o = refs[t], refs[n + t], refs[2 * n + t], refs[3 * n + t]
            s = g[0, 0] + r1[0]
            for r in range(3):
                s = s + r2[r].astype(F32)
            o[...] = s

    def rows(a):
        r = a.shape[-2]
        return r // nsteps if (r // nsteps) % 16 == 0 else r

    def step(a):
        return (lambda i: i) if rows(a) != a.shape[-2] else (lambda i: 0)

    in_specs = [pl.BlockSpec((1, 1, rows(g), g.shape[3]), lambda i, s, st=step(g): (s[0], s[1], st(i), 0)) for g in grads]
    in_specs += [pl.BlockSpec((1, rows(r), r.shape[2]), lambda i, s, st=step(r): (s[0], st(i), 0)) for r in recvd1]
    in_specs += [pl.BlockSpec((3, rows(r), r.shape[2]), lambda i, s, st=step(r): (0, st(i), 0)) for r in recvd2]
    return pl.pallas_call(
        body, name="rs_final_sum",
        grid_spec=pltpu.PrefetchScalarGridSpec(
            num_scalar_prefetch=1, grid=(nsteps,), in_specs=in_specs,
            out_specs=[pl.BlockSpec((rows(r), r.shape[2]), lambda i, s, st=step(r): (st(i), 0)) for r in recvd2]),
        out_shape=[jax.ShapeDtypeStruct(r.shape[1:], F32) for r in recvd2],
        compiler_params=_params(("arbitrary",)),
    )(idx, *grads, *recvd1, *recvd2)


def _sum8_call(parts):
    def body(p_ref, o_ref):
        s = p_ref[0]
        for j in range(1, N_DEV):
            s = s + p_ref[j]
        o_ref[...] = s

    return pl.pallas_call(body, name="sum_small_partials",
                          out_shape=jax.ShapeDtypeStruct(parts.shape[1:], parts.dtype))(parts)


def _adamw(w, g, m, v):
    m = ADAM_B1 * m + (1.0 - ADAM_B1) * g
    v = ADAM_B2 * v + (1.0 - ADAM_B2) * (g * g)
    m_hat = m / (1.0 - ADAM_B1 ** ADAM_STEP)
    v_hat = v / (1.0 - ADAM_B2 ** ADAM_STEP)
    delta = -ADAM_LR * (m_hat / (jnp.sqrt(v_hat) + ADAM_EPS) + ADAM_WD * w)
    return delta, m, v


def _adamw_call(ws, gs, ms, vs, nsteps, name):
    n = len(ws)

    def body(*refs):
        for t in range(n):
            w, g, m, v = (refs[k * n + t][...] for k in range(4))
            d, m2, v2 = _adamw(w, g, m, v)
            refs[4 * n + t][...] = d
            refs[5 * n + t][...] = m2
            refs[6 * n + t][...] = v2

    def spec(a):
        assert a.shape[0] % nsteps == 0 and (nsteps == 1 or (a.shape[0] // nsteps) % 8 == 0), a.shape
        return pl.BlockSpec((a.shape[0] // nsteps, a.shape[1]), lambda i: (i, 0))

    specs = [spec(a) for a in ws]
    outs = pl.pallas_call(
        body, name=name, grid=(nsteps,),
        in_specs=specs * 4, out_specs=specs * 3,
        out_shape=[jax.ShapeDtypeStruct(a.shape, F32) for a in ws] * 3,
        compiler_params=_params(("arbitrary",)),
    )(*ws, *gs, *ms, *vs)
    return outs[:n], outs[n:2 * n], outs[2 * n:]


def _adamw_rs_call(idx, after, gws, r1s, r2s, ws, ms, vs, nsteps, name):
    n = len(ws)

    def body(i_ref, after_ref, *refs):
        for t in range(n):
            gw, r1, r2, w, m, v = (refs[k * n + t] for k in range(6))
            g = gw[0, 0] + r1[0]
            for r in range(3):
                g = g + r2[r].astype(F32)
            d, m2, v2 = _adamw(w[...], g, m[...], v[...])
            refs[6 * n + t][...] = g
            refs[7 * n + t][...] = d
            refs[8 * n + t][...] = m2
            refs[9 * n + t][...] = v2

    def rb(a):
        r = a.shape[0] // nsteps
        assert a.shape[0] % nsteps == 0 and r % 16 == 0, a.shape
        return r

    in_specs = [pl.BlockSpec((1, 1, rb(w), w.shape[1]), lambda i, s: (s[0], s[1], i, 0)) for w in ws]
    in_specs += [pl.BlockSpec((1, rb(w), w.shape[1]), lambda i, s: (s[0], i, 0)) for w in ws]
    in_specs += [pl.BlockSpec((3, rb(w), w.shape[1]), lambda i, s: (0, i, 0)) for w in ws]
    plain = [pl.BlockSpec((rb(w), w.shape[1]), lambda i, s: (i, 0)) for w in ws]
    outs = pl.pallas_call(
        body, name=name,
        grid_spec=pltpu.PrefetchScalarGridSpec(
            num_scalar_prefetch=1, grid=(nsteps,),
            in_specs=[pl.BlockSpec(memory_space=pl.ANY)] + in_specs + plain * 3, out_specs=plain * 4),
        out_shape=[jax.ShapeDtypeStruct(w.shape, F32) for w in ws] * 4,
        compiler_params=_params(("arbitrary",)),
    )(idx, after, *gws, *r1s, *r2s, *ws, *ms, *vs)
    return outs[:n], outs[n:2 * n], outs[2 * n:3 * n], outs[3 * n:]


def _rows128(a, pad_rows):
    flat = a.reshape(-1).astype(F32)
    flat = jnp.pad(flat, (0, pad_rows * LANES - flat.shape[0]))
    return flat.reshape(pad_rows, LANES)


_SMALL_A = (("w_pool", 512), ("pool_scale", 8), ("attn_sinks", 8), ("g_mix_post", 8), ("g_mlp_pre", 8),
            ("g_mlp_post", 8), ("loss", 8), ("b_in_gates", 16))
_SMALL_A_ROWS = 640
_SMALL_B = (("g_mix_pre", 8), ("b_in_head", 16))


def _pack(parts, layout, total_rows):
    rows = [_rows128(parts[k], r) for k, r in layout]
    pad = total_rows - sum(r for _, r in layout)
    if pad:
        rows.append(jnp.zeros((pad, LANES), F32))
    return jnp.concatenate(rows, axis=0)


def _unpack(buf, layout, sizes):
    out, off = {}, 0
    for k, r in layout:
        out[k] = buf[off:off + r].reshape(-1)[:sizes[k]]
        off += r
    return out


def kernel(x, g_mix_pre, w_in, b_in, w_pool, pool_scale, attn_sinks, w_branch_pool, w_branch_attn, w_out, g_mix_post, g_mlp_pre, w_up, w_down, g_mlp_post, loss_target, m_g_mix_pre, m_w_in, m_b_in, m_w_pool, m_pool_scale, m_attn_sinks, m_w_branch_pool, m_w_branch_attn, m_w_out, m_g_mix_post, m_g_mlp_pre, m_w_up, m_w_down, m_g_mlp_post, v_g_mix_pre, v_w_in, v_b_in, v_w_pool, v_pool_scale, v_attn_sinks, v_w_branch_pool, v_w_branch_attn, v_w_out, v_g_mix_post, v_g_mlp_pre, v_w_up, v_w_down, v_g_mlp_post):
    B, S, _ = x.shape
    T = B * S
    xt = x.reshape(T, D_MODEL)
    tgt = loss_target.reshape(T, D_MODEL)
    cx, cy, cc = _coords()

    cidx = jnp.stack([2 * cx + cy, cc]).astype(jnp.int32)
    by_chip = lambda gr: gr.reshape((4, 2) + gr.shape[1:])
    bf = lambda w: w[0].astype(MXU_DTYPE)

    (win_s,) = _allgather_call([w_in[0].T.astype(MXU_DTYPE)])
    win_t = win_s.reshape(IN_WIDTH, D_MODEL)
    wpool_b = bf(w_pool)
    rc, rsa, rsb = _rot_tables(S)

    wbp_l, wba_l, wout_l, wup_l, wdown_l = bf(w_branch_pool), bf(w_branch_attn), bf(w_out), bf(w_up), bf(w_down)
    (c_br, c_up, c_dn), tok = _copies_start(
        [_rider_ag_remote([wbp_l, wba_l, wout_l]), _rider_ag_remote([wup_l]), _rider_ag_remote([wdown_l])],
        "allgather_start")
    (h, u, q, k4, v4, g), _ = _inproj_call(xt, g_mix_pre, win_t, b_in, rc, rsa, rsb, S, rider=_after(tok))
    yp = _pool_call(u, wpool_b, pool_scale, S)
    wbp_1, wba_1, wout_1 = _copies_wait([c_br], yp, "allgather_wait_branch")
    (ya,), (wbp_s, wba_s, wout_s) = _attn_call(
        attn_sinks, q, k4, v4, S,
        rider=_rider_ag([(wbp_l, wbp_1, LOCAL, ALL), (wba_l, wba_1, LOCAL, ALL), (wout_l, wout_1, LOCAL, ALL)]))
    wout_f = wout_s.reshape(D_MODEL, D_MODEL)
    (wup_1,) = _copies_wait([c_up], ya, "allgather_wait_up")
    (mix, x1, h2, h2_t), (wup_s,) = _mix_fwd_call(
        yp, ya, g, xt, wbp_s, wba_s, wout_f, g_mix_post, g_mlp_pre, rider=_rider_ag([(wup_l, wup_1, LOCAL, ALL)]))
    (wdown_1,) = _copies_wait([c_dn], h2, "allgather_wait_down")
    (wdown_s,) = _comm_call(_rider_ag([(wdown_l, wdown_1, LOCAL, ALL)]), "allgather_pass_down")

    act_t, da, dff, dx1, dg3, dg4, lossvec = _mlp_call(x1, h2, tgt, wup_s, wdown_s, g_mlp_pre, g_mlp_post)
    gw_down = by_chip(_wgrad_rows_call(act_t, dff, "wgrad_down")[0])
    (gw_up,), (r1_down,) = _wgrad_cols_call(h2_t, da, "wgrad_up", rider=_rider_rs_sibling([gw_down]))
    gw_up = by_chip(gw_up)
    (s_down,) = _chip_sum_call(cidx, [gw_down], [r1_down], [MXU_DTYPE], "rs_chip_sum_down")
    (c_down,), tok = _copies_start([_rider_rs_chips([s_down])], "rs_chips_start_down")
    (dyp, do, dgates, dg2, dbg, gw_out, gw_bp, gw_ba), (r1_up,) = _mix_bwd_call(
        dx1, mix, yp, ya, g, wbp_s, wba_s, wout_f, g_mix_post, rider=_after(tok, _rider_rs_sibling([gw_up])))
    gw_out = by_chip(gw_out.reshape(N_DEV, D_MODEL // N_DEV, D_MODEL))
    gw_bp, gw_ba = by_chip(gw_bp), by_chip(gw_ba)
    (s_up,) = _chip_sum_call(cidx, [gw_up], [r1_up], [MXU_DTYPE], "rs_chip_sum_up")
    (c_up,), tok = _copies_start([_rider_rs_chips([s_up])], "rs_chips_start_up")
    (dq, dk, dv, dsink), (r1_out, r1_bp, r1_ba) = _attn_bwd_call(
        attn_sinks, q, k4, v4, do, rc, rsa, rsb, S, rider=_after(tok, _rider_rs_sibling([gw_out, gw_bp, gw_ba])))
    s_obb = _chip_sum_call(cidx, [gw_out, gw_bp, gw_ba], [r1_out, r1_bp, r1_ba], [MXU_DTYPE] * 3, "rs_chip_sum_branch")
    (c_obb,), tok = _copies_start([_rider_rs_chips(s_obb)], "rs_chips_start_branch")
    (du, dwp, dps), _ = _pool_bwd_call(u, dyp, wpool_b, pool_scale, S, rider=_after(tok))
    (gw_in,) = _wgrad_in_call(du, dq, dk, dv, dgates, h)
    gw_in = by_chip(gw_in)

    small_a = {"w_pool": dwp, "pool_scale": dps,
               "attn_sinks": jnp.sum(dsink.reshape(B, 8, LANES)[:, 0, :N_Q_HEADS], axis=0), "g_mix_post": dg2,
               "g_mlp_pre": dg3, "g_mlp_post": dg4, "loss": lossvec, "b_in_gates": dbg}
    gw_sa = by_chip(_pack(small_a, _SMALL_A, _SMALL_A_ROWS).reshape(N_DEV, _SMALL_A_ROWS // N_DEV, LANES))
    r1_in, r1_sa = _comm_call(_rider_rs_sibling([gw_in, gw_sa]), "rs_sibling_in")
    s_in, s_sa = _chip_sum_call(cidx, [gw_in, gw_sa], [r1_in, r1_sa], [MXU_DTYPE, F32], "rs_chip_sum_in")
    (c_in,), tok = _copies_start([_rider_rs_chips([s_in, s_sa])], "rs_chips_start_in")
    (gx, dg1, dba_in), _ = _inproj_bwd_call(du, dq, dk, dv, dgates, dx1, xt, win_t, g_mix_pre, rider=_after(tok))
    r2_down, r2_up, r2_out, r2_bp, r2_ba, r2_in, r2_sa = _copies_wait([c_down, c_up, c_obb, c_in], dg1, "rs_chips_wait")

    (g_sa,) = _final_sum_call(cidx, [gw_sa], [r1_sa], [r2_sa])
    part_b = _pack({"g_mix_pre": dg1, "b_in_head": dba_in}, _SMALL_B, sum(r for _, r in _SMALL_B))
    (c_small,), tok = _copies_start([_rider_gather_remote([g_sa, part_b])], "allgather_small_start")

    in_t = _adamw_rs_call(cidx, tok, [gw_in], [r1_in], [r2_in], [w_in[0].T], [m_w_in[0].T], [v_w_in[0].T], 2,
                          "adamw_w_in")
    rest = _adamw_rs_call(
        cidx, tok, [gw_bp, gw_ba, gw_out, gw_up, gw_down], [r1_bp, r1_ba, r1_out, r1_up, r1_down],
        [r2_bp, r2_ba, r2_out, r2_up, r2_down], [w_branch_pool[0], w_branch_attn[0], w_out[0], w_up[0], w_down[0]],
        [m_w_branch_pool[0], m_w_branch_attn[0], m_w_out[0], m_w_up[0], m_w_down[0]],
        [v_w_branch_pool[0], v_w_branch_attn[0], v_w_out[0], v_w_up[0], v_w_down[0]], N_DEV, "adamw_shards")
    big_g, big_d, big_m2, big_v2 = ([a[0].T] + list(b) for a, b in zip(in_t, rest))

    sa_all, sb_all = _copies_wait([c_small], rest[0][0], "allgather_small_wait")
    me = (_slot((cx, cy, cc)), 0, 0)
    sa_all = lax.dynamic_update_slice(sa_all, g_sa[None], me)
    sb_sum = _sum8_call(lax.dynamic_update_slice(sb_all, part_b[None], me))

    names = ["g_mix_pre", "b_in", "w_pool", "pool_scale", "attn_sinks", "g_mix_post", "g_mlp_pre", "g_mlp_post"]
    sm_w = dict(g_mix_pre=g_mix_pre, b_in=b_in, w_pool=w_pool, pool_scale=pool_scale, attn_sinks=attn_sinks,
                g_mix_post=g_mix_post, g_mlp_pre=g_mlp_pre, g_mlp_post=g_mlp_post)
    sm_m = dict(g_mix_pre=m_g_mix_pre, b_in=m_b_in, w_pool=m_w_pool, pool_scale=m_pool_scale, attn_sinks=m_attn_sinks,
                g_mix_post=m_g_mix_post, g_mlp_pre=m_g_mlp_pre, g_mlp_post=m_g_mlp_post)
    sm_v = dict(g_mix_pre=v_g_mix_pre, b_in=v_b_in, w_pool=v_w_pool, pool_scale=v_pool_scale, attn_sinks=v_attn_sinks,
                g_mix_post=v_g_mix_post, g_mlp_pre=v_g_mlp_pre, g_mlp_post=v_g_mlp_post)
    sizes = {k: sm_w[k].size for k in names}
    sizes.update(loss=D_MODEL, b_in_gates=GATE_WIDTH, b_in_head=C_G)
    sm_g = _unpack(sa_all.reshape(_SMALL_A_ROWS, LANES), _SMALL_A, sizes)
    sm_g.update(_unpack(sb_sum, _SMALL_B, sizes))
    sm_g["b_in"] = jnp.concatenate([sm_g["b_in_head"], sm_g["b_in_gates"]])
    loss = (0.5 / D_MODEL) * jnp.sum(sm_g["loss"])
    two_d = lambda a: a.reshape(-1, a.shape[-1])
    sd_, sm2_, sv2_ = _adamw_call([two_d(sm_w[k]) for k in names], [two_d(sm_g[k].reshape(sm_w[k].shape)) for k in names],
                                  [two_d(sm_m[k]) for k in names], [two_d(sm_v[k]) for k in names], 1, "adamw_small")
    like = lambda vals: {k: a.reshape(sm_w[k].shape) for k, a in zip(names, vals)}
    sm_d, sm_m2, sm_v2 = like(sd_), like(sm2_), like(sv2_)
    sm_gr = {k: sm_g[k].reshape(sm_w[k].shape) for k in names}

    order = ["g_mix_pre", "w_in", "b_in", "w_pool", "pool_scale", "attn_sinks", "w_branch_pool", "w_branch_attn",
             "w_out", "g_mix_post", "g_mlp_pre", "w_up", "w_down", "g_mlp_post"]
    big_names = ["w_in", "w_branch_pool", "w_branch_attn", "w_out", "w_up", "w_down"]
    lead = lambda a: a[None]
    tables = []
    for small_t, big_t in ((sm_gr, big_g), (sm_d, big_d), (sm_m2, big_m2), (sm_v2, big_v2)):
        bt = dict(zip(big_names, big_t))
        tables.append([lead(bt[k]) if k in bt else small_t[k] for k in order])
    return (loss, gx.reshape(B, S, D_MODEL), *tables[0], *tables[1], *tables[2], *tables[3])
```

```python
import functools

import jax
import jax.numpy as jnp
from jax import lax
from jax.experimental import pallas as pl
from jax.experimental.pallas import tpu as pltpu

F32 = jnp.float32
MXU_DTYPE = jnp.bfloat16
MESH = pl.DeviceIdType.MESH

D_MODEL = 1024
POOL_WINDOWS = (2, 4, 8, 16)
POOL_WIDTH = 512
POOL_GC = 128
HEAD_DIM = 64
N_Q_HEADS = 8
N_KV_HEADS = 2
GROUP = 4
ATTN_WIDTH = 512
KV_WIDTH = 128
BLOCK = 128
GATE_WIDTH = 2048
IN_WIDTH = 3328
D_FF = 4096
EPS = 1e-6
NEG_INF = -1e30
ROPE_THETA = 500000.0
ROT_DIM = 16
SCALE = HEAD_DIM ** -0.5
C_Q, C_K, C_V, C_G = 512, 1024, 1152, 1280

ADAM_LR = 0.001
ADAM_B1 = 0.9
ADAM_B2 = 0.999
ADAM_EPS = 1e-08
ADAM_WD = 0.01
ADAM_STEP = 10

N_DEV = 8
LANES = 128
VMEM_LIMIT = 56 * 1024 * 1024

NN = (((1,), (0,)), ((), ()))
NT = (((1,), (1,)), ((), ()))
TN = (((0,), (0,)), ((), ()))


def _dot(a, b, dims):
    return lax.dot_general(a, b, dims, preferred_element_type=F32)


def _params(sem=None):
    return pltpu.CompilerParams(dimension_semantics=sem, vmem_limit_bytes=VMEM_LIMIT)


def _tile(n, pref):
    t = min(n, pref)
    assert n % t == 0, (n, t)
    return t


class _Rider:
    def __init__(self, ins, out_shape, n_remote, n_local, plan, aliases=None):
        self.ins, self.out_shape, self.n_remote, self.n_local = list(ins), list(out_shape), n_remote, n_local
        self.plan, self.aliases = plan, dict(aliases or {})


def _after(token, rider=None):
    r = rider or _Rider([], [], 0, 0, lambda ins, outs, send, recv, loc, r0, l0: ([], []))
    return _Rider(r.ins + [token], r.out_shape, r.n_remote, r.n_local, r.plan, r.aliases)


def _launch(body, args, *, name, grid, in_specs, out_specs, out_shape, scratch_shapes=(), sem=None, rider=None):
    if rider is None:
        return pl.pallas_call(body, name=name, grid=grid, in_specs=in_specs, out_specs=out_specs, out_shape=out_shape,
                              scratch_shapes=list(scratch_shapes), compiler_params=_params(sem))(*args)
    n_in, n_out, n_scr = len(args), len(out_shape), len(scratch_shapes)
    r_in, r_out = len(rider.ins), len(rider.out_shape)
    copies = rider.n_remote + rider.n_local > 0

    def wrapped(*refs):
        ins, rins = refs[:n_in], refs[n_in:n_in + r_in]
        o0 = n_in + r_in
        outs, routs = refs[o0:o0 + n_out], refs[o0 + n_out:o0 + n_out + r_out]
        s0 = o0 + n_out + r_out
        scr = refs[s0:s0 + n_scr]
        if not copies:
            return body(*ins, *outs, *scr)
        send, recv, loc = refs[s0 + n_scr:]
        first, last = None, None
        for d in range(len(grid)):
            f, l = pl.program_id(d) == 0, pl.program_id(d) == pl.num_programs(d) - 1
            first = f if first is None else first & f
            last = l if last is None else last & l

        def start():
            remote, local = rider.plan(rins, routs, send, recv, loc, 0, 0)
            for cp in local + remote:
                cp.start()

        def finish():
            remote, local = rider.plan(rins, routs, send, recv, loc, 0, 0)
            for cp in remote + local:
                cp.wait()

        if first is None:
            start()
            body(*ins, *outs, *scr)
            finish()
        else:
            pl.when(first)(start)
            body(*ins, *outs, *scr)
            pl.when(last)(finish)

    hbm = pl.BlockSpec(memory_space=pl.ANY)
    dma = pltpu.SemaphoreType.DMA
    res = pl.pallas_call(
        wrapped, name=name, grid=grid, in_specs=list(in_specs) + [hbm] * r_in,
        out_specs=list(out_specs) + [hbm] * r_out, out_shape=list(out_shape) + rider.out_shape,
        scratch_shapes=list(scratch_shapes) + (
            [dma((max(rider.n_remote, 1),)), dma((max(rider.n_remote, 1),)), dma((max(rider.n_local, 1),))] if copies else []),
        input_output_aliases={n_in + i: n_out + o for i, o in rider.aliases.items()},
        compiler_params=_params(sem),
    )(*args, *rider.ins)
    return list(res[:n_out]), list(res[n_out:])


def _comm_call(rider, name):
    return _launch(lambda: None, [], name=name, grid=(), in_specs=[], out_specs=[], out_shape=[], rider=rider)[1]


_HBM = pl.BlockSpec(memory_space=pltpu.HBM)
_SEM = pl.BlockSpec(memory_space=pltpu.SEMAPHORE)
_EFFECT = pltpu.SideEffectType.DATAFLOW_SIDE_EFFECTING


def _copies_start(riders, name, after=None):
    assert all(r.n_local == 0 and not r.aliases for r in riders)
    extra = [] if after is None else [after]
    sizes = [(len(r.ins), len(r.out_shape)) for r in riders]
    bufs = []
    for r in riders:
        bufs += [pltpu.with_memory_space_constraint(a, pltpu.HBM) for a in r.ins]
        bufs += [pltpu.with_memory_space_constraint(lax.empty(s.shape, s.dtype), pltpu.HBM) for s in r.out_shape]
    nb, ng, ne = len(bufs), len(riders), len(extra)

    def body(*refs):
        sems, token, at = refs[2 * nb + ne:2 * nb + ne + 2 * ng], refs[-1], 0
        for g, (r, (ni, no)) in enumerate(zip(riders, sizes)):
            remote, _ = r.plan(refs[at:at + ni], refs[at + ni:at + ni + no], sems[2 * g], sems[2 * g + 1], None, 0, 0)
            for cp in remote:
                cp.start()
            at += ni + no
        token[...] = jnp.zeros_like(token)

    res = pl.pallas_call(
        body, name=name, in_specs=[_HBM] * nb + [pl.BlockSpec(memory_space=pl.ANY)] * ne,
        out_specs=[_HBM] * nb + [_SEM] * (2 * ng) + [pl.BlockSpec(memory_space=pltpu.VMEM)],
        out_shape=[pltpu.HBM(a.shape, a.dtype) for a in bufs]
        + [pltpu.SemaphoreType.DMA((r.n_remote,)) for r in riders for _ in range(2)]
        + [jax.ShapeDtypeStruct((8, LANES), F32)],
        input_output_aliases={i: i for i in range(nb)},
        compiler_params=pltpu.CompilerParams(has_side_effects=_EFFECT),
    )(*bufs, *extra)
    handles, at = [], 0
    for g, (r, (ni, no)) in enumerate(zip(riders, sizes)):
        handles.append((r, list(res[at:at + ni + no]), res[nb + 2 * g], res[nb + 2 * g + 1]))
        at += ni + no
    return handles, res[-1]


def _copies_wait(handles, after, name):
    bufs = [b for _, bs, _, _ in handles for b in bs]
    sems = [s for _, _, send, recv in handles for s in (send, recv)]
    nb, ng = len(bufs), len(handles)

    def body(*refs):
        at = 0
        for g, (rider, bs, _, _) in enumerate(handles):
            ni = len(rider.ins)
            remote, _ = rider.plan(refs[at:at + ni], refs[at + ni:at + len(bs)], refs[nb + 2 * g], refs[nb + 2 * g + 1],
                                   None, 0, 0)
            for cp in remote:
                cp.wait_send()
                cp.wait_recv()
            at += len(bs)

    res = pl.pallas_call(
        body, name=name, in_specs=[_HBM] * nb + [_SEM] * (2 * ng) + [pl.BlockSpec(memory_space=pl.ANY)],
        out_specs=[_HBM] * nb, out_shape=[pltpu.HBM(a.shape, a.dtype) for a in bufs],
        input_output_aliases={i: i for i in range(nb)},
        compiler_params=pltpu.CompilerParams(has_side_effects=_EFFECT),
    )(*bufs, *sems, after)
    lands, at = [], 0
    for rider, bs, _, _ in handles:
        lands += list(res[at + len(rider.ins):at + len(bs)])
        at += len(bs)
    return lands


def _rms_r(x):
    return lax.rsqrt(jnp.mean(x * x, axis=-1, keepdims=True) + EPS)


def _rms_bwd(dn, x, r, g):
    xh = x * r
    dxh = dn * g
    dx = r * (dxh - xh * jnp.mean(dxh * xh, axis=-1, keepdims=True))
    return dx, dn * xh


def _rot(t, c, sa, sb):
    outs = []
    for j in range(t.shape[1] // LANES):
        tj = t[:, LANES * j:LANES * (j + 1)]
        outs.append(tj * c + pltpu.roll(tj, LANES - 8, 1) * sa + pltpu.roll(tj, 8, 1) * sb)
    return outs[0] if len(outs) == 1 else jnp.concatenate(outs, axis=1)


def _rot_tables(S):
    pos = jnp.arange(S, dtype=F32)
    inv_freq = ROPE_THETA ** (-jnp.arange(0, ROT_DIM, 2, dtype=F32) / ROT_DIM)
    ang = pos[:, None] * inv_freq[None, :]
    cos, sin = jnp.cos(ang), jnp.sin(ang)
    one = jnp.ones((S, HEAD_DIM - ROT_DIM), F32)
    zero = jnp.zeros((S, HEAD_DIM - ROT_DIM), F32)
    z8 = jnp.zeros((S, 8), F32)
    c = jnp.concatenate([cos, cos, one], axis=1)
    sa = jnp.concatenate([-sin, z8, zero], axis=1)
    sb = jnp.concatenate([z8, sin, zero], axis=1)
    rep = LANES // HEAD_DIM
    return jnp.tile(c, (1, rep)), jnp.tile(sa, (1, rep)), jnp.tile(sb, (1, rep))


def _lane_tile4(k):
    lane = lax.broadcasted_iota(jnp.int32, k.shape, 1)
    rk = pltpu.roll(k, HEAD_DIM, 1)
    x0 = jnp.where(lane < HEAD_DIM, k, rk)
    x1 = jnp.where(lane < HEAD_DIM, rk, k)
    return jnp.concatenate([x0, x0, x1, x1], axis=1)


def _fold_heads(acc):
    zs = []
    for hk in range(N_KV_HEADS):
        a = acc[:, 256 * hk:256 * hk + LANES] + acc[:, 256 * hk + LANES:256 * (hk + 1)]
        zs.append(a + pltpu.roll(a, HEAD_DIM, 1))
    lane = lax.broadcasted_iota(jnp.int32, zs[0].shape, 1)
    return jnp.where(lane < HEAD_DIM, zs[0], zs[1])


def _inproj_call(x, g1, win_t, b_in, rc, rsa, rsb, S, rider=None):
    T = x.shape[0]
    tm = _tile(S, 512)
    nst = S // tm

    def body(x_ref, g1_ref, w_ref, b_ref, c_ref, sa_ref, sb_ref,
             h_ref, u_ref, q_ref, k4_ref, v4_ref, g_ref):
        xv = x_ref[...]
        hb = ((xv * _rms_r(xv)) * g1_ref[...]).astype(MXU_DTYPE)
        h_ref[...] = hb

        def proj(lo, hi):
            return _dot(hb, w_ref[lo:hi, :], NT) + b_ref[:, lo:hi]

        c, sa, sb = c_ref[...], sa_ref[...], sb_ref[...]
        u_ref[...] = proj(0, C_Q)
        q_ref[...] = (_rot(proj(C_Q, C_K), c, sa, sb) * SCALE).astype(MXU_DTYPE)
        kv = proj(C_K, C_G)
        k4_ref[...] = _lane_tile4(_rot(kv[:, :KV_WIDTH], c, sa, sb)).astype(MXU_DTYPE)
        v4_ref[...] = _lane_tile4(kv[:, KV_WIDTH:]).astype(MXU_DTYPE)
        g_ref[...] = jax.nn.sigmoid(proj(C_G, IN_WIDTH)).astype(MXU_DTYPE)

    tok = lambda w: pl.BlockSpec((tm, w), lambda i: (i, 0))
    full = lambda a: pl.BlockSpec(a.shape, lambda i: (0,) * a.ndim)
    tab = pl.BlockSpec((tm, LANES), lambda i: (i % nst, 0))
    return _launch(
        body, [x, g1, win_t, b_in, rc, rsa, rsb], name="inproj_fwd", grid=(T // tm,),
        in_specs=[tok(D_MODEL), full(g1), full(win_t), full(b_in), tab, tab, tab],
        out_specs=[tok(D_MODEL), tok(POOL_WIDTH), tok(ATTN_WIDTH), tok(512), tok(512), tok(GATE_WIDTH)],
        out_shape=[jax.ShapeDtypeStruct((T, D_MODEL), MXU_DTYPE), jax.ShapeDtypeStruct((T, POOL_WIDTH), F32),
                   jax.ShapeDtypeStruct((T, ATTN_WIDTH), MXU_DTYPE), jax.ShapeDtypeStruct((T, 512), MXU_DTYPE),
                   jax.ShapeDtypeStruct((T, 512), MXU_DTYPE), jax.ShapeDtypeStruct((T, GATE_WIDTH), MXU_DTYPE)],
        sem=("arbitrary",), rider=rider)


def _shift_rows(a, k, rows):
    n = a.shape[0]
    if k > 0:
        return jnp.where(rows >= k, pltpu.roll(a, k, 0), 0.0)
    return jnp.where(rows < n + k, pltpu.roll(a, n + k, 0), 0.0)


def _win_sum(a, w, rows, sign):
    s, k = a, 1
    while k < w:
        s = s + _shift_rows(s, sign * k, rows)
        k *= 2
    return s


def _pool_diff(ug, w, rows):
    inv = 1.0 / jnp.minimum(rows + 1, w).astype(F32)
    return _win_sum(ug, w, rows, 1) * inv - ug, inv


def _pool_call(u, w_pool, pool_scale, S):
    T = u.shape[0]

    def body(u_ref, w_ref, ps_ref, y_ref):
        rows = lax.broadcasted_iota(jnp.int32, (S, POOL_GC), 0)
        for gi, w in enumerate(POOL_WINDOWS):
            sl = slice(POOL_GC * gi, POOL_GC * (gi + 1))
            diff, _ = _pool_diff(u_ref[:, sl], w, rows)
            mixed = _dot(diff.astype(MXU_DTYPE), w_ref[gi], NN)
            y_ref[:, sl] = (mixed * ps_ref[:, sl]).astype(MXU_DTYPE)

    seq = pl.BlockSpec((S, POOL_WIDTH), lambda b: (b, 0))
    return pl.pallas_call(
        body, name="pool_fwd", grid=(T // S,),
        in_specs=[seq, pl.BlockSpec(w_pool.shape, lambda b: (0, 0, 0)), pl.BlockSpec(pool_scale.shape, lambda b: (0, 0))],
        out_specs=seq, out_shape=jax.ShapeDtypeStruct((T, POOL_WIDTH), MXU_DTYPE),
        compiler_params=_params(("arbitrary",)),
    )(u, w_pool, pool_scale)


def _pool_bwd_call(u, dyp, w_pool, pool_scale, S, rider=None):
    T = u.shape[0]

    def body(u_ref, dy_ref, w_ref, ps_ref, du_ref, dw_ref, dps_ref):
        @pl.when(pl.program_id(0) == 0)
        def _():
            dw_ref[...] = jnp.zeros_like(dw_ref)
            dps_ref[...] = jnp.zeros_like(dps_ref)

        rows = lax.broadcasted_iota(jnp.int32, (S, POOL_GC), 0)
        for gi, w in enumerate(POOL_WINDOWS):
            sl = slice(POOL_GC * gi, POOL_GC * (gi + 1))
            diff, inv = _pool_diff(u_ref[:, sl], w, rows)
            diffb = diff.astype(MXU_DTYPE)
            wg = w_ref[gi]
            mixed = _dot(diffb, wg, NN)
            dy = dy_ref[:, sl]
            dps_ref[:, sl] += jnp.sum(dy * mixed, axis=0, keepdims=True)
            dmb = (dy * ps_ref[:, sl]).astype(MXU_DTYPE)
            dw_ref[gi] += _dot(diffb, dmb, TN)
            ddiff = _dot(dmb, wg, NT)
            du_ref[:, sl] = (_win_sum(ddiff * inv, w, rows, -1) - ddiff).astype(MXU_DTYPE)

    seq = pl.BlockSpec((S, POOL_WIDTH), lambda b: (b, 0))
    return _launch(
        body, [u, dyp, w_pool, pool_scale], name="pool_bwd", grid=(T // S,),
        in_specs=[seq, seq, pl.BlockSpec(w_pool.shape, lambda b: (0, 0, 0)), pl.BlockSpec(pool_scale.shape, lambda b: (0, 0))],
        out_specs=[seq, pl.BlockSpec(w_pool.shape, lambda b: (0, 0, 0)), pl.BlockSpec(pool_scale.shape, lambda b: (0, 0))],
        out_shape=[jax.ShapeDtypeStruct((T, POOL_WIDTH), MXU_DTYPE), jax.ShapeDtypeStruct(w_pool.shape, F32),
                   jax.ShapeDtypeStruct(pool_scale.shape, F32)],
        sem=("arbitrary",), rider=rider)


def _attn_consts():
    lane_g = lax.broadcasted_iota(jnp.int32, (BLOCK, 256), 1) >> 6
    rgrp = lax.broadcasted_iota(jnp.int32, (GROUP * BLOCK, 1), 0) >> 7
    rel = lax.broadcasted_iota(jnp.int32, (BLOCK, 256), 0) - lax.broadcasted_iota(jnp.int32, (BLOCK, 256), 1)

    def bias(off):
        ok = (rel + off >= 0) & (rel + off < BLOCK)
        return jnp.concatenate([jnp.where(ok, 0.0, NEG_INF)] * GROUP, axis=0)

    return lane_g, rgrp, bias(0), bias(BLOCK)


def _sink_rows(sink_ref, hk, rgrp):
    sv = jnp.zeros(rgrp.shape, F32)
    for g in range(GROUP):
        sv = jnp.where(rgrp == g, sink_ref[0, GROUP * hk + g], sv)
    return sv


def _stack_heads(xb, lane_g):
    return jnp.concatenate([jnp.where(lane_g == g, xb, jnp.zeros_like(xb)) for g in range(GROUP)], axis=0)


def _unstack_heads(xs, lane_g):
    out = jnp.where(lane_g == 0, xs[0:BLOCK], 0.0)
    for g in range(1, GROUP):
        out = out + jnp.where(lane_g == g, xs[BLOCK * g:BLOCK * (g + 1)], 0.0)
    return out


def _attn_probs(qs, kb, bias, sv):
    s = _dot(qs, kb, NT) + bias
    m = jnp.maximum(jnp.max(s, axis=1, keepdims=True), sv)
    e = jnp.exp(s - m)
    es = jnp.exp(sv - m)
    inv_l = 1.0 / (jnp.sum(e, axis=1, keepdims=True) + es)
    return e * inv_l, es * inv_l


def _attn_blocks(nb, blk, carry, per=1):
    carry = blk(0, 0, True, carry)
    per = per if (nb - 1) % per == 0 else 1

    def step(i, c):
        for k in range(per):
            n = 1 + per * i + k
            c = blk(pl.multiple_of(n * BLOCK, BLOCK), pl.multiple_of((n - 1) * BLOCK, BLOCK), False, c)
        return c

    return lax.fori_loop(0, (nb - 1) // per, step, carry)


def _attn_call(sinks, q, k4, v4, S, rider=None):
    T = q.shape[0]
    nb = S // BLOCK

    def body(sink_ref, q_ref, k_ref, v_ref, o_ref):
        lane_g, rgrp, bias_first, bias_later = _attn_consts()
        svs = [_sink_rows(sink_ref, hk, rgrp) for hk in range(N_KV_HEADS)]

        def blk(q0, k0, first, carry):
            for hk in range(N_KV_HEADS):
                cs = slice(256 * hk, 256 * (hk + 1))
                qs = _stack_heads(q_ref[pl.ds(q0, BLOCK), cs], lane_g)
                p, _ = _attn_probs(qs, k_ref[pl.ds(k0, 2 * BLOCK), cs], bias_first if first else bias_later, svs[hk])
                o = _dot(p.astype(MXU_DTYPE), v_ref[pl.ds(k0, 2 * BLOCK), cs], NN)
                o_ref[pl.ds(q0, BLOCK), cs] = _unstack_heads(o, lane_g).astype(MXU_DTYPE)
            return carry

        _attn_blocks(nb, blk, 0, per=3)

    seq = pl.BlockSpec((S, ATTN_WIDTH), lambda b: (b, 0))
    return _launch(
        body, [sinks, q, k4, v4], name="attn_fwd", grid=(T // S,),
        in_specs=[pl.BlockSpec(memory_space=pltpu.SMEM), seq, seq, seq],
        out_specs=[seq], out_shape=[jax.ShapeDtypeStruct((T, ATTN_WIDTH), MXU_DTYPE)],
        sem=("arbitrary",), rider=rider)


def _attn_bwd_call(sinks, q, k4, v4, do, rc, rsa, rsb, S, rider=None):
    T = q.shape[0]
    nb = S // BLOCK

    def body(sink_ref, q_ref, k_ref, v_ref, do_ref, c_ref, sa_ref, sb_ref,
             dq_ref, dk_ref, dv_ref, ds_ref, dk_acc, dv_acc):
        lane_g, rgrp, bias_first, bias_later = _attn_consts()
        svs = [_sink_rows(sink_ref, hk, rgrp) for hk in range(N_KV_HEADS)]
        lane1 = lax.broadcasted_iota(jnp.int32, (1, LANES), 1)
        dk_acc[...] = jnp.zeros_like(dk_acc)
        dv_acc[...] = jnp.zeros_like(dv_acc)

        def blk(q0, k0, first, dsink):
            rows = pl.ds(q0, BLOCK)
            c, sa, sb = c_ref[rows, :], sa_ref[rows, :], sb_ref[rows, :]
            for hk in range(N_KV_HEADS):
                cs = slice(256 * hk, 256 * (hk + 1))
                qs = _stack_heads(q_ref[rows, cs], lane_g)
                dos = _stack_heads(do_ref[rows, cs], lane_g)
                kb = k_ref[pl.ds(k0, 2 * BLOCK), cs]
                vb = v_ref[pl.ds(k0, 2 * BLOCK), cs]
                p, ps = _attn_probs(qs, kb, bias_first if first else bias_later, svs[hk])
                dp = _dot(dos, vb, NT)
                delta = jnp.sum(p * dp, axis=1, keepdims=True)
                dsb = (p * (dp - delta)).astype(MXU_DTYPE)
                dqb = _unstack_heads(_dot(dsb, kb, NN), lane_g) * SCALE
                dq_ref[rows, cs] = _rot(dqb, c, -sa, -sb).astype(MXU_DTYPE)
                dk_acc[pl.ds(k0, 2 * BLOCK), cs] += _dot(dsb, qs, TN)
                dv_acc[pl.ds(k0, 2 * BLOCK), cs] += _dot(p.astype(MXU_DTYPE), dos, TN)
                psd = ps * delta
                for g in range(GROUP):
                    val = -jnp.sum(psd[BLOCK * g:BLOCK * (g + 1)], axis=0, keepdims=True)
                    dsink = dsink + jnp.where(lane1 == GROUP * hk + g, val, 0.0)
            return dsink

        dsink = _attn_blocks(nb, blk, jnp.zeros((1, LANES), F32))
        dk_ref[...] = _rot(_fold_heads(dk_acc[...]), c_ref[...], -sa_ref[...], -sb_ref[...]).astype(MXU_DTYPE)
        dv_ref[...] = _fold_heads(dv_acc[...]).astype(MXU_DTYPE)
        ds_ref[...] = jnp.broadcast_to(dsink, ds_ref.shape)

    seq = pl.BlockSpec((S, ATTN_WIDTH), lambda b: (b, 0))
    kvs = pl.BlockSpec((S, KV_WIDTH), lambda b: (b, 0))
    tab = pl.BlockSpec((S, LANES), lambda b: (0, 0))
    nseq = T // S
    return _launch(
        body, [sinks, q, k4, v4, do, rc, rsa, rsb], name="attn_bwd", grid=(nseq,),
        in_specs=[pl.BlockSpec(memory_space=pltpu.SMEM), seq, seq, seq, seq, tab, tab, tab],
        out_specs=[seq, kvs, kvs, pl.BlockSpec((8, LANES), lambda b: (b, 0))],
        out_shape=[jax.ShapeDtypeStruct((T, ATTN_WIDTH), MXU_DTYPE), jax.ShapeDtypeStruct((T, KV_WIDTH), MXU_DTYPE),
                   jax.ShapeDtypeStruct((T, KV_WIDTH), MXU_DTYPE), jax.ShapeDtypeStruct((8 * nseq, LANES), F32)],
        scratch_shapes=[pltpu.VMEM((S, 512), F32), pltpu.VMEM((S, 512), F32)],
        sem=("arbitrary",), rider=rider)


def _branch_weights(wbp_ref, wba_ref, wbp_s, wba_s):
    @pl.when(pl.program_id(0) == 0)
    def _():
        for j in range(N_DEV):
            wbp_s[:, LANES * j:LANES * (j + 1)] = wbp_ref[j]
            wba_s[:, LANES * j:LANES * (j + 1)] = wba_ref[j]


def _mix_fwd_call(yp, ya, g, x, wbp, wba, wout, g2, g3, rider=None):
    T = x.shape[0]
    tm = _tile(T, 512)

    def body(yp_ref, ya_ref, g_ref, x_ref, wbp_ref, wba_ref, wout_ref, g2_ref, g3_ref,
             mix_ref, x1_ref, h2_ref, h2t_ref, wbp_s, wba_s):
        _branch_weights(wbp_ref, wba_ref, wbp_s, wba_s)
        bp = _dot(yp_ref[...], wbp_s[...], NN)
        ba = _dot(ya_ref[...], wba_s[...], NN)
        merged = g_ref[:, :D_MODEL].astype(F32) * bp + g_ref[:, D_MODEL:].astype(F32) * ba
        mix = _dot(merged.astype(MXU_DTYPE), wout_ref[...], NN)
        mix_ref[...] = mix
        x1 = x_ref[...] + (mix * _rms_r(mix)) * g2_ref[...]
        x1_ref[...] = x1
        h2 = (x1 * _rms_r(x1)) * g3_ref[...]
        h2_ref[...] = h2.astype(MXU_DTYPE)
        h2t_ref[...] = h2.T.astype(MXU_DTYPE)

    tok = lambda w: pl.BlockSpec((tm, w), lambda i: (i, 0))
    full = lambda a: pl.BlockSpec(a.shape, lambda i: (0,) * a.ndim)
    return _launch(
        body, [yp, ya, g, x, wbp, wba, wout, g2, g3], name="mix_fwd", grid=(T // tm,),
        in_specs=[tok(POOL_WIDTH), tok(ATTN_WIDTH), tok(GATE_WIDTH), tok(D_MODEL), full(wbp), full(wba), full(wout),
                  full(g2), full(g3)],
        out_specs=[tok(D_MODEL), tok(D_MODEL), tok(D_MODEL), pl.BlockSpec((D_MODEL, tm), lambda i: (0, i))],
        out_shape=[jax.ShapeDtypeStruct((T, D_MODEL), F32), jax.ShapeDtypeStruct((T, D_MODEL), F32),
                   jax.ShapeDtypeStruct((T, D_MODEL), MXU_DTYPE), jax.ShapeDtypeStruct((D_MODEL, T), MXU_DTYPE)],
        scratch_shapes=[pltpu.VMEM((POOL_WIDTH, D_MODEL), MXU_DTYPE), pltpu.VMEM((ATTN_WIDTH, D_MODEL), MXU_DTYPE)],
        sem=("arbitrary",), rider=rider)


def _mix_bwd_call(dx1, mix, yp, ya, g, wbp, wba, wout, g2, rider=None):
    T = dx1.shape[0]
    tm = _tile(T, 512)

    def body(dx1_ref, mix_ref, yp_ref, ya_ref, g_ref, wbp_ref, wba_ref, wout_ref, g2_ref,
             dyp_ref, do_ref, dgates_ref, dg2_ref, dbg_ref, gout_ref, gbp_ref, gba_ref,
             wbp_s, wba_s, acc_out, acc_bp, acc_ba, sem):
        _branch_weights(wbp_ref, wba_ref, wbp_s, wba_s)
        step = pl.program_id(0)

        @pl.when(step == 0)
        def _():
            dg2_ref[...] = jnp.zeros_like(dg2_ref)
            dbg_ref[...] = jnp.zeros_like(dbg_ref)
            acc_out[...] = jnp.zeros_like(acc_out)
            acc_bp[...] = jnp.zeros_like(acc_bp)
            acc_ba[...] = jnp.zeros_like(acc_ba)

        mix = mix_ref[...]
        dmix, dg2 = _rms_bwd(dx1_ref[...], mix, _rms_r(mix), g2_ref[...])
        dg2_ref[...] += jnp.sum(dg2, axis=0, keepdims=True)
        dmixb = dmix.astype(MXU_DTYPE)
        dmerged = _dot(dmixb, wout_ref[...], NT)
        yp, ya = yp_ref[...], ya_ref[...]
        bp = _dot(yp, wbp_s[...], NN)
        ba = _dot(ya, wba_s[...], NN)
        gp, ga = g_ref[:, :D_MODEL].astype(F32), g_ref[:, D_MODEL:].astype(F32)
        acc_out[...] += _dot((gp * bp + ga * ba).astype(MXU_DTYPE), dmixb, TN)
        dgp = dmerged * bp * (gp * (1.0 - gp))
        dga = dmerged * ba * (ga * (1.0 - ga))
        dbg_ref[:, :D_MODEL] += jnp.sum(dgp, axis=0, keepdims=True)
        dbg_ref[:, D_MODEL:] += jnp.sum(dga, axis=0, keepdims=True)
        dgates_ref[:, :D_MODEL] = dgp.astype(MXU_DTYPE)
        dgates_ref[:, D_MODEL:] = dga.astype(MXU_DTYPE)
        dbp = (dmerged * gp).astype(MXU_DTYPE)
        dba = (dmerged * ga).astype(MXU_DTYPE)
        acc_bp[...] += _dot(yp, dbp, TN)
        acc_ba[...] += _dot(ya, dba, TN)
        dyp_ref[...] = _dot(dbp, wbp_s[...], NT)
        do_ref[...] = _dot(dba, wba_s[...], NT).astype(MXU_DTYPE)

        @pl.when(step == pl.num_programs(0) - 1)
        def _():
            copies = [pltpu.make_async_copy(acc_out, gout_ref, sem.at[0])]
            for j in range(N_DEV):
                cols = slice(LANES * j, LANES * (j + 1))
                copies.append(pltpu.make_async_copy(acc_bp.at[:, cols], gbp_ref.at[j], sem.at[1 + j]))
                copies.append(pltpu.make_async_copy(acc_ba.at[:, cols], gba_ref.at[j], sem.at[1 + N_DEV + j]))
            for cp in copies:
                cp.start()
            for cp in copies:
                cp.wait()

    tok = lambda w: pl.BlockSpec((tm, w), lambda i: (i, 0))
    full = lambda a: pl.BlockSpec(a.shape, lambda i: (0,) * a.ndim)
    acc = lambda w: pl.BlockSpec((1, w), lambda i: (0, 0))
    hbm = pl.BlockSpec(memory_space=pl.ANY)
    sd = jax.ShapeDtypeStruct
    return _launch(
        body, [dx1, mix, yp, ya, g, wbp, wba, wout, g2], name="mix_bwd", grid=(T // tm,),
        in_specs=[tok(D_MODEL), tok(D_MODEL), tok(POOL_WIDTH), tok(ATTN_WIDTH), tok(GATE_WIDTH), full(wbp), full(wba),
                  full(wout), full(g2)],
        out_specs=[tok(POOL_WIDTH), tok(ATTN_WIDTH), tok(GATE_WIDTH), acc(D_MODEL), acc(GATE_WIDTH), hbm, hbm, hbm],
        out_shape=[sd((T, POOL_WIDTH), F32), sd((T, ATTN_WIDTH), MXU_DTYPE), sd((T, GATE_WIDTH), MXU_DTYPE),
                   sd((1, D_MODEL), F32), sd((1, GATE_WIDTH), F32), sd((D_MODEL, D_MODEL), F32),
                   sd((N_DEV, POOL_WIDTH, LANES), F32), sd((N_DEV, ATTN_WIDTH, LANES), F32)],
        scratch_shapes=[pltpu.VMEM((POOL_WIDTH, D_MODEL), MXU_DTYPE), pltpu.VMEM((ATTN_WIDTH, D_MODEL), MXU_DTYPE),
                        pltpu.VMEM((D_MODEL, D_MODEL), F32), pltpu.VMEM((POOL_WIDTH, D_MODEL), F32),
                        pltpu.VMEM((ATTN_WIDTH, D_MODEL), F32), pltpu.SemaphoreType.DMA((1 + 2 * N_DEV,))],
        sem=("arbitrary",), rider=rider)


def _mlp_call(x1, h2, target, wup, wdown, g3, g4):
    T = x1.shape[0]
    tm = _tile(T, 256)
    fc = D_FF // N_DEV

    def body(x1_ref, h2_ref, t_ref, wup_ref, wdown_ref, g3_ref, g4_ref,
             act_ref, da_ref, dff_ref, dx1_ref, dg3_ref, dg4_ref, loss_ref, rl_s):
        @pl.when(pl.program_id(0) == 0)
        def _():
            dg3_ref[...] = jnp.zeros_like(dg3_ref)
            dg4_ref[...] = jnp.zeros_like(dg4_ref)
            loss_ref[...] = jnp.zeros_like(loss_ref)

        h2 = h2_ref[...]
        ff = jnp.zeros((tm, D_MODEL), F32)
        for j in range(N_DEV):
            sl = slice(fc * j, fc * (j + 1))
            rl = jnp.maximum(_dot(h2, wup_ref[j], NN), 0.0)
            rl_s[:, sl] = rl
            act = rl * rl
            act_ref[sl, :] = act.T.astype(MXU_DTYPE)
            ff = ff + _dot(act.astype(MXU_DTYPE), wdown_ref[j], NN)
        x1 = x1_ref[...]
        r4 = _rms_r(ff)
        err = x1 + (ff * r4) * g4_ref[...] - t_ref[...]
        loss_ref[...] += jnp.sum(err * err, axis=0, keepdims=True)
        dy = err * (1.0 / D_MODEL)
        dff, dg4 = _rms_bwd(dy, ff, r4, g4_ref[...])
        dg4_ref[...] += jnp.sum(dg4, axis=0, keepdims=True)
        dffb = dff.astype(MXU_DTYPE)
        dff_ref[...] = dffb
        dh2 = jnp.zeros((tm, D_MODEL), F32)
        for j in range(N_DEV):
            sl = slice(fc * j, fc * (j + 1))
            dab = (_dot(dffb, wdown_ref[j], NT) * (2.0 * rl_s[:, sl])).astype(MXU_DTYPE)
            da_ref[:, sl] = dab
            dh2 = dh2 + _dot(dab, wup_ref[j], NT)
        dx1, dg3 = _rms_bwd(dh2, x1, _rms_r(x1), g3_ref[...])
        dg3_ref[...] += jnp.sum(dg3, axis=0, keepdims=True)
        dx1_ref[...] = dy + dx1

    tok = lambda w: pl.BlockSpec((tm, w), lambda i: (i, 0))
    full = lambda a: pl.BlockSpec(a.shape, lambda i: (0,) * a.ndim, pipeline_mode=pl.Buffered(1))
    vec = pl.BlockSpec((1, D_MODEL), lambda i: (0, 0))
    sd = jax.ShapeDtypeStruct
    return pl.pallas_call(
        body, name="mlp_fwd_bwd", grid=(T // tm,),
        in_specs=[tok(D_MODEL), tok(D_MODEL), tok(D_MODEL), full(wup), full(wdown), vec, vec],
        out_specs=[pl.BlockSpec((D_FF, tm), lambda i: (0, i)), tok(D_FF), tok(D_MODEL), tok(D_MODEL), vec, vec, vec],
        out_shape=[sd((D_FF, T), MXU_DTYPE), sd((T, D_FF), MXU_DTYPE), sd((T, D_MODEL), MXU_DTYPE),
                   sd((T, D_MODEL), F32), sd((1, D_MODEL), F32), sd((1, D_MODEL), F32), sd((1, D_MODEL), F32)],
        scratch_shapes=[pltpu.VMEM((tm, D_FF), F32)],
        compiler_params=_params(("arbitrary",)),
    )(x1, h2, target, wup, wdown, g3, g4)


def _inproj_bwd_call(du, dq, dk, dv, dgates, dx1, x, win_t, g1, rider=None):
    T = x.shape[0]
    tm = _tile(T, 512)

    def body(du_ref, dq_ref, dk_ref, dv_ref, dgt_ref, dx1_ref, x_ref, w_ref, g1_ref, gx_ref, dg1_ref, db_ref):
        @pl.when(pl.program_id(0) == 0)
        def _():
            dg1_ref[...] = jnp.zeros_like(dg1_ref)
            db_ref[...] = jnp.zeros_like(db_ref)

        dh = jnp.zeros((tm, D_MODEL), F32)
        for ref, lo, hi in ((du_ref, 0, C_Q), (dq_ref, C_Q, C_K), (dk_ref, C_K, C_V), (dv_ref, C_V, C_G),
                            (dgt_ref, C_G, IN_WIDTH)):
            piece = ref[...]
            dh = dh + _dot(piece, w_ref[lo:hi, :], NN)
            if hi <= C_G:
                db_ref[:, lo:hi] += jnp.sum(piece.astype(F32), axis=0, keepdims=True)
        xv = x_ref[...]
        dx, dg1 = _rms_bwd(dh, xv, _rms_r(xv), g1_ref[...])
        dg1_ref[...] += jnp.sum(dg1, axis=0, keepdims=True)
        gx_ref[...] = dx1_ref[...] + dx

    tok = lambda w: pl.BlockSpec((tm, w), lambda i: (i, 0))
    full = lambda a: pl.BlockSpec(a.shape, lambda i: (0,) * a.ndim)
    sd = jax.ShapeDtypeStruct
    return _launch(
        body, [du, dq, dk, dv, dgates, dx1, x, win_t, g1], name="inproj_bwd", grid=(T // tm,),
        in_specs=[tok(POOL_WIDTH), tok(ATTN_WIDTH), tok(KV_WIDTH), tok(KV_WIDTH), tok(GATE_WIDTH), tok(D_MODEL),
                  tok(D_MODEL), full(win_t), full(g1)],
        out_specs=[tok(D_MODEL), pl.BlockSpec((1, D_MODEL), lambda i: (0, 0)), pl.BlockSpec((1, C_G), lambda i: (0, 0))],
        out_shape=[sd((T, D_MODEL), F32), sd((1, D_MODEL), F32), sd((1, C_G), F32)],
        sem=("arbitrary",), rider=rider)


WGRAD_TOKENS = 1024


def _wgrad_rows_call(at, b, name, rider=None):
    K, T = at.shape
    N = b.shape[1]
    tm = _tile(T, WGRAD_TOKENS)
    kb = min(K, 1024)
    per = kb // (K // N_DEV)

    def body(a_ref, b_ref, o_ref):
        @pl.when(pl.program_id(1) == 0)
        def _():
            o_ref[...] = jnp.zeros_like(o_ref)

        d = _dot(a_ref[...], b_ref[...], NN)
        rs = kb // per
        for j in range(per):
            o_ref[j] += d[rs * j:rs * (j + 1)]

    return _launch(
        body, [at, b], name=name, grid=(K // kb, T // tm),
        in_specs=[pl.BlockSpec((kb, tm), lambda i, t: (i, t)), pl.BlockSpec((tm, N), lambda i, t: (t, 0))],
        out_specs=[pl.BlockSpec((per, K // N_DEV, N), lambda i, t: (i, 0, 0))],
        out_shape=[jax.ShapeDtypeStruct((N_DEV, K // N_DEV, N), F32)],
        sem=("arbitrary", "arbitrary"), rider=rider)


def _wgrad_cols_call(at, b, name, rider=None):
    K, T = at.shape
    N = b.shape[1]
    tm = _tile(T, WGRAD_TOKENS)
    nb = min(N, 1024)
    per = nb // (N // N_DEV)

    def body(a_ref, b_ref, o_ref):
        @pl.when(pl.program_id(1) == 0)
        def _():
            o_ref[...] = jnp.zeros_like(o_ref)

        d = _dot(a_ref[...], b_ref[...], NN)
        cs = nb // per
        for j in range(per):
            o_ref[j] += d[:, cs * j:cs * (j + 1)]

    return _launch(
        body, [at, b], name=name, grid=(N // nb, T // tm),
        in_specs=[pl.BlockSpec((K, tm), lambda i, t: (0, t)), pl.BlockSpec((tm, nb), lambda i, t: (t, i))],
        out_specs=[pl.BlockSpec((per, K, N // N_DEV), lambda i, t: (i, 0, 0))],
        out_shape=[jax.ShapeDtypeStruct((N_DEV, K, N // N_DEV), F32)],
        sem=("arbitrary", "arbitrary"), rider=rider)


def _wgrad_in_call(du, dq, dk, dv, dgates, h, rider=None):
    T = h.shape[0]
    tm = _tile(T, WGRAD_TOKENS)
    rows = IN_WIDTH // N_DEV

    def body(du_ref, dq_ref, dk_ref, dv_ref, dgt_ref, h_ref, o_ref, acc, sem):
        t = pl.program_id(0)

        @pl.when(t == 0)
        def _():
            acc[...] = jnp.zeros_like(acc)

        hv = h_ref[...]
        for ref, lo, hi in ((du_ref, 0, C_Q), (dq_ref, C_Q, C_K), (dk_ref, C_K, C_V), (dv_ref, C_V, C_G),
                            (dgt_ref, C_G, IN_WIDTH)):
            acc[lo:hi, :] += _dot(ref[...], hv, TN)

        @pl.when(t == pl.num_programs(0) - 1)
        def _():
            copies = [pltpu.make_async_copy(acc.at[pl.ds(rows * j, rows), :], o_ref.at[j], sem.at[j])
                      for j in range(N_DEV)]
            for cp in copies:
                cp.start()
            for cp in copies:
                cp.wait()

    tok = lambda w: pl.BlockSpec((tm, w), lambda t: (t, 0))
    return _launch(
        body, [du, dq, dk, dv, dgates, h], name="wgrad_in", grid=(T // tm,),
        in_specs=[tok(POOL_WIDTH), tok(ATTN_WIDTH), tok(KV_WIDTH), tok(KV_WIDTH), tok(GATE_WIDTH), tok(D_MODEL)],
        out_specs=[pl.BlockSpec(memory_space=pl.ANY)],
        out_shape=[jax.ShapeDtypeStruct((N_DEV, rows, D_MODEL), F32)],
        scratch_shapes=[pltpu.VMEM((IN_WIDTH, D_MODEL), F32), pltpu.SemaphoreType.DMA((N_DEV,))],
        sem=("arbitrary",), rider=rider)


def _coords():
    return lax.axis_index("x"), lax.axis_index("y"), lax.axis_index("c")


def _allgather_call(shards):
    n = len(shards)

    def body(*refs):
        ins, outs = refs[:n], refs[n:2 * n]
        send_sems, recv_sems, local_sems = refs[2 * n:]
        x, y, c = _coords()
        me, sibling = (x, y, c), (x, y, 1 - c)
        chips = [(1 - x, y), (x, 1 - y), (1 - x, 1 - y)]

        def slot(p):
            return 4 * p[0] + 2 * p[1] + p[2]

        def copy(t, k, block, to, src=None):
            dst = outs[t].at[slot(block)]
            return pltpu.make_async_remote_copy(
                src_ref=dst if src is None else src, dst_ref=dst, send_sem=send_sems.at[t, k],
                recv_sem=recv_sems.at[t, k], device_id=to, device_id_type=MESH)

        mine = [pltpu.make_async_copy(ins[t], outs[t].at[slot(me)], local_sems.at[t]) for t in range(n)]
        for cp in mine:
            cp.start()
        first = []
        for t in range(n):
            first.append(copy(t, 0, me, sibling, src=ins[t]))
            first += [copy(t, 1 + j, me, (*chip, c), src=ins[t]) for j, chip in enumerate(chips)]
        for cp in first:
            cp.start()
        passed = []
        for t in range(n):
            for j, chip in enumerate(chips):
                copy(t, 1 + j, (*chip, c), me).wait_recv()
                fwd = copy(t, 4 + j, (*chip, c), sibling)
                fwd.start()
                passed.append(fwd)
        for t in range(n):
            copy(t, 0, sibling, me).wait_recv()
            for j, chip in enumerate(chips):
                copy(t, 4 + j, (*chip, 1 - c), me).wait_recv()
        for cp in first + passed:
            cp.wait_send()
        for cp in mine:
            cp.wait()

    hbm = pl.BlockSpec(memory_space=pl.ANY)
    return pl.pallas_call(
        body, name="allgather_weights",
        in_specs=[hbm] * n, out_specs=[hbm] * n,
        out_shape=[jax.ShapeDtypeStruct((N_DEV,) + s.shape, s.dtype) for s in shards],
        scratch_shapes=[pltpu.SemaphoreType.DMA((n, 7)), pltpu.SemaphoreType.DMA((n, 7)), pltpu.SemaphoreType.DMA((n,))],
    )(*shards)


def _slot(p):
    return 4 * p[0] + 2 * p[1] + p[2]


def _rows(ref, span):
    return ref if span is None else ref.at[pl.ds(span[0], span[1])]


ALL = "all"
LOCAL = "local"


def _rows(ref, span):
    return ref if span == ALL else ref.at[pl.ds(span[0], span[1])]


def _rider_ag(items):
    ins, out_shape, aliases, where = [], [], {}, []
    n_remote = n_local = 0
    for t, (shard, buf, snd, fwd) in enumerate(items):
        i_shard = i_buf = None
        if snd is not None:
            i_shard = len(ins)
            ins.append(shard)
        if buf is not None:
            i_buf = len(ins)
            ins.append(buf)
            aliases[i_buf] = t
            out_shape.append(jax.ShapeDtypeStruct(buf.shape, buf.dtype))
        else:
            assert fwd is None and snd is not None
            out_shape.append(jax.ShapeDtypeStruct((N_DEV,) + shard.shape, shard.dtype))
        where.append((i_shard, i_buf, n_remote, n_local))
        n_remote += (4 if snd not in (None, LOCAL) else 0) + (3 if fwd is not None else 0)
        n_local += 1 if snd is not None else 0

    def plan(rins, routs, send, recv, loc, r0, l0):
        x, y, c = _coords()
        peers = [(x, y, 1 - c), (1 - x, y, c), (x, 1 - y, c), (1 - x, 1 - y, c)]
        remote, local = [], []
        for t, (shard, buf, snd, fwd) in enumerate(items):
            i_shard, i_buf, k, l = where[t]
            k, l = r0 + k, l0 + l
            if snd is not None:
                span = ALL if snd == LOCAL else snd
                src, dst = _rows(rins[i_shard], span), _rows(routs[t].at[_slot((x, y, c))], span)
                local.append(pltpu.make_async_copy(src, dst, loc.at[l]))
                for peer in (peers if snd != LOCAL else []):
                    remote.append(pltpu.make_async_remote_copy(
                        src_ref=src, dst_ref=dst, send_sem=send.at[k], recv_sem=recv.at[k],
                        device_id=peer, device_id_type=MESH))
                    k += 1
            if fwd is not None:
                for px, py, pc in peers[1:]:
                    s = _slot((px, py, pc))
                    remote.append(pltpu.make_async_remote_copy(
                        src_ref=_rows(rins[i_buf].at[s], fwd), dst_ref=_rows(routs[t].at[s], fwd),
                        send_sem=send.at[k], recv_sem=recv.at[k], device_id=peers[0], device_id_type=MESH))
                    k += 1
        return remote, local

    return _Rider(ins, out_shape, n_remote, n_local, plan, aliases)


def _rider_ag_remote(shards):
    n = len(shards)

    def plan(ins, outs, send, recv, loc, r0, l0):
        x, y, c = _coords()
        remote = []
        for t in range(n):
            dst = outs[t].at[_slot((x, y, c))]
            for k, peer in enumerate([(x, y, 1 - c), (1 - x, y, c), (x, 1 - y, c), (1 - x, 1 - y, c)]):
                remote.append(pltpu.make_async_remote_copy(
                    src_ref=ins[t], dst_ref=dst, send_sem=send.at[r0 + 4 * t + k], recv_sem=recv.at[r0 + 4 * t + k],
                    device_id=peer, device_id_type=MESH))
        return remote, []

    return _Rider(shards, [jax.ShapeDtypeStruct((N_DEV,) + s.shape, s.dtype) for s in shards], 4 * n, 0, plan)


def _rider_rs_sibling(grads):
    n = len(grads)

    def plan(ins, outs, send, recv, loc, r0, l0):
        x, y, c = _coords()
        remote = []
        for t in range(n):
            for q in range(4):
                remote.append(pltpu.make_async_remote_copy(
                    src_ref=ins[t].at[q, 1 - c], dst_ref=outs[t].at[q], send_sem=send.at[r0 + 4 * t + q],
                    recv_sem=recv.at[r0 + 4 * t + q], device_id=(x, y, 1 - c), device_id_type=MESH))
        return remote, []

    return _Rider(grads, [jax.ShapeDtypeStruct((4,) + g.shape[2:], g.dtype) for g in grads], 4 * n, 0, plan)


def _rider_rs_chips(sums, rows=None, into=None):
    n = len(sums)
    rows = rows or [ALL] * n

    def plan(ins, outs, send, recv, loc, r0, l0):
        x, y, c = _coords()
        remote = []
        for t in range(n):
            for r, (px, py) in enumerate([(1 - x, y), (x, 1 - y), (1 - x, 1 - y)]):
                remote.append(pltpu.make_async_remote_copy(
                    src_ref=_rows(ins[t].at[2 * px + py], rows[t]), dst_ref=_rows(outs[t].at[r], rows[t]),
                    send_sem=send.at[r0 + 3 * t + r], recv_sem=recv.at[r0 + 3 * t + r],
                    device_id=(px, py, c), device_id_type=MESH))
        return remote, []

    out_shape = [jax.ShapeDtypeStruct((3,) + s.shape[1:], s.dtype) for s in sums]
    if into is None:
        return _Rider(sums, out_shape, 3 * n, 0, plan)
    return _Rider(list(sums) + list(into), out_shape, 3 * n, 0, plan, aliases={n + t: t for t in range(n)})


def _rider_gather_remote(parts):
    n = len(parts)

    def plan(ins, outs, send, recv, loc, r0, l0):
        x, y, c = _coords()
        me = _slot((x, y, c))
        remote = []
        for t in range(n):
            for k in range(1, N_DEV):
                peer = (x ^ ((k >> 2) & 1), y ^ ((k >> 1) & 1), c ^ (k & 1))
                remote.append(pltpu.make_async_remote_copy(
                    src_ref=ins[t], dst_ref=outs[t].at[me], send_sem=send.at[r0 + 7 * t + k - 1],
                    recv_sem=recv.at[r0 + 7 * t + k - 1], device_id=peer, device_id_type=MESH))
        return remote, []

    return _Rider(parts, [jax.ShapeDtypeStruct((N_DEV,) + p.shape, p.dtype) for p in parts], 7 * n, 0, plan)


def _chip_sum_call(idx, grads, recvd, out_dtypes, name):
    n = len(grads)

    def body(i_ref, *refs):
        for t in range(n):
            refs[2 * n + t][0] = (refs[t][0, 0] + refs[n + t][0]).astype(out_dtypes[t])

    def chip(k, s):
        return jnp.where(k >= s[0], k + 1, k)

    in_specs = [pl.BlockSpec((1, 1) + g.shape[2:], lambda k, s: (chip(k, s), s[1], 0, 0)) for g in grads]
    in_specs += [pl.BlockSpec((1,) + r.shape[1:], lambda k, s: (chip(k, s), 0, 0)) for r in recvd]
    return pl.pallas_call(
        body, name=name,
        grid_spec=pltpu.PrefetchScalarGridSpec(
            num_scalar_prefetch=1, grid=(3,), in_specs=in_specs,
            out_specs=[pl.BlockSpec((1,) + r.shape[1:], lambda k, s: (chip(k, s), 0, 0)) for r in recvd]),
        out_shape=[jax.ShapeDtypeStruct(r.shape, dt) for r, dt in zip(recvd, out_dtypes)],
        compiler_params=_params(("arbitrary",)),
    )(idx, *grads, *recvd)


def _final_sum_call(idx, grads, recvd1, recvd2):
    n = len(grads)
    nsteps = 2

    def body(i_ref, *refs):
        for t in range(n):
            g, r1, r2, o = refs[t], refs[n + t], refs[2 * n + t], refs[3 * n + t]
            s = g[0, 0] + r1[0]
            for r in range(3):
                s = s + r2[r].astype(F32)
            o[...] = s

    def rows(a):
        r = a.shape[-2]
        return r // nsteps if (r // nsteps) % 16 == 0 else r

    def step(a):
        return (lambda i: i) if rows(a) != a.shape[-2] else (lambda i: 0)

    in_specs = [pl.BlockSpec((1, 1, rows(g), g.shape[3]), lambda i, s, st=step(g): (s[0], s[1], st(i), 0)) for g in grads]
    in_specs += [pl.BlockSpec((1, rows(r), r.shape[2]), lambda i, s, st=step(r): (s[0], st(i), 0)) for r in recvd1]
    in_specs += [pl.BlockSpec((3, rows(r), r.shape[2]), lambda i, s, st=step(r): (0, st(i), 0)) for r in recvd2]
    return pl.pallas_call(
        body, name="rs_final_sum",
        grid_spec=pltpu.PrefetchScalarGridSpec(
            num_scalar_prefetch=1, grid=(nsteps,), in_specs=in_specs,
            out_specs=[pl.BlockSpec((rows(r), r.shape[2]), lambda i, s, st=step(r): (st(i), 0)) for r in recvd2]),
        out_shape=[jax.ShapeDtypeStruct(r.shape[1:], F32) for r in recvd2],
        compiler_params=_params(("arbitrary",)),
    )(idx, *grads, *recvd1, *recvd2)


def _sum8_call(parts):
    def body(p_ref, o_ref):
        s = p_ref[0]
        for j in range(1, N_DEV):
            s = s + p_ref[j]
        o_ref[...] = s

    return pl.pallas_call(body, name="sum_small_partials",
                          out_shape=jax.ShapeDtypeStruct(parts.shape[1:], parts.dtype))(parts)


def _adamw(w, g, m, v):
    m = ADAM_B1 * m + (1.0 - ADAM_B1) * g
    v = ADAM_B2 * v + (1.0 - ADAM_B2) * (g * g)
    m_hat = m / (1.0 - ADAM_B1 ** ADAM_STEP)
    v_hat = v / (1.0 - ADAM_B2 ** ADAM_STEP)
    delta = -ADAM_LR * (m_hat / (jnp.sqrt(v_hat) + ADAM_EPS) + ADAM_WD * w)
    return delta, m, v


def _adamw_call(ws, gs, ms, vs, nsteps, name):
    n = len(ws)

    def body(*refs):
        for t in range(n):
            w, g, m, v = (refs[k * n + t][...] for k in range(4))
            d, m2, v2 = _adamw(w, g, m, v)
            refs[4 * n + t][...] = d
            refs[5 * n + t][...] = m2
            refs[6 * n + t][...] = v2

    def spec(a):
        assert a.shape[0] % nsteps == 0 and (nsteps == 1 or (a.shape[0] // nsteps) % 8 == 0), a.shape
        return pl.BlockSpec((a.shape[0] // nsteps, a.shape[1]), lambda i: (i, 0))

    specs = [spec(a) for a in ws]
    outs = pl.pallas_call(
        body, name=name, grid=(nsteps,),
        in_specs=specs * 4, out_specs=specs * 3,
        out_shape=[jax.ShapeDtypeStruct(a.shape, F32) for a in ws] * 3,
        compiler_params=_params(("arbitrary",)),
    )(*ws, *gs, *ms, *vs)
    return outs[:n], outs[n:2 * n], outs[2 * n:]


def _adamw_rs_call(idx, after, gws, r1s, r2s, ws, ms, vs, nsteps, name):
    n = len(ws)

    def body(i_ref, after_ref, *refs):
        for t in range(n):
            gw, r1, r2, w, m, v = (refs[k * n + t] for k in range(6))
            g = gw[0, 0] + r1[0]
            for r in range(3):
                g = g + r2[r].astype(F32)
            d, m2, v2 = _adamw(w[...], g, m[...], v[...])
            refs[6 * n + t][...] = g
            refs[7 * n + t][...] = d
            refs[8 * n + t][...] = m2
            refs[9 * n + t][...] = v2

    def rb(a):
        r = a.shape[0] // nsteps
        assert a.shape[0] % nsteps == 0 and r % 16 == 0, a.shape
        return r

    in_specs = [pl.BlockSpec((1, 1, rb(w), w.shape[1]), lambda i, s: (s[0], s[1], i, 0)) for w in ws]
    in_specs += [pl.BlockSpec((1, rb(w), w.shape[1]), lambda i, s: (s[0], i, 0)) for w in ws]
    in_specs += [pl.BlockSpec((3, rb(w), w.shape[1]), lambda i, s: (0, i, 0)) for w in ws]
    plain = [pl.BlockSpec((rb(w), w.shape[1]), lambda i, s: (i, 0)) for w in ws]
    outs = pl.pallas_call(
        body, name=name,
        grid_spec=pltpu.PrefetchScalarGridSpec(
            num_scalar_prefetch=1, grid=(nsteps,),
            in_specs=[pl.BlockSpec(memory_space=pl.ANY)] + in_specs + plain * 3, out_specs=plain * 4),
        out_shape=[jax.ShapeDtypeStruct(w.shape, F32) for w in ws] * 4,
        compiler_params=_params(("arbitrary",)),
    )(idx, after, *gws, *r1s, *r2s, *ws, *ms, *vs)
    return outs[:n], outs[n:2 * n], outs[2 * n:3 * n], outs[3 * n:]


def _rows128(a, pad_rows):
    flat = a.reshape(-1).astype(F32)
    flat = jnp.pad(flat, (0, pad_rows * LANES - flat.shape[0]))
    return flat.reshape(pad_rows, LANES)


_SMALL_A = (("w_pool", 512), ("pool_scale", 8), ("attn_sinks", 8), ("g_mix_post", 8), ("g_mlp_pre", 8),
            ("g_mlp_post", 8), ("loss", 8), ("b_in_gates", 16))
_SMALL_A_ROWS = 640
_SMALL_B = (("g_mix_pre", 8), ("b_in_head", 16))


def _pack(parts, layout, total_rows):
    rows = [_rows128(parts[k], r) for k, r in layout]
    pad = total_rows - sum(r for _, r in layout)
    if pad:
        rows.append(jnp.zeros((pad, LANES), F32))
    return jnp.concatenate(rows, axis=0)


def _unpack(buf, layout, sizes):
    out, off = {}, 0
    for k, r in layout:
        out[k] = buf[off:off + r].reshape(-1)[:sizes[k]]
        off += r
    return out


def kernel(x, g_mix_pre, w_in, b_in, w_pool, pool_scale, attn_sinks, w_branch_pool, w_branch_attn, w_out, g_mix_post, g_mlp_pre, w_up, w_down, g_mlp_post, loss_target, m_g_mix_pre, m_w_in, m_b_in, m_w_pool, m_pool_scale, m_attn_sinks, m_w_branch_pool, m_w_branch_attn, m_w_out, m_g_mix_post, m_g_mlp_pre, m_w_up, m_w_down, m_g_mlp_post, v_g_mix_pre, v_w_in, v_b_in, v_w_pool, v_pool_scale, v_attn_sinks, v_w_branch_pool, v_w_branch_attn, v_w_out, v_g_mix_post, v_g_mlp_pre, v_w_up, v_w_down, v_g_mlp_post):
    B, S, _ = x.shape
    T = B * S
    xt = x.reshape(T, D_MODEL)
    tgt = loss_target.reshape(T, D_MODEL)
    cx, cy, cc = _coords()

    cidx = jnp.stack([2 * cx + cy, cc]).astype(jnp.int32)
    by_chip = lambda gr: gr.reshape((4, 2) + gr.shape[1:])
    bf = lambda w: w[0].astype(MXU_DTYPE)

    (win_s,) = _allgather_call([w_in[0].T.astype(MXU_DTYPE)])
    win_t = win_s.reshape(IN_WIDTH, D_MODEL)
    wpool_b = bf(w_pool)
    rc, rsa, rsb = _rot_tables(S)

    wbp_l, wba_l, wout_l, wup_l, wdown_l = bf(w_branch_pool), bf(w_branch_attn), bf(w_out), bf(w_up), bf(w_down)
    (c_br, c_up, c_dn), tok = _copies_start(
        [_rider_ag_remote([wbp_l, wba_l, wout_l]), _rider_ag_remote([wup_l]), _rider_ag_remote([wdown_l])],
        "allgather_start", after=win_s)
    (h, u, q, k4, v4, g), _ = _inproj_call(xt, g_mix_pre, win_t, b_in, rc, rsa, rsb, S, rider=_after(tok))
    yp = _pool_call(u, wpool_b, pool_scale, S)
    wbp_1, wba_1, wout_1 = _copies_wait([c_br], yp, "allgather_wait_branch")
    (ya,), (wbp_s, wba_s, wout_s) = _attn_call(
        attn_sinks, q, k4, v4, S,
        rider=_rider_ag([(wbp_l, wbp_1, LOCAL, ALL), (wba_l, wba_1, LOCAL, ALL), (wout_l, wout_1, LOCAL, ALL)]))
    wout_f = wout_s.reshape(D_MODEL, D_MODEL)
    (wup_1,) = _copies_wait([c_up], ya, "allgather_wait_up")
    (mix, x1, h2, h2_t), (wup_s,) = _mix_fwd_call(
        yp, ya, g, xt, wbp_s, wba_s, wout_f, g_mix_post, g_mlp_pre, rider=_rider_ag([(wup_l, wup_1, LOCAL, ALL)]))
    (wdown_1,) = _copies_wait([c_dn], h2, "allgather_wait_down")
    (wdown_s,) = _comm_call(_rider_ag([(wdown_l, wdown_1, LOCAL, ALL)]), "allgather_pass_down")

    act_t, da, dff, dx1, dg3, dg4, lossvec = _mlp_call(x1, h2, tgt, wup_s, wdown_s, g_mlp_pre, g_mlp_post)
    gw_down = by_chip(_wgrad_rows_call(act_t, dff, "wgrad_down")[0])
    (gw_up,), (r1_down,) = _wgrad_cols_call(h2_t, da, "wgrad_up", rider=_rider_rs_sibling([gw_down]))
    gw_up = by_chip(gw_up)
    (s_down,) = _chip_sum_call(cidx, [gw_down], [r1_down], [MXU_DTYPE], "rs_chip_sum_down")
    (c_down,), tok = _copies_start([_rider_rs_chips([s_down])], "rs_chips_start_down")
    (dyp, do, dgates, dg2, dbg, gw_out, gw_bp, gw_ba), (r1_up,) = _mix_bwd_call(
        dx1, mix, yp, ya, g, wbp_s, wba_s, wout_f, g_mix_post, rider=_after(tok, _rider_rs_sibling([gw_up])))
    gw_out = by_chip(gw_out.reshape(N_DEV, D_MODEL // N_DEV, D_MODEL))
    gw_bp, gw_ba = by_chip(gw_bp), by_chip(gw_ba)
    (s_up,) = _chip_sum_call(cidx, [gw_up], [r1_up], [MXU_DTYPE], "rs_chip_sum_up")
    (c_up,), tok = _copies_start([_rider_rs_chips([s_up])], "rs_chips_start_up")
    (dq, dk, dv, dsink), (r1_out, r1_bp, r1_ba) = _attn_bwd_call(
        attn_sinks, q, k4, v4, do, rc, rsa, rsb, S, rider=_after(tok, _rider_rs_sibling([gw_out, gw_bp, gw_ba])))
    s_obb = _chip_sum_call(cidx, [gw_out, gw_bp, gw_ba], [r1_out, r1_bp, r1_ba], [MXU_DTYPE] * 3, "rs_chip_sum_branch")
    (c_obb,), tok = _copies_start([_rider_rs_chips(s_obb)], "rs_chips_start_branch")
    (du, dwp, dps), _ = _pool_bwd_call(u, dyp, wpool_b, pool_scale, S, rider=_after(tok))
    (gw_in,) = _wgrad_in_call(du, dq, dk, dv, dgates, h)
    gw_in = by_chip(gw_in)

    small_a = {"w_pool": dwp, "pool_scale": dps,
               "attn_sinks": jnp.sum(dsink.reshape(B, 8, LANES)[:, 0, :N_Q_HEADS], axis=0), "g_mix_post": dg2,
               "g_mlp_pre": dg3, "g_mlp_post": dg4, "loss": lossvec, "b_in_gates": dbg}
    gw_sa = by_chip(_pack(small_a, _SMALL_A, _SMALL_A_ROWS).reshape(N_DEV, _SMALL_A_ROWS // N_DEV, LANES))
    r1_in, r1_sa = _comm_call(_rider_rs_sibling([gw_in, gw_sa]), "rs_sibling_in")
    s_in, s_sa = _chip_sum_call(cidx, [gw_in, gw_sa], [r1_in, r1_sa], [MXU_DTYPE, F32], "rs_chip_sum_in")
    (c_in,), tok = _copies_start([_rider_rs_chips([s_in, s_sa])], "rs_chips_start_in")
    (gx, dg1, dba_in), _ = _inproj_bwd_call(du, dq, dk, dv, dgates, dx1, xt, win_t, g_mix_pre, rider=_after(tok))
    r2_down, r2_up, r2_out, r2_bp, r2_ba, r2_in, r2_sa = _copies_wait([c_down, c_up, c_obb, c_in], dg1, "rs_chips_wait")

    (g_sa,) = _final_sum_call(cidx, [gw_sa], [r1_sa], [r2_sa])
    part_b = _pack({"g_mix_pre": dg1, "b_in_head": dba_in}, _SMALL_B, sum(r for _, r in _SMALL_B))
    (c_small,), tok = _copies_start([_rider_gather_remote([g_sa, part_b])], "allgather_small_start")

    in_t = _adamw_rs_call(cidx, tok, [gw_in], [r1_in], [r2_in], [w_in[0].T], [m_w_in[0].T], [v_w_in[0].T], 2,
                          "adamw_w_in")
    rest = _adamw_rs_call(
        cidx, tok, [gw_bp, gw_ba, gw_out, gw_up, gw_down], [r1_bp, r1_ba, r1_out, r1_up, r1_down],
        [r2_bp, r2_ba, r2_out, r2_up, r2_down], [w_branch_pool[0], w_branch_attn[0], w_out[0], w_up[0], w_down[0]],
        [m_w_branch_pool[0], m_w_branch_attn[0], m_w_out[0], m_w_up[0], m_w_down[0]],
        [v_w_branch_pool[0], v_w_branch_attn[0], v_w_out[0], v_w_up[0], v_w_down[0]], N_DEV, "adamw_shards")
    big_g, big_d, big_m2, big_v2 = ([a[0].T] + list(b) for a, b in zip(in_t, rest))

    sa_all, sb_all = _copies_wait([c_small], rest[0][0], "allgather_small_wait")
    me = (_slot((cx, cy, cc)), 0, 0)
    sa_all = lax.dynamic_update_slice(sa_all, g_sa[None], me)
    sb_sum = _sum8_call(lax.dynamic_update_slice(sb_all, part_b[None], me))

    names = ["g_mix_pre", "b_in", "w_pool", "pool_scale", "attn_sinks", "g_mix_post", "g_mlp_pre", "g_mlp_post"]
    sm_w = dict(g_mix_pre=g_mix_pre, b_in=b_in, w_pool=w_pool, pool_scale=pool_scale, attn_sinks=attn_sinks,
                g_mix_post=g_mix_post, g_mlp_pre=g_mlp_pre, g_mlp_post=g_mlp_post)
    sm_m = dict(g_mix_pre=m_g_mix_pre, b_in=m_b_in, w_pool=m_w_pool, pool_scale=m_pool_scale, attn_sinks=m_attn_sinks,
                g_mix_post=m_g_mix_post, g_mlp_pre=m_g_mlp_pre, g_mlp_post=m_g_mlp_post)
    sm_v = dict(g_mix_pre=v_g_mix_pre, b_in=v_b_in, w_pool=v_w_pool, pool_scale=v_pool_scale, attn_sinks=v_attn_sinks,
                g_mix_post=v_g_mix_post, g_mlp_pre=v_g_mlp_pre, g_mlp_post=v_g_mlp_post)
    sizes = {k: sm_w[k].size for k in names}
    sizes.update(loss=D_MODEL, b_in_gates=GATE_WIDTH, b_in_head=C_G)
    sm_g = _unpack(sa_all.reshape(_SMALL_A_ROWS, LANES), _SMALL_A, sizes)
    sm_g.update(_unpack(sb_sum, _SMALL_B, sizes))
    sm_g["b_in"] = jnp.concatenate([sm_g["b_in_head"], sm_g["b_in_gates"]])
    loss = (0.5 / D_MODEL) * jnp.sum(sm_g["loss"])
    two_d = lambda a: a.reshape(-1, a.shape[-1])
    sd_, sm2_, sv2_ = _adamw_call([two_d(sm_w[k]) for k in names], [two_d(sm_g[k].reshape(sm_w[k].shape)) for k in names],
                                  [two_d(sm_m[k]) for k in names], [two_d(sm_v[k]) for k in names], 1, "adamw_small")
    like = lambda vals: {k: a.reshape(sm_w[k].shape) for k, a in zip(names, vals)}
    sm_d, sm_m2, sm_v2 = like(sd_), like(sm2_), like(sv2_)
    sm_gr = {k: sm_g[k].reshape(sm_w[k].shape) for k in names}

    order = ["g_mix_pre", "w_in", "b_in", "w_pool", "pool_scale", "attn_sinks", "w_branch_pool", "w_branch_attn",
             "w_out", "g_mix_post", "g_mlp_pre", "w_up", "w_down", "g_mlp_post"]
    big_names = ["w_in", "w_branch_pool", "w_branch_attn", "w_out", "w_up", "w_down"]
    lead = lambda a: a[None]
    tables = []
    for small_t, big_t in ((sm_gr, big_g), (sm_d, big_d), (sm_m2, big_m2), (sm_v2, big_v2)):
        bt = dict(zip(big_names, big_t))
        tables.append([lead(bt[k]) if k in bt else small_t[k] for k in order])
    return (loss, gx.reshape(B, S, D_MODEL), *tables[0], *tables[1], *tables[2], *tables[3])
```

```python
import functools

import jax
import jax.numpy as jnp
from jax import lax
from jax.experimental import pallas as pl
from jax.experimental.pallas import tpu as pltpu

F32 = jnp.float32
MXU_DTYPE = jnp.bfloat16
MESH = pl.DeviceIdType.MESH

D_MODEL = 1024
POOL_WINDOWS = (2, 4, 8, 16)
POOL_WIDTH = 512
POOL_GC = 128
HEAD_DIM = 64
N_Q_HEADS = 8
N_KV_HEADS = 2
GROUP = 4
ATTN_WIDTH = 512
KV_WIDTH = 128
BLOCK = 128
GATE_WIDTH = 2048
IN_WIDTH = 3328
D_FF = 4096
EPS = 1e-6
NEG_INF = -1e30
ROPE_THETA = 500000.0
ROT_DIM = 16
SCALE = HEAD_DIM ** -0.5
C_Q, C_K, C_V, C_G = 512, 1024, 1152, 1280

ADAM_LR = 0.001
ADAM_B1 = 0.9
ADAM_B2 = 0.999
ADAM_EPS = 1e-08
ADAM_WD = 0.01
ADAM_STEP = 10

N_DEV = 8
LANES = 128
VMEM_LIMIT = 56 * 1024 * 1024

NN = (((1,), (0,)), ((), ()))
NT = (((1,), (1,)), ((), ()))
TN = (((0,), (0,)), ((), ()))


def _dot(a, b, dims):
    return lax.dot_general(a, b, dims, preferred_element_type=F32)


def _params(sem=None):
    return pltpu.CompilerParams(dimension_semantics=sem, vmem_limit_bytes=VMEM_LIMIT)


def _tile(n, pref):
    t = min(n, pref)
    assert n % t == 0, (n, t)
    return t


class _Rider:
    def __init__(self, ins, out_shape, n_remote, n_local, plan, aliases=None):
        self.ins, self.out_shape, self.n_remote, self.n_local = list(ins), list(out_shape), n_remote, n_local
        self.plan, self.aliases = plan, dict(aliases or {})


def _after(token, rider=None):
    r = rider or _Rider([], [], 0, 0, lambda ins, outs, send, recv, loc, r0, l0: ([], []))
    return _Rider(r.ins + [token], r.out_shape, r.n_remote, r.n_local, r.plan, r.aliases)


def _launch(body, args, *, name, grid, in_specs, out_specs, out_shape, scratch_shapes=(), sem=None, rider=None):
    if rider is None:
        return pl.pallas_call(body, name=name, grid=grid, in_specs=in_specs, out_specs=out_specs, out_shape=out_shape,
                              scratch_shapes=list(scratch_shapes), compiler_params=_params(sem))(*args)
    n_in, n_out, n_scr = len(args), len(out_shape), len(scratch_shapes)
    r_in, r_out = len(rider.ins), len(rider.out_shape)
    copies = rider.n_remote + rider.n_local > 0

    def wrapped(*refs):
        ins, rins = refs[:n_in], refs[n_in:n_in + r_in]
        o0 = n_in + r_in
        outs, routs = refs[o0:o0 + n_out], refs[o0 + n_out:o0 + n_out + r_out]
        s0 = o0 + n_out + r_out
        scr = refs[s0:s0 + n_scr]
        if not copies:
            return body(*ins, *outs, *scr)
        send, recv, loc = refs[s0 + n_scr:]
        first, last = None, None
        for d in range(len(grid)):
            f, l = pl.program_id(d) == 0, pl.program_id(d) == pl.num_programs(d) - 1
            first = f if first is None else first & f
            last = l if last is None else last & l

        def start():
            remote, local = rider.plan(rins, routs, send, recv, loc, 0, 0)
            for cp in local + remote:
                cp.start()

        def finish():
            remote, local = rider.plan(rins, routs, send, recv, loc, 0, 0)
            for cp in remote + local:
                cp.wait()

        if first is None:
            start()
            body(*ins, *outs, *scr)
            finish()
        else:
            pl.when(first)(start)
            body(*ins, *outs, *scr)
            pl.when(last)(finish)

    hbm = pl.BlockSpec(memory_space=pl.ANY)
    dma = pltpu.SemaphoreType.DMA
    res = pl.pallas_call(
        wrapped, name=name, grid=grid, in_specs=list(in_specs) + [hbm] * r_in,
        out_specs=list(out_specs) + [hbm] * r_out, out_shape=list(out_shape) + rider.out_shape,
        scratch_shapes=list(scratch_shapes) + (
            [dma((max(rider.n_remote, 1),)), dma((max(rider.n_remote, 1),)), dma((max(rider.n_local, 1),))] if copies else []),
        input_output_aliases={n_in + i: n_out + o for i, o in rider.aliases.items()},
        compiler_params=_params(sem),
    )(*args, *rider.ins)
    return list(res[:n_out]), list(res[n_out:])


def _comm_call(rider, name):
    return _launch(lambda: None, [], name=name, grid=(), in_specs=[], out_specs=[], out_shape=[], rider=rider)[1]


_HBM = pl.BlockSpec(memory_space=pltpu.HBM)
_SEM = pl.BlockSpec(memory_space=pltpu.SEMAPHORE)
_EFFECT = pltpu.SideEffectType.DATAFLOW_SIDE_EFFECTING


def _copies_start(riders, name, after=None):
    assert all(r.n_local == 0 and not r.aliases for r in riders)
    extra = [] if after is None else [after]
    sizes = [(len(r.ins), len(r.out_shape)) for r in riders]
    bufs = []
    for r in riders:
        bufs += [pltpu.with_memory_space_constraint(a, pltpu.HBM) for a in r.ins]
        bufs += [pltpu.with_memory_space_constraint(lax.empty(s.shape, s.dtype), pltpu.HBM) for s in r.out_shape]
    nb, ng, ne = len(bufs), len(riders), len(extra)

    def body(*refs):
        sems, token, at = refs[2 * nb + ne:2 * nb + ne + 2 * ng], refs[-1], 0
        for g, (r, (ni, no)) in enumerate(zip(riders, sizes)):
            remote, _ = r.plan(refs[at:at + ni], refs[at + ni:at + ni + no], sems[2 * g], sems[2 * g + 1], None, 0, 0)
            for cp in remote:
                cp.start()
            at += ni + no
        token[...] = jnp.zeros_like(token)

    res = pl.pallas_call(
        body, name=name, in_specs=[_HBM] * nb + [pl.BlockSpec(memory_space=pl.ANY)] * ne,
        out_specs=[_HBM] * nb + [_SEM] * (2 * ng) + [pl.BlockSpec(memory_space=pltpu.VMEM)],
        out_shape=[pltpu.HBM(a.shape, a.dtype) for a in bufs]
        + [pltpu.SemaphoreType.DMA((r.n_remote,)) for r in riders for _ in range(2)]
        + [jax.ShapeDtypeStruct((8, LANES), F32)],
        input_output_aliases={i: i for i in range(nb)},
        compiler_params=pltpu.CompilerParams(has_side_effects=_EFFECT),
    )(*bufs, *extra)
    handles, at = [], 0
    for g, (r, (ni, no)) in enumerate(zip(riders, sizes)):
        handles.append((r, list(res[at:at + ni + no]), res[nb + 2 * g], res[nb + 2 * g + 1]))
        at += ni + no
    return handles, res[-1]


def _copies_wait(handles, after, name):
    bufs = [b for _, bs, _, _ in handles for b in bs]
    sems = [s for _, _, send, recv in handles for s in (send, recv)]
    nb, ng = len(bufs), len(handles)

    def body(*refs):
        at = 0
        for g, (rider, bs, _, _) in enumerate(handles):
            ni = len(rider.ins)
            remote, _ = rider.plan(refs[at:at + ni], refs[at + ni:at + len(bs)], refs[nb + 2 * g], refs[nb + 2 * g + 1],
                                   None, 0, 0)
            for cp in remote:
                cp.wait_send()
                cp.wait_recv()
            at += len(bs)

    res = pl.pallas_call(
        body, name=name, in_specs=[_HBM] * nb + [_SEM] * (2 * ng) + [pl.BlockSpec(memory_space=pl.ANY)],
        out_specs=[_HBM] * nb, out_shape=[pltpu.HBM(a.shape, a.dtype) for a in bufs],
        input_output_aliases={i: i for i in range(nb)},
        compiler_params=pltpu.CompilerParams(has_side_effects=_EFFECT),
    )(*bufs, *sems, after)
    lands, at = [], 0
    for rider, bs, _, _ in handles:
        lands += list(res[at + len(rider.ins):at + len(bs)])
        at += len(bs)
    return lands


def _rms_r(x):
    return lax.rsqrt(jnp.mean(x * x, axis=-1, keepdims=True) + EPS)


def _rms_bwd(dn, x, r, g):
    xh = x * r
    dxh = dn * g
    dx = r * (dxh - xh * jnp.mean(dxh * xh, axis=-1, keepdims=True))
    return dx, dn * xh


def _rot(t, c, sa, sb):
    outs = []
    for j in range(t.shape[1] // LANES):
        tj = t[:, LANES * j:LANES * (j + 1)]
        outs.append(tj * c + pltpu.roll(tj, LANES - 8, 1) * sa + pltpu.roll(tj, 8, 1) * sb)
    return outs[0] if len(outs) == 1 else jnp.concatenate(outs, axis=1)


def _rot_tables(S):
    pos = jnp.arange(S, dtype=F32)
    inv_freq = ROPE_THETA ** (-jnp.arange(0, ROT_DIM, 2, dtype=F32) / ROT_DIM)
    ang = pos[:, None] * inv_freq[None, :]
    cos, sin = jnp.cos(ang), jnp.sin(ang)
    one = jnp.ones((S, HEAD_DIM - ROT_DIM), F32)
    zero = jnp.zeros((S, HEAD_DIM - ROT_DIM), F32)
    z8 = jnp.zeros((S, 8), F32)
    c = jnp.concatenate([cos, cos, one], axis=1)
    sa = jnp.concatenate([-sin, z8, zero], axis=1)
    sb = jnp.concatenate([z8, sin, zero], axis=1)
    rep = LANES // HEAD_DIM
    return jnp.tile(c, (1, rep)), jnp.tile(sa, (1, rep)), jnp.tile(sb, (1, rep))


def _lane_tile4(k):
    lane = lax.broadcasted_iota(jnp.int32, k.shape, 1)
    rk = pltpu.roll(k, HEAD_DIM, 1)
    x0 = jnp.where(lane < HEAD_DIM, k, rk)
    x1 = jnp.where(lane < HEAD_DIM, rk, k)
    return jnp.concatenate([x0, x0, x1, x1], axis=1)


def _fold_heads(acc):
    zs = []
    for hk in range(N_KV_HEADS):
        a = acc[:, 256 * hk:256 * hk + LANES] + acc[:, 256 * hk + LANES:256 * (hk + 1)]
        zs.append(a + pltpu.roll(a, HEAD_DIM, 1))
    lane = lax.broadcasted_iota(jnp.int32, zs[0].shape, 1)
    return jnp.where(lane < HEAD_DIM, zs[0], zs[1])


def _inproj_call(x, g1, win_t, b_in, rc, rsa, rsb, S, rider=None):
    T = x.shape[0]
    tm = _tile(S, 512)
    nst = S // tm

    def body(x_ref, g1_ref, w_ref, b_ref, c_ref, sa_ref, sb_ref,
             h_ref, u_ref, q_ref, k4_ref, v4_ref, g_ref):
        xv = x_ref[...]
        hb = ((xv * _rms_r(xv)) * g1_ref[...]).astype(MXU_DTYPE)
        h_ref[...] = hb

        def proj(lo, hi):
            return _dot(hb, w_ref[lo:hi, :], NT) + b_ref[:, lo:hi]

        c, sa, sb = c_ref[...], sa_ref[...], sb_ref[...]
        u_ref[...] = proj(0, C_Q)
        q_ref[...] = (_rot(proj(C_Q, C_K), c, sa, sb) * SCALE).astype(MXU_DTYPE)
        kv = proj(C_K, C_G)
        k4_ref[...] = _lane_tile4(_rot(kv[:, :KV_WIDTH], c, sa, sb)).astype(MXU_DTYPE)
        v4_ref[...] = _lane_tile4(kv[:, KV_WIDTH:]).astype(MXU_DTYPE)
        g_ref[...] = jax.nn.sigmoid(proj(C_G, IN_WIDTH)).astype(MXU_DTYPE)

    tok = lambda w: pl.BlockSpec((tm, w), lambda i: (i, 0))
    full = lambda a: pl.BlockSpec(a.shape, lambda i: (0,) * a.ndim)
    tab = pl.BlockSpec((tm, LANES), lambda i: (i % nst, 0))
    return _launch(
        body, [x, g1, win_t, b_in, rc, rsa, rsb], name="inproj_fwd", grid=(T // tm,),
        in_specs=[tok(D_MODEL), full(g1), full(win_t), full(b_in), tab, tab, tab],
        out_specs=[tok(D_MODEL), tok(POOL_WIDTH), tok(ATTN_WIDTH), tok(512), tok(512), tok(GATE_WIDTH)],
        out_shape=[jax.ShapeDtypeStruct((T, D_MODEL), MXU_DTYPE), jax.ShapeDtypeStruct((T, POOL_WIDTH), F32),
                   jax.ShapeDtypeStruct((T, ATTN_WIDTH), MXU_DTYPE), jax.ShapeDtypeStruct((T, 512), MXU_DTYPE),
                   jax.ShapeDtypeStruct((T, 512), MXU_DTYPE), jax.ShapeDtypeStruct((T, GATE_WIDTH), MXU_DTYPE)],
        sem=("arbitrary",), rider=rider)


def _shift_rows(a, k, rows):
    n = a.shape[0]
    if k > 0:
        return jnp.where(rows >= k, pltpu.roll(a, k, 0), 0.0)
    return jnp.where(rows < n + k, pltpu.roll(a, n + k, 0), 0.0)


def _win_sum(a, w, rows, sign):
    s, k = a, 1
    while k < w:
        s = s + _shift_rows(s, sign * k, rows)
        k *= 2
    return s


def _pool_diff(ug, w, rows):
    inv = 1.0 / jnp.minimum(rows + 1, w).astype(F32)
    return _win_sum(ug, w, rows, 1) * inv - ug, inv


def _pool_call(u, w_pool, pool_scale, S):
    T = u.shape[0]

    def body(u_ref, w_ref, ps_ref, y_ref):
        rows = lax.broadcasted_iota(jnp.int32, (S, POOL_GC), 0)
        for gi, w in enumerate(POOL_WINDOWS):
            sl = slice(POOL_GC * gi, POOL_GC * (gi + 1))
            diff, _ = _pool_diff(u_ref[:, sl], w, rows)
            mixed = _dot(diff.astype(MXU_DTYPE), w_ref[gi], NN)
            y_ref[:, sl] = (mixed * ps_ref[:, sl]).astype(MXU_DTYPE)

    seq = pl.BlockSpec((S, POOL_WIDTH), lambda b: (b, 0))
    return pl.pallas_call(
        body, name="pool_fwd", grid=(T // S,),
        in_specs=[seq, pl.BlockSpec(w_pool.shape, lambda b: (0, 0, 0)), pl.BlockSpec(pool_scale.shape, lambda b: (0, 0))],
        out_specs=seq, out_shape=jax.ShapeDtypeStruct((T, POOL_WIDTH), MXU_DTYPE),
        compiler_params=_params(("arbitrary",)),
    )(u, w_pool, pool_scale)


def _pool_bwd_call(u, dyp, w_pool, pool_scale, S, rider=None):
    T = u.shape[0]

    def body(u_ref, dy_ref, w_ref, ps_ref, du_ref, dw_ref, dps_ref):
        @pl.when(pl.program_id(0) == 0)
        def _():
            dw_ref[...] = jnp.zeros_like(dw_ref)
            dps_ref[...] = jnp.zeros_like(dps_ref)

        rows = lax.broadcasted_iota(jnp.int32, (S, POOL_GC), 0)
        for gi, w in enumerate(POOL_WINDOWS):
            sl = slice(POOL_GC * gi, POOL_GC * (gi + 1))
            diff, inv = _pool_diff(u_ref[:, sl], w, rows)
            diffb = diff.astype(MXU_DTYPE)
            wg = w_ref[gi]
            mixed = _dot(diffb, wg, NN)
            dy = dy_ref[:, sl]
            dps_ref[:, sl] += jnp.sum(dy * mixed, axis=0, keepdims=True)
            dmb = (dy * ps_ref[:, sl]).astype(MXU_DTYPE)
            dw_ref[gi] += _dot(diffb, dmb, TN)
            ddiff = _dot(dmb, wg, NT)
            du_ref[:, sl] = (_win_sum(ddiff * inv, w, rows, -1) - ddiff).astype(MXU_DTYPE)

    seq = pl.BlockSpec((S, POOL_WIDTH), lambda b: (b, 0))
    return _launch(
        body, [u, dyp, w_pool, pool_scale], name="pool_bwd", grid=(T // S,),
        in_specs=[seq, seq, pl.BlockSpec(w_pool.shape, lambda b: (0, 0, 0)), pl.BlockSpec(pool_scale.shape, lambda b: (0, 0))],
        out_specs=[seq, pl.BlockSpec(w_pool.shape, lambda b: (0, 0, 0)), pl.BlockSpec(pool_scale.shape, lambda b: (0, 0))],
        out_shape=[jax.ShapeDtypeStruct((T, POOL_WIDTH), MXU_DTYPE), jax.ShapeDtypeStruct(w_pool.shape, F32),
                   jax.ShapeDtypeStruct(pool_scale.shape, F32)],
        sem=("arbitrary",), rider=rider)


def _attn_consts():
    lane_g = lax.broadcasted_iota(jnp.int32, (BLOCK, 256), 1) >> 6
    rgrp = lax.broadcasted_iota(jnp.int32, (GROUP * BLOCK, 1), 0) >> 7
    rel = lax.broadcasted_iota(jnp.int32, (BLOCK, 256), 0) - lax.broadcasted_iota(jnp.int32, (BLOCK, 256), 1)

    def bias(off):
        ok = (rel + off >= 0) & (rel + off < BLOCK)
        return jnp.concatenate([jnp.where(ok, 0.0, NEG_INF)] * GROUP, axis=0)

    return lane_g, rgrp, bias(0), bias(BLOCK)


def _sink_rows(sink_ref, hk, rgrp):
    sv = jnp.zeros(rgrp.shape, F32)
    for g in range(GROUP):
        sv = jnp.where(rgrp == g, sink_ref[0, GROUP * hk + g], sv)
    return sv


def _stack_heads(xb, lane_g):
    return jnp.concatenate([jnp.where(lane_g == g, xb, jnp.zeros_like(xb)) for g in range(GROUP)], axis=0)


def _unstack_heads(xs, lane_g):
    out = jnp.where(lane_g == 0, xs[0:BLOCK], 0.0)
    for g in range(1, GROUP):
        out = out + jnp.where(lane_g == g, xs[BLOCK * g:BLOCK * (g + 1)], 0.0)
    return out


def _attn_probs(qs, kb, bias, sv):
    s = _dot(qs, kb, NT) + bias
    m = jnp.maximum(jnp.max(s, axis=1, keepdims=True), sv)
    e = jnp.exp(s - m)
    es = jnp.exp(sv - m)
    inv_l = 1.0 / (jnp.sum(e, axis=1, keepdims=True) + es)
    return e * inv_l, es * inv_l


def _attn_blocks(nb, blk, carry, per=1):
    carry = blk(0, 0, True, carry)
    per = per if (nb - 1) % per == 0 else 1

    def step(i, c):
        for k in range(per):
            n = 1 + per * i + k
            c = blk(pl.multiple_of(n * BLOCK, BLOCK), pl.multiple_of((n - 1) * BLOCK, BLOCK), False, c)
        return c

    return lax.fori_loop(0, (nb - 1) // per, step, carry)


def _attn_call(sinks, q, k4, v4, S, rider=None):
    T = q.shape[0]
    nb = S // BLOCK

    def body(sink_ref, q_ref, k_ref, v_ref, o_ref):
        lane_g, rgrp, bias_first, bias_later = _attn_consts()
        svs = [_sink_rows(sink_ref, hk, rgrp) for hk in range(N_KV_HEADS)]

        def blk(q0, k0, first, carry):
            for hk in range(N_KV_HEADS):
                cs = slice(256 * hk, 256 * (hk + 1))
                qs = _stack_heads(q_ref[pl.ds(q0, BLOCK), cs], lane_g)
                p, _ = _attn_probs(qs, k_ref[pl.ds(k0, 2 * BLOCK), cs], bias_first if first else bias_later, svs[hk])
                o = _dot(p.astype(MXU_DTYPE), v_ref[pl.ds(k0, 2 * BLOCK), cs], NN)
                o_ref[pl.ds(q0, BLOCK), cs] = _unstack_heads(o, lane_g).astype(MXU_DTYPE)
            return carry

        _attn_blocks(nb, blk, 0, per=3)

    seq = pl.BlockSpec((S, ATTN_WIDTH), lambda b: (b, 0))
    return _launch(
        body, [sinks, q, k4, v4], name="attn_fwd", grid=(T // S,),
        in_specs=[pl.BlockSpec(memory_space=pltpu.SMEM), seq, seq, seq],
        out_specs=[seq], out_shape=[jax.ShapeDtypeStruct((T, ATTN_WIDTH), MXU_DTYPE)],
        sem=("arbitrary",), rider=rider)


def _attn_bwd_call(sinks, q, k4, v4, do, rc, rsa, rsb, S, rider=None):
    T = q.shape[0]
    nb = S // BLOCK

    def body(sink_ref, q_ref, k_ref, v_ref, do_ref, c_ref, sa_ref, sb_ref,
             dq_ref, dk_ref, dv_ref, ds_ref, dk_acc, dv_acc):
        lane_g, rgrp, bias_first, bias_later = _attn_consts()
        svs = [_sink_rows(sink_ref, hk, rgrp) for hk in range(N_KV_HEADS)]
        lane1 = lax.broadcasted_iota(jnp.int32, (1, LANES), 1)
        dk_acc[...] = jnp.zeros_like(dk_acc)
        dv_acc[...] = jnp.zeros_like(dv_acc)

        def blk(q0, k0, first, dsink):
            rows = pl.ds(q0, BLOCK)
            c, sa, sb = c_ref[rows, :], sa_ref[rows, :], sb_ref[rows, :]
            for hk in range(N_KV_HEADS):
                cs = slice(256 * hk, 256 * (hk + 1))
                qs = _stack_heads(q_ref[rows, cs], lane_g)
                dos = _stack_heads(do_ref[rows, cs], lane_g)
                kb = k_ref[pl.ds(k0, 2 * BLOCK), cs]
                vb = v_ref[pl.ds(k0, 2 * BLOCK), cs]
                p, ps = _attn_probs(qs, kb, bias_first if first else bias_later, svs[hk])
                dp = _dot(dos, vb, NT)
                delta = jnp.sum(p * dp, axis=1, keepdims=True)
                dsb = (p * (dp - delta)).astype(MXU_DTYPE)
                dqb = _unstack_heads(_dot(dsb, kb, NN), lane_g) * SCALE
                dq_ref[rows, cs] = _rot(dqb, c, -sa, -sb).astype(MXU_DTYPE)
                dk_acc[pl.ds(k0, 2 * BLOCK), cs] += _dot(dsb, qs, TN)
                dv_acc[pl.ds(k0, 2 * BLOCK), cs] += _dot(p.astype(MXU_DTYPE), dos, TN)
                psd = ps * delta
                for g in range(GROUP):
                    val = -jnp.sum(psd[BLOCK * g:BLOCK * (g + 1)], axis=0, keepdims=True)
                    dsink = dsink + jnp.where(lane1 == GROUP * hk + g, val, 0.0)
            return dsink

        dsink = _attn_blocks(nb, blk, jnp.zeros((1, LANES), F32))
        dk_ref[...] = _rot(_fold_heads(dk_acc[...]), c_ref[...], -sa_ref[...], -sb_ref[...]).astype(MXU_DTYPE)
        dv_ref[...] = _fold_heads(dv_acc[...]).astype(MXU_DTYPE)
        ds_ref[...] = jnp.broadcast_to(dsink, ds_ref.shape)

    seq = pl.BlockSpec((S, ATTN_WIDTH), lambda b: (b, 0))
    kvs = pl.BlockSpec((S, KV_WIDTH), lambda b: (b, 0))
    tab = pl.BlockSpec((S, LANES), lambda b: (0, 0))
    nseq = T // S
    return _launch(
        body, [sinks, q, k4, v4, do, rc, rsa, rsb], name="attn_bwd", grid=(nseq,),
        in_specs=[pl.BlockSpec(memory_space=pltpu.SMEM), seq, seq, seq, seq, tab, tab, tab],
        out_specs=[seq, kvs, kvs, pl.BlockSpec((8, LANES), lambda b: (b, 0))],
        out_shape=[jax.ShapeDtypeStruct((T, ATTN_WIDTH), MXU_DTYPE), jax.ShapeDtypeStruct((T, KV_WIDTH), MXU_DTYPE),
                   jax.ShapeDtypeStruct((T, KV_WIDTH), MXU_DTYPE), jax.ShapeDtypeStruct((8 * nseq, LANES), F32)],
        scratch_shapes=[pltpu.VMEM((S, 512), F32), pltpu.VMEM((S, 512), F32)],
        sem=("arbitrary",), rider=rider)


def _branch_weights(wbp_ref, wba_ref, wbp_s, wba_s):
    @pl.when(pl.program_id(0) == 0)
    def _():
        for j in range(N_DEV):
            wbp_s[:, LANES * j:LANES * (j + 1)] = wbp_ref[j]
            wba_s[:, LANES * j:LANES * (j + 1)] = wba_ref[j]


def _mix_fwd_call(yp, ya, g, x, wbp, wba, wout, g2, g3, rider=None):
    T = x.shape[0]
    tm = _tile(T, 512)

    def body(yp_ref, ya_ref, g_ref, x_ref, wbp_ref, wba_ref, wout_ref, g2_ref, g3_ref,
             mix_ref, x1_ref, h2_ref, h2t_ref, wbp_s, wba_s):
        _branch_weights(wbp_ref, wba_ref, wbp_s, wba_s)
        bp = _dot(yp_ref[...], wbp_s[...], NN)
        ba = _dot(ya_ref[...], wba_s[...], NN)
        merged = g_ref[:, :D_MODEL].astype(F32) * bp + g_ref[:, D_MODEL:].astype(F32) * ba
        mix = _dot(merged.astype(MXU_DTYPE), wout_ref[...], NN)
        mix_ref[...] = mix
        x1 = x_ref[...] + (mix * _rms_r(mix)) * g2_ref[...]
        x1_ref[...] = x1
        h2 = (x1 * _rms_r(x1)) * g3_ref[...]
        h2_ref[...] = h2.astype(MXU_DTYPE)
        h2t_ref[...] = h2.T.astype(MXU_DTYPE)

    tok = lambda w: pl.BlockSpec((tm, w), lambda i: (i, 0))
    full = lambda a: pl.BlockSpec(a.shape, lambda i: (0,) * a.ndim)
    return _launch(
        body, [yp, ya, g, x, wbp, wba, wout, g2, g3], name="mix_fwd", grid=(T // tm,),
        in_specs=[tok(POOL_WIDTH), tok(ATTN_WIDTH), tok(GATE_WIDTH), tok(D_MODEL), full(wbp), full(wba), full(wout),
                  full(g2), full(g3)],
        out_specs=[tok(D_MODEL), tok(D_MODEL), tok(D_MODEL), pl.BlockSpec((D_MODEL, tm), lambda i: (0, i))],
        out_shape=[jax.ShapeDtypeStruct((T, D_MODEL), F32), jax.ShapeDtypeStruct((T, D_MODEL), F32),
                   jax.ShapeDtypeStruct((T, D_MODEL), MXU_DTYPE), jax.ShapeDtypeStruct((D_MODEL, T), MXU_DTYPE)],
        scratch_shapes=[pltpu.VMEM((POOL_WIDTH, D_MODEL), MXU_DTYPE), pltpu.VMEM((ATTN_WIDTH, D_MODEL), MXU_DTYPE)],
        sem=("arbitrary",), rider=rider)


def _mix_bwd_call(dx1, mix, yp, ya, g, wbp, wba, wout, g2, rider=None):
    T = dx1.shape[0]
    tm = _tile(T, 512)

    def body(dx1_ref, mix_ref, yp_ref, ya_ref, g_ref, wbp_ref, wba_ref, wout_ref, g2_ref,
             dyp_ref, do_ref, dgates_ref, dg2_ref, dbg_ref, gout_ref, gbp_ref, gba_ref,
             wbp_s, wba_s, acc_out, acc_bp, acc_ba, sem):
        _branch_weights(wbp_ref, wba_ref, wbp_s, wba_s)
        step = pl.program_id(0)

        @pl.when(step == 0)
        def _():
            dg2_ref[...] = jnp.zeros_like(dg2_ref)
            dbg_ref[...] = jnp.zeros_like(dbg_ref)
            acc_out[...] = jnp.zeros_like(acc_out)
            acc_bp[...] = jnp.zeros_like(acc_bp)
            acc_ba[...] = jnp.zeros_like(acc_ba)

        mix = mix_ref[...]
        dmix, dg2 = _rms_bwd(dx1_ref[...], mix, _rms_r(mix), g2_ref[...])
        dg2_ref[...] += jnp.sum(dg2, axis=0, keepdims=True)
        dmixb = dmix.astype(MXU_DTYPE)
        dmerged = _dot(dmixb, wout_ref[...], NT)
        yp, ya = yp_ref[...], ya_ref[...]
        bp = _dot(yp, wbp_s[...], NN)
        ba = _dot(ya, wba_s[...], NN)
        gp, ga = g_ref[:, :D_MODEL].astype(F32), g_ref[:, D_MODEL:].astype(F32)
        acc_out[...] += _dot((gp * bp + ga * ba).astype(MXU_DTYPE), dmixb, TN)
        dgp = dmerged * bp * (gp * (1.0 - gp))
        dga = dmerged * ba * (ga * (1.0 - ga))
        dbg_ref[:, :D_MODEL] += jnp.sum(dgp, axis=0, keepdims=True)
        dbg_ref[:, D_MODEL:] += jnp.sum(dga, axis=0, keepdims=True)
        dgates_ref[:, :D_MODEL] = dgp.astype(MXU_DTYPE)
        dgates_ref[:, D_MODEL:] = dga.astype(MXU_DTYPE)
        dbp = (dmerged * gp).astype(MXU_DTYPE)
        dba = (dmerged * ga).astype(MXU_DTYPE)
        acc_bp[...] += _dot(yp, dbp, TN)
        acc_ba[...] += _dot(ya, dba, TN)
        dyp_ref[...] = _dot(dbp, wbp_s[...], NT)
        do_ref[...] = _dot(dba, wba_s[...], NT).astype(MXU_DTYPE)

        @pl.when(step == pl.num_programs(0) - 1)
        def _():
            copies = [pltpu.make_async_copy(acc_out, gout_ref, sem.at[0])]
            for j in range(N_DEV):
                cols = slice(LANES * j, LANES * (j + 1))
                copies.append(pltpu.make_async_copy(acc_bp.at[:, cols], gbp_ref.at[j], sem.at[1 + j]))
                copies.append(pltpu.make_async_copy(acc_ba.at[:, cols], gba_ref.at[j], sem.at[1 + N_DEV + j]))
            for cp in copies:
                cp.start()
            for cp in copies:
                cp.wait()

    tok = lambda w: pl.BlockSpec((tm, w), lambda i: (i, 0))
    full = lambda a: pl.BlockSpec(a.shape, lambda i: (0,) * a.ndim)
    acc = lambda w: pl.BlockSpec((1, w), lambda i: (0, 0))
    hbm = pl.BlockSpec(memory_space=pl.ANY)
    sd = jax.ShapeDtypeStruct
    return _launch(
        body, [dx1, mix, yp, ya, g, wbp, wba, wout, g2], name="mix_bwd", grid=(T // tm,),
        in_specs=[tok(D_MODEL), tok(D_MODEL), tok(POOL_WIDTH), tok(ATTN_WIDTH), tok(GATE_WIDTH), full(wbp), full(wba),
                  full(wout), full(g2)],
        out_specs=[tok(POOL_WIDTH), tok(ATTN_WIDTH), tok(GATE_WIDTH), acc(D_MODEL), acc(GATE_WIDTH), hbm, hbm, hbm],
        out_shape=[sd((T, POOL_WIDTH), F32), sd((T, ATTN_WIDTH), MXU_DTYPE), sd((T, GATE_WIDTH), MXU_DTYPE),
                   sd((1, D_MODEL), F32), sd((1, GATE_WIDTH), F32), sd((D_MODEL, D_MODEL), F32),
                   sd((N_DEV, POOL_WIDTH, LANES), F32), sd((N_DEV, ATTN_WIDTH, LANES), F32)],
        scratch_shapes=[pltpu.VMEM((POOL_WIDTH, D_MODEL), MXU_DTYPE), pltpu.VMEM((ATTN_WIDTH, D_MODEL), MXU_DTYPE),
                        pltpu.VMEM((D_MODEL, D_MODEL), F32), pltpu.VMEM((POOL_WIDTH, D_MODEL), F32),
                        pltpu.VMEM((ATTN_WIDTH, D_MODEL), F32), pltpu.SemaphoreType.DMA((1 + 2 * N_DEV,))],
        sem=("arbitrary",), rider=rider)


def _mlp_call(x1, h2, target, wup, wdown, g3, g4):
    T = x1.shape[0]
    tm = _tile(T, 256)
    fc = D_FF // N_DEV

    def body(x1_ref, h2_ref, t_ref, wup_ref, wdown_ref, g3_ref, g4_ref,
             act_ref, da_ref, dff_ref, dx1_ref, dg3_ref, dg4_ref, loss_ref, rl_s):
        @pl.when(pl.program_id(0) == 0)
        def _():
            dg3_ref[...] = jnp.zeros_like(dg3_ref)
            dg4_ref[...] = jnp.zeros_like(dg4_ref)
            loss_ref[...] = jnp.zeros_like(loss_ref)

        h2 = h2_ref[...]
        ff = jnp.zeros((tm, D_MODEL), F32)
        for j in range(N_DEV):
            sl = slice(fc * j, fc * (j + 1))
            rl = jnp.maximum(_dot(h2, wup_ref[j], NN), 0.0)
            rl_s[:, sl] = rl
            act = rl * rl
            act_ref[sl, :] = act.T.astype(MXU_DTYPE)
            ff = ff + _dot(act.astype(MXU_DTYPE), wdown_ref[j], NN)
        x1 = x1_ref[...]
        r4 = _rms_r(ff)
        err = x1 + (ff * r4) * g4_ref[...] - t_ref[...]
        loss_ref[...] += jnp.sum(err * err, axis=0, keepdims=True)
        dy = err * (1.0 / D_MODEL)
        dff, dg4 = _rms_bwd(dy, ff, r4, g4_ref[...])
        dg4_ref[...] += jnp.sum(dg4, axis=0, keepdims=True)
        dffb = dff.astype(MXU_DTYPE)
        dff_ref[...] = dffb
        dh2 = jnp.zeros((tm, D_MODEL), F32)
        for j in range(N_DEV):
            sl = slice(fc * j, fc * (j + 1))
            dab = (_dot(dffb, wdown_ref[j], NT) * (2.0 * rl_s[:, sl])).astype(MXU_DTYPE)
            da_ref[:, sl] = dab
            dh2 = dh2 + _dot(dab, wup_ref[j], NT)
        dx1, dg3 = _rms_bwd(dh2, x1, _rms_r(x1), g3_ref[...])
        dg3_ref[...] += jnp.sum(dg3, axis=0, keepdims=True)
        dx1_ref[...] = dy + dx1

    tok = lambda w: pl.BlockSpec((tm, w), lambda i: (i, 0))
    full = lambda a: pl.BlockSpec(a.shape, lambda i: (0,) * a.ndim, pipeline_mode=pl.Buffered(1))
    vec = pl.BlockSpec((1, D_MODEL), lambda i: (0, 0))
    sd = jax.ShapeDtypeStruct
    return pl.pallas_call(
        body, name="mlp_fwd_bwd", grid=(T // tm,),
        in_specs=[tok(D_MODEL), tok(D_MODEL), tok(D_MODEL), full(wup), full(wdown), vec, vec],
        out_specs=[pl.BlockSpec((D_FF, tm), lambda i: (0, i)), tok(D_FF), tok(D_MODEL), tok(D_MODEL), vec, vec, vec],
        out_shape=[sd((D_FF, T), MXU_DTYPE), sd((T, D_FF), MXU_DTYPE), sd((T, D_MODEL), MXU_DTYPE),
                   sd((T, D_MODEL), F32), sd((1, D_MODEL), F32), sd((1, D_MODEL), F32), sd((1, D_MODEL), F32)],
        scratch_shapes=[pltpu.VMEM((tm, D_FF), F32)],
        compiler_params=_params(("arbitrary",)),
    )(x1, h2, target, wup, wdown, g3, g4)


def _inproj_bwd_call(du, dq, dk, dv, dgates, dx1, x, win_t, g1, rider=None):
    T = x.shape[0]
    tm = _tile(T, 512)

    def body(du_ref, dq_ref, dk_ref, dv_ref, dgt_ref, dx1_ref, x_ref, w_ref, g1_ref, gx_ref, dg1_ref, db_ref):
        @pl.when(pl.program_id(0) == 0)
        def _():
            dg1_ref[...] = jnp.zeros_like(dg1_ref)
            db_ref[...] = jnp.zeros_like(db_ref)

        dh = jnp.zeros((tm, D_MODEL), F32)
        for ref, lo, hi in ((du_ref, 0, C_Q), (dq_ref, C_Q, C_K), (dk_ref, C_K, C_V), (dv_ref, C_V, C_G),
                            (dgt_ref, C_G, IN_WIDTH)):
            piece = ref[...]
            dh = dh + _dot(piece, w_ref[lo:hi, :], NN)
            if hi <= C_G:
                db_ref[:, lo:hi] += jnp.sum(piece.astype(F32), axis=0, keepdims=True)
        xv = x_ref[...]
        dx, dg1 = _rms_bwd(dh, xv, _rms_r(xv), g1_ref[...])
        dg1_ref[...] += jnp.sum(dg1, axis=0, keepdims=True)
        gx_ref[...] = dx1_ref[...] + dx

    tok = lambda w: pl.BlockSpec((tm, w), lambda i: (i, 0))
    full = lambda a: pl.BlockSpec(a.shape, lambda i: (0,) * a.ndim)
    sd = jax.ShapeDtypeStruct
    return _launch(
        body, [du, dq, dk, dv, dgates, dx1, x, win_t, g1], name="inproj_bwd", grid=(T // tm,),
        in_specs=[tok(POOL_WIDTH), tok(ATTN_WIDTH), tok(KV_WIDTH), tok(KV_WIDTH), tok(GATE_WIDTH), tok(D_MODEL),
                  tok(D_MODEL), full(win_t), full(g1)],
        out_specs=[tok(D_MODEL), pl.BlockSpec((1, D_MODEL), lambda i: (0, 0)), pl.BlockSpec((1, C_G), lambda i: (0, 0))],
        out_shape=[sd((T, D_MODEL), F32), sd((1, D_MODEL), F32), sd((1, C_G), F32)],
        sem=("arbitrary",), rider=rider)


WGRAD_TOKENS = 1024


def _wgrad_rows_call(at, b, name, rider=None):
    K, T = at.shape
    N = b.shape[1]
    tm = _tile(T, WGRAD_TOKENS)
    kb = min(K, 1024)
    per = kb // (K // N_DEV)

    def body(a_ref, b_ref, o_ref):
        @pl.when(pl.program_id(1) == 0)
        def _():
            o_ref[...] = jnp.zeros_like(o_ref)

        d = _dot(a_ref[...], b_ref[...], NN)
        rs = kb // per
        for j in range(per):
            o_ref[j] += d[rs * j:rs * (j + 1)]

    return _launch(
        body, [at, b], name=name, grid=(K // kb, T // tm),
        in_specs=[pl.BlockSpec((kb, tm), lambda i, t: (i, t)), pl.BlockSpec((tm, N), lambda i, t: (t, 0))],
        out_specs=[pl.BlockSpec((per, K // N_DEV, N), lambda i, t: (i, 0, 0))],
        out_shape=[jax.ShapeDtypeStruct((N_DEV, K // N_DEV, N), F32)],
        sem=("arbitrary", "arbitrary"), rider=rider)


def _wgrad_cols_call(at, b, name, rider=None):
    K, T = at.shape
    N = b.shape[1]
    tm = _tile(T, WGRAD_TOKENS)
    nb = min(N, 1024)
    per = nb // (N // N_DEV)

    def body(a_ref, b_ref, o_ref, ob_ref):
        @pl.when(pl.program_id(1) == 0)
        def _():
            o_ref[...] = jnp.zeros_like(o_ref)

        d = _dot(a_ref[...], b_ref[...], NN)
        cs = nb // per
        for j in range(per):
            o_ref[j] += d[:, cs * j:cs * (j + 1)]

        @pl.when(pl.program_id(1) == pl.num_programs(1) - 1)
        def _():
            ob_ref[...] = o_ref[...].astype(MXU_DTYPE)

    blk = pl.BlockSpec((per, K, N // N_DEV), lambda i, t: (i, 0, 0))
    return _launch(
        body, [at, b], name=name, grid=(N // nb, T // tm),
        in_specs=[pl.BlockSpec((K, tm), lambda i, t: (0, t)), pl.BlockSpec((tm, nb), lambda i, t: (t, i))],
        out_specs=[blk, blk],
        out_shape=[jax.ShapeDtypeStruct((N_DEV, K, N // N_DEV), F32), jax.ShapeDtypeStruct((N_DEV, K, N // N_DEV), MXU_DTYPE)],
        sem=("arbitrary", "arbitrary"), rider=rider)


def _wgrad_in_call(du, dq, dk, dv, dgates, h, rider=None):
    T = h.shape[0]
    tm = _tile(T, WGRAD_TOKENS)
    rows = IN_WIDTH // N_DEV

    def body(du_ref, dq_ref, dk_ref, dv_ref, dgt_ref, h_ref, o_ref, acc, sem):
        t = pl.program_id(0)

        @pl.when(t == 0)
        def _():
            acc[...] = jnp.zeros_like(acc)

        hv = h_ref[...]
        for ref, lo, hi in ((du_ref, 0, C_Q), (dq_ref, C_Q, C_K), (dk_ref, C_K, C_V), (dv_ref, C_V, C_G),
                            (dgt_ref, C_G, IN_WIDTH)):
            acc[lo:hi, :] += _dot(ref[...], hv, TN)

        @pl.when(t == pl.num_programs(0) - 1)
        def _():
            copies = [pltpu.make_async_copy(acc.at[pl.ds(rows * j, rows), :], o_ref.at[j], sem.at[j])
                      for j in range(N_DEV)]
            for cp in copies:
                cp.start()
            for cp in copies:
                cp.wait()

    tok = lambda w: pl.BlockSpec((tm, w), lambda t: (t, 0))
    return _launch(
        body, [du, dq, dk, dv, dgates, h], name="wgrad_in", grid=(T // tm,),
        in_specs=[tok(POOL_WIDTH), tok(ATTN_WIDTH), tok(KV_WIDTH), tok(KV_WIDTH), tok(GATE_WIDTH), tok(D_MODEL)],
        out_specs=[pl.BlockSpec(memory_space=pl.ANY)],
        out_shape=[jax.ShapeDtypeStruct((N_DEV, rows, D_MODEL), F32)],
        scratch_shapes=[pltpu.VMEM((IN_WIDTH, D_MODEL), F32), pltpu.SemaphoreType.DMA((N_DEV,))],
        sem=("arbitrary",), rider=rider)


def _coords():
    return lax.axis_index("x"), lax.axis_index("y"), lax.axis_index("c")


def _allgather_call(shards):
    n = len(shards)

    def body(*refs):
        ins, outs = refs[:n], refs[n:2 * n]
        send_sems, recv_sems, local_sems = refs[2 * n:]
        x, y, c = _coords()
        me, sibling = (x, y, c), (x, y, 1 - c)
        chips = [(1 - x, y), (x, 1 - y), (1 - x, 1 - y)]

        def slot(p):
            return 4 * p[0] + 2 * p[1] + p[2]

        def copy(t, k, block, to, src=None):
            dst = outs[t].at[slot(block)]
            return pltpu.make_async_remote_copy(
                src_ref=dst if src is None else src, dst_ref=dst, send_sem=send_sems.at[t, k],
                recv_sem=recv_sems.at[t, k], device_id=to, device_id_type=MESH)

        mine = [pltpu.make_async_copy(ins[t], outs[t].at[slot(me)], local_sems.at[t]) for t in range(n)]
        for cp in mine:
            cp.start()
        first = []
        for t in range(n):
            first.append(copy(t, 0, me, sibling, src=ins[t]))
            first += [copy(t, 1 + j, me, (*chip, c), src=ins[t]) for j, chip in enumerate(chips)]
        for cp in first:
            cp.start()
        passed = []
        for t in range(n):
            for j, chip in enumerate(chips):
                copy(t, 1 + j, (*chip, c), me).wait_recv()
                fwd = copy(t, 4 + j, (*chip, c), sibling)
                fwd.start()
                passed.append(fwd)
        for t in range(n):
            copy(t, 0, sibling, me).wait_recv()
            for j, chip in enumerate(chips):
                copy(t, 4 + j, (*chip, 1 - c), me).wait_recv()
        for cp in first + passed:
            cp.wait_send()
        for cp in mine:
            cp.wait()

    hbm = pl.BlockSpec(memory_space=pl.ANY)
    return pl.pallas_call(
        body, name="allgather_weights",
        in_specs=[hbm] * n, out_specs=[hbm] * n,
        out_shape=[jax.ShapeDtypeStruct((N_DEV,) + s.shape, s.dtype) for s in shards],
        scratch_shapes=[pltpu.SemaphoreType.DMA((n, 7)), pltpu.SemaphoreType.DMA((n, 7)), pltpu.SemaphoreType.DMA((n,))],
    )(*shards)


def _slot(p):
    return 4 * p[0] + 2 * p[1] + p[2]


def _rows(ref, span):
    return ref if span is None else ref.at[pl.ds(span[0], span[1])]


ALL = "all"
LOCAL = "local"


def _rows(ref, span):
    return ref if span == ALL else ref.at[pl.ds(span[0], span[1])]


def _rider_ag(items):
    ins, out_shape, aliases, where = [], [], {}, []
    n_remote = n_local = 0
    for t, (shard, buf, snd, fwd) in enumerate(items):
        i_shard = i_buf = None
        if snd is not None:
            i_shard = len(ins)
            ins.append(shard)
        if buf is not None:
            i_buf = len(ins)
            ins.append(buf)
            aliases[i_buf] = t
            out_shape.append(jax.ShapeDtypeStruct(buf.shape, buf.dtype))
        else:
            assert fwd is None and snd is not None
            out_shape.append(jax.ShapeDtypeStruct((N_DEV,) + shard.shape, shard.dtype))
        where.append((i_shard, i_buf, n_remote, n_local))
        n_remote += (4 if snd not in (None, LOCAL) else 0) + (3 if fwd is not None else 0)
        n_local += 1 if snd is not None else 0

    def plan(rins, routs, send, recv, loc, r0, l0):
        x, y, c = _coords()
        peers = [(x, y, 1 - c), (1 - x, y, c), (x, 1 - y, c), (1 - x, 1 - y, c)]
        remote, local = [], []
        for t, (shard, buf, snd, fwd) in enumerate(items):
            i_shard, i_buf, k, l = where[t]
            k, l = r0 + k, l0 + l
            if snd is not None:
                span = ALL if snd == LOCAL else snd
                src, dst = _rows(rins[i_shard], span), _rows(routs[t].at[_slot((x, y, c))], span)
                local.append(pltpu.make_async_copy(src, dst, loc.at[l]))
                for peer in (peers if snd != LOCAL else []):
                    remote.append(pltpu.make_async_remote_copy(
                        src_ref=src, dst_ref=dst, send_sem=send.at[k], recv_sem=recv.at[k],
                        device_id=peer, device_id_type=MESH))
                    k += 1
            if fwd is not None:
                for px, py, pc in peers[1:]:
                    s = _slot((px, py, pc))
                    remote.append(pltpu.make_async_remote_copy(
                        src_ref=_rows(rins[i_buf].at[s], fwd), dst_ref=_rows(routs[t].at[s], fwd),
                        send_sem=send.at[k], recv_sem=recv.at[k], device_id=peers[0], device_id_type=MESH))
                    k += 1
        return remote, local

    return _Rider(ins, out_shape, n_remote, n_local, plan, aliases)


def _rider_ag_remote(shards):
    n = len(shards)

    def plan(ins, outs, send, recv, loc, r0, l0):
        x, y, c = _coords()
        remote = []
        for t in range(n):
            dst = outs[t].at[_slot((x, y, c))]
            for k, peer in enumerate([(x, y, 1 - c), (1 - x, y, c), (x, 1 - y, c), (1 - x, 1 - y, c)]):
                remote.append(pltpu.make_async_remote_copy(
                    src_ref=ins[t], dst_ref=dst, send_sem=send.at[r0 + 4 * t + k], recv_sem=recv.at[r0 + 4 * t + k],
                    device_id=peer, device_id_type=MESH))
        return remote, []

    return _Rider(shards, [jax.ShapeDtypeStruct((N_DEV,) + s.shape, s.dtype) for s in shards], 4 * n, 0, plan)


def _rider_rs_sibling(grads):
    n = len(grads)

    def plan(ins, outs, send, recv, loc, r0, l0):
        x, y, c = _coords()
        remote = []
        for t in range(n):
            for q in range(4):
                remote.append(pltpu.make_async_remote_copy(
                    src_ref=ins[t].at[q, 1 - c], dst_ref=outs[t].at[q], send_sem=send.at[r0 + 4 * t + q],
                    recv_sem=recv.at[r0 + 4 * t + q], device_id=(x, y, 1 - c), device_id_type=MESH))
        return remote, []

    return _Rider(grads, [jax.ShapeDtypeStruct((4,) + g.shape[2:], g.dtype) for g in grads], 4 * n, 0, plan)


def _rider_rs_chips(sums, rows=None, into=None):
    n = len(sums)
    rows = rows or [ALL] * n

    def plan(ins, outs, send, recv, loc, r0, l0):
        x, y, c = _coords()
        remote = []
        for t in range(n):
            for r, (px, py) in enumerate([(1 - x, y), (x, 1 - y), (1 - x, 1 - y)]):
                remote.append(pltpu.make_async_remote_copy(
                    src_ref=_rows(ins[t].at[2 * px + py], rows[t]), dst_ref=_rows(outs[t].at[r], rows[t]),
                    send_sem=send.at[r0 + 3 * t + r], recv_sem=recv.at[r0 + 3 * t + r],
                    device_id=(px, py, c), device_id_type=MESH))
        return remote, []

    out_shape = [jax.ShapeDtypeStruct((3,) + s.shape[1:], s.dtype) for s in sums]
    if into is None:
        return _Rider(sums, out_shape, 3 * n, 0, plan)
    return _Rider(list(sums) + list(into), out_shape, 3 * n, 0, plan, aliases={n + t: t for t in range(n)})


def _rider_gather_remote(parts):
    n = len(parts)

    def plan(ins, outs, send, recv, loc, r0, l0):
        x, y, c = _coords()
        me = _slot((x, y, c))
        remote = []
        for t in range(n):
            for k in range(1, N_DEV):
                peer = (x ^ ((k >> 2) & 1), y ^ ((k >> 1) & 1), c ^ (k & 1))
                remote.append(pltpu.make_async_remote_copy(
                    src_ref=ins[t], dst_ref=outs[t].at[me], send_sem=send.at[r0 + 7 * t + k - 1],
                    recv_sem=recv.at[r0 + 7 * t + k - 1], device_id=peer, device_id_type=MESH))
        return remote, []

    return _Rider(parts, [jax.ShapeDtypeStruct((N_DEV,) + p.shape, p.dtype) for p in parts], 7 * n, 0, plan)


def _chip_sum_call(idx, grads, recvd, out_dtypes, name):
    n = len(grads)

    def body(i_ref, *refs):
        for t in range(n):
            refs[2 * n + t][0] = (refs[t][0, 0] + refs[n + t][0]).astype(out_dtypes[t])

    def chip(k, s):
        return jnp.where(k >= s[0], k + 1, k)

    in_specs = [pl.BlockSpec((1, 1) + g.shape[2:], lambda k, s: (chip(k, s), s[1], 0, 0)) for g in grads]
    in_specs += [pl.BlockSpec((1,) + r.shape[1:], lambda k, s: (chip(k, s), 0, 0)) for r in recvd]
    return pl.pallas_call(
        body, name=name,
        grid_spec=pltpu.PrefetchScalarGridSpec(
            num_scalar_prefetch=1, grid=(3,), in_specs=in_specs,
            out_specs=[pl.BlockSpec((1,) + r.shape[1:], lambda k, s: (chip(k, s), 0, 0)) for r in recvd]),
        out_shape=[jax.ShapeDtypeStruct(r.shape, dt) for r, dt in zip(recvd, out_dtypes)],
        compiler_params=_params(("arbitrary",)),
    )(idx, *grads, *recvd)


def _final_sum_call(idx, grads, recvd1, recvd2):
    n = len(grads)
    nsteps = 2

    def body(i_ref, *refs):
        for t in range(n):
            g, r1, r2, o = refs[t], refs[n + t], refs[2 * n + t], refs[3 * n + t]
            s = g[0, 0] + r1[0]
            for r in range(3):
                s = s + r2[r].astype(F32)
            o[...] = s

    def rows(a):
        r = a.shape[-2]
        return r // nsteps if (r // nsteps) % 16 == 0 else r

    def step(a):
        return (lambda i: i) if rows(a) != a.shape[-2] else (lambda i: 0)

    in_specs = [pl.BlockSpec((1, 1, rows(g), g.shape[3]), lambda i, s, st=step(g): (s[0], s[1], st(i), 0)) for g in grads]
    in_specs += [pl.BlockSpec((1, rows(r), r.shape[2]), lambda i, s, st=step(r): (s[0], st(i), 0)) for r in recvd1]
    in_specs += [pl.BlockSpec((3, rows(r), r.shape[2]), lambda i, s, st=step(r): (0, st(i), 0)) for r in recvd2]
    return pl.pallas_call(
        body, name="rs_final_sum",
        grid_spec=pltpu.PrefetchScalarGridSpec(
            num_scalar_prefetch=1, grid=(nsteps,), in_specs=in_specs,
            out_specs=[pl.BlockSpec((rows(r), r.shape[2]), lambda i, s, st=step(r): (st(i), 0)) for r in recvd2]),
        out_shape=[jax.ShapeDtypeStruct(r.shape[1:], F32) for r in recvd2],
        compiler_params=_params(("arbitrary",)),
    )(idx, *grads, *recvd1, *recvd2)


def _sum8_call(parts):
    def body(p_ref, o_ref):
        s = p_ref[0]
        for j in range(1, N_DEV):
            s = s + p_ref[j]
        o_ref[...] = s

    return pl.pallas_call(body, name="sum_small_partials",
                          out_shape=jax.ShapeDtypeStruct(parts.shape[1:], parts.dtype))(parts)


def _adamw(w, g, m, v):
    m = ADAM_B1 * m + (1.0 - ADAM_B1) * g
    v = ADAM_B2 * v + (1.0 - ADAM_B2) * (g * g)
    m_hat = m / (1.0 - ADAM_B1 ** ADAM_STEP)
    v_hat = v / (1.0 - ADAM_B2 ** ADAM_STEP)
    delta = -ADAM_LR * (m_hat / (jnp.sqrt(v_hat) + ADAM_EPS) + ADAM_WD * w)
    return delta, m, v


def _adamw_call(ws, gs, ms, vs, nsteps, name):
    n = len(ws)

    def body(*refs):
        for t in range(n):
            w, g, m, v = (refs[k * n + t][...] for k in range(4))
            d, m2, v2 = _adamw(w, g, m, v)
            refs[4 * n + t][...] = d
            refs[5 * n + t][...] = m2
            refs[6 * n + t][...] = v2

    def spec(a):
        assert a.shape[0] % nsteps == 0 and (nsteps == 1 or (a.shape[0] // nsteps) % 8 == 0), a.shape
        return pl.BlockSpec((a.shape[0] // nsteps, a.shape[1]), lambda i: (i, 0))

    specs = [spec(a) for a in ws]
    outs = pl.pallas_call(
        body, name=name, grid=(nsteps,),
        in_specs=specs * 4, out_specs=specs * 3,
        out_shape=[jax.ShapeDtypeStruct(a.shape, F32) for a in ws] * 3,
        compiler_params=_params(("arbitrary",)),
    )(*ws, *gs, *ms, *vs)
    return outs[:n], outs[n:2 * n], outs[2 * n:]


def _adamw_rs_call(idx, after, gws, r1s, r2s, ws, ms, vs, nsteps, name):
    n = len(ws)

    def body(i_ref, after_ref, *refs):
        for t in range(n):
            gw, r1, r2, w, m, v = (refs[k * n + t] for k in range(6))
            g = gw[0, 0] + r1[0]
            for r in range(3):
                g = g + r2[r].astype(F32)
            d, m2, v2 = _adamw(w[...], g, m[...], v[...])
            refs[6 * n + t][...] = g
            refs[7 * n + t][...] = d
            refs[8 * n + t][...] = m2
            refs[9 * n + t][...] = v2

    def rb(a):
        r = a.shape[0] // nsteps
        assert a.shape[0] % nsteps == 0 and r % 16 == 0, a.shape
        return r

    in_specs = [pl.BlockSpec((1, 1, rb(w), w.shape[1]), lambda i, s: (s[0], s[1], i, 0)) for w in ws]
    in_specs += [pl.BlockSpec((1, rb(w), w.shape[1]), lambda i, s: (s[0], i, 0)) for w in ws]
    in_specs += [pl.BlockSpec((3, rb(w), w.shape[1]), lambda i, s: (0, i, 0)) for w in ws]
    plain = [pl.BlockSpec((rb(w), w.shape[1]), lambda i, s: (i, 0)) for w in ws]
    outs = pl.pallas_call(
        body, name=name,
        grid_spec=pltpu.PrefetchScalarGridSpec(
            num_scalar_prefetch=1, grid=(nsteps,),
            in_specs=[pl.BlockSpec(memory_space=pl.ANY)] + in_specs + plain * 3, out_specs=plain * 4),
        out_shape=[jax.ShapeDtypeStruct(w.shape, F32) for w in ws] * 4,
        compiler_params=_params(("arbitrary",)),
    )(idx, after, *gws, *r1s, *r2s, *ws, *ms, *vs)
    return outs[:n], outs[n:2 * n], outs[2 * n:3 * n], outs[3 * n:]


def _rows128(a, pad_rows):
    flat = a.reshape(-1).astype(F32)
    flat = jnp.pad(flat, (0, pad_rows * LANES - flat.shape[0]))
    return flat.reshape(pad_rows, LANES)


_SMALL_A = (("w_pool", 512), ("pool_scale", 8), ("attn_sinks", 8), ("g_mix_post", 8), ("g_mlp_pre", 8),
            ("g_mlp_post", 8), ("loss", 8), ("b_in_gates", 16))
_SMALL_A_ROWS = 640
_SMALL_B = (("g_mix_pre", 8), ("b_in_head", 16))


def _pack(parts, layout, total_rows):
    rows = [_rows128(parts[k], r) for k, r in layout]
    pad = total_rows - sum(r for _, r in layout)
    if pad:
        rows.append(jnp.zeros((pad, LANES), F32))
    return jnp.concatenate(rows, axis=0)


def _unpack(buf, layout, sizes):
    out, off = {}, 0
    for k, r in layout:
        out[k] = buf[off:off + r].reshape(-1)[:sizes[k]]
        off += r
    return out


def kernel(x, g_mix_pre, w_in, b_in, w_pool, pool_scale, attn_sinks, w_branch_pool, w_branch_attn, w_out, g_mix_post, g_mlp_pre, w_up, w_down, g_mlp_post, loss_target, m_g_mix_pre, m_w_in, m_b_in, m_w_pool, m_pool_scale, m_attn_sinks, m_w_branch_pool, m_w_branch_attn, m_w_out, m_g_mix_post, m_g_mlp_pre, m_w_up, m_w_down, m_g_mlp_post, v_g_mix_pre, v_w_in, v_b_in, v_w_pool, v_pool_scale, v_attn_sinks, v_w_branch_pool, v_w_branch_attn, v_w_out, v_g_mix_post, v_g_mlp_pre, v_w_up, v_w_down, v_g_mlp_post):
    B, S, _ = x.shape
    T = B * S
    xt = x.reshape(T, D_MODEL)
    tgt = loss_target.reshape(T, D_MODEL)
    cx, cy, cc = _coords()

    cidx = jnp.stack([2 * cx + cy, cc]).astype(jnp.int32)
    by_chip = lambda gr: gr.reshape((4, 2) + gr.shape[1:])
    bf = lambda w: w[0].astype(MXU_DTYPE)

    (win_s,) = _allgather_call([w_in[0].T.astype(MXU_DTYPE)])
    win_t = win_s.reshape(IN_WIDTH, D_MODEL)
    wpool_b = bf(w_pool)
    rc, rsa, rsb = _rot_tables(S)

    wbp_l, wba_l, wout_l, wup_l, wdown_l = bf(w_branch_pool), bf(w_branch_attn), bf(w_out), bf(w_up), bf(w_down)
    (c_br, c_up, c_dn), tok = _copies_start(
        [_rider_ag_remote([wbp_l, wba_l, wout_l]), _rider_ag_remote([wup_l]), _rider_ag_remote([wdown_l])],
        "allgather_start", after=win_s)
    (h, u, q, k4, v4, g), _ = _inproj_call(xt, g_mix_pre, win_t, b_in, rc, rsa, rsb, S, rider=_after(tok))
    yp = _pool_call(u, wpool_b, pool_scale, S)
    wbp_1, wba_1, wout_1 = _copies_wait([c_br], yp, "allgather_wait_branch")
    (ya,), (wbp_s, wba_s, wout_s) = _attn_call(
        attn_sinks, q, k4, v4, S,
        rider=_rider_ag([(wbp_l, wbp_1, LOCAL, ALL), (wba_l, wba_1, LOCAL, ALL), (wout_l, wout_1, LOCAL, ALL)]))
    wout_f = wout_s.reshape(D_MODEL, D_MODEL)
    (wup_1,) = _copies_wait([c_up], ya, "allgather_wait_up")
    (mix, x1, h2, h2_t), (wup_s,) = _mix_fwd_call(
        yp, ya, g, xt, wbp_s, wba_s, wout_f, g_mix_post, g_mlp_pre, rider=_rider_ag([(wup_l, wup_1, LOCAL, ALL)]))
    (wdown_1,) = _copies_wait([c_dn], h2, "allgather_wait_down")
    (wdown_s,) = _comm_call(_rider_ag([(wdown_l, wdown_1, LOCAL, ALL)]), "allgather_pass_down")

    act_t, da, dff, dx1, dg3, dg4, lossvec = _mlp_call(x1, h2, tgt, wup_s, wdown_s, g_mlp_pre, g_mlp_post)
    gw_down = by_chip(_wgrad_rows_call(act_t, dff, "wgrad_down")[0])
    (gw_up, gw_up_n), (r1_down,) = _wgrad_cols_call(h2_t, da, "wgrad_up", rider=_rider_rs_sibling([gw_down]))
    gw_up, gw_up_n = by_chip(gw_up), by_chip(gw_up_n)
    (s_down,) = _chip_sum_call(cidx, [gw_down], [r1_down], [MXU_DTYPE], "rs_chip_sum_down")
    (c_down,), tok = _copies_start([_rider_rs_chips([s_down])], "rs_chips_start_down")
    (dyp, do, dgates, dg2, dbg, gw_out, gw_bp, gw_ba), (r1_up,) = _mix_bwd_call(
        dx1, mix, yp, ya, g, wbp_s, wba_s, wout_f, g_mix_post, rider=_after(tok, _rider_rs_sibling([gw_up_n])))
    gw_out = by_chip(gw_out.reshape(N_DEV, D_MODEL // N_DEV, D_MODEL))
    gw_bp, gw_ba = by_chip(gw_bp), by_chip(gw_ba)
    (s_up,) = _chip_sum_call(cidx, [gw_up], [r1_up], [MXU_DTYPE], "rs_chip_sum_up")
    (c_up,), tok = _copies_start([_rider_rs_chips([s_up])], "rs_chips_start_up")
    (dq, dk, dv, dsink), (r1_out, r1_bp, r1_ba) = _attn_bwd_call(
        attn_sinks, q, k4, v4, do, rc, rsa, rsb, S, rider=_after(tok, _rider_rs_sibling([gw_out, gw_bp, gw_ba])))
    s_obb = _chip_sum_call(cidx, [gw_out, gw_bp, gw_ba], [r1_out, r1_bp, r1_ba], [MXU_DTYPE] * 3, "rs_chip_sum_branch")
    (c_obb,), tok = _copies_start([_rider_rs_chips(s_obb)], "rs_chips_start_branch")
    (du, dwp, dps), _ = _pool_bwd_call(u, dyp, wpool_b, pool_scale, S, rider=_after(tok))
    (gw_in,) = _wgrad_in_call(du, dq, dk, dv, dgates, h)
    gw_in = by_chip(gw_in)

    small_a = {"w_pool": dwp, "pool_scale": dps,
               "attn_sinks": jnp.sum(dsink.reshape(B, 8, LANES)[:, 0, :N_Q_HEADS], axis=0), "g_mix_post": dg2,
               "g_mlp_pre": dg3, "g_mlp_post": dg4, "loss": lossvec, "b_in_gates": dbg}
    gw_sa = by_chip(_pack(small_a, _SMALL_A, _SMALL_A_ROWS).reshape(N_DEV, _SMALL_A_ROWS // N_DEV, LANES))
    r1_in, r1_sa = _comm_call(_rider_rs_sibling([gw_in, gw_sa]), "rs_sibling_in")
    s_in, s_sa = _chip_sum_call(cidx, [gw_in, gw_sa], [r1_in, r1_sa], [MXU_DTYPE, F32], "rs_chip_sum_in")
    (c_in,), tok = _copies_start([_rider_rs_chips([s_in, s_sa])], "rs_chips_start_in")
    (gx, dg1, dba_in), _ = _inproj_bwd_call(du, dq, dk, dv, dgates, dx1, xt, win_t, g_mix_pre, rider=_after(tok))
    r2_down, r2_up, r2_out, r2_bp, r2_ba, r2_in, r2_sa = _copies_wait([c_down, c_up, c_obb, c_in], dg1, "rs_chips_wait")

    (g_sa,) = _final_sum_call(cidx, [gw_sa], [r1_sa], [r2_sa])
    part_b = _pack({"g_mix_pre": dg1, "b_in_head": dba_in}, _SMALL_B, sum(r for _, r in _SMALL_B))
    (c_small,), tok = _copies_start([_rider_gather_remote([g_sa, part_b])], "allgather_small_start")

    in_t = _adamw_rs_call(cidx, tok, [gw_in], [r1_in], [r2_in], [w_in[0].T], [m_w_in[0].T], [v_w_in[0].T], 2,
                          "adamw_w_in")
    rest = _adamw_rs_call(
        cidx, tok, [gw_bp, gw_ba, gw_out, gw_up, gw_down], [r1_bp, r1_ba, r1_out, r1_up, r1_down],
        [r2_bp, r2_ba, r2_out, r2_up, r2_down], [w_branch_pool[0], w_branch_attn[0], w_out[0], w_up[0], w_down[0]],
        [m_w_branch_pool[0], m_w_branch_attn[0], m_w_out[0], m_w_up[0], m_w_down[0]],
        [v_w_branch_pool[0], v_w_branch_attn[0], v_w_out[0], v_w_up[0], v_w_down[0]], N_DEV, "adamw_shards")
    big_g, big_d, big_m2, big_v2 = ([a[0].T] + list(b) for a, b in zip(in_t, rest))

    sa_all, sb_all = _copies_wait([c_small], rest[0][0], "allgather_small_wait")
    me = (_slot((cx, cy, cc)), 0, 0)
    sa_all = lax.dynamic_update_slice(sa_all, g_sa[None], me)
    sb_sum = _sum8_call(lax.dynamic_update_slice(sb_all, part_b[None], me))

    names = ["g_mix_pre", "b_in", "w_pool", "pool_scale", "attn_sinks", "g_mix_post", "g_mlp_pre", "g_mlp_post"]
    sm_w = dict(g_mix_pre=g_mix_pre, b_in=b_in, w_pool=w_pool, pool_scale=pool_scale, attn_sinks=attn_sinks,
                g_mix_post=g_mix_post, g_mlp_pre=g_mlp_pre, g_mlp_post=g_mlp_post)
    sm_m = dict(g_mix_pre=m_g_mix_pre, b_in=m_b_in, w_pool=m_w_pool, pool_scale=m_pool_scale, attn_sinks=m_attn_sinks,
                g_mix_post=m_g_mix_post, g_mlp_pre=m_g_mlp_pre, g_mlp_post=m_g_mlp_post)
    sm_v = dict(g_mix_pre=v_g_mix_pre, b_in=v_b_in, w_pool=v_w_pool, pool_scale=v_pool_scale, attn_sinks=v_attn_sinks,
                g_mix_post=v_g_mix_post, g_mlp_pre=v_g_mlp_pre, g_mlp_post=v_g_mlp_post)
    sizes = {k: sm_w[k].size for k in names}
    sizes.update(loss=D_MODEL, b_in_gates=GATE_WIDTH, b_in_head=C_G)
    sm_g = _unpack(sa_all.reshape(_SMALL_A_ROWS, LANES), _SMALL_A, sizes)
    sm_g.update(_unpack(sb_sum, _SMALL_B, sizes))
    sm_g["b_in"] = jnp.concatenate([sm_g["b_in_head"], sm_g["b_in_gates"]])
    loss = (0.5 / D_MODEL) * jnp.sum(sm_g["loss"])
    two_d = lambda a: a.reshape(-1, a.shape[-1])
    sd_, sm2_, sv2_ = _adamw_call([two_d(sm_w[k]) for k in names], [two_d(sm_g[k].reshape(sm_w[k].shape)) for k in names],
                                  [two_d(sm_m[k]) for k in names], [two_d(sm_v[k]) for k in names], 1, "adamw_small")
    like = lambda vals: {k: a.reshape(sm_w[k].shape) for k, a in zip(names, vals)}
    sm_d, sm_m2, sm_v2 = like(sd_), like(sm2_), like(sv2_)
    sm_gr = {k: sm_g[k].reshape(sm_w[k].shape) for k in names}

    order = ["g_mix_pre", "w_in", "b_in", "w_pool", "pool_scale", "attn_sinks", "w_branch_pool", "w_branch_attn",
             "w_out", "g_mix_post", "g_mlp_pre", "w_up", "w_down", "g_mlp_post"]
    big_names = ["w_in", "w_branch_pool", "w_branch_attn", "w_out", "w_up", "w_down"]
    lead = lambda a: a[None]
    tables = []
    for small_t, big_t in ((sm_gr, big_g), (sm_d, big_d), (sm_m2, big_m2), (sm_v2, big_v2)):
        bt = dict(zip(big_names, big_t))
        tables.append([lead(bt[k]) if k in bt else small_t[k] for k in order])
    return (loss, gx.reshape(B, S, D_MODEL), *tables[0], *tables[1], *tables[2], *tables[3])
```

```python
import functools

import jax
import jax.numpy as jnp
from jax import lax
from jax.experimental import pallas as pl
from jax.experimental.pallas import tpu as pltpu

F32 = jnp.float32
MXU_DTYPE = jnp.bfloat16
MESH = pl.DeviceIdType.MESH

D_MODEL = 1024
POOL_WINDOWS = (2, 4, 8, 16)
POOL_WIDTH = 512
POOL_GC = 128
HEAD_DIM = 64
N_Q_HEADS = 8
N_KV_HEADS = 2
GROUP = 4
ATTN_WIDTH = 512
KV_WIDTH = 128
BLOCK = 128
GATE_WIDTH = 2048
IN_WIDTH = 3328
D_FF = 4096
EPS = 1e-6
NEG_INF = -1e30
ROPE_THETA = 500000.0
ROT_DIM = 16
SCALE = HEAD_DIM ** -0.5
C_Q, C_K, C_V, C_G = 512, 1024, 1152, 1280

ADAM_LR = 0.001
ADAM_B1 = 0.9
ADAM_B2 = 0.999
ADAM_EPS = 1e-08
ADAM_WD = 0.01
ADAM_STEP = 10

N_DEV = 8
LANES = 128
VMEM_LIMIT = 56 * 1024 * 1024

NN = (((1,), (0,)), ((), ()))
NT = (((1,), (1,)), ((), ()))
TN = (((0,), (0,)), ((), ()))


def _dot(a, b, dims):
    return lax.dot_general(a, b, dims, preferred_element_type=F32)


def _params(sem=None):
    return pltpu.CompilerParams(dimension_semantics=sem, vmem_limit_bytes=VMEM_LIMIT)


def _tile(n, pref):
    t = min(n, pref)
    assert n % t == 0, (n, t)
    return t


class _Rider:
    def __init__(self, ins, out_shape, n_remote, n_local, plan, aliases=None):
        self.ins, self.out_shape, self.n_remote, self.n_local = list(ins), list(out_shape), n_remote, n_local
        self.plan, self.aliases = plan, dict(aliases or {})


def _after(token, rider=None):
    r = rider or _Rider([], [], 0, 0, lambda ins, outs, send, recv, loc, r0, l0: ([], []))
    return _Rider(r.ins + [token], r.out_shape, r.n_remote, r.n_local, r.plan, r.aliases)


def _launch(body, args, *, name, grid, in_specs, out_specs, out_shape, scratch_shapes=(), sem=None, rider=None):
    if rider is None:
        return pl.pallas_call(body, name=name, grid=grid, in_specs=in_specs, out_specs=out_specs, out_shape=out_shape,
                              scratch_shapes=list(scratch_shapes), compiler_params=_params(sem))(*args)
    n_in, n_out, n_scr = len(args), len(out_shape), len(scratch_shapes)
    r_in, r_out = len(rider.ins), len(rider.out_shape)
    copies = rider.n_remote + rider.n_local > 0

    def wrapped(*refs):
        ins, rins = refs[:n_in], refs[n_in:n_in + r_in]
        o0 = n_in + r_in
        outs, routs = refs[o0:o0 + n_out], refs[o0 + n_out:o0 + n_out + r_out]
        s0 = o0 + n_out + r_out
        scr = refs[s0:s0 + n_scr]
        if not copies:
            return body(*ins, *outs, *scr)
        send, recv, loc = refs[s0 + n_scr:]
        first, last = None, None
        for d in range(len(grid)):
            f, l = pl.program_id(d) == 0, pl.program_id(d) == pl.num_programs(d) - 1
            first = f if first is None else first & f
            last = l if last is None else last & l

        def start():
            remote, local = rider.plan(rins, routs, send, recv, loc, 0, 0)
            for cp in local + remote:
                cp.start()

        def finish():
            remote, local = rider.plan(rins, routs, send, recv, loc, 0, 0)
            for cp in remote + local:
                cp.wait()

        if first is None:
            start()
            body(*ins, *outs, *scr)
            finish()
        else:
            pl.when(first)(start)
            body(*ins, *outs, *scr)
            pl.when(last)(finish)

    hbm = pl.BlockSpec(memory_space=pl.ANY)
    dma = pltpu.SemaphoreType.DMA
    res = pl.pallas_call(
        wrapped, name=name, grid=grid, in_specs=list(in_specs) + [hbm] * r_in,
        out_specs=list(out_specs) + [hbm] * r_out, out_shape=list(out_shape) + rider.out_shape,
        scratch_shapes=list(scratch_shapes) + (
            [dma((max(rider.n_remote, 1),)), dma((max(rider.n_remote, 1),)), dma((max(rider.n_local, 1),))] if copies else []),
        input_output_aliases={n_in + i: n_out + o for i, o in rider.aliases.items()},
        compiler_params=_params(sem),
    )(*args, *rider.ins)
    return list(res[:n_out]), list(res[n_out:])


def _comm_call(rider, name):
    return _launch(lambda: None, [], name=name, grid=(), in_specs=[], out_specs=[], out_shape=[], rider=rider)[1]


_HBM = pl.BlockSpec(memory_space=pltpu.HBM)
_SEM = pl.BlockSpec(memory_space=pltpu.SEMAPHORE)
_EFFECT = pltpu.SideEffectType.DATAFLOW_SIDE_EFFECTING


def _copies_start(riders, name, after=None):
    assert all(r.n_local == 0 and not r.aliases for r in riders)
    extra = [] if after is None else [after]
    sizes = [(len(r.ins), len(r.out_shape)) for r in riders]
    bufs = []
    for r in riders:
        bufs += [pltpu.with_memory_space_constraint(a, pltpu.HBM) for a in r.ins]
        bufs += [pltpu.with_memory_space_constraint(lax.empty(s.shape, s.dtype), pltpu.HBM) for s in r.out_shape]
    nb, ng, ne = len(bufs), len(riders), len(extra)

    def body(*refs):
        sems, token, at = refs[2 * nb + ne:2 * nb + ne + 2 * ng], refs[-1], 0
        for g, (r, (ni, no)) in enumerate(zip(riders, sizes)):
            remote, _ = r.plan(refs[at:at + ni], refs[at + ni:at + ni + no], sems[2 * g], sems[2 * g + 1], None, 0, 0)
            for cp in remote:
                cp.start()
            at += ni + no
        token[...] = jnp.zeros_like(token)

    res = pl.pallas_call(
        body, name=name, in_specs=[_HBM] * nb + [pl.BlockSpec(memory_space=pl.ANY)] * ne,
        out_specs=[_HBM] * nb + [_SEM] * (2 * ng) + [pl.BlockSpec(memory_space=pltpu.VMEM)],
        out_shape=[pltpu.HBM(a.shape, a.dtype) for a in bufs]
        + [pltpu.SemaphoreType.DMA((r.n_remote,)) for r in riders for _ in range(2)]
        + [jax.ShapeDtypeStruct((8, LANES), F32)],
        input_output_aliases={i: i for i in range(nb)},
        compiler_params=pltpu.CompilerParams(has_side_effects=_EFFECT),
    )(*bufs, *extra)
    handles, at = [], 0
    for g, (r, (ni, no)) in enumerate(zip(riders, sizes)):
        handles.append((r, list(res[at:at + ni + no]), res[nb + 2 * g], res[nb + 2 * g + 1]))
        at += ni + no
    return handles, res[-1]


def _copies_wait(handles, after, name, with_sources=False):
    bufs = [b for _, bs, _, _ in handles for b in bs]
    sems = [s for _, _, send, recv in handles for s in (send, recv)]
    nb, ng = len(bufs), len(handles)

    def body(*refs):
        at = 0
        for g, (rider, bs, _, _) in enumerate(handles):
            ni = len(rider.ins)
            remote, _ = rider.plan(refs[at:at + ni], refs[at + ni:at + len(bs)], refs[nb + 2 * g], refs[nb + 2 * g + 1],
                                   None, 0, 0)
            for cp in remote:
                cp.wait_send()
                cp.wait_recv()
            at += len(bs)

    res = pl.pallas_call(
        body, name=name, in_specs=[_HBM] * nb + [_SEM] * (2 * ng) + [pl.BlockSpec(memory_space=pl.ANY)],
        out_specs=[_HBM] * nb, out_shape=[pltpu.HBM(a.shape, a.dtype) for a in bufs],
        input_output_aliases={i: i for i in range(nb)},
        compiler_params=pltpu.CompilerParams(has_side_effects=_EFFECT),
    )(*bufs, *sems, after)
    srcs, lands, at = [], [], 0
    for rider, bs, _, _ in handles:
        srcs += list(res[at:at + len(rider.ins)])
        lands += list(res[at + len(rider.ins):at + len(bs)])
        at += len(bs)
    return (srcs, lands) if with_sources else lands


def _rms_r(x):
    return lax.rsqrt(jnp.mean(x * x, axis=-1, keepdims=True) + EPS)


def _rms_bwd(dn, x, r, g):
    xh = x * r
    dxh = dn * g
    dx = r * (dxh - xh * jnp.mean(dxh * xh, axis=-1, keepdims=True))
    return dx, dn * xh


def _rot(t, c, sa, sb):
    outs = []
    for j in range(t.shape[1] // LANES):
        tj = t[:, LANES * j:LANES * (j + 1)]
        outs.append(tj * c + pltpu.roll(tj, LANES - 8, 1) * sa + pltpu.roll(tj, 8, 1) * sb)
    return outs[0] if len(outs) == 1 else jnp.concatenate(outs, axis=1)


def _rot_tables(S):
    pos = jnp.arange(S, dtype=F32)
    inv_freq = ROPE_THETA ** (-jnp.arange(0, ROT_DIM, 2, dtype=F32) / ROT_DIM)
    ang = pos[:, None] * inv_freq[None, :]
    cos, sin = jnp.cos(ang), jnp.sin(ang)
    one = jnp.ones((S, HEAD_DIM - ROT_DIM), F32)
    zero = jnp.zeros((S, HEAD_DIM - ROT_DIM), F32)
    z8 = jnp.zeros((S, 8), F32)
    c = jnp.concatenate([cos, cos, one], axis=1)
    sa = jnp.concatenate([-sin, z8, zero], axis=1)
    sb = jnp.concatenate([z8, sin, zero], axis=1)
    rep = LANES // HEAD_DIM
    return jnp.tile(c, (1, rep)), jnp.tile(sa, (1, rep)), jnp.tile(sb, (1, rep))


def _lane_tile4(k):
    lane = lax.broadcasted_iota(jnp.int32, k.shape, 1)
    rk = pltpu.roll(k, HEAD_DIM, 1)
    x0 = jnp.where(lane < HEAD_DIM, k, rk)
    x1 = jnp.where(lane < HEAD_DIM, rk, k)
    return jnp.concatenate([x0, x0, x1, x1], axis=1)


def _fold_heads(acc):
    zs = []
    for hk in range(N_KV_HEADS):
        a = acc[:, 256 * hk:256 * hk + LANES] + acc[:, 256 * hk + LANES:256 * (hk + 1)]
        zs.append(a + pltpu.roll(a, HEAD_DIM, 1))
    lane = lax.broadcasted_iota(jnp.int32, zs[0].shape, 1)
    return jnp.where(lane < HEAD_DIM, zs[0], zs[1])


def _inproj_call(x, g1, win_t, b_in, rc, rsa, rsb, S, rider=None):
    T = x.shape[0]
    tm = _tile(S, 512)
    nst = S // tm

    def body(x_ref, g1_ref, w_ref, b_ref, c_ref, sa_ref, sb_ref,
             h_ref, u_ref, q_ref, k4_ref, v4_ref, g_ref):
        xv = x_ref[...]
        hb = ((xv * _rms_r(xv)) * g1_ref[...]).astype(MXU_DTYPE)
        h_ref[...] = hb

        def proj(lo, hi):
            return _dot(hb, w_ref[lo:hi, :], NT) + b_ref[:, lo:hi]

        c, sa, sb = c_ref[...], sa_ref[...], sb_ref[...]
        u_ref[...] = proj(0, C_Q)
        q_ref[...] = (_rot(proj(C_Q, C_K), c, sa, sb) * SCALE).astype(MXU_DTYPE)
        kv = proj(C_K, C_G)
        k4_ref[...] = _lane_tile4(_rot(kv[:, :KV_WIDTH], c, sa, sb)).astype(MXU_DTYPE)
        v4_ref[...] = _lane_tile4(kv[:, KV_WIDTH:]).astype(MXU_DTYPE)
        g_ref[...] = jax.nn.sigmoid(proj(C_G, IN_WIDTH)).astype(MXU_DTYPE)

    tok = lambda w: pl.BlockSpec((tm, w), lambda i: (i, 0))
    full = lambda a: pl.BlockSpec(a.shape, lambda i: (0,) * a.ndim)
    tab = pl.BlockSpec((tm, LANES), lambda i: (i % nst, 0))
    return _launch(
        body, [x, g1, win_t, b_in, rc, rsa, rsb], name="inproj_fwd", grid=(T // tm,),
        in_specs=[tok(D_MODEL), full(g1), full(win_t), full(b_in), tab, tab, tab],
        out_specs=[tok(D_MODEL), tok(POOL_WIDTH), tok(ATTN_WIDTH), tok(512), tok(512), tok(GATE_WIDTH)],
        out_shape=[jax.ShapeDtypeStruct((T, D_MODEL), MXU_DTYPE), jax.ShapeDtypeStruct((T, POOL_WIDTH), F32),
                   jax.ShapeDtypeStruct((T, ATTN_WIDTH), MXU_DTYPE), jax.ShapeDtypeStruct((T, 512), MXU_DTYPE),
                   jax.ShapeDtypeStruct((T, 512), MXU_DTYPE), jax.ShapeDtypeStruct((T, GATE_WIDTH), MXU_DTYPE)],
        sem=("arbitrary",), rider=rider)


def _shift_rows(a, k, rows):
    n = a.shape[0]
    if k > 0:
        return jnp.where(rows >= k, pltpu.roll(a, k, 0), 0.0)
    return jnp.where(rows < n + k, pltpu.roll(a, n + k, 0), 0.0)


def _win_sum(a, w, rows, sign):
    s, k = a, 1
    while k < w:
        s = s + _shift_rows(s, sign * k, rows)
        k *= 2
    return s


def _pool_diff(ug, w, rows):
    inv = 1.0 / jnp.minimum(rows + 1, w).astype(F32)
    return _win_sum(ug, w, rows, 1) * inv - ug, inv


def _pool_call(u, w_pool, pool_scale, S):
    T = u.shape[0]

    def body(u_ref, w_ref, ps_ref, y_ref):
        rows = lax.broadcasted_iota(jnp.int32, (S, POOL_GC), 0)
        for gi, w in enumerate(POOL_WINDOWS):
            sl = slice(POOL_GC * gi, POOL_GC * (gi + 1))
            diff, _ = _pool_diff(u_ref[:, sl], w, rows)
            mixed = _dot(diff.astype(MXU_DTYPE), w_ref[gi], NN)
            y_ref[:, sl] = (mixed * ps_ref[:, sl]).astype(MXU_DTYPE)

    seq = pl.BlockSpec((S, POOL_WIDTH), lambda b: (b, 0))
    return pl.pallas_call(
        body, name="pool_fwd", grid=(T // S,),
        in_specs=[seq, pl.BlockSpec(w_pool.shape, lambda b: (0, 0, 0)), pl.BlockSpec(pool_scale.shape, lambda b: (0, 0))],
        out_specs=seq, out_shape=jax.ShapeDtypeStruct((T, POOL_WIDTH), MXU_DTYPE),
        compiler_params=_params(("arbitrary",)),
    )(u, w_pool, pool_scale)


def _pool_bwd_call(u, dyp, w_pool, pool_scale, S, rider=None):
    T = u.shape[0]

    def body(u_ref, dy_ref, w_ref, ps_ref, du_ref, dw_ref, dps_ref):
        @pl.when(pl.program_id(0) == 0)
        def _():
            dw_ref[...] = jnp.zeros_like(dw_ref)
            dps_ref[...] = jnp.zeros_like(dps_ref)

        rows = lax.broadcasted_iota(jnp.int32, (S, POOL_GC), 0)
        for gi, w in enumerate(POOL_WINDOWS):
            sl = slice(POOL_GC * gi, POOL_GC * (gi + 1))
            diff, inv = _pool_diff(u_ref[:, sl], w, rows)
            diffb = diff.astype(MXU_DTYPE)
            wg = w_ref[gi]
            mixed = _dot(diffb, wg, NN)
            dy = dy_ref[:, sl]
            dps_ref[:, sl] += jnp.sum(dy * mixed, axis=0, keepdims=True)
            dmb = (dy * ps_ref[:, sl]).astype(MXU_DTYPE)
            dw_ref[gi] += _dot(diffb, dmb, TN)
            ddiff = _dot(dmb, wg, NT)
            du_ref[:, sl] = (_win_sum(ddiff * inv, w, rows, -1) - ddiff).astype(MXU_DTYPE)

    seq = pl.BlockSpec((S, POOL_WIDTH), lambda b: (b, 0))
    return _launch(
        body, [u, dyp, w_pool, pool_scale], name="pool_bwd", grid=(T // S,),
        in_specs=[seq, seq, pl.BlockSpec(w_pool.shape, lambda b: (0, 0, 0)), pl.BlockSpec(pool_scale.shape, lambda b: (0, 0))],
        out_specs=[seq, pl.BlockSpec(w_pool.shape, lambda b: (0, 0, 0)), pl.BlockSpec(pool_scale.shape, lambda b: (0, 0))],
        out_shape=[jax.ShapeDtypeStruct((T, POOL_WIDTH), MXU_DTYPE), jax.ShapeDtypeStruct(w_pool.shape, F32),
                   jax.ShapeDtypeStruct(pool_scale.shape, F32)],
        sem=("arbitrary",), rider=rider)


def _attn_consts():
    lane_g = lax.broadcasted_iota(jnp.int32, (BLOCK, 256), 1) >> 6
    rgrp = lax.broadcasted_iota(jnp.int32, (GROUP * BLOCK, 1), 0) >> 7
    rel = lax.broadcasted_iota(jnp.int32, (BLOCK, 256), 0) - lax.broadcasted_iota(jnp.int32, (BLOCK, 256), 1)

    def bias(off):
        ok = (rel + off >= 0) & (rel + off < BLOCK)
        return jnp.concatenate([jnp.where(ok, 0.0, NEG_INF)] * GROUP, axis=0)

    return lane_g, rgrp, bias(0), bias(BLOCK)


def _sink_rows(sink_ref, hk, rgrp):
    sv = jnp.zeros(rgrp.shape, F32)
    for g in range(GROUP):
        sv = jnp.where(rgrp == g, sink_ref[0, GROUP * hk + g], sv)
    return sv


def _stack_heads(xb, lane_g):
    return jnp.concatenate([jnp.where(lane_g == g, xb, jnp.zeros_like(xb)) for g in range(GROUP)], axis=0)


def _unstack_heads(xs, lane_g):
    out = jnp.where(lane_g == 0, xs[0:BLOCK], 0.0)
    for g in range(1, GROUP):
        out = out + jnp.where(lane_g == g, xs[BLOCK * g:BLOCK * (g + 1)], 0.0)
    return out


def _attn_probs(qs, kb, bias, sv):
    s = _dot(qs, kb, NT) + bias
    m = jnp.maximum(jnp.max(s, axis=1, keepdims=True), sv)
    e = jnp.exp(s - m)
    es = jnp.exp(sv - m)
    inv_l = 1.0 / (jnp.sum(e, axis=1, keepdims=True) + es)
    return e * inv_l, es * inv_l


def _attn_blocks(nb, blk, carry, per=1):
    carry = blk(0, 0, True, carry)
    per = per if (nb - 1) % per == 0 else 1

    def step(i, c):
        for k in range(per):
            n = 1 + per * i + k
            c = blk(pl.multiple_of(n * BLOCK, BLOCK), pl.multiple_of((n - 1) * BLOCK, BLOCK), False, c)
        return c

    return lax.fori_loop(0, (nb - 1) // per, step, carry)


def _attn_call(sinks, q, k4, v4, S, rider=None):
    T = q.shape[0]
    nb = S // BLOCK

    def body(sink_ref, q_ref, k_ref, v_ref, o_ref):
        lane_g, rgrp, bias_first, bias_later = _attn_consts()
        svs = [_sink_rows(sink_ref, hk, rgrp) for hk in range(N_KV_HEADS)]

        def blk(q0, k0, first, carry):
            for hk in range(N_KV_HEADS):
                cs = slice(256 * hk, 256 * (hk + 1))
                qs = _stack_heads(q_ref[pl.ds(q0, BLOCK), cs], lane_g)
                p, _ = _attn_probs(qs, k_ref[pl.ds(k0, 2 * BLOCK), cs], bias_first if first else bias_later, svs[hk])
                o = _dot(p.astype(MXU_DTYPE), v_ref[pl.ds(k0, 2 * BLOCK), cs], NN)
                o_ref[pl.ds(q0, BLOCK), cs] = _unstack_heads(o, lane_g).astype(MXU_DTYPE)
            return carry

        _attn_blocks(nb, blk, 0, per=3)

    seq = pl.BlockSpec((S, ATTN_WIDTH), lambda b: (b, 0))
    return _launch(
        body, [sinks, q, k4, v4], name="attn_fwd", grid=(T // S,),
        in_specs=[pl.BlockSpec(memory_space=pltpu.SMEM), seq, seq, seq],
        out_specs=[seq], out_shape=[jax.ShapeDtypeStruct((T, ATTN_WIDTH), MXU_DTYPE)],
        sem=("arbitrary",), rider=rider)


def _attn_bwd_call(sinks, q, k4, v4, do, rc, rsa, rsb, S, rider=None):
    T = q.shape[0]
    nb = S // BLOCK

    def body(sink_ref, q_ref, k_ref, v_ref, do_ref, c_ref, sa_ref, sb_ref,
             dq_ref, dk_ref, dv_ref, ds_ref, dk_acc, dv_acc):
        lane_g, rgrp, bias_first, bias_later = _attn_consts()
        svs = [_sink_rows(sink_ref, hk, rgrp) for hk in range(N_KV_HEADS)]
        lane1 = lax.broadcasted_iota(jnp.int32, (1, LANES), 1)
        dk_acc[...] = jnp.zeros_like(dk_acc)
        dv_acc[...] = jnp.zeros_like(dv_acc)

        def blk(q0, k0, first, dsink):
            rows = pl.ds(q0, BLOCK)
            c, sa, sb = c_ref[rows, :], sa_ref[rows, :], sb_ref[rows, :]
            for hk in range(N_KV_HEADS):
                cs = slice(256 * hk, 256 * (hk + 1))
                qs = _stack_heads(q_ref[rows, cs], lane_g)
                dos = _stack_heads(do_ref[rows, cs], lane_g)
                kb = k_ref[pl.ds(k0, 2 * BLOCK), cs]
                vb = v_ref[pl.ds(k0, 2 * BLOCK), cs]
                p, ps = _attn_probs(qs, kb, bias_first if first else bias_later, svs[hk])
                dp = _dot(dos, vb, NT)
                delta = jnp.sum(p * dp, axis=1, keepdims=True)
                dsb = (p * (dp - delta)).astype(MXU_DTYPE)
                dqb = _unstack_heads(_dot(dsb, kb, NN), lane_g) * SCALE
                dq_ref[rows, cs] = _rot(dqb, c, -sa, -sb).astype(MXU_DTYPE)
                dk_acc[pl.ds(k0, 2 * BLOCK), cs] += _dot(dsb, qs, TN)
                dv_acc[pl.ds(k0, 2 * BLOCK), cs] += _dot(p.astype(MXU_DTYPE), dos, TN)
                psd = ps * delta
                for g in range(GROUP):
                    val = -jnp.sum(psd[BLOCK * g:BLOCK * (g + 1)], axis=0, keepdims=True)
                    dsink = dsink + jnp.where(lane1 == GROUP * hk + g, val, 0.0)
            return dsink

        dsink = _attn_blocks(nb, blk, jnp.zeros((1, LANES), F32))
        dk_ref[...] = _rot(_fold_heads(dk_acc[...]), c_ref[...], -sa_ref[...], -sb_ref[...]).astype(MXU_DTYPE)
        dv_ref[...] = _fold_heads(dv_acc[...]).astype(MXU_DTYPE)
        ds_ref[...] = jnp.broadcast_to(dsink, ds_ref.shape)

    seq = pl.BlockSpec((S, ATTN_WIDTH), lambda b: (b, 0))
    kvs = pl.BlockSpec((S, KV_WIDTH), lambda b: (b, 0))
    tab = pl.BlockSpec((S, LANES), lambda b: (0, 0))
    nseq = T // S
    return _launch(
        body, [sinks, q, k4, v4, do, rc, rsa, rsb], name="attn_bwd", grid=(nseq,),
        in_specs=[pl.BlockSpec(memory_space=pltpu.SMEM), seq, seq, seq, seq, tab, tab, tab],
        out_specs=[seq, kvs, kvs, pl.BlockSpec((8, LANES), lambda b: (b, 0))],
        out_shape=[jax.ShapeDtypeStruct((T, ATTN_WIDTH), MXU_DTYPE), jax.ShapeDtypeStruct((T, KV_WIDTH), MXU_DTYPE),
                   jax.ShapeDtypeStruct((T, KV_WIDTH), MXU_DTYPE), jax.ShapeDtypeStruct((8 * nseq, LANES), F32)],
        scratch_shapes=[pltpu.VMEM((S, 512), F32), pltpu.VMEM((S, 512), F32)],
        sem=("arbitrary",), rider=rider)


def _branch_weights(wbp_ref, wba_ref, wbp_s, wba_s):
    @pl.when(pl.program_id(0) == 0)
    def _():
        for j in range(N_DEV):
            wbp_s[:, LANES * j:LANES * (j + 1)] = wbp_ref[j]
            wba_s[:, LANES * j:LANES * (j + 1)] = wba_ref[j]


def _mix_fwd_call(yp, ya, g, x, wbp, wba, wout, g2, g3, rider=None):
    T = x.shape[0]
    tm = _tile(T, 512)

    def body(yp_ref, ya_ref, g_ref, x_ref, wbp_ref, wba_ref, wout_ref, g2_ref, g3_ref,
             mix_ref, x1_ref, h2_ref, h2t_ref, wbp_s, wba_s):
        _branch_weights(wbp_ref, wba_ref, wbp_s, wba_s)
        bp = _dot(yp_ref[...], wbp_s[...], NN)
        ba = _dot(ya_ref[...], wba_s[...], NN)
        merged = g_ref[:, :D_MODEL].astype(F32) * bp + g_ref[:, D_MODEL:].astype(F32) * ba
        mix = _dot(merged.astype(MXU_DTYPE), wout_ref[...], NN)
        mix_ref[...] = mix
        x1 = x_ref[...] + (mix * _rms_r(mix)) * g2_ref[...]
        x1_ref[...] = x1
        h2 = (x1 * _rms_r(x1)) * g3_ref[...]
        h2_ref[...] = h2.astype(MXU_DTYPE)
        h2t_ref[...] = h2.T.astype(MXU_DTYPE)

    tok = lambda w: pl.BlockSpec((tm, w), lambda i: (i, 0))
    full = lambda a: pl.BlockSpec(a.shape, lambda i: (0,) * a.ndim)
    return _launch(
        body, [yp, ya, g, x, wbp, wba, wout, g2, g3], name="mix_fwd", grid=(T // tm,),
        in_specs=[tok(POOL_WIDTH), tok(ATTN_WIDTH), tok(GATE_WIDTH), tok(D_MODEL), full(wbp), full(wba), full(wout),
                  full(g2), full(g3)],
        out_specs=[tok(D_MODEL), tok(D_MODEL), tok(D_MODEL), pl.BlockSpec((D_MODEL, tm), lambda i: (0, i))],
        out_shape=[jax.ShapeDtypeStruct((T, D_MODEL), F32), jax.ShapeDtypeStruct((T, D_MODEL), F32),
                   jax.ShapeDtypeStruct((T, D_MODEL), MXU_DTYPE), jax.ShapeDtypeStruct((D_MODEL, T), MXU_DTYPE)],
        scratch_shapes=[pltpu.VMEM((POOL_WIDTH, D_MODEL), MXU_DTYPE), pltpu.VMEM((ATTN_WIDTH, D_MODEL), MXU_DTYPE)],
        sem=("arbitrary",), rider=rider)


def _mix_bwd_call(dx1, mix, yp, ya, g, wbp, wba, wout, g2, rider=None):
    T = dx1.shape[0]
    tm = _tile(T, 512)

    def body(dx1_ref, mix_ref, yp_ref, ya_ref, g_ref, wbp_ref, wba_ref, wout_ref, g2_ref,
             dyp_ref, do_ref, dgates_ref, dg2_ref, dbg_ref, gout_ref, gbp_ref, gba_ref,
             wbp_s, wba_s, acc_out, acc_bp, acc_ba, sem):
        _branch_weights(wbp_ref, wba_ref, wbp_s, wba_s)
        step = pl.program_id(0)

        @pl.when(step == 0)
        def _():
            dg2_ref[...] = jnp.zeros_like(dg2_ref)
            dbg_ref[...] = jnp.zeros_like(dbg_ref)
            acc_out[...] = jnp.zeros_like(acc_out)
            acc_bp[...] = jnp.zeros_like(acc_bp)
            acc_ba[...] = jnp.zeros_like(acc_ba)

        mix = mix_ref[...]
        dmix, dg2 = _rms_bwd(dx1_ref[...], mix, _rms_r(mix), g2_ref[...])
        dg2_ref[...] += jnp.sum(dg2, axis=0, keepdims=True)
        dmixb = dmix.astype(MXU_DTYPE)
        dmerged = _dot(dmixb, wout_ref[...], NT)
        yp, ya = yp_ref[...], ya_ref[...]
        bp = _dot(yp, wbp_s[...], NN)
        ba = _dot(ya, wba_s[...], NN)
        gp, ga = g_ref[:, :D_MODEL].astype(F32), g_ref[:, D_MODEL:].astype(F32)
        acc_out[...] += _dot((gp * bp + ga * ba).astype(MXU_DTYPE), dmixb, TN)
        dgp = dmerged * bp * (gp * (1.0 - gp))
        dga = dmerged * ba * (ga * (1.0 - ga))
        dbg_ref[:, :D_MODEL] += jnp.sum(dgp, axis=0, keepdims=True)
        dbg_ref[:, D_MODEL:] += jnp.sum(dga, axis=0, keepdims=True)
        dgates_ref[:, :D_MODEL] = dgp.astype(MXU_DTYPE)
        dgates_ref[:, D_MODEL:] = dga.astype(MXU_DTYPE)
        dbp = (dmerged * gp).astype(MXU_DTYPE)
        dba = (dmerged * ga).astype(MXU_DTYPE)
        acc_bp[...] += _dot(yp, dbp, TN)
        acc_ba[...] += _dot(ya, dba, TN)
        dyp_ref[...] = _dot(dbp, wbp_s[...], NT)
        do_ref[...] = _dot(dba, wba_s[...], NT).astype(MXU_DTYPE)

        @pl.when(step == pl.num_programs(0) - 1)
        def _():
            copies = [pltpu.make_async_copy(acc_out, gout_ref, sem.at[0])]
            for j in range(N_DEV):
                cols = slice(LANES * j, LANES * (j + 1))
                copies.append(pltpu.make_async_copy(acc_bp.at[:, cols], gbp_ref.at[j], sem.at[1 + j]))
                copies.append(pltpu.make_async_copy(acc_ba.at[:, cols], gba_ref.at[j], sem.at[1 + N_DEV + j]))
            for cp in copies:
                cp.start()
            for cp in copies:
                cp.wait()

    tok = lambda w: pl.BlockSpec((tm, w), lambda i: (i, 0))
    full = lambda a: pl.BlockSpec(a.shape, lambda i: (0,) * a.ndim)
    acc = lambda w: pl.BlockSpec((1, w), lambda i: (0, 0))
    hbm = pl.BlockSpec(memory_space=pl.ANY)
    sd = jax.ShapeDtypeStruct
    return _launch(
        body, [dx1, mix, yp, ya, g, wbp, wba, wout, g2], name="mix_bwd", grid=(T // tm,),
        in_specs=[tok(D_MODEL), tok(D_MODEL), tok(POOL_WIDTH), tok(ATTN_WIDTH), tok(GATE_WIDTH), full(wbp), full(wba),
                  full(wout), full(g2)],
        out_specs=[tok(POOL_WIDTH), tok(ATTN_WIDTH), tok(GATE_WIDTH), acc(D_MODEL), acc(GATE_WIDTH), hbm, hbm, hbm],
        out_shape=[sd((T, POOL_WIDTH), F32), sd((T, ATTN_WIDTH), MXU_DTYPE), sd((T, GATE_WIDTH), MXU_DTYPE),
                   sd((1, D_MODEL), F32), sd((1, GATE_WIDTH), F32), sd((D_MODEL, D_MODEL), F32),
                   sd((N_DEV, POOL_WIDTH, LANES), F32), sd((N_DEV, ATTN_WIDTH, LANES), F32)],
        scratch_shapes=[pltpu.VMEM((POOL_WIDTH, D_MODEL), MXU_DTYPE), pltpu.VMEM((ATTN_WIDTH, D_MODEL), MXU_DTYPE),
                        pltpu.VMEM((D_MODEL, D_MODEL), F32), pltpu.VMEM((POOL_WIDTH, D_MODEL), F32),
                        pltpu.VMEM((ATTN_WIDTH, D_MODEL), F32), pltpu.SemaphoreType.DMA((1 + 2 * N_DEV,))],
        sem=("arbitrary",), rider=rider)


def _mlp_call(x1, h2, target, wup, wdown, g3, g4):
    T = x1.shape[0]
    tm = _tile(T, 256)
    fc = D_FF // N_DEV

    def body(x1_ref, h2_ref, t_ref, wup_ref, wdown_ref, g3_ref, g4_ref,
             act_ref, da_ref, dff_ref, dx1_ref, dg3_ref, dg4_ref, loss_ref, rl_s):
        @pl.when(pl.program_id(0) == 0)
        def _():
            dg3_ref[...] = jnp.zeros_like(dg3_ref)
            dg4_ref[...] = jnp.zeros_like(dg4_ref)
            loss_ref[...] = jnp.zeros_like(loss_ref)

        h2 = h2_ref[...]
        ff = jnp.zeros((tm, D_MODEL), F32)
        for j in range(N_DEV):
            sl = slice(fc * j, fc * (j + 1))
            rl = jnp.maximum(_dot(h2, wup_ref[j], NN), 0.0)
            rl_s[:, sl] = rl
            act = rl * rl
            act_ref[sl, :] = act.T.astype(MXU_DTYPE)
            ff = ff + _dot(act.astype(MXU_DTYPE), wdown_ref[j], NN)
        x1 = x1_ref[...]
        r4 = _rms_r(ff)
        err = x1 + (ff * r4) * g4_ref[...] - t_ref[...]
        loss_ref[...] += jnp.sum(err * err, axis=0, keepdims=True)
        dy = err * (1.0 / D_MODEL)
        dff, dg4 = _rms_bwd(dy, ff, r4, g4_ref[...])
        dg4_ref[...] += jnp.sum(dg4, axis=0, keepdims=True)
        dffb = dff.astype(MXU_DTYPE)
        dff_ref[...] = dffb
        dh2 = jnp.zeros((tm, D_MODEL), F32)
        for j in range(N_DEV):
            sl = slice(fc * j, fc * (j + 1))
            dab = (_dot(dffb, wdown_ref[j], NT) * (2.0 * rl_s[:, sl])).astype(MXU_DTYPE)
            da_ref[:, sl] = dab
            dh2 = dh2 + _dot(dab, wup_ref[j], NT)
        dx1, dg3 = _rms_bwd(dh2, x1, _rms_r(x1), g3_ref[...])
        dg3_ref[...] += jnp.sum(dg3, axis=0, keepdims=True)
        dx1_ref[...] = dy + dx1

    tok = lambda w: pl.BlockSpec((tm, w), lambda i: (i, 0))
    full = lambda a: pl.BlockSpec(a.shape, lambda i: (0,) * a.ndim, pipeline_mode=pl.Buffered(1))
    vec = pl.BlockSpec((1, D_MODEL), lambda i: (0, 0))
    sd = jax.ShapeDtypeStruct
    return pl.pallas_call(
        body, name="mlp_fwd_bwd", grid=(T // tm,),
        in_specs=[tok(D_MODEL), tok(D_MODEL), tok(D_MODEL), full(wup), full(wdown), vec, vec],
        out_specs=[pl.BlockSpec((D_FF, tm), lambda i: (0, i)), tok(D_FF), tok(D_MODEL), tok(D_MODEL), vec, vec, vec],
        out_shape=[sd((D_FF, T), MXU_DTYPE), sd((T, D_FF), MXU_DTYPE), sd((T, D_MODEL), MXU_DTYPE),
                   sd((T, D_MODEL), F32), sd((1, D_MODEL), F32), sd((1, D_MODEL), F32), sd((1, D_MODEL), F32)],
        scratch_shapes=[pltpu.VMEM((tm, D_FF), F32)],
        compiler_params=_params(("arbitrary",)),
    )(x1, h2, target, wup, wdown, g3, g4)


def _inproj_bwd_call(du, dq, dk, dv, dgates, dx1, x, win_t, g1, rider=None):
    T = x.shape[0]
    tm = _tile(T, 512)

    def body(du_ref, dq_ref, dk_ref, dv_ref, dgt_ref, dx1_ref, x_ref, w_ref, g1_ref, gx_ref, dg1_ref, db_ref):
        @pl.when(pl.program_id(0) == 0)
        def _():
            dg1_ref[...] = jnp.zeros_like(dg1_ref)
            db_ref[...] = jnp.zeros_like(db_ref)

        dh = jnp.zeros((tm, D_MODEL), F32)
        for ref, lo, hi in ((du_ref, 0, C_Q), (dq_ref, C_Q, C_K), (dk_ref, C_K, C_V), (dv_ref, C_V, C_G),
                            (dgt_ref, C_G, IN_WIDTH)):
            piece = ref[...]
            dh = dh + _dot(piece, w_ref[lo:hi, :], NN)
            if hi <= C_G:
                db_ref[:, lo:hi] += jnp.sum(piece.astype(F32), axis=0, keepdims=True)
        xv = x_ref[...]
        dx, dg1 = _rms_bwd(dh, xv, _rms_r(xv), g1_ref[...])
        dg1_ref[...] += jnp.sum(dg1, axis=0, keepdims=True)
        gx_ref[...] = dx1_ref[...] + dx

    tok = lambda w: pl.BlockSpec((tm, w), lambda i: (i, 0))
    full = lambda a: pl.BlockSpec(a.shape, lambda i: (0,) * a.ndim)
    sd = jax.ShapeDtypeStruct
    return _launch(
        body, [du, dq, dk, dv, dgates, dx1, x, win_t, g1], name="inproj_bwd", grid=(T // tm,),
        in_specs=[tok(POOL_WIDTH), tok(ATTN_WIDTH), tok(KV_WIDTH), tok(KV_WIDTH), tok(GATE_WIDTH), tok(D_MODEL),
                  tok(D_MODEL), full(win_t), full(g1)],
        out_specs=[tok(D_MODEL), pl.BlockSpec((1, D_MODEL), lambda i: (0, 0)), pl.BlockSpec((1, C_G), lambda i: (0, 0))],
        out_shape=[sd((T, D_MODEL), F32), sd((1, D_MODEL), F32), sd((1, C_G), F32)],
        sem=("arbitrary",), rider=rider)


WGRAD_TOKENS = 1024


def _wgrad_rows_call(at, b, name, rider=None):
    K, T = at.shape
    N = b.shape[1]
    tm = _tile(T, WGRAD_TOKENS)
    kb = min(K, 1024)
    per = kb // (K // N_DEV)

    def body(a_ref, b_ref, o_ref):
        @pl.when(pl.program_id(1) == 0)
        def _():
            o_ref[...] = jnp.zeros_like(o_ref)

        d = _dot(a_ref[...], b_ref[...], NN)
        rs = kb // per
        for j in range(per):
            o_ref[j] += d[rs * j:rs * (j + 1)]

    return _launch(
        body, [at, b], name=name, grid=(K // kb, T // tm),
        in_specs=[pl.BlockSpec((kb, tm), lambda i, t: (i, t)), pl.BlockSpec((tm, N), lambda i, t: (t, 0))],
        out_specs=[pl.BlockSpec((per, K // N_DEV, N), lambda i, t: (i, 0, 0))],
        out_shape=[jax.ShapeDtypeStruct((N_DEV, K // N_DEV, N), F32)],
        sem=("arbitrary", "arbitrary"), rider=rider)


def _wgrad_cols_call(at, b, name, rider=None):
    K, T = at.shape
    N = b.shape[1]
    tm = _tile(T, WGRAD_TOKENS)
    nb = min(N, 1024)
    per = nb // (N // N_DEV)

    def body(a_ref, b_ref, o_ref):
        @pl.when(pl.program_id(1) == 0)
        def _():
            o_ref[...] = jnp.zeros_like(o_ref)

        d = _dot(a_ref[...], b_ref[...], NN)
        cs = nb // per
        for j in range(per):
            o_ref[j] += d[:, cs * j:cs * (j + 1)]

    return _launch(
        body, [at, b], name=name, grid=(N // nb, T // tm),
        in_specs=[pl.BlockSpec((K, tm), lambda i, t: (0, t)), pl.BlockSpec((tm, nb), lambda i, t: (t, i))],
        out_specs=[pl.BlockSpec((per, K, N // N_DEV), lambda i, t: (i, 0, 0))],
        out_shape=[jax.ShapeDtypeStruct((N_DEV, K, N // N_DEV), F32)],
        sem=("arbitrary", "arbitrary"), rider=rider)


def _wgrad_in_call(du, dq, dk, dv, dgates, h, rider=None):
    T = h.shape[0]
    tm = _tile(T, WGRAD_TOKENS)
    rows = IN_WIDTH // N_DEV

    def body(du_ref, dq_ref, dk_ref, dv_ref, dgt_ref, h_ref, o_ref, acc, sem):
        t = pl.program_id(0)

        @pl.when(t == 0)
        def _():
            acc[...] = jnp.zeros_like(acc)

        hv = h_ref[...]
        for ref, lo, hi in ((du_ref, 0, C_Q), (dq_ref, C_Q, C_K), (dk_ref, C_K, C_V), (dv_ref, C_V, C_G),
                            (dgt_ref, C_G, IN_WIDTH)):
            acc[lo:hi, :] += _dot(ref[...], hv, TN)

        @pl.when(t == pl.num_programs(0) - 1)
        def _():
            copies = [pltpu.make_async_copy(acc.at[pl.ds(rows * j, rows), :], o_ref.at[j], sem.at[j])
                      for j in range(N_DEV)]
            for cp in copies:
                cp.start()
            for cp in copies:
                cp.wait()

    tok = lambda w: pl.BlockSpec((tm, w), lambda t: (t, 0))
    return _launch(
        body, [du, dq, dk, dv, dgates, h], name="wgrad_in", grid=(T // tm,),
        in_specs=[tok(POOL_WIDTH), tok(ATTN_WIDTH), tok(KV_WIDTH), tok(KV_WIDTH), tok(GATE_WIDTH), tok(D_MODEL)],
        out_specs=[pl.BlockSpec(memory_space=pl.ANY)],
        out_shape=[jax.ShapeDtypeStruct((N_DEV, rows, D_MODEL), F32)],
        scratch_shapes=[pltpu.VMEM((IN_WIDTH, D_MODEL), F32), pltpu.SemaphoreType.DMA((N_DEV,))],
        sem=("arbitrary",), rider=rider)


def _coords():
    return lax.axis_index("x"), lax.axis_index("y"), lax.axis_index("c")


def _allgather_call(shards):
    n = len(shards)

    def body(*refs):
        ins, outs = refs[:n], refs[n:2 * n]
        send_sems, recv_sems, local_sems = refs[2 * n:]
        x, y, c = _coords()
        me, sibling = (x, y, c), (x, y, 1 - c)
        chips = [(1 - x, y), (x, 1 - y), (1 - x, 1 - y)]

        def slot(p):
            return 4 * p[0] + 2 * p[1] + p[2]

        def copy(t, k, block, to, src=None):
            dst = outs[t].at[slot(block)]
            return pltpu.make_async_remote_copy(
                src_ref=dst if src is None else src, dst_ref=dst, send_sem=send_sems.at[t, k],
                recv_sem=recv_sems.at[t, k], device_id=to, device_id_type=MESH)

        mine = [pltpu.make_async_copy(ins[t], outs[t].at[slot(me)], local_sems.at[t]) for t in range(n)]
        for cp in mine:
            cp.start()
        first = []
        for t in range(n):
            first.append(copy(t, 0, me, sibling, src=ins[t]))
            first += [copy(t, 1 + j, me, (*chip, c), src=ins[t]) for j, chip in enumerate(chips)]
        for cp in first:
            cp.start()
        passed = []
        for t in range(n):
            for j, chip in enumerate(chips):
                copy(t, 1 + j, (*chip, c), me).wait_recv()
                fwd = copy(t, 4 + j, (*chip, c), sibling)
                fwd.start()
                passed.append(fwd)
        for t in range(n):
            copy(t, 0, sibling, me).wait_recv()
            for j, chip in enumerate(chips):
                copy(t, 4 + j, (*chip, 1 - c), me).wait_recv()
        for cp in first + passed:
            cp.wait_send()
        for cp in mine:
            cp.wait()

    hbm = pl.BlockSpec(memory_space=pl.ANY)
    return pl.pallas_call(
        body, name="allgather_weights",
        in_specs=[hbm] * n, out_specs=[hbm] * n,
        out_shape=[jax.ShapeDtypeStruct((N_DEV,) + s.shape, s.dtype) for s in shards],
        scratch_shapes=[pltpu.SemaphoreType.DMA((n, 7)), pltpu.SemaphoreType.DMA((n, 7)), pltpu.SemaphoreType.DMA((n,))],
    )(*shards)


def _slot(p):
    return 4 * p[0] + 2 * p[1] + p[2]


def _rows(ref, span):
    return ref if span is None else ref.at[pl.ds(span[0], span[1])]


ALL = "all"
LOCAL = "local"


def _rows(ref, span):
    return ref if span == ALL else ref.at[pl.ds(span[0], span[1])]


def _rider_ag(items):
    ins, out_shape, aliases, where = [], [], {}, []
    n_remote = n_local = 0
    for t, (shard, buf, snd, fwd) in enumerate(items):
        i_shard = i_buf = None
        if snd is not None:
            i_shard = len(ins)
            ins.append(shard)
        if buf is not None:
            i_buf = len(ins)
            ins.append(buf)
            aliases[i_buf] = t
            out_shape.append(jax.ShapeDtypeStruct(buf.shape, buf.dtype))
        else:
            assert fwd is None and snd is not None
            out_shape.append(jax.ShapeDtypeStruct((N_DEV,) + shard.shape, shard.dtype))
        where.append((i_shard, i_buf, n_remote, n_local))
        n_remote += (4 if snd not in (None, LOCAL) else 0) + (3 if fwd is not None else 0)
        n_local += 1 if snd is not None else 0

    def plan(rins, routs, send, recv, loc, r0, l0):
        x, y, c = _coords()
        peers = [(x, y, 1 - c), (1 - x, y, c), (x, 1 - y, c), (1 - x, 1 - y, c)]
        remote, local = [], []
        for t, (shard, buf, snd, fwd) in enumerate(items):
            i_shard, i_buf, k, l = where[t]
            k, l = r0 + k, l0 + l
            if snd is not None:
                span = ALL if snd == LOCAL else snd
                src, dst = _rows(rins[i_shard], span), _rows(routs[t].at[_slot((x, y, c))], span)
                local.append(pltpu.make_async_copy(src, dst, loc.at[l]))
                for peer in (peers if snd != LOCAL else []):
                    remote.append(pltpu.make_async_remote_copy(
                        src_ref=src, dst_ref=dst, send_sem=send.at[k], recv_sem=recv.at[k],
                        device_id=peer, device_id_type=MESH))
                    k += 1
            if fwd is not None:
                for px, py, pc in peers[1:]:
                    s = _slot((px, py, pc))
                    remote.append(pltpu.make_async_remote_copy(
                        src_ref=_rows(rins[i_buf].at[s], fwd), dst_ref=_rows(routs[t].at[s], fwd),
                        send_sem=send.at[k], recv_sem=recv.at[k], device_id=peers[0], device_id_type=MESH))
                    k += 1
        return remote, local

    return _Rider(ins, out_shape, n_remote, n_local, plan, aliases)


def _rider_ag_remote(shards):
    n = len(shards)

    def plan(ins, outs, send, recv, loc, r0, l0):
        x, y, c = _coords()
        remote = []
        for t in range(n):
            dst = outs[t].at[_slot((x, y, c))]
            for k, peer in enumerate([(x, y, 1 - c), (1 - x, y, c), (x, 1 - y, c), (1 - x, 1 - y, c)]):
                remote.append(pltpu.make_async_remote_copy(
                    src_ref=ins[t], dst_ref=dst, send_sem=send.at[r0 + 4 * t + k], recv_sem=recv.at[r0 + 4 * t + k],
                    device_id=peer, device_id_type=MESH))
        return remote, []

    return _Rider(shards, [jax.ShapeDtypeStruct((N_DEV,) + s.shape, s.dtype) for s in shards], 4 * n, 0, plan)


def _rider_rs_sibling(grads):
    n = len(grads)

    def plan(ins, outs, send, recv, loc, r0, l0):
        x, y, c = _coords()
        remote = []
        for t in range(n):
            for q in range(4):
                remote.append(pltpu.make_async_remote_copy(
                    src_ref=ins[t].at[q, 1 - c], dst_ref=outs[t].at[q], send_sem=send.at[r0 + 4 * t + q],
                    recv_sem=recv.at[r0 + 4 * t + q], device_id=(x, y, 1 - c), device_id_type=MESH))
        return remote, []

    return _Rider(grads, [jax.ShapeDtypeStruct((4,) + g.shape[2:], g.dtype) for g in grads], 4 * n, 0, plan)


def _rider_rs_chips(sums, rows=None, into=None):
    n = len(sums)
    rows = rows or [ALL] * n

    def plan(ins, outs, send, recv, loc, r0, l0):
        x, y, c = _coords()
        remote = []
        for t in range(n):
            for r, (px, py) in enumerate([(1 - x, y), (x, 1 - y), (1 - x, 1 - y)]):
                remote.append(pltpu.make_async_remote_copy(
                    src_ref=_rows(ins[t].at[2 * px + py], rows[t]), dst_ref=_rows(outs[t].at[r], rows[t]),
                    send_sem=send.at[r0 + 3 * t + r], recv_sem=recv.at[r0 + 3 * t + r],
                    device_id=(px, py, c), device_id_type=MESH))
        return remote, []

    out_shape = [jax.ShapeDtypeStruct((3,) + s.shape[1:], s.dtype) for s in sums]
    if into is None:
        return _Rider(sums, out_shape, 3 * n, 0, plan)
    return _Rider(list(sums) + list(into), out_shape, 3 * n, 0, plan, aliases={n + t: t for t in range(n)})


def _rider_gather_remote(parts):
    n = len(parts)

    def plan(ins, outs, send, recv, loc, r0, l0):
        x, y, c = _coords()
        me = _slot((x, y, c))
        remote = []
        for t in range(n):
            for k in range(1, N_DEV):
                peer = (x ^ ((k >> 2) & 1), y ^ ((k >> 1) & 1), c ^ (k & 1))
                remote.append(pltpu.make_async_remote_copy(
                    src_ref=ins[t], dst_ref=outs[t].at[me], send_sem=send.at[r0 + 7 * t + k - 1],
                    recv_sem=recv.at[r0 + 7 * t + k - 1], device_id=peer, device_id_type=MESH))
        return remote, []

    return _Rider(parts, [jax.ShapeDtypeStruct((N_DEV,) + p.shape, p.dtype) for p in parts], 7 * n, 0, plan)


def _chip_sum_call(idx, grads, recvd, out_dtypes, name):
    n = len(grads)

    def body(i_ref, *refs):
        for t in range(n):
            refs[2 * n + t][0] = (refs[t][0, 0] + refs[n + t][0]).astype(out_dtypes[t])

    def chip(k, s):
        return jnp.where(k >= s[0], k + 1, k)

    in_specs = [pl.BlockSpec((1, 1) + g.shape[2:], lambda k, s: (chip(k, s), s[1], 0, 0)) for g in grads]
    in_specs += [pl.BlockSpec((1,) + r.shape[1:], lambda k, s: (chip(k, s), 0, 0)) for r in recvd]
    return pl.pallas_call(
        body, name=name,
        grid_spec=pltpu.PrefetchScalarGridSpec(
            num_scalar_prefetch=1, grid=(3,), in_specs=in_specs,
            out_specs=[pl.BlockSpec((1,) + r.shape[1:], lambda k, s: (chip(k, s), 0, 0)) for r in recvd]),
        out_shape=[jax.ShapeDtypeStruct(r.shape, dt) for r, dt in zip(recvd, out_dtypes)],
        compiler_params=_params(("arbitrary",)),
    )(idx, *grads, *recvd)


def _final_sum_call(idx, grads, recvd1, recvd2):
    n = len(grads)
    nsteps = 2

    def body(i_ref, *refs):
        for t in range(n):
            g, r1, r2, o = refs[t], refs[n + t], refs[2 * n + t], refs[3 * n + t]
            s = g[0, 0] + r1[0]
            for r in range(3):
                s = s + r2[r].astype(F32)
            o[...] = s

    def rows(a):
        r = a.shape[-2]
        return r // nsteps if (r // nsteps) % 16 == 0 else r

    def step(a):
        return (lambda i: i) if rows(a) != a.shape[-2] else (lambda i: 0)

    in_specs = [pl.BlockSpec((1, 1, rows(g), g.shape[3]), lambda i, s, st=step(g): (s[0], s[1], st(i), 0)) for g in grads]
    in_specs += [pl.BlockSpec((1, rows(r), r.shape[2]), lambda i, s, st=step(r): (s[0], st(i), 0)) for r in recvd1]
    in_specs += [pl.BlockSpec((3, rows(r), r.shape[2]), lambda i, s, st=step(r): (0, st(i), 0)) for r in recvd2]
    return pl.pallas_call(
        body, name="rs_final_sum",
        grid_spec=pltpu.PrefetchScalarGridSpec(
            num_scalar_prefetch=1, grid=(nsteps,), in_specs=in_specs,
            out_specs=[pl.BlockSpec((rows(r), r.shape[2]), lambda i, s, st=step(r): (st(i), 0)) for r in recvd2]),
        out_shape=[jax.ShapeDtypeStruct(r.shape[1:], F32) for r in recvd2],
        compiler_params=_params(("arbitrary",)),
    )(idx, *grads, *recvd1, *recvd2)


def _sum8_call(parts):
    def body(p_ref, o_ref):
        s = p_ref[0]
        for j in range(1, N_DEV):
            s = s + p_ref[j]
        o_ref[...] = s

    return pl.pallas_call(body, name="sum_small_partials",
                          out_shape=jax.ShapeDtypeStruct(parts.shape[1:], parts.dtype))(parts)


def _adamw(w, g, m, v):
    m = ADAM_B1 * m + (1.0 - ADAM_B1) * g
    v = ADAM_B2 * v + (1.0 - ADAM_B2) * (g * g)
    m_hat = m / (1.0 - ADAM_B1 ** ADAM_STEP)
    v_hat = v / (1.0 - ADAM_B2 ** ADAM_STEP)
    delta = -ADAM_LR * (m_hat / (jnp.sqrt(v_hat) + ADAM_EPS) + ADAM_WD * w)
    return delta, m, v


def _adamw_call(ws, gs, ms, vs, nsteps, name):
    n = len(ws)

    def body(*refs):
        for t in range(n):
            w, g, m, v = (refs[k * n + t][...] for k in range(4))
            d, m2, v2 = _adamw(w, g, m, v)
            refs[4 * n + t][...] = d
            refs[5 * n + t][...] = m2
            refs[6 * n + t][...] = v2

    def spec(a):
        assert a.shape[0] % nsteps == 0 and (nsteps == 1 or (a.shape[0] // nsteps) % 8 == 0), a.shape
        return pl.BlockSpec((a.shape[0] // nsteps, a.shape[1]), lambda i: (i, 0))

    specs = [spec(a) for a in ws]
    outs = pl.pallas_call(
        body, name=name, grid=(nsteps,),
        in_specs=specs * 4, out_specs=specs * 3,
        out_shape=[jax.ShapeDtypeStruct(a.shape, F32) for a in ws] * 3,
        compiler_params=_params(("arbitrary",)),
    )(*ws, *gs, *ms, *vs)
    return outs[:n], outs[n:2 * n], outs[2 * n:]


def _adamw_rs_call(idx, after, gws, r1s, r2s, ws, ms, vs, nsteps, name):
    n = len(ws)

    def body(i_ref, after_ref, *refs):
        for t in range(n):
            gw, r1, r2, w, m, v = (refs[k * n + t] for k in range(6))
            g = gw[0, 0] + r1[0]
            for r in range(3):
                g = g + r2[r].astype(F32)
            d, m2, v2 = _adamw(w[...], g, m[...], v[...])
            refs[6 * n + t][...] = g
            refs[7 * n + t][...] = d
            refs[8 * n + t][...] = m2
            refs[9 * n + t][...] = v2

    def rb(a):
        r = a.shape[0] // nsteps
        assert a.shape[0] % nsteps == 0 and r % 16 == 0, a.shape
        return r

    in_specs = [pl.BlockSpec((1, 1, rb(w), w.shape[1]), lambda i, s: (s[0], s[1], i, 0)) for w in ws]
    in_specs += [pl.BlockSpec((1, rb(w), w.shape[1]), lambda i, s: (s[0], i, 0)) for w in ws]
    in_specs += [pl.BlockSpec((3, rb(w), w.shape[1]), lambda i, s: (0, i, 0)) for w in ws]
    plain = [pl.BlockSpec((rb(w), w.shape[1]), lambda i, s: (i, 0)) for w in ws]
    outs = pl.pallas_call(
        body, name=name,
        grid_spec=pltpu.PrefetchScalarGridSpec(
            num_scalar_prefetch=1, grid=(nsteps,),
            in_specs=[pl.BlockSpec(memory_space=pl.ANY)] + in_specs + plain * 3, out_specs=plain * 4),
        out_shape=[jax.ShapeDtypeStruct(w.shape, F32) for w in ws] * 4,
        compiler_params=_params(("arbitrary",)),
    )(idx, after, *gws, *r1s, *r2s, *ws, *ms, *vs)
    return outs[:n], outs[n:2 * n], outs[2 * n:3 * n], outs[3 * n:]


def _rows128(a, pad_rows):
    flat = a.reshape(-1).astype(F32)
    flat = jnp.pad(flat, (0, pad_rows * LANES - flat.shape[0]))
    return flat.reshape(pad_rows, LANES)


_SMALL_A = (("w_pool", 512), ("pool_scale", 8), ("attn_sinks", 8), ("g_mix_post", 8), ("g_mlp_pre", 8),
            ("g_mlp_post", 8), ("loss", 8), ("b_in_gates", 16))
_SMALL_A_ROWS = 640
_SMALL_B = (("g_mix_pre", 8), ("b_in_head", 16))


def _pack(parts, layout, total_rows):
    rows = [_rows128(parts[k], r) for k, r in layout]
    pad = total_rows - sum(r for _, r in layout)
    if pad:
        rows.append(jnp.zeros((pad, LANES), F32))
    return jnp.concatenate(rows, axis=0)


def _unpack(buf, layout, sizes):
    out, off = {}, 0
    for k, r in layout:
        out[k] = buf[off:off + r].reshape(-1)[:sizes[k]]
        off += r
    return out


def kernel(x, g_mix_pre, w_in, b_in, w_pool, pool_scale, attn_sinks, w_branch_pool, w_branch_attn, w_out, g_mix_post, g_mlp_pre, w_up, w_down, g_mlp_post, loss_target, m_g_mix_pre, m_w_in, m_b_in, m_w_pool, m_pool_scale, m_attn_sinks, m_w_branch_pool, m_w_branch_attn, m_w_out, m_g_mix_post, m_g_mlp_pre, m_w_up, m_w_down, m_g_mlp_post, v_g_mix_pre, v_w_in, v_b_in, v_w_pool, v_pool_scale, v_attn_sinks, v_w_branch_pool, v_w_branch_attn, v_w_out, v_g_mix_post, v_g_mlp_pre, v_w_up, v_w_down, v_g_mlp_post):
    B, S, _ = x.shape
    T = B * S
    xt = x.reshape(T, D_MODEL)
    tgt = loss_target.reshape(T, D_MODEL)
    cx, cy, cc = _coords()

    cidx = jnp.stack([2 * cx + cy, cc]).astype(jnp.int32)
    by_chip = lambda gr: gr.reshape((4, 2) + gr.shape[1:])
    bf = lambda w: w[0].astype(MXU_DTYPE)

    (win_s,) = _allgather_call([w_in[0].T.astype(MXU_DTYPE)])
    win_t = win_s.reshape(IN_WIDTH, D_MODEL)
    wpool_b = bf(w_pool)
    rc, rsa, rsb = _rot_tables(S)

    wbp_l, wba_l, wout_l, wup_l, wdown_l = bf(w_branch_pool), bf(w_branch_attn), bf(w_out), bf(w_up), bf(w_down)
    (c_br, c_up, c_dn), tok = _copies_start(
        [_rider_ag_remote([wbp_l, wba_l, wout_l]), _rider_ag_remote([wup_l]), _rider_ag_remote([wdown_l])],
        "allgather_start", after=win_s)
    (h, u, q, k4, v4, g), _ = _inproj_call(xt, g_mix_pre, win_t, b_in, rc, rsa, rsb, S, rider=_after(tok))
    yp = _pool_call(u, wpool_b, pool_scale, S)
    wbp_1, wba_1, wout_1 = _copies_wait([c_br], yp, "allgather_wait_branch")
    (ya,), (wbp_s, wba_s, wout_s) = _attn_call(
        attn_sinks, q, k4, v4, S,
        rider=_rider_ag([(wbp_l, wbp_1, LOCAL, ALL), (wba_l, wba_1, LOCAL, ALL), (wout_l, wout_1, LOCAL, ALL)]))
    wout_f = wout_s.reshape(D_MODEL, D_MODEL)
    (wup_1,) = _copies_wait([c_up], ya, "allgather_wait_up")
    (mix, x1, h2, h2_t), (wup_s,) = _mix_fwd_call(
        yp, ya, g, xt, wbp_s, wba_s, wout_f, g_mix_post, g_mlp_pre, rider=_rider_ag([(wup_l, wup_1, LOCAL, ALL)]))
    (wdown_1,) = _copies_wait([c_dn], h2, "allgather_wait_down")
    (wdown_s,) = _comm_call(_rider_ag([(wdown_l, wdown_1, LOCAL, ALL)]), "allgather_pass_down")

    act_t, da, dff, dx1, dg3, dg4, lossvec = _mlp_call(x1, h2, tgt, wup_s, wdown_s, g_mlp_pre, g_mlp_post)
    gw_down = by_chip(_wgrad_rows_call(act_t, dff, "wgrad_down")[0])
    (gw_up,), (r1_down,) = _wgrad_cols_call(h2_t, da, "wgrad_up", rider=_rider_rs_sibling([gw_down]))
    gw_up = by_chip(gw_up)
    (c_sib_up,), tok = _copies_start([_rider_rs_sibling([gw_up])], "rs_sibling_start_up")
    (s_down,) = _chip_sum_call(cidx, [gw_down], [r1_down], [MXU_DTYPE], "rs_chip_sum_down")
    (c_down,), tok = _copies_start([_rider_rs_chips([s_down])], "rs_chips_start_down", after=tok)
    (dyp, do, dgates, dg2, dbg, gw_out, gw_bp, gw_ba), _ = _mix_bwd_call(
        dx1, mix, yp, ya, g, wbp_s, wba_s, wout_f, g_mix_post, rider=_after(tok))
    gw_out = by_chip(gw_out.reshape(N_DEV, D_MODEL // N_DEV, D_MODEL))
    gw_bp, gw_ba = by_chip(gw_bp), by_chip(gw_ba)
    (gw_up,), (r1_up,) = _copies_wait([c_sib_up], dg2, "rs_sibling_wait_up", with_sources=True)
    (s_up,) = _chip_sum_call(cidx, [gw_up], [r1_up], [MXU_DTYPE], "rs_chip_sum_up")
    (c_up,), tok = _copies_start([_rider_rs_chips([s_up])], "rs_chips_start_up")
    (dq, dk, dv, dsink), (r1_out, r1_bp, r1_ba) = _attn_bwd_call(
        attn_sinks, q, k4, v4, do, rc, rsa, rsb, S, rider=_after(tok, _rider_rs_sibling([gw_out, gw_bp, gw_ba])))
    s_obb = _chip_sum_call(cidx, [gw_out, gw_bp, gw_ba], [r1_out, r1_bp, r1_ba], [MXU_DTYPE] * 3, "rs_chip_sum_branch")
    (c_obb,), tok = _copies_start([_rider_rs_chips(s_obb)], "rs_chips_start_branch")
    (du, dwp, dps), _ = _pool_bwd_call(u, dyp, wpool_b, pool_scale, S, rider=_after(tok))
    (gw_in,) = _wgrad_in_call(du, dq, dk, dv, dgates, h)
    gw_in = by_chip(gw_in)

    small_a = {"w_pool": dwp, "pool_scale": dps,
               "attn_sinks": jnp.sum(dsink.reshape(B, 8, LANES)[:, 0, :N_Q_HEADS], axis=0), "g_mix_post": dg2,
               "g_mlp_pre": dg3, "g_mlp_post": dg4, "loss": lossvec, "b_in_gates": dbg}
    gw_sa = by_chip(_pack(small_a, _SMALL_A, _SMALL_A_ROWS).reshape(N_DEV, _SMALL_A_ROWS // N_DEV, LANES))
    r1_in, r1_sa = _comm_call(_rider_rs_sibling([gw_in, gw_sa]), "rs_sibling_in")
    s_in, s_sa = _chip_sum_call(cidx, [gw_in, gw_sa], [r1_in, r1_sa], [MXU_DTYPE, F32], "rs_chip_sum_in")
    (c_in,), tok = _copies_start([_rider_rs_chips([s_in, s_sa])], "rs_chips_start_in")
    (gx, dg1, dba_in), _ = _inproj_bwd_call(du, dq, dk, dv, dgates, dx1, xt, win_t, g_mix_pre, rider=_after(tok))
    r2_down, r2_up, r2_out, r2_bp, r2_ba, r2_in, r2_sa = _copies_wait([c_down, c_up, c_obb, c_in], dg1, "rs_chips_wait")

    (g_sa,) = _final_sum_call(cidx, [gw_sa], [r1_sa], [r2_sa])
    part_b = _pack({"g_mix_pre": dg1, "b_in_head": dba_in}, _SMALL_B, sum(r for _, r in _SMALL_B))
    (c_small,), tok = _copies_start([_rider_gather_remote([g_sa, part_b])], "allgather_small_start")

    in_t = _adamw_rs_call(cidx, tok, [gw_in], [r1_in], [r2_in], [w_in[0].T], [m_w_in[0].T], [v_w_in[0].T], 2,
                          "adamw_w_in")
    rest = _adamw_rs_call(
        cidx, tok, [gw_bp, gw_ba, gw_out, gw_up, gw_down], [r1_bp, r1_ba, r1_out, r1_up, r1_down],
        [r2_bp, r2_ba, r2_out, r2_up, r2_down], [w_branch_pool[0], w_branch_attn[0], w_out[0], w_up[0], w_down[0]],
        [m_w_branch_pool[0], m_w_branch_attn[0], m_w_out[0], m_w_up[0], m_w_down[0]],
        [v_w_branch_pool[0], v_w_branch_attn[0], v_w_out[0], v_w_up[0], v_w_down[0]], N_DEV, "adamw_shards")
    big_g, big_d, big_m2, big_v2 = ([a[0].T] + list(b) for a, b in zip(in_t, rest))

    sa_all, sb_all = _copies_wait([c_small], rest[0][0], "allgather_small_wait")
    me = (_slot((cx, cy, cc)), 0, 0)
    sa_all = lax.dynamic_update_slice(sa_all, g_sa[None], me)
    sb_sum = _sum8_call(lax.dynamic_update_slice(sb_all, part_b[None], me))

    names = ["g_mix_pre", "b_in", "w_pool", "pool_scale", "attn_sinks", "g_mix_post", "g_mlp_pre", "g_mlp_post"]
    sm_w = dict(g_mix_pre=g_mix_pre, b_in=b_in, w_pool=w_pool, pool_scale=pool_scale, attn_sinks=attn_sinks,
                g_mix_post=g_mix_post, g_mlp_pre=g_mlp_pre, g_mlp_post=g_mlp_post)
    sm_m = dict(g_mix_pre=m_g_mix_pre, b_in=m_b_in, w_pool=m_w_pool, pool_scale=m_pool_scale, attn_sinks=m_attn_sinks,
                g_mix_post=m_g_mix_post, g_mlp_pre=m_g_mlp_pre, g_mlp_post=m_g_mlp_post)
    sm_v = dict(g_mix_pre=v_g_mix_pre, b_in=v_b_in, w_pool=v_w_pool, pool_scale=v_pool_scale, attn_sinks=v_attn_sinks,
                g_mix_post=v_g_mix_post, g_mlp_pre=v_g_mlp_pre, g_mlp_post=v_g_mlp_post)
    sizes = {k: sm_w[k].size for k in names}
    sizes.update(loss=D_MODEL, b_in_gates=GATE_WIDTH, b_in_head=C_G)
    sm_g = _unpack(sa_all.reshape(_SMALL_A_ROWS, LANES), _SMALL_A, sizes)
    sm_g.update(_unpack(sb_sum, _SMALL_B, sizes))
    sm_g["b_in"] = jnp.concatenate([sm_g["b_in_head"], sm_g["b_in_gates"]])
    loss = (0.5 / D_MODEL) * jnp.sum(sm_g["loss"])
    two_d = lambda a: a.reshape(-1, a.shape[-1])
    sd_, sm2_, sv2_ = _adamw_call([two_d(sm_w[k]) for k in names], [two_d(sm_g[k].reshape(sm_w[k].shape)) for k in names],
                                  [two_d(sm_m[k]) for k in names], [two_d(sm_v[k]) for k in names], 1, "adamw_small")
    like = lambda vals: {k: a.reshape(sm_w[k].shape) for k, a in zip(names, vals)}
    sm_d, sm_m2, sm_v2 = like(sd_), like(sm2_), like(sv2_)
    sm_gr = {k: sm_g[k].reshape(sm_w[k].shape) for k in names}

    order = ["g_mix_pre", "w_in", "b_in", "w_pool", "pool_scale", "attn_sinks", "w_branch_pool", "w_branch_attn",
             "w_out", "g_mix_post", "g_mlp_pre", "w_up", "w_down", "g_mlp_post"]
    big_names = ["w_in", "w_branch_pool", "w_branch_attn", "w_out", "w_up", "w_down"]
    lead = lambda a: a[None]
    tables = []
    for small_t, big_t in ((sm_gr, big_g), (sm_d, big_d), (sm_m2, big_m2), (sm_v2, big_v2)):
        bt = dict(zip(big_names, big_t))
        tables.append([lead(bt[k]) if k in bt else small_t[k] for k in order])
    return (loss, gx.reshape(B, S, D_MODEL), *tables[0], *tables[1], *tables[2], *tables[3])
```

```python
import jax
import jax.numpy as jnp
from jax import lax
from jax.experimental import pallas as pl
from jax.experimental.pallas import tpu as pltpu

F32 = jnp.float32
MXU_DTYPE = jnp.bfloat16
MESH = pl.DeviceIdType.MESH

D_MODEL = 1024
POOL_WINDOWS = (2, 4, 8, 16)
POOL_WIDTH = 512
POOL_GC = 128
HEAD_DIM = 64
N_Q_HEADS = 8
N_KV_HEADS = 2
GROUP = 4
ATTN_WIDTH = 512
KV_WIDTH = 128
BLOCK = 128
GATE_WIDTH = 2048
IN_WIDTH = 3328
D_FF = 4096
EPS = 1e-6
NEG_INF = -1e30
ROPE_THETA = 500000.0
ROT_DIM = 16
SCALE = HEAD_DIM ** -0.5
C_Q, C_K, C_V, C_G = 512, 1024, 1152, 1280

ADAM_LR = 0.001
ADAM_B1 = 0.9
ADAM_B2 = 0.999
ADAM_EPS = 1e-08
ADAM_WD = 0.01
ADAM_STEP = 10

N_DEV = 8
LANES = 128
VMEM_LIMIT = 56 * 1024 * 1024

NN = (((1,), (0,)), ((), ()))
NT = (((1,), (1,)), ((), ()))
TN = (((0,), (0,)), ((), ()))


def _dot(a, b, dims):
    return lax.dot_general(a, b, dims, preferred_element_type=F32)


def _params(sem=None):
    return pltpu.CompilerParams(dimension_semantics=sem, vmem_limit_bytes=VMEM_LIMIT)


def _tile(n, pref):
    t = min(n, pref)
    assert n % t == 0, (n, t)
    return t


class _Rider:
    def __init__(self, ins, out_shape, n_remote, n_local, plan, aliases=None):
        self.ins, self.out_shape, self.n_remote, self.n_local = list(ins), list(out_shape), n_remote, n_local
        self.plan, self.aliases = plan, dict(aliases or {})


def _after(token, rider=None):
    r = rider or _Rider([], [], 0, 0, lambda ins, outs, send, recv, loc, r0, l0: ([], []))
    return _Rider(r.ins + [token], r.out_shape, r.n_remote, r.n_local, r.plan, r.aliases)


def _launch(body, args, *, name, grid, in_specs, out_specs, out_shape, scratch_shapes=(), sem=None, rider=None):
    if rider is None:
        return pl.pallas_call(body, name=name, grid=grid, in_specs=in_specs, out_specs=out_specs, out_shape=out_shape,
                              scratch_shapes=list(scratch_shapes), compiler_params=_params(sem))(*args)
    n_in, n_out, n_scr = len(args), len(out_shape), len(scratch_shapes)
    r_in, r_out = len(rider.ins), len(rider.out_shape)
    copies = rider.n_remote + rider.n_local > 0

    def wrapped(*refs):
        ins, rins = refs[:n_in], refs[n_in:n_in + r_in]
        o0 = n_in + r_in
        outs, routs = refs[o0:o0 + n_out], refs[o0 + n_out:o0 + n_out + r_out]
        s0 = o0 + n_out + r_out
        scr = refs[s0:s0 + n_scr]
        if not copies:
            return body(*ins, *outs, *scr)
        send, recv, loc = refs[s0 + n_scr:]
        first, last = None, None
        for d in range(len(grid)):
            f, l = pl.program_id(d) == 0, pl.program_id(d) == pl.num_programs(d) - 1
            first = f if first is None else first & f
            last = l if last is None else last & l

        def start():
            remote, local = rider.plan(rins, routs, send, recv, loc, 0, 0)
            for cp in local + remote:
                cp.start()

        def finish():
            remote, local = rider.plan(rins, routs, send, recv, loc, 0, 0)
            for cp in remote + local:
                cp.wait()

        if first is None:
            start()
            body(*ins, *outs, *scr)
            finish()
        else:
            pl.when(first)(start)
            body(*ins, *outs, *scr)
            pl.when(last)(finish)

    hbm = pl.BlockSpec(memory_space=pl.ANY)
    dma = pltpu.SemaphoreType.DMA
    res = pl.pallas_call(
        wrapped, name=name, grid=grid, in_specs=list(in_specs) + [hbm] * r_in,
        out_specs=list(out_specs) + [hbm] * r_out, out_shape=list(out_shape) + rider.out_shape,
        scratch_shapes=list(scratch_shapes) + (
            [dma((max(rider.n_remote, 1),)), dma((max(rider.n_remote, 1),)), dma((max(rider.n_local, 1),))] if copies else []),
        input_output_aliases={n_in + i: n_out + o for i, o in rider.aliases.items()},
        compiler_params=_params(sem),
    )(*args, *rider.ins)
    return list(res[:n_out]), list(res[n_out:])


def _comm_call(rider, name):
    return _launch(lambda: None, [], name=name, grid=(), in_specs=[], out_specs=[], out_shape=[], rider=rider)[1]


_HBM = pl.BlockSpec(memory_space=pltpu.HBM)
_SEM = pl.BlockSpec(memory_space=pltpu.SEMAPHORE)
_EFFECT = pltpu.SideEffectType.DATAFLOW_SIDE_EFFECTING


def _copies_start(riders, name, after=None):
    assert all(r.n_local == 0 and not r.aliases for r in riders)
    extra = [] if after is None else [after]
    sizes = [(len(r.ins), len(r.out_shape)) for r in riders]
    bufs = []
    for r in riders:
        bufs += [pltpu.with_memory_space_constraint(a, pltpu.HBM) for a in r.ins]
        bufs += [pltpu.with_memory_space_constraint(lax.empty(s.shape, s.dtype), pltpu.HBM) for s in r.out_shape]
    nb, ng, ne = len(bufs), len(riders), len(extra)

    def body(*refs):
        sems, token, at = refs[2 * nb + ne:2 * nb + ne + 2 * ng], refs[-1], 0
        for g, (r, (ni, no)) in enumerate(zip(riders, sizes)):
            remote, _ = r.plan(refs[at:at + ni], refs[at + ni:at + ni + no], sems[2 * g], sems[2 * g + 1], None, 0, 0)
            for cp in remote:
                cp.start()
            at += ni + no
        token[...] = jnp.zeros_like(token)

    res = pl.pallas_call(
        body, name=name, in_specs=[_HBM] * nb + [pl.BlockSpec(memory_space=pl.ANY)] * ne,
        out_specs=[_HBM] * nb + [_SEM] * (2 * ng) + [pl.BlockSpec(memory_space=pltpu.VMEM)],
        out_shape=[pltpu.HBM(a.shape, a.dtype) for a in bufs]
        + [pltpu.SemaphoreType.DMA((r.n_remote,)) for r in riders for _ in range(2)]
        + [jax.ShapeDtypeStruct((8, LANES), F32)],
        input_output_aliases={i: i for i in range(nb)},
        compiler_params=pltpu.CompilerParams(has_side_effects=_EFFECT),
    )(*bufs, *extra)
    handles, at = [], 0
    for g, (r, (ni, no)) in enumerate(zip(riders, sizes)):
        handles.append((r, list(res[at:at + ni + no]), res[nb + 2 * g], res[nb + 2 * g + 1]))
        at += ni + no
    return handles, res[-1]


def _copies_wait(handles, after, name):
    bufs = [b for _, bs, _, _ in handles for b in bs]
    sems = [s for _, _, send, recv in handles for s in (send, recv)]
    nb, ng = len(bufs), len(handles)

    def body(*refs):
        at = 0
        for g, (rider, bs, _, _) in enumerate(handles):
            ni = len(rider.ins)
            remote, _ = rider.plan(refs[at:at + ni], refs[at + ni:at + len(bs)], refs[nb + 2 * g], refs[nb + 2 * g + 1],
                                   None, 0, 0)
            for cp in remote:
                cp.wait_send()
                cp.wait_recv()
            at += len(bs)

    res = pl.pallas_call(
        body, name=name, in_specs=[_HBM] * nb + [_SEM] * (2 * ng) + [pl.BlockSpec(memory_space=pl.ANY)],
        out_specs=[_HBM] * nb, out_shape=[pltpu.HBM(a.shape, a.dtype) for a in bufs],
        input_output_aliases={i: i for i in range(nb)},
        compiler_params=pltpu.CompilerParams(has_side_effects=_EFFECT),
    )(*bufs, *sems, after)
    lands, at = [], 0
    for rider, bs, _, _ in handles:
        lands += list(res[at + len(rider.ins):at + len(bs)])
        at += len(bs)
    return lands


def _rms_r(x):
    return lax.rsqrt(jnp.mean(x * x, axis=-1, keepdims=True) + EPS)


def _rms_bwd(dn, x, r, g):
    xh = x * r
    dxh = dn * g
    dx = r * (dxh - xh * jnp.mean(dxh * xh, axis=-1, keepdims=True))
    return dx, dn * xh


def _rot(t, c, sa, sb):
    outs = []
    for j in range(t.shape[1] // LANES):
        tj = t[:, LANES * j:LANES * (j + 1)]
        outs.append(tj * c + pltpu.roll(tj, LANES - 8, 1) * sa + pltpu.roll(tj, 8, 1) * sb)
    return outs[0] if len(outs) == 1 else jnp.concatenate(outs, axis=1)


def _rot_tables(S):
    pos = jnp.arange(S, dtype=F32)
    inv_freq = ROPE_THETA ** (-jnp.arange(0, ROT_DIM, 2, dtype=F32) / ROT_DIM)
    ang = pos[:, None] * inv_freq[None, :]
    cos, sin = jnp.cos(ang), jnp.sin(ang)
    one = jnp.ones((S, HEAD_DIM - ROT_DIM), F32)
    zero = jnp.zeros((S, HEAD_DIM - ROT_DIM), F32)
    z8 = jnp.zeros((S, 8), F32)
    c = jnp.concatenate([cos, cos, one], axis=1)
    sa = jnp.concatenate([-sin, z8, zero], axis=1)
    sb = jnp.concatenate([z8, sin, zero], axis=1)
    rep = LANES // HEAD_DIM
    return jnp.tile(c, (1, rep)), jnp.tile(sa, (1, rep)), jnp.tile(sb, (1, rep))


def _lane_tile4(k):
    lane = lax.broadcasted_iota(jnp.int32, k.shape, 1)
    rk = pltpu.roll(k, HEAD_DIM, 1)
    x0 = jnp.where(lane < HEAD_DIM, k, rk)
    x1 = jnp.where(lane < HEAD_DIM, rk, k)
    return jnp.concatenate([x0, x0, x1, x1], axis=1)


def _fold_heads(acc):
    zs = []
    for hk in range(N_KV_HEADS):
        a = acc[:, 256 * hk:256 * hk + LANES] + acc[:, 256 * hk + LANES:256 * (hk + 1)]
        zs.append(a + pltpu.roll(a, HEAD_DIM, 1))
    lane = lax.broadcasted_iota(jnp.int32, zs[0].shape, 1)
    return jnp.where(lane < HEAD_DIM, zs[0], zs[1])


def _inproj_call(x, g1, win_t, b_in, rc, rsa, rsb, S, rider=None):
    T = x.shape[0]
    tm = _tile(S, 512)
    nst = S // tm

    def body(x_ref, g1_ref, w_ref, b_ref, c_ref, sa_ref, sb_ref,
             h_ref, u_ref, q_ref, k4_ref, v4_ref, g_ref):
        xv = x_ref[...]
        hb = ((xv * _rms_r(xv)) * g1_ref[...]).astype(MXU_DTYPE)
        h_ref[...] = hb

        def proj(lo, hi):
            return _dot(hb, w_ref[lo:hi, :], NT) + b_ref[:, lo:hi]

        c, sa, sb = c_ref[...], sa_ref[...], sb_ref[...]
        u_ref[...] = proj(0, C_Q)
        q_ref[...] = (_rot(proj(C_Q, C_K), c, sa, sb) * SCALE).astype(MXU_DTYPE)
        kv = proj(C_K, C_G)
        k4_ref[...] = _lane_tile4(_rot(kv[:, :KV_WIDTH], c, sa, sb)).astype(MXU_DTYPE)
        v4_ref[...] = _lane_tile4(kv[:, KV_WIDTH:]).astype(MXU_DTYPE)
        g_ref[...] = jax.nn.sigmoid(proj(C_G, IN_WIDTH)).astype(MXU_DTYPE)

    tok = lambda w: pl.BlockSpec((tm, w), lambda i: (i, 0))
    full = lambda a: pl.BlockSpec(a.shape, lambda i: (0,) * a.ndim)
    tab = pl.BlockSpec((tm, LANES), lambda i: (i % nst, 0))
    return _launch(
        body, [x, g1, win_t, b_in, rc, rsa, rsb], name="inproj_fwd", grid=(T // tm,),
        in_specs=[tok(D_MODEL), full(g1), full(win_t), full(b_in), tab, tab, tab],
        out_specs=[tok(D_MODEL), tok(POOL_WIDTH), tok(ATTN_WIDTH), tok(512), tok(512), tok(GATE_WIDTH)],
        out_shape=[jax.ShapeDtypeStruct((T, D_MODEL), MXU_DTYPE), jax.ShapeDtypeStruct((T, POOL_WIDTH), F32),
                   jax.ShapeDtypeStruct((T, ATTN_WIDTH), MXU_DTYPE), jax.ShapeDtypeStruct((T, 512), MXU_DTYPE),
                   jax.ShapeDtypeStruct((T, 512), MXU_DTYPE), jax.ShapeDtypeStruct((T, GATE_WIDTH), MXU_DTYPE)],
        sem=("arbitrary",), rider=rider)


def _shift_rows(a, k, rows):
    n = a.shape[0]
    if k > 0:
        return jnp.where(rows >= k, pltpu.roll(a, k, 0), 0.0)
    return jnp.where(rows < n + k, pltpu.roll(a, n + k, 0), 0.0)


def _win_sum(a, w, rows, sign):
    s, k = a, 1
    while k < w:
        s = s + _shift_rows(s, sign * k, rows)
        k *= 2
    return s


def _pool_diff(ug, w, rows):
    inv = 1.0 / jnp.minimum(rows + 1, w).astype(F32)
    return _win_sum(ug, w, rows, 1) * inv - ug, inv


def _pool_call(u, w_pool, pool_scale, S):
    T = u.shape[0]

    def body(u_ref, w_ref, ps_ref, y_ref):
        rows = lax.broadcasted_iota(jnp.int32, (S, POOL_GC), 0)
        for gi, w in enumerate(POOL_WINDOWS):
            sl = slice(POOL_GC * gi, POOL_GC * (gi + 1))
            diff, _ = _pool_diff(u_ref[:, sl], w, rows)
            mixed = _dot(diff.astype(MXU_DTYPE), w_ref[gi], NN)
            y_ref[:, sl] = (mixed * ps_ref[:, sl]).astype(MXU_DTYPE)

    seq = pl.BlockSpec((S, POOL_WIDTH), lambda b: (b, 0))
    return pl.pallas_call(
        body, name="pool_fwd", grid=(T // S,),
        in_specs=[seq, pl.BlockSpec(w_pool.shape, lambda b: (0, 0, 0)), pl.BlockSpec(pool_scale.shape, lambda b: (0, 0))],
        out_specs=seq, out_shape=jax.ShapeDtypeStruct((T, POOL_WIDTH), MXU_DTYPE),
        compiler_params=_params(("arbitrary",)),
    )(u, w_pool, pool_scale)


def _pool_bwd_call(u, dyp, w_pool, pool_scale, S, rider=None):
    T = u.shape[0]

    def body(u_ref, dy_ref, w_ref, ps_ref, du_ref, dw_ref, dps_ref):
        @pl.when(pl.program_id(0) == 0)
        def _():
            dw_ref[...] = jnp.zeros_like(dw_ref)
            dps_ref[...] = jnp.zeros_like(dps_ref)

        rows = lax.broadcasted_iota(jnp.int32, (S, POOL_GC), 0)
        for gi, w in enumerate(POOL_WINDOWS):
            sl = slice(POOL_GC * gi, POOL_GC * (gi + 1))
            diff, inv = _pool_diff(u_ref[:, sl], w, rows)
            diffb = diff.astype(MXU_DTYPE)
            wg = w_ref[gi]
            mixed = _dot(diffb, wg, NN)
            dy = dy_ref[:, sl]
            dps_ref[:, sl] += jnp.sum(dy * mixed, axis=0, keepdims=True)
            dmb = (dy * ps_ref[:, sl]).astype(MXU_DTYPE)
            dw_ref[gi] += _dot(diffb, dmb, TN)
            ddiff = _dot(dmb, wg, NT)
            du_ref[:, sl] = (_win_sum(ddiff * inv, w, rows, -1) - ddiff).astype(MXU_DTYPE)

    seq = pl.BlockSpec((S, POOL_WIDTH), lambda b: (b, 0))
    return _launch(
        body, [u, dyp, w_pool, pool_scale], name="pool_bwd", grid=(T // S,),
        in_specs=[seq, seq, pl.BlockSpec(w_pool.shape, lambda b: (0, 0, 0)), pl.BlockSpec(pool_scale.shape, lambda b: (0, 0))],
        out_specs=[seq, pl.BlockSpec(w_pool.shape, lambda b: (0, 0, 0)), pl.BlockSpec(pool_scale.shape, lambda b: (0, 0))],
        out_shape=[jax.ShapeDtypeStruct((T, POOL_WIDTH), MXU_DTYPE), jax.ShapeDtypeStruct(w_pool.shape, F32),
                   jax.ShapeDtypeStruct(pool_scale.shape, F32)],
        sem=("arbitrary",), rider=rider)


def _attn_consts():
    lane_g = lax.broadcasted_iota(jnp.int32, (BLOCK, 256), 1) >> 6
    rgrp = lax.broadcasted_iota(jnp.int32, (GROUP * BLOCK, 1), 0) >> 7
    rel = lax.broadcasted_iota(jnp.int32, (BLOCK, 256), 0) - lax.broadcasted_iota(jnp.int32, (BLOCK, 256), 1)

    def bias(off):
        ok = (rel + off >= 0) & (rel + off < BLOCK)
        return jnp.concatenate([jnp.where(ok, 0.0, NEG_INF)] * GROUP, axis=0)

    return lane_g, rgrp, bias(0), bias(BLOCK)


def _sink_rows(sink_ref, hk, rgrp):
    sv = jnp.zeros(rgrp.shape, F32)
    for g in range(GROUP):
        sv = jnp.where(rgrp == g, sink_ref[0, GROUP * hk + g], sv)
    return sv


def _stack_heads(xb, lane_g):
    return jnp.concatenate([jnp.where(lane_g == g, xb, jnp.zeros_like(xb)) for g in range(GROUP)], axis=0)


def _unstack_heads(xs, lane_g):
    out = jnp.where(lane_g == 0, xs[0:BLOCK], 0.0)
    for g in range(1, GROUP):
        out = out + jnp.where(lane_g == g, xs[BLOCK * g:BLOCK * (g + 1)], 0.0)
    return out


def _attn_probs(qs, kb, bias, sv):
    s = _dot(qs, kb, NT) + bias
    m = jnp.maximum(jnp.max(s, axis=1, keepdims=True), sv)
    e = jnp.exp(s - m)
    es = jnp.exp(sv - m)
    inv_l = 1.0 / (jnp.sum(e, axis=1, keepdims=True) + es)
    return e * inv_l, es * inv_l


def _attn_blocks(nb, blk, carry, per=1):
    carry = blk(0, 0, True, carry)
    per = per if (nb - 1) % per == 0 else 1

    def step(i, c):
        for k in range(per):
            n = 1 + per * i + k
            c = blk(pl.multiple_of(n * BLOCK, BLOCK), pl.multiple_of((n - 1) * BLOCK, BLOCK), False, c)
        return c

    return lax.fori_loop(0, (nb - 1) // per, step, carry)


def _attn_call(sinks, q, k4, v4, S, rider=None):
    T = q.shape[0]
    nb = S // BLOCK

    def body(sink_ref, q_ref, k_ref, v_ref, o_ref):
        lane_g, rgrp, bias_first, bias_later = _attn_consts()
        svs = [_sink_rows(sink_ref, hk, rgrp) for hk in range(N_KV_HEADS)]

        def blk(q0, k0, first, carry):
            for hk in range(N_KV_HEADS):
                cs = slice(256 * hk, 256 * (hk + 1))
                qs = _stack_heads(q_ref[pl.ds(q0, BLOCK), cs], lane_g)
                p, _ = _attn_probs(qs, k_ref[pl.ds(k0, 2 * BLOCK), cs], bias_first if first else bias_later, svs[hk])
                o = _dot(p.astype(MXU_DTYPE), v_ref[pl.ds(k0, 2 * BLOCK), cs], NN)
                o_ref[pl.ds(q0, BLOCK), cs] = _unstack_heads(o, lane_g).astype(MXU_DTYPE)
            return carry

        _attn_blocks(nb, blk, 0, per=3)

    seq = pl.BlockSpec((S, ATTN_WIDTH), lambda b: (b, 0))
    return _launch(
        body, [sinks, q, k4, v4], name="attn_fwd", grid=(T // S,),
        in_specs=[pl.BlockSpec(memory_space=pltpu.SMEM), seq, seq, seq],
        out_specs=[seq], out_shape=[jax.ShapeDtypeStruct((T, ATTN_WIDTH), MXU_DTYPE)],
        sem=("arbitrary",), rider=rider)


def _attn_bwd_call(sinks, q, k4, v4, do, rc, rsa, rsb, S, rider=None):
    T = q.shape[0]
    nb = S // BLOCK

    def body(sink_ref, q_ref, k_ref, v_ref, do_ref, c_ref, sa_ref, sb_ref,
             dq_ref, dk_ref, dv_ref, ds_ref, dk_acc, dv_acc):
        lane_g, rgrp, bias_first, bias_later = _attn_consts()
        svs = [_sink_rows(sink_ref, hk, rgrp) for hk in range(N_KV_HEADS)]
        lane1 = lax.broadcasted_iota(jnp.int32, (1, LANES), 1)
        dk_acc[...] = jnp.zeros_like(dk_acc)
        dv_acc[...] = jnp.zeros_like(dv_acc)

        def blk(q0, k0, first, dsink):
            rows = pl.ds(q0, BLOCK)
            c, sa, sb = c_ref[rows, :], sa_ref[rows, :], sb_ref[rows, :]
            for hk in range(N_KV_HEADS):
                cs = slice(256 * hk, 256 * (hk + 1))
                qs = _stack_heads(q_ref[rows, cs], lane_g)
                dos = _stack_heads(do_ref[rows, cs], lane_g)
                kb = k_ref[pl.ds(k0, 2 * BLOCK), cs]
                vb = v_ref[pl.ds(k0, 2 * BLOCK), cs]
                p, ps = _attn_probs(qs, kb, bias_first if first else bias_later, svs[hk])
                dp = _dot(dos, vb, NT)
                delta = jnp.sum(p * dp, axis=1, keepdims=True)
                dsb = (p * (dp - delta)).astype(MXU_DTYPE)
                dqb = _unstack_heads(_dot(dsb, kb, NN), lane_g) * SCALE
                dq_ref[rows, cs] = _rot(dqb, c, -sa, -sb).astype(MXU_DTYPE)
                dk_acc[pl.ds(k0, 2 * BLOCK), cs] += _dot(dsb, qs, TN)
                dv_acc[pl.ds(k0, 2 * BLOCK), cs] += _dot(p.astype(MXU_DTYPE), dos, TN)
                psd = ps * delta
                for g in range(GROUP):
                    val = -jnp.sum(psd[BLOCK * g:BLOCK * (g + 1)], axis=0, keepdims=True)
                    dsink = dsink + jnp.where(lane1 == GROUP * hk + g, val, 0.0)
            return dsink

        dsink = _attn_blocks(nb, blk, jnp.zeros((1, LANES), F32))
        dk_ref[...] = _rot(_fold_heads(dk_acc[...]), c_ref[...], -sa_ref[...], -sb_ref[...]).astype(MXU_DTYPE)
        dv_ref[...] = _fold_heads(dv_acc[...]).astype(MXU_DTYPE)
        ds_ref[...] = jnp.broadcast_to(dsink, ds_ref.shape)

    seq = pl.BlockSpec((S, ATTN_WIDTH), lambda b: (b, 0))
    kvs = pl.BlockSpec((S, KV_WIDTH), lambda b: (b, 0))
    tab = pl.BlockSpec((S, LANES), lambda b: (0, 0))
    nseq = T // S
    return _launch(
        body, [sinks, q, k4, v4, do, rc, rsa, rsb], name="attn_bwd", grid=(nseq,),
        in_specs=[pl.BlockSpec(memory_space=pltpu.SMEM), seq, seq, seq, seq, tab, tab, tab],
        out_specs=[seq, kvs, kvs, pl.BlockSpec((8, LANES), lambda b: (b, 0))],
        out_shape=[jax.ShapeDtypeStruct((T, ATTN_WIDTH), MXU_DTYPE), jax.ShapeDtypeStruct((T, KV_WIDTH), MXU_DTYPE),
                   jax.ShapeDtypeStruct((T, KV_WIDTH), MXU_DTYPE), jax.ShapeDtypeStruct((8 * nseq, LANES), F32)],
        scratch_shapes=[pltpu.VMEM((S, 512), F32), pltpu.VMEM((S, 512), F32)],
        sem=("arbitrary",), rider=rider)


def _branch_weights(wbp_ref, wba_ref, wbp_s, wba_s):
    @pl.when(pl.program_id(0) == 0)
    def _():
        for j in range(N_DEV):
            wbp_s[:, LANES * j:LANES * (j + 1)] = wbp_ref[j]
            wba_s[:, LANES * j:LANES * (j + 1)] = wba_ref[j]


def _mix_fwd_call(yp, ya, g, x, wbp, wba, wout, g2, g3, rider=None):
    T = x.shape[0]
    tm = _tile(T, 512)

    def body(yp_ref, ya_ref, g_ref, x_ref, wbp_ref, wba_ref, wout_ref, g2_ref, g3_ref,
             mix_ref, x1_ref, h2_ref, h2t_ref, wbp_s, wba_s):
        _branch_weights(wbp_ref, wba_ref, wbp_s, wba_s)
        bp = _dot(yp_ref[...], wbp_s[...], NN)
        ba = _dot(ya_ref[...], wba_s[...], NN)
        merged = g_ref[:, :D_MODEL].astype(F32) * bp + g_ref[:, D_MODEL:].astype(F32) * ba
        mix = _dot(merged.astype(MXU_DTYPE), wout_ref[...], NN)
        mix_ref[...] = mix
        x1 = x_ref[...] + (mix * _rms_r(mix)) * g2_ref[...]
        x1_ref[...] = x1
        h2 = (x1 * _rms_r(x1)) * g3_ref[...]
        h2_ref[...] = h2.astype(MXU_DTYPE)
        h2t_ref[...] = h2.T.astype(MXU_DTYPE)

    tok = lambda w: pl.BlockSpec((tm, w), lambda i: (i, 0))
    full = lambda a: pl.BlockSpec(a.shape, lambda i: (0,) * a.ndim)
    return _launch(
        body, [yp, ya, g, x, wbp, wba, wout, g2, g3], name="mix_fwd", grid=(T // tm,),
        in_specs=[tok(POOL_WIDTH), tok(ATTN_WIDTH), tok(GATE_WIDTH), tok(D_MODEL), full(wbp), full(wba), full(wout),
                  full(g2), full(g3)],
        out_specs=[tok(D_MODEL), tok(D_MODEL), tok(D_MODEL), pl.BlockSpec((D_MODEL, tm), lambda i: (0, i))],
        out_shape=[jax.ShapeDtypeStruct((T, D_MODEL), F32), jax.ShapeDtypeStruct((T, D_MODEL), F32),
                   jax.ShapeDtypeStruct((T, D_MODEL), MXU_DTYPE), jax.ShapeDtypeStruct((D_MODEL, T), MXU_DTYPE)],
        scratch_shapes=[pltpu.VMEM((POOL_WIDTH, D_MODEL), MXU_DTYPE), pltpu.VMEM((ATTN_WIDTH, D_MODEL), MXU_DTYPE)],
        sem=("arbitrary",), rider=rider)


def _mix_bwd_call(dx1, mix, yp, ya, g, wbp, wba, wout, g2, rider=None):
    T = dx1.shape[0]
    tm = _tile(T, 512)

    def body(dx1_ref, mix_ref, yp_ref, ya_ref, g_ref, wbp_ref, wba_ref, wout_ref, g2_ref,
             dyp_ref, do_ref, dgates_ref, dg2_ref, dbg_ref, gout_ref, gbp_ref, gba_ref,
             wbp_s, wba_s, acc_out, acc_bp, acc_ba, sem):
        _branch_weights(wbp_ref, wba_ref, wbp_s, wba_s)
        step = pl.program_id(0)

        @pl.when(step == 0)
        def _():
            dg2_ref[...] = jnp.zeros_like(dg2_ref)
            dbg_ref[...] = jnp.zeros_like(dbg_ref)
            acc_out[...] = jnp.zeros_like(acc_out)
            acc_bp[...] = jnp.zeros_like(acc_bp)
            acc_ba[...] = jnp.zeros_like(acc_ba)

        mix = mix_ref[...]
        dmix, dg2 = _rms_bwd(dx1_ref[...], mix, _rms_r(mix), g2_ref[...])
        dg2_ref[...] += jnp.sum(dg2, axis=0, keepdims=True)
        dmixb = dmix.astype(MXU_DTYPE)
        dmerged = _dot(dmixb, wout_ref[...], NT)
        yp, ya = yp_ref[...], ya_ref[...]
        bp = _dot(yp, wbp_s[...], NN)
        ba = _dot(ya, wba_s[...], NN)
        gp, ga = g_ref[:, :D_MODEL].astype(F32), g_ref[:, D_MODEL:].astype(F32)
        acc_out[...] += _dot((gp * bp + ga * ba).astype(MXU_DTYPE), dmixb, TN)
        dgp = dmerged * bp * (gp * (1.0 - gp))
        dga = dmerged * ba * (ga * (1.0 - ga))
        dbg_ref[:, :D_MODEL] += jnp.sum(dgp, axis=0, keepdims=True)
        dbg_ref[:, D_MODEL:] += jnp.sum(dga, axis=0, keepdims=True)
        dgates_ref[:, :D_MODEL] = dgp.astype(MXU_DTYPE)
        dgates_ref[:, D_MODEL:] = dga.astype(MXU_DTYPE)
        dbp = (dmerged * gp).astype(MXU_DTYPE)
        dba = (dmerged * ga).astype(MXU_DTYPE)
        acc_bp[...] += _dot(yp, dbp, TN)
        acc_ba[...] += _dot(ya, dba, TN)
        dyp_ref[...] = _dot(dbp, wbp_s[...], NT)
        do_ref[...] = _dot(dba, wba_s[...], NT).astype(MXU_DTYPE)

        @pl.when(step == pl.num_programs(0) - 1)
        def _():
            copies = [pltpu.make_async_copy(acc_out, gout_ref, sem.at[0])]
            for j in range(N_DEV):
                cols = slice(LANES * j, LANES * (j + 1))
                copies.append(pltpu.make_async_copy(acc_bp.at[:, cols], gbp_ref.at[j], sem.at[1 + j]))
                copies.append(pltpu.make_async_copy(acc_ba.at[:, cols], gba_ref.at[j], sem.at[1 + N_DEV + j]))
            for cp in copies:
                cp.start()
            for cp in copies:
                cp.wait()

    tok = lambda w: pl.BlockSpec((tm, w), lambda i: (i, 0))
    full = lambda a: pl.BlockSpec(a.shape, lambda i: (0,) * a.ndim)
    acc = lambda w: pl.BlockSpec((1, w), lambda i: (0, 0))
    hbm = pl.BlockSpec(memory_space=pl.ANY)
    sd = jax.ShapeDtypeStruct
    return _launch(
        body, [dx1, mix, yp, ya, g, wbp, wba, wout, g2], name="mix_bwd", grid=(T // tm,),
        in_specs=[tok(D_MODEL), tok(D_MODEL), tok(POOL_WIDTH), tok(ATTN_WIDTH), tok(GATE_WIDTH), full(wbp), full(wba),
                  full(wout), full(g2)],
        out_specs=[tok(POOL_WIDTH), tok(ATTN_WIDTH), tok(GATE_WIDTH), acc(D_MODEL), acc(GATE_WIDTH), hbm, hbm, hbm],
        out_shape=[sd((T, POOL_WIDTH), F32), sd((T, ATTN_WIDTH), MXU_DTYPE), sd((T, GATE_WIDTH), MXU_DTYPE),
                   sd((1, D_MODEL), F32), sd((1, GATE_WIDTH), F32), sd((D_MODEL, D_MODEL), F32),
                   sd((N_DEV, POOL_WIDTH, LANES), F32), sd((N_DEV, ATTN_WIDTH, LANES), F32)],
        scratch_shapes=[pltpu.VMEM((POOL_WIDTH, D_MODEL), MXU_DTYPE), pltpu.VMEM((ATTN_WIDTH, D_MODEL), MXU_DTYPE),
                        pltpu.VMEM((D_MODEL, D_MODEL), F32), pltpu.VMEM((POOL_WIDTH, D_MODEL), F32),
                        pltpu.VMEM((ATTN_WIDTH, D_MODEL), F32), pltpu.SemaphoreType.DMA((1 + 2 * N_DEV,))],
        sem=("arbitrary",), rider=rider)


def _mlp_up_call(h2, wup):
    T = h2.shape[0]
    tm = _tile(T, 512)
    fc = D_FF // N_DEV

    def body(h2_ref, wup_ref, act_ref, actt_ref):
        h2 = h2_ref[...]
        for j in range(N_DEV):
            sl = slice(fc * j, fc * (j + 1))
            rl = jnp.maximum(_dot(h2, wup_ref[j], NN), 0.0)
            act = rl * rl
            act_ref[:, sl] = act.astype(MXU_DTYPE)
            actt_ref[sl, :] = act.T.astype(MXU_DTYPE)

    sd = jax.ShapeDtypeStruct
    return pl.pallas_call(
        body, name="mlp_up", grid=(T // tm,),
        in_specs=[pl.BlockSpec((tm, D_MODEL), lambda i: (i, 0)),
                  pl.BlockSpec(wup.shape, lambda i: (0, 0, 0), pipeline_mode=pl.Buffered(1))],
        out_specs=[pl.BlockSpec((tm, D_FF), lambda i: (i, 0)), pl.BlockSpec((D_FF, tm), lambda i: (0, i))],
        out_shape=[sd((T, D_FF), MXU_DTYPE), sd((D_FF, T), MXU_DTYPE)],
        compiler_params=_params(("arbitrary",)),
    )(h2, wup)


def _mlp_call(x1, act, target, wup, wdown, g3, g4):
    T = x1.shape[0]
    tm = _tile(T, 256)
    fc = D_FF // N_DEV

    def body(x1_ref, act_ref, t_ref, wup_ref, wdown_ref, g3_ref, g4_ref,
             da_ref, dff_ref, dx1_ref, dg3_ref, dg4_ref, loss_ref):
        @pl.when(pl.program_id(0) == 0)
        def _():
            dg3_ref[...] = jnp.zeros_like(dg3_ref)
            dg4_ref[...] = jnp.zeros_like(dg4_ref)
            loss_ref[...] = jnp.zeros_like(loss_ref)

        ff = jnp.zeros((tm, D_MODEL), F32)
        for j in range(N_DEV):
            ff = ff + _dot(act_ref[:, fc * j:fc * (j + 1)], wdown_ref[j], NN)
        x1 = x1_ref[...]
        r4 = _rms_r(ff)
        err = x1 + (ff * r4) * g4_ref[...] - t_ref[...]
        loss_ref[...] += jnp.sum(err * err, axis=0, keepdims=True)
        dy = err * (1.0 / D_MODEL)
        dff, dg4 = _rms_bwd(dy, ff, r4, g4_ref[...])
        dg4_ref[...] += jnp.sum(dg4, axis=0, keepdims=True)
        dffb = dff.astype(MXU_DTYPE)
        dff_ref[...] = dffb
        dh2 = jnp.zeros((tm, D_MODEL), F32)
        for j in range(N_DEV):
            sl = slice(fc * j, fc * (j + 1))
            rl = jnp.sqrt(act_ref[:, sl].astype(F32))
            dab = (_dot(dffb, wdown_ref[j], NT) * (2.0 * rl)).astype(MXU_DTYPE)
            da_ref[:, sl] = dab
            dh2 = dh2 + _dot(dab, wup_ref[j], NT)
        dx1, dg3 = _rms_bwd(dh2, x1, _rms_r(x1), g3_ref[...])
        dg3_ref[...] += jnp.sum(dg3, axis=0, keepdims=True)
        dx1_ref[...] = dy + dx1

    tok = lambda w: pl.BlockSpec((tm, w), lambda i: (i, 0))
    full = lambda a: pl.BlockSpec(a.shape, lambda i: (0,) * a.ndim, pipeline_mode=pl.Buffered(1))
    vec = pl.BlockSpec((1, D_MODEL), lambda i: (0, 0))
    sd = jax.ShapeDtypeStruct
    return pl.pallas_call(
        body, name="mlp_down_bwd", grid=(T // tm,),
        in_specs=[tok(D_MODEL), tok(D_FF), tok(D_MODEL), full(wup), full(wdown), vec, vec],
        out_specs=[tok(D_FF), tok(D_MODEL), tok(D_MODEL), vec, vec, vec],
        out_shape=[sd((T, D_FF), MXU_DTYPE), sd((T, D_MODEL), MXU_DTYPE),
                   sd((T, D_MODEL), F32), sd((1, D_MODEL), F32), sd((1, D_MODEL), F32), sd((1, D_MODEL), F32)],
        compiler_params=_params(("arbitrary",)),
    )(x1, act, target, wup, wdown, g3, g4)


def _inproj_bwd_call(du, dq, dk, dv, dgates, dx1, x, win_t, g1, rider=None):
    T = x.shape[0]
    tm = _tile(T, 512)

    def body(du_ref, dq_ref, dk_ref, dv_ref, dgt_ref, dx1_ref, x_ref, w_ref, g1_ref, gx_ref, dg1_ref, db_ref):
        @pl.when(pl.program_id(0) == 0)
        def _():
            dg1_ref[...] = jnp.zeros_like(dg1_ref)
            db_ref[...] = jnp.zeros_like(db_ref)

        dh = jnp.zeros((tm, D_MODEL), F32)
        for ref, lo, hi in ((du_ref, 0, C_Q), (dq_ref, C_Q, C_K), (dk_ref, C_K, C_V), (dv_ref, C_V, C_G),
                            (dgt_ref, C_G, IN_WIDTH)):
            piece = ref[...]
            dh = dh + _dot(piece, w_ref[lo:hi, :], NN)
            if hi <= C_G:
                db_ref[:, lo:hi] += jnp.sum(piece.astype(F32), axis=0, keepdims=True)
        xv = x_ref[...]
        dx, dg1 = _rms_bwd(dh, xv, _rms_r(xv), g1_ref[...])
        dg1_ref[...] += jnp.sum(dg1, axis=0, keepdims=True)
        gx_ref[...] = dx1_ref[...] + dx

    tok = lambda w: pl.BlockSpec((tm, w), lambda i: (i, 0))
    full = lambda a: pl.BlockSpec(a.shape, lambda i: (0,) * a.ndim)
    sd = jax.ShapeDtypeStruct
    return _launch(
        body, [du, dq, dk, dv, dgates, dx1, x, win_t, g1], name="inproj_bwd", grid=(T // tm,),
        in_specs=[tok(POOL_WIDTH), tok(ATTN_WIDTH), tok(KV_WIDTH), tok(KV_WIDTH), tok(GATE_WIDTH), tok(D_MODEL),
                  tok(D_MODEL), full(win_t), full(g1)],
        out_specs=[tok(D_MODEL), pl.BlockSpec((1, D_MODEL), lambda i: (0, 0)), pl.BlockSpec((1, C_G), lambda i: (0, 0))],
        out_shape=[sd((T, D_MODEL), F32), sd((1, D_MODEL), F32), sd((1, C_G), F32)],
        sem=("arbitrary",), rider=rider)


WGRAD_TOKENS = 1024


def _wgrad_rows_call(at, b, name, rider=None):
    K, T = at.shape
    N = b.shape[1]
    tm = _tile(T, WGRAD_TOKENS)
    kb = min(K, 1024)
    per = kb // (K // N_DEV)

    def body(a_ref, b_ref, o_ref):
        @pl.when(pl.program_id(1) == 0)
        def _():
            o_ref[...] = jnp.zeros_like(o_ref)

        d = _dot(a_ref[...], b_ref[...], NN)
        rs = kb // per
        for j in range(per):
            o_ref[j] += d[rs * j:rs * (j + 1)]

    return _launch(
        body, [at, b], name=name, grid=(K // kb, T // tm),
        in_specs=[pl.BlockSpec((kb, tm), lambda i, t: (i, t)), pl.BlockSpec((tm, N), lambda i, t: (t, 0))],
        out_specs=[pl.BlockSpec((per, K // N_DEV, N), lambda i, t: (i, 0, 0))],
        out_shape=[jax.ShapeDtypeStruct((N_DEV, K // N_DEV, N), F32)],
        sem=("arbitrary", "arbitrary"), rider=rider)


def _wgrad_cols_call(at, b, name, rider=None):
    K, T = at.shape
    N = b.shape[1]
    tm = _tile(T, WGRAD_TOKENS)
    nb = min(N, 1024)
    per = nb // (N // N_DEV)

    def body(a_ref, b_ref, o_ref):
        @pl.when(pl.program_id(1) == 0)
        def _():
            o_ref[...] = jnp.zeros_like(o_ref)

        d = _dot(a_ref[...], b_ref[...], NN)
        cs = nb // per
        for j in range(per):
            o_ref[j] += d[:, cs * j:cs * (j + 1)]

    return _launch(
        body, [at, b], name=name, grid=(N // nb, T // tm),
        in_specs=[pl.BlockSpec((K, tm), lambda i, t: (0, t)), pl.BlockSpec((tm, nb), lambda i, t: (t, i))],
        out_specs=[pl.BlockSpec((per, K, N // N_DEV), lambda i, t: (i, 0, 0))],
        out_shape=[jax.ShapeDtypeStruct((N_DEV, K, N // N_DEV), F32)],
        sem=("arbitrary", "arbitrary"), rider=rider)


def _wgrad_in_call(du, dq, dk, dv, dgates, h, rider=None):
    T = h.shape[0]
    tm = _tile(T, WGRAD_TOKENS)
    rows = IN_WIDTH // N_DEV

    def body(du_ref, dq_ref, dk_ref, dv_ref, dgt_ref, h_ref, o_ref, acc, sem):
        t = pl.program_id(0)

        @pl.when(t == 0)
        def _():
            acc[...] = jnp.zeros_like(acc)

        hv = h_ref[...]
        for ref, lo, hi in ((du_ref, 0, C_Q), (dq_ref, C_Q, C_K), (dk_ref, C_K, C_V), (dv_ref, C_V, C_G),
                            (dgt_ref, C_G, IN_WIDTH)):
            acc[lo:hi, :] += _dot(ref[...], hv, TN)

        @pl.when(t == pl.num_programs(0) - 1)
        def _():
            copies = [pltpu.make_async_copy(acc.at[pl.ds(rows * j, rows), :], o_ref.at[j], sem.at[j])
                      for j in range(N_DEV)]
            for cp in copies:
                cp.start()
            for cp in copies:
                cp.wait()

    tok = lambda w: pl.BlockSpec((tm, w), lambda t: (t, 0))
    return _launch(
        body, [du, dq, dk, dv, dgates, h], name="wgrad_in", grid=(T // tm,),
        in_specs=[tok(POOL_WIDTH), tok(ATTN_WIDTH), tok(KV_WIDTH), tok(KV_WIDTH), tok(GATE_WIDTH), tok(D_MODEL)],
        out_specs=[pl.BlockSpec(memory_space=pl.ANY)],
        out_shape=[jax.ShapeDtypeStruct((N_DEV, rows, D_MODEL), F32)],
        scratch_shapes=[pltpu.VMEM((IN_WIDTH, D_MODEL), F32), pltpu.SemaphoreType.DMA((N_DEV,))],
        sem=("arbitrary",), rider=rider)


def _coords():
    return lax.axis_index("x"), lax.axis_index("y"), lax.axis_index("c")


def _allgather_call(shards):
    n = len(shards)

    def body(*refs):
        ins, outs = refs[:n], refs[n:2 * n]
        send_sems, recv_sems, local_sems = refs[2 * n:]
        x, y, c = _coords()
        me, sibling = (x, y, c), (x, y, 1 - c)
        chips = [(1 - x, y), (x, 1 - y), (1 - x, 1 - y)]

        def slot(p):
            return 4 * p[0] + 2 * p[1] + p[2]

        def copy(t, k, block, to, src=None):
            dst = outs[t].at[slot(block)]
            return pltpu.make_async_remote_copy(
                src_ref=dst if src is None else src, dst_ref=dst, send_sem=send_sems.at[t, k],
                recv_sem=recv_sems.at[t, k], device_id=to, device_id_type=MESH)

        mine = [pltpu.make_async_copy(ins[t], outs[t].at[slot(me)], local_sems.at[t]) for t in range(n)]
        for cp in mine:
            cp.start()
        first = []
        for t in range(n):
            first.append(copy(t, 0, me, sibling, src=ins[t]))
            first += [copy(t, 1 + j, me, (*chip, c), src=ins[t]) for j, chip in enumerate(chips)]
        for cp in first:
            cp.start()
        passed = []
        for t in range(n):
            for j, chip in enumerate(chips):
                copy(t, 1 + j, (*chip, c), me).wait_recv()
                fwd = copy(t, 4 + j, (*chip, c), sibling)
                fwd.start()
                passed.append(fwd)
        for t in range(n):
            copy(t, 0, sibling, me).wait_recv()
            for j, chip in enumerate(chips):
                copy(t, 4 + j, (*chip, 1 - c), me).wait_recv()
        for cp in first + passed:
            cp.wait_send()
        for cp in mine:
            cp.wait()

    hbm = pl.BlockSpec(memory_space=pl.ANY)
    return pl.pallas_call(
        body, name="allgather_weights",
        in_specs=[hbm] * n, out_specs=[hbm] * n,
        out_shape=[jax.ShapeDtypeStruct((N_DEV,) + s.shape, s.dtype) for s in shards],
        scratch_shapes=[pltpu.SemaphoreType.DMA((n, 7)), pltpu.SemaphoreType.DMA((n, 7)), pltpu.SemaphoreType.DMA((n,))],
    )(*shards)


def _slot(p):
    return 4 * p[0] + 2 * p[1] + p[2]


def _rows(ref, span):
    return ref if span is None else ref.at[pl.ds(span[0], span[1])]


ALL = "all"
LOCAL = "local"


def _rows(ref, span):
    return ref if span == ALL else ref.at[pl.ds(span[0], span[1])]


def _rider_ag(items):
    ins, out_shape, aliases, where = [], [], {}, []
    n_remote = n_local = 0
    for t, (shard, buf, snd, fwd) in enumerate(items):
        i_shard = i_buf = None
        if snd is not None:
            i_shard = len(ins)
            ins.append(shard)
        if buf is not None:
            i_buf = len(ins)
            ins.append(buf)
            aliases[i_buf] = t
            out_shape.append(jax.ShapeDtypeStruct(buf.shape, buf.dtype))
        else:
            assert fwd is None and snd is not None
            out_shape.append(jax.ShapeDtypeStruct((N_DEV,) + shard.shape, shard.dtype))
        where.append((i_shard, i_buf, n_remote, n_local))
        n_remote += (4 if snd not in (None, LOCAL) else 0) + (3 if fwd is not None else 0)
        n_local += 1 if snd is not None else 0

    def plan(rins, routs, send, recv, loc, r0, l0):
        x, y, c = _coords()
        peers = [(x, y, 1 - c), (1 - x, y, c), (x, 1 - y, c), (1 - x, 1 - y, c)]
        remote, local = [], []
        for t, (shard, buf, snd, fwd) in enumerate(items):
            i_shard, i_buf, k, l = where[t]
            k, l = r0 + k, l0 + l
            if snd is not None:
                span = ALL if snd == LOCAL else snd
                src, dst = _rows(rins[i_shard], span), _rows(routs[t].at[_slot((x, y, c))], span)
                local.append(pltpu.make_async_copy(src, dst, loc.at[l]))
                for peer in (peers if snd != LOCAL else []):
                    remote.append(pltpu.make_async_remote_copy(
                        src_ref=src, dst_ref=dst, send_sem=send.at[k], recv_sem=recv.at[k],
                        device_id=peer, device_id_type=MESH))
                    k += 1
            if fwd is not None:
                for px, py, pc in peers[1:]:
                    s = _slot((px, py, pc))
                    remote.append(pltpu.make_async_remote_copy(
                        src_ref=_rows(rins[i_buf].at[s], fwd), dst_ref=_rows(routs[t].at[s], fwd),
                        send_sem=send.at[k], recv_sem=recv.at[k], device_id=peers[0], device_id_type=MESH))
                    k += 1
        return remote, local

    return _Rider(ins, out_shape, n_remote, n_local, plan, aliases)


def _rider_ag_remote(shards):
    n = len(shards)

    def plan(ins, outs, send, recv, loc, r0, l0):
        x, y, c = _coords()
        remote = []
        for t in range(n):
            dst = outs[t].at[_slot((x, y, c))]
            for k, peer in enumerate([(x, y, 1 - c), (1 - x, y, c), (x, 1 - y, c), (1 - x, 1 - y, c)]):
                remote.append(pltpu.make_async_remote_copy(
                    src_ref=ins[t], dst_ref=dst, send_sem=send.at[r0 + 4 * t + k], recv_sem=recv.at[r0 + 4 * t + k],
                    device_id=peer, device_id_type=MESH))
        return remote, []

    return _Rider(shards, [jax.ShapeDtypeStruct((N_DEV,) + s.shape, s.dtype) for s in shards], 4 * n, 0, plan)


def _rider_rs_sibling(grads):
    n = len(grads)

    def plan(ins, outs, send, recv, loc, r0, l0):
        x, y, c = _coords()
        remote = []
        for t in range(n):
            for q in range(4):
                remote.append(pltpu.make_async_remote_copy(
                    src_ref=ins[t].at[q, 1 - c], dst_ref=outs[t].at[q], send_sem=send.at[r0 + 4 * t + q],
                    recv_sem=recv.at[r0 + 4 * t + q], device_id=(x, y, 1 - c), device_id_type=MESH))
        return remote, []

    return _Rider(grads, [jax.ShapeDtypeStruct((4,) + g.shape[2:], g.dtype) for g in grads], 4 * n, 0, plan)


def _rider_rs_chips(sums, rows=None, into=None):
    n = len(sums)
    rows = rows or [ALL] * n

    def plan(ins, outs, send, recv, loc, r0, l0):
        x, y, c = _coords()
        remote = []
        for t in range(n):
            for r, (px, py) in enumerate([(1 - x, y), (x, 1 - y), (1 - x, 1 - y)]):
                remote.append(pltpu.make_async_remote_copy(
                    src_ref=_rows(ins[t].at[2 * px + py], rows[t]), dst_ref=_rows(outs[t].at[r], rows[t]),
                    send_sem=send.at[r0 + 3 * t + r], recv_sem=recv.at[r0 + 3 * t + r],
                    device_id=(px, py, c), device_id_type=MESH))
        return remote, []

    out_shape = [jax.ShapeDtypeStruct((3,) + s.shape[1:], s.dtype) for s in sums]
    if into is None:
        return _Rider(sums, out_shape, 3 * n, 0, plan)
    return _Rider(list(sums) + list(into), out_shape, 3 * n, 0, plan, aliases={n + t: t for t in range(n)})


def _rider_gather_remote(parts):
    n = len(parts)

    def plan(ins, outs, send, recv, loc, r0, l0):
        x, y, c = _coords()
        me = _slot((x, y, c))
        remote = []
        for t in range(n):
            for k in range(1, N_DEV):
                peer = (x ^ ((k >> 2) & 1), y ^ ((k >> 1) & 1), c ^ (k & 1))
                remote.append(pltpu.make_async_remote_copy(
                    src_ref=ins[t], dst_ref=outs[t].at[me], send_sem=send.at[r0 + 7 * t + k - 1],
                    recv_sem=recv.at[r0 + 7 * t + k - 1], device_id=peer, device_id_type=MESH))
        return remote, []

    return _Rider(parts, [jax.ShapeDtypeStruct((N_DEV,) + p.shape, p.dtype) for p in parts], 7 * n, 0, plan)


def _chip_sum_call(idx, grads, recvd, out_dtypes, name):
    n = len(grads)

    def body(i_ref, *refs):
        for t in range(n):
            refs[2 * n + t][0] = (refs[t][0, 0] + refs[n + t][0]).astype(out_dtypes[t])

    def chip(k, s):
        return jnp.where(k >= s[0], k + 1, k)

    in_specs = [pl.BlockSpec((1, 1) + g.shape[2:], lambda k, s: (chip(k, s), s[1], 0, 0)) for g in grads]
    in_specs += [pl.BlockSpec((1,) + r.shape[1:], lambda k, s: (chip(k, s), 0, 0)) for r in recvd]
    return pl.pallas_call(
        body, name=name,
        grid_spec=pltpu.PrefetchScalarGridSpec(
            num_scalar_prefetch=1, grid=(3,), in_specs=in_specs,
            out_specs=[pl.BlockSpec((1,) + r.shape[1:], lambda k, s: (chip(k, s), 0, 0)) for r in recvd]),
        out_shape=[jax.ShapeDtypeStruct(r.shape, dt) for r, dt in zip(recvd, out_dtypes)],
        compiler_params=_params(("arbitrary",)),
    )(idx, *grads, *recvd)


def _final_sum_call(idx, grads, recvd1, recvd2):
    n = len(grads)
    nsteps = 2

    def body(i_ref, *refs):
        for t in range(n):
            g, r1, r2, o = refs[t], refs[n + t], refs[2 * n + t], refs[3 * n + t]
            s = g[0, 0] + r1[0]
            for r in range(3):
                s = s + r2[r].astype(F32)
            o[...] = s

    def rows(a):
        r = a.shape[-2]
        return r // nsteps if (r // nsteps) % 16 == 0 else r

    def step(a):
        return (lambda i: i) if rows(a) != a.shape[-2] else (lambda i: 0)

    in_specs = [pl.BlockSpec((1, 1, rows(g), g.shape[3]), lambda i, s, st=step(g): (s[0], s[1], st(i), 0)) for g in grads]
    in_specs += [pl.BlockSpec((1, rows(r), r.shape[2]), lambda i, s, st=step(r): (s[0], st(i), 0)) for r in recvd1]
    in_specs += [pl.BlockSpec((3, rows(r), r.shape[2]), lambda i, s, st=step(r): (0, st(i), 0)) for r in recvd2]
    return pl.pallas_call(
        body, name="rs_final_sum",
        grid_spec=pltpu.PrefetchScalarGridSpec(
            num_scalar_prefetch=1, grid=(nsteps,), in_specs=in_specs,
            out_specs=[pl.BlockSpec((rows(r), r.shape[2]), lambda i, s, st=step(r): (st(i), 0)) for r in recvd2]),
        out_shape=[jax.ShapeDtypeStruct(r.shape[1:], F32) for r in recvd2],
        compiler_params=_params(("arbitrary",)),
    )(idx, *grads, *recvd1, *recvd2)


def _sum8_call(parts):
    def body(p_ref, o_ref):
        s = p_ref[0]
        for j in range(1, N_DEV):
            s = s + p_ref[j]
        o_ref[...] = s

    return pl.pallas_call(body, name="sum_small_partials",
                          out_shape=jax.ShapeDtypeStruct(parts.shape[1:], parts.dtype))(parts)


def _adamw(w, g, m, v):
    m = ADAM_B1 * m + (1.0 - ADAM_B1) * g
    v = ADAM_B2 * v + (1.0 - ADAM_B2) * (g * g)
    m_hat = m / (1.0 - ADAM_B1 ** ADAM_STEP)
    v_hat = v / (1.0 - ADAM_B2 ** ADAM_STEP)
    delta = -ADAM_LR * (m_hat / (jnp.sqrt(v_hat) + ADAM_EPS) + ADAM_WD * w)
    return delta, m, v


def _adamw_call(ws, gs, ms, vs, nsteps, name):
    n = len(ws)

    def body(*refs):
        for t in range(n):
            w, g, m, v = (refs[k * n + t][...] for k in range(4))
            d, m2, v2 = _adamw(w, g, m, v)
            refs[4 * n + t][...] = d
            refs[5 * n + t][...] = m2
            refs[6 * n + t][...] = v2

    def spec(a):
        assert a.shape[0] % nsteps == 0 and (nsteps == 1 or (a.shape[0] // nsteps) % 8 == 0), a.shape
        return pl.BlockSpec((a.shape[0] // nsteps, a.shape[1]), lambda i: (i, 0))

    specs = [spec(a) for a in ws]
    outs = pl.pallas_call(
        body, name=name, grid=(nsteps,),
        in_specs=specs * 4, out_specs=specs * 3,
        out_shape=[jax.ShapeDtypeStruct(a.shape, F32) for a in ws] * 3,
        compiler_params=_params(("arbitrary",)),
    )(*ws, *gs, *ms, *vs)
    return outs[:n], outs[n:2 * n], outs[2 * n:]


def _adamw_rs_call(idx, after, gws, r1s, r2s, ws, ms, vs, nsteps, name):
    n = len(ws)

    def body(i_ref, after_ref, *refs):
        for t in range(n):
            gw, r1, r2, w, m, v = (refs[k * n + t] for k in range(6))
            g = gw[0, 0] + r1[0]
            for r in range(3):
                g = g + r2[r].astype(F32)
            d, m2, v2 = _adamw(w[...], g, m[...], v[...])
            refs[6 * n + t][...] = g
            refs[7 * n + t][...] = d
            refs[8 * n + t][...] = m2
            refs[9 * n + t][...] = v2

    def rb(a):
        r = a.shape[0] // nsteps
        assert a.shape[0] % nsteps == 0 and r % 16 == 0, a.shape
        return r

    in_specs = [pl.BlockSpec((1, 1, rb(w), w.shape[1]), lambda i, s: (s[0], s[1], i, 0)) for w in ws]
    in_specs += [pl.BlockSpec((1, rb(w), w.shape[1]), lambda i, s: (s[0], i, 0)) for w in ws]
    in_specs += [pl.BlockSpec((3, rb(w), w.shape[1]), lambda i, s: (0, i, 0)) for w in ws]
    plain = [pl.BlockSpec((rb(w), w.shape[1]), lambda i, s: (i, 0)) for w in ws]
    outs = pl.pallas_call(
        body, name=name,
        grid_spec=pltpu.PrefetchScalarGridSpec(
            num_scalar_prefetch=1, grid=(nsteps,),
            in_specs=[pl.BlockSpec(memory_space=pl.ANY)] + in_specs + plain * 3, out_specs=plain * 4),
        out_shape=[jax.ShapeDtypeStruct(w.shape, F32) for w in ws] * 4,
        compiler_params=_params(("arbitrary",)),
    )(idx, after, *gws, *r1s, *r2s, *ws, *ms, *vs)
    return outs[:n], outs[n:2 * n], outs[2 * n:3 * n], outs[3 * n:]


def _rows128(a, pad_rows):
    flat = a.reshape(-1).astype(F32)
    flat = jnp.pad(flat, (0, pad_rows * LANES - flat.shape[0]))
    return flat.reshape(pad_rows, LANES)


_SMALL_A = (("w_pool", 512), ("pool_scale", 8), ("attn_sinks", 8), ("g_mix_post", 8), ("g_mlp_pre", 8),
            ("g_mlp_post", 8), ("loss", 8), ("b_in_gates", 16))
_SMALL_A_ROWS = 640
_SMALL_B = (("g_mix_pre", 8), ("b_in_head", 16))


def _pack(parts, layout, total_rows):
    rows = [_rows128(parts[k], r) for k, r in layout]
    pad = total_rows - sum(r for _, r in layout)
    if pad:
        rows.append(jnp.zeros((pad, LANES), F32))
    return jnp.concatenate(rows, axis=0)


def _unpack(buf, layout, sizes):
    out, off = {}, 0
    for k, r in layout:
        out[k] = buf[off:off + r].reshape(-1)[:sizes[k]]
        off += r
    return out


def kernel(x, g_mix_pre, w_in, b_in, w_pool, pool_scale, attn_sinks, w_branch_pool, w_branch_attn, w_out, g_mix_post, g_mlp_pre, w_up, w_down, g_mlp_post, loss_target, m_g_mix_pre, m_w_in, m_b_in, m_w_pool, m_pool_scale, m_attn_sinks, m_w_branch_pool, m_w_branch_attn, m_w_out, m_g_mix_post, m_g_mlp_pre, m_w_up, m_w_down, m_g_mlp_post, v_g_mix_pre, v_w_in, v_b_in, v_w_pool, v_pool_scale, v_attn_sinks, v_w_branch_pool, v_w_branch_attn, v_w_out, v_g_mix_post, v_g_mlp_pre, v_w_up, v_w_down, v_g_mlp_post):
    B, S, _ = x.shape
    T = B * S
    xt = x.reshape(T, D_MODEL)
    tgt = loss_target.reshape(T, D_MODEL)
    cx, cy, cc = _coords()

    cidx = jnp.stack([2 * cx + cy, cc]).astype(jnp.int32)
    by_chip = lambda gr: gr.reshape((4, 2) + gr.shape[1:])
    bf = lambda w: w[0].astype(MXU_DTYPE)

    (win_s,) = _allgather_call([w_in[0].T.astype(MXU_DTYPE)])
    win_t = win_s.reshape(IN_WIDTH, D_MODEL)
    wpool_b = bf(w_pool)
    rc, rsa, rsb = _rot_tables(S)

    wbp_l, wba_l, wout_l, wup_l, wdown_l = bf(w_branch_pool), bf(w_branch_attn), bf(w_out), bf(w_up), bf(w_down)
    (c_br, c_up, c_dn), tok = _copies_start(
        [_rider_ag_remote([wbp_l, wba_l, wout_l]), _rider_ag_remote([wup_l]), _rider_ag_remote([wdown_l])],
        "allgather_start", after=win_s)
    (h, u, q, k4, v4, g), _ = _inproj_call(xt, g_mix_pre, win_t, b_in, rc, rsa, rsb, S, rider=_after(tok))
    yp = _pool_call(u, wpool_b, pool_scale, S)
    wbp_1, wba_1, wout_1 = _copies_wait([c_br], yp, "allgather_wait_branch")
    (ya,), (wbp_s, wba_s, wout_s) = _attn_call(
        attn_sinks, q, k4, v4, S,
        rider=_rider_ag([(wbp_l, wbp_1, LOCAL, ALL), (wba_l, wba_1, LOCAL, ALL), (wout_l, wout_1, LOCAL, ALL)]))
    wout_f = wout_s.reshape(D_MODEL, D_MODEL)
    (wup_1,) = _copies_wait([c_up], ya, "allgather_wait_up")
    (mix, x1, h2, h2_t), (wup_s,) = _mix_fwd_call(
        yp, ya, g, xt, wbp_s, wba_s, wout_f, g_mix_post, g_mlp_pre, rider=_rider_ag([(wup_l, wup_1, LOCAL, ALL)]))
    act, act_t = _mlp_up_call(h2, wup_s)
    (wdown_1,) = _copies_wait([c_dn], act, "allgather_wait_down")
    (wdown_s,) = _comm_call(_rider_ag([(wdown_l, wdown_1, LOCAL, ALL)]), "allgather_pass_down")

    da, dff, dx1, dg3, dg4, lossvec = _mlp_call(x1, act, tgt, wup_s, wdown_s, g_mlp_pre, g_mlp_post)
    gw_down = by_chip(_wgrad_rows_call(act_t, dff, "wgrad_down")[0])
    (gw_up,), (r1_down,) = _wgrad_cols_call(h2_t, da, "wgrad_up", rider=_rider_rs_sibling([gw_down]))
    gw_up = by_chip(gw_up)
    (s_down,) = _chip_sum_call(cidx, [gw_down], [r1_down], [MXU_DTYPE], "rs_chip_sum_down")
    (c_down,), tok = _copies_start([_rider_rs_chips([s_down])], "rs_chips_start_down")
    (dyp, do, dgates, dg2, dbg, gw_out, gw_bp, gw_ba), (r1_up,) = _mix_bwd_call(
        dx1, mix, yp, ya, g, wbp_s, wba_s, wout_f, g_mix_post, rider=_after(tok, _rider_rs_sibling([gw_up])))
    gw_out = by_chip(gw_out.reshape(N_DEV, D_MODEL // N_DEV, D_MODEL))
    gw_bp, gw_ba = by_chip(gw_bp), by_chip(gw_ba)
    (s_up,) = _chip_sum_call(cidx, [gw_up], [r1_up], [MXU_DTYPE], "rs_chip_sum_up")
    (c_up,), tok = _copies_start([_rider_rs_chips([s_up])], "rs_chips_start_up")
    (dq, dk, dv, dsink), (r1_out, r1_bp, r1_ba) = _attn_bwd_call(
        attn_sinks, q, k4, v4, do, rc, rsa, rsb, S, rider=_after(tok, _rider_rs_sibling([gw_out, gw_bp, gw_ba])))
    s_obb = _chip_sum_call(cidx, [gw_out, gw_bp, gw_ba], [r1_out, r1_bp, r1_ba], [MXU_DTYPE] * 3, "rs_chip_sum_branch")
    (c_obb,), tok = _copies_start([_rider_rs_chips(s_obb)], "rs_chips_start_branch")
    (du, dwp, dps), _ = _pool_bwd_call(u, dyp, wpool_b, pool_scale, S, rider=_after(tok))
    (gw_in,) = _wgrad_in_call(du, dq, dk, dv, dgates, h)
    gw_in = by_chip(gw_in)

    small_a = {"w_pool": dwp, "pool_scale": dps,
               "attn_sinks": jnp.sum(dsink.reshape(B, 8, LANES)[:, 0, :N_Q_HEADS], axis=0), "g_mix_post": dg2,
               "g_mlp_pre": dg3, "g_mlp_post": dg4, "loss": lossvec, "b_in_gates": dbg}
    gw_sa = by_chip(_pack(small_a, _SMALL_A, _SMALL_A_ROWS).reshape(N_DEV, _SMALL_A_ROWS // N_DEV, LANES))
    r1_in, r1_sa = _comm_call(_rider_rs_sibling([gw_in, gw_sa]), "rs_sibling_in")
    s_in, s_sa = _chip_sum_call(cidx, [gw_in, gw_sa], [r1_in, r1_sa], [MXU_DTYPE, F32], "rs_chip_sum_in")
    (c_in,), tok = _copies_start([_rider_rs_chips([s_in, s_sa])], "rs_chips_start_in")
    (gx, dg1, dba_in), _ = _inproj_bwd_call(du, dq, dk, dv, dgates, dx1, xt, win_t, g_mix_pre, rider=_after(tok))
    r2_down, r2_up, r2_out, r2_bp, r2_ba, r2_in, r2_sa = _copies_wait([c_down, c_up, c_obb, c_in], dg1, "rs_chips_wait")

    (g_sa,) = _final_sum_call(cidx, [gw_sa], [r1_sa], [r2_sa])
    part_b = _pack({"g_mix_pre": dg1, "b_in_head": dba_in}, _SMALL_B, sum(r for _, r in _SMALL_B))
    (c_small,), tok = _copies_start([_rider_gather_remote([g_sa, part_b])], "allgather_small_start")

    in_t = _adamw_rs_call(cidx, tok, [gw_in], [r1_in], [r2_in], [w_in[0].T], [m_w_in[0].T], [v_w_in[0].T], 2,
                          "adamw_w_in")
    rest = _adamw_rs_call(
        cidx, tok, [gw_bp, gw_ba, gw_out, gw_up, gw_down], [r1_bp, r1_ba, r1_out, r1_up, r1_down],
        [r2_bp, r2_ba, r2_out, r2_up, r2_down], [w_branch_pool[0], w_branch_attn[0], w_out[0], w_up[0], w_down[0]],
        [m_w_branch_pool[0], m_w_branch_attn[0], m_w_out[0], m_w_up[0], m_w_down[0]],
        [v_w_branch_pool[0], v_w_branch_attn[0], v_w_out[0], v_w_up[0], v_w_down[0]], N_DEV, "adamw_shards")
    big_g, big_d, big_m2, big_v2 = ([a[0].T] + list(b) for a, b in zip(in_t, rest))

    sa_all, sb_all = _copies_wait([c_small], rest[0][0], "allgather_small_wait")
    me = (_slot((cx, cy, cc)), 0, 0)
    sa_all = lax.dynamic_update_slice(sa_all, g_sa[None], me)
    sb_sum = _sum8_call(lax.dynamic_update_slice(sb_all, part_b[None], me))

    names = ["g_mix_pre", "b_in", "w_pool", "pool_scale", "attn_sinks", "g_mix_post", "g_mlp_pre", "g_mlp_post"]
    sm_w = dict(g_mix_pre=g_mix_pre, b_in=b_in, w_pool=w_pool, pool_scale=pool_scale, attn_sinks=attn_sinks,
                g_mix_post=g_mix_post, g_mlp_pre=g_mlp_pre, g_mlp_post=g_mlp_post)
    sm_m = dict(g_mix_pre=m_g_mix_pre, b_in=m_b_in, w_pool=m_w_pool, pool_scale=m_pool_scale, attn_sinks=m_attn_sinks,
                g_mix_post=m_g_mix_post, g_mlp_pre=m_g_mlp_pre, g_mlp_post=m_g_mlp_post)
    sm_v = dict(g_mix_pre=v_g_mix_pre, b_in=v_b_in, w_pool=v_w_pool, pool_scale=v_pool_scale, attn_sinks=v_attn_sinks,
                g_mix_post=v_g_mix_post, g_mlp_pre=v_g_mlp_pre, g_mlp_post=v_g_mlp_post)
    sizes = {k: sm_w[k].size for k in names}
    sizes.update(loss=D_MODEL, b_in_gates=GATE_WIDTH, b_in_head=C_G)
    sm_g = _unpack(sa_all.reshape(_SMALL_A_ROWS, LANES), _SMALL_A, sizes)
    sm_g.update(_unpack(sb_sum, _SMALL_B, sizes))
    sm_g["b_in"] = jnp.concatenate([sm_g["b_in_head"], sm_g["b_in_gates"]])
    loss = (0.5 / D_MODEL) * jnp.sum(sm_g["loss"])
    two_d = lambda a: a.reshape(-1, a.shape[-1])
    sd_, sm2_, sv2_ = _adamw_call([two_d(sm_w[k]) for k in names], [two_d(sm_g[k].reshape(sm_w[k].shape)) for k in names],
                                  [two_d(sm_m[k]) for k in names], [two_d(sm_v[k]) for k in names], 1, "adamw_small")
    like = lambda vals: {k: a.reshape(sm_w[k].shape) for k, a in zip(names, vals)}
    sm_d, sm_m2, sm_v2 = like(sd_), like(sm2_), like(sv2_)
    sm_gr = {k: sm_g[k].reshape(sm_w[k].shape) for k in names}

    order = ["g_mix_pre", "w_in", "b_in", "w_pool", "pool_scale", "attn_sinks", "w_branch_pool", "w_branch_attn",
             "w_out", "g_mix_post", "g_mlp_pre", "w_up", "w_down", "g_mlp_post"]
    big_names = ["w_in", "w_branch_pool", "w_branch_attn", "w_out", "w_up", "w_down"]
    lead = lambda a: a[None]
    tables = []
    for small_t, big_t in ((sm_gr, big_g), (sm_d, big_d), (sm_m2, big_m2), (sm_v2, big_v2)):
        bt = dict(zip(big_names, big_t))
        tables.append([lead(bt[k]) if k in bt else small_t[k] for k in order])
    return (loss, gx.reshape(B, S, D_MODEL), *tables[0], *tables[1], *tables[2], *tables[3])
```

```python
import jax
import jax.numpy as jnp
from jax import lax
from jax.experimental import pallas as pl
from jax.experimental.pallas import tpu as pltpu

F32 = jnp.float32
MXU_DTYPE = jnp.bfloat16
MESH = pl.DeviceIdType.MESH

D_MODEL = 1024
POOL_WINDOWS = (2, 4, 8, 16)
POOL_WIDTH = 512
POOL_GC = 128
HEAD_DIM = 64
N_Q_HEADS = 8
N_KV_HEADS = 2
GROUP = 4
ATTN_WIDTH = 512
KV_WIDTH = 128
BLOCK = 128
GATE_WIDTH = 2048
IN_WIDTH = 3328
D_FF = 4096
EPS = 1e-6
NEG_INF = -1e30
ROPE_THETA = 500000.0
ROT_DIM = 16
SCALE = HEAD_DIM ** -0.5
C_Q, C_K, C_V, C_G = 512, 1024, 1152, 1280

ADAM_LR = 0.001
ADAM_B1 = 0.9
ADAM_B2 = 0.999
ADAM_EPS = 1e-08
ADAM_WD = 0.01
ADAM_STEP = 10

N_DEV = 8
LANES = 128
VMEM_LIMIT = 56 * 1024 * 1024

NN = (((1,), (0,)), ((), ()))
NT = (((1,), (1,)), ((), ()))
TN = (((0,), (0,)), ((), ()))


def _dot(a, b, dims):
    return lax.dot_general(a, b, dims, preferred_element_type=F32)


def _params(sem=None):
    return pltpu.CompilerParams(dimension_semantics=sem, vmem_limit_bytes=VMEM_LIMIT)


def _tile(n, pref):
    t = min(n, pref)
    assert n % t == 0, (n, t)
    return t


class _Rider:
    def __init__(self, ins, out_shape, n_remote, n_local, plan, aliases=None, lands=None):
        self.ins, self.out_shape, self.n_remote, self.n_local = list(ins), list(out_shape), n_remote, n_local
        self.plan, self.aliases = plan, dict(aliases or {})
        self.lands = lands


def _after(token, rider=None):
    r = rider or _Rider([], [], 0, 0, lambda ins, outs, send, recv, loc, r0, l0: ([], []))
    return _Rider(r.ins + [token], r.out_shape, r.n_remote, r.n_local, r.plan, r.aliases)


def _launch(body, args, *, name, grid, in_specs, out_specs, out_shape, scratch_shapes=(), sem=None, rider=None):
    if rider is None:
        return pl.pallas_call(body, name=name, grid=grid, in_specs=in_specs, out_specs=out_specs, out_shape=out_shape,
                              scratch_shapes=list(scratch_shapes), compiler_params=_params(sem))(*args)
    n_in, n_out, n_scr = len(args), len(out_shape), len(scratch_shapes)
    r_in, r_out = len(rider.ins), len(rider.out_shape)
    copies = rider.n_remote + rider.n_local > 0

    def wrapped(*refs):
        ins, rins = refs[:n_in], refs[n_in:n_in + r_in]
        o0 = n_in + r_in
        outs, routs = refs[o0:o0 + n_out], refs[o0 + n_out:o0 + n_out + r_out]
        s0 = o0 + n_out + r_out
        scr = refs[s0:s0 + n_scr]
        if not copies:
            return body(*ins, *outs, *scr)
        send, recv, loc = refs[s0 + n_scr:]
        first, last = None, None
        for d in range(len(grid)):
            f, l = pl.program_id(d) == 0, pl.program_id(d) == pl.num_programs(d) - 1
            first = f if first is None else first & f
            last = l if last is None else last & l

        def start():
            remote, local = rider.plan(rins, routs, send, recv, loc, 0, 0)
            for cp in local + remote:
                cp.start()

        def finish():
            remote, local = rider.plan(rins, routs, send, recv, loc, 0, 0)
            for cp in remote + local:
                cp.wait()

        if first is None:
            start()
            body(*ins, *outs, *scr)
            finish()
        else:
            pl.when(first)(start)
            body(*ins, *outs, *scr)
            pl.when(last)(finish)

    hbm = pl.BlockSpec(memory_space=pl.ANY)
    dma = pltpu.SemaphoreType.DMA
    res = pl.pallas_call(
        wrapped, name=name, grid=grid, in_specs=list(in_specs) + [hbm] * r_in,
        out_specs=list(out_specs) + [hbm] * r_out, out_shape=list(out_shape) + rider.out_shape,
        scratch_shapes=list(scratch_shapes) + (
            [dma((max(rider.n_remote, 1),)), dma((max(rider.n_remote, 1),)), dma((max(rider.n_local, 1),))] if copies else []),
        input_output_aliases={n_in + i: n_out + o for i, o in rider.aliases.items()},
        compiler_params=_params(sem),
    )(*args, *rider.ins)
    return list(res[:n_out]), list(res[n_out:])


def _comm_call(rider, name):
    return _launch(lambda: None, [], name=name, grid=(), in_specs=[], out_specs=[], out_shape=[], rider=rider)[1]


_HBM = pl.BlockSpec(memory_space=pltpu.HBM)
_SEM = pl.BlockSpec(memory_space=pltpu.SEMAPHORE)
_EFFECT = pltpu.SideEffectType.DATAFLOW_SIDE_EFFECTING


def _copies_start(riders, name, after=None):
    assert all(r.n_local == 0 and not r.aliases for r in riders)
    extra = [] if after is None else [after]
    sizes = [(len(r.ins), len(r.out_shape)) for r in riders]
    bufs = []
    for r in riders:
        lands = r.lands or [lax.empty(s.shape, s.dtype) for s in r.out_shape]
        bufs += [pltpu.with_memory_space_constraint(a, pltpu.HBM) for a in list(r.ins) + list(lands)]
    nb, ng, ne = len(bufs), len(riders), len(extra)

    def body(*refs):
        sems, token, at = refs[2 * nb + ne:2 * nb + ne + 2 * ng], refs[-1], 0
        for g, (r, (ni, no)) in enumerate(zip(riders, sizes)):
            remote, _ = r.plan(refs[at:at + ni], refs[at + ni:at + ni + no], sems[2 * g], sems[2 * g + 1], None, 0, 0)
            for cp in remote:
                cp.start()
            at += ni + no
        token[...] = jnp.zeros_like(token)

    res = pl.pallas_call(
        body, name=name, in_specs=[_HBM] * nb + [pl.BlockSpec(memory_space=pl.ANY)] * ne,
        out_specs=[_HBM] * nb + [_SEM] * (2 * ng) + [pl.BlockSpec(memory_space=pltpu.VMEM)],
        out_shape=[pltpu.HBM(a.shape, a.dtype) for a in bufs]
        + [pltpu.SemaphoreType.DMA((r.n_remote,)) for r in riders for _ in range(2)]
        + [jax.ShapeDtypeStruct((8, LANES), F32)],
        input_output_aliases={i: i for i in range(nb)},
        compiler_params=pltpu.CompilerParams(has_side_effects=_EFFECT),
    )(*bufs, *extra)
    handles, at = [], 0
    for g, (r, (ni, no)) in enumerate(zip(riders, sizes)):
        handles.append((r, list(res[at:at + ni + no]), res[nb + 2 * g], res[nb + 2 * g + 1]))
        at += ni + no
    return handles, res[-1]


def _copies_wait(handles, after, name):
    bufs = [b for _, bs, _, _ in handles for b in bs]
    sems = [s for _, _, send, recv in handles for s in (send, recv)]
    nb, ng = len(bufs), len(handles)

    def body(*refs):
        at = 0
        for g, (rider, bs, _, _) in enumerate(handles):
            ni = len(rider.ins)
            remote, _ = rider.plan(refs[at:at + ni], refs[at + ni:at + len(bs)], refs[nb + 2 * g], refs[nb + 2 * g + 1],
                                   None, 0, 0)
            for cp in remote:
                cp.wait_send()
                cp.wait_recv()
            at += len(bs)

    res = pl.pallas_call(
        body, name=name, in_specs=[_HBM] * nb + [_SEM] * (2 * ng) + [pl.BlockSpec(memory_space=pl.ANY)],
        out_specs=[_HBM] * nb, out_shape=[pltpu.HBM(a.shape, a.dtype) for a in bufs],
        input_output_aliases={i: i for i in range(nb)},
        compiler_params=pltpu.CompilerParams(has_side_effects=_EFFECT),
    )(*bufs, *sems, after)
    lands, at = [], 0
    for rider, bs, _, _ in handles:
        lands += list(res[at + len(rider.ins):at + len(bs)])
        at += len(bs)
    return lands


def _rms_r(x):
    return lax.rsqrt(jnp.mean(x * x, axis=-1, keepdims=True) + EPS)


def _rms_bwd(dn, x, r, g):
    xh = x * r
    dxh = dn * g
    dx = r * (dxh - xh * jnp.mean(dxh * xh, axis=-1, keepdims=True))
    return dx, dn * xh


def _rot(t, c, sa, sb):
    outs = []
    for j in range(t.shape[1] // LANES):
        tj = t[:, LANES * j:LANES * (j + 1)]
        outs.append(tj * c + pltpu.roll(tj, LANES - 8, 1) * sa + pltpu.roll(tj, 8, 1) * sb)
    return outs[0] if len(outs) == 1 else jnp.concatenate(outs, axis=1)


def _rot_tables(S):
    pos = jnp.arange(S, dtype=F32)
    inv_freq = ROPE_THETA ** (-jnp.arange(0, ROT_DIM, 2, dtype=F32) / ROT_DIM)
    ang = pos[:, None] * inv_freq[None, :]
    cos, sin = jnp.cos(ang), jnp.sin(ang)
    one = jnp.ones((S, HEAD_DIM - ROT_DIM), F32)
    zero = jnp.zeros((S, HEAD_DIM - ROT_DIM), F32)
    z8 = jnp.zeros((S, 8), F32)
    c = jnp.concatenate([cos, cos, one], axis=1)
    sa = jnp.concatenate([-sin, z8, zero], axis=1)
    sb = jnp.concatenate([z8, sin, zero], axis=1)
    rep = LANES // HEAD_DIM
    return jnp.tile(c, (1, rep)), jnp.tile(sa, (1, rep)), jnp.tile(sb, (1, rep))


def _lane_tile4(k):
    lane = lax.broadcasted_iota(jnp.int32, k.shape, 1)
    rk = pltpu.roll(k, HEAD_DIM, 1)
    x0 = jnp.where(lane < HEAD_DIM, k, rk)
    x1 = jnp.where(lane < HEAD_DIM, rk, k)
    return jnp.concatenate([x0, x0, x1, x1], axis=1)


def _fold_heads(acc):
    zs = []
    for hk in range(N_KV_HEADS):
        a = acc[:, 256 * hk:256 * hk + LANES] + acc[:, 256 * hk + LANES:256 * (hk + 1)]
        zs.append(a + pltpu.roll(a, HEAD_DIM, 1))
    lane = lax.broadcasted_iota(jnp.int32, zs[0].shape, 1)
    return jnp.where(lane < HEAD_DIM, zs[0], zs[1])


def _inproj_call(x, g1, win_t, b_in, rc, rsa, rsb, S, rider=None):
    T = x.shape[0]
    tm = _tile(S, 512)
    nst = S // tm

    def body(x_ref, g1_ref, w_ref, b_ref, c_ref, sa_ref, sb_ref,
             h_ref, u_ref, q_ref, k4_ref, v4_ref, g_ref):
        xv = x_ref[...]
        hb = ((xv * _rms_r(xv)) * g1_ref[...]).astype(MXU_DTYPE)
        h_ref[...] = hb

        def proj(lo, hi):
            return _dot(hb, w_ref[lo:hi, :], NT) + b_ref[:, lo:hi]

        c, sa, sb = c_ref[...], sa_ref[...], sb_ref[...]
        u_ref[...] = proj(0, C_Q)
        q_ref[...] = (_rot(proj(C_Q, C_K), c, sa, sb) * SCALE).astype(MXU_DTYPE)
        kv = proj(C_K, C_G)
        k4_ref[...] = _lane_tile4(_rot(kv[:, :KV_WIDTH], c, sa, sb)).astype(MXU_DTYPE)
        v4_ref[...] = _lane_tile4(kv[:, KV_WIDTH:]).astype(MXU_DTYPE)
        g_ref[...] = jax.nn.sigmoid(proj(C_G, IN_WIDTH)).astype(MXU_DTYPE)

    tok = lambda w: pl.BlockSpec((tm, w), lambda i: (i, 0))
    full = lambda a: pl.BlockSpec(a.shape, lambda i: (0,) * a.ndim)
    tab = pl.BlockSpec((tm, LANES), lambda i: (i % nst, 0))
    return _launch(
        body, [x, g1, win_t, b_in, rc, rsa, rsb], name="inproj_fwd", grid=(T // tm,),
        in_specs=[tok(D_MODEL), full(g1), full(win_t), full(b_in), tab, tab, tab],
        out_specs=[tok(D_MODEL), tok(POOL_WIDTH), tok(ATTN_WIDTH), tok(512), tok(512), tok(GATE_WIDTH)],
        out_shape=[jax.ShapeDtypeStruct((T, D_MODEL), MXU_DTYPE), jax.ShapeDtypeStruct((T, POOL_WIDTH), F32),
                   jax.ShapeDtypeStruct((T, ATTN_WIDTH), MXU_DTYPE), jax.ShapeDtypeStruct((T, 512), MXU_DTYPE),
                   jax.ShapeDtypeStruct((T, 512), MXU_DTYPE), jax.ShapeDtypeStruct((T, GATE_WIDTH), MXU_DTYPE)],
        sem=("arbitrary",), rider=rider)


def _shift_rows(a, k, rows):
    n = a.shape[0]
    if k > 0:
        return jnp.where(rows >= k, pltpu.roll(a, k, 0), 0.0)
    return jnp.where(rows < n + k, pltpu.roll(a, n + k, 0), 0.0)


def _win_sum(a, w, rows, sign):
    s, k = a, 1
    while k < w:
        s = s + _shift_rows(s, sign * k, rows)
        k *= 2
    return s


def _pool_diff(ug, w, rows):
    inv = 1.0 / jnp.minimum(rows + 1, w).astype(F32)
    return _win_sum(ug, w, rows, 1) * inv - ug, inv


def _pool_call(u, w_pool, pool_scale, S):
    T = u.shape[0]

    def body(u_ref, w_ref, ps_ref, y_ref):
        rows = lax.broadcasted_iota(jnp.int32, (S, POOL_GC), 0)
        for gi, w in enumerate(POOL_WINDOWS):
            sl = slice(POOL_GC * gi, POOL_GC * (gi + 1))
            diff, _ = _pool_diff(u_ref[:, sl], w, rows)
            mixed = _dot(diff.astype(MXU_DTYPE), w_ref[gi], NN)
            y_ref[:, sl] = (mixed * ps_ref[:, sl]).astype(MXU_DTYPE)

    seq = pl.BlockSpec((S, POOL_WIDTH), lambda b: (b, 0))
    return pl.pallas_call(
        body, name="pool_fwd", grid=(T // S,),
        in_specs=[seq, pl.BlockSpec(w_pool.shape, lambda b: (0, 0, 0)), pl.BlockSpec(pool_scale.shape, lambda b: (0, 0))],
        out_specs=seq, out_shape=jax.ShapeDtypeStruct((T, POOL_WIDTH), MXU_DTYPE),
        compiler_params=_params(("arbitrary",)),
    )(u, w_pool, pool_scale)


def _pool_bwd_call(u, dyp, w_pool, pool_scale, S, rider=None):
    T = u.shape[0]

    def body(u_ref, dy_ref, w_ref, ps_ref, du_ref, dw_ref, dps_ref):
        @pl.when(pl.program_id(0) == 0)
        def _():
            dw_ref[...] = jnp.zeros_like(dw_ref)
            dps_ref[...] = jnp.zeros_like(dps_ref)

        rows = lax.broadcasted_iota(jnp.int32, (S, POOL_GC), 0)
        for gi, w in enumerate(POOL_WINDOWS):
            sl = slice(POOL_GC * gi, POOL_GC * (gi + 1))
            diff, inv = _pool_diff(u_ref[:, sl], w, rows)
            diffb = diff.astype(MXU_DTYPE)
            wg = w_ref[gi]
            mixed = _dot(diffb, wg, NN)
            dy = dy_ref[:, sl]
            dps_ref[:, sl] += jnp.sum(dy * mixed, axis=0, keepdims=True)
            dmb = (dy * ps_ref[:, sl]).astype(MXU_DTYPE)
            dw_ref[gi] += _dot(diffb, dmb, TN)
            ddiff = _dot(dmb, wg, NT)
            du_ref[:, sl] = (_win_sum(ddiff * inv, w, rows, -1) - ddiff).astype(MXU_DTYPE)

    seq = pl.BlockSpec((S, POOL_WIDTH), lambda b: (b, 0))
    return _launch(
        body, [u, dyp, w_pool, pool_scale], name="pool_bwd", grid=(T // S,),
        in_specs=[seq, seq, pl.BlockSpec(w_pool.shape, lambda b: (0, 0, 0)), pl.BlockSpec(pool_scale.shape, lambda b: (0, 0))],
        out_specs=[seq, pl.BlockSpec(w_pool.shape, lambda b: (0, 0, 0)), pl.BlockSpec(pool_scale.shape, lambda b: (0, 0))],
        out_shape=[jax.ShapeDtypeStruct((T, POOL_WIDTH), MXU_DTYPE), jax.ShapeDtypeStruct(w_pool.shape, F32),
                   jax.ShapeDtypeStruct(pool_scale.shape, F32)],
        sem=("arbitrary",), rider=rider)


def _attn_consts():
    lane_g = lax.broadcasted_iota(jnp.int32, (BLOCK, 256), 1) >> 6
    rgrp = lax.broadcasted_iota(jnp.int32, (GROUP * BLOCK, 1), 0) >> 7
    rel = lax.broadcasted_iota(jnp.int32, (BLOCK, 256), 0) - lax.broadcasted_iota(jnp.int32, (BLOCK, 256), 1)

    def bias(off):
        ok = (rel + off >= 0) & (rel + off < BLOCK)
        return jnp.concatenate([jnp.where(ok, 0.0, NEG_INF)] * GROUP, axis=0)

    return lane_g, rgrp, bias(0), bias(BLOCK)


def _sink_rows(sink_ref, hk, rgrp):
    sv = jnp.zeros(rgrp.shape, F32)
    for g in range(GROUP):
        sv = jnp.where(rgrp == g, sink_ref[0, GROUP * hk + g], sv)
    return sv


def _stack_heads(xb, lane_g):
    return jnp.concatenate([jnp.where(lane_g == g, xb, jnp.zeros_like(xb)) for g in range(GROUP)], axis=0)


def _unstack_heads(xs, lane_g):
    out = jnp.where(lane_g == 0, xs[0:BLOCK], 0.0)
    for g in range(1, GROUP):
        out = out + jnp.where(lane_g == g, xs[BLOCK * g:BLOCK * (g + 1)], 0.0)
    return out


def _attn_probs(qs, kb, bias, sv):
    s = _dot(qs, kb, NT) + bias
    m = jnp.maximum(jnp.max(s, axis=1, keepdims=True), sv)
    e = jnp.exp(s - m)
    es = jnp.exp(sv - m)
    inv_l = 1.0 / (jnp.sum(e, axis=1, keepdims=True) + es)
    return e * inv_l, es * inv_l


def _attn_blocks(nb, blk, carry, per=1):
    carry = blk(0, 0, True, carry)
    per = per if (nb - 1) % per == 0 else 1

    def step(i, c):
        for k in range(per):
            n = 1 + per * i + k
            c = blk(pl.multiple_of(n * BLOCK, BLOCK), pl.multiple_of((n - 1) * BLOCK, BLOCK), False, c)
        return c

    return lax.fori_loop(0, (nb - 1) // per, step, carry)


def _attn_call(sinks, q, k4, v4, S, rider=None):
    T = q.shape[0]
    nb = S // BLOCK

    def body(sink_ref, q_ref, k_ref, v_ref, o_ref):
        lane_g, rgrp, bias_first, bias_later = _attn_consts()
        svs = [_sink_rows(sink_ref, hk, rgrp) for hk in range(N_KV_HEADS)]

        def blk(q0, k0, first, carry):
            for hk in range(N_KV_HEADS):
                cs = slice(256 * hk, 256 * (hk + 1))
                qs = _stack_heads(q_ref[pl.ds(q0, BLOCK), cs], lane_g)
                p, _ = _attn_probs(qs, k_ref[pl.ds(k0, 2 * BLOCK), cs], bias_first if first else bias_later, svs[hk])
                o = _dot(p.astype(MXU_DTYPE), v_ref[pl.ds(k0, 2 * BLOCK), cs], NN)
                o_ref[pl.ds(q0, BLOCK), cs] = _unstack_heads(o, lane_g).astype(MXU_DTYPE)
            return carry

        _attn_blocks(nb, blk, 0, per=3)

    seq = pl.BlockSpec((S, ATTN_WIDTH), lambda b: (b, 0))
    return _launch(
        body, [sinks, q, k4, v4], name="attn_fwd", grid=(T // S,),
        in_specs=[pl.BlockSpec(memory_space=pltpu.SMEM), seq, seq, seq],
        out_specs=[seq], out_shape=[jax.ShapeDtypeStruct((T, ATTN_WIDTH), MXU_DTYPE)],
        sem=("arbitrary",), rider=rider)


def _attn_bwd_call(sinks, q, k4, v4, do, rc, rsa, rsb, S, rider=None):
    T = q.shape[0]
    nb = S // BLOCK

    def body(sink_ref, q_ref, k_ref, v_ref, do_ref, c_ref, sa_ref, sb_ref,
             dq_ref, dk_ref, dv_ref, ds_ref, dk_acc, dv_acc):
        lane_g, rgrp, bias_first, bias_later = _attn_consts()
        svs = [_sink_rows(sink_ref, hk, rgrp) for hk in range(N_KV_HEADS)]
        lane1 = lax.broadcasted_iota(jnp.int32, (1, LANES), 1)
        dk_acc[...] = jnp.zeros_like(dk_acc)
        dv_acc[...] = jnp.zeros_like(dv_acc)

        def blk(q0, k0, first, dsink):
            rows = pl.ds(q0, BLOCK)
            c, sa, sb = c_ref[rows, :], sa_ref[rows, :], sb_ref[rows, :]
            for hk in range(N_KV_HEADS):
                cs = slice(256 * hk, 256 * (hk + 1))
                qs = _stack_heads(q_ref[rows, cs], lane_g)
                dos = _stack_heads(do_ref[rows, cs], lane_g)
                kb = k_ref[pl.ds(k0, 2 * BLOCK), cs]
                vb = v_ref[pl.ds(k0, 2 * BLOCK), cs]
                p, ps = _attn_probs(qs, kb, bias_first if first else bias_later, svs[hk])
                dp = _dot(dos, vb, NT)
                delta = jnp.sum(p * dp, axis=1, keepdims=True)
                dsb = (p * (dp - delta)).astype(MXU_DTYPE)
                dqb = _unstack_heads(_dot(dsb, kb, NN), lane_g) * SCALE
                dq_ref[rows, cs] = _rot(dqb, c, -sa, -sb).astype(MXU_DTYPE)
                dk_acc[pl.ds(k0, 2 * BLOCK), cs] += _dot(dsb, qs, TN)
                dv_acc[pl.ds(k0, 2 * BLOCK), cs] += _dot(p.astype(MXU_DTYPE), dos, TN)
                psd = ps * delta
                for g in range(GROUP):
                    val = -jnp.sum(psd[BLOCK * g:BLOCK * (g + 1)], axis=0, keepdims=True)
                    dsink = dsink + jnp.where(lane1 == GROUP * hk + g, val, 0.0)
            return dsink

        dsink = _attn_blocks(nb, blk, jnp.zeros((1, LANES), F32))
        dk_ref[...] = _rot(_fold_heads(dk_acc[...]), c_ref[...], -sa_ref[...], -sb_ref[...]).astype(MXU_DTYPE)
        dv_ref[...] = _fold_heads(dv_acc[...]).astype(MXU_DTYPE)
        ds_ref[...] = jnp.broadcast_to(dsink, ds_ref.shape)

    seq = pl.BlockSpec((S, ATTN_WIDTH), lambda b: (b, 0))
    kvs = pl.BlockSpec((S, KV_WIDTH), lambda b: (b, 0))
    tab = pl.BlockSpec((S, LANES), lambda b: (0, 0))
    nseq = T // S
    return _launch(
        body, [sinks, q, k4, v4, do, rc, rsa, rsb], name="attn_bwd", grid=(nseq,),
        in_specs=[pl.BlockSpec(memory_space=pltpu.SMEM), seq, seq, seq, seq, tab, tab, tab],
        out_specs=[seq, kvs, kvs, pl.BlockSpec((8, LANES), lambda b: (b, 0))],
        out_shape=[jax.ShapeDtypeStruct((T, ATTN_WIDTH), MXU_DTYPE), jax.ShapeDtypeStruct((T, KV_WIDTH), MXU_DTYPE),
                   jax.ShapeDtypeStruct((T, KV_WIDTH), MXU_DTYPE), jax.ShapeDtypeStruct((8 * nseq, LANES), F32)],
        scratch_shapes=[pltpu.VMEM((S, 512), F32), pltpu.VMEM((S, 512), F32)],
        sem=("arbitrary",), rider=rider)


def _branch_weights(wbp_ref, wba_ref, wbp_s, wba_s):
    @pl.when(pl.program_id(0) == 0)
    def _():
        for j in range(N_DEV):
            wbp_s[:, LANES * j:LANES * (j + 1)] = wbp_ref[j]
            wba_s[:, LANES * j:LANES * (j + 1)] = wba_ref[j]


def _mix_fwd_call(yp, ya, g, x, wbp, wba, wout, g2, g3, rider=None):
    T = x.shape[0]
    tm = _tile(T, 512)

    def body(yp_ref, ya_ref, g_ref, x_ref, wbp_ref, wba_ref, wout_ref, g2_ref, g3_ref,
             mix_ref, x1_ref, h2_ref, h2t_ref, wbp_s, wba_s):
        _branch_weights(wbp_ref, wba_ref, wbp_s, wba_s)
        bp = _dot(yp_ref[...], wbp_s[...], NN)
        ba = _dot(ya_ref[...], wba_s[...], NN)
        merged = g_ref[:, :D_MODEL].astype(F32) * bp + g_ref[:, D_MODEL:].astype(F32) * ba
        mix = _dot(merged.astype(MXU_DTYPE), wout_ref[...], NN)
        mix_ref[...] = mix
        x1 = x_ref[...] + (mix * _rms_r(mix)) * g2_ref[...]
        x1_ref[...] = x1
        h2 = (x1 * _rms_r(x1)) * g3_ref[...]
        h2_ref[...] = h2.astype(MXU_DTYPE)
        h2t_ref[...] = h2.T.astype(MXU_DTYPE)

    tok = lambda w: pl.BlockSpec((tm, w), lambda i: (i, 0))
    full = lambda a: pl.BlockSpec(a.shape, lambda i: (0,) * a.ndim)
    return _launch(
        body, [yp, ya, g, x, wbp, wba, wout, g2, g3], name="mix_fwd", grid=(T // tm,),
        in_specs=[tok(POOL_WIDTH), tok(ATTN_WIDTH), tok(GATE_WIDTH), tok(D_MODEL), full(wbp), full(wba), full(wout),
                  full(g2), full(g3)],
        out_specs=[tok(D_MODEL), tok(D_MODEL), tok(D_MODEL), pl.BlockSpec((D_MODEL, tm), lambda i: (0, i))],
        out_shape=[jax.ShapeDtypeStruct((T, D_MODEL), F32), jax.ShapeDtypeStruct((T, D_MODEL), F32),
                   jax.ShapeDtypeStruct((T, D_MODEL), MXU_DTYPE), jax.ShapeDtypeStruct((D_MODEL, T), MXU_DTYPE)],
        scratch_shapes=[pltpu.VMEM((POOL_WIDTH, D_MODEL), MXU_DTYPE), pltpu.VMEM((ATTN_WIDTH, D_MODEL), MXU_DTYPE)],
        sem=("arbitrary",), rider=rider)


def _mix_bwd_call(dx1, mix, yp, ya, g, wbp, wba, wout, g2, rider=None):
    T = dx1.shape[0]
    tm = _tile(T, 512)

    def body(dx1_ref, mix_ref, yp_ref, ya_ref, g_ref, wbp_ref, wba_ref, wout_ref, g2_ref,
             dyp_ref, do_ref, dgates_ref, dg2_ref, dbg_ref, gout_ref, gbp_ref, gba_ref,
             wbp_s, wba_s, acc_out, acc_bp, acc_ba, sem):
        _branch_weights(wbp_ref, wba_ref, wbp_s, wba_s)
        step = pl.program_id(0)

        @pl.when(step == 0)
        def _():
            dg2_ref[...] = jnp.zeros_like(dg2_ref)
            dbg_ref[...] = jnp.zeros_like(dbg_ref)
            acc_out[...] = jnp.zeros_like(acc_out)
            acc_bp[...] = jnp.zeros_like(acc_bp)
            acc_ba[...] = jnp.zeros_like(acc_ba)

        mix = mix_ref[...]
        dmix, dg2 = _rms_bwd(dx1_ref[...], mix, _rms_r(mix), g2_ref[...])
        dg2_ref[...] += jnp.sum(dg2, axis=0, keepdims=True)
        dmixb = dmix.astype(MXU_DTYPE)
        dmerged = _dot(dmixb, wout_ref[...], NT)
        yp, ya = yp_ref[...], ya_ref[...]
        bp = _dot(yp, wbp_s[...], NN)
        ba = _dot(ya, wba_s[...], NN)
        gp, ga = g_ref[:, :D_MODEL].astype(F32), g_ref[:, D_MODEL:].astype(F32)
        acc_out[...] += _dot((gp * bp + ga * ba).astype(MXU_DTYPE), dmixb, TN)
        dgp = dmerged * bp * (gp * (1.0 - gp))
        dga = dmerged * ba * (ga * (1.0 - ga))
        dbg_ref[:, :D_MODEL] += jnp.sum(dgp, axis=0, keepdims=True)
        dbg_ref[:, D_MODEL:] += jnp.sum(dga, axis=0, keepdims=True)
        dgates_ref[:, :D_MODEL] = dgp.astype(MXU_DTYPE)
        dgates_ref[:, D_MODEL:] = dga.astype(MXU_DTYPE)
        dbp = (dmerged * gp).astype(MXU_DTYPE)
        dba = (dmerged * ga).astype(MXU_DTYPE)
        acc_bp[...] += _dot(yp, dbp, TN)
        acc_ba[...] += _dot(ya, dba, TN)
        dyp_ref[...] = _dot(dbp, wbp_s[...], NT)
        do_ref[...] = _dot(dba, wba_s[...], NT).astype(MXU_DTYPE)

        @pl.when(step == pl.num_programs(0) - 1)
        def _():
            copies = [pltpu.make_async_copy(acc_out, gout_ref, sem.at[0])]
            for j in range(N_DEV):
                cols = slice(LANES * j, LANES * (j + 1))
                copies.append(pltpu.make_async_copy(acc_bp.at[:, cols], gbp_ref.at[j], sem.at[1 + j]))
                copies.append(pltpu.make_async_copy(acc_ba.at[:, cols], gba_ref.at[j], sem.at[1 + N_DEV + j]))
            for cp in copies:
                cp.start()
            for cp in copies:
                cp.wait()

    tok = lambda w: pl.BlockSpec((tm, w), lambda i: (i, 0))
    full = lambda a: pl.BlockSpec(a.shape, lambda i: (0,) * a.ndim)
    acc = lambda w: pl.BlockSpec((1, w), lambda i: (0, 0))
    hbm = pl.BlockSpec(memory_space=pl.ANY)
    sd = jax.ShapeDtypeStruct
    return _launch(
        body, [dx1, mix, yp, ya, g, wbp, wba, wout, g2], name="mix_bwd", grid=(T // tm,),
        in_specs=[tok(D_MODEL), tok(D_MODEL), tok(POOL_WIDTH), tok(ATTN_WIDTH), tok(GATE_WIDTH), full(wbp), full(wba),
                  full(wout), full(g2)],
        out_specs=[tok(POOL_WIDTH), tok(ATTN_WIDTH), tok(GATE_WIDTH), acc(D_MODEL), acc(GATE_WIDTH), hbm, hbm, hbm],
        out_shape=[sd((T, POOL_WIDTH), F32), sd((T, ATTN_WIDTH), MXU_DTYPE), sd((T, GATE_WIDTH), MXU_DTYPE),
                   sd((1, D_MODEL), F32), sd((1, GATE_WIDTH), F32), sd((D_MODEL, D_MODEL), F32),
                   sd((N_DEV, POOL_WIDTH, LANES), F32), sd((N_DEV, ATTN_WIDTH, LANES), F32)],
        scratch_shapes=[pltpu.VMEM((POOL_WIDTH, D_MODEL), MXU_DTYPE), pltpu.VMEM((ATTN_WIDTH, D_MODEL), MXU_DTYPE),
                        pltpu.VMEM((D_MODEL, D_MODEL), F32), pltpu.VMEM((POOL_WIDTH, D_MODEL), F32),
                        pltpu.VMEM((ATTN_WIDTH, D_MODEL), F32), pltpu.SemaphoreType.DMA((1 + 2 * N_DEV,))],
        sem=("arbitrary",), rider=rider)


def _mlp_up_call(h2, wup):
    T = h2.shape[0]
    tm = _tile(T, 512)
    fc = D_FF // N_DEV

    def body(h2_ref, wup_ref, act_ref, actt_ref):
        h2 = h2_ref[...]
        for j in range(N_DEV):
            sl = slice(fc * j, fc * (j + 1))
            rl = jnp.maximum(_dot(h2, wup_ref[j], NN), 0.0)
            act = rl * rl
            act_ref[:, sl] = act.astype(MXU_DTYPE)
            actt_ref[sl, :] = act.T.astype(MXU_DTYPE)

    sd = jax.ShapeDtypeStruct
    return pl.pallas_call(
        body, name="mlp_up", grid=(T // tm,),
        in_specs=[pl.BlockSpec((tm, D_MODEL), lambda i: (i, 0)),
                  pl.BlockSpec(wup.shape, lambda i: (0, 0, 0), pipeline_mode=pl.Buffered(1))],
        out_specs=[pl.BlockSpec((tm, D_FF), lambda i: (i, 0)), pl.BlockSpec((D_FF, tm), lambda i: (0, i))],
        out_shape=[sd((T, D_FF), MXU_DTYPE), sd((D_FF, T), MXU_DTYPE)],
        compiler_params=_params(("arbitrary",)),
    )(h2, wup)


def _mlp_call(x1, act, target, wup, wdown, g3, g4):
    T = x1.shape[0]
    tm = _tile(T, 256)
    fc = D_FF // N_DEV

    def body(x1_ref, act_ref, t_ref, wup_ref, wdown_ref, g3_ref, g4_ref,
             da_ref, dff_ref, dx1_ref, dg3_ref, dg4_ref, loss_ref):
        @pl.when(pl.program_id(0) == 0)
        def _():
            dg3_ref[...] = jnp.zeros_like(dg3_ref)
            dg4_ref[...] = jnp.zeros_like(dg4_ref)
            loss_ref[...] = jnp.zeros_like(loss_ref)

        ff = jnp.zeros((tm, D_MODEL), F32)
        for j in range(N_DEV):
            ff = ff + _dot(act_ref[:, fc * j:fc * (j + 1)], wdown_ref[j], NN)
        x1 = x1_ref[...]
        r4 = _rms_r(ff)
        err = x1 + (ff * r4) * g4_ref[...] - t_ref[...]
        loss_ref[...] += jnp.sum(err * err, axis=0, keepdims=True)
        dy = err * (1.0 / D_MODEL)
        dff, dg4 = _rms_bwd(dy, ff, r4, g4_ref[...])
        dg4_ref[...] += jnp.sum(dg4, axis=0, keepdims=True)
        dffb = dff.astype(MXU_DTYPE)
        dff_ref[...] = dffb
        dh2 = jnp.zeros((tm, D_MODEL), F32)
        for j in range(N_DEV):
            sl = slice(fc * j, fc * (j + 1))
            rl = jnp.sqrt(act_ref[:, sl].astype(F32))
            dab = (_dot(dffb, wdown_ref[j], NT) * (2.0 * rl)).astype(MXU_DTYPE)
            da_ref[:, sl] = dab
            dh2 = dh2 + _dot(dab, wup_ref[j], NT)
        dx1, dg3 = _rms_bwd(dh2, x1, _rms_r(x1), g3_ref[...])
        dg3_ref[...] += jnp.sum(dg3, axis=0, keepdims=True)
        dx1_ref[...] = dy + dx1

    tok = lambda w: pl.BlockSpec((tm, w), lambda i: (i, 0))
    full = lambda a: pl.BlockSpec(a.shape, lambda i: (0,) * a.ndim, pipeline_mode=pl.Buffered(1))
    vec = pl.BlockSpec((1, D_MODEL), lambda i: (0, 0))
    sd = jax.ShapeDtypeStruct
    return pl.pallas_call(
        body, name="mlp_down_bwd", grid=(T // tm,),
        in_specs=[tok(D_MODEL), tok(D_FF), tok(D_MODEL), full(wup), full(wdown), vec, vec],
        out_specs=[tok(D_FF), tok(D_MODEL), tok(D_MODEL), vec, vec, vec],
        out_shape=[sd((T, D_FF), MXU_DTYPE), sd((T, D_MODEL), MXU_DTYPE),
                   sd((T, D_MODEL), F32), sd((1, D_MODEL), F32), sd((1, D_MODEL), F32), sd((1, D_MODEL), F32)],
        compiler_params=_params(("arbitrary",)),
    )(x1, act, target, wup, wdown, g3, g4)


def _inproj_bwd_call(du, dq, dk, dv, dgates, dx1, x, win_t, g1, rider=None):
    T = x.shape[0]
    tm = _tile(T, 512)

    def body(du_ref, dq_ref, dk_ref, dv_ref, dgt_ref, dx1_ref, x_ref, w_ref, g1_ref, gx_ref, dg1_ref, db_ref):
        @pl.when(pl.program_id(0) == 0)
        def _():
            dg1_ref[...] = jnp.zeros_like(dg1_ref)
            db_ref[...] = jnp.zeros_like(db_ref)

        dh = jnp.zeros((tm, D_MODEL), F32)
        for ref, lo, hi in ((du_ref, 0, C_Q), (dq_ref, C_Q, C_K), (dk_ref, C_K, C_V), (dv_ref, C_V, C_G),
                            (dgt_ref, C_G, IN_WIDTH)):
            piece = ref[...]
            dh = dh + _dot(piece, w_ref[lo:hi, :], NN)
            if hi <= C_G:
                db_ref[:, lo:hi] += jnp.sum(piece.astype(F32), axis=0, keepdims=True)
        xv = x_ref[...]
        dx, dg1 = _rms_bwd(dh, xv, _rms_r(xv), g1_ref[...])
        dg1_ref[...] += jnp.sum(dg1, axis=0, keepdims=True)
        gx_ref[...] = dx1_ref[...] + dx

    tok = lambda w: pl.BlockSpec((tm, w), lambda i: (i, 0))
    full = lambda a: pl.BlockSpec(a.shape, lambda i: (0,) * a.ndim)
    sd = jax.ShapeDtypeStruct
    return _launch(
        body, [du, dq, dk, dv, dgates, dx1, x, win_t, g1], name="inproj_bwd", grid=(T // tm,),
        in_specs=[tok(POOL_WIDTH), tok(ATTN_WIDTH), tok(KV_WIDTH), tok(KV_WIDTH), tok(GATE_WIDTH), tok(D_MODEL),
                  tok(D_MODEL), full(win_t), full(g1)],
        out_specs=[tok(D_MODEL), pl.BlockSpec((1, D_MODEL), lambda i: (0, 0)), pl.BlockSpec((1, C_G), lambda i: (0, 0))],
        out_shape=[sd((T, D_MODEL), F32), sd((1, D_MODEL), F32), sd((1, C_G), F32)],
        sem=("arbitrary",), rider=rider)


WGRAD_TOKENS = 1024


def _wgrad_rows_call(at, b, name, rider=None):
    K, T = at.shape
    N = b.shape[1]
    tm = _tile(T, WGRAD_TOKENS)
    kb = min(K, 1024)
    per = kb // (K // N_DEV)

    def body(a_ref, b_ref, o_ref):
        @pl.when(pl.program_id(1) == 0)
        def _():
            o_ref[...] = jnp.zeros_like(o_ref)

        d = _dot(a_ref[...], b_ref[...], NN)
        rs = kb // per
        for j in range(per):
            o_ref[j] += d[rs * j:rs * (j + 1)]

    return _launch(
        body, [at, b], name=name, grid=(K // kb, T // tm),
        in_specs=[pl.BlockSpec((kb, tm), lambda i, t: (i, t)), pl.BlockSpec((tm, N), lambda i, t: (t, 0))],
        out_specs=[pl.BlockSpec((per, K // N_DEV, N), lambda i, t: (i, 0, 0))],
        out_shape=[jax.ShapeDtypeStruct((N_DEV, K // N_DEV, N), F32)],
        sem=("arbitrary", "arbitrary"), rider=rider)


def _wgrad_cols_call(at, b, name, rider=None):
    K, T = at.shape
    N = b.shape[1]
    tm = _tile(T, WGRAD_TOKENS)
    nb = min(N, 1024)
    per = nb // (N // N_DEV)

    def body(a_ref, b_ref, o_ref):
        @pl.when(pl.program_id(1) == 0)
        def _():
            o_ref[...] = jnp.zeros_like(o_ref)

        d = _dot(a_ref[...], b_ref[...], NN)
        cs = nb // per
        for j in range(per):
            o_ref[j] += d[:, cs * j:cs * (j + 1)]

    return _launch(
        body, [at, b], name=name, grid=(N // nb, T // tm),
        in_specs=[pl.BlockSpec((K, tm), lambda i, t: (0, t)), pl.BlockSpec((tm, nb), lambda i, t: (t, i))],
        out_specs=[pl.BlockSpec((per, K, N // N_DEV), lambda i, t: (i, 0, 0))],
        out_shape=[jax.ShapeDtypeStruct((N_DEV, K, N // N_DEV), F32)],
        sem=("arbitrary", "arbitrary"), rider=rider)


def _wgrad_in_call(du, dq, dk, dv, dgates, h, rider=None):
    T = h.shape[0]
    tm = _tile(T, WGRAD_TOKENS)
    rows = IN_WIDTH // N_DEV

    def body(du_ref, dq_ref, dk_ref, dv_ref, dgt_ref, h_ref, o_ref, acc, sem):
        t = pl.program_id(0)

        @pl.when(t == 0)
        def _():
            acc[...] = jnp.zeros_like(acc)

        hv = h_ref[...]
        for ref, lo, hi in ((du_ref, 0, C_Q), (dq_ref, C_Q, C_K), (dk_ref, C_K, C_V), (dv_ref, C_V, C_G),
                            (dgt_ref, C_G, IN_WIDTH)):
            acc[lo:hi, :] += _dot(ref[...], hv, TN)

        @pl.when(t == pl.num_programs(0) - 1)
        def _():
            copies = [pltpu.make_async_copy(acc.at[pl.ds(rows * j, rows), :], o_ref.at[j], sem.at[j])
                      for j in range(N_DEV)]
            for cp in copies:
                cp.start()
            for cp in copies:
                cp.wait()

    tok = lambda w: pl.BlockSpec((tm, w), lambda t: (t, 0))
    return _launch(
        body, [du, dq, dk, dv, dgates, h], name="wgrad_in", grid=(T // tm,),
        in_specs=[tok(POOL_WIDTH), tok(ATTN_WIDTH), tok(KV_WIDTH), tok(KV_WIDTH), tok(GATE_WIDTH), tok(D_MODEL)],
        out_specs=[pl.BlockSpec(memory_space=pl.ANY)],
        out_shape=[jax.ShapeDtypeStruct((N_DEV, rows, D_MODEL), F32)],
        scratch_shapes=[pltpu.VMEM((IN_WIDTH, D_MODEL), F32), pltpu.SemaphoreType.DMA((N_DEV,))],
        sem=("arbitrary",), rider=rider)


def _coords():
    return lax.axis_index("x"), lax.axis_index("y"), lax.axis_index("c")


def _allgather_call(shards, bufs):
    n = len(shards)

    def body(*refs):
        ins, outs = refs[:n], refs[2 * n:3 * n]
        send_sems, recv_sems = refs[3 * n:]
        x, y, c = _coords()
        me, sibling = (x, y, c), (x, y, 1 - c)
        chips = [(1 - x, y), (x, 1 - y), (1 - x, 1 - y)]

        def slot(p):
            return 4 * p[0] + 2 * p[1] + p[2]

        def copy(t, k, block, to, src=None):
            dst = outs[t].at[slot(block)]
            return pltpu.make_async_remote_copy(
                src_ref=dst if src is None else src, dst_ref=dst, send_sem=send_sems.at[t, k],
                recv_sem=recv_sems.at[t, k], device_id=to, device_id_type=MESH)

        first = []
        for t in range(n):
            first.append(copy(t, 0, me, sibling, src=ins[t]))
            first += [copy(t, 1 + j, me, (*chip, c), src=ins[t]) for j, chip in enumerate(chips)]
        for cp in first:
            cp.start()
        passed = []
        for t in range(n):
            for j, chip in enumerate(chips):
                copy(t, 1 + j, (*chip, c), me).wait_recv()
                fwd = copy(t, 4 + j, (*chip, c), sibling)
                fwd.start()
                passed.append(fwd)
        for t in range(n):
            copy(t, 0, sibling, me).wait_recv()
            for j, chip in enumerate(chips):
                copy(t, 4 + j, (*chip, 1 - c), me).wait_recv()
        for cp in first + passed:
            cp.wait_send()

    hbm = pl.BlockSpec(memory_space=pl.ANY)
    return pl.pallas_call(
        body, name="allgather_weights",
        in_specs=[hbm] * (2 * n), out_specs=[hbm] * n,
        out_shape=[jax.ShapeDtypeStruct(b.shape, b.dtype) for b in bufs],
        scratch_shapes=[pltpu.SemaphoreType.DMA((n, 7)), pltpu.SemaphoreType.DMA((n, 7))],
        input_output_aliases={n + t: t for t in range(n)},
    )(*shards, *bufs)


def _slot(p):
    return 4 * p[0] + 2 * p[1] + p[2]


def _rows(ref, span):
    return ref if span is None else ref.at[pl.ds(span[0], span[1])]


ALL = "all"
LOCAL = "local"


def _rows(ref, span):
    return ref if span == ALL else ref.at[pl.ds(span[0], span[1])]


def _rider_ag(items):
    ins, out_shape, aliases, where = [], [], {}, []
    n_remote = n_local = 0
    for t, (shard, buf, snd, fwd) in enumerate(items):
        i_shard = i_buf = None
        if snd is not None:
            i_shard = len(ins)
            ins.append(shard)
        if buf is not None:
            i_buf = len(ins)
            ins.append(buf)
            aliases[i_buf] = t
            out_shape.append(jax.ShapeDtypeStruct(buf.shape, buf.dtype))
        else:
            assert fwd is None and snd is not None
            out_shape.append(jax.ShapeDtypeStruct((N_DEV,) + shard.shape, shard.dtype))
        where.append((i_shard, i_buf, n_remote, n_local))
        n_remote += (4 if snd not in (None, LOCAL) else 0) + (3 if fwd is not None else 0)
        n_local += 1 if snd is not None else 0

    def plan(rins, routs, send, recv, loc, r0, l0):
        x, y, c = _coords()
        peers = [(x, y, 1 - c), (1 - x, y, c), (x, 1 - y, c), (1 - x, 1 - y, c)]
        remote, local = [], []
        for t, (shard, buf, snd, fwd) in enumerate(items):
            i_shard, i_buf, k, l = where[t]
            k, l = r0 + k, l0 + l
            if snd is not None:
                span = ALL if snd == LOCAL else snd
                src, dst = _rows(rins[i_shard], span), _rows(routs[t].at[_slot((x, y, c))], span)
                local.append(pltpu.make_async_copy(src, dst, loc.at[l]))
                for peer in (peers if snd != LOCAL else []):
                    remote.append(pltpu.make_async_remote_copy(
                        src_ref=src, dst_ref=dst, send_sem=send.at[k], recv_sem=recv.at[k],
                        device_id=peer, device_id_type=MESH))
                    k += 1
            if fwd is not None:
                for px, py, pc in peers[1:]:
                    s = _slot((px, py, pc))
                    remote.append(pltpu.make_async_remote_copy(
                        src_ref=_rows(rins[i_buf].at[s], fwd), dst_ref=_rows(routs[t].at[s], fwd),
                        send_sem=send.at[k], recv_sem=recv.at[k], device_id=peers[0], device_id_type=MESH))
                    k += 1
        return remote, local

    return _Rider(ins, out_shape, n_remote, n_local, plan, aliases)


def _gather_buffer(shard, me):
    return lax.dynamic_update_slice(lax.empty((N_DEV,) + shard.shape, shard.dtype), shard[None], (me, 0, 0))


def _rider_ag_remote(shards, me):
    n = len(shards)

    def plan(ins, outs, send, recv, loc, r0, l0):
        x, y, c = _coords()
        remote = []
        for t in range(n):
            dst = outs[t].at[_slot((x, y, c))]
            for k, peer in enumerate([(x, y, 1 - c), (1 - x, y, c), (x, 1 - y, c), (1 - x, 1 - y, c)]):
                remote.append(pltpu.make_async_remote_copy(
                    src_ref=ins[t], dst_ref=dst, send_sem=send.at[r0 + 4 * t + k], recv_sem=recv.at[r0 + 4 * t + k],
                    device_id=peer, device_id_type=MESH))
        return remote, []

    return _Rider(shards, [jax.ShapeDtypeStruct((N_DEV,) + s.shape, s.dtype) for s in shards], 4 * n, 0, plan,
                  lands=[_gather_buffer(s, me) for s in shards])


def _rider_rs_sibling(grads):
    n = len(grads)

    def plan(ins, outs, send, recv, loc, r0, l0):
        x, y, c = _coords()
        remote = []
        for t in range(n):
            for q in range(4):
                remote.append(pltpu.make_async_remote_copy(
                    src_ref=ins[t].at[q, 1 - c], dst_ref=outs[t].at[q], send_sem=send.at[r0 + 4 * t + q],
                    recv_sem=recv.at[r0 + 4 * t + q], device_id=(x, y, 1 - c), device_id_type=MESH))
        return remote, []

    return _Rider(grads, [jax.ShapeDtypeStruct((4,) + g.shape[2:], g.dtype) for g in grads], 4 * n, 0, plan)


def _rider_rs_chips(sums, rows=None, into=None):
    n = len(sums)
    rows = rows or [ALL] * n

    def plan(ins, outs, send, recv, loc, r0, l0):
        x, y, c = _coords()
        remote = []
        for t in range(n):
            for r, (px, py) in enumerate([(1 - x, y), (x, 1 - y), (1 - x, 1 - y)]):
                remote.append(pltpu.make_async_remote_copy(
                    src_ref=_rows(ins[t].at[2 * px + py], rows[t]), dst_ref=_rows(outs[t].at[r], rows[t]),
                    send_sem=send.at[r0 + 3 * t + r], recv_sem=recv.at[r0 + 3 * t + r],
                    device_id=(px, py, c), device_id_type=MESH))
        return remote, []

    out_shape = [jax.ShapeDtypeStruct((3,) + s.shape[1:], s.dtype) for s in sums]
    if into is None:
        return _Rider(sums, out_shape, 3 * n, 0, plan)
    return _Rider(list(sums) + list(into), out_shape, 3 * n, 0, plan, aliases={n + t: t for t in range(n)})


def _rider_gather_remote(parts):
    n = len(parts)

    def plan(ins, outs, send, recv, loc, r0, l0):
        x, y, c = _coords()
        me = _slot((x, y, c))
        remote = []
        for t in range(n):
            for k in range(1, N_DEV):
                peer = (x ^ ((k >> 2) & 1), y ^ ((k >> 1) & 1), c ^ (k & 1))
                remote.append(pltpu.make_async_remote_copy(
                    src_ref=ins[t], dst_ref=outs[t].at[me], send_sem=send.at[r0 + 7 * t + k - 1],
                    recv_sem=recv.at[r0 + 7 * t + k - 1], device_id=peer, device_id_type=MESH))
        return remote, []

    return _Rider(parts, [jax.ShapeDtypeStruct((N_DEV,) + p.shape, p.dtype) for p in parts], 7 * n, 0, plan)


def _chip_sum_call(idx, grads, recvd, out_dtypes, name):
    n = len(grads)

    def body(i_ref, *refs):
        for t in range(n):
            refs[2 * n + t][0] = (refs[t][0, 0] + refs[n + t][0]).astype(out_dtypes[t])

    def chip(k, s):
        return jnp.where(k >= s[0], k + 1, k)

    in_specs = [pl.BlockSpec((1, 1) + g.shape[2:], lambda k, s: (chip(k, s), s[1], 0, 0)) for g in grads]
    in_specs += [pl.BlockSpec((1,) + r.shape[1:], lambda k, s: (chip(k, s), 0, 0)) for r in recvd]
    return pl.pallas_call(
        body, name=name,
        grid_spec=pltpu.PrefetchScalarGridSpec(
            num_scalar_prefetch=1, grid=(3,), in_specs=in_specs,
            out_specs=[pl.BlockSpec((1,) + r.shape[1:], lambda k, s: (chip(k, s), 0, 0)) for r in recvd]),
        out_shape=[jax.ShapeDtypeStruct(r.shape, dt) for r, dt in zip(recvd, out_dtypes)],
        compiler_params=_params(("arbitrary",)),
    )(idx, *grads, *recvd)


def _final_sum_call(idx, grads, recvd1, recvd2):
    n = len(grads)
    nsteps = 2

    def body(i_ref, *refs):
        for t in range(n):
            g, r1, r2, o = refs[t], refs[n + t], refs[2 * n + t], refs[3 * n + t]
            s = g[0, 0] + r1[0]
            for r in range(3):
                s = s + r2[r].astype(F32)
            o[...] = s

    def rows(a):
        r = a.shape[-2]
        return r // nsteps if (r // nsteps) % 16 == 0 else r

    def step(a):
        return (lambda i: i) if rows(a) != a.shape[-2] else (lambda i: 0)

    in_specs = [pl.BlockSpec((1, 1, rows(g), g.shape[3]), lambda i, s, st=step(g): (s[0], s[1], st(i), 0)) for g in grads]
    in_specs += [pl.BlockSpec((1, rows(r), r.shape[2]), lambda i, s, st=step(r): (s[0], st(i), 0)) for r in recvd1]
    in_specs += [pl.BlockSpec((3, rows(r), r.shape[2]), lambda i, s, st=step(r): (0, st(i), 0)) for r in recvd2]
    return pl.pallas_call(
        body, name="rs_final_sum",
        grid_spec=pltpu.PrefetchScalarGridSpec(
            num_scalar_prefetch=1, grid=(nsteps,), in_specs=in_specs,
            out_specs=[pl.BlockSpec((rows(r), r.shape[2]), lambda i, s, st=step(r): (st(i), 0)) for r in recvd2]),
        out_shape=[jax.ShapeDtypeStruct(r.shape[1:], F32) for r in recvd2],
        compiler_params=_params(("arbitrary",)),
    )(idx, *grads, *recvd1, *recvd2)


def _sum8_call(parts):
    def body(p_ref, o_ref):
        s = p_ref[0]
        for j in range(1, N_DEV):
            s = s + p_ref[j]
        o_ref[...] = s

    return pl.pallas_call(body, name="sum_small_partials",
                          out_shape=jax.ShapeDtypeStruct(parts.shape[1:], parts.dtype))(parts)


def _adamw(w, g, m, v):
    m = ADAM_B1 * m + (1.0 - ADAM_B1) * g
    v = ADAM_B2 * v + (1.0 - ADAM_B2) * (g * g)
    m_hat = m / (1.0 - ADAM_B1 ** ADAM_STEP)
    v_hat = v / (1.0 - ADAM_B2 ** ADAM_STEP)
    delta = -ADAM_LR * (m_hat / (jnp.sqrt(v_hat) + ADAM_EPS) + ADAM_WD * w)
    return delta, m, v


def _adamw_call(ws, gs, ms, vs, nsteps, name):
    n = len(ws)

    def body(*refs):
        for t in range(n):
            w, g, m, v = (refs[k * n + t][...] for k in range(4))
            d, m2, v2 = _adamw(w, g, m, v)
            refs[4 * n + t][...] = d
            refs[5 * n + t][...] = m2
            refs[6 * n + t][...] = v2

    def spec(a):
        assert a.shape[0] % nsteps == 0 and (nsteps == 1 or (a.shape[0] // nsteps) % 8 == 0), a.shape
        return pl.BlockSpec((a.shape[0] // nsteps, a.shape[1]), lambda i: (i, 0))

    specs = [spec(a) for a in ws]
    outs = pl.pallas_call(
        body, name=name, grid=(nsteps,),
        in_specs=specs * 4, out_specs=specs * 3,
        out_shape=[jax.ShapeDtypeStruct(a.shape, F32) for a in ws] * 3,
        compiler_params=_params(("arbitrary",)),
    )(*ws, *gs, *ms, *vs)
    return outs[:n], outs[n:2 * n], outs[2 * n:]


def _adamw_rs_call(idx, after, gws, r1s, r2s, ws, ms, vs, nsteps, name):
    n = len(ws)

    def body(i_ref, after_ref, *refs):
        for t in range(n):
            gw, r1, r2, w, m, v = (refs[k * n + t] for k in range(6))
            g = gw[0, 0] + r1[0]
            for r in range(3):
                g = g + r2[r].astype(F32)
            d, m2, v2 = _adamw(w[...], g, m[...], v[...])
            refs[6 * n + t][...] = g
            refs[7 * n + t][...] = d
            refs[8 * n + t][...] = m2
            refs[9 * n + t][...] = v2

    def rb(a):
        r = a.shape[0] // nsteps
        assert a.shape[0] % nsteps == 0 and r % 16 == 0, a.shape
        return r

    in_specs = [pl.BlockSpec((1, 1, rb(w), w.shape[1]), lambda i, s: (s[0], s[1], i, 0)) for w in ws]
    in_specs += [pl.BlockSpec((1, rb(w), w.shape[1]), lambda i, s: (s[0], i, 0)) for w in ws]
    in_specs += [pl.BlockSpec((3, rb(w), w.shape[1]), lambda i, s: (0, i, 0)) for w in ws]
    plain = [pl.BlockSpec((rb(w), w.shape[1]), lambda i, s: (i, 0)) for w in ws]
    outs = pl.pallas_call(
        body, name=name,
        grid_spec=pltpu.PrefetchScalarGridSpec(
            num_scalar_prefetch=1, grid=(nsteps,),
            in_specs=[pl.BlockSpec(memory_space=pl.ANY)] + in_specs + plain * 3, out_specs=plain * 4),
        out_shape=[jax.ShapeDtypeStruct(w.shape, F32) for w in ws] * 4,
        compiler_params=_params(("arbitrary",)),
    )(idx, after, *gws, *r1s, *r2s, *ws, *ms, *vs)
    return outs[:n], outs[n:2 * n], outs[2 * n:3 * n], outs[3 * n:]


def _rows128(a, pad_rows):
    flat = a.reshape(-1).astype(F32)
    flat = jnp.pad(flat, (0, pad_rows * LANES - flat.shape[0]))
    return flat.reshape(pad_rows, LANES)


_SMALL_A = (("w_pool", 512), ("pool_scale", 8), ("attn_sinks", 8), ("g_mix_post", 8), ("g_mlp_pre", 8),
            ("g_mlp_post", 8), ("loss", 8), ("b_in_gates", 16))
_SMALL_A_ROWS = 640
_SMALL_B = (("g_mix_pre", 8), ("b_in_head", 16))


def _pack(parts, layout, total_rows):
    rows = [_rows128(parts[k], r) for k, r in layout]
    pad = total_rows - sum(r for _, r in layout)
    if pad:
        rows.append(jnp.zeros((pad, LANES), F32))
    return jnp.concatenate(rows, axis=0)


def _unpack(buf, layout, sizes):
    out, off = {}, 0
    for k, r in layout:
        out[k] = buf[off:off + r].reshape(-1)[:sizes[k]]
        off += r
    return out


def kernel(x, g_mix_pre, w_in, b_in, w_pool, pool_scale, attn_sinks, w_branch_pool, w_branch_attn, w_out, g_mix_post, g_mlp_pre, w_up, w_down, g_mlp_post, loss_target, m_g_mix_pre, m_w_in, m_b_in, m_w_pool, m_pool_scale, m_attn_sinks, m_w_branch_pool, m_w_branch_attn, m_w_out, m_g_mix_post, m_g_mlp_pre, m_w_up, m_w_down, m_g_mlp_post, v_g_mix_pre, v_w_in, v_b_in, v_w_pool, v_pool_scale, v_attn_sinks, v_w_branch_pool, v_w_branch_attn, v_w_out, v_g_mix_post, v_g_mlp_pre, v_w_up, v_w_down, v_g_mlp_post):
    B, S, _ = x.shape
    T = B * S
    xt = x.reshape(T, D_MODEL)
    tgt = loss_target.reshape(T, D_MODEL)
    cx, cy, cc = _coords()

    cidx = jnp.stack([2 * cx + cy, cc]).astype(jnp.int32)
    by_chip = lambda gr: gr.reshape((4, 2) + gr.shape[1:])
    bf = lambda w: w[0].astype(MXU_DTYPE)

    me = _slot((cx, cy, cc))
    win_l = w_in[0].T.astype(MXU_DTYPE)
    (win_s,) = _allgather_call([win_l], [_gather_buffer(win_l, me)])
    win_t = win_s.reshape(IN_WIDTH, D_MODEL)
    wpool_b = bf(w_pool)
    rc, rsa, rsb = _rot_tables(S)

    wbp_l, wba_l, wout_l, wup_l, wdown_l = bf(w_branch_pool), bf(w_branch_attn), bf(w_out), bf(w_up), bf(w_down)
    (c_br, c_up, c_dn), tok = _copies_start(
        [_rider_ag_remote([wbp_l, wba_l, wout_l], me), _rider_ag_remote([wup_l], me), _rider_ag_remote([wdown_l], me)],
        "allgather_start", after=win_s)
    (h, u, q, k4, v4, g), _ = _inproj_call(xt, g_mix_pre, win_t, b_in, rc, rsa, rsb, S, rider=_after(tok))
    yp = _pool_call(u, wpool_b, pool_scale, S)
    wbp_1, wba_1, wout_1 = _copies_wait([c_br], yp, "allgather_wait_branch")
    (ya,), (wbp_s, wba_s, wout_s) = _attn_call(
        attn_sinks, q, k4, v4, S,
        rider=_rider_ag([(None, wbp_1, None, ALL), (None, wba_1, None, ALL), (None, wout_1, None, ALL)]))
    wout_f = wout_s.reshape(D_MODEL, D_MODEL)
    (wup_1,) = _copies_wait([c_up], ya, "allgather_wait_up")
    (mix, x1, h2, h2_t), (wup_s,) = _mix_fwd_call(
        yp, ya, g, xt, wbp_s, wba_s, wout_f, g_mix_post, g_mlp_pre, rider=_rider_ag([(None, wup_1, None, ALL)]))
    act, act_t = _mlp_up_call(h2, wup_s)
    (wdown_1,) = _copies_wait([c_dn], act, "allgather_wait_down")
    (wdown_s,) = _comm_call(_rider_ag([(None, wdown_1, None, ALL)]), "allgather_pass_down")

    da, dff, dx1, dg3, dg4, lossvec = _mlp_call(x1, act, tgt, wup_s, wdown_s, g_mlp_pre, g_mlp_post)
    gw_down = by_chip(_wgrad_rows_call(act_t, dff, "wgrad_down")[0])
    (gw_up,), (r1_down,) = _wgrad_cols_call(h2_t, da, "wgrad_up", rider=_rider_rs_sibling([gw_down]))
    gw_up = by_chip(gw_up)
    (s_down,) = _chip_sum_call(cidx, [gw_down], [r1_down], [MXU_DTYPE], "rs_chip_sum_down")
    (c_down,), tok = _copies_start([_rider_rs_chips([s_down])], "rs_chips_start_down")
    (dyp, do, dgates, dg2, dbg, gw_out, gw_bp, gw_ba), (r1_up,) = _mix_bwd_call(
        dx1, mix, yp, ya, g, wbp_s, wba_s, wout_f, g_mix_post, rider=_after(tok, _rider_rs_sibling([gw_up])))
    gw_out = by_chip(gw_out.reshape(N_DEV, D_MODEL // N_DEV, D_MODEL))
    gw_bp, gw_ba = by_chip(gw_bp), by_chip(gw_ba)
    (s_up,) = _chip_sum_call(cidx, [gw_up], [r1_up], [MXU_DTYPE], "rs_chip_sum_up")
    (c_up,), tok = _copies_start([_rider_rs_chips([s_up])], "rs_chips_start_up")
    (dq, dk, dv, dsink), (r1_out, r1_bp, r1_ba) = _attn_bwd_call(
        attn_sinks, q, k4, v4, do, rc, rsa, rsb, S, rider=_after(tok, _rider_rs_sibling([gw_out, gw_bp, gw_ba])))
    s_obb = _chip_sum_call(cidx, [gw_out, gw_bp, gw_ba], [r1_out, r1_bp, r1_ba], [MXU_DTYPE] * 3, "rs_chip_sum_branch")
    (c_obb,), tok = _copies_start([_rider_rs_chips(s_obb)], "rs_chips_start_branch")
    (du, dwp, dps), _ = _pool_bwd_call(u, dyp, wpool_b, pool_scale, S, rider=_after(tok))
    (gw_in,) = _wgrad_in_call(du, dq, dk, dv, dgates, h)
    gw_in = by_chip(gw_in)

    small_a = {"w_pool": dwp, "pool_scale": dps,
               "attn_sinks": jnp.sum(dsink.reshape(B, 8, LANES)[:, 0, :N_Q_HEADS], axis=0), "g_mix_post": dg2,
               "g_mlp_pre": dg3, "g_mlp_post": dg4, "loss": lossvec, "b_in_gates": dbg}
    gw_sa = by_chip(_pack(small_a, _SMALL_A, _SMALL_A_ROWS).reshape(N_DEV, _SMALL_A_ROWS // N_DEV, LANES))
    r1_in, r1_sa = _comm_call(_rider_rs_sibling([gw_in, gw_sa]), "rs_sibling_in")
    s_in, s_sa = _chip_sum_call(cidx, [gw_in, gw_sa], [r1_in, r1_sa], [MXU_DTYPE, F32], "rs_chip_sum_in")
    (c_in,), tok = _copies_start([_rider_rs_chips([s_in, s_sa])], "rs_chips_start_in")
    (gx, dg1, dba_in), _ = _inproj_bwd_call(du, dq, dk, dv, dgates, dx1, xt, win_t, g_mix_pre, rider=_after(tok))
    r2_down, r2_up, r2_out, r2_bp, r2_ba, r2_in, r2_sa = _copies_wait([c_down, c_up, c_obb, c_in], dg1, "rs_chips_wait")

    (g_sa,) = _final_sum_call(cidx, [gw_sa], [r1_sa], [r2_sa])
    part_b = _pack({"g_mix_pre": dg1, "b_in_head": dba_in}, _SMALL_B, sum(r for _, r in _SMALL_B))
    (c_small,), tok = _copies_start([_rider_gather_remote([g_sa, part_b])], "allgather_small_start")

    in_t = _adamw_rs_call(cidx, tok, [gw_in], [r1_in], [r2_in], [w_in[0].T], [m_w_in[0].T], [v_w_in[0].T], 2,
                          "adamw_w_in")
    rest = _adamw_rs_call(
        cidx, tok, [gw_bp, gw_ba, gw_out, gw_up, gw_down], [r1_bp, r1_ba, r1_out, r1_up, r1_down],
        [r2_bp, r2_ba, r2_out, r2_up, r2_down], [w_branch_pool[0], w_branch_attn[0], w_out[0], w_up[0], w_down[0]],
        [m_w_branch_pool[0], m_w_branch_attn[0], m_w_out[0], m_w_up[0], m_w_down[0]],
        [v_w_branch_pool[0], v_w_branch_attn[0], v_w_out[0], v_w_up[0], v_w_down[0]], N_DEV, "adamw_shards")
    big_g, big_d, big_m2, big_v2 = ([a[0].T] + list(b) for a, b in zip(in_t, rest))

    sa_all, sb_all = _copies_wait([c_small], rest[0][0], "allgather_small_wait")
    sa_all = lax.dynamic_update_slice(sa_all, g_sa[None], (me, 0, 0))
    sb_sum = _sum8_call(lax.dynamic_update_slice(sb_all, part_b[None], (me, 0, 0)))

    names = ["g_mix_pre", "b_in", "w_pool", "pool_scale", "attn_sinks", "g_mix_post", "g_mlp_pre", "g_mlp_post"]
    sm_w = dict(g_mix_pre=g_mix_pre, b_in=b_in, w_pool=w_pool, pool_scale=pool_scale, attn_sinks=attn_sinks,
                g_mix_post=g_mix_post, g_mlp_pre=g_mlp_pre, g_mlp_post=g_mlp_post)
    sm_m = dict(g_mix_pre=m_g_mix_pre, b_in=m_b_in, w_pool=m_w_pool, pool_scale=m_pool_scale, attn_sinks=m_attn_sinks,
                g_mix_post=m_g_mix_post, g_mlp_pre=m_g_mlp_pre, g_mlp_post=m_g_mlp_post)
    sm_v = dict(g_mix_pre=v_g_mix_pre, b_in=v_b_in, w_pool=v_w_pool, pool_scale=v_pool_scale, attn_sinks=v_attn_sinks,
                g_mix_post=v_g_mix_post, g_mlp_pre=v_g_mlp_pre, g_mlp_post=v_g_mlp_post)
    sizes = {k: sm_w[k].size for k in names}
    sizes.update(loss=D_MODEL, b_in_gates=GATE_WIDTH, b_in_head=C_G)
    sm_g = _unpack(sa_all.reshape(_SMALL_A_ROWS, LANES), _SMALL_A, sizes)
    sm_g.update(_unpack(sb_sum, _SMALL_B, sizes))
    sm_g["b_in"] = jnp.concatenate([sm_g["b_in_head"], sm_g["b_in_gates"]])
    loss = (0.5 / D_MODEL) * jnp.sum(sm_g["loss"])
    two_d = lambda a: a.reshape(-1, a.shape[-1])
    sd_, sm2_, sv2_ = _adamw_call([two_d(sm_w[k]) for k in names], [two_d(sm_g[k].reshape(sm_w[k].shape)) for k in names],
                                  [two_d(sm_m[k]) for k in names], [two_d(sm_v[k]) for k in names], 1, "adamw_small")
    like = lambda vals: {k: a.reshape(sm_w[k].shape) for k, a in zip(names, vals)}
    sm_d, sm_m2, sm_v2 = like(sd_), like(sm2_), like(sv2_)
    sm_gr = {k: sm_g[k].reshape(sm_w[k].shape) for k in names}

    order = ["g_mix_pre", "w_in", "b_in", "w_pool", "pool_scale", "attn_sinks", "w_branch_pool", "w_branch_attn",
             "w_out", "g_mix_post", "g_mlp_pre", "w_up", "w_down", "g_mlp_post"]
    big_names = ["w_in", "w_branch_pool", "w_branch_attn", "w_out", "w_up", "w_down"]
    lead = lambda a: a[None]
    tables = []
    for small_t, big_t in ((sm_gr, big_g), (sm_d, big_d), (sm_m2, big_m2), (sm_v2, big_v2)):
        bt = dict(zip(big_names, big_t))
        tables.append([lead(bt[k]) if k in bt else small_t[k] for k in order])
    return (loss, gx.reshape(B, S, D_MODEL), *tables[0], *tables[1], *tables[2], *tables[3])
```

```python
import jax
import jax.numpy as jnp
from jax import lax
from jax.experimental import pallas as pl
from jax.experimental.pallas import tpu as pltpu

F32 = jnp.float32
MXU_DTYPE = jnp.bfloat16
MESH = pl.DeviceIdType.MESH

D_MODEL = 1024
POOL_WINDOWS = (2, 4, 8, 16)
POOL_WIDTH = 512
POOL_GC = 128
HEAD_DIM = 64
N_Q_HEADS = 8
N_KV_HEADS = 2
GROUP = 4
ATTN_WIDTH = 512
KV_WIDTH = 128
BLOCK = 128
GATE_WIDTH = 2048
IN_WIDTH = 3328
D_FF = 4096
EPS = 1e-6
NEG_INF = -1e30
ROPE_THETA = 500000.0
ROT_DIM = 16
SCALE = HEAD_DIM ** -0.5
C_Q, C_K, C_V, C_G = 512, 1024, 1152, 1280

ADAM_LR = 0.001
ADAM_B1 = 0.9
ADAM_B2 = 0.999
ADAM_EPS = 1e-08
ADAM_WD = 0.01
ADAM_STEP = 10

N_DEV = 8
LANES = 128
VMEM_LIMIT = 56 * 1024 * 1024

NN = (((1,), (0,)), ((), ()))
NT = (((1,), (1,)), ((), ()))
TN = (((0,), (0,)), ((), ()))


def _dot(a, b, dims):
    return lax.dot_general(a, b, dims, preferred_element_type=F32)


def _params(sem=None):
    return pltpu.CompilerParams(dimension_semantics=sem, vmem_limit_bytes=VMEM_LIMIT)


def _tile(n, pref):
    t = min(n, pref)
    assert n % t == 0, (n, t)
    return t


class _Rider:
    def __init__(self, ins, out_shape, n_remote, n_local, plan, aliases=None, lands=None):
        self.ins, self.out_shape, self.n_remote, self.n_local = list(ins), list(out_shape), n_remote, n_local
        self.plan, self.aliases = plan, dict(aliases or {})
        self.lands = lands


def _after(token, rider=None):
    r = rider or _Rider([], [], 0, 0, lambda ins, outs, send, recv, loc, r0, l0: ([], []))
    return _Rider(r.ins + [token], r.out_shape, r.n_remote, r.n_local, r.plan, r.aliases)


def _launch(body, args, *, name, grid, in_specs, out_specs, out_shape, scratch_shapes=(), sem=None, rider=None):
    if rider is None:
        return pl.pallas_call(body, name=name, grid=grid, in_specs=in_specs, out_specs=out_specs, out_shape=out_shape,
                              scratch_shapes=list(scratch_shapes), compiler_params=_params(sem))(*args)
    n_in, n_out, n_scr = len(args), len(out_shape), len(scratch_shapes)
    r_in, r_out = len(rider.ins), len(rider.out_shape)
    copies = rider.n_remote + rider.n_local > 0

    def wrapped(*refs):
        ins, rins = refs[:n_in], refs[n_in:n_in + r_in]
        o0 = n_in + r_in
        outs, routs = refs[o0:o0 + n_out], refs[o0 + n_out:o0 + n_out + r_out]
        s0 = o0 + n_out + r_out
        scr = refs[s0:s0 + n_scr]
        if not copies:
            return body(*ins, *outs, *scr)
        send, recv, loc = refs[s0 + n_scr:]
        first, last = None, None
        for d in range(len(grid)):
            f, l = pl.program_id(d) == 0, pl.program_id(d) == pl.num_programs(d) - 1
            first = f if first is None else first & f
            last = l if last is None else last & l

        def start():
            remote, local = rider.plan(rins, routs, send, recv, loc, 0, 0)
            for cp in local + remote:
                cp.start()

        def finish():
            remote, local = rider.plan(rins, routs, send, recv, loc, 0, 0)
            for cp in remote + local:
                cp.wait()

        if first is None:
            start()
            body(*ins, *outs, *scr)
            finish()
        else:
            pl.when(first)(start)
            body(*ins, *outs, *scr)
            pl.when(last)(finish)

    hbm = pl.BlockSpec(memory_space=pl.ANY)
    dma = pltpu.SemaphoreType.DMA
    res = pl.pallas_call(
        wrapped, name=name, grid=grid, in_specs=list(in_specs) + [hbm] * r_in,
        out_specs=list(out_specs) + [hbm] * r_out, out_shape=list(out_shape) + rider.out_shape,
        scratch_shapes=list(scratch_shapes) + (
            [dma((max(rider.n_remote, 1),)), dma((max(rider.n_remote, 1),)), dma((max(rider.n_local, 1),))] if copies else []),
        input_output_aliases={n_in + i: n_out + o for i, o in rider.aliases.items()},
        compiler_params=_params(sem),
    )(*args, *rider.ins)
    return list(res[:n_out]), list(res[n_out:])


def _comm_call(rider, name):
    return _launch(lambda: None, [], name=name, grid=(), in_specs=[], out_specs=[], out_shape=[], rider=rider)[1]


_HBM = pl.BlockSpec(memory_space=pltpu.HBM)
_SEM = pl.BlockSpec(memory_space=pltpu.SEMAPHORE)
_EFFECT = pltpu.SideEffectType.DATAFLOW_SIDE_EFFECTING


def _copies_start(riders, name, after=None):
    assert all(r.n_local == 0 and not r.aliases for r in riders)
    extra = [] if after is None else [after]
    sizes = [(len(r.ins), len(r.out_shape)) for r in riders]
    bufs = []
    for r in riders:
        lands = r.lands or [lax.empty(s.shape, s.dtype) for s in r.out_shape]
        bufs += [pltpu.with_memory_space_constraint(a, pltpu.HBM) for a in list(r.ins) + list(lands)]
    nb, ng, ne = len(bufs), len(riders), len(extra)

    def body(*refs):
        sems, token, at = refs[2 * nb + ne:2 * nb + ne + 2 * ng], refs[-1], 0
        for g, (r, (ni, no)) in enumerate(zip(riders, sizes)):
            remote, _ = r.plan(refs[at:at + ni], refs[at + ni:at + ni + no], sems[2 * g], sems[2 * g + 1], None, 0, 0)
            for cp in remote:
                cp.start()
            at += ni + no
        token[...] = jnp.zeros_like(token)

    res = pl.pallas_call(
        body, name=name, in_specs=[_HBM] * nb + [pl.BlockSpec(memory_space=pl.ANY)] * ne,
        out_specs=[_HBM] * nb + [_SEM] * (2 * ng) + [pl.BlockSpec(memory_space=pltpu.VMEM)],
        out_shape=[pltpu.HBM(a.shape, a.dtype) for a in bufs]
        + [pltpu.SemaphoreType.DMA((r.n_remote,)) for r in riders for _ in range(2)]
        + [jax.ShapeDtypeStruct((8, LANES), F32)],
        input_output_aliases={i: i for i in range(nb)},
        compiler_params=pltpu.CompilerParams(has_side_effects=_EFFECT),
    )(*bufs, *extra)
    handles, at = [], 0
    for g, (r, (ni, no)) in enumerate(zip(riders, sizes)):
        handles.append((r, list(res[at:at + ni + no]), res[nb + 2 * g], res[nb + 2 * g + 1]))
        at += ni + no
    return handles, res[-1]


def _copies_wait(handles, after, name):
    bufs = [b for _, bs, _, _ in handles for b in bs]
    sems = [s for _, _, send, recv in handles for s in (send, recv)]
    nb, ng = len(bufs), len(handles)

    def body(*refs):
        at = 0
        for g, (rider, bs, _, _) in enumerate(handles):
            ni = len(rider.ins)
            remote, _ = rider.plan(refs[at:at + ni], refs[at + ni:at + len(bs)], refs[nb + 2 * g], refs[nb + 2 * g + 1],
                                   None, 0, 0)
            for cp in remote:
                cp.wait_send()
                cp.wait_recv()
            at += len(bs)

    res = pl.pallas_call(
        body, name=name, in_specs=[_HBM] * nb + [_SEM] * (2 * ng) + [pl.BlockSpec(memory_space=pl.ANY)],
        out_specs=[_HBM] * nb, out_shape=[pltpu.HBM(a.shape, a.dtype) for a in bufs],
        input_output_aliases={i: i for i in range(nb)},
        compiler_params=pltpu.CompilerParams(has_side_effects=_EFFECT),
    )(*bufs, *sems, after)
    lands, at = [], 0
    for rider, bs, _, _ in handles:
        lands += list(res[at + len(rider.ins):at + len(bs)])
        at += len(bs)
    return lands


def _rms_r(x):
    return lax.rsqrt(jnp.mean(x * x, axis=-1, keepdims=True) + EPS)


def _rms_bwd(dn, x, r, g):
    xh = x * r
    dxh = dn * g
    dx = r * (dxh - xh * jnp.mean(dxh * xh, axis=-1, keepdims=True))
    return dx, dn * xh


def _rot(t, c, sa, sb):
    outs = []
    for j in range(t.shape[1] // LANES):
        tj = t[:, LANES * j:LANES * (j + 1)]
        outs.append(tj * c + pltpu.roll(tj, LANES - 8, 1) * sa + pltpu.roll(tj, 8, 1) * sb)
    return outs[0] if len(outs) == 1 else jnp.concatenate(outs, axis=1)


def _rot_tables(S):
    pos = jnp.arange(S, dtype=F32)
    inv_freq = ROPE_THETA ** (-jnp.arange(0, ROT_DIM, 2, dtype=F32) / ROT_DIM)
    ang = pos[:, None] * inv_freq[None, :]
    cos, sin = jnp.cos(ang), jnp.sin(ang)
    one = jnp.ones((S, HEAD_DIM - ROT_DIM), F32)
    zero = jnp.zeros((S, HEAD_DIM - ROT_DIM), F32)
    z8 = jnp.zeros((S, 8), F32)
    c = jnp.concatenate([cos, cos, one], axis=1)
    sa = jnp.concatenate([-sin, z8, zero], axis=1)
    sb = jnp.concatenate([z8, sin, zero], axis=1)
    rep = LANES // HEAD_DIM
    return jnp.tile(c, (1, rep)), jnp.tile(sa, (1, rep)), jnp.tile(sb, (1, rep))


def _lane_tile4(k):
    lane = lax.broadcasted_iota(jnp.int32, k.shape, 1)
    rk = pltpu.roll(k, HEAD_DIM, 1)
    x0 = jnp.where(lane < HEAD_DIM, k, rk)
    x1 = jnp.where(lane < HEAD_DIM, rk, k)
    return jnp.concatenate([x0, x0, x1, x1], axis=1)


def _fold_heads(acc):
    zs = []
    for hk in range(N_KV_HEADS):
        a = acc[:, 256 * hk:256 * hk + LANES] + acc[:, 256 * hk + LANES:256 * (hk + 1)]
        zs.append(a + pltpu.roll(a, HEAD_DIM, 1))
    lane = lax.broadcasted_iota(jnp.int32, zs[0].shape, 1)
    return jnp.where(lane < HEAD_DIM, zs[0], zs[1])


def _inproj_call(x, g1, win_t, b_in, rc, rsa, rsb, S, rider=None):
    T = x.shape[0]
    tm = _tile(S, 512)
    nst = S // tm

    def body(x_ref, g1_ref, w_ref, b_ref, c_ref, sa_ref, sb_ref,
             h_ref, u_ref, q_ref, k4_ref, v4_ref, g_ref):
        xv = x_ref[...]
        hb = ((xv * _rms_r(xv)) * g1_ref[...]).astype(MXU_DTYPE)
        h_ref[...] = hb

        def proj(lo, hi):
            return _dot(hb, w_ref[lo:hi, :], NT) + b_ref[:, lo:hi]

        c, sa, sb = c_ref[...], sa_ref[...], sb_ref[...]
        u_ref[...] = proj(0, C_Q)
        q_ref[...] = (_rot(proj(C_Q, C_K), c, sa, sb) * SCALE).astype(MXU_DTYPE)
        kv = proj(C_K, C_G)
        k4_ref[...] = _lane_tile4(_rot(kv[:, :KV_WIDTH], c, sa, sb)).astype(MXU_DTYPE)
        v4_ref[...] = _lane_tile4(kv[:, KV_WIDTH:]).astype(MXU_DTYPE)
        g_ref[...] = jax.nn.sigmoid(proj(C_G, IN_WIDTH)).astype(MXU_DTYPE)

    tok = lambda w: pl.BlockSpec((tm, w), lambda i: (i, 0))
    full = lambda a: pl.BlockSpec(a.shape, lambda i: (0,) * a.ndim)
    tab = pl.BlockSpec((tm, LANES), lambda i: (i % nst, 0))
    return _launch(
        body, [x, g1, win_t, b_in, rc, rsa, rsb], name="inproj_fwd", grid=(T // tm,),
        in_specs=[tok(D_MODEL), full(g1), full(win_t), full(b_in), tab, tab, tab],
        out_specs=[tok(D_MODEL), tok(POOL_WIDTH), tok(ATTN_WIDTH), tok(512), tok(512), tok(GATE_WIDTH)],
        out_shape=[jax.ShapeDtypeStruct((T, D_MODEL), MXU_DTYPE), jax.ShapeDtypeStruct((T, POOL_WIDTH), F32),
                   jax.ShapeDtypeStruct((T, ATTN_WIDTH), MXU_DTYPE), jax.ShapeDtypeStruct((T, 512), MXU_DTYPE),
                   jax.ShapeDtypeStruct((T, 512), MXU_DTYPE), jax.ShapeDtypeStruct((T, GATE_WIDTH), MXU_DTYPE)],
        sem=("arbitrary",), rider=rider)


def _shift_rows(a, k, rows):
    n = a.shape[0]
    if k > 0:
        return jnp.where(rows >= k, pltpu.roll(a, k, 0), 0.0)
    return jnp.where(rows < n + k, pltpu.roll(a, n + k, 0), 0.0)


def _win_sum(a, w, rows, sign):
    s, k = a, 1
    while k < w:
        s = s + _shift_rows(s, sign * k, rows)
        k *= 2
    return s


def _pool_diff(ug, w, rows):
    inv = 1.0 / jnp.minimum(rows + 1, w).astype(F32)
    return _win_sum(ug, w, rows, 1) * inv - ug, inv


def _pool_call(u, w_pool, pool_scale, S):
    T = u.shape[0]

    def body(u_ref, w_ref, ps_ref, y_ref):
        rows = lax.broadcasted_iota(jnp.int32, (S, POOL_GC), 0)
        for gi, w in enumerate(POOL_WINDOWS):
            sl = slice(POOL_GC * gi, POOL_GC * (gi + 1))
            diff, _ = _pool_diff(u_ref[:, sl], w, rows)
            mixed = _dot(diff.astype(MXU_DTYPE), w_ref[gi], NN)
            y_ref[:, sl] = (mixed * ps_ref[:, sl]).astype(MXU_DTYPE)

    seq = pl.BlockSpec((S, POOL_WIDTH), lambda b: (b, 0))
    return pl.pallas_call(
        body, name="pool_fwd", grid=(T // S,),
        in_specs=[seq, pl.BlockSpec(w_pool.shape, lambda b: (0, 0, 0)), pl.BlockSpec(pool_scale.shape, lambda b: (0, 0))],
        out_specs=seq, out_shape=jax.ShapeDtypeStruct((T, POOL_WIDTH), MXU_DTYPE),
        compiler_params=_params(("arbitrary",)),
    )(u, w_pool, pool_scale)


def _pool_bwd_call(u, dyp, w_pool, pool_scale, S, rider=None):
    T = u.shape[0]

    def body(u_ref, dy_ref, w_ref, ps_ref, du_ref, dw_ref, dps_ref):
        @pl.when(pl.program_id(0) == 0)
        def _():
            dw_ref[...] = jnp.zeros_like(dw_ref)
            dps_ref[...] = jnp.zeros_like(dps_ref)

        rows = lax.broadcasted_iota(jnp.int32, (S, POOL_GC), 0)
        for gi, w in enumerate(POOL_WINDOWS):
            sl = slice(POOL_GC * gi, POOL_GC * (gi + 1))
            diff, inv = _pool_diff(u_ref[:, sl], w, rows)
            diffb = diff.astype(MXU_DTYPE)
            wg = w_ref[gi]
            mixed = _dot(diffb, wg, NN)
            dy = dy_ref[:, sl]
            dps_ref[:, sl] += jnp.sum(dy * mixed, axis=0, keepdims=True)
            dmb = (dy * ps_ref[:, sl]).astype(MXU_DTYPE)
            dw_ref[gi] += _dot(diffb, dmb, TN)
            ddiff = _dot(dmb, wg, NT)
            du_ref[:, sl] = (_win_sum(ddiff * inv, w, rows, -1) - ddiff).astype(MXU_DTYPE)

    seq = pl.BlockSpec((S, POOL_WIDTH), lambda b: (b, 0))
    return _launch(
        body, [u, dyp, w_pool, pool_scale], name="pool_bwd", grid=(T // S,),
        in_specs=[seq, seq, pl.BlockSpec(w_pool.shape, lambda b: (0, 0, 0)), pl.BlockSpec(pool_scale.shape, lambda b: (0, 0))],
        out_specs=[seq, pl.BlockSpec(w_pool.shape, lambda b: (0, 0, 0)), pl.BlockSpec(pool_scale.shape, lambda b: (0, 0))],
        out_shape=[jax.ShapeDtypeStruct((T, POOL_WIDTH), MXU_DTYPE), jax.ShapeDtypeStruct(w_pool.shape, F32),
                   jax.ShapeDtypeStruct(pool_scale.shape, F32)],
        sem=("arbitrary",), rider=rider)


def _attn_consts():
    lane_g = lax.broadcasted_iota(jnp.int32, (BLOCK, 256), 1) >> 6
    rgrp = lax.broadcasted_iota(jnp.int32, (GROUP * BLOCK, 1), 0) >> 7
    rel = lax.broadcasted_iota(jnp.int32, (BLOCK, 256), 0) - lax.broadcasted_iota(jnp.int32, (BLOCK, 256), 1)

    def bias(off):
        ok = (rel + off >= 0) & (rel + off < BLOCK)
        return jnp.concatenate([jnp.where(ok, 0.0, NEG_INF)] * GROUP, axis=0)

    return lane_g, rgrp, bias(0), bias(BLOCK)


def _sink_rows(sink_ref, hk, rgrp):
    sv = jnp.zeros(rgrp.shape, F32)
    for g in range(GROUP):
        sv = jnp.where(rgrp == g, sink_ref[0, GROUP * hk + g], sv)
    return sv


def _stack_heads(xb, lane_g):
    return jnp.concatenate([jnp.where(lane_g == g, xb, jnp.zeros_like(xb)) for g in range(GROUP)], axis=0)


def _unstack_heads(xs, lane_g):
    out = jnp.where(lane_g == 0, xs[0:BLOCK], 0.0)
    for g in range(1, GROUP):
        out = out + jnp.where(lane_g == g, xs[BLOCK * g:BLOCK * (g + 1)], 0.0)
    return out


def _attn_probs(qs, kb, bias, sv):
    s = _dot(qs, kb, NT) + bias
    m = jnp.maximum(jnp.max(s, axis=1, keepdims=True), sv)
    e = jnp.exp(s - m)
    es = jnp.exp(sv - m)
    inv_l = 1.0 / (jnp.sum(e, axis=1, keepdims=True) + es)
    return e * inv_l, es * inv_l


def _attn_blocks(nb, blk, carry, per=1):
    carry = blk(0, 0, True, carry)
    per = per if (nb - 1) % per == 0 else 1

    def step(i, c):
        for k in range(per):
            n = 1 + per * i + k
            c = blk(pl.multiple_of(n * BLOCK, BLOCK), pl.multiple_of((n - 1) * BLOCK, BLOCK), False, c)
        return c

    return lax.fori_loop(0, (nb - 1) // per, step, carry)


def _attn_call(sinks, q, k4, v4, S, rider=None):
    T = q.shape[0]
    nb = S // BLOCK

    def body(sink_ref, q_ref, k_ref, v_ref, o_ref):
        lane_g, rgrp, bias_first, bias_later = _attn_consts()
        svs = [_sink_rows(sink_ref, hk, rgrp) for hk in range(N_KV_HEADS)]

        def blk(q0, k0, first, carry):
            for hk in range(N_KV_HEADS):
                cs = slice(256 * hk, 256 * (hk + 1))
                qs = _stack_heads(q_ref[pl.ds(q0, BLOCK), cs], lane_g)
                p, _ = _attn_probs(qs, k_ref[pl.ds(k0, 2 * BLOCK), cs], bias_first if first else bias_later, svs[hk])
                o = _dot(p.astype(MXU_DTYPE), v_ref[pl.ds(k0, 2 * BLOCK), cs], NN)
                o_ref[pl.ds(q0, BLOCK), cs] = _unstack_heads(o, lane_g).astype(MXU_DTYPE)
            return carry

        _attn_blocks(nb, blk, 0, per=3)

    seq = pl.BlockSpec((S, ATTN_WIDTH), lambda b: (b, 0))
    return _launch(
        body, [sinks, q, k4, v4], name="attn_fwd", grid=(T // S,),
        in_specs=[pl.BlockSpec(memory_space=pltpu.SMEM), seq, seq, seq],
        out_specs=[seq], out_shape=[jax.ShapeDtypeStruct((T, ATTN_WIDTH), MXU_DTYPE)],
        sem=("arbitrary",), rider=rider)


def _attn_bwd_call(sinks, q, k4, v4, do, rc, rsa, rsb, S, rider=None):
    T = q.shape[0]
    nb = S // BLOCK

    def body(sink_ref, q_ref, k_ref, v_ref, do_ref, c_ref, sa_ref, sb_ref,
             dq_ref, dk_ref, dv_ref, ds_ref, dk_acc, dv_acc):
        lane_g, rgrp, bias_first, bias_later = _attn_consts()
        svs = [_sink_rows(sink_ref, hk, rgrp) for hk in range(N_KV_HEADS)]
        lane1 = lax.broadcasted_iota(jnp.int32, (1, LANES), 1)
        dk_acc[...] = jnp.zeros_like(dk_acc)
        dv_acc[...] = jnp.zeros_like(dv_acc)

        def blk(q0, k0, first, dsink):
            rows = pl.ds(q0, BLOCK)
            c, sa, sb = c_ref[rows, :], sa_ref[rows, :], sb_ref[rows, :]
            for hk in range(N_KV_HEADS):
                cs = slice(256 * hk, 256 * (hk + 1))
                qs = _stack_heads(q_ref[rows, cs], lane_g)
                dos = _stack_heads(do_ref[rows, cs], lane_g)
                kb = k_ref[pl.ds(k0, 2 * BLOCK), cs]
                vb = v_ref[pl.ds(k0, 2 * BLOCK), cs]
                p, ps = _attn_probs(qs, kb, bias_first if first else bias_later, svs[hk])
                dp = _dot(dos, vb, NT)
                delta = jnp.sum(p * dp, axis=1, keepdims=True)
                dsb = (p * (dp - delta)).astype(MXU_DTYPE)
                dqb = _unstack_heads(_dot(dsb, kb, NN), lane_g) * SCALE
                dq_ref[rows, cs] = _rot(dqb, c, -sa, -sb).astype(MXU_DTYPE)
                dk_acc[pl.ds(k0, 2 * BLOCK), cs] += _dot(dsb, qs, TN)
                dv_acc[pl.ds(k0, 2 * BLOCK), cs] += _dot(p.astype(MXU_DTYPE), dos, TN)
                psd = ps * delta
                for g in range(GROUP):
                    val = -jnp.sum(psd[BLOCK * g:BLOCK * (g + 1)], axis=0, keepdims=True)
                    dsink = dsink + jnp.where(lane1 == GROUP * hk + g, val, 0.0)
            return dsink

        dsink = _attn_blocks(nb, blk, jnp.zeros((1, LANES), F32))
        dk_ref[...] = _rot(_fold_heads(dk_acc[...]), c_ref[...], -sa_ref[...], -sb_ref[...]).astype(MXU_DTYPE)
        dv_ref[...] = _fold_heads(dv_acc[...]).astype(MXU_DTYPE)
        ds_ref[...] = jnp.broadcast_to(dsink, ds_ref.shape)

    seq = pl.BlockSpec((S, ATTN_WIDTH), lambda b: (b, 0))
    kvs = pl.BlockSpec((S, KV_WIDTH), lambda b: (b, 0))
    tab = pl.BlockSpec((S, LANES), lambda b: (0, 0))
    nseq = T // S
    return _launch(
        body, [sinks, q, k4, v4, do, rc, rsa, rsb], name="attn_bwd", grid=(nseq,),
        in_specs=[pl.BlockSpec(memory_space=pltpu.SMEM), seq, seq, seq, seq, tab, tab, tab],
        out_specs=[seq, kvs, kvs, pl.BlockSpec((8, LANES), lambda b: (b, 0))],
        out_shape=[jax.ShapeDtypeStruct((T, ATTN_WIDTH), MXU_DTYPE), jax.ShapeDtypeStruct((T, KV_WIDTH), MXU_DTYPE),
                   jax.ShapeDtypeStruct((T, KV_WIDTH), MXU_DTYPE), jax.ShapeDtypeStruct((8 * nseq, LANES), F32)],
        scratch_shapes=[pltpu.VMEM((S, 512), F32), pltpu.VMEM((S, 512), F32)],
        sem=("arbitrary",), rider=rider)


def _branch_weights(wbp_ref, wba_ref, wbp_s, wba_s):
    @pl.when(pl.program_id(0) == 0)
    def _():
        for j in range(N_DEV):
            wbp_s[:, LANES * j:LANES * (j + 1)] = wbp_ref[j]
            wba_s[:, LANES * j:LANES * (j + 1)] = wba_ref[j]


def _mix_fwd_call(yp, ya, g, x, wbp, wba, wout, g2, g3, rider=None):
    T = x.shape[0]
    tm = _tile(T, 512)

    def body(yp_ref, ya_ref, g_ref, x_ref, wbp_ref, wba_ref, wout_ref, g2_ref, g3_ref,
             mix_ref, x1_ref, h2_ref, wbp_s, wba_s):
        _branch_weights(wbp_ref, wba_ref, wbp_s, wba_s)
        bp = _dot(yp_ref[...], wbp_s[...], NN)
        ba = _dot(ya_ref[...], wba_s[...], NN)
        merged = g_ref[:, :D_MODEL].astype(F32) * bp + g_ref[:, D_MODEL:].astype(F32) * ba
        mix = _dot(merged.astype(MXU_DTYPE), wout_ref[...], NN)
        mix_ref[...] = mix
        x1 = x_ref[...] + (mix * _rms_r(mix)) * g2_ref[...]
        x1_ref[...] = x1
        h2_ref[...] = ((x1 * _rms_r(x1)) * g3_ref[...]).astype(MXU_DTYPE)

    tok = lambda w: pl.BlockSpec((tm, w), lambda i: (i, 0))
    full = lambda a: pl.BlockSpec(a.shape, lambda i: (0,) * a.ndim)
    return _launch(
        body, [yp, ya, g, x, wbp, wba, wout, g2, g3], name="mix_fwd", grid=(T // tm,),
        in_specs=[tok(POOL_WIDTH), tok(ATTN_WIDTH), tok(GATE_WIDTH), tok(D_MODEL), full(wbp), full(wba), full(wout),
                  full(g2), full(g3)],
        out_specs=[tok(D_MODEL), tok(D_MODEL), tok(D_MODEL)],
        out_shape=[jax.ShapeDtypeStruct((T, D_MODEL), F32), jax.ShapeDtypeStruct((T, D_MODEL), F32),
                   jax.ShapeDtypeStruct((T, D_MODEL), MXU_DTYPE)],
        scratch_shapes=[pltpu.VMEM((POOL_WIDTH, D_MODEL), MXU_DTYPE), pltpu.VMEM((ATTN_WIDTH, D_MODEL), MXU_DTYPE)],
        sem=("arbitrary",), rider=rider)


def _mix_bwd_call(dx1, mix, yp, ya, g, wbp, wba, wout, g2, rider=None):
    T = dx1.shape[0]
    tm = _tile(T, 512)

    def body(dx1_ref, mix_ref, yp_ref, ya_ref, g_ref, wbp_ref, wba_ref, wout_ref, g2_ref,
             dyp_ref, do_ref, dgates_ref, dg2_ref, dbg_ref, gout_ref, gbp_ref, gba_ref,
             wbp_s, wba_s, acc_out, acc_bp, acc_ba, sem):
        _branch_weights(wbp_ref, wba_ref, wbp_s, wba_s)
        step = pl.program_id(0)

        @pl.when(step == 0)
        def _():
            dg2_ref[...] = jnp.zeros_like(dg2_ref)
            dbg_ref[...] = jnp.zeros_like(dbg_ref)
            acc_out[...] = jnp.zeros_like(acc_out)
            acc_bp[...] = jnp.zeros_like(acc_bp)
            acc_ba[...] = jnp.zeros_like(acc_ba)

        mix = mix_ref[...]
        dmix, dg2 = _rms_bwd(dx1_ref[...], mix, _rms_r(mix), g2_ref[...])
        dg2_ref[...] += jnp.sum(dg2, axis=0, keepdims=True)
        dmixb = dmix.astype(MXU_DTYPE)
        dmerged = _dot(dmixb, wout_ref[...], NT)
        yp, ya = yp_ref[...], ya_ref[...]
        bp = _dot(yp, wbp_s[...], NN)
        ba = _dot(ya, wba_s[...], NN)
        gp, ga = g_ref[:, :D_MODEL].astype(F32), g_ref[:, D_MODEL:].astype(F32)
        acc_out[...] += _dot((gp * bp + ga * ba).astype(MXU_DTYPE), dmixb, TN)
        dgp = dmerged * bp * (gp * (1.0 - gp))
        dga = dmerged * ba * (ga * (1.0 - ga))
        dbg_ref[:, :D_MODEL] += jnp.sum(dgp, axis=0, keepdims=True)
        dbg_ref[:, D_MODEL:] += jnp.sum(dga, axis=0, keepdims=True)
        dgates_ref[:, :D_MODEL] = dgp.astype(MXU_DTYPE)
        dgates_ref[:, D_MODEL:] = dga.astype(MXU_DTYPE)
        dbp = (dmerged * gp).astype(MXU_DTYPE)
        dba = (dmerged * ga).astype(MXU_DTYPE)
        acc_bp[...] += _dot(yp, dbp, TN)
        acc_ba[...] += _dot(ya, dba, TN)
        dyp_ref[...] = _dot(dbp, wbp_s[...], NT)
        do_ref[...] = _dot(dba, wba_s[...], NT).astype(MXU_DTYPE)

        @pl.when(step == pl.num_programs(0) - 1)
        def _():
            copies = [pltpu.make_async_copy(acc_out, gout_ref, sem.at[0])]
            for j in range(N_DEV):
                cols = slice(LANES * j, LANES * (j + 1))
                copies.append(pltpu.make_async_copy(acc_bp.at[:, cols], gbp_ref.at[j], sem.at[1 + j]))
                copies.append(pltpu.make_async_copy(acc_ba.at[:, cols], gba_ref.at[j], sem.at[1 + N_DEV + j]))
            for cp in copies:
                cp.start()
            for cp in copies:
                cp.wait()

    tok = lambda w: pl.BlockSpec((tm, w), lambda i: (i, 0))
    full = lambda a: pl.BlockSpec(a.shape, lambda i: (0,) * a.ndim)
    acc = lambda w: pl.BlockSpec((1, w), lambda i: (0, 0))
    hbm = pl.BlockSpec(memory_space=pl.ANY)
    sd = jax.ShapeDtypeStruct
    return _launch(
        body, [dx1, mix, yp, ya, g, wbp, wba, wout, g2], name="mix_bwd", grid=(T // tm,),
        in_specs=[tok(D_MODEL), tok(D_MODEL), tok(POOL_WIDTH), tok(ATTN_WIDTH), tok(GATE_WIDTH), full(wbp), full(wba),
                  full(wout), full(g2)],
        out_specs=[tok(POOL_WIDTH), tok(ATTN_WIDTH), tok(GATE_WIDTH), acc(D_MODEL), acc(GATE_WIDTH), hbm, hbm, hbm],
        out_shape=[sd((T, POOL_WIDTH), F32), sd((T, ATTN_WIDTH), MXU_DTYPE), sd((T, GATE_WIDTH), MXU_DTYPE),
                   sd((1, D_MODEL), F32), sd((1, GATE_WIDTH), F32), sd((D_MODEL, D_MODEL), F32),
                   sd((N_DEV, POOL_WIDTH, LANES), F32), sd((N_DEV, ATTN_WIDTH, LANES), F32)],
        scratch_shapes=[pltpu.VMEM((POOL_WIDTH, D_MODEL), MXU_DTYPE), pltpu.VMEM((ATTN_WIDTH, D_MODEL), MXU_DTYPE),
                        pltpu.VMEM((D_MODEL, D_MODEL), F32), pltpu.VMEM((POOL_WIDTH, D_MODEL), F32),
                        pltpu.VMEM((ATTN_WIDTH, D_MODEL), F32), pltpu.SemaphoreType.DMA((1 + 2 * N_DEV,))],
        sem=("arbitrary",), rider=rider)


def _mlp_up_call(h2, wup):
    T = h2.shape[0]
    tm = _tile(T, 512)
    fc = D_FF // N_DEV

    def body(h2_ref, wup_ref, act_ref):
        h2 = h2_ref[...]
        for j in range(N_DEV):
            rl = jnp.maximum(_dot(h2, wup_ref[j], NN), 0.0)
            act_ref[:, fc * j:fc * (j + 1)] = (rl * rl).astype(MXU_DTYPE)

    sd = jax.ShapeDtypeStruct
    return pl.pallas_call(
        body, name="mlp_up", grid=(T // tm,),
        in_specs=[pl.BlockSpec((tm, D_MODEL), lambda i: (i, 0)),
                  pl.BlockSpec(wup.shape, lambda i: (0, 0, 0), pipeline_mode=pl.Buffered(1))],
        out_specs=pl.BlockSpec((tm, D_FF), lambda i: (i, 0)), out_shape=sd((T, D_FF), MXU_DTYPE),
        compiler_params=_params(("arbitrary",)),
    )(h2, wup)


def _mlp_call(x1, act, target, wup, wdown, g3, g4):
    T = x1.shape[0]
    tm = _tile(T, 256)
    fc = D_FF // N_DEV

    def body(x1_ref, act_ref, t_ref, wup_ref, wdown_ref, g3_ref, g4_ref,
             da_ref, dff_ref, dx1_ref, dg3_ref, dg4_ref, loss_ref):
        @pl.when(pl.program_id(0) == 0)
        def _():
            dg3_ref[...] = jnp.zeros_like(dg3_ref)
            dg4_ref[...] = jnp.zeros_like(dg4_ref)
            loss_ref[...] = jnp.zeros_like(loss_ref)

        ff = jnp.zeros((tm, D_MODEL), F32)
        for j in range(N_DEV):
            ff = ff + _dot(act_ref[:, fc * j:fc * (j + 1)], wdown_ref[j], NN)
        x1 = x1_ref[...]
        r4 = _rms_r(ff)
        err = x1 + (ff * r4) * g4_ref[...] - t_ref[...]
        loss_ref[...] += jnp.sum(err * err, axis=0, keepdims=True)
        dy = err * (1.0 / D_MODEL)
        dff, dg4 = _rms_bwd(dy, ff, r4, g4_ref[...])
        dg4_ref[...] += jnp.sum(dg4, axis=0, keepdims=True)
        dffb = dff.astype(MXU_DTYPE)
        dff_ref[...] = dffb
        dh2 = jnp.zeros((tm, D_MODEL), F32)
        for j in range(N_DEV):
            sl = slice(fc * j, fc * (j + 1))
            rl = jnp.sqrt(act_ref[:, sl].astype(F32))
            dab = (_dot(dffb, wdown_ref[j], NT) * (2.0 * rl)).astype(MXU_DTYPE)
            da_ref[:, sl] = dab
            dh2 = dh2 + _dot(dab, wup_ref[j], NT)
        dx1, dg3 = _rms_bwd(dh2, x1, _rms_r(x1), g3_ref[...])
        dg3_ref[...] += jnp.sum(dg3, axis=0, keepdims=True)
        dx1_ref[...] = dy + dx1

    tok = lambda w: pl.BlockSpec((tm, w), lambda i: (i, 0))
    full = lambda a: pl.BlockSpec(a.shape, lambda i: (0,) * a.ndim, pipeline_mode=pl.Buffered(1))
    vec = pl.BlockSpec((1, D_MODEL), lambda i: (0, 0))
    sd = jax.ShapeDtypeStruct
    return pl.pallas_call(
        body, name="mlp_down_bwd", grid=(T // tm,),
        in_specs=[tok(D_MODEL), tok(D_FF), tok(D_MODEL), full(wup), full(wdown), vec, vec],
        out_specs=[tok(D_FF), tok(D_MODEL), tok(D_MODEL), vec, vec, vec],
        out_shape=[sd((T, D_FF), MXU_DTYPE), sd((T, D_MODEL), MXU_DTYPE),
                   sd((T, D_MODEL), F32), sd((1, D_MODEL), F32), sd((1, D_MODEL), F32), sd((1, D_MODEL), F32)],
        compiler_params=_params(("arbitrary",)),
    )(x1, act, target, wup, wdown, g3, g4)


def _inproj_bwd_call(du, dq, dk, dv, dgates, dx1, x, win_t, g1, rider=None):
    T = x.shape[0]
    tm = _tile(T, 512)

    def body(du_ref, dq_ref, dk_ref, dv_ref, dgt_ref, dx1_ref, x_ref, w_ref, g1_ref, gx_ref, dg1_ref, db_ref):
        @pl.when(pl.program_id(0) == 0)
        def _():
            dg1_ref[...] = jnp.zeros_like(dg1_ref)
            db_ref[...] = jnp.zeros_like(db_ref)

        dh = jnp.zeros((tm, D_MODEL), F32)
        for ref, lo, hi in ((du_ref, 0, C_Q), (dq_ref, C_Q, C_K), (dk_ref, C_K, C_V), (dv_ref, C_V, C_G),
                            (dgt_ref, C_G, IN_WIDTH)):
            piece = ref[...]
            dh = dh + _dot(piece, w_ref[lo:hi, :], NN)
            if hi <= C_G:
                db_ref[:, lo:hi] += jnp.sum(piece.astype(F32), axis=0, keepdims=True)
        xv = x_ref[...]
        dx, dg1 = _rms_bwd(dh, xv, _rms_r(xv), g1_ref[...])
        dg1_ref[...] += jnp.sum(dg1, axis=0, keepdims=True)
        gx_ref[...] = dx1_ref[...] + dx

    tok = lambda w: pl.BlockSpec((tm, w), lambda i: (i, 0))
    full = lambda a: pl.BlockSpec(a.shape, lambda i: (0,) * a.ndim)
    sd = jax.ShapeDtypeStruct
    return _launch(
        body, [du, dq, dk, dv, dgates, dx1, x, win_t, g1], name="inproj_bwd", grid=(T // tm,),
        in_specs=[tok(POOL_WIDTH), tok(ATTN_WIDTH), tok(KV_WIDTH), tok(KV_WIDTH), tok(GATE_WIDTH), tok(D_MODEL),
                  tok(D_MODEL), full(win_t), full(g1)],
        out_specs=[tok(D_MODEL), pl.BlockSpec((1, D_MODEL), lambda i: (0, 0)), pl.BlockSpec((1, C_G), lambda i: (0, 0))],
        out_shape=[sd((T, D_MODEL), F32), sd((1, D_MODEL), F32), sd((1, C_G), F32)],
        sem=("arbitrary",), rider=rider)


WGRAD_TOKENS = 1024


def _wgrad_rows_call(a, b, name, rider=None):
    T, K = a.shape
    N = b.shape[1]
    tm = _tile(T, WGRAD_TOKENS)
    kb = min(K, 1024)
    per = kb // (K // N_DEV)

    def body(a_ref, b_ref, o_ref):
        @pl.when(pl.program_id(1) == 0)
        def _():
            o_ref[...] = jnp.zeros_like(o_ref)

        d = _dot(a_ref[...], b_ref[...], TN)
        rs = kb // per
        for j in range(per):
            o_ref[j] += d[rs * j:rs * (j + 1)]

    return _launch(
        body, [a, b], name=name, grid=(K // kb, T // tm),
        in_specs=[pl.BlockSpec((tm, kb), lambda i, t: (t, i)), pl.BlockSpec((tm, N), lambda i, t: (t, 0))],
        out_specs=[pl.BlockSpec((per, K // N_DEV, N), lambda i, t: (i, 0, 0))],
        out_shape=[jax.ShapeDtypeStruct((N_DEV, K // N_DEV, N), F32)],
        sem=("arbitrary", "arbitrary"), rider=rider)


def _wgrad_cols_call(a, b, name, rider=None):
    T, K = a.shape
    N = b.shape[1]
    tm = _tile(T, WGRAD_TOKENS)
    nb = min(N, 1024)
    per = nb // (N // N_DEV)

    def body(a_ref, b_ref, o_ref):
        @pl.when(pl.program_id(1) == 0)
        def _():
            o_ref[...] = jnp.zeros_like(o_ref)

        d = _dot(a_ref[...], b_ref[...], TN)
        cs = nb // per
        for j in range(per):
            o_ref[j] += d[:, cs * j:cs * (j + 1)]

    return _launch(
        body, [a, b], name=name, grid=(N // nb, T // tm),
        in_specs=[pl.BlockSpec((tm, K), lambda i, t: (t, 0)), pl.BlockSpec((tm, nb), lambda i, t: (t, i))],
        out_specs=[pl.BlockSpec((per, K, N // N_DEV), lambda i, t: (i, 0, 0))],
        out_shape=[jax.ShapeDtypeStruct((N_DEV, K, N // N_DEV), F32)],
        sem=("arbitrary", "arbitrary"), rider=rider)


def _wgrad_in_call(du, dq, dk, dv, dgates, h, rider=None):
    T = h.shape[0]
    tm = _tile(T, WGRAD_TOKENS)
    rows = IN_WIDTH // N_DEV

    def body(du_ref, dq_ref, dk_ref, dv_ref, dgt_ref, h_ref, o_ref, acc, sem):
        t = pl.program_id(0)

        @pl.when(t == 0)
        def _():
            acc[...] = jnp.zeros_like(acc)

        hv = h_ref[...]
        for ref, lo, hi in ((du_ref, 0, C_Q), (dq_ref, C_Q, C_K), (dk_ref, C_K, C_V), (dv_ref, C_V, C_G),
                            (dgt_ref, C_G, IN_WIDTH)):
            acc[lo:hi, :] += _dot(ref[...], hv, TN)

        @pl.when(t == pl.num_programs(0) - 1)
        def _():
            copies = [pltpu.make_async_copy(acc.at[pl.ds(rows * j, rows), :], o_ref.at[j], sem.at[j])
                      for j in range(N_DEV)]
            for cp in copies:
                cp.start()
            for cp in copies:
                cp.wait()

    tok = lambda w: pl.BlockSpec((tm, w), lambda t: (t, 0))
    return _launch(
        body, [du, dq, dk, dv, dgates, h], name="wgrad_in", grid=(T // tm,),
        in_specs=[tok(POOL_WIDTH), tok(ATTN_WIDTH), tok(KV_WIDTH), tok(KV_WIDTH), tok(GATE_WIDTH), tok(D_MODEL)],
        out_specs=[pl.BlockSpec(memory_space=pl.ANY)],
        out_shape=[jax.ShapeDtypeStruct((N_DEV, rows, D_MODEL), F32)],
        scratch_shapes=[pltpu.VMEM((IN_WIDTH, D_MODEL), F32), pltpu.SemaphoreType.DMA((N_DEV,))],
        sem=("arbitrary",), rider=rider)


def _coords():
    return lax.axis_index("x"), lax.axis_index("y"), lax.axis_index("c")


def _allgather_call(shards, bufs):
    n = len(shards)

    def body(*refs):
        ins, outs = refs[:n], refs[2 * n:3 * n]
        send_sems, recv_sems = refs[3 * n:]
        x, y, c = _coords()
        me, sibling = (x, y, c), (x, y, 1 - c)
        chips = [(1 - x, y), (x, 1 - y), (1 - x, 1 - y)]

        def slot(p):
            return 4 * p[0] + 2 * p[1] + p[2]

        def copy(t, k, block, to, src=None):
            dst = outs[t].at[slot(block)]
            return pltpu.make_async_remote_copy(
                src_ref=dst if src is None else src, dst_ref=dst, send_sem=send_sems.at[t, k],
                recv_sem=recv_sems.at[t, k], device_id=to, device_id_type=MESH)

        first = []
        for t in range(n):
            first.append(copy(t, 0, me, sibling, src=ins[t]))
            first += [copy(t, 1 + j, me, (*chip, c), src=ins[t]) for j, chip in enumerate(chips)]
        for cp in first:
            cp.start()
        passed = []
        for t in range(n):
            for j, chip in enumerate(chips):
                copy(t, 1 + j, (*chip, c), me).wait_recv()
                fwd = copy(t, 4 + j, (*chip, c), sibling)
                fwd.start()
                passed.append(fwd)
        for t in range(n):
            copy(t, 0, sibling, me).wait_recv()
            for j, chip in enumerate(chips):
                copy(t, 4 + j, (*chip, 1 - c), me).wait_recv()
        for cp in first + passed:
            cp.wait_send()

    hbm = pl.BlockSpec(memory_space=pl.ANY)
    return pl.pallas_call(
        body, name="allgather_weights",
        in_specs=[hbm] * (2 * n), out_specs=[hbm] * n,
        out_shape=[jax.ShapeDtypeStruct(b.shape, b.dtype) for b in bufs],
        scratch_shapes=[pltpu.SemaphoreType.DMA((n, 7)), pltpu.SemaphoreType.DMA((n, 7))],
        input_output_aliases={n + t: t for t in range(n)},
    )(*shards, *bufs)


def _slot(p):
    return 4 * p[0] + 2 * p[1] + p[2]


def _rows(ref, span):
    return ref if span is None else ref.at[pl.ds(span[0], span[1])]


ALL = "all"
LOCAL = "local"


def _rows(ref, span):
    return ref if span == ALL else ref.at[pl.ds(span[0], span[1])]


def _rider_ag(items):
    ins, out_shape, aliases, where = [], [], {}, []
    n_remote = n_local = 0
    for t, (shard, buf, snd, fwd) in enumerate(items):
        i_shard = i_buf = None
        if snd is not None:
            i_shard = len(ins)
            ins.append(shard)
        if buf is not None:
            i_buf = len(ins)
            ins.append(buf)
            aliases[i_buf] = t
            out_shape.append(jax.ShapeDtypeStruct(buf.shape, buf.dtype))
        else:
            assert fwd is None and snd is not None
            out_shape.append(jax.ShapeDtypeStruct((N_DEV,) + shard.shape, shard.dtype))
        where.append((i_shard, i_buf, n_remote, n_local))
        n_remote += (4 if snd not in (None, LOCAL) else 0) + (3 if fwd is not None else 0)
        n_local += 1 if snd is not None else 0

    def plan(rins, routs, send, recv, loc, r0, l0):
        x, y, c = _coords()
        peers = [(x, y, 1 - c), (1 - x, y, c), (x, 1 - y, c), (1 - x, 1 - y, c)]
        remote, local = [], []
        for t, (shard, buf, snd, fwd) in enumerate(items):
            i_shard, i_buf, k, l = where[t]
            k, l = r0 + k, l0 + l
            if snd is not None:
                span = ALL if snd == LOCAL else snd
                src, dst = _rows(rins[i_shard], span), _rows(routs[t].at[_slot((x, y, c))], span)
                local.append(pltpu.make_async_copy(src, dst, loc.at[l]))
                for peer in (peers if snd != LOCAL else []):
                    remote.append(pltpu.make_async_remote_copy(
                        src_ref=src, dst_ref=dst, send_sem=send.at[k], recv_sem=recv.at[k],
                        device_id=peer, device_id_type=MESH))
                    k += 1
            if fwd is not None:
                for px, py, pc in peers[1:]:
                    s = _slot((px, py, pc))
                    remote.append(pltpu.make_async_remote_copy(
                        src_ref=_rows(rins[i_buf].at[s], fwd), dst_ref=_rows(routs[t].at[s], fwd),
                        send_sem=send.at[k], recv_sem=recv.at[k], device_id=peers[0], device_id_type=MESH))
                    k += 1
        return remote, local

    return _Rider(ins, out_shape, n_remote, n_local, plan, aliases)


def _gather_buffer(shard, me):
    return lax.dynamic_update_slice(lax.empty((N_DEV,) + shard.shape, shard.dtype), shard[None], (me, 0, 0))


def _rider_ag_remote(shards, me):
    n = len(shards)

    def plan(ins, outs, send, recv, loc, r0, l0):
        x, y, c = _coords()
        remote = []
        for t in range(n):
            dst = outs[t].at[_slot((x, y, c))]
            for k, peer in enumerate([(x, y, 1 - c), (1 - x, y, c), (x, 1 - y, c), (1 - x, 1 - y, c)]):
                remote.append(pltpu.make_async_remote_copy(
                    src_ref=ins[t], dst_ref=dst, send_sem=send.at[r0 + 4 * t + k], recv_sem=recv.at[r0 + 4 * t + k],
                    device_id=peer, device_id_type=MESH))
        return remote, []

    return _Rider(shards, [jax.ShapeDtypeStruct((N_DEV,) + s.shape, s.dtype) for s in shards], 4 * n, 0, plan,
                  lands=[_gather_buffer(s, me) for s in shards])


def _rider_rs_sibling(grads):
    n = len(grads)

    def plan(ins, outs, send, recv, loc, r0, l0):
        x, y, c = _coords()
        remote = []
        for t in range(n):
            for q in range(4):
                remote.append(pltpu.make_async_remote_copy(
                    src_ref=ins[t].at[q, 1 - c], dst_ref=outs[t].at[q], send_sem=send.at[r0 + 4 * t + q],
                    recv_sem=recv.at[r0 + 4 * t + q], device_id=(x, y, 1 - c), device_id_type=MESH))
        return remote, []

    return _Rider(grads, [jax.ShapeDtypeStruct((4,) + g.shape[2:], g.dtype) for g in grads], 4 * n, 0, plan)


def _rider_rs_chips(sums, rows=None, into=None):
    n = len(sums)
    rows = rows or [ALL] * n

    def plan(ins, outs, send, recv, loc, r0, l0):
        x, y, c = _coords()
        remote = []
        for t in range(n):
            for r, (px, py) in enumerate([(1 - x, y), (x, 1 - y), (1 - x, 1 - y)]):
                remote.append(pltpu.make_async_remote_copy(
                    src_ref=_rows(ins[t].at[2 * px + py], rows[t]), dst_ref=_rows(outs[t].at[r], rows[t]),
                    send_sem=send.at[r0 + 3 * t + r], recv_sem=recv.at[r0 + 3 * t + r],
                    device_id=(px, py, c), device_id_type=MESH))
        return remote, []

    out_shape = [jax.ShapeDtypeStruct((3,) + s.shape[1:], s.dtype) for s in sums]
    if into is None:
        return _Rider(sums, out_shape, 3 * n, 0, plan)
    return _Rider(list(sums) + list(into), out_shape, 3 * n, 0, plan, aliases={n + t: t for t in range(n)})


def _rider_gather_remote(parts):
    n = len(parts)

    def plan(ins, outs, send, recv, loc, r0, l0):
        x, y, c = _coords()
        me = _slot((x, y, c))
        remote = []
        for t in range(n):
            for k in range(1, N_DEV):
                peer = (x ^ ((k >> 2) & 1), y ^ ((k >> 1) & 1), c ^ (k & 1))
                remote.append(pltpu.make_async_remote_copy(
                    src_ref=ins[t], dst_ref=outs[t].at[me], send_sem=send.at[r0 + 7 * t + k - 1],
                    recv_sem=recv.at[r0 + 7 * t + k - 1], device_id=peer, device_id_type=MESH))
        return remote, []

    return _Rider(parts, [jax.ShapeDtypeStruct((N_DEV,) + p.shape, p.dtype) for p in parts], 7 * n, 0, plan)


def _chip_sum_call(idx, grads, recvd, out_dtypes, name):
    n = len(grads)

    def body(i_ref, *refs):
        for t in range(n):
            refs[2 * n + t][0] = (refs[t][0, 0] + refs[n + t][0]).astype(out_dtypes[t])

    def chip(k, s):
        return jnp.where(k >= s[0], k + 1, k)

    in_specs = [pl.BlockSpec((1, 1) + g.shape[2:], lambda k, s: (chip(k, s), s[1], 0, 0)) for g in grads]
    in_specs += [pl.BlockSpec((1,) + r.shape[1:], lambda k, s: (chip(k, s), 0, 0)) for r in recvd]
    return pl.pallas_call(
        body, name=name,
        grid_spec=pltpu.PrefetchScalarGridSpec(
            num_scalar_prefetch=1, grid=(3,), in_specs=in_specs,
            out_specs=[pl.BlockSpec((1,) + r.shape[1:], lambda k, s: (chip(k, s), 0, 0)) for r in recvd]),
        out_shape=[jax.ShapeDtypeStruct(r.shape, dt) for r, dt in zip(recvd, out_dtypes)],
        compiler_params=_params(("arbitrary",)),
    )(idx, *grads, *recvd)


def _final_sum_call(idx, grads, recvd1, recvd2):
    n = len(grads)
    nsteps = 2

    def body(i_ref, *refs):
        for t in range(n):
            g, r1, r2, o = refs[t], refs[n + t], refs[2 * n + t], refs[3 * n + t]
            s = g[0, 0] + r1[0]
            for r in range(3):
                s = s + r2[r].astype(F32)
            o[...] = s

    def rows(a):
        r = a.shape[-2]
        return r // nsteps if (r // nsteps) % 16 == 0 else r

    def step(a):
        return (lambda i: i) if rows(a) != a.shape[-2] else (lambda i: 0)

    in_specs = [pl.BlockSpec((1, 1, rows(g), g.shape[3]), lambda i, s, st=step(g): (s[0], s[1], st(i), 0)) for g in grads]
    in_specs += [pl.BlockSpec((1, rows(r), r.shape[2]), lambda i, s, st=step(r): (s[0], st(i), 0)) for r in recvd1]
    in_specs += [pl.BlockSpec((3, rows(r), r.shape[2]), lambda i, s, st=step(r): (0, st(i), 0)) for r in recvd2]
    return pl.pallas_call(
        body, name="rs_final_sum",
        grid_spec=pltpu.PrefetchScalarGridSpec(
            num_scalar_prefetch=1, grid=(nsteps,), in_specs=in_specs,
            out_specs=[pl.BlockSpec((rows(r), r.shape[2]), lambda i, s, st=step(r): (st(i), 0)) for r in recvd2]),
        out_shape=[jax.ShapeDtypeStruct(r.shape[1:], F32) for r in recvd2],
        compiler_params=_params(("arbitrary",)),
    )(idx, *grads, *recvd1, *recvd2)


def _sum8_call(parts):
    def body(p_ref, o_ref):
        s = p_ref[0]
        for j in range(1, N_DEV):
            s = s + p_ref[j]
        o_ref[...] = s

    return pl.pallas_call(body, name="sum_small_partials",
                          out_shape=jax.ShapeDtypeStruct(parts.shape[1:], parts.dtype))(parts)


def _adamw(w, g, m, v):
    m = ADAM_B1 * m + (1.0 - ADAM_B1) * g
    v = ADAM_B2 * v + (1.0 - ADAM_B2) * (g * g)
    m_hat = m / (1.0 - ADAM_B1 ** ADAM_STEP)
    v_hat = v / (1.0 - ADAM_B2 ** ADAM_STEP)
    delta = -ADAM_LR * (m_hat / (jnp.sqrt(v_hat) + ADAM_EPS) + ADAM_WD * w)
    return delta, m, v


def _adamw_call(ws, gs, ms, vs, nsteps, name):
    n = len(ws)

    def body(*refs):
        for t in range(n):
            w, g, m, v = (refs[k * n + t][...] for k in range(4))
            d, m2, v2 = _adamw(w, g, m, v)
            refs[4 * n + t][...] = d
            refs[5 * n + t][...] = m2
            refs[6 * n + t][...] = v2

    def spec(a):
        assert a.shape[0] % nsteps == 0 and (nsteps == 1 or (a.shape[0] // nsteps) % 8 == 0), a.shape
        return pl.BlockSpec((a.shape[0] // nsteps, a.shape[1]), lambda i: (i, 0))

    specs = [spec(a) for a in ws]
    outs = pl.pallas_call(
        body, name=name, grid=(nsteps,),
        in_specs=specs * 4, out_specs=specs * 3,
        out_shape=[jax.ShapeDtypeStruct(a.shape, F32) for a in ws] * 3,
        compiler_params=_params(("arbitrary",)),
    )(*ws, *gs, *ms, *vs)
    return outs[:n], outs[n:2 * n], outs[2 * n:]


def _adamw_rs_call(idx, after, gws, r1s, r2s, ws, ms, vs, nsteps, name):
    n = len(ws)

    def body(i_ref, after_ref, *refs):
        for t in range(n):
            gw, r1, r2, w, m, v = (refs[k * n + t] for k in range(6))
            g = gw[0, 0] + r1[0]
            for r in range(3):
                g = g + r2[r].astype(F32)
            d, m2, v2 = _adamw(w[...], g, m[...], v[...])
            refs[6 * n + t][...] = g
            refs[7 * n + t][...] = d
            refs[8 * n + t][...] = m2
            refs[9 * n + t][...] = v2

    def rb(a):
        r = a.shape[0] // nsteps
        assert a.shape[0] % nsteps == 0 and r % 16 == 0, a.shape
        return r

    in_specs = [pl.BlockSpec((1, 1, rb(w), w.shape[1]), lambda i, s: (s[0], s[1], i, 0)) for w in ws]
    in_specs += [pl.BlockSpec((1, rb(w), w.shape[1]), lambda i, s: (s[0], i, 0)) for w in ws]
    in_specs += [pl.BlockSpec((3, rb(w), w.shape[1]), lambda i, s: (0, i, 0)) for w in ws]
    plain = [pl.BlockSpec((rb(w), w.shape[1]), lambda i, s: (i, 0)) for w in ws]
    outs = pl.pallas_call(
        body, name=name,
        grid_spec=pltpu.PrefetchScalarGridSpec(
            num_scalar_prefetch=1, grid=(nsteps,),
            in_specs=[pl.BlockSpec(memory_space=pl.ANY)] + in_specs + plain * 3, out_specs=plain * 4),
        out_shape=[jax.ShapeDtypeStruct(w.shape, F32) for w in ws] * 4,
        compiler_params=_params(("arbitrary",)),
    )(idx, after, *gws, *r1s, *r2s, *ws, *ms, *vs)
    return outs[:n], outs[n:2 * n], outs[2 * n:3 * n], outs[3 * n:]


def _rows128(a, pad_rows):
    flat = a.reshape(-1).astype(F32)
    flat = jnp.pad(flat, (0, pad_rows * LANES - flat.shape[0]))
    return flat.reshape(pad_rows, LANES)


_SMALL_A = (("w_pool", 512), ("pool_scale", 8), ("attn_sinks", 8), ("g_mix_post", 8), ("g_mlp_pre", 8),
            ("g_mlp_post", 8), ("loss", 8), ("b_in_gates", 16))
_SMALL_A_ROWS = 640
_SMALL_B = (("g_mix_pre", 8), ("b_in_head", 16))


def _pack(parts, layout, total_rows):
    rows = [_rows128(parts[k], r) for k, r in layout]
    pad = total_rows - sum(r for _, r in layout)
    if pad:
        rows.append(jnp.zeros((pad, LANES), F32))
    return jnp.concatenate(rows, axis=0)


def _unpack(buf, layout, sizes):
    out, off = {}, 0
    for k, r in layout:
        out[k] = buf[off:off + r].reshape(-1)[:sizes[k]]
        off += r
    return out


def kernel(x, g_mix_pre, w_in, b_in, w_pool, pool_scale, attn_sinks, w_branch_pool, w_branch_attn, w_out, g_mix_post, g_mlp_pre, w_up, w_down, g_mlp_post, loss_target, m_g_mix_pre, m_w_in, m_b_in, m_w_pool, m_pool_scale, m_attn_sinks, m_w_branch_pool, m_w_branch_attn, m_w_out, m_g_mix_post, m_g_mlp_pre, m_w_up, m_w_down, m_g_mlp_post, v_g_mix_pre, v_w_in, v_b_in, v_w_pool, v_pool_scale, v_attn_sinks, v_w_branch_pool, v_w_branch_attn, v_w_out, v_g_mix_post, v_g_mlp_pre, v_w_up, v_w_down, v_g_mlp_post):
    B, S, _ = x.shape
    T = B * S
    xt = x.reshape(T, D_MODEL)
    tgt = loss_target.reshape(T, D_MODEL)
    cx, cy, cc = _coords()

    cidx = jnp.stack([2 * cx + cy, cc]).astype(jnp.int32)
    by_chip = lambda gr: gr.reshape((4, 2) + gr.shape[1:])
    bf = lambda w: w[0].astype(MXU_DTYPE)

    me = _slot((cx, cy, cc))
    win_l = w_in[0].T.astype(MXU_DTYPE)
    (win_s,) = _allgather_call([win_l], [_gather_buffer(win_l, me)])
    win_t = win_s.reshape(IN_WIDTH, D_MODEL)
    wpool_b = bf(w_pool)
    rc, rsa, rsb = _rot_tables(S)

    wbp_l, wba_l, wout_l, wup_l, wdown_l = bf(w_branch_pool), bf(w_branch_attn), bf(w_out), bf(w_up), bf(w_down)
    (c_br, c_up, c_dn), tok = _copies_start(
        [_rider_ag_remote([wbp_l, wba_l, wout_l], me), _rider_ag_remote([wup_l], me), _rider_ag_remote([wdown_l], me)],
        "allgather_start", after=win_s)
    (h, u, q, k4, v4, g), _ = _inproj_call(xt, g_mix_pre, win_t, b_in, rc, rsa, rsb, S, rider=_after(tok))
    yp = _pool_call(u, wpool_b, pool_scale, S)
    wbp_1, wba_1, wout_1 = _copies_wait([c_br], yp, "allgather_wait_branch")
    (ya,), (wbp_s, wba_s, wout_s) = _attn_call(
        attn_sinks, q, k4, v4, S,
        rider=_rider_ag([(None, wbp_1, None, ALL), (None, wba_1, None, ALL), (None, wout_1, None, ALL)]))
    wout_f = wout_s.reshape(D_MODEL, D_MODEL)
    (wup_1,) = _copies_wait([c_up], ya, "allgather_wait_up")
    (mix, x1, h2), (wup_s,) = _mix_fwd_call(
        yp, ya, g, xt, wbp_s, wba_s, wout_f, g_mix_post, g_mlp_pre, rider=_rider_ag([(None, wup_1, None, ALL)]))
    act = _mlp_up_call(h2, wup_s)
    (wdown_1,) = _copies_wait([c_dn], act, "allgather_wait_down")
    (wdown_s,) = _comm_call(_rider_ag([(None, wdown_1, None, ALL)]), "allgather_pass_down")

    da, dff, dx1, dg3, dg4, lossvec = _mlp_call(x1, act, tgt, wup_s, wdown_s, g_mlp_pre, g_mlp_post)
    gw_down = by_chip(_wgrad_rows_call(act, dff, "wgrad_down")[0])
    (gw_up,), (r1_down,) = _wgrad_cols_call(h2, da, "wgrad_up", rider=_rider_rs_sibling([gw_down]))
    gw_up = by_chip(gw_up)
    (s_down,) = _chip_sum_call(cidx, [gw_down], [r1_down], [MXU_DTYPE], "rs_chip_sum_down")
    (c_down,), tok = _copies_start([_rider_rs_chips([s_down])], "rs_chips_start_down")
    (dyp, do, dgates, dg2, dbg, gw_out, gw_bp, gw_ba), (r1_up,) = _mix_bwd_call(
        dx1, mix, yp, ya, g, wbp_s, wba_s, wout_f, g_mix_post, rider=_after(tok, _rider_rs_sibling([gw_up])))
    gw_out = by_chip(gw_out.reshape(N_DEV, D_MODEL // N_DEV, D_MODEL))
    gw_bp, gw_ba = by_chip(gw_bp), by_chip(gw_ba)
    (s_up,) = _chip_sum_call(cidx, [gw_up], [r1_up], [MXU_DTYPE], "rs_chip_sum_up")
    (c_up,), tok = _copies_start([_rider_rs_chips([s_up])], "rs_chips_start_up")
    (dq, dk, dv, dsink), (r1_out, r1_bp, r1_ba) = _attn_bwd_call(
        attn_sinks, q, k4, v4, do, rc, rsa, rsb, S, rider=_after(tok, _rider_rs_sibling([gw_out, gw_bp, gw_ba])))
    s_obb = _chip_sum_call(cidx, [gw_out, gw_bp, gw_ba], [r1_out, r1_bp, r1_ba], [MXU_DTYPE] * 3, "rs_chip_sum_branch")
    (c_obb,), tok = _copies_start([_rider_rs_chips(s_obb)], "rs_chips_start_branch")
    (du, dwp, dps), _ = _pool_bwd_call(u, dyp, wpool_b, pool_scale, S, rider=_after(tok))
    (gw_in,) = _wgrad_in_call(du, dq, dk, dv, dgates, h)
    gw_in = by_chip(gw_in)

    small_a = {"w_pool": dwp, "pool_scale": dps,
               "attn_sinks": jnp.sum(dsink.reshape(B, 8, LANES)[:, 0, :N_Q_HEADS], axis=0), "g_mix_post": dg2,
               "g_mlp_pre": dg3, "g_mlp_post": dg4, "loss": lossvec, "b_in_gates": dbg}
    gw_sa = by_chip(_pack(small_a, _SMALL_A, _SMALL_A_ROWS).reshape(N_DEV, _SMALL_A_ROWS // N_DEV, LANES))
    r1_in, r1_sa = _comm_call(_rider_rs_sibling([gw_in, gw_sa]), "rs_sibling_in")
    s_in, s_sa = _chip_sum_call(cidx, [gw_in, gw_sa], [r1_in, r1_sa], [MXU_DTYPE, F32], "rs_chip_sum_in")
    (c_in,), tok = _copies_start([_rider_rs_chips([s_in, s_sa])], "rs_chips_start_in")
    (gx, dg1, dba_in), _ = _inproj_bwd_call(du, dq, dk, dv, dgates, dx1, xt, win_t, g_mix_pre, rider=_after(tok))
    r2_down, r2_up, r2_out, r2_bp, r2_ba, r2_in, r2_sa = _copies_wait([c_down, c_up, c_obb, c_in], dg1, "rs_chips_wait")

    (g_sa,) = _final_sum_call(cidx, [gw_sa], [r1_sa], [r2_sa])
    part_b = _pack({"g_mix_pre": dg1, "b_in_head": dba_in}, _SMALL_B, sum(r for _, r in _SMALL_B))
    (c_small,), tok = _copies_start([_rider_gather_remote([g_sa, part_b])], "allgather_small_start")

    in_t = _adamw_rs_call(cidx, tok, [gw_in], [r1_in], [r2_in], [w_in[0].T], [m_w_in[0].T], [v_w_in[0].T], 2,
                          "adamw_w_in")
    rest = _adamw_rs_call(
        cidx, tok, [gw_bp, gw_ba, gw_out, gw_up, gw_down], [r1_bp, r1_ba, r1_out, r1_up, r1_down],
        [r2_bp, r2_ba, r2_out, r2_up, r2_down], [w_branch_pool[0], w_branch_attn[0], w_out[0], w_up[0], w_down[0]],
        [m_w_branch_pool[0], m_w_branch_attn[0], m_w_out[0], m_w_up[0], m_w_down[0]],
        [v_w_branch_pool[0], v_w_branch_attn[0], v_w_out[0], v_w_up[0], v_w_down[0]], N_DEV, "adamw_shards")
    big_g, big_d, big_m2, big_v2 = ([a[0].T] + list(b) for a, b in zip(in_t, rest))

    sa_all, sb_all = _copies_wait([c_small], rest[0][0], "allgather_small_wait")
    sa_all = lax.dynamic_update_slice(sa_all, g_sa[None], (me, 0, 0))
    sb_sum = _sum8_call(lax.dynamic_update_slice(sb_all, part_b[None], (me, 0, 0)))

    names = ["g_mix_pre", "b_in", "w_pool", "pool_scale", "attn_sinks", "g_mix_post", "g_mlp_pre", "g_mlp_post"]
    sm_w = dict(g_mix_pre=g_mix_pre, b_in=b_in, w_pool=w_pool, pool_scale=pool_scale, attn_sinks=attn_sinks,
                g_mix_post=g_mix_post, g_mlp_pre=g_mlp_pre, g_mlp_post=g_mlp_post)
    sm_m = dict(g_mix_pre=m_g_mix_pre, b_in=m_b_in, w_pool=m_w_pool, pool_scale=m_pool_scale, attn_sinks=m_attn_sinks,
                g_mix_post=m_g_mix_post, g_mlp_pre=m_g_mlp_pre, g_mlp_post=m_g_mlp_post)
    sm_v = dict(g_mix_pre=v_g_mix_pre, b_in=v_b_in, w_pool=v_w_pool, pool_scale=v_pool_scale, attn_sinks=v_attn_sinks,
                g_mix_post=v_g_mix_post, g_mlp_pre=v_g_mlp_pre, g_mlp_post=v_g_mlp_post)
    sizes = {k: sm_w[k].size for k in names}
    sizes.update(loss=D_MODEL, b_in_gates=GATE_WIDTH, b_in_head=C_G)
    sm_g = _unpack(sa_all.reshape(_SMALL_A_ROWS, LANES), _SMALL_A, sizes)
    sm_g.update(_unpack(sb_sum, _SMALL_B, sizes))
    sm_g["b_in"] = jnp.concatenate([sm_g["b_in_head"], sm_g["b_in_gates"]])
    loss = (0.5 / D_MODEL) * jnp.sum(sm_g["loss"])
    two_d = lambda a: a.reshape(-1, a.shape[-1])
    sd_, sm2_, sv2_ = _adamw_call([two_d(sm_w[k]) for k in names], [two_d(sm_g[k].reshape(sm_w[k].shape)) for k in names],
                                  [two_d(sm_m[k]) for k in names], [two_d(sm_v[k]) for k in names], 1, "adamw_small")
    like = lambda vals: {k: a.reshape(sm_w[k].shape) for k, a in zip(names, vals)}
    sm_d, sm_m2, sm_v2 = like(sd_), like(sm2_), like(sv2_)
    sm_gr = {k: sm_g[k].reshape(sm_w[k].shape) for k in names}

    order = ["g_mix_pre", "w_in", "b_in", "w_pool", "pool_scale", "attn_sinks", "w_branch_pool", "w_branch_attn",
             "w_out", "g_mix_post", "g_mlp_pre", "w_up", "w_down", "g_mlp_post"]
    big_names = ["w_in", "w_branch_pool", "w_branch_attn", "w_out", "w_up", "w_down"]
    lead = lambda a: a[None]
    tables = []
    for small_t, big_t in ((sm_gr, big_g), (sm_d, big_d), (sm_m2, big_m2), (sm_v2, big_v2)):
        bt = dict(zip(big_names, big_t))
        tables.append([lead(bt[k]) if k in bt else small_t[k] for k in order])
    return (loss, gx.reshape(B, S, D_MODEL), *tables[0], *tables[1], *tables[2], *tables[3])
```

```python
import jax
import jax.numpy as jnp
from jax import lax
from jax.experimental import pallas as pl
from jax.experimental.pallas import tpu as pltpu

F32 = jnp.float32
MXU_DTYPE = jnp.bfloat16
MESH = pl.DeviceIdType.MESH

D_MODEL = 1024
POOL_WINDOWS = (2, 4, 8, 16)
POOL_WIDTH = 512
POOL_GC = 128
HEAD_DIM = 64
N_Q_HEADS = 8
N_KV_HEADS = 2
GROUP = 4
ATTN_WIDTH = 512
KV_WIDTH = 128
BLOCK = 128
GATE_WIDTH = 2048
IN_WIDTH = 3328
D_FF = 4096
EPS = 1e-6
NEG_INF = -1e30
ROPE_THETA = 500000.0
ROT_DIM = 16
SCALE = HEAD_DIM ** -0.5
C_Q, C_K, C_V, C_G = 512, 1024, 1152, 1280

ADAM_LR = 0.001
ADAM_B1 = 0.9
ADAM_B2 = 0.999
ADAM_EPS = 1e-08
ADAM_WD = 0.01
ADAM_STEP = 10

N_DEV = 8
LANES = 128
VMEM_LIMIT = 56 * 1024 * 1024

NN = (((1,), (0,)), ((), ()))
NT = (((1,), (1,)), ((), ()))
TN = (((0,), (0,)), ((), ()))


def _dot(a, b, dims):
    return lax.dot_general(a, b, dims, preferred_element_type=F32)


def _params(sem=None):
    return pltpu.CompilerParams(dimension_semantics=sem, vmem_limit_bytes=VMEM_LIMIT)


def _tile(n, pref):
    t = min(n, pref)
    assert n % t == 0, (n, t)
    return t


class _Rider:
    def __init__(self, ins, out_shape, n_remote, n_local, plan, aliases=None, lands=None):
        self.ins, self.out_shape, self.n_remote, self.n_local = list(ins), list(out_shape), n_remote, n_local
        self.plan, self.aliases = plan, dict(aliases or {})
        self.lands = lands


def _after(token, rider=None):
    r = rider or _Rider([], [], 0, 0, lambda ins, outs, send, recv, loc, r0, l0: ([], []))
    return _Rider(r.ins + [token], r.out_shape, r.n_remote, r.n_local, r.plan, r.aliases)


def _launch(body, args, *, name, grid, in_specs, out_specs, out_shape, scratch_shapes=(), sem=None, rider=None):
    if rider is None:
        return pl.pallas_call(body, name=name, grid=grid, in_specs=in_specs, out_specs=out_specs, out_shape=out_shape,
                              scratch_shapes=list(scratch_shapes), compiler_params=_params(sem))(*args)
    n_in, n_out, n_scr = len(args), len(out_shape), len(scratch_shapes)
    r_in, r_out = len(rider.ins), len(rider.out_shape)
    copies = rider.n_remote + rider.n_local > 0

    def wrapped(*refs):
        ins, rins = refs[:n_in], refs[n_in:n_in + r_in]
        o0 = n_in + r_in
        outs, routs = refs[o0:o0 + n_out], refs[o0 + n_out:o0 + n_out + r_out]
        s0 = o0 + n_out + r_out
        scr = refs[s0:s0 + n_scr]
        if not copies:
            return body(*ins, *outs, *scr)
        send, recv, loc = refs[s0 + n_scr:]
        first, last = None, None
        for d in range(len(grid)):
            f, l = pl.program_id(d) == 0, pl.program_id(d) == pl.num_programs(d) - 1
            first = f if first is None else first & f
            last = l if last is None else last & l

        def start():
            remote, local = rider.plan(rins, routs, send, recv, loc, 0, 0)
            for cp in local + remote:
                cp.start()

        def finish():
            remote, local = rider.plan(rins, routs, send, recv, loc, 0, 0)
            for cp in remote + local:
                cp.wait()

        if first is None:
            start()
            body(*ins, *outs, *scr)
            finish()
        else:
            pl.when(first)(start)
            body(*ins, *outs, *scr)
            pl.when(last)(finish)

    hbm = pl.BlockSpec(memory_space=pl.ANY)
    dma = pltpu.SemaphoreType.DMA
    res = pl.pallas_call(
        wrapped, name=name, grid=grid, in_specs=list(in_specs) + [hbm] * r_in,
        out_specs=list(out_specs) + [hbm] * r_out, out_shape=list(out_shape) + rider.out_shape,
        scratch_shapes=list(scratch_shapes) + (
            [dma((max(rider.n_remote, 1),)), dma((max(rider.n_remote, 1),)), dma((max(rider.n_local, 1),))] if copies else []),
        input_output_aliases={n_in + i: n_out + o for i, o in rider.aliases.items()},
        compiler_params=_params(sem),
    )(*args, *rider.ins)
    return list(res[:n_out]), list(res[n_out:])


def _comm_call(rider, name):
    return _launch(lambda: None, [], name=name, grid=(), in_specs=[], out_specs=[], out_shape=[], rider=rider)[1]


_HBM = pl.BlockSpec(memory_space=pltpu.HBM)
_SEM = pl.BlockSpec(memory_space=pltpu.SEMAPHORE)
_EFFECT = pltpu.SideEffectType.DATAFLOW_SIDE_EFFECTING


def _copies_start(riders, name, after=None):
    assert all(r.n_local == 0 and not r.aliases for r in riders)
    extra = [] if after is None else [after]
    sizes = [(len(r.ins), len(r.out_shape)) for r in riders]
    bufs = []
    for r in riders:
        lands = r.lands or [lax.empty(s.shape, s.dtype) for s in r.out_shape]
        bufs += [pltpu.with_memory_space_constraint(a, pltpu.HBM) for a in list(r.ins) + list(lands)]
    nb, ng, ne = len(bufs), len(riders), len(extra)

    def body(*refs):
        sems, token, at = refs[2 * nb + ne:2 * nb + ne + 2 * ng], refs[-1], 0
        for g, (r, (ni, no)) in enumerate(zip(riders, sizes)):
            remote, _ = r.plan(refs[at:at + ni], refs[at + ni:at + ni + no], sems[2 * g], sems[2 * g + 1], None, 0, 0)
            for cp in remote:
                cp.start()
            at += ni + no
        token[...] = jnp.zeros_like(token)

    res = pl.pallas_call(
        body, name=name, in_specs=[_HBM] * nb + [pl.BlockSpec(memory_space=pl.ANY)] * ne,
        out_specs=[_HBM] * nb + [_SEM] * (2 * ng) + [pl.BlockSpec(memory_space=pltpu.VMEM)],
        out_shape=[pltpu.HBM(a.shape, a.dtype) for a in bufs]
        + [pltpu.SemaphoreType.DMA((r.n_remote,)) for r in riders for _ in range(2)]
        + [jax.ShapeDtypeStruct((8, LANES), F32)],
        input_output_aliases={i: i for i in range(nb)},
        compiler_params=pltpu.CompilerParams(has_side_effects=_EFFECT),
    )(*bufs, *extra)
    handles, at = [], 0
    for g, (r, (ni, no)) in enumerate(zip(riders, sizes)):
        handles.append((r, list(res[at:at + ni + no]), res[nb + 2 * g], res[nb + 2 * g + 1]))
        at += ni + no
    return handles, res[-1]


def _copies_wait(handles, after, name):
    bufs = [b for _, bs, _, _ in handles for b in bs]
    sems = [s for _, _, send, recv in handles for s in (send, recv)]
    nb, ng = len(bufs), len(handles)

    def body(*refs):
        at = 0
        for g, (rider, bs, _, _) in enumerate(handles):
            ni = len(rider.ins)
            remote, _ = rider.plan(refs[at:at + ni], refs[at + ni:at + len(bs)], refs[nb + 2 * g], refs[nb + 2 * g + 1],
                                   None, 0, 0)
            for cp in remote:
                cp.wait_send()
                cp.wait_recv()
            at += len(bs)

    res = pl.pallas_call(
        body, name=name, in_specs=[_HBM] * nb + [_SEM] * (2 * ng) + [pl.BlockSpec(memory_space=pl.ANY)],
        out_specs=[_HBM] * nb, out_shape=[pltpu.HBM(a.shape, a.dtype) for a in bufs],
        input_output_aliases={i: i for i in range(nb)},
        compiler_params=pltpu.CompilerParams(has_side_effects=_EFFECT),
    )(*bufs, *sems, after)
    lands, at = [], 0
    for rider, bs, _, _ in handles:
        lands += list(res[at + len(rider.ins):at + len(bs)])
        at += len(bs)
    return lands


def _rms_r(x):
    return lax.rsqrt(jnp.mean(x * x, axis=-1, keepdims=True) + EPS)


def _rms_bwd(dn, x, r, g):
    xh = x * r
    dxh = dn * g
    dx = r * (dxh - xh * jnp.mean(dxh * xh, axis=-1, keepdims=True))
    return dx, dn * xh


def _rot(t, c, sa, sb):
    outs = []
    for j in range(t.shape[1] // LANES):
        tj = t[:, LANES * j:LANES * (j + 1)]
        outs.append(tj * c + pltpu.roll(tj, LANES - 8, 1) * sa + pltpu.roll(tj, 8, 1) * sb)
    return outs[0] if len(outs) == 1 else jnp.concatenate(outs, axis=1)


def _rot_tables(S):
    pos = jnp.arange(S, dtype=F32)
    inv_freq = ROPE_THETA ** (-jnp.arange(0, ROT_DIM, 2, dtype=F32) / ROT_DIM)
    ang = pos[:, None] * inv_freq[None, :]
    cos, sin = jnp.cos(ang), jnp.sin(ang)
    one = jnp.ones((S, HEAD_DIM - ROT_DIM), F32)
    zero = jnp.zeros((S, HEAD_DIM - ROT_DIM), F32)
    z8 = jnp.zeros((S, 8), F32)
    c = jnp.concatenate([cos, cos, one], axis=1)
    sa = jnp.concatenate([-sin, z8, zero], axis=1)
    sb = jnp.concatenate([z8, sin, zero], axis=1)
    rep = LANES // HEAD_DIM
    return jnp.tile(c, (1, rep)), jnp.tile(sa, (1, rep)), jnp.tile(sb, (1, rep))


def _lane_tile4(k):
    lane = lax.broadcasted_iota(jnp.int32, k.shape, 1)
    rk = pltpu.roll(k, HEAD_DIM, 1)
    x0 = jnp.where(lane < HEAD_DIM, k, rk)
    x1 = jnp.where(lane < HEAD_DIM, rk, k)
    return jnp.concatenate([x0, x0, x1, x1], axis=1)


def _fold_heads(acc):
    zs = []
    for hk in range(N_KV_HEADS):
        a = acc[:, 256 * hk:256 * hk + LANES] + acc[:, 256 * hk + LANES:256 * (hk + 1)]
        zs.append(a + pltpu.roll(a, HEAD_DIM, 1))
    lane = lax.broadcasted_iota(jnp.int32, zs[0].shape, 1)
    return jnp.where(lane < HEAD_DIM, zs[0], zs[1])


def _inproj_call(x, g1, win_t, b_in, rc, rsa, rsb, S, rider=None):
    T = x.shape[0]
    tm = _tile(S, 512)
    nst = S // tm

    def body(x_ref, g1_ref, w_ref, b_ref, c_ref, sa_ref, sb_ref,
             h_ref, u_ref, q_ref, k4_ref, v4_ref, g_ref):
        xv = x_ref[...]
        hb = ((xv * _rms_r(xv)) * g1_ref[...]).astype(MXU_DTYPE)
        h_ref[...] = hb

        def proj(lo, hi):
            return _dot(hb, w_ref[lo:hi, :], NT) + b_ref[:, lo:hi]

        c, sa, sb = c_ref[...], sa_ref[...], sb_ref[...]
        u_ref[...] = proj(0, C_Q)
        q_ref[...] = (_rot(proj(C_Q, C_K), c, sa, sb) * SCALE).astype(MXU_DTYPE)
        kv = proj(C_K, C_G)
        k4_ref[...] = _lane_tile4(_rot(kv[:, :KV_WIDTH], c, sa, sb)).astype(MXU_DTYPE)
        v4_ref[...] = _lane_tile4(kv[:, KV_WIDTH:]).astype(MXU_DTYPE)
        g_ref[...] = jax.nn.sigmoid(proj(C_G, IN_WIDTH)).astype(MXU_DTYPE)

    tok = lambda w: pl.BlockSpec((tm, w), lambda i: (i, 0))
    full = lambda a: pl.BlockSpec(a.shape, lambda i: (0,) * a.ndim)
    tab = pl.BlockSpec((tm, LANES), lambda i: (i % nst, 0))
    return _launch(
        body, [x, g1, win_t, b_in, rc, rsa, rsb], name="inproj_fwd", grid=(T // tm,),
        in_specs=[tok(D_MODEL), full(g1), full(win_t), full(b_in), tab, tab, tab],
        out_specs=[tok(D_MODEL), tok(POOL_WIDTH), tok(ATTN_WIDTH), tok(512), tok(512), tok(GATE_WIDTH)],
        out_shape=[jax.ShapeDtypeStruct((T, D_MODEL), MXU_DTYPE), jax.ShapeDtypeStruct((T, POOL_WIDTH), F32),
                   jax.ShapeDtypeStruct((T, ATTN_WIDTH), MXU_DTYPE), jax.ShapeDtypeStruct((T, 512), MXU_DTYPE),
                   jax.ShapeDtypeStruct((T, 512), MXU_DTYPE), jax.ShapeDtypeStruct((T, GATE_WIDTH), MXU_DTYPE)],
        sem=("arbitrary",), rider=rider)


def _shift_rows(a, k, rows):
    n = a.shape[0]
    if k > 0:
        return jnp.where(rows >= k, pltpu.roll(a, k, 0), 0.0)
    return jnp.where(rows < n + k, pltpu.roll(a, n + k, 0), 0.0)


def _win_sum(a, w, rows, sign):
    s, k = a, 1
    while k < w:
        s = s + _shift_rows(s, sign * k, rows)
        k *= 2
    return s


def _pool_diff(ug, w, rows):
    inv = 1.0 / jnp.minimum(rows + 1, w).astype(F32)
    return _win_sum(ug, w, rows, 1) * inv - ug, inv


def _pool_call(u, w_pool, pool_scale, S):
    T = u.shape[0]

    def body(u_ref, w_ref, ps_ref, y_ref):
        rows = lax.broadcasted_iota(jnp.int32, (S, POOL_GC), 0)
        for gi, w in enumerate(POOL_WINDOWS):
            sl = slice(POOL_GC * gi, POOL_GC * (gi + 1))
            diff, _ = _pool_diff(u_ref[:, sl], w, rows)
            mixed = _dot(diff.astype(MXU_DTYPE), w_ref[gi], NN)
            y_ref[:, sl] = (mixed * ps_ref[:, sl]).astype(MXU_DTYPE)

    seq = pl.BlockSpec((S, POOL_WIDTH), lambda b: (b, 0))
    return pl.pallas_call(
        body, name="pool_fwd", grid=(T // S,),
        in_specs=[seq, pl.BlockSpec(w_pool.shape, lambda b: (0, 0, 0)), pl.BlockSpec(pool_scale.shape, lambda b: (0, 0))],
        out_specs=seq, out_shape=jax.ShapeDtypeStruct((T, POOL_WIDTH), MXU_DTYPE),
        compiler_params=_params(("arbitrary",)),
    )(u, w_pool, pool_scale)


def _pool_bwd_call(u, dyp, w_pool, pool_scale, S, rider=None):
    T = u.shape[0]

    def body(u_ref, dy_ref, w_ref, ps_ref, du_ref, dw_ref, dps_ref):
        @pl.when(pl.program_id(0) == 0)
        def _():
            dw_ref[...] = jnp.zeros_like(dw_ref)
            dps_ref[...] = jnp.zeros_like(dps_ref)

        rows = lax.broadcasted_iota(jnp.int32, (S, POOL_GC), 0)
        for gi, w in enumerate(POOL_WINDOWS):
            sl = slice(POOL_GC * gi, POOL_GC * (gi + 1))
            diff, inv = _pool_diff(u_ref[:, sl], w, rows)
            diffb = diff.astype(MXU_DTYPE)
            wg = w_ref[gi]
            mixed = _dot(diffb, wg, NN)
            dy = dy_ref[:, sl]
            dps_ref[:, sl] += jnp.sum(dy * mixed, axis=0, keepdims=True)
            dmb = (dy * ps_ref[:, sl]).astype(MXU_DTYPE)
            dw_ref[gi] += _dot(diffb, dmb, TN)
            ddiff = _dot(dmb, wg, NT)
            du_ref[:, sl] = (_win_sum(ddiff * inv, w, rows, -1) - ddiff).astype(MXU_DTYPE)

    seq = pl.BlockSpec((S, POOL_WIDTH), lambda b: (b, 0))
    return _launch(
        body, [u, dyp, w_pool, pool_scale], name="pool_bwd", grid=(T // S,),
        in_specs=[seq, seq, pl.BlockSpec(w_pool.shape, lambda b: (0, 0, 0)), pl.BlockSpec(pool_scale.shape, lambda b: (0, 0))],
        out_specs=[seq, pl.BlockSpec(w_pool.shape, lambda b: (0, 0, 0)), pl.BlockSpec(pool_scale.shape, lambda b: (0, 0))],
        out_shape=[jax.ShapeDtypeStruct((T, POOL_WIDTH), MXU_DTYPE), jax.ShapeDtypeStruct(w_pool.shape, F32),
                   jax.ShapeDtypeStruct(pool_scale.shape, F32)],
        sem=("arbitrary",), rider=rider)


def _attn_consts():
    lane_g = lax.broadcasted_iota(jnp.int32, (BLOCK, 256), 1) >> 6
    rgrp = lax.broadcasted_iota(jnp.int32, (GROUP * BLOCK, 1), 0) >> 7
    rel = lax.broadcasted_iota(jnp.int32, (BLOCK, 256), 0) - lax.broadcasted_iota(jnp.int32, (BLOCK, 256), 1)

    def bias(off):
        ok = (rel + off >= 0) & (rel + off < BLOCK)
        return jnp.concatenate([jnp.where(ok, 0.0, NEG_INF)] * GROUP, axis=0)

    return lane_g, rgrp, bias(0), bias(BLOCK)


def _sink_rows(sink_ref, hk, rgrp):
    sv = jnp.zeros(rgrp.shape, F32)
    for g in range(GROUP):
        sv = jnp.where(rgrp == g, sink_ref[0, GROUP * hk + g], sv)
    return sv


def _stack_heads(xb, lane_g):
    return jnp.concatenate([jnp.where(lane_g == g, xb, jnp.zeros_like(xb)) for g in range(GROUP)], axis=0)


def _unstack_heads(xs, lane_g):
    out = jnp.where(lane_g == 0, xs[0:BLOCK], 0.0)
    for g in range(1, GROUP):
        out = out + jnp.where(lane_g == g, xs[BLOCK * g:BLOCK * (g + 1)], 0.0)
    return out


def _attn_probs(qs, kb, bias, sv):
    s = _dot(qs, kb, NT) + bias
    m = jnp.maximum(jnp.max(s, axis=1, keepdims=True), sv)
    e = jnp.exp(s - m)
    es = jnp.exp(sv - m)
    inv_l = 1.0 / (jnp.sum(e, axis=1, keepdims=True) + es)
    return e * inv_l, es * inv_l


def _attn_blocks(nb, blk, carry, per=1):
    carry = blk(0, 0, True, carry)
    per = per if (nb - 1) % per == 0 else 1

    def step(i, c):
        for k in range(per):
            n = 1 + per * i + k
            c = blk(pl.multiple_of(n * BLOCK, BLOCK), pl.multiple_of((n - 1) * BLOCK, BLOCK), False, c)
        return c

    return lax.fori_loop(0, (nb - 1) // per, step, carry)


def _attn_call(sinks, q, k4, v4, S, rider=None):
    T = q.shape[0]
    nb = S // BLOCK

    def body(sink_ref, q_ref, k_ref, v_ref, o_ref):
        lane_g, rgrp, bias_first, bias_later = _attn_consts()
        svs = [_sink_rows(sink_ref, hk, rgrp) for hk in range(N_KV_HEADS)]

        def blk(q0, k0, first, carry):
            for hk in range(N_KV_HEADS):
                cs = slice(256 * hk, 256 * (hk + 1))
                qs = _stack_heads(q_ref[pl.ds(q0, BLOCK), cs], lane_g)
                p, _ = _attn_probs(qs, k_ref[pl.ds(k0, 2 * BLOCK), cs], bias_first if first else bias_later, svs[hk])
                o = _dot(p.astype(MXU_DTYPE), v_ref[pl.ds(k0, 2 * BLOCK), cs], NN)
                o_ref[pl.ds(q0, BLOCK), cs] = _unstack_heads(o, lane_g).astype(MXU_DTYPE)
            return carry

        _attn_blocks(nb, blk, 0, per=3)

    seq = pl.BlockSpec((S, ATTN_WIDTH), lambda b: (b, 0))
    return _launch(
        body, [sinks, q, k4, v4], name="attn_fwd", grid=(T // S,),
        in_specs=[pl.BlockSpec(memory_space=pltpu.SMEM), seq, seq, seq],
        out_specs=[seq], out_shape=[jax.ShapeDtypeStruct((T, ATTN_WIDTH), MXU_DTYPE)],
        sem=("arbitrary",), rider=rider)


def _attn_bwd_call(sinks, q, k4, v4, do, rc, rsa, rsb, S, rider=None):
    T = q.shape[0]
    nb = S // BLOCK

    def body(sink_ref, q_ref, k_ref, v_ref, do_ref, c_ref, sa_ref, sb_ref,
             dq_ref, dk_ref, dv_ref, ds_ref, dk_acc, dv_acc):
        lane_g, rgrp, bias_first, bias_later = _attn_consts()
        svs = [_sink_rows(sink_ref, hk, rgrp) for hk in range(N_KV_HEADS)]
        lane1 = lax.broadcasted_iota(jnp.int32, (1, LANES), 1)
        dk_acc[...] = jnp.zeros_like(dk_acc)
        dv_acc[...] = jnp.zeros_like(dv_acc)

        def blk(q0, k0, first, dsink):
            rows = pl.ds(q0, BLOCK)
            c, sa, sb = c_ref[rows, :], sa_ref[rows, :], sb_ref[rows, :]
            for hk in range(N_KV_HEADS):
                cs = slice(256 * hk, 256 * (hk + 1))
                qs = _stack_heads(q_ref[rows, cs], lane_g)
                dos = _stack_heads(do_ref[rows, cs], lane_g)
                kb = k_ref[pl.ds(k0, 2 * BLOCK), cs]
                vb = v_ref[pl.ds(k0, 2 * BLOCK), cs]
                p, ps = _attn_probs(qs, kb, bias_first if first else bias_later, svs[hk])
                dp = _dot(dos, vb, NT)
                delta = jnp.sum(p * dp, axis=1, keepdims=True)
                dsb = (p * (dp - delta)).astype(MXU_DTYPE)
                dqb = _unstack_heads(_dot(dsb, kb, NN), lane_g) * SCALE
                dq_ref[rows, cs] = _rot(dqb, c, -sa, -sb).astype(MXU_DTYPE)
                dk_acc[pl.ds(k0, 2 * BLOCK), cs] += _dot(dsb, qs, TN)
                dv_acc[pl.ds(k0, 2 * BLOCK), cs] += _dot(p.astype(MXU_DTYPE), dos, TN)
                psd = ps * delta
                for g in range(GROUP):
                    val = -jnp.sum(psd[BLOCK * g:BLOCK * (g + 1)], axis=0, keepdims=True)
                    dsink = dsink + jnp.where(lane1 == GROUP * hk + g, val, 0.0)
            return dsink

        dsink = _attn_blocks(nb, blk, jnp.zeros((1, LANES), F32))
        dk_ref[...] = _rot(_fold_heads(dk_acc[...]), c_ref[...], -sa_ref[...], -sb_ref[...]).astype(MXU_DTYPE)
        dv_ref[...] = _fold_heads(dv_acc[...]).astype(MXU_DTYPE)
        ds_ref[...] = jnp.broadcast_to(dsink, ds_ref.shape)

    seq = pl.BlockSpec((S, ATTN_WIDTH), lambda b: (b, 0))
    kvs = pl.BlockSpec((S, KV_WIDTH), lambda b: (b, 0))
    tab = pl.BlockSpec((S, LANES), lambda b: (0, 0))
    nseq = T // S
    return _launch(
        body, [sinks, q, k4, v4, do, rc, rsa, rsb], name="attn_bwd", grid=(nseq,),
        in_specs=[pl.BlockSpec(memory_space=pltpu.SMEM), seq, seq, seq, seq, tab, tab, tab],
        out_specs=[seq, kvs, kvs, pl.BlockSpec((8, LANES), lambda b: (b, 0))],
        out_shape=[jax.ShapeDtypeStruct((T, ATTN_WIDTH), MXU_DTYPE), jax.ShapeDtypeStruct((T, KV_WIDTH), MXU_DTYPE),
                   jax.ShapeDtypeStruct((T, KV_WIDTH), MXU_DTYPE), jax.ShapeDtypeStruct((8 * nseq, LANES), F32)],
        scratch_shapes=[pltpu.VMEM((S, 512), F32), pltpu.VMEM((S, 512), F32)],
        sem=("arbitrary",), rider=rider)


def _branch_weights(wbp_ref, wba_ref, wbp_s, wba_s):
    @pl.when(pl.program_id(0) == 0)
    def _():
        for j in range(N_DEV):
            wbp_s[:, LANES * j:LANES * (j + 1)] = wbp_ref[j]
            wba_s[:, LANES * j:LANES * (j + 1)] = wba_ref[j]


def _mix_fwd_call(yp, ya, g, x, wbp, wba, wout, g2, g3, rider=None):
    T = x.shape[0]
    tm = _tile(T, 512)

    def body(yp_ref, ya_ref, g_ref, x_ref, wbp_ref, wba_ref, wout_ref, g2_ref, g3_ref,
             mix_ref, x1_ref, h2_ref, wbp_s, wba_s):
        _branch_weights(wbp_ref, wba_ref, wbp_s, wba_s)
        bp = _dot(yp_ref[...], wbp_s[...], NN)
        ba = _dot(ya_ref[...], wba_s[...], NN)
        merged = g_ref[:, :D_MODEL].astype(F32) * bp + g_ref[:, D_MODEL:].astype(F32) * ba
        mix = _dot(merged.astype(MXU_DTYPE), wout_ref[...], NN)
        mix_ref[...] = mix
        x1 = x_ref[...] + (mix * _rms_r(mix)) * g2_ref[...]
        x1_ref[...] = x1
        h2_ref[...] = ((x1 * _rms_r(x1)) * g3_ref[...]).astype(MXU_DTYPE)

    tok = lambda w: pl.BlockSpec((tm, w), lambda i: (i, 0))
    full = lambda a: pl.BlockSpec(a.shape, lambda i: (0,) * a.ndim)
    return _launch(
        body, [yp, ya, g, x, wbp, wba, wout, g2, g3], name="mix_fwd", grid=(T // tm,),
        in_specs=[tok(POOL_WIDTH), tok(ATTN_WIDTH), tok(GATE_WIDTH), tok(D_MODEL), full(wbp), full(wba), full(wout),
                  full(g2), full(g3)],
        out_specs=[tok(D_MODEL), tok(D_MODEL), tok(D_MODEL)],
        out_shape=[jax.ShapeDtypeStruct((T, D_MODEL), F32), jax.ShapeDtypeStruct((T, D_MODEL), F32),
                   jax.ShapeDtypeStruct((T, D_MODEL), MXU_DTYPE)],
        scratch_shapes=[pltpu.VMEM((POOL_WIDTH, D_MODEL), MXU_DTYPE), pltpu.VMEM((ATTN_WIDTH, D_MODEL), MXU_DTYPE)],
        sem=("arbitrary",), rider=rider)


def _mix_bwd_call(dx1, mix, yp, ya, g, wbp, wba, wout, g2, rider=None):
    T = dx1.shape[0]
    tm = _tile(T, 512)

    def body(dx1_ref, mix_ref, yp_ref, ya_ref, g_ref, wbp_ref, wba_ref, wout_ref, g2_ref,
             dyp_ref, do_ref, dgates_ref, dg2_ref, dbg_ref, gout_ref, gbp_ref, gba_ref,
             wbp_s, wba_s, acc_out, acc_bp, acc_ba, sem):
        _branch_weights(wbp_ref, wba_ref, wbp_s, wba_s)
        step = pl.program_id(0)

        @pl.when(step == 0)
        def _():
            dg2_ref[...] = jnp.zeros_like(dg2_ref)
            dbg_ref[...] = jnp.zeros_like(dbg_ref)
            acc_out[...] = jnp.zeros_like(acc_out)
            acc_bp[...] = jnp.zeros_like(acc_bp)
            acc_ba[...] = jnp.zeros_like(acc_ba)

        mix = mix_ref[...]
        dmix, dg2 = _rms_bwd(dx1_ref[...], mix, _rms_r(mix), g2_ref[...])
        dg2_ref[...] += jnp.sum(dg2, axis=0, keepdims=True)
        dmixb = dmix.astype(MXU_DTYPE)
        dmerged = _dot(dmixb, wout_ref[...], NT)
        yp, ya = yp_ref[...], ya_ref[...]
        bp = _dot(yp, wbp_s[...], NN)
        ba = _dot(ya, wba_s[...], NN)
        gp, ga = g_ref[:, :D_MODEL].astype(F32), g_ref[:, D_MODEL:].astype(F32)
        acc_out[...] += _dot((gp * bp + ga * ba).astype(MXU_DTYPE), dmixb, TN)
        dgp = dmerged * bp * (gp * (1.0 - gp))
        dga = dmerged * ba * (ga * (1.0 - ga))
        dbg_ref[:, :D_MODEL] += jnp.sum(dgp, axis=0, keepdims=True)
        dbg_ref[:, D_MODEL:] += jnp.sum(dga, axis=0, keepdims=True)
        dgates_ref[:, :D_MODEL] = dgp.astype(MXU_DTYPE)
        dgates_ref[:, D_MODEL:] = dga.astype(MXU_DTYPE)
        dbp = (dmerged * gp).astype(MXU_DTYPE)
        dba = (dmerged * ga).astype(MXU_DTYPE)
        acc_bp[...] += _dot(yp, dbp, TN)
        acc_ba[...] += _dot(ya, dba, TN)
        dyp_ref[...] = _dot(dbp, wbp_s[...], NT)
        do_ref[...] = _dot(dba, wba_s[...], NT).astype(MXU_DTYPE)

        @pl.when(step == pl.num_programs(0) - 1)
        def _():
            copies = [pltpu.make_async_copy(acc_out, gout_ref, sem.at[0])]
            for j in range(N_DEV):
                cols = slice(LANES * j, LANES * (j + 1))
                copies.append(pltpu.make_async_copy(acc_bp.at[:, cols], gbp_ref.at[j], sem.at[1 + j]))
                copies.append(pltpu.make_async_copy(acc_ba.at[:, cols], gba_ref.at[j], sem.at[1 + N_DEV + j]))
            for cp in copies:
                cp.start()
            for cp in copies:
                cp.wait()

    tok = lambda w: pl.BlockSpec((tm, w), lambda i: (i, 0))
    full = lambda a: pl.BlockSpec(a.shape, lambda i: (0,) * a.ndim)
    acc = lambda w: pl.BlockSpec((1, w), lambda i: (0, 0))
    hbm = pl.BlockSpec(memory_space=pl.ANY)
    sd = jax.ShapeDtypeStruct
    return _launch(
        body, [dx1, mix, yp, ya, g, wbp, wba, wout, g2], name="mix_bwd", grid=(T // tm,),
        in_specs=[tok(D_MODEL), tok(D_MODEL), tok(POOL_WIDTH), tok(ATTN_WIDTH), tok(GATE_WIDTH), full(wbp), full(wba),
                  full(wout), full(g2)],
        out_specs=[tok(POOL_WIDTH), tok(ATTN_WIDTH), tok(GATE_WIDTH), acc(D_MODEL), acc(GATE_WIDTH), hbm, hbm, hbm],
        out_shape=[sd((T, POOL_WIDTH), F32), sd((T, ATTN_WIDTH), MXU_DTYPE), sd((T, GATE_WIDTH), MXU_DTYPE),
                   sd((1, D_MODEL), F32), sd((1, GATE_WIDTH), F32), sd((D_MODEL, D_MODEL), F32),
                   sd((N_DEV, POOL_WIDTH, LANES), F32), sd((N_DEV, ATTN_WIDTH, LANES), F32)],
        scratch_shapes=[pltpu.VMEM((POOL_WIDTH, D_MODEL), MXU_DTYPE), pltpu.VMEM((ATTN_WIDTH, D_MODEL), MXU_DTYPE),
                        pltpu.VMEM((D_MODEL, D_MODEL), F32), pltpu.VMEM((POOL_WIDTH, D_MODEL), F32),
                        pltpu.VMEM((ATTN_WIDTH, D_MODEL), F32), pltpu.SemaphoreType.DMA((1 + 2 * N_DEV,))],
        sem=("arbitrary",), rider=rider)


def _mlp_up_call(h2, wup):
    T = h2.shape[0]
    tm = _tile(T, 512)
    fc = D_FF // N_DEV

    def body(h2_ref, wup_ref, act_ref):
        h2 = h2_ref[...]
        for j in range(N_DEV):
            rl = jnp.maximum(_dot(h2, wup_ref[j], NN), 0.0)
            act_ref[:, fc * j:fc * (j + 1)] = (rl * rl).astype(MXU_DTYPE)

    sd = jax.ShapeDtypeStruct
    return pl.pallas_call(
        body, name="mlp_up", grid=(T // tm,),
        in_specs=[pl.BlockSpec((tm, D_MODEL), lambda i: (i, 0)),
                  pl.BlockSpec(wup.shape, lambda i: (0, 0, 0), pipeline_mode=pl.Buffered(1))],
        out_specs=pl.BlockSpec((tm, D_FF), lambda i: (i, 0)), out_shape=sd((T, D_FF), MXU_DTYPE),
        compiler_params=_params(("arbitrary",)),
    )(h2, wup)


def _mlp_call(x1, act, target, wup, wdown, g3, g4):
    T = x1.shape[0]
    tm = _tile(T, 256)
    fc = D_FF // N_DEV

    def body(x1_ref, act_ref, t_ref, wup_ref, wdown_ref, g3_ref, g4_ref,
             da_ref, dff_ref, dx1_ref, dg3_ref, dg4_ref, loss_ref):
        @pl.when(pl.program_id(0) == 0)
        def _():
            dg3_ref[...] = jnp.zeros_like(dg3_ref)
            dg4_ref[...] = jnp.zeros_like(dg4_ref)
            loss_ref[...] = jnp.zeros_like(loss_ref)

        ff = jnp.zeros((tm, D_MODEL), F32)
        for j in range(N_DEV):
            ff = ff + _dot(act_ref[:, fc * j:fc * (j + 1)], wdown_ref[j], NN)
        x1 = x1_ref[...]
        r4 = _rms_r(ff)
        err = x1 + (ff * r4) * g4_ref[...] - t_ref[...]
        loss_ref[...] += jnp.sum(err * err, axis=0, keepdims=True)
        dy = err * (1.0 / D_MODEL)
        dff, dg4 = _rms_bwd(dy, ff, r4, g4_ref[...])
        dg4_ref[...] += jnp.sum(dg4, axis=0, keepdims=True)
        dffb = dff.astype(MXU_DTYPE)
        dff_ref[...] = dffb
        dh2 = jnp.zeros((tm, D_MODEL), F32)
        for j in range(N_DEV):
            sl = slice(fc * j, fc * (j + 1))
            rl = jnp.sqrt(act_ref[:, sl].astype(F32))
            dab = (_dot(dffb, wdown_ref[j], NT) * (2.0 * rl)).astype(MXU_DTYPE)
            da_ref[:, sl] = dab
            dh2 = dh2 + _dot(dab, wup_ref[j], NT)
        dx1, dg3 = _rms_bwd(dh2, x1, _rms_r(x1), g3_ref[...])
        dg3_ref[...] += jnp.sum(dg3, axis=0, keepdims=True)
        dx1_ref[...] = dy + dx1

    tok = lambda w: pl.BlockSpec((tm, w), lambda i: (i, 0))
    full = lambda a: pl.BlockSpec(a.shape, lambda i: (0,) * a.ndim, pipeline_mode=pl.Buffered(1))
    vec = pl.BlockSpec((1, D_MODEL), lambda i: (0, 0))
    sd = jax.ShapeDtypeStruct
    return pl.pallas_call(
        body, name="mlp_down_bwd", grid=(T // tm,),
        in_specs=[tok(D_MODEL), tok(D_FF), tok(D_MODEL), full(wup), full(wdown), vec, vec],
        out_specs=[tok(D_FF), tok(D_MODEL), tok(D_MODEL), vec, vec, vec],
        out_shape=[sd((T, D_FF), MXU_DTYPE), sd((T, D_MODEL), MXU_DTYPE),
                   sd((T, D_MODEL), F32), sd((1, D_MODEL), F32), sd((1, D_MODEL), F32), sd((1, D_MODEL), F32)],
        compiler_params=_params(("arbitrary",)),
    )(x1, act, target, wup, wdown, g3, g4)


def _inproj_bwd_call(du, dq, dk, dv, dgates, dx1, x, win_t, g1, rider=None):
    T = x.shape[0]
    tm = _tile(T, 512)

    def body(du_ref, dq_ref, dk_ref, dv_ref, dgt_ref, dx1_ref, x_ref, w_ref, g1_ref, gx_ref, dg1_ref, db_ref):
        @pl.when(pl.program_id(0) == 0)
        def _():
            dg1_ref[...] = jnp.zeros_like(dg1_ref)
            db_ref[...] = jnp.zeros_like(db_ref)

        dh = jnp.zeros((tm, D_MODEL), F32)
        for ref, lo, hi in ((du_ref, 0, C_Q), (dq_ref, C_Q, C_K), (dk_ref, C_K, C_V), (dv_ref, C_V, C_G),
                            (dgt_ref, C_G, IN_WIDTH)):
            piece = ref[...]
            dh = dh + _dot(piece, w_ref[lo:hi, :], NN)
            if hi <= C_G:
                db_ref[:, lo:hi] += jnp.sum(piece.astype(F32), axis=0, keepdims=True)
        xv = x_ref[...]
        dx, dg1 = _rms_bwd(dh, xv, _rms_r(xv), g1_ref[...])
        dg1_ref[...] += jnp.sum(dg1, axis=0, keepdims=True)
        gx_ref[...] = dx1_ref[...] + dx

    tok = lambda w: pl.BlockSpec((tm, w), lambda i: (i, 0))
    full = lambda a: pl.BlockSpec(a.shape, lambda i: (0,) * a.ndim)
    sd = jax.ShapeDtypeStruct
    return _launch(
        body, [du, dq, dk, dv, dgates, dx1, x, win_t, g1], name="inproj_bwd", grid=(T // tm,),
        in_specs=[tok(POOL_WIDTH), tok(ATTN_WIDTH), tok(KV_WIDTH), tok(KV_WIDTH), tok(GATE_WIDTH), tok(D_MODEL),
                  tok(D_MODEL), full(win_t), full(g1)],
        out_specs=[tok(D_MODEL), pl.BlockSpec((1, D_MODEL), lambda i: (0, 0)), pl.BlockSpec((1, C_G), lambda i: (0, 0))],
        out_shape=[sd((T, D_MODEL), F32), sd((1, D_MODEL), F32), sd((1, C_G), F32)],
        sem=("arbitrary",), rider=rider)


WGRAD_TOKENS = 1024


def _wgrad_rows_call(a, b, name, rider=None):
    T, K = a.shape
    N = b.shape[1]
    tm = _tile(T, WGRAD_TOKENS)
    kb = min(K, 1024)
    per = kb // (K // N_DEV)

    def body(a_ref, b_ref, o_ref):
        @pl.when(pl.program_id(1) == 0)
        def _():
            o_ref[...] = jnp.zeros_like(o_ref)

        d = _dot(a_ref[...], b_ref[...], TN)
        rs = kb // per
        for j in range(per):
            o_ref[j] += d[rs * j:rs * (j + 1)]

    return _launch(
        body, [a, b], name=name, grid=(K // kb, T // tm),
        in_specs=[pl.BlockSpec((tm, kb), lambda i, t: (t, i)), pl.BlockSpec((tm, N), lambda i, t: (t, 0))],
        out_specs=[pl.BlockSpec((per, K // N_DEV, N), lambda i, t: (i, 0, 0))],
        out_shape=[jax.ShapeDtypeStruct((N_DEV, K // N_DEV, N), F32)],
        sem=("arbitrary", "arbitrary"), rider=rider)


def _wgrad_cols_call(a, b, name, rider=None):
    T, K = a.shape
    N = b.shape[1]
    tm = _tile(T, WGRAD_TOKENS)
    nb = min(N, 1024)
    per = nb // (N // N_DEV)

    def body(a_ref, b_ref, o_ref):
        @pl.when(pl.program_id(1) == 0)
        def _():
            o_ref[...] = jnp.zeros_like(o_ref)

        d = _dot(a_ref[...], b_ref[...], TN)
        cs = nb // per
        for j in range(per):
            o_ref[j] += d[:, cs * j:cs * (j + 1)]

    return _launch(
        body, [a, b], name=name, grid=(N // nb, T // tm),
        in_specs=[pl.BlockSpec((tm, K), lambda i, t: (t, 0)), pl.BlockSpec((tm, nb), lambda i, t: (t, i))],
        out_specs=[pl.BlockSpec((per, K, N // N_DEV), lambda i, t: (i, 0, 0))],
        out_shape=[jax.ShapeDtypeStruct((N_DEV, K, N // N_DEV), F32)],
        sem=("arbitrary", "arbitrary"), rider=rider)


def _wgrad_in_call(du, dq, dk, dv, dgates, h, rider=None):
    T = h.shape[0]
    tm = _tile(T, WGRAD_TOKENS)
    rows = IN_WIDTH // N_DEV

    def body(du_ref, dq_ref, dk_ref, dv_ref, dgt_ref, h_ref, o_ref, acc, sem):
        t = pl.program_id(0)

        @pl.when(t == 0)
        def _():
            acc[...] = jnp.zeros_like(acc)

        hv = h_ref[...]
        for ref, lo, hi in ((du_ref, 0, C_Q), (dq_ref, C_Q, C_K), (dk_ref, C_K, C_V), (dv_ref, C_V, C_G),
                            (dgt_ref, C_G, IN_WIDTH)):
            acc[lo:hi, :] += _dot(ref[...], hv, TN)

        @pl.when(t == pl.num_programs(0) - 1)
        def _():
            copies = [pltpu.make_async_copy(acc.at[pl.ds(rows * j, rows), :], o_ref.at[j], sem.at[j])
                      for j in range(N_DEV)]
            for cp in copies:
                cp.start()
            for cp in copies:
                cp.wait()

    tok = lambda w: pl.BlockSpec((tm, w), lambda t: (t, 0))
    return _launch(
        body, [du, dq, dk, dv, dgates, h], name="wgrad_in", grid=(T // tm,),
        in_specs=[tok(POOL_WIDTH), tok(ATTN_WIDTH), tok(KV_WIDTH), tok(KV_WIDTH), tok(GATE_WIDTH), tok(D_MODEL)],
        out_specs=[pl.BlockSpec(memory_space=pl.ANY)],
        out_shape=[jax.ShapeDtypeStruct((N_DEV, rows, D_MODEL), F32)],
        scratch_shapes=[pltpu.VMEM((IN_WIDTH, D_MODEL), F32), pltpu.SemaphoreType.DMA((N_DEV,))],
        sem=("arbitrary",), rider=rider)


def _coords():
    return lax.axis_index("x"), lax.axis_index("y"), lax.axis_index("c")


def _allgather_call(shards, bufs):
    n = len(shards)

    def body(*refs):
        ins, outs = refs[:n], refs[2 * n:3 * n]
        send_sems, recv_sems = refs[3 * n:]
        x, y, c = _coords()
        me, sibling = (x, y, c), (x, y, 1 - c)
        chips = [(1 - x, y), (x, 1 - y), (1 - x, 1 - y)]

        def slot(p):
            return 4 * p[0] + 2 * p[1] + p[2]

        def copy(t, k, block, to, src=None):
            dst = outs[t].at[slot(block)]
            return pltpu.make_async_remote_copy(
                src_ref=dst if src is None else src, dst_ref=dst, send_sem=send_sems.at[t, k],
                recv_sem=recv_sems.at[t, k], device_id=to, device_id_type=MESH)

        first = []
        for t in range(n):
            first.append(copy(t, 0, me, sibling, src=ins[t]))
            first += [copy(t, 1 + j, me, (*chip, c), src=ins[t]) for j, chip in enumerate(chips)]
        for cp in first:
            cp.start()
        passed = []
        for t in range(n):
            for j, chip in enumerate(chips):
                copy(t, 1 + j, (*chip, c), me).wait_recv()
                fwd = copy(t, 4 + j, (*chip, c), sibling)
                fwd.start()
                passed.append(fwd)
        for t in range(n):
            copy(t, 0, sibling, me).wait_recv()
            for j, chip in enumerate(chips):
                copy(t, 4 + j, (*chip, 1 - c), me).wait_recv()
        for cp in first + passed:
            cp.wait_send()

    hbm = pl.BlockSpec(memory_space=pl.ANY)
    return pl.pallas_call(
        body, name="allgather_weights",
        in_specs=[hbm] * (2 * n), out_specs=[hbm] * n,
        out_shape=[jax.ShapeDtypeStruct(b.shape, b.dtype) for b in bufs],
        scratch_shapes=[pltpu.SemaphoreType.DMA((n, 7)), pltpu.SemaphoreType.DMA((n, 7))],
        input_output_aliases={n + t: t for t in range(n)},
    )(*shards, *bufs)


def _slot(p):
    return 4 * p[0] + 2 * p[1] + p[2]


def _rows(ref, span):
    return ref if span is None else ref.at[pl.ds(span[0], span[1])]


ALL = "all"
LOCAL = "local"


def _rows(ref, span):
    return ref if span == ALL else ref.at[pl.ds(span[0], span[1])]


def _rider_ag(items):
    ins, out_shape, aliases, where = [], [], {}, []
    n_remote = n_local = 0
    for t, (shard, buf, snd, fwd) in enumerate(items):
        i_shard = i_buf = None
        if snd is not None:
            i_shard = len(ins)
            ins.append(shard)
        if buf is not None:
            i_buf = len(ins)
            ins.append(buf)
            aliases[i_buf] = t
            out_shape.append(jax.ShapeDtypeStruct(buf.shape, buf.dtype))
        else:
            assert fwd is None and snd is not None
            out_shape.append(jax.ShapeDtypeStruct((N_DEV,) + shard.shape, shard.dtype))
        where.append((i_shard, i_buf, n_remote, n_local))
        n_remote += (4 if snd not in (None, LOCAL) else 0) + (3 if fwd is not None else 0)
        n_local += 1 if snd is not None else 0

    def plan(rins, routs, send, recv, loc, r0, l0):
        x, y, c = _coords()
        peers = [(x, y, 1 - c), (1 - x, y, c), (x, 1 - y, c), (1 - x, 1 - y, c)]
        remote, local = [], []
        for t, (shard, buf, snd, fwd) in enumerate(items):
            i_shard, i_buf, k, l = where[t]
            k, l = r0 + k, l0 + l
            if snd is not None:
                span = ALL if snd == LOCAL else snd
                src, dst = _rows(rins[i_shard], span), _rows(routs[t].at[_slot((x, y, c))], span)
                local.append(pltpu.make_async_copy(src, dst, loc.at[l]))
                for peer in (peers if snd != LOCAL else []):
                    remote.append(pltpu.make_async_remote_copy(
                        src_ref=src, dst_ref=dst, send_sem=send.at[k], recv_sem=recv.at[k],
                        device_id=peer, device_id_type=MESH))
                    k += 1
            if fwd is not None:
                for px, py, pc in peers[1:]:
                    s = _slot((px, py, pc))
                    remote.append(pltpu.make_async_remote_copy(
                        src_ref=_rows(rins[i_buf].at[s], fwd), dst_ref=_rows(routs[t].at[s], fwd),
                        send_sem=send.at[k], recv_sem=recv.at[k], device_id=peers[0], device_id_type=MESH))
                    k += 1
        return remote, local

    return _Rider(ins, out_shape, n_remote, n_local, plan, aliases)


def _gather_buffer(shard, me):
    return lax.dynamic_update_slice(lax.empty((N_DEV,) + shard.shape, shard.dtype), shard[None], (me, 0, 0))


def _rider_ag_remote(shards, me):
    n = len(shards)

    def plan(ins, outs, send, recv, loc, r0, l0):
        x, y, c = _coords()
        remote = []
        for t in range(n):
            dst = outs[t].at[_slot((x, y, c))]
            for k, peer in enumerate([(x, y, 1 - c), (1 - x, y, c), (x, 1 - y, c), (1 - x, 1 - y, c)]):
                remote.append(pltpu.make_async_remote_copy(
                    src_ref=ins[t], dst_ref=dst, send_sem=send.at[r0 + 4 * t + k], recv_sem=recv.at[r0 + 4 * t + k],
                    device_id=peer, device_id_type=MESH))
        return remote, []

    return _Rider(shards, [jax.ShapeDtypeStruct((N_DEV,) + s.shape, s.dtype) for s in shards], 4 * n, 0, plan,
                  lands=[_gather_buffer(s, me) for s in shards])


def _rider_rs_sibling(grads, parts=1):
    n = len(grads)

    def plan(ins, outs, send, recv, loc, r0, l0):
        x, y, c = _coords()
        remote, k = [], r0
        for t in range(n):
            rows = ins[t].shape[2] // parts
            for q in range(4):
                for p in range(parts):
                    span = pl.ds(rows * p, rows)
                    remote.append(pltpu.make_async_remote_copy(
                        src_ref=ins[t].at[q, 1 - c, span], dst_ref=outs[t].at[q, span], send_sem=send.at[k],
                        recv_sem=recv.at[k], device_id=(x, y, 1 - c), device_id_type=MESH))
                    k += 1
        return remote, []

    return _Rider(grads, [jax.ShapeDtypeStruct((4,) + g.shape[2:], g.dtype) for g in grads], 4 * parts * n, 0, plan)


def _rider_rs_chips(sums, rows=None, into=None):
    n = len(sums)
    rows = rows or [ALL] * n

    def plan(ins, outs, send, recv, loc, r0, l0):
        x, y, c = _coords()
        remote = []
        for t in range(n):
            for r, (px, py) in enumerate([(1 - x, y), (x, 1 - y), (1 - x, 1 - y)]):
                remote.append(pltpu.make_async_remote_copy(
                    src_ref=_rows(ins[t].at[2 * px + py], rows[t]), dst_ref=_rows(outs[t].at[r], rows[t]),
                    send_sem=send.at[r0 + 3 * t + r], recv_sem=recv.at[r0 + 3 * t + r],
                    device_id=(px, py, c), device_id_type=MESH))
        return remote, []

    out_shape = [jax.ShapeDtypeStruct((3,) + s.shape[1:], s.dtype) for s in sums]
    if into is None:
        return _Rider(sums, out_shape, 3 * n, 0, plan)
    return _Rider(list(sums) + list(into), out_shape, 3 * n, 0, plan, aliases={n + t: t for t in range(n)})


def _rider_gather_remote(parts):
    n = len(parts)

    def plan(ins, outs, send, recv, loc, r0, l0):
        x, y, c = _coords()
        me = _slot((x, y, c))
        remote = []
        for t in range(n):
            for k in range(1, N_DEV):
                peer = (x ^ ((k >> 2) & 1), y ^ ((k >> 1) & 1), c ^ (k & 1))
                remote.append(pltpu.make_async_remote_copy(
                    src_ref=ins[t], dst_ref=outs[t].at[me], send_sem=send.at[r0 + 7 * t + k - 1],
                    recv_sem=recv.at[r0 + 7 * t + k - 1], device_id=peer, device_id_type=MESH))
        return remote, []

    return _Rider(parts, [jax.ShapeDtypeStruct((N_DEV,) + p.shape, p.dtype) for p in parts], 7 * n, 0, plan)


def _chip_sum_call(idx, grads, recvd, out_dtypes, name):
    n = len(grads)

    def body(i_ref, *refs):
        for t in range(n):
            refs[2 * n + t][0] = (refs[t][0, 0] + refs[n + t][0]).astype(out_dtypes[t])

    def chip(k, s):
        return jnp.where(k >= s[0], k + 1, k)

    in_specs = [pl.BlockSpec((1, 1) + g.shape[2:], lambda k, s: (chip(k, s), s[1], 0, 0)) for g in grads]
    in_specs += [pl.BlockSpec((1,) + r.shape[1:], lambda k, s: (chip(k, s), 0, 0)) for r in recvd]
    return pl.pallas_call(
        body, name=name,
        grid_spec=pltpu.PrefetchScalarGridSpec(
            num_scalar_prefetch=1, grid=(3,), in_specs=in_specs,
            out_specs=[pl.BlockSpec((1,) + r.shape[1:], lambda k, s: (chip(k, s), 0, 0)) for r in recvd]),
        out_shape=[jax.ShapeDtypeStruct(r.shape, dt) for r, dt in zip(recvd, out_dtypes)],
        compiler_params=_params(("arbitrary",)),
    )(idx, *grads, *recvd)


def _final_sum_call(idx, grads, recvd1, recvd2):
    n = len(grads)
    nsteps = 2

    def body(i_ref, *refs):
        for t in range(n):
            g, r1, r2, o = refs[t], refs[n + t], refs[2 * n + t], refs[3 * n + t]
            s = g[0, 0] + r1[0]
            for r in range(3):
                s = s + r2[r].astype(F32)
            o[...] = s

    def rows(a):
        r = a.shape[-2]
        return r // nsteps if (r // nsteps) % 16 == 0 else r

    def step(a):
        return (lambda i: i) if rows(a) != a.shape[-2] else (lambda i: 0)

    in_specs = [pl.BlockSpec((1, 1, rows(g), g.shape[3]), lambda i, s, st=step(g): (s[0], s[1], st(i), 0)) for g in grads]
    in_specs += [pl.BlockSpec((1, rows(r), r.shape[2]), lambda i, s, st=step(r): (s[0], st(i), 0)) for r in recvd1]
    in_specs += [pl.BlockSpec((3, rows(r), r.shape[2]), lambda i, s, st=step(r): (0, st(i), 0)) for r in recvd2]
    return pl.pallas_call(
        body, name="rs_final_sum",
        grid_spec=pltpu.PrefetchScalarGridSpec(
            num_scalar_prefetch=1, grid=(nsteps,), in_specs=in_specs,
            out_specs=[pl.BlockSpec((rows(r), r.shape[2]), lambda i, s, st=step(r): (st(i), 0)) for r in recvd2]),
        out_shape=[jax.ShapeDtypeStruct(r.shape[1:], F32) for r in recvd2],
        compiler_params=_params(("arbitrary",)),
    )(idx, *grads, *recvd1, *recvd2)


def _sum8_call(parts):
    def body(p_ref, o_ref):
        s = p_ref[0]
        for j in range(1, N_DEV):
            s = s + p_ref[j]
        o_ref[...] = s

    return pl.pallas_call(body, name="sum_small_partials",
                          out_shape=jax.ShapeDtypeStruct(parts.shape[1:], parts.dtype))(parts)


def _adamw(w, g, m, v):
    m = ADAM_B1 * m + (1.0 - ADAM_B1) * g
    v = ADAM_B2 * v + (1.0 - ADAM_B2) * (g * g)
    m_hat = m / (1.0 - ADAM_B1 ** ADAM_STEP)
    v_hat = v / (1.0 - ADAM_B2 ** ADAM_STEP)
    delta = -ADAM_LR * (m_hat / (jnp.sqrt(v_hat) + ADAM_EPS) + ADAM_WD * w)
    return delta, m, v


def _adamw_call(ws, gs, ms, vs, nsteps, name):
    n = len(ws)

    def body(*refs):
        for t in range(n):
            w, g, m, v = (refs[k * n + t][...] for k in range(4))
            d, m2, v2 = _adamw(w, g, m, v)
            refs[4 * n + t][...] = d
            refs[5 * n + t][...] = m2
            refs[6 * n + t][...] = v2

    def spec(a):
        assert a.shape[0] % nsteps == 0 and (nsteps == 1 or (a.shape[0] // nsteps) % 8 == 0), a.shape
        return pl.BlockSpec((a.shape[0] // nsteps, a.shape[1]), lambda i: (i, 0))

    specs = [spec(a) for a in ws]
    outs = pl.pallas_call(
        body, name=name, grid=(nsteps,),
        in_specs=specs * 4, out_specs=specs * 3,
        out_shape=[jax.ShapeDtypeStruct(a.shape, F32) for a in ws] * 3,
        compiler_params=_params(("arbitrary",)),
    )(*ws, *gs, *ms, *vs)
    return outs[:n], outs[n:2 * n], outs[2 * n:]


def _adamw_rs_call(idx, after, gws, r1s, r2s, ws, ms, vs, nsteps, name):
    n = len(ws)

    def body(i_ref, after_ref, *refs):
        for t in range(n):
            gw, r1, r2, w, m, v = (refs[k * n + t] for k in range(6))
            g = gw[0, 0] + r1[0]
            for r in range(3):
                g = g + r2[r].astype(F32)
            d, m2, v2 = _adamw(w[...], g, m[...], v[...])
            refs[6 * n + t][...] = g
            refs[7 * n + t][...] = d
            refs[8 * n + t][...] = m2
            refs[9 * n + t][...] = v2

    def rb(a):
        r = a.shape[0] // nsteps
        assert a.shape[0] % nsteps == 0 and r % 16 == 0, a.shape
        return r

    in_specs = [pl.BlockSpec((1, 1, rb(w), w.shape[1]), lambda i, s: (s[0], s[1], i, 0)) for w in ws]
    in_specs += [pl.BlockSpec((1, rb(w), w.shape[1]), lambda i, s: (s[0], i, 0)) for w in ws]
    in_specs += [pl.BlockSpec((3, rb(w), w.shape[1]), lambda i, s: (0, i, 0)) for w in ws]
    plain = [pl.BlockSpec((rb(w), w.shape[1]), lambda i, s: (i, 0)) for w in ws]
    outs = pl.pallas_call(
        body, name=name,
        grid_spec=pltpu.PrefetchScalarGridSpec(
            num_scalar_prefetch=1, grid=(nsteps,),
            in_specs=[pl.BlockSpec(memory_space=pl.ANY)] + in_specs + plain * 3, out_specs=plain * 4),
        out_shape=[jax.ShapeDtypeStruct(w.shape, F32) for w in ws] * 4,
        compiler_params=_params(("arbitrary",)),
    )(idx, after, *gws, *r1s, *r2s, *ws, *ms, *vs)
    return outs[:n], outs[n:2 * n], outs[2 * n:3 * n], outs[3 * n:]


def _rows128(a, pad_rows):
    flat = a.reshape(-1).astype(F32)
    flat = jnp.pad(flat, (0, pad_rows * LANES - flat.shape[0]))
    return flat.reshape(pad_rows, LANES)


_SMALL_A = (("w_pool", 512), ("pool_scale", 8), ("attn_sinks", 8), ("g_mix_post", 8), ("g_mlp_pre", 8),
            ("g_mlp_post", 8), ("loss", 8), ("b_in_gates", 16))
_SMALL_A_ROWS = 640
_SMALL_B = (("g_mix_pre", 8), ("b_in_head", 16))


def _pack(parts, layout, total_rows):
    rows = [_rows128(parts[k], r) for k, r in layout]
    pad = total_rows - sum(r for _, r in layout)
    if pad:
        rows.append(jnp.zeros((pad, LANES), F32))
    return jnp.concatenate(rows, axis=0)


def _unpack(buf, layout, sizes):
    out, off = {}, 0
    for k, r in layout:
        out[k] = buf[off:off + r].reshape(-1)[:sizes[k]]
        off += r
    return out


def kernel(x, g_mix_pre, w_in, b_in, w_pool, pool_scale, attn_sinks, w_branch_pool, w_branch_attn, w_out, g_mix_post, g_mlp_pre, w_up, w_down, g_mlp_post, loss_target, m_g_mix_pre, m_w_in, m_b_in, m_w_pool, m_pool_scale, m_attn_sinks, m_w_branch_pool, m_w_branch_attn, m_w_out, m_g_mix_post, m_g_mlp_pre, m_w_up, m_w_down, m_g_mlp_post, v_g_mix_pre, v_w_in, v_b_in, v_w_pool, v_pool_scale, v_attn_sinks, v_w_branch_pool, v_w_branch_attn, v_w_out, v_g_mix_post, v_g_mlp_pre, v_w_up, v_w_down, v_g_mlp_post):
    B, S, _ = x.shape
    T = B * S
    xt = x.reshape(T, D_MODEL)
    tgt = loss_target.reshape(T, D_MODEL)
    cx, cy, cc = _coords()

    cidx = jnp.stack([2 * cx + cy, cc]).astype(jnp.int32)
    by_chip = lambda gr: gr.reshape((4, 2) + gr.shape[1:])
    bf = lambda w: w[0].astype(MXU_DTYPE)

    me = _slot((cx, cy, cc))
    win_l = w_in[0].T.astype(MXU_DTYPE)
    (win_s,) = _allgather_call([win_l], [_gather_buffer(win_l, me)])
    win_t = win_s.reshape(IN_WIDTH, D_MODEL)
    wpool_b = bf(w_pool)
    rc, rsa, rsb = _rot_tables(S)

    wbp_l, wba_l, wout_l, wup_l, wdown_l = bf(w_branch_pool), bf(w_branch_attn), bf(w_out), bf(w_up), bf(w_down)
    (c_br, c_up, c_dn), tok = _copies_start(
        [_rider_ag_remote([wbp_l, wba_l, wout_l], me), _rider_ag_remote([wup_l], me), _rider_ag_remote([wdown_l], me)],
        "allgather_start", after=win_s)
    (h, u, q, k4, v4, g), _ = _inproj_call(xt, g_mix_pre, win_t, b_in, rc, rsa, rsb, S, rider=_after(tok))
    yp = _pool_call(u, wpool_b, pool_scale, S)
    wbp_1, wba_1, wout_1 = _copies_wait([c_br], yp, "allgather_wait_branch")
    (ya,), (wbp_s, wba_s, wout_s) = _attn_call(
        attn_sinks, q, k4, v4, S,
        rider=_rider_ag([(None, wbp_1, None, ALL), (None, wba_1, None, ALL), (None, wout_1, None, ALL)]))
    wout_f = wout_s.reshape(D_MODEL, D_MODEL)
    (wup_1,) = _copies_wait([c_up], ya, "allgather_wait_up")
    (mix, x1, h2), (wup_s,) = _mix_fwd_call(
        yp, ya, g, xt, wbp_s, wba_s, wout_f, g_mix_post, g_mlp_pre, rider=_rider_ag([(None, wup_1, None, ALL)]))
    act = _mlp_up_call(h2, wup_s)
    (wdown_1,) = _copies_wait([c_dn], act, "allgather_wait_down")
    (wdown_s,) = _comm_call(_rider_ag([(None, wdown_1, None, ALL)]), "allgather_pass_down")

    da, dff, dx1, dg3, dg4, lossvec = _mlp_call(x1, act, tgt, wup_s, wdown_s, g_mlp_pre, g_mlp_post)
    gw_down = by_chip(_wgrad_rows_call(act, dff, "wgrad_down")[0])
    (gw_up,), (r1_down,) = _wgrad_cols_call(h2, da, "wgrad_up", rider=_rider_rs_sibling([gw_down]))
    gw_up = by_chip(gw_up)
    (s_down,) = _chip_sum_call(cidx, [gw_down], [r1_down], [MXU_DTYPE], "rs_chip_sum_down")
    (c_down,), tok = _copies_start([_rider_rs_chips([s_down])], "rs_chips_start_down")
    (dyp, do, dgates, dg2, dbg, gw_out, gw_bp, gw_ba), (r1_up,) = _mix_bwd_call(
        dx1, mix, yp, ya, g, wbp_s, wba_s, wout_f, g_mix_post, rider=_after(tok, _rider_rs_sibling([gw_up], parts=4)))
    gw_out = by_chip(gw_out.reshape(N_DEV, D_MODEL // N_DEV, D_MODEL))
    gw_bp, gw_ba = by_chip(gw_bp), by_chip(gw_ba)
    (s_up,) = _chip_sum_call(cidx, [gw_up], [r1_up], [MXU_DTYPE], "rs_chip_sum_up")
    (c_up,), tok = _copies_start([_rider_rs_chips([s_up])], "rs_chips_start_up")
    (dq, dk, dv, dsink), (r1_out, r1_bp, r1_ba) = _attn_bwd_call(
        attn_sinks, q, k4, v4, do, rc, rsa, rsb, S, rider=_after(tok, _rider_rs_sibling([gw_out, gw_bp, gw_ba], parts=4)))
    s_obb = _chip_sum_call(cidx, [gw_out, gw_bp, gw_ba], [r1_out, r1_bp, r1_ba], [MXU_DTYPE] * 3, "rs_chip_sum_branch")
    (c_obb,), tok = _copies_start([_rider_rs_chips(s_obb)], "rs_chips_start_branch")
    (du, dwp, dps), _ = _pool_bwd_call(u, dyp, wpool_b, pool_scale, S, rider=_after(tok))
    (gw_in,) = _wgrad_in_call(du, dq, dk, dv, dgates, h)
    gw_in = by_chip(gw_in)

    small_a = {"w_pool": dwp, "pool_scale": dps,
               "attn_sinks": jnp.sum(dsink.reshape(B, 8, LANES)[:, 0, :N_Q_HEADS], axis=0), "g_mix_post": dg2,
               "g_mlp_pre": dg3, "g_mlp_post": dg4, "loss": lossvec, "b_in_gates": dbg}
    gw_sa = by_chip(_pack(small_a, _SMALL_A, _SMALL_A_ROWS).reshape(N_DEV, _SMALL_A_ROWS // N_DEV, LANES))
    r1_in, r1_sa = _comm_call(_rider_rs_sibling([gw_in, gw_sa]), "rs_sibling_in")
    s_in, s_sa = _chip_sum_call(cidx, [gw_in, gw_sa], [r1_in, r1_sa], [MXU_DTYPE, F32], "rs_chip_sum_in")
    (c_in,), tok = _copies_start([_rider_rs_chips([s_in, s_sa])], "rs_chips_start_in")
    (gx, dg1, dba_in), _ = _inproj_bwd_call(du, dq, dk, dv, dgates, dx1, xt, win_t, g_mix_pre, rider=_after(tok))
    r2_down, r2_up, r2_out, r2_bp, r2_ba, r2_in, r2_sa = _copies_wait([c_down, c_up, c_obb, c_in], dg1, "rs_chips_wait")

    (g_sa,) = _final_sum_call(cidx, [gw_sa], [r1_sa], [r2_sa])
    part_b = _pack({"g_mix_pre": dg1, "b_in_head": dba_in}, _SMALL_B, sum(r for _, r in _SMALL_B))
    (c_small,), tok = _copies_start([_rider_gather_remote([g_sa, part_b])], "allgather_small_start")

    in_t = _adamw_rs_call(cidx, tok, [gw_in], [r1_in], [r2_in], [w_in[0].T], [m_w_in[0].T], [v_w_in[0].T], 2,
                          "adamw_w_in")
    rest = _adamw_rs_call(
        cidx, tok, [gw_bp, gw_ba, gw_out, gw_up, gw_down], [r1_bp, r1_ba, r1_out, r1_up, r1_down],
        [r2_bp, r2_ba, r2_out, r2_up, r2_down], [w_branch_pool[0], w_branch_attn[0], w_out[0], w_up[0], w_down[0]],
        [m_w_branch_pool[0], m_w_branch_attn[0], m_w_out[0], m_w_up[0], m_w_down[0]],
        [v_w_branch_pool[0], v_w_branch_attn[0], v_w_out[0], v_w_up[0], v_w_down[0]], N_DEV, "adamw_shards")
    big_g, big_d, big_m2, big_v2 = ([a[0].T] + list(b) for a, b in zip(in_t, rest))

    sa_all, sb_all = _copies_wait([c_small], rest[0][0], "allgather_small_wait")
    sa_all = lax.dynamic_update_slice(sa_all, g_sa[None], (me, 0, 0))
    sb_sum = _sum8_call(lax.dynamic_update_slice(sb_all, part_b[None], (me, 0, 0)))

    names = ["g_mix_pre", "b_in", "w_pool", "pool_scale", "attn_sinks", "g_mix_post", "g_mlp_pre", "g_mlp_post"]
    sm_w = dict(g_mix_pre=g_mix_pre, b_in=b_in, w_pool=w_pool, pool_scale=pool_scale, attn_sinks=attn_sinks,
                g_mix_post=g_mix_post, g_mlp_pre=g_mlp_pre, g_mlp_post=g_mlp_post)
    sm_m = dict(g_mix_pre=m_g_mix_pre, b_in=m_b_in, w_pool=m_w_pool, pool_scale=m_pool_scale, attn_sinks=m_attn_sinks,
                g_mix_post=m_g_mix_post, g_mlp_pre=m_g_mlp_pre, g_mlp_post=m_g_mlp_post)
    sm_v = dict(g_mix_pre=v_g_mix_pre, b_in=v_b_in, w_pool=v_w_pool, pool_scale=v_pool_scale, attn_sinks=v_attn_sinks,
                g_mix_post=v_g_mix_post, g_mlp_pre=v_g_mlp_pre, g_mlp_post=v_g_mlp_post)
    sizes = {k: sm_w[k].size for k in names}
    sizes.update(loss=D_MODEL, b_in_gates=GATE_WIDTH, b_in_head=C_G)
    sm_g = _unpack(sa_all.reshape(_SMALL_A_ROWS, LANES), _SMALL_A, sizes)
    sm_g.update(_unpack(sb_sum, _SMALL_B, sizes))
    sm_g["b_in"] = jnp.concatenate([sm_g["b_in_head"], sm_g["b_in_gates"]])
    loss = (0.5 / D_MODEL) * jnp.sum(sm_g["loss"])
    two_d = lambda a: a.reshape(-1, a.shape[-1])
    sd_, sm2_, sv2_ = _adamw_call([two_d(sm_w[k]) for k in names], [two_d(sm_g[k].reshape(sm_w[k].shape)) for k in names],
                                  [two_d(sm_m[k]) for k in names], [two_d(sm_v[k]) for k in names], 1, "adamw_small")
    like = lambda vals: {k: a.reshape(sm_w[k].shape) for k, a in zip(names, vals)}
    sm_d, sm_m2, sm_v2 = like(sd_), like(sm2_), like(sv2_)
    sm_gr = {k: sm_g[k].reshape(sm_w[k].shape) for k in names}

    order = ["g_mix_pre", "w_in", "b_in", "w_pool", "pool_scale", "attn_sinks", "w_branch_pool", "w_branch_attn",
             "w_out", "g_mix_post", "g_mlp_pre", "w_up", "w_down", "g_mlp_post"]
    big_names = ["w_in", "w_branch_pool", "w_branch_attn", "w_out", "w_up", "w_down"]
    lead = lambda a: a[None]
    tables = []
    for small_t, big_t in ((sm_gr, big_g), (sm_d, big_d), (sm_m2, big_m2), (sm_v2, big_v2)):
        bt = dict(zip(big_names, big_t))
        tables.append([lead(bt[k]) if k in bt else small_t[k] for k in order])
    return (loss, gx.reshape(B, S, D_MODEL), *tables[0], *tables[1], *tables[2], *tables[3])
```

```python
import jax
import jax.numpy as jnp
from jax import lax
from jax.experimental import pallas as pl
from jax.experimental.pallas import tpu as pltpu

F32 = jnp.float32
MXU_DTYPE = jnp.bfloat16
MESH = pl.DeviceIdType.MESH

D_MODEL = 1024
POOL_WINDOWS = (2, 4, 8, 16)
POOL_WIDTH = 512
POOL_GC = 128
HEAD_DIM = 64
N_Q_HEADS = 8
N_KV_HEADS = 2
GROUP = 4
ATTN_WIDTH = 512
KV_WIDTH = 128
BLOCK = 128
GATE_WIDTH = 2048
IN_WIDTH = 3328
D_FF = 4096
EPS = 1e-6
NEG_INF = -1e30
ROPE_THETA = 500000.0
ROT_DIM = 16
SCALE = HEAD_DIM ** -0.5
C_Q, C_K, C_V, C_G = 512, 1024, 1152, 1280

ADAM_LR = 0.001
ADAM_B1 = 0.9
ADAM_B2 = 0.999
ADAM_EPS = 1e-08
ADAM_WD = 0.01
ADAM_STEP = 10

N_DEV = 8
LANES = 128
VMEM_LIMIT = 56 * 1024 * 1024

NN = (((1,), (0,)), ((), ()))
NT = (((1,), (1,)), ((), ()))
TN = (((0,), (0,)), ((), ()))


def _dot(a, b, dims):
    return lax.dot_general(a, b, dims, preferred_element_type=F32)


def _params(sem=None):
    return pltpu.CompilerParams(dimension_semantics=sem, vmem_limit_bytes=VMEM_LIMIT)


def _tile(n, pref):
    t = min(n, pref)
    assert n % t == 0, (n, t)
    return t


class _Rider:
    def __init__(self, ins, out_shape, n_remote, n_local, plan, aliases=None, lands=None):
        self.ins, self.out_shape, self.n_remote, self.n_local = list(ins), list(out_shape), n_remote, n_local
        self.plan, self.aliases = plan, dict(aliases or {})
        self.lands = lands


def _after(token, rider=None):
    r = rider or _Rider([], [], 0, 0, lambda ins, outs, send, recv, loc, r0, l0: ([], []))
    return _Rider(r.ins + [token], r.out_shape, r.n_remote, r.n_local, r.plan, r.aliases)


def _launch(body, args, *, name, grid, in_specs, out_specs, out_shape, scratch_shapes=(), sem=None, rider=None):
    if rider is None:
        return pl.pallas_call(body, name=name, grid=grid, in_specs=in_specs, out_specs=out_specs, out_shape=out_shape,
                              scratch_shapes=list(scratch_shapes), compiler_params=_params(sem))(*args)
    n_in, n_out, n_scr = len(args), len(out_shape), len(scratch_shapes)
    r_in, r_out = len(rider.ins), len(rider.out_shape)
    copies = rider.n_remote + rider.n_local > 0

    def wrapped(*refs):
        ins, rins = refs[:n_in], refs[n_in:n_in + r_in]
        o0 = n_in + r_in
        outs, routs = refs[o0:o0 + n_out], refs[o0 + n_out:o0 + n_out + r_out]
        s0 = o0 + n_out + r_out
        scr = refs[s0:s0 + n_scr]
        if not copies:
            return body(*ins, *outs, *scr)
        send, recv, loc = refs[s0 + n_scr:]
        first, last = None, None
        for d in range(len(grid)):
            f, l = pl.program_id(d) == 0, pl.program_id(d) == pl.num_programs(d) - 1
            first = f if first is None else first & f
            last = l if last is None else last & l

        def start():
            remote, local = rider.plan(rins, routs, send, recv, loc, 0, 0)
            for cp in local + remote:
                cp.start()

        def finish():
            remote, local = rider.plan(rins, routs, send, recv, loc, 0, 0)
            for cp in remote + local:
                cp.wait()

        if first is None:
            start()
            body(*ins, *outs, *scr)
            finish()
        else:
            pl.when(first)(start)
            body(*ins, *outs, *scr)
            pl.when(last)(finish)

    hbm = pl.BlockSpec(memory_space=pl.ANY)
    dma = pltpu.SemaphoreType.DMA
    res = pl.pallas_call(
        wrapped, name=name, grid=grid, in_specs=list(in_specs) + [hbm] * r_in,
        out_specs=list(out_specs) + [hbm] * r_out, out_shape=list(out_shape) + rider.out_shape,
        scratch_shapes=list(scratch_shapes) + (
            [dma((max(rider.n_remote, 1),)), dma((max(rider.n_remote, 1),)), dma((max(rider.n_local, 1),))] if copies else []),
        input_output_aliases={n_in + i: n_out + o for i, o in rider.aliases.items()},
        compiler_params=_params(sem),
    )(*args, *rider.ins)
    return list(res[:n_out]), list(res[n_out:])


def _comm_call(rider, name):
    return _launch(lambda: None, [], name=name, grid=(), in_specs=[], out_specs=[], out_shape=[], rider=rider)[1]


_HBM = pl.BlockSpec(memory_space=pltpu.HBM)
_SEM = pl.BlockSpec(memory_space=pltpu.SEMAPHORE)
_EFFECT = pltpu.SideEffectType.DATAFLOW_SIDE_EFFECTING


def _copies_start(riders, name, after=None):
    assert all(r.n_local == 0 and not r.aliases for r in riders)
    extra = [] if after is None else [after]
    sizes = [(len(r.ins), len(r.out_shape)) for r in riders]
    bufs = []
    for r in riders:
        lands = r.lands or [lax.empty(s.shape, s.dtype) for s in r.out_shape]
        bufs += [pltpu.with_memory_space_constraint(a, pltpu.HBM) for a in list(r.ins) + list(lands)]
    nb, ng, ne = len(bufs), len(riders), len(extra)

    def body(*refs):
        sems, token, at = refs[2 * nb + ne:2 * nb + ne + 2 * ng], refs[-1], 0
        for g, (r, (ni, no)) in enumerate(zip(riders, sizes)):
            remote, _ = r.plan(refs[at:at + ni], refs[at + ni:at + ni + no], sems[2 * g], sems[2 * g + 1], None, 0, 0)
            for cp in remote:
                cp.start()
            at += ni + no
        token[...] = jnp.zeros_like(token)

    res = pl.pallas_call(
        body, name=name, in_specs=[_HBM] * nb + [pl.BlockSpec(memory_space=pl.ANY)] * ne,
        out_specs=[_HBM] * nb + [_SEM] * (2 * ng) + [pl.BlockSpec(memory_space=pltpu.VMEM)],
        out_shape=[pltpu.HBM(a.shape, a.dtype) for a in bufs]
        + [pltpu.SemaphoreType.DMA((r.n_remote,)) for r in riders for _ in range(2)]
        + [jax.ShapeDtypeStruct((8, LANES), F32)],
        input_output_aliases={i: i for i in range(nb)},
        compiler_params=pltpu.CompilerParams(has_side_effects=_EFFECT),
    )(*bufs, *extra)
    handles, at = [], 0
    for g, (r, (ni, no)) in enumerate(zip(riders, sizes)):
        handles.append((r, list(res[at:at + ni + no]), res[nb + 2 * g], res[nb + 2 * g + 1]))
        at += ni + no
    return handles, res[-1]


def _copies_wait(handles, after, name):
    bufs = [b for _, bs, _, _ in handles for b in bs]
    sems = [s for _, _, send, recv in handles for s in (send, recv)]
    nb, ng = len(bufs), len(handles)

    def body(*refs):
        at = 0
        for g, (rider, bs, _, _) in enumerate(handles):
            ni = len(rider.ins)
            remote, _ = rider.plan(refs[at:at + ni], refs[at + ni:at + len(bs)], refs[nb + 2 * g], refs[nb + 2 * g + 1],
                                   None, 0, 0)
            for cp in remote:
                cp.wait_send()
                cp.wait_recv()
            at += len(bs)

    res = pl.pallas_call(
        body, name=name, in_specs=[_HBM] * nb + [_SEM] * (2 * ng) + [pl.BlockSpec(memory_space=pl.ANY)],
        out_specs=[_HBM] * nb, out_shape=[pltpu.HBM(a.shape, a.dtype) for a in bufs],
        input_output_aliases={i: i for i in range(nb)},
        compiler_params=pltpu.CompilerParams(has_side_effects=_EFFECT),
    )(*bufs, *sems, after)
    lands, at = [], 0
    for rider, bs, _, _ in handles:
        lands += list(res[at + len(rider.ins):at + len(bs)])
        at += len(bs)
    return lands


def _rms_r(x):
    return lax.rsqrt(jnp.mean(x * x, axis=-1, keepdims=True) + EPS)


def _rms_bwd(dn, x, r, g):
    xh = x * r
    dxh = dn * g
    dx = r * (dxh - xh * jnp.mean(dxh * xh, axis=-1, keepdims=True))
    return dx, dn * xh


def _rot(t, c, sa, sb):
    outs = []
    for j in range(t.shape[1] // LANES):
        tj = t[:, LANES * j:LANES * (j + 1)]
        outs.append(tj * c + pltpu.roll(tj, LANES - 8, 1) * sa + pltpu.roll(tj, 8, 1) * sb)
    return outs[0] if len(outs) == 1 else jnp.concatenate(outs, axis=1)


def _rot_tables(S):
    pos = jnp.arange(S, dtype=F32)
    inv_freq = ROPE_THETA ** (-jnp.arange(0, ROT_DIM, 2, dtype=F32) / ROT_DIM)
    ang = pos[:, None] * inv_freq[None, :]
    cos, sin = jnp.cos(ang), jnp.sin(ang)
    one = jnp.ones((S, HEAD_DIM - ROT_DIM), F32)
    zero = jnp.zeros((S, HEAD_DIM - ROT_DIM), F32)
    z8 = jnp.zeros((S, 8), F32)
    c = jnp.concatenate([cos, cos, one], axis=1)
    sa = jnp.concatenate([-sin, z8, zero], axis=1)
    sb = jnp.concatenate([z8, sin, zero], axis=1)
    rep = LANES // HEAD_DIM
    return jnp.tile(c, (1, rep)), jnp.tile(sa, (1, rep)), jnp.tile(sb, (1, rep))


def _lane_tile4(k):
    lane = lax.broadcasted_iota(jnp.int32, k.shape, 1)
    rk = pltpu.roll(k, HEAD_DIM, 1)
    x0 = jnp.where(lane < HEAD_DIM, k, rk)
    x1 = jnp.where(lane < HEAD_DIM, rk, k)
    return jnp.concatenate([x0, x0, x1, x1], axis=1)


def _fold_heads(acc):
    zs = []
    for hk in range(N_KV_HEADS):
        a = acc[:, 256 * hk:256 * hk + LANES] + acc[:, 256 * hk + LANES:256 * (hk + 1)]
        zs.append(a + pltpu.roll(a, HEAD_DIM, 1))
    lane = lax.broadcasted_iota(jnp.int32, zs[0].shape, 1)
    return jnp.where(lane < HEAD_DIM, zs[0], zs[1])


def _inproj_call(x, g1, win_t, b_in, rc, rsa, rsb, S, rider=None):
    T = x.shape[0]
    tm = _tile(S, 512)
    nst = S // tm

    def body(x_ref, g1_ref, w_ref, b_ref, c_ref, sa_ref, sb_ref,
             h_ref, u_ref, q_ref, k4_ref, v4_ref, g_ref):
        xv = x_ref[...]
        hb = ((xv * _rms_r(xv)) * g1_ref[...]).astype(MXU_DTYPE)
        h_ref[...] = hb

        def proj(lo, hi):
            return _dot(hb, w_ref[lo:hi, :], NT) + b_ref[:, lo:hi]

        c, sa, sb = c_ref[...], sa_ref[...], sb_ref[...]
        u_ref[...] = proj(0, C_Q)
        q_ref[...] = (_rot(proj(C_Q, C_K), c, sa, sb) * SCALE).astype(MXU_DTYPE)
        kv = proj(C_K, C_G)
        k4_ref[...] = _lane_tile4(_rot(kv[:, :KV_WIDTH], c, sa, sb)).astype(MXU_DTYPE)
        v4_ref[...] = _lane_tile4(kv[:, KV_WIDTH:]).astype(MXU_DTYPE)
        g_ref[...] = jax.nn.sigmoid(proj(C_G, IN_WIDTH)).astype(MXU_DTYPE)

    tok = lambda w: pl.BlockSpec((tm, w), lambda i: (i, 0))
    full = lambda a: pl.BlockSpec(a.shape, lambda i: (0,) * a.ndim)
    tab = pl.BlockSpec((tm, LANES), lambda i: (i % nst, 0))
    return _launch(
        body, [x, g1, win_t, b_in, rc, rsa, rsb], name="inproj_fwd", grid=(T // tm,),
        in_specs=[tok(D_MODEL), full(g1), full(win_t), full(b_in), tab, tab, tab],
        out_specs=[tok(D_MODEL), tok(POOL_WIDTH), tok(ATTN_WIDTH), tok(512), tok(512), tok(GATE_WIDTH)],
        out_shape=[jax.ShapeDtypeStruct((T, D_MODEL), MXU_DTYPE), jax.ShapeDtypeStruct((T, POOL_WIDTH), F32),
                   jax.ShapeDtypeStruct((T, ATTN_WIDTH), MXU_DTYPE), jax.ShapeDtypeStruct((T, 512), MXU_DTYPE),
                   jax.ShapeDtypeStruct((T, 512), MXU_DTYPE), jax.ShapeDtypeStruct((T, GATE_WIDTH), MXU_DTYPE)],
        sem=("arbitrary",), rider=rider)


def _shift_rows(a, k, rows):
    n = a.shape[0]
    if k > 0:
        return jnp.where(rows >= k, pltpu.roll(a, k, 0), 0.0)
    return jnp.where(rows < n + k, pltpu.roll(a, n + k, 0), 0.0)


def _win_sum(a, w, rows, sign):
    s, k = a, 1
    while k < w:
        s = s + _shift_rows(s, sign * k, rows)
        k *= 2
    return s


def _pool_diff(ug, w, rows):
    inv = 1.0 / jnp.minimum(rows + 1, w).astype(F32)
    return _win_sum(ug, w, rows, 1) * inv - ug, inv


def _pool_call(u, w_pool, pool_scale, S):
    T = u.shape[0]

    def body(u_ref, w_ref, ps_ref, y_ref):
        rows = lax.broadcasted_iota(jnp.int32, (S, POOL_GC), 0)
        for gi, w in enumerate(POOL_WINDOWS):
            sl = slice(POOL_GC * gi, POOL_GC * (gi + 1))
            diff, _ = _pool_diff(u_ref[:, sl], w, rows)
            mixed = _dot(diff.astype(MXU_DTYPE), w_ref[gi], NN)
            y_ref[:, sl] = (mixed * ps_ref[:, sl]).astype(MXU_DTYPE)

    seq = pl.BlockSpec((S, POOL_WIDTH), lambda b: (b, 0))
    return pl.pallas_call(
        body, name="pool_fwd", grid=(T // S,),
        in_specs=[seq, pl.BlockSpec(w_pool.shape, lambda b: (0, 0, 0)), pl.BlockSpec(pool_scale.shape, lambda b: (0, 0))],
        out_specs=seq, out_shape=jax.ShapeDtypeStruct((T, POOL_WIDTH), MXU_DTYPE),
        compiler_params=_params(("arbitrary",)),
    )(u, w_pool, pool_scale)


def _pool_bwd_call(u, dyp, w_pool, pool_scale, S, rider=None):
    T = u.shape[0]

    def body(u_ref, dy_ref, w_ref, ps_ref, du_ref, dw_ref, dps_ref):
        @pl.when(pl.program_id(0) == 0)
        def _():
            dw_ref[...] = jnp.zeros_like(dw_ref)
            dps_ref[...] = jnp.zeros_like(dps_ref)

        rows = lax.broadcasted_iota(jnp.int32, (S, POOL_GC), 0)
        for gi, w in enumerate(POOL_WINDOWS):
            sl = slice(POOL_GC * gi, POOL_GC * (gi + 1))
            diff, inv = _pool_diff(u_ref[:, sl], w, rows)
            diffb = diff.astype(MXU_DTYPE)
            wg = w_ref[gi]
            mixed = _dot(diffb, wg, NN)
            dy = dy_ref[:, sl]
            dps_ref[:, sl] += jnp.sum(dy * mixed, axis=0, keepdims=True)
            dmb = (dy * ps_ref[:, sl]).astype(MXU_DTYPE)
            dw_ref[gi] += _dot(diffb, dmb, TN)
            ddiff = _dot(dmb, wg, NT)
            du_ref[:, sl] = (_win_sum(ddiff * inv, w, rows, -1) - ddiff).astype(MXU_DTYPE)

    seq = pl.BlockSpec((S, POOL_WIDTH), lambda b: (b, 0))
    return _launch(
        body, [u, dyp, w_pool, pool_scale], name="pool_bwd", grid=(T // S,),
        in_specs=[seq, seq, pl.BlockSpec(w_pool.shape, lambda b: (0, 0, 0)), pl.BlockSpec(pool_scale.shape, lambda b: (0, 0))],
        out_specs=[seq, pl.BlockSpec(w_pool.shape, lambda b: (0, 0, 0)), pl.BlockSpec(pool_scale.shape, lambda b: (0, 0))],
        out_shape=[jax.ShapeDtypeStruct((T, POOL_WIDTH), MXU_DTYPE), jax.ShapeDtypeStruct(w_pool.shape, F32),
                   jax.ShapeDtypeStruct(pool_scale.shape, F32)],
        sem=("arbitrary",), rider=rider)


def _attn_consts():
    lane_g = lax.broadcasted_iota(jnp.int32, (BLOCK, 256), 1) >> 6
    rgrp = lax.broadcasted_iota(jnp.int32, (GROUP * BLOCK, 1), 0) >> 7
    rel = lax.broadcasted_iota(jnp.int32, (BLOCK, 256), 0) - lax.broadcasted_iota(jnp.int32, (BLOCK, 256), 1)

    def bias(off):
        ok = (rel + off >= 0) & (rel + off < BLOCK)
        return jnp.concatenate([jnp.where(ok, 0.0, NEG_INF)] * GROUP, axis=0)

    return lane_g, rgrp, bias(0), bias(BLOCK)


def _sink_rows(sink_ref, hk, rgrp):
    sv = jnp.zeros(rgrp.shape, F32)
    for g in range(GROUP):
        sv = jnp.where(rgrp == g, sink_ref[0, GROUP * hk + g], sv)
    return sv


def _stack_heads(xb, lane_g):
    return jnp.concatenate([jnp.where(lane_g == g, xb, jnp.zeros_like(xb)) for g in range(GROUP)], axis=0)


def _unstack_heads(xs, lane_g):
    out = jnp.where(lane_g == 0, xs[0:BLOCK], 0.0)
    for g in range(1, GROUP):
        out = out + jnp.where(lane_g == g, xs[BLOCK * g:BLOCK * (g + 1)], 0.0)
    return out


def _attn_probs(qs, kb, bias, sv):
    s = _dot(qs, kb, NT) + bias
    m = jnp.maximum(jnp.max(s, axis=1, keepdims=True), sv)
    e = jnp.exp(s - m)
    es = jnp.exp(sv - m)
    inv_l = 1.0 / (jnp.sum(e, axis=1, keepdims=True) + es)
    return e * inv_l, es * inv_l


def _attn_blocks(nb, blk, carry, per=1):
    carry = blk(0, 0, True, carry)
    per = per if (nb - 1) % per == 0 else 1

    def step(i, c):
        for k in range(per):
            n = 1 + per * i + k
            c = blk(pl.multiple_of(n * BLOCK, BLOCK), pl.multiple_of((n - 1) * BLOCK, BLOCK), False, c)
        return c

    return lax.fori_loop(0, (nb - 1) // per, step, carry)


def _attn_call(sinks, q, k4, v4, S, rider=None):
    T = q.shape[0]
    nb = S // BLOCK

    def body(sink_ref, q_ref, k_ref, v_ref, o_ref):
        lane_g, rgrp, bias_first, bias_later = _attn_consts()
        svs = [_sink_rows(sink_ref, hk, rgrp) for hk in range(N_KV_HEADS)]

        def blk(q0, k0, first, carry):
            for hk in range(N_KV_HEADS):
                cs = slice(256 * hk, 256 * (hk + 1))
                qs = _stack_heads(q_ref[pl.ds(q0, BLOCK), cs], lane_g)
                p, _ = _attn_probs(qs, k_ref[pl.ds(k0, 2 * BLOCK), cs], bias_first if first else bias_later, svs[hk])
                o = _dot(p.astype(MXU_DTYPE), v_ref[pl.ds(k0, 2 * BLOCK), cs], NN)
                o_ref[pl.ds(q0, BLOCK), cs] = _unstack_heads(o, lane_g).astype(MXU_DTYPE)
            return carry

        _attn_blocks(nb, blk, 0, per=3)

    seq = pl.BlockSpec((S, ATTN_WIDTH), lambda b: (b, 0))
    return _launch(
        body, [sinks, q, k4, v4], name="attn_fwd", grid=(T // S,),
        in_specs=[pl.BlockSpec(memory_space=pltpu.SMEM), seq, seq, seq],
        out_specs=[seq], out_shape=[jax.ShapeDtypeStruct((T, ATTN_WIDTH), MXU_DTYPE)],
        sem=("arbitrary",), rider=rider)


def _attn_bwd_call(sinks, q, k4, v4, do, rc, rsa, rsb, S, rider=None):
    T = q.shape[0]
    nb = S // BLOCK

    def body(sink_ref, q_ref, k_ref, v_ref, do_ref, c_ref, sa_ref, sb_ref,
             dq_ref, dk_ref, dv_ref, ds_ref, dk_acc, dv_acc):
        lane_g, rgrp, bias_first, bias_later = _attn_consts()
        svs = [_sink_rows(sink_ref, hk, rgrp) for hk in range(N_KV_HEADS)]
        lane1 = lax.broadcasted_iota(jnp.int32, (1, LANES), 1)
        dk_acc[...] = jnp.zeros_like(dk_acc)
        dv_acc[...] = jnp.zeros_like(dv_acc)

        def blk(q0, k0, first, dsink):
            rows = pl.ds(q0, BLOCK)
            c, sa, sb = c_ref[rows, :], sa_ref[rows, :], sb_ref[rows, :]
            for hk in range(N_KV_HEADS):
                cs = slice(256 * hk, 256 * (hk + 1))
                qs = _stack_heads(q_ref[rows, cs], lane_g)
                dos = _stack_heads(do_ref[rows, cs], lane_g)
                kb = k_ref[pl.ds(k0, 2 * BLOCK), cs]
                vb = v_ref[pl.ds(k0, 2 * BLOCK), cs]
                p, ps = _attn_probs(qs, kb, bias_first if first else bias_later, svs[hk])
                dp = _dot(dos, vb, NT)
                delta = jnp.sum(p * dp, axis=1, keepdims=True)
                dsb = (p * (dp - delta)).astype(MXU_DTYPE)
                dqb = _unstack_heads(_dot(dsb, kb, NN), lane_g) * SCALE
                dq_ref[rows, cs] = _rot(dqb, c, -sa, -sb).astype(MXU_DTYPE)
                dk_acc[pl.ds(k0, 2 * BLOCK), cs] += _dot(dsb, qs, TN)
                dv_acc[pl.ds(k0, 2 * BLOCK), cs] += _dot(p.astype(MXU_DTYPE), dos, TN)
                psd = ps * delta
                for g in range(GROUP):
                    val = -jnp.sum(psd[BLOCK * g:BLOCK * (g + 1)], axis=0, keepdims=True)
                    dsink = dsink + jnp.where(lane1 == GROUP * hk + g, val, 0.0)
            return dsink

        dsink = _attn_blocks(nb, blk, jnp.zeros((1, LANES), F32))
        dk_ref[...] = _rot(_fold_heads(dk_acc[...]), c_ref[...], -sa_ref[...], -sb_ref[...]).astype(MXU_DTYPE)
        dv_ref[...] = _fold_heads(dv_acc[...]).astype(MXU_DTYPE)
        ds_ref[...] = jnp.broadcast_to(dsink, ds_ref.shape)

    seq = pl.BlockSpec((S, ATTN_WIDTH), lambda b: (b, 0))
    kvs = pl.BlockSpec((S, KV_WIDTH), lambda b: (b, 0))
    tab = pl.BlockSpec((S, LANES), lambda b: (0, 0))
    nseq = T // S
    return _launch(
        body, [sinks, q, k4, v4, do, rc, rsa, rsb], name="attn_bwd", grid=(nseq,),
        in_specs=[pl.BlockSpec(memory_space=pltpu.SMEM), seq, seq, seq, seq, tab, tab, tab],
        out_specs=[seq, kvs, kvs, pl.BlockSpec((8, LANES), lambda b: (b, 0))],
        out_shape=[jax.ShapeDtypeStruct((T, ATTN_WIDTH), MXU_DTYPE), jax.ShapeDtypeStruct((T, KV_WIDTH), MXU_DTYPE),
                   jax.ShapeDtypeStruct((T, KV_WIDTH), MXU_DTYPE), jax.ShapeDtypeStruct((8 * nseq, LANES), F32)],
        scratch_shapes=[pltpu.VMEM((S, 512), F32), pltpu.VMEM((S, 512), F32)],
        sem=("arbitrary",), rider=rider)


def _branch_weights(wbp_ref, wba_ref, wbp_s, wba_s):
    @pl.when(pl.program_id(0) == 0)
    def _():
        for j in range(N_DEV):
            wbp_s[:, LANES * j:LANES * (j + 1)] = wbp_ref[j]
            wba_s[:, LANES * j:LANES * (j + 1)] = wba_ref[j]


def _mix_fwd_call(yp, ya, g, x, wbp, wba, wout, g2, g3, rider=None):
    T = x.shape[0]
    tm = _tile(T, 512)

    def body(yp_ref, ya_ref, g_ref, x_ref, wbp_ref, wba_ref, wout_ref, g2_ref, g3_ref,
             mix_ref, x1_ref, h2_ref, wbp_s, wba_s):
        _branch_weights(wbp_ref, wba_ref, wbp_s, wba_s)
        bp = _dot(yp_ref[...], wbp_s[...], NN)
        ba = _dot(ya_ref[...], wba_s[...], NN)
        merged = g_ref[:, :D_MODEL].astype(F32) * bp + g_ref[:, D_MODEL:].astype(F32) * ba
        mix = _dot(merged.astype(MXU_DTYPE), wout_ref[...], NN)
        mix_ref[...] = mix
        x1 = x_ref[...] + (mix * _rms_r(mix)) * g2_ref[...]
        x1_ref[...] = x1
        h2_ref[...] = ((x1 * _rms_r(x1)) * g3_ref[...]).astype(MXU_DTYPE)

    tok = lambda w: pl.BlockSpec((tm, w), lambda i: (i, 0))
    full = lambda a: pl.BlockSpec(a.shape, lambda i: (0,) * a.ndim)
    return _launch(
        body, [yp, ya, g, x, wbp, wba, wout, g2, g3], name="mix_fwd", grid=(T // tm,),
        in_specs=[tok(POOL_WIDTH), tok(ATTN_WIDTH), tok(GATE_WIDTH), tok(D_MODEL), full(wbp), full(wba), full(wout),
                  full(g2), full(g3)],
        out_specs=[tok(D_MODEL), tok(D_MODEL), tok(D_MODEL)],
        out_shape=[jax.ShapeDtypeStruct((T, D_MODEL), F32), jax.ShapeDtypeStruct((T, D_MODEL), F32),
                   jax.ShapeDtypeStruct((T, D_MODEL), MXU_DTYPE)],
        scratch_shapes=[pltpu.VMEM((POOL_WIDTH, D_MODEL), MXU_DTYPE), pltpu.VMEM((ATTN_WIDTH, D_MODEL), MXU_DTYPE)],
        sem=("arbitrary",), rider=rider)


def _mix_bwd_call(dx1, mix, yp, ya, g, wbp, wba, wout, g2, rider=None):
    T = dx1.shape[0]
    tm = _tile(T, 512)

    def body(dx1_ref, mix_ref, yp_ref, ya_ref, g_ref, wbp_ref, wba_ref, wout_ref, g2_ref,
             dyp_ref, do_ref, dgates_ref, dg2_ref, dbg_ref, gout_ref, gbp_ref, gba_ref,
             wbp_s, wba_s, acc_out, acc_bp, acc_ba, sem):
        _branch_weights(wbp_ref, wba_ref, wbp_s, wba_s)
        step = pl.program_id(0)

        @pl.when(step == 0)
        def _():
            dg2_ref[...] = jnp.zeros_like(dg2_ref)
            dbg_ref[...] = jnp.zeros_like(dbg_ref)
            acc_out[...] = jnp.zeros_like(acc_out)
            acc_bp[...] = jnp.zeros_like(acc_bp)
            acc_ba[...] = jnp.zeros_like(acc_ba)

        mix = mix_ref[...]
        dmix, dg2 = _rms_bwd(dx1_ref[...], mix, _rms_r(mix), g2_ref[...])
        dg2_ref[...] += jnp.sum(dg2, axis=0, keepdims=True)
        dmixb = dmix.astype(MXU_DTYPE)
        dmerged = _dot(dmixb, wout_ref[...], NT)
        yp, ya = yp_ref[...], ya_ref[...]
        bp = _dot(yp, wbp_s[...], NN)
        ba = _dot(ya, wba_s[...], NN)
        gp, ga = g_ref[:, :D_MODEL].astype(F32), g_ref[:, D_MODEL:].astype(F32)
        acc_out[...] += _dot((gp * bp + ga * ba).astype(MXU_DTYPE), dmixb, TN)
        dgp = dmerged * bp * (gp * (1.0 - gp))
        dga = dmerged * ba * (ga * (1.0 - ga))
        dbg_ref[:, :D_MODEL] += jnp.sum(dgp, axis=0, keepdims=True)
        dbg_ref[:, D_MODEL:] += jnp.sum(dga, axis=0, keepdims=True)
        dgates_ref[:, :D_MODEL] = dgp.astype(MXU_DTYPE)
        dgates_ref[:, D_MODEL:] = dga.astype(MXU_DTYPE)
        dbp = (dmerged * gp).astype(MXU_DTYPE)
        dba = (dmerged * ga).astype(MXU_DTYPE)
        acc_bp[...] += _dot(yp, dbp, TN)
        acc_ba[...] += _dot(ya, dba, TN)
        dyp_ref[...] = _dot(dbp, wbp_s[...], NT)
        do_ref[...] = _dot(dba, wba_s[...], NT).astype(MXU_DTYPE)

        @pl.when(step == pl.num_programs(0) - 1)
        def _():
            copies = [pltpu.make_async_copy(acc_out, gout_ref, sem.at[0])]
            for j in range(N_DEV):
                cols = slice(LANES * j, LANES * (j + 1))
                copies.append(pltpu.make_async_copy(acc_bp.at[:, cols], gbp_ref.at[j], sem.at[1 + j]))
                copies.append(pltpu.make_async_copy(acc_ba.at[:, cols], gba_ref.at[j], sem.at[1 + N_DEV + j]))
            for cp in copies:
                cp.start()
            for cp in copies:
                cp.wait()

    tok = lambda w: pl.BlockSpec((tm, w), lambda i: (i, 0))
    full = lambda a: pl.BlockSpec(a.shape, lambda i: (0,) * a.ndim)
    acc = lambda w: pl.BlockSpec((1, w), lambda i: (0, 0))
    hbm = pl.BlockSpec(memory_space=pl.ANY)
    sd = jax.ShapeDtypeStruct
    return _launch(
        body, [dx1, mix, yp, ya, g, wbp, wba, wout, g2], name="mix_bwd", grid=(T // tm,),
        in_specs=[tok(D_MODEL), tok(D_MODEL), tok(POOL_WIDTH), tok(ATTN_WIDTH), tok(GATE_WIDTH), full(wbp), full(wba),
                  full(wout), full(g2)],
        out_specs=[tok(POOL_WIDTH), tok(ATTN_WIDTH), tok(GATE_WIDTH), acc(D_MODEL), acc(GATE_WIDTH), hbm, hbm, hbm],
        out_shape=[sd((T, POOL_WIDTH), F32), sd((T, ATTN_WIDTH), MXU_DTYPE), sd((T, GATE_WIDTH), MXU_DTYPE),
                   sd((1, D_MODEL), F32), sd((1, GATE_WIDTH), F32), sd((D_MODEL, D_MODEL), F32),
                   sd((N_DEV, POOL_WIDTH, LANES), F32), sd((N_DEV, ATTN_WIDTH, LANES), F32)],
        scratch_shapes=[pltpu.VMEM((POOL_WIDTH, D_MODEL), MXU_DTYPE), pltpu.VMEM((ATTN_WIDTH, D_MODEL), MXU_DTYPE),
                        pltpu.VMEM((D_MODEL, D_MODEL), F32), pltpu.VMEM((POOL_WIDTH, D_MODEL), F32),
                        pltpu.VMEM((ATTN_WIDTH, D_MODEL), F32), pltpu.SemaphoreType.DMA((1 + 2 * N_DEV,))],
        sem=("arbitrary",), rider=rider)


def _mlp_up_call(h2, wup):
    T = h2.shape[0]
    tm = _tile(T, 512)
    fc = D_FF // N_DEV

    def body(h2_ref, wup_ref, act_ref):
        h2 = h2_ref[...]
        for j in range(N_DEV):
            rl = jnp.maximum(_dot(h2, wup_ref[j], NN), 0.0)
            act_ref[:, fc * j:fc * (j + 1)] = (rl * rl).astype(MXU_DTYPE)

    sd = jax.ShapeDtypeStruct
    return pl.pallas_call(
        body, name="mlp_up", grid=(T // tm,),
        in_specs=[pl.BlockSpec((tm, D_MODEL), lambda i: (i, 0)),
                  pl.BlockSpec(wup.shape, lambda i: (0, 0, 0), pipeline_mode=pl.Buffered(1))],
        out_specs=pl.BlockSpec((tm, D_FF), lambda i: (i, 0)), out_shape=sd((T, D_FF), MXU_DTYPE),
        compiler_params=_params(("arbitrary",)),
    )(h2, wup)


def _mlp_call(x1, act, target, wup, wdown, g3, g4):
    T = x1.shape[0]
    tm = _tile(T, 256)
    fc = D_FF // N_DEV

    def body(x1_ref, act_ref, t_ref, wup_ref, wdown_ref, g3_ref, g4_ref,
             da_ref, dff_ref, dx1_ref, dg3_ref, dg4_ref, loss_ref):
        @pl.when(pl.program_id(0) == 0)
        def _():
            dg3_ref[...] = jnp.zeros_like(dg3_ref)
            dg4_ref[...] = jnp.zeros_like(dg4_ref)
            loss_ref[...] = jnp.zeros_like(loss_ref)

        ff = jnp.zeros((tm, D_MODEL), F32)
        for j in range(N_DEV):
            ff = ff + _dot(act_ref[:, fc * j:fc * (j + 1)], wdown_ref[j], NN)
        x1 = x1_ref[...]
        r4 = _rms_r(ff)
        err = x1 + (ff * r4) * g4_ref[...] - t_ref[...]
        loss_ref[...] += jnp.sum(err * err, axis=0, keepdims=True)
        dy = err * (1.0 / D_MODEL)
        dff, dg4 = _rms_bwd(dy, ff, r4, g4_ref[...])
        dg4_ref[...] += jnp.sum(dg4, axis=0, keepdims=True)
        dffb = dff.astype(MXU_DTYPE)
        dff_ref[...] = dffb
        dh2 = jnp.zeros((tm, D_MODEL), F32)
        for j in range(N_DEV):
            sl = slice(fc * j, fc * (j + 1))
            rl = jnp.sqrt(act_ref[:, sl].astype(F32))
            dab = (_dot(dffb, wdown_ref[j], NT) * (2.0 * rl)).astype(MXU_DTYPE)
            da_ref[:, sl] = dab
            dh2 = dh2 + _dot(dab, wup_ref[j], NT)
        dx1, dg3 = _rms_bwd(dh2, x1, _rms_r(x1), g3_ref[...])
        dg3_ref[...] += jnp.sum(dg3, axis=0, keepdims=True)
        dx1_ref[...] = dy + dx1

    tok = lambda w: pl.BlockSpec((tm, w), lambda i: (i, 0))
    full = lambda a: pl.BlockSpec(a.shape, lambda i: (0,) * a.ndim, pipeline_mode=pl.Buffered(1))
    vec = pl.BlockSpec((1, D_MODEL), lambda i: (0, 0))
    sd = jax.ShapeDtypeStruct
    return pl.pallas_call(
        body, name="mlp_down_bwd", grid=(T // tm,),
        in_specs=[tok(D_MODEL), tok(D_FF), tok(D_MODEL), full(wup), full(wdown), vec, vec],
        out_specs=[tok(D_FF), tok(D_MODEL), tok(D_MODEL), vec, vec, vec],
        out_shape=[sd((T, D_FF), MXU_DTYPE), sd((T, D_MODEL), MXU_DTYPE),
                   sd((T, D_MODEL), F32), sd((1, D_MODEL), F32), sd((1, D_MODEL), F32), sd((1, D_MODEL), F32)],
        compiler_params=_params(("arbitrary",)),
    )(x1, act, target, wup, wdown, g3, g4)


def _inproj_bwd_call(du, dq, dk, dv, dgates, dx1, x, win_t, g1, rider=None):
    T = x.shape[0]
    tm = _tile(T, 512)

    def body(du_ref, dq_ref, dk_ref, dv_ref, dgt_ref, dx1_ref, x_ref, w_ref, g1_ref, gx_ref, dg1_ref, db_ref):
        @pl.when(pl.program_id(0) == 0)
        def _():
            dg1_ref[...] = jnp.zeros_like(dg1_ref)
            db_ref[...] = jnp.zeros_like(db_ref)

        dh = jnp.zeros((tm, D_MODEL), F32)
        for ref, lo, hi in ((du_ref, 0, C_Q), (dq_ref, C_Q, C_K), (dk_ref, C_K, C_V), (dv_ref, C_V, C_G),
                            (dgt_ref, C_G, IN_WIDTH)):
            piece = ref[...]
            dh = dh + _dot(piece, w_ref[lo:hi, :], NN)
            if hi <= C_G:
                db_ref[:, lo:hi] += jnp.sum(piece.astype(F32), axis=0, keepdims=True)
        xv = x_ref[...]
        dx, dg1 = _rms_bwd(dh, xv, _rms_r(xv), g1_ref[...])
        dg1_ref[...] += jnp.sum(dg1, axis=0, keepdims=True)
        gx_ref[...] = dx1_ref[...] + dx

    tok = lambda w: pl.BlockSpec((tm, w), lambda i: (i, 0))
    full = lambda a: pl.BlockSpec(a.shape, lambda i: (0,) * a.ndim)
    sd = jax.ShapeDtypeStruct
    return _launch(
        body, [du, dq, dk, dv, dgates, dx1, x, win_t, g1], name="inproj_bwd", grid=(T // tm,),
        in_specs=[tok(POOL_WIDTH), tok(ATTN_WIDTH), tok(KV_WIDTH), tok(KV_WIDTH), tok(GATE_WIDTH), tok(D_MODEL),
                  tok(D_MODEL), full(win_t), full(g1)],
        out_specs=[tok(D_MODEL), pl.BlockSpec((1, D_MODEL), lambda i: (0, 0)), pl.BlockSpec((1, C_G), lambda i: (0, 0))],
        out_shape=[sd((T, D_MODEL), F32), sd((1, D_MODEL), F32), sd((1, C_G), F32)],
        sem=("arbitrary",), rider=rider)


WGRAD_TOKENS = 1024


def _wgrad_rows_call(a, b, name, rider=None):
    T, K = a.shape
    N = b.shape[1]
    tm = _tile(T, WGRAD_TOKENS)
    kb = min(K, 1024)
    per = kb // (K // N_DEV)

    def body(a_ref, b_ref, o_ref):
        @pl.when(pl.program_id(1) == 0)
        def _():
            o_ref[...] = jnp.zeros_like(o_ref)

        d = _dot(a_ref[...], b_ref[...], TN)
        rs = kb // per
        for j in range(per):
            o_ref[j] += d[rs * j:rs * (j + 1)]

    return _launch(
        body, [a, b], name=name, grid=(K // kb, T // tm),
        in_specs=[pl.BlockSpec((tm, kb), lambda i, t: (t, i)), pl.BlockSpec((tm, N), lambda i, t: (t, 0))],
        out_specs=[pl.BlockSpec((per, K // N_DEV, N), lambda i, t: (i, 0, 0))],
        out_shape=[jax.ShapeDtypeStruct((N_DEV, K // N_DEV, N), F32)],
        sem=("arbitrary", "arbitrary"), rider=rider)


def _wgrad_cols_call(a, b, name, rider=None):
    T, K = a.shape
    N = b.shape[1]
    tm = _tile(T, WGRAD_TOKENS)
    nb = min(N, 1024)
    per = nb // (N // N_DEV)

    def body(a_ref, b_ref, o_ref):
        @pl.when(pl.program_id(1) == 0)
        def _():
            o_ref[...] = jnp.zeros_like(o_ref)

        d = _dot(a_ref[...], b_ref[...], TN)
        cs = nb // per
        for j in range(per):
            o_ref[j] += d[:, cs * j:cs * (j + 1)]

    return _launch(
        body, [a, b], name=name, grid=(N // nb, T // tm),
        in_specs=[pl.BlockSpec((tm, K), lambda i, t: (t, 0)), pl.BlockSpec((tm, nb), lambda i, t: (t, i))],
        out_specs=[pl.BlockSpec((per, K, N // N_DEV), lambda i, t: (i, 0, 0))],
        out_shape=[jax.ShapeDtypeStruct((N_DEV, K, N // N_DEV), F32)],
        sem=("arbitrary", "arbitrary"), rider=rider)


def _wgrad_in_call(du, dq, dk, dv, dgates, h, rider=None):
    T = h.shape[0]
    tm = _tile(T, WGRAD_TOKENS)
    rows = IN_WIDTH // N_DEV

    def body(du_ref, dq_ref, dk_ref, dv_ref, dgt_ref, h_ref, o_ref, acc, sem):
        t = pl.program_id(0)

        @pl.when(t == 0)
        def _():
            acc[...] = jnp.zeros_like(acc)

        hv = h_ref[...]
        for ref, lo, hi in ((du_ref, 0, C_Q), (dq_ref, C_Q, C_K), (dk_ref, C_K, C_V), (dv_ref, C_V, C_G),
                            (dgt_ref, C_G, IN_WIDTH)):
            acc[lo:hi, :] += _dot(ref[...], hv, TN)

        @pl.when(t == pl.num_programs(0) - 1)
        def _():
            copies = [pltpu.make_async_copy(acc.at[pl.ds(rows * j, rows), :], o_ref.at[j], sem.at[j])
                      for j in range(N_DEV)]
            for cp in copies:
                cp.start()
            for cp in copies:
                cp.wait()

    tok = lambda w: pl.BlockSpec((tm, w), lambda t: (t, 0))
    return _launch(
        body, [du, dq, dk, dv, dgates, h], name="wgrad_in", grid=(T // tm,),
        in_specs=[tok(POOL_WIDTH), tok(ATTN_WIDTH), tok(KV_WIDTH), tok(KV_WIDTH), tok(GATE_WIDTH), tok(D_MODEL)],
        out_specs=[pl.BlockSpec(memory_space=pl.ANY)],
        out_shape=[jax.ShapeDtypeStruct((N_DEV, rows, D_MODEL), F32)],
        scratch_shapes=[pltpu.VMEM((IN_WIDTH, D_MODEL), F32), pltpu.SemaphoreType.DMA((N_DEV,))],
        sem=("arbitrary",), rider=rider)


def _coords():
    return lax.axis_index("x"), lax.axis_index("y"), lax.axis_index("c")


def _allgather_call(shards, bufs):
    n = len(shards)

    def body(*refs):
        ins, outs = refs[:n], refs[2 * n:3 * n]
        send_sems, recv_sems = refs[3 * n:]
        x, y, c = _coords()
        me, sibling = (x, y, c), (x, y, 1 - c)
        chips = [(1 - x, y), (x, 1 - y), (1 - x, 1 - y)]

        def slot(p):
            return 4 * p[0] + 2 * p[1] + p[2]

        def copy(t, k, block, to, src=None):
            dst = outs[t].at[slot(block)]
            return pltpu.make_async_remote_copy(
                src_ref=dst if src is None else src, dst_ref=dst, send_sem=send_sems.at[t, k],
                recv_sem=recv_sems.at[t, k], device_id=to, device_id_type=MESH)

        first = []
        for t in range(n):
            first.append(copy(t, 0, me, sibling, src=ins[t]))
            first += [copy(t, 1 + j, me, (*chip, c), src=ins[t]) for j, chip in enumerate(chips)]
        for cp in first:
            cp.start()
        passed = []
        for t in range(n):
            for j, chip in enumerate(chips):
                copy(t, 1 + j, (*chip, c), me).wait_recv()
                fwd = copy(t, 4 + j, (*chip, c), sibling)
                fwd.start()
                passed.append(fwd)
        for t in range(n):
            copy(t, 0, sibling, me).wait_recv()
            for j, chip in enumerate(chips):
                copy(t, 4 + j, (*chip, 1 - c), me).wait_recv()
        for cp in first + passed:
            cp.wait_send()

    hbm = pl.BlockSpec(memory_space=pl.ANY)
    return pl.pallas_call(
        body, name="allgather_weights",
        in_specs=[hbm] * (2 * n), out_specs=[hbm] * n,
        out_shape=[jax.ShapeDtypeStruct(b.shape, b.dtype) for b in bufs],
        scratch_shapes=[pltpu.SemaphoreType.DMA((n, 7)), pltpu.SemaphoreType.DMA((n, 7))],
        input_output_aliases={n + t: t for t in range(n)},
    )(*shards, *bufs)


def _slot(p):
    return 4 * p[0] + 2 * p[1] + p[2]


def _rows(ref, span):
    return ref if span is None else ref.at[pl.ds(span[0], span[1])]


ALL = "all"
LOCAL = "local"


def _rows(ref, span):
    return ref if span == ALL else ref.at[pl.ds(span[0], span[1])]


def _rider_ag(items):
    ins, out_shape, aliases, where = [], [], {}, []
    n_remote = n_local = 0
    for t, (shard, buf, snd, fwd) in enumerate(items):
        i_shard = i_buf = None
        if snd is not None:
            i_shard = len(ins)
            ins.append(shard)
        if buf is not None:
            i_buf = len(ins)
            ins.append(buf)
            aliases[i_buf] = t
            out_shape.append(jax.ShapeDtypeStruct(buf.shape, buf.dtype))
        else:
            assert fwd is None and snd is not None
            out_shape.append(jax.ShapeDtypeStruct((N_DEV,) + shard.shape, shard.dtype))
        where.append((i_shard, i_buf, n_remote, n_local))
        n_remote += (4 if snd not in (None, LOCAL) else 0) + (3 if fwd is not None else 0)
        n_local += 1 if snd is not None else 0

    def plan(rins, routs, send, recv, loc, r0, l0):
        x, y, c = _coords()
        peers = [(x, y, 1 - c), (1 - x, y, c), (x, 1 - y, c), (1 - x, 1 - y, c)]
        remote, local = [], []
        for t, (shard, buf, snd, fwd) in enumerate(items):
            i_shard, i_buf, k, l = where[t]
            k, l = r0 + k, l0 + l
            if snd is not None:
                span = ALL if snd == LOCAL else snd
                src, dst = _rows(rins[i_shard], span), _rows(routs[t].at[_slot((x, y, c))], span)
                local.append(pltpu.make_async_copy(src, dst, loc.at[l]))
                for peer in (peers if snd != LOCAL else []):
                    remote.append(pltpu.make_async_remote_copy(
                        src_ref=src, dst_ref=dst, send_sem=send.at[k], recv_sem=recv.at[k],
                        device_id=peer, device_id_type=MESH))
                    k += 1
            if fwd is not None:
                for px, py, pc in peers[1:]:
                    s = _slot((px, py, pc))
                    remote.append(pltpu.make_async_remote_copy(
                        src_ref=_rows(rins[i_buf].at[s], fwd), dst_ref=_rows(routs[t].at[s], fwd),
                        send_sem=send.at[k], recv_sem=recv.at[k], device_id=peers[0], device_id_type=MESH))
                    k += 1
        return remote, local

    return _Rider(ins, out_shape, n_remote, n_local, plan, aliases)


def _gather_buffer(shard, me):
    return lax.dynamic_update_slice(lax.empty((N_DEV,) + shard.shape, shard.dtype), shard[None], (me, 0, 0))


def _rider_ag_remote(shards, me):
    n = len(shards)

    def plan(ins, outs, send, recv, loc, r0, l0):
        x, y, c = _coords()
        remote = []
        for t in range(n):
            dst = outs[t].at[_slot((x, y, c))]
            for k, peer in enumerate([(x, y, 1 - c), (1 - x, y, c), (x, 1 - y, c), (1 - x, 1 - y, c)]):
                remote.append(pltpu.make_async_remote_copy(
                    src_ref=ins[t], dst_ref=dst, send_sem=send.at[r0 + 4 * t + k], recv_sem=recv.at[r0 + 4 * t + k],
                    device_id=peer, device_id_type=MESH))
        return remote, []

    return _Rider(shards, [jax.ShapeDtypeStruct((N_DEV,) + s.shape, s.dtype) for s in shards], 4 * n, 0, plan,
                  lands=[_gather_buffer(s, me) for s in shards])


def _rider_rs_sibling(grads):
    n = len(grads)

    def plan(ins, outs, send, recv, loc, r0, l0):
        x, y, c = _coords()
        remote = []
        for t in range(n):
            for q in range(4):
                remote.append(pltpu.make_async_remote_copy(
                    src_ref=ins[t].at[q, 1 - c], dst_ref=outs[t].at[q], send_sem=send.at[r0 + 4 * t + q],
                    recv_sem=recv.at[r0 + 4 * t + q], device_id=(x, y, 1 - c), device_id_type=MESH))
        return remote, []

    return _Rider(grads, [jax.ShapeDtypeStruct((4,) + g.shape[2:], g.dtype) for g in grads], 4 * n, 0, plan)


def _rider_rs_chips(sums, rows=None, into=None):
    n = len(sums)
    rows = rows or [ALL] * n

    def plan(ins, outs, send, recv, loc, r0, l0):
        x, y, c = _coords()
        remote = []
        for t in range(n):
            for r, (px, py) in enumerate([(1 - x, y), (x, 1 - y), (1 - x, 1 - y)]):
                remote.append(pltpu.make_async_remote_copy(
                    src_ref=_rows(ins[t].at[2 * px + py], rows[t]), dst_ref=_rows(outs[t].at[r], rows[t]),
                    send_sem=send.at[r0 + 3 * t + r], recv_sem=recv.at[r0 + 3 * t + r],
                    device_id=(px, py, c), device_id_type=MESH))
        return remote, []

    out_shape = [jax.ShapeDtypeStruct((3,) + s.shape[1:], s.dtype) for s in sums]
    if into is None:
        return _Rider(sums, out_shape, 3 * n, 0, plan)
    return _Rider(list(sums) + list(into), out_shape, 3 * n, 0, plan, aliases={n + t: t for t in range(n)})


def _rider_gather_remote(parts):
    n = len(parts)

    def plan(ins, outs, send, recv, loc, r0, l0):
        x, y, c = _coords()
        me = _slot((x, y, c))
        remote = []
        for t in range(n):
            for k in range(1, N_DEV):
                peer = (x ^ ((k >> 2) & 1), y ^ ((k >> 1) & 1), c ^ (k & 1))
                remote.append(pltpu.make_async_remote_copy(
                    src_ref=ins[t], dst_ref=outs[t].at[me], send_sem=send.at[r0 + 7 * t + k - 1],
                    recv_sem=recv.at[r0 + 7 * t + k - 1], device_id=peer, device_id_type=MESH))
        return remote, []

    return _Rider(parts, [jax.ShapeDtypeStruct((N_DEV,) + p.shape, p.dtype) for p in parts], 7 * n, 0, plan)


def _chip_sum_call(idx, grads, recvd, out_dtypes, name):
    n = len(grads)

    def body(i_ref, *refs):
        for t in range(n):
            refs[2 * n + t][0] = (refs[t][0, 0] + refs[n + t][0]).astype(out_dtypes[t])

    def chip(k, s):
        return jnp.where(k >= s[0], k + 1, k)

    in_specs = [pl.BlockSpec((1, 1) + g.shape[2:], lambda k, s: (chip(k, s), s[1], 0, 0)) for g in grads]
    in_specs += [pl.BlockSpec((1,) + r.shape[1:], lambda k, s: (chip(k, s), 0, 0)) for r in recvd]
    return pl.pallas_call(
        body, name=name,
        grid_spec=pltpu.PrefetchScalarGridSpec(
            num_scalar_prefetch=1, grid=(3,), in_specs=in_specs,
            out_specs=[pl.BlockSpec((1,) + r.shape[1:], lambda k, s: (chip(k, s), 0, 0)) for r in recvd]),
        out_shape=[jax.ShapeDtypeStruct(r.shape, dt) for r, dt in zip(recvd, out_dtypes)],
        compiler_params=_params(("arbitrary",)),
    )(idx, *grads, *recvd)


def _final_sum_call(idx, grads, recvd1, recvd2):
    n = len(grads)
    nsteps = 2

    def body(i_ref, *refs):
        for t in range(n):
            g, r1, r2, o = refs[t], refs[n + t], refs[2 * n + t], refs[3 * n + t]
            s = g[0, 0] + r1[0]
            for r in range(3):
                s = s + r2[r].astype(F32)
            o[...] = s

    def rows(a):
        r = a.shape[-2]
        return r // nsteps if (r // nsteps) % 16 == 0 else r

    def step(a):
        return (lambda i: i) if rows(a) != a.shape[-2] else (lambda i: 0)

    in_specs = [pl.BlockSpec((1, 1, rows(g), g.shape[3]), lambda i, s, st=step(g): (s[0], s[1], st(i), 0)) for g in grads]
    in_specs += [pl.BlockSpec((1, rows(r), r.shape[2]), lambda i, s, st=step(r): (s[0], st(i), 0)) for r in recvd1]
    in_specs += [pl.BlockSpec((3, rows(r), r.shape[2]), lambda i, s, st=step(r): (0, st(i), 0)) for r in recvd2]
    return pl.pallas_call(
        body, name="rs_final_sum",
        grid_spec=pltpu.PrefetchScalarGridSpec(
            num_scalar_prefetch=1, grid=(nsteps,), in_specs=in_specs,
            out_specs=[pl.BlockSpec((rows(r), r.shape[2]), lambda i, s, st=step(r): (st(i), 0)) for r in recvd2]),
        out_shape=[jax.ShapeDtypeStruct(r.shape[1:], F32) for r in recvd2],
        compiler_params=_params(("arbitrary",)),
    )(idx, *grads, *recvd1, *recvd2)


def _sum8_call(parts):
    def body(p_ref, o_ref):
        s = p_ref[0]
        for j in range(1, N_DEV):
            s = s + p_ref[j]
        o_ref[...] = s

    return pl.pallas_call(body, name="sum_small_partials",
                          out_shape=jax.ShapeDtypeStruct(parts.shape[1:], parts.dtype))(parts)


def _adamw(w, g, m, v):
    m = ADAM_B1 * m + (1.0 - ADAM_B1) * g
    v = ADAM_B2 * v + (1.0 - ADAM_B2) * (g * g)
    m_hat = m / (1.0 - ADAM_B1 ** ADAM_STEP)
    v_hat = v / (1.0 - ADAM_B2 ** ADAM_STEP)
    delta = -ADAM_LR * (m_hat / (jnp.sqrt(v_hat) + ADAM_EPS) + ADAM_WD * w)
    return delta, m, v


def _adamw_call(ws, gs, ms, vs, nsteps, name):
    n = len(ws)

    def body(*refs):
        for t in range(n):
            w, g, m, v = (refs[k * n + t][...] for k in range(4))
            d, m2, v2 = _adamw(w, g, m, v)
            refs[4 * n + t][...] = d
            refs[5 * n + t][...] = m2
            refs[6 * n + t][...] = v2

    def spec(a):
        assert a.shape[0] % nsteps == 0 and (nsteps == 1 or (a.shape[0] // nsteps) % 8 == 0), a.shape
        return pl.BlockSpec((a.shape[0] // nsteps, a.shape[1]), lambda i: (i, 0))

    specs = [spec(a) for a in ws]
    outs = pl.pallas_call(
        body, name=name, grid=(nsteps,),
        in_specs=specs * 4, out_specs=specs * 3,
        out_shape=[jax.ShapeDtypeStruct(a.shape, F32) for a in ws] * 3,
        compiler_params=_params(("arbitrary",)),
    )(*ws, *gs, *ms, *vs)
    return outs[:n], outs[n:2 * n], outs[2 * n:]


def _adamw_rs_call(idx, after, gws, r1s, r2s, ws, ms, vs, nsteps, name):
    n = len(ws)

    def body(i_ref, after_ref, *refs):
        for t in range(n):
            gw, r1, r2, w, m, v = (refs[k * n + t] for k in range(6))
            g = gw[0, 0] + r1[0]
            for r in range(3):
                g = g + r2[r].astype(F32)
            d, m2, v2 = _adamw(w[...], g, m[...], v[...])
            refs[6 * n + t][...] = g
            refs[7 * n + t][...] = d
            refs[8 * n + t][...] = m2
            refs[9 * n + t][...] = v2

    def rb(a):
        r = a.shape[0] // nsteps
        assert a.shape[0] % nsteps == 0 and r % 16 == 0, a.shape
        return r

    in_specs = [pl.BlockSpec((1, 1, rb(w), w.shape[1]), lambda i, s: (s[0], s[1], i, 0)) for w in ws]
    in_specs += [pl.BlockSpec((1, rb(w), w.shape[1]), lambda i, s: (s[0], i, 0)) for w in ws]
    in_specs += [pl.BlockSpec((3, rb(w), w.shape[1]), lambda i, s: (0, i, 0)) for w in ws]
    plain = [pl.BlockSpec((rb(w), w.shape[1]), lambda i, s: (i, 0)) for w in ws]
    outs = pl.pallas_call(
        body, name=name,
        grid_spec=pltpu.PrefetchScalarGridSpec(
            num_scalar_prefetch=1, grid=(nsteps,),
            in_specs=[pl.BlockSpec(memory_space=pl.ANY)] + in_specs + plain * 3, out_specs=plain * 4),
        out_shape=[jax.ShapeDtypeStruct(w.shape, F32) for w in ws] * 4,
        compiler_params=_params(("arbitrary",)),
    )(idx, after, *gws, *r1s, *r2s, *ws, *ms, *vs)
    return outs[:n], outs[n:2 * n], outs[2 * n:3 * n], outs[3 * n:]


def _rows128(a, pad_rows):
    flat = a.reshape(-1).astype(F32)
    flat = jnp.pad(flat, (0, pad_rows * LANES - flat.shape[0]))
    return flat.reshape(pad_rows, LANES)


_SMALL_A = (("w_pool", 512), ("pool_scale", 8), ("attn_sinks", 8), ("g_mix_post", 8), ("g_mlp_pre", 8),
            ("g_mlp_post", 8), ("loss", 8), ("b_in_gates", 16))
_SMALL_A_ROWS = 640
_SMALL_B = (("g_mix_pre", 8), ("b_in_head", 16))


def _pack(parts, layout, total_rows):
    rows = [_rows128(parts[k], r) for k, r in layout]
    pad = total_rows - sum(r for _, r in layout)
    if pad:
        rows.append(jnp.zeros((pad, LANES), F32))
    return jnp.concatenate(rows, axis=0)


def _unpack(buf, layout, sizes):
    out, off = {}, 0
    for k, r in layout:
        out[k] = buf[off:off + r].reshape(-1)[:sizes[k]]
        off += r
    return out


def kernel(x, g_mix_pre, w_in, b_in, w_pool, pool_scale, attn_sinks, w_branch_pool, w_branch_attn, w_out, g_mix_post, g_mlp_pre, w_up, w_down, g_mlp_post, loss_target, m_g_mix_pre, m_w_in, m_b_in, m_w_pool, m_pool_scale, m_attn_sinks, m_w_branch_pool, m_w_branch_attn, m_w_out, m_g_mix_post, m_g_mlp_pre, m_w_up, m_w_down, m_g_mlp_post, v_g_mix_pre, v_w_in, v_b_in, v_w_pool, v_pool_scale, v_attn_sinks, v_w_branch_pool, v_w_branch_attn, v_w_out, v_g_mix_post, v_g_mlp_pre, v_w_up, v_w_down, v_g_mlp_post):
    B, S, _ = x.shape
    T = B * S
    xt = x.reshape(T, D_MODEL)
    tgt = loss_target.reshape(T, D_MODEL)
    cx, cy, cc = _coords()

    cidx = jnp.stack([2 * cx + cy, cc]).astype(jnp.int32)
    by_chip = lambda gr: gr.reshape((4, 2) + gr.shape[1:])
    bf = lambda w: w[0].astype(MXU_DTYPE)

    me = _slot((cx, cy, cc))
    win_l = w_in[0].T.astype(MXU_DTYPE)
    (win_s,) = _allgather_call([win_l], [_gather_buffer(win_l, me)])
    win_t = win_s.reshape(IN_WIDTH, D_MODEL)
    wpool_b = bf(w_pool)
    rc, rsa, rsb = _rot_tables(S)

    wbp_l, wba_l, wout_l, wup_l, wdown_l = bf(w_branch_pool), bf(w_branch_attn), bf(w_out), bf(w_up), bf(w_down)
    (c_br, c_up, c_dn), tok = _copies_start(
        [_rider_ag_remote([wbp_l, wba_l, wout_l], me), _rider_ag_remote([wup_l], me), _rider_ag_remote([wdown_l], me)],
        "allgather_start", after=win_s)
    (h, u, q, k4, v4, g), _ = _inproj_call(xt, g_mix_pre, win_t, b_in, rc, rsa, rsb, S, rider=_after(tok))
    yp = _pool_call(u, wpool_b, pool_scale, S)
    wbp_1, wba_1, wout_1 = _copies_wait([c_br], yp, "allgather_wait_branch")
    (ya,), (wbp_s, wba_s, wout_s) = _attn_call(
        attn_sinks, q, k4, v4, S,
        rider=_rider_ag([(None, wbp_1, None, ALL), (None, wba_1, None, ALL), (None, wout_1, None, ALL)]))
    wout_f = wout_s.reshape(D_MODEL, D_MODEL)
    (wup_1,) = _copies_wait([c_up], ya, "allgather_wait_up")
    (mix, x1, h2), (wup_s,) = _mix_fwd_call(
        yp, ya, g, xt, wbp_s, wba_s, wout_f, g_mix_post, g_mlp_pre, rider=_rider_ag([(None, wup_1, None, ALL)]))
    act = _mlp_up_call(h2, wup_s)
    (wdown_1,) = _copies_wait([c_dn], act, "allgather_wait_down")
    (wdown_s,) = _comm_call(_rider_ag([(None, wdown_1, None, ALL)]), "allgather_pass_down")

    da, dff, dx1, dg3, dg4, lossvec = _mlp_call(x1, act, tgt, wup_s, wdown_s, g_mlp_pre, g_mlp_post)
    gw_down = by_chip(_wgrad_rows_call(act, dff, "wgrad_down")[0])
    (gw_up,), (r1_down,) = _wgrad_cols_call(h2, da, "wgrad_up", rider=_rider_rs_sibling([gw_down]))
    gw_up = by_chip(gw_up)
    (s_down,) = _chip_sum_call(cidx, [gw_down], [r1_down], [MXU_DTYPE], "rs_chip_sum_down")
    (dyp, do, dgates, dg2, dbg, gw_out, gw_bp, gw_ba), (r1_up,) = _mix_bwd_call(
        dx1, mix, yp, ya, g, wbp_s, wba_s, wout_f, g_mix_post, rider=_rider_rs_sibling([gw_up]))
    gw_out = by_chip(gw_out.reshape(N_DEV, D_MODEL // N_DEV, D_MODEL))
    gw_bp, gw_ba = by_chip(gw_bp), by_chip(gw_ba)
    (s_up,) = _chip_sum_call(cidx, [gw_up], [r1_up], [MXU_DTYPE], "rs_chip_sum_up")
    (c_down, c_up), tok = _copies_start([_rider_rs_chips([s_down]), _rider_rs_chips([s_up])], "rs_chips_start_mlp")
    (dq, dk, dv, dsink), (r1_out, r1_bp, r1_ba) = _attn_bwd_call(
        attn_sinks, q, k4, v4, do, rc, rsa, rsb, S, rider=_after(tok, _rider_rs_sibling([gw_out, gw_bp, gw_ba])))
    s_obb = _chip_sum_call(cidx, [gw_out, gw_bp, gw_ba], [r1_out, r1_bp, r1_ba], [MXU_DTYPE] * 3, "rs_chip_sum_branch")
    (c_obb,), tok = _copies_start([_rider_rs_chips(s_obb)], "rs_chips_start_branch")
    (du, dwp, dps), _ = _pool_bwd_call(u, dyp, wpool_b, pool_scale, S, rider=_after(tok))
    (gw_in,) = _wgrad_in_call(du, dq, dk, dv, dgates, h)
    gw_in = by_chip(gw_in)

    small_a = {"w_pool": dwp, "pool_scale": dps,
               "attn_sinks": jnp.sum(dsink.reshape(B, 8, LANES)[:, 0, :N_Q_HEADS], axis=0), "g_mix_post": dg2,
               "g_mlp_pre": dg3, "g_mlp_post": dg4, "loss": lossvec, "b_in_gates": dbg}
    gw_sa = by_chip(_pack(small_a, _SMALL_A, _SMALL_A_ROWS).reshape(N_DEV, _SMALL_A_ROWS // N_DEV, LANES))
    r1_in, r1_sa = _comm_call(_rider_rs_sibling([gw_in, gw_sa]), "rs_sibling_in")
    s_in, s_sa = _chip_sum_call(cidx, [gw_in, gw_sa], [r1_in, r1_sa], [MXU_DTYPE, F32], "rs_chip_sum_in")
    (c_in,), tok = _copies_start([_rider_rs_chips([s_in, s_sa])], "rs_chips_start_in")
    (gx, dg1, dba_in), _ = _inproj_bwd_call(du, dq, dk, dv, dgates, dx1, xt, win_t, g_mix_pre, rider=_after(tok))
    r2_down, r2_up, r2_out, r2_bp, r2_ba, r2_in, r2_sa = _copies_wait([c_down, c_up, c_obb, c_in], dg1, "rs_chips_wait")

    (g_sa,) = _final_sum_call(cidx, [gw_sa], [r1_sa], [r2_sa])
    part_b = _pack({"g_mix_pre": dg1, "b_in_head": dba_in}, _SMALL_B, sum(r for _, r in _SMALL_B))
    (c_small,), tok = _copies_start([_rider_gather_remote([g_sa, part_b])], "allgather_small_start")

    in_t = _adamw_rs_call(cidx, tok, [gw_in], [r1_in], [r2_in], [w_in[0].T], [m_w_in[0].T], [v_w_in[0].T], 2,
                          "adamw_w_in")
    rest = _adamw_rs_call(
        cidx, tok, [gw_bp, gw_ba, gw_out, gw_up, gw_down], [r1_bp, r1_ba, r1_out, r1_up, r1_down],
        [r2_bp, r2_ba, r2_out, r2_up, r2_down], [w_branch_pool[0], w_branch_attn[0], w_out[0], w_up[0], w_down[0]],
        [m_w_branch_pool[0], m_w_branch_attn[0], m_w_out[0], m_w_up[0], m_w_down[0]],
        [v_w_branch_pool[0], v_w_branch_attn[0], v_w_out[0], v_w_up[0], v_w_down[0]], N_DEV, "adamw_shards")
    big_g, big_d, big_m2, big_v2 = ([a[0].T] + list(b) for a, b in zip(in_t, rest))

    sa_all, sb_all = _copies_wait([c_small], rest[0][0], "allgather_small_wait")
    sa_all = lax.dynamic_update_slice(sa_all, g_sa[None], (me, 0, 0))
    sb_sum = _sum8_call(lax.dynamic_update_slice(sb_all, part_b[None], (me, 0, 0)))

    names = ["g_mix_pre", "b_in", "w_pool", "pool_scale", "attn_sinks", "g_mix_post", "g_mlp_pre", "g_mlp_post"]
    sm_w = dict(g_mix_pre=g_mix_pre, b_in=b_in, w_pool=w_pool, pool_scale=pool_scale, attn_sinks=attn_sinks,
                g_mix_post=g_mix_post, g_mlp_pre=g_mlp_pre, g_mlp_post=g_mlp_post)
    sm_m = dict(g_mix_pre=m_g_mix_pre, b_in=m_b_in, w_pool=m_w_pool, pool_scale=m_pool_scale, attn_sinks=m_attn_sinks,
                g_mix_post=m_g_mix_post, g_mlp_pre=m_g_mlp_pre, g_mlp_post=m_g_mlp_post)
    sm_v = dict(g_mix_pre=v_g_mix_pre, b_in=v_b_in, w_pool=v_w_pool, pool_scale=v_pool_scale, attn_sinks=v_attn_sinks,
                g_mix_post=v_g_mix_post, g_mlp_pre=v_g_mlp_pre, g_mlp_post=v_g_mlp_post)
    sizes = {k: sm_w[k].size for k in names}
    sizes.update(loss=D_MODEL, b_in_gates=GATE_WIDTH, b_in_head=C_G)
    sm_g = _unpack(sa_all.reshape(_SMALL_A_ROWS, LANES), _SMALL_A, sizes)
    sm_g.update(_unpack(sb_sum, _SMALL_B, sizes))
    sm_g["b_in"] = jnp.concatenate([sm_g["b_in_head"], sm_g["b_in_gates"]])
    loss = (0.5 / D_MODEL) * jnp.sum(sm_g["loss"])
    two_d = lambda a: a.reshape(-1, a.shape[-1])
    sd_, sm2_, sv2_ = _adamw_call([two_d(sm_w[k]) for k in names], [two_d(sm_g[k].reshape(sm_w[k].shape)) for k in names],
                                  [two_d(sm_m[k]) for k in names], [two_d(sm_v[k]) for k in names], 1, "adamw_small")
    like = lambda vals: {k: a.reshape(sm_w[k].shape) for k, a in zip(names, vals)}
    sm_d, sm_m2, sm_v2 = like(sd_), like(sm2_), like(sv2_)
    sm_gr = {k: sm_g[k].reshape(sm_w[k].shape) for k in names}

    order = ["g_mix_pre", "w_in", "b_in", "w_pool", "pool_scale", "attn_sinks", "w_branch_pool", "w_branch_attn",
             "w_out", "g_mix_post", "g_mlp_pre", "w_up", "w_down", "g_mlp_post"]
    big_names = ["w_in", "w_branch_pool", "w_branch_attn", "w_out", "w_up", "w_down"]
    lead = lambda a: a[None]
    tables = []
    for small_t, big_t in ((sm_gr, big_g), (sm_d, big_d), (sm_m2, big_m2), (sm_v2, big_v2)):
        bt = dict(zip(big_names, big_t))
        tables.append([lead(bt[k]) if k in bt else small_t[k] for k in order])
    return (loss, gx.reshape(B, S, D_MODEL), *tables[0], *tables[1], *tables[2], *tables[3])
```

```python
import jax
import jax.numpy as jnp
from jax import lax
from jax.experimental import pallas as pl
from jax.experimental.pallas import tpu as pltpu

F32 = jnp.float32
MXU_DTYPE = jnp.bfloat16
MESH = pl.DeviceIdType.MESH

D_MODEL = 1024
POOL_WINDOWS = (2, 4, 8, 16)
POOL_WIDTH = 512
POOL_GC = 128
HEAD_DIM = 64
N_Q_HEADS = 8
N_KV_HEADS = 2
GROUP = 4
ATTN_WIDTH = 512
KV_WIDTH = 128
BLOCK = 128
GATE_WIDTH = 2048
IN_WIDTH = 3328
D_FF = 4096
EPS = 1e-6
NEG_INF = -1e30
ROPE_THETA = 500000.0
ROT_DIM = 16
SCALE = HEAD_DIM ** -0.5
C_Q, C_K, C_V, C_G = 512, 1024, 1152, 1280

ADAM_LR = 0.001
ADAM_B1 = 0.9
ADAM_B2 = 0.999
ADAM_EPS = 1e-08
ADAM_WD = 0.01
ADAM_STEP = 10

N_DEV = 8
LANES = 128
VMEM_LIMIT = 56 * 1024 * 1024

NN = (((1,), (0,)), ((), ()))
NT = (((1,), (1,)), ((), ()))
TN = (((0,), (0,)), ((), ()))


def _dot(a, b, dims):
    return lax.dot_general(a, b, dims, preferred_element_type=F32)


def _params(sem=None):
    return pltpu.CompilerParams(dimension_semantics=sem, vmem_limit_bytes=VMEM_LIMIT)


def _tile(n, pref):
    t = min(n, pref)
    assert n % t == 0, (n, t)
    return t


class _Rider:
    def __init__(self, ins, out_shape, n_remote, n_local, plan, aliases=None, lands=None):
        self.ins, self.out_shape, self.n_remote, self.n_local = list(ins), list(out_shape), n_remote, n_local
        self.plan, self.aliases = plan, dict(aliases or {})
        self.lands = lands


def _after(token, rider=None):
    r = rider or _Rider([], [], 0, 0, lambda ins, outs, send, recv, loc, r0, l0: ([], []))
    return _Rider(r.ins + [token], r.out_shape, r.n_remote, r.n_local, r.plan, r.aliases)


def _launch(body, args, *, name, grid, in_specs, out_specs, out_shape, scratch_shapes=(), sem=None, rider=None):
    if rider is None:
        return pl.pallas_call(body, name=name, grid=grid, in_specs=in_specs, out_specs=out_specs, out_shape=out_shape,
                              scratch_shapes=list(scratch_shapes), compiler_params=_params(sem))(*args)
    n_in, n_out, n_scr = len(args), len(out_shape), len(scratch_shapes)
    r_in, r_out = len(rider.ins), len(rider.out_shape)
    copies = rider.n_remote + rider.n_local > 0

    def wrapped(*refs):
        ins, rins = refs[:n_in], refs[n_in:n_in + r_in]
        o0 = n_in + r_in
        outs, routs = refs[o0:o0 + n_out], refs[o0 + n_out:o0 + n_out + r_out]
        s0 = o0 + n_out + r_out
        scr = refs[s0:s0 + n_scr]
        if not copies:
            return body(*ins, *outs, *scr)
        send, recv, loc = refs[s0 + n_scr:]
        first, last = None, None
        for d in range(len(grid)):
            f, l = pl.program_id(d) == 0, pl.program_id(d) == pl.num_programs(d) - 1
            first = f if first is None else first & f
            last = l if last is None else last & l

        def start():
            remote, local = rider.plan(rins, routs, send, recv, loc, 0, 0)
            for cp in local + remote:
                cp.start()

        def finish():
            remote, local = rider.plan(rins, routs, send, recv, loc, 0, 0)
            for cp in remote + local:
                cp.wait()

        if first is None:
            start()
            body(*ins, *outs, *scr)
            finish()
        else:
            pl.when(first)(start)
            body(*ins, *outs, *scr)
            pl.when(last)(finish)

    hbm = pl.BlockSpec(memory_space=pl.ANY)
    dma = pltpu.SemaphoreType.DMA
    res = pl.pallas_call(
        wrapped, name=name, grid=grid, in_specs=list(in_specs) + [hbm] * r_in,
        out_specs=list(out_specs) + [hbm] * r_out, out_shape=list(out_shape) + rider.out_shape,
        scratch_shapes=list(scratch_shapes) + (
            [dma((max(rider.n_remote, 1),)), dma((max(rider.n_remote, 1),)), dma((max(rider.n_local, 1),))] if copies else []),
        input_output_aliases={n_in + i: n_out + o for i, o in rider.aliases.items()},
        compiler_params=_params(sem),
    )(*args, *rider.ins)
    return list(res[:n_out]), list(res[n_out:])


def _comm_call(rider, name):
    return _launch(lambda: None, [], name=name, grid=(), in_specs=[], out_specs=[], out_shape=[], rider=rider)[1]


_HBM = pl.BlockSpec(memory_space=pltpu.HBM)
_SEM = pl.BlockSpec(memory_space=pltpu.SEMAPHORE)
_EFFECT = pltpu.SideEffectType.DATAFLOW_SIDE_EFFECTING


def _copies_start(riders, name, after=None):
    assert all(r.n_local == 0 and not r.aliases for r in riders)
    extra = [] if after is None else [after]
    sizes = [(len(r.ins), len(r.out_shape)) for r in riders]
    bufs = []
    for r in riders:
        lands = r.lands or [lax.empty(s.shape, s.dtype) for s in r.out_shape]
        bufs += [pltpu.with_memory_space_constraint(a, pltpu.HBM) for a in list(r.ins) + list(lands)]
    nb, ng, ne = len(bufs), len(riders), len(extra)

    def body(*refs):
        sems, token, at = refs[2 * nb + ne:2 * nb + ne + 2 * ng], refs[-1], 0
        for g, (r, (ni, no)) in enumerate(zip(riders, sizes)):
            remote, _ = r.plan(refs[at:at + ni], refs[at + ni:at + ni + no], sems[2 * g], sems[2 * g + 1], None, 0, 0)
            for cp in remote:
                cp.start()
            at += ni + no
        token[...] = jnp.zeros_like(token)

    res = pl.pallas_call(
        body, name=name, in_specs=[_HBM] * nb + [pl.BlockSpec(memory_space=pl.ANY)] * ne,
        out_specs=[_HBM] * nb + [_SEM] * (2 * ng) + [pl.BlockSpec(memory_space=pltpu.VMEM)],
        out_shape=[pltpu.HBM(a.shape, a.dtype) for a in bufs]
        + [pltpu.SemaphoreType.DMA((r.n_remote,)) for r in riders for _ in range(2)]
        + [jax.ShapeDtypeStruct((8, LANES), F32)],
        input_output_aliases={i: i for i in range(nb)},
        compiler_params=pltpu.CompilerParams(has_side_effects=_EFFECT),
    )(*bufs, *extra)
    handles, at = [], 0
    for g, (r, (ni, no)) in enumerate(zip(riders, sizes)):
        handles.append((r, list(res[at:at + ni + no]), res[nb + 2 * g], res[nb + 2 * g + 1]))
        at += ni + no
    return handles, res[-1]


def _copies_wait(handles, after, name):
    bufs = [b for _, bs, _, _ in handles for b in bs]
    sems = [s for _, _, send, recv in handles for s in (send, recv)]
    nb, ng = len(bufs), len(handles)

    def body(*refs):
        at = 0
        for g, (rider, bs, _, _) in enumerate(handles):
            ni = len(rider.ins)
            remote, _ = rider.plan(refs[at:at + ni], refs[at + ni:at + len(bs)], refs[nb + 2 * g], refs[nb + 2 * g + 1],
                                   None, 0, 0)
            for cp in remote:
                cp.wait_send()
                cp.wait_recv()
            at += len(bs)

    res = pl.pallas_call(
        body, name=name, in_specs=[_HBM] * nb + [_SEM] * (2 * ng) + [pl.BlockSpec(memory_space=pl.ANY)],
        out_specs=[_HBM] * nb, out_shape=[pltpu.HBM(a.shape, a.dtype) for a in bufs],
        input_output_aliases={i: i for i in range(nb)},
        compiler_params=pltpu.CompilerParams(has_side_effects=_EFFECT),
    )(*bufs, *sems, after)
    lands, at = [], 0
    for rider, bs, _, _ in handles:
        lands += list(res[at + len(rider.ins):at + len(bs)])
        at += len(bs)
    return lands


def _rms_r(x):
    return lax.rsqrt(jnp.mean(x * x, axis=-1, keepdims=True) + EPS)


def _rms_bwd(dn, x, r, g):
    xh = x * r
    dxh = dn * g
    dx = r * (dxh - xh * jnp.mean(dxh * xh, axis=-1, keepdims=True))
    return dx, dn * xh


def _rot(t, c, sa, sb):
    outs = []
    for j in range(t.shape[1] // LANES):
        tj = t[:, LANES * j:LANES * (j + 1)]
        outs.append(tj * c + pltpu.roll(tj, LANES - 8, 1) * sa + pltpu.roll(tj, 8, 1) * sb)
    return outs[0] if len(outs) == 1 else jnp.concatenate(outs, axis=1)


def _rot_tables(S):
    pos = jnp.arange(S, dtype=F32)
    inv_freq = ROPE_THETA ** (-jnp.arange(0, ROT_DIM, 2, dtype=F32) / ROT_DIM)
    ang = pos[:, None] * inv_freq[None, :]
    cos, sin = jnp.cos(ang), jnp.sin(ang)
    one = jnp.ones((S, HEAD_DIM - ROT_DIM), F32)
    zero = jnp.zeros((S, HEAD_DIM - ROT_DIM), F32)
    z8 = jnp.zeros((S, 8), F32)
    c = jnp.concatenate([cos, cos, one], axis=1)
    sa = jnp.concatenate([-sin, z8, zero], axis=1)
    sb = jnp.concatenate([z8, sin, zero], axis=1)
    rep = LANES // HEAD_DIM
    return jnp.tile(c, (1, rep)), jnp.tile(sa, (1, rep)), jnp.tile(sb, (1, rep))


def _lane_tile4(k):
    lane = lax.broadcasted_iota(jnp.int32, k.shape, 1)
    rk = pltpu.roll(k, HEAD_DIM, 1)
    x0 = jnp.where(lane < HEAD_DIM, k, rk)
    x1 = jnp.where(lane < HEAD_DIM, rk, k)
    return jnp.concatenate([x0, x0, x1, x1], axis=1)


def _fold_heads(acc):
    zs = []
    for hk in range(N_KV_HEADS):
        a = acc[:, 256 * hk:256 * hk + LANES] + acc[:, 256 * hk + LANES:256 * (hk + 1)]
        zs.append(a + pltpu.roll(a, HEAD_DIM, 1))
    lane = lax.broadcasted_iota(jnp.int32, zs[0].shape, 1)
    return jnp.where(lane < HEAD_DIM, zs[0], zs[1])


def _inproj_call(x, g1, win_t, b_in, rc, rsa, rsb, S, rider=None):
    T = x.shape[0]
    tm = _tile(S, 512)
    nst = S // tm

    def body(x_ref, g1_ref, w_ref, b_ref, c_ref, sa_ref, sb_ref,
             h_ref, u_ref, q_ref, k4_ref, v4_ref, g_ref):
        xv = x_ref[...]
        hb = ((xv * _rms_r(xv)) * g1_ref[...]).astype(MXU_DTYPE)
        h_ref[...] = hb

        def proj(lo, hi):
            return _dot(hb, w_ref[lo:hi, :], NT) + b_ref[:, lo:hi]

        c, sa, sb = c_ref[...], sa_ref[...], sb_ref[...]
        u_ref[...] = proj(0, C_Q)
        q_ref[...] = (_rot(proj(C_Q, C_K), c, sa, sb) * SCALE).astype(MXU_DTYPE)
        kv = proj(C_K, C_G)
        k4_ref[...] = _lane_tile4(_rot(kv[:, :KV_WIDTH], c, sa, sb)).astype(MXU_DTYPE)
        v4_ref[...] = _lane_tile4(kv[:, KV_WIDTH:]).astype(MXU_DTYPE)
        g_ref[...] = jax.nn.sigmoid(proj(C_G, IN_WIDTH)).astype(MXU_DTYPE)

    tok = lambda w: pl.BlockSpec((tm, w), lambda i: (i, 0))
    full = lambda a: pl.BlockSpec(a.shape, lambda i: (0,) * a.ndim)
    tab = pl.BlockSpec((tm, LANES), lambda i: (i % nst, 0))
    return _launch(
        body, [x, g1, win_t, b_in, rc, rsa, rsb], name="inproj_fwd", grid=(T // tm,),
        in_specs=[tok(D_MODEL), full(g1), full(win_t), full(b_in), tab, tab, tab],
        out_specs=[tok(D_MODEL), tok(POOL_WIDTH), tok(ATTN_WIDTH), tok(512), tok(512), tok(GATE_WIDTH)],
        out_shape=[jax.ShapeDtypeStruct((T, D_MODEL), MXU_DTYPE), jax.ShapeDtypeStruct((T, POOL_WIDTH), F32),
                   jax.ShapeDtypeStruct((T, ATTN_WIDTH), MXU_DTYPE), jax.ShapeDtypeStruct((T, 512), MXU_DTYPE),
                   jax.ShapeDtypeStruct((T, 512), MXU_DTYPE), jax.ShapeDtypeStruct((T, GATE_WIDTH), MXU_DTYPE)],
        sem=("arbitrary",), rider=rider)


def _shift_rows(a, k, rows):
    n = a.shape[0]
    if k > 0:
        return jnp.where(rows >= k, pltpu.roll(a, k, 0), 0.0)
    return jnp.where(rows < n + k, pltpu.roll(a, n + k, 0), 0.0)


def _win_sum(a, w, rows, sign):
    s, k = a, 1
    while k < w:
        s = s + _shift_rows(s, sign * k, rows)
        k *= 2
    return s


def _pool_diff(ug, w, rows):
    inv = 1.0 / jnp.minimum(rows + 1, w).astype(F32)
    return _win_sum(ug, w, rows, 1) * inv - ug, inv


def _pool_call(u, w_pool, pool_scale, S):
    T = u.shape[0]

    def body(u_ref, w_ref, ps_ref, y_ref):
        rows = lax.broadcasted_iota(jnp.int32, (S, POOL_GC), 0)
        for gi, w in enumerate(POOL_WINDOWS):
            sl = slice(POOL_GC * gi, POOL_GC * (gi + 1))
            diff, _ = _pool_diff(u_ref[:, sl], w, rows)
            mixed = _dot(diff.astype(MXU_DTYPE), w_ref[gi], NN)
            y_ref[:, sl] = (mixed * ps_ref[:, sl]).astype(MXU_DTYPE)

    seq = pl.BlockSpec((S, POOL_WIDTH), lambda b: (b, 0))
    return pl.pallas_call(
        body, name="pool_fwd", grid=(T // S,),
        in_specs=[seq, pl.BlockSpec(w_pool.shape, lambda b: (0, 0, 0)), pl.BlockSpec(pool_scale.shape, lambda b: (0, 0))],
        out_specs=seq, out_shape=jax.ShapeDtypeStruct((T, POOL_WIDTH), MXU_DTYPE),
        compiler_params=_params(("arbitrary",)),
    )(u, w_pool, pool_scale)


def _pool_bwd_call(u, dyp, w_pool, pool_scale, S, rider=None):
    T = u.shape[0]

    def body(u_ref, dy_ref, w_ref, ps_ref, du_ref, dw_ref, dps_ref):
        @pl.when(pl.program_id(0) == 0)
        def _():
            dw_ref[...] = jnp.zeros_like(dw_ref)
            dps_ref[...] = jnp.zeros_like(dps_ref)

        rows = lax.broadcasted_iota(jnp.int32, (S, POOL_GC), 0)
        for gi, w in enumerate(POOL_WINDOWS):
            sl = slice(POOL_GC * gi, POOL_GC * (gi + 1))
            diff, inv = _pool_diff(u_ref[:, sl], w, rows)
            diffb = diff.astype(MXU_DTYPE)
            wg = w_ref[gi]
            mixed = _dot(diffb, wg, NN)
            dy = dy_ref[:, sl]
            dps_ref[:, sl] += jnp.sum(dy * mixed, axis=0, keepdims=True)
            dmb = (dy * ps_ref[:, sl]).astype(MXU_DTYPE)
            dw_ref[gi] += _dot(diffb, dmb, TN)
            ddiff = _dot(dmb, wg, NT)
            du_ref[:, sl] = (_win_sum(ddiff * inv, w, rows, -1) - ddiff).astype(MXU_DTYPE)

    seq = pl.BlockSpec((S, POOL_WIDTH), lambda b: (b, 0))
    return _launch(
        body, [u, dyp, w_pool, pool_scale], name="pool_bwd", grid=(T // S,),
        in_specs=[seq, seq, pl.BlockSpec(w_pool.shape, lambda b: (0, 0, 0)), pl.BlockSpec(pool_scale.shape, lambda b: (0, 0))],
        out_specs=[seq, pl.BlockSpec(w_pool.shape, lambda b: (0, 0, 0)), pl.BlockSpec(pool_scale.shape, lambda b: (0, 0))],
        out_shape=[jax.ShapeDtypeStruct((T, POOL_WIDTH), MXU_DTYPE), jax.ShapeDtypeStruct(w_pool.shape, F32),
                   jax.ShapeDtypeStruct(pool_scale.shape, F32)],
        sem=("arbitrary",), rider=rider)


def _attn_consts():
    lane_g = lax.broadcasted_iota(jnp.int32, (BLOCK, 256), 1) >> 6
    rgrp = lax.broadcasted_iota(jnp.int32, (GROUP * BLOCK, 1), 0) >> 7
    rel = lax.broadcasted_iota(jnp.int32, (BLOCK, 256), 0) - lax.broadcasted_iota(jnp.int32, (BLOCK, 256), 1)

    def bias(off):
        ok = (rel + off >= 0) & (rel + off < BLOCK)
        return jnp.concatenate([jnp.where(ok, 0.0, NEG_INF)] * GROUP, axis=0)

    return lane_g, rgrp, bias(0), bias(BLOCK)


def _sink_rows(sink_ref, hk, rgrp):
    sv = jnp.zeros(rgrp.shape, F32)
    for g in range(GROUP):
        sv = jnp.where(rgrp == g, sink_ref[0, GROUP * hk + g], sv)
    return sv


def _stack_heads(xb, lane_g):
    return jnp.concatenate([jnp.where(lane_g == g, xb, jnp.zeros_like(xb)) for g in range(GROUP)], axis=0)


def _unstack_heads(xs, lane_g):
    out = jnp.where(lane_g == 0, xs[0:BLOCK], 0.0)
    for g in range(1, GROUP):
        out = out + jnp.where(lane_g == g, xs[BLOCK * g:BLOCK * (g + 1)], 0.0)
    return out


def _attn_probs(qs, kb, bias, sv):
    s = _dot(qs, kb, NT) + bias
    m = jnp.maximum(jnp.max(s, axis=1, keepdims=True), sv)
    e = jnp.exp(s - m)
    es = jnp.exp(sv - m)
    inv_l = 1.0 / (jnp.sum(e, axis=1, keepdims=True) + es)
    return e * inv_l, es * inv_l


def _attn_blocks(nb, blk, carry, per=1):
    carry = blk(0, 0, True, carry)
    per = per if (nb - 1) % per == 0 else 1

    def step(i, c):
        for k in range(per):
            n = 1 + per * i + k
            c = blk(pl.multiple_of(n * BLOCK, BLOCK), pl.multiple_of((n - 1) * BLOCK, BLOCK), False, c)
        return c

    return lax.fori_loop(0, (nb - 1) // per, step, carry)


def _attn_call(sinks, q, k4, v4, S, rider=None):
    T = q.shape[0]
    nb = S // BLOCK

    def body(sink_ref, q_ref, k_ref, v_ref, o_ref):
        lane_g, rgrp, bias_first, bias_later = _attn_consts()
        svs = [_sink_rows(sink_ref, hk, rgrp) for hk in range(N_KV_HEADS)]

        def blk(q0, k0, first, carry):
            for hk in range(N_KV_HEADS):
                cs = slice(256 * hk, 256 * (hk + 1))
                qs = _stack_heads(q_ref[pl.ds(q0, BLOCK), cs], lane_g)
                p, _ = _attn_probs(qs, k_ref[pl.ds(k0, 2 * BLOCK), cs], bias_first if first else bias_later, svs[hk])
                o = _dot(p.astype(MXU_DTYPE), v_ref[pl.ds(k0, 2 * BLOCK), cs], NN)
                o_ref[pl.ds(q0, BLOCK), cs] = _unstack_heads(o, lane_g).astype(MXU_DTYPE)
            return carry

        _attn_blocks(nb, blk, 0, per=3)

    seq = pl.BlockSpec((S, ATTN_WIDTH), lambda b: (b, 0))
    return _launch(
        body, [sinks, q, k4, v4], name="attn_fwd", grid=(T // S,),
        in_specs=[pl.BlockSpec(memory_space=pltpu.SMEM), seq, seq, seq],
        out_specs=[seq], out_shape=[jax.ShapeDtypeStruct((T, ATTN_WIDTH), MXU_DTYPE)],
        sem=("arbitrary",), rider=rider)


def _attn_bwd_call(sinks, q, k4, v4, do, rc, rsa, rsb, S, rider=None):
    T = q.shape[0]
    nb = S // BLOCK

    def body(sink_ref, q_ref, k_ref, v_ref, do_ref, c_ref, sa_ref, sb_ref,
             dq_ref, dk_ref, dv_ref, ds_ref, dk_acc, dv_acc):
        lane_g, rgrp, bias_first, bias_later = _attn_consts()
        svs = [_sink_rows(sink_ref, hk, rgrp) for hk in range(N_KV_HEADS)]
        lane1 = lax.broadcasted_iota(jnp.int32, (1, LANES), 1)
        dk_acc[...] = jnp.zeros_like(dk_acc)
        dv_acc[...] = jnp.zeros_like(dv_acc)

        def blk(q0, k0, first, dsink):
            rows = pl.ds(q0, BLOCK)
            c, sa, sb = c_ref[rows, :], sa_ref[rows, :], sb_ref[rows, :]
            for hk in range(N_KV_HEADS):
                cs = slice(256 * hk, 256 * (hk + 1))
                qs = _stack_heads(q_ref[rows, cs], lane_g)
                dos = _stack_heads(do_ref[rows, cs], lane_g)
                kb = k_ref[pl.ds(k0, 2 * BLOCK), cs]
                vb = v_ref[pl.ds(k0, 2 * BLOCK), cs]
                p, ps = _attn_probs(qs, kb, bias_first if first else bias_later, svs[hk])
                dp = _dot(dos, vb, NT)
                delta = jnp.sum(p * dp, axis=1, keepdims=True)
                dsb = (p * (dp - delta)).astype(MXU_DTYPE)
                dqb = _unstack_heads(_dot(dsb, kb, NN), lane_g) * SCALE
                dq_ref[rows, cs] = _rot(dqb, c, -sa, -sb).astype(MXU_DTYPE)
                dk_acc[pl.ds(k0, 2 * BLOCK), cs] += _dot(dsb, qs, TN)
                dv_acc[pl.ds(k0, 2 * BLOCK), cs] += _dot(p.astype(MXU_DTYPE), dos, TN)
                psd = ps * delta
                for g in range(GROUP):
                    val = -jnp.sum(psd[BLOCK * g:BLOCK * (g + 1)], axis=0, keepdims=True)
                    dsink = dsink + jnp.where(lane1 == GROUP * hk + g, val, 0.0)
            return dsink

        dsink = _attn_blocks(nb, blk, jnp.zeros((1, LANES), F32))
        dk_ref[...] = _rot(_fold_heads(dk_acc[...]), c_ref[...], -sa_ref[...], -sb_ref[...]).astype(MXU_DTYPE)
        dv_ref[...] = _fold_heads(dv_acc[...]).astype(MXU_DTYPE)
        ds_ref[...] = jnp.broadcast_to(dsink, ds_ref.shape)

    seq = pl.BlockSpec((S, ATTN_WIDTH), lambda b: (b, 0))
    kvs = pl.BlockSpec((S, KV_WIDTH), lambda b: (b, 0))
    tab = pl.BlockSpec((S, LANES), lambda b: (0, 0))
    nseq = T // S
    return _launch(
        body, [sinks, q, k4, v4, do, rc, rsa, rsb], name="attn_bwd", grid=(nseq,),
        in_specs=[pl.BlockSpec(memory_space=pltpu.SMEM), seq, seq, seq, seq, tab, tab, tab],
        out_specs=[seq, kvs, kvs, pl.BlockSpec((8, LANES), lambda b: (b, 0))],
        out_shape=[jax.ShapeDtypeStruct((T, ATTN_WIDTH), MXU_DTYPE), jax.ShapeDtypeStruct((T, KV_WIDTH), MXU_DTYPE),
                   jax.ShapeDtypeStruct((T, KV_WIDTH), MXU_DTYPE), jax.ShapeDtypeStruct((8 * nseq, LANES), F32)],
        scratch_shapes=[pltpu.VMEM((S, 512), F32), pltpu.VMEM((S, 512), F32)],
        sem=("arbitrary",), rider=rider)


def _branch_weights(wbp_ref, wba_ref, wbp_s, wba_s):
    @pl.when(pl.program_id(0) == 0)
    def _():
        for j in range(N_DEV):
            wbp_s[:, LANES * j:LANES * (j + 1)] = wbp_ref[j]
            wba_s[:, LANES * j:LANES * (j + 1)] = wba_ref[j]


def _mix_fwd_call(yp, ya, g, x, wbp, wba, wout, g2, g3, rider=None):
    T = x.shape[0]
    tm = _tile(T, 512)

    def body(yp_ref, ya_ref, g_ref, x_ref, wbp_ref, wba_ref, wout_ref, g2_ref, g3_ref,
             mix_ref, x1_ref, h2_ref, wbp_s, wba_s):
        _branch_weights(wbp_ref, wba_ref, wbp_s, wba_s)
        bp = _dot(yp_ref[...], wbp_s[...], NN)
        ba = _dot(ya_ref[...], wba_s[...], NN)
        merged = g_ref[:, :D_MODEL].astype(F32) * bp + g_ref[:, D_MODEL:].astype(F32) * ba
        mix = _dot(merged.astype(MXU_DTYPE), wout_ref[...], NN)
        mix_ref[...] = mix
        x1 = x_ref[...] + (mix * _rms_r(mix)) * g2_ref[...]
        x1_ref[...] = x1
        h2_ref[...] = ((x1 * _rms_r(x1)) * g3_ref[...]).astype(MXU_DTYPE)

    tok = lambda w: pl.BlockSpec((tm, w), lambda i: (i, 0))
    full = lambda a: pl.BlockSpec(a.shape, lambda i: (0,) * a.ndim)
    return _launch(
        body, [yp, ya, g, x, wbp, wba, wout, g2, g3], name="mix_fwd", grid=(T // tm,),
        in_specs=[tok(POOL_WIDTH), tok(ATTN_WIDTH), tok(GATE_WIDTH), tok(D_MODEL), full(wbp), full(wba), full(wout),
                  full(g2), full(g3)],
        out_specs=[tok(D_MODEL), tok(D_MODEL), tok(D_MODEL)],
        out_shape=[jax.ShapeDtypeStruct((T, D_MODEL), F32), jax.ShapeDtypeStruct((T, D_MODEL), F32),
                   jax.ShapeDtypeStruct((T, D_MODEL), MXU_DTYPE)],
        scratch_shapes=[pltpu.VMEM((POOL_WIDTH, D_MODEL), MXU_DTYPE), pltpu.VMEM((ATTN_WIDTH, D_MODEL), MXU_DTYPE)],
        sem=("arbitrary",), rider=rider)


def _mix_bwd_call(dx1, mix, yp, ya, g, wbp, wba, wout, g2, rider=None):
    T = dx1.shape[0]
    tm = _tile(T, 512)

    def body(dx1_ref, mix_ref, yp_ref, ya_ref, g_ref, wbp_ref, wba_ref, wout_ref, g2_ref,
             dyp_ref, do_ref, dgates_ref, dg2_ref, dbg_ref, gout_ref, gbp_ref, gba_ref,
             wbp_s, wba_s, acc_out, acc_bp, acc_ba, sem):
        _branch_weights(wbp_ref, wba_ref, wbp_s, wba_s)
        step = pl.program_id(0)

        @pl.when(step == 0)
        def _():
            dg2_ref[...] = jnp.zeros_like(dg2_ref)
            dbg_ref[...] = jnp.zeros_like(dbg_ref)
            acc_out[...] = jnp.zeros_like(acc_out)
            acc_bp[...] = jnp.zeros_like(acc_bp)
            acc_ba[...] = jnp.zeros_like(acc_ba)

        mix = mix_ref[...]
        dmix, dg2 = _rms_bwd(dx1_ref[...], mix, _rms_r(mix), g2_ref[...])
        dg2_ref[...] += jnp.sum(dg2, axis=0, keepdims=True)
        dmixb = dmix.astype(MXU_DTYPE)
        dmerged = _dot(dmixb, wout_ref[...], NT)
        yp, ya = yp_ref[...], ya_ref[...]
        bp = _dot(yp, wbp_s[...], NN)
        ba = _dot(ya, wba_s[...], NN)
        gp, ga = g_ref[:, :D_MODEL].astype(F32), g_ref[:, D_MODEL:].astype(F32)
        acc_out[...] += _dot((gp * bp + ga * ba).astype(MXU_DTYPE), dmixb, TN)
        dgp = dmerged * bp * (gp * (1.0 - gp))
        dga = dmerged * ba * (ga * (1.0 - ga))
        dbg_ref[:, :D_MODEL] += jnp.sum(dgp, axis=0, keepdims=True)
        dbg_ref[:, D_MODEL:] += jnp.sum(dga, axis=0, keepdims=True)
        dgates_ref[:, :D_MODEL] = dgp.astype(MXU_DTYPE)
        dgates_ref[:, D_MODEL:] = dga.astype(MXU_DTYPE)
        dbp = (dmerged * gp).astype(MXU_DTYPE)
        dba = (dmerged * ga).astype(MXU_DTYPE)
        acc_bp[...] += _dot(yp, dbp, TN)
        acc_ba[...] += _dot(ya, dba, TN)
        dyp_ref[...] = _dot(dbp, wbp_s[...], NT)
        do_ref[...] = _dot(dba, wba_s[...], NT).astype(MXU_DTYPE)

        @pl.when(step == pl.num_programs(0) - 1)
        def _():
            copies = [pltpu.make_async_copy(acc_out, gout_ref, sem.at[0])]
            for j in range(N_DEV):
                cols = slice(LANES * j, LANES * (j + 1))
                copies.append(pltpu.make_async_copy(acc_bp.at[:, cols], gbp_ref.at[j], sem.at[1 + j]))
                copies.append(pltpu.make_async_copy(acc_ba.at[:, cols], gba_ref.at[j], sem.at[1 + N_DEV + j]))
            for cp in copies:
                cp.start()
            for cp in copies:
                cp.wait()

    tok = lambda w: pl.BlockSpec((tm, w), lambda i: (i, 0))
    full = lambda a: pl.BlockSpec(a.shape, lambda i: (0,) * a.ndim)
    acc = lambda w: pl.BlockSpec((1, w), lambda i: (0, 0))
    hbm = pl.BlockSpec(memory_space=pl.ANY)
    sd = jax.ShapeDtypeStruct
    return _launch(
        body, [dx1, mix, yp, ya, g, wbp, wba, wout, g2], name="mix_bwd", grid=(T // tm,),
        in_specs=[tok(D_MODEL), tok(D_MODEL), tok(POOL_WIDTH), tok(ATTN_WIDTH), tok(GATE_WIDTH), full(wbp), full(wba),
                  full(wout), full(g2)],
        out_specs=[tok(POOL_WIDTH), tok(ATTN_WIDTH), tok(GATE_WIDTH), acc(D_MODEL), acc(GATE_WIDTH), hbm, hbm, hbm],
        out_shape=[sd((T, POOL_WIDTH), F32), sd((T, ATTN_WIDTH), MXU_DTYPE), sd((T, GATE_WIDTH), MXU_DTYPE),
                   sd((1, D_MODEL), F32), sd((1, GATE_WIDTH), F32), sd((D_MODEL, D_MODEL), F32),
                   sd((N_DEV, POOL_WIDTH, LANES), F32), sd((N_DEV, ATTN_WIDTH, LANES), F32)],
        scratch_shapes=[pltpu.VMEM((POOL_WIDTH, D_MODEL), MXU_DTYPE), pltpu.VMEM((ATTN_WIDTH, D_MODEL), MXU_DTYPE),
                        pltpu.VMEM((D_MODEL, D_MODEL), F32), pltpu.VMEM((POOL_WIDTH, D_MODEL), F32),
                        pltpu.VMEM((ATTN_WIDTH, D_MODEL), F32), pltpu.SemaphoreType.DMA((1 + 2 * N_DEV,))],
        sem=("arbitrary",), rider=rider)


def _mlp_up_call(h2, wup):
    T = h2.shape[0]
    tm = _tile(T, 512)
    fc = D_FF // N_DEV

    def body(h2_ref, wup_ref, act_ref):
        h2 = h2_ref[...]
        for j in range(N_DEV):
            rl = jnp.maximum(_dot(h2, wup_ref[j], NN), 0.0)
            act_ref[:, fc * j:fc * (j + 1)] = (rl * rl).astype(MXU_DTYPE)

    sd = jax.ShapeDtypeStruct
    return pl.pallas_call(
        body, name="mlp_up", grid=(T // tm,),
        in_specs=[pl.BlockSpec((tm, D_MODEL), lambda i: (i, 0)),
                  pl.BlockSpec(wup.shape, lambda i: (0, 0, 0), pipeline_mode=pl.Buffered(1))],
        out_specs=pl.BlockSpec((tm, D_FF), lambda i: (i, 0)), out_shape=sd((T, D_FF), MXU_DTYPE),
        compiler_params=_params(("arbitrary",)),
    )(h2, wup)


def _mlp_call(x1, act, target, wup, wdown, g3, g4):
    T = x1.shape[0]
    tm = _tile(T, 256)
    fc = D_FF // N_DEV

    def body(x1_ref, act_ref, t_ref, wup_ref, wdown_ref, g3_ref, g4_ref,
             da_ref, dff_ref, dx1_ref, dg3_ref, dg4_ref, loss_ref):
        @pl.when(pl.program_id(0) == 0)
        def _():
            dg3_ref[...] = jnp.zeros_like(dg3_ref)
            dg4_ref[...] = jnp.zeros_like(dg4_ref)
            loss_ref[...] = jnp.zeros_like(loss_ref)

        ff = jnp.zeros((tm, D_MODEL), F32)
        for j in range(N_DEV):
            ff = ff + _dot(act_ref[:, fc * j:fc * (j + 1)], wdown_ref[j], NN)
        x1 = x1_ref[...]
        r4 = _rms_r(ff)
        err = x1 + (ff * r4) * g4_ref[...] - t_ref[...]
        loss_ref[...] += jnp.sum(err * err, axis=0, keepdims=True)
        dy = err * (1.0 / D_MODEL)
        dff, dg4 = _rms_bwd(dy, ff, r4, g4_ref[...])
        dg4_ref[...] += jnp.sum(dg4, axis=0, keepdims=True)
        dffb = dff.astype(MXU_DTYPE)
        dff_ref[...] = dffb
        dh2 = jnp.zeros((tm, D_MODEL), F32)
        for j in range(N_DEV):
            sl = slice(fc * j, fc * (j + 1))
            rl = jnp.sqrt(act_ref[:, sl].astype(F32))
            dab = (_dot(dffb, wdown_ref[j], NT) * (2.0 * rl)).astype(MXU_DTYPE)
            da_ref[:, sl] = dab
            dh2 = dh2 + _dot(dab, wup_ref[j], NT)
        dx1, dg3 = _rms_bwd(dh2, x1, _rms_r(x1), g3_ref[...])
        dg3_ref[...] += jnp.sum(dg3, axis=0, keepdims=True)
        dx1_ref[...] = dy + dx1

    tok = lambda w: pl.BlockSpec((tm, w), lambda i: (i, 0))
    full = lambda a: pl.BlockSpec(a.shape, lambda i: (0,) * a.ndim, pipeline_mode=pl.Buffered(1))
    vec = pl.BlockSpec((1, D_MODEL), lambda i: (0, 0))
    sd = jax.ShapeDtypeStruct
    return pl.pallas_call(
        body, name="mlp_down_bwd", grid=(T // tm,),
        in_specs=[tok(D_MODEL), tok(D_FF), tok(D_MODEL), full(wup), full(wdown), vec, vec],
        out_specs=[tok(D_FF), tok(D_MODEL), tok(D_MODEL), vec, vec, vec],
        out_shape=[sd((T, D_FF), MXU_DTYPE), sd((T, D_MODEL), MXU_DTYPE),
                   sd((T, D_MODEL), F32), sd((1, D_MODEL), F32), sd((1, D_MODEL), F32), sd((1, D_MODEL), F32)],
        compiler_params=_params(("arbitrary",)),
    )(x1, act, target, wup, wdown, g3, g4)


def _inproj_bwd_call(du, dq, dk, dv, dgates, dx1, x, win_t, g1, rider=None):
    T = x.shape[0]
    tm = _tile(T, 512)

    def body(du_ref, dq_ref, dk_ref, dv_ref, dgt_ref, dx1_ref, x_ref, w_ref, g1_ref, gx_ref, dg1_ref, db_ref):
        @pl.when(pl.program_id(0) == 0)
        def _():
            dg1_ref[...] = jnp.zeros_like(dg1_ref)
            db_ref[...] = jnp.zeros_like(db_ref)

        dh = jnp.zeros((tm, D_MODEL), F32)
        for ref, lo, hi in ((du_ref, 0, C_Q), (dq_ref, C_Q, C_K), (dk_ref, C_K, C_V), (dv_ref, C_V, C_G),
                            (dgt_ref, C_G, IN_WIDTH)):
            piece = ref[...]
            dh = dh + _dot(piece, w_ref[lo:hi, :], NN)
            if hi <= C_G:
                db_ref[:, lo:hi] += jnp.sum(piece.astype(F32), axis=0, keepdims=True)
        xv = x_ref[...]
        dx, dg1 = _rms_bwd(dh, xv, _rms_r(xv), g1_ref[...])
        dg1_ref[...] += jnp.sum(dg1, axis=0, keepdims=True)
        gx_ref[...] = dx1_ref[...] + dx

    tok = lambda w: pl.BlockSpec((tm, w), lambda i: (i, 0))
    full = lambda a: pl.BlockSpec(a.shape, lambda i: (0,) * a.ndim)
    sd = jax.ShapeDtypeStruct
    return _launch(
        body, [du, dq, dk, dv, dgates, dx1, x, win_t, g1], name="inproj_bwd", grid=(T // tm,),
        in_specs=[tok(POOL_WIDTH), tok(ATTN_WIDTH), tok(KV_WIDTH), tok(KV_WIDTH), tok(GATE_WIDTH), tok(D_MODEL),
                  tok(D_MODEL), full(win_t), full(g1)],
        out_specs=[tok(D_MODEL), pl.BlockSpec((1, D_MODEL), lambda i: (0, 0)), pl.BlockSpec((1, C_G), lambda i: (0, 0))],
        out_shape=[sd((T, D_MODEL), F32), sd((1, D_MODEL), F32), sd((1, C_G), F32)],
        sem=("arbitrary",), rider=rider)


WGRAD_TOKENS = 1024


def _wgrad_rows_call(a, b, name, rider=None):
    T, K = a.shape
    N = b.shape[1]
    tm = _tile(T, WGRAD_TOKENS)
    kb = min(K, 1024)
    per = kb // (K // N_DEV)

    def body(a_ref, b_ref, o_ref):
        @pl.when(pl.program_id(1) == 0)
        def _():
            o_ref[...] = jnp.zeros_like(o_ref)

        d = _dot(a_ref[...], b_ref[...], TN)
        rs = kb // per
        for j in range(per):
            o_ref[j] += d[rs * j:rs * (j + 1)]

    return _launch(
        body, [a, b], name=name, grid=(K // kb, T // tm),
        in_specs=[pl.BlockSpec((tm, kb), lambda i, t: (t, i)), pl.BlockSpec((tm, N), lambda i, t: (t, 0))],
        out_specs=[pl.BlockSpec((per, K // N_DEV, N), lambda i, t: (i, 0, 0))],
        out_shape=[jax.ShapeDtypeStruct((N_DEV, K // N_DEV, N), F32)],
        sem=("arbitrary", "arbitrary"), rider=rider)


def _wgrad_cols_call(a, b, name, rider=None):
    T, K = a.shape
    N = b.shape[1]
    tm = _tile(T, WGRAD_TOKENS)
    nb = min(N, 1024)
    per = nb // (N // N_DEV)

    def body(a_ref, b_ref, o_ref):
        @pl.when(pl.program_id(1) == 0)
        def _():
            o_ref[...] = jnp.zeros_like(o_ref)

        d = _dot(a_ref[...], b_ref[...], TN)
        cs = nb // per
        for j in range(per):
            o_ref[j] += d[:, cs * j:cs * (j + 1)]

    return _launch(
        body, [a, b], name=name, grid=(N // nb, T // tm),
        in_specs=[pl.BlockSpec((tm, K), lambda i, t: (t, 0)), pl.BlockSpec((tm, nb), lambda i, t: (t, i))],
        out_specs=[pl.BlockSpec((per, K, N // N_DEV), lambda i, t: (i, 0, 0))],
        out_shape=[jax.ShapeDtypeStruct((N_DEV, K, N // N_DEV), F32)],
        sem=("arbitrary", "arbitrary"), rider=rider)


def _wgrad_in_call(du, dq, dk, dv, dgates, h, rider=None):
    T = h.shape[0]
    tm = _tile(T, WGRAD_TOKENS)
    rows = IN_WIDTH // N_DEV

    def body(du_ref, dq_ref, dk_ref, dv_ref, dgt_ref, h_ref, o_ref, acc, sem):
        t = pl.program_id(0)

        @pl.when(t == 0)
        def _():
            acc[...] = jnp.zeros_like(acc)

        hv = h_ref[...]
        for ref, lo, hi in ((du_ref, 0, C_Q), (dq_ref, C_Q, C_K), (dk_ref, C_K, C_V), (dv_ref, C_V, C_G),
                            (dgt_ref, C_G, IN_WIDTH)):
            acc[lo:hi, :] += _dot(ref[...], hv, TN)

        @pl.when(t == pl.num_programs(0) - 1)
        def _():
            copies = [pltpu.make_async_copy(acc.at[pl.ds(rows * j, rows), :], o_ref.at[j], sem.at[j])
                      for j in range(N_DEV)]
            for cp in copies:
                cp.start()
            for cp in copies:
                cp.wait()

    tok = lambda w: pl.BlockSpec((tm, w), lambda t: (t, 0))
    return _launch(
        body, [du, dq, dk, dv, dgates, h], name="wgrad_in", grid=(T // tm,),
        in_specs=[tok(POOL_WIDTH), tok(ATTN_WIDTH), tok(KV_WIDTH), tok(KV_WIDTH), tok(GATE_WIDTH), tok(D_MODEL)],
        out_specs=[pl.BlockSpec(memory_space=pl.ANY)],
        out_shape=[jax.ShapeDtypeStruct((N_DEV, rows, D_MODEL), F32)],
        scratch_shapes=[pltpu.VMEM((IN_WIDTH, D_MODEL), F32), pltpu.SemaphoreType.DMA((N_DEV,))],
        sem=("arbitrary",), rider=rider)


def _coords():
    return lax.axis_index("x"), lax.axis_index("y"), lax.axis_index("c")


def _allgather_call(shards, bufs):
    n = len(shards)

    def body(*refs):
        ins, outs = refs[:n], refs[2 * n:3 * n]
        send_sems, recv_sems = refs[3 * n:]
        x, y, c = _coords()
        me, sibling = (x, y, c), (x, y, 1 - c)
        fx, fy = 1 - c, c
        near1 = (x ^ fx, y ^ fy, c)
        near2 = (x ^ fy, y ^ fx, c)
        diag = (1 - x, 1 - y, c)
        arrivals = [near1, near2, diag]

        def slot(p):
            return 4 * p[0] + 2 * p[1] + p[2]

        def copy(t, k, block, to, src=None):
            dst = outs[t].at[slot(block)]
            return pltpu.make_async_remote_copy(
                src_ref=dst if src is None else src, dst_ref=dst, send_sem=send_sems.at[t, k],
                recv_sem=recv_sems.at[t, k], device_id=to, device_id_type=MESH)

        sent = []
        for t in range(n):
            sent += [copy(t, 0, me, sibling, src=ins[t]), copy(t, 1, me, near1, src=ins[t]),
                     copy(t, 2, me, near2, src=ins[t])]
        for cp in sent:
            cp.start()
        for t in range(n):
            for j, block in enumerate(arrivals):
                copy(t, 1 + j, block, me).wait_recv()
                onward = [copy(t, 4 + j, block, sibling)] + ([copy(t, 3, block, near2)] if j == 0 else [])
                for cp in onward[::-1]:
                    cp.start()
                sent += onward
        for t in range(n):
            copy(t, 0, sibling, me).wait_recv()
            for j, block in enumerate([near2, near1, diag]):
                copy(t, 4 + j, (block[0], block[1], 1 - c), me).wait_recv()
        for cp in sent:
            cp.wait_send()

    hbm = pl.BlockSpec(memory_space=pl.ANY)
    return pl.pallas_call(
        body, name="allgather_weights",
        in_specs=[hbm] * (2 * n), out_specs=[hbm] * n,
        out_shape=[jax.ShapeDtypeStruct(b.shape, b.dtype) for b in bufs],
        scratch_shapes=[pltpu.SemaphoreType.DMA((n, 7)), pltpu.SemaphoreType.DMA((n, 7))],
        input_output_aliases={n + t: t for t in range(n)},
    )(*shards, *bufs)


def _slot(p):
    return 4 * p[0] + 2 * p[1] + p[2]


def _rows(ref, span):
    return ref if span is None else ref.at[pl.ds(span[0], span[1])]


ALL = "all"
LOCAL = "local"


def _rows(ref, span):
    return ref if span == ALL else ref.at[pl.ds(span[0], span[1])]


def _rider_ag(items):
    ins, out_shape, aliases, where = [], [], {}, []
    n_remote = n_local = 0
    for t, (shard, buf, snd, fwd) in enumerate(items):
        i_shard = i_buf = None
        if snd is not None:
            i_shard = len(ins)
            ins.append(shard)
        if buf is not None:
            i_buf = len(ins)
            ins.append(buf)
            aliases[i_buf] = t
            out_shape.append(jax.ShapeDtypeStruct(buf.shape, buf.dtype))
        else:
            assert fwd is None and snd is not None
            out_shape.append(jax.ShapeDtypeStruct((N_DEV,) + shard.shape, shard.dtype))
        where.append((i_shard, i_buf, n_remote, n_local))
        n_remote += (4 if snd not in (None, LOCAL) else 0) + (3 if fwd is not None else 0)
        n_local += 1 if snd is not None else 0

    def plan(rins, routs, send, recv, loc, r0, l0):
        x, y, c = _coords()
        peers = [(x, y, 1 - c), (1 - x, y, c), (x, 1 - y, c), (1 - x, 1 - y, c)]
        remote, local = [], []
        for t, (shard, buf, snd, fwd) in enumerate(items):
            i_shard, i_buf, k, l = where[t]
            k, l = r0 + k, l0 + l
            if snd is not None:
                span = ALL if snd == LOCAL else snd
                src, dst = _rows(rins[i_shard], span), _rows(routs[t].at[_slot((x, y, c))], span)
                local.append(pltpu.make_async_copy(src, dst, loc.at[l]))
                for peer in (peers if snd != LOCAL else []):
                    remote.append(pltpu.make_async_remote_copy(
                        src_ref=src, dst_ref=dst, send_sem=send.at[k], recv_sem=recv.at[k],
                        device_id=peer, device_id_type=MESH))
                    k += 1
            if fwd is not None:
                for px, py, pc in peers[1:]:
                    s = _slot((px, py, pc))
                    remote.append(pltpu.make_async_remote_copy(
                        src_ref=_rows(rins[i_buf].at[s], fwd), dst_ref=_rows(routs[t].at[s], fwd),
                        send_sem=send.at[k], recv_sem=recv.at[k], device_id=peers[0], device_id_type=MESH))
                    k += 1
        return remote, local

    return _Rider(ins, out_shape, n_remote, n_local, plan, aliases)


def _gather_buffer(shard, me):
    return lax.dynamic_update_slice(lax.empty((N_DEV,) + shard.shape, shard.dtype), shard[None], (me, 0, 0))


def _rider_ag_remote(shards, me):
    n = len(shards)

    def plan(ins, outs, send, recv, loc, r0, l0):
        x, y, c = _coords()
        remote = []
        for t in range(n):
            dst = outs[t].at[_slot((x, y, c))]
            for k, peer in enumerate([(x, y, 1 - c), (1 - x, y, c), (x, 1 - y, c), (1 - x, 1 - y, c)]):
                remote.append(pltpu.make_async_remote_copy(
                    src_ref=ins[t], dst_ref=dst, send_sem=send.at[r0 + 4 * t + k], recv_sem=recv.at[r0 + 4 * t + k],
                    device_id=peer, device_id_type=MESH))
        return remote, []

    return _Rider(shards, [jax.ShapeDtypeStruct((N_DEV,) + s.shape, s.dtype) for s in shards], 4 * n, 0, plan,
                  lands=[_gather_buffer(s, me) for s in shards])


def _rider_rs_sibling(grads):
    n = len(grads)

    def plan(ins, outs, send, recv, loc, r0, l0):
        x, y, c = _coords()
        remote = []
        for t in range(n):
            for q in range(4):
                remote.append(pltpu.make_async_remote_copy(
                    src_ref=ins[t].at[q, 1 - c], dst_ref=outs[t].at[q], send_sem=send.at[r0 + 4 * t + q],
                    recv_sem=recv.at[r0 + 4 * t + q], device_id=(x, y, 1 - c), device_id_type=MESH))
        return remote, []

    return _Rider(grads, [jax.ShapeDtypeStruct((4,) + g.shape[2:], g.dtype) for g in grads], 4 * n, 0, plan)


def _rider_rs_chips(sums, rows=None, into=None):
    n = len(sums)
    rows = rows or [ALL] * n

    def plan(ins, outs, send, recv, loc, r0, l0):
        x, y, c = _coords()
        remote = []
        for t in range(n):
            for r, (px, py) in enumerate([(1 - x, y), (x, 1 - y), (1 - x, 1 - y)]):
                remote.append(pltpu.make_async_remote_copy(
                    src_ref=_rows(ins[t].at[2 * px + py], rows[t]), dst_ref=_rows(outs[t].at[r], rows[t]),
                    send_sem=send.at[r0 + 3 * t + r], recv_sem=recv.at[r0 + 3 * t + r],
                    device_id=(px, py, c), device_id_type=MESH))
        return remote, []

    out_shape = [jax.ShapeDtypeStruct((3,) + s.shape[1:], s.dtype) for s in sums]
    if into is None:
        return _Rider(sums, out_shape, 3 * n, 0, plan)
    return _Rider(list(sums) + list(into), out_shape, 3 * n, 0, plan, aliases={n + t: t for t in range(n)})


def _rider_gather_remote(parts):
    n = len(parts)

    def plan(ins, outs, send, recv, loc, r0, l0):
        x, y, c = _coords()
        me = _slot((x, y, c))
        remote = []
        for t in range(n):
            for k in range(1, N_DEV):
                peer = (x ^ ((k >> 2) & 1), y ^ ((k >> 1) & 1), c ^ (k & 1))
                remote.append(pltpu.make_async_remote_copy(
                    src_ref=ins[t], dst_ref=outs[t].at[me], send_sem=send.at[r0 + 7 * t + k - 1],
                    recv_sem=recv.at[r0 + 7 * t + k - 1], device_id=peer, device_id_type=MESH))
        return remote, []

    return _Rider(parts, [jax.ShapeDtypeStruct((N_DEV,) + p.shape, p.dtype) for p in parts], 7 * n, 0, plan)


def _chip_sum_call(idx, grads, recvd, out_dtypes, name):
    n = len(grads)

    def body(i_ref, *refs):
        for t in range(n):
            refs[2 * n + t][0] = (refs[t][0, 0] + refs[n + t][0]).astype(out_dtypes[t])

    def chip(k, s):
        return jnp.where(k >= s[0], k + 1, k)

    in_specs = [pl.BlockSpec((1, 1) + g.shape[2:], lambda k, s: (chip(k, s), s[1], 0, 0)) for g in grads]
    in_specs += [pl.BlockSpec((1,) + r.shape[1:], lambda k, s: (chip(k, s), 0, 0)) for r in recvd]
    return pl.pallas_call(
        body, name=name,
        grid_spec=pltpu.PrefetchScalarGridSpec(
            num_scalar_prefetch=1, grid=(3,), in_specs=in_specs,
            out_specs=[pl.BlockSpec((1,) + r.shape[1:], lambda k, s: (chip(k, s), 0, 0)) for r in recvd]),
        out_shape=[jax.ShapeDtypeStruct(r.shape, dt) for r, dt in zip(recvd, out_dtypes)],
        compiler_params=_params(("arbitrary",)),
    )(idx, *grads, *recvd)


def _final_sum_call(idx, grads, recvd1, recvd2):
    n = len(grads)
    nsteps = 2

    def body(i_ref, *refs):
        for t in range(n):
            g, r1, r2, o = refs[t], refs[n + t], refs[2 * n + t], refs[3 * n + t]
            s = g[0, 0] + r1[0]
            for r in range(3):
                s = s + r2[r].astype(F32)
            o[...] = s

    def rows(a):
        r = a.shape[-2]
        return r // nsteps if (r // nsteps) % 16 == 0 else r

    def step(a):
        return (lambda i: i) if rows(a) != a.shape[-2] else (lambda i: 0)

    in_specs = [pl.BlockSpec((1, 1, rows(g), g.shape[3]), lambda i, s, st=step(g): (s[0], s[1], st(i), 0)) for g in grads]
    in_specs += [pl.BlockSpec((1, rows(r), r.shape[2]), lambda i, s, st=step(r): (s[0], st(i), 0)) for r in recvd1]
    in_specs += [pl.BlockSpec((3, rows(r), r.shape[2]), lambda i, s, st=step(r): (0, st(i), 0)) for r in recvd2]
    return pl.pallas_call(
        body, name="rs_final_sum",
        grid_spec=pltpu.PrefetchScalarGridSpec(
            num_scalar_prefetch=1, grid=(nsteps,), in_specs=in_specs,
            out_specs=[pl.BlockSpec((rows(r), r.shape[2]), lambda i, s, st=step(r): (st(i), 0)) for r in recvd2]),
        out_shape=[jax.ShapeDtypeStruct(r.shape[1:], F32) for r in recvd2],
        compiler_params=_params(("arbitrary",)),
    )(idx, *grads, *recvd1, *recvd2)


def _sum8_call(parts):
    def body(p_ref, o_ref):
        s = p_ref[0]
        for j in range(1, N_DEV):
            s = s + p_ref[j]
        o_ref[...] = s

    return pl.pallas_call(body, name="sum_small_partials",
                          out_shape=jax.ShapeDtypeStruct(parts.shape[1:], parts.dtype))(parts)


def _adamw(w, g, m, v):
    m = ADAM_B1 * m + (1.0 - ADAM_B1) * g
    v = ADAM_B2 * v + (1.0 - ADAM_B2) * (g * g)
    m_hat = m / (1.0 - ADAM_B1 ** ADAM_STEP)
    v_hat = v / (1.0 - ADAM_B2 ** ADAM_STEP)
    delta = -ADAM_LR * (m_hat / (jnp.sqrt(v_hat) + ADAM_EPS) + ADAM_WD * w)
    return delta, m, v


def _adamw_call(ws, gs, ms, vs, nsteps, name):
    n = len(ws)

    def body(*refs):
        for t in range(n):
            w, g, m, v = (refs[k * n + t][...] for k in range(4))
            d, m2, v2 = _adamw(w, g, m, v)
            refs[4 * n + t][...] = d
            refs[5 * n + t][...] = m2
            refs[6 * n + t][...] = v2

    def spec(a):
        assert a.shape[0] % nsteps == 0 and (nsteps == 1 or (a.shape[0] // nsteps) % 8 == 0), a.shape
        return pl.BlockSpec((a.shape[0] // nsteps, a.shape[1]), lambda i: (i, 0))

    specs = [spec(a) for a in ws]
    outs = pl.pallas_call(
        body, name=name, grid=(nsteps,),
        in_specs=specs * 4, out_specs=specs * 3,
        out_shape=[jax.ShapeDtypeStruct(a.shape, F32) for a in ws] * 3,
        compiler_params=_params(("arbitrary",)),
    )(*ws, *gs, *ms, *vs)
    return outs[:n], outs[n:2 * n], outs[2 * n:]


def _adamw_rs_call(idx, after, gws, r1s, r2s, ws, ms, vs, nsteps, name):
    n = len(ws)

    def body(i_ref, after_ref, *refs):
        for t in range(n):
            gw, r1, r2, w, m, v = (refs[k * n + t] for k in range(6))
            g = gw[0, 0] + r1[0]
            for r in range(3):
                g = g + r2[r].astype(F32)
            d, m2, v2 = _adamw(w[...], g, m[...], v[...])
            refs[6 * n + t][...] = g
            refs[7 * n + t][...] = d
            refs[8 * n + t][...] = m2
            refs[9 * n + t][...] = v2

    def rb(a):
        r = a.shape[0] // nsteps
        assert a.shape[0] % nsteps == 0 and r % 16 == 0, a.shape
        return r

    in_specs = [pl.BlockSpec((1, 1, rb(w), w.shape[1]), lambda i, s: (s[0], s[1], i, 0)) for w in ws]
    in_specs += [pl.BlockSpec((1, rb(w), w.shape[1]), lambda i, s: (s[0], i, 0)) for w in ws]
    in_specs += [pl.BlockSpec((3, rb(w), w.shape[1]), lambda i, s: (0, i, 0)) for w in ws]
    plain = [pl.BlockSpec((rb(w), w.shape[1]), lambda i, s: (i, 0)) for w in ws]
    outs = pl.pallas_call(
        body, name=name,
        grid_spec=pltpu.PrefetchScalarGridSpec(
            num_scalar_prefetch=1, grid=(nsteps,),
            in_specs=[pl.BlockSpec(memory_space=pl.ANY)] + in_specs + plain * 3, out_specs=plain * 4),
        out_shape=[jax.ShapeDtypeStruct(w.shape, F32) for w in ws] * 4,
        compiler_params=_params(("arbitrary",)),
    )(idx, after, *gws, *r1s, *r2s, *ws, *ms, *vs)
    return outs[:n], outs[n:2 * n], outs[2 * n:3 * n], outs[3 * n:]


def _rows128(a, pad_rows):
    flat = a.reshape(-1).astype(F32)
    flat = jnp.pad(flat, (0, pad_rows * LANES - flat.shape[0]))
    return flat.reshape(pad_rows, LANES)


_SMALL_A = (("w_pool", 512), ("pool_scale", 8), ("attn_sinks", 8), ("g_mix_post", 8), ("g_mlp_pre", 8),
            ("g_mlp_post", 8), ("loss", 8), ("b_in_gates", 16))
_SMALL_A_ROWS = 640
_SMALL_B = (("g_mix_pre", 8), ("b_in_head", 16))


def _pack(parts, layout, total_rows):
    rows = [_rows128(parts[k], r) for k, r in layout]
    pad = total_rows - sum(r for _, r in layout)
    if pad:
        rows.append(jnp.zeros((pad, LANES), F32))
    return jnp.concatenate(rows, axis=0)


def _unpack(buf, layout, sizes):
    out, off = {}, 0
    for k, r in layout:
        out[k] = buf[off:off + r].reshape(-1)[:sizes[k]]
        off += r
    return out


def kernel(x, g_mix_pre, w_in, b_in, w_pool, pool_scale, attn_sinks, w_branch_pool, w_branch_attn, w_out, g_mix_post, g_mlp_pre, w_up, w_down, g_mlp_post, loss_target, m_g_mix_pre, m_w_in, m_b_in, m_w_pool, m_pool_scale, m_attn_sinks, m_w_branch_pool, m_w_branch_attn, m_w_out, m_g_mix_post, m_g_mlp_pre, m_w_up, m_w_down, m_g_mlp_post, v_g_mix_pre, v_w_in, v_b_in, v_w_pool, v_pool_scale, v_attn_sinks, v_w_branch_pool, v_w_branch_attn, v_w_out, v_g_mix_post, v_g_mlp_pre, v_w_up, v_w_down, v_g_mlp_post):
    B, S, _ = x.shape
    T = B * S
    xt = x.reshape(T, D_MODEL)
    tgt = loss_target.reshape(T, D_MODEL)
    cx, cy, cc = _coords()

    cidx = jnp.stack([2 * cx + cy, cc]).astype(jnp.int32)
    by_chip = lambda gr: gr.reshape((4, 2) + gr.shape[1:])
    bf = lambda w: w[0].astype(MXU_DTYPE)

    me = _slot((cx, cy, cc))
    win_l = w_in[0].T.astype(MXU_DTYPE)
    (win_s,) = _allgather_call([win_l], [_gather_buffer(win_l, me)])
    win_t = win_s.reshape(IN_WIDTH, D_MODEL)
    wpool_b = bf(w_pool)
    rc, rsa, rsb = _rot_tables(S)

    wbp_l, wba_l, wout_l, wup_l, wdown_l = bf(w_branch_pool), bf(w_branch_attn), bf(w_out), bf(w_up), bf(w_down)
    (c_br, c_up, c_dn), tok = _copies_start(
        [_rider_ag_remote([wbp_l, wba_l, wout_l], me), _rider_ag_remote([wup_l], me), _rider_ag_remote([wdown_l], me)],
        "allgather_start", after=win_s)
    (h, u, q, k4, v4, g), _ = _inproj_call(xt, g_mix_pre, win_t, b_in, rc, rsa, rsb, S, rider=_after(tok))
    yp = _pool_call(u, wpool_b, pool_scale, S)
    wbp_1, wba_1, wout_1 = _copies_wait([c_br], yp, "allgather_wait_branch")
    (ya,), (wbp_s, wba_s, wout_s) = _attn_call(
        attn_sinks, q, k4, v4, S,
        rider=_rider_ag([(None, wbp_1, None, ALL), (None, wba_1, None, ALL), (None, wout_1, None, ALL)]))
    wout_f = wout_s.reshape(D_MODEL, D_MODEL)
    (wup_1,) = _copies_wait([c_up], ya, "allgather_wait_up")
    (mix, x1, h2), (wup_s,) = _mix_fwd_call(
        yp, ya, g, xt, wbp_s, wba_s, wout_f, g_mix_post, g_mlp_pre, rider=_rider_ag([(None, wup_1, None, ALL)]))
    act = _mlp_up_call(h2, wup_s)
    (wdown_1,) = _copies_wait([c_dn], act, "allgather_wait_down")
    (wdown_s,) = _comm_call(_rider_ag([(None, wdown_1, None, ALL)]), "allgather_pass_down")

    da, dff, dx1, dg3, dg4, lossvec = _mlp_call(x1, act, tgt, wup_s, wdown_s, g_mlp_pre, g_mlp_post)
    gw_down = by_chip(_wgrad_rows_call(act, dff, "wgrad_down")[0])
    (gw_up,), (r1_down,) = _wgrad_cols_call(h2, da, "wgrad_up", rider=_rider_rs_sibling([gw_down]))
    gw_up = by_chip(gw_up)
    (s_down,) = _chip_sum_call(cidx, [gw_down], [r1_down], [MXU_DTYPE], "rs_chip_sum_down")
    (dyp, do, dgates, dg2, dbg, gw_out, gw_bp, gw_ba), (r1_up,) = _mix_bwd_call(
        dx1, mix, yp, ya, g, wbp_s, wba_s, wout_f, g_mix_post, rider=_rider_rs_sibling([gw_up]))
    gw_out = by_chip(gw_out.reshape(N_DEV, D_MODEL // N_DEV, D_MODEL))
    gw_bp, gw_ba = by_chip(gw_bp), by_chip(gw_ba)
    (s_up,) = _chip_sum_call(cidx, [gw_up], [r1_up], [MXU_DTYPE], "rs_chip_sum_up")
    (c_down, c_up), tok = _copies_start([_rider_rs_chips([s_down]), _rider_rs_chips([s_up])], "rs_chips_start_mlp")
    (dq, dk, dv, dsink), (r1_out, r1_bp, r1_ba) = _attn_bwd_call(
        attn_sinks, q, k4, v4, do, rc, rsa, rsb, S, rider=_after(tok, _rider_rs_sibling([gw_out, gw_bp, gw_ba])))
    s_obb = _chip_sum_call(cidx, [gw_out, gw_bp, gw_ba], [r1_out, r1_bp, r1_ba], [MXU_DTYPE] * 3, "rs_chip_sum_branch")
    (c_obb,), tok = _copies_start([_rider_rs_chips(s_obb)], "rs_chips_start_branch")
    (du, dwp, dps), _ = _pool_bwd_call(u, dyp, wpool_b, pool_scale, S, rider=_after(tok))
    (gw_in,) = _wgrad_in_call(du, dq, dk, dv, dgates, h)
    gw_in = by_chip(gw_in)

    small_a = {"w_pool": dwp, "pool_scale": dps,
               "attn_sinks": jnp.sum(dsink.reshape(B, 8, LANES)[:, 0, :N_Q_HEADS], axis=0), "g_mix_post": dg2,
               "g_mlp_pre": dg3, "g_mlp_post": dg4, "loss": lossvec, "b_in_gates": dbg}
    gw_sa = by_chip(_pack(small_a, _SMALL_A, _SMALL_A_ROWS).reshape(N_DEV, _SMALL_A_ROWS // N_DEV, LANES))
    r1_in, r1_sa = _comm_call(_rider_rs_sibling([gw_in, gw_sa]), "rs_sibling_in")
    s_in, s_sa = _chip_sum_call(cidx, [gw_in, gw_sa], [r1_in, r1_sa], [MXU_DTYPE, F32], "rs_chip_sum_in")
    (c_in,), tok = _copies_start([_rider_rs_chips([s_in, s_sa])], "rs_chips_start_in")
    (gx, dg1, dba_in), _ = _inproj_bwd_call(du, dq, dk, dv, dgates, dx1, xt, win_t, g_mix_pre, rider=_after(tok))
    r2_down, r2_up, r2_out, r2_bp, r2_ba, r2_in, r2_sa = _copies_wait([c_down, c_up, c_obb, c_in], dg1, "rs_chips_wait")

    (g_sa,) = _final_sum_call(cidx, [gw_sa], [r1_sa], [r2_sa])
    part_b = _pack({"g_mix_pre": dg1, "b_in_head": dba_in}, _SMALL_B, sum(r for _, r in _SMALL_B))
    (c_small,), tok = _copies_start([_rider_gather_remote([g_sa, part_b])], "allgather_small_start")

    in_t = _adamw_rs_call(cidx, tok, [gw_in], [r1_in], [r2_in], [w_in[0].T], [m_w_in[0].T], [v_w_in[0].T], 2,
                          "adamw_w_in")
    rest = _adamw_rs_call(
        cidx, tok, [gw_bp, gw_ba, gw_out, gw_up, gw_down], [r1_bp, r1_ba, r1_out, r1_up, r1_down],
        [r2_bp, r2_ba, r2_out, r2_up, r2_down], [w_branch_pool[0], w_branch_attn[0], w_out[0], w_up[0], w_down[0]],
        [m_w_branch_pool[0], m_w_branch_attn[0], m_w_out[0], m_w_up[0], m_w_down[0]],
        [v_w_branch_pool[0], v_w_branch_attn[0], v_w_out[0], v_w_up[0], v_w_down[0]], N_DEV, "adamw_shards")
    big_g, big_d, big_m2, big_v2 = ([a[0].T] + list(b) for a, b in zip(in_t, rest))

    sa_all, sb_all = _copies_wait([c_small], rest[0][0], "allgather_small_wait")
    sa_all = lax.dynamic_update_slice(sa_all, g_sa[None], (me, 0, 0))
    sb_sum = _sum8_call(lax.dynamic_update_slice(sb_all, part_b[None], (me, 0, 0)))

    names = ["g_mix_pre", "b_in", "w_pool", "pool_scale", "attn_sinks", "g_mix_post", "g_mlp_pre", "g_mlp_post"]
    sm_w = dict(g_mix_pre=g_mix_pre, b_in=b_in, w_pool=w_pool, pool_scale=pool_scale, attn_sinks=attn_sinks,
                g_mix_post=g_mix_post, g_mlp_pre=g_mlp_pre, g_mlp_post=g_mlp_post)
    sm_m = dict(g_mix_pre=m_g_mix_pre, b_in=m_b_in, w_pool=m_w_pool, pool_scale=m_pool_scale, attn_sinks=m_attn_sinks,
                g_mix_post=m_g_mix_post, g_mlp_pre=m_g_mlp_pre, g_mlp_post=m_g_mlp_post)
    sm_v = dict(g_mix_pre=v_g_mix_pre, b_in=v_b_in, w_pool=v_w_pool, pool_scale=v_pool_scale, attn_sinks=v_attn_sinks,
                g_mix_post=v_g_mix_post, g_mlp_pre=v_g_mlp_pre, g_mlp_post=v_g_mlp_post)
    sizes = {k: sm_w[k].size for k in names}
    sizes.update(loss=D_MODEL, b_in_gates=GATE_WIDTH, b_in_head=C_G)
    sm_g = _unpack(sa_all.reshape(_SMALL_A_ROWS, LANES), _SMALL_A, sizes)
    sm_g.update(_unpack(sb_sum, _SMALL_B, sizes))
    sm_g["b_in"] = jnp.concatenate([sm_g["b_in_head"], sm_g["b_in_gates"]])
    loss = (0.5 / D_MODEL) * jnp.sum(sm_g["loss"])
    two_d = lambda a: a.reshape(-1, a.shape[-1])
    sd_, sm2_, sv2_ = _adamw_call([two_d(sm_w[k]) for k in names], [two_d(sm_g[k].reshape(sm_w[k].shape)) for k in names],
                                  [two_d(sm_m[k]) for k in names], [two_d(sm_v[k]) for k in names], 1, "adamw_small")
    like = lambda vals: {k: a.reshape(sm_w[k].shape) for k, a in zip(names, vals)}
    sm_d, sm_m2, sm_v2 = like(sd_), like(sm2_), like(sv2_)
    sm_gr = {k: sm_g[k].reshape(sm_w[k].shape) for k in names}

    order = ["g_mix_pre", "w_in", "b_in", "w_pool", "pool_scale", "attn_sinks", "w_branch_pool", "w_branch_attn",
             "w_out", "g_mix_post", "g_mlp_pre", "w_up", "w_down", "g_mlp_post"]
    big_names = ["w_in", "w_branch_pool", "w_branch_attn", "w_out", "w_up", "w_down"]
    lead = lambda a: a[None]
    tables = []
    for small_t, big_t in ((sm_gr, big_g), (sm_d, big_d), (sm_m2, big_m2), (sm_v2, big_v2)):
        bt = dict(zip(big_names, big_t))
        tables.append([lead(bt[k]) if k in bt else small_t[k] for k in order])
    return (loss, gx.reshape(B, S, D_MODEL), *tables[0], *tables[1], *tables[2], *tables[3])
```

```python
import jax
import jax.numpy as jnp
from jax import lax
from jax.experimental import pallas as pl
from jax.experimental.pallas import tpu as pltpu

F32 = jnp.float32
MXU_DTYPE = jnp.bfloat16
MESH = pl.DeviceIdType.MESH

D_MODEL = 1024
POOL_WINDOWS = (2, 4, 8, 16)
POOL_WIDTH = 512
POOL_GC = 128
HEAD_DIM = 64
N_Q_HEADS = 8
N_KV_HEADS = 2
GROUP = 4
ATTN_WIDTH = 512
KV_WIDTH = 128
BLOCK = 128
GATE_WIDTH = 2048
IN_WIDTH = 3328
D_FF = 4096
EPS = 1e-6
NEG_INF = -1e30
ROPE_THETA = 500000.0
ROT_DIM = 16
SCALE = HEAD_DIM ** -0.5
C_Q, C_K, C_V, C_G = 512, 1024, 1152, 1280

ADAM_LR = 0.001
ADAM_B1 = 0.9
ADAM_B2 = 0.999
ADAM_EPS = 1e-08
ADAM_WD = 0.01
ADAM_STEP = 10

N_DEV = 8
LANES = 128
VMEM_LIMIT = 56 * 1024 * 1024

NN = (((1,), (0,)), ((), ()))
NT = (((1,), (1,)), ((), ()))
TN = (((0,), (0,)), ((), ()))


def _dot(a, b, dims):
    return lax.dot_general(a, b, dims, preferred_element_type=F32)


def _params(sem=None):
    return pltpu.CompilerParams(dimension_semantics=sem, vmem_limit_bytes=VMEM_LIMIT)


def _tile(n, pref):
    t = min(n, pref)
    assert n % t == 0, (n, t)
    return t


class _Rider:
    def __init__(self, ins, out_shape, n_remote, n_local, plan, aliases=None, lands=None):
        self.ins, self.out_shape, self.n_remote, self.n_local = list(ins), list(out_shape), n_remote, n_local
        self.plan, self.aliases = plan, dict(aliases or {})
        self.lands = lands


def _after(token, rider=None):
    r = rider or _Rider([], [], 0, 0, lambda ins, outs, send, recv, loc, r0, l0: ([], []))
    return _Rider(r.ins + [token], r.out_shape, r.n_remote, r.n_local, r.plan, r.aliases)


def _launch(body, args, *, name, grid, in_specs, out_specs, out_shape, scratch_shapes=(), sem=None, rider=None):
    if rider is None:
        return pl.pallas_call(body, name=name, grid=grid, in_specs=in_specs, out_specs=out_specs, out_shape=out_shape,
                              scratch_shapes=list(scratch_shapes), compiler_params=_params(sem))(*args)
    n_in, n_out, n_scr = len(args), len(out_shape), len(scratch_shapes)
    r_in, r_out = len(rider.ins), len(rider.out_shape)
    copies = rider.n_remote + rider.n_local > 0

    def wrapped(*refs):
        ins, rins = refs[:n_in], refs[n_in:n_in + r_in]
        o0 = n_in + r_in
        outs, routs = refs[o0:o0 + n_out], refs[o0 + n_out:o0 + n_out + r_out]
        s0 = o0 + n_out + r_out
        scr = refs[s0:s0 + n_scr]
        if not copies:
            return body(*ins, *outs, *scr)
        send, recv, loc = refs[s0 + n_scr:]
        first, last = None, None
        for d in range(len(grid)):
            f, l = pl.program_id(d) == 0, pl.program_id(d) == pl.num_programs(d) - 1
            first = f if first is None else first & f
            last = l if last is None else last & l

        def start():
            remote, local = rider.plan(rins, routs, send, recv, loc, 0, 0)
            for cp in local + remote:
                cp.start()

        def finish():
            remote, local = rider.plan(rins, routs, send, recv, loc, 0, 0)
            for cp in remote + local:
                cp.wait()

        if first is None:
            start()
            body(*ins, *outs, *scr)
            finish()
        else:
            pl.when(first)(start)
            body(*ins, *outs, *scr)
            pl.when(last)(finish)

    hbm = pl.BlockSpec(memory_space=pl.ANY)
    dma = pltpu.SemaphoreType.DMA
    res = pl.pallas_call(
        wrapped, name=name, grid=grid, in_specs=list(in_specs) + [hbm] * r_in,
        out_specs=list(out_specs) + [hbm] * r_out, out_shape=list(out_shape) + rider.out_shape,
        scratch_shapes=list(scratch_shapes) + (
            [dma((max(rider.n_remote, 1),)), dma((max(rider.n_remote, 1),)), dma((max(rider.n_local, 1),))] if copies else []),
        input_output_aliases={n_in + i: n_out + o for i, o in rider.aliases.items()},
        compiler_params=_params(sem),
    )(*args, *rider.ins)
    return list(res[:n_out]), list(res[n_out:])


def _comm_call(rider, name):
    return _launch(lambda: None, [], name=name, grid=(), in_specs=[], out_specs=[], out_shape=[], rider=rider)[1]


_HBM = pl.BlockSpec(memory_space=pltpu.HBM)
_SEM = pl.BlockSpec(memory_space=pltpu.SEMAPHORE)
_EFFECT = pltpu.SideEffectType.DATAFLOW_SIDE_EFFECTING


def _copies_start(riders, name, after=None):
    assert all(r.n_local == 0 and not r.aliases for r in riders)
    extra = [] if after is None else [after]
    sizes = [(len(r.ins), len(r.out_shape)) for r in riders]
    bufs = []
    for r in riders:
        lands = r.lands or [lax.empty(s.shape, s.dtype) for s in r.out_shape]
        bufs += [pltpu.with_memory_space_constraint(a, pltpu.HBM) for a in list(r.ins) + list(lands)]
    nb, ng, ne = len(bufs), len(riders), len(extra)

    def body(*refs):
        sems, token, at = refs[2 * nb + ne:2 * nb + ne + 2 * ng], refs[-1], 0
        for g, (r, (ni, no)) in enumerate(zip(riders, sizes)):
            remote, _ = r.plan(refs[at:at + ni], refs[at + ni:at + ni + no], sems[2 * g], sems[2 * g + 1], None, 0, 0)
            for cp in remote:
                cp.start()
            at += ni + no
        token[...] = jnp.zeros_like(token)

    res = pl.pallas_call(
        body, name=name, in_specs=[_HBM] * nb + [pl.BlockSpec(memory_space=pl.ANY)] * ne,
        out_specs=[_HBM] * nb + [_SEM] * (2 * ng) + [pl.BlockSpec(memory_space=pltpu.VMEM)],
        out_shape=[pltpu.HBM(a.shape, a.dtype) for a in bufs]
        + [pltpu.SemaphoreType.DMA((r.n_remote,)) for r in riders for _ in range(2)]
        + [jax.ShapeDtypeStruct((8, LANES), F32)],
        input_output_aliases={i: i for i in range(nb)},
        compiler_params=pltpu.CompilerParams(has_side_effects=_EFFECT),
    )(*bufs, *extra)
    handles, at = [], 0
    for g, (r, (ni, no)) in enumerate(zip(riders, sizes)):
        handles.append((r, list(res[at:at + ni + no]), res[nb + 2 * g], res[nb + 2 * g + 1]))
        at += ni + no
    return handles, res[-1]


def _copies_wait(handles, after, name):
    bufs = [b for _, bs, _, _ in handles for b in bs]
    sems = [s for _, _, send, recv in handles for s in (send, recv)]
    nb, ng = len(bufs), len(handles)

    def body(*refs):
        at = 0
        for g, (rider, bs, _, _) in enumerate(handles):
            ni = len(rider.ins)
            remote, _ = rider.plan(refs[at:at + ni], refs[at + ni:at + len(bs)], refs[nb + 2 * g], refs[nb + 2 * g + 1],
                                   None, 0, 0)
            for cp in remote:
                cp.wait_send()
                cp.wait_recv()
            at += len(bs)

    res = pl.pallas_call(
        body, name=name, in_specs=[_HBM] * nb + [_SEM] * (2 * ng) + [pl.BlockSpec(memory_space=pl.ANY)],
        out_specs=[_HBM] * nb, out_shape=[pltpu.HBM(a.shape, a.dtype) for a in bufs],
        input_output_aliases={i: i for i in range(nb)},
        compiler_params=pltpu.CompilerParams(has_side_effects=_EFFECT),
    )(*bufs, *sems, after)
    lands, at = [], 0
    for rider, bs, _, _ in handles:
        lands += list(res[at + len(rider.ins):at + len(bs)])
        at += len(bs)
    return lands


def _rms_r(x):
    return lax.rsqrt(jnp.mean(x * x, axis=-1, keepdims=True) + EPS)


def _rms_bwd(dn, x, r, g):
    xh = x * r
    dxh = dn * g
    dx = r * (dxh - xh * jnp.mean(dxh * xh, axis=-1, keepdims=True))
    return dx, dn * xh


def _rot(t, c, sa, sb):
    outs = []
    for j in range(t.shape[1] // LANES):
        tj = t[:, LANES * j:LANES * (j + 1)]
        outs.append(tj * c + pltpu.roll(tj, LANES - 8, 1) * sa + pltpu.roll(tj, 8, 1) * sb)
    return outs[0] if len(outs) == 1 else jnp.concatenate(outs, axis=1)


def _rot_tables(S):
    pos = jnp.arange(S, dtype=F32)
    inv_freq = ROPE_THETA ** (-jnp.arange(0, ROT_DIM, 2, dtype=F32) / ROT_DIM)
    ang = pos[:, None] * inv_freq[None, :]
    cos, sin = jnp.cos(ang), jnp.sin(ang)
    one = jnp.ones((S, HEAD_DIM - ROT_DIM), F32)
    zero = jnp.zeros((S, HEAD_DIM - ROT_DIM), F32)
    z8 = jnp.zeros((S, 8), F32)
    c = jnp.concatenate([cos, cos, one], axis=1)
    sa = jnp.concatenate([-sin, z8, zero], axis=1)
    sb = jnp.concatenate([z8, sin, zero], axis=1)
    rep = LANES // HEAD_DIM
    return jnp.tile(c, (1, rep)), jnp.tile(sa, (1, rep)), jnp.tile(sb, (1, rep))


def _lane_tile4(k):
    lane = lax.broadcasted_iota(jnp.int32, k.shape, 1)
    rk = pltpu.roll(k, HEAD_DIM, 1)
    x0 = jnp.where(lane < HEAD_DIM, k, rk)
    x1 = jnp.where(lane < HEAD_DIM, rk, k)
    return jnp.concatenate([x0, x0, x1, x1], axis=1)


def _fold_heads(acc):
    zs = []
    for hk in range(N_KV_HEADS):
        a = acc[:, 256 * hk:256 * hk + LANES] + acc[:, 256 * hk + LANES:256 * (hk + 1)]
        zs.append(a + pltpu.roll(a, HEAD_DIM, 1))
    lane = lax.broadcasted_iota(jnp.int32, zs[0].shape, 1)
    return jnp.where(lane < HEAD_DIM, zs[0], zs[1])


def _inproj_call(x, g1, win_t, b_in, rc, rsa, rsb, S, rider=None):
    T = x.shape[0]
    tm = _tile(S, 512)
    nst = S // tm

    def body(x_ref, g1_ref, w_ref, b_ref, c_ref, sa_ref, sb_ref,
             h_ref, u_ref, q_ref, k4_ref, v4_ref, g_ref):
        xv = x_ref[...]
        hb = ((xv * _rms_r(xv)) * g1_ref[...]).astype(MXU_DTYPE)
        h_ref[...] = hb

        def proj(lo, hi):
            return _dot(hb, w_ref[lo:hi, :], NT) + b_ref[:, lo:hi]

        c, sa, sb = c_ref[...], sa_ref[...], sb_ref[...]
        u_ref[...] = proj(0, C_Q)
        q_ref[...] = (_rot(proj(C_Q, C_K), c, sa, sb) * SCALE).astype(MXU_DTYPE)
        kv = proj(C_K, C_G)
        k4_ref[...] = _lane_tile4(_rot(kv[:, :KV_WIDTH], c, sa, sb)).astype(MXU_DTYPE)
        v4_ref[...] = _lane_tile4(kv[:, KV_WIDTH:]).astype(MXU_DTYPE)
        g_ref[...] = jax.nn.sigmoid(proj(C_G, IN_WIDTH)).astype(MXU_DTYPE)

    tok = lambda w: pl.BlockSpec((tm, w), lambda i: (i, 0))
    full = lambda a: pl.BlockSpec(a.shape, lambda i: (0,) * a.ndim)
    tab = pl.BlockSpec((tm, LANES), lambda i: (i % nst, 0))
    return _launch(
        body, [x, g1, win_t, b_in, rc, rsa, rsb], name="inproj_fwd", grid=(T // tm,),
        in_specs=[tok(D_MODEL), full(g1), full(win_t), full(b_in), tab, tab, tab],
        out_specs=[tok(D_MODEL), tok(POOL_WIDTH), tok(ATTN_WIDTH), tok(512), tok(512), tok(GATE_WIDTH)],
        out_shape=[jax.ShapeDtypeStruct((T, D_MODEL), MXU_DTYPE), jax.ShapeDtypeStruct((T, POOL_WIDTH), F32),
                   jax.ShapeDtypeStruct((T, ATTN_WIDTH), MXU_DTYPE), jax.ShapeDtypeStruct((T, 512), MXU_DTYPE),
                   jax.ShapeDtypeStruct((T, 512), MXU_DTYPE), jax.ShapeDtypeStruct((T, GATE_WIDTH), MXU_DTYPE)],
        sem=("arbitrary",), rider=rider)


def _shift_rows(a, k, rows):
    n = a.shape[0]
    if k > 0:
        return jnp.where(rows >= k, pltpu.roll(a, k, 0), 0.0)
    return jnp.where(rows < n + k, pltpu.roll(a, n + k, 0), 0.0)


def _win_sum(a, w, rows, sign):
    s, k = a, 1
    while k < w:
        s = s + _shift_rows(s, sign * k, rows)
        k *= 2
    return s


def _pool_diff(ug, w, rows):
    inv = 1.0 / jnp.minimum(rows + 1, w).astype(F32)
    return _win_sum(ug, w, rows, 1) * inv - ug, inv


def _pool_call(u, w_pool, pool_scale, S):
    T = u.shape[0]

    def body(u_ref, w_ref, ps_ref, y_ref):
        rows = lax.broadcasted_iota(jnp.int32, (S, POOL_GC), 0)
        for gi, w in enumerate(POOL_WINDOWS):
            sl = slice(POOL_GC * gi, POOL_GC * (gi + 1))
            diff, _ = _pool_diff(u_ref[:, sl], w, rows)
            mixed = _dot(diff.astype(MXU_DTYPE), w_ref[gi], NN)
            y_ref[:, sl] = (mixed * ps_ref[:, sl]).astype(MXU_DTYPE)

    seq = pl.BlockSpec((S, POOL_WIDTH), lambda b: (b, 0))
    return pl.pallas_call(
        body, name="pool_fwd", grid=(T // S,),
        in_specs=[seq, pl.BlockSpec(w_pool.shape, lambda b: (0, 0, 0)), pl.BlockSpec(pool_scale.shape, lambda b: (0, 0))],
        out_specs=seq, out_shape=jax.ShapeDtypeStruct((T, POOL_WIDTH), MXU_DTYPE),
        compiler_params=_params(("arbitrary",)),
    )(u, w_pool, pool_scale)


def _pool_bwd_call(u, dyp, w_pool, pool_scale, S, rider=None):
    T = u.shape[0]

    def body(u_ref, dy_ref, w_ref, ps_ref, du_ref, dw_ref, dps_ref):
        @pl.when(pl.program_id(0) == 0)
        def _():
            dw_ref[...] = jnp.zeros_like(dw_ref)
            dps_ref[...] = jnp.zeros_like(dps_ref)

        rows = lax.broadcasted_iota(jnp.int32, (S, POOL_GC), 0)
        for gi, w in enumerate(POOL_WINDOWS):
            sl = slice(POOL_GC * gi, POOL_GC * (gi + 1))
            diff, inv = _pool_diff(u_ref[:, sl], w, rows)
            diffb = diff.astype(MXU_DTYPE)
            wg = w_ref[gi]
            mixed = _dot(diffb, wg, NN)
            dy = dy_ref[:, sl]
            dps_ref[:, sl] += jnp.sum(dy * mixed, axis=0, keepdims=True)
            dmb = (dy * ps_ref[:, sl]).astype(MXU_DTYPE)
            dw_ref[gi] += _dot(diffb, dmb, TN)
            ddiff = _dot(dmb, wg, NT)
            du_ref[:, sl] = (_win_sum(ddiff * inv, w, rows, -1) - ddiff).astype(MXU_DTYPE)

    seq = pl.BlockSpec((S, POOL_WIDTH), lambda b: (b, 0))
    return _launch(
        body, [u, dyp, w_pool, pool_scale], name="pool_bwd", grid=(T // S,),
        in_specs=[seq, seq, pl.BlockSpec(w_pool.shape, lambda b: (0, 0, 0)), pl.BlockSpec(pool_scale.shape, lambda b: (0, 0))],
        out_specs=[seq, pl.BlockSpec(w_pool.shape, lambda b: (0, 0, 0)), pl.BlockSpec(pool_scale.shape, lambda b: (0, 0))],
        out_shape=[jax.ShapeDtypeStruct((T, POOL_WIDTH), MXU_DTYPE), jax.ShapeDtypeStruct(w_pool.shape, F32),
                   jax.ShapeDtypeStruct(pool_scale.shape, F32)],
        sem=("arbitrary",), rider=rider)


def _attn_consts():
    lane_g = lax.broadcasted_iota(jnp.int32, (BLOCK, 256), 1) >> 6
    rgrp = lax.broadcasted_iota(jnp.int32, (GROUP * BLOCK, 1), 0) >> 7
    rel = lax.broadcasted_iota(jnp.int32, (BLOCK, 256), 0) - lax.broadcasted_iota(jnp.int32, (BLOCK, 256), 1)

    def bias(off):
        ok = (rel + off >= 0) & (rel + off < BLOCK)
        return jnp.concatenate([jnp.where(ok, 0.0, NEG_INF)] * GROUP, axis=0)

    return lane_g, rgrp, bias(0), bias(BLOCK)


def _sink_rows(sink_ref, hk, rgrp):
    sv = jnp.zeros(rgrp.shape, F32)
    for g in range(GROUP):
        sv = jnp.where(rgrp == g, sink_ref[0, GROUP * hk + g], sv)
    return sv


def _stack_heads(xb, lane_g):
    return jnp.concatenate([jnp.where(lane_g == g, xb, jnp.zeros_like(xb)) for g in range(GROUP)], axis=0)


def _unstack_heads(xs, lane_g):
    out = jnp.where(lane_g == 0, xs[0:BLOCK], 0.0)
    for g in range(1, GROUP):
        out = out + jnp.where(lane_g == g, xs[BLOCK * g:BLOCK * (g + 1)], 0.0)
    return out


def _attn_probs(qs, kb, bias, sv):
    s = _dot(qs, kb, NT) + bias
    m = jnp.maximum(jnp.max(s, axis=1, keepdims=True), sv)
    e = jnp.exp(s - m)
    es = jnp.exp(sv - m)
    inv_l = 1.0 / (jnp.sum(e, axis=1, keepdims=True) + es)
    return e * inv_l, es * inv_l


def _attn_blocks(nb, blk, carry, per=1):
    carry = blk(0, 0, True, carry)
    per = per if (nb - 1) % per == 0 else 1

    def step(i, c):
        for k in range(per):
            n = 1 + per * i + k
            c = blk(pl.multiple_of(n * BLOCK, BLOCK), pl.multiple_of((n - 1) * BLOCK, BLOCK), False, c)
        return c

    return lax.fori_loop(0, (nb - 1) // per, step, carry)


def _attn_call(sinks, q, k4, v4, S, rider=None):
    T = q.shape[0]
    nb = S // BLOCK

    def body(sink_ref, q_ref, k_ref, v_ref, o_ref):
        lane_g, rgrp, bias_first, bias_later = _attn_consts()
        svs = [_sink_rows(sink_ref, hk, rgrp) for hk in range(N_KV_HEADS)]

        def blk(q0, k0, first, carry):
            for hk in range(N_KV_HEADS):
                cs = slice(256 * hk, 256 * (hk + 1))
                qs = _stack_heads(q_ref[pl.ds(q0, BLOCK), cs], lane_g)
                p, _ = _attn_probs(qs, k_ref[pl.ds(k0, 2 * BLOCK), cs], bias_first if first else bias_later, svs[hk])
                o = _dot(p.astype(MXU_DTYPE), v_ref[pl.ds(k0, 2 * BLOCK), cs], NN)
                o_ref[pl.ds(q0, BLOCK), cs] = _unstack_heads(o, lane_g).astype(MXU_DTYPE)
            return carry

        _attn_blocks(nb, blk, 0, per=3)

    seq = pl.BlockSpec((S, ATTN_WIDTH), lambda b: (b, 0))
    return _launch(
        body, [sinks, q, k4, v4], name="attn_fwd", grid=(T // S,),
        in_specs=[pl.BlockSpec(memory_space=pltpu.SMEM), seq, seq, seq],
        out_specs=[seq], out_shape=[jax.ShapeDtypeStruct((T, ATTN_WIDTH), MXU_DTYPE)],
        sem=("arbitrary",), rider=rider)


def _attn_bwd_call(sinks, q, k4, v4, do, rc, rsa, rsb, S, rider=None):
    T = q.shape[0]
    nb = S // BLOCK

    def body(sink_ref, q_ref, k_ref, v_ref, do_ref, c_ref, sa_ref, sb_ref,
             dq_ref, dk_ref, dv_ref, ds_ref, dk_acc, dv_acc):
        lane_g, rgrp, bias_first, bias_later = _attn_consts()
        svs = [_sink_rows(sink_ref, hk, rgrp) for hk in range(N_KV_HEADS)]
        lane1 = lax.broadcasted_iota(jnp.int32, (1, LANES), 1)
        dk_acc[...] = jnp.zeros_like(dk_acc)
        dv_acc[...] = jnp.zeros_like(dv_acc)

        def blk(q0, k0, first, dsink):
            rows = pl.ds(q0, BLOCK)
            c, sa, sb = c_ref[rows, :], sa_ref[rows, :], sb_ref[rows, :]
            for hk in range(N_KV_HEADS):
                cs = slice(256 * hk, 256 * (hk + 1))
                qs = _stack_heads(q_ref[rows, cs], lane_g)
                dos = _stack_heads(do_ref[rows, cs], lane_g)
                kb = k_ref[pl.ds(k0, 2 * BLOCK), cs]
                vb = v_ref[pl.ds(k0, 2 * BLOCK), cs]
                p, ps = _attn_probs(qs, kb, bias_first if first else bias_later, svs[hk])
                dp = _dot(dos, vb, NT)
                delta = jnp.sum(p * dp, axis=1, keepdims=True)
                dsb = (p * (dp - delta)).astype(MXU_DTYPE)
                dqb = _unstack_heads(_dot(dsb, kb, NN), lane_g) * SCALE
                dq_ref[rows, cs] = _rot(dqb, c, -sa, -sb).astype(MXU_DTYPE)
                dk_acc[pl.ds(k0, 2 * BLOCK), cs] += _dot(dsb, qs, TN)
                dv_acc[pl.ds(k0, 2 * BLOCK), cs] += _dot(p.astype(MXU_DTYPE), dos, TN)
                psd = ps * delta
                for g in range(GROUP):
                    val = -jnp.sum(psd[BLOCK * g:BLOCK * (g + 1)], axis=0, keepdims=True)
                    dsink = dsink + jnp.where(lane1 == GROUP * hk + g, val, 0.0)
            return dsink

        dsink = _attn_blocks(nb, blk, jnp.zeros((1, LANES), F32))
        dk_ref[...] = _rot(_fold_heads(dk_acc[...]), c_ref[...], -sa_ref[...], -sb_ref[...]).astype(MXU_DTYPE)
        dv_ref[...] = _fold_heads(dv_acc[...]).astype(MXU_DTYPE)
        ds_ref[...] = jnp.broadcast_to(dsink, ds_ref.shape)

    seq = pl.BlockSpec((S, ATTN_WIDTH), lambda b: (b, 0))
    kvs = pl.BlockSpec((S, KV_WIDTH), lambda b: (b, 0))
    tab = pl.BlockSpec((S, LANES), lambda b: (0, 0))
    nseq = T // S
    return _launch(
        body, [sinks, q, k4, v4, do, rc, rsa, rsb], name="attn_bwd", grid=(nseq,),
        in_specs=[pl.BlockSpec(memory_space=pltpu.SMEM), seq, seq, seq, seq, tab, tab, tab],
        out_specs=[seq, kvs, kvs, pl.BlockSpec((8, LANES), lambda b: (b, 0))],
        out_shape=[jax.ShapeDtypeStruct((T, ATTN_WIDTH), MXU_DTYPE), jax.ShapeDtypeStruct((T, KV_WIDTH), MXU_DTYPE),
                   jax.ShapeDtypeStruct((T, KV_WIDTH), MXU_DTYPE), jax.ShapeDtypeStruct((8 * nseq, LANES), F32)],
        scratch_shapes=[pltpu.VMEM((S, 512), F32), pltpu.VMEM((S, 512), F32)],
        sem=("arbitrary",), rider=rider)


def _branch_weights(wbp_ref, wba_ref, wbp_s, wba_s):
    @pl.when(pl.program_id(0) == 0)
    def _():
        for j in range(N_DEV):
            wbp_s[:, LANES * j:LANES * (j + 1)] = wbp_ref[j]
            wba_s[:, LANES * j:LANES * (j + 1)] = wba_ref[j]


def _mix_fwd_call(yp, ya, g, x, wbp, wba, wout, g2, g3, rider=None):
    T = x.shape[0]
    tm = _tile(T, 512)

    def body(yp_ref, ya_ref, g_ref, x_ref, wbp_ref, wba_ref, wout_ref, g2_ref, g3_ref,
             mix_ref, x1_ref, h2_ref, wbp_s, wba_s):
        _branch_weights(wbp_ref, wba_ref, wbp_s, wba_s)
        bp = _dot(yp_ref[...], wbp_s[...], NN)
        ba = _dot(ya_ref[...], wba_s[...], NN)
        merged = g_ref[:, :D_MODEL].astype(F32) * bp + g_ref[:, D_MODEL:].astype(F32) * ba
        mix = _dot(merged.astype(MXU_DTYPE), wout_ref[...], NN)
        mix_ref[...] = mix
        x1 = x_ref[...] + (mix * _rms_r(mix)) * g2_ref[...]
        x1_ref[...] = x1
        h2_ref[...] = ((x1 * _rms_r(x1)) * g3_ref[...]).astype(MXU_DTYPE)

    tok = lambda w: pl.BlockSpec((tm, w), lambda i: (i, 0))
    full = lambda a: pl.BlockSpec(a.shape, lambda i: (0,) * a.ndim)
    return _launch(
        body, [yp, ya, g, x, wbp, wba, wout, g2, g3], name="mix_fwd", grid=(T // tm,),
        in_specs=[tok(POOL_WIDTH), tok(ATTN_WIDTH), tok(GATE_WIDTH), tok(D_MODEL), full(wbp), full(wba), full(wout),
                  full(g2), full(g3)],
        out_specs=[tok(D_MODEL), tok(D_MODEL), tok(D_MODEL)],
        out_shape=[jax.ShapeDtypeStruct((T, D_MODEL), F32), jax.ShapeDtypeStruct((T, D_MODEL), F32),
                   jax.ShapeDtypeStruct((T, D_MODEL), MXU_DTYPE)],
        scratch_shapes=[pltpu.VMEM((POOL_WIDTH, D_MODEL), MXU_DTYPE), pltpu.VMEM((ATTN_WIDTH, D_MODEL), MXU_DTYPE)],
        sem=("arbitrary",), rider=rider)


def _mix_bwd_call(dx1, mix, yp, ya, g, wbp, wba, wout, g2, rider=None):
    T = dx1.shape[0]
    tm = _tile(T, 512)

    def body(dx1_ref, mix_ref, yp_ref, ya_ref, g_ref, wbp_ref, wba_ref, wout_ref, g2_ref,
             dyp_ref, do_ref, dgates_ref, dg2_ref, dbg_ref, gout_ref, gbp_ref, gba_ref,
             wbp_s, wba_s, acc_out, acc_bp, acc_ba, sem):
        _branch_weights(wbp_ref, wba_ref, wbp_s, wba_s)
        step = pl.program_id(0)

        @pl.when(step == 0)
        def _():
            dg2_ref[...] = jnp.zeros_like(dg2_ref)
            dbg_ref[...] = jnp.zeros_like(dbg_ref)
            acc_out[...] = jnp.zeros_like(acc_out)
            acc_bp[...] = jnp.zeros_like(acc_bp)
            acc_ba[...] = jnp.zeros_like(acc_ba)

        mix = mix_ref[...]
        dmix, dg2 = _rms_bwd(dx1_ref[...], mix, _rms_r(mix), g2_ref[...])
        dg2_ref[...] += jnp.sum(dg2, axis=0, keepdims=True)
        dmixb = dmix.astype(MXU_DTYPE)
        dmerged = _dot(dmixb, wout_ref[...], NT)
        yp, ya = yp_ref[...], ya_ref[...]
        bp = _dot(yp, wbp_s[...], NN)
        ba = _dot(ya, wba_s[...], NN)
        gp, ga = g_ref[:, :D_MODEL].astype(F32), g_ref[:, D_MODEL:].astype(F32)
        acc_out[...] += _dot((gp * bp + ga * ba).astype(MXU_DTYPE), dmixb, TN)
        dgp = dmerged * bp * (gp * (1.0 - gp))
        dga = dmerged * ba * (ga * (1.0 - ga))
        dbg_ref[:, :D_MODEL] += jnp.sum(dgp, axis=0, keepdims=True)
        dbg_ref[:, D_MODEL:] += jnp.sum(dga, axis=0, keepdims=True)
        dgates_ref[:, :D_MODEL] = dgp.astype(MXU_DTYPE)
        dgates_ref[:, D_MODEL:] = dga.astype(MXU_DTYPE)
        dbp = (dmerged * gp).astype(MXU_DTYPE)
        dba = (dmerged * ga).astype(MXU_DTYPE)
        acc_bp[...] += _dot(yp, dbp, TN)
        acc_ba[...] += _dot(ya, dba, TN)
        dyp_ref[...] = _dot(dbp, wbp_s[...], NT)
        do_ref[...] = _dot(dba, wba_s[...], NT).astype(MXU_DTYPE)

        @pl.when(step == pl.num_programs(0) - 1)
        def _():
            copies = [pltpu.make_async_copy(acc_out, gout_ref, sem.at[0])]
            for j in range(N_DEV):
                cols = slice(LANES * j, LANES * (j + 1))
                copies.append(pltpu.make_async_copy(acc_bp.at[:, cols], gbp_ref.at[j], sem.at[1 + j]))
                copies.append(pltpu.make_async_copy(acc_ba.at[:, cols], gba_ref.at[j], sem.at[1 + N_DEV + j]))
            for cp in copies:
                cp.start()
            for cp in copies:
                cp.wait()

    tok = lambda w: pl.BlockSpec((tm, w), lambda i: (i, 0))
    full = lambda a: pl.BlockSpec(a.shape, lambda i: (0,) * a.ndim)
    acc = lambda w: pl.BlockSpec((1, w), lambda i: (0, 0))
    hbm = pl.BlockSpec(memory_space=pl.ANY)
    sd = jax.ShapeDtypeStruct
    return _launch(
        body, [dx1, mix, yp, ya, g, wbp, wba, wout, g2], name="mix_bwd", grid=(T // tm,),
        in_specs=[tok(D_MODEL), tok(D_MODEL), tok(POOL_WIDTH), tok(ATTN_WIDTH), tok(GATE_WIDTH), full(wbp), full(wba),
                  full(wout), full(g2)],
        out_specs=[tok(POOL_WIDTH), tok(ATTN_WIDTH), tok(GATE_WIDTH), acc(D_MODEL), acc(GATE_WIDTH), hbm, hbm, hbm],
        out_shape=[sd((T, POOL_WIDTH), F32), sd((T, ATTN_WIDTH), MXU_DTYPE), sd((T, GATE_WIDTH), MXU_DTYPE),
                   sd((1, D_MODEL), F32), sd((1, GATE_WIDTH), F32), sd((D_MODEL, D_MODEL), F32),
                   sd((N_DEV, POOL_WIDTH, LANES), F32), sd((N_DEV, ATTN_WIDTH, LANES), F32)],
        scratch_shapes=[pltpu.VMEM((POOL_WIDTH, D_MODEL), MXU_DTYPE), pltpu.VMEM((ATTN_WIDTH, D_MODEL), MXU_DTYPE),
                        pltpu.VMEM((D_MODEL, D_MODEL), F32), pltpu.VMEM((POOL_WIDTH, D_MODEL), F32),
                        pltpu.VMEM((ATTN_WIDTH, D_MODEL), F32), pltpu.SemaphoreType.DMA((1 + 2 * N_DEV,))],
        sem=("arbitrary",), rider=rider)


def _mlp_up_call(h2, wup):
    T = h2.shape[0]
    tm = _tile(T, 512)
    fc = D_FF // N_DEV

    def body(h2_ref, wup_ref, act_ref):
        h2 = h2_ref[...]
        for j in range(N_DEV):
            rl = jnp.maximum(_dot(h2, wup_ref[j], NN), 0.0)
            act_ref[:, fc * j:fc * (j + 1)] = (rl * rl).astype(MXU_DTYPE)

    sd = jax.ShapeDtypeStruct
    return pl.pallas_call(
        body, name="mlp_up", grid=(T // tm,),
        in_specs=[pl.BlockSpec((tm, D_MODEL), lambda i: (i, 0)),
                  pl.BlockSpec(wup.shape, lambda i: (0, 0, 0), pipeline_mode=pl.Buffered(1))],
        out_specs=pl.BlockSpec((tm, D_FF), lambda i: (i, 0)), out_shape=sd((T, D_FF), MXU_DTYPE),
        compiler_params=_params(("arbitrary",)),
    )(h2, wup)


def _mlp_call(x1, act, target, wup, wdown, g3, g4):
    T = x1.shape[0]
    tm = _tile(T, 256)
    fc = D_FF // N_DEV

    def body(x1_ref, act_ref, t_ref, wup_ref, wdown_ref, g3_ref, g4_ref,
             da_ref, dff_ref, dx1_ref, dg3_ref, dg4_ref, loss_ref):
        @pl.when(pl.program_id(0) == 0)
        def _():
            dg3_ref[...] = jnp.zeros_like(dg3_ref)
            dg4_ref[...] = jnp.zeros_like(dg4_ref)
            loss_ref[...] = jnp.zeros_like(loss_ref)

        ff = jnp.zeros((tm, D_MODEL), F32)
        for j in range(N_DEV):
            ff = ff + _dot(act_ref[:, fc * j:fc * (j + 1)], wdown_ref[j], NN)
        x1 = x1_ref[...]
        r4 = _rms_r(ff)
        err = x1 + (ff * r4) * g4_ref[...] - t_ref[...]
        loss_ref[...] += jnp.sum(err * err, axis=0, keepdims=True)
        dy = err * (1.0 / D_MODEL)
        dff, dg4 = _rms_bwd(dy, ff, r4, g4_ref[...])
        dg4_ref[...] += jnp.sum(dg4, axis=0, keepdims=True)
        dffb = dff.astype(MXU_DTYPE)
        dff_ref[...] = dffb
        dh2 = jnp.zeros((tm, D_MODEL), F32)
        for j in range(N_DEV):
            sl = slice(fc * j, fc * (j + 1))
            rl = jnp.sqrt(act_ref[:, sl].astype(F32))
            dab = (_dot(dffb, wdown_ref[j], NT) * (2.0 * rl)).astype(MXU_DTYPE)
            da_ref[:, sl] = dab
            dh2 = dh2 + _dot(dab, wup_ref[j], NT)
        dx1, dg3 = _rms_bwd(dh2, x1, _rms_r(x1), g3_ref[...])
        dg3_ref[...] += jnp.sum(dg3, axis=0, keepdims=True)
        dx1_ref[...] = dy + dx1

    tok = lambda w: pl.BlockSpec((tm, w), lambda i: (i, 0))
    full = lambda a: pl.BlockSpec(a.shape, lambda i: (0,) * a.ndim, pipeline_mode=pl.Buffered(1))
    vec = pl.BlockSpec((1, D_MODEL), lambda i: (0, 0))
    sd = jax.ShapeDtypeStruct
    return pl.pallas_call(
        body, name="mlp_down_bwd", grid=(T // tm,),
        in_specs=[tok(D_MODEL), tok(D_FF), tok(D_MODEL), full(wup), full(wdown), vec, vec],
        out_specs=[tok(D_FF), tok(D_MODEL), tok(D_MODEL), vec, vec, vec],
        out_shape=[sd((T, D_FF), MXU_DTYPE), sd((T, D_MODEL), MXU_DTYPE),
                   sd((T, D_MODEL), F32), sd((1, D_MODEL), F32), sd((1, D_MODEL), F32), sd((1, D_MODEL), F32)],
        compiler_params=_params(("arbitrary",)),
    )(x1, act, target, wup, wdown, g3, g4)


def _inproj_bwd_call(du, dq, dk, dv, dgates, dx1, x, win_t, g1, rider=None):
    T = x.shape[0]
    tm = _tile(T, 512)

    def body(du_ref, dq_ref, dk_ref, dv_ref, dgt_ref, dx1_ref, x_ref, w_ref, g1_ref, gx_ref, dg1_ref, db_ref):
        @pl.when(pl.program_id(0) == 0)
        def _():
            dg1_ref[...] = jnp.zeros_like(dg1_ref)
            db_ref[...] = jnp.zeros_like(db_ref)

        dh = jnp.zeros((tm, D_MODEL), F32)
        for ref, lo, hi in ((du_ref, 0, C_Q), (dq_ref, C_Q, C_K), (dk_ref, C_K, C_V), (dv_ref, C_V, C_G),
                            (dgt_ref, C_G, IN_WIDTH)):
            piece = ref[...]
            dh = dh + _dot(piece, w_ref[lo:hi, :], NN)
            if hi <= C_G:
                db_ref[:, lo:hi] += jnp.sum(piece.astype(F32), axis=0, keepdims=True)
        xv = x_ref[...]
        dx, dg1 = _rms_bwd(dh, xv, _rms_r(xv), g1_ref[...])
        dg1_ref[...] += jnp.sum(dg1, axis=0, keepdims=True)
        gx_ref[...] = dx1_ref[...] + dx

    tok = lambda w: pl.BlockSpec((tm, w), lambda i: (i, 0))
    full = lambda a: pl.BlockSpec(a.shape, lambda i: (0,) * a.ndim)
    sd = jax.ShapeDtypeStruct
    return _launch(
        body, [du, dq, dk, dv, dgates, dx1, x, win_t, g1], name="inproj_bwd", grid=(T // tm,),
        in_specs=[tok(POOL_WIDTH), tok(ATTN_WIDTH), tok(KV_WIDTH), tok(KV_WIDTH), tok(GATE_WIDTH), tok(D_MODEL),
                  tok(D_MODEL), full(win_t), full(g1)],
        out_specs=[tok(D_MODEL), pl.BlockSpec((1, D_MODEL), lambda i: (0, 0)), pl.BlockSpec((1, C_G), lambda i: (0, 0))],
        out_shape=[sd((T, D_MODEL), F32), sd((1, D_MODEL), F32), sd((1, C_G), F32)],
        sem=("arbitrary",), rider=rider)


WGRAD_TOKENS = 1024


def _wgrad_rows_call(a, b, name, rider=None):
    T, K = a.shape
    N = b.shape[1]
    tm = _tile(T, WGRAD_TOKENS)
    kb = min(K, 1024)
    per = kb // (K // N_DEV)

    def body(a_ref, b_ref, o_ref):
        @pl.when(pl.program_id(1) == 0)
        def _():
            o_ref[...] = jnp.zeros_like(o_ref)

        d = _dot(a_ref[...], b_ref[...], TN)
        rs = kb // per
        for j in range(per):
            o_ref[j] += d[rs * j:rs * (j + 1)]

    return _launch(
        body, [a, b], name=name, grid=(K // kb, T // tm),
        in_specs=[pl.BlockSpec((tm, kb), lambda i, t: (t, i)), pl.BlockSpec((tm, N), lambda i, t: (t, 0))],
        out_specs=[pl.BlockSpec((per, K // N_DEV, N), lambda i, t: (i, 0, 0))],
        out_shape=[jax.ShapeDtypeStruct((N_DEV, K // N_DEV, N), F32)],
        sem=("arbitrary", "arbitrary"), rider=rider)


def _wgrad_cols_call(a, b, name, rider=None):
    T, K = a.shape
    N = b.shape[1]
    tm = _tile(T, WGRAD_TOKENS)
    nb = min(N, 1024)
    per = nb // (N // N_DEV)

    def body(a_ref, b_ref, o_ref):
        @pl.when(pl.program_id(1) == 0)
        def _():
            o_ref[...] = jnp.zeros_like(o_ref)

        d = _dot(a_ref[...], b_ref[...], TN)
        cs = nb // per
        for j in range(per):
            o_ref[j] += d[:, cs * j:cs * (j + 1)]

    return _launch(
        body, [a, b], name=name, grid=(N // nb, T // tm),
        in_specs=[pl.BlockSpec((tm, K), lambda i, t: (t, 0)), pl.BlockSpec((tm, nb), lambda i, t: (t, i))],
        out_specs=[pl.BlockSpec((per, K, N // N_DEV), lambda i, t: (i, 0, 0))],
        out_shape=[jax.ShapeDtypeStruct((N_DEV, K, N // N_DEV), F32)],
        sem=("arbitrary", "arbitrary"), rider=rider)


def _wgrad_in_call(du, dq, dk, dv, dgates, h, rider=None):
    T = h.shape[0]
    tm = _tile(T, WGRAD_TOKENS)
    rows = IN_WIDTH // N_DEV

    def body(du_ref, dq_ref, dk_ref, dv_ref, dgt_ref, h_ref, o_ref, acc, sem):
        t = pl.program_id(0)

        @pl.when(t == 0)
        def _():
            acc[...] = jnp.zeros_like(acc)

        hv = h_ref[...]
        for ref, lo, hi in ((du_ref, 0, C_Q), (dq_ref, C_Q, C_K), (dk_ref, C_K, C_V), (dv_ref, C_V, C_G),
                            (dgt_ref, C_G, IN_WIDTH)):
            acc[lo:hi, :] += _dot(ref[...], hv, TN)

        @pl.when(t == pl.num_programs(0) - 1)
        def _():
            copies = [pltpu.make_async_copy(acc.at[pl.ds(rows * j, rows), :], o_ref.at[j], sem.at[j])
                      for j in range(N_DEV)]
            for cp in copies:
                cp.start()
            for cp in copies:
                cp.wait()

    tok = lambda w: pl.BlockSpec((tm, w), lambda t: (t, 0))
    return _launch(
        body, [du, dq, dk, dv, dgates, h], name="wgrad_in", grid=(T // tm,),
        in_specs=[tok(POOL_WIDTH), tok(ATTN_WIDTH), tok(KV_WIDTH), tok(KV_WIDTH), tok(GATE_WIDTH), tok(D_MODEL)],
        out_specs=[pl.BlockSpec(memory_space=pl.ANY)],
        out_shape=[jax.ShapeDtypeStruct((N_DEV, rows, D_MODEL), F32)],
        scratch_shapes=[pltpu.VMEM((IN_WIDTH, D_MODEL), F32), pltpu.SemaphoreType.DMA((N_DEV,))],
        sem=("arbitrary",), rider=rider)


def _coords():
    return lax.axis_index("x"), lax.axis_index("y"), lax.axis_index("c")


AG_PARTS = 2


def _allgather_call(shards, bufs):
    n = len(shards)

    def body(*refs):
        ins, outs = refs[:n], refs[2 * n:3 * n]
        send_sems, recv_sems = refs[3 * n:]
        x, y, c = _coords()
        me, sibling = (x, y, c), (x, y, 1 - c)
        fx, fy = 1 - c, c
        near1 = (x ^ fx, y ^ fy, c)
        near2 = (x ^ fy, y ^ fx, c)
        diag = (1 - x, 1 - y, c)
        arrivals = [near1, near2, diag]

        def slot(p):
            return 4 * p[0] + 2 * p[1] + p[2]

        def copy(t, k, h, block, to, src=None):
            rows = pl.ds(h * (ins[t].shape[0] // AG_PARTS), ins[t].shape[0] // AG_PARTS)
            dst = outs[t].at[slot(block), rows]
            return pltpu.make_async_remote_copy(
                src_ref=dst if src is None else src.at[rows], dst_ref=dst, send_sem=send_sems.at[t, k, h],
                recv_sem=recv_sems.at[t, k, h], device_id=to, device_id_type=MESH)

        parts = range(AG_PARTS)
        sent = []
        for t in range(n):
            sent += [copy(t, 0, h, me, sibling, src=ins[t]) for h in parts]
            for h in parts:
                sent += [copy(t, 1, h, me, near1, src=ins[t]), copy(t, 2, h, me, near2, src=ins[t])]
        for cp in sent:
            cp.start()
        for t in range(n):
            for j, block in enumerate(arrivals):
                for h in parts:
                    copy(t, 1 + j, h, block, me).wait_recv()
                    onward = [copy(t, 4 + j, h, block, sibling)] + ([copy(t, 3, h, block, near2)] if j == 0 else [])
                    for cp in onward[::-1]:
                        cp.start()
                    sent += onward
        for t in range(n):
            for h in parts:
                copy(t, 0, h, sibling, me).wait_recv()
                for j, block in enumerate([near2, near1, diag]):
                    copy(t, 4 + j, h, (block[0], block[1], 1 - c), me).wait_recv()
        for cp in sent:
            cp.wait_send()

    hbm = pl.BlockSpec(memory_space=pl.ANY)
    return pl.pallas_call(
        body, name="allgather_weights",
        in_specs=[hbm] * (2 * n), out_specs=[hbm] * n,
        out_shape=[jax.ShapeDtypeStruct(b.shape, b.dtype) for b in bufs],
        scratch_shapes=[pltpu.SemaphoreType.DMA((n, 7, AG_PARTS)), pltpu.SemaphoreType.DMA((n, 7, AG_PARTS))],
        input_output_aliases={n + t: t for t in range(n)},
    )(*shards, *bufs)


def _slot(p):
    return 4 * p[0] + 2 * p[1] + p[2]


def _rows(ref, span):
    return ref if span is None else ref.at[pl.ds(span[0], span[1])]


ALL = "all"
LOCAL = "local"


def _rows(ref, span):
    return ref if span == ALL else ref.at[pl.ds(span[0], span[1])]


def _rider_ag(items):
    ins, out_shape, aliases, where = [], [], {}, []
    n_remote = n_local = 0
    for t, (shard, buf, snd, fwd) in enumerate(items):
        i_shard = i_buf = None
        if snd is not None:
            i_shard = len(ins)
            ins.append(shard)
        if buf is not None:
            i_buf = len(ins)
            ins.append(buf)
            aliases[i_buf] = t
            out_shape.append(jax.ShapeDtypeStruct(buf.shape, buf.dtype))
        else:
            assert fwd is None and snd is not None
            out_shape.append(jax.ShapeDtypeStruct((N_DEV,) + shard.shape, shard.dtype))
        where.append((i_shard, i_buf, n_remote, n_local))
        n_remote += (4 if snd not in (None, LOCAL) else 0) + (3 if fwd is not None else 0)
        n_local += 1 if snd is not None else 0

    def plan(rins, routs, send, recv, loc, r0, l0):
        x, y, c = _coords()
        peers = [(x, y, 1 - c), (1 - x, y, c), (x, 1 - y, c), (1 - x, 1 - y, c)]
        remote, local = [], []
        for t, (shard, buf, snd, fwd) in enumerate(items):
            i_shard, i_buf, k, l = where[t]
            k, l = r0 + k, l0 + l
            if snd is not None:
                span = ALL if snd == LOCAL else snd
                src, dst = _rows(rins[i_shard], span), _rows(routs[t].at[_slot((x, y, c))], span)
                local.append(pltpu.make_async_copy(src, dst, loc.at[l]))
                for peer in (peers if snd != LOCAL else []):
                    remote.append(pltpu.make_async_remote_copy(
                        src_ref=src, dst_ref=dst, send_sem=send.at[k], recv_sem=recv.at[k],
                        device_id=peer, device_id_type=MESH))
                    k += 1
            if fwd is not None:
                for px, py, pc in peers[1:]:
                    s = _slot((px, py, pc))
                    remote.append(pltpu.make_async_remote_copy(
                        src_ref=_rows(rins[i_buf].at[s], fwd), dst_ref=_rows(routs[t].at[s], fwd),
                        send_sem=send.at[k], recv_sem=recv.at[k], device_id=peers[0], device_id_type=MESH))
                    k += 1
        return remote, local

    return _Rider(ins, out_shape, n_remote, n_local, plan, aliases)


def _gather_buffer(shard, me):
    return lax.dynamic_update_slice(lax.empty((N_DEV,) + shard.shape, shard.dtype), shard[None], (me, 0, 0))


def _rider_ag_remote(shards, me):
    n = len(shards)

    def plan(ins, outs, send, recv, loc, r0, l0):
        x, y, c = _coords()
        remote = []
        for t in range(n):
            dst = outs[t].at[_slot((x, y, c))]
            for k, peer in enumerate([(x, y, 1 - c), (1 - x, y, c), (x, 1 - y, c), (1 - x, 1 - y, c)]):
                remote.append(pltpu.make_async_remote_copy(
                    src_ref=ins[t], dst_ref=dst, send_sem=send.at[r0 + 4 * t + k], recv_sem=recv.at[r0 + 4 * t + k],
                    device_id=peer, device_id_type=MESH))
        return remote, []

    return _Rider(shards, [jax.ShapeDtypeStruct((N_DEV,) + s.shape, s.dtype) for s in shards], 4 * n, 0, plan,
                  lands=[_gather_buffer(s, me) for s in shards])


def _rider_rs_sibling(grads):
    n = len(grads)

    def plan(ins, outs, send, recv, loc, r0, l0):
        x, y, c = _coords()
        remote = []
        for t in range(n):
            for q in range(4):
                remote.append(pltpu.make_async_remote_copy(
                    src_ref=ins[t].at[q, 1 - c], dst_ref=outs[t].at[q], send_sem=send.at[r0 + 4 * t + q],
                    recv_sem=recv.at[r0 + 4 * t + q], device_id=(x, y, 1 - c), device_id_type=MESH))
        return remote, []

    return _Rider(grads, [jax.ShapeDtypeStruct((4,) + g.shape[2:], g.dtype) for g in grads], 4 * n, 0, plan)


def _rider_rs_chips(sums, rows=None, into=None):
    n = len(sums)
    rows = rows or [ALL] * n

    def plan(ins, outs, send, recv, loc, r0, l0):
        x, y, c = _coords()
        remote = []
        for t in range(n):
            for r, (px, py) in enumerate([(1 - x, y), (x, 1 - y), (1 - x, 1 - y)]):
                remote.append(pltpu.make_async_remote_copy(
                    src_ref=_rows(ins[t].at[2 * px + py], rows[t]), dst_ref=_rows(outs[t].at[r], rows[t]),
                    send_sem=send.at[r0 + 3 * t + r], recv_sem=recv.at[r0 + 3 * t + r],
                    device_id=(px, py, c), device_id_type=MESH))
        return remote, []

    out_shape = [jax.ShapeDtypeStruct((3,) + s.shape[1:], s.dtype) for s in sums]
    if into is None:
        return _Rider(sums, out_shape, 3 * n, 0, plan)
    return _Rider(list(sums) + list(into), out_shape, 3 * n, 0, plan, aliases={n + t: t for t in range(n)})


def _rider_gather_remote(parts):
    n = len(parts)

    def plan(ins, outs, send, recv, loc, r0, l0):
        x, y, c = _coords()
        me = _slot((x, y, c))
        remote = []
        for t in range(n):
            for k in range(1, N_DEV):
                peer = (x ^ ((k >> 2) & 1), y ^ ((k >> 1) & 1), c ^ (k & 1))
                remote.append(pltpu.make_async_remote_copy(
                    src_ref=ins[t], dst_ref=outs[t].at[me], send_sem=send.at[r0 + 7 * t + k - 1],
                    recv_sem=recv.at[r0 + 7 * t + k - 1], device_id=peer, device_id_type=MESH))
        return remote, []

    return _Rider(parts, [jax.ShapeDtypeStruct((N_DEV,) + p.shape, p.dtype) for p in parts], 7 * n, 0, plan)


def _chip_sum_call(idx, grads, recvd, out_dtypes, name):
    n = len(grads)

    def body(i_ref, *refs):
        for t in range(n):
            refs[2 * n + t][0] = (refs[t][0, 0] + refs[n + t][0]).astype(out_dtypes[t])

    def chip(k, s):
        return jnp.where(k >= s[0], k + 1, k)

    in_specs = [pl.BlockSpec((1, 1) + g.shape[2:], lambda k, s: (chip(k, s), s[1], 0, 0)) for g in grads]
    in_specs += [pl.BlockSpec((1,) + r.shape[1:], lambda k, s: (chip(k, s), 0, 0)) for r in recvd]
    return pl.pallas_call(
        body, name=name,
        grid_spec=pltpu.PrefetchScalarGridSpec(
            num_scalar_prefetch=1, grid=(3,), in_specs=in_specs,
            out_specs=[pl.BlockSpec((1,) + r.shape[1:], lambda k, s: (chip(k, s), 0, 0)) for r in recvd]),
        out_shape=[jax.ShapeDtypeStruct(r.shape, dt) for r, dt in zip(recvd, out_dtypes)],
        compiler_params=_params(("arbitrary",)),
    )(idx, *grads, *recvd)


def _final_sum_call(idx, grads, recvd1, recvd2):
    n = len(grads)
    nsteps = 2

    def body(i_ref, *refs):
        for t in range(n):
            g, r1, r2, o = refs[t], refs[n + t], refs[2 * n + t], refs[3 * n + t]
            s = g[0, 0] + r1[0]
            for r in range(3):
                s = s + r2[r].astype(F32)
            o[...] = s

    def rows(a):
        r = a.shape[-2]
        return r // nsteps if (r // nsteps) % 16 == 0 else r

    def step(a):
        return (lambda i: i) if rows(a) != a.shape[-2] else (lambda i: 0)

    in_specs = [pl.BlockSpec((1, 1, rows(g), g.shape[3]), lambda i, s, st=step(g): (s[0], s[1], st(i), 0)) for g in grads]
    in_specs += [pl.BlockSpec((1, rows(r), r.shape[2]), lambda i, s, st=step(r): (s[0], st(i), 0)) for r in recvd1]
    in_specs += [pl.BlockSpec((3, rows(r), r.shape[2]), lambda i, s, st=step(r): (0, st(i), 0)) for r in recvd2]
    return pl.pallas_call(
        body, name="rs_final_sum",
        grid_spec=pltpu.PrefetchScalarGridSpec(
            num_scalar_prefetch=1, grid=(nsteps,), in_specs=in_specs,
            out_specs=[pl.BlockSpec((rows(r), r.shape[2]), lambda i, s, st=step(r): (st(i), 0)) for r in recvd2]),
        out_shape=[jax.ShapeDtypeStruct(r.shape[1:], F32) for r in recvd2],
        compiler_params=_params(("arbitrary",)),
    )(idx, *grads, *recvd1, *recvd2)


def _sum8_call(parts):
    def body(p_ref, o_ref):
        s = p_ref[0]
        for j in range(1, N_DEV):
            s = s + p_ref[j]
        o_ref[...] = s

    return pl.pallas_call(body, name="sum_small_partials",
                          out_shape=jax.ShapeDtypeStruct(parts.shape[1:], parts.dtype))(parts)


def _adamw(w, g, m, v):
    m = ADAM_B1 * m + (1.0 - ADAM_B1) * g
    v = ADAM_B2 * v + (1.0 - ADAM_B2) * (g * g)
    m_hat = m / (1.0 - ADAM_B1 ** ADAM_STEP)
    v_hat = v / (1.0 - ADAM_B2 ** ADAM_STEP)
    delta = -ADAM_LR * (m_hat / (jnp.sqrt(v_hat) + ADAM_EPS) + ADAM_WD * w)
    return delta, m, v


def _adamw_call(ws, gs, ms, vs, nsteps, name):
    n = len(ws)

    def body(*refs):
        for t in range(n):
            w, g, m, v = (refs[k * n + t][...] for k in range(4))
            d, m2, v2 = _adamw(w, g, m, v)
            refs[4 * n + t][...] = d
            refs[5 * n + t][...] = m2
            refs[6 * n + t][...] = v2

    def spec(a):
        assert a.shape[0] % nsteps == 0 and (nsteps == 1 or (a.shape[0] // nsteps) % 8 == 0), a.shape
        return pl.BlockSpec((a.shape[0] // nsteps, a.shape[1]), lambda i: (i, 0))

    specs = [spec(a) for a in ws]
    outs = pl.pallas_call(
        body, name=name, grid=(nsteps,),
        in_specs=specs * 4, out_specs=specs * 3,
        out_shape=[jax.ShapeDtypeStruct(a.shape, F32) for a in ws] * 3,
        compiler_params=_params(("arbitrary",)),
    )(*ws, *gs, *ms, *vs)
    return outs[:n], outs[n:2 * n], outs[2 * n:]


def _adamw_rs_call(idx, after, gws, r1s, r2s, ws, ms, vs, nsteps, name):
    n = len(ws)

    def body(i_ref, after_ref, *refs):
        for t in range(n):
            gw, r1, r2, w, m, v = (refs[k * n + t] for k in range(6))
            g = gw[0, 0] + r1[0]
            for r in range(3):
                g = g + r2[r].astype(F32)
            d, m2, v2 = _adamw(w[...], g, m[...], v[...])
            refs[6 * n + t][...] = g
            refs[7 * n + t][...] = d
            refs[8 * n + t][...] = m2
            refs[9 * n + t][...] = v2

    def rb(a):
        r = a.shape[0] // nsteps
        assert a.shape[0] % nsteps == 0 and r % 16 == 0, a.shape
        return r

    in_specs = [pl.BlockSpec((1, 1, rb(w), w.shape[1]), lambda i, s: (s[0], s[1], i, 0)) for w in ws]
    in_specs += [pl.BlockSpec((1, rb(w), w.shape[1]), lambda i, s: (s[0], i, 0)) for w in ws]
    in_specs += [pl.BlockSpec((3, rb(w), w.shape[1]), lambda i, s: (0, i, 0)) for w in ws]
    plain = [pl.BlockSpec((rb(w), w.shape[1]), lambda i, s: (i, 0)) for w in ws]
    outs = pl.pallas_call(
        body, name=name,
        grid_spec=pltpu.PrefetchScalarGridSpec(
            num_scalar_prefetch=1, grid=(nsteps,),
            in_specs=[pl.BlockSpec(memory_space=pl.ANY)] + in_specs + plain * 3, out_specs=plain * 4),
        out_shape=[jax.ShapeDtypeStruct(w.shape, F32) for w in ws] * 4,
        compiler_params=_params(("arbitrary",)),
    )(idx, after, *gws, *r1s, *r2s, *ws, *ms, *vs)
    return outs[:n], outs[n:2 * n], outs[2 * n:3 * n], outs[3 * n:]


def _rows128(a, pad_rows):
    flat = a.reshape(-1).astype(F32)
    flat = jnp.pad(flat, (0, pad_rows * LANES - flat.shape[0]))
    return flat.reshape(pad_rows, LANES)


_SMALL_A = (("w_pool", 512), ("pool_scale", 8), ("attn_sinks", 8), ("g_mix_post", 8), ("g_mlp_pre", 8),
            ("g_mlp_post", 8), ("loss", 8), ("b_in_gates", 16))
_SMALL_A_ROWS = 640
_SMALL_B = (("g_mix_pre", 8), ("b_in_head", 16))


def _pack(parts, layout, total_rows):
    rows = [_rows128(parts[k], r) for k, r in layout]
    pad = total_rows - sum(r for _, r in layout)
    if pad:
        rows.append(jnp.zeros((pad, LANES), F32))
    return jnp.concatenate(rows, axis=0)


def _unpack(buf, layout, sizes):
    out, off = {}, 0
    for k, r in layout:
        out[k] = buf[off:off + r].reshape(-1)[:sizes[k]]
        off += r
    return out


def kernel(x, g_mix_pre, w_in, b_in, w_pool, pool_scale, attn_sinks, w_branch_pool, w_branch_attn, w_out, g_mix_post, g_mlp_pre, w_up, w_down, g_mlp_post, loss_target, m_g_mix_pre, m_w_in, m_b_in, m_w_pool, m_pool_scale, m_attn_sinks, m_w_branch_pool, m_w_branch_attn, m_w_out, m_g_mix_post, m_g_mlp_pre, m_w_up, m_w_down, m_g_mlp_post, v_g_mix_pre, v_w_in, v_b_in, v_w_pool, v_pool_scale, v_attn_sinks, v_w_branch_pool, v_w_branch_attn, v_w_out, v_g_mix_post, v_g_mlp_pre, v_w_up, v_w_down, v_g_mlp_post):
    B, S, _ = x.shape
    T = B * S
    xt = x.reshape(T, D_MODEL)
    tgt = loss_target.reshape(T, D_MODEL)
    cx, cy, cc = _coords()

    cidx = jnp.stack([2 * cx + cy, cc]).astype(jnp.int32)
    by_chip = lambda gr: gr.reshape((4, 2) + gr.shape[1:])
    bf = lambda w: w[0].astype(MXU_DTYPE)

    me = _slot((cx, cy, cc))
    win_l = w_in[0].T.astype(MXU_DTYPE)
    (win_s,) = _allgather_call([win_l], [_gather_buffer(win_l, me)])
    win_t = win_s.reshape(IN_WIDTH, D_MODEL)
    wpool_b = bf(w_pool)
    rc, rsa, rsb = _rot_tables(S)

    wbp_l, wba_l, wout_l, wup_l, wdown_l = bf(w_branch_pool), bf(w_branch_attn), bf(w_out), bf(w_up), bf(w_down)
    (c_br, c_up, c_dn), tok = _copies_start(
        [_rider_ag_remote([wbp_l, wba_l, wout_l], me), _rider_ag_remote([wup_l], me), _rider_ag_remote([wdown_l], me)],
        "allgather_start", after=win_s)
    (h, u, q, k4, v4, g), _ = _inproj_call(xt, g_mix_pre, win_t, b_in, rc, rsa, rsb, S, rider=_after(tok))
    yp = _pool_call(u, wpool_b, pool_scale, S)
    wbp_1, wba_1, wout_1 = _copies_wait([c_br], yp, "allgather_wait_branch")
    (ya,), (wbp_s, wba_s, wout_s) = _attn_call(
        attn_sinks, q, k4, v4, S,
        rider=_rider_ag([(None, wbp_1, None, ALL), (None, wba_1, None, ALL), (None, wout_1, None, ALL)]))
    wout_f = wout_s.reshape(D_MODEL, D_MODEL)
    (wup_1,) = _copies_wait([c_up], ya, "allgather_wait_up")
    (mix, x1, h2), (wup_s,) = _mix_fwd_call(
        yp, ya, g, xt, wbp_s, wba_s, wout_f, g_mix_post, g_mlp_pre, rider=_rider_ag([(None, wup_1, None, ALL)]))
    act = _mlp_up_call(h2, wup_s)
    (wdown_1,) = _copies_wait([c_dn], act, "allgather_wait_down")
    (wdown_s,) = _comm_call(_rider_ag([(None, wdown_1, None, ALL)]), "allgather_pass_down")

    da, dff, dx1, dg3, dg4, lossvec = _mlp_call(x1, act, tgt, wup_s, wdown_s, g_mlp_pre, g_mlp_post)
    gw_down = by_chip(_wgrad_rows_call(act, dff, "wgrad_down")[0])
    (gw_up,), (r1_down,) = _wgrad_cols_call(h2, da, "wgrad_up", rider=_rider_rs_sibling([gw_down]))
    gw_up = by_chip(gw_up)
    (s_down,) = _chip_sum_call(cidx, [gw_down], [r1_down], [MXU_DTYPE], "rs_chip_sum_down")
    (dyp, do, dgates, dg2, dbg, gw_out, gw_bp, gw_ba), (r1_up,) = _mix_bwd_call(
        dx1, mix, yp, ya, g, wbp_s, wba_s, wout_f, g_mix_post, rider=_rider_rs_sibling([gw_up]))
    gw_out = by_chip(gw_out.reshape(N_DEV, D_MODEL // N_DEV, D_MODEL))
    gw_bp, gw_ba = by_chip(gw_bp), by_chip(gw_ba)
    (s_up,) = _chip_sum_call(cidx, [gw_up], [r1_up], [MXU_DTYPE], "rs_chip_sum_up")
    (c_down, c_up), tok = _copies_start([_rider_rs_chips([s_down]), _rider_rs_chips([s_up])], "rs_chips_start_mlp")
    (dq, dk, dv, dsink), (r1_out, r1_bp, r1_ba) = _attn_bwd_call(
        attn_sinks, q, k4, v4, do, rc, rsa, rsb, S, rider=_after(tok, _rider_rs_sibling([gw_out, gw_bp, gw_ba])))
    s_obb = _chip_sum_call(cidx, [gw_out, gw_bp, gw_ba], [r1_out, r1_bp, r1_ba], [MXU_DTYPE] * 3, "rs_chip_sum_branch")
    (c_obb,), tok = _copies_start([_rider_rs_chips(s_obb)], "rs_chips_start_branch")
    (du, dwp, dps), _ = _pool_bwd_call(u, dyp, wpool_b, pool_scale, S, rider=_after(tok))
    (gw_in,) = _wgrad_in_call(du, dq, dk, dv, dgates, h)
    gw_in = by_chip(gw_in)

    small_a = {"w_pool": dwp, "pool_scale": dps,
               "attn_sinks": jnp.sum(dsink.reshape(B, 8, LANES)[:, 0, :N_Q_HEADS], axis=0), "g_mix_post": dg2,
               "g_mlp_pre": dg3, "g_mlp_post": dg4, "loss": lossvec, "b_in_gates": dbg}
    gw_sa = by_chip(_pack(small_a, _SMALL_A, _SMALL_A_ROWS).reshape(N_DEV, _SMALL_A_ROWS // N_DEV, LANES))
    r1_in, r1_sa = _comm_call(_rider_rs_sibling([gw_in, gw_sa]), "rs_sibling_in")
    s_in, s_sa = _chip_sum_call(cidx, [gw_in, gw_sa], [r1_in, r1_sa], [MXU_DTYPE, F32], "rs_chip_sum_in")
    (c_in,), tok = _copies_start([_rider_rs_chips([s_in, s_sa])], "rs_chips_start_in")
    (gx, dg1, dba_in), _ = _inproj_bwd_call(du, dq, dk, dv, dgates, dx1, xt, win_t, g_mix_pre, rider=_after(tok))
    r2_down, r2_up, r2_out, r2_bp, r2_ba, r2_in, r2_sa = _copies_wait([c_down, c_up, c_obb, c_in], dg1, "rs_chips_wait")

    (g_sa,) = _final_sum_call(cidx, [gw_sa], [r1_sa], [r2_sa])
    part_b = _pack({"g_mix_pre": dg1, "b_in_head": dba_in}, _SMALL_B, sum(r for _, r in _SMALL_B))
    (c_small,), tok = _copies_start([_rider_gather_remote([g_sa, part_b])], "allgather_small_start")

    in_t = _adamw_rs_call(cidx, tok, [gw_in], [r1_in], [r2_in], [w_in[0].T], [m_w_in[0].T], [v_w_in[0].T], 2,
                          "adamw_w_in")
    rest = _adamw_rs_call(
        cidx, tok, [gw_bp, gw_ba, gw_out, gw_up, gw_down], [r1_bp, r1_ba, r1_out, r1_up, r1_down],
        [r2_bp, r2_ba, r2_out, r2_up, r2_down], [w_branch_pool[0], w_branch_attn[0], w_out[0], w_up[0], w_down[0]],
        [m_w_branch_pool[0], m_w_branch_attn[0], m_w_out[0], m_w_up[0], m_w_down[0]],
        [v_w_branch_pool[0], v_w_branch_attn[0], v_w_out[0], v_w_up[0], v_w_down[0]], N_DEV, "adamw_shards")
    big_g, big_d, big_m2, big_v2 = ([a[0].T] + list(b) for a, b in zip(in_t, rest))

    sa_all, sb_all = _copies_wait([c_small], rest[0][0], "allgather_small_wait")
    sa_all = lax.dynamic_update_slice(sa_all, g_sa[None], (me, 0, 0))
    sb_sum = _sum8_call(lax.dynamic_update_slice(sb_all, part_b[None], (me, 0, 0)))

    names = ["g_mix_pre", "b_in", "w_pool", "pool_scale", "attn_sinks", "g_mix_post", "g_mlp_pre", "g_mlp_post"]
    sm_w = dict(g_mix_pre=g_mix_pre, b_in=b_in, w_pool=w_pool, pool_scale=pool_scale, attn_sinks=attn_sinks,
                g_mix_post=g_mix_post, g_mlp_pre=g_mlp_pre, g_mlp_post=g_mlp_post)
    sm_m = dict(g_mix_pre=m_g_mix_pre, b_in=m_b_in, w_pool=m_w_pool, pool_scale=m_pool_scale, attn_sinks=m_attn_sinks,
                g_mix_post=m_g_mix_post, g_mlp_pre=m_g_mlp_pre, g_mlp_post=m_g_mlp_post)
    sm_v = dict(g_mix_pre=v_g_mix_pre, b_in=v_b_in, w_pool=v_w_pool, pool_scale=v_pool_scale, attn_sinks=v_attn_sinks,
                g_mix_post=v_g_mix_post, g_mlp_pre=v_g_mlp_pre, g_mlp_post=v_g_mlp_post)
    sizes = {k: sm_w[k].size for k in names}
    sizes.update(loss=D_MODEL, b_in_gates=GATE_WIDTH, b_in_head=C_G)
    sm_g = _unpack(sa_all.reshape(_SMALL_A_ROWS, LANES), _SMALL_A, sizes)
    sm_g.update(_unpack(sb_sum, _SMALL_B, sizes))
    sm_g["b_in"] = jnp.concatenate([sm_g["b_in_head"], sm_g["b_in_gates"]])
    loss = (0.5 / D_MODEL) * jnp.sum(sm_g["loss"])
    two_d = lambda a: a.reshape(-1, a.shape[-1])
    sd_, sm2_, sv2_ = _adamw_call([two_d(sm_w[k]) for k in names], [two_d(sm_g[k].reshape(sm_w[k].shape)) for k in names],
                                  [two_d(sm_m[k]) for k in names], [two_d(sm_v[k]) for k in names], 1, "adamw_small")
    like = lambda vals: {k: a.reshape(sm_w[k].shape) for k, a in zip(names, vals)}
    sm_d, sm_m2, sm_v2 = like(sd_), like(sm2_), like(sv2_)
    sm_gr = {k: sm_g[k].reshape(sm_w[k].shape) for k in names}

    order = ["g_mix_pre", "w_in", "b_in", "w_pool", "pool_scale", "attn_sinks", "w_branch_pool", "w_branch_attn",
             "w_out", "g_mix_post", "g_mlp_pre", "w_up", "w_down", "g_mlp_post"]
    big_names = ["w_in", "w_branch_pool", "w_branch_attn", "w_out", "w_up", "w_down"]
    lead = lambda a: a[None]
    tables = []
    for small_t, big_t in ((sm_gr, big_g), (sm_d, big_d), (sm_m2, big_m2), (sm_v2, big_v2)):
        bt = dict(zip(big_names, big_t))
        tables.append([lead(bt[k]) if k in bt else small_t[k] for k in order])
    return (loss, gx.reshape(B, S, D_MODEL), *tables[0], *tables[1], *tables[2], *tables[3])
```

```python
import jax
import jax.numpy as jnp
from jax import lax
from jax.experimental import pallas as pl
from jax.experimental.pallas import tpu as pltpu

F32 = jnp.float32
MXU_DTYPE = jnp.bfloat16
MESH = pl.DeviceIdType.MESH

D_MODEL = 1024
POOL_WINDOWS = (2, 4, 8, 16)
POOL_WIDTH = 512
POOL_GC = 128
HEAD_DIM = 64
N_Q_HEADS = 8
N_KV_HEADS = 2
GROUP = 4
ATTN_WIDTH = 512
KV_WIDTH = 128
BLOCK = 128
GATE_WIDTH = 2048
IN_WIDTH = 3328
D_FF = 4096
EPS = 1e-6
NEG_INF = -1e30
ROPE_THETA = 500000.0
ROT_DIM = 16
SCALE = HEAD_DIM ** -0.5
C_Q, C_K, C_V, C_G = 512, 1024, 1152, 1280

ADAM_LR = 0.001
ADAM_B1 = 0.9
ADAM_B2 = 0.999
ADAM_EPS = 1e-08
ADAM_WD = 0.01
ADAM_STEP = 10

N_DEV = 8
LANES = 128
VMEM_LIMIT = 56 * 1024 * 1024

NN = (((1,), (0,)), ((), ()))
NT = (((1,), (1,)), ((), ()))
TN = (((0,), (0,)), ((), ()))


def _dot(a, b, dims):
    return lax.dot_general(a, b, dims, preferred_element_type=F32)


def _params(sem=None):
    return pltpu.CompilerParams(dimension_semantics=sem, vmem_limit_bytes=VMEM_LIMIT)


def _tile(n, pref):
    t = min(n, pref)
    assert n % t == 0, (n, t)
    return t


class _Rider:
    def __init__(self, ins, out_shape, n_remote, n_local, plan, aliases=None, lands=None):
        self.ins, self.out_shape, self.n_remote, self.n_local = list(ins), list(out_shape), n_remote, n_local
        self.plan, self.aliases = plan, dict(aliases or {})
        self.lands = lands


def _after(token, rider=None):
    r = rider or _Rider([], [], 0, 0, lambda ins, outs, send, recv, loc, r0, l0: ([], []))
    return _Rider(r.ins + [token], r.out_shape, r.n_remote, r.n_local, r.plan, r.aliases)


def _launch(body, args, *, name, grid, in_specs, out_specs, out_shape, scratch_shapes=(), sem=None, rider=None):
    if rider is None:
        return pl.pallas_call(body, name=name, grid=grid, in_specs=in_specs, out_specs=out_specs, out_shape=out_shape,
                              scratch_shapes=list(scratch_shapes), compiler_params=_params(sem))(*args)
    n_in, n_out, n_scr = len(args), len(out_shape), len(scratch_shapes)
    r_in, r_out = len(rider.ins), len(rider.out_shape)
    copies = rider.n_remote + rider.n_local > 0

    def wrapped(*refs):
        ins, rins = refs[:n_in], refs[n_in:n_in + r_in]
        o0 = n_in + r_in
        outs, routs = refs[o0:o0 + n_out], refs[o0 + n_out:o0 + n_out + r_out]
        s0 = o0 + n_out + r_out
        scr = refs[s0:s0 + n_scr]
        if not copies:
            return body(*ins, *outs, *scr)
        send, recv, loc = refs[s0 + n_scr:]
        first, last = None, None
        for d in range(len(grid)):
            f, l = pl.program_id(d) == 0, pl.program_id(d) == pl.num_programs(d) - 1
            first = f if first is None else first & f
            last = l if last is None else last & l

        def start():
            remote, local = rider.plan(rins, routs, send, recv, loc, 0, 0)
            for cp in local + remote:
                cp.start()

        def finish():
            remote, local = rider.plan(rins, routs, send, recv, loc, 0, 0)
            for cp in remote + local:
                cp.wait()

        if first is None:
            start()
            body(*ins, *outs, *scr)
            finish()
        else:
            pl.when(first)(start)
            body(*ins, *outs, *scr)
            pl.when(last)(finish)

    hbm = pl.BlockSpec(memory_space=pl.ANY)
    dma = pltpu.SemaphoreType.DMA
    res = pl.pallas_call(
        wrapped, name=name, grid=grid, in_specs=list(in_specs) + [hbm] * r_in,
        out_specs=list(out_specs) + [hbm] * r_out, out_shape=list(out_shape) + rider.out_shape,
        scratch_shapes=list(scratch_shapes) + (
            [dma((max(rider.n_remote, 1),)), dma((max(rider.n_remote, 1),)), dma((max(rider.n_local, 1),))] if copies else []),
        input_output_aliases={n_in + i: n_out + o for i, o in rider.aliases.items()},
        compiler_params=_params(sem),
    )(*args, *rider.ins)
    return list(res[:n_out]), list(res[n_out:])


def _comm_call(rider, name):
    return _launch(lambda: None, [], name=name, grid=(), in_specs=[], out_specs=[], out_shape=[], rider=rider)[1]


_HBM = pl.BlockSpec(memory_space=pltpu.HBM)
_SEM = pl.BlockSpec(memory_space=pltpu.SEMAPHORE)
_EFFECT = pltpu.SideEffectType.DATAFLOW_SIDE_EFFECTING


def _copies_start(riders, name, after=None):
    assert all(r.n_local == 0 and not r.aliases for r in riders)
    extra = [] if after is None else [after]
    sizes = [(len(r.ins), len(r.out_shape)) for r in riders]
    bufs = []
    for r in riders:
        lands = r.lands or [lax.empty(s.shape, s.dtype) for s in r.out_shape]
        bufs += [pltpu.with_memory_space_constraint(a, pltpu.HBM) for a in list(r.ins) + list(lands)]
    nb, ng, ne = len(bufs), len(riders), len(extra)

    def body(*refs):
        sems, token, at = refs[2 * nb + ne:2 * nb + ne + 2 * ng], refs[-1], 0
        for g, (r, (ni, no)) in enumerate(zip(riders, sizes)):
            remote, _ = r.plan(refs[at:at + ni], refs[at + ni:at + ni + no], sems[2 * g], sems[2 * g + 1], None, 0, 0)
            for cp in remote:
                cp.start()
            at += ni + no
        token[...] = jnp.zeros_like(token)

    res = pl.pallas_call(
        body, name=name, in_specs=[_HBM] * nb + [pl.BlockSpec(memory_space=pl.ANY)] * ne,
        out_specs=[_HBM] * nb + [_SEM] * (2 * ng) + [pl.BlockSpec(memory_space=pltpu.VMEM)],
        out_shape=[pltpu.HBM(a.shape, a.dtype) for a in bufs]
        + [pltpu.SemaphoreType.DMA((r.n_remote,)) for r in riders for _ in range(2)]
        + [jax.ShapeDtypeStruct((8, LANES), F32)],
        input_output_aliases={i: i for i in range(nb)},
        compiler_params=pltpu.CompilerParams(has_side_effects=_EFFECT),
    )(*bufs, *extra)
    handles, at = [], 0
    for g, (r, (ni, no)) in enumerate(zip(riders, sizes)):
        handles.append((r, list(res[at:at + ni + no]), res[nb + 2 * g], res[nb + 2 * g + 1]))
        at += ni + no
    return handles, res[-1]


def _copies_wait(handles, after, name):
    bufs = [b for _, bs, _, _ in handles for b in bs]
    sems = [s for _, _, send, recv in handles for s in (send, recv)]
    nb, ng = len(bufs), len(handles)
    after = list(after) if isinstance(after, (list, tuple)) else [after]

    def body(*refs):
        at = 0
        for g, (rider, bs, _, _) in enumerate(handles):
            ni = len(rider.ins)
            remote, _ = rider.plan(refs[at:at + ni], refs[at + ni:at + len(bs)], refs[nb + 2 * g], refs[nb + 2 * g + 1],
                                   None, 0, 0)
            for cp in remote:
                cp.wait_send()
                cp.wait_recv()
            at += len(bs)

    res = pl.pallas_call(
        body, name=name, in_specs=[_HBM] * nb + [_SEM] * (2 * ng) + [pl.BlockSpec(memory_space=pl.ANY)] * len(after),
        out_specs=[_HBM] * nb, out_shape=[pltpu.HBM(a.shape, a.dtype) for a in bufs],
        input_output_aliases={i: i for i in range(nb)},
        compiler_params=pltpu.CompilerParams(has_side_effects=_EFFECT),
    )(*bufs, *sems, *after)
    lands, at = [], 0
    for rider, bs, _, _ in handles:
        lands += list(res[at + len(rider.ins):at + len(bs)])
        at += len(bs)
    return lands


def _rms_r(x):
    return lax.rsqrt(jnp.mean(x * x, axis=-1, keepdims=True) + EPS)


def _rms_bwd(dn, x, r, g):
    xh = x * r
    dxh = dn * g
    dx = r * (dxh - xh * jnp.mean(dxh * xh, axis=-1, keepdims=True))
    return dx, dn * xh


def _rot(t, c, sa, sb):
    outs = []
    for j in range(t.shape[1] // LANES):
        tj = t[:, LANES * j:LANES * (j + 1)]
        outs.append(tj * c + pltpu.roll(tj, LANES - 8, 1) * sa + pltpu.roll(tj, 8, 1) * sb)
    return outs[0] if len(outs) == 1 else jnp.concatenate(outs, axis=1)


def _rot_tables(S):
    pos = jnp.arange(S, dtype=F32)
    inv_freq = ROPE_THETA ** (-jnp.arange(0, ROT_DIM, 2, dtype=F32) / ROT_DIM)
    ang = pos[:, None] * inv_freq[None, :]
    cos, sin = jnp.cos(ang), jnp.sin(ang)
    one = jnp.ones((S, HEAD_DIM - ROT_DIM), F32)
    zero = jnp.zeros((S, HEAD_DIM - ROT_DIM), F32)
    z8 = jnp.zeros((S, 8), F32)
    c = jnp.concatenate([cos, cos, one], axis=1)
    sa = jnp.concatenate([-sin, z8, zero], axis=1)
    sb = jnp.concatenate([z8, sin, zero], axis=1)
    rep = LANES // HEAD_DIM
    return jnp.tile(c, (1, rep)), jnp.tile(sa, (1, rep)), jnp.tile(sb, (1, rep))


def _lane_tile4(k):
    lane = lax.broadcasted_iota(jnp.int32, k.shape, 1)
    rk = pltpu.roll(k, HEAD_DIM, 1)
    x0 = jnp.where(lane < HEAD_DIM, k, rk)
    x1 = jnp.where(lane < HEAD_DIM, rk, k)
    return jnp.concatenate([x0, x0, x1, x1], axis=1)


def _fold_heads(acc):
    zs = []
    for hk in range(N_KV_HEADS):
        a = acc[:, 256 * hk:256 * hk + LANES] + acc[:, 256 * hk + LANES:256 * (hk + 1)]
        zs.append(a + pltpu.roll(a, HEAD_DIM, 1))
    lane = lax.broadcasted_iota(jnp.int32, zs[0].shape, 1)
    return jnp.where(lane < HEAD_DIM, zs[0], zs[1])


def _inproj_call(x, g1, win_t, b_in, rc, rsa, rsb, S, rider=None):
    T = x.shape[0]
    tm = _tile(S, 512)
    nst = S // tm

    def body(x_ref, g1_ref, w_ref, b_ref, c_ref, sa_ref, sb_ref,
             h_ref, u_ref, q_ref, k4_ref, v4_ref, g_ref):
        xv = x_ref[...]
        hb = ((xv * _rms_r(xv)) * g1_ref[...]).astype(MXU_DTYPE)
        h_ref[...] = hb

        def proj(lo, hi):
            return _dot(hb, w_ref[lo:hi, :], NT) + b_ref[:, lo:hi]

        c, sa, sb = c_ref[...], sa_ref[...], sb_ref[...]
        u_ref[...] = proj(0, C_Q)
        q_ref[...] = (_rot(proj(C_Q, C_K), c, sa, sb) * SCALE).astype(MXU_DTYPE)
        kv = proj(C_K, C_G)
        k4_ref[...] = _lane_tile4(_rot(kv[:, :KV_WIDTH], c, sa, sb)).astype(MXU_DTYPE)
        v4_ref[...] = _lane_tile4(kv[:, KV_WIDTH:]).astype(MXU_DTYPE)
        g_ref[...] = jax.nn.sigmoid(proj(C_G, IN_WIDTH)).astype(MXU_DTYPE)

    tok = lambda w: pl.BlockSpec((tm, w), lambda i: (i, 0))
    full = lambda a: pl.BlockSpec(a.shape, lambda i: (0,) * a.ndim)
    tab = pl.BlockSpec((tm, LANES), lambda i: (i % nst, 0))
    return _launch(
        body, [x, g1, win_t, b_in, rc, rsa, rsb], name="inproj_fwd", grid=(T // tm,),
        in_specs=[tok(D_MODEL), full(g1), full(win_t), full(b_in), tab, tab, tab],
        out_specs=[tok(D_MODEL), tok(POOL_WIDTH), tok(ATTN_WIDTH), tok(512), tok(512), tok(GATE_WIDTH)],
        out_shape=[jax.ShapeDtypeStruct((T, D_MODEL), MXU_DTYPE), jax.ShapeDtypeStruct((T, POOL_WIDTH), F32),
                   jax.ShapeDtypeStruct((T, ATTN_WIDTH), MXU_DTYPE), jax.ShapeDtypeStruct((T, 512), MXU_DTYPE),
                   jax.ShapeDtypeStruct((T, 512), MXU_DTYPE), jax.ShapeDtypeStruct((T, GATE_WIDTH), MXU_DTYPE)],
        sem=("arbitrary",), rider=rider)


def _shift_rows(a, k, rows):
    n = a.shape[0]
    if k > 0:
        return jnp.where(rows >= k, pltpu.roll(a, k, 0), 0.0)
    return jnp.where(rows < n + k, pltpu.roll(a, n + k, 0), 0.0)


def _win_sum(a, w, rows, sign):
    s, k = a, 1
    while k < w:
        s = s + _shift_rows(s, sign * k, rows)
        k *= 2
    return s


def _pool_diff(ug, w, rows):
    inv = 1.0 / jnp.minimum(rows + 1, w).astype(F32)
    return _win_sum(ug, w, rows, 1) * inv - ug, inv


def _pool_call(u, w_pool, pool_scale, S):
    T = u.shape[0]

    def body(u_ref, w_ref, ps_ref, y_ref):
        rows = lax.broadcasted_iota(jnp.int32, (S, POOL_GC), 0)
        for gi, w in enumerate(POOL_WINDOWS):
            sl = slice(POOL_GC * gi, POOL_GC * (gi + 1))
            diff, _ = _pool_diff(u_ref[:, sl], w, rows)
            mixed = _dot(diff.astype(MXU_DTYPE), w_ref[gi], NN)
            y_ref[:, sl] = (mixed * ps_ref[:, sl]).astype(MXU_DTYPE)

    seq = pl.BlockSpec((S, POOL_WIDTH), lambda b: (b, 0))
    return pl.pallas_call(
        body, name="pool_fwd", grid=(T // S,),
        in_specs=[seq, pl.BlockSpec(w_pool.shape, lambda b: (0, 0, 0)), pl.BlockSpec(pool_scale.shape, lambda b: (0, 0))],
        out_specs=seq, out_shape=jax.ShapeDtypeStruct((T, POOL_WIDTH), MXU_DTYPE),
        compiler_params=_params(("arbitrary",)),
    )(u, w_pool, pool_scale)


def _pool_bwd_call(u, dyp, w_pool, pool_scale, S, rider=None):
    T = u.shape[0]

    def body(u_ref, dy_ref, w_ref, ps_ref, du_ref, dw_ref, dps_ref):
        @pl.when(pl.program_id(0) == 0)
        def _():
            dw_ref[...] = jnp.zeros_like(dw_ref)
            dps_ref[...] = jnp.zeros_like(dps_ref)

        rows = lax.broadcasted_iota(jnp.int32, (S, POOL_GC), 0)
        for gi, w in enumerate(POOL_WINDOWS):
            sl = slice(POOL_GC * gi, POOL_GC * (gi + 1))
            diff, inv = _pool_diff(u_ref[:, sl], w, rows)
            diffb = diff.astype(MXU_DTYPE)
            wg = w_ref[gi]
            mixed = _dot(diffb, wg, NN)
            dy = dy_ref[:, sl]
            dps_ref[:, sl] += jnp.sum(dy * mixed, axis=0, keepdims=True)
            dmb = (dy * ps_ref[:, sl]).astype(MXU_DTYPE)
            dw_ref[gi] += _dot(diffb, dmb, TN)
            ddiff = _dot(dmb, wg, NT)
            du_ref[:, sl] = (_win_sum(ddiff * inv, w, rows, -1) - ddiff).astype(MXU_DTYPE)

    seq = pl.BlockSpec((S, POOL_WIDTH), lambda b: (b, 0))
    return _launch(
        body, [u, dyp, w_pool, pool_scale], name="pool_bwd", grid=(T // S,),
        in_specs=[seq, seq, pl.BlockSpec(w_pool.shape, lambda b: (0, 0, 0)), pl.BlockSpec(pool_scale.shape, lambda b: (0, 0))],
        out_specs=[seq, pl.BlockSpec(w_pool.shape, lambda b: (0, 0, 0)), pl.BlockSpec(pool_scale.shape, lambda b: (0, 0))],
        out_shape=[jax.ShapeDtypeStruct((T, POOL_WIDTH), MXU_DTYPE), jax.ShapeDtypeStruct(w_pool.shape, F32),
                   jax.ShapeDtypeStruct(pool_scale.shape, F32)],
        sem=("arbitrary",), rider=rider)


def _attn_consts():
    lane_g = lax.broadcasted_iota(jnp.int32, (BLOCK, 256), 1) >> 6
    rgrp = lax.broadcasted_iota(jnp.int32, (GROUP * BLOCK, 1), 0) >> 7
    rel = lax.broadcasted_iota(jnp.int32, (BLOCK, 256), 0) - lax.broadcasted_iota(jnp.int32, (BLOCK, 256), 1)

    def bias(off):
        ok = (rel + off >= 0) & (rel + off < BLOCK)
        return jnp.concatenate([jnp.where(ok, 0.0, NEG_INF)] * GROUP, axis=0)

    return lane_g, rgrp, bias(0), bias(BLOCK)


def _sink_rows(sink_ref, hk, rgrp):
    sv = jnp.zeros(rgrp.shape, F32)
    for g in range(GROUP):
        sv = jnp.where(rgrp == g, sink_ref[0, GROUP * hk + g], sv)
    return sv


def _stack_heads(xb, lane_g):
    return jnp.concatenate([jnp.where(lane_g == g, xb, jnp.zeros_like(xb)) for g in range(GROUP)], axis=0)


def _unstack_heads(xs, lane_g):
    out = jnp.where(lane_g == 0, xs[0:BLOCK], 0.0)
    for g in range(1, GROUP):
        out = out + jnp.where(lane_g == g, xs[BLOCK * g:BLOCK * (g + 1)], 0.0)
    return out


def _attn_probs(qs, kb, bias, sv):
    s = _dot(qs, kb, NT) + bias
    m = jnp.maximum(jnp.max(s, axis=1, keepdims=True), sv)
    e = jnp.exp(s - m)
    es = jnp.exp(sv - m)
    inv_l = 1.0 / (jnp.sum(e, axis=1, keepdims=True) + es)
    return e * inv_l, es * inv_l


def _attn_blocks(nb, blk, carry, per=1):
    carry = blk(0, 0, True, carry)
    per = per if (nb - 1) % per == 0 else 1

    def step(i, c):
        for k in range(per):
            n = 1 + per * i + k
            c = blk(pl.multiple_of(n * BLOCK, BLOCK), pl.multiple_of((n - 1) * BLOCK, BLOCK), False, c)
        return c

    return lax.fori_loop(0, (nb - 1) // per, step, carry)


def _attn_call(sinks, q, k4, v4, S, rider=None):
    T = q.shape[0]
    nb = S // BLOCK

    def body(sink_ref, q_ref, k_ref, v_ref, o_ref):
        lane_g, rgrp, bias_first, bias_later = _attn_consts()
        svs = [_sink_rows(sink_ref, hk, rgrp) for hk in range(N_KV_HEADS)]

        def blk(q0, k0, first, carry):
            for hk in range(N_KV_HEADS):
                cs = slice(256 * hk, 256 * (hk + 1))
                qs = _stack_heads(q_ref[pl.ds(q0, BLOCK), cs], lane_g)
                p, _ = _attn_probs(qs, k_ref[pl.ds(k0, 2 * BLOCK), cs], bias_first if first else bias_later, svs[hk])
                o = _dot(p.astype(MXU_DTYPE), v_ref[pl.ds(k0, 2 * BLOCK), cs], NN)
                o_ref[pl.ds(q0, BLOCK), cs] = _unstack_heads(o, lane_g).astype(MXU_DTYPE)
            return carry

        _attn_blocks(nb, blk, 0, per=3)

    seq = pl.BlockSpec((S, ATTN_WIDTH), lambda b: (b, 0))
    return _launch(
        body, [sinks, q, k4, v4], name="attn_fwd", grid=(T // S,),
        in_specs=[pl.BlockSpec(memory_space=pltpu.SMEM), seq, seq, seq],
        out_specs=[seq], out_shape=[jax.ShapeDtypeStruct((T, ATTN_WIDTH), MXU_DTYPE)],
        sem=("arbitrary",), rider=rider)


def _attn_bwd_call(sinks, q, k4, v4, do, rc, rsa, rsb, S, rider=None):
    T = q.shape[0]
    nb = S // BLOCK

    def body(sink_ref, q_ref, k_ref, v_ref, do_ref, c_ref, sa_ref, sb_ref,
             dq_ref, dk_ref, dv_ref, ds_ref, dk_acc, dv_acc):
        lane_g, rgrp, bias_first, bias_later = _attn_consts()
        svs = [_sink_rows(sink_ref, hk, rgrp) for hk in range(N_KV_HEADS)]
        lane1 = lax.broadcasted_iota(jnp.int32, (1, LANES), 1)
        dk_acc[...] = jnp.zeros_like(dk_acc)
        dv_acc[...] = jnp.zeros_like(dv_acc)

        def blk(q0, k0, first, dsink):
            rows = pl.ds(q0, BLOCK)
            c, sa, sb = c_ref[rows, :], sa_ref[rows, :], sb_ref[rows, :]
            for hk in range(N_KV_HEADS):
                cs = slice(256 * hk, 256 * (hk + 1))
                qs = _stack_heads(q_ref[rows, cs], lane_g)
                dos = _stack_heads(do_ref[rows, cs], lane_g)
                kb = k_ref[pl.ds(k0, 2 * BLOCK), cs]
                vb = v_ref[pl.ds(k0, 2 * BLOCK), cs]
                p, ps = _attn_probs(qs, kb, bias_first if first else bias_later, svs[hk])
                dp = _dot(dos, vb, NT)
                delta = jnp.sum(p * dp, axis=1, keepdims=True)
                dsb = (p * (dp - delta)).astype(MXU_DTYPE)
                dqb = _unstack_heads(_dot(dsb, kb, NN), lane_g) * SCALE
                dq_ref[rows, cs] = _rot(dqb, c, -sa, -sb).astype(MXU_DTYPE)
                dk_acc[pl.ds(k0, 2 * BLOCK), cs] += _dot(dsb, qs, TN)
                dv_acc[pl.ds(k0, 2 * BLOCK), cs] += _dot(p.astype(MXU_DTYPE), dos, TN)
                psd = ps * delta
                for g in range(GROUP):
                    val = -jnp.sum(psd[BLOCK * g:BLOCK * (g + 1)], axis=0, keepdims=True)
                    dsink = dsink + jnp.where(lane1 == GROUP * hk + g, val, 0.0)
            return dsink

        dsink = _attn_blocks(nb, blk, jnp.zeros((1, LANES), F32))
        dk_ref[...] = _rot(_fold_heads(dk_acc[...]), c_ref[...], -sa_ref[...], -sb_ref[...]).astype(MXU_DTYPE)
        dv_ref[...] = _fold_heads(dv_acc[...]).astype(MXU_DTYPE)
        ds_ref[...] = jnp.broadcast_to(dsink, ds_ref.shape)

    seq = pl.BlockSpec((S, ATTN_WIDTH), lambda b: (b, 0))
    kvs = pl.BlockSpec((S, KV_WIDTH), lambda b: (b, 0))
    tab = pl.BlockSpec((S, LANES), lambda b: (0, 0))
    nseq = T // S
    return _launch(
        body, [sinks, q, k4, v4, do, rc, rsa, rsb], name="attn_bwd", grid=(nseq,),
        in_specs=[pl.BlockSpec(memory_space=pltpu.SMEM), seq, seq, seq, seq, tab, tab, tab],
        out_specs=[seq, kvs, kvs, pl.BlockSpec((8, LANES), lambda b: (b, 0))],
        out_shape=[jax.ShapeDtypeStruct((T, ATTN_WIDTH), MXU_DTYPE), jax.ShapeDtypeStruct((T, KV_WIDTH), MXU_DTYPE),
                   jax.ShapeDtypeStruct((T, KV_WIDTH), MXU_DTYPE), jax.ShapeDtypeStruct((8 * nseq, LANES), F32)],
        scratch_shapes=[pltpu.VMEM((S, 512), F32), pltpu.VMEM((S, 512), F32)],
        sem=("arbitrary",), rider=rider)


def _branch_weights(wbp_ref, wba_ref, wbp_s, wba_s):
    @pl.when(pl.program_id(0) == 0)
    def _():
        for j in range(N_DEV):
            wbp_s[:, LANES * j:LANES * (j + 1)] = wbp_ref[j]
            wba_s[:, LANES * j:LANES * (j + 1)] = wba_ref[j]


def _mix_fwd_call(yp, ya, g, x, wbp, wba, wout, g2, g3, rider=None):
    T = x.shape[0]
    tm = _tile(T, 512)

    def body(yp_ref, ya_ref, g_ref, x_ref, wbp_ref, wba_ref, wout_ref, g2_ref, g3_ref,
             mix_ref, x1_ref, h2_ref, wbp_s, wba_s):
        _branch_weights(wbp_ref, wba_ref, wbp_s, wba_s)
        bp = _dot(yp_ref[...], wbp_s[...], NN)
        ba = _dot(ya_ref[...], wba_s[...], NN)
        merged = g_ref[:, :D_MODEL].astype(F32) * bp + g_ref[:, D_MODEL:].astype(F32) * ba
        mix = _dot(merged.astype(MXU_DTYPE), wout_ref[...], NN)
        mix_ref[...] = mix
        x1 = x_ref[...] + (mix * _rms_r(mix)) * g2_ref[...]
        x1_ref[...] = x1
        h2_ref[...] = ((x1 * _rms_r(x1)) * g3_ref[...]).astype(MXU_DTYPE)

    tok = lambda w: pl.BlockSpec((tm, w), lambda i: (i, 0))
    full = lambda a: pl.BlockSpec(a.shape, lambda i: (0,) * a.ndim)
    return _launch(
        body, [yp, ya, g, x, wbp, wba, wout, g2, g3], name="mix_fwd", grid=(T // tm,),
        in_specs=[tok(POOL_WIDTH), tok(ATTN_WIDTH), tok(GATE_WIDTH), tok(D_MODEL), full(wbp), full(wba), full(wout),
                  full(g2), full(g3)],
        out_specs=[tok(D_MODEL), tok(D_MODEL), tok(D_MODEL)],
        out_shape=[jax.ShapeDtypeStruct((T, D_MODEL), F32), jax.ShapeDtypeStruct((T, D_MODEL), F32),
                   jax.ShapeDtypeStruct((T, D_MODEL), MXU_DTYPE)],
        scratch_shapes=[pltpu.VMEM((POOL_WIDTH, D_MODEL), MXU_DTYPE), pltpu.VMEM((ATTN_WIDTH, D_MODEL), MXU_DTYPE)],
        sem=("arbitrary",), rider=rider)


def _mix_bwd_call(dx1, mix, yp, ya, g, wbp, wba, wout, g2, rider=None):
    T = dx1.shape[0]
    tm = _tile(T, 512)

    def body(dx1_ref, mix_ref, yp_ref, ya_ref, g_ref, wbp_ref, wba_ref, wout_ref, g2_ref,
             dyp_ref, do_ref, dgates_ref, dg2_ref, dbg_ref, gout_ref, gbp_ref, gba_ref,
             wbp_s, wba_s, acc_out, acc_bp, acc_ba, sem):
        _branch_weights(wbp_ref, wba_ref, wbp_s, wba_s)
        step = pl.program_id(0)

        @pl.when(step == 0)
        def _():
            dg2_ref[...] = jnp.zeros_like(dg2_ref)
            dbg_ref[...] = jnp.zeros_like(dbg_ref)
            acc_out[...] = jnp.zeros_like(acc_out)
            acc_bp[...] = jnp.zeros_like(acc_bp)
            acc_ba[...] = jnp.zeros_like(acc_ba)

        mix = mix_ref[...]
        dmix, dg2 = _rms_bwd(dx1_ref[...], mix, _rms_r(mix), g2_ref[...])
        dg2_ref[...] += jnp.sum(dg2, axis=0, keepdims=True)
        dmixb = dmix.astype(MXU_DTYPE)
        dmerged = _dot(dmixb, wout_ref[...], NT)
        yp, ya = yp_ref[...], ya_ref[...]
        bp = _dot(yp, wbp_s[...], NN)
        ba = _dot(ya, wba_s[...], NN)
        gp, ga = g_ref[:, :D_MODEL].astype(F32), g_ref[:, D_MODEL:].astype(F32)
        acc_out[...] += _dot((gp * bp + ga * ba).astype(MXU_DTYPE), dmixb, TN)
        dgp = dmerged * bp * (gp * (1.0 - gp))
        dga = dmerged * ba * (ga * (1.0 - ga))
        dbg_ref[:, :D_MODEL] += jnp.sum(dgp, axis=0, keepdims=True)
        dbg_ref[:, D_MODEL:] += jnp.sum(dga, axis=0, keepdims=True)
        dgates_ref[:, :D_MODEL] = dgp.astype(MXU_DTYPE)
        dgates_ref[:, D_MODEL:] = dga.astype(MXU_DTYPE)
        dbp = (dmerged * gp).astype(MXU_DTYPE)
        dba = (dmerged * ga).astype(MXU_DTYPE)
        acc_bp[...] += _dot(yp, dbp, TN)
        acc_ba[...] += _dot(ya, dba, TN)
        dyp_ref[...] = _dot(dbp, wbp_s[...], NT)
        do_ref[...] = _dot(dba, wba_s[...], NT).astype(MXU_DTYPE)

        @pl.when(step == pl.num_programs(0) - 1)
        def _():
            copies = [pltpu.make_async_copy(acc_out, gout_ref, sem.at[0])]
            for j in range(N_DEV):
                cols = slice(LANES * j, LANES * (j + 1))
                copies.append(pltpu.make_async_copy(acc_bp.at[:, cols], gbp_ref.at[j], sem.at[1 + j]))
                copies.append(pltpu.make_async_copy(acc_ba.at[:, cols], gba_ref.at[j], sem.at[1 + N_DEV + j]))
            for cp in copies:
                cp.start()
            for cp in copies:
                cp.wait()

    tok = lambda w: pl.BlockSpec((tm, w), lambda i: (i, 0))
    full = lambda a: pl.BlockSpec(a.shape, lambda i: (0,) * a.ndim)
    acc = lambda w: pl.BlockSpec((1, w), lambda i: (0, 0))
    hbm = pl.BlockSpec(memory_space=pl.ANY)
    sd = jax.ShapeDtypeStruct
    return _launch(
        body, [dx1, mix, yp, ya, g, wbp, wba, wout, g2], name="mix_bwd", grid=(T // tm,),
        in_specs=[tok(D_MODEL), tok(D_MODEL), tok(POOL_WIDTH), tok(ATTN_WIDTH), tok(GATE_WIDTH), full(wbp), full(wba),
                  full(wout), full(g2)],
        out_specs=[tok(POOL_WIDTH), tok(ATTN_WIDTH), tok(GATE_WIDTH), acc(D_MODEL), acc(GATE_WIDTH), hbm, hbm, hbm],
        out_shape=[sd((T, POOL_WIDTH), F32), sd((T, ATTN_WIDTH), MXU_DTYPE), sd((T, GATE_WIDTH), MXU_DTYPE),
                   sd((1, D_MODEL), F32), sd((1, GATE_WIDTH), F32), sd((D_MODEL, D_MODEL), F32),
                   sd((N_DEV, POOL_WIDTH, LANES), F32), sd((N_DEV, ATTN_WIDTH, LANES), F32)],
        scratch_shapes=[pltpu.VMEM((POOL_WIDTH, D_MODEL), MXU_DTYPE), pltpu.VMEM((ATTN_WIDTH, D_MODEL), MXU_DTYPE),
                        pltpu.VMEM((D_MODEL, D_MODEL), F32), pltpu.VMEM((POOL_WIDTH, D_MODEL), F32),
                        pltpu.VMEM((ATTN_WIDTH, D_MODEL), F32), pltpu.SemaphoreType.DMA((1 + 2 * N_DEV,))],
        sem=("arbitrary",), rider=rider)


def _mlp_up_call(h2, wup):
    T = h2.shape[0]
    tm = _tile(T, 512)
    fc = D_FF // N_DEV

    def body(h2_ref, wup_ref, act_ref):
        h2 = h2_ref[...]
        for j in range(N_DEV):
            rl = jnp.maximum(_dot(h2, wup_ref[j], NN), 0.0)
            act_ref[:, fc * j:fc * (j + 1)] = (rl * rl).astype(MXU_DTYPE)

    sd = jax.ShapeDtypeStruct
    return pl.pallas_call(
        body, name="mlp_up", grid=(T // tm,),
        in_specs=[pl.BlockSpec((tm, D_MODEL), lambda i: (i, 0)),
                  pl.BlockSpec(wup.shape, lambda i: (0, 0, 0), pipeline_mode=pl.Buffered(1))],
        out_specs=pl.BlockSpec((tm, D_FF), lambda i: (i, 0)), out_shape=sd((T, D_FF), MXU_DTYPE),
        compiler_params=_params(("arbitrary",)),
    )(h2, wup)


def _mlp_call(x1, act, target, wup, wdown, g3, g4):
    T = x1.shape[0]
    tm = _tile(T, 256)
    fc = D_FF // N_DEV

    def body(x1_ref, act_ref, t_ref, wup_ref, wdown_ref, g3_ref, g4_ref,
             da_ref, dff_ref, dx1_ref, dg3_ref, dg4_ref, loss_ref):
        @pl.when(pl.program_id(0) == 0)
        def _():
            dg3_ref[...] = jnp.zeros_like(dg3_ref)
            dg4_ref[...] = jnp.zeros_like(dg4_ref)
            loss_ref[...] = jnp.zeros_like(loss_ref)

        ff = jnp.zeros((tm, D_MODEL), F32)
        for j in range(N_DEV):
            ff = ff + _dot(act_ref[:, fc * j:fc * (j + 1)], wdown_ref[j], NN)
        x1 = x1_ref[...]
        r4 = _rms_r(ff)
        err = x1 + (ff * r4) * g4_ref[...] - t_ref[...]
        loss_ref[...] += jnp.sum(err * err, axis=0, keepdims=True)
        dy = err * (1.0 / D_MODEL)
        dff, dg4 = _rms_bwd(dy, ff, r4, g4_ref[...])
        dg4_ref[...] += jnp.sum(dg4, axis=0, keepdims=True)
        dffb = dff.astype(MXU_DTYPE)
        dff_ref[...] = dffb
        dh2 = jnp.zeros((tm, D_MODEL), F32)
        for j in range(N_DEV):
            sl = slice(fc * j, fc * (j + 1))
            rl = jnp.sqrt(act_ref[:, sl].astype(F32))
            dab = (_dot(dffb, wdown_ref[j], NT) * (2.0 * rl)).astype(MXU_DTYPE)
            da_ref[:, sl] = dab
            dh2 = dh2 + _dot(dab, wup_ref[j], NT)
        dx1, dg3 = _rms_bwd(dh2, x1, _rms_r(x1), g3_ref[...])
        dg3_ref[...] += jnp.sum(dg3, axis=0, keepdims=True)
        dx1_ref[...] = dy + dx1

    tok = lambda w: pl.BlockSpec((tm, w), lambda i: (i, 0))
    full = lambda a: pl.BlockSpec(a.shape, lambda i: (0,) * a.ndim, pipeline_mode=pl.Buffered(1))
    vec = pl.BlockSpec((1, D_MODEL), lambda i: (0, 0))
    sd = jax.ShapeDtypeStruct
    return pl.pallas_call(
        body, name="mlp_down_bwd", grid=(T // tm,),
        in_specs=[tok(D_MODEL), tok(D_FF), tok(D_MODEL), full(wup), full(wdown), vec, vec],
        out_specs=[tok(D_FF), tok(D_MODEL), tok(D_MODEL), vec, vec, vec],
        out_shape=[sd((T, D_FF), MXU_DTYPE), sd((T, D_MODEL), MXU_DTYPE),
                   sd((T, D_MODEL), F32), sd((1, D_MODEL), F32), sd((1, D_MODEL), F32), sd((1, D_MODEL), F32)],
        compiler_params=_params(("arbitrary",)),
    )(x1, act, target, wup, wdown, g3, g4)


def _inproj_bwd_call(du, dq, dk, dv, dgates, dx1, x, win_t, g1, rider=None):
    T = x.shape[0]
    tm = _tile(T, 512)

    def body(du_ref, dq_ref, dk_ref, dv_ref, dgt_ref, dx1_ref, x_ref, w_ref, g1_ref, gx_ref, dg1_ref, db_ref):
        @pl.when(pl.program_id(0) == 0)
        def _():
            dg1_ref[...] = jnp.zeros_like(dg1_ref)
            db_ref[...] = jnp.zeros_like(db_ref)

        dh = jnp.zeros((tm, D_MODEL), F32)
        for ref, lo, hi in ((du_ref, 0, C_Q), (dq_ref, C_Q, C_K), (dk_ref, C_K, C_V), (dv_ref, C_V, C_G),
                            (dgt_ref, C_G, IN_WIDTH)):
            piece = ref[...]
            dh = dh + _dot(piece, w_ref[lo:hi, :], NN)
            if hi <= C_G:
                db_ref[:, lo:hi] += jnp.sum(piece.astype(F32), axis=0, keepdims=True)
        xv = x_ref[...]
        dx, dg1 = _rms_bwd(dh, xv, _rms_r(xv), g1_ref[...])
        dg1_ref[...] += jnp.sum(dg1, axis=0, keepdims=True)
        gx_ref[...] = dx1_ref[...] + dx

    tok = lambda w: pl.BlockSpec((tm, w), lambda i: (i, 0))
    full = lambda a: pl.BlockSpec(a.shape, lambda i: (0,) * a.ndim)
    sd = jax.ShapeDtypeStruct
    return _launch(
        body, [du, dq, dk, dv, dgates, dx1, x, win_t, g1], name="inproj_bwd", grid=(T // tm,),
        in_specs=[tok(POOL_WIDTH), tok(ATTN_WIDTH), tok(KV_WIDTH), tok(KV_WIDTH), tok(GATE_WIDTH), tok(D_MODEL),
                  tok(D_MODEL), full(win_t), full(g1)],
        out_specs=[tok(D_MODEL), pl.BlockSpec((1, D_MODEL), lambda i: (0, 0)), pl.BlockSpec((1, C_G), lambda i: (0, 0))],
        out_shape=[sd((T, D_MODEL), F32), sd((1, D_MODEL), F32), sd((1, C_G), F32)],
        sem=("arbitrary",), rider=rider)


WGRAD_TOKENS = 1024


def _wgrad_rows_call(a, b, name, rider=None):
    T, K = a.shape
    N = b.shape[1]
    tm = _tile(T, WGRAD_TOKENS)
    kb = min(K, 1024)
    per = kb // (K // N_DEV)

    def body(a_ref, b_ref, o_ref):
        @pl.when(pl.program_id(1) == 0)
        def _():
            o_ref[...] = jnp.zeros_like(o_ref)

        d = _dot(a_ref[...], b_ref[...], TN)
        rs = kb // per
        for j in range(per):
            o_ref[j] += d[rs * j:rs * (j + 1)]

    return _launch(
        body, [a, b], name=name, grid=(K // kb, T // tm),
        in_specs=[pl.BlockSpec((tm, kb), lambda i, t: (t, i)), pl.BlockSpec((tm, N), lambda i, t: (t, 0))],
        out_specs=[pl.BlockSpec((per, K // N_DEV, N), lambda i, t: (i, 0, 0))],
        out_shape=[jax.ShapeDtypeStruct((N_DEV, K // N_DEV, N), F32)],
        sem=("arbitrary", "arbitrary"), rider=rider)


def _wgrad_cols_call(a, b, name, rider=None):
    T, K = a.shape
    N = b.shape[1]
    tm = _tile(T, WGRAD_TOKENS)
    nb = min(N, 1024)
    per = nb // (N // N_DEV)

    def body(a_ref, b_ref, o_ref):
        @pl.when(pl.program_id(1) == 0)
        def _():
            o_ref[...] = jnp.zeros_like(o_ref)

        d = _dot(a_ref[...], b_ref[...], TN)
        cs = nb // per
        for j in range(per):
            o_ref[j] += d[:, cs * j:cs * (j + 1)]

    return _launch(
        body, [a, b], name=name, grid=(N // nb, T // tm),
        in_specs=[pl.BlockSpec((tm, K), lambda i, t: (t, 0)), pl.BlockSpec((tm, nb), lambda i, t: (t, i))],
        out_specs=[pl.BlockSpec((per, K, N // N_DEV), lambda i, t: (i, 0, 0))],
        out_shape=[jax.ShapeDtypeStruct((N_DEV, K, N // N_DEV), F32)],
        sem=("arbitrary", "arbitrary"), rider=rider)


def _wgrad_in_call(du, dq, dk, dv, dgates, h, rider=None):
    T = h.shape[0]
    tm = _tile(T, WGRAD_TOKENS)
    rows = IN_WIDTH // N_DEV

    def body(du_ref, dq_ref, dk_ref, dv_ref, dgt_ref, h_ref, o_ref, acc, sem):
        t = pl.program_id(0)

        @pl.when(t == 0)
        def _():
            acc[...] = jnp.zeros_like(acc)

        hv = h_ref[...]
        for ref, lo, hi in ((du_ref, 0, C_Q), (dq_ref, C_Q, C_K), (dk_ref, C_K, C_V), (dv_ref, C_V, C_G),
                            (dgt_ref, C_G, IN_WIDTH)):
            acc[lo:hi, :] += _dot(ref[...], hv, TN)

        @pl.when(t == pl.num_programs(0) - 1)
        def _():
            copies = [pltpu.make_async_copy(acc.at[pl.ds(rows * j, rows), :], o_ref.at[j], sem.at[j])
                      for j in range(N_DEV)]
            for cp in copies:
                cp.start()
            for cp in copies:
                cp.wait()

    tok = lambda w: pl.BlockSpec((tm, w), lambda t: (t, 0))
    return _launch(
        body, [du, dq, dk, dv, dgates, h], name="wgrad_in", grid=(T // tm,),
        in_specs=[tok(POOL_WIDTH), tok(ATTN_WIDTH), tok(KV_WIDTH), tok(KV_WIDTH), tok(GATE_WIDTH), tok(D_MODEL)],
        out_specs=[pl.BlockSpec(memory_space=pl.ANY)],
        out_shape=[jax.ShapeDtypeStruct((N_DEV, rows, D_MODEL), F32)],
        scratch_shapes=[pltpu.VMEM((IN_WIDTH, D_MODEL), F32), pltpu.SemaphoreType.DMA((N_DEV,))],
        sem=("arbitrary",), rider=rider)


def _coords():
    return lax.axis_index("x"), lax.axis_index("y"), lax.axis_index("c")


def _ag_route():
    x, y, c = _coords()
    return (x, y, c), (x, y, 1 - c), (x ^ (1 - c), y ^ c, c), (x ^ c, y ^ (1 - c), c), (1 - x, 1 - y, c)


def _rider_ag_first(shard, me):
    def plan(ins, outs, send, recv, loc, r0, l0):
        own, *peers = _ag_route()
        return [pltpu.make_async_remote_copy(
            src_ref=ins[0], dst_ref=outs[0].at[_slot(own)], send_sem=send.at[r0 + k], recv_sem=recv.at[r0 + k],
            device_id=peers[k], device_id_type=MESH) for k in range(3)], []

    return _Rider([shard], [jax.ShapeDtypeStruct((N_DEV,) + shard.shape, shard.dtype)], 3, 0, plan,
                  lands=[_gather_buffer(shard, me)])


def _allgather_relay_call(buf):
    def body(_, out, send_sems, recv_sems):
        own, sibling, near1, near2, diag = _ag_route()

        def copy(k, block, to):
            part = out.at[_slot(block)]
            return pltpu.make_async_remote_copy(src_ref=part, dst_ref=part, send_sem=send_sems.at[k],
                                                recv_sem=recv_sems.at[k], device_id=to, device_id_type=MESH)

        sent = [copy(0, near1, near2), copy(1, near1, sibling), copy(2, near2, sibling)]
        for cp in sent:
            cp.start()
        copy(0, diag, own).wait_recv()
        sent.append(copy(3, diag, sibling))
        sent[-1].start()
        for k, block in enumerate([near2, near1, diag]):
            copy(1 + k, (block[0], block[1], sibling[2]), own).wait_recv()
        for cp in sent:
            cp.wait_send()

    hbm = pl.BlockSpec(memory_space=pl.ANY)
    return pl.pallas_call(
        body, name="allgather_relay", in_specs=[hbm], out_specs=hbm, out_shape=jax.ShapeDtypeStruct(buf.shape, buf.dtype),
        scratch_shapes=[pltpu.SemaphoreType.DMA((4,)), pltpu.SemaphoreType.DMA((4,))], input_output_aliases={0: 0},
    )(buf)


def _slot(p):
    return 4 * p[0] + 2 * p[1] + p[2]


def _rows(ref, span):
    return ref if span is None else ref.at[pl.ds(span[0], span[1])]


ALL = "all"
LOCAL = "local"


def _rows(ref, span):
    return ref if span == ALL else ref.at[pl.ds(span[0], span[1])]


def _rider_ag(items):
    ins, out_shape, aliases, where = [], [], {}, []
    n_remote = n_local = 0
    for t, (shard, buf, snd, fwd) in enumerate(items):
        i_shard = i_buf = None
        if snd is not None:
            i_shard = len(ins)
            ins.append(shard)
        if buf is not None:
            i_buf = len(ins)
            ins.append(buf)
            aliases[i_buf] = t
            out_shape.append(jax.ShapeDtypeStruct(buf.shape, buf.dtype))
        else:
            assert fwd is None and snd is not None
            out_shape.append(jax.ShapeDtypeStruct((N_DEV,) + shard.shape, shard.dtype))
        where.append((i_shard, i_buf, n_remote, n_local))
        n_remote += (4 if snd not in (None, LOCAL) else 0) + (3 if fwd is not None else 0)
        n_local += 1 if snd is not None else 0

    def plan(rins, routs, send, recv, loc, r0, l0):
        x, y, c = _coords()
        peers = [(x, y, 1 - c), (1 - x, y, c), (x, 1 - y, c), (1 - x, 1 - y, c)]
        remote, local = [], []
        for t, (shard, buf, snd, fwd) in enumerate(items):
            i_shard, i_buf, k, l = where[t]
            k, l = r0 + k, l0 + l
            if snd is not None:
                span = ALL if snd == LOCAL else snd
                src, dst = _rows(rins[i_shard], span), _rows(routs[t].at[_slot((x, y, c))], span)
                local.append(pltpu.make_async_copy(src, dst, loc.at[l]))
                for peer in (peers if snd != LOCAL else []):
                    remote.append(pltpu.make_async_remote_copy(
                        src_ref=src, dst_ref=dst, send_sem=send.at[k], recv_sem=recv.at[k],
                        device_id=peer, device_id_type=MESH))
                    k += 1
            if fwd is not None:
                for px, py, pc in peers[1:]:
                    s = _slot((px, py, pc))
                    remote.append(pltpu.make_async_remote_copy(
                        src_ref=_rows(rins[i_buf].at[s], fwd), dst_ref=_rows(routs[t].at[s], fwd),
                        send_sem=send.at[k], recv_sem=recv.at[k], device_id=peers[0], device_id_type=MESH))
                    k += 1
        return remote, local

    return _Rider(ins, out_shape, n_remote, n_local, plan, aliases)


def _gather_buffer(shard, me):
    return lax.dynamic_update_slice(lax.empty((N_DEV,) + shard.shape, shard.dtype), shard[None], (me, 0, 0))


def _rider_ag_remote(shards, me):
    n = len(shards)

    def plan(ins, outs, send, recv, loc, r0, l0):
        x, y, c = _coords()
        remote = []
        for t in range(n):
            dst = outs[t].at[_slot((x, y, c))]
            for k, peer in enumerate([(x, y, 1 - c), (1 - x, y, c), (x, 1 - y, c), (1 - x, 1 - y, c)]):
                remote.append(pltpu.make_async_remote_copy(
                    src_ref=ins[t], dst_ref=dst, send_sem=send.at[r0 + 4 * t + k], recv_sem=recv.at[r0 + 4 * t + k],
                    device_id=peer, device_id_type=MESH))
        return remote, []

    return _Rider(shards, [jax.ShapeDtypeStruct((N_DEV,) + s.shape, s.dtype) for s in shards], 4 * n, 0, plan,
                  lands=[_gather_buffer(s, me) for s in shards])


def _rider_rs_sibling(grads):
    n = len(grads)

    def plan(ins, outs, send, recv, loc, r0, l0):
        x, y, c = _coords()
        remote = []
        for t in range(n):
            for q in range(4):
                remote.append(pltpu.make_async_remote_copy(
                    src_ref=ins[t].at[q, 1 - c], dst_ref=outs[t].at[q], send_sem=send.at[r0 + 4 * t + q],
                    recv_sem=recv.at[r0 + 4 * t + q], device_id=(x, y, 1 - c), device_id_type=MESH))
        return remote, []

    return _Rider(grads, [jax.ShapeDtypeStruct((4,) + g.shape[2:], g.dtype) for g in grads], 4 * n, 0, plan)


def _rider_rs_chips(sums, rows=None, into=None):
    n = len(sums)
    rows = rows or [ALL] * n

    def plan(ins, outs, send, recv, loc, r0, l0):
        x, y, c = _coords()
        remote = []
        for t in range(n):
            for r, (px, py) in enumerate([(1 - x, y), (x, 1 - y), (1 - x, 1 - y)]):
                remote.append(pltpu.make_async_remote_copy(
                    src_ref=_rows(ins[t].at[2 * px + py], rows[t]), dst_ref=_rows(outs[t].at[r], rows[t]),
                    send_sem=send.at[r0 + 3 * t + r], recv_sem=recv.at[r0 + 3 * t + r],
                    device_id=(px, py, c), device_id_type=MESH))
        return remote, []

    out_shape = [jax.ShapeDtypeStruct((3,) + s.shape[1:], s.dtype) for s in sums]
    if into is None:
        return _Rider(sums, out_shape, 3 * n, 0, plan)
    return _Rider(list(sums) + list(into), out_shape, 3 * n, 0, plan, aliases={n + t: t for t in range(n)})


def _rider_gather_remote(parts):
    n = len(parts)

    def plan(ins, outs, send, recv, loc, r0, l0):
        x, y, c = _coords()
        me = _slot((x, y, c))
        remote = []
        for t in range(n):
            for k in range(1, N_DEV):
                peer = (x ^ ((k >> 2) & 1), y ^ ((k >> 1) & 1), c ^ (k & 1))
                remote.append(pltpu.make_async_remote_copy(
                    src_ref=ins[t], dst_ref=outs[t].at[me], send_sem=send.at[r0 + 7 * t + k - 1],
                    recv_sem=recv.at[r0 + 7 * t + k - 1], device_id=peer, device_id_type=MESH))
        return remote, []

    return _Rider(parts, [jax.ShapeDtypeStruct((N_DEV,) + p.shape, p.dtype) for p in parts], 7 * n, 0, plan)


def _chip_sum_call(idx, grads, recvd, out_dtypes, name):
    n = len(grads)

    def body(i_ref, *refs):
        for t in range(n):
            refs[2 * n + t][0] = (refs[t][0, 0] + refs[n + t][0]).astype(out_dtypes[t])

    def chip(k, s):
        return jnp.where(k >= s[0], k + 1, k)

    in_specs = [pl.BlockSpec((1, 1) + g.shape[2:], lambda k, s: (chip(k, s), s[1], 0, 0)) for g in grads]
    in_specs += [pl.BlockSpec((1,) + r.shape[1:], lambda k, s: (chip(k, s), 0, 0)) for r in recvd]
    return pl.pallas_call(
        body, name=name,
        grid_spec=pltpu.PrefetchScalarGridSpec(
            num_scalar_prefetch=1, grid=(3,), in_specs=in_specs,
            out_specs=[pl.BlockSpec((1,) + r.shape[1:], lambda k, s: (chip(k, s), 0, 0)) for r in recvd]),
        out_shape=[jax.ShapeDtypeStruct(r.shape, dt) for r, dt in zip(recvd, out_dtypes)],
        compiler_params=_params(("arbitrary",)),
    )(idx, *grads, *recvd)


def _final_sum_call(idx, grads, recvd1, recvd2):
    n = len(grads)
    nsteps = 2

    def body(i_ref, *refs):
        for t in range(n):
            g, r1, r2, o = refs[t], refs[n + t], refs[2 * n + t], refs[3 * n + t]
            s = g[0, 0] + r1[0]
            for r in range(3):
                s = s + r2[r].astype(F32)
            o[...] = s

    def rows(a):
        r = a.shape[-2]
        return r // nsteps if (r // nsteps) % 16 == 0 else r

    def step(a):
        return (lambda i: i) if rows(a) != a.shape[-2] else (lambda i: 0)

    in_specs = [pl.BlockSpec((1, 1, rows(g), g.shape[3]), lambda i, s, st=step(g): (s[0], s[1], st(i), 0)) for g in grads]
    in_specs += [pl.BlockSpec((1, rows(r), r.shape[2]), lambda i, s, st=step(r): (s[0], st(i), 0)) for r in recvd1]
    in_specs += [pl.BlockSpec((3, rows(r), r.shape[2]), lambda i, s, st=step(r): (0, st(i), 0)) for r in recvd2]
    return pl.pallas_call(
        body, name="rs_final_sum",
        grid_spec=pltpu.PrefetchScalarGridSpec(
            num_scalar_prefetch=1, grid=(nsteps,), in_specs=in_specs,
            out_specs=[pl.BlockSpec((rows(r), r.shape[2]), lambda i, s, st=step(r): (st(i), 0)) for r in recvd2]),
        out_shape=[jax.ShapeDtypeStruct(r.shape[1:], F32) for r in recvd2],
        compiler_params=_params(("arbitrary",)),
    )(idx, *grads, *recvd1, *recvd2)


def _sum8_call(parts):
    def body(p_ref, o_ref):
        s = p_ref[0]
        for j in range(1, N_DEV):
            s = s + p_ref[j]
        o_ref[...] = s

    return pl.pallas_call(body, name="sum_small_partials",
                          out_shape=jax.ShapeDtypeStruct(parts.shape[1:], parts.dtype))(parts)


def _adamw(w, g, m, v):
    m = ADAM_B1 * m + (1.0 - ADAM_B1) * g
    v = ADAM_B2 * v + (1.0 - ADAM_B2) * (g * g)
    m_hat = m / (1.0 - ADAM_B1 ** ADAM_STEP)
    v_hat = v / (1.0 - ADAM_B2 ** ADAM_STEP)
    delta = -ADAM_LR * (m_hat / (jnp.sqrt(v_hat) + ADAM_EPS) + ADAM_WD * w)
    return delta, m, v


def _adamw_call(ws, gs, ms, vs, nsteps, name):
    n = len(ws)

    def body(*refs):
        for t in range(n):
            w, g, m, v = (refs[k * n + t][...] for k in range(4))
            d, m2, v2 = _adamw(w, g, m, v)
            refs[4 * n + t][...] = d
            refs[5 * n + t][...] = m2
            refs[6 * n + t][...] = v2

    def spec(a):
        assert a.shape[0] % nsteps == 0 and (nsteps == 1 or (a.shape[0] // nsteps) % 8 == 0), a.shape
        return pl.BlockSpec((a.shape[0] // nsteps, a.shape[1]), lambda i: (i, 0))

    specs = [spec(a) for a in ws]
    outs = pl.pallas_call(
        body, name=name, grid=(nsteps,),
        in_specs=specs * 4, out_specs=specs * 3,
        out_shape=[jax.ShapeDtypeStruct(a.shape, F32) for a in ws] * 3,
        compiler_params=_params(("arbitrary",)),
    )(*ws, *gs, *ms, *vs)
    return outs[:n], outs[n:2 * n], outs[2 * n:]


def _adamw_rs_call(idx, after, gws, r1s, r2s, ws, ms, vs, nsteps, name):
    n = len(ws)

    def body(i_ref, after_ref, *refs):
        for t in range(n):
            gw, r1, r2, w, m, v = (refs[k * n + t] for k in range(6))
            g = gw[0, 0] + r1[0]
            for r in range(3):
                g = g + r2[r].astype(F32)
            d, m2, v2 = _adamw(w[...], g, m[...], v[...])
            refs[6 * n + t][...] = g
            refs[7 * n + t][...] = d
            refs[8 * n + t][...] = m2
            refs[9 * n + t][...] = v2

    def rb(a):
        r = a.shape[0] // nsteps
        assert a.shape[0] % nsteps == 0 and r % 16 == 0, a.shape
        return r

    in_specs = [pl.BlockSpec((1, 1, rb(w), w.shape[1]), lambda i, s: (s[0], s[1], i, 0)) for w in ws]
    in_specs += [pl.BlockSpec((1, rb(w), w.shape[1]), lambda i, s: (s[0], i, 0)) for w in ws]
    in_specs += [pl.BlockSpec((3, rb(w), w.shape[1]), lambda i, s: (0, i, 0)) for w in ws]
    plain = [pl.BlockSpec((rb(w), w.shape[1]), lambda i, s: (i, 0)) for w in ws]
    outs = pl.pallas_call(
        body, name=name,
        grid_spec=pltpu.PrefetchScalarGridSpec(
            num_scalar_prefetch=1, grid=(nsteps,),
            in_specs=[pl.BlockSpec(memory_space=pl.ANY)] + in_specs + plain * 3, out_specs=plain * 4),
        out_shape=[jax.ShapeDtypeStruct(w.shape, F32) for w in ws] * 4,
        compiler_params=_params(("arbitrary",)),
    )(idx, after, *gws, *r1s, *r2s, *ws, *ms, *vs)
    return outs[:n], outs[n:2 * n], outs[2 * n:3 * n], outs[3 * n:]


def _rows128(a, pad_rows):
    flat = a.reshape(-1).astype(F32)
    flat = jnp.pad(flat, (0, pad_rows * LANES - flat.shape[0]))
    return flat.reshape(pad_rows, LANES)


_SMALL_A = (("w_pool", 512), ("pool_scale", 8), ("attn_sinks", 8), ("g_mix_post", 8), ("g_mlp_pre", 8),
            ("g_mlp_post", 8), ("loss", 8), ("b_in_gates", 16))
_SMALL_A_ROWS = 640
_SMALL_B = (("g_mix_pre", 8), ("b_in_head", 16))


def _pack(parts, layout, total_rows):
    rows = [_rows128(parts[k], r) for k, r in layout]
    pad = total_rows - sum(r for _, r in layout)
    if pad:
        rows.append(jnp.zeros((pad, LANES), F32))
    return jnp.concatenate(rows, axis=0)


def _unpack(buf, layout, sizes):
    out, off = {}, 0
    for k, r in layout:
        out[k] = buf[off:off + r].reshape(-1)[:sizes[k]]
        off += r
    return out


def kernel(x, g_mix_pre, w_in, b_in, w_pool, pool_scale, attn_sinks, w_branch_pool, w_branch_attn, w_out, g_mix_post, g_mlp_pre, w_up, w_down, g_mlp_post, loss_target, m_g_mix_pre, m_w_in, m_b_in, m_w_pool, m_pool_scale, m_attn_sinks, m_w_branch_pool, m_w_branch_attn, m_w_out, m_g_mix_post, m_g_mlp_pre, m_w_up, m_w_down, m_g_mlp_post, v_g_mix_pre, v_w_in, v_b_in, v_w_pool, v_pool_scale, v_attn_sinks, v_w_branch_pool, v_w_branch_attn, v_w_out, v_g_mix_post, v_g_mlp_pre, v_w_up, v_w_down, v_g_mlp_post):
    B, S, _ = x.shape
    T = B * S
    xt = x.reshape(T, D_MODEL)
    tgt = loss_target.reshape(T, D_MODEL)
    cx, cy, cc = _coords()

    cidx = jnp.stack([2 * cx + cy, cc]).astype(jnp.int32)
    by_chip = lambda gr: gr.reshape((4, 2) + gr.shape[1:])
    bf = lambda w: w[0].astype(MXU_DTYPE)

    me = _slot((cx, cy, cc))
    win_l = w_in[0].T.astype(MXU_DTYPE)
    (c_win,), _ = _copies_start([_rider_ag_first(win_l, me)], "allgather_first")
    wpool_b = bf(w_pool)
    rc, rsa, rsb = _rot_tables(S)
    wbp_l, wba_l, wout_l, wup_l, wdown_l = bf(w_branch_pool), bf(w_branch_attn), bf(w_out), bf(w_up), bf(w_down)
    gathers = [_rider_ag_remote([wbp_l, wba_l, wout_l], me), _rider_ag_remote([wup_l], me), _rider_ag_remote([wdown_l], me)]
    (win_1,) = _copies_wait([c_win], [b for r in gathers for b in r.lands] + [rc, rsa, rsb], "allgather_first_wait")
    win_s = _allgather_relay_call(win_1)
    win_t = win_s.reshape(IN_WIDTH, D_MODEL)

    (c_br, c_up, c_dn), tok = _copies_start(gathers, "allgather_start", after=win_s)
    (h, u, q, k4, v4, g), _ = _inproj_call(xt, g_mix_pre, win_t, b_in, rc, rsa, rsb, S, rider=_after(tok))
    yp = _pool_call(u, wpool_b, pool_scale, S)
    wbp_1, wba_1, wout_1 = _copies_wait([c_br], yp, "allgather_wait_branch")
    (ya,), (wbp_s, wba_s, wout_s) = _attn_call(
        attn_sinks, q, k4, v4, S,
        rider=_rider_ag([(None, wbp_1, None, ALL), (None, wba_1, None, ALL), (None, wout_1, None, ALL)]))
    wout_f = wout_s.reshape(D_MODEL, D_MODEL)
    (wup_1,) = _copies_wait([c_up], ya, "allgather_wait_up")
    (mix, x1, h2), (wup_s,) = _mix_fwd_call(
        yp, ya, g, xt, wbp_s, wba_s, wout_f, g_mix_post, g_mlp_pre, rider=_rider_ag([(None, wup_1, None, ALL)]))
    act = _mlp_up_call(h2, wup_s)
    (wdown_1,) = _copies_wait([c_dn], act, "allgather_wait_down")
    (wdown_s,) = _comm_call(_rider_ag([(None, wdown_1, None, ALL)]), "allgather_pass_down")

    da, dff, dx1, dg3, dg4, lossvec = _mlp_call(x1, act, tgt, wup_s, wdown_s, g_mlp_pre, g_mlp_post)
    gw_down = by_chip(_wgrad_rows_call(act, dff, "wgrad_down")[0])
    (gw_up,), (r1_down,) = _wgrad_cols_call(h2, da, "wgrad_up", rider=_rider_rs_sibling([gw_down]))
    gw_up = by_chip(gw_up)
    (s_down,) = _chip_sum_call(cidx, [gw_down], [r1_down], [MXU_DTYPE], "rs_chip_sum_down")
    (dyp, do, dgates, dg2, dbg, gw_out, gw_bp, gw_ba), (r1_up,) = _mix_bwd_call(
        dx1, mix, yp, ya, g, wbp_s, wba_s, wout_f, g_mix_post, rider=_rider_rs_sibling([gw_up]))
    gw_out = by_chip(gw_out.reshape(N_DEV, D_MODEL // N_DEV, D_MODEL))
    gw_bp, gw_ba = by_chip(gw_bp), by_chip(gw_ba)
    (s_up,) = _chip_sum_call(cidx, [gw_up], [r1_up], [MXU_DTYPE], "rs_chip_sum_up")
    (c_down, c_up), tok = _copies_start([_rider_rs_chips([s_down]), _rider_rs_chips([s_up])], "rs_chips_start_mlp")
    (dq, dk, dv, dsink), (r1_out, r1_bp, r1_ba) = _attn_bwd_call(
        attn_sinks, q, k4, v4, do, rc, rsa, rsb, S, rider=_after(tok, _rider_rs_sibling([gw_out, gw_bp, gw_ba])))
    s_obb = _chip_sum_call(cidx, [gw_out, gw_bp, gw_ba], [r1_out, r1_bp, r1_ba], [MXU_DTYPE] * 3, "rs_chip_sum_branch")
    (c_obb,), tok = _copies_start([_rider_rs_chips(s_obb)], "rs_chips_start_branch")
    (du, dwp, dps), _ = _pool_bwd_call(u, dyp, wpool_b, pool_scale, S, rider=_after(tok))
    (gw_in,) = _wgrad_in_call(du, dq, dk, dv, dgates, h)
    gw_in = by_chip(gw_in)

    small_a = {"w_pool": dwp, "pool_scale": dps,
               "attn_sinks": jnp.sum(dsink.reshape(B, 8, LANES)[:, 0, :N_Q_HEADS], axis=0), "g_mix_post": dg2,
               "g_mlp_pre": dg3, "g_mlp_post": dg4, "loss": lossvec, "b_in_gates": dbg}
    gw_sa = by_chip(_pack(small_a, _SMALL_A, _SMALL_A_ROWS).reshape(N_DEV, _SMALL_A_ROWS // N_DEV, LANES))
    r1_in, r1_sa = _comm_call(_rider_rs_sibling([gw_in, gw_sa]), "rs_sibling_in")
    s_in, s_sa = _chip_sum_call(cidx, [gw_in, gw_sa], [r1_in, r1_sa], [MXU_DTYPE, F32], "rs_chip_sum_in")
    (c_in,), tok = _copies_start([_rider_rs_chips([s_in, s_sa])], "rs_chips_start_in")
    (gx, dg1, dba_in), _ = _inproj_bwd_call(du, dq, dk, dv, dgates, dx1, xt, win_t, g_mix_pre, rider=_after(tok))
    r2_down, r2_up, r2_out, r2_bp, r2_ba, r2_in, r2_sa = _copies_wait([c_down, c_up, c_obb, c_in], dg1, "rs_chips_wait")

    (g_sa,) = _final_sum_call(cidx, [gw_sa], [r1_sa], [r2_sa])
    part_b = _pack({"g_mix_pre": dg1, "b_in_head": dba_in}, _SMALL_B, sum(r for _, r in _SMALL_B))
    (c_small,), tok = _copies_start([_rider_gather_remote([g_sa, part_b])], "allgather_small_start")

    in_t = _adamw_rs_call(cidx, tok, [gw_in], [r1_in], [r2_in], [w_in[0].T], [m_w_in[0].T], [v_w_in[0].T], 2,
                          "adamw_w_in")
    rest = _adamw_rs_call(
        cidx, tok, [gw_bp, gw_ba, gw_out, gw_up, gw_down], [r1_bp, r1_ba, r1_out, r1_up, r1_down],
        [r2_bp, r2_ba, r2_out, r2_up, r2_down], [w_branch_pool[0], w_branch_attn[0], w_out[0], w_up[0], w_down[0]],
        [m_w_branch_pool[0], m_w_branch_attn[0], m_w_out[0], m_w_up[0], m_w_down[0]],
        [v_w_branch_pool[0], v_w_branch_attn[0], v_w_out[0], v_w_up[0], v_w_down[0]], N_DEV, "adamw_shards")
    big_g, big_d, big_m2, big_v2 = ([a[0].T] + list(b) for a, b in zip(in_t, rest))

    sa_all, sb_all = _copies_wait([c_small], rest[0][0], "allgather_small_wait")
    sa_all = lax.dynamic_update_slice(sa_all, g_sa[None], (me, 0, 0))
    sb_sum = _sum8_call(lax.dynamic_update_slice(sb_all, part_b[None], (me, 0, 0)))

    names = ["g_mix_pre", "b_in", "w_pool", "pool_scale", "attn_sinks", "g_mix_post", "g_mlp_pre", "g_mlp_post"]
    sm_w = dict(g_mix_pre=g_mix_pre, b_in=b_in, w_pool=w_pool, pool_scale=pool_scale, attn_sinks=attn_sinks,
                g_mix_post=g_mix_post, g_mlp_pre=g_mlp_pre, g_mlp_post=g_mlp_post)
    sm_m = dict(g_mix_pre=m_g_mix_pre, b_in=m_b_in, w_pool=m_w_pool, pool_scale=m_pool_scale, attn_sinks=m_attn_sinks,
                g_mix_post=m_g_mix_post, g_mlp_pre=m_g_mlp_pre, g_mlp_post=m_g_mlp_post)
    sm_v = dict(g_mix_pre=v_g_mix_pre, b_in=v_b_in, w_pool=v_w_pool, pool_scale=v_pool_scale, attn_sinks=v_attn_sinks,
                g_mix_post=v_g_mix_post, g_mlp_pre=v_g_mlp_pre, g_mlp_post=v_g_mlp_post)
    sizes = {k: sm_w[k].size for k in names}
    sizes.update(loss=D_MODEL, b_in_gates=GATE_WIDTH, b_in_head=C_G)
    sm_g = _unpack(sa_all.reshape(_SMALL_A_ROWS, LANES), _SMALL_A, sizes)
    sm_g.update(_unpack(sb_sum, _SMALL_B, sizes))
    sm_g["b_in"] = jnp.concatenate([sm_g["b_in_head"], sm_g["b_in_gates"]])
    loss = (0.5 / D_MODEL) * jnp.sum(sm_g["loss"])
    two_d = lambda a: a.reshape(-1, a.shape[-1])
    sd_, sm2_, sv2_ = _adamw_call([two_d(sm_w[k]) for k in names], [two_d(sm_g[k].reshape(sm_w[k].shape)) for k in names],
                                  [two_d(sm_m[k]) for k in names], [two_d(sm_v[k]) for k in names], 1, "adamw_small")
    like = lambda vals: {k: a.reshape(sm_w[k].shape) for k, a in zip(names, vals)}
    sm_d, sm_m2, sm_v2 = like(sd_), like(sm2_), like(sv2_)
    sm_gr = {k: sm_g[k].reshape(sm_w[k].shape) for k in names}

    order = ["g_mix_pre", "w_in", "b_in", "w_pool", "pool_scale", "attn_sinks", "w_branch_pool", "w_branch_attn",
             "w_out", "g_mix_post", "g_mlp_pre", "w_up", "w_down", "g_mlp_post"]
    big_names = ["w_in", "w_branch_pool", "w_branch_attn", "w_out", "w_up", "w_down"]
    lead = lambda a: a[None]
    tables = []
    for small_t, big_t in ((sm_gr, big_g), (sm_d, big_d), (sm_m2, big_m2), (sm_v2, big_v2)):
        bt = dict(zip(big_names, big_t))
        tables.append([lead(bt[k]) if k in bt else small_t[k] for k in order])
    return (loss, gx.reshape(B, S, D_MODEL), *tables[0], *tables[1], *tables[2], *tables[3])
```

```python
import jax
import jax.numpy as jnp
from jax import lax
from jax.experimental import pallas as pl
from jax.experimental.pallas import tpu as pltpu

F32 = jnp.float32
MXU_DTYPE = jnp.bfloat16
MESH = pl.DeviceIdType.MESH

D_MODEL = 1024
POOL_WINDOWS = (2, 4, 8, 16)
POOL_WIDTH = 512
POOL_GC = 128
HEAD_DIM = 64
N_Q_HEADS = 8
N_KV_HEADS = 2
GROUP = 4
ATTN_WIDTH = 512
KV_WIDTH = 128
BLOCK = 128
GATE_WIDTH = 2048
IN_WIDTH = 3328
D_FF = 4096
EPS = 1e-6
NEG_INF = -1e30
ROPE_THETA = 500000.0
ROT_DIM = 16
SCALE = HEAD_DIM ** -0.5
C_Q, C_K, C_V, C_G = 512, 1024, 1152, 1280

ADAM_LR = 0.001
ADAM_B1 = 0.9
ADAM_B2 = 0.999
ADAM_EPS = 1e-08
ADAM_WD = 0.01
ADAM_STEP = 10

N_DEV = 8
LANES = 128
VMEM_LIMIT = 56 * 1024 * 1024

NN = (((1,), (0,)), ((), ()))
NT = (((1,), (1,)), ((), ()))
TN = (((0,), (0,)), ((), ()))


def _dot(a, b, dims):
    return lax.dot_general(a, b, dims, preferred_element_type=F32)


def _params(sem=None):
    return pltpu.CompilerParams(dimension_semantics=sem, vmem_limit_bytes=VMEM_LIMIT)


def _tile(n, pref):
    t = min(n, pref)
    assert n % t == 0, (n, t)
    return t


class _Rider:
    def __init__(self, ins, out_shape, n_remote, n_local, plan, aliases=None, lands=None):
        self.ins, self.out_shape, self.n_remote, self.n_local = list(ins), list(out_shape), n_remote, n_local
        self.plan, self.aliases = plan, dict(aliases or {})
        self.lands = lands


def _after(token, rider=None):
    r = rider or _Rider([], [], 0, 0, lambda ins, outs, send, recv, loc, r0, l0: ([], []))
    return _Rider(r.ins + [token], r.out_shape, r.n_remote, r.n_local, r.plan, r.aliases)


def _launch(body, args, *, name, grid, in_specs, out_specs, out_shape, scratch_shapes=(), sem=None, rider=None):
    if rider is None:
        return pl.pallas_call(body, name=name, grid=grid, in_specs=in_specs, out_specs=out_specs, out_shape=out_shape,
                              scratch_shapes=list(scratch_shapes), compiler_params=_params(sem))(*args)
    n_in, n_out, n_scr = len(args), len(out_shape), len(scratch_shapes)
    r_in, r_out = len(rider.ins), len(rider.out_shape)
    copies = rider.n_remote + rider.n_local > 0

    def wrapped(*refs):
        ins, rins = refs[:n_in], refs[n_in:n_in + r_in]
        o0 = n_in + r_in
        outs, routs = refs[o0:o0 + n_out], refs[o0 + n_out:o0 + n_out + r_out]
        s0 = o0 + n_out + r_out
        scr = refs[s0:s0 + n_scr]
        if not copies:
            return body(*ins, *outs, *scr)
        send, recv, loc = refs[s0 + n_scr:]
        first, last = None, None
        for d in range(len(grid)):
            f, l = pl.program_id(d) == 0, pl.program_id(d) == pl.num_programs(d) - 1
            first = f if first is None else first & f
            last = l if last is None else last & l

        def start():
            remote, local = rider.plan(rins, routs, send, recv, loc, 0, 0)
            for cp in local + remote:
                cp.start()

        def finish():
            remote, local = rider.plan(rins, routs, send, recv, loc, 0, 0)
            for cp in remote + local:
                cp.wait()

        if first is None:
            start()
            body(*ins, *outs, *scr)
            finish()
        else:
            pl.when(first)(start)
            body(*ins, *outs, *scr)
            pl.when(last)(finish)

    hbm = pl.BlockSpec(memory_space=pl.ANY)
    dma = pltpu.SemaphoreType.DMA
    res = pl.pallas_call(
        wrapped, name=name, grid=grid, in_specs=list(in_specs) + [hbm] * r_in,
        out_specs=list(out_specs) + [hbm] * r_out, out_shape=list(out_shape) + rider.out_shape,
        scratch_shapes=list(scratch_shapes) + (
            [dma((max(rider.n_remote, 1),)), dma((max(rider.n_remote, 1),)), dma((max(rider.n_local, 1),))] if copies else []),
        input_output_aliases={n_in + i: n_out + o for i, o in rider.aliases.items()},
        compiler_params=_params(sem),
    )(*args, *rider.ins)
    return list(res[:n_out]), list(res[n_out:])


def _comm_call(rider, name):
    return _launch(lambda: None, [], name=name, grid=(), in_specs=[], out_specs=[], out_shape=[], rider=rider)[1]


_HBM = pl.BlockSpec(memory_space=pltpu.HBM)
_SEM = pl.BlockSpec(memory_space=pltpu.SEMAPHORE)
_EFFECT = pltpu.SideEffectType.DATAFLOW_SIDE_EFFECTING


def _copies_start(riders, name, after=None):
    assert all(r.n_local == 0 and not r.aliases for r in riders)
    extra = [] if after is None else [after]
    sizes = [(len(r.ins), len(r.out_shape)) for r in riders]
    bufs = []
    for r in riders:
        lands = r.lands or [lax.empty(s.shape, s.dtype) for s in r.out_shape]
        bufs += [pltpu.with_memory_space_constraint(a, pltpu.HBM) for a in list(r.ins) + list(lands)]
    nb, ng, ne = len(bufs), len(riders), len(extra)

    def body(*refs):
        sems, token, at = refs[2 * nb + ne:2 * nb + ne + 2 * ng], refs[-1], 0
        for g, (r, (ni, no)) in enumerate(zip(riders, sizes)):
            remote, _ = r.plan(refs[at:at + ni], refs[at + ni:at + ni + no], sems[2 * g], sems[2 * g + 1], None, 0, 0)
            for cp in remote:
                cp.start()
            at += ni + no
        token[...] = jnp.zeros_like(token)

    res = pl.pallas_call(
        body, name=name, in_specs=[_HBM] * nb + [pl.BlockSpec(memory_space=pl.ANY)] * ne,
        out_specs=[_HBM] * nb + [_SEM] * (2 * ng) + [pl.BlockSpec(memory_space=pltpu.VMEM)],
        out_shape=[pltpu.HBM(a.shape, a.dtype) for a in bufs]
        + [pltpu.SemaphoreType.DMA((r.n_remote,)) for r in riders for _ in range(2)]
        + [jax.ShapeDtypeStruct((8, LANES), F32)],
        input_output_aliases={i: i for i in range(nb)},
        compiler_params=pltpu.CompilerParams(has_side_effects=_EFFECT),
    )(*bufs, *extra)
    handles, at = [], 0
    for g, (r, (ni, no)) in enumerate(zip(riders, sizes)):
        handles.append((r, list(res[at:at + ni + no]), res[nb + 2 * g], res[nb + 2 * g + 1]))
        at += ni + no
    return handles, res[-1]


def _copies_wait(handles, after, name):
    bufs = [b for _, bs, _, _ in handles for b in bs]
    sems = [s for _, _, send, recv in handles for s in (send, recv)]
    nb, ng = len(bufs), len(handles)
    after = list(after) if isinstance(after, (list, tuple)) else [after]

    def body(*refs):
        at = 0
        for g, (rider, bs, _, _) in enumerate(handles):
            ni = len(rider.ins)
            remote, _ = rider.plan(refs[at:at + ni], refs[at + ni:at + len(bs)], refs[nb + 2 * g], refs[nb + 2 * g + 1],
                                   None, 0, 0)
            for cp in remote:
                cp.wait_send()
                cp.wait_recv()
            at += len(bs)

    res = pl.pallas_call(
        body, name=name, in_specs=[_HBM] * nb + [_SEM] * (2 * ng) + [pl.BlockSpec(memory_space=pl.ANY)] * len(after),
        out_specs=[_HBM] * nb, out_shape=[pltpu.HBM(a.shape, a.dtype) for a in bufs],
        input_output_aliases={i: i for i in range(nb)},
        compiler_params=pltpu.CompilerParams(has_side_effects=_EFFECT),
    )(*bufs, *sems, *after)
    lands, at = [], 0
    for rider, bs, _, _ in handles:
        lands += list(res[at + len(rider.ins):at + len(bs)])
        at += len(bs)
    return lands


def _copies_pass(handles, riders, after, name):
    bufs = [b for _, bs, _, _ in handles for b in bs]
    sems = [s for _, _, send, recv in handles for s in (send, recv)]
    nb, ng = len(bufs), len(handles)
    after = list(after) if isinstance(after, (list, tuple)) else [after]

    def body(*refs):
        new_sems, at = refs[2 * nb + 2 * ng + len(after):], 0
        for g, ((rider, bs, _, _), then) in enumerate(zip(handles, riders)):
            ins, outs = refs[at:at + len(rider.ins)], refs[at + len(rider.ins):at + len(bs)]
            for cp in rider.plan(ins, outs, refs[nb + 2 * g], refs[nb + 2 * g + 1], None, 0, 0)[0]:
                cp.wait_send()
                cp.wait_recv()
            for cp in then.plan(ins, outs, new_sems[2 * g], new_sems[2 * g + 1], None, 0, 0)[0]:
                cp.start()
            at += len(bs)

    res = pl.pallas_call(
        body, name=name, in_specs=[_HBM] * nb + [_SEM] * (2 * ng) + [pl.BlockSpec(memory_space=pl.ANY)] * len(after),
        out_specs=[_HBM] * nb + [_SEM] * (2 * ng),
        out_shape=[pltpu.HBM(a.shape, a.dtype) for a in bufs]
        + [pltpu.SemaphoreType.DMA((r.n_remote,)) for r in riders for _ in range(2)],
        input_output_aliases={i: i for i in range(nb)},
        compiler_params=pltpu.CompilerParams(has_side_effects=_EFFECT),
    )(*bufs, *sems, *after)
    new, at = [], 0
    for g, ((_, bs, _, _), then) in enumerate(zip(handles, riders)):
        new.append((then, list(res[at:at + len(bs)]), res[nb + 2 * g], res[nb + 2 * g + 1]))
        at += len(bs)
    return new


def _rms_r(x):
    return lax.rsqrt(jnp.mean(x * x, axis=-1, keepdims=True) + EPS)


def _rms_bwd(dn, x, r, g):
    xh = x * r
    dxh = dn * g
    dx = r * (dxh - xh * jnp.mean(dxh * xh, axis=-1, keepdims=True))
    return dx, dn * xh


def _rot(t, c, sa, sb):
    outs = []
    for j in range(t.shape[1] // LANES):
        tj = t[:, LANES * j:LANES * (j + 1)]
        outs.append(tj * c + pltpu.roll(tj, LANES - 8, 1) * sa + pltpu.roll(tj, 8, 1) * sb)
    return outs[0] if len(outs) == 1 else jnp.concatenate(outs, axis=1)


def _rot_tables(S):
    pos = jnp.arange(S, dtype=F32)
    inv_freq = ROPE_THETA ** (-jnp.arange(0, ROT_DIM, 2, dtype=F32) / ROT_DIM)
    ang = pos[:, None] * inv_freq[None, :]
    cos, sin = jnp.cos(ang), jnp.sin(ang)
    one = jnp.ones((S, HEAD_DIM - ROT_DIM), F32)
    zero = jnp.zeros((S, HEAD_DIM - ROT_DIM), F32)
    z8 = jnp.zeros((S, 8), F32)
    c = jnp.concatenate([cos, cos, one], axis=1)
    sa = jnp.concatenate([-sin, z8, zero], axis=1)
    sb = jnp.concatenate([z8, sin, zero], axis=1)
    rep = LANES // HEAD_DIM
    return jnp.tile(c, (1, rep)), jnp.tile(sa, (1, rep)), jnp.tile(sb, (1, rep))


def _lane_tile4(k):
    lane = lax.broadcasted_iota(jnp.int32, k.shape, 1)
    rk = pltpu.roll(k, HEAD_DIM, 1)
    x0 = jnp.where(lane < HEAD_DIM, k, rk)
    x1 = jnp.where(lane < HEAD_DIM, rk, k)
    return jnp.concatenate([x0, x0, x1, x1], axis=1)


def _fold_heads(acc):
    zs = []
    for hk in range(N_KV_HEADS):
        a = acc[:, 256 * hk:256 * hk + LANES] + acc[:, 256 * hk + LANES:256 * (hk + 1)]
        zs.append(a + pltpu.roll(a, HEAD_DIM, 1))
    lane = lax.broadcasted_iota(jnp.int32, zs[0].shape, 1)
    return jnp.where(lane < HEAD_DIM, zs[0], zs[1])


def _inproj_call(x, g1, win_t, b_in, rc, rsa, rsb, S, rider=None):
    T = x.shape[0]
    tm = _tile(S, 512)
    nst = S // tm

    def body(x_ref, g1_ref, w_ref, b_ref, c_ref, sa_ref, sb_ref,
             h_ref, u_ref, q_ref, k4_ref, v4_ref, g_ref):
        xv = x_ref[...]
        hb = ((xv * _rms_r(xv)) * g1_ref[...]).astype(MXU_DTYPE)
        h_ref[...] = hb

        def proj(lo, hi):
            return _dot(hb, w_ref[lo:hi, :], NT) + b_ref[:, lo:hi]

        c, sa, sb = c_ref[...], sa_ref[...], sb_ref[...]
        u_ref[...] = proj(0, C_Q)
        q_ref[...] = (_rot(proj(C_Q, C_K), c, sa, sb) * SCALE).astype(MXU_DTYPE)
        kv = proj(C_K, C_G)
        k4_ref[...] = _lane_tile4(_rot(kv[:, :KV_WIDTH], c, sa, sb)).astype(MXU_DTYPE)
        v4_ref[...] = _lane_tile4(kv[:, KV_WIDTH:]).astype(MXU_DTYPE)
        g_ref[...] = jax.nn.sigmoid(proj(C_G, IN_WIDTH)).astype(MXU_DTYPE)

    tok = lambda w: pl.BlockSpec((tm, w), lambda i: (i, 0))
    full = lambda a: pl.BlockSpec(a.shape, lambda i: (0,) * a.ndim)
    tab = pl.BlockSpec((tm, LANES), lambda i: (i % nst, 0))
    return _launch(
        body, [x, g1, win_t, b_in, rc, rsa, rsb], name="inproj_fwd", grid=(T // tm,),
        in_specs=[tok(D_MODEL), full(g1), full(win_t), full(b_in), tab, tab, tab],
        out_specs=[tok(D_MODEL), tok(POOL_WIDTH), tok(ATTN_WIDTH), tok(512), tok(512), tok(GATE_WIDTH)],
        out_shape=[jax.ShapeDtypeStruct((T, D_MODEL), MXU_DTYPE), jax.ShapeDtypeStruct((T, POOL_WIDTH), F32),
                   jax.ShapeDtypeStruct((T, ATTN_WIDTH), MXU_DTYPE), jax.ShapeDtypeStruct((T, 512), MXU_DTYPE),
                   jax.ShapeDtypeStruct((T, 512), MXU_DTYPE), jax.ShapeDtypeStruct((T, GATE_WIDTH), MXU_DTYPE)],
        sem=("arbitrary",), rider=rider)


def _shift_rows(a, k, rows):
    n = a.shape[0]
    if k > 0:
        return jnp.where(rows >= k, pltpu.roll(a, k, 0), 0.0)
    return jnp.where(rows < n + k, pltpu.roll(a, n + k, 0), 0.0)


def _win_sum(a, w, rows, sign):
    s, k = a, 1
    while k < w:
        s = s + _shift_rows(s, sign * k, rows)
        k *= 2
    return s


def _pool_diff(ug, w, rows):
    inv = 1.0 / jnp.minimum(rows + 1, w).astype(F32)
    return _win_sum(ug, w, rows, 1) * inv - ug, inv


def _pool_call(u, w_pool, pool_scale, S):
    T = u.shape[0]

    def body(u_ref, w_ref, ps_ref, y_ref):
        rows = lax.broadcasted_iota(jnp.int32, (S, POOL_GC), 0)
        for gi, w in enumerate(POOL_WINDOWS):
            sl = slice(POOL_GC * gi, POOL_GC * (gi + 1))
            diff, _ = _pool_diff(u_ref[:, sl], w, rows)
            mixed = _dot(diff.astype(MXU_DTYPE), w_ref[gi], NN)
            y_ref[:, sl] = (mixed * ps_ref[:, sl]).astype(MXU_DTYPE)

    seq = pl.BlockSpec((S, POOL_WIDTH), lambda b: (b, 0))
    return pl.pallas_call(
        body, name="pool_fwd", grid=(T // S,),
        in_specs=[seq, pl.BlockSpec(w_pool.shape, lambda b: (0, 0, 0)), pl.BlockSpec(pool_scale.shape, lambda b: (0, 0))],
        out_specs=seq, out_shape=jax.ShapeDtypeStruct((T, POOL_WIDTH), MXU_DTYPE),
        compiler_params=_params(("arbitrary",)),
    )(u, w_pool, pool_scale)


def _pool_bwd_call(u, dyp, w_pool, pool_scale, S, rider=None):
    T = u.shape[0]

    def body(u_ref, dy_ref, w_ref, ps_ref, du_ref, dw_ref, dps_ref):
        @pl.when(pl.program_id(0) == 0)
        def _():
            dw_ref[...] = jnp.zeros_like(dw_ref)
            dps_ref[...] = jnp.zeros_like(dps_ref)

        rows = lax.broadcasted_iota(jnp.int32, (S, POOL_GC), 0)
        for gi, w in enumerate(POOL_WINDOWS):
            sl = slice(POOL_GC * gi, POOL_GC * (gi + 1))
            diff, inv = _pool_diff(u_ref[:, sl], w, rows)
            diffb = diff.astype(MXU_DTYPE)
            wg = w_ref[gi]
            mixed = _dot(diffb, wg, NN)
            dy = dy_ref[:, sl]
            dps_ref[:, sl] += jnp.sum(dy * mixed, axis=0, keepdims=True)
            dmb = (dy * ps_ref[:, sl]).astype(MXU_DTYPE)
            dw_ref[gi] += _dot(diffb, dmb, TN)
            ddiff = _dot(dmb, wg, NT)
            du_ref[:, sl] = (_win_sum(ddiff * inv, w, rows, -1) - ddiff).astype(MXU_DTYPE)

    seq = pl.BlockSpec((S, POOL_WIDTH), lambda b: (b, 0))
    return _launch(
        body, [u, dyp, w_pool, pool_scale], name="pool_bwd", grid=(T // S,),
        in_specs=[seq, seq, pl.BlockSpec(w_pool.shape, lambda b: (0, 0, 0)), pl.BlockSpec(pool_scale.shape, lambda b: (0, 0))],
        out_specs=[seq, pl.BlockSpec(w_pool.shape, lambda b: (0, 0, 0)), pl.BlockSpec(pool_scale.shape, lambda b: (0, 0))],
        out_shape=[jax.ShapeDtypeStruct((T, POOL_WIDTH), MXU_DTYPE), jax.ShapeDtypeStruct(w_pool.shape, F32),
                   jax.ShapeDtypeStruct(pool_scale.shape, F32)],
        sem=("arbitrary",), rider=rider)


def _attn_consts():
    lane_g = lax.broadcasted_iota(jnp.int32, (BLOCK, 256), 1) >> 6
    rgrp = lax.broadcasted_iota(jnp.int32, (GROUP * BLOCK, 1), 0) >> 7
    rel = lax.broadcasted_iota(jnp.int32, (BLOCK, 256), 0) - lax.broadcasted_iota(jnp.int32, (BLOCK, 256), 1)

    def bias(off):
        ok = (rel + off >= 0) & (rel + off < BLOCK)
        return jnp.concatenate([jnp.where(ok, 0.0, NEG_INF)] * GROUP, axis=0)

    return lane_g, rgrp, bias(0), bias(BLOCK)


def _sink_rows(sink_ref, hk, rgrp):
    sv = jnp.zeros(rgrp.shape, F32)
    for g in range(GROUP):
        sv = jnp.where(rgrp == g, sink_ref[0, GROUP * hk + g], sv)
    return sv


def _stack_heads(xb, lane_g):
    return jnp.concatenate([jnp.where(lane_g == g, xb, jnp.zeros_like(xb)) for g in range(GROUP)], axis=0)


def _unstack_heads(xs, lane_g):
    out = jnp.where(lane_g == 0, xs[0:BLOCK], 0.0)
    for g in range(1, GROUP):
        out = out + jnp.where(lane_g == g, xs[BLOCK * g:BLOCK * (g + 1)], 0.0)
    return out


def _attn_probs(qs, kb, bias, sv):
    s = _dot(qs, kb, NT) + bias
    m = jnp.maximum(jnp.max(s, axis=1, keepdims=True), sv)
    e = jnp.exp(s - m)
    es = jnp.exp(sv - m)
    inv_l = 1.0 / (jnp.sum(e, axis=1, keepdims=True) + es)
    return e * inv_l, es * inv_l


def _attn_blocks(nb, blk, carry, per=1):
    carry = blk(0, 0, True, carry)
    per = per if (nb - 1) % per == 0 else 1

    def step(i, c):
        for k in range(per):
            n = 1 + per * i + k
            c = blk(pl.multiple_of(n * BLOCK, BLOCK), pl.multiple_of((n - 1) * BLOCK, BLOCK), False, c)
        return c

    return lax.fori_loop(0, (nb - 1) // per, step, carry)


def _attn_call(sinks, q, k4, v4, S, rider=None):
    T = q.shape[0]
    nb = S // BLOCK

    def body(sink_ref, q_ref, k_ref, v_ref, o_ref):
        lane_g, rgrp, bias_first, bias_later = _attn_consts()
        svs = [_sink_rows(sink_ref, hk, rgrp) for hk in range(N_KV_HEADS)]

        def blk(q0, k0, first, carry):
            for hk in range(N_KV_HEADS):
                cs = slice(256 * hk, 256 * (hk + 1))
                qs = _stack_heads(q_ref[pl.ds(q0, BLOCK), cs], lane_g)
                p, _ = _attn_probs(qs, k_ref[pl.ds(k0, 2 * BLOCK), cs], bias_first if first else bias_later, svs[hk])
                o = _dot(p.astype(MXU_DTYPE), v_ref[pl.ds(k0, 2 * BLOCK), cs], NN)
                o_ref[pl.ds(q0, BLOCK), cs] = _unstack_heads(o, lane_g).astype(MXU_DTYPE)
            return carry

        _attn_blocks(nb, blk, 0, per=3)

    seq = pl.BlockSpec((S, ATTN_WIDTH), lambda b: (b, 0))
    return _launch(
        body, [sinks, q, k4, v4], name="attn_fwd", grid=(T // S,),
        in_specs=[pl.BlockSpec(memory_space=pltpu.SMEM), seq, seq, seq],
        out_specs=[seq], out_shape=[jax.ShapeDtypeStruct((T, ATTN_WIDTH), MXU_DTYPE)],
        sem=("arbitrary",), rider=rider)


def _attn_bwd_call(sinks, q, k4, v4, do, rc, rsa, rsb, S, rider=None):
    T = q.shape[0]
    nb = S // BLOCK

    def body(sink_ref, q_ref, k_ref, v_ref, do_ref, c_ref, sa_ref, sb_ref,
             dq_ref, dk_ref, dv_ref, ds_ref, dk_acc, dv_acc):
        lane_g, rgrp, bias_first, bias_later = _attn_consts()
        svs = [_sink_rows(sink_ref, hk, rgrp) for hk in range(N_KV_HEADS)]
        lane1 = lax.broadcasted_iota(jnp.int32, (1, LANES), 1)
        dk_acc[...] = jnp.zeros_like(dk_acc)
        dv_acc[...] = jnp.zeros_like(dv_acc)

        def blk(q0, k0, first, dsink):
            rows = pl.ds(q0, BLOCK)
            c, sa, sb = c_ref[rows, :], sa_ref[rows, :], sb_ref[rows, :]
            for hk in range(N_KV_HEADS):
                cs = slice(256 * hk, 256 * (hk + 1))
                qs = _stack_heads(q_ref[rows, cs], lane_g)
                dos = _stack_heads(do_ref[rows, cs], lane_g)
                kb = k_ref[pl.ds(k0, 2 * BLOCK), cs]
                vb = v_ref[pl.ds(k0, 2 * BLOCK), cs]
                p, ps = _attn_probs(qs, kb, bias_first if first else bias_later, svs[hk])
                dp = _dot(dos, vb, NT)
                delta = jnp.sum(p * dp, axis=1, keepdims=True)
                dsb = (p * (dp - delta)).astype(MXU_DTYPE)
                dqb = _unstack_heads(_dot(dsb, kb, NN), lane_g) * SCALE
                dq_ref[rows, cs] = _rot(dqb, c, -sa, -sb).astype(MXU_DTYPE)
                dk_acc[pl.ds(k0, 2 * BLOCK), cs] += _dot(dsb, qs, TN)
                dv_acc[pl.ds(k0, 2 * BLOCK), cs] += _dot(p.astype(MXU_DTYPE), dos, TN)
                psd = ps * delta
                for g in range(GROUP):
                    val = -jnp.sum(psd[BLOCK * g:BLOCK * (g + 1)], axis=0, keepdims=True)
                    dsink = dsink + jnp.where(lane1 == GROUP * hk + g, val, 0.0)
            return dsink

        dsink = _attn_blocks(nb, blk, jnp.zeros((1, LANES), F32))
        dk_ref[...] = _rot(_fold_heads(dk_acc[...]), c_ref[...], -sa_ref[...], -sb_ref[...]).astype(MXU_DTYPE)
        dv_ref[...] = _fold_heads(dv_acc[...]).astype(MXU_DTYPE)
        ds_ref[...] = jnp.broadcast_to(dsink, ds_ref.shape)

    seq = pl.BlockSpec((S, ATTN_WIDTH), lambda b: (b, 0))
    kvs = pl.BlockSpec((S, KV_WIDTH), lambda b: (b, 0))
    tab = pl.BlockSpec((S, LANES), lambda b: (0, 0))
    nseq = T // S
    return _launch(
        body, [sinks, q, k4, v4, do, rc, rsa, rsb], name="attn_bwd", grid=(nseq,),
        in_specs=[pl.BlockSpec(memory_space=pltpu.SMEM), seq, seq, seq, seq, tab, tab, tab],
        out_specs=[seq, kvs, kvs, pl.BlockSpec((8, LANES), lambda b: (b, 0))],
        out_shape=[jax.ShapeDtypeStruct((T, ATTN_WIDTH), MXU_DTYPE), jax.ShapeDtypeStruct((T, KV_WIDTH), MXU_DTYPE),
                   jax.ShapeDtypeStruct((T, KV_WIDTH), MXU_DTYPE), jax.ShapeDtypeStruct((8 * nseq, LANES), F32)],
        scratch_shapes=[pltpu.VMEM((S, 512), F32), pltpu.VMEM((S, 512), F32)],
        sem=("arbitrary",), rider=rider)


def _branch_weights(wbp_ref, wba_ref, wbp_s, wba_s):
    @pl.when(pl.program_id(0) == 0)
    def _():
        for j in range(N_DEV):
            wbp_s[:, LANES * j:LANES * (j + 1)] = wbp_ref[j]
            wba_s[:, LANES * j:LANES * (j + 1)] = wba_ref[j]


def _mix_fwd_call(yp, ya, g, x, wbp, wba, wout, g2, g3, rider=None):
    T = x.shape[0]
    tm = _tile(T, 512)

    def body(yp_ref, ya_ref, g_ref, x_ref, wbp_ref, wba_ref, wout_ref, g2_ref, g3_ref,
             mix_ref, x1_ref, h2_ref, wbp_s, wba_s):
        _branch_weights(wbp_ref, wba_ref, wbp_s, wba_s)
        bp = _dot(yp_ref[...], wbp_s[...], NN)
        ba = _dot(ya_ref[...], wba_s[...], NN)
        merged = g_ref[:, :D_MODEL].astype(F32) * bp + g_ref[:, D_MODEL:].astype(F32) * ba
        mix = _dot(merged.astype(MXU_DTYPE), wout_ref[...], NN)
        mix_ref[...] = mix
        x1 = x_ref[...] + (mix * _rms_r(mix)) * g2_ref[...]
        x1_ref[...] = x1
        h2_ref[...] = ((x1 * _rms_r(x1)) * g3_ref[...]).astype(MXU_DTYPE)

    tok = lambda w: pl.BlockSpec((tm, w), lambda i: (i, 0))
    full = lambda a: pl.BlockSpec(a.shape, lambda i: (0,) * a.ndim)
    return _launch(
        body, [yp, ya, g, x, wbp, wba, wout, g2, g3], name="mix_fwd", grid=(T // tm,),
        in_specs=[tok(POOL_WIDTH), tok(ATTN_WIDTH), tok(GATE_WIDTH), tok(D_MODEL), full(wbp), full(wba), full(wout),
                  full(g2), full(g3)],
        out_specs=[tok(D_MODEL), tok(D_MODEL), tok(D_MODEL)],
        out_shape=[jax.ShapeDtypeStruct((T, D_MODEL), F32), jax.ShapeDtypeStruct((T, D_MODEL), F32),
                   jax.ShapeDtypeStruct((T, D_MODEL), MXU_DTYPE)],
        scratch_shapes=[pltpu.VMEM((POOL_WIDTH, D_MODEL), MXU_DTYPE), pltpu.VMEM((ATTN_WIDTH, D_MODEL), MXU_DTYPE)],
        sem=("arbitrary",), rider=rider)


def _mix_bwd_call(dx1, mix, yp, ya, g, wbp, wba, wout, g2, rider=None):
    T = dx1.shape[0]
    tm = _tile(T, 512)

    def body(dx1_ref, mix_ref, yp_ref, ya_ref, g_ref, wbp_ref, wba_ref, wout_ref, g2_ref,
             dyp_ref, do_ref, dgates_ref, dg2_ref, dbg_ref, gout_ref, gbp_ref, gba_ref,
             wbp_s, wba_s, acc_out, acc_bp, acc_ba, sem):
        _branch_weights(wbp_ref, wba_ref, wbp_s, wba_s)
        step = pl.program_id(0)

        @pl.when(step == 0)
        def _():
            dg2_ref[...] = jnp.zeros_like(dg2_ref)
            dbg_ref[...] = jnp.zeros_like(dbg_ref)
            acc_out[...] = jnp.zeros_like(acc_out)
            acc_bp[...] = jnp.zeros_like(acc_bp)
            acc_ba[...] = jnp.zeros_like(acc_ba)

        mix = mix_ref[...]
        dmix, dg2 = _rms_bwd(dx1_ref[...], mix, _rms_r(mix), g2_ref[...])
        dg2_ref[...] += jnp.sum(dg2, axis=0, keepdims=True)
        dmixb = dmix.astype(MXU_DTYPE)
        dmerged = _dot(dmixb, wout_ref[...], NT)
        yp, ya = yp_ref[...], ya_ref[...]
        bp = _dot(yp, wbp_s[...], NN)
        ba = _dot(ya, wba_s[...], NN)
        gp, ga = g_ref[:, :D_MODEL].astype(F32), g_ref[:, D_MODEL:].astype(F32)
        acc_out[...] += _dot((gp * bp + ga * ba).astype(MXU_DTYPE), dmixb, TN)
        dgp = dmerged * bp * (gp * (1.0 - gp))
        dga = dmerged * ba * (ga * (1.0 - ga))
        dbg_ref[:, :D_MODEL] += jnp.sum(dgp, axis=0, keepdims=True)
        dbg_ref[:, D_MODEL:] += jnp.sum(dga, axis=0, keepdims=True)
        dgates_ref[:, :D_MODEL] = dgp.astype(MXU_DTYPE)
        dgates_ref[:, D_MODEL:] = dga.astype(MXU_DTYPE)
        dbp = (dmerged * gp).astype(MXU_DTYPE)
        dba = (dmerged * ga).astype(MXU_DTYPE)
        acc_bp[...] += _dot(yp, dbp, TN)
        acc_ba[...] += _dot(ya, dba, TN)
        dyp_ref[...] = _dot(dbp, wbp_s[...], NT)
        do_ref[...] = _dot(dba, wba_s[...], NT).astype(MXU_DTYPE)

        @pl.when(step == pl.num_programs(0) - 1)
        def _():
            copies = [pltpu.make_async_copy(acc_out, gout_ref, sem.at[0])]
            for j in range(N_DEV):
                cols = slice(LANES * j, LANES * (j + 1))
                copies.append(pltpu.make_async_copy(acc_bp.at[:, cols], gbp_ref.at[j], sem.at[1 + j]))
                copies.append(pltpu.make_async_copy(acc_ba.at[:, cols], gba_ref.at[j], sem.at[1 + N_DEV + j]))
            for cp in copies:
                cp.start()
            for cp in copies:
                cp.wait()

    tok = lambda w: pl.BlockSpec((tm, w), lambda i: (i, 0))
    full = lambda a: pl.BlockSpec(a.shape, lambda i: (0,) * a.ndim)
    acc = lambda w: pl.BlockSpec((1, w), lambda i: (0, 0))
    hbm = pl.BlockSpec(memory_space=pl.ANY)
    sd = jax.ShapeDtypeStruct
    return _launch(
        body, [dx1, mix, yp, ya, g, wbp, wba, wout, g2], name="mix_bwd", grid=(T // tm,),
        in_specs=[tok(D_MODEL), tok(D_MODEL), tok(POOL_WIDTH), tok(ATTN_WIDTH), tok(GATE_WIDTH), full(wbp), full(wba),
                  full(wout), full(g2)],
        out_specs=[tok(POOL_WIDTH), tok(ATTN_WIDTH), tok(GATE_WIDTH), acc(D_MODEL), acc(GATE_WIDTH), hbm, hbm, hbm],
        out_shape=[sd((T, POOL_WIDTH), F32), sd((T, ATTN_WIDTH), MXU_DTYPE), sd((T, GATE_WIDTH), MXU_DTYPE),
                   sd((1, D_MODEL), F32), sd((1, GATE_WIDTH), F32), sd((D_MODEL, D_MODEL), F32),
                   sd((N_DEV, POOL_WIDTH, LANES), F32), sd((N_DEV, ATTN_WIDTH, LANES), F32)],
        scratch_shapes=[pltpu.VMEM((POOL_WIDTH, D_MODEL), MXU_DTYPE), pltpu.VMEM((ATTN_WIDTH, D_MODEL), MXU_DTYPE),
                        pltpu.VMEM((D_MODEL, D_MODEL), F32), pltpu.VMEM((POOL_WIDTH, D_MODEL), F32),
                        pltpu.VMEM((ATTN_WIDTH, D_MODEL), F32), pltpu.SemaphoreType.DMA((1 + 2 * N_DEV,))],
        sem=("arbitrary",), rider=rider)


def _mlp_up_call(h2, wup):
    T = h2.shape[0]
    tm = _tile(T, 512)
    fc = D_FF // N_DEV

    def body(h2_ref, wup_ref, act_ref):
        h2 = h2_ref[...]
        for j in range(N_DEV):
            rl = jnp.maximum(_dot(h2, wup_ref[j], NN), 0.0)
            act_ref[:, fc * j:fc * (j + 1)] = (rl * rl).astype(MXU_DTYPE)

    sd = jax.ShapeDtypeStruct
    return pl.pallas_call(
        body, name="mlp_up", grid=(T // tm,),
        in_specs=[pl.BlockSpec((tm, D_MODEL), lambda i: (i, 0)),
                  pl.BlockSpec(wup.shape, lambda i: (0, 0, 0), pipeline_mode=pl.Buffered(1))],
        out_specs=pl.BlockSpec((tm, D_FF), lambda i: (i, 0)), out_shape=sd((T, D_FF), MXU_DTYPE),
        compiler_params=_params(("arbitrary",)),
    )(h2, wup)


def _mlp_call(x1, act, target, wup, wdown, g3, g4):
    T = x1.shape[0]
    tm = _tile(T, 256)
    fc = D_FF // N_DEV

    def body(x1_ref, act_ref, t_ref, wup_ref, wdown_ref, g3_ref, g4_ref,
             da_ref, dff_ref, dx1_ref, dg3_ref, dg4_ref, loss_ref):
        @pl.when(pl.program_id(0) == 0)
        def _():
            dg3_ref[...] = jnp.zeros_like(dg3_ref)
            dg4_ref[...] = jnp.zeros_like(dg4_ref)
            loss_ref[...] = jnp.zeros_like(loss_ref)

        ff = jnp.zeros((tm, D_MODEL), F32)
        for j in range(N_DEV):
            ff = ff + _dot(act_ref[:, fc * j:fc * (j + 1)], wdown_ref[j], NN)
        x1 = x1_ref[...]
        r4 = _rms_r(ff)
        err = x1 + (ff * r4) * g4_ref[...] - t_ref[...]
        loss_ref[...] += jnp.sum(err * err, axis=0, keepdims=True)
        dy = err * (1.0 / D_MODEL)
        dff, dg4 = _rms_bwd(dy, ff, r4, g4_ref[...])
        dg4_ref[...] += jnp.sum(dg4, axis=0, keepdims=True)
        dffb = dff.astype(MXU_DTYPE)
        dff_ref[...] = dffb
        dh2 = jnp.zeros((tm, D_MODEL), F32)
        for j in range(N_DEV):
            sl = slice(fc * j, fc * (j + 1))
            rl = jnp.sqrt(act_ref[:, sl].astype(F32))
            dab = (_dot(dffb, wdown_ref[j], NT) * (2.0 * rl)).astype(MXU_DTYPE)
            da_ref[:, sl] = dab
            dh2 = dh2 + _dot(dab, wup_ref[j], NT)
        dx1, dg3 = _rms_bwd(dh2, x1, _rms_r(x1), g3_ref[...])
        dg3_ref[...] += jnp.sum(dg3, axis=0, keepdims=True)
        dx1_ref[...] = dy + dx1

    tok = lambda w: pl.BlockSpec((tm, w), lambda i: (i, 0))
    full = lambda a: pl.BlockSpec(a.shape, lambda i: (0,) * a.ndim, pipeline_mode=pl.Buffered(1))
    vec = pl.BlockSpec((1, D_MODEL), lambda i: (0, 0))
    sd = jax.ShapeDtypeStruct
    return pl.pallas_call(
        body, name="mlp_down_bwd", grid=(T // tm,),
        in_specs=[tok(D_MODEL), tok(D_FF), tok(D_MODEL), full(wup), full(wdown), vec, vec],
        out_specs=[tok(D_FF), tok(D_MODEL), tok(D_MODEL), vec, vec, vec],
        out_shape=[sd((T, D_FF), MXU_DTYPE), sd((T, D_MODEL), MXU_DTYPE),
                   sd((T, D_MODEL), F32), sd((1, D_MODEL), F32), sd((1, D_MODEL), F32), sd((1, D_MODEL), F32)],
        compiler_params=_params(("arbitrary",)),
    )(x1, act, target, wup, wdown, g3, g4)


def _inproj_bwd_call(du, dq, dk, dv, dgates, dx1, x, win_t, g1, rider=None):
    T = x.shape[0]
    tm = _tile(T, 512)

    def body(du_ref, dq_ref, dk_ref, dv_ref, dgt_ref, dx1_ref, x_ref, w_ref, g1_ref, gx_ref, dg1_ref, db_ref):
        @pl.when(pl.program_id(0) == 0)
        def _():
            dg1_ref[...] = jnp.zeros_like(dg1_ref)
            db_ref[...] = jnp.zeros_like(db_ref)

        dh = jnp.zeros((tm, D_MODEL), F32)
        for ref, lo, hi in ((du_ref, 0, C_Q), (dq_ref, C_Q, C_K), (dk_ref, C_K, C_V), (dv_ref, C_V, C_G),
                            (dgt_ref, C_G, IN_WIDTH)):
            piece = ref[...]
            dh = dh + _dot(piece, w_ref[lo:hi, :], NN)
            if hi <= C_G:
                db_ref[:, lo:hi] += jnp.sum(piece.astype(F32), axis=0, keepdims=True)
        xv = x_ref[...]
        dx, dg1 = _rms_bwd(dh, xv, _rms_r(xv), g1_ref[...])
        dg1_ref[...] += jnp.sum(dg1, axis=0, keepdims=True)
        gx_ref[...] = dx1_ref[...] + dx

    tok = lambda w: pl.BlockSpec((tm, w), lambda i: (i, 0))
    full = lambda a: pl.BlockSpec(a.shape, lambda i: (0,) * a.ndim)
    sd = jax.ShapeDtypeStruct
    return _launch(
        body, [du, dq, dk, dv, dgates, dx1, x, win_t, g1], name="inproj_bwd", grid=(T // tm,),
        in_specs=[tok(POOL_WIDTH), tok(ATTN_WIDTH), tok(KV_WIDTH), tok(KV_WIDTH), tok(GATE_WIDTH), tok(D_MODEL),
                  tok(D_MODEL), full(win_t), full(g1)],
        out_specs=[tok(D_MODEL), pl.BlockSpec((1, D_MODEL), lambda i: (0, 0)), pl.BlockSpec((1, C_G), lambda i: (0, 0))],
        out_shape=[sd((T, D_MODEL), F32), sd((1, D_MODEL), F32), sd((1, C_G), F32)],
        sem=("arbitrary",), rider=rider)


WGRAD_TOKENS = 1024


def _wgrad_rows_call(a, b, name, rider=None):
    T, K = a.shape
    N = b.shape[1]
    tm = _tile(T, WGRAD_TOKENS)
    kb = min(K, 1024)
    per = kb // (K // N_DEV)

    def body(a_ref, b_ref, o_ref):
        @pl.when(pl.program_id(1) == 0)
        def _():
            o_ref[...] = jnp.zeros_like(o_ref)

        d = _dot(a_ref[...], b_ref[...], TN)
        rs = kb // per
        for j in range(per):
            o_ref[j] += d[rs * j:rs * (j + 1)]

    return _launch(
        body, [a, b], name=name, grid=(K // kb, T // tm),
        in_specs=[pl.BlockSpec((tm, kb), lambda i, t: (t, i)), pl.BlockSpec((tm, N), lambda i, t: (t, 0))],
        out_specs=[pl.BlockSpec((per, K // N_DEV, N), lambda i, t: (i, 0, 0))],
        out_shape=[jax.ShapeDtypeStruct((N_DEV, K // N_DEV, N), F32)],
        sem=("arbitrary", "arbitrary"), rider=rider)


def _wgrad_cols_call(a, b, name, rider=None):
    T, K = a.shape
    N = b.shape[1]
    tm = _tile(T, WGRAD_TOKENS)
    nb = min(N, 1024)
    per = nb // (N // N_DEV)

    def body(a_ref, b_ref, o_ref):
        @pl.when(pl.program_id(1) == 0)
        def _():
            o_ref[...] = jnp.zeros_like(o_ref)

        d = _dot(a_ref[...], b_ref[...], TN)
        cs = nb // per
        for j in range(per):
            o_ref[j] += d[:, cs * j:cs * (j + 1)]

    return _launch(
        body, [a, b], name=name, grid=(N // nb, T // tm),
        in_specs=[pl.BlockSpec((tm, K), lambda i, t: (t, 0)), pl.BlockSpec((tm, nb), lambda i, t: (t, i))],
        out_specs=[pl.BlockSpec((per, K, N // N_DEV), lambda i, t: (i, 0, 0))],
        out_shape=[jax.ShapeDtypeStruct((N_DEV, K, N // N_DEV), F32)],
        sem=("arbitrary", "arbitrary"), rider=rider)


def _wgrad_in_call(du, dq, dk, dv, dgates, h, rider=None):
    T = h.shape[0]
    tm = _tile(T, WGRAD_TOKENS)
    rows = IN_WIDTH // N_DEV

    def body(du_ref, dq_ref, dk_ref, dv_ref, dgt_ref, h_ref, o_ref, acc, sem):
        t = pl.program_id(0)

        @pl.when(t == 0)
        def _():
            acc[...] = jnp.zeros_like(acc)

        hv = h_ref[...]
        for ref, lo, hi in ((du_ref, 0, C_Q), (dq_ref, C_Q, C_K), (dk_ref, C_K, C_V), (dv_ref, C_V, C_G),
                            (dgt_ref, C_G, IN_WIDTH)):
            acc[lo:hi, :] += _dot(ref[...], hv, TN)

        @pl.when(t == pl.num_programs(0) - 1)
        def _():
            copies = [pltpu.make_async_copy(acc.at[pl.ds(rows * j, rows), :], o_ref.at[j], sem.at[j])
                      for j in range(N_DEV)]
            for cp in copies:
                cp.start()
            for cp in copies:
                cp.wait()

    tok = lambda w: pl.BlockSpec((tm, w), lambda t: (t, 0))
    return _launch(
        body, [du, dq, dk, dv, dgates, h], name="wgrad_in", grid=(T // tm,),
        in_specs=[tok(POOL_WIDTH), tok(ATTN_WIDTH), tok(KV_WIDTH), tok(KV_WIDTH), tok(GATE_WIDTH), tok(D_MODEL)],
        out_specs=[pl.BlockSpec(memory_space=pl.ANY)],
        out_shape=[jax.ShapeDtypeStruct((N_DEV, rows, D_MODEL), F32)],
        scratch_shapes=[pltpu.VMEM((IN_WIDTH, D_MODEL), F32), pltpu.SemaphoreType.DMA((N_DEV,))],
        sem=("arbitrary",), rider=rider)


def _coords():
    return lax.axis_index("x"), lax.axis_index("y"), lax.axis_index("c")


def _ag_route():
    x, y, c = _coords()
    return (x, y, c), (x, y, 1 - c), (x ^ (1 - c), y ^ c, c), (x ^ c, y ^ (1 - c), c), (1 - x, 1 - y, c)


def _rider_ag_first(shard, me):
    def plan(ins, outs, send, recv, loc, r0, l0):
        own, *peers = _ag_route()
        return [pltpu.make_async_remote_copy(
            src_ref=ins[0], dst_ref=outs[0].at[_slot(own)], send_sem=send.at[r0 + k], recv_sem=recv.at[r0 + k],
            device_id=peers[k], device_id_type=MESH) for k in range(3)], []

    return _Rider([shard], [jax.ShapeDtypeStruct((N_DEV,) + shard.shape, shard.dtype)], 3, 0, plan,
                  lands=[_gather_buffer(shard, me)])


def _rider_ag_onward(shard, stage):
    def plan(ins, outs, send, recv, loc, r0, l0):
        own, sibling, near1, near2, diag = _ag_route()
        moves = [(near1, near2), (near1, sibling), (near2, sibling)] if stage == 2 else [(diag, sibling)]
        copies = []
        for k, (block, to) in enumerate(moves):
            part = outs[0].at[_slot(block)]
            copies.append(pltpu.make_async_remote_copy(src_ref=part, dst_ref=part, send_sem=send.at[r0 + k],
                                                       recv_sem=recv.at[r0 + k], device_id=to, device_id_type=MESH))
        return copies, []

    return _Rider([shard], [jax.ShapeDtypeStruct((N_DEV,) + shard.shape, shard.dtype)], 3 if stage == 2 else 1, 0, plan)


def _slot(p):
    return 4 * p[0] + 2 * p[1] + p[2]


def _rows(ref, span):
    return ref if span is None else ref.at[pl.ds(span[0], span[1])]


ALL = "all"
LOCAL = "local"


def _rows(ref, span):
    return ref if span == ALL else ref.at[pl.ds(span[0], span[1])]


def _rider_ag(items):
    ins, out_shape, aliases, where = [], [], {}, []
    n_remote = n_local = 0
    for t, (shard, buf, snd, fwd) in enumerate(items):
        i_shard = i_buf = None
        if snd is not None:
            i_shard = len(ins)
            ins.append(shard)
        if buf is not None:
            i_buf = len(ins)
            ins.append(buf)
            aliases[i_buf] = t
            out_shape.append(jax.ShapeDtypeStruct(buf.shape, buf.dtype))
        else:
            assert fwd is None and snd is not None
            out_shape.append(jax.ShapeDtypeStruct((N_DEV,) + shard.shape, shard.dtype))
        where.append((i_shard, i_buf, n_remote, n_local))
        n_remote += (4 if snd not in (None, LOCAL) else 0) + (3 if fwd is not None else 0)
        n_local += 1 if snd is not None else 0

    def plan(rins, routs, send, recv, loc, r0, l0):
        x, y, c = _coords()
        peers = [(x, y, 1 - c), (1 - x, y, c), (x, 1 - y, c), (1 - x, 1 - y, c)]
        remote, local = [], []
        for t, (shard, buf, snd, fwd) in enumerate(items):
            i_shard, i_buf, k, l = where[t]
            k, l = r0 + k, l0 + l
            if snd is not None:
                span = ALL if snd == LOCAL else snd
                src, dst = _rows(rins[i_shard], span), _rows(routs[t].at[_slot((x, y, c))], span)
                local.append(pltpu.make_async_copy(src, dst, loc.at[l]))
                for peer in (peers if snd != LOCAL else []):
                    remote.append(pltpu.make_async_remote_copy(
                        src_ref=src, dst_ref=dst, send_sem=send.at[k], recv_sem=recv.at[k],
                        device_id=peer, device_id_type=MESH))
                    k += 1
            if fwd is not None:
                for px, py, pc in peers[1:]:
                    s = _slot((px, py, pc))
                    remote.append(pltpu.make_async_remote_copy(
                        src_ref=_rows(rins[i_buf].at[s], fwd), dst_ref=_rows(routs[t].at[s], fwd),
                        send_sem=send.at[k], recv_sem=recv.at[k], device_id=peers[0], device_id_type=MESH))
                    k += 1
        return remote, local

    return _Rider(ins, out_shape, n_remote, n_local, plan, aliases)


def _gather_buffer(shard, me):
    return lax.dynamic_update_slice(lax.empty((N_DEV,) + shard.shape, shard.dtype), shard[None], (me, 0, 0))


def _rider_ag_remote(shards, me):
    n = len(shards)

    def plan(ins, outs, send, recv, loc, r0, l0):
        x, y, c = _coords()
        remote = []
        for t in range(n):
            dst = outs[t].at[_slot((x, y, c))]
            for k, peer in enumerate([(x, y, 1 - c), (1 - x, y, c), (x, 1 - y, c), (1 - x, 1 - y, c)]):
                remote.append(pltpu.make_async_remote_copy(
                    src_ref=ins[t], dst_ref=dst, send_sem=send.at[r0 + 4 * t + k], recv_sem=recv.at[r0 + 4 * t + k],
                    device_id=peer, device_id_type=MESH))
        return remote, []

    return _Rider(shards, [jax.ShapeDtypeStruct((N_DEV,) + s.shape, s.dtype) for s in shards], 4 * n, 0, plan,
                  lands=[_gather_buffer(s, me) for s in shards])


def _rider_rs_sibling(grads):
    n = len(grads)

    def plan(ins, outs, send, recv, loc, r0, l0):
        x, y, c = _coords()
        remote = []
        for t in range(n):
            for q in range(4):
                remote.append(pltpu.make_async_remote_copy(
                    src_ref=ins[t].at[q, 1 - c], dst_ref=outs[t].at[q], send_sem=send.at[r0 + 4 * t + q],
                    recv_sem=recv.at[r0 + 4 * t + q], device_id=(x, y, 1 - c), device_id_type=MESH))
        return remote, []

    return _Rider(grads, [jax.ShapeDtypeStruct((4,) + g.shape[2:], g.dtype) for g in grads], 4 * n, 0, plan)


def _rider_rs_chips(sums, rows=None, into=None):
    n = len(sums)
    rows = rows or [ALL] * n

    def plan(ins, outs, send, recv, loc, r0, l0):
        x, y, c = _coords()
        remote = []
        for t in range(n):
            for r, (px, py) in enumerate([(1 - x, y), (x, 1 - y), (1 - x, 1 - y)]):
                remote.append(pltpu.make_async_remote_copy(
                    src_ref=_rows(ins[t].at[2 * px + py], rows[t]), dst_ref=_rows(outs[t].at[r], rows[t]),
                    send_sem=send.at[r0 + 3 * t + r], recv_sem=recv.at[r0 + 3 * t + r],
                    device_id=(px, py, c), device_id_type=MESH))
        return remote, []

    out_shape = [jax.ShapeDtypeStruct((3,) + s.shape[1:], s.dtype) for s in sums]
    if into is None:
        return _Rider(sums, out_shape, 3 * n, 0, plan)
    return _Rider(list(sums) + list(into), out_shape, 3 * n, 0, plan, aliases={n + t: t for t in range(n)})


def _rider_gather_remote(parts):
    n = len(parts)

    def plan(ins, outs, send, recv, loc, r0, l0):
        x, y, c = _coords()
        me = _slot((x, y, c))
        remote = []
        for t in range(n):
            for k in range(1, N_DEV):
                peer = (x ^ ((k >> 2) & 1), y ^ ((k >> 1) & 1), c ^ (k & 1))
                remote.append(pltpu.make_async_remote_copy(
                    src_ref=ins[t], dst_ref=outs[t].at[me], send_sem=send.at[r0 + 7 * t + k - 1],
                    recv_sem=recv.at[r0 + 7 * t + k - 1], device_id=peer, device_id_type=MESH))
        return remote, []

    return _Rider(parts, [jax.ShapeDtypeStruct((N_DEV,) + p.shape, p.dtype) for p in parts], 7 * n, 0, plan)


def _chip_sum_call(idx, grads, recvd, out_dtypes, name):
    n = len(grads)

    def body(i_ref, *refs):
        for t in range(n):
            refs[2 * n + t][0] = (refs[t][0, 0] + refs[n + t][0]).astype(out_dtypes[t])

    def chip(k, s):
        return jnp.where(k >= s[0], k + 1, k)

    in_specs = [pl.BlockSpec((1, 1) + g.shape[2:], lambda k, s: (chip(k, s), s[1], 0, 0)) for g in grads]
    in_specs += [pl.BlockSpec((1,) + r.shape[1:], lambda k, s: (chip(k, s), 0, 0)) for r in recvd]
    return pl.pallas_call(
        body, name=name,
        grid_spec=pltpu.PrefetchScalarGridSpec(
            num_scalar_prefetch=1, grid=(3,), in_specs=in_specs,
            out_specs=[pl.BlockSpec((1,) + r.shape[1:], lambda k, s: (chip(k, s), 0, 0)) for r in recvd]),
        out_shape=[jax.ShapeDtypeStruct(r.shape, dt) for r, dt in zip(recvd, out_dtypes)],
        compiler_params=_params(("arbitrary",)),
    )(idx, *grads, *recvd)


def _final_sum_call(idx, grads, recvd1, recvd2):
    n = len(grads)
    nsteps = 2

    def body(i_ref, *refs):
        for t in range(n):
            g, r1, r2, o = refs[t], refs[n + t], refs[2 * n + t], refs[3 * n + t]
            s = g[0, 0] + r1[0]
            for r in range(3):
                s = s + r2[r].astype(F32)
            o[...] = s

    def rows(a):
        r = a.shape[-2]
        return r // nsteps if (r // nsteps) % 16 == 0 else r

    def step(a):
        return (lambda i: i) if rows(a) != a.shape[-2] else (lambda i: 0)

    in_specs = [pl.BlockSpec((1, 1, rows(g), g.shape[3]), lambda i, s, st=step(g): (s[0], s[1], st(i), 0)) for g in grads]
    in_specs += [pl.BlockSpec((1, rows(r), r.shape[2]), lambda i, s, st=step(r): (s[0], st(i), 0)) for r in recvd1]
    in_specs += [pl.BlockSpec((3, rows(r), r.shape[2]), lambda i, s, st=step(r): (0, st(i), 0)) for r in recvd2]
    return pl.pallas_call(
        body, name="rs_final_sum",
        grid_spec=pltpu.PrefetchScalarGridSpec(
            num_scalar_prefetch=1, grid=(nsteps,), in_specs=in_specs,
            out_specs=[pl.BlockSpec((rows(r), r.shape[2]), lambda i, s, st=step(r): (st(i), 0)) for r in recvd2]),
        out_shape=[jax.ShapeDtypeStruct(r.shape[1:], F32) for r in recvd2],
        compiler_params=_params(("arbitrary",)),
    )(idx, *grads, *recvd1, *recvd2)


def _sum8_call(parts):
    def body(p_ref, o_ref):
        s = p_ref[0]
        for j in range(1, N_DEV):
            s = s + p_ref[j]
        o_ref[...] = s

    return pl.pallas_call(body, name="sum_small_partials",
                          out_shape=jax.ShapeDtypeStruct(parts.shape[1:], parts.dtype))(parts)


def _adamw(w, g, m, v):
    m = ADAM_B1 * m + (1.0 - ADAM_B1) * g
    v = ADAM_B2 * v + (1.0 - ADAM_B2) * (g * g)
    m_hat = m / (1.0 - ADAM_B1 ** ADAM_STEP)
    v_hat = v / (1.0 - ADAM_B2 ** ADAM_STEP)
    delta = -ADAM_LR * (m_hat / (jnp.sqrt(v_hat) + ADAM_EPS) + ADAM_WD * w)
    return delta, m, v


def _adamw_call(ws, gs, ms, vs, nsteps, name):
    n = len(ws)

    def body(*refs):
        for t in range(n):
            w, g, m, v = (refs[k * n + t][...] for k in range(4))
            d, m2, v2 = _adamw(w, g, m, v)
            refs[4 * n + t][...] = d
            refs[5 * n + t][...] = m2
            refs[6 * n + t][...] = v2

    def spec(a):
        assert a.shape[0] % nsteps == 0 and (nsteps == 1 or (a.shape[0] // nsteps) % 8 == 0), a.shape
        return pl.BlockSpec((a.shape[0] // nsteps, a.shape[1]), lambda i: (i, 0))

    specs = [spec(a) for a in ws]
    outs = pl.pallas_call(
        body, name=name, grid=(nsteps,),
        in_specs=specs * 4, out_specs=specs * 3,
        out_shape=[jax.ShapeDtypeStruct(a.shape, F32) for a in ws] * 3,
        compiler_params=_params(("arbitrary",)),
    )(*ws, *gs, *ms, *vs)
    return outs[:n], outs[n:2 * n], outs[2 * n:]


def _adamw_rs_call(idx, after, gws, r1s, r2s, ws, ms, vs, nsteps, name):
    n = len(ws)

    def body(i_ref, after_ref, *refs):
        for t in range(n):
            gw, r1, r2, w, m, v = (refs[k * n + t] for k in range(6))
            g = gw[0, 0] + r1[0]
            for r in range(3):
                g = g + r2[r].astype(F32)
            d, m2, v2 = _adamw(w[...], g, m[...], v[...])
            refs[6 * n + t][...] = g
            refs[7 * n + t][...] = d
            refs[8 * n + t][...] = m2
            refs[9 * n + t][...] = v2

    def rb(a):
        r = a.shape[0] // nsteps
        assert a.shape[0] % nsteps == 0 and r % 16 == 0, a.shape
        return r

    in_specs = [pl.BlockSpec((1, 1, rb(w), w.shape[1]), lambda i, s: (s[0], s[1], i, 0)) for w in ws]
    in_specs += [pl.BlockSpec((1, rb(w), w.shape[1]), lambda i, s: (s[0], i, 0)) for w in ws]
    in_specs += [pl.BlockSpec((3, rb(w), w.shape[1]), lambda i, s: (0, i, 0)) for w in ws]
    plain = [pl.BlockSpec((rb(w), w.shape[1]), lambda i, s: (i, 0)) for w in ws]
    outs = pl.pallas_call(
        body, name=name,
        grid_spec=pltpu.PrefetchScalarGridSpec(
            num_scalar_prefetch=1, grid=(nsteps,),
            in_specs=[pl.BlockSpec(memory_space=pl.ANY)] + in_specs + plain * 3, out_specs=plain * 4),
        out_shape=[jax.ShapeDtypeStruct(w.shape, F32) for w in ws] * 4,
        compiler_params=_params(("arbitrary",)),
    )(idx, after, *gws, *r1s, *r2s, *ws, *ms, *vs)
    return outs[:n], outs[n:2 * n], outs[2 * n:3 * n], outs[3 * n:]


def _rows128(a, pad_rows):
    flat = a.reshape(-1).astype(F32)
    flat = jnp.pad(flat, (0, pad_rows * LANES - flat.shape[0]))
    return flat.reshape(pad_rows, LANES)


_SMALL_A = (("w_pool", 512), ("pool_scale", 8), ("attn_sinks", 8), ("g_mix_post", 8), ("g_mlp_pre", 8),
            ("g_mlp_post", 8), ("loss", 8), ("b_in_gates", 16))
_SMALL_A_ROWS = 640
_SMALL_B = (("g_mix_pre", 8), ("b_in_head", 16))


def _pack(parts, layout, total_rows):
    rows = [_rows128(parts[k], r) for k, r in layout]
    pad = total_rows - sum(r for _, r in layout)
    if pad:
        rows.append(jnp.zeros((pad, LANES), F32))
    return jnp.concatenate(rows, axis=0)


def _unpack(buf, layout, sizes):
    out, off = {}, 0
    for k, r in layout:
        out[k] = buf[off:off + r].reshape(-1)[:sizes[k]]
        off += r
    return out


def kernel(x, g_mix_pre, w_in, b_in, w_pool, pool_scale, attn_sinks, w_branch_pool, w_branch_attn, w_out, g_mix_post, g_mlp_pre, w_up, w_down, g_mlp_post, loss_target, m_g_mix_pre, m_w_in, m_b_in, m_w_pool, m_pool_scale, m_attn_sinks, m_w_branch_pool, m_w_branch_attn, m_w_out, m_g_mix_post, m_g_mlp_pre, m_w_up, m_w_down, m_g_mlp_post, v_g_mix_pre, v_w_in, v_b_in, v_w_pool, v_pool_scale, v_attn_sinks, v_w_branch_pool, v_w_branch_attn, v_w_out, v_g_mix_post, v_g_mlp_pre, v_w_up, v_w_down, v_g_mlp_post):
    B, S, _ = x.shape
    T = B * S
    xt = x.reshape(T, D_MODEL)
    tgt = loss_target.reshape(T, D_MODEL)
    cx, cy, cc = _coords()

    cidx = jnp.stack([2 * cx + cy, cc]).astype(jnp.int32)
    by_chip = lambda gr: gr.reshape((4, 2) + gr.shape[1:])
    bf = lambda w: w[0].astype(MXU_DTYPE)

    me = _slot((cx, cy, cc))
    win_l = w_in[0].T.astype(MXU_DTYPE)
    (c_win,), _ = _copies_start([_rider_ag_first(win_l, me)], "allgather_first")
    wpool_b = bf(w_pool)
    rc, rsa, rsb = _rot_tables(S)
    wbp_l, wba_l, wout_l, wup_l, wdown_l = bf(w_branch_pool), bf(w_branch_attn), bf(w_out), bf(w_up), bf(w_down)
    gathers = [_rider_ag_remote([wbp_l, wba_l, wout_l], me), _rider_ag_remote([wup_l], me), _rider_ag_remote([wdown_l], me)]
    c_win = _copies_pass([c_win], [_rider_ag_onward(win_l, 2)], [b for r in gathers for b in r.lands] + [rc, rsa, rsb],
                         "allgather_second")
    c_win = _copies_pass(c_win, [_rider_ag_onward(win_l, 3)], [wbp_l, wba_l, wout_l, wup_l, wdown_l], "allgather_third")
    (win_s,) = _copies_wait(c_win, wpool_b, "allgather_weights")
    win_t = win_s.reshape(IN_WIDTH, D_MODEL)

    (c_br, c_up, c_dn), tok = _copies_start(gathers, "allgather_start", after=win_s)
    (h, u, q, k4, v4, g), _ = _inproj_call(xt, g_mix_pre, win_t, b_in, rc, rsa, rsb, S, rider=_after(tok))
    yp = _pool_call(u, wpool_b, pool_scale, S)
    wbp_1, wba_1, wout_1 = _copies_wait([c_br], yp, "allgather_wait_branch")
    (ya,), (wbp_s, wba_s, wout_s) = _attn_call(
        attn_sinks, q, k4, v4, S,
        rider=_rider_ag([(None, wbp_1, None, ALL), (None, wba_1, None, ALL), (None, wout_1, None, ALL)]))
    wout_f = wout_s.reshape(D_MODEL, D_MODEL)
    (wup_1,) = _copies_wait([c_up], ya, "allgather_wait_up")
    (mix, x1, h2), (wup_s,) = _mix_fwd_call(
        yp, ya, g, xt, wbp_s, wba_s, wout_f, g_mix_post, g_mlp_pre, rider=_rider_ag([(None, wup_1, None, ALL)]))
    act = _mlp_up_call(h2, wup_s)
    (wdown_1,) = _copies_wait([c_dn], act, "allgather_wait_down")
    (wdown_s,) = _comm_call(_rider_ag([(None, wdown_1, None, ALL)]), "allgather_pass_down")

    da, dff, dx1, dg3, dg4, lossvec = _mlp_call(x1, act, tgt, wup_s, wdown_s, g_mlp_pre, g_mlp_post)
    gw_down = by_chip(_wgrad_rows_call(act, dff, "wgrad_down")[0])
    (gw_up,), (r1_down,) = _wgrad_cols_call(h2, da, "wgrad_up", rider=_rider_rs_sibling([gw_down]))
    gw_up = by_chip(gw_up)
    (s_down,) = _chip_sum_call(cidx, [gw_down], [r1_down], [MXU_DTYPE], "rs_chip_sum_down")
    (dyp, do, dgates, dg2, dbg, gw_out, gw_bp, gw_ba), (r1_up,) = _mix_bwd_call(
        dx1, mix, yp, ya, g, wbp_s, wba_s, wout_f, g_mix_post, rider=_rider_rs_sibling([gw_up]))
    gw_out = by_chip(gw_out.reshape(N_DEV, D_MODEL // N_DEV, D_MODEL))
    gw_bp, gw_ba = by_chip(gw_bp), by_chip(gw_ba)
    (s_up,) = _chip_sum_call(cidx, [gw_up], [r1_up], [MXU_DTYPE], "rs_chip_sum_up")
    (c_down, c_up), tok = _copies_start([_rider_rs_chips([s_down]), _rider_rs_chips([s_up])], "rs_chips_start_mlp")
    (dq, dk, dv, dsink), (r1_out, r1_bp, r1_ba) = _attn_bwd_call(
        attn_sinks, q, k4, v4, do, rc, rsa, rsb, S, rider=_after(tok, _rider_rs_sibling([gw_out, gw_bp, gw_ba])))
    s_obb = _chip_sum_call(cidx, [gw_out, gw_bp, gw_ba], [r1_out, r1_bp, r1_ba], [MXU_DTYPE] * 3, "rs_chip_sum_branch")
    (c_obb,), tok = _copies_start([_rider_rs_chips(s_obb)], "rs_chips_start_branch")
    (du, dwp, dps), _ = _pool_bwd_call(u, dyp, wpool_b, pool_scale, S, rider=_after(tok))
    (gw_in,) = _wgrad_in_call(du, dq, dk, dv, dgates, h)
    gw_in = by_chip(gw_in)

    small_a = {"w_pool": dwp, "pool_scale": dps,
               "attn_sinks": jnp.sum(dsink.reshape(B, 8, LANES)[:, 0, :N_Q_HEADS], axis=0), "g_mix_post": dg2,
               "g_mlp_pre": dg3, "g_mlp_post": dg4, "loss": lossvec, "b_in_gates": dbg}
    gw_sa = by_chip(_pack(small_a, _SMALL_A, _SMALL_A_ROWS).reshape(N_DEV, _SMALL_A_ROWS // N_DEV, LANES))
    r1_in, r1_sa = _comm_call(_rider_rs_sibling([gw_in, gw_sa]), "rs_sibling_in")
    s_in, s_sa = _chip_sum_call(cidx, [gw_in, gw_sa], [r1_in, r1_sa], [MXU_DTYPE, F32], "rs_chip_sum_in")
    (c_in,), tok = _copies_start([_rider_rs_chips([s_in, s_sa])], "rs_chips_start_in")
    (gx, dg1, dba_in), _ = _inproj_bwd_call(du, dq, dk, dv, dgates, dx1, xt, win_t, g_mix_pre, rider=_after(tok))
    r2_down, r2_up, r2_out, r2_bp, r2_ba, r2_in, r2_sa = _copies_wait([c_down, c_up, c_obb, c_in], dg1, "rs_chips_wait")

    (g_sa,) = _final_sum_call(cidx, [gw_sa], [r1_sa], [r2_sa])
    part_b = _pack({"g_mix_pre": dg1, "b_in_head": dba_in}, _SMALL_B, sum(r for _, r in _SMALL_B))
    (c_small,), tok = _copies_start([_rider_gather_remote([g_sa, part_b])], "allgather_small_start")

    in_t = _adamw_rs_call(cidx, tok, [gw_in], [r1_in], [r2_in], [w_in[0].T], [m_w_in[0].T], [v_w_in[0].T], 2,
                          "adamw_w_in")
    rest = _adamw_rs_call(
        cidx, tok, [gw_bp, gw_ba, gw_out, gw_up, gw_down], [r1_bp, r1_ba, r1_out, r1_up, r1_down],
        [r2_bp, r2_ba, r2_out, r2_up, r2_down], [w_branch_pool[0], w_branch_attn[0], w_out[0], w_up[0], w_down[0]],
        [m_w_branch_pool[0], m_w_branch_attn[0], m_w_out[0], m_w_up[0], m_w_down[0]],
        [v_w_branch_pool[0], v_w_branch_attn[0], v_w_out[0], v_w_up[0], v_w_down[0]], N_DEV, "adamw_shards")
    big_g, big_d, big_m2, big_v2 = ([a[0].T] + list(b) for a, b in zip(in_t, rest))

    sa_all, sb_all = _copies_wait([c_small], rest[0][0], "allgather_small_wait")
    sa_all = lax.dynamic_update_slice(sa_all, g_sa[None], (me, 0, 0))
    sb_sum = _sum8_call(lax.dynamic_update_slice(sb_all, part_b[None], (me, 0, 0)))

    names = ["g_mix_pre", "b_in", "w_pool", "pool_scale", "attn_sinks", "g_mix_post", "g_mlp_pre", "g_mlp_post"]
    sm_w = dict(g_mix_pre=g_mix_pre, b_in=b_in, w_pool=w_pool, pool_scale=pool_scale, attn_sinks=attn_sinks,
                g_mix_post=g_mix_post, g_mlp_pre=g_mlp_pre, g_mlp_post=g_mlp_post)
    sm_m = dict(g_mix_pre=m_g_mix_pre, b_in=m_b_in, w_pool=m_w_pool, pool_scale=m_pool_scale, attn_sinks=m_attn_sinks,
                g_mix_post=m_g_mix_post, g_mlp_pre=m_g_mlp_pre, g_mlp_post=m_g_mlp_post)
    sm_v = dict(g_mix_pre=v_g_mix_pre, b_in=v_b_in, w_pool=v_w_pool, pool_scale=v_pool_scale, attn_sinks=v_attn_sinks,
                g_mix_post=v_g_mix_post, g_mlp_pre=v_g_mlp_pre, g_mlp_post=v_g_mlp_post)
    sizes = {k: sm_w[k].size for k in names}
    sizes.update(loss=D_MODEL, b_in_gates=GATE_WIDTH, b_in_head=C_G)
    sm_g = _unpack(sa_all.reshape(_SMALL_A_ROWS, LANES), _SMALL_A, sizes)
    sm_g.update(_unpack(sb_sum, _SMALL_B, sizes))
    sm_g["b_in"] = jnp.concatenate([sm_g["b_in_head"], sm_g["b_in_gates"]])
    loss = (0.5 / D_MODEL) * jnp.sum(sm_g["loss"])
    two_d = lambda a: a.reshape(-1, a.shape[-1])
    sd_, sm2_, sv2_ = _adamw_call([two_d(sm_w[k]) for k in names], [two_d(sm_g[k].reshape(sm_w[k].shape)) for k in names],
                                  [two_d(sm_m[k]) for k in names], [two_d(sm_v[k]) for k in names], 1, "adamw_small")
    like = lambda vals: {k: a.reshape(sm_w[k].shape) for k, a in zip(names, vals)}
    sm_d, sm_m2, sm_v2 = like(sd_), like(sm2_), like(sv2_)
    sm_gr = {k: sm_g[k].reshape(sm_w[k].shape) for k in names}

    order = ["g_mix_pre", "w_in", "b_in", "w_pool", "pool_scale", "attn_sinks", "w_branch_pool", "w_branch_attn",
             "w_out", "g_mix_post", "g_mlp_pre", "w_up", "w_down", "g_mlp_post"]
    big_names = ["w_in", "w_branch_pool", "w_branch_attn", "w_out", "w_up", "w_down"]
    lead = lambda a: a[None]
    tables = []
    for small_t, big_t in ((sm_gr, big_g), (sm_d, big_d), (sm_m2, big_m2), (sm_v2, big_v2)):
        bt = dict(zip(big_names, big_t))
        tables.append([lead(bt[k]) if k in bt else small_t[k] for k in order])
    return (loss, gx.reshape(B, S, D_MODEL), *tables[0], *tables[1], *tables[2], *tables[3])
```

```python
import jax
import jax.numpy as jnp
from jax import lax
from jax.experimental import pallas as pl
from jax.experimental.pallas import tpu as pltpu

F32 = jnp.float32
MXU_DTYPE = jnp.bfloat16
MESH = pl.DeviceIdType.MESH

D_MODEL = 1024
POOL_WINDOWS = (2, 4, 8, 16)
POOL_WIDTH = 512
POOL_GC = 128
HEAD_DIM = 64
N_Q_HEADS = 8
N_KV_HEADS = 2
GROUP = 4
ATTN_WIDTH = 512
KV_WIDTH = 128
BLOCK = 128
GATE_WIDTH = 2048
IN_WIDTH = 3328
D_FF = 4096
EPS = 1e-6
NEG_INF = -1e30
ROPE_THETA = 500000.0
ROT_DIM = 16
SCALE = HEAD_DIM ** -0.5
C_Q, C_K, C_V, C_G = 512, 1024, 1152, 1280

ADAM_LR = 0.001
ADAM_B1 = 0.9
ADAM_B2 = 0.999
ADAM_EPS = 1e-08
ADAM_WD = 0.01
ADAM_STEP = 10

N_DEV = 8
LANES = 128
VMEM_LIMIT = 56 * 1024 * 1024

NN = (((1,), (0,)), ((), ()))
NT = (((1,), (1,)), ((), ()))
TN = (((0,), (0,)), ((), ()))


def _dot(a, b, dims):
    return lax.dot_general(a, b, dims, preferred_element_type=F32)


def _params(sem=None):
    return pltpu.CompilerParams(dimension_semantics=sem, vmem_limit_bytes=VMEM_LIMIT)


def _tile(n, pref):
    t = min(n, pref)
    assert n % t == 0, (n, t)
    return t


class _Rider:
    def __init__(self, ins, out_shape, n_remote, n_local, plan, aliases=None, lands=None):
        self.ins, self.out_shape, self.n_remote, self.n_local = list(ins), list(out_shape), n_remote, n_local
        self.plan, self.aliases = plan, dict(aliases or {})
        self.lands = lands


def _after(token, rider=None):
    r = rider or _Rider([], [], 0, 0, lambda ins, outs, send, recv, loc, r0, l0: ([], []))
    return _Rider(r.ins + [token], r.out_shape, r.n_remote, r.n_local, r.plan, r.aliases)


def _launch(body, args, *, name, grid, in_specs, out_specs, out_shape, scratch_shapes=(), sem=None, rider=None):
    if rider is None:
        return pl.pallas_call(body, name=name, grid=grid, in_specs=in_specs, out_specs=out_specs, out_shape=out_shape,
                              scratch_shapes=list(scratch_shapes), compiler_params=_params(sem))(*args)
    n_in, n_out, n_scr = len(args), len(out_shape), len(scratch_shapes)
    r_in, r_out = len(rider.ins), len(rider.out_shape)
    copies = rider.n_remote + rider.n_local > 0

    def wrapped(*refs):
        ins, rins = refs[:n_in], refs[n_in:n_in + r_in]
        o0 = n_in + r_in
        outs, routs = refs[o0:o0 + n_out], refs[o0 + n_out:o0 + n_out + r_out]
        s0 = o0 + n_out + r_out
        scr = refs[s0:s0 + n_scr]
        if not copies:
            return body(*ins, *outs, *scr)
        send, recv, loc = refs[s0 + n_scr:]
        first, last = None, None
        for d in range(len(grid)):
            f, l = pl.program_id(d) == 0, pl.program_id(d) == pl.num_programs(d) - 1
            first = f if first is None else first & f
            last = l if last is None else last & l

        def start():
            remote, local = rider.plan(rins, routs, send, recv, loc, 0, 0)
            for cp in local + remote:
                cp.start()

        def finish():
            remote, local = rider.plan(rins, routs, send, recv, loc, 0, 0)
            for cp in remote + local:
                cp.wait()

        if first is None:
            start()
            body(*ins, *outs, *scr)
            finish()
        else:
            pl.when(first)(start)
            body(*ins, *outs, *scr)
            pl.when(last)(finish)

    hbm = pl.BlockSpec(memory_space=pl.ANY)
    dma = pltpu.SemaphoreType.DMA
    res = pl.pallas_call(
        wrapped, name=name, grid=grid, in_specs=list(in_specs) + [hbm] * r_in,
        out_specs=list(out_specs) + [hbm] * r_out, out_shape=list(out_shape) + rider.out_shape,
        scratch_shapes=list(scratch_shapes) + (
            [dma((max(rider.n_remote, 1),)), dma((max(rider.n_remote, 1),)), dma((max(rider.n_local, 1),))] if copies else []),
        input_output_aliases={n_in + i: n_out + o for i, o in rider.aliases.items()},
        compiler_params=_params(sem),
    )(*args, *rider.ins)
    return list(res[:n_out]), list(res[n_out:])


def _comm_call(rider, name):
    return _launch(lambda: None, [], name=name, grid=(), in_specs=[], out_specs=[], out_shape=[], rider=rider)[1]


_HBM = pl.BlockSpec(memory_space=pltpu.HBM)
_SEM = pl.BlockSpec(memory_space=pltpu.SEMAPHORE)
_EFFECT = pltpu.SideEffectType.DATAFLOW_SIDE_EFFECTING


def _copies_start(riders, name, after=None):
    assert all(r.n_local == 0 and not r.aliases for r in riders)
    extra = [] if after is None else [after]
    sizes = [(len(r.ins), len(r.out_shape)) for r in riders]
    bufs = []
    for r in riders:
        lands = r.lands or [lax.empty(s.shape, s.dtype) for s in r.out_shape]
        bufs += [pltpu.with_memory_space_constraint(a, pltpu.HBM) for a in list(r.ins) + list(lands)]
    nb, ng, ne = len(bufs), len(riders), len(extra)

    def body(*refs):
        sems, token, at = refs[2 * nb + ne:2 * nb + ne + 2 * ng], refs[-1], 0
        for g, (r, (ni, no)) in enumerate(zip(riders, sizes)):
            remote, _ = r.plan(refs[at:at + ni], refs[at + ni:at + ni + no], sems[2 * g], sems[2 * g + 1], None, 0, 0)
            for cp in remote:
                cp.start()
            at += ni + no
        token[...] = jnp.zeros_like(token)

    res = pl.pallas_call(
        body, name=name, in_specs=[_HBM] * nb + [pl.BlockSpec(memory_space=pl.ANY)] * ne,
        out_specs=[_HBM] * nb + [_SEM] * (2 * ng) + [pl.BlockSpec(memory_space=pltpu.VMEM)],
        out_shape=[pltpu.HBM(a.shape, a.dtype) for a in bufs]
        + [pltpu.SemaphoreType.DMA((r.n_remote,)) for r in riders for _ in range(2)]
        + [jax.ShapeDtypeStruct((8, LANES), F32)],
        input_output_aliases={i: i for i in range(nb)},
        compiler_params=pltpu.CompilerParams(has_side_effects=_EFFECT),
    )(*bufs, *extra)
    handles, at = [], 0
    for g, (r, (ni, no)) in enumerate(zip(riders, sizes)):
        handles.append((r, list(res[at:at + ni + no]), res[nb + 2 * g], res[nb + 2 * g + 1]))
        at += ni + no
    return handles, res[-1]


def _copies_wait(handles, after, name):
    bufs = [b for _, bs, _, _ in handles for b in bs]
    sems = [s for _, _, send, recv in handles for s in (send, recv)]
    nb, ng = len(bufs), len(handles)
    after = list(after) if isinstance(after, (list, tuple)) else [after]

    def body(*refs):
        at = 0
        for g, (rider, bs, _, _) in enumerate(handles):
            ni = len(rider.ins)
            remote, _ = rider.plan(refs[at:at + ni], refs[at + ni:at + len(bs)], refs[nb + 2 * g], refs[nb + 2 * g + 1],
                                   None, 0, 0)
            for cp in remote:
                cp.wait_send()
                cp.wait_recv()
            at += len(bs)

    res = pl.pallas_call(
        body, name=name, in_specs=[_HBM] * nb + [_SEM] * (2 * ng) + [pl.BlockSpec(memory_space=pl.ANY)] * len(after),
        out_specs=[_HBM] * nb, out_shape=[pltpu.HBM(a.shape, a.dtype) for a in bufs],
        input_output_aliases={i: i for i in range(nb)},
        compiler_params=pltpu.CompilerParams(has_side_effects=_EFFECT),
    )(*bufs, *sems, *after)
    lands, at = [], 0
    for rider, bs, _, _ in handles:
        lands += list(res[at + len(rider.ins):at + len(bs)])
        at += len(bs)
    return lands


def _copies_pass(handles, riders, after, name):
    bufs = [b for _, bs, _, _ in handles for b in bs]
    sems = [s for _, _, send, recv in handles for s in (send, recv)]
    nb, ng = len(bufs), len(handles)
    after = list(after) if isinstance(after, (list, tuple)) else [after]

    def body(*refs):
        new_sems, at = refs[2 * nb + 2 * ng + len(after):], 0
        for g, ((rider, bs, _, _), then) in enumerate(zip(handles, riders)):
            ins, outs = refs[at:at + len(rider.ins)], refs[at + len(rider.ins):at + len(bs)]
            for cp in rider.plan(ins, outs, refs[nb + 2 * g], refs[nb + 2 * g + 1], None, 0, 0)[0]:
                cp.wait_send()
                cp.wait_recv()
            for cp in then.plan(ins, outs, new_sems[2 * g], new_sems[2 * g + 1], None, 0, 0)[0]:
                cp.start()
            at += len(bs)

    res = pl.pallas_call(
        body, name=name, in_specs=[_HBM] * nb + [_SEM] * (2 * ng) + [pl.BlockSpec(memory_space=pl.ANY)] * len(after),
        out_specs=[_HBM] * nb + [_SEM] * (2 * ng),
        out_shape=[pltpu.HBM(a.shape, a.dtype) for a in bufs]
        + [pltpu.SemaphoreType.DMA((r.n_remote,)) for r in riders for _ in range(2)],
        input_output_aliases={i: i for i in range(nb)},
        compiler_params=pltpu.CompilerParams(has_side_effects=_EFFECT),
    )(*bufs, *sems, *after)
    new, at = [], 0
    for g, ((_, bs, _, _), then) in enumerate(zip(handles, riders)):
        new.append((then, list(res[at:at + len(bs)]), res[nb + 2 * g], res[nb + 2 * g + 1]))
        at += len(bs)
    return new


def _rms_r(x):
    return lax.rsqrt(jnp.mean(x * x, axis=-1, keepdims=True) + EPS)


def _rms_bwd(dn, x, r, g):
    xh = x * r
    dxh = dn * g
    dx = r * (dxh - xh * jnp.mean(dxh * xh, axis=-1, keepdims=True))
    return dx, dn * xh


def _rot(t, c, sa, sb):
    outs = []
    for j in range(t.shape[1] // LANES):
        tj = t[:, LANES * j:LANES * (j + 1)]
        outs.append(tj * c + pltpu.roll(tj, LANES - 8, 1) * sa + pltpu.roll(tj, 8, 1) * sb)
    return outs[0] if len(outs) == 1 else jnp.concatenate(outs, axis=1)


def _rot_tables(S):
    pos = jnp.arange(S, dtype=F32)
    inv_freq = ROPE_THETA ** (-jnp.arange(0, ROT_DIM, 2, dtype=F32) / ROT_DIM)
    ang = pos[:, None] * inv_freq[None, :]
    cos, sin = jnp.cos(ang), jnp.sin(ang)
    one = jnp.ones((S, HEAD_DIM - ROT_DIM), F32)
    zero = jnp.zeros((S, HEAD_DIM - ROT_DIM), F32)
    z8 = jnp.zeros((S, 8), F32)
    c = jnp.concatenate([cos, cos, one], axis=1)
    sa = jnp.concatenate([-sin, z8, zero], axis=1)
    sb = jnp.concatenate([z8, sin, zero], axis=1)
    rep = LANES // HEAD_DIM
    return jnp.tile(c, (1, rep)), jnp.tile(sa, (1, rep)), jnp.tile(sb, (1, rep))


def _lane_tile4(k):
    lane = lax.broadcasted_iota(jnp.int32, k.shape, 1)
    rk = pltpu.roll(k, HEAD_DIM, 1)
    x0 = jnp.where(lane < HEAD_DIM, k, rk)
    x1 = jnp.where(lane < HEAD_DIM, rk, k)
    return jnp.concatenate([x0, x0, x1, x1], axis=1)


def _fold_heads(acc):
    zs = []
    for hk in range(N_KV_HEADS):
        a = acc[:, 256 * hk:256 * hk + LANES] + acc[:, 256 * hk + LANES:256 * (hk + 1)]
        zs.append(a + pltpu.roll(a, HEAD_DIM, 1))
    lane = lax.broadcasted_iota(jnp.int32, zs[0].shape, 1)
    return jnp.where(lane < HEAD_DIM, zs[0], zs[1])


def _inproj_call(x, g1, win_t, b_in, rc, rsa, rsb, S, rider=None):
    T = x.shape[0]
    tm = _tile(S, 512)
    nst = S // tm

    def body(x_ref, g1_ref, w_ref, b_ref, c_ref, sa_ref, sb_ref,
             h_ref, u_ref, q_ref, k4_ref, v4_ref, g_ref):
        xv = x_ref[...]
        hb = ((xv * _rms_r(xv)) * g1_ref[...]).astype(MXU_DTYPE)
        h_ref[...] = hb

        def proj(lo, hi):
            return _dot(hb, w_ref[lo:hi, :], NT) + b_ref[:, lo:hi]

        c, sa, sb = c_ref[...], sa_ref[...], sb_ref[...]
        u_ref[...] = proj(0, C_Q)
        q_ref[...] = (_rot(proj(C_Q, C_K), c, sa, sb) * SCALE).astype(MXU_DTYPE)
        kv = proj(C_K, C_G)
        k4_ref[...] = _lane_tile4(_rot(kv[:, :KV_WIDTH], c, sa, sb)).astype(MXU_DTYPE)
        v4_ref[...] = _lane_tile4(kv[:, KV_WIDTH:]).astype(MXU_DTYPE)
        g_ref[...] = jax.nn.sigmoid(proj(C_G, IN_WIDTH)).astype(MXU_DTYPE)

    tok = lambda w: pl.BlockSpec((tm, w), lambda i: (i, 0))
    full = lambda a: pl.BlockSpec(a.shape, lambda i: (0,) * a.ndim)
    tab = pl.BlockSpec((tm, LANES), lambda i: (i % nst, 0))
    return _launch(
        body, [x, g1, win_t, b_in, rc, rsa, rsb], name="inproj_fwd", grid=(T // tm,),
        in_specs=[tok(D_MODEL), full(g1), full(win_t), full(b_in), tab, tab, tab],
        out_specs=[tok(D_MODEL), tok(POOL_WIDTH), tok(ATTN_WIDTH), tok(512), tok(512), tok(GATE_WIDTH)],
        out_shape=[jax.ShapeDtypeStruct((T, D_MODEL), MXU_DTYPE), jax.ShapeDtypeStruct((T, POOL_WIDTH), F32),
                   jax.ShapeDtypeStruct((T, ATTN_WIDTH), MXU_DTYPE), jax.ShapeDtypeStruct((T, 512), MXU_DTYPE),
                   jax.ShapeDtypeStruct((T, 512), MXU_DTYPE), jax.ShapeDtypeStruct((T, GATE_WIDTH), MXU_DTYPE)],
        sem=("arbitrary",), rider=rider)


def _shift_rows(a, k, rows):
    n = a.shape[0]
    if k > 0:
        return jnp.where(rows >= k, pltpu.roll(a, k, 0), 0.0)
    return jnp.where(rows < n + k, pltpu.roll(a, n + k, 0), 0.0)


def _win_sum(a, w, rows, sign):
    s, k = a, 1
    while k < w:
        s = s + _shift_rows(s, sign * k, rows)
        k *= 2
    return s


def _pool_diff(ug, w, rows):
    inv = 1.0 / jnp.minimum(rows + 1, w).astype(F32)
    return _win_sum(ug, w, rows, 1) * inv - ug, inv


def _pool_call(u, w_pool, pool_scale, S):
    T = u.shape[0]

    def body(u_ref, w_ref, ps_ref, y_ref):
        rows = lax.broadcasted_iota(jnp.int32, (S, POOL_GC), 0)
        for gi, w in enumerate(POOL_WINDOWS):
            sl = slice(POOL_GC * gi, POOL_GC * (gi + 1))
            diff, _ = _pool_diff(u_ref[:, sl], w, rows)
            mixed = _dot(diff.astype(MXU_DTYPE), w_ref[gi], NN)
            y_ref[:, sl] = (mixed * ps_ref[:, sl]).astype(MXU_DTYPE)

    seq = pl.BlockSpec((S, POOL_WIDTH), lambda b: (b, 0))
    return pl.pallas_call(
        body, name="pool_fwd", grid=(T // S,),
        in_specs=[seq, pl.BlockSpec(w_pool.shape, lambda b: (0, 0, 0)), pl.BlockSpec(pool_scale.shape, lambda b: (0, 0))],
        out_specs=seq, out_shape=jax.ShapeDtypeStruct((T, POOL_WIDTH), MXU_DTYPE),
        compiler_params=_params(("arbitrary",)),
    )(u, w_pool, pool_scale)


def _pool_bwd_call(u, dyp, w_pool, pool_scale, S, rider=None):
    T = u.shape[0]

    def body(u_ref, dy_ref, w_ref, ps_ref, du_ref, dw_ref, dps_ref):
        @pl.when(pl.program_id(0) == 0)
        def _():
            dw_ref[...] = jnp.zeros_like(dw_ref)
            dps_ref[...] = jnp.zeros_like(dps_ref)

        rows = lax.broadcasted_iota(jnp.int32, (S, POOL_GC), 0)
        for gi, w in enumerate(POOL_WINDOWS):
            sl = slice(POOL_GC * gi, POOL_GC * (gi + 1))
            diff, inv = _pool_diff(u_ref[:, sl], w, rows)
            diffb = diff.astype(MXU_DTYPE)
            wg = w_ref[gi]
            mixed = _dot(diffb, wg, NN)
            dy = dy_ref[:, sl]
            dps_ref[:, sl] += jnp.sum(dy * mixed, axis=0, keepdims=True)
            dmb = (dy * ps_ref[:, sl]).astype(MXU_DTYPE)
            dw_ref[gi] += _dot(diffb, dmb, TN)
            ddiff = _dot(dmb, wg, NT)
            du_ref[:, sl] = (_win_sum(ddiff * inv, w, rows, -1) - ddiff).astype(MXU_DTYPE)

    seq = pl.BlockSpec((S, POOL_WIDTH), lambda b: (b, 0))
    return _launch(
        body, [u, dyp, w_pool, pool_scale], name="pool_bwd", grid=(T // S,),
        in_specs=[seq, seq, pl.BlockSpec(w_pool.shape, lambda b: (0, 0, 0)), pl.BlockSpec(pool_scale.shape, lambda b: (0, 0))],
        out_specs=[seq, pl.BlockSpec(w_pool.shape, lambda b: (0, 0, 0)), pl.BlockSpec(pool_scale.shape, lambda b: (0, 0))],
        out_shape=[jax.ShapeDtypeStruct((T, POOL_WIDTH), MXU_DTYPE), jax.ShapeDtypeStruct(w_pool.shape, F32),
                   jax.ShapeDtypeStruct(pool_scale.shape, F32)],
        sem=("arbitrary",), rider=rider)


def _attn_consts():
    lane_g = lax.broadcasted_iota(jnp.int32, (BLOCK, 256), 1) >> 6
    rgrp = lax.broadcasted_iota(jnp.int32, (GROUP * BLOCK, 1), 0) >> 7
    rel = lax.broadcasted_iota(jnp.int32, (BLOCK, 256), 0) - lax.broadcasted_iota(jnp.int32, (BLOCK, 256), 1)

    def bias(off):
        ok = (rel + off >= 0) & (rel + off < BLOCK)
        return jnp.concatenate([jnp.where(ok, 0.0, NEG_INF)] * GROUP, axis=0)

    return lane_g, rgrp, bias(0), bias(BLOCK)


def _sink_rows(sink_ref, hk, rgrp):
    sv = jnp.zeros(rgrp.shape, F32)
    for g in range(GROUP):
        sv = jnp.where(rgrp == g, sink_ref[0, GROUP * hk + g], sv)
    return sv


def _stack_heads(xb, lane_g):
    return jnp.concatenate([jnp.where(lane_g == g, xb, jnp.zeros_like(xb)) for g in range(GROUP)], axis=0)


def _unstack_heads(xs, lane_g):
    out = jnp.where(lane_g == 0, xs[0:BLOCK], 0.0)
    for g in range(1, GROUP):
        out = out + jnp.where(lane_g == g, xs[BLOCK * g:BLOCK * (g + 1)], 0.0)
    return out


def _attn_probs(qs, kb, bias, sv):
    s = _dot(qs, kb, NT) + bias
    m = jnp.maximum(jnp.max(s, axis=1, keepdims=True), sv)
    e = jnp.exp(s - m)
    es = jnp.exp(sv - m)
    inv_l = 1.0 / (jnp.sum(e, axis=1, keepdims=True) + es)
    return e * inv_l, es * inv_l


def _attn_blocks(nb, blk, carry, per=1):
    carry = blk(0, 0, True, carry)
    per = per if (nb - 1) % per == 0 else 1

    def step(i, c):
        for k in range(per):
            n = 1 + per * i + k
            c = blk(pl.multiple_of(n * BLOCK, BLOCK), pl.multiple_of((n - 1) * BLOCK, BLOCK), False, c)
        return c

    return lax.fori_loop(0, (nb - 1) // per, step, carry)


def _attn_call(sinks, q, k4, v4, S, rider=None):
    T = q.shape[0]
    nb = S // BLOCK

    def body(sink_ref, q_ref, k_ref, v_ref, o_ref):
        lane_g, rgrp, bias_first, bias_later = _attn_consts()
        svs = [_sink_rows(sink_ref, hk, rgrp) for hk in range(N_KV_HEADS)]

        def blk(q0, k0, first, carry):
            for hk in range(N_KV_HEADS):
                cs = slice(256 * hk, 256 * (hk + 1))
                qs = _stack_heads(q_ref[pl.ds(q0, BLOCK), cs], lane_g)
                p, _ = _attn_probs(qs, k_ref[pl.ds(k0, 2 * BLOCK), cs], bias_first if first else bias_later, svs[hk])
                o = _dot(p.astype(MXU_DTYPE), v_ref[pl.ds(k0, 2 * BLOCK), cs], NN)
                o_ref[pl.ds(q0, BLOCK), cs] = _unstack_heads(o, lane_g).astype(MXU_DTYPE)
            return carry

        _attn_blocks(nb, blk, 0, per=3)

    seq = pl.BlockSpec((S, ATTN_WIDTH), lambda b: (b, 0))
    return _launch(
        body, [sinks, q, k4, v4], name="attn_fwd", grid=(T // S,),
        in_specs=[pl.BlockSpec(memory_space=pltpu.SMEM), seq, seq, seq],
        out_specs=[seq], out_shape=[jax.ShapeDtypeStruct((T, ATTN_WIDTH), MXU_DTYPE)],
        sem=("arbitrary",), rider=rider)


def _attn_bwd_call(sinks, q, k4, v4, do, rc, rsa, rsb, S, rider=None):
    T = q.shape[0]
    nb = S // BLOCK

    def body(sink_ref, q_ref, k_ref, v_ref, do_ref, c_ref, sa_ref, sb_ref,
             dq_ref, dk_ref, dv_ref, ds_ref, dk_acc, dv_acc):
        lane_g, rgrp, bias_first, bias_later = _attn_consts()
        svs = [_sink_rows(sink_ref, hk, rgrp) for hk in range(N_KV_HEADS)]
        lane1 = lax.broadcasted_iota(jnp.int32, (1, LANES), 1)
        dk_acc[...] = jnp.zeros_like(dk_acc)
        dv_acc[...] = jnp.zeros_like(dv_acc)

        def blk(q0, k0, first, dsink):
            rows = pl.ds(q0, BLOCK)
            c, sa, sb = c_ref[rows, :], sa_ref[rows, :], sb_ref[rows, :]
            for hk in range(N_KV_HEADS):
                cs = slice(256 * hk, 256 * (hk + 1))
                qs = _stack_heads(q_ref[rows, cs], lane_g)
                dos = _stack_heads(do_ref[rows, cs], lane_g)
                kb = k_ref[pl.ds(k0, 2 * BLOCK), cs]
                vb = v_ref[pl.ds(k0, 2 * BLOCK), cs]
                p, ps = _attn_probs(qs, kb, bias_first if first else bias_later, svs[hk])
                dp = _dot(dos, vb, NT)
                delta = jnp.sum(p * dp, axis=1, keepdims=True)
                dsb = (p * (dp - delta)).astype(MXU_DTYPE)
                dqb = _unstack_heads(_dot(dsb, kb, NN), lane_g) * SCALE
                dq_ref[rows, cs] = _rot(dqb, c, -sa, -sb).astype(MXU_DTYPE)
                dk_acc[pl.ds(k0, 2 * BLOCK), cs] += _dot(dsb, qs, TN)
                dv_acc[pl.ds(k0, 2 * BLOCK), cs] += _dot(p.astype(MXU_DTYPE), dos, TN)
                psd = ps * delta
                for g in range(GROUP):
                    val = -jnp.sum(psd[BLOCK * g:BLOCK * (g + 1)], axis=0, keepdims=True)
                    dsink = dsink + jnp.where(lane1 == GROUP * hk + g, val, 0.0)
            return dsink

        dsink = _attn_blocks(nb, blk, jnp.zeros((1, LANES), F32))
        dk_ref[...] = _rot(_fold_heads(dk_acc[...]), c_ref[...], -sa_ref[...], -sb_ref[...]).astype(MXU_DTYPE)
        dv_ref[...] = _fold_heads(dv_acc[...]).astype(MXU_DTYPE)
        ds_ref[...] = jnp.broadcast_to(dsink, ds_ref.shape)

    seq = pl.BlockSpec((S, ATTN_WIDTH), lambda b: (b, 0))
    kvs = pl.BlockSpec((S, KV_WIDTH), lambda b: (b, 0))
    tab = pl.BlockSpec((S, LANES), lambda b: (0, 0))
    nseq = T // S
    return _launch(
        body, [sinks, q, k4, v4, do, rc, rsa, rsb], name="attn_bwd", grid=(nseq,),
        in_specs=[pl.BlockSpec(memory_space=pltpu.SMEM), seq, seq, seq, seq, tab, tab, tab],
        out_specs=[seq, kvs, kvs, pl.BlockSpec((8, LANES), lambda b: (b, 0))],
        out_shape=[jax.ShapeDtypeStruct((T, ATTN_WIDTH), MXU_DTYPE), jax.ShapeDtypeStruct((T, KV_WIDTH), MXU_DTYPE),
                   jax.ShapeDtypeStruct((T, KV_WIDTH), MXU_DTYPE), jax.ShapeDtypeStruct((8 * nseq, LANES), F32)],
        scratch_shapes=[pltpu.VMEM((S, 512), F32), pltpu.VMEM((S, 512), F32)],
        sem=("arbitrary",), rider=rider)


def _branch_weights(wbp_ref, wba_ref, wbp_s, wba_s):
    @pl.when(pl.program_id(0) == 0)
    def _():
        for j in range(N_DEV):
            wbp_s[:, LANES * j:LANES * (j + 1)] = wbp_ref[j]
            wba_s[:, LANES * j:LANES * (j + 1)] = wba_ref[j]


def _mix_fwd_call(yp, ya, g, x, wbp, wba, wout, g2, g3, rider=None):
    T = x.shape[0]
    tm = _tile(T, 512)

    def body(yp_ref, ya_ref, g_ref, x_ref, wbp_ref, wba_ref, wout_ref, g2_ref, g3_ref,
             mix_ref, x1_ref, h2_ref, wbp_s, wba_s):
        _branch_weights(wbp_ref, wba_ref, wbp_s, wba_s)
        bp = _dot(yp_ref[...], wbp_s[...], NN)
        ba = _dot(ya_ref[...], wba_s[...], NN)
        merged = g_ref[:, :D_MODEL].astype(F32) * bp + g_ref[:, D_MODEL:].astype(F32) * ba
        mix = _dot(merged.astype(MXU_DTYPE), wout_ref[...], NN)
        mix_ref[...] = mix
        x1 = x_ref[...] + (mix * _rms_r(mix)) * g2_ref[...]
        x1_ref[...] = x1
        h2_ref[...] = ((x1 * _rms_r(x1)) * g3_ref[...]).astype(MXU_DTYPE)

    tok = lambda w: pl.BlockSpec((tm, w), lambda i: (i, 0))
    full = lambda a: pl.BlockSpec(a.shape, lambda i: (0,) * a.ndim)
    return _launch(
        body, [yp, ya, g, x, wbp, wba, wout, g2, g3], name="mix_fwd", grid=(T // tm,),
        in_specs=[tok(POOL_WIDTH), tok(ATTN_WIDTH), tok(GATE_WIDTH), tok(D_MODEL), full(wbp), full(wba), full(wout),
                  full(g2), full(g3)],
        out_specs=[tok(D_MODEL), tok(D_MODEL), tok(D_MODEL)],
        out_shape=[jax.ShapeDtypeStruct((T, D_MODEL), F32), jax.ShapeDtypeStruct((T, D_MODEL), F32),
                   jax.ShapeDtypeStruct((T, D_MODEL), MXU_DTYPE)],
        scratch_shapes=[pltpu.VMEM((POOL_WIDTH, D_MODEL), MXU_DTYPE), pltpu.VMEM((ATTN_WIDTH, D_MODEL), MXU_DTYPE)],
        sem=("arbitrary",), rider=rider)


def _mix_bwd_call(dx1, mix, yp, ya, g, wbp, wba, wout, g2, rider=None):
    T = dx1.shape[0]
    tm = _tile(T, 512)

    def body(dx1_ref, mix_ref, yp_ref, ya_ref, g_ref, wbp_ref, wba_ref, wout_ref, g2_ref,
             dyp_ref, do_ref, dgates_ref, dg2_ref, dbg_ref, gout_ref, gbp_ref, gba_ref,
             wbp_s, wba_s, acc_out, acc_bp, acc_ba, sem):
        _branch_weights(wbp_ref, wba_ref, wbp_s, wba_s)
        step = pl.program_id(0)

        @pl.when(step == 0)
        def _():
            dg2_ref[...] = jnp.zeros_like(dg2_ref)
            dbg_ref[...] = jnp.zeros_like(dbg_ref)
            acc_out[...] = jnp.zeros_like(acc_out)
            acc_bp[...] = jnp.zeros_like(acc_bp)
            acc_ba[...] = jnp.zeros_like(acc_ba)

        mix = mix_ref[...]
        dmix, dg2 = _rms_bwd(dx1_ref[...], mix, _rms_r(mix), g2_ref[...])
        dg2_ref[...] += jnp.sum(dg2, axis=0, keepdims=True)
        dmixb = dmix.astype(MXU_DTYPE)
        dmerged = _dot(dmixb, wout_ref[...], NT)
        yp, ya = yp_ref[...], ya_ref[...]
        bp = _dot(yp, wbp_s[...], NN)
        ba = _dot(ya, wba_s[...], NN)
        gp, ga = g_ref[:, :D_MODEL].astype(F32), g_ref[:, D_MODEL:].astype(F32)
        acc_out[...] += _dot((gp * bp + ga * ba).astype(MXU_DTYPE), dmixb, TN)
        dgp = dmerged * bp * (gp * (1.0 - gp))
        dga = dmerged * ba * (ga * (1.0 - ga))
        dbg_ref[:, :D_MODEL] += jnp.sum(dgp, axis=0, keepdims=True)
        dbg_ref[:, D_MODEL:] += jnp.sum(dga, axis=0, keepdims=True)
        dgates_ref[:, :D_MODEL] = dgp.astype(MXU_DTYPE)
        dgates_ref[:, D_MODEL:] = dga.astype(MXU_DTYPE)
        dbp = (dmerged * gp).astype(MXU_DTYPE)
        dba = (dmerged * ga).astype(MXU_DTYPE)
        acc_bp[...] += _dot(yp, dbp, TN)
        acc_ba[...] += _dot(ya, dba, TN)
        dyp_ref[...] = _dot(dbp, wbp_s[...], NT)
        do_ref[...] = _dot(dba, wba_s[...], NT).astype(MXU_DTYPE)

        @pl.when(step == pl.num_programs(0) - 1)
        def _():
            copies = [pltpu.make_async_copy(acc_out, gout_ref, sem.at[0])]
            for j in range(N_DEV):
                cols = slice(LANES * j, LANES * (j + 1))
                copies.append(pltpu.make_async_copy(acc_bp.at[:, cols], gbp_ref.at[j], sem.at[1 + j]))
                copies.append(pltpu.make_async_copy(acc_ba.at[:, cols], gba_ref.at[j], sem.at[1 + N_DEV + j]))
            for cp in copies:
                cp.start()
            for cp in copies:
                cp.wait()

    tok = lambda w: pl.BlockSpec((tm, w), lambda i: (i, 0))
    full = lambda a: pl.BlockSpec(a.shape, lambda i: (0,) * a.ndim)
    acc = lambda w: pl.BlockSpec((1, w), lambda i: (0, 0))
    hbm = pl.BlockSpec(memory_space=pl.ANY)
    sd = jax.ShapeDtypeStruct
    return _launch(
        body, [dx1, mix, yp, ya, g, wbp, wba, wout, g2], name="mix_bwd", grid=(T // tm,),
        in_specs=[tok(D_MODEL), tok(D_MODEL), tok(POOL_WIDTH), tok(ATTN_WIDTH), tok(GATE_WIDTH), full(wbp), full(wba),
                  full(wout), full(g2)],
        out_specs=[tok(POOL_WIDTH), tok(ATTN_WIDTH), tok(GATE_WIDTH), acc(D_MODEL), acc(GATE_WIDTH), hbm, hbm, hbm],
        out_shape=[sd((T, POOL_WIDTH), F32), sd((T, ATTN_WIDTH), MXU_DTYPE), sd((T, GATE_WIDTH), MXU_DTYPE),
                   sd((1, D_MODEL), F32), sd((1, GATE_WIDTH), F32), sd((D_MODEL, D_MODEL), F32),
                   sd((N_DEV, POOL_WIDTH, LANES), F32), sd((N_DEV, ATTN_WIDTH, LANES), F32)],
        scratch_shapes=[pltpu.VMEM((POOL_WIDTH, D_MODEL), MXU_DTYPE), pltpu.VMEM((ATTN_WIDTH, D_MODEL), MXU_DTYPE),
                        pltpu.VMEM((D_MODEL, D_MODEL), F32), pltpu.VMEM((POOL_WIDTH, D_MODEL), F32),
                        pltpu.VMEM((ATTN_WIDTH, D_MODEL), F32), pltpu.SemaphoreType.DMA((1 + 2 * N_DEV,))],
        sem=("arbitrary",), rider=rider)


def _mlp_up_call(h2, wup):
    T = h2.shape[0]
    tm = _tile(T, 512)
    fc = D_FF // N_DEV

    def body(h2_ref, wup_ref, act_ref):
        h2 = h2_ref[...]
        for j in range(N_DEV):
            rl = jnp.maximum(_dot(h2, wup_ref[j], NN), 0.0)
            act_ref[:, fc * j:fc * (j + 1)] = (rl * rl).astype(MXU_DTYPE)

    sd = jax.ShapeDtypeStruct
    return pl.pallas_call(
        body, name="mlp_up", grid=(T // tm,),
        in_specs=[pl.BlockSpec((tm, D_MODEL), lambda i: (i, 0)),
                  pl.BlockSpec(wup.shape, lambda i: (0, 0, 0), pipeline_mode=pl.Buffered(1))],
        out_specs=pl.BlockSpec((tm, D_FF), lambda i: (i, 0)), out_shape=sd((T, D_FF), MXU_DTYPE),
        compiler_params=_params(("arbitrary",)),
    )(h2, wup)


def _mlp_call(x1, act, target, wup, wdown, g3, g4):
    T = x1.shape[0]
    tm = _tile(T, 256)
    fc = D_FF // N_DEV

    def body(x1_ref, act_ref, t_ref, wup_ref, wdown_ref, g3_ref, g4_ref,
             da_ref, dff_ref, dx1_ref, dg3_ref, dg4_ref, loss_ref):
        @pl.when(pl.program_id(0) == 0)
        def _():
            dg3_ref[...] = jnp.zeros_like(dg3_ref)
            dg4_ref[...] = jnp.zeros_like(dg4_ref)
            loss_ref[...] = jnp.zeros_like(loss_ref)

        ff = jnp.zeros((tm, D_MODEL), F32)
        for j in range(N_DEV):
            ff = ff + _dot(act_ref[:, fc * j:fc * (j + 1)], wdown_ref[j], NN)
        x1 = x1_ref[...]
        r4 = _rms_r(ff)
        err = x1 + (ff * r4) * g4_ref[...] - t_ref[...]
        loss_ref[...] += jnp.sum(err * err, axis=0, keepdims=True)
        dy = err * (1.0 / D_MODEL)
        dff, dg4 = _rms_bwd(dy, ff, r4, g4_ref[...])
        dg4_ref[...] += jnp.sum(dg4, axis=0, keepdims=True)
        dffb = dff.astype(MXU_DTYPE)
        dff_ref[...] = dffb
        dh2 = jnp.zeros((tm, D_MODEL), F32)
        for j in range(N_DEV):
            sl = slice(fc * j, fc * (j + 1))
            rl = jnp.sqrt(act_ref[:, sl].astype(F32))
            dab = (_dot(dffb, wdown_ref[j], NT) * (2.0 * rl)).astype(MXU_DTYPE)
            da_ref[:, sl] = dab
            dh2 = dh2 + _dot(dab, wup_ref[j], NT)
        dx1, dg3 = _rms_bwd(dh2, x1, _rms_r(x1), g3_ref[...])
        dg3_ref[...] += jnp.sum(dg3, axis=0, keepdims=True)
        dx1_ref[...] = dy + dx1

    tok = lambda w: pl.BlockSpec((tm, w), lambda i: (i, 0))
    full = lambda a: pl.BlockSpec(a.shape, lambda i: (0,) * a.ndim, pipeline_mode=pl.Buffered(1))
    vec = pl.BlockSpec((1, D_MODEL), lambda i: (0, 0))
    sd = jax.ShapeDtypeStruct
    return pl.pallas_call(
        body, name="mlp_down_bwd", grid=(T // tm,),
        in_specs=[tok(D_MODEL), tok(D_FF), tok(D_MODEL), full(wup), full(wdown), vec, vec],
        out_specs=[tok(D_FF), tok(D_MODEL), tok(D_MODEL), vec, vec, vec],
        out_shape=[sd((T, D_FF), MXU_DTYPE), sd((T, D_MODEL), MXU_DTYPE),
                   sd((T, D_MODEL), F32), sd((1, D_MODEL), F32), sd((1, D_MODEL), F32), sd((1, D_MODEL), F32)],
        compiler_params=_params(("arbitrary",)),
    )(x1, act, target, wup, wdown, g3, g4)


def _inproj_bwd_call(du, dq, dk, dv, dgates, dx1, x, win_t, g1, rider=None):
    T = x.shape[0]
    tm = _tile(T, 512)

    def body(du_ref, dq_ref, dk_ref, dv_ref, dgt_ref, dx1_ref, x_ref, w_ref, g1_ref, gx_ref, dg1_ref, db_ref):
        @pl.when(pl.program_id(0) == 0)
        def _():
            dg1_ref[...] = jnp.zeros_like(dg1_ref)
            db_ref[...] = jnp.zeros_like(db_ref)

        dh = jnp.zeros((tm, D_MODEL), F32)
        for ref, lo, hi in ((du_ref, 0, C_Q), (dq_ref, C_Q, C_K), (dk_ref, C_K, C_V), (dv_ref, C_V, C_G),
                            (dgt_ref, C_G, IN_WIDTH)):
            piece = ref[...]
            dh = dh + _dot(piece, w_ref[lo:hi, :], NN)
            if hi <= C_G:
                db_ref[:, lo:hi] += jnp.sum(piece.astype(F32), axis=0, keepdims=True)
        xv = x_ref[...]
        dx, dg1 = _rms_bwd(dh, xv, _rms_r(xv), g1_ref[...])
        dg1_ref[...] += jnp.sum(dg1, axis=0, keepdims=True)
        gx_ref[...] = dx1_ref[...] + dx

    tok = lambda w: pl.BlockSpec((tm, w), lambda i: (i, 0))
    full = lambda a: pl.BlockSpec(a.shape, lambda i: (0,) * a.ndim)
    sd = jax.ShapeDtypeStruct
    return _launch(
        body, [du, dq, dk, dv, dgates, dx1, x, win_t, g1], name="inproj_bwd", grid=(T // tm,),
        in_specs=[tok(POOL_WIDTH), tok(ATTN_WIDTH), tok(KV_WIDTH), tok(KV_WIDTH), tok(GATE_WIDTH), tok(D_MODEL),
                  tok(D_MODEL), full(win_t), full(g1)],
        out_specs=[tok(D_MODEL), pl.BlockSpec((1, D_MODEL), lambda i: (0, 0)), pl.BlockSpec((1, C_G), lambda i: (0, 0))],
        out_shape=[sd((T, D_MODEL), F32), sd((1, D_MODEL), F32), sd((1, C_G), F32)],
        sem=("arbitrary",), rider=rider)


WGRAD_TOKENS = 1024


def _wgrad_rows_call(a, b, name, rider=None):
    T, K = a.shape
    N = b.shape[1]
    tm = _tile(T, WGRAD_TOKENS)
    kb = min(K, 1024)
    per = kb // (K // N_DEV)

    def body(a_ref, b_ref, o_ref):
        @pl.when(pl.program_id(1) == 0)
        def _():
            o_ref[...] = jnp.zeros_like(o_ref)

        d = _dot(a_ref[...], b_ref[...], TN)
        rs = kb // per
        for j in range(per):
            o_ref[j] += d[rs * j:rs * (j + 1)]

    return _launch(
        body, [a, b], name=name, grid=(K // kb, T // tm),
        in_specs=[pl.BlockSpec((tm, kb), lambda i, t: (t, i)), pl.BlockSpec((tm, N), lambda i, t: (t, 0))],
        out_specs=[pl.BlockSpec((per, K // N_DEV, N), lambda i, t: (i, 0, 0))],
        out_shape=[jax.ShapeDtypeStruct((N_DEV, K // N_DEV, N), F32)],
        sem=("arbitrary", "arbitrary"), rider=rider)


def _wgrad_cols_call(a, b, name, rider=None):
    T, K = a.shape
    N = b.shape[1]
    tm = _tile(T, WGRAD_TOKENS)
    nb = min(N, 1024)
    per = nb // (N // N_DEV)

    def body(a_ref, b_ref, o_ref):
        @pl.when(pl.program_id(1) == 0)
        def _():
            o_ref[...] = jnp.zeros_like(o_ref)

        d = _dot(a_ref[...], b_ref[...], TN)
        cs = nb // per
        for j in range(per):
            o_ref[j] += d[:, cs * j:cs * (j + 1)]

    return _launch(
        body, [a, b], name=name, grid=(N // nb, T // tm),
        in_specs=[pl.BlockSpec((tm, K), lambda i, t: (t, 0)), pl.BlockSpec((tm, nb), lambda i, t: (t, i))],
        out_specs=[pl.BlockSpec((per, K, N // N_DEV), lambda i, t: (i, 0, 0))],
        out_shape=[jax.ShapeDtypeStruct((N_DEV, K, N // N_DEV), F32)],
        sem=("arbitrary", "arbitrary"), rider=rider)


def _wgrad_in_call(du, dq, dk, dv, dgates, h, rider=None):
    T = h.shape[0]
    tm = _tile(T, WGRAD_TOKENS)
    rows = IN_WIDTH // N_DEV

    def body(du_ref, dq_ref, dk_ref, dv_ref, dgt_ref, h_ref, o_ref, acc, sem):
        t = pl.program_id(0)

        @pl.when(t == 0)
        def _():
            acc[...] = jnp.zeros_like(acc)

        hv = h_ref[...]
        for ref, lo, hi in ((du_ref, 0, C_Q), (dq_ref, C_Q, C_K), (dk_ref, C_K, C_V), (dv_ref, C_V, C_G),
                            (dgt_ref, C_G, IN_WIDTH)):
            acc[lo:hi, :] += _dot(ref[...], hv, TN)

        @pl.when(t == pl.num_programs(0) - 1)
        def _():
            copies = [pltpu.make_async_copy(acc.at[pl.ds(rows * j, rows), :], o_ref.at[j], sem.at[j])
                      for j in range(N_DEV)]
            for cp in copies:
                cp.start()
            for cp in copies:
                cp.wait()

    tok = lambda w: pl.BlockSpec((tm, w), lambda t: (t, 0))
    return _launch(
        body, [du, dq, dk, dv, dgates, h], name="wgrad_in", grid=(T // tm,),
        in_specs=[tok(POOL_WIDTH), tok(ATTN_WIDTH), tok(KV_WIDTH), tok(KV_WIDTH), tok(GATE_WIDTH), tok(D_MODEL)],
        out_specs=[pl.BlockSpec(memory_space=pl.ANY)],
        out_shape=[jax.ShapeDtypeStruct((N_DEV, rows, D_MODEL), F32)],
        scratch_shapes=[pltpu.VMEM((IN_WIDTH, D_MODEL), F32), pltpu.SemaphoreType.DMA((N_DEV,))],
        sem=("arbitrary",), rider=rider)


def _coords():
    return lax.axis_index("x"), lax.axis_index("y"), lax.axis_index("c")


def _ag_route():
    x, y, c = _coords()
    return (x, y, c), (x, y, 1 - c), (x ^ (1 - c), y ^ c, c), (x ^ c, y ^ (1 - c), c), (1 - x, 1 - y, c)


def _rider_ag_first(shard, me):
    def plan(ins, outs, send, recv, loc, r0, l0):
        own, *peers = _ag_route()
        return [pltpu.make_async_remote_copy(
            src_ref=ins[0], dst_ref=outs[0].at[_slot(own)], send_sem=send.at[r0 + k], recv_sem=recv.at[r0 + k],
            device_id=peers[k], device_id_type=MESH) for k in range(3)], []

    return _Rider([shard], [jax.ShapeDtypeStruct((N_DEV,) + shard.shape, shard.dtype)], 3, 0, plan,
                  lands=[_gather_buffer(shard, me)])


def _rider_ag_onward(shard, stage):
    def plan(ins, outs, send, recv, loc, r0, l0):
        own, sibling, near1, near2, diag = _ag_route()
        moves = [(near1, near2), (near1, sibling), (near2, sibling)] if stage == 2 else [(diag, sibling)]
        copies = []
        for k, (block, to) in enumerate(moves):
            part = outs[0].at[_slot(block)]
            copies.append(pltpu.make_async_remote_copy(src_ref=part, dst_ref=part, send_sem=send.at[r0 + k],
                                                       recv_sem=recv.at[r0 + k], device_id=to, device_id_type=MESH))
        return copies, []

    return _Rider([shard], [jax.ShapeDtypeStruct((N_DEV,) + shard.shape, shard.dtype)], 3 if stage == 2 else 1, 0, plan)


def _slot(p):
    return 4 * p[0] + 2 * p[1] + p[2]


def _rows(ref, span):
    return ref if span is None else ref.at[pl.ds(span[0], span[1])]


ALL = "all"
LOCAL = "local"


def _rows(ref, span):
    return ref if span == ALL else ref.at[pl.ds(span[0], span[1])]


def _rider_ag(items):
    ins, out_shape, aliases, where = [], [], {}, []
    n_remote = n_local = 0
    for t, (shard, buf, snd, fwd) in enumerate(items):
        i_shard = i_buf = None
        if snd is not None:
            i_shard = len(ins)
            ins.append(shard)
        if buf is not None:
            i_buf = len(ins)
            ins.append(buf)
            aliases[i_buf] = t
            out_shape.append(jax.ShapeDtypeStruct(buf.shape, buf.dtype))
        else:
            assert fwd is None and snd is not None
            out_shape.append(jax.ShapeDtypeStruct((N_DEV,) + shard.shape, shard.dtype))
        where.append((i_shard, i_buf, n_remote, n_local))
        n_remote += (4 if snd not in (None, LOCAL) else 0) + (3 if fwd is not None else 0)
        n_local += 1 if snd is not None else 0

    def plan(rins, routs, send, recv, loc, r0, l0):
        x, y, c = _coords()
        peers = [(x, y, 1 - c), (1 - x, y, c), (x, 1 - y, c), (1 - x, 1 - y, c)]
        remote, local = [], []
        for t, (shard, buf, snd, fwd) in enumerate(items):
            i_shard, i_buf, k, l = where[t]
            k, l = r0 + k, l0 + l
            if snd is not None:
                span = ALL if snd == LOCAL else snd
                src, dst = _rows(rins[i_shard], span), _rows(routs[t].at[_slot((x, y, c))], span)
                local.append(pltpu.make_async_copy(src, dst, loc.at[l]))
                for peer in (peers if snd != LOCAL else []):
                    remote.append(pltpu.make_async_remote_copy(
                        src_ref=src, dst_ref=dst, send_sem=send.at[k], recv_sem=recv.at[k],
                        device_id=peer, device_id_type=MESH))
                    k += 1
            if fwd is not None:
                for px, py, pc in peers[1:]:
                    s = _slot((px, py, pc))
                    remote.append(pltpu.make_async_remote_copy(
                        src_ref=_rows(rins[i_buf].at[s], fwd), dst_ref=_rows(routs[t].at[s], fwd),
                        send_sem=send.at[k], recv_sem=recv.at[k], device_id=peers[0], device_id_type=MESH))
                    k += 1
        return remote, local

    return _Rider(ins, out_shape, n_remote, n_local, plan, aliases)


def _gather_buffer(shard, me):
    return lax.dynamic_update_slice(lax.empty((N_DEV,) + shard.shape, shard.dtype), shard[None], (me, 0, 0))


def _rider_ag_remote(shards, me):
    n = len(shards)

    def plan(ins, outs, send, recv, loc, r0, l0):
        x, y, c = _coords()
        remote = []
        for t in range(n):
            dst = outs[t].at[_slot((x, y, c))]
            for k, peer in enumerate([(x, y, 1 - c), (1 - x, y, c), (x, 1 - y, c), (1 - x, 1 - y, c)]):
                remote.append(pltpu.make_async_remote_copy(
                    src_ref=ins[t], dst_ref=dst, send_sem=send.at[r0 + 4 * t + k], recv_sem=recv.at[r0 + 4 * t + k],
                    device_id=peer, device_id_type=MESH))
        return remote, []

    return _Rider(shards, [jax.ShapeDtypeStruct((N_DEV,) + s.shape, s.dtype) for s in shards], 4 * n, 0, plan,
                  lands=[_gather_buffer(s, me) for s in shards])


def _rider_rs_sibling(grads):
    n = len(grads)

    def plan(ins, outs, send, recv, loc, r0, l0):
        x, y, c = _coords()
        remote = []
        for t in range(n):
            for q in range(4):
                remote.append(pltpu.make_async_remote_copy(
                    src_ref=ins[t].at[q, 1 - c], dst_ref=outs[t].at[q], send_sem=send.at[r0 + 4 * t + q],
                    recv_sem=recv.at[r0 + 4 * t + q], device_id=(x, y, 1 - c), device_id_type=MESH))
        return remote, []

    return _Rider(grads, [jax.ShapeDtypeStruct((4,) + g.shape[2:], g.dtype) for g in grads], 4 * n, 0, plan)


def _rider_rs_chips(sums, rows=None, into=None):
    n = len(sums)
    rows = rows or [ALL] * n

    def plan(ins, outs, send, recv, loc, r0, l0):
        x, y, c = _coords()
        remote = []
        for t in range(n):
            for r, (px, py) in enumerate([(1 - x, y), (x, 1 - y), (1 - x, 1 - y)]):
                remote.append(pltpu.make_async_remote_copy(
                    src_ref=_rows(ins[t].at[2 * px + py], rows[t]), dst_ref=_rows(outs[t].at[r], rows[t]),
                    send_sem=send.at[r0 + 3 * t + r], recv_sem=recv.at[r0 + 3 * t + r],
                    device_id=(px, py, c), device_id_type=MESH))
        return remote, []

    out_shape = [jax.ShapeDtypeStruct((3,) + s.shape[1:], s.dtype) for s in sums]
    if into is None:
        return _Rider(sums, out_shape, 3 * n, 0, plan)
    return _Rider(list(sums) + list(into), out_shape, 3 * n, 0, plan, aliases={n + t: t for t in range(n)})


def _rider_gather_remote(parts):
    n = len(parts)

    def plan(ins, outs, send, recv, loc, r0, l0):
        x, y, c = _coords()
        me = _slot((x, y, c))
        remote = []
        for t in range(n):
            for k in range(1, N_DEV):
                peer = (x ^ ((k >> 2) & 1), y ^ ((k >> 1) & 1), c ^ (k & 1))
                remote.append(pltpu.make_async_remote_copy(
                    src_ref=ins[t], dst_ref=outs[t].at[me], send_sem=send.at[r0 + 7 * t + k - 1],
                    recv_sem=recv.at[r0 + 7 * t + k - 1], device_id=peer, device_id_type=MESH))
        return remote, []

    return _Rider(parts, [jax.ShapeDtypeStruct((N_DEV,) + p.shape, p.dtype) for p in parts], 7 * n, 0, plan)


def _chip_sum_call(idx, grads, recvd, out_dtypes, name):
    n = len(grads)

    def body(i_ref, *refs):
        for t in range(n):
            refs[2 * n + t][0] = (refs[t][0, 0] + refs[n + t][0]).astype(out_dtypes[t])

    def chip(k, s):
        return jnp.where(k >= s[0], k + 1, k)

    in_specs = [pl.BlockSpec((1, 1) + g.shape[2:], lambda k, s: (chip(k, s), s[1], 0, 0)) for g in grads]
    in_specs += [pl.BlockSpec((1,) + r.shape[1:], lambda k, s: (chip(k, s), 0, 0)) for r in recvd]
    return pl.pallas_call(
        body, name=name,
        grid_spec=pltpu.PrefetchScalarGridSpec(
            num_scalar_prefetch=1, grid=(3,), in_specs=in_specs,
            out_specs=[pl.BlockSpec((1,) + r.shape[1:], lambda k, s: (chip(k, s), 0, 0)) for r in recvd]),
        out_shape=[jax.ShapeDtypeStruct(r.shape, dt) for r, dt in zip(recvd, out_dtypes)],
        compiler_params=_params(("arbitrary",)),
    )(idx, *grads, *recvd)


def _final_sum_call(idx, grads, recvd1, recvd2):
    n = len(grads)
    nsteps = 2

    def body(i_ref, *refs):
        for t in range(n):
            g, r1, r2, o = refs[t], refs[n + t], refs[2 * n + t], refs[3 * n + t]
            s = g[0, 0] + r1[0]
            for r in range(3):
                s = s + r2[r].astype(F32)
            o[...] = s

    def rows(a):
        r = a.shape[-2]
        return r // nsteps if (r // nsteps) % 16 == 0 else r

    def step(a):
        return (lambda i: i) if rows(a) != a.shape[-2] else (lambda i: 0)

    in_specs = [pl.BlockSpec((1, 1, rows(g), g.shape[3]), lambda i, s, st=step(g): (s[0], s[1], st(i), 0)) for g in grads]
    in_specs += [pl.BlockSpec((1, rows(r), r.shape[2]), lambda i, s, st=step(r): (s[0], st(i), 0)) for r in recvd1]
    in_specs += [pl.BlockSpec((3, rows(r), r.shape[2]), lambda i, s, st=step(r): (0, st(i), 0)) for r in recvd2]
    return pl.pallas_call(
        body, name="rs_final_sum",
        grid_spec=pltpu.PrefetchScalarGridSpec(
            num_scalar_prefetch=1, grid=(nsteps,), in_specs=in_specs,
            out_specs=[pl.BlockSpec((rows(r), r.shape[2]), lambda i, s, st=step(r): (st(i), 0)) for r in recvd2]),
        out_shape=[jax.ShapeDtypeStruct(r.shape[1:], F32) for r in recvd2],
        compiler_params=_params(("arbitrary",)),
    )(idx, *grads, *recvd1, *recvd2)


def _sum8_call(parts):
    def body(p_ref, o_ref):
        s = p_ref[0]
        for j in range(1, N_DEV):
            s = s + p_ref[j]
        o_ref[...] = s

    return pl.pallas_call(body, name="sum_small_partials",
                          out_shape=jax.ShapeDtypeStruct(parts.shape[1:], parts.dtype))(parts)


def _adamw(w, g, m, v):
    m = ADAM_B1 * m + (1.0 - ADAM_B1) * g
    v = ADAM_B2 * v + (1.0 - ADAM_B2) * (g * g)
    m_hat = m / (1.0 - ADAM_B1 ** ADAM_STEP)
    v_hat = v / (1.0 - ADAM_B2 ** ADAM_STEP)
    delta = -ADAM_LR * (m_hat / (jnp.sqrt(v_hat) + ADAM_EPS) + ADAM_WD * w)
    return delta, m, v


def _adamw_call(ws, gs, ms, vs, nsteps, name):
    n = len(ws)

    def body(*refs):
        for t in range(n):
            w, g, m, v = (refs[k * n + t][...] for k in range(4))
            d, m2, v2 = _adamw(w, g, m, v)
            refs[4 * n + t][...] = d
            refs[5 * n + t][...] = m2
            refs[6 * n + t][...] = v2

    def spec(a):
        assert a.shape[0] % nsteps == 0 and (nsteps == 1 or (a.shape[0] // nsteps) % 8 == 0), a.shape
        return pl.BlockSpec((a.shape[0] // nsteps, a.shape[1]), lambda i: (i, 0))

    specs = [spec(a) for a in ws]
    outs = pl.pallas_call(
        body, name=name, grid=(nsteps,),
        in_specs=specs * 4, out_specs=specs * 3,
        out_shape=[jax.ShapeDtypeStruct(a.shape, F32) for a in ws] * 3,
        compiler_params=_params(("arbitrary",)),
    )(*ws, *gs, *ms, *vs)
    return outs[:n], outs[n:2 * n], outs[2 * n:]


def _adamw_rs_call(idx, after, gws, r1s, r2s, ws, ms, vs, nsteps, name):
    n = len(ws)

    def body(i_ref, after_ref, *refs):
        for t in range(n):
            gw, r1, r2, w, m, v = (refs[k * n + t] for k in range(6))
            g = gw[0, 0] + r1[0]
            for r in range(3):
                g = g + r2[r].astype(F32)
            d, m2, v2 = _adamw(w[...], g, m[...], v[...])
            refs[6 * n + t][...] = g
            refs[7 * n + t][...] = d
            refs[8 * n + t][...] = m2
            refs[9 * n + t][...] = v2

    def rb(a):
        r = a.shape[0] // nsteps
        assert a.shape[0] % nsteps == 0 and r % 16 == 0, a.shape
        return r

    in_specs = [pl.BlockSpec((1, 1, rb(w), w.shape[1]), lambda i, s: (s[0], s[1], i, 0)) for w in ws]
    in_specs += [pl.BlockSpec((1, rb(w), w.shape[1]), lambda i, s: (s[0], i, 0)) for w in ws]
    in_specs += [pl.BlockSpec((3, rb(w), w.shape[1]), lambda i, s: (0, i, 0)) for w in ws]
    plain = [pl.BlockSpec((rb(w), w.shape[1]), lambda i, s: (i, 0)) for w in ws]
    outs = pl.pallas_call(
        body, name=name,
        grid_spec=pltpu.PrefetchScalarGridSpec(
            num_scalar_prefetch=1, grid=(nsteps,),
            in_specs=[pl.BlockSpec(memory_space=pl.ANY)] + in_specs + plain * 3, out_specs=plain * 4),
        out_shape=[jax.ShapeDtypeStruct(w.shape, F32) for w in ws] * 4,
        compiler_params=_params(("arbitrary",)),
    )(idx, after, *gws, *r1s, *r2s, *ws, *ms, *vs)
    return outs[:n], outs[n:2 * n], outs[2 * n:3 * n], outs[3 * n:]


def _rows128(a, pad_rows):
    flat = a.reshape(-1).astype(F32)
    flat = jnp.pad(flat, (0, pad_rows * LANES - flat.shape[0]))
    return flat.reshape(pad_rows, LANES)


_SMALL_A = (("w_pool", 512), ("pool_scale", 8), ("attn_sinks", 8), ("g_mix_post", 8), ("g_mlp_pre", 8),
            ("g_mlp_post", 8), ("loss", 8), ("b_in_gates", 16))
_SMALL_A_ROWS = 640
_SMALL_B = (("g_mix_pre", 8), ("b_in_head", 16))


def _pack(parts, layout, total_rows):
    rows = [_rows128(parts[k], r) for k, r in layout]
    pad = total_rows - sum(r for _, r in layout)
    if pad:
        rows.append(jnp.zeros((pad, LANES), F32))
    return jnp.concatenate(rows, axis=0)


def _unpack(buf, layout, sizes):
    out, off = {}, 0
    for k, r in layout:
        out[k] = buf[off:off + r].reshape(-1)[:sizes[k]]
        off += r
    return out


def kernel(x, g_mix_pre, w_in, b_in, w_pool, pool_scale, attn_sinks, w_branch_pool, w_branch_attn, w_out, g_mix_post, g_mlp_pre, w_up, w_down, g_mlp_post, loss_target, m_g_mix_pre, m_w_in, m_b_in, m_w_pool, m_pool_scale, m_attn_sinks, m_w_branch_pool, m_w_branch_attn, m_w_out, m_g_mix_post, m_g_mlp_pre, m_w_up, m_w_down, m_g_mlp_post, v_g_mix_pre, v_w_in, v_b_in, v_w_pool, v_pool_scale, v_attn_sinks, v_w_branch_pool, v_w_branch_attn, v_w_out, v_g_mix_post, v_g_mlp_pre, v_w_up, v_w_down, v_g_mlp_post):
    B, S, _ = x.shape
    T = B * S
    xt = x.reshape(T, D_MODEL)
    tgt = loss_target.reshape(T, D_MODEL)
    cx, cy, cc = _coords()

    cidx = jnp.stack([2 * cx + cy, cc]).astype(jnp.int32)
    by_chip = lambda gr: gr.reshape((4, 2) + gr.shape[1:])
    bf = lambda w: w[0].astype(MXU_DTYPE)

    me = _slot((cx, cy, cc))
    win_l = w_in[0].T.astype(MXU_DTYPE)
    (c_win,), _ = _copies_start([_rider_ag_first(win_l, me)], "allgather_first")
    wpool_b = bf(w_pool)
    rc, rsa, rsb = _rot_tables(S)
    wbp_l, wba_l, wout_l, wup_l, wdown_l = bf(w_branch_pool), bf(w_branch_attn), bf(w_out), bf(w_up), bf(w_down)
    gathers = [_rider_ag_remote([wbp_l, wba_l, wout_l], me), _rider_ag_remote([wup_l], me), _rider_ag_remote([wdown_l], me)]
    c_win = _copies_pass([c_win], [_rider_ag_onward(win_l, 2)], [b for r in gathers for b in r.lands] + [rc, rsa, rsb],
                         "allgather_second")
    c_win = _copies_pass(c_win, [_rider_ag_onward(win_l, 3)], [wbp_l, wba_l, wout_l, wup_l, wdown_l], "allgather_third")
    (win_s,) = _copies_wait(c_win, wpool_b, "allgather_weights")
    win_t = win_s.reshape(IN_WIDTH, D_MODEL)

    (c_br, c_up, c_dn), tok = _copies_start(gathers, "allgather_start", after=win_s)
    (h, u, q, k4, v4, g), _ = _inproj_call(xt, g_mix_pre, win_t, b_in, rc, rsa, rsb, S, rider=_after(tok))
    yp = _pool_call(u, wpool_b, pool_scale, S)
    wbp_1, wba_1, wout_1 = _copies_wait([c_br], yp, "allgather_wait_branch")
    (ya,), (wbp_s, wba_s, wout_s) = _attn_call(
        attn_sinks, q, k4, v4, S,
        rider=_rider_ag([(None, wbp_1, None, ALL), (None, wba_1, None, ALL), (None, wout_1, None, ALL)]))
    wout_f = wout_s.reshape(D_MODEL, D_MODEL)
    (wup_1,) = _copies_wait([c_up], ya, "allgather_wait_up")
    (mix, x1, h2), (wup_s,) = _mix_fwd_call(
        yp, ya, g, xt, wbp_s, wba_s, wout_f, g_mix_post, g_mlp_pre, rider=_rider_ag([(None, wup_1, None, ALL)]))
    act = _mlp_up_call(h2, wup_s)
    (wdown_1,) = _copies_wait([c_dn], act, "allgather_wait_down")
    (wdown_s,) = _comm_call(_rider_ag([(None, wdown_1, None, ALL)]), "allgather_pass_down")

    da, dff, dx1, dg3, dg4, lossvec = _mlp_call(x1, act, tgt, wup_s, wdown_s, g_mlp_pre, g_mlp_post)
    gw_up = by_chip(_wgrad_cols_call(h2, da, "wgrad_up")[0])
    (gw_down,), (r1_up,) = _wgrad_rows_call(act, dff, "wgrad_down", rider=_rider_rs_sibling([gw_up]))
    gw_down = by_chip(gw_down)
    (s_up,) = _chip_sum_call(cidx, [gw_up], [r1_up], [MXU_DTYPE], "rs_chip_sum_up")
    (c_up,), tok = _copies_start([_rider_rs_chips([s_up])], "rs_chips_start_up")
    (dyp, do, dgates, dg2, dbg, gw_out, gw_bp, gw_ba), _ = _mix_bwd_call(
        dx1, mix, yp, ya, g, wbp_s, wba_s, wout_f, g_mix_post, rider=_after(tok))
    gw_out = by_chip(gw_out.reshape(N_DEV, D_MODEL // N_DEV, D_MODEL))
    gw_bp, gw_ba = by_chip(gw_bp), by_chip(gw_ba)
    (dq, dk, dv, dsink), (r1_down, r1_out, r1_bp, r1_ba) = _attn_bwd_call(
        attn_sinks, q, k4, v4, do, rc, rsa, rsb, S, rider=_rider_rs_sibling([gw_down, gw_out, gw_bp, gw_ba]))
    s_down, *s_obb = _chip_sum_call(cidx, [gw_down, gw_out, gw_bp, gw_ba], [r1_down, r1_out, r1_bp, r1_ba],
                                    [MXU_DTYPE] * 4, "rs_chip_sum_branch")
    (c_obb,), tok = _copies_start([_rider_rs_chips([s_down] + s_obb)], "rs_chips_start_branch")
    (du, dwp, dps), _ = _pool_bwd_call(u, dyp, wpool_b, pool_scale, S, rider=_after(tok))
    (gw_in,) = _wgrad_in_call(du, dq, dk, dv, dgates, h)
    gw_in = by_chip(gw_in)

    small_a = {"w_pool": dwp, "pool_scale": dps,
               "attn_sinks": jnp.sum(dsink.reshape(B, 8, LANES)[:, 0, :N_Q_HEADS], axis=0), "g_mix_post": dg2,
               "g_mlp_pre": dg3, "g_mlp_post": dg4, "loss": lossvec, "b_in_gates": dbg}
    gw_sa = by_chip(_pack(small_a, _SMALL_A, _SMALL_A_ROWS).reshape(N_DEV, _SMALL_A_ROWS // N_DEV, LANES))
    r1_in, r1_sa = _comm_call(_rider_rs_sibling([gw_in, gw_sa]), "rs_sibling_in")
    s_in, s_sa = _chip_sum_call(cidx, [gw_in, gw_sa], [r1_in, r1_sa], [MXU_DTYPE, F32], "rs_chip_sum_in")
    (c_in,), tok = _copies_start([_rider_rs_chips([s_in, s_sa])], "rs_chips_start_in")
    (gx, dg1, dba_in), _ = _inproj_bwd_call(du, dq, dk, dv, dgates, dx1, xt, win_t, g_mix_pre, rider=_after(tok))
    r2_up, r2_down, r2_out, r2_bp, r2_ba, r2_in, r2_sa = _copies_wait([c_up, c_obb, c_in], dg1, "rs_chips_wait")

    (g_sa,) = _final_sum_call(cidx, [gw_sa], [r1_sa], [r2_sa])
    part_b = _pack({"g_mix_pre": dg1, "b_in_head": dba_in}, _SMALL_B, sum(r for _, r in _SMALL_B))
    (c_small,), tok = _copies_start([_rider_gather_remote([g_sa, part_b])], "allgather_small_start")

    in_t = _adamw_rs_call(cidx, tok, [gw_in], [r1_in], [r2_in], [w_in[0].T], [m_w_in[0].T], [v_w_in[0].T], 2,
                          "adamw_w_in")
    rest = _adamw_rs_call(
        cidx, tok, [gw_bp, gw_ba, gw_out, gw_up, gw_down], [r1_bp, r1_ba, r1_out, r1_up, r1_down],
        [r2_bp, r2_ba, r2_out, r2_up, r2_down], [w_branch_pool[0], w_branch_attn[0], w_out[0], w_up[0], w_down[0]],
        [m_w_branch_pool[0], m_w_branch_attn[0], m_w_out[0], m_w_up[0], m_w_down[0]],
        [v_w_branch_pool[0], v_w_branch_attn[0], v_w_out[0], v_w_up[0], v_w_down[0]], N_DEV, "adamw_shards")
    big_g, big_d, big_m2, big_v2 = ([a[0].T] + list(b) for a, b in zip(in_t, rest))

    sa_all, sb_all = _copies_wait([c_small], rest[0][0], "allgather_small_wait")
    sa_all = lax.dynamic_update_slice(sa_all, g_sa[None], (me, 0, 0))
    sb_sum = _sum8_call(lax.dynamic_update_slice(sb_all, part_b[None], (me, 0, 0)))

    names = ["g_mix_pre", "b_in", "w_pool", "pool_scale", "attn_sinks", "g_mix_post", "g_mlp_pre", "g_mlp_post"]
    sm_w = dict(g_mix_pre=g_mix_pre, b_in=b_in, w_pool=w_pool, pool_scale=pool_scale, attn_sinks=attn_sinks,
                g_mix_post=g_mix_post, g_mlp_pre=g_mlp_pre, g_mlp_post=g_mlp_post)
    sm_m = dict(g_mix_pre=m_g_mix_pre, b_in=m_b_in, w_pool=m_w_pool, pool_scale=m_pool_scale, attn_sinks=m_attn_sinks,
                g_mix_post=m_g_mix_post, g_mlp_pre=m_g_mlp_pre, g_mlp_post=m_g_mlp_post)
    sm_v = dict(g_mix_pre=v_g_mix_pre, b_in=v_b_in, w_pool=v_w_pool, pool_scale=v_pool_scale, attn_sinks=v_attn_sinks,
                g_mix_post=v_g_mix_post, g_mlp_pre=v_g_mlp_pre, g_mlp_post=v_g_mlp_post)
    sizes = {k: sm_w[k].size for k in names}
    sizes.update(loss=D_MODEL, b_in_gates=GATE_WIDTH, b_in_head=C_G)
    sm_g = _unpack(sa_all.reshape(_SMALL_A_ROWS, LANES), _SMALL_A, sizes)
    sm_g.update(_unpack(sb_sum, _SMALL_B, sizes))
    sm_g["b_in"] = jnp.concatenate([sm_g["b_in_head"], sm_g["b_in_gates"]])
    loss = (0.5 / D_MODEL) * jnp.sum(sm_g["loss"])
    two_d = lambda a: a.reshape(-1, a.shape[-1])
    sd_, sm2_, sv2_ = _adamw_call([two_d(sm_w[k]) for k in names], [two_d(sm_g[k].reshape(sm_w[k].shape)) for k in names],
                                  [two_d(sm_m[k]) for k in names], [two_d(sm_v[k]) for k in names], 1, "adamw_small")
    like = lambda vals: {k: a.reshape(sm_w[k].shape) for k, a in zip(names, vals)}
    sm_d, sm_m2, sm_v2 = like(sd_), like(sm2_), like(sv2_)
    sm_gr = {k: sm_g[k].reshape(sm_w[k].shape) for k in names}

    order = ["g_mix_pre", "w_in", "b_in", "w_pool", "pool_scale", "attn_sinks", "w_branch_pool", "w_branch_attn",
             "w_out", "g_mix_post", "g_mlp_pre", "w_up", "w_down", "g_mlp_post"]
    big_names = ["w_in", "w_branch_pool", "w_branch_attn", "w_out", "w_up", "w_down"]
    lead = lambda a: a[None]
    tables = []
    for small_t, big_t in ((sm_gr, big_g), (sm_d, big_d), (sm_m2, big_m2), (sm_v2, big_v2)):
        bt = dict(zip(big_names, big_t))
        tables.append([lead(bt[k]) if k in bt else small_t[k] for k in order])
    return (loss, gx.reshape(B, S, D_MODEL), *tables[0], *tables[1], *tables[2], *tables[3])
```

```python
import jax
import jax.numpy as jnp
from jax import lax
from jax.experimental import pallas as pl
from jax.experimental.pallas import tpu as pltpu

F32 = jnp.float32
MXU_DTYPE = jnp.bfloat16
MESH = pl.DeviceIdType.MESH

D_MODEL = 1024
POOL_WINDOWS = (2, 4, 8, 16)
POOL_WIDTH = 512
POOL_GC = 128
HEAD_DIM = 64
N_Q_HEADS = 8
N_KV_HEADS = 2
GROUP = 4
ATTN_WIDTH = 512
KV_WIDTH = 128
BLOCK = 128
GATE_WIDTH = 2048
IN_WIDTH = 3328
D_FF = 4096
EPS = 1e-6
NEG_INF = -1e30
ROPE_THETA = 500000.0
ROT_DIM = 16
SCALE = HEAD_DIM ** -0.5
C_Q, C_K, C_V, C_G = 512, 1024, 1152, 1280

ADAM_LR = 0.001
ADAM_B1 = 0.9
ADAM_B2 = 0.999
ADAM_EPS = 1e-08
ADAM_WD = 0.01
ADAM_STEP = 10

N_DEV = 8
LANES = 128
VMEM_LIMIT = 56 * 1024 * 1024

NN = (((1,), (0,)), ((), ()))
NT = (((1,), (1,)), ((), ()))
TN = (((0,), (0,)), ((), ()))


def _dot(a, b, dims):
    return lax.dot_general(a, b, dims, preferred_element_type=F32)


def _params(sem=None):
    return pltpu.CompilerParams(dimension_semantics=sem, vmem_limit_bytes=VMEM_LIMIT)


def _tile(n, pref):
    t = min(n, pref)
    assert n % t == 0, (n, t)
    return t


class _Rider:
    def __init__(self, ins, out_shape, n_remote, n_local, plan, aliases=None, lands=None):
        self.ins, self.out_shape, self.n_remote, self.n_local = list(ins), list(out_shape), n_remote, n_local
        self.plan, self.aliases = plan, dict(aliases or {})
        self.lands = lands


def _after(token, rider=None):
    r = rider or _Rider([], [], 0, 0, lambda ins, outs, send, recv, loc, r0, l0: ([], []))
    return _Rider(r.ins + [token], r.out_shape, r.n_remote, r.n_local, r.plan, r.aliases)


def _launch(body, args, *, name, grid, in_specs, out_specs, out_shape, scratch_shapes=(), sem=None, rider=None):
    if rider is None:
        return pl.pallas_call(body, name=name, grid=grid, in_specs=in_specs, out_specs=out_specs, out_shape=out_shape,
                              scratch_shapes=list(scratch_shapes), compiler_params=_params(sem))(*args)
    n_in, n_out, n_scr = len(args), len(out_shape), len(scratch_shapes)
    r_in, r_out = len(rider.ins), len(rider.out_shape)
    copies = rider.n_remote + rider.n_local > 0

    def wrapped(*refs):
        ins, rins = refs[:n_in], refs[n_in:n_in + r_in]
        o0 = n_in + r_in
        outs, routs = refs[o0:o0 + n_out], refs[o0 + n_out:o0 + n_out + r_out]
        s0 = o0 + n_out + r_out
        scr = refs[s0:s0 + n_scr]
        if not copies:
            return body(*ins, *outs, *scr)
        send, recv, loc = refs[s0 + n_scr:]
        first, last = None, None
        for d in range(len(grid)):
            f, l = pl.program_id(d) == 0, pl.program_id(d) == pl.num_programs(d) - 1
            first = f if first is None else first & f
            last = l if last is None else last & l

        def start():
            remote, local = rider.plan(rins, routs, send, recv, loc, 0, 0)
            for cp in local + remote:
                cp.start()

        def finish():
            remote, local = rider.plan(rins, routs, send, recv, loc, 0, 0)
            for cp in remote + local:
                cp.wait()

        if first is None:
            start()
            body(*ins, *outs, *scr)
            finish()
        else:
            pl.when(first)(start)
            body(*ins, *outs, *scr)
            pl.when(last)(finish)

    hbm = pl.BlockSpec(memory_space=pl.ANY)
    dma = pltpu.SemaphoreType.DMA
    res = pl.pallas_call(
        wrapped, name=name, grid=grid, in_specs=list(in_specs) + [hbm] * r_in,
        out_specs=list(out_specs) + [hbm] * r_out, out_shape=list(out_shape) + rider.out_shape,
        scratch_shapes=list(scratch_shapes) + (
            [dma((max(rider.n_remote, 1),)), dma((max(rider.n_remote, 1),)), dma((max(rider.n_local, 1),))] if copies else []),
        input_output_aliases={n_in + i: n_out + o for i, o in rider.aliases.items()},
        compiler_params=_params(sem),
    )(*args, *rider.ins)
    return list(res[:n_out]), list(res[n_out:])


def _comm_call(rider, name):
    return _launch(lambda: None, [], name=name, grid=(), in_specs=[], out_specs=[], out_shape=[], rider=rider)[1]


_HBM = pl.BlockSpec(memory_space=pltpu.HBM)
_SEM = pl.BlockSpec(memory_space=pltpu.SEMAPHORE)
_EFFECT = pltpu.SideEffectType.DATAFLOW_SIDE_EFFECTING


def _copies_start(riders, name, after=None):
    assert all(r.n_local == 0 and not r.aliases for r in riders)
    extra = [] if after is None else [after]
    sizes = [(len(r.ins), len(r.out_shape)) for r in riders]
    bufs = []
    for r in riders:
        lands = r.lands or [lax.empty(s.shape, s.dtype) for s in r.out_shape]
        bufs += [pltpu.with_memory_space_constraint(a, pltpu.HBM) for a in list(r.ins) + list(lands)]
    nb, ng, ne = len(bufs), len(riders), len(extra)

    def body(*refs):
        sems, token, at = refs[2 * nb + ne:2 * nb + ne + 2 * ng], refs[-1], 0
        for g, (r, (ni, no)) in enumerate(zip(riders, sizes)):
            remote, _ = r.plan(refs[at:at + ni], refs[at + ni:at + ni + no], sems[2 * g], sems[2 * g + 1], None, 0, 0)
            for cp in remote:
                cp.start()
            at += ni + no
        token[...] = jnp.zeros_like(token)

    res = pl.pallas_call(
        body, name=name, in_specs=[_HBM] * nb + [pl.BlockSpec(memory_space=pl.ANY)] * ne,
        out_specs=[_HBM] * nb + [_SEM] * (2 * ng) + [pl.BlockSpec(memory_space=pltpu.VMEM)],
        out_shape=[pltpu.HBM(a.shape, a.dtype) for a in bufs]
        + [pltpu.SemaphoreType.DMA((r.n_remote,)) for r in riders for _ in range(2)]
        + [jax.ShapeDtypeStruct((8, LANES), F32)],
        input_output_aliases={i: i for i in range(nb)},
        compiler_params=pltpu.CompilerParams(has_side_effects=_EFFECT),
    )(*bufs, *extra)
    handles, at = [], 0
    for g, (r, (ni, no)) in enumerate(zip(riders, sizes)):
        handles.append((r, list(res[at:at + ni + no]), res[nb + 2 * g], res[nb + 2 * g + 1]))
        at += ni + no
    return handles, res[-1]


def _copies_wait(handles, after, name):
    bufs = [b for _, bs, _, _ in handles for b in bs]
    sems = [s for _, _, send, recv in handles for s in (send, recv)]
    nb, ng = len(bufs), len(handles)
    after = list(after) if isinstance(after, (list, tuple)) else [after]

    def body(*refs):
        at = 0
        for g, (rider, bs, _, _) in enumerate(handles):
            ni = len(rider.ins)
            remote, _ = rider.plan(refs[at:at + ni], refs[at + ni:at + len(bs)], refs[nb + 2 * g], refs[nb + 2 * g + 1],
                                   None, 0, 0)
            for cp in remote:
                cp.wait_send()
                cp.wait_recv()
            at += len(bs)

    res = pl.pallas_call(
        body, name=name, in_specs=[_HBM] * nb + [_SEM] * (2 * ng) + [pl.BlockSpec(memory_space=pl.ANY)] * len(after),
        out_specs=[_HBM] * nb, out_shape=[pltpu.HBM(a.shape, a.dtype) for a in bufs],
        input_output_aliases={i: i for i in range(nb)},
        compiler_params=pltpu.CompilerParams(has_side_effects=_EFFECT),
    )(*bufs, *sems, *after)
    lands, at = [], 0
    for rider, bs, _, _ in handles:
        lands += list(res[at + len(rider.ins):at + len(bs)])
        at += len(bs)
    return lands


def _copies_pass(handles, riders, after, name):
    bufs = [b for _, bs, _, _ in handles for b in bs]
    sems = [s for _, _, send, recv in handles for s in (send, recv)]
    nb, ng = len(bufs), len(handles)
    after = list(after) if isinstance(after, (list, tuple)) else [after]

    def body(*refs):
        new_sems, at = refs[2 * nb + 2 * ng + len(after):], 0
        for g, ((rider, bs, _, _), then) in enumerate(zip(handles, riders)):
            ins, outs = refs[at:at + len(rider.ins)], refs[at + len(rider.ins):at + len(bs)]
            for cp in rider.plan(ins, outs, refs[nb + 2 * g], refs[nb + 2 * g + 1], None, 0, 0)[0]:
                cp.wait_send()
                cp.wait_recv()
            for cp in then.plan(ins, outs, new_sems[2 * g], new_sems[2 * g + 1], None, 0, 0)[0]:
                cp.start()
            at += len(bs)

    res = pl.pallas_call(
        body, name=name, in_specs=[_HBM] * nb + [_SEM] * (2 * ng) + [pl.BlockSpec(memory_space=pl.ANY)] * len(after),
        out_specs=[_HBM] * nb + [_SEM] * (2 * ng),
        out_shape=[pltpu.HBM(a.shape, a.dtype) for a in bufs]
        + [pltpu.SemaphoreType.DMA((r.n_remote,)) for r in riders for _ in range(2)],
        input_output_aliases={i: i for i in range(nb)},
        compiler_params=pltpu.CompilerParams(has_side_effects=_EFFECT),
    )(*bufs, *sems, *after)
    new, at = [], 0
    for g, ((_, bs, _, _), then) in enumerate(zip(handles, riders)):
        new.append((then, list(res[at:at + len(bs)]), res[nb + 2 * g], res[nb + 2 * g + 1]))
        at += len(bs)
    return new


def _rms_r(x):
    return lax.rsqrt(jnp.mean(x * x, axis=-1, keepdims=True) + EPS)


def _rms_bwd(dn, x, r, g):
    xh = x * r
    dxh = dn * g
    dx = r * (dxh - xh * jnp.mean(dxh * xh, axis=-1, keepdims=True))
    return dx, dn * xh


def _rot(t, c, sa, sb):
    outs = []
    for j in range(t.shape[1] // LANES):
        tj = t[:, LANES * j:LANES * (j + 1)]
        outs.append(tj * c + pltpu.roll(tj, LANES - 8, 1) * sa + pltpu.roll(tj, 8, 1) * sb)
    return outs[0] if len(outs) == 1 else jnp.concatenate(outs, axis=1)


def _rot_tables(S):
    pos = jnp.arange(S, dtype=F32)
    inv_freq = ROPE_THETA ** (-jnp.arange(0, ROT_DIM, 2, dtype=F32) / ROT_DIM)
    ang = pos[:, None] * inv_freq[None, :]
    cos, sin = jnp.cos(ang), jnp.sin(ang)
    one = jnp.ones((S, HEAD_DIM - ROT_DIM), F32)
    zero = jnp.zeros((S, HEAD_DIM - ROT_DIM), F32)
    z8 = jnp.zeros((S, 8), F32)
    c = jnp.concatenate([cos, cos, one], axis=1)
    sa = jnp.concatenate([-sin, z8, zero], axis=1)
    sb = jnp.concatenate([z8, sin, zero], axis=1)
    rep = LANES // HEAD_DIM
    return jnp.tile(c, (1, rep)), jnp.tile(sa, (1, rep)), jnp.tile(sb, (1, rep))


def _lane_tile4(k):
    lane = lax.broadcasted_iota(jnp.int32, k.shape, 1)
    rk = pltpu.roll(k, HEAD_DIM, 1)
    x0 = jnp.where(lane < HEAD_DIM, k, rk)
    x1 = jnp.where(lane < HEAD_DIM, rk, k)
    return jnp.concatenate([x0, x0, x1, x1], axis=1)


def _fold_heads(acc):
    zs = []
    for hk in range(N_KV_HEADS):
        a = acc[:, 256 * hk:256 * hk + LANES] + acc[:, 256 * hk + LANES:256 * (hk + 1)]
        zs.append(a + pltpu.roll(a, HEAD_DIM, 1))
    lane = lax.broadcasted_iota(jnp.int32, zs[0].shape, 1)
    return jnp.where(lane < HEAD_DIM, zs[0], zs[1])


def _inproj_call(x, g1, win_t, b_in, rc, rsa, rsb, S, rider=None):
    T = x.shape[0]
    tm = _tile(S, 512)
    nst = S // tm

    def body(x_ref, g1_ref, w_ref, b_ref, c_ref, sa_ref, sb_ref,
             h_ref, u_ref, q_ref, k4_ref, v4_ref, g_ref):
        xv = x_ref[...]
        hb = ((xv * _rms_r(xv)) * g1_ref[...]).astype(MXU_DTYPE)
        h_ref[...] = hb

        def proj(lo, hi):
            return _dot(hb, w_ref[lo:hi, :], NT) + b_ref[:, lo:hi]

        c, sa, sb = c_ref[...], sa_ref[...], sb_ref[...]
        u_ref[...] = proj(0, C_Q)
        q_ref[...] = (_rot(proj(C_Q, C_K), c, sa, sb) * SCALE).astype(MXU_DTYPE)
        kv = proj(C_K, C_G)
        k4_ref[...] = _lane_tile4(_rot(kv[:, :KV_WIDTH], c, sa, sb)).astype(MXU_DTYPE)
        v4_ref[...] = _lane_tile4(kv[:, KV_WIDTH:]).astype(MXU_DTYPE)
        g_ref[...] = jax.nn.sigmoid(proj(C_G, IN_WIDTH)).astype(MXU_DTYPE)

    tok = lambda w: pl.BlockSpec((tm, w), lambda i: (i, 0))
    full = lambda a: pl.BlockSpec(a.shape, lambda i: (0,) * a.ndim)
    tab = pl.BlockSpec((tm, LANES), lambda i: (i % nst, 0))
    return _launch(
        body, [x, g1, win_t, b_in, rc, rsa, rsb], name="inproj_fwd", grid=(T // tm,),
        in_specs=[tok(D_MODEL), full(g1), full(win_t), full(b_in), tab, tab, tab],
        out_specs=[tok(D_MODEL), tok(POOL_WIDTH), tok(ATTN_WIDTH), tok(512), tok(512), tok(GATE_WIDTH)],
        out_shape=[jax.ShapeDtypeStruct((T, D_MODEL), MXU_DTYPE), jax.ShapeDtypeStruct((T, POOL_WIDTH), F32),
                   jax.ShapeDtypeStruct((T, ATTN_WIDTH), MXU_DTYPE), jax.ShapeDtypeStruct((T, 512), MXU_DTYPE),
                   jax.ShapeDtypeStruct((T, 512), MXU_DTYPE), jax.ShapeDtypeStruct((T, GATE_WIDTH), MXU_DTYPE)],
        sem=("arbitrary",), rider=rider)


def _shift_rows(a, k, rows):
    n = a.shape[0]
    if k > 0:
        return jnp.where(rows >= k, pltpu.roll(a, k, 0), 0.0)
    return jnp.where(rows < n + k, pltpu.roll(a, n + k, 0), 0.0)


def _win_sum(a, w, rows, sign):
    s, k = a, 1
    while k < w:
        s = s + _shift_rows(s, sign * k, rows)
        k *= 2
    return s


def _pool_diff(ug, w, rows):
    inv = 1.0 / jnp.minimum(rows + 1, w).astype(F32)
    return _win_sum(ug, w, rows, 1) * inv - ug, inv


def _pool_call(u, w_pool, pool_scale, S):
    T = u.shape[0]

    def body(u_ref, w_ref, ps_ref, y_ref):
        rows = lax.broadcasted_iota(jnp.int32, (S, POOL_GC), 0)
        for gi, w in enumerate(POOL_WINDOWS):
            sl = slice(POOL_GC * gi, POOL_GC * (gi + 1))
            diff, _ = _pool_diff(u_ref[:, sl], w, rows)
            mixed = _dot(diff.astype(MXU_DTYPE), w_ref[gi], NN)
            y_ref[:, sl] = (mixed * ps_ref[:, sl]).astype(MXU_DTYPE)

    seq = pl.BlockSpec((S, POOL_WIDTH), lambda b: (b, 0))
    return pl.pallas_call(
        body, name="pool_fwd", grid=(T // S,),
        in_specs=[seq, pl.BlockSpec(w_pool.shape, lambda b: (0, 0, 0)), pl.BlockSpec(pool_scale.shape, lambda b: (0, 0))],
        out_specs=seq, out_shape=jax.ShapeDtypeStruct((T, POOL_WIDTH), MXU_DTYPE),
        compiler_params=_params(("arbitrary",)),
    )(u, w_pool, pool_scale)


def _pool_bwd_call(u, dyp, w_pool, pool_scale, S, rider=None):
    T = u.shape[0]

    def body(u_ref, dy_ref, w_ref, ps_ref, du_ref, dw_ref, dps_ref):
        @pl.when(pl.program_id(0) == 0)
        def _():
            dw_ref[...] = jnp.zeros_like(dw_ref)
            dps_ref[...] = jnp.zeros_like(dps_ref)

        rows = lax.broadcasted_iota(jnp.int32, (S, POOL_GC), 0)
        for gi, w in enumerate(POOL_WINDOWS):
            sl = slice(POOL_GC * gi, POOL_GC * (gi + 1))
            diff, inv = _pool_diff(u_ref[:, sl], w, rows)
            diffb = diff.astype(MXU_DTYPE)
            wg = w_ref[gi]
            mixed = _dot(diffb, wg, NN)
            dy = dy_ref[:, sl]
            dps_ref[:, sl] += jnp.sum(dy * mixed, axis=0, keepdims=True)
            dmb = (dy * ps_ref[:, sl]).astype(MXU_DTYPE)
            dw_ref[gi] += _dot(diffb, dmb, TN)
            ddiff = _dot(dmb, wg, NT)
            du_ref[:, sl] = (_win_sum(ddiff * inv, w, rows, -1) - ddiff).astype(MXU_DTYPE)

    seq = pl.BlockSpec((S, POOL_WIDTH), lambda b: (b, 0))
    return _launch(
        body, [u, dyp, w_pool, pool_scale], name="pool_bwd", grid=(T // S,),
        in_specs=[seq, seq, pl.BlockSpec(w_pool.shape, lambda b: (0, 0, 0)), pl.BlockSpec(pool_scale.shape, lambda b: (0, 0))],
        out_specs=[seq, pl.BlockSpec(w_pool.shape, lambda b: (0, 0, 0)), pl.BlockSpec(pool_scale.shape, lambda b: (0, 0))],
        out_shape=[jax.ShapeDtypeStruct((T, POOL_WIDTH), MXU_DTYPE), jax.ShapeDtypeStruct(w_pool.shape, F32),
                   jax.ShapeDtypeStruct(pool_scale.shape, F32)],
        sem=("arbitrary",), rider=rider)


def _attn_consts():
    lane_g = lax.broadcasted_iota(jnp.int32, (BLOCK, 256), 1) >> 6
    rgrp = lax.broadcasted_iota(jnp.int32, (GROUP * BLOCK, 1), 0) >> 7
    rel = lax.broadcasted_iota(jnp.int32, (BLOCK, 256), 0) - lax.broadcasted_iota(jnp.int32, (BLOCK, 256), 1)

    def bias(off):
        ok = (rel + off >= 0) & (rel + off < BLOCK)
        return jnp.concatenate([jnp.where(ok, 0.0, NEG_INF)] * GROUP, axis=0)

    return lane_g, rgrp, bias(0), bias(BLOCK)


def _sink_rows(sink_ref, hk, rgrp):
    sv = jnp.zeros(rgrp.shape, F32)
    for g in range(GROUP):
        sv = jnp.where(rgrp == g, sink_ref[0, GROUP * hk + g], sv)
    return sv


def _stack_heads(xb, lane_g):
    return jnp.concatenate([jnp.where(lane_g == g, xb, jnp.zeros_like(xb)) for g in range(GROUP)], axis=0)


def _unstack_heads(xs, lane_g):
    out = jnp.where(lane_g == 0, xs[0:BLOCK], 0.0)
    for g in range(1, GROUP):
        out = out + jnp.where(lane_g == g, xs[BLOCK * g:BLOCK * (g + 1)], 0.0)
    return out


def _attn_probs(qs, kb, bias, sv):
    s = _dot(qs, kb, NT) + bias
    m = jnp.maximum(jnp.max(s, axis=1, keepdims=True), sv)
    e = jnp.exp(s - m)
    es = jnp.exp(sv - m)
    inv_l = 1.0 / (jnp.sum(e, axis=1, keepdims=True) + es)
    return e * inv_l, es * inv_l


def _attn_blocks(nb, blk, carry, per=1):
    carry = blk(0, 0, True, carry)
    per = per if (nb - 1) % per == 0 else 1

    def step(i, c):
        for k in range(per):
            n = 1 + per * i + k
            c = blk(pl.multiple_of(n * BLOCK, BLOCK), pl.multiple_of((n - 1) * BLOCK, BLOCK), False, c)
        return c

    return lax.fori_loop(0, (nb - 1) // per, step, carry)


def _attn_call(sinks, q, k4, v4, S, rider=None):
    T = q.shape[0]
    nb = S // BLOCK

    def body(sink_ref, q_ref, k_ref, v_ref, o_ref):
        lane_g, rgrp, bias_first, bias_later = _attn_consts()
        svs = [_sink_rows(sink_ref, hk, rgrp) for hk in range(N_KV_HEADS)]

        def blk(q0, k0, first, carry):
            for hk in range(N_KV_HEADS):
                cs = slice(256 * hk, 256 * (hk + 1))
                qs = _stack_heads(q_ref[pl.ds(q0, BLOCK), cs], lane_g)
                p, _ = _attn_probs(qs, k_ref[pl.ds(k0, 2 * BLOCK), cs], bias_first if first else bias_later, svs[hk])
                o = _dot(p.astype(MXU_DTYPE), v_ref[pl.ds(k0, 2 * BLOCK), cs], NN)
                o_ref[pl.ds(q0, BLOCK), cs] = _unstack_heads(o, lane_g).astype(MXU_DTYPE)
            return carry

        _attn_blocks(nb, blk, 0, per=3)

    seq = pl.BlockSpec((S, ATTN_WIDTH), lambda b: (b, 0))
    return _launch(
        body, [sinks, q, k4, v4], name="attn_fwd", grid=(T // S,),
        in_specs=[pl.BlockSpec(memory_space=pltpu.SMEM), seq, seq, seq],
        out_specs=[seq], out_shape=[jax.ShapeDtypeStruct((T, ATTN_WIDTH), MXU_DTYPE)],
        sem=("arbitrary",), rider=rider)


def _attn_bwd_call(sinks, q, k4, v4, do, rc, rsa, rsb, S, rider=None):
    T = q.shape[0]
    nb = S // BLOCK

    def body(sink_ref, q_ref, k_ref, v_ref, do_ref, c_ref, sa_ref, sb_ref,
             dq_ref, dk_ref, dv_ref, ds_ref, dk_acc, dv_acc):
        lane_g, rgrp, bias_first, bias_later = _attn_consts()
        svs = [_sink_rows(sink_ref, hk, rgrp) for hk in range(N_KV_HEADS)]
        lane1 = lax.broadcasted_iota(jnp.int32, (1, LANES), 1)
        dk_acc[...] = jnp.zeros_like(dk_acc)
        dv_acc[...] = jnp.zeros_like(dv_acc)

        def blk(q0, k0, first, dsink):
            rows = pl.ds(q0, BLOCK)
            c, sa, sb = c_ref[rows, :], sa_ref[rows, :], sb_ref[rows, :]
            for hk in range(N_KV_HEADS):
                cs = slice(256 * hk, 256 * (hk + 1))
                qs = _stack_heads(q_ref[rows, cs], lane_g)
                dos = _stack_heads(do_ref[rows, cs], lane_g)
                kb = k_ref[pl.ds(k0, 2 * BLOCK), cs]
                vb = v_ref[pl.ds(k0, 2 * BLOCK), cs]
                p, ps = _attn_probs(qs, kb, bias_first if first else bias_later, svs[hk])
                dp = _dot(dos, vb, NT)
                delta = jnp.sum(p * dp, axis=1, keepdims=True)
                dsb = (p * (dp - delta)).astype(MXU_DTYPE)
                dqb = _unstack_heads(_dot(dsb, kb, NN), lane_g) * SCALE
                dq_ref[rows, cs] = _rot(dqb, c, -sa, -sb).astype(MXU_DTYPE)
                dk_acc[pl.ds(k0, 2 * BLOCK), cs] += _dot(dsb, qs, TN)
                dv_acc[pl.ds(k0, 2 * BLOCK), cs] += _dot(p.astype(MXU_DTYPE), dos, TN)
                psd = ps * delta
                for g in range(GROUP):
                    val = -jnp.sum(psd[BLOCK * g:BLOCK * (g + 1)], axis=0, keepdims=True)
                    dsink = dsink + jnp.where(lane1 == GROUP * hk + g, val, 0.0)
            return dsink

        dsink = _attn_blocks(nb, blk, jnp.zeros((1, LANES), F32))
        dk_ref[...] = _rot(_fold_heads(dk_acc[...]), c_ref[...], -sa_ref[...], -sb_ref[...]).astype(MXU_DTYPE)
        dv_ref[...] = _fold_heads(dv_acc[...]).astype(MXU_DTYPE)
        ds_ref[...] = jnp.broadcast_to(dsink, ds_ref.shape)

    seq = pl.BlockSpec((S, ATTN_WIDTH), lambda b: (b, 0))
    kvs = pl.BlockSpec((S, KV_WIDTH), lambda b: (b, 0))
    tab = pl.BlockSpec((S, LANES), lambda b: (0, 0))
    nseq = T // S
    return _launch(
        body, [sinks, q, k4, v4, do, rc, rsa, rsb], name="attn_bwd", grid=(nseq,),
        in_specs=[pl.BlockSpec(memory_space=pltpu.SMEM), seq, seq, seq, seq, tab, tab, tab],
        out_specs=[seq, kvs, kvs, pl.BlockSpec((8, LANES), lambda b: (b, 0))],
        out_shape=[jax.ShapeDtypeStruct((T, ATTN_WIDTH), MXU_DTYPE), jax.ShapeDtypeStruct((T, KV_WIDTH), MXU_DTYPE),
                   jax.ShapeDtypeStruct((T, KV_WIDTH), MXU_DTYPE), jax.ShapeDtypeStruct((8 * nseq, LANES), F32)],
        scratch_shapes=[pltpu.VMEM((S, 512), F32), pltpu.VMEM((S, 512), F32)],
        sem=("arbitrary",), rider=rider)


def _branch_weights(wbp_ref, wba_ref, wbp_s, wba_s):
    @pl.when(pl.program_id(0) == 0)
    def _():
        for j in range(N_DEV):
            wbp_s[:, LANES * j:LANES * (j + 1)] = wbp_ref[j]
            wba_s[:, LANES * j:LANES * (j + 1)] = wba_ref[j]


def _mix_fwd_call(yp, ya, g, x, wbp, wba, wout, g2, g3, rider=None):
    T = x.shape[0]
    tm = _tile(T, 512)

    def body(yp_ref, ya_ref, g_ref, x_ref, wbp_ref, wba_ref, wout_ref, g2_ref, g3_ref,
             mix_ref, x1_ref, h2_ref, wbp_s, wba_s):
        _branch_weights(wbp_ref, wba_ref, wbp_s, wba_s)
        bp = _dot(yp_ref[...], wbp_s[...], NN)
        ba = _dot(ya_ref[...], wba_s[...], NN)
        merged = g_ref[:, :D_MODEL].astype(F32) * bp + g_ref[:, D_MODEL:].astype(F32) * ba
        mix = _dot(merged.astype(MXU_DTYPE), wout_ref[...], NN)
        mix_ref[...] = mix
        x1 = x_ref[...] + (mix * _rms_r(mix)) * g2_ref[...]
        x1_ref[...] = x1
        h2_ref[...] = ((x1 * _rms_r(x1)) * g3_ref[...]).astype(MXU_DTYPE)

    tok = lambda w: pl.BlockSpec((tm, w), lambda i: (i, 0))
    full = lambda a: pl.BlockSpec(a.shape, lambda i: (0,) * a.ndim)
    return _launch(
        body, [yp, ya, g, x, wbp, wba, wout, g2, g3], name="mix_fwd", grid=(T // tm,),
        in_specs=[tok(POOL_WIDTH), tok(ATTN_WIDTH), tok(GATE_WIDTH), tok(D_MODEL), full(wbp), full(wba), full(wout),
                  full(g2), full(g3)],
        out_specs=[tok(D_MODEL), tok(D_MODEL), tok(D_MODEL)],
        out_shape=[jax.ShapeDtypeStruct((T, D_MODEL), F32), jax.ShapeDtypeStruct((T, D_MODEL), F32),
                   jax.ShapeDtypeStruct((T, D_MODEL), MXU_DTYPE)],
        scratch_shapes=[pltpu.VMEM((POOL_WIDTH, D_MODEL), MXU_DTYPE), pltpu.VMEM((ATTN_WIDTH, D_MODEL), MXU_DTYPE)],
        sem=("arbitrary",), rider=rider)


def _mix_bwd_call(dx1, mix, yp, ya, g, wbp, wba, wout, g2, rider=None):
    T = dx1.shape[0]
    tm = _tile(T, 512)

    def body(dx1_ref, mix_ref, yp_ref, ya_ref, g_ref, wbp_ref, wba_ref, wout_ref, g2_ref,
             dyp_ref, do_ref, dgates_ref, dg2_ref, dbg_ref, gout_ref, gbp_ref, gba_ref,
             wbp_s, wba_s, acc_out, acc_bp, acc_ba, sem):
        _branch_weights(wbp_ref, wba_ref, wbp_s, wba_s)
        step = pl.program_id(0)

        @pl.when(step == 0)
        def _():
            dg2_ref[...] = jnp.zeros_like(dg2_ref)
            dbg_ref[...] = jnp.zeros_like(dbg_ref)
            acc_out[...] = jnp.zeros_like(acc_out)
            acc_bp[...] = jnp.zeros_like(acc_bp)
            acc_ba[...] = jnp.zeros_like(acc_ba)

        mix = mix_ref[...]
        dmix, dg2 = _rms_bwd(dx1_ref[...], mix, _rms_r(mix), g2_ref[...])
        dg2_ref[...] += jnp.sum(dg2, axis=0, keepdims=True)
        dmixb = dmix.astype(MXU_DTYPE)
        dmerged = _dot(dmixb, wout_ref[...], NT)
        yp, ya = yp_ref[...], ya_ref[...]
        bp = _dot(yp, wbp_s[...], NN)
        ba = _dot(ya, wba_s[...], NN)
        gp, ga = g_ref[:, :D_MODEL].astype(F32), g_ref[:, D_MODEL:].astype(F32)
        acc_out[...] += _dot((gp * bp + ga * ba).astype(MXU_DTYPE), dmixb, TN)
        dgp = dmerged * bp * (gp * (1.0 - gp))
        dga = dmerged * ba * (ga * (1.0 - ga))
        dbg_ref[:, :D_MODEL] += jnp.sum(dgp, axis=0, keepdims=True)
        dbg_ref[:, D_MODEL:] += jnp.sum(dga, axis=0, keepdims=True)
        dgates_ref[:, :D_MODEL] = dgp.astype(MXU_DTYPE)
        dgates_ref[:, D_MODEL:] = dga.astype(MXU_DTYPE)
        dbp = (dmerged * gp).astype(MXU_DTYPE)
        dba = (dmerged * ga).astype(MXU_DTYPE)
        acc_bp[...] += _dot(yp, dbp, TN)
        acc_ba[...] += _dot(ya, dba, TN)
        dyp_ref[...] = _dot(dbp, wbp_s[...], NT)
        do_ref[...] = _dot(dba, wba_s[...], NT).astype(MXU_DTYPE)

        @pl.when(step == pl.num_programs(0) - 1)
        def _():
            copies = [pltpu.make_async_copy(acc_out, gout_ref, sem.at[0])]
            for j in range(N_DEV):
                cols = slice(LANES * j, LANES * (j + 1))
                copies.append(pltpu.make_async_copy(acc_bp.at[:, cols], gbp_ref.at[j], sem.at[1 + j]))
                copies.append(pltpu.make_async_copy(acc_ba.at[:, cols], gba_ref.at[j], sem.at[1 + N_DEV + j]))
            for cp in copies:
                cp.start()
            for cp in copies:
                cp.wait()

    tok = lambda w: pl.BlockSpec((tm, w), lambda i: (i, 0))
    full = lambda a: pl.BlockSpec(a.shape, lambda i: (0,) * a.ndim)
    acc = lambda w: pl.BlockSpec((1, w), lambda i: (0, 0))
    hbm = pl.BlockSpec(memory_space=pl.ANY)
    sd = jax.ShapeDtypeStruct
    return _launch(
        body, [dx1, mix, yp, ya, g, wbp, wba, wout, g2], name="mix_bwd", grid=(T // tm,),
        in_specs=[tok(D_MODEL), tok(D_MODEL), tok(POOL_WIDTH), tok(ATTN_WIDTH), tok(GATE_WIDTH), full(wbp), full(wba),
                  full(wout), full(g2)],
        out_specs=[tok(POOL_WIDTH), tok(ATTN_WIDTH), tok(GATE_WIDTH), acc(D_MODEL), acc(GATE_WIDTH), hbm, hbm, hbm],
        out_shape=[sd((T, POOL_WIDTH), F32), sd((T, ATTN_WIDTH), MXU_DTYPE), sd((T, GATE_WIDTH), MXU_DTYPE),
                   sd((1, D_MODEL), F32), sd((1, GATE_WIDTH), F32), sd((D_MODEL, D_MODEL), F32),
                   sd((N_DEV, POOL_WIDTH, LANES), F32), sd((N_DEV, ATTN_WIDTH, LANES), F32)],
        scratch_shapes=[pltpu.VMEM((POOL_WIDTH, D_MODEL), MXU_DTYPE), pltpu.VMEM((ATTN_WIDTH, D_MODEL), MXU_DTYPE),
                        pltpu.VMEM((D_MODEL, D_MODEL), F32), pltpu.VMEM((POOL_WIDTH, D_MODEL), F32),
                        pltpu.VMEM((ATTN_WIDTH, D_MODEL), F32), pltpu.SemaphoreType.DMA((1 + 2 * N_DEV,))],
        sem=("arbitrary",), rider=rider)


def _mlp_up_call(h2, wup):
    T = h2.shape[0]
    tm = _tile(T, 512)
    fc = D_FF // N_DEV

    def body(h2_ref, wup_ref, act_ref):
        h2 = h2_ref[...]
        for j in range(N_DEV):
            rl = jnp.maximum(_dot(h2, wup_ref[j], NN), 0.0)
            act_ref[:, fc * j:fc * (j + 1)] = (rl * rl).astype(MXU_DTYPE)

    sd = jax.ShapeDtypeStruct
    return pl.pallas_call(
        body, name="mlp_up", grid=(T // tm,),
        in_specs=[pl.BlockSpec((tm, D_MODEL), lambda i: (i, 0)),
                  pl.BlockSpec(wup.shape, lambda i: (0, 0, 0), pipeline_mode=pl.Buffered(1))],
        out_specs=pl.BlockSpec((tm, D_FF), lambda i: (i, 0)), out_shape=sd((T, D_FF), MXU_DTYPE),
        compiler_params=_params(("arbitrary",)),
    )(h2, wup)


def _mlp_call(x1, act, target, wup, wdown, g3, g4):
    T = x1.shape[0]
    tm = _tile(T, 256)
    fc = D_FF // N_DEV

    def body(x1_ref, act_ref, t_ref, wup_ref, wdown_ref, g3_ref, g4_ref,
             da_ref, dff_ref, dx1_ref, dg3_ref, dg4_ref, loss_ref):
        @pl.when(pl.program_id(0) == 0)
        def _():
            dg3_ref[...] = jnp.zeros_like(dg3_ref)
            dg4_ref[...] = jnp.zeros_like(dg4_ref)
            loss_ref[...] = jnp.zeros_like(loss_ref)

        ff = jnp.zeros((tm, D_MODEL), F32)
        for j in range(N_DEV):
            ff = ff + _dot(act_ref[:, fc * j:fc * (j + 1)], wdown_ref[j], NN)
        x1 = x1_ref[...]
        r4 = _rms_r(ff)
        err = x1 + (ff * r4) * g4_ref[...] - t_ref[...]
        loss_ref[...] += jnp.sum(err * err, axis=0, keepdims=True)
        dy = err * (1.0 / D_MODEL)
        dff, dg4 = _rms_bwd(dy, ff, r4, g4_ref[...])
        dg4_ref[...] += jnp.sum(dg4, axis=0, keepdims=True)
        dffb = dff.astype(MXU_DTYPE)
        dff_ref[...] = dffb
        dh2 = jnp.zeros((tm, D_MODEL), F32)
        for j in range(N_DEV):
            sl = slice(fc * j, fc * (j + 1))
            rl = jnp.sqrt(act_ref[:, sl].astype(F32))
            dab = (_dot(dffb, wdown_ref[j], NT) * (2.0 * rl)).astype(MXU_DTYPE)
            da_ref[:, sl] = dab
            dh2 = dh2 + _dot(dab, wup_ref[j], NT)
        dx1, dg3 = _rms_bwd(dh2, x1, _rms_r(x1), g3_ref[...])
        dg3_ref[...] += jnp.sum(dg3, axis=0, keepdims=True)
        dx1_ref[...] = dy + dx1

    tok = lambda w: pl.BlockSpec((tm, w), lambda i: (i, 0))
    full = lambda a: pl.BlockSpec(a.shape, lambda i: (0,) * a.ndim, pipeline_mode=pl.Buffered(1))
    vec = pl.BlockSpec((1, D_MODEL), lambda i: (0, 0))
    sd = jax.ShapeDtypeStruct
    return pl.pallas_call(
        body, name="mlp_down_bwd", grid=(T // tm,),
        in_specs=[tok(D_MODEL), tok(D_FF), tok(D_MODEL), full(wup), full(wdown), vec, vec],
        out_specs=[tok(D_FF), tok(D_MODEL), tok(D_MODEL), vec, vec, vec],
        out_shape=[sd((T, D_FF), MXU_DTYPE), sd((T, D_MODEL), MXU_DTYPE),
                   sd((T, D_MODEL), F32), sd((1, D_MODEL), F32), sd((1, D_MODEL), F32), sd((1, D_MODEL), F32)],
        compiler_params=_params(("arbitrary",)),
    )(x1, act, target, wup, wdown, g3, g4)


def _inproj_bwd_call(du, dq, dk, dv, dgates, dx1, x, win_t, g1, rider=None):
    T = x.shape[0]
    tm = _tile(T, 512)

    def body(du_ref, dq_ref, dk_ref, dv_ref, dgt_ref, dx1_ref, x_ref, w_ref, g1_ref, gx_ref, dg1_ref, db_ref):
        @pl.when(pl.program_id(0) == 0)
        def _():
            dg1_ref[...] = jnp.zeros_like(dg1_ref)
            db_ref[...] = jnp.zeros_like(db_ref)

        dh = jnp.zeros((tm, D_MODEL), F32)
        for ref, lo, hi in ((du_ref, 0, C_Q), (dq_ref, C_Q, C_K), (dk_ref, C_K, C_V), (dv_ref, C_V, C_G),
                            (dgt_ref, C_G, IN_WIDTH)):
            piece = ref[...]
            dh = dh + _dot(piece, w_ref[lo:hi, :], NN)
            if hi <= C_G:
                db_ref[:, lo:hi] += jnp.sum(piece.astype(F32), axis=0, keepdims=True)
        xv = x_ref[...]
        dx, dg1 = _rms_bwd(dh, xv, _rms_r(xv), g1_ref[...])
        dg1_ref[...] += jnp.sum(dg1, axis=0, keepdims=True)
        gx_ref[...] = dx1_ref[...] + dx

    tok = lambda w: pl.BlockSpec((tm, w), lambda i: (i, 0))
    full = lambda a: pl.BlockSpec(a.shape, lambda i: (0,) * a.ndim)
    sd = jax.ShapeDtypeStruct
    return _launch(
        body, [du, dq, dk, dv, dgates, dx1, x, win_t, g1], name="inproj_bwd", grid=(T // tm,),
        in_specs=[tok(POOL_WIDTH), tok(ATTN_WIDTH), tok(KV_WIDTH), tok(KV_WIDTH), tok(GATE_WIDTH), tok(D_MODEL),
                  tok(D_MODEL), full(win_t), full(g1)],
        out_specs=[tok(D_MODEL), pl.BlockSpec((1, D_MODEL), lambda i: (0, 0)), pl.BlockSpec((1, C_G), lambda i: (0, 0))],
        out_shape=[sd((T, D_MODEL), F32), sd((1, D_MODEL), F32), sd((1, C_G), F32)],
        sem=("arbitrary",), rider=rider)


WGRAD_TOKENS = 1024


def _wgrad_rows_call(a, b, name, rider=None):
    T, K = a.shape
    N = b.shape[1]
    tm = _tile(T, WGRAD_TOKENS)
    kb = min(K, 1024)
    per = kb // (K // N_DEV)

    def body(a_ref, b_ref, o_ref):
        @pl.when(pl.program_id(1) == 0)
        def _():
            o_ref[...] = jnp.zeros_like(o_ref)

        d = _dot(a_ref[...], b_ref[...], TN)
        rs = kb // per
        for j in range(per):
            o_ref[j] += d[rs * j:rs * (j + 1)]

    return _launch(
        body, [a, b], name=name, grid=(K // kb, T // tm),
        in_specs=[pl.BlockSpec((tm, kb), lambda i, t: (t, i)), pl.BlockSpec((tm, N), lambda i, t: (t, 0))],
        out_specs=[pl.BlockSpec((per, K // N_DEV, N), lambda i, t: (i, 0, 0))],
        out_shape=[jax.ShapeDtypeStruct((N_DEV, K // N_DEV, N), F32)],
        sem=("arbitrary", "arbitrary"), rider=rider)


def _wgrad_cols_call(a, b, name, rider=None):
    T, K = a.shape
    N = b.shape[1]
    tm = _tile(T, WGRAD_TOKENS)
    nb = min(N, 1024)
    per = nb // (N // N_DEV)

    def body(a_ref, b_ref, o_ref):
        @pl.when(pl.program_id(1) == 0)
        def _():
            o_ref[...] = jnp.zeros_like(o_ref)

        d = _dot(a_ref[...], b_ref[...], TN)
        cs = nb // per
        for j in range(per):
            o_ref[j] += d[:, cs * j:cs * (j + 1)]

    return _launch(
        body, [a, b], name=name, grid=(N // nb, T // tm),
        in_specs=[pl.BlockSpec((tm, K), lambda i, t: (t, 0)), pl.BlockSpec((tm, nb), lambda i, t: (t, i))],
        out_specs=[pl.BlockSpec((per, K, N // N_DEV), lambda i, t: (i, 0, 0))],
        out_shape=[jax.ShapeDtypeStruct((N_DEV, K, N // N_DEV), F32)],
        sem=("arbitrary", "arbitrary"), rider=rider)


def _wgrad_in_call(du, dq, dk, dv, dgates, h, rider=None):
    T = h.shape[0]
    tm = _tile(T, WGRAD_TOKENS)
    rows = IN_WIDTH // N_DEV

    def body(du_ref, dq_ref, dk_ref, dv_ref, dgt_ref, h_ref, o_ref, acc, sem):
        t = pl.program_id(0)

        @pl.when(t == 0)
        def _():
            acc[...] = jnp.zeros_like(acc)

        hv = h_ref[...]
        for ref, lo, hi in ((du_ref, 0, C_Q), (dq_ref, C_Q, C_K), (dk_ref, C_K, C_V), (dv_ref, C_V, C_G),
                            (dgt_ref, C_G, IN_WIDTH)):
            acc[lo:hi, :] += _dot(ref[...], hv, TN)

        @pl.when(t == pl.num_programs(0) - 1)
        def _():
            copies = [pltpu.make_async_copy(acc.at[pl.ds(rows * j, rows), :], o_ref.at[j], sem.at[j])
                      for j in range(N_DEV)]
            for cp in copies:
                cp.start()
            for cp in copies:
                cp.wait()

    tok = lambda w: pl.BlockSpec((tm, w), lambda t: (t, 0))
    return _launch(
        body, [du, dq, dk, dv, dgates, h], name="wgrad_in", grid=(T // tm,),
        in_specs=[tok(POOL_WIDTH), tok(ATTN_WIDTH), tok(KV_WIDTH), tok(KV_WIDTH), tok(GATE_WIDTH), tok(D_MODEL)],
        out_specs=[pl.BlockSpec(memory_space=pl.ANY)],
        out_shape=[jax.ShapeDtypeStruct((N_DEV, rows, D_MODEL), F32)],
        scratch_shapes=[pltpu.VMEM((IN_WIDTH, D_MODEL), F32), pltpu.SemaphoreType.DMA((N_DEV,))],
        sem=("arbitrary",), rider=rider)


def _coords():
    return lax.axis_index("x"), lax.axis_index("y"), lax.axis_index("c")


def _ag_route():
    x, y, c = _coords()
    return (x, y, c), (x, y, 1 - c), (x ^ (1 - c), y ^ c, c), (x ^ c, y ^ (1 - c), c), (1 - x, 1 - y, c)


def _rider_ag_first(shard, me):
    def plan(ins, outs, send, recv, loc, r0, l0):
        own, *peers = _ag_route()
        return [pltpu.make_async_remote_copy(
            src_ref=ins[0], dst_ref=outs[0].at[_slot(own)], send_sem=send.at[r0 + k], recv_sem=recv.at[r0 + k],
            device_id=peers[k], device_id_type=MESH) for k in range(3)], []

    return _Rider([shard], [jax.ShapeDtypeStruct((N_DEV,) + shard.shape, shard.dtype)], 3, 0, plan,
                  lands=[_gather_buffer(shard, me)])


def _rider_ag_onward(shard, stage):
    def plan(ins, outs, send, recv, loc, r0, l0):
        own, sibling, near1, near2, diag = _ag_route()
        moves = [(near1, near2), (near1, sibling), (near2, sibling)] if stage == 2 else [(diag, sibling)]
        copies = []
        for k, (block, to) in enumerate(moves):
            part = outs[0].at[_slot(block)]
            copies.append(pltpu.make_async_remote_copy(src_ref=part, dst_ref=part, send_sem=send.at[r0 + k],
                                                       recv_sem=recv.at[r0 + k], device_id=to, device_id_type=MESH))
        return copies, []

    return _Rider([shard], [jax.ShapeDtypeStruct((N_DEV,) + shard.shape, shard.dtype)], 3 if stage == 2 else 1, 0, plan)


def _slot(p):
    return 4 * p[0] + 2 * p[1] + p[2]


def _rows(ref, span):
    return ref if span is None else ref.at[pl.ds(span[0], span[1])]


ALL = "all"
LOCAL = "local"


def _rows(ref, span):
    return ref if span == ALL else ref.at[pl.ds(span[0], span[1])]


def _rider_ag(items):
    ins, out_shape, aliases, where = [], [], {}, []
    n_remote = n_local = 0
    for t, (shard, buf, snd, fwd) in enumerate(items):
        i_shard = i_buf = None
        if snd is not None:
            i_shard = len(ins)
            ins.append(shard)
        if buf is not None:
            i_buf = len(ins)
            ins.append(buf)
            aliases[i_buf] = t
            out_shape.append(jax.ShapeDtypeStruct(buf.shape, buf.dtype))
        else:
            assert fwd is None and snd is not None
            out_shape.append(jax.ShapeDtypeStruct((N_DEV,) + shard.shape, shard.dtype))
        where.append((i_shard, i_buf, n_remote, n_local))
        n_remote += (4 if snd not in (None, LOCAL) else 0) + (3 if fwd is not None else 0)
        n_local += 1 if snd is not None else 0

    def plan(rins, routs, send, recv, loc, r0, l0):
        x, y, c = _coords()
        peers = [(x, y, 1 - c), (1 - x, y, c), (x, 1 - y, c), (1 - x, 1 - y, c)]
        remote, local = [], []
        for t, (shard, buf, snd, fwd) in enumerate(items):
            i_shard, i_buf, k, l = where[t]
            k, l = r0 + k, l0 + l
            if snd is not None:
                span = ALL if snd == LOCAL else snd
                src, dst = _rows(rins[i_shard], span), _rows(routs[t].at[_slot((x, y, c))], span)
                local.append(pltpu.make_async_copy(src, dst, loc.at[l]))
                for peer in (peers if snd != LOCAL else []):
                    remote.append(pltpu.make_async_remote_copy(
                        src_ref=src, dst_ref=dst, send_sem=send.at[k], recv_sem=recv.at[k],
                        device_id=peer, device_id_type=MESH))
                    k += 1
            if fwd is not None:
                for px, py, pc in peers[1:]:
                    s = _slot((px, py, pc))
                    remote.append(pltpu.make_async_remote_copy(
                        src_ref=_rows(rins[i_buf].at[s], fwd), dst_ref=_rows(routs[t].at[s], fwd),
                        send_sem=send.at[k], recv_sem=recv.at[k], device_id=peers[0], device_id_type=MESH))
                    k += 1
        return remote, local

    return _Rider(ins, out_shape, n_remote, n_local, plan, aliases)


def _gather_buffer(shard, me):
    return lax.dynamic_update_slice(lax.empty((N_DEV,) + shard.shape, shard.dtype), shard[None], (me, 0, 0))


def _rider_ag_remote(shards, me):
    n = len(shards)

    def plan(ins, outs, send, recv, loc, r0, l0):
        x, y, c = _coords()
        remote = []
        for t in range(n):
            dst = outs[t].at[_slot((x, y, c))]
            for k, peer in enumerate([(x, y, 1 - c), (1 - x, y, c), (x, 1 - y, c), (1 - x, 1 - y, c)]):
                remote.append(pltpu.make_async_remote_copy(
                    src_ref=ins[t], dst_ref=dst, send_sem=send.at[r0 + 4 * t + k], recv_sem=recv.at[r0 + 4 * t + k],
                    device_id=peer, device_id_type=MESH))
        return remote, []

    return _Rider(shards, [jax.ShapeDtypeStruct((N_DEV,) + s.shape, s.dtype) for s in shards], 4 * n, 0, plan,
                  lands=[_gather_buffer(s, me) for s in shards])


def _rider_rs_sibling(grads):
    n = len(grads)

    def plan(ins, outs, send, recv, loc, r0, l0):
        x, y, c = _coords()
        remote = []
        for t in range(n):
            for q in range(4):
                remote.append(pltpu.make_async_remote_copy(
                    src_ref=ins[t].at[q, 1 - c], dst_ref=outs[t].at[q], send_sem=send.at[r0 + 4 * t + q],
                    recv_sem=recv.at[r0 + 4 * t + q], device_id=(x, y, 1 - c), device_id_type=MESH))
        return remote, []

    return _Rider(grads, [jax.ShapeDtypeStruct((4,) + g.shape[2:], g.dtype) for g in grads], 4 * n, 0, plan)


def _rider_rs_chips(sums, rows=None, into=None):
    n = len(sums)
    rows = rows or [ALL] * n

    def plan(ins, outs, send, recv, loc, r0, l0):
        x, y, c = _coords()
        remote = []
        for t in range(n):
            for r, (px, py) in enumerate([(1 - x, y), (x, 1 - y), (1 - x, 1 - y)]):
                remote.append(pltpu.make_async_remote_copy(
                    src_ref=_rows(ins[t].at[2 * px + py], rows[t]), dst_ref=_rows(outs[t].at[r], rows[t]),
                    send_sem=send.at[r0 + 3 * t + r], recv_sem=recv.at[r0 + 3 * t + r],
                    device_id=(px, py, c), device_id_type=MESH))
        return remote, []

    out_shape = [jax.ShapeDtypeStruct((3,) + s.shape[1:], s.dtype) for s in sums]
    if into is None:
        return _Rider(sums, out_shape, 3 * n, 0, plan)
    return _Rider(list(sums) + list(into), out_shape, 3 * n, 0, plan, aliases={n + t: t for t in range(n)})


def _rider_gather_remote(parts):
    n = len(parts)

    def plan(ins, outs, send, recv, loc, r0, l0):
        x, y, c = _coords()
        me = _slot((x, y, c))
        remote = []
        for t in range(n):
            for k in range(1, N_DEV):
                peer = (x ^ ((k >> 2) & 1), y ^ ((k >> 1) & 1), c ^ (k & 1))
                remote.append(pltpu.make_async_remote_copy(
                    src_ref=ins[t], dst_ref=outs[t].at[me], send_sem=send.at[r0 + 7 * t + k - 1],
                    recv_sem=recv.at[r0 + 7 * t + k - 1], device_id=peer, device_id_type=MESH))
        return remote, []

    return _Rider(parts, [jax.ShapeDtypeStruct((N_DEV,) + p.shape, p.dtype) for p in parts], 7 * n, 0, plan)


def _chip_sum_call(idx, grads, recvd, out_dtypes, name):
    n = len(grads)

    def body(i_ref, *refs):
        for t in range(n):
            refs[2 * n + t][0] = (refs[t][0, 0] + refs[n + t][0]).astype(out_dtypes[t])

    def chip(k, s):
        return jnp.where(k >= s[0], k + 1, k)

    in_specs = [pl.BlockSpec((1, 1) + g.shape[2:], lambda k, s: (chip(k, s), s[1], 0, 0)) for g in grads]
    in_specs += [pl.BlockSpec((1,) + r.shape[1:], lambda k, s: (chip(k, s), 0, 0)) for r in recvd]
    return pl.pallas_call(
        body, name=name,
        grid_spec=pltpu.PrefetchScalarGridSpec(
            num_scalar_prefetch=1, grid=(3,), in_specs=in_specs,
            out_specs=[pl.BlockSpec((1,) + r.shape[1:], lambda k, s: (chip(k, s), 0, 0)) for r in recvd]),
        out_shape=[jax.ShapeDtypeStruct(r.shape, dt) for r, dt in zip(recvd, out_dtypes)],
        compiler_params=_params(("arbitrary",)),
    )(idx, *grads, *recvd)


def _final_sum_call(idx, grads, recvd1, recvd2):
    n = len(grads)
    nsteps = 2

    def body(i_ref, *refs):
        for t in range(n):
            g, r1, r2, o = refs[t], refs[n + t], refs[2 * n + t], refs[3 * n + t]
            s = g[0, 0] + r1[0]
            for r in range(3):
                s = s + r2[r].astype(F32)
            o[...] = s

    def rows(a):
        r = a.shape[-2]
        return r // nsteps if (r // nsteps) % 16 == 0 else r

    def step(a):
        return (lambda i: i) if rows(a) != a.shape[-2] else (lambda i: 0)

    in_specs = [pl.BlockSpec((1, 1, rows(g), g.shape[3]), lambda i, s, st=step(g): (s[0], s[1], st(i), 0)) for g in grads]
    in_specs += [pl.BlockSpec((1, rows(r), r.shape[2]), lambda i, s, st=step(r): (s[0], st(i), 0)) for r in recvd1]
    in_specs += [pl.BlockSpec((3, rows(r), r.shape[2]), lambda i, s, st=step(r): (0, st(i), 0)) for r in recvd2]
    return pl.pallas_call(
        body, name="rs_final_sum",
        grid_spec=pltpu.PrefetchScalarGridSpec(
            num_scalar_prefetch=1, grid=(nsteps,), in_specs=in_specs,
            out_specs=[pl.BlockSpec((rows(r), r.shape[2]), lambda i, s, st=step(r): (st(i), 0)) for r in recvd2]),
        out_shape=[jax.ShapeDtypeStruct(r.shape[1:], F32) for r in recvd2],
        compiler_params=_params(("arbitrary",)),
    )(idx, *grads, *recvd1, *recvd2)


def _sum8_call(parts):
    def body(p_ref, o_ref):
        s = p_ref[0]
        for j in range(1, N_DEV):
            s = s + p_ref[j]
        o_ref[...] = s

    return pl.pallas_call(body, name="sum_small_partials",
                          out_shape=jax.ShapeDtypeStruct(parts.shape[1:], parts.dtype))(parts)


def _adamw(w, g, m, v):
    m = ADAM_B1 * m + (1.0 - ADAM_B1) * g
    v = ADAM_B2 * v + (1.0 - ADAM_B2) * (g * g)
    m_hat = m / (1.0 - ADAM_B1 ** ADAM_STEP)
    v_hat = v / (1.0 - ADAM_B2 ** ADAM_STEP)
    delta = -ADAM_LR * (m_hat / (jnp.sqrt(v_hat) + ADAM_EPS) + ADAM_WD * w)
    return delta, m, v


def _adamw_call(ws, gs, ms, vs, nsteps, name):
    n = len(ws)

    def body(*refs):
        for t in range(n):
            w, g, m, v = (refs[k * n + t][...] for k in range(4))
            d, m2, v2 = _adamw(w, g, m, v)
            refs[4 * n + t][...] = d
            refs[5 * n + t][...] = m2
            refs[6 * n + t][...] = v2

    def spec(a):
        assert a.shape[0] % nsteps == 0 and (nsteps == 1 or (a.shape[0] // nsteps) % 8 == 0), a.shape
        return pl.BlockSpec((a.shape[0] // nsteps, a.shape[1]), lambda i: (i, 0))

    specs = [spec(a) for a in ws]
    outs = pl.pallas_call(
        body, name=name, grid=(nsteps,),
        in_specs=specs * 4, out_specs=specs * 3,
        out_shape=[jax.ShapeDtypeStruct(a.shape, F32) for a in ws] * 3,
        compiler_params=_params(("arbitrary",)),
    )(*ws, *gs, *ms, *vs)
    return outs[:n], outs[n:2 * n], outs[2 * n:]


def _adamw_rs_call(idx, after, gws, r1s, r2s, ws, ms, vs, nsteps, name):
    n = len(ws)

    def body(i_ref, after_ref, *refs):
        for t in range(n):
            gw, r1, r2, w, m, v = (refs[k * n + t] for k in range(6))
            g = gw[0, 0] + r1[0]
            for r in range(3):
                g = g + r2[r].astype(F32)
            d, m2, v2 = _adamw(w[...], g, m[...], v[...])
            refs[6 * n + t][...] = g
            refs[7 * n + t][...] = d
            refs[8 * n + t][...] = m2
            refs[9 * n + t][...] = v2

    def rb(a):
        r = a.shape[0] // nsteps
        assert a.shape[0] % nsteps == 0 and r % 16 == 0, a.shape
        return r

    in_specs = [pl.BlockSpec((1, 1, rb(w), w.shape[1]), lambda i, s: (s[0], s[1], i, 0)) for w in ws]
    in_specs += [pl.BlockSpec((1, rb(w), w.shape[1]), lambda i, s: (s[0], i, 0)) for w in ws]
    in_specs += [pl.BlockSpec((3, rb(w), w.shape[1]), lambda i, s: (0, i, 0)) for w in ws]
    plain = [pl.BlockSpec((rb(w), w.shape[1]), lambda i, s: (i, 0)) for w in ws]
    outs = pl.pallas_call(
        body, name=name,
        grid_spec=pltpu.PrefetchScalarGridSpec(
            num_scalar_prefetch=1, grid=(nsteps,),
            in_specs=[pl.BlockSpec(memory_space=pl.ANY)] + in_specs + plain * 3, out_specs=plain * 4),
        out_shape=[jax.ShapeDtypeStruct(w.shape, F32) for w in ws] * 4,
        compiler_params=_params(("arbitrary",)),
    )(idx, after, *gws, *r1s, *r2s, *ws, *ms, *vs)
    return outs[:n], outs[n:2 * n], outs[2 * n:3 * n], outs[3 * n:]


def _rows128(a, pad_rows):
    flat = a.reshape(-1).astype(F32)
    flat = jnp.pad(flat, (0, pad_rows * LANES - flat.shape[0]))
    return flat.reshape(pad_rows, LANES)


_SMALL_A = (("w_pool", 512), ("pool_scale", 8), ("attn_sinks", 8), ("g_mix_post", 8), ("g_mlp_pre", 8),
            ("g_mlp_post", 8), ("loss", 8), ("b_in_gates", 16))
_SMALL_A_ROWS = 640
_SMALL_B = (("g_mix_pre", 8), ("b_in_head", 16))


def _pack(parts, layout, total_rows):
    rows = [_rows128(parts[k], r) for k, r in layout]
    pad = total_rows - sum(r for _, r in layout)
    if pad:
        rows.append(jnp.zeros((pad, LANES), F32))
    return jnp.concatenate(rows, axis=0)


def _unpack(buf, layout, sizes):
    out, off = {}, 0
    for k, r in layout:
        out[k] = buf[off:off + r].reshape(-1)[:sizes[k]]
        off += r
    return out


def kernel(x, g_mix_pre, w_in, b_in, w_pool, pool_scale, attn_sinks, w_branch_pool, w_branch_attn, w_out, g_mix_post, g_mlp_pre, w_up, w_down, g_mlp_post, loss_target, m_g_mix_pre, m_w_in, m_b_in, m_w_pool, m_pool_scale, m_attn_sinks, m_w_branch_pool, m_w_branch_attn, m_w_out, m_g_mix_post, m_g_mlp_pre, m_w_up, m_w_down, m_g_mlp_post, v_g_mix_pre, v_w_in, v_b_in, v_w_pool, v_pool_scale, v_attn_sinks, v_w_branch_pool, v_w_branch_attn, v_w_out, v_g_mix_post, v_g_mlp_pre, v_w_up, v_w_down, v_g_mlp_post):
    B, S, _ = x.shape
    T = B * S
    xt = x.reshape(T, D_MODEL)
    tgt = loss_target.reshape(T, D_MODEL)
    cx, cy, cc = _coords()

    cidx = jnp.stack([2 * cx + cy, cc]).astype(jnp.int32)
    by_chip = lambda gr: gr.reshape((4, 2) + gr.shape[1:])
    bf = lambda w: w[0].astype(MXU_DTYPE)

    me = _slot((cx, cy, cc))
    win_l = w_in[0].T.astype(MXU_DTYPE)
    (c_win,), _ = _copies_start([_rider_ag_first(win_l, me)], "allgather_first")
    wpool_b = bf(w_pool)
    rc, rsa, rsb = _rot_tables(S)
    wbp_l, wba_l, wout_l, wup_l, wdown_l = bf(w_branch_pool), bf(w_branch_attn), bf(w_out), bf(w_up), bf(w_down)
    gathers = [_rider_ag_remote([wbp_l, wba_l, wout_l], me), _rider_ag_remote([wup_l], me), _rider_ag_remote([wdown_l], me)]
    c_win = _copies_pass([c_win], [_rider_ag_onward(win_l, 2)], [b for r in gathers for b in r.lands] + [rc, rsa, rsb],
                         "allgather_second")
    c_win = _copies_pass(c_win, [_rider_ag_onward(win_l, 3)], [wbp_l, wba_l, wout_l, wup_l, wdown_l], "allgather_third")
    (win_s,) = _copies_wait(c_win, wpool_b, "allgather_weights")
    win_t = win_s.reshape(IN_WIDTH, D_MODEL)

    (c_br, c_up, c_dn), tok = _copies_start(gathers, "allgather_start", after=win_s)
    (h, u, q, k4, v4, g), _ = _inproj_call(xt, g_mix_pre, win_t, b_in, rc, rsa, rsb, S, rider=_after(tok))
    yp = _pool_call(u, wpool_b, pool_scale, S)
    wbp_1, wba_1, wout_1 = _copies_wait([c_br], yp, "allgather_wait_branch")
    (ya,), (wbp_s, wba_s, wout_s) = _attn_call(
        attn_sinks, q, k4, v4, S,
        rider=_rider_ag([(None, wbp_1, None, ALL), (None, wba_1, None, ALL), (None, wout_1, None, ALL)]))
    wout_f = wout_s.reshape(D_MODEL, D_MODEL)
    (wup_1,) = _copies_wait([c_up], ya, "allgather_wait_up")
    (mix, x1, h2), (wup_s,) = _mix_fwd_call(
        yp, ya, g, xt, wbp_s, wba_s, wout_f, g_mix_post, g_mlp_pre, rider=_rider_ag([(None, wup_1, None, ALL)]))
    act = _mlp_up_call(h2, wup_s)
    (wdown_1,) = _copies_wait([c_dn], act, "allgather_wait_down")
    (wdown_s,) = _comm_call(_rider_ag([(None, wdown_1, None, ALL)]), "allgather_pass_down")

    da, dff, dx1, dg3, dg4, lossvec = _mlp_call(x1, act, tgt, wup_s, wdown_s, g_mlp_pre, g_mlp_post)
    gw_up = by_chip(_wgrad_cols_call(h2, da, "wgrad_up")[0])
    (gw_down,), (r1_up,) = _wgrad_rows_call(act, dff, "wgrad_down", rider=_rider_rs_sibling([gw_up]))
    gw_down = by_chip(gw_down)
    (s_up,) = _chip_sum_call(cidx, [gw_up], [r1_up], [MXU_DTYPE], "rs_chip_sum_up")
    (dyp, do, dgates, dg2, dbg, gw_out, gw_bp, gw_ba), _ = _mix_bwd_call(
        dx1, mix, yp, ya, g, wbp_s, wba_s, wout_f, g_mix_post, rider=_after(s_up))
    gw_out = by_chip(gw_out.reshape(N_DEV, D_MODEL // N_DEV, D_MODEL))
    gw_bp, gw_ba = by_chip(gw_bp), by_chip(gw_ba)
    (c_up,), tok = _copies_start([_rider_rs_chips([s_up])], "rs_chips_start_up", after=dyp)
    (dq, dk, dv, dsink), (r1_down, r1_out, r1_bp, r1_ba) = _attn_bwd_call(
        attn_sinks, q, k4, v4, do, rc, rsa, rsb, S,
        rider=_after(tok, _rider_rs_sibling([gw_down, gw_out, gw_bp, gw_ba])))
    s_down, *s_obb = _chip_sum_call(cidx, [gw_down, gw_out, gw_bp, gw_ba], [r1_down, r1_out, r1_bp, r1_ba],
                                    [MXU_DTYPE] * 4, "rs_chip_sum_branch")
    (c_obb,), tok = _copies_start([_rider_rs_chips([s_down] + s_obb)], "rs_chips_start_branch")
    (du, dwp, dps), _ = _pool_bwd_call(u, dyp, wpool_b, pool_scale, S, rider=_after(tok))
    (gw_in,) = _wgrad_in_call(du, dq, dk, dv, dgates, h)
    gw_in = by_chip(gw_in)

    small_a = {"w_pool": dwp, "pool_scale": dps,
               "attn_sinks": jnp.sum(dsink.reshape(B, 8, LANES)[:, 0, :N_Q_HEADS], axis=0), "g_mix_post": dg2,
               "g_mlp_pre": dg3, "g_mlp_post": dg4, "loss": lossvec, "b_in_gates": dbg}
    gw_sa = by_chip(_pack(small_a, _SMALL_A, _SMALL_A_ROWS).reshape(N_DEV, _SMALL_A_ROWS // N_DEV, LANES))
    r1_in, r1_sa = _comm_call(_rider_rs_sibling([gw_in, gw_sa]), "rs_sibling_in")
    s_in, s_sa = _chip_sum_call(cidx, [gw_in, gw_sa], [r1_in, r1_sa], [MXU_DTYPE, F32], "rs_chip_sum_in")
    (c_in,), tok = _copies_start([_rider_rs_chips([s_in, s_sa])], "rs_chips_start_in")
    (gx, dg1, dba_in), _ = _inproj_bwd_call(du, dq, dk, dv, dgates, dx1, xt, win_t, g_mix_pre, rider=_after(tok))
    r2_up, r2_down, r2_out, r2_bp, r2_ba, r2_in, r2_sa = _copies_wait([c_up, c_obb, c_in], dg1, "rs_chips_wait")

    (g_sa,) = _final_sum_call(cidx, [gw_sa], [r1_sa], [r2_sa])
    part_b = _pack({"g_mix_pre": dg1, "b_in_head": dba_in}, _SMALL_B, sum(r for _, r in _SMALL_B))
    (c_small,), tok = _copies_start([_rider_gather_remote([g_sa, part_b])], "allgather_small_start")

    in_t = _adamw_rs_call(cidx, tok, [gw_in], [r1_in], [r2_in], [w_in[0].T], [m_w_in[0].T], [v_w_in[0].T], 2,
                          "adamw_w_in")
    rest = _adamw_rs_call(
        cidx, tok, [gw_bp, gw_ba, gw_out, gw_up, gw_down], [r1_bp, r1_ba, r1_out, r1_up, r1_down],
        [r2_bp, r2_ba, r2_out, r2_up, r2_down], [w_branch_pool[0], w_branch_attn[0], w_out[0], w_up[0], w_down[0]],
        [m_w_branch_pool[0], m_w_branch_attn[0], m_w_out[0], m_w_up[0], m_w_down[0]],
        [v_w_branch_pool[0], v_w_branch_attn[0], v_w_out[0], v_w_up[0], v_w_down[0]], N_DEV, "adamw_shards")
    big_g, big_d, big_m2, big_v2 = ([a[0].T] + list(b) for a, b in zip(in_t, rest))

    sa_all, sb_all = _copies_wait([c_small], rest[0][0], "allgather_small_wait")
    sa_all = lax.dynamic_update_slice(sa_all, g_sa[None], (me, 0, 0))
    sb_sum = _sum8_call(lax.dynamic_update_slice(sb_all, part_b[None], (me, 0, 0)))

    names = ["g_mix_pre", "b_in", "w_pool", "pool_scale", "attn_sinks", "g_mix_post", "g_mlp_pre", "g_mlp_post"]
    sm_w = dict(g_mix_pre=g_mix_pre, b_in=b_in, w_pool=w_pool, pool_scale=pool_scale, attn_sinks=attn_sinks,
                g_mix_post=g_mix_post, g_mlp_pre=g_mlp_pre, g_mlp_post=g_mlp_post)
    sm_m = dict(g_mix_pre=m_g_mix_pre, b_in=m_b_in, w_pool=m_w_pool, pool_scale=m_pool_scale, attn_sinks=m_attn_sinks,
                g_mix_post=m_g_mix_post, g_mlp_pre=m_g_mlp_pre, g_mlp_post=m_g_mlp_post)
    sm_v = dict(g_mix_pre=v_g_mix_pre, b_in=v_b_in, w_pool=v_w_pool, pool_scale=v_pool_scale, attn_sinks=v_attn_sinks,
                g_mix_post=v_g_mix_post, g_mlp_pre=v_g_mlp_pre, g_mlp_post=v_g_mlp_post)
    sizes = {k: sm_w[k].size for k in names}
    sizes.update(loss=D_MODEL, b_in_gates=GATE_WIDTH, b_in_head=C_G)
    sm_g = _unpack(sa_all.reshape(_SMALL_A_ROWS, LANES), _SMALL_A, sizes)
    sm_g.update(_unpack(sb_sum, _SMALL_B, sizes))
    sm_g["b_in"] = jnp.concatenate([sm_g["b_in_head"], sm_g["b_in_gates"]])
    loss = (0.5 / D_MODEL) * jnp.sum(sm_g["loss"])
    two_d = lambda a: a.reshape(-1, a.shape[-1])
    sd_, sm2_, sv2_ = _adamw_call([two_d(sm_w[k]) for k in names], [two_d(sm_g[k].reshape(sm_w[k].shape)) for k in names],
                                  [two_d(sm_m[k]) for k in names], [two_d(sm_v[k]) for k in names], 1, "adamw_small")
    like = lambda vals: {k: a.reshape(sm_w[k].shape) for k, a in zip(names, vals)}
    sm_d, sm_m2, sm_v2 = like(sd_), like(sm2_), like(sv2_)
    sm_gr = {k: sm_g[k].reshape(sm_w[k].shape) for k in names}

    order = ["g_mix_pre", "w_in", "b_in", "w_pool", "pool_scale", "attn_sinks", "w_branch_pool", "w_branch_attn",
             "w_out", "g_mix_post", "g_mlp_pre", "w_up", "w_down", "g_mlp_post"]
    big_names = ["w_in", "w_branch_pool", "w_branch_attn", "w_out", "w_up", "w_down"]
    lead = lambda a: a[None]
    tables = []
    for small_t, big_t in ((sm_gr, big_g), (sm_d, big_d), (sm_m2, big_m2), (sm_v2, big_v2)):
        bt = dict(zip(big_names, big_t))
        tables.append([lead(bt[k]) if k in bt else small_t[k] for k in order])
    return (loss, gx.reshape(B, S, D_MODEL), *tables[0], *tables[1], *tables[2], *tables[3])
```

```python
import jax
import jax.numpy as jnp
from jax import lax
from jax.experimental import pallas as pl
from jax.experimental.pallas import tpu as pltpu

F32 = jnp.float32
MXU_DTYPE = jnp.bfloat16
MESH = pl.DeviceIdType.MESH

D_MODEL = 1024
POOL_WINDOWS = (2, 4, 8, 16)
POOL_WIDTH = 512
POOL_GC = 128
HEAD_DIM = 64
N_Q_HEADS = 8
N_KV_HEADS = 2
GROUP = 4
ATTN_WIDTH = 512
KV_WIDTH = 128
BLOCK = 128
GATE_WIDTH = 2048
IN_WIDTH = 3328
D_FF = 4096
EPS = 1e-6
NEG_INF = -1e30
ROPE_THETA = 500000.0
ROT_DIM = 16
SCALE = HEAD_DIM ** -0.5
C_Q, C_K, C_V, C_G = 512, 1024, 1152, 1280

ADAM_LR = 0.001
ADAM_B1 = 0.9
ADAM_B2 = 0.999
ADAM_EPS = 1e-08
ADAM_WD = 0.01
ADAM_STEP = 10

N_DEV = 8
LANES = 128
VMEM_LIMIT = 56 * 1024 * 1024

NN = (((1,), (0,)), ((), ()))
NT = (((1,), (1,)), ((), ()))
TN = (((0,), (0,)), ((), ()))


def _dot(a, b, dims):
    return lax.dot_general(a, b, dims, preferred_element_type=F32)


def _params(sem=None):
    return pltpu.CompilerParams(dimension_semantics=sem, vmem_limit_bytes=VMEM_LIMIT)


def _tile(n, pref):
    t = min(n, pref)
    assert n % t == 0, (n, t)
    return t


class _Rider:
    def __init__(self, ins, out_shape, n_remote, n_local, plan, aliases=None, lands=None):
        self.ins, self.out_shape, self.n_remote, self.n_local = list(ins), list(out_shape), n_remote, n_local
        self.plan, self.aliases = plan, dict(aliases or {})
        self.lands = lands


def _after(token, rider=None):
    r = rider or _Rider([], [], 0, 0, lambda ins, outs, send, recv, loc, r0, l0: ([], []))
    return _Rider(r.ins + [token], r.out_shape, r.n_remote, r.n_local, r.plan, r.aliases)


def _launch(body, args, *, name, grid, in_specs, out_specs, out_shape, scratch_shapes=(), sem=None, rider=None):
    if rider is None:
        return pl.pallas_call(body, name=name, grid=grid, in_specs=in_specs, out_specs=out_specs, out_shape=out_shape,
                              scratch_shapes=list(scratch_shapes), compiler_params=_params(sem))(*args)
    n_in, n_out, n_scr = len(args), len(out_shape), len(scratch_shapes)
    r_in, r_out = len(rider.ins), len(rider.out_shape)
    copies = rider.n_remote + rider.n_local > 0

    def wrapped(*refs):
        ins, rins = refs[:n_in], refs[n_in:n_in + r_in]
        o0 = n_in + r_in
        outs, routs = refs[o0:o0 + n_out], refs[o0 + n_out:o0 + n_out + r_out]
        s0 = o0 + n_out + r_out
        scr = refs[s0:s0 + n_scr]
        if not copies:
            return body(*ins, *outs, *scr)
        send, recv, loc = refs[s0 + n_scr:]
        first, last = None, None
        for d in range(len(grid)):
            f, l = pl.program_id(d) == 0, pl.program_id(d) == pl.num_programs(d) - 1
            first = f if first is None else first & f
            last = l if last is None else last & l

        def start():
            remote, local = rider.plan(rins, routs, send, recv, loc, 0, 0)
            for cp in local + remote:
                cp.start()

        def finish():
            remote, local = rider.plan(rins, routs, send, recv, loc, 0, 0)
            for cp in remote + local:
                cp.wait()

        if first is None:
            start()
            body(*ins, *outs, *scr)
            finish()
        else:
            pl.when(first)(start)
            body(*ins, *outs, *scr)
            pl.when(last)(finish)

    hbm = pl.BlockSpec(memory_space=pl.ANY)
    dma = pltpu.SemaphoreType.DMA
    res = pl.pallas_call(
        wrapped, name=name, grid=grid, in_specs=list(in_specs) + [hbm] * r_in,
        out_specs=list(out_specs) + [hbm] * r_out, out_shape=list(out_shape) + rider.out_shape,
        scratch_shapes=list(scratch_shapes) + (
            [dma((max(rider.n_remote, 1),)), dma((max(rider.n_remote, 1),)), dma((max(rider.n_local, 1),))] if copies else []),
        input_output_aliases={n_in + i: n_out + o for i, o in rider.aliases.items()},
        compiler_params=_params(sem),
    )(*args, *rider.ins)
    return list(res[:n_out]), list(res[n_out:])


def _comm_call(rider, name):
    return _launch(lambda: None, [], name=name, grid=(), in_specs=[], out_specs=[], out_shape=[], rider=rider)[1]


_HBM = pl.BlockSpec(memory_space=pltpu.HBM)
_SEM = pl.BlockSpec(memory_space=pltpu.SEMAPHORE)
_EFFECT = pltpu.SideEffectType.DATAFLOW_SIDE_EFFECTING


def _copies_start(riders, name, after=None):
    assert all(r.n_local == 0 and not r.aliases for r in riders)
    extra = [] if after is None else [after]
    sizes = [(len(r.ins), len(r.out_shape)) for r in riders]
    bufs = []
    for r in riders:
        lands = r.lands or [lax.empty(s.shape, s.dtype) for s in r.out_shape]
        bufs += [pltpu.with_memory_space_constraint(a, pltpu.HBM) for a in list(r.ins) + list(lands)]
    nb, ng, ne = len(bufs), len(riders), len(extra)

    def body(*refs):
        sems, token, at = refs[2 * nb + ne:2 * nb + ne + 2 * ng], refs[-1], 0
        for g, (r, (ni, no)) in enumerate(zip(riders, sizes)):
            remote, _ = r.plan(refs[at:at + ni], refs[at + ni:at + ni + no], sems[2 * g], sems[2 * g + 1], None, 0, 0)
            for cp in remote:
                cp.start()
            at += ni + no
        token[...] = jnp.zeros_like(token)

    res = pl.pallas_call(
        body, name=name, in_specs=[_HBM] * nb + [pl.BlockSpec(memory_space=pl.ANY)] * ne,
        out_specs=[_HBM] * nb + [_SEM] * (2 * ng) + [pl.BlockSpec(memory_space=pltpu.VMEM)],
        out_shape=[pltpu.HBM(a.shape, a.dtype) for a in bufs]
        + [pltpu.SemaphoreType.DMA((r.n_remote,)) for r in riders for _ in range(2)]
        + [jax.ShapeDtypeStruct((8, LANES), F32)],
        input_output_aliases={i: i for i in range(nb)},
        compiler_params=pltpu.CompilerParams(has_side_effects=_EFFECT),
    )(*bufs, *extra)
    handles, at = [], 0
    for g, (r, (ni, no)) in enumerate(zip(riders, sizes)):
        handles.append((r, list(res[at:at + ni + no]), res[nb + 2 * g], res[nb + 2 * g + 1]))
        at += ni + no
    return handles, res[-1]


def _copies_wait(handles, after, name):
    bufs = [b for _, bs, _, _ in handles for b in bs]
    sems = [s for _, _, send, recv in handles for s in (send, recv)]
    nb, ng = len(bufs), len(handles)
    after = list(after) if isinstance(after, (list, tuple)) else [after]

    def body(*refs):
        at = 0
        for g, (rider, bs, _, _) in enumerate(handles):
            ni = len(rider.ins)
            remote, _ = rider.plan(refs[at:at + ni], refs[at + ni:at + len(bs)], refs[nb + 2 * g], refs[nb + 2 * g + 1],
                                   None, 0, 0)
            for cp in remote:
                cp.wait_send()
                cp.wait_recv()
            at += len(bs)

    res = pl.pallas_call(
        body, name=name, in_specs=[_HBM] * nb + [_SEM] * (2 * ng) + [pl.BlockSpec(memory_space=pl.ANY)] * len(after),
        out_specs=[_HBM] * nb, out_shape=[pltpu.HBM(a.shape, a.dtype) for a in bufs],
        input_output_aliases={i: i for i in range(nb)},
        compiler_params=pltpu.CompilerParams(has_side_effects=_EFFECT),
    )(*bufs, *sems, *after)
    lands, at = [], 0
    for rider, bs, _, _ in handles:
        lands += list(res[at + len(rider.ins):at + len(bs)])
        at += len(bs)
    return lands


def _copies_pass(handles, riders, after, name):
    bufs = [b for _, bs, _, _ in handles for b in bs]
    sems = [s for _, _, send, recv in handles for s in (send, recv)]
    nb, ng = len(bufs), len(handles)
    after = list(after) if isinstance(after, (list, tuple)) else [after]

    def body(*refs):
        new_sems, at = refs[2 * nb + 2 * ng + len(after):], 0
        for g, ((rider, bs, _, _), then) in enumerate(zip(handles, riders)):
            ins, outs = refs[at:at + len(rider.ins)], refs[at + len(rider.ins):at + len(bs)]
            for cp in rider.plan(ins, outs, refs[nb + 2 * g], refs[nb + 2 * g + 1], None, 0, 0)[0]:
                cp.wait_send()
                cp.wait_recv()
            for cp in then.plan(ins, outs, new_sems[2 * g], new_sems[2 * g + 1], None, 0, 0)[0]:
                cp.start()
            at += len(bs)

    res = pl.pallas_call(
        body, name=name, in_specs=[_HBM] * nb + [_SEM] * (2 * ng) + [pl.BlockSpec(memory_space=pl.ANY)] * len(after),
        out_specs=[_HBM] * nb + [_SEM] * (2 * ng),
        out_shape=[pltpu.HBM(a.shape, a.dtype) for a in bufs]
        + [pltpu.SemaphoreType.DMA((r.n_remote,)) for r in riders for _ in range(2)],
        input_output_aliases={i: i for i in range(nb)},
        compiler_params=pltpu.CompilerParams(has_side_effects=_EFFECT),
    )(*bufs, *sems, *after)
    new, at = [], 0
    for g, ((_, bs, _, _), then) in enumerate(zip(handles, riders)):
        new.append((then, list(res[at:at + len(bs)]), res[nb + 2 * g], res[nb + 2 * g + 1]))
        at += len(bs)
    return new


def _rms_r(x):
    return lax.rsqrt(jnp.mean(x * x, axis=-1, keepdims=True) + EPS)


def _rms_bwd(dn, x, r, g):
    xh = x * r
    dxh = dn * g
    dx = r * (dxh - xh * jnp.mean(dxh * xh, axis=-1, keepdims=True))
    return dx, dn * xh


def _rot(t, c, sa, sb):
    outs = []
    for j in range(t.shape[1] // LANES):
        tj = t[:, LANES * j:LANES * (j + 1)]
        outs.append(tj * c + pltpu.roll(tj, LANES - 8, 1) * sa + pltpu.roll(tj, 8, 1) * sb)
    return outs[0] if len(outs) == 1 else jnp.concatenate(outs, axis=1)


def _rot_tables(S):
    pos = jnp.arange(S, dtype=F32)
    inv_freq = ROPE_THETA ** (-jnp.arange(0, ROT_DIM, 2, dtype=F32) / ROT_DIM)
    ang = pos[:, None] * inv_freq[None, :]
    cos, sin = jnp.cos(ang), jnp.sin(ang)
    one = jnp.ones((S, HEAD_DIM - ROT_DIM), F32)
    zero = jnp.zeros((S, HEAD_DIM - ROT_DIM), F32)
    z8 = jnp.zeros((S, 8), F32)
    c = jnp.concatenate([cos, cos, one], axis=1)
    sa = jnp.concatenate([-sin, z8, zero], axis=1)
    sb = jnp.concatenate([z8, sin, zero], axis=1)
    rep = LANES // HEAD_DIM
    return jnp.tile(c, (1, rep)), jnp.tile(sa, (1, rep)), jnp.tile(sb, (1, rep))


def _lane_tile4(k):
    lane = lax.broadcasted_iota(jnp.int32, k.shape, 1)
    rk = pltpu.roll(k, HEAD_DIM, 1)
    x0 = jnp.where(lane < HEAD_DIM, k, rk)
    x1 = jnp.where(lane < HEAD_DIM, rk, k)
    return jnp.concatenate([x0, x0, x1, x1], axis=1)


def _fold_heads(acc):
    zs = []
    for hk in range(N_KV_HEADS):
        a = acc[:, 256 * hk:256 * hk + LANES] + acc[:, 256 * hk + LANES:256 * (hk + 1)]
        zs.append(a + pltpu.roll(a, HEAD_DIM, 1))
    lane = lax.broadcasted_iota(jnp.int32, zs[0].shape, 1)
    return jnp.where(lane < HEAD_DIM, zs[0], zs[1])


def _inproj_call(x, g1, win_t, b_in, rc, rsa, rsb, S, rider=None):
    T = x.shape[0]
    tm = _tile(S, 512)
    nst = S // tm

    def body(x_ref, g1_ref, w_ref, b_ref, c_ref, sa_ref, sb_ref,
             h_ref, u_ref, q_ref, k4_ref, v4_ref, g_ref):
        xv = x_ref[...]
        hb = ((xv * _rms_r(xv)) * g1_ref[...]).astype(MXU_DTYPE)
        h_ref[...] = hb

        def proj(lo, hi):
            return _dot(hb, w_ref[lo:hi, :], NT) + b_ref[:, lo:hi]

        c, sa, sb = c_ref[...], sa_ref[...], sb_ref[...]
        u_ref[...] = proj(0, C_Q)
        q_ref[...] = (_rot(proj(C_Q, C_K), c, sa, sb) * SCALE).astype(MXU_DTYPE)
        kv = proj(C_K, C_G)
        k4_ref[...] = _lane_tile4(_rot(kv[:, :KV_WIDTH], c, sa, sb)).astype(MXU_DTYPE)
        v4_ref[...] = _lane_tile4(kv[:, KV_WIDTH:]).astype(MXU_DTYPE)
        g_ref[...] = jax.nn.sigmoid(proj(C_G, IN_WIDTH)).astype(MXU_DTYPE)

    tok = lambda w: pl.BlockSpec((tm, w), lambda i: (i, 0))
    full = lambda a: pl.BlockSpec(a.shape, lambda i: (0,) * a.ndim)
    tab = pl.BlockSpec((tm, LANES), lambda i: (i % nst, 0))
    return _launch(
        body, [x, g1, win_t, b_in, rc, rsa, rsb], name="inproj_fwd", grid=(T // tm,),
        in_specs=[tok(D_MODEL), full(g1), full(win_t), full(b_in), tab, tab, tab],
        out_specs=[tok(D_MODEL), tok(POOL_WIDTH), tok(ATTN_WIDTH), tok(512), tok(512), tok(GATE_WIDTH)],
        out_shape=[jax.ShapeDtypeStruct((T, D_MODEL), MXU_DTYPE), jax.ShapeDtypeStruct((T, POOL_WIDTH), F32),
                   jax.ShapeDtypeStruct((T, ATTN_WIDTH), MXU_DTYPE), jax.ShapeDtypeStruct((T, 512), MXU_DTYPE),
                   jax.ShapeDtypeStruct((T, 512), MXU_DTYPE), jax.ShapeDtypeStruct((T, GATE_WIDTH), MXU_DTYPE)],
        sem=("arbitrary",), rider=rider)


def _shift_rows(a, k, rows):
    n = a.shape[0]
    if k > 0:
        return jnp.where(rows >= k, pltpu.roll(a, k, 0), 0.0)
    return jnp.where(rows < n + k, pltpu.roll(a, n + k, 0), 0.0)


def _win_sum(a, w, rows, sign):
    s, k = a, 1
    while k < w:
        s = s + _shift_rows(s, sign * k, rows)
        k *= 2
    return s


def _pool_diff(ug, w, rows):
    inv = 1.0 / jnp.minimum(rows + 1, w).astype(F32)
    return _win_sum(ug, w, rows, 1) * inv - ug, inv


def _pool_call(u, w_pool, pool_scale, S, rider):
    T = u.shape[0]

    def body(u_ref, w_ref, ps_ref, y_ref):
        rows = lax.broadcasted_iota(jnp.int32, (S, POOL_GC), 0)
        for gi, w in enumerate(POOL_WINDOWS):
            sl = slice(POOL_GC * gi, POOL_GC * (gi + 1))
            diff, _ = _pool_diff(u_ref[:, sl], w, rows)
            mixed = _dot(diff.astype(MXU_DTYPE), w_ref[gi], NN)
            y_ref[:, sl] = (mixed * ps_ref[:, sl]).astype(MXU_DTYPE)

    seq = pl.BlockSpec((S, POOL_WIDTH), lambda b: (b, 0))
    return _launch(
        body, [u, w_pool, pool_scale], name="pool_fwd", grid=(T // S,),
        in_specs=[seq, pl.BlockSpec(w_pool.shape, lambda b: (0, 0, 0)), pl.BlockSpec(pool_scale.shape, lambda b: (0, 0))],
        out_specs=[seq], out_shape=[jax.ShapeDtypeStruct((T, POOL_WIDTH), MXU_DTYPE)], sem=("arbitrary",), rider=rider)


def _pool_bwd_call(u, dyp, w_pool, pool_scale, S, rider=None):
    T = u.shape[0]

    def body(u_ref, dy_ref, w_ref, ps_ref, du_ref, dw_ref, dps_ref):
        @pl.when(pl.program_id(0) == 0)
        def _():
            dw_ref[...] = jnp.zeros_like(dw_ref)
            dps_ref[...] = jnp.zeros_like(dps_ref)

        rows = lax.broadcasted_iota(jnp.int32, (S, POOL_GC), 0)
        for gi, w in enumerate(POOL_WINDOWS):
            sl = slice(POOL_GC * gi, POOL_GC * (gi + 1))
            diff, inv = _pool_diff(u_ref[:, sl], w, rows)
            diffb = diff.astype(MXU_DTYPE)
            wg = w_ref[gi]
            mixed = _dot(diffb, wg, NN)
            dy = dy_ref[:, sl]
            dps_ref[:, sl] += jnp.sum(dy * mixed, axis=0, keepdims=True)
            dmb = (dy * ps_ref[:, sl]).astype(MXU_DTYPE)
            dw_ref[gi] += _dot(diffb, dmb, TN)
            ddiff = _dot(dmb, wg, NT)
            du_ref[:, sl] = (_win_sum(ddiff * inv, w, rows, -1) - ddiff).astype(MXU_DTYPE)

    seq = pl.BlockSpec((S, POOL_WIDTH), lambda b: (b, 0))
    return _launch(
        body, [u, dyp, w_pool, pool_scale], name="pool_bwd", grid=(T // S,),
        in_specs=[seq, seq, pl.BlockSpec(w_pool.shape, lambda b: (0, 0, 0)), pl.BlockSpec(pool_scale.shape, lambda b: (0, 0))],
        out_specs=[seq, pl.BlockSpec(w_pool.shape, lambda b: (0, 0, 0)), pl.BlockSpec(pool_scale.shape, lambda b: (0, 0))],
        out_shape=[jax.ShapeDtypeStruct((T, POOL_WIDTH), MXU_DTYPE), jax.ShapeDtypeStruct(w_pool.shape, F32),
                   jax.ShapeDtypeStruct(pool_scale.shape, F32)],
        sem=("arbitrary",), rider=rider)


def _attn_consts():
    lane_g = lax.broadcasted_iota(jnp.int32, (BLOCK, 256), 1) >> 6
    rgrp = lax.broadcasted_iota(jnp.int32, (GROUP * BLOCK, 1), 0) >> 7
    rel = lax.broadcasted_iota(jnp.int32, (BLOCK, 256), 0) - lax.broadcasted_iota(jnp.int32, (BLOCK, 256), 1)

    def bias(off):
        ok = (rel + off >= 0) & (rel + off < BLOCK)
        return jnp.concatenate([jnp.where(ok, 0.0, NEG_INF)] * GROUP, axis=0)

    return lane_g, rgrp, bias(0), bias(BLOCK)


def _sink_rows(sink_ref, hk, rgrp):
    sv = jnp.zeros(rgrp.shape, F32)
    for g in range(GROUP):
        sv = jnp.where(rgrp == g, sink_ref[0, GROUP * hk + g], sv)
    return sv


def _stack_heads(xb, lane_g):
    return jnp.concatenate([jnp.where(lane_g == g, xb, jnp.zeros_like(xb)) for g in range(GROUP)], axis=0)


def _unstack_heads(xs, lane_g):
    out = jnp.where(lane_g == 0, xs[0:BLOCK], 0.0)
    for g in range(1, GROUP):
        out = out + jnp.where(lane_g == g, xs[BLOCK * g:BLOCK * (g + 1)], 0.0)
    return out


def _attn_probs(qs, kb, bias, sv):
    s = _dot(qs, kb, NT) + bias
    m = jnp.maximum(jnp.max(s, axis=1, keepdims=True), sv)
    e = jnp.exp(s - m)
    es = jnp.exp(sv - m)
    inv_l = 1.0 / (jnp.sum(e, axis=1, keepdims=True) + es)
    return e * inv_l, es * inv_l


def _attn_blocks(nb, blk, carry, per=1):
    carry = blk(0, 0, True, carry)
    per = per if (nb - 1) % per == 0 else 1

    def step(i, c):
        for k in range(per):
            n = 1 + per * i + k
            c = blk(pl.multiple_of(n * BLOCK, BLOCK), pl.multiple_of((n - 1) * BLOCK, BLOCK), False, c)
        return c

    return lax.fori_loop(0, (nb - 1) // per, step, carry)


def _attn_call(sinks, q, k4, v4, S, rider=None):
    T = q.shape[0]
    nb = S // BLOCK

    def body(sink_ref, q_ref, k_ref, v_ref, o_ref):
        lane_g, rgrp, bias_first, bias_later = _attn_consts()
        svs = [_sink_rows(sink_ref, hk, rgrp) for hk in range(N_KV_HEADS)]

        def blk(q0, k0, first, carry):
            for hk in range(N_KV_HEADS):
                cs = slice(256 * hk, 256 * (hk + 1))
                qs = _stack_heads(q_ref[pl.ds(q0, BLOCK), cs], lane_g)
                p, _ = _attn_probs(qs, k_ref[pl.ds(k0, 2 * BLOCK), cs], bias_first if first else bias_later, svs[hk])
                o = _dot(p.astype(MXU_DTYPE), v_ref[pl.ds(k0, 2 * BLOCK), cs], NN)
                o_ref[pl.ds(q0, BLOCK), cs] = _unstack_heads(o, lane_g).astype(MXU_DTYPE)
            return carry

        _attn_blocks(nb, blk, 0, per=3)

    seq = pl.BlockSpec((S, ATTN_WIDTH), lambda b: (b, 0))
    return _launch(
        body, [sinks, q, k4, v4], name="attn_fwd", grid=(T // S,),
        in_specs=[pl.BlockSpec(memory_space=pltpu.SMEM), seq, seq, seq],
        out_specs=[seq], out_shape=[jax.ShapeDtypeStruct((T, ATTN_WIDTH), MXU_DTYPE)],
        sem=("arbitrary",), rider=rider)


def _attn_bwd_call(sinks, q, k4, v4, do, rc, rsa, rsb, S, rider=None):
    T = q.shape[0]
    nb = S // BLOCK

    def body(sink_ref, q_ref, k_ref, v_ref, do_ref, c_ref, sa_ref, sb_ref,
             dq_ref, dk_ref, dv_ref, ds_ref, dk_acc, dv_acc):
        lane_g, rgrp, bias_first, bias_later = _attn_consts()
        svs = [_sink_rows(sink_ref, hk, rgrp) for hk in range(N_KV_HEADS)]
        lane1 = lax.broadcasted_iota(jnp.int32, (1, LANES), 1)
        dk_acc[...] = jnp.zeros_like(dk_acc)
        dv_acc[...] = jnp.zeros_like(dv_acc)

        def blk(q0, k0, first, dsink):
            rows = pl.ds(q0, BLOCK)
            c, sa, sb = c_ref[rows, :], sa_ref[rows, :], sb_ref[rows, :]
            for hk in range(N_KV_HEADS):
                cs = slice(256 * hk, 256 * (hk + 1))
                qs = _stack_heads(q_ref[rows, cs], lane_g)
                dos = _stack_heads(do_ref[rows, cs], lane_g)
                kb = k_ref[pl.ds(k0, 2 * BLOCK), cs]
                vb = v_ref[pl.ds(k0, 2 * BLOCK), cs]
                p, ps = _attn_probs(qs, kb, bias_first if first else bias_later, svs[hk])
                dp = _dot(dos, vb, NT)
                delta = jnp.sum(p * dp, axis=1, keepdims=True)
                dsb = (p * (dp - delta)).astype(MXU_DTYPE)
                dqb = _unstack_heads(_dot(dsb, kb, NN), lane_g) * SCALE
                dq_ref[rows, cs] = _rot(dqb, c, -sa, -sb).astype(MXU_DTYPE)
                dk_acc[pl.ds(k0, 2 * BLOCK), cs] += _dot(dsb, qs, TN)
                dv_acc[pl.ds(k0, 2 * BLOCK), cs] += _dot(p.astype(MXU_DTYPE), dos, TN)
                psd = ps * delta
                for g in range(GROUP):
                    val = -jnp.sum(psd[BLOCK * g:BLOCK * (g + 1)], axis=0, keepdims=True)
                    dsink = dsink + jnp.where(lane1 == GROUP * hk + g, val, 0.0)
            return dsink

        dsink = _attn_blocks(nb, blk, jnp.zeros((1, LANES), F32))
        dk_ref[...] = _rot(_fold_heads(dk_acc[...]), c_ref[...], -sa_ref[...], -sb_ref[...]).astype(MXU_DTYPE)
        dv_ref[...] = _fold_heads(dv_acc[...]).astype(MXU_DTYPE)
        ds_ref[...] = jnp.broadcast_to(dsink, ds_ref.shape)

    seq = pl.BlockSpec((S, ATTN_WIDTH), lambda b: (b, 0))
    kvs = pl.BlockSpec((S, KV_WIDTH), lambda b: (b, 0))
    tab = pl.BlockSpec((S, LANES), lambda b: (0, 0))
    nseq = T // S
    return _launch(
        body, [sinks, q, k4, v4, do, rc, rsa, rsb], name="attn_bwd", grid=(nseq,),
        in_specs=[pl.BlockSpec(memory_space=pltpu.SMEM), seq, seq, seq, seq, tab, tab, tab],
        out_specs=[seq, kvs, kvs, pl.BlockSpec((8, LANES), lambda b: (b, 0))],
        out_shape=[jax.ShapeDtypeStruct((T, ATTN_WIDTH), MXU_DTYPE), jax.ShapeDtypeStruct((T, KV_WIDTH), MXU_DTYPE),
                   jax.ShapeDtypeStruct((T, KV_WIDTH), MXU_DTYPE), jax.ShapeDtypeStruct((8 * nseq, LANES), F32)],
        scratch_shapes=[pltpu.VMEM((S, 512), F32), pltpu.VMEM((S, 512), F32)],
        sem=("arbitrary",), rider=rider)


def _branch_weights(wbp_ref, wba_ref, wbp_s, wba_s):
    @pl.when(pl.program_id(0) == 0)
    def _():
        for j in range(N_DEV):
            wbp_s[:, LANES * j:LANES * (j + 1)] = wbp_ref[j]
            wba_s[:, LANES * j:LANES * (j + 1)] = wba_ref[j]


def _mix_fwd_call(yp, ya, g, x, wbp, wba, wout, g2, g3, rider=None):
    T = x.shape[0]
    tm = _tile(T, 512)

    def body(yp_ref, ya_ref, g_ref, x_ref, wbp_ref, wba_ref, wout_ref, g2_ref, g3_ref,
             mix_ref, x1_ref, h2_ref, wbp_s, wba_s):
        _branch_weights(wbp_ref, wba_ref, wbp_s, wba_s)
        bp = _dot(yp_ref[...], wbp_s[...], NN)
        ba = _dot(ya_ref[...], wba_s[...], NN)
        merged = g_ref[:, :D_MODEL].astype(F32) * bp + g_ref[:, D_MODEL:].astype(F32) * ba
        mix = _dot(merged.astype(MXU_DTYPE), wout_ref[...], NN)
        mix_ref[...] = mix
        x1 = x_ref[...] + (mix * _rms_r(mix)) * g2_ref[...]
        x1_ref[...] = x1
        h2_ref[...] = ((x1 * _rms_r(x1)) * g3_ref[...]).astype(MXU_DTYPE)

    tok = lambda w: pl.BlockSpec((tm, w), lambda i: (i, 0))
    full = lambda a: pl.BlockSpec(a.shape, lambda i: (0,) * a.ndim)
    return _launch(
        body, [yp, ya, g, x, wbp, wba, wout, g2, g3], name="mix_fwd", grid=(T // tm,),
        in_specs=[tok(POOL_WIDTH), tok(ATTN_WIDTH), tok(GATE_WIDTH), tok(D_MODEL), full(wbp), full(wba), full(wout),
                  full(g2), full(g3)],
        out_specs=[tok(D_MODEL), tok(D_MODEL), tok(D_MODEL)],
        out_shape=[jax.ShapeDtypeStruct((T, D_MODEL), F32), jax.ShapeDtypeStruct((T, D_MODEL), F32),
                   jax.ShapeDtypeStruct((T, D_MODEL), MXU_DTYPE)],
        scratch_shapes=[pltpu.VMEM((POOL_WIDTH, D_MODEL), MXU_DTYPE), pltpu.VMEM((ATTN_WIDTH, D_MODEL), MXU_DTYPE)],
        sem=("arbitrary",), rider=rider)


def _mix_bwd_call(dx1, mix, yp, ya, g, wbp, wba, wout, g2, rider=None):
    T = dx1.shape[0]
    tm = _tile(T, 512)

    def body(dx1_ref, mix_ref, yp_ref, ya_ref, g_ref, wbp_ref, wba_ref, wout_ref, g2_ref,
             dyp_ref, do_ref, dgates_ref, dg2_ref, dbg_ref, gout_ref, gbp_ref, gba_ref,
             wbp_s, wba_s, acc_out, acc_bp, acc_ba, sem):
        _branch_weights(wbp_ref, wba_ref, wbp_s, wba_s)
        step = pl.program_id(0)

        @pl.when(step == 0)
        def _():
            dg2_ref[...] = jnp.zeros_like(dg2_ref)
            dbg_ref[...] = jnp.zeros_like(dbg_ref)
            acc_out[...] = jnp.zeros_like(acc_out)
            acc_bp[...] = jnp.zeros_like(acc_bp)
            acc_ba[...] = jnp.zeros_like(acc_ba)

        mix = mix_ref[...]
        dmix, dg2 = _rms_bwd(dx1_ref[...], mix, _rms_r(mix), g2_ref[...])
        dg2_ref[...] += jnp.sum(dg2, axis=0, keepdims=True)
        dmixb = dmix.astype(MXU_DTYPE)
        dmerged = _dot(dmixb, wout_ref[...], NT)
        yp, ya = yp_ref[...], ya_ref[...]
        bp = _dot(yp, wbp_s[...], NN)
        ba = _dot(ya, wba_s[...], NN)
        gp, ga = g_ref[:, :D_MODEL].astype(F32), g_ref[:, D_MODEL:].astype(F32)
        acc_out[...] += _dot((gp * bp + ga * ba).astype(MXU_DTYPE), dmixb, TN)
        dgp = dmerged * bp * (gp * (1.0 - gp))
        dga = dmerged * ba * (ga * (1.0 - ga))
        dbg_ref[:, :D_MODEL] += jnp.sum(dgp, axis=0, keepdims=True)
        dbg_ref[:, D_MODEL:] += jnp.sum(dga, axis=0, keepdims=True)
        dgates_ref[:, :D_MODEL] = dgp.astype(MXU_DTYPE)
        dgates_ref[:, D_MODEL:] = dga.astype(MXU_DTYPE)
        dbp = (dmerged * gp).astype(MXU_DTYPE)
        dba = (dmerged * ga).astype(MXU_DTYPE)
        acc_bp[...] += _dot(yp, dbp, TN)
        acc_ba[...] += _dot(ya, dba, TN)
        dyp_ref[...] = _dot(dbp, wbp_s[...], NT)
        do_ref[...] = _dot(dba, wba_s[...], NT).astype(MXU_DTYPE)

        @pl.when(step == pl.num_programs(0) - 1)
        def _():
            copies = [pltpu.make_async_copy(acc_out, gout_ref, sem.at[0])]
            for j in range(N_DEV):
                cols = slice(LANES * j, LANES * (j + 1))
                copies.append(pltpu.make_async_copy(acc_bp.at[:, cols], gbp_ref.at[j], sem.at[1 + j]))
                copies.append(pltpu.make_async_copy(acc_ba.at[:, cols], gba_ref.at[j], sem.at[1 + N_DEV + j]))
            for cp in copies:
                cp.start()
            for cp in copies:
                cp.wait()

    tok = lambda w: pl.BlockSpec((tm, w), lambda i: (i, 0))
    full = lambda a: pl.BlockSpec(a.shape, lambda i: (0,) * a.ndim)
    acc = lambda w: pl.BlockSpec((1, w), lambda i: (0, 0))
    hbm = pl.BlockSpec(memory_space=pl.ANY)
    sd = jax.ShapeDtypeStruct
    return _launch(
        body, [dx1, mix, yp, ya, g, wbp, wba, wout, g2], name="mix_bwd", grid=(T // tm,),
        in_specs=[tok(D_MODEL), tok(D_MODEL), tok(POOL_WIDTH), tok(ATTN_WIDTH), tok(GATE_WIDTH), full(wbp), full(wba),
                  full(wout), full(g2)],
        out_specs=[tok(POOL_WIDTH), tok(ATTN_WIDTH), tok(GATE_WIDTH), acc(D_MODEL), acc(GATE_WIDTH), hbm, hbm, hbm],
        out_shape=[sd((T, POOL_WIDTH), F32), sd((T, ATTN_WIDTH), MXU_DTYPE), sd((T, GATE_WIDTH), MXU_DTYPE),
                   sd((1, D_MODEL), F32), sd((1, GATE_WIDTH), F32), sd((D_MODEL, D_MODEL), F32),
                   sd((N_DEV, POOL_WIDTH, LANES), F32), sd((N_DEV, ATTN_WIDTH, LANES), F32)],
        scratch_shapes=[pltpu.VMEM((POOL_WIDTH, D_MODEL), MXU_DTYPE), pltpu.VMEM((ATTN_WIDTH, D_MODEL), MXU_DTYPE),
                        pltpu.VMEM((D_MODEL, D_MODEL), F32), pltpu.VMEM((POOL_WIDTH, D_MODEL), F32),
                        pltpu.VMEM((ATTN_WIDTH, D_MODEL), F32), pltpu.SemaphoreType.DMA((1 + 2 * N_DEV,))],
        sem=("arbitrary",), rider=rider)


def _mlp_up_call(h2, wup):
    T = h2.shape[0]
    tm = _tile(T, 512)
    fc = D_FF // N_DEV

    def body(h2_ref, wup_ref, act_ref):
        h2 = h2_ref[...]
        for j in range(N_DEV):
            rl = jnp.maximum(_dot(h2, wup_ref[j], NN), 0.0)
            act_ref[:, fc * j:fc * (j + 1)] = (rl * rl).astype(MXU_DTYPE)

    sd = jax.ShapeDtypeStruct
    return pl.pallas_call(
        body, name="mlp_up", grid=(T // tm,),
        in_specs=[pl.BlockSpec((tm, D_MODEL), lambda i: (i, 0)),
                  pl.BlockSpec(wup.shape, lambda i: (0, 0, 0), pipeline_mode=pl.Buffered(1))],
        out_specs=pl.BlockSpec((tm, D_FF), lambda i: (i, 0)), out_shape=sd((T, D_FF), MXU_DTYPE),
        compiler_params=_params(("arbitrary",)),
    )(h2, wup)


def _mlp_call(x1, act, target, wup, wdown, g3, g4):
    T = x1.shape[0]
    tm = _tile(T, 256)
    fc = D_FF // N_DEV

    def body(x1_ref, act_ref, t_ref, wup_ref, wdown_ref, g3_ref, g4_ref,
             da_ref, dff_ref, dx1_ref, dg3_ref, dg4_ref, loss_ref):
        @pl.when(pl.program_id(0) == 0)
        def _():
            dg3_ref[...] = jnp.zeros_like(dg3_ref)
            dg4_ref[...] = jnp.zeros_like(dg4_ref)
            loss_ref[...] = jnp.zeros_like(loss_ref)

        ff = jnp.zeros((tm, D_MODEL), F32)
        for j in range(N_DEV):
            ff = ff + _dot(act_ref[:, fc * j:fc * (j + 1)], wdown_ref[j], NN)
        x1 = x1_ref[...]
        r4 = _rms_r(ff)
        err = x1 + (ff * r4) * g4_ref[...] - t_ref[...]
        loss_ref[...] += jnp.sum(err * err, axis=0, keepdims=True)
        dy = err * (1.0 / D_MODEL)
        dff, dg4 = _rms_bwd(dy, ff, r4, g4_ref[...])
        dg4_ref[...] += jnp.sum(dg4, axis=0, keepdims=True)
        dffb = dff.astype(MXU_DTYPE)
        dff_ref[...] = dffb
        dh2 = jnp.zeros((tm, D_MODEL), F32)
        for j in range(N_DEV):
            sl = slice(fc * j, fc * (j + 1))
            rl = jnp.sqrt(act_ref[:, sl].astype(F32))
            dab = (_dot(dffb, wdown_ref[j], NT) * (2.0 * rl)).astype(MXU_DTYPE)
            da_ref[:, sl] = dab
            dh2 = dh2 + _dot(dab, wup_ref[j], NT)
        dx1, dg3 = _rms_bwd(dh2, x1, _rms_r(x1), g3_ref[...])
        dg3_ref[...] += jnp.sum(dg3, axis=0, keepdims=True)
        dx1_ref[...] = dy + dx1

    tok = lambda w: pl.BlockSpec((tm, w), lambda i: (i, 0))
    full = lambda a: pl.BlockSpec(a.shape, lambda i: (0,) * a.ndim, pipeline_mode=pl.Buffered(1))
    vec = pl.BlockSpec((1, D_MODEL), lambda i: (0, 0))
    sd = jax.ShapeDtypeStruct
    return pl.pallas_call(
        body, name="mlp_down_bwd", grid=(T // tm,),
        in_specs=[tok(D_MODEL), tok(D_FF), tok(D_MODEL), full(wup), full(wdown), vec, vec],
        out_specs=[tok(D_FF), tok(D_MODEL), tok(D_MODEL), vec, vec, vec],
        out_shape=[sd((T, D_FF), MXU_DTYPE), sd((T, D_MODEL), MXU_DTYPE),
                   sd((T, D_MODEL), F32), sd((1, D_MODEL), F32), sd((1, D_MODEL), F32), sd((1, D_MODEL), F32)],
        compiler_params=_params(("arbitrary",)),
    )(x1, act, target, wup, wdown, g3, g4)


def _inproj_bwd_call(du, dq, dk, dv, dgates, dx1, x, win_t, g1, rider=None):
    T = x.shape[0]
    tm = _tile(T, 512)

    def body(du_ref, dq_ref, dk_ref, dv_ref, dgt_ref, dx1_ref, x_ref, w_ref, g1_ref, gx_ref, dg1_ref, db_ref):
        @pl.when(pl.program_id(0) == 0)
        def _():
            dg1_ref[...] = jnp.zeros_like(dg1_ref)
            db_ref[...] = jnp.zeros_like(db_ref)

        dh = jnp.zeros((tm, D_MODEL), F32)
        for ref, lo, hi in ((du_ref, 0, C_Q), (dq_ref, C_Q, C_K), (dk_ref, C_K, C_V), (dv_ref, C_V, C_G),
                            (dgt_ref, C_G, IN_WIDTH)):
            piece = ref[...]
            dh = dh + _dot(piece, w_ref[lo:hi, :], NN)
            if hi <= C_G:
                db_ref[:, lo:hi] += jnp.sum(piece.astype(F32), axis=0, keepdims=True)
        xv = x_ref[...]
        dx, dg1 = _rms_bwd(dh, xv, _rms_r(xv), g1_ref[...])
        dg1_ref[...] += jnp.sum(dg1, axis=0, keepdims=True)
        gx_ref[...] = dx1_ref[...] + dx

    tok = lambda w: pl.BlockSpec((tm, w), lambda i: (i, 0))
    full = lambda a: pl.BlockSpec(a.shape, lambda i: (0,) * a.ndim)
    sd = jax.ShapeDtypeStruct
    return _launch(
        body, [du, dq, dk, dv, dgates, dx1, x, win_t, g1], name="inproj_bwd", grid=(T // tm,),
        in_specs=[tok(POOL_WIDTH), tok(ATTN_WIDTH), tok(KV_WIDTH), tok(KV_WIDTH), tok(GATE_WIDTH), tok(D_MODEL),
                  tok(D_MODEL), full(win_t), full(g1)],
        out_specs=[tok(D_MODEL), pl.BlockSpec((1, D_MODEL), lambda i: (0, 0)), pl.BlockSpec((1, C_G), lambda i: (0, 0))],
        out_shape=[sd((T, D_MODEL), F32), sd((1, D_MODEL), F32), sd((1, C_G), F32)],
        sem=("arbitrary",), rider=rider)


WGRAD_TOKENS = 1024


def _wgrad_rows_call(a, b, name, rider=None):
    T, K = a.shape
    N = b.shape[1]
    tm = _tile(T, WGRAD_TOKENS)
    kb = min(K, 1024)
    per = kb // (K // N_DEV)

    def body(a_ref, b_ref, o_ref):
        @pl.when(pl.program_id(1) == 0)
        def _():
            o_ref[...] = jnp.zeros_like(o_ref)

        d = _dot(a_ref[...], b_ref[...], TN)
        rs = kb // per
        for j in range(per):
            o_ref[j] += d[rs * j:rs * (j + 1)]

    return _launch(
        body, [a, b], name=name, grid=(K // kb, T // tm),
        in_specs=[pl.BlockSpec((tm, kb), lambda i, t: (t, i)), pl.BlockSpec((tm, N), lambda i, t: (t, 0))],
        out_specs=[pl.BlockSpec((per, K // N_DEV, N), lambda i, t: (i, 0, 0))],
        out_shape=[jax.ShapeDtypeStruct((N_DEV, K // N_DEV, N), F32)],
        sem=("arbitrary", "arbitrary"), rider=rider)


def _wgrad_cols_call(a, b, name, rider=None):
    T, K = a.shape
    N = b.shape[1]
    tm = _tile(T, WGRAD_TOKENS)
    nb = min(N, 1024)
    per = nb // (N // N_DEV)

    def body(a_ref, b_ref, o_ref):
        @pl.when(pl.program_id(1) == 0)
        def _():
            o_ref[...] = jnp.zeros_like(o_ref)

        d = _dot(a_ref[...], b_ref[...], TN)
        cs = nb // per
        for j in range(per):
            o_ref[j] += d[:, cs * j:cs * (j + 1)]

    return _launch(
        body, [a, b], name=name, grid=(N // nb, T // tm),
        in_specs=[pl.BlockSpec((tm, K), lambda i, t: (t, 0)), pl.BlockSpec((tm, nb), lambda i, t: (t, i))],
        out_specs=[pl.BlockSpec((per, K, N // N_DEV), lambda i, t: (i, 0, 0))],
        out_shape=[jax.ShapeDtypeStruct((N_DEV, K, N // N_DEV), F32)],
        sem=("arbitrary", "arbitrary"), rider=rider)


def _wgrad_in_call(du, dq, dk, dv, dgates, h, rider=None):
    T = h.shape[0]
    tm = _tile(T, WGRAD_TOKENS)
    rows = IN_WIDTH // N_DEV

    def body(du_ref, dq_ref, dk_ref, dv_ref, dgt_ref, h_ref, o_ref, acc, sem):
        t = pl.program_id(0)

        @pl.when(t == 0)
        def _():
            acc[...] = jnp.zeros_like(acc)

        hv = h_ref[...]
        for ref, lo, hi in ((du_ref, 0, C_Q), (dq_ref, C_Q, C_K), (dk_ref, C_K, C_V), (dv_ref, C_V, C_G),
                            (dgt_ref, C_G, IN_WIDTH)):
            acc[lo:hi, :] += _dot(ref[...], hv, TN)

        @pl.when(t == pl.num_programs(0) - 1)
        def _():
            copies = [pltpu.make_async_copy(acc.at[pl.ds(rows * j, rows), :], o_ref.at[j], sem.at[j])
                      for j in range(N_DEV)]
            for cp in copies:
                cp.start()
            for cp in copies:
                cp.wait()

    tok = lambda w: pl.BlockSpec((tm, w), lambda t: (t, 0))
    return _launch(
        body, [du, dq, dk, dv, dgates, h], name="wgrad_in", grid=(T // tm,),
        in_specs=[tok(POOL_WIDTH), tok(ATTN_WIDTH), tok(KV_WIDTH), tok(KV_WIDTH), tok(GATE_WIDTH), tok(D_MODEL)],
        out_specs=[pl.BlockSpec(memory_space=pl.ANY)],
        out_shape=[jax.ShapeDtypeStruct((N_DEV, rows, D_MODEL), F32)],
        scratch_shapes=[pltpu.VMEM((IN_WIDTH, D_MODEL), F32), pltpu.SemaphoreType.DMA((N_DEV,))],
        sem=("arbitrary",), rider=rider)


def _coords():
    return lax.axis_index("x"), lax.axis_index("y"), lax.axis_index("c")


def _ag_route():
    x, y, c = _coords()
    return (x, y, c), (x, y, 1 - c), (x ^ (1 - c), y ^ c, c), (x ^ c, y ^ (1 - c), c), (1 - x, 1 - y, c)


def _rider_ag_first(shard, me):
    def plan(ins, outs, send, recv, loc, r0, l0):
        own, *peers = _ag_route()
        return [pltpu.make_async_remote_copy(
            src_ref=ins[0], dst_ref=outs[0].at[_slot(own)], send_sem=send.at[r0 + k], recv_sem=recv.at[r0 + k],
            device_id=peers[k], device_id_type=MESH) for k in range(3)], []

    return _Rider([shard], [jax.ShapeDtypeStruct((N_DEV,) + shard.shape, shard.dtype)], 3, 0, plan,
                  lands=[_gather_buffer(shard, me)])


def _rider_ag_onward(shard, stage):
    def plan(ins, outs, send, recv, loc, r0, l0):
        own, sibling, near1, near2, diag = _ag_route()
        moves = [(near1, near2), (near1, sibling), (near2, sibling)] if stage == 2 else [(diag, sibling)]
        copies = []
        for k, (block, to) in enumerate(moves):
            part = outs[0].at[_slot(block)]
            copies.append(pltpu.make_async_remote_copy(src_ref=part, dst_ref=part, send_sem=send.at[r0 + k],
                                                       recv_sem=recv.at[r0 + k], device_id=to, device_id_type=MESH))
        return copies, []

    return _Rider([shard], [jax.ShapeDtypeStruct((N_DEV,) + shard.shape, shard.dtype)], 3 if stage == 2 else 1, 0, plan)


def _slot(p):
    return 4 * p[0] + 2 * p[1] + p[2]


def _rows(ref, span):
    return ref if span is None else ref.at[pl.ds(span[0], span[1])]


ALL = "all"
LOCAL = "local"


def _rows(ref, span):
    return ref if span == ALL else ref.at[pl.ds(span[0], span[1])]


def _rider_ag(items):
    ins, out_shape, aliases, where = [], [], {}, []
    n_remote = n_local = 0
    for t, (shard, buf, snd, fwd) in enumerate(items):
        i_shard = i_buf = None
        if snd is not None:
            i_shard = len(ins)
            ins.append(shard)
        if buf is not None:
            i_buf = len(ins)
            ins.append(buf)
            aliases[i_buf] = t
            out_shape.append(jax.ShapeDtypeStruct(buf.shape, buf.dtype))
        else:
            assert fwd is None and snd is not None
            out_shape.append(jax.ShapeDtypeStruct((N_DEV,) + shard.shape, shard.dtype))
        where.append((i_shard, i_buf, n_remote, n_local))
        n_remote += (4 if snd not in (None, LOCAL) else 0) + (3 if fwd is not None else 0)
        n_local += 1 if snd is not None else 0

    def plan(rins, routs, send, recv, loc, r0, l0):
        x, y, c = _coords()
        peers = [(x, y, 1 - c), (1 - x, y, c), (x, 1 - y, c), (1 - x, 1 - y, c)]
        remote, local = [], []
        for t, (shard, buf, snd, fwd) in enumerate(items):
            i_shard, i_buf, k, l = where[t]
            k, l = r0 + k, l0 + l
            if snd is not None:
                span = ALL if snd == LOCAL else snd
                src, dst = _rows(rins[i_shard], span), _rows(routs[t].at[_slot((x, y, c))], span)
                local.append(pltpu.make_async_copy(src, dst, loc.at[l]))
                for peer in (peers if snd != LOCAL else []):
                    remote.append(pltpu.make_async_remote_copy(
                        src_ref=src, dst_ref=dst, send_sem=send.at[k], recv_sem=recv.at[k],
                        device_id=peer, device_id_type=MESH))
                    k += 1
            if fwd is not None:
                for px, py, pc in peers[1:]:
                    s = _slot((px, py, pc))
                    remote.append(pltpu.make_async_remote_copy(
                        src_ref=_rows(rins[i_buf].at[s], fwd), dst_ref=_rows(routs[t].at[s], fwd),
                        send_sem=send.at[k], recv_sem=recv.at[k], device_id=peers[0], device_id_type=MESH))
                    k += 1
        return remote, local

    return _Rider(ins, out_shape, n_remote, n_local, plan, aliases)


def _gather_buffer(shard, me):
    return lax.dynamic_update_slice(lax.empty((N_DEV,) + shard.shape, shard.dtype), shard[None], (me, 0, 0))


def _rider_ag_remote(shards, me):
    n = len(shards)

    def plan(ins, outs, send, recv, loc, r0, l0):
        x, y, c = _coords()
        remote = []
        for t in range(n):
            dst = outs[t].at[_slot((x, y, c))]
            for k, peer in enumerate([(x, y, 1 - c), (1 - x, y, c), (x, 1 - y, c), (1 - x, 1 - y, c)]):
                remote.append(pltpu.make_async_remote_copy(
                    src_ref=ins[t], dst_ref=dst, send_sem=send.at[r0 + 4 * t + k], recv_sem=recv.at[r0 + 4 * t + k],
                    device_id=peer, device_id_type=MESH))
        return remote, []

    return _Rider(shards, [jax.ShapeDtypeStruct((N_DEV,) + s.shape, s.dtype) for s in shards], 4 * n, 0, plan,
                  lands=[_gather_buffer(s, me) for s in shards])


def _rider_rs_sibling(grads):
    n = len(grads)

    def plan(ins, outs, send, recv, loc, r0, l0):
        x, y, c = _coords()
        remote = []
        for t in range(n):
            for q in range(4):
                remote.append(pltpu.make_async_remote_copy(
                    src_ref=ins[t].at[q, 1 - c], dst_ref=outs[t].at[q], send_sem=send.at[r0 + 4 * t + q],
                    recv_sem=recv.at[r0 + 4 * t + q], device_id=(x, y, 1 - c), device_id_type=MESH))
        return remote, []

    return _Rider(grads, [jax.ShapeDtypeStruct((4,) + g.shape[2:], g.dtype) for g in grads], 4 * n, 0, plan)


def _rider_rs_chips(sums, rows=None, into=None):
    n = len(sums)
    rows = rows or [ALL] * n

    def plan(ins, outs, send, recv, loc, r0, l0):
        x, y, c = _coords()
        remote = []
        for t in range(n):
            for r, (px, py) in enumerate([(1 - x, y), (x, 1 - y), (1 - x, 1 - y)]):
                remote.append(pltpu.make_async_remote_copy(
                    src_ref=_rows(ins[t].at[2 * px + py], rows[t]), dst_ref=_rows(outs[t].at[r], rows[t]),
                    send_sem=send.at[r0 + 3 * t + r], recv_sem=recv.at[r0 + 3 * t + r],
                    device_id=(px, py, c), device_id_type=MESH))
        return remote, []

    out_shape = [jax.ShapeDtypeStruct((3,) + s.shape[1:], s.dtype) for s in sums]
    if into is None:
        return _Rider(sums, out_shape, 3 * n, 0, plan)
    return _Rider(list(sums) + list(into), out_shape, 3 * n, 0, plan, aliases={n + t: t for t in range(n)})


def _rider_gather_remote(parts):
    n = len(parts)

    def plan(ins, outs, send, recv, loc, r0, l0):
        x, y, c = _coords()
        me = _slot((x, y, c))
        remote = []
        for t in range(n):
            for k in range(1, N_DEV):
                peer = (x ^ ((k >> 2) & 1), y ^ ((k >> 1) & 1), c ^ (k & 1))
                remote.append(pltpu.make_async_remote_copy(
                    src_ref=ins[t], dst_ref=outs[t].at[me], send_sem=send.at[r0 + 7 * t + k - 1],
                    recv_sem=recv.at[r0 + 7 * t + k - 1], device_id=peer, device_id_type=MESH))
        return remote, []

    return _Rider(parts, [jax.ShapeDtypeStruct((N_DEV,) + p.shape, p.dtype) for p in parts], 7 * n, 0, plan)


def _chip_sum_call(idx, grads, recvd, out_dtypes, name):
    n = len(grads)

    def body(i_ref, *refs):
        for t in range(n):
            refs[2 * n + t][0] = (refs[t][0, 0] + refs[n + t][0]).astype(out_dtypes[t])

    def chip(k, s):
        return jnp.where(k >= s[0], k + 1, k)

    in_specs = [pl.BlockSpec((1, 1) + g.shape[2:], lambda k, s: (chip(k, s), s[1], 0, 0)) for g in grads]
    in_specs += [pl.BlockSpec((1,) + r.shape[1:], lambda k, s: (chip(k, s), 0, 0)) for r in recvd]
    return pl.pallas_call(
        body, name=name,
        grid_spec=pltpu.PrefetchScalarGridSpec(
            num_scalar_prefetch=1, grid=(3,), in_specs=in_specs,
            out_specs=[pl.BlockSpec((1,) + r.shape[1:], lambda k, s: (chip(k, s), 0, 0)) for r in recvd]),
        out_shape=[jax.ShapeDtypeStruct(r.shape, dt) for r, dt in zip(recvd, out_dtypes)],
        compiler_params=_params(("arbitrary",)),
    )(idx, *grads, *recvd)


def _final_sum_call(idx, grads, recvd1, recvd2):
    n = len(grads)
    nsteps = 2

    def body(i_ref, *refs):
        for t in range(n):
            g, r1, r2, o = refs[t], refs[n + t], refs[2 * n + t], refs[3 * n + t]
            s = g[0, 0] + r1[0]
            for r in range(3):
                s = s + r2[r].astype(F32)
            o[...] = s

    def rows(a):
        r = a.shape[-2]
        return r // nsteps if (r // nsteps) % 16 == 0 else r

    def step(a):
        return (lambda i: i) if rows(a) != a.shape[-2] else (lambda i: 0)

    in_specs = [pl.BlockSpec((1, 1, rows(g), g.shape[3]), lambda i, s, st=step(g): (s[0], s[1], st(i), 0)) for g in grads]
    in_specs += [pl.BlockSpec((1, rows(r), r.shape[2]), lambda i, s, st=step(r): (s[0], st(i), 0)) for r in recvd1]
    in_specs += [pl.BlockSpec((3, rows(r), r.shape[2]), lambda i, s, st=step(r): (0, st(i), 0)) for r in recvd2]
    return pl.pallas_call(
        body, name="rs_final_sum",
        grid_spec=pltpu.PrefetchScalarGridSpec(
            num_scalar_prefetch=1, grid=(nsteps,), in_specs=in_specs,
            out_specs=[pl.BlockSpec((rows(r), r.shape[2]), lambda i, s, st=step(r): (st(i), 0)) for r in recvd2]),
        out_shape=[jax.ShapeDtypeStruct(r.shape[1:], F32) for r in recvd2],
        compiler_params=_params(("arbitrary",)),
    )(idx, *grads, *recvd1, *recvd2)


def _sum8_call(parts):
    def body(p_ref, o_ref):
        s = p_ref[0]
        for j in range(1, N_DEV):
            s = s + p_ref[j]
        o_ref[...] = s

    return pl.pallas_call(body, name="sum_small_partials",
                          out_shape=jax.ShapeDtypeStruct(parts.shape[1:], parts.dtype))(parts)


def _adamw(w, g, m, v):
    m = ADAM_B1 * m + (1.0 - ADAM_B1) * g
    v = ADAM_B2 * v + (1.0 - ADAM_B2) * (g * g)
    m_hat = m / (1.0 - ADAM_B1 ** ADAM_STEP)
    v_hat = v / (1.0 - ADAM_B2 ** ADAM_STEP)
    delta = -ADAM_LR * (m_hat / (jnp.sqrt(v_hat) + ADAM_EPS) + ADAM_WD * w)
    return delta, m, v


def _adamw_call(ws, gs, ms, vs, nsteps, name):
    n = len(ws)

    def body(*refs):
        for t in range(n):
            w, g, m, v = (refs[k * n + t][...] for k in range(4))
            d, m2, v2 = _adamw(w, g, m, v)
            refs[4 * n + t][...] = d
            refs[5 * n + t][...] = m2
            refs[6 * n + t][...] = v2

    def spec(a):
        assert a.shape[0] % nsteps == 0 and (nsteps == 1 or (a.shape[0] // nsteps) % 8 == 0), a.shape
        return pl.BlockSpec((a.shape[0] // nsteps, a.shape[1]), lambda i: (i, 0))

    specs = [spec(a) for a in ws]
    outs = pl.pallas_call(
        body, name=name, grid=(nsteps,),
        in_specs=specs * 4, out_specs=specs * 3,
        out_shape=[jax.ShapeDtypeStruct(a.shape, F32) for a in ws] * 3,
        compiler_params=_params(("arbitrary",)),
    )(*ws, *gs, *ms, *vs)
    return outs[:n], outs[n:2 * n], outs[2 * n:]


def _adamw_rs_call(idx, after, gws, r1s, r2s, ws, ms, vs, nsteps, name):
    n = len(ws)

    def body(i_ref, after_ref, *refs):
        for t in range(n):
            gw, r1, r2, w, m, v = (refs[k * n + t] for k in range(6))
            g = gw[0, 0] + r1[0]
            for r in range(3):
                g = g + r2[r].astype(F32)
            d, m2, v2 = _adamw(w[...], g, m[...], v[...])
            refs[6 * n + t][...] = g
            refs[7 * n + t][...] = d
            refs[8 * n + t][...] = m2
            refs[9 * n + t][...] = v2

    def rb(a):
        r = a.shape[0] // nsteps
        assert a.shape[0] % nsteps == 0 and r % 16 == 0, a.shape
        return r

    in_specs = [pl.BlockSpec((1, 1, rb(w), w.shape[1]), lambda i, s: (s[0], s[1], i, 0)) for w in ws]
    in_specs += [pl.BlockSpec((1, rb(w), w.shape[1]), lambda i, s: (s[0], i, 0)) for w in ws]
    in_specs += [pl.BlockSpec((3, rb(w), w.shape[1]), lambda i, s: (0, i, 0)) for w in ws]
    plain = [pl.BlockSpec((rb(w), w.shape[1]), lambda i, s: (i, 0)) for w in ws]
    outs = pl.pallas_call(
        body, name=name,
        grid_spec=pltpu.PrefetchScalarGridSpec(
            num_scalar_prefetch=1, grid=(nsteps,),
            in_specs=[pl.BlockSpec(memory_space=pl.ANY)] + in_specs + plain * 3, out_specs=plain * 4),
        out_shape=[jax.ShapeDtypeStruct(w.shape, F32) for w in ws] * 4,
        compiler_params=_params(("arbitrary",)),
    )(idx, after, *gws, *r1s, *r2s, *ws, *ms, *vs)
    return outs[:n], outs[n:2 * n], outs[2 * n:3 * n], outs[3 * n:]


def _rows128(a, pad_rows):
    flat = a.reshape(-1).astype(F32)
    flat = jnp.pad(flat, (0, pad_rows * LANES - flat.shape[0]))
    return flat.reshape(pad_rows, LANES)


_SMALL_A = (("w_pool", 512), ("pool_scale", 8), ("attn_sinks", 8), ("g_mix_post", 8), ("g_mlp_pre", 8),
            ("g_mlp_post", 8), ("loss", 8), ("b_in_gates", 16))
_SMALL_A_ROWS = 640
_SMALL_B = (("g_mix_pre", 8), ("b_in_head", 16))


def _pack(parts, layout, total_rows):
    rows = [_rows128(parts[k], r) for k, r in layout]
    pad = total_rows - sum(r for _, r in layout)
    if pad:
        rows.append(jnp.zeros((pad, LANES), F32))
    return jnp.concatenate(rows, axis=0)


def _unpack(buf, layout, sizes):
    out, off = {}, 0
    for k, r in layout:
        out[k] = buf[off:off + r].reshape(-1)[:sizes[k]]
        off += r
    return out


def kernel(x, g_mix_pre, w_in, b_in, w_pool, pool_scale, attn_sinks, w_branch_pool, w_branch_attn, w_out, g_mix_post, g_mlp_pre, w_up, w_down, g_mlp_post, loss_target, m_g_mix_pre, m_w_in, m_b_in, m_w_pool, m_pool_scale, m_attn_sinks, m_w_branch_pool, m_w_branch_attn, m_w_out, m_g_mix_post, m_g_mlp_pre, m_w_up, m_w_down, m_g_mlp_post, v_g_mix_pre, v_w_in, v_b_in, v_w_pool, v_pool_scale, v_attn_sinks, v_w_branch_pool, v_w_branch_attn, v_w_out, v_g_mix_post, v_g_mlp_pre, v_w_up, v_w_down, v_g_mlp_post):
    B, S, _ = x.shape
    T = B * S
    xt = x.reshape(T, D_MODEL)
    tgt = loss_target.reshape(T, D_MODEL)
    cx, cy, cc = _coords()

    cidx = jnp.stack([2 * cx + cy, cc]).astype(jnp.int32)
    by_chip = lambda gr: gr.reshape((4, 2) + gr.shape[1:])
    bf = lambda w: w[0].astype(MXU_DTYPE)

    me = _slot((cx, cy, cc))
    win_l = w_in[0].T.astype(MXU_DTYPE)
    (c_win,), _ = _copies_start([_rider_ag_first(win_l, me)], "allgather_first")
    wpool_b = bf(w_pool)
    rc, rsa, rsb = _rot_tables(S)
    wbp_l, wba_l, wout_l, wup_l, wdown_l = bf(w_branch_pool), bf(w_branch_attn), bf(w_out), bf(w_up), bf(w_down)
    gathers = [_rider_ag_remote([wbp_l, wba_l, wout_l], me), _rider_ag_remote([wup_l], me), _rider_ag_remote([wdown_l], me)]
    c_win = _copies_pass([c_win], [_rider_ag_onward(win_l, 2)], [b for r in gathers for b in r.lands] + [rc, rsa, rsb],
                         "allgather_second")
    c_win = _copies_pass(c_win, [_rider_ag_onward(win_l, 3)], [wbp_l, wba_l, wout_l, wup_l, wdown_l], "allgather_third")
    (win_s,) = _copies_wait(c_win, wpool_b, "allgather_weights")
    win_t = win_s.reshape(IN_WIDTH, D_MODEL)

    (c_br, c_up, c_dn), tok = _copies_start(gathers, "allgather_start", after=win_s)
    (h, u, q, k4, v4, g), _ = _inproj_call(xt, g_mix_pre, win_t, b_in, rc, rsa, rsb, S, rider=_after(tok))
    wbp_1, wba_1, wout_1 = _copies_wait([c_br], h, "allgather_wait_branch")
    (yp,), (wbp_s, wba_s, wout_s) = _pool_call(
        u, wpool_b, pool_scale, S,
        rider=_rider_ag([(None, wbp_1, None, ALL), (None, wba_1, None, ALL), (None, wout_1, None, ALL)]))
    (ya,) = _attn_call(attn_sinks, q, k4, v4, S)
    wout_f = wout_s.reshape(D_MODEL, D_MODEL)
    (wup_1,) = _copies_wait([c_up], ya, "allgather_wait_up")
    (mix, x1, h2), (wup_s,) = _mix_fwd_call(
        yp, ya, g, xt, wbp_s, wba_s, wout_f, g_mix_post, g_mlp_pre, rider=_rider_ag([(None, wup_1, None, ALL)]))
    act = _mlp_up_call(h2, wup_s)
    (wdown_1,) = _copies_wait([c_dn], act, "allgather_wait_down")
    (wdown_s,) = _comm_call(_rider_ag([(None, wdown_1, None, ALL)]), "allgather_pass_down")

    da, dff, dx1, dg3, dg4, lossvec = _mlp_call(x1, act, tgt, wup_s, wdown_s, g_mlp_pre, g_mlp_post)
    gw_up = by_chip(_wgrad_cols_call(h2, da, "wgrad_up")[0])
    (gw_down,), (r1_up,) = _wgrad_rows_call(act, dff, "wgrad_down", rider=_rider_rs_sibling([gw_up]))
    gw_down = by_chip(gw_down)
    (s_up,) = _chip_sum_call(cidx, [gw_up], [r1_up], [MXU_DTYPE], "rs_chip_sum_up")
    (dyp, do, dgates, dg2, dbg, gw_out, gw_bp, gw_ba), _ = _mix_bwd_call(
        dx1, mix, yp, ya, g, wbp_s, wba_s, wout_f, g_mix_post, rider=_after(s_up))
    gw_out = by_chip(gw_out.reshape(N_DEV, D_MODEL // N_DEV, D_MODEL))
    gw_bp, gw_ba = by_chip(gw_bp), by_chip(gw_ba)
    (c_up,), tok = _copies_start([_rider_rs_chips([s_up])], "rs_chips_start_up", after=dyp)
    (dq, dk, dv, dsink), (r1_down, r1_out, r1_bp, r1_ba) = _attn_bwd_call(
        attn_sinks, q, k4, v4, do, rc, rsa, rsb, S,
        rider=_after(tok, _rider_rs_sibling([gw_down, gw_out, gw_bp, gw_ba])))
    s_down, *s_obb = _chip_sum_call(cidx, [gw_down, gw_out, gw_bp, gw_ba], [r1_down, r1_out, r1_bp, r1_ba],
                                    [MXU_DTYPE] * 4, "rs_chip_sum_branch")
    (c_obb,), tok = _copies_start([_rider_rs_chips([s_down] + s_obb)], "rs_chips_start_branch")
    (du, dwp, dps), _ = _pool_bwd_call(u, dyp, wpool_b, pool_scale, S, rider=_after(tok))
    (gw_in,) = _wgrad_in_call(du, dq, dk, dv, dgates, h)
    gw_in = by_chip(gw_in)

    small_a = {"w_pool": dwp, "pool_scale": dps,
               "attn_sinks": jnp.sum(dsink.reshape(B, 8, LANES)[:, 0, :N_Q_HEADS], axis=0), "g_mix_post": dg2,
               "g_mlp_pre": dg3, "g_mlp_post": dg4, "loss": lossvec, "b_in_gates": dbg}
    gw_sa = by_chip(_pack(small_a, _SMALL_A, _SMALL_A_ROWS).reshape(N_DEV, _SMALL_A_ROWS // N_DEV, LANES))
    r1_in, r1_sa = _comm_call(_rider_rs_sibling([gw_in, gw_sa]), "rs_sibling_in")
    s_in, s_sa = _chip_sum_call(cidx, [gw_in, gw_sa], [r1_in, r1_sa], [MXU_DTYPE, F32], "rs_chip_sum_in")
    (c_in,), tok = _copies_start([_rider_rs_chips([s_in, s_sa])], "rs_chips_start_in")
    (gx, dg1, dba_in), _ = _inproj_bwd_call(du, dq, dk, dv, dgates, dx1, xt, win_t, g_mix_pre, rider=_after(tok))
    r2_up, r2_down, r2_out, r2_bp, r2_ba, r2_in, r2_sa = _copies_wait([c_up, c_obb, c_in], dg1, "rs_chips_wait")

    (g_sa,) = _final_sum_call(cidx, [gw_sa], [r1_sa], [r2_sa])
    part_b = _pack({"g_mix_pre": dg1, "b_in_head": dba_in}, _SMALL_B, sum(r for _, r in _SMALL_B))
    (c_small,), tok = _copies_start([_rider_gather_remote([g_sa, part_b])], "allgather_small_start")

    in_t = _adamw_rs_call(cidx, tok, [gw_in], [r1_in], [r2_in], [w_in[0].T], [m_w_in[0].T], [v_w_in[0].T], 2,
                          "adamw_w_in")
    rest = _adamw_rs_call(
        cidx, tok, [gw_bp, gw_ba, gw_out, gw_up, gw_down], [r1_bp, r1_ba, r1_out, r1_up, r1_down],
        [r2_bp, r2_ba, r2_out, r2_up, r2_down], [w_branch_pool[0], w_branch_attn[0], w_out[0], w_up[0], w_down[0]],
        [m_w_branch_pool[0], m_w_branch_attn[0], m_w_out[0], m_w_up[0], m_w_down[0]],
        [v_w_branch_pool[0], v_w_branch_attn[0], v_w_out[0], v_w_up[0], v_w_down[0]], N_DEV, "adamw_shards")
    big_g, big_d, big_m2, big_v2 = ([a[0].T] + list(b) for a, b in zip(in_t, rest))

    sa_all, sb_all = _copies_wait([c_small], rest[0][0], "allgather_small_wait")
    sa_all = lax.dynamic_update_slice(sa_all, g_sa[None], (me, 0, 0))
    sb_sum = _sum8_call(lax.dynamic_update_slice(sb_all, part_b[None], (me, 0, 0)))

    names = ["g_mix_pre", "b_in", "w_pool", "pool_scale", "attn_sinks", "g_mix_post", "g_mlp_pre", "g_mlp_post"]
    sm_w = dict(g_mix_pre=g_mix_pre, b_in=b_in, w_pool=w_pool, pool_scale=pool_scale, attn_sinks=attn_sinks,
                g_mix_post=g_mix_post, g_mlp_pre=g_mlp_pre, g_mlp_post=g_mlp_post)
    sm_m = dict(g_mix_pre=m_g_mix_pre, b_in=m_b_in, w_pool=m_w_pool, pool_scale=m_pool_scale, attn_sinks=m_attn_sinks,
                g_mix_post=m_g_mix_post, g_mlp_pre=m_g_mlp_pre, g_mlp_post=m_g_mlp_post)
    sm_v = dict(g_mix_pre=v_g_mix_pre, b_in=v_b_in, w_pool=v_w_pool, pool_scale=v_pool_scale, attn_sinks=v_attn_sinks,
                g_mix_post=v_g_mix_post, g_mlp_pre=v_g_mlp_pre, g_mlp_post=v_g_mlp_post)
    sizes = {k: sm_w[k].size for k in names}
    sizes.update(loss=D_MODEL, b_in_gates=GATE_WIDTH, b_in_head=C_G)
    sm_g = _unpack(sa_all.reshape(_SMALL_A_ROWS, LANES), _SMALL_A, sizes)
    sm_g.update(_unpack(sb_sum, _SMALL_B, sizes))
    sm_g["b_in"] = jnp.concatenate([sm_g["b_in_head"], sm_g["b_in_gates"]])
    loss = (0.5 / D_MODEL) * jnp.sum(sm_g["loss"])
    two_d = lambda a: a.reshape(-1, a.shape[-1])
    sd_, sm2_, sv2_ = _adamw_call([two_d(sm_w[k]) for k in names], [two_d(sm_g[k].reshape(sm_w[k].shape)) for k in names],
                                  [two_d(sm_m[k]) for k in names], [two_d(sm_v[k]) for k in names], 1, "adamw_small")
    like = lambda vals: {k: a.reshape(sm_w[k].shape) for k, a in zip(names, vals)}
    sm_d, sm_m2, sm_v2 = like(sd_), like(sm2_), like(sv2_)
    sm_gr = {k: sm_g[k].reshape(sm_w[k].shape) for k in names}

    order = ["g_mix_pre", "w_in", "b_in", "w_pool", "pool_scale", "attn_sinks", "w_branch_pool", "w_branch_attn",
             "w_out", "g_mix_post", "g_mlp_pre", "w_up", "w_down", "g_mlp_post"]
    big_names = ["w_in", "w_branch_pool", "w_branch_attn", "w_out", "w_up", "w_down"]
    lead = lambda a: a[None]
    tables = []
    for small_t, big_t in ((sm_gr, big_g), (sm_d, big_d), (sm_m2, big_m2), (sm_v2, big_v2)):
        bt = dict(zip(big_names, big_t))
        tables.append([lead(bt[k]) if k in bt else small_t[k] for k in order])
    return (loss, gx.reshape(B, S, D_MODEL), *tables[0], *tables[1], *tables[2], *tables[3])
```

```python
import jax
import jax.numpy as jnp
from jax import lax
from jax.experimental import pallas as pl
from jax.experimental.pallas import tpu as pltpu

F32 = jnp.float32
MXU_DTYPE = jnp.bfloat16
MESH = pl.DeviceIdType.MESH

D_MODEL = 1024
POOL_WINDOWS = (2, 4, 8, 16)
POOL_WIDTH = 512
POOL_GC = 128
HEAD_DIM = 64
N_Q_HEADS = 8
N_KV_HEADS = 2
GROUP = 4
ATTN_WIDTH = 512
KV_WIDTH = 128
BLOCK = 128
GATE_WIDTH = 2048
IN_WIDTH = 3328
D_FF = 4096
EPS = 1e-6
NEG_INF = -1e30
ROPE_THETA = 500000.0
ROT_DIM = 16
SCALE = HEAD_DIM ** -0.5
C_Q, C_K, C_V, C_G = 512, 1024, 1152, 1280

ADAM_LR = 0.001
ADAM_B1 = 0.9
ADAM_B2 = 0.999
ADAM_EPS = 1e-08
ADAM_WD = 0.01
ADAM_STEP = 10

N_DEV = 8
LANES = 128
VMEM_LIMIT = 56 * 1024 * 1024

NN = (((1,), (0,)), ((), ()))
NT = (((1,), (1,)), ((), ()))
TN = (((0,), (0,)), ((), ()))


def _dot(a, b, dims):
    return lax.dot_general(a, b, dims, preferred_element_type=F32)


def _params(sem=None):
    return pltpu.CompilerParams(dimension_semantics=sem, vmem_limit_bytes=VMEM_LIMIT)


def _tile(n, pref):
    t = min(n, pref)
    assert n % t == 0, (n, t)
    return t


class _Rider:
    def __init__(self, ins, out_shape, n_remote, n_local, plan, aliases=None, lands=None):
        self.ins, self.out_shape, self.n_remote, self.n_local = list(ins), list(out_shape), n_remote, n_local
        self.plan, self.aliases = plan, dict(aliases or {})
        self.lands = lands


def _after(token, rider=None):
    r = rider or _Rider([], [], 0, 0, lambda ins, outs, send, recv, loc, r0, l0: ([], []))
    return _Rider(r.ins + [token], r.out_shape, r.n_remote, r.n_local, r.plan, r.aliases)


def _join(a, b):
    assert not a.aliases and not b.aliases
    n_in, n_out = len(a.ins), len(a.out_shape)

    def plan(ins, outs, send, recv, loc, r0, l0):
        ra, la = a.plan(ins[:n_in], outs[:n_out], send, recv, loc, r0, l0)
        rb, lb = b.plan(ins[n_in:], outs[n_out:], send, recv, loc, r0 + a.n_remote, l0 + a.n_local)
        return ra + rb, la + lb

    return _Rider(a.ins + b.ins, a.out_shape + b.out_shape, a.n_remote + b.n_remote, a.n_local + b.n_local, plan)


def _launch(body, args, *, name, grid, in_specs, out_specs, out_shape, scratch_shapes=(), sem=None, rider=None):
    if rider is None:
        return pl.pallas_call(body, name=name, grid=grid, in_specs=in_specs, out_specs=out_specs, out_shape=out_shape,
                              scratch_shapes=list(scratch_shapes), compiler_params=_params(sem))(*args)
    n_in, n_out, n_scr = len(args), len(out_shape), len(scratch_shapes)
    r_in, r_out = len(rider.ins), len(rider.out_shape)
    copies = rider.n_remote + rider.n_local > 0

    def wrapped(*refs):
        ins, rins = refs[:n_in], refs[n_in:n_in + r_in]
        o0 = n_in + r_in
        outs, routs = refs[o0:o0 + n_out], refs[o0 + n_out:o0 + n_out + r_out]
        s0 = o0 + n_out + r_out
        scr = refs[s0:s0 + n_scr]
        if not copies:
            return body(*ins, *outs, *scr)
        send, recv, loc = refs[s0 + n_scr:]
        first, last = None, None
        for d in range(len(grid)):
            f, l = pl.program_id(d) == 0, pl.program_id(d) == pl.num_programs(d) - 1
            first = f if first is None else first & f
            last = l if last is None else last & l

        def start():
            remote, local = rider.plan(rins, routs, send, recv, loc, 0, 0)
            for cp in local + remote:
                cp.start()

        def finish():
            remote, local = rider.plan(rins, routs, send, recv, loc, 0, 0)
            for cp in remote + local:
                cp.wait()

        if first is None:
            start()
            body(*ins, *outs, *scr)
            finish()
        else:
            pl.when(first)(start)
            body(*ins, *outs, *scr)
            pl.when(last)(finish)

    hbm = pl.BlockSpec(memory_space=pl.ANY)
    dma = pltpu.SemaphoreType.DMA
    res = pl.pallas_call(
        wrapped, name=name, grid=grid, in_specs=list(in_specs) + [hbm] * r_in,
        out_specs=list(out_specs) + [hbm] * r_out, out_shape=list(out_shape) + rider.out_shape,
        scratch_shapes=list(scratch_shapes) + (
            [dma((max(rider.n_remote, 1),)), dma((max(rider.n_remote, 1),)), dma((max(rider.n_local, 1),))] if copies else []),
        input_output_aliases={n_in + i: n_out + o for i, o in rider.aliases.items()},
        compiler_params=_params(sem),
    )(*args, *rider.ins)
    return list(res[:n_out]), list(res[n_out:])


def _comm_call(rider, name):
    return _launch(lambda: None, [], name=name, grid=(), in_specs=[], out_specs=[], out_shape=[], rider=rider)[1]


_HBM = pl.BlockSpec(memory_space=pltpu.HBM)
_SEM = pl.BlockSpec(memory_space=pltpu.SEMAPHORE)
_EFFECT = pltpu.SideEffectType.DATAFLOW_SIDE_EFFECTING


def _copies_start(riders, name, after=None):
    assert all(r.n_local == 0 and not r.aliases for r in riders)
    extra = [] if after is None else [after]
    sizes = [(len(r.ins), len(r.out_shape)) for r in riders]
    bufs = []
    for r in riders:
        lands = r.lands or [lax.empty(s.shape, s.dtype) for s in r.out_shape]
        bufs += [pltpu.with_memory_space_constraint(a, pltpu.HBM) for a in list(r.ins) + list(lands)]
    nb, ng, ne = len(bufs), len(riders), len(extra)

    def body(*refs):
        sems, token, at = refs[2 * nb + ne:2 * nb + ne + 2 * ng], refs[-1], 0
        for g, (r, (ni, no)) in enumerate(zip(riders, sizes)):
            remote, _ = r.plan(refs[at:at + ni], refs[at + ni:at + ni + no], sems[2 * g], sems[2 * g + 1], None, 0, 0)
            for cp in remote:
                cp.start()
            at += ni + no
        token[...] = jnp.zeros_like(token)

    res = pl.pallas_call(
        body, name=name, in_specs=[_HBM] * nb + [pl.BlockSpec(memory_space=pl.ANY)] * ne,
        out_specs=[_HBM] * nb + [_SEM] * (2 * ng) + [pl.BlockSpec(memory_space=pltpu.VMEM)],
        out_shape=[pltpu.HBM(a.shape, a.dtype) for a in bufs]
        + [pltpu.SemaphoreType.DMA((r.n_remote,)) for r in riders for _ in range(2)]
        + [jax.ShapeDtypeStruct((8, LANES), F32)],
        input_output_aliases={i: i for i in range(nb)},
        compiler_params=pltpu.CompilerParams(has_side_effects=_EFFECT),
    )(*bufs, *extra)
    handles, at = [], 0
    for g, (r, (ni, no)) in enumerate(zip(riders, sizes)):
        handles.append((r, list(res[at:at + ni + no]), res[nb + 2 * g], res[nb + 2 * g + 1]))
        at += ni + no
    return handles, res[-1]


def _copies_wait(handles, after, name):
    bufs = [b for _, bs, _, _ in handles for b in bs]
    sems = [s for _, _, send, recv in handles for s in (send, recv)]
    nb, ng = len(bufs), len(handles)
    after = list(after) if isinstance(after, (list, tuple)) else [after]

    def body(*refs):
        at = 0
        for g, (rider, bs, _, _) in enumerate(handles):
            ni = len(rider.ins)
            remote, _ = rider.plan(refs[at:at + ni], refs[at + ni:at + len(bs)], refs[nb + 2 * g], refs[nb + 2 * g + 1],
                                   None, 0, 0)
            for cp in remote:
                cp.wait_send()
                cp.wait_recv()
            at += len(bs)

    res = pl.pallas_call(
        body, name=name, in_specs=[_HBM] * nb + [_SEM] * (2 * ng) + [pl.BlockSpec(memory_space=pl.ANY)] * len(after),
        out_specs=[_HBM] * nb, out_shape=[pltpu.HBM(a.shape, a.dtype) for a in bufs],
        input_output_aliases={i: i for i in range(nb)},
        compiler_params=pltpu.CompilerParams(has_side_effects=_EFFECT),
    )(*bufs, *sems, *after)
    lands, at = [], 0
    for rider, bs, _, _ in handles:
        lands += list(res[at + len(rider.ins):at + len(bs)])
        at += len(bs)
    return lands


def _copies_pass(handles, riders, after, name):
    bufs = [b for _, bs, _, _ in handles for b in bs]
    sems = [s for _, _, send, recv in handles for s in (send, recv)]
    nb, ng = len(bufs), len(handles)
    after = list(after) if isinstance(after, (list, tuple)) else [after]

    def body(*refs):
        new_sems, at = refs[2 * nb + 2 * ng + len(after):], 0
        for g, ((rider, bs, _, _), then) in enumerate(zip(handles, riders)):
            ins, outs = refs[at:at + len(rider.ins)], refs[at + len(rider.ins):at + len(bs)]
            for cp in rider.plan(ins, outs, refs[nb + 2 * g], refs[nb + 2 * g + 1], None, 0, 0)[0]:
                cp.wait_send()
                cp.wait_recv()
            for cp in then.plan(ins, outs, new_sems[2 * g], new_sems[2 * g + 1], None, 0, 0)[0]:
                cp.start()
            at += len(bs)

    res = pl.pallas_call(
        body, name=name, in_specs=[_HBM] * nb + [_SEM] * (2 * ng) + [pl.BlockSpec(memory_space=pl.ANY)] * len(after),
        out_specs=[_HBM] * nb + [_SEM] * (2 * ng),
        out_shape=[pltpu.HBM(a.shape, a.dtype) for a in bufs]
        + [pltpu.SemaphoreType.DMA((r.n_remote,)) for r in riders for _ in range(2)],
        input_output_aliases={i: i for i in range(nb)},
        compiler_params=pltpu.CompilerParams(has_side_effects=_EFFECT),
    )(*bufs, *sems, *after)
    new, at = [], 0
    for g, ((_, bs, _, _), then) in enumerate(zip(handles, riders)):
        new.append((then, list(res[at:at + len(bs)]), res[nb + 2 * g], res[nb + 2 * g + 1]))
        at += len(bs)
    return new


def _rms_r(x):
    return lax.rsqrt(jnp.mean(x * x, axis=-1, keepdims=True) + EPS)


def _rms_bwd(dn, x, r, g):
    xh = x * r
    dxh = dn * g
    dx = r * (dxh - xh * jnp.mean(dxh * xh, axis=-1, keepdims=True))
    return dx, dn * xh


def _rot(t, c, sa, sb):
    outs = []
    for j in range(t.shape[1] // LANES):
        tj = t[:, LANES * j:LANES * (j + 1)]
        outs.append(tj * c + pltpu.roll(tj, LANES - 8, 1) * sa + pltpu.roll(tj, 8, 1) * sb)
    return outs[0] if len(outs) == 1 else jnp.concatenate(outs, axis=1)


def _rot_tables(S):
    pos = jnp.arange(S, dtype=F32)
    inv_freq = ROPE_THETA ** (-jnp.arange(0, ROT_DIM, 2, dtype=F32) / ROT_DIM)
    ang = pos[:, None] * inv_freq[None, :]
    cos, sin = jnp.cos(ang), jnp.sin(ang)
    one = jnp.ones((S, HEAD_DIM - ROT_DIM), F32)
    zero = jnp.zeros((S, HEAD_DIM - ROT_DIM), F32)
    z8 = jnp.zeros((S, 8), F32)
    c = jnp.concatenate([cos, cos, one], axis=1)
    sa = jnp.concatenate([-sin, z8, zero], axis=1)
    sb = jnp.concatenate([z8, sin, zero], axis=1)
    rep = LANES // HEAD_DIM
    return jnp.tile(c, (1, rep)), jnp.tile(sa, (1, rep)), jnp.tile(sb, (1, rep))


def _lane_tile4(k):
    lane = lax.broadcasted_iota(jnp.int32, k.shape, 1)
    rk = pltpu.roll(k, HEAD_DIM, 1)
    x0 = jnp.where(lane < HEAD_DIM, k, rk)
    x1 = jnp.where(lane < HEAD_DIM, rk, k)
    return jnp.concatenate([x0, x0, x1, x1], axis=1)


def _fold_heads(acc):
    zs = []
    for hk in range(N_KV_HEADS):
        a = acc[:, 256 * hk:256 * hk + LANES] + acc[:, 256 * hk + LANES:256 * (hk + 1)]
        zs.append(a + pltpu.roll(a, HEAD_DIM, 1))
    lane = lax.broadcasted_iota(jnp.int32, zs[0].shape, 1)
    return jnp.where(lane < HEAD_DIM, zs[0], zs[1])


def _inproj_call(x, g1, win_t, b_in, rc, rsa, rsb, S, rider=None):
    T = x.shape[0]
    tm = _tile(S, 512)
    nst = S // tm

    def body(x_ref, g1_ref, w_ref, b_ref, c_ref, sa_ref, sb_ref,
             h_ref, u_ref, q_ref, k4_ref, v4_ref, g_ref):
        xv = x_ref[...]
        hb = ((xv * _rms_r(xv)) * g1_ref[...]).astype(MXU_DTYPE)
        h_ref[...] = hb

        def proj(lo, hi):
            return _dot(hb, w_ref[lo:hi, :], NT) + b_ref[:, lo:hi]

        c, sa, sb = c_ref[...], sa_ref[...], sb_ref[...]
        u_ref[...] = proj(0, C_Q)
        q_ref[...] = (_rot(proj(C_Q, C_K), c, sa, sb) * SCALE).astype(MXU_DTYPE)
        kv = proj(C_K, C_G)
        k4_ref[...] = _lane_tile4(_rot(kv[:, :KV_WIDTH], c, sa, sb)).astype(MXU_DTYPE)
        v4_ref[...] = _lane_tile4(kv[:, KV_WIDTH:]).astype(MXU_DTYPE)
        g_ref[...] = jax.nn.sigmoid(proj(C_G, IN_WIDTH)).astype(MXU_DTYPE)

    tok = lambda w: pl.BlockSpec((tm, w), lambda i: (i, 0))
    full = lambda a: pl.BlockSpec(a.shape, lambda i: (0,) * a.ndim)
    tab = pl.BlockSpec((tm, LANES), lambda i: (i % nst, 0))
    return _launch(
        body, [x, g1, win_t, b_in, rc, rsa, rsb], name="inproj_fwd", grid=(T // tm,),
        in_specs=[tok(D_MODEL), full(g1), full(win_t), full(b_in), tab, tab, tab],
        out_specs=[tok(D_MODEL), tok(POOL_WIDTH), tok(ATTN_WIDTH), tok(512), tok(512), tok(GATE_WIDTH)],
        out_shape=[jax.ShapeDtypeStruct((T, D_MODEL), MXU_DTYPE), jax.ShapeDtypeStruct((T, POOL_WIDTH), F32),
                   jax.ShapeDtypeStruct((T, ATTN_WIDTH), MXU_DTYPE), jax.ShapeDtypeStruct((T, 512), MXU_DTYPE),
                   jax.ShapeDtypeStruct((T, 512), MXU_DTYPE), jax.ShapeDtypeStruct((T, GATE_WIDTH), MXU_DTYPE)],
        sem=("arbitrary",), rider=rider)


def _shift_rows(a, k, rows):
    n = a.shape[0]
    if k > 0:
        return jnp.where(rows >= k, pltpu.roll(a, k, 0), 0.0)
    return jnp.where(rows < n + k, pltpu.roll(a, n + k, 0), 0.0)


def _win_sum(a, w, rows, sign):
    s, k = a, 1
    while k < w:
        s = s + _shift_rows(s, sign * k, rows)
        k *= 2
    return s


def _pool_diff(ug, w, rows):
    inv = 1.0 / jnp.minimum(rows + 1, w).astype(F32)
    return _win_sum(ug, w, rows, 1) * inv - ug, inv


def _pool_call(u, w_pool, pool_scale, S, rider):
    T = u.shape[0]

    def body(u_ref, w_ref, ps_ref, y_ref):
        rows = lax.broadcasted_iota(jnp.int32, (S, POOL_GC), 0)
        for gi, w in enumerate(POOL_WINDOWS):
            sl = slice(POOL_GC * gi, POOL_GC * (gi + 1))
            diff, _ = _pool_diff(u_ref[:, sl], w, rows)
            mixed = _dot(diff.astype(MXU_DTYPE), w_ref[gi], NN)
            y_ref[:, sl] = (mixed * ps_ref[:, sl]).astype(MXU_DTYPE)

    seq = pl.BlockSpec((S, POOL_WIDTH), lambda b: (b, 0))
    return _launch(
        body, [u, w_pool, pool_scale], name="pool_fwd", grid=(T // S,),
        in_specs=[seq, pl.BlockSpec(w_pool.shape, lambda b: (0, 0, 0)), pl.BlockSpec(pool_scale.shape, lambda b: (0, 0))],
        out_specs=[seq], out_shape=[jax.ShapeDtypeStruct((T, POOL_WIDTH), MXU_DTYPE)], sem=("arbitrary",), rider=rider)


def _pool_bwd_call(u, dyp, w_pool, pool_scale, S, rider=None):
    T = u.shape[0]

    def body(u_ref, dy_ref, w_ref, ps_ref, du_ref, dw_ref, dps_ref):
        @pl.when(pl.program_id(0) == 0)
        def _():
            dw_ref[...] = jnp.zeros_like(dw_ref)
            dps_ref[...] = jnp.zeros_like(dps_ref)

        rows = lax.broadcasted_iota(jnp.int32, (S, POOL_GC), 0)
        for gi, w in enumerate(POOL_WINDOWS):
            sl = slice(POOL_GC * gi, POOL_GC * (gi + 1))
            diff, inv = _pool_diff(u_ref[:, sl], w, rows)
            diffb = diff.astype(MXU_DTYPE)
            wg = w_ref[gi]
            mixed = _dot(diffb, wg, NN)
            dy = dy_ref[:, sl]
            dps_ref[:, sl] += jnp.sum(dy * mixed, axis=0, keepdims=True)
            dmb = (dy * ps_ref[:, sl]).astype(MXU_DTYPE)
            dw_ref[gi] += _dot(diffb, dmb, TN)
            ddiff = _dot(dmb, wg, NT)
            du_ref[:, sl] = (_win_sum(ddiff * inv, w, rows, -1) - ddiff).astype(MXU_DTYPE)

    seq = pl.BlockSpec((S, POOL_WIDTH), lambda b: (b, 0))
    return _launch(
        body, [u, dyp, w_pool, pool_scale], name="pool_bwd", grid=(T // S,),
        in_specs=[seq, seq, pl.BlockSpec(w_pool.shape, lambda b: (0, 0, 0)), pl.BlockSpec(pool_scale.shape, lambda b: (0, 0))],
        out_specs=[seq, pl.BlockSpec(w_pool.shape, lambda b: (0, 0, 0)), pl.BlockSpec(pool_scale.shape, lambda b: (0, 0))],
        out_shape=[jax.ShapeDtypeStruct((T, POOL_WIDTH), MXU_DTYPE), jax.ShapeDtypeStruct(w_pool.shape, F32),
                   jax.ShapeDtypeStruct(pool_scale.shape, F32)],
        sem=("arbitrary",), rider=rider)


def _attn_consts():
    lane_g = lax.broadcasted_iota(jnp.int32, (BLOCK, 256), 1) >> 6
    rgrp = lax.broadcasted_iota(jnp.int32, (GROUP * BLOCK, 1), 0) >> 7
    rel = lax.broadcasted_iota(jnp.int32, (BLOCK, 256), 0) - lax.broadcasted_iota(jnp.int32, (BLOCK, 256), 1)

    def bias(off):
        ok = (rel + off >= 0) & (rel + off < BLOCK)
        return jnp.concatenate([jnp.where(ok, 0.0, NEG_INF)] * GROUP, axis=0)

    return lane_g, rgrp, bias(0), bias(BLOCK)


def _sink_rows(sink_ref, hk, rgrp):
    sv = jnp.zeros(rgrp.shape, F32)
    for g in range(GROUP):
        sv = jnp.where(rgrp == g, sink_ref[0, GROUP * hk + g], sv)
    return sv


def _stack_heads(xb, lane_g):
    return jnp.concatenate([jnp.where(lane_g == g, xb, jnp.zeros_like(xb)) for g in range(GROUP)], axis=0)


def _unstack_heads(xs, lane_g):
    out = jnp.where(lane_g == 0, xs[0:BLOCK], 0.0)
    for g in range(1, GROUP):
        out = out + jnp.where(lane_g == g, xs[BLOCK * g:BLOCK * (g + 1)], 0.0)
    return out


def _attn_probs(qs, kb, bias, sv):
    s = _dot(qs, kb, NT) + bias
    m = jnp.maximum(jnp.max(s, axis=1, keepdims=True), sv)
    e = jnp.exp(s - m)
    es = jnp.exp(sv - m)
    inv_l = 1.0 / (jnp.sum(e, axis=1, keepdims=True) + es)
    return e * inv_l, es * inv_l


def _attn_blocks(nb, blk, carry, per=1):
    carry = blk(0, 0, True, carry)
    per = per if (nb - 1) % per == 0 else 1

    def step(i, c):
        for k in range(per):
            n = 1 + per * i + k
            c = blk(pl.multiple_of(n * BLOCK, BLOCK), pl.multiple_of((n - 1) * BLOCK, BLOCK), False, c)
        return c

    return lax.fori_loop(0, (nb - 1) // per, step, carry)


def _attn_call(sinks, q, k4, v4, S, rider=None):
    T = q.shape[0]
    nb = S // BLOCK

    def body(sink_ref, q_ref, k_ref, v_ref, o_ref):
        lane_g, rgrp, bias_first, bias_later = _attn_consts()
        svs = [_sink_rows(sink_ref, hk, rgrp) for hk in range(N_KV_HEADS)]

        def blk(q0, k0, first, carry):
            for hk in range(N_KV_HEADS):
                cs = slice(256 * hk, 256 * (hk + 1))
                qs = _stack_heads(q_ref[pl.ds(q0, BLOCK), cs], lane_g)
                p, _ = _attn_probs(qs, k_ref[pl.ds(k0, 2 * BLOCK), cs], bias_first if first else bias_later, svs[hk])
                o = _dot(p.astype(MXU_DTYPE), v_ref[pl.ds(k0, 2 * BLOCK), cs], NN)
                o_ref[pl.ds(q0, BLOCK), cs] = _unstack_heads(o, lane_g).astype(MXU_DTYPE)
            return carry

        _attn_blocks(nb, blk, 0, per=3)

    seq = pl.BlockSpec((S, ATTN_WIDTH), lambda b: (b, 0))
    return _launch(
        body, [sinks, q, k4, v4], name="attn_fwd", grid=(T // S,),
        in_specs=[pl.BlockSpec(memory_space=pltpu.SMEM), seq, seq, seq],
        out_specs=[seq], out_shape=[jax.ShapeDtypeStruct((T, ATTN_WIDTH), MXU_DTYPE)],
        sem=("arbitrary",), rider=rider)


def _attn_bwd_call(sinks, q, k4, v4, do, rc, rsa, rsb, S, rider=None):
    T = q.shape[0]
    nb = S // BLOCK

    def body(sink_ref, q_ref, k_ref, v_ref, do_ref, c_ref, sa_ref, sb_ref,
             dq_ref, dk_ref, dv_ref, ds_ref, dk_acc, dv_acc):
        lane_g, rgrp, bias_first, bias_later = _attn_consts()
        svs = [_sink_rows(sink_ref, hk, rgrp) for hk in range(N_KV_HEADS)]
        lane1 = lax.broadcasted_iota(jnp.int32, (1, LANES), 1)
        dk_acc[...] = jnp.zeros_like(dk_acc)
        dv_acc[...] = jnp.zeros_like(dv_acc)

        def blk(q0, k0, first, dsink):
            rows = pl.ds(q0, BLOCK)
            c, sa, sb = c_ref[rows, :], sa_ref[rows, :], sb_ref[rows, :]
            for hk in range(N_KV_HEADS):
                cs = slice(256 * hk, 256 * (hk + 1))
                qs = _stack_heads(q_ref[rows, cs], lane_g)
                dos = _stack_heads(do_ref[rows, cs], lane_g)
                kb = k_ref[pl.ds(k0, 2 * BLOCK), cs]
                vb = v_ref[pl.ds(k0, 2 * BLOCK), cs]
                p, ps = _attn_probs(qs, kb, bias_first if first else bias_later, svs[hk])
                dp = _dot(dos, vb, NT)
                delta = jnp.sum(p * dp, axis=1, keepdims=True)
                dsb = (p * (dp - delta)).astype(MXU_DTYPE)
                dqb = _unstack_heads(_dot(dsb, kb, NN), lane_g) * SCALE
                dq_ref[rows, cs] = _rot(dqb, c, -sa, -sb).astype(MXU_DTYPE)
                dk_acc[pl.ds(k0, 2 * BLOCK), cs] += _dot(dsb, qs, TN)
                dv_acc[pl.ds(k0, 2 * BLOCK), cs] += _dot(p.astype(MXU_DTYPE), dos, TN)
                psd = ps * delta
                for g in range(GROUP):
                    val = -jnp.sum(psd[BLOCK * g:BLOCK * (g + 1)], axis=0, keepdims=True)
                    dsink = dsink + jnp.where(lane1 == GROUP * hk + g, val, 0.0)
            return dsink

        dsink = _attn_blocks(nb, blk, jnp.zeros((1, LANES), F32))
        dk_ref[...] = _rot(_fold_heads(dk_acc[...]), c_ref[...], -sa_ref[...], -sb_ref[...]).astype(MXU_DTYPE)
        dv_ref[...] = _fold_heads(dv_acc[...]).astype(MXU_DTYPE)
        ds_ref[...] = jnp.broadcast_to(dsink, ds_ref.shape)

    seq = pl.BlockSpec((S, ATTN_WIDTH), lambda b: (b, 0))
    kvs = pl.BlockSpec((S, KV_WIDTH), lambda b: (b, 0))
    tab = pl.BlockSpec((S, LANES), lambda b: (0, 0))
    nseq = T // S
    return _launch(
        body, [sinks, q, k4, v4, do, rc, rsa, rsb], name="attn_bwd", grid=(nseq,),
        in_specs=[pl.BlockSpec(memory_space=pltpu.SMEM), seq, seq, seq, seq, tab, tab, tab],
        out_specs=[seq, kvs, kvs, pl.BlockSpec((8, LANES), lambda b: (b, 0))],
        out_shape=[jax.ShapeDtypeStruct((T, ATTN_WIDTH), MXU_DTYPE), jax.ShapeDtypeStruct((T, KV_WIDTH), MXU_DTYPE),
                   jax.ShapeDtypeStruct((T, KV_WIDTH), MXU_DTYPE), jax.ShapeDtypeStruct((8 * nseq, LANES), F32)],
        scratch_shapes=[pltpu.VMEM((S, 512), F32), pltpu.VMEM((S, 512), F32)],
        sem=("arbitrary",), rider=rider)


def _branch_weights(wbp_ref, wba_ref, wbp_s, wba_s):
    @pl.when(pl.program_id(0) == 0)
    def _():
        for j in range(N_DEV):
            wbp_s[:, LANES * j:LANES * (j + 1)] = wbp_ref[j]
            wba_s[:, LANES * j:LANES * (j + 1)] = wba_ref[j]


def _mix_fwd_call(yp, ya, g, x, wbp, wba, wout, g2, g3, rider=None):
    T = x.shape[0]
    tm = _tile(T, 512)

    def body(yp_ref, ya_ref, g_ref, x_ref, wbp_ref, wba_ref, wout_ref, g2_ref, g3_ref,
             mix_ref, x1_ref, h2_ref, wbp_s, wba_s):
        _branch_weights(wbp_ref, wba_ref, wbp_s, wba_s)
        bp = _dot(yp_ref[...], wbp_s[...], NN)
        ba = _dot(ya_ref[...], wba_s[...], NN)
        merged = g_ref[:, :D_MODEL].astype(F32) * bp + g_ref[:, D_MODEL:].astype(F32) * ba
        mix = _dot(merged.astype(MXU_DTYPE), wout_ref[...], NN)
        mix_ref[...] = mix
        x1 = x_ref[...] + (mix * _rms_r(mix)) * g2_ref[...]
        x1_ref[...] = x1
        h2_ref[...] = ((x1 * _rms_r(x1)) * g3_ref[...]).astype(MXU_DTYPE)

    tok = lambda w: pl.BlockSpec((tm, w), lambda i: (i, 0))
    full = lambda a: pl.BlockSpec(a.shape, lambda i: (0,) * a.ndim)
    return _launch(
        body, [yp, ya, g, x, wbp, wba, wout, g2, g3], name="mix_fwd", grid=(T // tm,),
        in_specs=[tok(POOL_WIDTH), tok(ATTN_WIDTH), tok(GATE_WIDTH), tok(D_MODEL), full(wbp), full(wba), full(wout),
                  full(g2), full(g3)],
        out_specs=[tok(D_MODEL), tok(D_MODEL), tok(D_MODEL)],
        out_shape=[jax.ShapeDtypeStruct((T, D_MODEL), F32), jax.ShapeDtypeStruct((T, D_MODEL), F32),
                   jax.ShapeDtypeStruct((T, D_MODEL), MXU_DTYPE)],
        scratch_shapes=[pltpu.VMEM((POOL_WIDTH, D_MODEL), MXU_DTYPE), pltpu.VMEM((ATTN_WIDTH, D_MODEL), MXU_DTYPE)],
        sem=("arbitrary",), rider=rider)


def _mix_bwd_call(dx1, mix, yp, ya, g, wbp, wba, wout, g2, rider=None):
    T = dx1.shape[0]
    tm = _tile(T, 512)

    def body(dx1_ref, mix_ref, yp_ref, ya_ref, g_ref, wbp_ref, wba_ref, wout_ref, g2_ref,
             dyp_ref, do_ref, dgates_ref, dg2_ref, dbg_ref, gout_ref, gbp_ref, gba_ref,
             wbp_s, wba_s, acc_out, acc_bp, acc_ba, sem):
        _branch_weights(wbp_ref, wba_ref, wbp_s, wba_s)
        step = pl.program_id(0)

        @pl.when(step == 0)
        def _():
            dg2_ref[...] = jnp.zeros_like(dg2_ref)
            dbg_ref[...] = jnp.zeros_like(dbg_ref)
            acc_out[...] = jnp.zeros_like(acc_out)
            acc_bp[...] = jnp.zeros_like(acc_bp)
            acc_ba[...] = jnp.zeros_like(acc_ba)

        mix = mix_ref[...]
        dmix, dg2 = _rms_bwd(dx1_ref[...], mix, _rms_r(mix), g2_ref[...])
        dg2_ref[...] += jnp.sum(dg2, axis=0, keepdims=True)
        dmixb = dmix.astype(MXU_DTYPE)
        dmerged = _dot(dmixb, wout_ref[...], NT)
        yp, ya = yp_ref[...], ya_ref[...]
        bp = _dot(yp, wbp_s[...], NN)
        ba = _dot(ya, wba_s[...], NN)
        gp, ga = g_ref[:, :D_MODEL].astype(F32), g_ref[:, D_MODEL:].astype(F32)
        acc_out[...] += _dot((gp * bp + ga * ba).astype(MXU_DTYPE), dmixb, TN)
        dgp = dmerged * bp * (gp * (1.0 - gp))
        dga = dmerged * ba * (ga * (1.0 - ga))
        dbg_ref[:, :D_MODEL] += jnp.sum(dgp, axis=0, keepdims=True)
        dbg_ref[:, D_MODEL:] += jnp.sum(dga, axis=0, keepdims=True)
        dgates_ref[:, :D_MODEL] = dgp.astype(MXU_DTYPE)
        dgates_ref[:, D_MODEL:] = dga.astype(MXU_DTYPE)
        dbp = (dmerged * gp).astype(MXU_DTYPE)
        dba = (dmerged * ga).astype(MXU_DTYPE)
        acc_bp[...] += _dot(yp, dbp, TN)
        acc_ba[...] += _dot(ya, dba, TN)
        dyp_ref[...] = _dot(dbp, wbp_s[...], NT)
        do_ref[...] = _dot(dba, wba_s[...], NT).astype(MXU_DTYPE)

        @pl.when(step == pl.num_programs(0) - 1)
        def _():
            copies = [pltpu.make_async_copy(acc_out, gout_ref, sem.at[0])]
            for j in range(N_DEV):
                cols = slice(LANES * j, LANES * (j + 1))
                copies.append(pltpu.make_async_copy(acc_bp.at[:, cols], gbp_ref.at[j], sem.at[1 + j]))
                copies.append(pltpu.make_async_copy(acc_ba.at[:, cols], gba_ref.at[j], sem.at[1 + N_DEV + j]))
            for cp in copies:
                cp.start()
            for cp in copies:
                cp.wait()

    tok = lambda w: pl.BlockSpec((tm, w), lambda i: (i, 0))
    full = lambda a: pl.BlockSpec(a.shape, lambda i: (0,) * a.ndim)
    acc = lambda w: pl.BlockSpec((1, w), lambda i: (0, 0))
    hbm = pl.BlockSpec(memory_space=pl.ANY)
    sd = jax.ShapeDtypeStruct
    return _launch(
        body, [dx1, mix, yp, ya, g, wbp, wba, wout, g2], name="mix_bwd", grid=(T // tm,),
        in_specs=[tok(D_MODEL), tok(D_MODEL), tok(POOL_WIDTH), tok(ATTN_WIDTH), tok(GATE_WIDTH), full(wbp), full(wba),
                  full(wout), full(g2)],
        out_specs=[tok(POOL_WIDTH), tok(ATTN_WIDTH), tok(GATE_WIDTH), acc(D_MODEL), acc(GATE_WIDTH), hbm, hbm, hbm],
        out_shape=[sd((T, POOL_WIDTH), F32), sd((T, ATTN_WIDTH), MXU_DTYPE), sd((T, GATE_WIDTH), MXU_DTYPE),
                   sd((1, D_MODEL), F32), sd((1, GATE_WIDTH), F32), sd((D_MODEL, D_MODEL), F32),
                   sd((N_DEV, POOL_WIDTH, LANES), F32), sd((N_DEV, ATTN_WIDTH, LANES), F32)],
        scratch_shapes=[pltpu.VMEM((POOL_WIDTH, D_MODEL), MXU_DTYPE), pltpu.VMEM((ATTN_WIDTH, D_MODEL), MXU_DTYPE),
                        pltpu.VMEM((D_MODEL, D_MODEL), F32), pltpu.VMEM((POOL_WIDTH, D_MODEL), F32),
                        pltpu.VMEM((ATTN_WIDTH, D_MODEL), F32), pltpu.SemaphoreType.DMA((1 + 2 * N_DEV,))],
        sem=("arbitrary",), rider=rider)


def _mlp_up_call(h2, wup):
    T = h2.shape[0]
    tm = _tile(T, 512)
    fc = D_FF // N_DEV

    def body(h2_ref, wup_ref, act_ref):
        h2 = h2_ref[...]
        for j in range(N_DEV):
            rl = jnp.maximum(_dot(h2, wup_ref[j], NN), 0.0)
            act_ref[:, fc * j:fc * (j + 1)] = (rl * rl).astype(MXU_DTYPE)

    sd = jax.ShapeDtypeStruct
    return pl.pallas_call(
        body, name="mlp_up", grid=(T // tm,),
        in_specs=[pl.BlockSpec((tm, D_MODEL), lambda i: (i, 0)),
                  pl.BlockSpec(wup.shape, lambda i: (0, 0, 0), pipeline_mode=pl.Buffered(1))],
        out_specs=pl.BlockSpec((tm, D_FF), lambda i: (i, 0)), out_shape=sd((T, D_FF), MXU_DTYPE),
        compiler_params=_params(("arbitrary",)),
    )(h2, wup)


def _mlp_call(x1, act, target, wup, wdown, g3, g4):
    T = x1.shape[0]
    tm = _tile(T, 256)
    fc = D_FF // N_DEV

    def body(x1_ref, act_ref, t_ref, wup_ref, wdown_ref, g3_ref, g4_ref,
             da_ref, dff_ref, dx1_ref, dg3_ref, dg4_ref, loss_ref):
        @pl.when(pl.program_id(0) == 0)
        def _():
            dg3_ref[...] = jnp.zeros_like(dg3_ref)
            dg4_ref[...] = jnp.zeros_like(dg4_ref)
            loss_ref[...] = jnp.zeros_like(loss_ref)

        ff = jnp.zeros((tm, D_MODEL), F32)
        for j in range(N_DEV):
            ff = ff + _dot(act_ref[:, fc * j:fc * (j + 1)], wdown_ref[j], NN)
        x1 = x1_ref[...]
        r4 = _rms_r(ff)
        err = x1 + (ff * r4) * g4_ref[...] - t_ref[...]
        loss_ref[...] += jnp.sum(err * err, axis=0, keepdims=True)
        dy = err * (1.0 / D_MODEL)
        dff, dg4 = _rms_bwd(dy, ff, r4, g4_ref[...])
        dg4_ref[...] += jnp.sum(dg4, axis=0, keepdims=True)
        dffb = dff.astype(MXU_DTYPE)
        dff_ref[...] = dffb
        dh2 = jnp.zeros((tm, D_MODEL), F32)
        for j in range(N_DEV):
            sl = slice(fc * j, fc * (j + 1))
            rl = jnp.sqrt(act_ref[:, sl].astype(F32))
            dab = (_dot(dffb, wdown_ref[j], NT) * (2.0 * rl)).astype(MXU_DTYPE)
            da_ref[:, sl] = dab
            dh2 = dh2 + _dot(dab, wup_ref[j], NT)
        dx1, dg3 = _rms_bwd(dh2, x1, _rms_r(x1), g3_ref[...])
        dg3_ref[...] += jnp.sum(dg3, axis=0, keepdims=True)
        dx1_ref[...] = dy + dx1

    tok = lambda w: pl.BlockSpec((tm, w), lambda i: (i, 0))
    full = lambda a: pl.BlockSpec(a.shape, lambda i: (0,) * a.ndim, pipeline_mode=pl.Buffered(1))
    vec = pl.BlockSpec((1, D_MODEL), lambda i: (0, 0))
    sd = jax.ShapeDtypeStruct
    return pl.pallas_call(
        body, name="mlp_down_bwd", grid=(T // tm,),
        in_specs=[tok(D_MODEL), tok(D_FF), tok(D_MODEL), full(wup), full(wdown), vec, vec],
        out_specs=[tok(D_FF), tok(D_MODEL), tok(D_MODEL), vec, vec, vec],
        out_shape=[sd((T, D_FF), MXU_DTYPE), sd((T, D_MODEL), MXU_DTYPE),
                   sd((T, D_MODEL), F32), sd((1, D_MODEL), F32), sd((1, D_MODEL), F32), sd((1, D_MODEL), F32)],
        compiler_params=_params(("arbitrary",)),
    )(x1, act, target, wup, wdown, g3, g4)


def _inproj_bwd_call(du, dq, dk, dv, dgates, dx1, x, win_t, g1, rider=None):
    T = x.shape[0]
    tm = _tile(T, 512)

    def body(du_ref, dq_ref, dk_ref, dv_ref, dgt_ref, dx1_ref, x_ref, w_ref, g1_ref, gx_ref, dg1_ref, db_ref):
        @pl.when(pl.program_id(0) == 0)
        def _():
            dg1_ref[...] = jnp.zeros_like(dg1_ref)
            db_ref[...] = jnp.zeros_like(db_ref)

        dh = jnp.zeros((tm, D_MODEL), F32)
        for ref, lo, hi in ((du_ref, 0, C_Q), (dq_ref, C_Q, C_K), (dk_ref, C_K, C_V), (dv_ref, C_V, C_G),
                            (dgt_ref, C_G, IN_WIDTH)):
            piece = ref[...]
            dh = dh + _dot(piece, w_ref[lo:hi, :], NN)
            if hi <= C_G:
                db_ref[:, lo:hi] += jnp.sum(piece.astype(F32), axis=0, keepdims=True)
        xv = x_ref[...]
        dx, dg1 = _rms_bwd(dh, xv, _rms_r(xv), g1_ref[...])
        dg1_ref[...] += jnp.sum(dg1, axis=0, keepdims=True)
        gx_ref[...] = dx1_ref[...] + dx

    tok = lambda w: pl.BlockSpec((tm, w), lambda i: (i, 0))
    full = lambda a: pl.BlockSpec(a.shape, lambda i: (0,) * a.ndim)
    sd = jax.ShapeDtypeStruct
    return _launch(
        body, [du, dq, dk, dv, dgates, dx1, x, win_t, g1], name="inproj_bwd", grid=(T // tm,),
        in_specs=[tok(POOL_WIDTH), tok(ATTN_WIDTH), tok(KV_WIDTH), tok(KV_WIDTH), tok(GATE_WIDTH), tok(D_MODEL),
                  tok(D_MODEL), full(win_t), full(g1)],
        out_specs=[tok(D_MODEL), pl.BlockSpec((1, D_MODEL), lambda i: (0, 0)), pl.BlockSpec((1, C_G), lambda i: (0, 0))],
        out_shape=[sd((T, D_MODEL), F32), sd((1, D_MODEL), F32), sd((1, C_G), F32)],
        sem=("arbitrary",), rider=rider)


WGRAD_TOKENS = 1024


def _wgrad_rows_call(a, b, name, rider=None):
    T, K = a.shape
    N = b.shape[1]
    tm = _tile(T, WGRAD_TOKENS)
    kb = min(K, 1024)
    per = kb // (K // N_DEV)

    def body(a_ref, b_ref, o_ref):
        @pl.when(pl.program_id(1) == 0)
        def _():
            o_ref[...] = jnp.zeros_like(o_ref)

        d = _dot(a_ref[...], b_ref[...], TN)
        rs = kb // per
        for j in range(per):
            o_ref[j] += d[rs * j:rs * (j + 1)]

    return _launch(
        body, [a, b], name=name, grid=(K // kb, T // tm),
        in_specs=[pl.BlockSpec((tm, kb), lambda i, t: (t, i)), pl.BlockSpec((tm, N), lambda i, t: (t, 0))],
        out_specs=[pl.BlockSpec((per, K // N_DEV, N), lambda i, t: (i, 0, 0))],
        out_shape=[jax.ShapeDtypeStruct((N_DEV, K // N_DEV, N), F32)],
        sem=("arbitrary", "arbitrary"), rider=rider)


def _wgrad_cols_call(a, b, name, rider=None):
    T, K = a.shape
    N = b.shape[1]
    tm = _tile(T, WGRAD_TOKENS)
    nb = min(N, 1024)
    per = nb // (N // N_DEV)

    def body(a_ref, b_ref, o_ref):
        @pl.when(pl.program_id(1) == 0)
        def _():
            o_ref[...] = jnp.zeros_like(o_ref)

        d = _dot(a_ref[...], b_ref[...], TN)
        cs = nb // per
        for j in range(per):
            o_ref[j] += d[:, cs * j:cs * (j + 1)]

    return _launch(
        body, [a, b], name=name, grid=(N // nb, T // tm),
        in_specs=[pl.BlockSpec((tm, K), lambda i, t: (t, 0)), pl.BlockSpec((tm, nb), lambda i, t: (t, i))],
        out_specs=[pl.BlockSpec((per, K, N // N_DEV), lambda i, t: (i, 0, 0))],
        out_shape=[jax.ShapeDtypeStruct((N_DEV, K, N // N_DEV), F32)],
        sem=("arbitrary", "arbitrary"), rider=rider)


def _wgrad_in_call(du, dq, dk, dv, dgates, h, rider=None):
    T = h.shape[0]
    tm = _tile(T, WGRAD_TOKENS)
    rows = IN_WIDTH // N_DEV

    def body(du_ref, dq_ref, dk_ref, dv_ref, dgt_ref, h_ref, o_ref, acc, sem):
        t = pl.program_id(0)

        @pl.when(t == 0)
        def _():
            acc[...] = jnp.zeros_like(acc)

        hv = h_ref[...]
        for ref, lo, hi in ((du_ref, 0, C_Q), (dq_ref, C_Q, C_K), (dk_ref, C_K, C_V), (dv_ref, C_V, C_G),
                            (dgt_ref, C_G, IN_WIDTH)):
            acc[lo:hi, :] += _dot(ref[...], hv, TN)

        @pl.when(t == pl.num_programs(0) - 1)
        def _():
            copies = [pltpu.make_async_copy(acc.at[pl.ds(rows * j, rows), :], o_ref.at[j], sem.at[j])
                      for j in range(N_DEV)]
            for cp in copies:
                cp.start()
            for cp in copies:
                cp.wait()

    tok = lambda w: pl.BlockSpec((tm, w), lambda t: (t, 0))
    return _launch(
        body, [du, dq, dk, dv, dgates, h], name="wgrad_in", grid=(T // tm,),
        in_specs=[tok(POOL_WIDTH), tok(ATTN_WIDTH), tok(KV_WIDTH), tok(KV_WIDTH), tok(GATE_WIDTH), tok(D_MODEL)],
        out_specs=[pl.BlockSpec(memory_space=pl.ANY)],
        out_shape=[jax.ShapeDtypeStruct((N_DEV, rows, D_MODEL), F32)],
        scratch_shapes=[pltpu.VMEM((IN_WIDTH, D_MODEL), F32), pltpu.SemaphoreType.DMA((N_DEV,))],
        sem=("arbitrary",), rider=rider)


def _coords():
    return lax.axis_index("x"), lax.axis_index("y"), lax.axis_index("c")


def _ag_route():
    x, y, c = _coords()
    return (x, y, c), (x, y, 1 - c), (x ^ (1 - c), y ^ c, c), (x ^ c, y ^ (1 - c), c), (1 - x, 1 - y, c)


def _rider_ag_first(shard, me):
    def plan(ins, outs, send, recv, loc, r0, l0):
        own, *peers = _ag_route()
        return [pltpu.make_async_remote_copy(
            src_ref=ins[0], dst_ref=outs[0].at[_slot(own)], send_sem=send.at[r0 + k], recv_sem=recv.at[r0 + k],
            device_id=peers[k], device_id_type=MESH) for k in range(3)], []

    return _Rider([shard], [jax.ShapeDtypeStruct((N_DEV,) + shard.shape, shard.dtype)], 3, 0, plan,
                  lands=[_gather_buffer(shard, me)])


def _rider_ag_onward(shard, stage):
    def plan(ins, outs, send, recv, loc, r0, l0):
        own, sibling, near1, near2, diag = _ag_route()
        moves = [(near1, near2), (near1, sibling), (near2, sibling)] if stage == 2 else [(diag, sibling)]
        copies = []
        for k, (block, to) in enumerate(moves):
            part = outs[0].at[_slot(block)]
            copies.append(pltpu.make_async_remote_copy(src_ref=part, dst_ref=part, send_sem=send.at[r0 + k],
                                                       recv_sem=recv.at[r0 + k], device_id=to, device_id_type=MESH))
        return copies, []

    return _Rider([shard], [jax.ShapeDtypeStruct((N_DEV,) + shard.shape, shard.dtype)], 3 if stage == 2 else 1, 0, plan)


def _slot(p):
    return 4 * p[0] + 2 * p[1] + p[2]


ALL = "all"
LOCAL = "local"


def _rows(ref, span):
    return ref if span == ALL else ref.at[pl.ds(span[0], span[1])]


def _rider_ag(items):
    ins, out_shape, aliases, where = [], [], {}, []
    n_remote = n_local = 0
    for t, (shard, buf, snd, fwd) in enumerate(items):
        i_shard = i_buf = None
        if snd is not None:
            i_shard = len(ins)
            ins.append(shard)
        if buf is not None:
            i_buf = len(ins)
            ins.append(buf)
            aliases[i_buf] = t
            out_shape.append(jax.ShapeDtypeStruct(buf.shape, buf.dtype))
        else:
            assert fwd is None and snd is not None
            out_shape.append(jax.ShapeDtypeStruct((N_DEV,) + shard.shape, shard.dtype))
        where.append((i_shard, i_buf, n_remote, n_local))
        n_remote += (4 if snd not in (None, LOCAL) else 0) + (3 if fwd is not None else 0)
        n_local += 1 if snd is not None else 0

    def plan(rins, routs, send, recv, loc, r0, l0):
        x, y, c = _coords()
        peers = [(x, y, 1 - c), (1 - x, y, c), (x, 1 - y, c), (1 - x, 1 - y, c)]
        remote, local = [], []
        for t, (shard, buf, snd, fwd) in enumerate(items):
            i_shard, i_buf, k, l = where[t]
            k, l = r0 + k, l0 + l
            if snd is not None:
                span = ALL if snd == LOCAL else snd
                src, dst = _rows(rins[i_shard], span), _rows(routs[t].at[_slot((x, y, c))], span)
                local.append(pltpu.make_async_copy(src, dst, loc.at[l]))
                for peer in (peers if snd != LOCAL else []):
                    remote.append(pltpu.make_async_remote_copy(
                        src_ref=src, dst_ref=dst, send_sem=send.at[k], recv_sem=recv.at[k],
                        device_id=peer, device_id_type=MESH))
                    k += 1
            if fwd is not None:
                for px, py, pc in peers[1:]:
                    s = _slot((px, py, pc))
                    remote.append(pltpu.make_async_remote_copy(
                        src_ref=_rows(rins[i_buf].at[s], fwd), dst_ref=_rows(routs[t].at[s], fwd),
                        send_sem=send.at[k], recv_sem=recv.at[k], device_id=peers[0], device_id_type=MESH))
                    k += 1
        return remote, local

    return _Rider(ins, out_shape, n_remote, n_local, plan, aliases)


def _gather_buffer(shard, me):
    return lax.dynamic_update_slice(lax.empty((N_DEV,) + shard.shape, shard.dtype), shard[None], (me, 0, 0))


def _rider_ag_remote(shards, me):
    n = len(shards)

    def plan(ins, outs, send, recv, loc, r0, l0):
        x, y, c = _coords()
        remote = []
        for t in range(n):
            dst = outs[t].at[_slot((x, y, c))]
            for k, peer in enumerate([(x, y, 1 - c), (1 - x, y, c), (x, 1 - y, c), (1 - x, 1 - y, c)]):
                remote.append(pltpu.make_async_remote_copy(
                    src_ref=ins[t], dst_ref=dst, send_sem=send.at[r0 + 4 * t + k], recv_sem=recv.at[r0 + 4 * t + k],
                    device_id=peer, device_id_type=MESH))
        return remote, []

    return _Rider(shards, [jax.ShapeDtypeStruct((N_DEV,) + s.shape, s.dtype) for s in shards], 4 * n, 0, plan,
                  lands=[_gather_buffer(s, me) for s in shards])


def _rider_rs_sibling(grads):
    n = len(grads)

    def plan(ins, outs, send, recv, loc, r0, l0):
        x, y, c = _coords()
        remote = []
        for t in range(n):
            for q in range(4):
                remote.append(pltpu.make_async_remote_copy(
                    src_ref=ins[t].at[q, 1 - c], dst_ref=outs[t].at[q], send_sem=send.at[r0 + 4 * t + q],
                    recv_sem=recv.at[r0 + 4 * t + q], device_id=(x, y, 1 - c), device_id_type=MESH))
        return remote, []

    return _Rider(grads, [jax.ShapeDtypeStruct((4,) + g.shape[2:], g.dtype) for g in grads], 4 * n, 0, plan)


def _rider_rs_chips(sums, rows=None, into=None):
    n = len(sums)
    rows = rows or [ALL] * n

    def plan(ins, outs, send, recv, loc, r0, l0):
        x, y, c = _coords()
        remote = []
        for t in range(n):
            for r, (px, py) in enumerate([(1 - x, y), (x, 1 - y), (1 - x, 1 - y)]):
                remote.append(pltpu.make_async_remote_copy(
                    src_ref=_rows(ins[t].at[2 * px + py], rows[t]), dst_ref=_rows(outs[t].at[r], rows[t]),
                    send_sem=send.at[r0 + 3 * t + r], recv_sem=recv.at[r0 + 3 * t + r],
                    device_id=(px, py, c), device_id_type=MESH))
        return remote, []

    out_shape = [jax.ShapeDtypeStruct((3,) + s.shape[1:], s.dtype) for s in sums]
    if into is None:
        return _Rider(sums, out_shape, 3 * n, 0, plan)
    return _Rider(list(sums) + list(into), out_shape, 3 * n, 0, plan, aliases={n + t: t for t in range(n)})


def _rider_gather_remote(parts):
    n = len(parts)

    def plan(ins, outs, send, recv, loc, r0, l0):
        x, y, c = _coords()
        me = _slot((x, y, c))
        remote = []
        for t in range(n):
            for k in range(1, N_DEV):
                peer = (x ^ ((k >> 2) & 1), y ^ ((k >> 1) & 1), c ^ (k & 1))
                remote.append(pltpu.make_async_remote_copy(
                    src_ref=ins[t], dst_ref=outs[t].at[me], send_sem=send.at[r0 + 7 * t + k - 1],
                    recv_sem=recv.at[r0 + 7 * t + k - 1], device_id=peer, device_id_type=MESH))
        return remote, []

    return _Rider(parts, [jax.ShapeDtypeStruct((N_DEV,) + p.shape, p.dtype) for p in parts], 7 * n, 0, plan)


def _chip_sum_call(idx, grads, recvd, out_dtypes, name):
    n = len(grads)

    def body(i_ref, *refs):
        for t in range(n):
            refs[2 * n + t][0] = (refs[t][0, 0] + refs[n + t][0]).astype(out_dtypes[t])

    def chip(k, s):
        return jnp.where(k >= s[0], k + 1, k)

    in_specs = [pl.BlockSpec((1, 1) + g.shape[2:], lambda k, s: (chip(k, s), s[1], 0, 0)) for g in grads]
    in_specs += [pl.BlockSpec((1,) + r.shape[1:], lambda k, s: (chip(k, s), 0, 0)) for r in recvd]
    return pl.pallas_call(
        body, name=name,
        grid_spec=pltpu.PrefetchScalarGridSpec(
            num_scalar_prefetch=1, grid=(3,), in_specs=in_specs,
            out_specs=[pl.BlockSpec((1,) + r.shape[1:], lambda k, s: (chip(k, s), 0, 0)) for r in recvd]),
        out_shape=[jax.ShapeDtypeStruct(r.shape, dt) for r, dt in zip(recvd, out_dtypes)],
        compiler_params=_params(("arbitrary",)),
    )(idx, *grads, *recvd)


def _final_sum_call(idx, grads, recvd1, recvd2):
    n = len(grads)
    nsteps = 2

    def body(i_ref, *refs):
        for t in range(n):
            g, r1, r2, o = refs[t], refs[n + t], refs[2 * n + t], refs[3 * n + t]
            s = g[0, 0] + r1[0]
            for r in range(3):
                s = s + r2[r].astype(F32)
            o[...] = s

    def rows(a):
        r = a.shape[-2]
        return r // nsteps if (r // nsteps) % 16 == 0 else r

    def step(a):
        return (lambda i: i) if rows(a) != a.shape[-2] else (lambda i: 0)

    in_specs = [pl.BlockSpec((1, 1, rows(g), g.shape[3]), lambda i, s, st=step(g): (s[0], s[1], st(i), 0)) for g in grads]
    in_specs += [pl.BlockSpec((1, rows(r), r.shape[2]), lambda i, s, st=step(r): (s[0], st(i), 0)) for r in recvd1]
    in_specs += [pl.BlockSpec((3, rows(r), r.shape[2]), lambda i, s, st=step(r): (0, st(i), 0)) for r in recvd2]
    return pl.pallas_call(
        body, name="rs_final_sum",
        grid_spec=pltpu.PrefetchScalarGridSpec(
            num_scalar_prefetch=1, grid=(nsteps,), in_specs=in_specs,
            out_specs=[pl.BlockSpec((rows(r), r.shape[2]), lambda i, s, st=step(r): (st(i), 0)) for r in recvd2]),
        out_shape=[jax.ShapeDtypeStruct(r.shape[1:], F32) for r in recvd2],
        compiler_params=_params(("arbitrary",)),
    )(idx, *grads, *recvd1, *recvd2)


def _sum8_call(parts):
    def body(p_ref, o_ref):
        s = p_ref[0]
        for j in range(1, N_DEV):
            s = s + p_ref[j]
        o_ref[...] = s

    return pl.pallas_call(body, name="sum_small_partials",
                          out_shape=jax.ShapeDtypeStruct(parts.shape[1:], parts.dtype))(parts)


def _adamw(w, g, m, v):
    m = ADAM_B1 * m + (1.0 - ADAM_B1) * g
    v = ADAM_B2 * v + (1.0 - ADAM_B2) * (g * g)
    m_hat = m / (1.0 - ADAM_B1 ** ADAM_STEP)
    v_hat = v / (1.0 - ADAM_B2 ** ADAM_STEP)
    delta = -ADAM_LR * (m_hat / (jnp.sqrt(v_hat) + ADAM_EPS) + ADAM_WD * w)
    return delta, m, v


def _adamw_call(ws, gs, ms, vs, nsteps, name):
    n = len(ws)

    def body(*refs):
        for t in range(n):
            w, g, m, v = (refs[k * n + t][...] for k in range(4))
            d, m2, v2 = _adamw(w, g, m, v)
            refs[4 * n + t][...] = d
            refs[5 * n + t][...] = m2
            refs[6 * n + t][...] = v2

    def spec(a):
        assert a.shape[0] % nsteps == 0 and (nsteps == 1 or (a.shape[0] // nsteps) % 8 == 0), a.shape
        return pl.BlockSpec((a.shape[0] // nsteps, a.shape[1]), lambda i: (i, 0))

    specs = [spec(a) for a in ws]
    outs = pl.pallas_call(
        body, name=name, grid=(nsteps,),
        in_specs=specs * 4, out_specs=specs * 3,
        out_shape=[jax.ShapeDtypeStruct(a.shape, F32) for a in ws] * 3,
        compiler_params=_params(("arbitrary",)),
    )(*ws, *gs, *ms, *vs)
    return outs[:n], outs[n:2 * n], outs[2 * n:]


def _adamw_rs_call(idx, after, gws, r1s, r2s, ws, ms, vs, nsteps, name):
    n = len(ws)

    def body(i_ref, after_ref, *refs):
        for t in range(n):
            gw, r1, r2, w, m, v = (refs[k * n + t] for k in range(6))
            g = gw[0, 0] + r1[0]
            for r in range(3):
                g = g + r2[r].astype(F32)
            d, m2, v2 = _adamw(w[...], g, m[...], v[...])
            refs[6 * n + t][...] = g
            refs[7 * n + t][...] = d
            refs[8 * n + t][...] = m2
            refs[9 * n + t][...] = v2

    def rb(a):
        r = a.shape[0] // nsteps
        assert a.shape[0] % nsteps == 0 and r % 16 == 0, a.shape
        return r

    in_specs = [pl.BlockSpec((1, 1, rb(w), w.shape[1]), lambda i, s: (s[0], s[1], i, 0)) for w in ws]
    in_specs += [pl.BlockSpec((1, rb(w), w.shape[1]), lambda i, s: (s[0], i, 0)) for w in ws]
    in_specs += [pl.BlockSpec((3, rb(w), w.shape[1]), lambda i, s: (0, i, 0)) for w in ws]
    plain = [pl.BlockSpec((rb(w), w.shape[1]), lambda i, s: (i, 0)) for w in ws]
    outs = pl.pallas_call(
        body, name=name,
        grid_spec=pltpu.PrefetchScalarGridSpec(
            num_scalar_prefetch=1, grid=(nsteps,),
            in_specs=[pl.BlockSpec(memory_space=pl.ANY)] + in_specs + plain * 3, out_specs=plain * 4),
        out_shape=[jax.ShapeDtypeStruct(w.shape, F32) for w in ws] * 4,
        compiler_params=_params(("arbitrary",)),
    )(idx, after, *gws, *r1s, *r2s, *ws, *ms, *vs)
    return outs[:n], outs[n:2 * n], outs[2 * n:3 * n], outs[3 * n:]


def _rows128(a, pad_rows):
    flat = a.reshape(-1).astype(F32)
    flat = jnp.pad(flat, (0, pad_rows * LANES - flat.shape[0]))
    return flat.reshape(pad_rows, LANES)


_SMALL_A = (("w_pool", 512), ("pool_scale", 8), ("attn_sinks", 8), ("g_mix_post", 8), ("g_mlp_pre", 8),
            ("g_mlp_post", 8), ("loss", 8), ("b_in_gates", 16))
_SMALL_A_ROWS = 640
_SMALL_B = (("g_mix_pre", 8), ("b_in_head", 16))


def _pack(parts, layout, total_rows):
    rows = [_rows128(parts[k], r) for k, r in layout]
    pad = total_rows - sum(r for _, r in layout)
    if pad:
        rows.append(jnp.zeros((pad, LANES), F32))
    return jnp.concatenate(rows, axis=0)


def _unpack(buf, layout, sizes):
    out, off = {}, 0
    for k, r in layout:
        out[k] = buf[off:off + r].reshape(-1)[:sizes[k]]
        off += r
    return out


def kernel(x, g_mix_pre, w_in, b_in, w_pool, pool_scale, attn_sinks, w_branch_pool, w_branch_attn, w_out, g_mix_post, g_mlp_pre, w_up, w_down, g_mlp_post, loss_target, m_g_mix_pre, m_w_in, m_b_in, m_w_pool, m_pool_scale, m_attn_sinks, m_w_branch_pool, m_w_branch_attn, m_w_out, m_g_mix_post, m_g_mlp_pre, m_w_up, m_w_down, m_g_mlp_post, v_g_mix_pre, v_w_in, v_b_in, v_w_pool, v_pool_scale, v_attn_sinks, v_w_branch_pool, v_w_branch_attn, v_w_out, v_g_mix_post, v_g_mlp_pre, v_w_up, v_w_down, v_g_mlp_post):
    B, S, _ = x.shape
    T = B * S
    xt = x.reshape(T, D_MODEL)
    tgt = loss_target.reshape(T, D_MODEL)
    cx, cy, cc = _coords()

    cidx = jnp.stack([2 * cx + cy, cc]).astype(jnp.int32)
    by_chip = lambda gr: gr.reshape((4, 2) + gr.shape[1:])
    bf = lambda w: w[0].astype(MXU_DTYPE)

    me = _slot((cx, cy, cc))
    win_l = w_in[0].T.astype(MXU_DTYPE)
    (c_win,), _ = _copies_start([_rider_ag_first(win_l, me)], "allgather_first")
    wpool_b = bf(w_pool)
    rc, rsa, rsb = _rot_tables(S)
    wbp_l, wba_l, wout_l, wup_l, wdown_l = bf(w_branch_pool), bf(w_branch_attn), bf(w_out), bf(w_up), bf(w_down)
    gathers = [_rider_ag_remote([wbp_l, wba_l, wout_l], me), _rider_ag_remote([wup_l], me), _rider_ag_remote([wdown_l], me)]
    c_win = _copies_pass([c_win], [_rider_ag_onward(win_l, 2)], [b for r in gathers for b in r.lands] + [rc, rsa, rsb],
                         "allgather_second")
    c_win = _copies_pass(c_win, [_rider_ag_onward(win_l, 3)], [wbp_l, wba_l, wout_l, wup_l, wdown_l], "allgather_third")
    (win_s,) = _copies_wait(c_win, wpool_b, "allgather_weights")
    win_t = win_s.reshape(IN_WIDTH, D_MODEL)

    (c_br, c_up, c_dn), tok = _copies_start(gathers, "allgather_start", after=win_s)
    (h, u, q, k4, v4, g), _ = _inproj_call(xt, g_mix_pre, win_t, b_in, rc, rsa, rsb, S, rider=_after(tok))
    wbp_1, wba_1, wout_1 = _copies_wait([c_br], h, "allgather_wait_branch")
    (yp,), (wbp_s, wba_s, wout_s) = _pool_call(
        u, wpool_b, pool_scale, S,
        rider=_rider_ag([(None, wbp_1, None, ALL), (None, wba_1, None, ALL), (None, wout_1, None, ALL)]))
    (ya,) = _attn_call(attn_sinks, q, k4, v4, S)
    wout_f = wout_s.reshape(D_MODEL, D_MODEL)
    (wup_1,) = _copies_wait([c_up], ya, "allgather_wait_up")
    (mix, x1, h2), (wup_s,) = _mix_fwd_call(
        yp, ya, g, xt, wbp_s, wba_s, wout_f, g_mix_post, g_mlp_pre, rider=_rider_ag([(None, wup_1, None, ALL)]))
    act = _mlp_up_call(h2, wup_s)
    (wdown_1,) = _copies_wait([c_dn], act, "allgather_wait_down")
    (wdown_s,) = _comm_call(_rider_ag([(None, wdown_1, None, ALL)]), "allgather_pass_down")

    da, dff, dx1, dg3, dg4, lossvec = _mlp_call(x1, act, tgt, wup_s, wdown_s, g_mlp_pre, g_mlp_post)
    gw_up = by_chip(_wgrad_cols_call(h2, da, "wgrad_up")[0])
    (gw_down,), (r1_up,) = _wgrad_rows_call(act, dff, "wgrad_down", rider=_rider_rs_sibling([gw_up]))
    gw_down = by_chip(gw_down)
    (s_up,) = _chip_sum_call(cidx, [gw_up], [r1_up], [MXU_DTYPE], "rs_chip_sum_up")
    (dyp, do, dgates, dg2, dbg, gw_out, gw_bp, gw_ba), _ = _mix_bwd_call(
        dx1, mix, yp, ya, g, wbp_s, wba_s, wout_f, g_mix_post, rider=_after(s_up))
    gw_out = by_chip(gw_out.reshape(N_DEV, D_MODEL // N_DEV, D_MODEL))
    gw_bp, gw_ba = by_chip(gw_bp), by_chip(gw_ba)
    (dq, dk, dv, dsink), (r1_down, r1_out, r1_bp, r1_ba, r2_up) = _attn_bwd_call(
        attn_sinks, q, k4, v4, do, rc, rsa, rsb, S,
        rider=_join(_rider_rs_sibling([gw_down, gw_out, gw_bp, gw_ba]), _rider_rs_chips([s_up])))
    s_down, *s_obb = _chip_sum_call(cidx, [gw_down, gw_out, gw_bp, gw_ba], [r1_down, r1_out, r1_bp, r1_ba],
                                    [MXU_DTYPE] * 4, "rs_chip_sum_branch")
    (c_obb,), tok = _copies_start([_rider_rs_chips([s_down] + s_obb)], "rs_chips_start_branch")
    (du, dwp, dps), _ = _pool_bwd_call(u, dyp, wpool_b, pool_scale, S, rider=_after(tok))
    (gw_in,) = _wgrad_in_call(du, dq, dk, dv, dgates, h)
    gw_in = by_chip(gw_in)

    small_a = {"w_pool": dwp, "pool_scale": dps,
               "attn_sinks": jnp.sum(dsink.reshape(B, 8, LANES)[:, 0, :N_Q_HEADS], axis=0), "g_mix_post": dg2,
               "g_mlp_pre": dg3, "g_mlp_post": dg4, "loss": lossvec, "b_in_gates": dbg}
    gw_sa = by_chip(_pack(small_a, _SMALL_A, _SMALL_A_ROWS).reshape(N_DEV, _SMALL_A_ROWS // N_DEV, LANES))
    r1_in, r1_sa = _comm_call(_rider_rs_sibling([gw_in, gw_sa]), "rs_sibling_in")
    s_in, s_sa = _chip_sum_call(cidx, [gw_in, gw_sa], [r1_in, r1_sa], [MXU_DTYPE, F32], "rs_chip_sum_in")
    (c_in,), tok = _copies_start([_rider_rs_chips([s_in, s_sa])], "rs_chips_start_in")
    (gx, dg1, dba_in), _ = _inproj_bwd_call(du, dq, dk, dv, dgates, dx1, xt, win_t, g_mix_pre, rider=_after(tok))
    r2_down, r2_out, r2_bp, r2_ba, r2_in, r2_sa = _copies_wait([c_obb, c_in], dg1, "rs_chips_wait")

    (g_sa,) = _final_sum_call(cidx, [gw_sa], [r1_sa], [r2_sa])
    part_b = _pack({"g_mix_pre": dg1, "b_in_head": dba_in}, _SMALL_B, sum(r for _, r in _SMALL_B))
    (c_small,), tok = _copies_start([_rider_gather_remote([g_sa, part_b])], "allgather_small_start")

    in_t = _adamw_rs_call(cidx, tok, [gw_in], [r1_in], [r2_in], [w_in[0].T], [m_w_in[0].T], [v_w_in[0].T], 2,
                          "adamw_w_in")
    rest = _adamw_rs_call(
        cidx, tok, [gw_bp, gw_ba, gw_out, gw_up, gw_down], [r1_bp, r1_ba, r1_out, r1_up, r1_down],
        [r2_bp, r2_ba, r2_out, r2_up, r2_down], [w_branch_pool[0], w_branch_attn[0], w_out[0], w_up[0], w_down[0]],
        [m_w_branch_pool[0], m_w_branch_attn[0], m_w_out[0], m_w_up[0], m_w_down[0]],
        [v_w_branch_pool[0], v_w_branch_attn[0], v_w_out[0], v_w_up[0], v_w_down[0]], N_DEV, "adamw_shards")
    big_g, big_d, big_m2, big_v2 = ([a[0].T] + list(b) for a, b in zip(in_t, rest))

    sa_all, sb_all = _copies_wait([c_small], rest[0][0], "allgather_small_wait")
    sa_all = lax.dynamic_update_slice(sa_all, g_sa[None], (me, 0, 0))
    sb_sum = _sum8_call(lax.dynamic_update_slice(sb_all, part_b[None], (me, 0, 0)))

    names = ["g_mix_pre", "b_in", "w_pool", "pool_scale", "attn_sinks", "g_mix_post", "g_mlp_pre", "g_mlp_post"]
    sm_w = dict(g_mix_pre=g_mix_pre, b_in=b_in, w_pool=w_pool, pool_scale=pool_scale, attn_sinks=attn_sinks,
                g_mix_post=g_mix_post, g_mlp_pre=g_mlp_pre, g_mlp_post=g_mlp_post)
    sm_m = dict(g_mix_pre=m_g_mix_pre, b_in=m_b_in, w_pool=m_w_pool, pool_scale=m_pool_scale, attn_sinks=m_attn_sinks,
                g_mix_post=m_g_mix_post, g_mlp_pre=m_g_mlp_pre, g_mlp_post=m_g_mlp_post)
    sm_v = dict(g_mix_pre=v_g_mix_pre, b_in=v_b_in, w_pool=v_w_pool, pool_scale=v_pool_scale, attn_sinks=v_attn_sinks,
                g_mix_post=v_g_mix_post, g_mlp_pre=v_g_mlp_pre, g_mlp_post=v_g_mlp_post)
    sizes = {k: sm_w[k].size for k in names}
    sizes.update(loss=D_MODEL, b_in_gates=GATE_WIDTH, b_in_head=C_G)
    sm_g = _unpack(sa_all.reshape(_SMALL_A_ROWS, LANES), _SMALL_A, sizes)
    sm_g.update(_unpack(sb_sum, _SMALL_B, sizes))
    sm_g["b_in"] = jnp.concatenate([sm_g["b_in_head"], sm_g["b_in_gates"]])
    loss = (0.5 / D_MODEL) * jnp.sum(sm_g["loss"])
    two_d = lambda a: a.reshape(-1, a.shape[-1])
    sd_, sm2_, sv2_ = _adamw_call([two_d(sm_w[k]) for k in names], [two_d(sm_g[k].reshape(sm_w[k].shape)) for k in names],
                                  [two_d(sm_m[k]) for k in names], [two_d(sm_v[k]) for k in names], 1, "adamw_small")
    like = lambda vals: {k: a.reshape(sm_w[k].shape) for k, a in zip(names, vals)}
    sm_d, sm_m2, sm_v2 = like(sd_), like(sm2_), like(sv2_)
    sm_gr = {k: sm_g[k].reshape(sm_w[k].shape) for k in names}

    order = ["g_mix_pre", "w_in", "b_in", "w_pool", "pool_scale", "attn_sinks", "w_branch_pool", "w_branch_attn",
             "w_out", "g_mix_post", "g_mlp_pre", "w_up", "w_down", "g_mlp_post"]
    big_names = ["w_in", "w_branch_pool", "w_branch_attn", "w_out", "w_up", "w_down"]
    lead = lambda a: a[None]
    tables = []
    for small_t, big_t in ((sm_gr, big_g), (sm_d, big_d), (sm_m2, big_m2), (sm_v2, big_v2)):
        bt = dict(zip(big_names, big_t))
        tables.append([lead(bt[k]) if k in bt else small_t[k] for k in order])
    return (loss, gx.reshape(B, S, D_MODEL), *tables[0], *tables[1], *tables[2], *tables[3])
```

```python
import jax
import jax.numpy as jnp
from jax import lax
from jax.experimental import pallas as pl
from jax.experimental.pallas import tpu as pltpu

F32 = jnp.float32
MXU_DTYPE = jnp.bfloat16
MESH = pl.DeviceIdType.MESH

D_MODEL = 1024
POOL_WINDOWS = (2, 4, 8, 16)
POOL_WIDTH = 512
POOL_GC = 128
HEAD_DIM = 64
N_Q_HEADS = 8
N_KV_HEADS = 2
GROUP = 4
ATTN_WIDTH = 512
KV_WIDTH = 128
BLOCK = 128
GATE_WIDTH = 2048
IN_WIDTH = 3328
D_FF = 4096
EPS = 1e-6
NEG_INF = -1e30
ROPE_THETA = 500000.0
ROT_DIM = 16
SCALE = HEAD_DIM ** -0.5
C_Q, C_K, C_V, C_G = 512, 1024, 1152, 1280

ADAM_LR = 0.001
ADAM_B1 = 0.9
ADAM_B2 = 0.999
ADAM_EPS = 1e-08
ADAM_WD = 0.01
ADAM_STEP = 10

N_DEV = 8
LANES = 128
VMEM_LIMIT = 56 * 1024 * 1024

NN = (((1,), (0,)), ((), ()))
NT = (((1,), (1,)), ((), ()))
TN = (((0,), (0,)), ((), ()))


def _dot(a, b, dims):
    return lax.dot_general(a, b, dims, preferred_element_type=F32)


def _params(sem=None):
    return pltpu.CompilerParams(dimension_semantics=sem, vmem_limit_bytes=VMEM_LIMIT)


def _tile(n, pref):
    t = min(n, pref)
    assert n % t == 0, (n, t)
    return t


class _Rider:
    def __init__(self, ins, out_shape, n_remote, n_local, plan, aliases=None, lands=None):
        self.ins, self.out_shape, self.n_remote, self.n_local = list(ins), list(out_shape), n_remote, n_local
        self.plan, self.aliases = plan, dict(aliases or {})
        self.lands = lands


def _after(token, rider=None):
    r = rider or _Rider([], [], 0, 0, lambda ins, outs, send, recv, loc, r0, l0: ([], []))
    return _Rider(r.ins + [token], r.out_shape, r.n_remote, r.n_local, r.plan, r.aliases)


def _join(a, b):
    assert not a.aliases and not b.aliases
    n_in, n_out = len(a.ins), len(a.out_shape)

    def plan(ins, outs, send, recv, loc, r0, l0):
        ra, la = a.plan(ins[:n_in], outs[:n_out], send, recv, loc, r0, l0)
        rb, lb = b.plan(ins[n_in:], outs[n_out:], send, recv, loc, r0 + a.n_remote, l0 + a.n_local)
        return ra + rb, la + lb

    return _Rider(a.ins + b.ins, a.out_shape + b.out_shape, a.n_remote + b.n_remote, a.n_local + b.n_local, plan)


def _launch(body, args, *, name, grid, in_specs, out_specs, out_shape, scratch_shapes=(), sem=None, rider=None):
    if rider is None:
        return pl.pallas_call(body, name=name, grid=grid, in_specs=in_specs, out_specs=out_specs, out_shape=out_shape,
                              scratch_shapes=list(scratch_shapes), compiler_params=_params(sem))(*args)
    n_in, n_out, n_scr = len(args), len(out_shape), len(scratch_shapes)
    r_in, r_out = len(rider.ins), len(rider.out_shape)
    copies = rider.n_remote + rider.n_local > 0

    def wrapped(*refs):
        ins, rins = refs[:n_in], refs[n_in:n_in + r_in]
        o0 = n_in + r_in
        outs, routs = refs[o0:o0 + n_out], refs[o0 + n_out:o0 + n_out + r_out]
        s0 = o0 + n_out + r_out
        scr = refs[s0:s0 + n_scr]
        if not copies:
            return body(*ins, *outs, *scr)
        send, recv, loc = refs[s0 + n_scr:]
        first, last = None, None
        for d in range(len(grid)):
            f, l = pl.program_id(d) == 0, pl.program_id(d) == pl.num_programs(d) - 1
            first = f if first is None else first & f
            last = l if last is None else last & l

        def start():
            remote, local = rider.plan(rins, routs, send, recv, loc, 0, 0)
            for cp in local + remote:
                cp.start()

        def finish():
            remote, local = rider.plan(rins, routs, send, recv, loc, 0, 0)
            for cp in remote + local:
                cp.wait()

        if first is None:
            start()
            body(*ins, *outs, *scr)
            finish()
        else:
            pl.when(first)(start)
            body(*ins, *outs, *scr)
            pl.when(last)(finish)

    hbm = pl.BlockSpec(memory_space=pl.ANY)
    dma = pltpu.SemaphoreType.DMA
    res = pl.pallas_call(
        wrapped, name=name, grid=grid, in_specs=list(in_specs) + [hbm] * r_in,
        out_specs=list(out_specs) + [hbm] * r_out, out_shape=list(out_shape) + rider.out_shape,
        scratch_shapes=list(scratch_shapes) + (
            [dma((max(rider.n_remote, 1),)), dma((max(rider.n_remote, 1),)), dma((max(rider.n_local, 1),))] if copies else []),
        input_output_aliases={n_in + i: n_out + o for i, o in rider.aliases.items()},
        compiler_params=_params(sem),
    )(*args, *rider.ins)
    return list(res[:n_out]), list(res[n_out:])


def _comm_call(rider, name):
    return _launch(lambda: None, [], name=name, grid=(), in_specs=[], out_specs=[], out_shape=[], rider=rider)[1]


_HBM = pl.BlockSpec(memory_space=pltpu.HBM)
_SEM = pl.BlockSpec(memory_space=pltpu.SEMAPHORE)
_EFFECT = pltpu.SideEffectType.DATAFLOW_SIDE_EFFECTING


def _copies_start(riders, name, after=None):
    assert all(r.n_local == 0 and not r.aliases for r in riders)
    extra = [] if after is None else [after]
    sizes = [(len(r.ins), len(r.out_shape)) for r in riders]
    bufs = []
    for r in riders:
        lands = r.lands or [lax.empty(s.shape, s.dtype) for s in r.out_shape]
        bufs += [pltpu.with_memory_space_constraint(a, pltpu.HBM) for a in list(r.ins) + list(lands)]
    nb, ng, ne = len(bufs), len(riders), len(extra)

    def body(*refs):
        sems, token, at = refs[2 * nb + ne:2 * nb + ne + 2 * ng], refs[-1], 0
        for g, (r, (ni, no)) in enumerate(zip(riders, sizes)):
            remote, _ = r.plan(refs[at:at + ni], refs[at + ni:at + ni + no], sems[2 * g], sems[2 * g + 1], None, 0, 0)
            for cp in remote:
                cp.start()
            at += ni + no
        token[...] = jnp.zeros_like(token)

    res = pl.pallas_call(
        body, name=name, in_specs=[_HBM] * nb + [pl.BlockSpec(memory_space=pl.ANY)] * ne,
        out_specs=[_HBM] * nb + [_SEM] * (2 * ng) + [pl.BlockSpec(memory_space=pltpu.VMEM)],
        out_shape=[pltpu.HBM(a.shape, a.dtype) for a in bufs]
        + [pltpu.SemaphoreType.DMA((r.n_remote,)) for r in riders for _ in range(2)]
        + [jax.ShapeDtypeStruct((8, LANES), F32)],
        input_output_aliases={i: i for i in range(nb)},
        compiler_params=pltpu.CompilerParams(has_side_effects=_EFFECT),
    )(*bufs, *extra)
    handles, at = [], 0
    for g, (r, (ni, no)) in enumerate(zip(riders, sizes)):
        handles.append((r, list(res[at:at + ni + no]), res[nb + 2 * g], res[nb + 2 * g + 1]))
        at += ni + no
    return handles, res[-1]


def _copies_wait(handles, after, name):
    bufs = [b for _, bs, _, _ in handles for b in bs]
    sems = [s for _, _, send, recv in handles for s in (send, recv)]
    nb, ng = len(bufs), len(handles)
    after = list(after) if isinstance(after, (list, tuple)) else [after]

    def body(*refs):
        at = 0
        for g, (rider, bs, _, _) in enumerate(handles):
            ni = len(rider.ins)
            remote, _ = rider.plan(refs[at:at + ni], refs[at + ni:at + len(bs)], refs[nb + 2 * g], refs[nb + 2 * g + 1],
                                   None, 0, 0)
            for cp in remote:
                cp.wait_send()
                cp.wait_recv()
            at += len(bs)

    res = pl.pallas_call(
        body, name=name, in_specs=[_HBM] * nb + [_SEM] * (2 * ng) + [pl.BlockSpec(memory_space=pl.ANY)] * len(after),
        out_specs=[_HBM] * nb, out_shape=[pltpu.HBM(a.shape, a.dtype) for a in bufs],
        input_output_aliases={i: i for i in range(nb)},
        compiler_params=pltpu.CompilerParams(has_side_effects=_EFFECT),
    )(*bufs, *sems, *after)
    lands, at = [], 0
    for rider, bs, _, _ in handles:
        lands += list(res[at + len(rider.ins):at + len(bs)])
        at += len(bs)
    return lands


def _copies_pass(handles, riders, after, name):
    bufs = [b for _, bs, _, _ in handles for b in bs]
    sems = [s for _, _, send, recv in handles for s in (send, recv)]
    nb, ng = len(bufs), len(handles)
    after = list(after) if isinstance(after, (list, tuple)) else [after]

    def body(*refs):
        new_sems, at = refs[2 * nb + 2 * ng + len(after):], 0
        for g, ((rider, bs, _, _), then) in enumerate(zip(handles, riders)):
            ins, outs = refs[at:at + len(rider.ins)], refs[at + len(rider.ins):at + len(bs)]
            for cp in rider.plan(ins, outs, refs[nb + 2 * g], refs[nb + 2 * g + 1], None, 0, 0)[0]:
                cp.wait_send()
                cp.wait_recv()
            for cp in then.plan(ins, outs, new_sems[2 * g], new_sems[2 * g + 1], None, 0, 0)[0]:
                cp.start()
            at += len(bs)

    res = pl.pallas_call(
        body, name=name, in_specs=[_HBM] * nb + [_SEM] * (2 * ng) + [pl.BlockSpec(memory_space=pl.ANY)] * len(after),
        out_specs=[_HBM] * nb + [_SEM] * (2 * ng),
        out_shape=[pltpu.HBM(a.shape, a.dtype) for a in bufs]
        + [pltpu.SemaphoreType.DMA((r.n_remote,)) for r in riders for _ in range(2)],
        input_output_aliases={i: i for i in range(nb)},
        compiler_params=pltpu.CompilerParams(has_side_effects=_EFFECT),
    )(*bufs, *sems, *after)
    new, at = [], 0
    for g, ((_, bs, _, _), then) in enumerate(zip(handles, riders)):
        new.append((then, list(res[at:at + len(bs)]), res[nb + 2 * g], res[nb + 2 * g + 1]))
        at += len(bs)
    return new


def _rms_r(x):
    return lax.rsqrt(jnp.mean(x * x, axis=-1, keepdims=True) + EPS)


def _rms_bwd(dn, x, r, g):
    xh = x * r
    dxh = dn * g
    dx = r * (dxh - xh * jnp.mean(dxh * xh, axis=-1, keepdims=True))
    return dx, dn * xh


def _rot(t, c, sa, sb):
    outs = []
    for j in range(t.shape[1] // LANES):
        tj = t[:, LANES * j:LANES * (j + 1)]
        outs.append(tj * c + pltpu.roll(tj, LANES - 8, 1) * sa + pltpu.roll(tj, 8, 1) * sb)
    return outs[0] if len(outs) == 1 else jnp.concatenate(outs, axis=1)


def _rot_tables(S):
    pos = jnp.arange(S, dtype=F32)
    inv_freq = ROPE_THETA ** (-jnp.arange(0, ROT_DIM, 2, dtype=F32) / ROT_DIM)
    ang = pos[:, None] * inv_freq[None, :]
    cos, sin = jnp.cos(ang), jnp.sin(ang)
    one = jnp.ones((S, HEAD_DIM - ROT_DIM), F32)
    zero = jnp.zeros((S, HEAD_DIM - ROT_DIM), F32)
    z8 = jnp.zeros((S, 8), F32)
    c = jnp.concatenate([cos, cos, one], axis=1)
    sa = jnp.concatenate([-sin, z8, zero], axis=1)
    sb = jnp.concatenate([z8, sin, zero], axis=1)
    rep = LANES // HEAD_DIM
    return jnp.tile(c, (1, rep)), jnp.tile(sa, (1, rep)), jnp.tile(sb, (1, rep))


def _lane_tile4(k):
    lane = lax.broadcasted_iota(jnp.int32, k.shape, 1)
    rk = pltpu.roll(k, HEAD_DIM, 1)
    x0 = jnp.where(lane < HEAD_DIM, k, rk)
    x1 = jnp.where(lane < HEAD_DIM, rk, k)
    return jnp.concatenate([x0, x0, x1, x1], axis=1)


def _fold_heads(acc):
    zs = []
    for hk in range(N_KV_HEADS):
        a = acc[:, 256 * hk:256 * hk + LANES] + acc[:, 256 * hk + LANES:256 * (hk + 1)]
        zs.append(a + pltpu.roll(a, HEAD_DIM, 1))
    lane = lax.broadcasted_iota(jnp.int32, zs[0].shape, 1)
    return jnp.where(lane < HEAD_DIM, zs[0], zs[1])


def _inproj_call(x, g1, win_t, b_in, rc, rsa, rsb, S, rider=None):
    T = x.shape[0]
    tm = _tile(S, 512)
    nst = S // tm

    def body(x_ref, g1_ref, w_ref, b_ref, c_ref, sa_ref, sb_ref,
             h_ref, u_ref, q_ref, k4_ref, v4_ref, g_ref):
        xv = x_ref[...]
        hb = ((xv * _rms_r(xv)) * g1_ref[...]).astype(MXU_DTYPE)
        h_ref[...] = hb

        def proj(lo, hi):
            return _dot(hb, w_ref[lo:hi, :], NT) + b_ref[:, lo:hi]

        c, sa, sb = c_ref[...], sa_ref[...], sb_ref[...]
        u_ref[...] = proj(0, C_Q)
        q_ref[...] = (_rot(proj(C_Q, C_K), c, sa, sb) * SCALE).astype(MXU_DTYPE)
        kv = proj(C_K, C_G)
        k4_ref[...] = _lane_tile4(_rot(kv[:, :KV_WIDTH], c, sa, sb)).astype(MXU_DTYPE)
        v4_ref[...] = _lane_tile4(kv[:, KV_WIDTH:]).astype(MXU_DTYPE)
        g_ref[...] = jax.nn.sigmoid(proj(C_G, IN_WIDTH)).astype(MXU_DTYPE)

    tok = lambda w: pl.BlockSpec((tm, w), lambda i: (i, 0))
    full = lambda a: pl.BlockSpec(a.shape, lambda i: (0,) * a.ndim)
    tab = pl.BlockSpec((tm, LANES), lambda i: (i % nst, 0))
    return _launch(
        body, [x, g1, win_t, b_in, rc, rsa, rsb], name="inproj_fwd", grid=(T // tm,),
        in_specs=[tok(D_MODEL), full(g1), full(win_t), full(b_in), tab, tab, tab],
        out_specs=[tok(D_MODEL), tok(POOL_WIDTH), tok(ATTN_WIDTH), tok(512), tok(512), tok(GATE_WIDTH)],
        out_shape=[jax.ShapeDtypeStruct((T, D_MODEL), MXU_DTYPE), jax.ShapeDtypeStruct((T, POOL_WIDTH), F32),
                   jax.ShapeDtypeStruct((T, ATTN_WIDTH), MXU_DTYPE), jax.ShapeDtypeStruct((T, 512), MXU_DTYPE),
                   jax.ShapeDtypeStruct((T, 512), MXU_DTYPE), jax.ShapeDtypeStruct((T, GATE_WIDTH), MXU_DTYPE)],
        sem=("arbitrary",), rider=rider)


def _shift_rows(a, k, rows):
    n = a.shape[0]
    if k > 0:
        return jnp.where(rows >= k, pltpu.roll(a, k, 0), 0.0)
    return jnp.where(rows < n + k, pltpu.roll(a, n + k, 0), 0.0)


def _win_sum(a, w, rows, sign):
    s, k = a, 1
    while k < w:
        s = s + _shift_rows(s, sign * k, rows)
        k *= 2
    return s


def _pool_diff(ug, w, rows):
    inv = 1.0 / jnp.minimum(rows + 1, w).astype(F32)
    return _win_sum(ug, w, rows, 1) * inv - ug, inv


def _pool_call(u, w_pool, pool_scale, S, rider):
    T = u.shape[0]

    def body(u_ref, w_ref, ps_ref, y_ref):
        rows = lax.broadcasted_iota(jnp.int32, (S, POOL_GC), 0)
        for gi, w in enumerate(POOL_WINDOWS):
            sl = slice(POOL_GC * gi, POOL_GC * (gi + 1))
            diff, _ = _pool_diff(u_ref[:, sl], w, rows)
            mixed = _dot(diff.astype(MXU_DTYPE), w_ref[gi], NN)
            y_ref[:, sl] = (mixed * ps_ref[:, sl]).astype(MXU_DTYPE)

    seq = pl.BlockSpec((S, POOL_WIDTH), lambda b: (b, 0))
    return _launch(
        body, [u, w_pool, pool_scale], name="pool_fwd", grid=(T // S,),
        in_specs=[seq, pl.BlockSpec(w_pool.shape, lambda b: (0, 0, 0)), pl.BlockSpec(pool_scale.shape, lambda b: (0, 0))],
        out_specs=[seq], out_shape=[jax.ShapeDtypeStruct((T, POOL_WIDTH), MXU_DTYPE)], sem=("arbitrary",), rider=rider)


def _pool_bwd_call(u, dyp, w_pool, pool_scale, S, rider=None):
    T = u.shape[0]

    def body(u_ref, dy_ref, w_ref, ps_ref, du_ref, dw_ref, dps_ref):
        @pl.when(pl.program_id(0) == 0)
        def _():
            dw_ref[...] = jnp.zeros_like(dw_ref)
            dps_ref[...] = jnp.zeros_like(dps_ref)

        rows = lax.broadcasted_iota(jnp.int32, (S, POOL_GC), 0)
        for gi, w in enumerate(POOL_WINDOWS):
            sl = slice(POOL_GC * gi, POOL_GC * (gi + 1))
            diff, inv = _pool_diff(u_ref[:, sl], w, rows)
            diffb = diff.astype(MXU_DTYPE)
            wg = w_ref[gi]
            mixed = _dot(diffb, wg, NN)
            dy = dy_ref[:, sl]
            dps_ref[:, sl] += jnp.sum(dy * mixed, axis=0, keepdims=True)
            dmb = (dy * ps_ref[:, sl]).astype(MXU_DTYPE)
            dw_ref[gi] += _dot(diffb, dmb, TN)
            ddiff = _dot(dmb, wg, NT)
            du_ref[:, sl] = (_win_sum(ddiff * inv, w, rows, -1) - ddiff).astype(MXU_DTYPE)

    seq = pl.BlockSpec((S, POOL_WIDTH), lambda b: (b, 0))
    return _launch(
        body, [u, dyp, w_pool, pool_scale], name="pool_bwd", grid=(T // S,),
        in_specs=[seq, seq, pl.BlockSpec(w_pool.shape, lambda b: (0, 0, 0)), pl.BlockSpec(pool_scale.shape, lambda b: (0, 0))],
        out_specs=[seq, pl.BlockSpec(w_pool.shape, lambda b: (0, 0, 0)), pl.BlockSpec(pool_scale.shape, lambda b: (0, 0))],
        out_shape=[jax.ShapeDtypeStruct((T, POOL_WIDTH), MXU_DTYPE), jax.ShapeDtypeStruct(w_pool.shape, F32),
                   jax.ShapeDtypeStruct(pool_scale.shape, F32)],
        sem=("arbitrary",), rider=rider)


def _attn_consts():
    lane_g = lax.broadcasted_iota(jnp.int32, (BLOCK, 256), 1) >> 6
    rgrp = lax.broadcasted_iota(jnp.int32, (GROUP * BLOCK, 1), 0) >> 7
    rel = lax.broadcasted_iota(jnp.int32, (BLOCK, 256), 0) - lax.broadcasted_iota(jnp.int32, (BLOCK, 256), 1)

    def bias(off):
        ok = (rel + off >= 0) & (rel + off < BLOCK)
        return jnp.concatenate([jnp.where(ok, 0.0, NEG_INF)] * GROUP, axis=0)

    return lane_g, rgrp, bias(0), bias(BLOCK)


def _sink_rows(sink_ref, hk, rgrp):
    sv = jnp.zeros(rgrp.shape, F32)
    for g in range(GROUP):
        sv = jnp.where(rgrp == g, sink_ref[0, GROUP * hk + g], sv)
    return sv


def _stack_heads(xb, lane_g):
    return jnp.concatenate([jnp.where(lane_g == g, xb, jnp.zeros_like(xb)) for g in range(GROUP)], axis=0)


def _unstack_heads(xs, lane_g):
    out = jnp.where(lane_g == 0, xs[0:BLOCK], 0.0)
    for g in range(1, GROUP):
        out = out + jnp.where(lane_g == g, xs[BLOCK * g:BLOCK * (g + 1)], 0.0)
    return out


def _attn_probs(qs, kb, bias, sv):
    s = _dot(qs, kb, NT) + bias
    m = jnp.maximum(jnp.max(s, axis=1, keepdims=True), sv)
    e = jnp.exp(s - m)
    es = jnp.exp(sv - m)
    inv_l = 1.0 / (jnp.sum(e, axis=1, keepdims=True) + es)
    return e * inv_l, es * inv_l


def _attn_blocks(nb, blk, carry, per=1):
    carry = blk(0, 0, True, carry)
    per = per if (nb - 1) % per == 0 else 1

    def step(i, c):
        for k in range(per):
            n = 1 + per * i + k
            c = blk(pl.multiple_of(n * BLOCK, BLOCK), pl.multiple_of((n - 1) * BLOCK, BLOCK), False, c)
        return c

    return lax.fori_loop(0, (nb - 1) // per, step, carry)


def _attn_call(sinks, q, k4, v4, S, rider=None):
    T = q.shape[0]
    nb = S // BLOCK

    def body(sink_ref, q_ref, k_ref, v_ref, o_ref):
        lane_g, rgrp, bias_first, bias_later = _attn_consts()
        svs = [_sink_rows(sink_ref, hk, rgrp) for hk in range(N_KV_HEADS)]

        def blk(q0, k0, first, carry):
            for hk in range(N_KV_HEADS):
                cs = slice(256 * hk, 256 * (hk + 1))
                qs = _stack_heads(q_ref[pl.ds(q0, BLOCK), cs], lane_g)
                p, _ = _attn_probs(qs, k_ref[pl.ds(k0, 2 * BLOCK), cs], bias_first if first else bias_later, svs[hk])
                o = _dot(p.astype(MXU_DTYPE), v_ref[pl.ds(k0, 2 * BLOCK), cs], NN)
                o_ref[pl.ds(q0, BLOCK), cs] = _unstack_heads(o, lane_g).astype(MXU_DTYPE)
            return carry

        _attn_blocks(nb, blk, 0, per=3)

    seq = pl.BlockSpec((S, ATTN_WIDTH), lambda b: (b, 0))
    return _launch(
        body, [sinks, q, k4, v4], name="attn_fwd", grid=(T // S,),
        in_specs=[pl.BlockSpec(memory_space=pltpu.SMEM), seq, seq, seq],
        out_specs=[seq], out_shape=[jax.ShapeDtypeStruct((T, ATTN_WIDTH), MXU_DTYPE)],
        sem=("arbitrary",), rider=rider)


def _attn_bwd_call(sinks, q, k4, v4, do, rc, rsa, rsb, S, rider=None):
    T = q.shape[0]
    nb = S // BLOCK

    def body(sink_ref, q_ref, k_ref, v_ref, do_ref, c_ref, sa_ref, sb_ref,
             dq_ref, dk_ref, dv_ref, ds_ref, dk_acc, dv_acc):
        lane_g, rgrp, bias_first, bias_later = _attn_consts()
        svs = [_sink_rows(sink_ref, hk, rgrp) for hk in range(N_KV_HEADS)]
        lane1 = lax.broadcasted_iota(jnp.int32, (1, LANES), 1)
        dk_acc[...] = jnp.zeros_like(dk_acc)
        dv_acc[...] = jnp.zeros_like(dv_acc)

        def blk(q0, k0, first, dsink):
            rows = pl.ds(q0, BLOCK)
            c, sa, sb = c_ref[rows, :], sa_ref[rows, :], sb_ref[rows, :]
            for hk in range(N_KV_HEADS):
                cs = slice(256 * hk, 256 * (hk + 1))
                qs = _stack_heads(q_ref[rows, cs], lane_g)
                dos = _stack_heads(do_ref[rows, cs], lane_g)
                kb = k_ref[pl.ds(k0, 2 * BLOCK), cs]
                vb = v_ref[pl.ds(k0, 2 * BLOCK), cs]
                p, ps = _attn_probs(qs, kb, bias_first if first else bias_later, svs[hk])
                dp = _dot(dos, vb, NT)
                delta = jnp.sum(p * dp, axis=1, keepdims=True)
                dsb = (p * (dp - delta)).astype(MXU_DTYPE)
                dqb = _unstack_heads(_dot(dsb, kb, NN), lane_g) * SCALE
                dq_ref[rows, cs] = _rot(dqb, c, -sa, -sb).astype(MXU_DTYPE)
                dk_acc[pl.ds(k0, 2 * BLOCK), cs] += _dot(dsb, qs, TN)
                dv_acc[pl.ds(k0, 2 * BLOCK), cs] += _dot(p.astype(MXU_DTYPE), dos, TN)
                psd = ps * delta
                for g in range(GROUP):
                    val = -jnp.sum(psd[BLOCK * g:BLOCK * (g + 1)], axis=0, keepdims=True)
                    dsink = dsink + jnp.where(lane1 == GROUP * hk + g, val, 0.0)
            return dsink

        dsink = _attn_blocks(nb, blk, jnp.zeros((1, LANES), F32))
        dk_ref[...] = _rot(_fold_heads(dk_acc[...]), c_ref[...], -sa_ref[...], -sb_ref[...]).astype(MXU_DTYPE)
        dv_ref[...] = _fold_heads(dv_acc[...]).astype(MXU_DTYPE)
        ds_ref[...] = jnp.broadcast_to(dsink, ds_ref.shape)

    seq = pl.BlockSpec((S, ATTN_WIDTH), lambda b: (b, 0))
    kvs = pl.BlockSpec((S, KV_WIDTH), lambda b: (b, 0))
    tab = pl.BlockSpec((S, LANES), lambda b: (0, 0))
    nseq = T // S
    return _launch(
        body, [sinks, q, k4, v4, do, rc, rsa, rsb], name="attn_bwd", grid=(nseq,),
        in_specs=[pl.BlockSpec(memory_space=pltpu.SMEM), seq, seq, seq, seq, tab, tab, tab],
        out_specs=[seq, kvs, kvs, pl.BlockSpec((8, LANES), lambda b: (b, 0))],
        out_shape=[jax.ShapeDtypeStruct((T, ATTN_WIDTH), MXU_DTYPE), jax.ShapeDtypeStruct((T, KV_WIDTH), MXU_DTYPE),
                   jax.ShapeDtypeStruct((T, KV_WIDTH), MXU_DTYPE), jax.ShapeDtypeStruct((8 * nseq, LANES), F32)],
        scratch_shapes=[pltpu.VMEM((S, 512), F32), pltpu.VMEM((S, 512), F32)],
        sem=("arbitrary",), rider=rider)


def _branch_weights(wbp_ref, wba_ref, wbp_s, wba_s):
    @pl.when(pl.program_id(0) == 0)
    def _():
        for j in range(N_DEV):
            wbp_s[:, LANES * j:LANES * (j + 1)] = wbp_ref[j]
            wba_s[:, LANES * j:LANES * (j + 1)] = wba_ref[j]


def _mix_fwd_call(yp, ya, g, x, wbp, wba, wout, g2, g3, rider=None):
    T = x.shape[0]
    tm = _tile(T, 512)

    def body(yp_ref, ya_ref, g_ref, x_ref, wbp_ref, wba_ref, wout_ref, g2_ref, g3_ref,
             mix_ref, x1_ref, h2_ref, wbp_s, wba_s):
        _branch_weights(wbp_ref, wba_ref, wbp_s, wba_s)
        bp = _dot(yp_ref[...], wbp_s[...], NN)
        ba = _dot(ya_ref[...], wba_s[...], NN)
        merged = g_ref[:, :D_MODEL].astype(F32) * bp + g_ref[:, D_MODEL:].astype(F32) * ba
        mix = _dot(merged.astype(MXU_DTYPE), wout_ref[...], NN)
        mix_ref[...] = mix
        x1 = x_ref[...] + (mix * _rms_r(mix)) * g2_ref[...]
        x1_ref[...] = x1
        h2_ref[...] = ((x1 * _rms_r(x1)) * g3_ref[...]).astype(MXU_DTYPE)

    tok = lambda w: pl.BlockSpec((tm, w), lambda i: (i, 0))
    full = lambda a: pl.BlockSpec(a.shape, lambda i: (0,) * a.ndim)
    return _launch(
        body, [yp, ya, g, x, wbp, wba, wout, g2, g3], name="mix_fwd", grid=(T // tm,),
        in_specs=[tok(POOL_WIDTH), tok(ATTN_WIDTH), tok(GATE_WIDTH), tok(D_MODEL), full(wbp), full(wba), full(wout),
                  full(g2), full(g3)],
        out_specs=[tok(D_MODEL), tok(D_MODEL), tok(D_MODEL)],
        out_shape=[jax.ShapeDtypeStruct((T, D_MODEL), F32), jax.ShapeDtypeStruct((T, D_MODEL), F32),
                   jax.ShapeDtypeStruct((T, D_MODEL), MXU_DTYPE)],
        scratch_shapes=[pltpu.VMEM((POOL_WIDTH, D_MODEL), MXU_DTYPE), pltpu.VMEM((ATTN_WIDTH, D_MODEL), MXU_DTYPE)],
        sem=("arbitrary",), rider=rider)


def _mix_bwd_call(dx1, mix, yp, ya, g, wbp, wba, wout, g2, rider=None):
    T = dx1.shape[0]
    tm = _tile(T, 512)

    def body(dx1_ref, mix_ref, yp_ref, ya_ref, g_ref, wbp_ref, wba_ref, wout_ref, g2_ref,
             dyp_ref, do_ref, dgates_ref, dg2_ref, dbg_ref, gout_ref, gbp_ref, gba_ref,
             wbp_s, wba_s, acc_out, acc_bp, acc_ba, sem):
        _branch_weights(wbp_ref, wba_ref, wbp_s, wba_s)
        step = pl.program_id(0)

        @pl.when(step == 0)
        def _():
            dg2_ref[...] = jnp.zeros_like(dg2_ref)
            dbg_ref[...] = jnp.zeros_like(dbg_ref)
            acc_out[...] = jnp.zeros_like(acc_out)
            acc_bp[...] = jnp.zeros_like(acc_bp)
            acc_ba[...] = jnp.zeros_like(acc_ba)

        mix = mix_ref[...]
        dmix, dg2 = _rms_bwd(dx1_ref[...], mix, _rms_r(mix), g2_ref[...])
        dg2_ref[...] += jnp.sum(dg2, axis=0, keepdims=True)
        dmixb = dmix.astype(MXU_DTYPE)
        dmerged = _dot(dmixb, wout_ref[...], NT)
        yp, ya = yp_ref[...], ya_ref[...]
        bp = _dot(yp, wbp_s[...], NN)
        ba = _dot(ya, wba_s[...], NN)
        gp, ga = g_ref[:, :D_MODEL].astype(F32), g_ref[:, D_MODEL:].astype(F32)
        acc_out[...] += _dot((gp * bp + ga * ba).astype(MXU_DTYPE), dmixb, TN)
        dgp = dmerged * bp * (gp * (1.0 - gp))
        dga = dmerged * ba * (ga * (1.0 - ga))
        dbg_ref[:, :D_MODEL] += jnp.sum(dgp, axis=0, keepdims=True)
        dbg_ref[:, D_MODEL:] += jnp.sum(dga, axis=0, keepdims=True)
        dgates_ref[:, :D_MODEL] = dgp.astype(MXU_DTYPE)
        dgates_ref[:, D_MODEL:] = dga.astype(MXU_DTYPE)
        dbp = (dmerged * gp).astype(MXU_DTYPE)
        dba = (dmerged * ga).astype(MXU_DTYPE)
        acc_bp[...] += _dot(yp, dbp, TN)
        acc_ba[...] += _dot(ya, dba, TN)
        dyp_ref[...] = _dot(dbp, wbp_s[...], NT)
        do_ref[...] = _dot(dba, wba_s[...], NT).astype(MXU_DTYPE)

        @pl.when(step == pl.num_programs(0) - 1)
        def _():
            copies = [pltpu.make_async_copy(acc_out, gout_ref, sem.at[0])]
            for j in range(N_DEV):
                cols = slice(LANES * j, LANES * (j + 1))
                copies.append(pltpu.make_async_copy(acc_bp.at[:, cols], gbp_ref.at[j], sem.at[1 + j]))
                copies.append(pltpu.make_async_copy(acc_ba.at[:, cols], gba_ref.at[j], sem.at[1 + N_DEV + j]))
            for cp in copies:
                cp.start()
            for cp in copies:
                cp.wait()

    tok = lambda w: pl.BlockSpec((tm, w), lambda i: (i, 0))
    full = lambda a: pl.BlockSpec(a.shape, lambda i: (0,) * a.ndim)
    acc = lambda w: pl.BlockSpec((1, w), lambda i: (0, 0))
    hbm = pl.BlockSpec(memory_space=pl.ANY)
    sd = jax.ShapeDtypeStruct
    return _launch(
        body, [dx1, mix, yp, ya, g, wbp, wba, wout, g2], name="mix_bwd", grid=(T // tm,),
        in_specs=[tok(D_MODEL), tok(D_MODEL), tok(POOL_WIDTH), tok(ATTN_WIDTH), tok(GATE_WIDTH), full(wbp), full(wba),
                  full(wout), full(g2)],
        out_specs=[tok(POOL_WIDTH), tok(ATTN_WIDTH), tok(GATE_WIDTH), acc(D_MODEL), acc(GATE_WIDTH), hbm, hbm, hbm],
        out_shape=[sd((T, POOL_WIDTH), F32), sd((T, ATTN_WIDTH), MXU_DTYPE), sd((T, GATE_WIDTH), MXU_DTYPE),
                   sd((1, D_MODEL), F32), sd((1, GATE_WIDTH), F32), sd((D_MODEL, D_MODEL), F32),
                   sd((N_DEV, POOL_WIDTH, LANES), F32), sd((N_DEV, ATTN_WIDTH, LANES), F32)],
        scratch_shapes=[pltpu.VMEM((POOL_WIDTH, D_MODEL), MXU_DTYPE), pltpu.VMEM((ATTN_WIDTH, D_MODEL), MXU_DTYPE),
                        pltpu.VMEM((D_MODEL, D_MODEL), F32), pltpu.VMEM((POOL_WIDTH, D_MODEL), F32),
                        pltpu.VMEM((ATTN_WIDTH, D_MODEL), F32), pltpu.SemaphoreType.DMA((1 + 2 * N_DEV,))],
        sem=("arbitrary",), rider=rider)


def _mlp_up_call(h2, wup):
    T = h2.shape[0]
    tm = _tile(T, 512)
    fc = D_FF // N_DEV

    def body(h2_ref, wup_ref, act_ref):
        h2 = h2_ref[...]
        for j in range(N_DEV):
            rl = jnp.maximum(_dot(h2, wup_ref[j], NN), 0.0)
            act_ref[:, fc * j:fc * (j + 1)] = (rl * rl).astype(MXU_DTYPE)

    sd = jax.ShapeDtypeStruct
    return pl.pallas_call(
        body, name="mlp_up", grid=(T // tm,),
        in_specs=[pl.BlockSpec((tm, D_MODEL), lambda i: (i, 0)),
                  pl.BlockSpec(wup.shape, lambda i: (0, 0, 0), pipeline_mode=pl.Buffered(1))],
        out_specs=pl.BlockSpec((tm, D_FF), lambda i: (i, 0)), out_shape=sd((T, D_FF), MXU_DTYPE),
        compiler_params=_params(("arbitrary",)),
    )(h2, wup)


def _mlp_call(x1, act, target, wup, wdown, g3, g4):
    T = x1.shape[0]
    tm = _tile(T, 256)
    fc = D_FF // N_DEV

    def body(x1_ref, act_ref, t_ref, wup_ref, wdown_ref, g3_ref, g4_ref,
             da_ref, dff_ref, dx1_ref, dg3_ref, dg4_ref, loss_ref):
        @pl.when(pl.program_id(0) == 0)
        def _():
            dg3_ref[...] = jnp.zeros_like(dg3_ref)
            dg4_ref[...] = jnp.zeros_like(dg4_ref)
            loss_ref[...] = jnp.zeros_like(loss_ref)

        ff = jnp.zeros((tm, D_MODEL), F32)
        for j in range(N_DEV):
            ff = ff + _dot(act_ref[:, fc * j:fc * (j + 1)], wdown_ref[j], NN)
        x1 = x1_ref[...]
        r4 = _rms_r(ff)
        err = x1 + (ff * r4) * g4_ref[...] - t_ref[...]
        loss_ref[...] += jnp.sum(err * err, axis=0, keepdims=True)
        dy = err * (1.0 / D_MODEL)
        dff, dg4 = _rms_bwd(dy, ff, r4, g4_ref[...])
        dg4_ref[...] += jnp.sum(dg4, axis=0, keepdims=True)
        dffb = dff.astype(MXU_DTYPE)
        dff_ref[...] = dffb
        dh2 = jnp.zeros((tm, D_MODEL), F32)
        for j in range(N_DEV):
            sl = slice(fc * j, fc * (j + 1))
            rl = jnp.sqrt(act_ref[:, sl].astype(F32))
            dab = (_dot(dffb, wdown_ref[j], NT) * (2.0 * rl)).astype(MXU_DTYPE)
            da_ref[:, sl] = dab
            dh2 = dh2 + _dot(dab, wup_ref[j], NT)
        dx1, dg3 = _rms_bwd(dh2, x1, _rms_r(x1), g3_ref[...])
        dg3_ref[...] += jnp.sum(dg3, axis=0, keepdims=True)
        dx1_ref[...] = dy + dx1

    tok = lambda w: pl.BlockSpec((tm, w), lambda i: (i, 0))
    full = lambda a: pl.BlockSpec(a.shape, lambda i: (0,) * a.ndim, pipeline_mode=pl.Buffered(1))
    vec = pl.BlockSpec((1, D_MODEL), lambda i: (0, 0))
    sd = jax.ShapeDtypeStruct
    return pl.pallas_call(
        body, name="mlp_down_bwd", grid=(T // tm,),
        in_specs=[tok(D_MODEL), tok(D_FF), tok(D_MODEL), full(wup), full(wdown), vec, vec],
        out_specs=[tok(D_FF), tok(D_MODEL), tok(D_MODEL), vec, vec, vec],
        out_shape=[sd((T, D_FF), MXU_DTYPE), sd((T, D_MODEL), MXU_DTYPE),
                   sd((T, D_MODEL), F32), sd((1, D_MODEL), F32), sd((1, D_MODEL), F32), sd((1, D_MODEL), F32)],
        compiler_params=_params(("arbitrary",)),
    )(x1, act, target, wup, wdown, g3, g4)


def _inproj_bwd_call(du, dq, dk, dv, dgates, dx1, x, win_t, g1, rider=None):
    T = x.shape[0]
    tm = _tile(T, 512)

    def body(du_ref, dq_ref, dk_ref, dv_ref, dgt_ref, dx1_ref, x_ref, w_ref, g1_ref, gx_ref, dg1_ref, db_ref):
        @pl.when(pl.program_id(0) == 0)
        def _():
            dg1_ref[...] = jnp.zeros_like(dg1_ref)
            db_ref[...] = jnp.zeros_like(db_ref)

        dh = jnp.zeros((tm, D_MODEL), F32)
        for ref, lo, hi in ((du_ref, 0, C_Q), (dq_ref, C_Q, C_K), (dk_ref, C_K, C_V), (dv_ref, C_V, C_G),
                            (dgt_ref, C_G, IN_WIDTH)):
            piece = ref[...]
            dh = dh + _dot(piece, w_ref[lo:hi, :], NN)
            if hi <= C_G:
                db_ref[:, lo:hi] += jnp.sum(piece.astype(F32), axis=0, keepdims=True)
        xv = x_ref[...]
        dx, dg1 = _rms_bwd(dh, xv, _rms_r(xv), g1_ref[...])
        dg1_ref[...] += jnp.sum(dg1, axis=0, keepdims=True)
        gx_ref[...] = dx1_ref[...] + dx

    tok = lambda w: pl.BlockSpec((tm, w), lambda i: (i, 0))
    full = lambda a: pl.BlockSpec(a.shape, lambda i: (0,) * a.ndim)
    sd = jax.ShapeDtypeStruct
    return _launch(
        body, [du, dq, dk, dv, dgates, dx1, x, win_t, g1], name="inproj_bwd", grid=(T // tm,),
        in_specs=[tok(POOL_WIDTH), tok(ATTN_WIDTH), tok(KV_WIDTH), tok(KV_WIDTH), tok(GATE_WIDTH), tok(D_MODEL),
                  tok(D_MODEL), full(win_t), full(g1)],
        out_specs=[tok(D_MODEL), pl.BlockSpec((1, D_MODEL), lambda i: (0, 0)), pl.BlockSpec((1, C_G), lambda i: (0, 0))],
        out_shape=[sd((T, D_MODEL), F32), sd((1, D_MODEL), F32), sd((1, C_G), F32)],
        sem=("arbitrary",), rider=rider)


WGRAD_TOKENS = 1024


def _wgrad_rows_call(a, b, name, rider=None):
    T, K = a.shape
    N = b.shape[1]
    tm = _tile(T, WGRAD_TOKENS)
    kb = min(K, 1024)
    per = kb // (K // N_DEV)

    def body(a_ref, b_ref, o_ref):
        @pl.when(pl.program_id(1) == 0)
        def _():
            o_ref[...] = jnp.zeros_like(o_ref)

        d = _dot(a_ref[...], b_ref[...], TN)
        rs = kb // per
        for j in range(per):
            o_ref[j] += d[rs * j:rs * (j + 1)]

    return _launch(
        body, [a, b], name=name, grid=(K // kb, T // tm),
        in_specs=[pl.BlockSpec((tm, kb), lambda i, t: (t, i)), pl.BlockSpec((tm, N), lambda i, t: (t, 0))],
        out_specs=[pl.BlockSpec((per, K // N_DEV, N), lambda i, t: (i, 0, 0))],
        out_shape=[jax.ShapeDtypeStruct((N_DEV, K // N_DEV, N), F32)],
        sem=("arbitrary", "arbitrary"), rider=rider)


def _wgrad_cols_call(a, b, name, rider=None):
    T, K = a.shape
    N = b.shape[1]
    tm = _tile(T, WGRAD_TOKENS)
    nb = min(N, 1024)
    per = nb // (N // N_DEV)

    def body(a_ref, b_ref, o_ref):
        @pl.when(pl.program_id(1) == 0)
        def _():
            o_ref[...] = jnp.zeros_like(o_ref)

        d = _dot(a_ref[...], b_ref[...], TN)
        cs = nb // per
        for j in range(per):
            o_ref[j] += d[:, cs * j:cs * (j + 1)]

    return _launch(
        body, [a, b], name=name, grid=(N // nb, T // tm),
        in_specs=[pl.BlockSpec((tm, K), lambda i, t: (t, 0)), pl.BlockSpec((tm, nb), lambda i, t: (t, i))],
        out_specs=[pl.BlockSpec((per, K, N // N_DEV), lambda i, t: (i, 0, 0))],
        out_shape=[jax.ShapeDtypeStruct((N_DEV, K, N // N_DEV), F32)],
        sem=("arbitrary", "arbitrary"), rider=rider)


def _wgrad_in_call(du, dq, dk, dv, dgates, h, rider=None):
    T = h.shape[0]
    tm = _tile(T, WGRAD_TOKENS)
    rows = IN_WIDTH // N_DEV

    def body(du_ref, dq_ref, dk_ref, dv_ref, dgt_ref, h_ref, o_ref, acc, sem):
        t = pl.program_id(0)

        @pl.when(t == 0)
        def _():
            acc[...] = jnp.zeros_like(acc)

        hv = h_ref[...]
        for ref, lo, hi in ((du_ref, 0, C_Q), (dq_ref, C_Q, C_K), (dk_ref, C_K, C_V), (dv_ref, C_V, C_G),
                            (dgt_ref, C_G, IN_WIDTH)):
            acc[lo:hi, :] += _dot(ref[...], hv, TN)

        @pl.when(t == pl.num_programs(0) - 1)
        def _():
            copies = [pltpu.make_async_copy(acc.at[pl.ds(rows * j, rows), :], o_ref.at[j], sem.at[j])
                      for j in range(N_DEV)]
            for cp in copies:
                cp.start()
            for cp in copies:
                cp.wait()

    tok = lambda w: pl.BlockSpec((tm, w), lambda t: (t, 0))
    return _launch(
        body, [du, dq, dk, dv, dgates, h], name="wgrad_in", grid=(T // tm,),
        in_specs=[tok(POOL_WIDTH), tok(ATTN_WIDTH), tok(KV_WIDTH), tok(KV_WIDTH), tok(GATE_WIDTH), tok(D_MODEL)],
        out_specs=[pl.BlockSpec(memory_space=pl.ANY)],
        out_shape=[jax.ShapeDtypeStruct((N_DEV, rows, D_MODEL), F32)],
        scratch_shapes=[pltpu.VMEM((IN_WIDTH, D_MODEL), F32), pltpu.SemaphoreType.DMA((N_DEV,))],
        sem=("arbitrary",), rider=rider)


def _coords():
    return lax.axis_index("x"), lax.axis_index("y"), lax.axis_index("c")


def _ag_route():
    x, y, c = _coords()
    return (x, y, c), (x, y, 1 - c), (x ^ (1 - c), y ^ c, c), (x ^ c, y ^ (1 - c), c), (1 - x, 1 - y, c)


def _rider_ag_first(shard, me):
    def plan(ins, outs, send, recv, loc, r0, l0):
        own, *peers = _ag_route()
        return [pltpu.make_async_remote_copy(
            src_ref=ins[0], dst_ref=outs[0].at[_slot(own)], send_sem=send.at[r0 + k], recv_sem=recv.at[r0 + k],
            device_id=peers[k], device_id_type=MESH) for k in range(3)], []

    return _Rider([shard], [jax.ShapeDtypeStruct((N_DEV,) + shard.shape, shard.dtype)], 3, 0, plan,
                  lands=[_gather_buffer(shard, me)])


def _rider_ag_onward(shard, stage):
    def plan(ins, outs, send, recv, loc, r0, l0):
        own, sibling, near1, near2, diag = _ag_route()
        moves = [(near1, near2), (near1, sibling), (near2, sibling)] if stage == 2 else [(diag, sibling)]
        copies = []
        for k, (block, to) in enumerate(moves):
            part = outs[0].at[_slot(block)]
            copies.append(pltpu.make_async_remote_copy(src_ref=part, dst_ref=part, send_sem=send.at[r0 + k],
                                                       recv_sem=recv.at[r0 + k], device_id=to, device_id_type=MESH))
        return copies, []

    return _Rider([shard], [jax.ShapeDtypeStruct((N_DEV,) + shard.shape, shard.dtype)], 3 if stage == 2 else 1, 0, plan)


def _slot(p):
    return 4 * p[0] + 2 * p[1] + p[2]


ALL = "all"
LOCAL = "local"


def _rows(ref, span):
    return ref if span == ALL else ref.at[pl.ds(span[0], span[1])]


def _rider_ag(items):
    ins, out_shape, aliases, where = [], [], {}, []
    n_remote = n_local = 0
    for t, (shard, buf, snd, fwd) in enumerate(items):
        i_shard = i_buf = None
        if snd is not None:
            i_shard = len(ins)
            ins.append(shard)
        if buf is not None:
            i_buf = len(ins)
            ins.append(buf)
            aliases[i_buf] = t
            out_shape.append(jax.ShapeDtypeStruct(buf.shape, buf.dtype))
        else:
            assert fwd is None and snd is not None
            out_shape.append(jax.ShapeDtypeStruct((N_DEV,) + shard.shape, shard.dtype))
        where.append((i_shard, i_buf, n_remote, n_local))
        n_remote += (4 if snd not in (None, LOCAL) else 0) + (3 if fwd is not None else 0)
        n_local += 1 if snd is not None else 0

    def plan(rins, routs, send, recv, loc, r0, l0):
        x, y, c = _coords()
        peers = [(x, y, 1 - c), (1 - x, y, c), (x, 1 - y, c), (1 - x, 1 - y, c)]
        remote, local = [], []
        for t, (shard, buf, snd, fwd) in enumerate(items):
            i_shard, i_buf, k, l = where[t]
            k, l = r0 + k, l0 + l
            if snd is not None:
                span = ALL if snd == LOCAL else snd
                src, dst = _rows(rins[i_shard], span), _rows(routs[t].at[_slot((x, y, c))], span)
                local.append(pltpu.make_async_copy(src, dst, loc.at[l]))
                for peer in (peers if snd != LOCAL else []):
                    remote.append(pltpu.make_async_remote_copy(
                        src_ref=src, dst_ref=dst, send_sem=send.at[k], recv_sem=recv.at[k],
                        device_id=peer, device_id_type=MESH))
                    k += 1
            if fwd is not None:
                for px, py, pc in peers[1:]:
                    s = _slot((px, py, pc))
                    remote.append(pltpu.make_async_remote_copy(
                        src_ref=_rows(rins[i_buf].at[s], fwd), dst_ref=_rows(routs[t].at[s], fwd),
                        send_sem=send.at[k], recv_sem=recv.at[k], device_id=peers[0], device_id_type=MESH))
                    k += 1
        return remote, local

    return _Rider(ins, out_shape, n_remote, n_local, plan, aliases)


def _gather_buffer(shard, me):
    return lax.dynamic_update_slice(lax.empty((N_DEV,) + shard.shape, shard.dtype), shard[None], (me, 0, 0))


def _rider_ag_remote(shards, me):
    n = len(shards)

    def plan(ins, outs, send, recv, loc, r0, l0):
        x, y, c = _coords()
        remote = []
        for t in range(n):
            dst = outs[t].at[_slot((x, y, c))]
            for k, peer in enumerate([(x, y, 1 - c), (1 - x, y, c), (x, 1 - y, c), (1 - x, 1 - y, c)]):
                remote.append(pltpu.make_async_remote_copy(
                    src_ref=ins[t], dst_ref=dst, send_sem=send.at[r0 + 4 * t + k], recv_sem=recv.at[r0 + 4 * t + k],
                    device_id=peer, device_id_type=MESH))
        return remote, []

    return _Rider(shards, [jax.ShapeDtypeStruct((N_DEV,) + s.shape, s.dtype) for s in shards], 4 * n, 0, plan,
                  lands=[_gather_buffer(s, me) for s in shards])


def _rider_rs_sibling(grads):
    n = len(grads)

    def plan(ins, outs, send, recv, loc, r0, l0):
        x, y, c = _coords()
        remote = []
        for t in range(n):
            for q in range(4):
                remote.append(pltpu.make_async_remote_copy(
                    src_ref=ins[t].at[q, 1 - c], dst_ref=outs[t].at[q], send_sem=send.at[r0 + 4 * t + q],
                    recv_sem=recv.at[r0 + 4 * t + q], device_id=(x, y, 1 - c), device_id_type=MESH))
        return remote, []

    return _Rider(grads, [jax.ShapeDtypeStruct((4,) + g.shape[2:], g.dtype) for g in grads], 4 * n, 0, plan)


def _rider_rs_chips(sums, rows=None, into=None):
    n = len(sums)
    rows = rows or [ALL] * n

    def plan(ins, outs, send, recv, loc, r0, l0):
        x, y, c = _coords()
        remote = []
        for t in range(n):
            for r, (px, py) in enumerate([(1 - x, y), (x, 1 - y), (1 - x, 1 - y)]):
                remote.append(pltpu.make_async_remote_copy(
                    src_ref=_rows(ins[t].at[2 * px + py], rows[t]), dst_ref=_rows(outs[t].at[r], rows[t]),
                    send_sem=send.at[r0 + 3 * t + r], recv_sem=recv.at[r0 + 3 * t + r],
                    device_id=(px, py, c), device_id_type=MESH))
        return remote, []

    out_shape = [jax.ShapeDtypeStruct((3,) + s.shape[1:], s.dtype) for s in sums]
    if into is None:
        return _Rider(sums, out_shape, 3 * n, 0, plan)
    return _Rider(list(sums) + list(into), out_shape, 3 * n, 0, plan, aliases={n + t: t for t in range(n)})


def _rider_gather_remote(parts):
    n = len(parts)

    def plan(ins, outs, send, recv, loc, r0, l0):
        x, y, c = _coords()
        me = _slot((x, y, c))
        remote = []
        for t in range(n):
            for k in range(1, N_DEV):
                peer = (x ^ ((k >> 2) & 1), y ^ ((k >> 1) & 1), c ^ (k & 1))
                remote.append(pltpu.make_async_remote_copy(
                    src_ref=ins[t], dst_ref=outs[t].at[me], send_sem=send.at[r0 + 7 * t + k - 1],
                    recv_sem=recv.at[r0 + 7 * t + k - 1], device_id=peer, device_id_type=MESH))
        return remote, []

    return _Rider(parts, [jax.ShapeDtypeStruct((N_DEV,) + p.shape, p.dtype) for p in parts], 7 * n, 0, plan)


def _chip_sum_call(idx, grads, recvd, out_dtypes, name):
    n = len(grads)

    def body(i_ref, *refs):
        for t in range(n):
            refs[2 * n + t][0] = (refs[t][0, 0] + refs[n + t][0]).astype(out_dtypes[t])

    def chip(k, s):
        return jnp.where(k >= s[0], k + 1, k)

    in_specs = [pl.BlockSpec((1, 1) + g.shape[2:], lambda k, s: (chip(k, s), s[1], 0, 0)) for g in grads]
    in_specs += [pl.BlockSpec((1,) + r.shape[1:], lambda k, s: (chip(k, s), 0, 0)) for r in recvd]
    return pl.pallas_call(
        body, name=name,
        grid_spec=pltpu.PrefetchScalarGridSpec(
            num_scalar_prefetch=1, grid=(3,), in_specs=in_specs,
            out_specs=[pl.BlockSpec((1,) + r.shape[1:], lambda k, s: (chip(k, s), 0, 0)) for r in recvd]),
        out_shape=[jax.ShapeDtypeStruct(r.shape, dt) for r, dt in zip(recvd, out_dtypes)],
        compiler_params=_params(("arbitrary",)),
    )(idx, *grads, *recvd)


def _final_sum_call(idx, grads, recvd1, recvd2):
    n = len(grads)
    nsteps = 2

    def body(i_ref, *refs):
        for t in range(n):
            g, r1, r2, o = refs[t], refs[n + t], refs[2 * n + t], refs[3 * n + t]
            s = g[0, 0] + r1[0]
            for r in range(3):
                s = s + r2[r].astype(F32)
            o[...] = s

    def rows(a):
        r = a.shape[-2]
        return r // nsteps if (r // nsteps) % 16 == 0 else r

    def step(a):
        return (lambda i: i) if rows(a) != a.shape[-2] else (lambda i: 0)

    in_specs = [pl.BlockSpec((1, 1, rows(g), g.shape[3]), lambda i, s, st=step(g): (s[0], s[1], st(i), 0)) for g in grads]
    in_specs += [pl.BlockSpec((1, rows(r), r.shape[2]), lambda i, s, st=step(r): (s[0], st(i), 0)) for r in recvd1]
    in_specs += [pl.BlockSpec((3, rows(r), r.shape[2]), lambda i, s, st=step(r): (0, st(i), 0)) for r in recvd2]
    return pl.pallas_call(
        body, name="rs_final_sum",
        grid_spec=pltpu.PrefetchScalarGridSpec(
            num_scalar_prefetch=1, grid=(nsteps,), in_specs=in_specs,
            out_specs=[pl.BlockSpec((rows(r), r.shape[2]), lambda i, s, st=step(r): (st(i), 0)) for r in recvd2]),
        out_shape=[jax.ShapeDtypeStruct(r.shape[1:], F32) for r in recvd2],
        compiler_params=_params(("arbitrary",)),
    )(idx, *grads, *recvd1, *recvd2)


def _sum8_call(parts):
    def body(p_ref, o_ref):
        s = p_ref[0]
        for j in range(1, N_DEV):
            s = s + p_ref[j]
        o_ref[...] = s

    return pl.pallas_call(body, name="sum_small_partials",
                          out_shape=jax.ShapeDtypeStruct(parts.shape[1:], parts.dtype))(parts)


def _adamw(w, g, m, v):
    m = ADAM_B1 * m + (1.0 - ADAM_B1) * g
    v = ADAM_B2 * v + (1.0 - ADAM_B2) * (g * g)
    m_hat = m / (1.0 - ADAM_B1 ** ADAM_STEP)
    v_hat = v / (1.0 - ADAM_B2 ** ADAM_STEP)
    delta = -ADAM_LR * (m_hat / (jnp.sqrt(v_hat) + ADAM_EPS) + ADAM_WD * w)
    return delta, m, v


def _adamw_call(ws, gs, ms, vs, nsteps, name):
    n = len(ws)

    def body(*refs):
        for t in range(n):
            w, g, m, v = (refs[k * n + t][...] for k in range(4))
            d, m2, v2 = _adamw(w, g, m, v)
            refs[4 * n + t][...] = d
            refs[5 * n + t][...] = m2
            refs[6 * n + t][...] = v2

    def spec(a):
        assert a.shape[0] % nsteps == 0 and (nsteps == 1 or (a.shape[0] // nsteps) % 8 == 0), a.shape
        return pl.BlockSpec((a.shape[0] // nsteps, a.shape[1]), lambda i: (i, 0))

    specs = [spec(a) for a in ws]
    outs = pl.pallas_call(
        body, name=name, grid=(nsteps,),
        in_specs=specs * 4, out_specs=specs * 3,
        out_shape=[jax.ShapeDtypeStruct(a.shape, F32) for a in ws] * 3,
        compiler_params=_params(("arbitrary",)),
    )(*ws, *gs, *ms, *vs)
    return outs[:n], outs[n:2 * n], outs[2 * n:]


def _adamw_rs_call(idx, after, gws, r1s, r2s, ws, ms, vs, nsteps, name):
    n = len(ws)

    def body(i_ref, after_ref, *refs):
        for t in range(n):
            gw, r1, r2, w, m, v = (refs[k * n + t] for k in range(6))
            g = gw[0, 0] + r1[0]
            for r in range(3):
                g = g + r2[r].astype(F32)
            d, m2, v2 = _adamw(w[...], g, m[...], v[...])
            refs[6 * n + t][...] = g
            refs[7 * n + t][...] = d
            refs[8 * n + t][...] = m2
            refs[9 * n + t][...] = v2

    def rb(a):
        r = a.shape[0] // nsteps
        assert a.shape[0] % nsteps == 0 and r % 16 == 0, a.shape
        return r

    in_specs = [pl.BlockSpec((1, 1, rb(w), w.shape[1]), lambda i, s: (s[0], s[1], i, 0)) for w in ws]
    in_specs += [pl.BlockSpec((1, rb(w), w.shape[1]), lambda i, s: (s[0], i, 0)) for w in ws]
    in_specs += [pl.BlockSpec((3, rb(w), w.shape[1]), lambda i, s: (0, i, 0)) for w in ws]
    plain = [pl.BlockSpec((rb(w), w.shape[1]), lambda i, s: (i, 0)) for w in ws]
    outs = pl.pallas_call(
        body, name=name,
        grid_spec=pltpu.PrefetchScalarGridSpec(
            num_scalar_prefetch=1, grid=(nsteps,),
            in_specs=[pl.BlockSpec(memory_space=pl.ANY)] + in_specs + plain * 3, out_specs=plain * 4),
        out_shape=[jax.ShapeDtypeStruct(w.shape, F32) for w in ws] * 4,
        compiler_params=_params(("arbitrary",)),
    )(idx, after, *gws, *r1s, *r2s, *ws, *ms, *vs)
    return outs[:n], outs[n:2 * n], outs[2 * n:3 * n], outs[3 * n:]


def _rows128(a, pad_rows):
    flat = a.reshape(-1).astype(F32)
    flat = jnp.pad(flat, (0, pad_rows * LANES - flat.shape[0]))
    return flat.reshape(pad_rows, LANES)


_SMALL_A = (("w_pool", 512), ("pool_scale", 8), ("attn_sinks", 8), ("g_mix_post", 8), ("g_mlp_pre", 8),
            ("g_mlp_post", 8), ("loss", 8), ("b_in_gates", 16))
_SMALL_A_ROWS = 640
_SMALL_B = (("g_mix_pre", 8), ("b_in_head", 16))


def _pack(parts, layout, total_rows):
    rows = [_rows128(parts[k], r) for k, r in layout]
    pad = total_rows - sum(r for _, r in layout)
    if pad:
        rows.append(jnp.zeros((pad, LANES), F32))
    return jnp.concatenate(rows, axis=0)


def _unpack(buf, layout, sizes):
    out, off = {}, 0
    for k, r in layout:
        out[k] = buf[off:off + r].reshape(-1)[:sizes[k]]
        off += r
    return out


def kernel(x, g_mix_pre, w_in, b_in, w_pool, pool_scale, attn_sinks, w_branch_pool, w_branch_attn, w_out, g_mix_post, g_mlp_pre, w_up, w_down, g_mlp_post, loss_target, m_g_mix_pre, m_w_in, m_b_in, m_w_pool, m_pool_scale, m_attn_sinks, m_w_branch_pool, m_w_branch_attn, m_w_out, m_g_mix_post, m_g_mlp_pre, m_w_up, m_w_down, m_g_mlp_post, v_g_mix_pre, v_w_in, v_b_in, v_w_pool, v_pool_scale, v_attn_sinks, v_w_branch_pool, v_w_branch_attn, v_w_out, v_g_mix_post, v_g_mlp_pre, v_w_up, v_w_down, v_g_mlp_post):
    B, S, _ = x.shape
    T = B * S
    xt = x.reshape(T, D_MODEL)
    tgt = loss_target.reshape(T, D_MODEL)
    cx, cy, cc = _coords()

    cidx = jnp.stack([2 * cx + cy, cc]).astype(jnp.int32)
    by_chip = lambda gr: gr.reshape((4, 2) + gr.shape[1:])
    bf = lambda w: w[0].astype(MXU_DTYPE)

    me = _slot((cx, cy, cc))
    win_l = w_in[0].T.astype(MXU_DTYPE)
    (c_win,), _ = _copies_start([_rider_ag_first(win_l, me)], "allgather_first")
    wpool_b = bf(w_pool)
    rc, rsa, rsb = _rot_tables(S)
    wbp_l, wba_l, wout_l, wup_l, wdown_l = bf(w_branch_pool), bf(w_branch_attn), bf(w_out), bf(w_up), bf(w_down)
    gathers = [_rider_ag_remote([wbp_l, wba_l, wout_l], me), _rider_ag_remote([wup_l], me), _rider_ag_remote([wdown_l], me)]
    c_win = _copies_pass([c_win], [_rider_ag_onward(win_l, 2)], [b for r in gathers for b in r.lands] + [rc, rsa, rsb],
                         "allgather_second")
    c_win = _copies_pass(c_win, [_rider_ag_onward(win_l, 3)], [wbp_l, wba_l, wout_l, wup_l, wdown_l], "allgather_third")
    (win_s,) = _copies_wait(c_win, wpool_b, "allgather_weights")
    win_t = win_s.reshape(IN_WIDTH, D_MODEL)

    (c_br, c_up, c_dn), tok = _copies_start(gathers, "allgather_start", after=win_s)
    (h, u, q, k4, v4, g), _ = _inproj_call(xt, g_mix_pre, win_t, b_in, rc, rsa, rsb, S, rider=_after(tok))
    wbp_1, wba_1, wout_1 = _copies_wait([c_br], h, "allgather_wait_branch")
    (yp,), (wbp_s, wba_s, wout_s) = _pool_call(
        u, wpool_b, pool_scale, S,
        rider=_rider_ag([(None, wbp_1, None, ALL), (None, wba_1, None, ALL), (None, wout_1, None, ALL)]))
    (ya,) = _attn_call(attn_sinks, q, k4, v4, S)
    wout_f = wout_s.reshape(D_MODEL, D_MODEL)
    (wup_1,) = _copies_wait([c_up], ya, "allgather_wait_up")
    (mix, x1, h2), (wup_s,) = _mix_fwd_call(
        yp, ya, g, xt, wbp_s, wba_s, wout_f, g_mix_post, g_mlp_pre, rider=_rider_ag([(None, wup_1, None, ALL)]))
    act = _mlp_up_call(h2, wup_s)
    (wdown_1,) = _copies_wait([c_dn], act, "allgather_wait_down")
    (wdown_s,) = _comm_call(_rider_ag([(None, wdown_1, None, ALL)]), "allgather_pass_down")

    da, dff, dx1, dg3, dg4, lossvec = _mlp_call(x1, act, tgt, wup_s, wdown_s, g_mlp_pre, g_mlp_post)
    gw_up = by_chip(_wgrad_cols_call(h2, da, "wgrad_up")[0])
    (gw_down,), (r1_up,) = _wgrad_rows_call(act, dff, "wgrad_down", rider=_rider_rs_sibling([gw_up]))
    gw_down = by_chip(gw_down)
    (s_up,) = _chip_sum_call(cidx, [gw_up], [r1_up], [MXU_DTYPE], "rs_chip_sum_up")
    (dyp, do, dgates, dg2, dbg, gw_out, gw_bp, gw_ba), _ = _mix_bwd_call(
        dx1, mix, yp, ya, g, wbp_s, wba_s, wout_f, g_mix_post, rider=_after(s_up))
    gw_out = by_chip(gw_out.reshape(N_DEV, D_MODEL // N_DEV, D_MODEL))
    gw_bp, gw_ba = by_chip(gw_bp), by_chip(gw_ba)
    (dq, dk, dv, dsink), (r1_down, r1_out, r1_bp, r1_ba, r2_up) = _attn_bwd_call(
        attn_sinks, q, k4, v4, do, rc, rsa, rsb, S,
        rider=_join(_rider_rs_sibling([gw_down, gw_out, gw_bp, gw_ba]), _rider_rs_chips([s_up])))
    s_down, *s_obb = _chip_sum_call(cidx, [gw_down, gw_out, gw_bp, gw_ba], [r1_down, r1_out, r1_bp, r1_ba],
                                    [MXU_DTYPE] * 4, "rs_chip_sum_branch")
    (c_obb,), tok = _copies_start([_rider_rs_chips([s_down] + s_obb)], "rs_chips_start_branch")
    (du, dwp, dps), _ = _pool_bwd_call(u, dyp, wpool_b, pool_scale, S, rider=_after(tok))
    (gw_in,) = _wgrad_in_call(du, dq, dk, dv, dgates, h)
    gw_in = by_chip(gw_in)

    small_a = {"w_pool": dwp, "pool_scale": dps,
               "attn_sinks": jnp.sum(dsink.reshape(B, 8, LANES)[:, 0, :N_Q_HEADS], axis=0), "g_mix_post": dg2,
               "g_mlp_pre": dg3, "g_mlp_post": dg4, "loss": lossvec, "b_in_gates": dbg}
    gw_sa = by_chip(_pack(small_a, _SMALL_A, _SMALL_A_ROWS).reshape(N_DEV, _SMALL_A_ROWS // N_DEV, LANES))
    r1_in, r1_sa = _comm_call(_rider_rs_sibling([gw_in, gw_sa]), "rs_sibling_in")
    s_in, s_sa = _chip_sum_call(cidx, [gw_in, gw_sa], [r1_in, r1_sa], [MXU_DTYPE, F32], "rs_chip_sum_in")
    (gx, dg1, dba_in), (r2_in, r2_sa) = _inproj_bwd_call(
        du, dq, dk, dv, dgates, dx1, xt, win_t, g_mix_pre, rider=_rider_rs_chips([s_in, s_sa]))
    r2_down, r2_out, r2_bp, r2_ba = _copies_wait([c_obb], dg1, "rs_chips_wait")

    (g_sa,) = _final_sum_call(cidx, [gw_sa], [r1_sa], [r2_sa])
    part_b = _pack({"g_mix_pre": dg1, "b_in_head": dba_in}, _SMALL_B, sum(r for _, r in _SMALL_B))
    (c_small,), tok = _copies_start([_rider_gather_remote([g_sa, part_b])], "allgather_small_start")

    in_t = _adamw_rs_call(cidx, tok, [gw_in], [r1_in], [r2_in], [w_in[0].T], [m_w_in[0].T], [v_w_in[0].T], 2,
                          "adamw_w_in")
    rest = _adamw_rs_call(
        cidx, tok, [gw_bp, gw_ba, gw_out, gw_up, gw_down], [r1_bp, r1_ba, r1_out, r1_up, r1_down],
        [r2_bp, r2_ba, r2_out, r2_up, r2_down], [w_branch_pool[0], w_branch_attn[0], w_out[0], w_up[0], w_down[0]],
        [m_w_branch_pool[0], m_w_branch_attn[0], m_w_out[0], m_w_up[0], m_w_down[0]],
        [v_w_branch_pool[0], v_w_branch_attn[0], v_w_out[0], v_w_up[0], v_w_down[0]], N_DEV, "adamw_shards")
    big_g, big_d, big_m2, big_v2 = ([a[0].T] + list(b) for a, b in zip(in_t, rest))

    sa_all, sb_all = _copies_wait([c_small], rest[0][0], "allgather_small_wait")
    sa_all = lax.dynamic_update_slice(sa_all, g_sa[None], (me, 0, 0))
    sb_sum = _sum8_call(lax.dynamic_update_slice(sb_all, part_b[None], (me, 0, 0)))

    names = ["g_mix_pre", "b_in", "w_pool", "pool_scale", "attn_sinks", "g_mix_post", "g_mlp_pre", "g_mlp_post"]
    sm_w = dict(g_mix_pre=g_mix_pre, b_in=b_in, w_pool=w_pool, pool_scale=pool_scale, attn_sinks=attn_sinks,
                g_mix_post=g_mix_post, g_mlp_pre=g_mlp_pre, g_mlp_post=g_mlp_post)
    sm_m = dict(g_mix_pre=m_g_mix_pre, b_in=m_b_in, w_pool=m_w_pool, pool_scale=m_pool_scale, attn_sinks=m_attn_sinks,
                g_mix_post=m_g_mix_post, g_mlp_pre=m_g_mlp_pre, g_mlp_post=m_g_mlp_post)
    sm_v = dict(g_mix_pre=v_g_mix_pre, b_in=v_b_in, w_pool=v_w_pool, pool_scale=v_pool_scale, attn_sinks=v_attn_sinks,
                g_mix_post=v_g_mix_post, g_mlp_pre=v_g_mlp_pre, g_mlp_post=v_g_mlp_post)
    sizes = {k: sm_w[k].size for k in names}
    sizes.update(loss=D_MODEL, b_in_gates=GATE_WIDTH, b_in_head=C_G)
    sm_g = _unpack(sa_all.reshape(_SMALL_A_ROWS, LANES), _SMALL_A, sizes)
    sm_g.update(_unpack(sb_sum, _SMALL_B, sizes))
    sm_g["b_in"] = jnp.concatenate([sm_g["b_in_head"], sm_g["b_in_gates"]])
    loss = (0.5 / D_MODEL) * jnp.sum(sm_g["loss"])
    two_d = lambda a: a.reshape(-1, a.shape[-1])
    sd_, sm2_, sv2_ = _adamw_call([two_d(sm_w[k]) for k in names], [two_d(sm_g[k].reshape(sm_w[k].shape)) for k in names],
                                  [two_d(sm_m[k]) for k in names], [two_d(sm_v[k]) for k in names], 1, "adamw_small")
    like = lambda vals: {k: a.reshape(sm_w[k].shape) for k, a in zip(names, vals)}
    sm_d, sm_m2, sm_v2 = like(sd_), like(sm2_), like(sv2_)
    sm_gr = {k: sm_g[k].reshape(sm_w[k].shape) for k in names}

    order = ["g_mix_pre", "w_in", "b_in", "w_pool", "pool_scale", "attn_sinks", "w_branch_pool", "w_branch_attn",
             "w_out", "g_mix_post", "g_mlp_pre", "w_up", "w_down", "g_mlp_post"]
    big_names = ["w_in", "w_branch_pool", "w_branch_attn", "w_out", "w_up", "w_down"]
    lead = lambda a: a[None]
    tables = []
    for small_t, big_t in ((sm_gr, big_g), (sm_d, big_d), (sm_m2, big_m2), (sm_v2, big_v2)):
        bt = dict(zip(big_names, big_t))
        tables.append([lead(bt[k]) if k in bt else small_t[k] for k in order])
    return (loss, gx.reshape(B, S, D_MODEL), *tables[0], *tables[1], *tables[2], *tables[3])
```

```python
import jax
import jax.numpy as jnp
from jax import lax
from jax.experimental import pallas as pl
from jax.experimental.pallas import tpu as pltpu

F32 = jnp.float32
MXU_DTYPE = jnp.bfloat16
MESH = pl.DeviceIdType.MESH

D_MODEL = 1024
POOL_WINDOWS = (2, 4, 8, 16)
POOL_WIDTH = 512
POOL_GC = 128
HEAD_DIM = 64
N_Q_HEADS = 8
N_KV_HEADS = 2
GROUP = 4
ATTN_WIDTH = 512
KV_WIDTH = 128
BLOCK = 128
GATE_WIDTH = 2048
IN_WIDTH = 3328
D_FF = 4096
EPS = 1e-6
NEG_INF = -1e30
ROPE_THETA = 500000.0
ROT_DIM = 16
SCALE = HEAD_DIM ** -0.5
C_Q, C_K, C_V, C_G = 512, 1024, 1152, 1280

ADAM_LR = 0.001
ADAM_B1 = 0.9
ADAM_B2 = 0.999
ADAM_EPS = 1e-08
ADAM_WD = 0.01
ADAM_STEP = 10

N_DEV = 8
LANES = 128
VMEM_LIMIT = 56 * 1024 * 1024

NN = (((1,), (0,)), ((), ()))
NT = (((1,), (1,)), ((), ()))
TN = (((0,), (0,)), ((), ()))


def _dot(a, b, dims):
    return lax.dot_general(a, b, dims, preferred_element_type=F32)


def _params(sem=None):
    return pltpu.CompilerParams(dimension_semantics=sem, vmem_limit_bytes=VMEM_LIMIT)


def _tile(n, pref):
    t = min(n, pref)
    assert n % t == 0, (n, t)
    return t


class _Rider:
    def __init__(self, ins, out_shape, n_remote, n_local, plan, aliases=None, lands=None):
        self.ins, self.out_shape, self.n_remote, self.n_local = list(ins), list(out_shape), n_remote, n_local
        self.plan, self.aliases = plan, dict(aliases or {})
        self.lands = lands


def _after(token, rider=None):
    r = rider or _Rider([], [], 0, 0, lambda ins, outs, send, recv, loc, r0, l0: ([], []))
    return _Rider(r.ins + [token], r.out_shape, r.n_remote, r.n_local, r.plan, r.aliases)


def _join(a, b):
    assert not a.aliases and not b.aliases
    n_in, n_out = len(a.ins), len(a.out_shape)

    def plan(ins, outs, send, recv, loc, r0, l0):
        ra, la = a.plan(ins[:n_in], outs[:n_out], send, recv, loc, r0, l0)
        rb, lb = b.plan(ins[n_in:], outs[n_out:], send, recv, loc, r0 + a.n_remote, l0 + a.n_local)
        return ra + rb, la + lb

    return _Rider(a.ins + b.ins, a.out_shape + b.out_shape, a.n_remote + b.n_remote, a.n_local + b.n_local, plan)


def _launch(body, args, *, name, grid, in_specs, out_specs, out_shape, scratch_shapes=(), sem=None, rider=None):
    if rider is None:
        return pl.pallas_call(body, name=name, grid=grid, in_specs=in_specs, out_specs=out_specs, out_shape=out_shape,
                              scratch_shapes=list(scratch_shapes), compiler_params=_params(sem))(*args)
    n_in, n_out, n_scr = len(args), len(out_shape), len(scratch_shapes)
    r_in, r_out = len(rider.ins), len(rider.out_shape)
    copies = rider.n_remote + rider.n_local > 0

    def wrapped(*refs):
        ins, rins = refs[:n_in], refs[n_in:n_in + r_in]
        o0 = n_in + r_in
        outs, routs = refs[o0:o0 + n_out], refs[o0 + n_out:o0 + n_out + r_out]
        s0 = o0 + n_out + r_out
        scr = refs[s0:s0 + n_scr]
        if not copies:
            return body(*ins, *outs, *scr)
        send, recv, loc = refs[s0 + n_scr:]
        first, last = None, None
        for d in range(len(grid)):
            f, l = pl.program_id(d) == 0, pl.program_id(d) == pl.num_programs(d) - 1
            first = f if first is None else first & f
            last = l if last is None else last & l

        def start():
            remote, local = rider.plan(rins, routs, send, recv, loc, 0, 0)
            for cp in local + remote:
                cp.start()

        def finish():
            remote, local = rider.plan(rins, routs, send, recv, loc, 0, 0)
            for cp in remote + local:
                cp.wait()

        if first is None:
            start()
            body(*ins, *outs, *scr)
            finish()
        else:
            pl.when(first)(start)
            body(*ins, *outs, *scr)
            pl.when(last)(finish)

    hbm = pl.BlockSpec(memory_space=pl.ANY)
    dma = pltpu.SemaphoreType.DMA
    res = pl.pallas_call(
        wrapped, name=name, grid=grid, in_specs=list(in_specs) + [hbm] * r_in,
        out_specs=list(out_specs) + [hbm] * r_out, out_shape=list(out_shape) + rider.out_shape,
        scratch_shapes=list(scratch_shapes) + (
            [dma((max(rider.n_remote, 1),)), dma((max(rider.n_remote, 1),)), dma((max(rider.n_local, 1),))] if copies else []),
        input_output_aliases={n_in + i: n_out + o for i, o in rider.aliases.items()},
        compiler_params=_params(sem),
    )(*args, *rider.ins)
    return list(res[:n_out]), list(res[n_out:])


def _comm_call(rider, name):
    return _launch(lambda: None, [], name=name, grid=(), in_specs=[], out_specs=[], out_shape=[], rider=rider)[1]


_HBM = pl.BlockSpec(memory_space=pltpu.HBM)
_SEM = pl.BlockSpec(memory_space=pltpu.SEMAPHORE)
_EFFECT = pltpu.SideEffectType.DATAFLOW_SIDE_EFFECTING


def _copies_start(riders, name, after=None):
    assert all(r.n_local == 0 and not r.aliases for r in riders)
    extra = [] if after is None else [after]
    sizes = [(len(r.ins), len(r.out_shape)) for r in riders]
    bufs = []
    for r in riders:
        lands = r.lands or [lax.empty(s.shape, s.dtype) for s in r.out_shape]
        bufs += [pltpu.with_memory_space_constraint(a, pltpu.HBM) for a in list(r.ins) + list(lands)]
    nb, ng, ne = len(bufs), len(riders), len(extra)

    def body(*refs):
        sems, token, at = refs[2 * nb + ne:2 * nb + ne + 2 * ng], refs[-1], 0
        for g, (r, (ni, no)) in enumerate(zip(riders, sizes)):
            remote, _ = r.plan(refs[at:at + ni], refs[at + ni:at + ni + no], sems[2 * g], sems[2 * g + 1], None, 0, 0)
            for cp in remote:
                cp.start()
            at += ni + no
        token[...] = jnp.zeros_like(token)

    res = pl.pallas_call(
        body, name=name, in_specs=[_HBM] * nb + [pl.BlockSpec(memory_space=pl.ANY)] * ne,
        out_specs=[_HBM] * nb + [_SEM] * (2 * ng) + [pl.BlockSpec(memory_space=pltpu.VMEM)],
        out_shape=[pltpu.HBM(a.shape, a.dtype) for a in bufs]
        + [pltpu.SemaphoreType.DMA((r.n_remote,)) for r in riders for _ in range(2)]
        + [jax.ShapeDtypeStruct((8, LANES), F32)],
        input_output_aliases={i: i for i in range(nb)},
        compiler_params=pltpu.CompilerParams(has_side_effects=_EFFECT),
    )(*bufs, *extra)
    handles, at = [], 0
    for g, (r, (ni, no)) in enumerate(zip(riders, sizes)):
        handles.append((r, list(res[at:at + ni + no]), res[nb + 2 * g], res[nb + 2 * g + 1]))
        at += ni + no
    return handles, res[-1]


def _copies_wait(handles, after, name):
    bufs = [b for _, bs, _, _ in handles for b in bs]
    sems = [s for _, _, send, recv in handles for s in (send, recv)]
    nb, ng = len(bufs), len(handles)
    after = list(after) if isinstance(after, (list, tuple)) else [after]

    def body(*refs):
        at = 0
        for g, (rider, bs, _, _) in enumerate(handles):
            ni = len(rider.ins)
            remote, _ = rider.plan(refs[at:at + ni], refs[at + ni:at + len(bs)], refs[nb + 2 * g], refs[nb + 2 * g + 1],
                                   None, 0, 0)
            for cp in remote:
                cp.wait_send()
                cp.wait_recv()
            at += len(bs)

    res = pl.pallas_call(
        body, name=name, in_specs=[_HBM] * nb + [_SEM] * (2 * ng) + [pl.BlockSpec(memory_space=pl.ANY)] * len(after),
        out_specs=[_HBM] * nb, out_shape=[pltpu.HBM(a.shape, a.dtype) for a in bufs],
        input_output_aliases={i: i for i in range(nb)},
        compiler_params=pltpu.CompilerParams(has_side_effects=_EFFECT),
    )(*bufs, *sems, *after)
    lands, at = [], 0
    for rider, bs, _, _ in handles:
        lands += list(res[at + len(rider.ins):at + len(bs)])
        at += len(bs)
    return lands


def _copies_pass(handles, riders, after, name):
    bufs = [b for _, bs, _, _ in handles for b in bs]
    sems = [s for _, _, send, recv in handles for s in (send, recv)]
    nb, ng = len(bufs), len(handles)
    after = list(after) if isinstance(after, (list, tuple)) else [after]

    def body(*refs):
        new_sems, at = refs[2 * nb + 2 * ng + len(after):], 0
        for g, ((rider, bs, _, _), then) in enumerate(zip(handles, riders)):
            ins, outs = refs[at:at + len(rider.ins)], refs[at + len(rider.ins):at + len(bs)]
            for cp in rider.plan(ins, outs, refs[nb + 2 * g], refs[nb + 2 * g + 1], None, 0, 0)[0]:
                cp.wait_send()
                cp.wait_recv()
            for cp in then.plan(ins, outs, new_sems[2 * g], new_sems[2 * g + 1], None, 0, 0)[0]:
                cp.start()
            at += len(bs)

    res = pl.pallas_call(
        body, name=name, in_specs=[_HBM] * nb + [_SEM] * (2 * ng) + [pl.BlockSpec(memory_space=pl.ANY)] * len(after),
        out_specs=[_HBM] * nb + [_SEM] * (2 * ng),
        out_shape=[pltpu.HBM(a.shape, a.dtype) for a in bufs]
        + [pltpu.SemaphoreType.DMA((r.n_remote,)) for r in riders for _ in range(2)],
        input_output_aliases={i: i for i in range(nb)},
        compiler_params=pltpu.CompilerParams(has_side_effects=_EFFECT),
    )(*bufs, *sems, *after)
    new, at = [], 0
    for g, ((_, bs, _, _), then) in enumerate(zip(handles, riders)):
        new.append((then, list(res[at:at + len(bs)]), res[nb + 2 * g], res[nb + 2 * g + 1]))
        at += len(bs)
    return new


def _rms_r(x):
    return lax.rsqrt(jnp.mean(x * x, axis=-1, keepdims=True) + EPS)


def _rms_bwd(dn, x, r, g):
    xh = x * r
    dxh = dn * g
    dx = r * (dxh - xh * jnp.mean(dxh * xh, axis=-1, keepdims=True))
    return dx, dn * xh


def _rot(t, c, sa, sb):
    outs = []
    for j in range(t.shape[1] // LANES):
        tj = t[:, LANES * j:LANES * (j + 1)]
        outs.append(tj * c + pltpu.roll(tj, LANES - 8, 1) * sa + pltpu.roll(tj, 8, 1) * sb)
    return outs[0] if len(outs) == 1 else jnp.concatenate(outs, axis=1)


def _rot_tables(S):
    pos = jnp.arange(S, dtype=F32)
    inv_freq = ROPE_THETA ** (-jnp.arange(0, ROT_DIM, 2, dtype=F32) / ROT_DIM)
    ang = pos[:, None] * inv_freq[None, :]
    cos, sin = jnp.cos(ang), jnp.sin(ang)
    one = jnp.ones((S, HEAD_DIM - ROT_DIM), F32)
    zero = jnp.zeros((S, HEAD_DIM - ROT_DIM), F32)
    z8 = jnp.zeros((S, 8), F32)
    c = jnp.concatenate([cos, cos, one], axis=1)
    sa = jnp.concatenate([-sin, z8, zero], axis=1)
    sb = jnp.concatenate([z8, sin, zero], axis=1)
    rep = LANES // HEAD_DIM
    return jnp.tile(c, (1, rep)), jnp.tile(sa, (1, rep)), jnp.tile(sb, (1, rep))


def _lane_tile4(k):
    lane = lax.broadcasted_iota(jnp.int32, k.shape, 1)
    rk = pltpu.roll(k, HEAD_DIM, 1)
    x0 = jnp.where(lane < HEAD_DIM, k, rk)
    x1 = jnp.where(lane < HEAD_DIM, rk, k)
    return jnp.concatenate([x0, x0, x1, x1], axis=1)


def _fold_heads(acc):
    zs = []
    for hk in range(N_KV_HEADS):
        a = acc[:, 256 * hk:256 * hk + LANES] + acc[:, 256 * hk + LANES:256 * (hk + 1)]
        zs.append(a + pltpu.roll(a, HEAD_DIM, 1))
    lane = lax.broadcasted_iota(jnp.int32, zs[0].shape, 1)
    return jnp.where(lane < HEAD_DIM, zs[0], zs[1])


def _inproj_call(x, g1, win_t, b_in, rc, rsa, rsb, S, rider=None):
    T = x.shape[0]
    tm = _tile(S, 512)
    nst = S // tm

    def body(x_ref, g1_ref, w_ref, b_ref, c_ref, sa_ref, sb_ref,
             h_ref, u_ref, q_ref, k4_ref, v4_ref, g_ref):
        xv = x_ref[...]
        hb = ((xv * _rms_r(xv)) * g1_ref[...]).astype(MXU_DTYPE)
        h_ref[...] = hb

        def proj(lo, hi):
            return _dot(hb, w_ref[lo:hi, :], NT) + b_ref[:, lo:hi]

        c, sa, sb = c_ref[...], sa_ref[...], sb_ref[...]
        u_ref[...] = proj(0, C_Q)
        q_ref[...] = (_rot(proj(C_Q, C_K), c, sa, sb) * SCALE).astype(MXU_DTYPE)
        kv = proj(C_K, C_G)
        k4_ref[...] = _lane_tile4(_rot(kv[:, :KV_WIDTH], c, sa, sb)).astype(MXU_DTYPE)
        v4_ref[...] = _lane_tile4(kv[:, KV_WIDTH:]).astype(MXU_DTYPE)
        g_ref[...] = jax.nn.sigmoid(proj(C_G, IN_WIDTH)).astype(MXU_DTYPE)

    tok = lambda w: pl.BlockSpec((tm, w), lambda i: (i, 0))
    full = lambda a: pl.BlockSpec(a.shape, lambda i: (0,) * a.ndim)
    tab = pl.BlockSpec((tm, LANES), lambda i: (i % nst, 0))
    return _launch(
        body, [x, g1, win_t, b_in, rc, rsa, rsb], name="inproj_fwd", grid=(T // tm,),
        in_specs=[tok(D_MODEL), full(g1), full(win_t), full(b_in), tab, tab, tab],
        out_specs=[tok(D_MODEL), tok(POOL_WIDTH), tok(ATTN_WIDTH), tok(512), tok(512), tok(GATE_WIDTH)],
        out_shape=[jax.ShapeDtypeStruct((T, D_MODEL), MXU_DTYPE), jax.ShapeDtypeStruct((T, POOL_WIDTH), F32),
                   jax.ShapeDtypeStruct((T, ATTN_WIDTH), MXU_DTYPE), jax.ShapeDtypeStruct((T, 512), MXU_DTYPE),
                   jax.ShapeDtypeStruct((T, 512), MXU_DTYPE), jax.ShapeDtypeStruct((T, GATE_WIDTH), MXU_DTYPE)],
        sem=("arbitrary",), rider=rider)


def _shift_rows(a, k, rows):
    n = a.shape[0]
    if k > 0:
        return jnp.where(rows >= k, pltpu.roll(a, k, 0), 0.0)
    return jnp.where(rows < n + k, pltpu.roll(a, n + k, 0), 0.0)


def _win_sum(a, w, rows, sign):
    s, k = a, 1
    while k < w:
        s = s + _shift_rows(s, sign * k, rows)
        k *= 2
    return s


def _pool_diff(ug, w, rows):
    inv = 1.0 / jnp.minimum(rows + 1, w).astype(F32)
    return _win_sum(ug, w, rows, 1) * inv - ug, inv


def _pool_call(u, w_pool, pool_scale, S, rider):
    T = u.shape[0]

    def body(u_ref, w_ref, ps_ref, y_ref):
        rows = lax.broadcasted_iota(jnp.int32, (S, POOL_GC), 0)
        for gi, w in enumerate(POOL_WINDOWS):
            sl = slice(POOL_GC * gi, POOL_GC * (gi + 1))
            diff, _ = _pool_diff(u_ref[:, sl], w, rows)
            mixed = _dot(diff.astype(MXU_DTYPE), w_ref[gi], NN)
            y_ref[:, sl] = (mixed * ps_ref[:, sl]).astype(MXU_DTYPE)

    seq = pl.BlockSpec((S, POOL_WIDTH), lambda b: (b, 0))
    return _launch(
        body, [u, w_pool, pool_scale], name="pool_fwd", grid=(T // S,),
        in_specs=[seq, pl.BlockSpec(w_pool.shape, lambda b: (0, 0, 0)), pl.BlockSpec(pool_scale.shape, lambda b: (0, 0))],
        out_specs=[seq], out_shape=[jax.ShapeDtypeStruct((T, POOL_WIDTH), MXU_DTYPE)], sem=("arbitrary",), rider=rider)


def _pool_bwd_call(u, dyp, w_pool, pool_scale, S, rider=None):
    T = u.shape[0]

    def body(u_ref, dy_ref, w_ref, ps_ref, du_ref, dw_ref, dps_ref):
        @pl.when(pl.program_id(0) == 0)
        def _():
            dw_ref[...] = jnp.zeros_like(dw_ref)
            dps_ref[...] = jnp.zeros_like(dps_ref)

        rows = lax.broadcasted_iota(jnp.int32, (S, POOL_GC), 0)
        for gi, w in enumerate(POOL_WINDOWS):
            sl = slice(POOL_GC * gi, POOL_GC * (gi + 1))
            diff, inv = _pool_diff(u_ref[:, sl], w, rows)
            diffb = diff.astype(MXU_DTYPE)
            wg = w_ref[gi]
            mixed = _dot(diffb, wg, NN)
            dy = dy_ref[:, sl]
            dps_ref[:, sl] += jnp.sum(dy * mixed, axis=0, keepdims=True)
            dmb = (dy * ps_ref[:, sl]).astype(MXU_DTYPE)
            dw_ref[gi] += _dot(diffb, dmb, TN)
            ddiff = _dot(dmb, wg, NT)
            du_ref[:, sl] = (_win_sum(ddiff * inv, w, rows, -1) - ddiff).astype(MXU_DTYPE)

    seq = pl.BlockSpec((S, POOL_WIDTH), lambda b: (b, 0))
    return _launch(
        body, [u, dyp, w_pool, pool_scale], name="pool_bwd", grid=(T // S,),
        in_specs=[seq, seq, pl.BlockSpec(w_pool.shape, lambda b: (0, 0, 0)), pl.BlockSpec(pool_scale.shape, lambda b: (0, 0))],
        out_specs=[seq, pl.BlockSpec(w_pool.shape, lambda b: (0, 0, 0)), pl.BlockSpec(pool_scale.shape, lambda b: (0, 0))],
        out_shape=[jax.ShapeDtypeStruct((T, POOL_WIDTH), MXU_DTYPE), jax.ShapeDtypeStruct(w_pool.shape, F32),
                   jax.ShapeDtypeStruct(pool_scale.shape, F32)],
        sem=("arbitrary",), rider=rider)


def _attn_consts():
    lane_g = lax.broadcasted_iota(jnp.int32, (BLOCK, 256), 1) >> 6
    rgrp = lax.broadcasted_iota(jnp.int32, (GROUP * BLOCK, 1), 0) >> 7
    rel = lax.broadcasted_iota(jnp.int32, (BLOCK, 256), 0) - lax.broadcasted_iota(jnp.int32, (BLOCK, 256), 1)

    def bias(off):
        ok = (rel + off >= 0) & (rel + off < BLOCK)
        return jnp.concatenate([jnp.where(ok, 0.0, NEG_INF)] * GROUP, axis=0)

    return lane_g, rgrp, bias(0), bias(BLOCK)


def _sink_rows(sink_ref, hk, rgrp):
    sv = jnp.zeros(rgrp.shape, F32)
    for g in range(GROUP):
        sv = jnp.where(rgrp == g, sink_ref[0, GROUP * hk + g], sv)
    return sv


def _stack_heads(xb, lane_g):
    return jnp.concatenate([jnp.where(lane_g == g, xb, jnp.zeros_like(xb)) for g in range(GROUP)], axis=0)


def _unstack_heads(xs, lane_g):
    out = jnp.where(lane_g == 0, xs[0:BLOCK], 0.0)
    for g in range(1, GROUP):
        out = out + jnp.where(lane_g == g, xs[BLOCK * g:BLOCK * (g + 1)], 0.0)
    return out


def _attn_probs(qs, kb, bias, sv):
    s = _dot(qs, kb, NT) + bias
    m = jnp.maximum(jnp.max(s, axis=1, keepdims=True), sv)
    e = jnp.exp(s - m)
    es = jnp.exp(sv - m)
    inv_l = 1.0 / (jnp.sum(e, axis=1, keepdims=True) + es)
    return e * inv_l, es * inv_l


def _attn_blocks(nb, blk, carry, per=1):
    carry = blk(0, 0, True, carry)
    per = per if (nb - 1) % per == 0 else 1

    def step(i, c):
        for k in range(per):
            n = 1 + per * i + k
            c = blk(pl.multiple_of(n * BLOCK, BLOCK), pl.multiple_of((n - 1) * BLOCK, BLOCK), False, c)
        return c

    return lax.fori_loop(0, (nb - 1) // per, step, carry)


def _attn_call(sinks, q, k4, v4, S, rider=None):
    T = q.shape[0]
    nb = S // BLOCK

    def body(sink_ref, q_ref, k_ref, v_ref, o_ref):
        lane_g, rgrp, bias_first, bias_later = _attn_consts()
        svs = [_sink_rows(sink_ref, hk, rgrp) for hk in range(N_KV_HEADS)]

        def blk(q0, k0, first, carry):
            for hk in range(N_KV_HEADS):
                cs = slice(256 * hk, 256 * (hk + 1))
                qs = _stack_heads(q_ref[pl.ds(q0, BLOCK), cs], lane_g)
                p, _ = _attn_probs(qs, k_ref[pl.ds(k0, 2 * BLOCK), cs], bias_first if first else bias_later, svs[hk])
                o = _dot(p.astype(MXU_DTYPE), v_ref[pl.ds(k0, 2 * BLOCK), cs], NN)
                o_ref[pl.ds(q0, BLOCK), cs] = _unstack_heads(o, lane_g).astype(MXU_DTYPE)
            return carry

        _attn_blocks(nb, blk, 0, per=3)

    seq = pl.BlockSpec((S, ATTN_WIDTH), lambda b: (b, 0))
    return _launch(
        body, [sinks, q, k4, v4], name="attn_fwd", grid=(T // S,),
        in_specs=[pl.BlockSpec(memory_space=pltpu.SMEM), seq, seq, seq],
        out_specs=[seq], out_shape=[jax.ShapeDtypeStruct((T, ATTN_WIDTH), MXU_DTYPE)],
        sem=("arbitrary",), rider=rider)


def _attn_bwd_call(sinks, q, k4, v4, do, rc, rsa, rsb, S, rider=None):
    T = q.shape[0]
    nb = S // BLOCK

    def body(sink_ref, q_ref, k_ref, v_ref, do_ref, c_ref, sa_ref, sb_ref,
             dq_ref, dk_ref, dv_ref, ds_ref, dk_acc, dv_acc):
        lane_g, rgrp, bias_first, bias_later = _attn_consts()
        svs = [_sink_rows(sink_ref, hk, rgrp) for hk in range(N_KV_HEADS)]
        lane1 = lax.broadcasted_iota(jnp.int32, (1, LANES), 1)
        dk_acc[...] = jnp.zeros_like(dk_acc)
        dv_acc[...] = jnp.zeros_like(dv_acc)

        def blk(q0, k0, first, dsink):
            rows = pl.ds(q0, BLOCK)
            c, sa, sb = c_ref[rows, :], sa_ref[rows, :], sb_ref[rows, :]
            for hk in range(N_KV_HEADS):
                cs = slice(256 * hk, 256 * (hk + 1))
                qs = _stack_heads(q_ref[rows, cs], lane_g)
                dos = _stack_heads(do_ref[rows, cs], lane_g)
                kb = k_ref[pl.ds(k0, 2 * BLOCK), cs]
                vb = v_ref[pl.ds(k0, 2 * BLOCK), cs]
                p, ps = _attn_probs(qs, kb, bias_first if first else bias_later, svs[hk])
                dp = _dot(dos, vb, NT)
                delta = jnp.sum(p * dp, axis=1, keepdims=True)
                dsb = (p * (dp - delta)).astype(MXU_DTYPE)
                dqb = _unstack_heads(_dot(dsb, kb, NN), lane_g) * SCALE
                dq_ref[rows, cs] = _rot(dqb, c, -sa, -sb).astype(MXU_DTYPE)
                dk_acc[pl.ds(k0, 2 * BLOCK), cs] += _dot(dsb, qs, TN)
                dv_acc[pl.ds(k0, 2 * BLOCK), cs] += _dot(p.astype(MXU_DTYPE), dos, TN)
                psd = ps * delta
                for g in range(GROUP):
                    val = -jnp.sum(psd[BLOCK * g:BLOCK * (g + 1)], axis=0, keepdims=True)
                    dsink = dsink + jnp.where(lane1 == GROUP * hk + g, val, 0.0)
            return dsink

        dsink = _attn_blocks(nb, blk, jnp.zeros((1, LANES), F32))
        dk_ref[...] = _rot(_fold_heads(dk_acc[...]), c_ref[...], -sa_ref[...], -sb_ref[...]).astype(MXU_DTYPE)
        dv_ref[...] = _fold_heads(dv_acc[...]).astype(MXU_DTYPE)
        ds_ref[...] = jnp.broadcast_to(dsink, ds_ref.shape)

    seq = pl.BlockSpec((S, ATTN_WIDTH), lambda b: (b, 0))
    kvs = pl.BlockSpec((S, KV_WIDTH), lambda b: (b, 0))
    tab = pl.BlockSpec((S, LANES), lambda b: (0, 0))
    nseq = T // S
    return _launch(
        body, [sinks, q, k4, v4, do, rc, rsa, rsb], name="attn_bwd", grid=(nseq,),
        in_specs=[pl.BlockSpec(memory_space=pltpu.SMEM), seq, seq, seq, seq, tab, tab, tab],
        out_specs=[seq, kvs, kvs, pl.BlockSpec((8, LANES), lambda b: (b, 0))],
        out_shape=[jax.ShapeDtypeStruct((T, ATTN_WIDTH), MXU_DTYPE), jax.ShapeDtypeStruct((T, KV_WIDTH), MXU_DTYPE),
                   jax.ShapeDtypeStruct((T, KV_WIDTH), MXU_DTYPE), jax.ShapeDtypeStruct((8 * nseq, LANES), F32)],
        scratch_shapes=[pltpu.VMEM((S, 512), F32), pltpu.VMEM((S, 512), F32)],
        sem=("arbitrary",), rider=rider)


def _branch_weights(wbp_ref, wba_ref, wbp_s, wba_s):
    @pl.when(pl.program_id(0) == 0)
    def _():
        for j in range(N_DEV):
            wbp_s[:, LANES * j:LANES * (j + 1)] = wbp_ref[j]
            wba_s[:, LANES * j:LANES * (j + 1)] = wba_ref[j]


def _mix_fwd_call(yp, ya, g, x, wbp, wba, wout, g2, g3, rider=None):
    T = x.shape[0]
    tm = _tile(T, 512)

    def body(yp_ref, ya_ref, g_ref, x_ref, wbp_ref, wba_ref, wout_ref, g2_ref, g3_ref,
             mix_ref, x1_ref, h2_ref, wbp_s, wba_s):
        _branch_weights(wbp_ref, wba_ref, wbp_s, wba_s)
        bp = _dot(yp_ref[...], wbp_s[...], NN)
        ba = _dot(ya_ref[...], wba_s[...], NN)
        merged = g_ref[:, :D_MODEL].astype(F32) * bp + g_ref[:, D_MODEL:].astype(F32) * ba
        mix = _dot(merged.astype(MXU_DTYPE), wout_ref[...], NN)
        mix_ref[...] = mix
        x1 = x_ref[...] + (mix * _rms_r(mix)) * g2_ref[...]
        x1_ref[...] = x1
        h2_ref[...] = ((x1 * _rms_r(x1)) * g3_ref[...]).astype(MXU_DTYPE)

    tok = lambda w: pl.BlockSpec((tm, w), lambda i: (i, 0))
    full = lambda a: pl.BlockSpec(a.shape, lambda i: (0,) * a.ndim)
    return _launch(
        body, [yp, ya, g, x, wbp, wba, wout, g2, g3], name="mix_fwd", grid=(T // tm,),
        in_specs=[tok(POOL_WIDTH), tok(ATTN_WIDTH), tok(GATE_WIDTH), tok(D_MODEL), full(wbp), full(wba), full(wout),
                  full(g2), full(g3)],
        out_specs=[tok(D_MODEL), tok(D_MODEL), tok(D_MODEL)],
        out_shape=[jax.ShapeDtypeStruct((T, D_MODEL), F32), jax.ShapeDtypeStruct((T, D_MODEL), F32),
                   jax.ShapeDtypeStruct((T, D_MODEL), MXU_DTYPE)],
        scratch_shapes=[pltpu.VMEM((POOL_WIDTH, D_MODEL), MXU_DTYPE), pltpu.VMEM((ATTN_WIDTH, D_MODEL), MXU_DTYPE)],
        sem=("arbitrary",), rider=rider)


def _mix_bwd_call(dx1, mix, yp, ya, g, wbp, wba, wout, g2, rider=None):
    T = dx1.shape[0]
    tm = _tile(T, 512)

    def body(dx1_ref, mix_ref, yp_ref, ya_ref, g_ref, wbp_ref, wba_ref, wout_ref, g2_ref,
             dyp_ref, do_ref, dgates_ref, dg2_ref, dbg_ref, gout_ref, gbp_ref, gba_ref,
             wbp_s, wba_s, acc_out, acc_bp, acc_ba, sem):
        _branch_weights(wbp_ref, wba_ref, wbp_s, wba_s)
        step = pl.program_id(0)

        @pl.when(step == 0)
        def _():
            dg2_ref[...] = jnp.zeros_like(dg2_ref)
            dbg_ref[...] = jnp.zeros_like(dbg_ref)
            acc_out[...] = jnp.zeros_like(acc_out)
            acc_bp[...] = jnp.zeros_like(acc_bp)
            acc_ba[...] = jnp.zeros_like(acc_ba)

        mix = mix_ref[...]
        dmix, dg2 = _rms_bwd(dx1_ref[...], mix, _rms_r(mix), g2_ref[...])
        dg2_ref[...] += jnp.sum(dg2, axis=0, keepdims=True)
        dmixb = dmix.astype(MXU_DTYPE)
        dmerged = _dot(dmixb, wout_ref[...], NT)
        yp, ya = yp_ref[...], ya_ref[...]
        bp = _dot(yp, wbp_s[...], NN)
        ba = _dot(ya, wba_s[...], NN)
        gp, ga = g_ref[:, :D_MODEL].astype(F32), g_ref[:, D_MODEL:].astype(F32)
        acc_out[...] += _dot((gp * bp + ga * ba).astype(MXU_DTYPE), dmixb, TN)
        dgp = dmerged * bp * (gp * (1.0 - gp))
        dga = dmerged * ba * (ga * (1.0 - ga))
        dbg_ref[:, :D_MODEL] += jnp.sum(dgp, axis=0, keepdims=True)
        dbg_ref[:, D_MODEL:] += jnp.sum(dga, axis=0, keepdims=True)
        dgates_ref[:, :D_MODEL] = dgp.astype(MXU_DTYPE)
        dgates_ref[:, D_MODEL:] = dga.astype(MXU_DTYPE)
        dbp = (dmerged * gp).astype(MXU_DTYPE)
        dba = (dmerged * ga).astype(MXU_DTYPE)
        acc_bp[...] += _dot(yp, dbp, TN)
        acc_ba[...] += _dot(ya, dba, TN)
        dyp_ref[...] = _dot(dbp, wbp_s[...], NT)
        do_ref[...] = _dot(dba, wba_s[...], NT).astype(MXU_DTYPE)

        @pl.when(step == pl.num_programs(0) - 1)
        def _():
            copies = [pltpu.make_async_copy(acc_out, gout_ref, sem.at[0])]
            for j in range(N_DEV):
                cols = slice(LANES * j, LANES * (j + 1))
                copies.append(pltpu.make_async_copy(acc_bp.at[:, cols], gbp_ref.at[j], sem.at[1 + j]))
                copies.append(pltpu.make_async_copy(acc_ba.at[:, cols], gba_ref.at[j], sem.at[1 + N_DEV + j]))
            for cp in copies:
                cp.start()
            for cp in copies:
                cp.wait()

    tok = lambda w: pl.BlockSpec((tm, w), lambda i: (i, 0))
    full = lambda a: pl.BlockSpec(a.shape, lambda i: (0,) * a.ndim)
    acc = lambda w: pl.BlockSpec((1, w), lambda i: (0, 0))
    hbm = pl.BlockSpec(memory_space=pl.ANY)
    sd = jax.ShapeDtypeStruct
    return _launch(
        body, [dx1, mix, yp, ya, g, wbp, wba, wout, g2], name="mix_bwd", grid=(T // tm,),
        in_specs=[tok(D_MODEL), tok(D_MODEL), tok(POOL_WIDTH), tok(ATTN_WIDTH), tok(GATE_WIDTH), full(wbp), full(wba),
                  full(wout), full(g2)],
        out_specs=[tok(POOL_WIDTH), tok(ATTN_WIDTH), tok(GATE_WIDTH), acc(D_MODEL), acc(GATE_WIDTH), hbm, hbm, hbm],
        out_shape=[sd((T, POOL_WIDTH), F32), sd((T, ATTN_WIDTH), MXU_DTYPE), sd((T, GATE_WIDTH), MXU_DTYPE),
                   sd((1, D_MODEL), F32), sd((1, GATE_WIDTH), F32), sd((D_MODEL, D_MODEL), F32),
                   sd((N_DEV, POOL_WIDTH, LANES), F32), sd((N_DEV, ATTN_WIDTH, LANES), F32)],
        scratch_shapes=[pltpu.VMEM((POOL_WIDTH, D_MODEL), MXU_DTYPE), pltpu.VMEM((ATTN_WIDTH, D_MODEL), MXU_DTYPE),
                        pltpu.VMEM((D_MODEL, D_MODEL), F32), pltpu.VMEM((POOL_WIDTH, D_MODEL), F32),
                        pltpu.VMEM((ATTN_WIDTH, D_MODEL), F32), pltpu.SemaphoreType.DMA((1 + 2 * N_DEV,))],
        sem=("arbitrary",), rider=rider)


def _mlp_up_call(h2, wup, rider):
    T = h2.shape[0]
    tm = _tile(T, 512)
    fc = D_FF // N_DEV

    def body(h2_ref, wup_ref, act_ref):
        h2 = h2_ref[...]
        for j in range(N_DEV):
            rl = jnp.maximum(_dot(h2, wup_ref[j], NN), 0.0)
            act_ref[:, fc * j:fc * (j + 1)] = (rl * rl).astype(MXU_DTYPE)

    sd = jax.ShapeDtypeStruct
    return _launch(
        body, [h2, wup], name="mlp_up", grid=(T // tm,),
        in_specs=[pl.BlockSpec((tm, D_MODEL), lambda i: (i, 0)),
                  pl.BlockSpec(wup.shape, lambda i: (0, 0, 0), pipeline_mode=pl.Buffered(1))],
        out_specs=[pl.BlockSpec((tm, D_FF), lambda i: (i, 0))], out_shape=[sd((T, D_FF), MXU_DTYPE)],
        sem=("arbitrary",), rider=rider)


def _mlp_call(x1, act, target, wup, wdown, g3, g4):
    T = x1.shape[0]
    tm = _tile(T, 256)
    fc = D_FF // N_DEV

    def body(x1_ref, act_ref, t_ref, wup_ref, wdown_ref, g3_ref, g4_ref,
             da_ref, dff_ref, dx1_ref, dg3_ref, dg4_ref, loss_ref):
        @pl.when(pl.program_id(0) == 0)
        def _():
            dg3_ref[...] = jnp.zeros_like(dg3_ref)
            dg4_ref[...] = jnp.zeros_like(dg4_ref)
            loss_ref[...] = jnp.zeros_like(loss_ref)

        ff = jnp.zeros((tm, D_MODEL), F32)
        for j in range(N_DEV):
            ff = ff + _dot(act_ref[:, fc * j:fc * (j + 1)], wdown_ref[j], NN)
        x1 = x1_ref[...]
        r4 = _rms_r(ff)
        err = x1 + (ff * r4) * g4_ref[...] - t_ref[...]
        loss_ref[...] += jnp.sum(err * err, axis=0, keepdims=True)
        dy = err * (1.0 / D_MODEL)
        dff, dg4 = _rms_bwd(dy, ff, r4, g4_ref[...])
        dg4_ref[...] += jnp.sum(dg4, axis=0, keepdims=True)
        dffb = dff.astype(MXU_DTYPE)
        dff_ref[...] = dffb
        dh2 = jnp.zeros((tm, D_MODEL), F32)
        for j in range(N_DEV):
            sl = slice(fc * j, fc * (j + 1))
            rl = jnp.sqrt(act_ref[:, sl].astype(F32))
            dab = (_dot(dffb, wdown_ref[j], NT) * (2.0 * rl)).astype(MXU_DTYPE)
            da_ref[:, sl] = dab
            dh2 = dh2 + _dot(dab, wup_ref[j], NT)
        dx1, dg3 = _rms_bwd(dh2, x1, _rms_r(x1), g3_ref[...])
        dg3_ref[...] += jnp.sum(dg3, axis=0, keepdims=True)
        dx1_ref[...] = dy + dx1

    tok = lambda w: pl.BlockSpec((tm, w), lambda i: (i, 0))
    full = lambda a: pl.BlockSpec(a.shape, lambda i: (0,) * a.ndim, pipeline_mode=pl.Buffered(1))
    vec = pl.BlockSpec((1, D_MODEL), lambda i: (0, 0))
    sd = jax.ShapeDtypeStruct
    return pl.pallas_call(
        body, name="mlp_down_bwd", grid=(T // tm,),
        in_specs=[tok(D_MODEL), tok(D_FF), tok(D_MODEL), full(wup), full(wdown), vec, vec],
        out_specs=[tok(D_FF), tok(D_MODEL), tok(D_MODEL), vec, vec, vec],
        out_shape=[sd((T, D_FF), MXU_DTYPE), sd((T, D_MODEL), MXU_DTYPE),
                   sd((T, D_MODEL), F32), sd((1, D_MODEL), F32), sd((1, D_MODEL), F32), sd((1, D_MODEL), F32)],
        compiler_params=_params(("arbitrary",)),
    )(x1, act, target, wup, wdown, g3, g4)


def _inproj_bwd_call(du, dq, dk, dv, dgates, dx1, x, win_t, g1, rider=None):
    T = x.shape[0]
    tm = _tile(T, 512)

    def body(du_ref, dq_ref, dk_ref, dv_ref, dgt_ref, dx1_ref, x_ref, w_ref, g1_ref, gx_ref, dg1_ref, db_ref):
        @pl.when(pl.program_id(0) == 0)
        def _():
            dg1_ref[...] = jnp.zeros_like(dg1_ref)
            db_ref[...] = jnp.zeros_like(db_ref)

        dh = jnp.zeros((tm, D_MODEL), F32)
        for ref, lo, hi in ((du_ref, 0, C_Q), (dq_ref, C_Q, C_K), (dk_ref, C_K, C_V), (dv_ref, C_V, C_G),
                            (dgt_ref, C_G, IN_WIDTH)):
            piece = ref[...]
            dh = dh + _dot(piece, w_ref[lo:hi, :], NN)
            if hi <= C_G:
                db_ref[:, lo:hi] += jnp.sum(piece.astype(F32), axis=0, keepdims=True)
        xv = x_ref[...]
        dx, dg1 = _rms_bwd(dh, xv, _rms_r(xv), g1_ref[...])
        dg1_ref[...] += jnp.sum(dg1, axis=0, keepdims=True)
        gx_ref[...] = dx1_ref[...] + dx

    tok = lambda w: pl.BlockSpec((tm, w), lambda i: (i, 0))
    full = lambda a: pl.BlockSpec(a.shape, lambda i: (0,) * a.ndim)
    sd = jax.ShapeDtypeStruct
    return _launch(
        body, [du, dq, dk, dv, dgates, dx1, x, win_t, g1], name="inproj_bwd", grid=(T // tm,),
        in_specs=[tok(POOL_WIDTH), tok(ATTN_WIDTH), tok(KV_WIDTH), tok(KV_WIDTH), tok(GATE_WIDTH), tok(D_MODEL),
                  tok(D_MODEL), full(win_t), full(g1)],
        out_specs=[tok(D_MODEL), pl.BlockSpec((1, D_MODEL), lambda i: (0, 0)), pl.BlockSpec((1, C_G), lambda i: (0, 0))],
        out_shape=[sd((T, D_MODEL), F32), sd((1, D_MODEL), F32), sd((1, C_G), F32)],
        sem=("arbitrary",), rider=rider)


WGRAD_TOKENS = 1024


def _wgrad_rows_call(a, b, name, rider=None):
    T, K = a.shape
    N = b.shape[1]
    tm = _tile(T, WGRAD_TOKENS)
    kb = min(K, 1024)
    per = kb // (K // N_DEV)

    def body(a_ref, b_ref, o_ref):
        @pl.when(pl.program_id(1) == 0)
        def _():
            o_ref[...] = jnp.zeros_like(o_ref)

        d = _dot(a_ref[...], b_ref[...], TN)
        rs = kb // per
        for j in range(per):
            o_ref[j] += d[rs * j:rs * (j + 1)]

    return _launch(
        body, [a, b], name=name, grid=(K // kb, T // tm),
        in_specs=[pl.BlockSpec((tm, kb), lambda i, t: (t, i)), pl.BlockSpec((tm, N), lambda i, t: (t, 0))],
        out_specs=[pl.BlockSpec((per, K // N_DEV, N), lambda i, t: (i, 0, 0))],
        out_shape=[jax.ShapeDtypeStruct((N_DEV, K // N_DEV, N), F32)],
        sem=("arbitrary", "arbitrary"), rider=rider)


def _wgrad_cols_call(a, b, name, rider=None):
    T, K = a.shape
    N = b.shape[1]
    tm = _tile(T, WGRAD_TOKENS)
    nb = min(N, 1024)
    per = nb // (N // N_DEV)

    def body(a_ref, b_ref, o_ref):
        @pl.when(pl.program_id(1) == 0)
        def _():
            o_ref[...] = jnp.zeros_like(o_ref)

        d = _dot(a_ref[...], b_ref[...], TN)
        cs = nb // per
        for j in range(per):
            o_ref[j] += d[:, cs * j:cs * (j + 1)]

    return _launch(
        body, [a, b], name=name, grid=(N // nb, T // tm),
        in_specs=[pl.BlockSpec((tm, K), lambda i, t: (t, 0)), pl.BlockSpec((tm, nb), lambda i, t: (t, i))],
        out_specs=[pl.BlockSpec((per, K, N // N_DEV), lambda i, t: (i, 0, 0))],
        out_shape=[jax.ShapeDtypeStruct((N_DEV, K, N // N_DEV), F32)],
        sem=("arbitrary", "arbitrary"), rider=rider)


def _wgrad_in_call(du, dq, dk, dv, dgates, h, rider=None):
    T = h.shape[0]
    tm = _tile(T, WGRAD_TOKENS)
    rows = IN_WIDTH // N_DEV

    def body(du_ref, dq_ref, dk_ref, dv_ref, dgt_ref, h_ref, o_ref, acc, sem):
        t = pl.program_id(0)

        @pl.when(t == 0)
        def _():
            acc[...] = jnp.zeros_like(acc)

        hv = h_ref[...]
        for ref, lo, hi in ((du_ref, 0, C_Q), (dq_ref, C_Q, C_K), (dk_ref, C_K, C_V), (dv_ref, C_V, C_G),
                            (dgt_ref, C_G, IN_WIDTH)):
            acc[lo:hi, :] += _dot(ref[...], hv, TN)

        @pl.when(t == pl.num_programs(0) - 1)
        def _():
            copies = [pltpu.make_async_copy(acc.at[pl.ds(rows * j, rows), :], o_ref.at[j], sem.at[j])
                      for j in range(N_DEV)]
            for cp in copies:
                cp.start()
            for cp in copies:
                cp.wait()

    tok = lambda w: pl.BlockSpec((tm, w), lambda t: (t, 0))
    return _launch(
        body, [du, dq, dk, dv, dgates, h], name="wgrad_in", grid=(T // tm,),
        in_specs=[tok(POOL_WIDTH), tok(ATTN_WIDTH), tok(KV_WIDTH), tok(KV_WIDTH), tok(GATE_WIDTH), tok(D_MODEL)],
        out_specs=[pl.BlockSpec(memory_space=pl.ANY)],
        out_shape=[jax.ShapeDtypeStruct((N_DEV, rows, D_MODEL), F32)],
        scratch_shapes=[pltpu.VMEM((IN_WIDTH, D_MODEL), F32), pltpu.SemaphoreType.DMA((N_DEV,))],
        sem=("arbitrary",), rider=rider)


def _coords():
    return lax.axis_index("x"), lax.axis_index("y"), lax.axis_index("c")


def _ag_route():
    x, y, c = _coords()
    return (x, y, c), (x, y, 1 - c), (x ^ (1 - c), y ^ c, c), (x ^ c, y ^ (1 - c), c), (1 - x, 1 - y, c)


def _rider_ag_first(shard, me):
    def plan(ins, outs, send, recv, loc, r0, l0):
        own, *peers = _ag_route()
        return [pltpu.make_async_remote_copy(
            src_ref=ins[0], dst_ref=outs[0].at[_slot(own)], send_sem=send.at[r0 + k], recv_sem=recv.at[r0 + k],
            device_id=peers[k], device_id_type=MESH) for k in range(3)], []

    return _Rider([shard], [jax.ShapeDtypeStruct((N_DEV,) + shard.shape, shard.dtype)], 3, 0, plan,
                  lands=[_gather_buffer(shard, me)])


def _rider_ag_onward(shard, stage):
    def plan(ins, outs, send, recv, loc, r0, l0):
        own, sibling, near1, near2, diag = _ag_route()
        moves = [(near1, near2), (near1, sibling), (near2, sibling)] if stage == 2 else [(diag, sibling)]
        copies = []
        for k, (block, to) in enumerate(moves):
            part = outs[0].at[_slot(block)]
            copies.append(pltpu.make_async_remote_copy(src_ref=part, dst_ref=part, send_sem=send.at[r0 + k],
                                                       recv_sem=recv.at[r0 + k], device_id=to, device_id_type=MESH))
        return copies, []

    return _Rider([shard], [jax.ShapeDtypeStruct((N_DEV,) + shard.shape, shard.dtype)], 3 if stage == 2 else 1, 0, plan)


def _slot(p):
    return 4 * p[0] + 2 * p[1] + p[2]


ALL = "all"
LOCAL = "local"


def _rows(ref, span):
    return ref if span == ALL else ref.at[pl.ds(span[0], span[1])]


def _rider_ag(items):
    ins, out_shape, aliases, where = [], [], {}, []
    n_remote = n_local = 0
    for t, (shard, buf, snd, fwd) in enumerate(items):
        i_shard = i_buf = None
        if snd is not None:
            i_shard = len(ins)
            ins.append(shard)
        if buf is not None:
            i_buf = len(ins)
            ins.append(buf)
            aliases[i_buf] = t
            out_shape.append(jax.ShapeDtypeStruct(buf.shape, buf.dtype))
        else:
            assert fwd is None and snd is not None
            out_shape.append(jax.ShapeDtypeStruct((N_DEV,) + shard.shape, shard.dtype))
        where.append((i_shard, i_buf, n_remote, n_local))
        n_remote += (4 if snd not in (None, LOCAL) else 0) + (3 if fwd is not None else 0)
        n_local += 1 if snd is not None else 0

    def plan(rins, routs, send, recv, loc, r0, l0):
        x, y, c = _coords()
        peers = [(x, y, 1 - c), (1 - x, y, c), (x, 1 - y, c), (1 - x, 1 - y, c)]
        remote, local = [], []
        for t, (shard, buf, snd, fwd) in enumerate(items):
            i_shard, i_buf, k, l = where[t]
            k, l = r0 + k, l0 + l
            if snd is not None:
                span = ALL if snd == LOCAL else snd
                src, dst = _rows(rins[i_shard], span), _rows(routs[t].at[_slot((x, y, c))], span)
                local.append(pltpu.make_async_copy(src, dst, loc.at[l]))
                for peer in (peers if snd != LOCAL else []):
                    remote.append(pltpu.make_async_remote_copy(
                        src_ref=src, dst_ref=dst, send_sem=send.at[k], recv_sem=recv.at[k],
                        device_id=peer, device_id_type=MESH))
                    k += 1
            if fwd is not None:
                for px, py, pc in peers[1:]:
                    s = _slot((px, py, pc))
                    remote.append(pltpu.make_async_remote_copy(
                        src_ref=_rows(rins[i_buf].at[s], fwd), dst_ref=_rows(routs[t].at[s], fwd),
                        send_sem=send.at[k], recv_sem=recv.at[k], device_id=peers[0], device_id_type=MESH))
                    k += 1
        return remote, local

    return _Rider(ins, out_shape, n_remote, n_local, plan, aliases)


def _gather_buffer(shard, me):
    return lax.dynamic_update_slice(lax.empty((N_DEV,) + shard.shape, shard.dtype), shard[None], (me, 0, 0))


def _rider_ag_remote(shards, me):
    n = len(shards)

    def plan(ins, outs, send, recv, loc, r0, l0):
        x, y, c = _coords()
        remote = []
        for t in range(n):
            dst = outs[t].at[_slot((x, y, c))]
            for k, peer in enumerate([(x, y, 1 - c), (1 - x, y, c), (x, 1 - y, c), (1 - x, 1 - y, c)]):
                remote.append(pltpu.make_async_remote_copy(
                    src_ref=ins[t], dst_ref=dst, send_sem=send.at[r0 + 4 * t + k], recv_sem=recv.at[r0 + 4 * t + k],
                    device_id=peer, device_id_type=MESH))
        return remote, []

    return _Rider(shards, [jax.ShapeDtypeStruct((N_DEV,) + s.shape, s.dtype) for s in shards], 4 * n, 0, plan,
                  lands=[_gather_buffer(s, me) for s in shards])


def _rider_rs_sibling(grads):
    n = len(grads)

    def plan(ins, outs, send, recv, loc, r0, l0):
        x, y, c = _coords()
        remote = []
        for t in range(n):
            for q in range(4):
                remote.append(pltpu.make_async_remote_copy(
                    src_ref=ins[t].at[q, 1 - c], dst_ref=outs[t].at[q], send_sem=send.at[r0 + 4 * t + q],
                    recv_sem=recv.at[r0 + 4 * t + q], device_id=(x, y, 1 - c), device_id_type=MESH))
        return remote, []

    return _Rider(grads, [jax.ShapeDtypeStruct((4,) + g.shape[2:], g.dtype) for g in grads], 4 * n, 0, plan)


def _rider_rs_chips(sums, rows=None, into=None):
    n = len(sums)
    rows = rows or [ALL] * n

    def plan(ins, outs, send, recv, loc, r0, l0):
        x, y, c = _coords()
        remote = []
        for t in range(n):
            for r, (px, py) in enumerate([(1 - x, y), (x, 1 - y), (1 - x, 1 - y)]):
                remote.append(pltpu.make_async_remote_copy(
                    src_ref=_rows(ins[t].at[2 * px + py], rows[t]), dst_ref=_rows(outs[t].at[r], rows[t]),
                    send_sem=send.at[r0 + 3 * t + r], recv_sem=recv.at[r0 + 3 * t + r],
                    device_id=(px, py, c), device_id_type=MESH))
        return remote, []

    out_shape = [jax.ShapeDtypeStruct((3,) + s.shape[1:], s.dtype) for s in sums]
    if into is None:
        return _Rider(sums, out_shape, 3 * n, 0, plan)
    return _Rider(list(sums) + list(into), out_shape, 3 * n, 0, plan, aliases={n + t: t for t in range(n)})


def _rider_gather_remote(parts):
    n = len(parts)

    def plan(ins, outs, send, recv, loc, r0, l0):
        x, y, c = _coords()
        me = _slot((x, y, c))
        remote = []
        for t in range(n):
            for k in range(1, N_DEV):
                peer = (x ^ ((k >> 2) & 1), y ^ ((k >> 1) & 1), c ^ (k & 1))
                remote.append(pltpu.make_async_remote_copy(
                    src_ref=ins[t], dst_ref=outs[t].at[me], send_sem=send.at[r0 + 7 * t + k - 1],
                    recv_sem=recv.at[r0 + 7 * t + k - 1], device_id=peer, device_id_type=MESH))
        return remote, []

    return _Rider(parts, [jax.ShapeDtypeStruct((N_DEV,) + p.shape, p.dtype) for p in parts], 7 * n, 0, plan)


def _chip_sum_call(idx, grads, recvd, out_dtypes, name):
    n = len(grads)

    def body(i_ref, *refs):
        for t in range(n):
            refs[2 * n + t][0] = (refs[t][0, 0] + refs[n + t][0]).astype(out_dtypes[t])

    def chip(k, s):
        return jnp.where(k >= s[0], k + 1, k)

    in_specs = [pl.BlockSpec((1, 1) + g.shape[2:], lambda k, s: (chip(k, s), s[1], 0, 0)) for g in grads]
    in_specs += [pl.BlockSpec((1,) + r.shape[1:], lambda k, s: (chip(k, s), 0, 0)) for r in recvd]
    return pl.pallas_call(
        body, name=name,
        grid_spec=pltpu.PrefetchScalarGridSpec(
            num_scalar_prefetch=1, grid=(3,), in_specs=in_specs,
            out_specs=[pl.BlockSpec((1,) + r.shape[1:], lambda k, s: (chip(k, s), 0, 0)) for r in recvd]),
        out_shape=[jax.ShapeDtypeStruct(r.shape, dt) for r, dt in zip(recvd, out_dtypes)],
        compiler_params=_params(("arbitrary",)),
    )(idx, *grads, *recvd)


def _final_sum_call(idx, grads, recvd1, recvd2):
    n = len(grads)
    nsteps = 2

    def body(i_ref, *refs):
        for t in range(n):
            g, r1, r2, o = refs[t], refs[n + t], refs[2 * n + t], refs[3 * n + t]
            s = g[0, 0] + r1[0]
            for r in range(3):
                s = s + r2[r].astype(F32)
            o[...] = s

    def rows(a):
        r = a.shape[-2]
        return r // nsteps if (r // nsteps) % 16 == 0 else r

    def step(a):
        return (lambda i: i) if rows(a) != a.shape[-2] else (lambda i: 0)

    in_specs = [pl.BlockSpec((1, 1, rows(g), g.shape[3]), lambda i, s, st=step(g): (s[0], s[1], st(i), 0)) for g in grads]
    in_specs += [pl.BlockSpec((1, rows(r), r.shape[2]), lambda i, s, st=step(r): (s[0], st(i), 0)) for r in recvd1]
    in_specs += [pl.BlockSpec((3, rows(r), r.shape[2]), lambda i, s, st=step(r): (0, st(i), 0)) for r in recvd2]
    return pl.pallas_call(
        body, name="rs_final_sum",
        grid_spec=pltpu.PrefetchScalarGridSpec(
            num_scalar_prefetch=1, grid=(nsteps,), in_specs=in_specs,
            out_specs=[pl.BlockSpec((rows(r), r.shape[2]), lambda i, s, st=step(r): (st(i), 0)) for r in recvd2]),
        out_shape=[jax.ShapeDtypeStruct(r.shape[1:], F32) for r in recvd2],
        compiler_params=_params(("arbitrary",)),
    )(idx, *grads, *recvd1, *recvd2)


def _sum8_call(parts):
    def body(p_ref, o_ref):
        s = p_ref[0]
        for j in range(1, N_DEV):
            s = s + p_ref[j]
        o_ref[...] = s

    return pl.pallas_call(body, name="sum_small_partials",
                          out_shape=jax.ShapeDtypeStruct(parts.shape[1:], parts.dtype))(parts)


def _adamw(w, g, m, v):
    m = ADAM_B1 * m + (1.0 - ADAM_B1) * g
    v = ADAM_B2 * v + (1.0 - ADAM_B2) * (g * g)
    m_hat = m / (1.0 - ADAM_B1 ** ADAM_STEP)
    v_hat = v / (1.0 - ADAM_B2 ** ADAM_STEP)
    delta = -ADAM_LR * (m_hat / (jnp.sqrt(v_hat) + ADAM_EPS) + ADAM_WD * w)
    return delta, m, v


def _adamw_call(ws, gs, ms, vs, nsteps, name):
    n = len(ws)

    def body(*refs):
        for t in range(n):
            w, g, m, v = (refs[k * n + t][...] for k in range(4))
            d, m2, v2 = _adamw(w, g, m, v)
            refs[4 * n + t][...] = d
            refs[5 * n + t][...] = m2
            refs[6 * n + t][...] = v2

    def spec(a):
        assert a.shape[0] % nsteps == 0 and (nsteps == 1 or (a.shape[0] // nsteps) % 8 == 0), a.shape
        return pl.BlockSpec((a.shape[0] // nsteps, a.shape[1]), lambda i: (i, 0))

    specs = [spec(a) for a in ws]
    outs = pl.pallas_call(
        body, name=name, grid=(nsteps,),
        in_specs=specs * 4, out_specs=specs * 3,
        out_shape=[jax.ShapeDtypeStruct(a.shape, F32) for a in ws] * 3,
        compiler_params=_params(("arbitrary",)),
    )(*ws, *gs, *ms, *vs)
    return outs[:n], outs[n:2 * n], outs[2 * n:]


def _adamw_rs_call(idx, after, gws, r1s, r2s, ws, ms, vs, nsteps, name):
    n = len(ws)

    def body(i_ref, after_ref, *refs):
        for t in range(n):
            gw, r1, r2, w, m, v = (refs[k * n + t] for k in range(6))
            g = gw[0, 0] + r1[0]
            for r in range(3):
                g = g + r2[r].astype(F32)
            d, m2, v2 = _adamw(w[...], g, m[...], v[...])
            refs[6 * n + t][...] = g
            refs[7 * n + t][...] = d
            refs[8 * n + t][...] = m2
            refs[9 * n + t][...] = v2

    def rb(a):
        r = a.shape[0] // nsteps
        assert a.shape[0] % nsteps == 0 and r % 16 == 0, a.shape
        return r

    in_specs = [pl.BlockSpec((1, 1, rb(w), w.shape[1]), lambda i, s: (s[0], s[1], i, 0)) for w in ws]
    in_specs += [pl.BlockSpec((1, rb(w), w.shape[1]), lambda i, s: (s[0], i, 0)) for w in ws]
    in_specs += [pl.BlockSpec((3, rb(w), w.shape[1]), lambda i, s: (0, i, 0)) for w in ws]
    plain = [pl.BlockSpec((rb(w), w.shape[1]), lambda i, s: (i, 0)) for w in ws]
    outs = pl.pallas_call(
        body, name=name,
        grid_spec=pltpu.PrefetchScalarGridSpec(
            num_scalar_prefetch=1, grid=(nsteps,),
            in_specs=[pl.BlockSpec(memory_space=pl.ANY)] + in_specs + plain * 3, out_specs=plain * 4),
        out_shape=[jax.ShapeDtypeStruct(w.shape, F32) for w in ws] * 4,
        compiler_params=_params(("arbitrary",)),
    )(idx, after, *gws, *r1s, *r2s, *ws, *ms, *vs)
    return outs[:n], outs[n:2 * n], outs[2 * n:3 * n], outs[3 * n:]


def _rows128(a, pad_rows):
    flat = a.reshape(-1).astype(F32)
    flat = jnp.pad(flat, (0, pad_rows * LANES - flat.shape[0]))
    return flat.reshape(pad_rows, LANES)


_SMALL_A = (("w_pool", 512), ("pool_scale", 8), ("attn_sinks", 8), ("g_mix_post", 8), ("g_mlp_pre", 8),
            ("g_mlp_post", 8), ("loss", 8), ("b_in_gates", 16))
_SMALL_A_ROWS = 640
_SMALL_B = (("g_mix_pre", 8), ("b_in_head", 16))


def _pack(parts, layout, total_rows):
    rows = [_rows128(parts[k], r) for k, r in layout]
    pad = total_rows - sum(r for _, r in layout)
    if pad:
        rows.append(jnp.zeros((pad, LANES), F32))
    return jnp.concatenate(rows, axis=0)


def _unpack(buf, layout, sizes):
    out, off = {}, 0
    for k, r in layout:
        out[k] = buf[off:off + r].reshape(-1)[:sizes[k]]
        off += r
    return out


def kernel(x, g_mix_pre, w_in, b_in, w_pool, pool_scale, attn_sinks, w_branch_pool, w_branch_attn, w_out, g_mix_post, g_mlp_pre, w_up, w_down, g_mlp_post, loss_target, m_g_mix_pre, m_w_in, m_b_in, m_w_pool, m_pool_scale, m_attn_sinks, m_w_branch_pool, m_w_branch_attn, m_w_out, m_g_mix_post, m_g_mlp_pre, m_w_up, m_w_down, m_g_mlp_post, v_g_mix_pre, v_w_in, v_b_in, v_w_pool, v_pool_scale, v_attn_sinks, v_w_branch_pool, v_w_branch_attn, v_w_out, v_g_mix_post, v_g_mlp_pre, v_w_up, v_w_down, v_g_mlp_post):
    B, S, _ = x.shape
    T = B * S
    xt = x.reshape(T, D_MODEL)
    tgt = loss_target.reshape(T, D_MODEL)
    cx, cy, cc = _coords()

    cidx = jnp.stack([2 * cx + cy, cc]).astype(jnp.int32)
    by_chip = lambda gr: gr.reshape((4, 2) + gr.shape[1:])
    bf = lambda w: w[0].astype(MXU_DTYPE)

    me = _slot((cx, cy, cc))
    win_l = w_in[0].T.astype(MXU_DTYPE)
    (c_win,), _ = _copies_start([_rider_ag_first(win_l, me)], "allgather_first")
    wpool_b = bf(w_pool)
    rc, rsa, rsb = _rot_tables(S)
    wbp_l, wba_l, wout_l, wup_l, wdown_l = bf(w_branch_pool), bf(w_branch_attn), bf(w_out), bf(w_up), bf(w_down)
    gathers = [_rider_ag_remote([wbp_l, wba_l, wout_l], me), _rider_ag_remote([wup_l], me), _rider_ag_remote([wdown_l], me)]
    c_win = _copies_pass([c_win], [_rider_ag_onward(win_l, 2)], [b for r in gathers for b in r.lands] + [rc, rsa, rsb],
                         "allgather_second")
    c_win = _copies_pass(c_win, [_rider_ag_onward(win_l, 3)], [wbp_l, wba_l, wout_l, wup_l, wdown_l], "allgather_third")
    (win_s,) = _copies_wait(c_win, wpool_b, "allgather_weights")
    win_t = win_s.reshape(IN_WIDTH, D_MODEL)

    (c_br, c_up, c_dn), tok = _copies_start(gathers, "allgather_start", after=win_s)
    (h, u, q, k4, v4, g), _ = _inproj_call(xt, g_mix_pre, win_t, b_in, rc, rsa, rsb, S, rider=_after(tok))
    wbp_1, wba_1, wout_1 = _copies_wait([c_br], h, "allgather_wait_branch")
    (yp,), (wbp_s, wba_s, wout_s) = _pool_call(
        u, wpool_b, pool_scale, S,
        rider=_rider_ag([(None, wbp_1, None, ALL), (None, wba_1, None, ALL), (None, wout_1, None, ALL)]))
    (ya,) = _attn_call(attn_sinks, q, k4, v4, S)
    wout_f = wout_s.reshape(D_MODEL, D_MODEL)
    (wup_1,) = _copies_wait([c_up], ya, "allgather_wait_up")
    (mix, x1, h2), (wup_s,) = _mix_fwd_call(
        yp, ya, g, xt, wbp_s, wba_s, wout_f, g_mix_post, g_mlp_pre, rider=_rider_ag([(None, wup_1, None, ALL)]))
    (wdown_1,) = _copies_wait([c_dn], h2, "allgather_wait_down")
    (act,), (wdown_s,) = _mlp_up_call(h2, wup_s, rider=_rider_ag([(None, wdown_1, None, ALL)]))

    da, dff, dx1, dg3, dg4, lossvec = _mlp_call(x1, act, tgt, wup_s, wdown_s, g_mlp_pre, g_mlp_post)
    gw_up = by_chip(_wgrad_cols_call(h2, da, "wgrad_up")[0])
    (gw_down,), (r1_up,) = _wgrad_rows_call(act, dff, "wgrad_down", rider=_rider_rs_sibling([gw_up]))
    gw_down = by_chip(gw_down)
    (s_up,) = _chip_sum_call(cidx, [gw_up], [r1_up], [MXU_DTYPE], "rs_chip_sum_up")
    (dyp, do, dgates, dg2, dbg, gw_out, gw_bp, gw_ba), _ = _mix_bwd_call(
        dx1, mix, yp, ya, g, wbp_s, wba_s, wout_f, g_mix_post, rider=_after(s_up))
    gw_out = by_chip(gw_out.reshape(N_DEV, D_MODEL // N_DEV, D_MODEL))
    gw_bp, gw_ba = by_chip(gw_bp), by_chip(gw_ba)
    (dq, dk, dv, dsink), (r1_down, r1_out, r1_bp, r1_ba, r2_up) = _attn_bwd_call(
        attn_sinks, q, k4, v4, do, rc, rsa, rsb, S,
        rider=_join(_rider_rs_sibling([gw_down, gw_out, gw_bp, gw_ba]), _rider_rs_chips([s_up])))
    s_down, *s_obb = _chip_sum_call(cidx, [gw_down, gw_out, gw_bp, gw_ba], [r1_down, r1_out, r1_bp, r1_ba],
                                    [MXU_DTYPE] * 4, "rs_chip_sum_branch")
    (c_obb,), tok = _copies_start([_rider_rs_chips([s_down] + s_obb)], "rs_chips_start_branch")
    (du, dwp, dps), _ = _pool_bwd_call(u, dyp, wpool_b, pool_scale, S, rider=_after(tok))
    (gw_in,) = _wgrad_in_call(du, dq, dk, dv, dgates, h)
    gw_in = by_chip(gw_in)

    small_a = {"w_pool": dwp, "pool_scale": dps,
               "attn_sinks": jnp.sum(dsink.reshape(B, 8, LANES)[:, 0, :N_Q_HEADS], axis=0), "g_mix_post": dg2,
               "g_mlp_pre": dg3, "g_mlp_post": dg4, "loss": lossvec, "b_in_gates": dbg}
    gw_sa = by_chip(_pack(small_a, _SMALL_A, _SMALL_A_ROWS).reshape(N_DEV, _SMALL_A_ROWS // N_DEV, LANES))
    r1_in, r1_sa = _comm_call(_rider_rs_sibling([gw_in, gw_sa]), "rs_sibling_in")
    s_in, s_sa = _chip_sum_call(cidx, [gw_in, gw_sa], [r1_in, r1_sa], [MXU_DTYPE, F32], "rs_chip_sum_in")
    (c_in,), tok = _copies_start([_rider_rs_chips([s_in, s_sa])], "rs_chips_start_in")
    (gx, dg1, dba_in), _ = _inproj_bwd_call(du, dq, dk, dv, dgates, dx1, xt, win_t, g_mix_pre, rider=_after(tok))
    r2_down, r2_out, r2_bp, r2_ba, r2_in, r2_sa = _copies_wait([c_obb, c_in], dg1, "rs_chips_wait")

    (g_sa,) = _final_sum_call(cidx, [gw_sa], [r1_sa], [r2_sa])
    part_b = _pack({"g_mix_pre": dg1, "b_in_head": dba_in}, _SMALL_B, sum(r for _, r in _SMALL_B))
    (c_small,), tok = _copies_start([_rider_gather_remote([g_sa, part_b])], "allgather_small_start")

    in_t = _adamw_rs_call(cidx, tok, [gw_in], [r1_in], [r2_in], [w_in[0].T], [m_w_in[0].T], [v_w_in[0].T], 2,
                          "adamw_w_in")
    rest = _adamw_rs_call(
        cidx, tok, [gw_bp, gw_ba, gw_out, gw_up, gw_down], [r1_bp, r1_ba, r1_out, r1_up, r1_down],
        [r2_bp, r2_ba, r2_out, r2_up, r2_down], [w_branch_pool[0], w_branch_attn[0], w_out[0], w_up[0], w_down[0]],
        [m_w_branch_pool[0], m_w_branch_attn[0], m_w_out[0], m_w_up[0], m_w_down[0]],
        [v_w_branch_pool[0], v_w_branch_attn[0], v_w_out[0], v_w_up[0], v_w_down[0]], N_DEV, "adamw_shards")
    big_g, big_d, big_m2, big_v2 = ([a[0].T] + list(b) for a, b in zip(in_t, rest))

    sa_all, sb_all = _copies_wait([c_small], rest[0][0], "allgather_small_wait")
    sa_all = lax.dynamic_update_slice(sa_all, g_sa[None], (me, 0, 0))
    sb_sum = _sum8_call(lax.dynamic_update_slice(sb_all, part_b[None], (me, 0, 0)))

    names = ["g_mix_pre", "b_in", "w_pool", "pool_scale", "attn_sinks", "g_mix_post", "g_mlp_pre", "g_mlp_post"]
    sm_w = dict(g_mix_pre=g_mix_pre, b_in=b_in, w_pool=w_pool, pool_scale=pool_scale, attn_sinks=attn_sinks,
                g_mix_post=g_mix_post, g_mlp_pre=g_mlp_pre, g_mlp_post=g_mlp_post)
    sm_m = dict(g_mix_pre=m_g_mix_pre, b_in=m_b_in, w_pool=m_w_pool, pool_scale=m_pool_scale, attn_sinks=m_attn_sinks,
                g_mix_post=m_g_mix_post, g_mlp_pre=m_g_mlp_pre, g_mlp_post=m_g_mlp_post)
    sm_v = dict(g_mix_pre=v_g_mix_pre, b_in=v_b_in, w_pool=v_w_pool, pool_scale=v_pool_scale, attn_sinks=v_attn_sinks,
                g_mix_post=v_g_mix_post, g_mlp_pre=v_g_mlp_pre, g_mlp_post=v_g_mlp_post)
    sizes = {k: sm_w[k].size for k in names}
    sizes.update(loss=D_MODEL, b_in_gates=GATE_WIDTH, b_in_head=C_G)
    sm_g = _unpack(sa_all.reshape(_SMALL_A_ROWS, LANES), _SMALL_A, sizes)
    sm_g.update(_unpack(sb_sum, _SMALL_B, sizes))
    sm_g["b_in"] = jnp.concatenate([sm_g["b_in_head"], sm_g["b_in_gates"]])
    loss = (0.5 / D_MODEL) * jnp.sum(sm_g["loss"])
    two_d = lambda a: a.reshape(-1, a.shape[-1])
    sd_, sm2_, sv2_ = _adamw_call([two_d(sm_w[k]) for k in names], [two_d(sm_g[k].reshape(sm_w[k].shape)) for k in names],
                                  [two_d(sm_m[k]) for k in names], [two_d(sm_v[k]) for k in names], 1, "adamw_small")
    like = lambda vals: {k: a.reshape(sm_w[k].shape) for k, a in zip(names, vals)}
    sm_d, sm_m2, sm_v2 = like(sd_), like(sm2_), like(sv2_)
    sm_gr = {k: sm_g[k].reshape(sm_w[k].shape) for k in names}

    order = ["g_mix_pre", "w_in", "b_in", "w_pool", "pool_scale", "attn_sinks", "w_branch_pool", "w_branch_attn",
             "w_out", "g_mix_post", "g_mlp_pre", "w_up", "w_down", "g_mlp_post"]
    big_names = ["w_in", "w_branch_pool", "w_branch_attn", "w_out", "w_up", "w_down"]
    lead = lambda a: a[None]
    tables = []
    for small_t, big_t in ((sm_gr, big_g), (sm_d, big_d), (sm_m2, big_m2), (sm_v2, big_v2)):
        bt = dict(zip(big_names, big_t))
        tables.append([lead(bt[k]) if k in bt else small_t[k] for k in order])
    return (loss, gx.reshape(B, S, D_MODEL), *tables[0], *tables[1], *tables[2], *tables[3])
```
